```python
import jax, jax.numpy as jnp
from jax import lax
import numpy as np

D_MODEL = 1024
BATCH = 16
SEQ = 2048
DEPTH = 2

N_MIXERS = 2
N_MLA_LAYERS = (DEPTH + 1) // 2
N_FOX_LAYERS = DEPTH // 2

MLA_HEADS = 8
MLA_NOPE_DIM = 128
MLA_ROPE_DIM = 64
MLA_V_DIM = 128
MLA_Q_RANK = 256
MLA_KV_RANK = 256
ROPE_THETA = 10000.0

FOX_HEADS = 16
FOX_HEAD_DIM = D_MODEL // FOX_HEADS

D_FF = -(-8 * D_MODEL // (3 * 256)) * 256

Q_BLOCK = 128
DEEPNORM_ALPHA = (2.0 * DEPTH) ** 0.25
DEEPNORM_BETA = (8.0 * DEPTH) ** -0.25
NORM_EPS = 1e-5
MLA_IN_DIM = MLA_Q_RANK + MLA_KV_RANK + MLA_ROPE_DIM
FOX_IN_DIM = 3 * D_MODEL + FOX_HEADS

kernel_name = "hybrid_mla_fox_deepnorm_adaln"


def rms_norm(x, g):
    xf = x.astype(jnp.float32)
    y = xf * lax.rsqrt(jnp.mean(xf * xf, axis=-1, keepdims=True) + NORM_EPS)
    return (y * g.astype(jnp.float32)).astype(x.dtype)


def layer_norm(x, g, b):
    xf = x.astype(jnp.float32)
    mu = jnp.mean(xf, axis=-1, keepdims=True)
    var = jnp.mean(jnp.square(xf - mu), axis=-1, keepdims=True)
    y = (xf - mu) * lax.rsqrt(var + NORM_EPS)
    return (y * g.astype(jnp.float32) + b.astype(jnp.float32)).astype(x.dtype)


def rotary_angles(positions, dim):
    half = dim // 2
    inv_freq = ROPE_THETA ** (-jnp.arange(half, dtype=jnp.float32) / half)
    ang = positions.astype(jnp.float32)[..., None] * inv_freq
    return jnp.cos(ang), jnp.sin(ang)


def apply_rotary(x, cos, sin):
    half = x.shape[-1] // 2
    x1, x2 = x[..., :half], x[..., half:]
    cos = cos.astype(x.dtype)
    sin = sin.astype(x.dtype)
    return jnp.concatenate([x1 * cos - x2 * sin, x2 * cos + x1 * sin], axis=-1)


def causal_block_attention(logits_fn, v):
    b, h, s, dv = v.shape
    key_pos = jnp.arange(s)

    def one_block(blk):
        q_start = blk * Q_BLOCK
        logits = logits_fn(q_start)
        q_pos = q_start + jnp.arange(Q_BLOCK)
        causal = q_pos[:, None] >= key_pos[None, :]
        probs = jax.nn.softmax(jnp.where(causal, logits, -jnp.inf), axis=-1)
        return jnp.einsum('bhqs,bhsd->bhqd', probs.astype(v.dtype), v)

    out = lax.map(one_block, jnp.arange(s // Q_BLOCK))
    return out.transpose(1, 0, 3, 2, 4).reshape(b, s, h * dv)


def mla_mixer(u, cos, sin, w_in, g_q, w_uq, g_kv, w_uk, w_uv, w_o):
    b, s, _ = u.shape
    h_in = u @ w_in
    c_q = rms_norm(h_in[..., :MLA_Q_RANK], g_q)
    c_kv = rms_norm(h_in[..., MLA_Q_RANK:MLA_Q_RANK + MLA_KV_RANK], g_kv)
    k_rope = apply_rotary(h_in[..., MLA_Q_RANK + MLA_KV_RANK:], cos, sin)

    q = (c_q @ w_uq).reshape(b, s, MLA_HEADS, MLA_NOPE_DIM + MLA_ROPE_DIM)
    q_nope = q[..., :MLA_NOPE_DIM].transpose(0, 2, 1, 3)
    q_rope = apply_rotary(q[..., MLA_NOPE_DIM:], cos[:, :, None, :], sin[:, :, None, :])
    q_rope = q_rope.transpose(0, 2, 1, 3)
    k_nope = (c_kv @ w_uk).reshape(b, s, MLA_HEADS, MLA_NOPE_DIM).transpose(0, 2, 1, 3)
    v = (c_kv @ w_uv).reshape(b, s, MLA_HEADS, MLA_V_DIM).transpose(0, 2, 1, 3)
    scale = (MLA_NOPE_DIM + MLA_ROPE_DIM) ** -0.5

    def logits_fn(q_start):
        qn = lax.dynamic_slice_in_dim(q_nope, q_start, Q_BLOCK, axis=2)
        qr = lax.dynamic_slice_in_dim(q_rope, q_start, Q_BLOCK, axis=2)
        sc = (jnp.einsum('bhqd,bhsd->bhqs', qn, k_nope)
              + jnp.einsum('bhqr,bsr->bhqs', qr, k_rope))
        return sc.astype(jnp.float32) * scale

    return causal_block_attention(logits_fn, v) @ w_o


def fox_mixer(u, w_in, b_f, w_o):
    b, s, d = u.shape
    h_in = u @ w_in
    def heads(t):
        return t.reshape(b, s, FOX_HEADS, FOX_HEAD_DIM).transpose(0, 2, 1, 3)
    q = heads(h_in[..., :d])
    k = heads(h_in[..., d:2 * d])
    v = heads(h_in[..., 2 * d:3 * d])
    log_f = jax.nn.log_sigmoid(h_in[..., 3 * d:].astype(jnp.float32) + b_f.astype(jnp.float32))
    cum_log_f = lax.cumsum(log_f, axis=1).transpose(0, 2, 1)
    scale = FOX_HEAD_DIM ** -0.5

    def logits_fn(q_start):
        qb = lax.dynamic_slice_in_dim(q, q_start, Q_BLOCK, axis=2)
        fq = lax.dynamic_slice_in_dim(cum_log_f, q_start, Q_BLOCK, axis=2)
        sc = jnp.einsum('bhqd,bhsd->bhqs', qb, k).astype(jnp.float32) * scale
        return sc + fq[..., :, None] - cum_log_f[:, :, None, :]

    return causal_block_attention(logits_fn, v) @ w_o


def swiglu(u, w_gate, w_up, w_down):
    return (jax.nn.silu(u @ w_gate) * (u @ w_up)) @ w_down


def modulate(x, shift, scale):
    return x * (1.0 + scale[:, None, :]) + shift[:, None, :]


def _fwd_setup_inputs(seed: int = 0) -> dict:
    key = jax.random.key(seed)
    ks = iter(jax.random.split(key, 40))
    f32 = jnp.float32
    def nrm(shape, std):
        return jax.random.normal(next(ks), shape, f32) * std
    D, H, Hf = D_MODEL, MLA_HEADS, FOX_HEADS
    beta = DEEPNORM_BETA
    nm, nf = N_MLA_LAYERS, N_FOX_LAYERS

    x = jax.random.normal(next(ks), (BATCH, SEQ, D), f32)
    c = jax.random.normal(next(ks), (BATCH, D), f32)
    positions = (jnp.arange(SEQ, dtype=jnp.int32)[None, :]
                 + jax.random.randint(next(ks), (BATCH, 1), 0, 128, dtype=jnp.int32))

    mla_w_in = nrm((nm, D, MLA_IN_DIM), D ** -0.5)
    mla_g_q = 1.0 + nrm((nm, MLA_Q_RANK), 0.02)
    mla_w_uq = nrm((nm, MLA_Q_RANK, H * (MLA_NOPE_DIM + MLA_ROPE_DIM)), MLA_Q_RANK ** -0.5)
    mla_g_kv = 1.0 + nrm((nm, MLA_KV_RANK), 0.02)
    mla_w_uk = nrm((nm, MLA_KV_RANK, H * MLA_NOPE_DIM), MLA_KV_RANK ** -0.5)
    mla_w_uv = nrm((nm, MLA_KV_RANK, H * MLA_V_DIM), beta * MLA_KV_RANK ** -0.5)
    mla_w_o = nrm((nm, H * MLA_V_DIM, D), beta * (H * MLA_V_DIM) ** -0.5)

    fox_w_in = jnp.concatenate([
        nrm((nf, D, 2 * D), D ** -0.5),
        nrm((nf, D, D), beta * D ** -0.5),
        nrm((nf, D, Hf), D ** -0.5),
    ], axis=-1)
    fox_b_f = 2.0 + nrm((nf, Hf), 0.5)
    fox_w_o = nrm((nf, D, D), beta * D ** -0.5)

    ada_w = nrm((DEPTH, D, 6 * D), 0.1 * D ** -0.5)
    ada_b = nrm((DEPTH, 6 * D), 0.02)

    ffn_w_gate = nrm((DEPTH, D, D_FF), beta * D ** -0.5)
    ffn_w_up = nrm((DEPTH, D, D_FF), beta * D ** -0.5)
    ffn_w_down = nrm((DEPTH, D_FF, D), beta * D_FF ** -0.5)

    ln_g = 1.0 + nrm((DEPTH, 2, D), 0.02)
    ln_b = nrm((DEPTH, 2, D), 0.02)

    return {"x": x, "c": c, "positions": positions,
            "mla_w_in": mla_w_in, "mla_g_q": mla_g_q, "mla_w_uq": mla_w_uq,
            "mla_g_kv": mla_g_kv, "mla_w_uk": mla_w_uk, "mla_w_uv": mla_w_uv, "mla_w_o": mla_w_o,
            "fox_w_in": fox_w_in, "fox_b_f": fox_b_f, "fox_w_o": fox_w_o,
            "ada_w": ada_w, "ada_b": ada_b,
            "ffn_w_gate": ffn_w_gate, "ffn_w_up": ffn_w_up, "ffn_w_down": ffn_w_down,
            "ln_g": ln_g, "ln_b": ln_b}


def _fwd_reference(x, c, positions, mla_w_in, mla_g_q, mla_w_uq, mla_g_kv, mla_w_uk, mla_w_uv, mla_w_o,
              fox_w_in, fox_b_f, fox_w_o, ada_w, ada_b, ffn_w_gate, ffn_w_up, ffn_w_down,
              ln_g, ln_b):
    d = D_MODEL
    cos, sin = rotary_angles(positions, MLA_ROPE_DIM)
    c_act = jax.nn.silu(c)
    for i in range(DEPTH):
        mod = c_act @ ada_w[i] + ada_b[i]
        sh_a, sc_a, gt_a = mod[:, :d], mod[:, d:2 * d], mod[:, 2 * d:3 * d]
        sh_f, sc_f, gt_f = mod[:, 3 * d:4 * d], mod[:, 4 * d:5 * d], mod[:, 5 * d:]

        u = modulate(x, sh_a, sc_a)
        j = i // N_MIXERS
        if i % N_MIXERS == 0:
            y = mla_mixer(u, cos, sin, mla_w_in[j], mla_g_q[j], mla_w_uq[j], mla_g_kv[j],
                          mla_w_uk[j], mla_w_uv[j], mla_w_o[j])
        else:
            y = fox_mixer(u, fox_w_in[j], fox_b_f[j], fox_w_o[j])
        x = layer_norm(DEEPNORM_ALPHA * x + (1.0 + gt_a[:, None, :]) * y, ln_g[i, 0], ln_b[i, 0])

        u = modulate(x, sh_f, sc_f)
        y = swiglu(u, ffn_w_gate[i], ffn_w_up[i], ffn_w_down[i])
        x = layer_norm(DEEPNORM_ALPHA * x + (1.0 + gt_f[:, None, :]) * y, ln_g[i, 1], ln_b[i, 1])
    return x


import jax as _jax
import jax.numpy as _jnp

TWIN_FORMAT = 'train_step'
FWD_PARAMS = ['x', 'c', 'positions', 'mla_w_in', 'mla_g_q', 'mla_w_uq', 'mla_g_kv', 'mla_w_uk', 'mla_w_uv', 'mla_w_o', 'fox_w_in', 'fox_b_f', 'fox_w_o', 'ada_w', 'ada_b', 'ffn_w_gate', 'ffn_w_up', 'ffn_w_down', 'ln_g', 'ln_b']
TWIN_WEIGHTS = ['mla_w_in', 'mla_g_q', 'mla_w_uq', 'mla_g_kv', 'mla_w_uk', 'mla_w_uv', 'mla_w_o', 'fox_w_in', 'fox_b_f', 'fox_w_o', 'ada_w', 'ada_b', 'ffn_w_gate', 'ffn_w_up', 'ffn_w_down', 'ln_g', 'ln_b']
TWIN_DIFF_INPUT = 'x'
TWIN_INPUTS = ['x', 'c', 'positions', 'mla_w_in', 'mla_g_q', 'mla_w_uq', 'mla_g_kv', 'mla_w_uk', 'mla_w_uv', 'mla_w_o', 'fox_w_in', 'fox_b_f', 'fox_w_o', 'ada_w', 'ada_b', 'ffn_w_gate', 'ffn_w_up', 'ffn_w_down', 'ln_g', 'ln_b', 'loss_target', 'm_mla_w_in', 'm_mla_g_q', 'm_mla_w_uq', 'm_mla_g_kv', 'm_mla_w_uk', 'm_mla_w_uv', 'm_mla_w_o', 'm_fox_w_in', 'm_fox_b_f', 'm_fox_w_o', 'm_ada_w', 'm_ada_b', 'm_ffn_w_gate', 'm_ffn_w_up', 'm_ffn_w_down', 'm_ln_g', 'm_ln_b', 'v_mla_w_in', 'v_mla_g_q', 'v_mla_w_uq', 'v_mla_g_kv', 'v_mla_w_uk', 'v_mla_w_uv', 'v_mla_w_o', 'v_fox_w_in', 'v_fox_b_f', 'v_fox_w_o', 'v_ada_w', 'v_ada_b', 'v_ffn_w_gate', 'v_ffn_w_up', 'v_ffn_w_down', 'v_ln_g', 'v_ln_b']
TWIN_OUTPUTS = ['loss', 'grad_x', 'grad_mla_w_in', 'grad_mla_g_q', 'grad_mla_w_uq', 'grad_mla_g_kv', 'grad_mla_w_uk', 'grad_mla_w_uv', 'grad_mla_w_o', 'grad_fox_w_in', 'grad_fox_b_f', 'grad_fox_w_o', 'grad_ada_w', 'grad_ada_b', 'grad_ffn_w_gate', 'grad_ffn_w_up', 'grad_ffn_w_down', 'grad_ln_g', 'grad_ln_b', 'delta_mla_w_in', 'delta_mla_g_q', 'delta_mla_w_uq', 'delta_mla_g_kv', 'delta_mla_w_uk', 'delta_mla_w_uv', 'delta_mla_w_o', 'delta_fox_w_in', 'delta_fox_b_f', 'delta_fox_w_o', 'delta_ada_w', 'delta_ada_b', 'delta_ffn_w_gate', 'delta_ffn_w_up', 'delta_ffn_w_down', 'delta_ln_g', 'delta_ln_b', 'new_m_mla_w_in', 'new_m_mla_g_q', 'new_m_mla_w_uq', 'new_m_mla_g_kv', 'new_m_mla_w_uk', 'new_m_mla_w_uv', 'new_m_mla_w_o', 'new_m_fox_w_in', 'new_m_fox_b_f', 'new_m_fox_w_o', 'new_m_ada_w', 'new_m_ada_b', 'new_m_ffn_w_gate', 'new_m_ffn_w_up', 'new_m_ffn_w_down', 'new_m_ln_g', 'new_m_ln_b', 'new_v_mla_w_in', 'new_v_mla_g_q', 'new_v_mla_w_uq', 'new_v_mla_g_kv', 'new_v_mla_w_uk', 'new_v_mla_w_uv', 'new_v_mla_w_o', 'new_v_fox_w_in', 'new_v_fox_b_f', 'new_v_fox_w_o', 'new_v_ada_w', 'new_v_ada_b', 'new_v_ffn_w_gate', 'new_v_ffn_w_up', 'new_v_ffn_w_down', 'new_v_ln_g', 'new_v_ln_b']
TWIN_LEAF_KINDS = {'loss': 'loss', 'grad_x': 'grad_x', 'grad_mla_w_in': 'grad_w', 'grad_mla_g_q': 'grad_w', 'grad_mla_w_uq': 'grad_w', 'grad_mla_g_kv': 'grad_w', 'grad_mla_w_uk': 'grad_w', 'grad_mla_w_uv': 'grad_w', 'grad_mla_w_o': 'grad_w', 'grad_fox_w_in': 'grad_w', 'grad_fox_b_f': 'grad_w', 'grad_fox_w_o': 'grad_w', 'grad_ada_w': 'grad_w', 'grad_ada_b': 'grad_w', 'grad_ffn_w_gate': 'grad_w', 'grad_ffn_w_up': 'grad_w', 'grad_ffn_w_down': 'grad_w', 'grad_ln_g': 'grad_w', 'grad_ln_b': 'grad_w', 'delta_mla_w_in': 'delta_w', 'delta_mla_g_q': 'delta_w', 'delta_mla_w_uq': 'delta_w', 'delta_mla_g_kv': 'delta_w', 'delta_mla_w_uk': 'delta_w', 'delta_mla_w_uv': 'delta_w', 'delta_mla_w_o': 'delta_w', 'delta_fox_w_in': 'delta_w', 'delta_fox_b_f': 'delta_w', 'delta_fox_w_o': 'delta_w', 'delta_ada_w': 'delta_w', 'delta_ada_b': 'delta_w', 'delta_ffn_w_gate': 'delta_w', 'delta_ffn_w_up': 'delta_w', 'delta_ffn_w_down': 'delta_w', 'delta_ln_g': 'delta_w', 'delta_ln_b': 'delta_w', 'new_m_mla_w_in': 'new_m', 'new_m_mla_g_q': 'new_m', 'new_m_mla_w_uq': 'new_m', 'new_m_mla_g_kv': 'new_m', 'new_m_mla_w_uk': 'new_m', 'new_m_mla_w_uv': 'new_m', 'new_m_mla_w_o': 'new_m', 'new_m_fox_w_in': 'new_m', 'new_m_fox_b_f': 'new_m', 'new_m_fox_w_o': 'new_m', 'new_m_ada_w': 'new_m', 'new_m_ada_b': 'new_m', 'new_m_ffn_w_gate': 'new_m', 'new_m_ffn_w_up': 'new_m', 'new_m_ffn_w_down': 'new_m', 'new_m_ln_g': 'new_m', 'new_m_ln_b': 'new_m', 'new_v_mla_w_in': 'new_v', 'new_v_mla_g_q': 'new_v', 'new_v_mla_w_uq': 'new_v', 'new_v_mla_g_kv': 'new_v', 'new_v_mla_w_uk': 'new_v', 'new_v_mla_w_uv': 'new_v', 'new_v_mla_w_o': 'new_v', 'new_v_fox_w_in': 'new_v', 'new_v_fox_b_f': 'new_v', 'new_v_fox_w_o': 'new_v', 'new_v_ada_w': 'new_v', 'new_v_ada_b': 'new_v', 'new_v_ffn_w_gate': 'new_v', 'new_v_ffn_w_up': 'new_v', 'new_v_ffn_w_down': 'new_v', 'new_v_ln_g': 'new_v', 'new_v_ln_b': 'new_v'}


def _forward(args):
    return _fwd_reference(*[args[k] for k in FWD_PARAMS])


def _output_shape():
    out = _jax.eval_shape(lambda: _forward(_fwd_setup_inputs(0)))
    return out.shape, out.dtype

N_MICROBATCH = 1
ADAM_LR = 0.001
ADAM_B1 = 0.9
ADAM_B2 = 0.999
ADAM_EPS = 1e-08
ADAM_WD = 0.01
ADAM_STEP = 10
PER_EXAMPLE_BATCH_AXIS = {'x': 0, 'c': 0, 'positions': 0, 'loss_target': 0}
SHARED_INPUTS = []
_WEIGHT_DTYPES = {'mla_w_in': _jnp.float32, 'mla_g_q': _jnp.float32, 'mla_w_uq': _jnp.float32, 'mla_g_kv': _jnp.float32, 'mla_w_uk': _jnp.float32, 'mla_w_uv': _jnp.float32, 'mla_w_o': _jnp.float32, 'fox_w_in': _jnp.float32, 'fox_b_f': _jnp.float32, 'fox_w_o': _jnp.float32, 'ada_w': _jnp.float32, 'ada_b': _jnp.float32, 'ffn_w_gate': _jnp.float32, 'ffn_w_up': _jnp.float32, 'ffn_w_down': _jnp.float32, 'ln_g': _jnp.float32, 'ln_b': _jnp.float32}
MOMENT_SCALE = {'mla_w_in': 1.441204e-02, 'mla_g_q': 1.105238e-02, 'mla_w_uq': 4.676474e-03, 'mla_g_kv': 1.870320e-02, 'mla_w_uk': 4.737057e-03, 'mla_w_uv': 1.489956e-02, 'mla_w_o': 1.485799e-02, 'fox_w_in': 1.679084e-02, 'fox_b_f': 1.123272e-01, 'fox_w_o': 2.563123e-02, 'ada_w': 1.316144e-02, 'ada_b': 3.160083e-02, 'ffn_w_gate': 1.047976e-02, 'ffn_w_up': 1.028543e-02, 'ffn_w_down': 1.705588e-02, 'ln_g': 1.604803e+01, 'ln_b': 6.484235e-01}


def _to_microbatches(a, axis):
    t = _jnp.moveaxis(a, axis, 0)
    t = t.reshape((N_MICROBATCH, t.shape[0] // N_MICROBATCH) + t.shape[1:])
    return _jnp.moveaxis(t, 1, axis + 1)


def setup_inputs(seed: int = 0) -> dict:
    inp = _fwd_setup_inputs(seed)
    key = _jax.random.fold_in(_jax.random.key(seed), 7919)
    shape, _ = _output_shape()
    out = dict(inp)
    out["loss_target"] = _jax.random.normal(_jax.random.fold_in(key, 0), shape, _jnp.float32)
    for i, name in enumerate(TWIN_WEIGHTS):
        w = inp[name].astype(_jnp.float32)
        if MOMENT_SCALE is None:
            s = _jnp.sqrt(_jnp.mean(_jnp.square(w)) + 1e-30)
        else:
            s = MOMENT_SCALE[name]
        km, kv = _jax.random.split(_jax.random.fold_in(key, i + 1))
        out[name] = w
        out["m_" + name] = s * _jax.random.normal(km, w.shape, _jnp.float32)
        out["v_" + name] = (s * s) * _jax.random.uniform(kv, w.shape, _jnp.float32, 0.5, 1.5)
    if N_MICROBATCH > 1:
        for name, axis in PER_EXAMPLE_BATCH_AXIS.items():
            out[name] = _to_microbatches(out[name], axis)
    return {'x': out['x'], 'c': out['c'], 'positions': out['positions'], 'mla_w_in': out['mla_w_in'], 'mla_g_q': out['mla_g_q'], 'mla_w_uq': out['mla_w_uq'], 'mla_g_kv': out['mla_g_kv'], 'mla_w_uk': out['mla_w_uk'], 'mla_w_uv': out['mla_w_uv'], 'mla_w_o': out['mla_w_o'], 'fox_w_in': out['fox_w_in'], 'fox_b_f': out['fox_b_f'], 'fox_w_o': out['fox_w_o'], 'ada_w': out['ada_w'], 'ada_b': out['ada_b'], 'ffn_w_gate': out['ffn_w_gate'], 'ffn_w_up': out['ffn_w_up'], 'ffn_w_down': out['ffn_w_down'], 'ln_g': out['ln_g'], 'ln_b': out['ln_b'], 'loss_target': out['loss_target'], 'm_mla_w_in': out['m_mla_w_in'], 'm_mla_g_q': out['m_mla_g_q'], 'm_mla_w_uq': out['m_mla_w_uq'], 'm_mla_g_kv': out['m_mla_g_kv'], 'm_mla_w_uk': out['m_mla_w_uk'], 'm_mla_w_uv': out['m_mla_w_uv'], 'm_mla_w_o': out['m_mla_w_o'], 'm_fox_w_in': out['m_fox_w_in'], 'm_fox_b_f': out['m_fox_b_f'], 'm_fox_w_o': out['m_fox_w_o'], 'm_ada_w': out['m_ada_w'], 'm_ada_b': out['m_ada_b'], 'm_ffn_w_gate': out['m_ffn_w_gate'], 'm_ffn_w_up': out['m_ffn_w_up'], 'm_ffn_w_down': out['m_ffn_w_down'], 'm_ln_g': out['m_ln_g'], 'm_ln_b': out['m_ln_b'], 'v_mla_w_in': out['v_mla_w_in'], 'v_mla_g_q': out['v_mla_g_q'], 'v_mla_w_uq': out['v_mla_w_uq'], 'v_mla_g_kv': out['v_mla_g_kv'], 'v_mla_w_uk': out['v_mla_w_uk'], 'v_mla_w_uv': out['v_mla_w_uv'], 'v_mla_w_o': out['v_mla_w_o'], 'v_fox_w_in': out['v_fox_w_in'], 'v_fox_b_f': out['v_fox_b_f'], 'v_fox_w_o': out['v_fox_w_o'], 'v_ada_w': out['v_ada_w'], 'v_ada_b': out['v_ada_b'], 'v_ffn_w_gate': out['v_ffn_w_gate'], 'v_ffn_w_up': out['v_ffn_w_up'], 'v_ffn_w_down': out['v_ffn_w_down'], 'v_ln_g': out['v_ln_g'], 'v_ln_b': out['v_ln_b']}


def _loss(weights, diff, rest, loss_target):
    with _jax.named_scope("forward"):
        args = {**rest, TWIN_DIFF_INPUT: diff, **{k: w.astype(_WEIGHT_DTYPES[k]) for k, w in weights.items()}}
        y = _forward(args)
    with _jax.named_scope("loss_head"):
        err = _jnp.square(y.astype(_jnp.float32) - loss_target)
        return 0.5 * _jnp.sum(_jnp.mean(err, axis=-1)) if err.ndim else 0.5 * err


def _adamw(w, g, m, v):
    m = ADAM_B1 * m + (1.0 - ADAM_B1) * g
    v = ADAM_B2 * v + (1.0 - ADAM_B2) * _jnp.square(g)
    m_hat = m / (1.0 - ADAM_B1 ** ADAM_STEP)
    v_hat = v / (1.0 - ADAM_B2 ** ADAM_STEP)
    delta = -ADAM_LR * (m_hat / (_jnp.sqrt(v_hat) + ADAM_EPS) + ADAM_WD * w)
    return delta, m, v


def reference(x, c, positions, mla_w_in, mla_g_q, mla_w_uq, mla_g_kv, mla_w_uk, mla_w_uv, mla_w_o, fox_w_in, fox_b_f, fox_w_o, ada_w, ada_b, ffn_w_gate, ffn_w_up, ffn_w_down, ln_g, ln_b, loss_target, m_mla_w_in, m_mla_g_q, m_mla_w_uq, m_mla_g_kv, m_mla_w_uk, m_mla_w_uv, m_mla_w_o, m_fox_w_in, m_fox_b_f, m_fox_w_o, m_ada_w, m_ada_b, m_ffn_w_gate, m_ffn_w_up, m_ffn_w_down, m_ln_g, m_ln_b, v_mla_w_in, v_mla_g_q, v_mla_w_uq, v_mla_g_kv, v_mla_w_uk, v_mla_w_uv, v_mla_w_o, v_fox_w_in, v_fox_b_f, v_fox_w_o, v_ada_w, v_ada_b, v_ffn_w_gate, v_ffn_w_up, v_ffn_w_down, v_ln_g, v_ln_b):
    given = dict(x=x, c=c, positions=positions, mla_w_in=mla_w_in, mla_g_q=mla_g_q, mla_w_uq=mla_w_uq, mla_g_kv=mla_g_kv, mla_w_uk=mla_w_uk, mla_w_uv=mla_w_uv, mla_w_o=mla_w_o, fox_w_in=fox_w_in, fox_b_f=fox_b_f, fox_w_o=fox_w_o, ada_w=ada_w, ada_b=ada_b, ffn_w_gate=ffn_w_gate, ffn_w_up=ffn_w_up, ffn_w_down=ffn_w_down, ln_g=ln_g, ln_b=ln_b, loss_target=loss_target, m_mla_w_in=m_mla_w_in, m_mla_g_q=m_mla_g_q, m_mla_w_uq=m_mla_w_uq, m_mla_g_kv=m_mla_g_kv, m_mla_w_uk=m_mla_w_uk, m_mla_w_uv=m_mla_w_uv, m_mla_w_o=m_mla_w_o, m_fox_w_in=m_fox_w_in, m_fox_b_f=m_fox_b_f, m_fox_w_o=m_fox_w_o, m_ada_w=m_ada_w, m_ada_b=m_ada_b, m_ffn_w_gate=m_ffn_w_gate, m_ffn_w_up=m_ffn_w_up, m_ffn_w_down=m_ffn_w_down, m_ln_g=m_ln_g, m_ln_b=m_ln_b, v_mla_w_in=v_mla_w_in, v_mla_g_q=v_mla_g_q, v_mla_w_uq=v_mla_w_uq, v_mla_g_kv=v_mla_g_kv, v_mla_w_uk=v_mla_w_uk, v_mla_w_uv=v_mla_w_uv, v_mla_w_o=v_mla_w_o, v_fox_w_in=v_fox_w_in, v_fox_b_f=v_fox_b_f, v_fox_w_o=v_fox_w_o, v_ada_w=v_ada_w, v_ada_b=v_ada_b, v_ffn_w_gate=v_ffn_w_gate, v_ffn_w_up=v_ffn_w_up, v_ffn_w_down=v_ffn_w_down, v_ln_g=v_ln_g, v_ln_b=v_ln_b)
    weights = {n: given[n] for n in TWIN_WEIGHTS}
    shared = {n: given[n] for n in SHARED_INPUTS}
    per_example = {n: given[n] for n in ['x', 'c', 'positions']}
    grad_fn = _jax.value_and_grad(_loss, argnums=(0, 1))

    def one_microbatch(ex, loss_target):
        ex = dict(ex)
        diff = ex.pop(TWIN_DIFF_INPUT)
        return grad_fn(weights, diff, {**shared, **ex}, loss_target)

    if N_MICROBATCH == 1:
        loss, (grad_w, grad_x) = one_microbatch(per_example, given["loss_target"])
    else:
        def body(carry, xs):
            loss_sum, grad_sum = carry
            l_k, (gw_k, gx_k) = one_microbatch(xs[0], xs[1])
            with _jax.named_scope("update"):
                return (loss_sum + l_k, _jax.tree.map(_jnp.add, grad_sum, gw_k)), gx_k

        init = (_jnp.zeros((), _jnp.float32), _jax.tree.map(_jnp.zeros_like, weights))
        (loss, grad_w), grad_x = _jax.lax.scan(body, init, (per_example, given["loss_target"]))
    with _jax.named_scope("update"):
        delta_w, new_m, new_v = {}, {}, {}
        for n in TWIN_WEIGHTS:
            delta_w[n], new_m[n], new_v[n] = _adamw(weights[n], grad_w[n], given["m_" + n], given["v_" + n])
    return (loss, grad_x, *[grad_w[n] for n in TWIN_WEIGHTS], *[delta_w[n] for n in TWIN_WEIGHTS],
            *[new_m[n] for n in TWIN_WEIGHTS], *[new_v[n] for n in TWIN_WEIGHTS])
```

```python
import functools
import math

import numpy as np
import jax
import jax.numpy as jnp
from jax import lax
from jax.experimental import pallas as pl
from jax.experimental.pallas import tpu as pltpu

F32 = jnp.float32
BF16 = jnp.bfloat16
MESH = pl.DeviceIdType.MESH

D_MODEL = 1024
DEPTH = 2
MLA_HEADS = 8
MLA_NOPE = 128
MLA_ROPE = 64
MLA_V = 128
MLA_QR = 256
MLA_KVR = 256
ROPE_THETA = 10000.0
FOX_HEADS = 16
FOX_HD = 64
D_FF = 2816
N_CHIPS = 4
FF_CHUNK = D_FF // N_CHIPS
ALPHA = (2.0 * DEPTH) ** 0.25
EPS = 1e-5
ADAM_LR = 0.001
ADAM_B1 = 0.9
ADAM_B2 = 0.999
ADAM_EPS = 1e-08
ADAM_WD = 0.01
ADAM_STEP = 10

VMEM_LIMIT = 56 * 1024 * 1024
TOKEN_TILE = 256
ATTN_TILE = 512
PACK_COLS = 1024
PACK_ROW_ALIGN = 256
ADAMW_BLOCK_BYTES = 1024 * 1024


def _cp(n_axes):
    return pltpu.CompilerParams(dimension_semantics=("arbitrary",) * n_axes, vmem_limit_bytes=VMEM_LIMIT)


def _dot(a, b):
    return jnp.dot(a, b, preferred_element_type=F32)


def _dot_nt(a, b):
    return lax.dot_general(a, b, (((1,), (1,)), ((), ())), preferred_element_type=F32)


def _dot_tn(a, b):
    return lax.dot_general(a, b, (((0,), (0,)), ((), ())), preferred_element_type=F32)


def _dot_f32(a, b):
    return jnp.dot(a, b, preferred_element_type=F32, precision=lax.Precision.HIGHEST)


def _sds(shape, dtype):
    return jax.ShapeDtypeStruct(shape, dtype)


def mod_linear(x, shift, scale, w, out_dtype, name, tn=None, emit_u=False):
    t, d = x.shape
    n = w.shape[1]
    tn = n if tn is None else tn
    tm = TOKEN_TILE
    tps = (t // shift.shape[0]) // tm

    def body(x_ref, sh_ref, sc_ref, w_ref, o_ref, *rest):
        u = (x_ref[...] * (1.0 + sc_ref[...]) + sh_ref[...]).astype(BF16)
        o_ref[...] = _dot(u, w_ref[...]).astype(out_dtype)
        if emit_u:
            @pl.when(pl.program_id(1) == 0)
            def _():
                rest[0][...] = u

    vec = pl.BlockSpec((None, 1, d), lambda i, j: (i // tps, 0, 0))
    out_shape = [_sds((t, n), out_dtype)]
    out_specs = [pl.BlockSpec((tm, tn), lambda i, j: (i, j))]
    if emit_u:
        out_shape.append(_sds((t, d), BF16))
        out_specs.append(pl.BlockSpec((tm, d), lambda i, j: (i, 0)))
    res = pl.pallas_call(
        body, name=name, grid=(t // tm, n // tn),
        in_specs=[pl.BlockSpec((tm, d), lambda i, j: (i, 0)), vec, vec,
                  pl.BlockSpec((d, tn), lambda i, j: (0, j))],
        out_specs=out_specs, out_shape=out_shape, compiler_params=_cp(2),
    )(x, shift, scale, w)
    return res if emit_u else res[0]


def _rms(h, g):
    rstd = lax.rsqrt(jnp.mean(h * h, axis=-1, keepdims=True) + EPS)
    return h * rstd, rstd


def mla_mid_fwd(h, g_q, g_kv, w_uq, w_uk, w_uv, cos8, sin8, cos64, sin64s, swap64, name):
    t = h.shape[0]
    tm = TOKEN_TILE
    hq = MLA_HEADS * MLA_NOPE
    hr = MLA_HEADS * MLA_ROPE // 2

    def body(h_ref, gq_ref, gkv_ref, wuq_ref, wuk_ref, wuv_ref, c8_ref, s8_ref, c64_ref, s64_ref, sw_ref,
             q_ref, kn_ref, v_ref, kr_ref, cq_ref, ckv_ref):
        hh = h_ref[...]
        cq = (_rms(hh[:, :MLA_QR], None)[0] * gq_ref[...]).astype(BF16)
        ckv = (_rms(hh[:, MLA_QR:MLA_QR + MLA_KVR], None)[0] * gkv_ref[...]).astype(BF16)
        cq_ref[...] = cq
        ckv_ref[...] = ckv
        q = _dot(cq, wuq_ref[...])
        x1 = q[:, hq:hq + hr]
        x2 = q[:, hq + hr:]
        cs = c8_ref[...]
        sn = s8_ref[...]
        q_ref[...] = jnp.concatenate([q[:, :hq], x1 * cs - x2 * sn, x2 * cs + x1 * sn], axis=1).astype(BF16)
        kn_ref[...] = _dot(ckv, wuk_ref[...]).astype(BF16)
        v_ref[...] = _dot(ckv, wuv_ref[...]).astype(BF16)
        kr = hh[:, MLA_QR + MLA_KVR:]
        kr_ref[...] = (kr * c64_ref[...] + _dot_f32(kr, sw_ref[...]) * s64_ref[...]).astype(BF16)

    def rows(n):
        return pl.BlockSpec((tm, n), lambda i: (i, 0))

    def whole(a):
        return pl.BlockSpec(a.shape, lambda i: (0,) * a.ndim)

    nq = w_uq.shape[1]
    return pl.pallas_call(
        body, name=name, grid=(t // tm,),
        in_specs=[rows(h.shape[1]), whole(g_q), whole(g_kv), whole(w_uq), whole(w_uk), whole(w_uv),
                  rows(hr), rows(hr), rows(MLA_ROPE), rows(MLA_ROPE), whole(swap64)],
        out_specs=[rows(nq), rows(hq), rows(hq), rows(MLA_ROPE), rows(MLA_QR), rows(MLA_KVR)],
        out_shape=[_sds((t, nq), BF16), _sds((t, hq), BF16), _sds((t, hq), BF16), _sds((t, MLA_ROPE), BF16),
                   _sds((t, MLA_QR), BF16), _sds((t, MLA_KVR), BF16)],
        compiler_params=_cp(1),
    )(h, g_q, g_kv, w_uq, w_uk, w_uv, cos8, sin8, cos64, sin64s, swap64)


def attn_fwd(q, k, v, fq, fk, scale, name):
    b, h, s, dk = q.shape
    dv = v.shape[-1]
    tq = ATTN_TILE
    nq = s // tq
    has_bias = fq is not None

    def body(*refs):
        if has_bias:
            q_ref, k_ref, v_ref, fq_ref, fk_ref, o_ref, lse_ref = refs
        else:
            q_ref, k_ref, v_ref, o_ref, lse_ref = refs
        i = pl.program_id(2)
        qb = q_ref[...]

        def block(j, carry, masked):
            m, l, acc = carry
            start = pl.multiple_of(j * tq, tq)
            kb = k_ref[pl.ds(start, tq), :]
            vb = v_ref[pl.ds(start, tq), :]
            sc = _dot_nt(qb, kb) * scale
            if has_bias:
                sc = sc + fq_ref[...] - fk_ref[j]
            if masked:
                keep = lax.broadcasted_iota(jnp.int32, (tq, tq), 0) >= lax.broadcasted_iota(jnp.int32, (tq, tq), 1)
                sc = jnp.where(keep, sc, -1e30)
            m_new = jnp.maximum(m, jnp.max(sc, axis=1, keepdims=True))
            a = jnp.exp(m - m_new)
            p = jnp.exp(sc - m_new)
            l = a * l + jnp.sum(p, axis=1, keepdims=True)
            acc = a * acc + _dot(p.astype(BF16), vb)
            return m_new, l, acc

        init = (jnp.full((tq, 1), -1e30, F32), jnp.zeros((tq, 1), F32), jnp.zeros((tq, dv), F32))
        carry = lax.fori_loop(0, i, lambda j, c: block(j, c, False), init)
        m, l, acc = block(i, carry, True)
        o_ref[...] = (acc / l).astype(BF16)
        lse_ref[...] = m + jnp.log(l)

    def qspec(n):
        return pl.BlockSpec((None, None, tq, n), lambda bb, hh, i: (bb, hh, i, 0))

    def full(n):
        return pl.BlockSpec((None, None, s, n), lambda bb, hh, i: (bb, hh, 0, 0))

    in_specs = [qspec(dk), full(dk), full(dv)]
    args = [q, k, v]
    if has_bias:
        in_specs += [qspec(1), pl.BlockSpec((None, None, nq, 1, tq), lambda bb, hh, i: (bb, hh, 0, 0, 0))]
        args += [fq, fk]
    return pl.pallas_call(
        body, name=name, grid=(b, h, nq), in_specs=in_specs,
        out_specs=[qspec(dv), qspec(1)],
        out_shape=[_sds((b, h, s, dv), BF16), _sds((b, h, s, 1), F32)],
        compiler_params=_cp(3),
    )(*args)


def _layer_norm(z, g, b):
    mu = jnp.mean(z, axis=-1, keepdims=True)
    zc = z - mu
    rstd = lax.rsqrt(jnp.mean(zc * zc, axis=-1, keepdims=True) + EPS)
    xhat = zc * rstd
    return xhat * g + b, xhat, rstd


def linear_resid_ln(a, w, x_in, gate, ln_g, ln_b, name):
    t, kdim = a.shape
    d = w.shape[1]
    tm = TOKEN_TILE
    tps = (t // gate.shape[0]) // tm

    def body(a_ref, w_ref, x_ref, gt_ref, g_ref, b_ref, y_ref, xo_ref):
        y = _dot(a_ref[...], w_ref[...])
        y_ref[...] = y
        z = ALPHA * x_ref[...] + (1.0 + gt_ref[...]) * y
        xo_ref[...] = _layer_norm(z, g_ref[...], b_ref[...])[0]

    rows = pl.BlockSpec((tm, d), lambda i: (i, 0))
    vec = pl.BlockSpec((1, d), lambda i: (0, 0))
    return pl.pallas_call(
        body, name=name, grid=(t // tm,),
        in_specs=[pl.BlockSpec((tm, kdim), lambda i: (i, 0)), pl.BlockSpec((kdim, d), lambda i: (0, 0)), rows,
                  pl.BlockSpec((None, 1, d), lambda i: (i // tps, 0, 0)), vec, vec],
        out_specs=[rows, rows], out_shape=[_sds((t, d), F32), _sds((t, d), F32)],
        compiler_params=_cp(1),
    )(a, w, x_in, gate, ln_g, ln_b)


def ffn_fwd(x_in, shift, scale, gate, wg, wu, wd, ln_g, ln_b, name):
    t, d = x_in.shape
    c, _, fc = wg.shape
    tm = TOKEN_TILE
    tps = (t // gate.shape[0]) // tm

    def body(x_ref, sh_ref, sc_ref, gt_ref, wg_ref, wu_ref, wd_ref, g_ref, b_ref,
             u_ref, hg_ref, hu_ref, y_ref, xo_ref, acc_ref):
        cc = pl.program_id(1)

        @pl.when(cc == 0)
        def _():
            u_ref[...] = (x_ref[...] * (1.0 + sc_ref[...]) + sh_ref[...]).astype(BF16)
            acc_ref[...] = jnp.zeros_like(acc_ref)

        u = u_ref[...]
        hg = _dot(u, wg_ref[...])
        hu = _dot(u, wu_ref[...])
        hg_ref[...] = hg.astype(BF16)
        hu_ref[...] = hu.astype(BF16)
        act = (hg * jax.nn.sigmoid(hg) * hu).astype(BF16)
        acc_ref[...] += _dot(act, wd_ref[...])

        @pl.when(cc == c - 1)
        def _():
            y = acc_ref[...]
            y_ref[...] = y
            z = ALPHA * x_ref[...] + (1.0 + gt_ref[...]) * y
            xo_ref[...] = _layer_norm(z, g_ref[...], b_ref[...])[0]

    rows = pl.BlockSpec((tm, d), lambda i, cc: (i, 0))
    bvec = pl.BlockSpec((None, 1, d), lambda i, cc: (i // tps, 0, 0))
    vec = pl.BlockSpec((1, d), lambda i, cc: (0, 0))
    hspec = pl.BlockSpec((None, tm, fc), lambda i, cc: (cc, i, 0))
    return pl.pallas_call(
        body, name=name, grid=(t // tm, c),
        in_specs=[rows, bvec, bvec, bvec,
                  pl.BlockSpec((None, d, fc), lambda i, cc: (cc, 0, 0)),
                  pl.BlockSpec((None, d, fc), lambda i, cc: (cc, 0, 0)),
                  pl.BlockSpec((None, fc, d), lambda i, cc: (cc, 0, 0)), vec, vec],
        out_specs=[rows, hspec, hspec, rows, rows],
        out_shape=[_sds((t, d), BF16), _sds((c, t, fc), BF16), _sds((c, t, fc), BF16),
                   _sds((t, d), F32), _sds((t, d), F32)],
        scratch_shapes=[pltpu.VMEM((tm, d), F32)],
        compiler_params=_cp(2),
    )(x_in, shift, scale, gate, wg, wu, wd, ln_g, ln_b)


def fox_gate_fwd(hf, b_f, tri, n_batch, name):
    t, n = hf.shape
    blk = tri.shape[0]
    nb = (t // n_batch) // blk

    def body(hf_ref, b_ref, tri_ref, o_ref, carry_ref):
        @pl.when(pl.program_id(1) == 0)
        def _():
            carry_ref[...] = jnp.zeros_like(carry_ref)

        xx = hf_ref[...] + b_ref[...]
        lf = jnp.minimum(xx, 0.0) - jnp.log(1.0 + jnp.exp(-jnp.abs(xx)))
        cum = _dot_f32(tri_ref[...], lf) + carry_ref[...]
        o_ref[...] = cum
        carry_ref[...] = cum[blk - 1:blk, :]

    return pl.pallas_call(
        body, name=name, grid=(n_batch, nb),
        in_specs=[pl.BlockSpec((blk, n), lambda bb, i: (bb * nb + i, 0)), pl.BlockSpec((1, n), lambda bb, i: (0, 0)),
                  pl.BlockSpec((blk, blk), lambda bb, i: (0, 0))],
        out_specs=pl.BlockSpec((blk, n), lambda bb, i: (bb * nb + i, 0)),
        out_shape=_sds((t, n), F32), scratch_shapes=[pltpu.VMEM((1, n), F32)],
        compiler_params=_cp(2),
    )(hf, b_f, tri)


def loss_grad(x_out, target, name):
    t, d = x_out.shape
    tm = TOKEN_TILE

    def body(x_ref, t_ref, g_ref, l_ref):
        @pl.when(pl.program_id(0) == 0)
        def _():
            l_ref[...] = jnp.zeros_like(l_ref)

        err = x_ref[...] - t_ref[...]
        g_ref[...] = err / d
        l_ref[...] += jnp.sum(err * err, axis=0, keepdims=True)

    rows = pl.BlockSpec((tm, d), lambda i: (i, 0))
    return pl.pallas_call(
        body, name=name, grid=(t // tm,), in_specs=[rows, rows],
        out_specs=[rows, pl.BlockSpec((1, d), lambda i: (0, 0))],
        out_shape=[_sds((t, d), F32), _sds((1, d), F32)], compiler_params=_cp(1),
    )(x_out, target)


def ln_bwd(dxo, x_in, y, gate, ln_g, name):
    t, d = dxo.shape
    nb = gate.shape[0]
    tm = TOKEN_TILE
    tps = (t // nb) // tm

    def body(dxo_ref, x_ref, y_ref, gt_ref, g_ref, dz_ref, dy_ref, dg_ref, db_ref, dgt_ref):
        i = pl.program_id(0)

        @pl.when(i == 0)
        def _():
            dg_ref[...] = jnp.zeros_like(dg_ref)
            db_ref[...] = jnp.zeros_like(db_ref)

        @pl.when(i % tps == 0)
        def _():
            dgt_ref[...] = jnp.zeros_like(dgt_ref)

        yy = y_ref[...]
        g1 = 1.0 + gt_ref[...]
        z = ALPHA * x_ref[...] + g1 * yy
        _, xhat, rstd = _layer_norm(z, 1.0, 0.0)
        dxo_v = dxo_ref[...]
        dg_ref[...] += jnp.sum(dxo_v * xhat, axis=0, keepdims=True)
        db_ref[...] += jnp.sum(dxo_v, axis=0, keepdims=True)
        dxh = dxo_v * g_ref[...]
        dz = rstd * (dxh - jnp.mean(dxh, axis=-1, keepdims=True) - xhat * jnp.mean(dxh * xhat, axis=-1, keepdims=True))
        dz_ref[...] = dz
        dy_ref[...] = (g1 * dz).astype(BF16)
        dgt_ref[...] += jnp.sum(dz * yy, axis=0, keepdims=True)

    rows = pl.BlockSpec((tm, d), lambda i: (i, 0))
    vec = pl.BlockSpec((1, d), lambda i: (0, 0))
    bvec = pl.BlockSpec((None, 1, d), lambda i: (i // tps, 0, 0))
    return pl.pallas_call(
        body, name=name, grid=(t // tm,), in_specs=[rows, rows, rows, bvec, vec],
        out_specs=[rows, rows, vec, vec, bvec],
        out_shape=[_sds((t, d), F32), _sds((t, d), BF16), _sds((1, d), F32), _sds((1, d), F32), _sds((nb, 1, d), F32)],
        compiler_params=_cp(1),
    )(dxo, x_in, y, gate, ln_g)


def _mod_bwd_tail(du, dz_ref, x_ref, sc_ref, dx_ref, dsc_ref, dsh_ref, first):
    @pl.when(first)
    def _():
        dsc_ref[...] = jnp.zeros_like(dsc_ref)
        dsh_ref[...] = jnp.zeros_like(dsh_ref)

    dx_ref[...] = ALPHA * dz_ref[...] + du * (1.0 + sc_ref[...])
    dsc_ref[...] += jnp.sum(du * x_ref[...], axis=0, keepdims=True)
    dsh_ref[...] += jnp.sum(du, axis=0, keepdims=True)


def ffn_bwd(dy, hg, hu, wg, wu, wd, dz, x_in, scale, name):
    t, d = dy.shape
    c, _, fc = wg.shape
    nb = scale.shape[0]
    tm = TOKEN_TILE
    tps = (t // nb) // tm

    def body(dy_ref, hg_ref, hu_ref, wg_ref, wu_ref, wd_ref, dz_ref, x_ref, sc_ref,
             dhg_ref, dhu_ref, act_ref, dx_ref, dsc_ref, dsh_ref, acc_ref):
        i = pl.program_id(0)
        cc = pl.program_id(1)

        @pl.when(cc == 0)
        def _():
            acc_ref[...] = jnp.zeros_like(acc_ref)

        hgv = hg_ref[...].astype(F32)
        huv = hu_ref[...].astype(F32)
        da = _dot_nt(dy_ref[...], wd_ref[...])
        sg = jax.nn.sigmoid(hgv)
        sl = hgv * sg
        act_ref[...] = (sl * huv).astype(BF16)
        dhu = (da * sl).astype(BF16)
        dhg = (da * huv * (sg * (1.0 + hgv * (1.0 - sg)))).astype(BF16)
        dhu_ref[...] = dhu
        dhg_ref[...] = dhg
        acc_ref[...] += _dot_nt(dhg, wg_ref[...]) + _dot_nt(dhu, wu_ref[...])

        @pl.when(cc == c - 1)
        def _():
            _mod_bwd_tail(acc_ref[...], dz_ref, x_ref, sc_ref, dx_ref, dsc_ref, dsh_ref, i % tps == 0)

    rows = pl.BlockSpec((tm, d), lambda i, cc: (i, 0))
    bvec = pl.BlockSpec((None, 1, d), lambda i, cc: (i // tps, 0, 0))
    hspec = pl.BlockSpec((None, tm, fc), lambda i, cc: (cc, i, 0))
    wcol = pl.BlockSpec((None, d, fc), lambda i, cc: (cc, 0, 0))
    return pl.pallas_call(
        body, name=name, grid=(t // tm, c),
        in_specs=[rows, hspec, hspec, wcol, wcol, pl.BlockSpec((None, fc, d), lambda i, cc: (cc, 0, 0)),
                  rows, rows, bvec],
        out_specs=[hspec, hspec, hspec, rows, bvec, bvec],
        out_shape=[_sds((c, t, fc), BF16), _sds((c, t, fc), BF16), _sds((c, t, fc), BF16), _sds((t, d), F32),
                   _sds((nb, 1, d), F32), _sds((nb, 1, d), F32)],
        scratch_shapes=[pltpu.VMEM((tm, d), F32)],
        compiler_params=_cp(2),
    )(dy, hg, hu, wg, wu, wd, dz, x_in, scale)


def linear_nt_mod_bwd(pairs, dz, x_in, scale, name):
    t, d = dz.shape
    nb = scale.shape[0]
    tm = TOKEN_TILE
    tps = (t // nb) // tm
    npairs = len(pairs)

    def body(*refs):
        dh_refs = refs[:npairs]
        w_refs = refs[npairs:2 * npairs]
        dz_ref, x_ref, sc_ref, dx_ref, dsc_ref, dsh_ref = refs[2 * npairs:]
        du = _dot_nt(dh_refs[0][...], w_refs[0][...])
        for kk in range(1, npairs):
            du = du + _dot_nt(dh_refs[kk][...], w_refs[kk][...])
        _mod_bwd_tail(du, dz_ref, x_ref, sc_ref, dx_ref, dsc_ref, dsh_ref, pl.program_id(0) % tps == 0)

    rows = pl.BlockSpec((tm, d), lambda i: (i, 0))
    bvec = pl.BlockSpec((None, 1, d), lambda i: (i // tps, 0, 0))
    in_specs = [pl.BlockSpec((tm, dh.shape[1]), lambda i: (i, 0)) for dh, _ in pairs]
    in_specs += [pl.BlockSpec(w.shape, lambda i: (0, 0)) for _, w in pairs]
    in_specs += [rows, rows, bvec]
    return pl.pallas_call(
        body, name=name, grid=(t // tm,), in_specs=in_specs,
        out_specs=[rows, bvec, bvec],
        out_shape=[_sds((t, d), F32), _sds((nb, 1, d), F32), _sds((nb, 1, d), F32)],
        compiler_params=_cp(1),
    )(*[dh for dh, _ in pairs], *[w for _, w in pairs], dz, x_in, scale)


def linear_nt_delta(dy, w_o, o, head_sel, name):
    t, d = dy.shape
    hdv = w_o.shape[0]
    tm = TOKEN_TILE

    def body(dy_ref, w_ref, o_ref, sel_ref, do_ref, dl_ref):
        do = _dot_nt(dy_ref[...], w_ref[...])
        do_ref[...] = do.astype(BF16)
        dl_ref[...] = _dot_f32(do * o_ref[...].astype(F32), sel_ref[...])

    return pl.pallas_call(
        body, name=name, grid=(t // tm,),
        in_specs=[pl.BlockSpec((tm, d), lambda i: (i, 0)), pl.BlockSpec((hdv, d), lambda i: (0, 0)),
                  pl.BlockSpec((tm, hdv), lambda i: (i, 0)), pl.BlockSpec(head_sel.shape, lambda i: (0, 0))],
        out_specs=[pl.BlockSpec((tm, hdv), lambda i: (i, 0)), pl.BlockSpec((tm, 128), lambda i: (i, 0))],
        out_shape=[_sds((t, hdv), BF16), _sds((t, 128), F32)], compiler_params=_cp(1),
    )(dy, w_o, o, head_sel)


def attn_bwd(q, k, v, do, lse_r, delta_r, fq_r, fk_c, scale, name):
    b, h, s, dk = q.shape
    dv = v.shape[-1]
    tk = ATTN_TILE
    nk = s // tk
    has_bias = fq_r is not None

    def body(*refs):
        if has_bias:
            (q_ref, k_ref, v_ref, do_ref, lse_ref, dl_ref, fq_ref, fk_ref,
             dq_ref, dk_ref, dv_ref, dfk_ref, dfq_ref) = refs
        else:
            q_ref, k_ref, v_ref, do_ref, lse_ref, dl_ref, dq_ref, dk_ref, dv_ref = refs
        j = pl.program_id(2)
        kb = k_ref[...]
        vb = v_ref[...]

        @pl.when(j == 0)
        def _():
            dq_ref[...] = jnp.zeros_like(dq_ref)
            if has_bias:
                dfq_ref[...] = jnp.zeros_like(dfq_ref)

        def block(i, carry, masked):
            dk_acc, dv_acc, dfk_acc = carry
            start = pl.multiple_of(i * tk, tk)
            qb = q_ref[pl.ds(start, tk), :]
            dob = do_ref[pl.ds(start, tk), :]
            st = _dot_nt(kb, qb) * scale
            if has_bias:
                st = st + fq_ref[i] - fk_ref[...]
            if masked:
                keep = lax.broadcasted_iota(jnp.int32, (tk, tk), 1) >= lax.broadcasted_iota(jnp.int32, (tk, tk), 0)
                st = jnp.where(keep, st, -1e30)
            pt = jnp.exp(st - lse_ref[i])
            dv_acc = dv_acc + _dot(pt.astype(BF16), dob)
            dpt = _dot_nt(vb, dob)
            dst = pt * (dpt - dl_ref[i])
            if has_bias:
                dfk_acc = dfk_acc - jnp.sum(dst, axis=1, keepdims=True)
                dfq_ref[i] += jnp.sum(dst, axis=0, keepdims=True)
            dsb = (dst * scale).astype(BF16)
            dk_acc = dk_acc + _dot(dsb, qb)
            dq_ref[pl.ds(start, tk), :] += _dot_tn(dsb, kb)
            return dk_acc, dv_acc, dfk_acc

        init = (jnp.zeros((tk, dk), F32), jnp.zeros((tk, dv), F32), jnp.zeros((tk, 1), F32))
        carry = block(j, init, True)
        dk_acc, dv_acc, dfk_acc = lax.fori_loop(j + 1, nk, lambda i, c: block(i, c, False), carry)
        dk_ref[...] = dk_acc.astype(BF16)
        dv_ref[...] = dv_acc.astype(BF16)
        if has_bias:
            dfk_ref[...] = dfk_acc

    def full(n):
        return pl.BlockSpec((None, None, s, n), lambda bb, hh, j: (bb, hh, 0, 0))

    def kspec(n):
        return pl.BlockSpec((None, None, tk, n), lambda bb, hh, j: (bb, hh, j, 0))

    rowv = pl.BlockSpec((None, None, nk, 1, tk), lambda bb, hh, j: (bb, hh, 0, 0, 0))
    in_specs = [full(dk), kspec(dk), kspec(dv), full(dv), rowv, rowv]
    args = [q, k, v, do, lse_r, delta_r]
    out_specs = [full(dk), kspec(dk), kspec(dv)]
    out_shape = [_sds((b, h, s, dk), F32), _sds((b, h, s, dk), BF16), _sds((b, h, s, dv), BF16)]
    if has_bias:
        in_specs += [rowv, kspec(1)]
        args += [fq_r, fk_c]
        out_specs += [kspec(1), rowv]
        out_shape += [_sds((b, h, s, 1), F32), _sds((b, h, nk, 1, tk), F32)]
    return pl.pallas_call(
        body, name=name, grid=(b, h, nk), in_specs=in_specs, out_specs=out_specs, out_shape=out_shape,
        compiler_params=_cp(3),
    )(*args)


def mla_mid_bwd(dq, dkn, dv, dkr_heads, h, g_q, g_kv, w_uq, w_uk, w_uv, cos8, sin8, cos64, sin64s, swap64, head_sum, name):
    t = h.shape[0]
    tm = TOKEN_TILE
    hq = MLA_HEADS * MLA_NOPE
    hr = MLA_HEADS * MLA_ROPE // 2
    nq = w_uq.shape[1]

    def body(dq_ref, dkn_ref, dv_ref, dkr_ref, h_ref, gq_ref, gkv_ref, wuq_ref, wuk_ref, wuv_ref,
             c8_ref, s8_ref, c64_ref, s64_ref, sw_ref, hs_ref, dh_ref, dqp_ref, dgq_ref, dgkv_ref):
        @pl.when(pl.program_id(0) == 0)
        def _():
            dgq_ref[...] = jnp.zeros_like(dgq_ref)
            dgkv_ref[...] = jnp.zeros_like(dgkv_ref)

        dqv = dq_ref[...].astype(F32)
        o1 = dqv[:, hq:hq + hr]
        o2 = dqv[:, hq + hr:]
        cs = c8_ref[...]
        sn = s8_ref[...]
        dqp = jnp.concatenate([dqv[:, :hq], o1 * cs + o2 * sn, o2 * cs - o1 * sn], axis=1).astype(BF16)
        dqp_ref[...] = dqp
        dcq = _dot_nt(dqp, wuq_ref[...])
        dckv = _dot_nt(dkn_ref[...], wuk_ref[...]) + _dot_nt(dv_ref[...], wuv_ref[...])
        hh = h_ref[...]

        def rms_bwd(hpart, g, dc, dg_ref):
            hhat, rstd = _rms(hpart, None)
            dg_ref[...] += jnp.sum(dc * hhat, axis=0, keepdims=True)
            dcg = dc * g
            return rstd * (dcg - hhat * jnp.mean(dcg * hhat, axis=-1, keepdims=True))

        dhq = rms_bwd(hh[:, :MLA_QR], gq_ref[...], dcq, dgq_ref)
        dhkv = rms_bwd(hh[:, MLA_QR:MLA_QR + MLA_KVR], gkv_ref[...], dckv, dgkv_ref)
        dkr = _dot(dkr_ref[...], hs_ref[...])
        dkr_pre = dkr * c64_ref[...] + _dot_f32(dkr * s64_ref[...], sw_ref[...])
        dh_ref[...] = jnp.concatenate([dhq, dhkv, dkr_pre], axis=1).astype(BF16)

    def rows(n):
        return pl.BlockSpec((tm, n), lambda i: (i, 0))

    def whole(a):
        return pl.BlockSpec(a.shape, lambda i: (0,) * a.ndim)

    return pl.pallas_call(
        body, name=name, grid=(t // tm,),
        in_specs=[rows(nq), rows(hq), rows(hq), rows(MLA_HEADS * MLA_ROPE), rows(h.shape[1]), whole(g_q), whole(g_kv),
                  whole(w_uq), whole(w_uk), whole(w_uv), rows(hr), rows(hr), rows(MLA_ROPE), rows(MLA_ROPE),
                  whole(swap64), whole(head_sum)],
        out_specs=[rows(h.shape[1]), rows(nq), pl.BlockSpec((1, MLA_QR), lambda i: (0, 0)),
                   pl.BlockSpec((1, MLA_KVR), lambda i: (0, 0))],
        out_shape=[_sds((t, h.shape[1]), BF16), _sds((t, nq), BF16), _sds((1, MLA_QR), F32), _sds((1, MLA_KVR), F32)],
        compiler_params=_cp(1),
    )(dq, dkn, dv, dkr_heads, h, g_q, g_kv, w_uq, w_uk, w_uv, cos8, sin8, cos64, sin64s, swap64, head_sum)


def fox_gate_bwd(dcum, hf, b_f, triu, n_batch, name):
    t, n = hf.shape
    blk = triu.shape[0]
    nb = (t // n_batch) // blk

    def body(dc_ref, hf_ref, b_ref, tri_ref, o_ref, db_ref, carry_ref):
        @pl.when(pl.program_id(1) == 0)
        def _():
            carry_ref[...] = jnp.zeros_like(carry_ref)

        @pl.when((pl.program_id(0) == 0) & (pl.program_id(1) == 0))
        def _():
            db_ref[...] = jnp.zeros_like(db_ref)

        rc = _dot_f32(tri_ref[...], dc_ref[...]) + carry_ref[...]
        carry_ref[...] = rc[0:1, :]
        dhf = rc * jax.nn.sigmoid(-(hf_ref[...] + b_ref[...]))
        o_ref[...] = dhf.astype(BF16)
        db_ref[...] += jnp.sum(dhf, axis=0, keepdims=True)

    rev = pl.BlockSpec((blk, n), lambda bb, i: (bb * nb + nb - 1 - i, 0))
    return pl.pallas_call(
        body, name=name, grid=(n_batch, nb),
        in_specs=[rev, rev, pl.BlockSpec((1, n), lambda bb, i: (0, 0)), pl.BlockSpec((blk, blk), lambda bb, i: (0, 0))],
        out_specs=[rev, pl.BlockSpec((1, n), lambda bb, i: (0, 0))],
        out_shape=[_sds((t, n), BF16), _sds((1, n), F32)], scratch_shapes=[pltpu.VMEM((1, n), F32)],
        compiler_params=_cp(2),
    )(dcum, hf, b_f, triu)


def wgrad(a, bm, name, bt=512):
    ca, t, kd = a.shape
    cb, _, nd = bm.shape
    c = max(ca, cb)
    bn = nd
    if nd > 1024 and nd % 1024 == 0:
        bn = 1024

    def body(a_ref, b_ref, o_ref):
        @pl.when(pl.program_id(2) == 0)
        def _():
            o_ref[...] = jnp.zeros_like(o_ref)

        o_ref[...] += _dot_tn(a_ref[...], b_ref[...])

    return pl.pallas_call(
        body, name=name, grid=(c, nd // bn, t // bt),
        in_specs=[pl.BlockSpec((None, bt, kd), lambda cc, n, tt: (cc if ca > 1 else 0, tt, 0)),
                  pl.BlockSpec((None, bt, bn), lambda cc, n, tt: (cc if cb > 1 else 0, tt, n))],
        out_specs=pl.BlockSpec((None, kd, bn), lambda cc, n, tt: (cc, 0, n)),
        out_shape=_sds((c, kd, nd), F32), compiler_params=_cp(3),
    )(a, bm)


def ada_mod_part(c_all, ada_w, name):
    nl, d, n = ada_w.shape
    rows = c_all.shape[0]
    tn = 512

    def body(c_ref, w_ref, o_ref):
        cv = c_ref[...]
        act = (cv * jax.nn.sigmoid(cv)).astype(BF16)
        o_ref[...] = _dot(act, w_ref[...].astype(BF16))

    return pl.pallas_call(
        body, name=name, grid=(nl, n // tn),
        in_specs=[pl.BlockSpec((rows, d), lambda l, j: (0, 0)), pl.BlockSpec((None, d, tn), lambda l, j: (l, 0, j))],
        out_specs=pl.BlockSpec((None, rows, tn), lambda l, j: (l, 0, j)),
        out_shape=_sds((nl, rows, n), F32), compiler_params=_cp(2),
    )(c_all, ada_w)


def ada_grad(c_all_t, dmod, name):
    nl, rows, n = dmod.shape
    d = c_all_t.shape[0]
    tn = 512

    def body(c_ref, dm_ref, o_ref):
        cv = c_ref[...]
        act = (cv * jax.nn.sigmoid(cv)).astype(BF16)
        o_ref[...] = _dot(act, dm_ref[...].astype(BF16))

    return pl.pallas_call(
        body, name=name, grid=(nl, n // tn),
        in_specs=[pl.BlockSpec((d, rows), lambda l, j: (0, 0)), pl.BlockSpec((None, rows, tn), lambda l, j: (l, 0, j))],
        out_specs=pl.BlockSpec((None, d, tn), lambda l, j: (l, 0, j)),
        out_shape=_sds((nl, d, n), F32), compiler_params=_cp(2),
    )(c_all_t, dmod)


def sum_leading(a, name):
    g, r, n = a.shape

    def body(a_ref, o_ref):
        acc = a_ref[0]
        for kk in range(1, g):
            acc = acc + a_ref[kk]
        o_ref[...] = acc

    return pl.pallas_call(
        body, name=name, grid=(1,), in_specs=[pl.BlockSpec((g, r, n), lambda i: (0, 0, 0))],
        out_specs=pl.BlockSpec((r, n), lambda i: (0, 0)), out_shape=_sds((r, n), F32), compiler_params=_cp(1),
    )(a)


def adamw(w, g, m, v, name):
    r, n = w.shape
    br = r
    for cand in (512, 256, 128, 64, 32, 16, 8):
        if r % cand == 0 and r > cand and cand * n * 4 <= ADAMW_BLOCK_BYTES:
            br = cand
            break
    c1 = 1.0 - ADAM_B1 ** ADAM_STEP
    c2 = 1.0 - ADAM_B2 ** ADAM_STEP

    def body(w_ref, g_ref, m_ref, v_ref, d_ref, mo_ref, vo_ref):
        gv = g_ref[...]
        mn = ADAM_B1 * m_ref[...] + (1.0 - ADAM_B1) * gv
        vn = ADAM_B2 * v_ref[...] + (1.0 - ADAM_B2) * (gv * gv)
        mo_ref[...] = mn
        vo_ref[...] = vn
        d_ref[...] = -ADAM_LR * ((mn / c1) / (jnp.sqrt(vn / c2) + ADAM_EPS) + ADAM_WD * w_ref[...])

    spec = pl.BlockSpec((br, n), lambda i: (i, 0))
    return pl.pallas_call(
        body, name=name, grid=(r // br,), in_specs=[spec] * 4, out_specs=[spec] * 3,
        out_shape=[_sds((r, n), F32)] * 3, compiler_params=_cp(1),
    )(w, g, m, v)


def _place():
    return lax.axis_index("x"), lax.axis_index("y"), lax.axis_index("c")


def all_gather8(x_blk, name):
    m_per, n = x_blk.shape

    def body(x_ref, out_ref, send_sems, recv_sems, local_sem):
        x, y, c = _place()
        me, sibling = (x, y, c), (x, y, 1 - c)
        chips = [(1 - x, y), (x, 1 - y), (1 - x, 1 - y)]

        def rows(px, py, pc):
            return out_ref.at[pl.ds((4 * px + 2 * py + pc) * m_per, m_per), :]

        def copy(k, block, to, src=None):
            return pltpu.make_async_remote_copy(
                src_ref=rows(*block) if src is None else src, dst_ref=rows(*block),
                send_sem=send_sems.at[k], recv_sem=recv_sems.at[k], device_id=to, device_id_type=MESH)

        mine = pltpu.make_async_copy(x_ref, rows(*me), local_sem)
        mine.start()
        first = [copy(0, me, sibling, src=x_ref)]
        first += [copy(1 + j, me, (*chip, c), src=x_ref) for j, chip in enumerate(chips)]
        for cp in first:
            cp.start()
        passed = [copy(4 + j, (*chip, c), sibling) for j, chip in enumerate(chips)]
        for j, chip in enumerate(chips):
            copy(1 + j, (*chip, c), me).wait_recv()
            passed[j].start()
        copy(0, sibling, me).wait_recv()
        for j, chip in enumerate(chips):
            copy(4 + j, (*chip, 1 - c), me).wait_recv()
        for cp in first + passed:
            cp.wait_send()
        mine.wait()

    return pl.pallas_call(
        body, name=name, out_shape=_sds((8 * m_per, n), x_blk.dtype),
        in_specs=[pl.BlockSpec(memory_space=pltpu.VMEM)], out_specs=pl.BlockSpec(memory_space=pltpu.VMEM),
        scratch_shapes=[pltpu.SemaphoreType.DMA((7,)), pltpu.SemaphoreType.DMA((7,)), pltpu.SemaphoreType.DMA],
        compiler_params=pltpu.CompilerParams(vmem_limit_bytes=VMEM_LIMIT),
    )(x_blk)


def all_gather_chips(w_pack, name):
    r, n = w_pack.shape
    half = r // 2

    def body(w_ref, out_ref, send_sems, recv_sems, local_sem):
        x, y, c = _place()
        sibling = (x, y, 1 - c)
        chips = [(1 - x, y), (x, 1 - y), (1 - x, 1 - y)]

        def blk(px, py, hh):
            return out_ref.at[2 * px + py, pl.ds(hh * half, half), :]

        def copy(k, block, to, src=None):
            return pltpu.make_async_remote_copy(
                src_ref=blk(*block) if src is None else src, dst_ref=blk(*block),
                send_sem=send_sems.at[k], recv_sem=recv_sems.at[k], device_id=to, device_id_type=MESH)

        mine = pltpu.make_async_copy(w_ref, out_ref.at[2 * x + y], local_sem)
        mine.start()
        my_half = w_ref.at[pl.ds(c * half, half), :]
        first = [copy(j, (x, y, c), (*chip, c), src=my_half) for j, chip in enumerate(chips)]
        for cp in first:
            cp.start()
        passed = [copy(3 + j, (*chip, c), sibling) for j, chip in enumerate(chips)]
        for j, chip in enumerate(chips):
            copy(j, (*chip, c), (x, y, c)).wait_recv()
            passed[j].start()
        for j, chip in enumerate(chips):
            copy(3 + j, (*chip, 1 - c), (x, y, c)).wait_recv()
        for cp in first + passed:
            cp.wait_send()
        mine.wait()

    return pl.pallas_call(
        body, name=name, out_shape=_sds((N_CHIPS, r, n), w_pack.dtype),
        in_specs=[pl.BlockSpec(memory_space=pl.ANY)], out_specs=pl.BlockSpec(memory_space=pl.ANY),
        scratch_shapes=[pltpu.SemaphoreType.DMA((6,)), pltpu.SemaphoreType.DMA((6,)), pltpu.SemaphoreType.DMA],
    )(w_pack)


def sibling_swap_halves(g_pack, name):
    nch, _, half, n = g_pack.shape

    def body(g_ref, a_ref, send_sems, recv_sems):
        x, y, c = _place()
        cps = [pltpu.make_async_remote_copy(
            src_ref=g_ref.at[j, 1 - c], dst_ref=a_ref.at[j], send_sem=send_sems.at[j], recv_sem=recv_sems.at[j],
            device_id=(x, y, 1 - c), device_id_type=MESH) for j in range(nch)]
        for cp in cps:
            cp.start()
        for cp in cps:
            cp.wait()

    return pl.pallas_call(
        body, name=name, out_shape=_sds((nch, half, n), g_pack.dtype),
        in_specs=[pl.BlockSpec(memory_space=pl.ANY)], out_specs=pl.BlockSpec(memory_space=pl.ANY),
        scratch_shapes=[pltpu.SemaphoreType.DMA((nch,)), pltpu.SemaphoreType.DMA((nch,))],
    )(g_pack)


def add_own_half(g_pack, recv, core, name):
    nch, _, half, n = g_pack.shape
    br = half // 4

    def body(c_ref, g_ref, a_ref, o_ref):
        o_ref[...] = (g_ref[...] + a_ref[...]).astype(BF16)

    return pl.pallas_call(
        body, name=name,
        grid_spec=pltpu.PrefetchScalarGridSpec(
            num_scalar_prefetch=1, grid=(nch, half // br),
            in_specs=[pl.BlockSpec((None, None, br, n), lambda j, rr, cref: (j, cref[0], rr, 0)),
                      pl.BlockSpec((None, br, n), lambda j, rr, cref: (j, rr, 0))],
            out_specs=pl.BlockSpec((None, br, n), lambda j, rr, cref: (j, rr, 0))),
        out_shape=_sds((nch, half, n), BF16), compiler_params=_cp(2),
    )(core, g_pack, recv)


def chip_exchange(s_pack, name):
    nch, half, n = s_pack.shape

    def body(s_ref, b_ref, send_sems, recv_sems, local_sem):
        x, y, c = _place()
        k = 2 * x + y
        chips = [(1 - x, y), (x, 1 - y), (1 - x, 1 - y)]
        mine = pltpu.make_async_copy(s_ref.at[k], b_ref.at[k], local_sem)
        mine.start()
        cps = [pltpu.make_async_remote_copy(
            src_ref=s_ref.at[2 * px + py], dst_ref=b_ref.at[k], send_sem=send_sems.at[j], recv_sem=recv_sems.at[j],
            device_id=(px, py, c), device_id_type=MESH) for j, (px, py) in enumerate(chips)]
        for cp in cps:
            cp.start()
        for j, (px, py) in enumerate(chips):
            pltpu.make_async_remote_copy(
                src_ref=s_ref.at[k], dst_ref=b_ref.at[2 * px + py], send_sem=send_sems.at[j], recv_sem=recv_sems.at[j],
                device_id=(px, py, c), device_id_type=MESH).wait_recv()
        for cp in cps:
            cp.wait_send()
        mine.wait()

    return pl.pallas_call(
        body, name=name, out_shape=_sds((nch, half, n), s_pack.dtype),
        in_specs=[pl.BlockSpec(memory_space=pl.ANY)], out_specs=pl.BlockSpec(memory_space=pl.ANY),
        scratch_shapes=[pltpu.SemaphoreType.DMA((3,)), pltpu.SemaphoreType.DMA((3,)), pltpu.SemaphoreType.DMA],
    )(s_pack)


def sum_pieces(b_pack, name):
    nch, half, n = b_pack.shape
    br = half // 4

    def body(b_ref, o_ref):
        acc = b_ref[0].astype(F32)
        for kk in range(1, nch):
            acc = acc + b_ref[kk].astype(F32)
        o_ref[...] = acc

    return pl.pallas_call(
        body, name=name, grid=(half // br,), in_specs=[pl.BlockSpec((nch, br, n), lambda i: (0, i, 0))],
        out_specs=pl.BlockSpec((br, n), lambda i: (i, 0)), out_shape=_sds((half, n), F32), compiler_params=_cp(1),
    )(b_pack)


def sibling_join_halves(f_half, name):
    half, n = f_half.shape

    def body(f_ref, o_ref, send_sem, recv_sem, local_sem):
        x, y, c = _place()
        mine = pltpu.make_async_copy(f_ref, o_ref.at[c], local_sem)
        mine.start()
        cp = pltpu.make_async_remote_copy(
            src_ref=f_ref, dst_ref=o_ref.at[c], send_sem=send_sem, recv_sem=recv_sem,
            device_id=(x, y, 1 - c), device_id_type=MESH)
        cp.start()
        pltpu.make_async_remote_copy(
            src_ref=f_ref, dst_ref=o_ref.at[1 - c], send_sem=send_sem, recv_sem=recv_sem,
            device_id=(x, y, 1 - c), device_id_type=MESH).wait_recv()
        cp.wait_send()
        mine.wait()

    return pl.pallas_call(
        body, name=name, out_shape=_sds((2, half, n), f_half.dtype),
        in_specs=[pl.BlockSpec(memory_space=pl.ANY)], out_specs=pl.BlockSpec(memory_space=pl.ANY),
        scratch_shapes=[pltpu.SemaphoreType.DMA, pltpu.SemaphoreType.DMA, pltpu.SemaphoreType.DMA],
    )(f_half)


_PACKED = ("mla_w_in", "mla_w_uq", "mla_w_uk", "mla_w_uv", "mla_w_o", "fox_w_in", "fox_w_o",
           "ffn_w_gate", "ffn_w_up", "ffn_w_down")


def _pack_rows(total):
    rows = -(-total // PACK_COLS)
    return -(-rows // PACK_ROW_ALIGN) * PACK_ROW_ALIGN


def _pack(blocks, dtype):
    g = blocks[0].shape[0]
    flat = jnp.concatenate([b.reshape(g, -1).astype(dtype) for b in blocks], axis=1)
    rows = _pack_rows(flat.shape[1])
    flat = jnp.pad(flat, ((0, 0), (0, rows * PACK_COLS - flat.shape[1])))
    return flat.reshape(g, rows, PACK_COLS)


def _unpack(packed, shapes):
    lead = packed.shape[:-2]
    flat = packed.reshape(*lead, -1)
    out, off = [], 0
    for shp in shapes:
        size = math.prod(shp)
        out.append(flat[..., off:off + size].reshape(*lead, *shp))
        off += size
    return out


def _cols_to_full(g):
    return jnp.transpose(g, (1, 0, 2)).reshape(g.shape[1], -1)


def _full_to_cols(w):
    k, n4 = w.shape
    return jnp.transpose(w.reshape(k, N_CHIPS, n4 // N_CHIPS), (1, 0, 2))


def _uq_perm():
    per = MLA_NOPE + MLA_ROPE
    half = MLA_ROPE // 2
    nope = [h * per + d for h in range(MLA_HEADS) for d in range(MLA_NOPE)]
    r1 = [h * per + MLA_NOPE + r for h in range(MLA_HEADS) for r in range(half)]
    r2 = [h * per + MLA_NOPE + half + r for h in range(MLA_HEADS) for r in range(half)]
    perm = np.array(nope + r1 + r2, dtype=np.int32)
    return perm, np.argsort(perm).astype(np.int32)


def _heads(a, nb, h):
    t = a.shape[0]
    return jnp.transpose(a.reshape(nb, t // nb, h, -1), (0, 2, 1, 3))


def _unheads(a):
    nb, h, s, dd = a.shape
    return jnp.transpose(a, (0, 2, 1, 3)).reshape(nb * s, h * dd)


def _row_tiles(a):
    nb, h, s, _ = a.shape
    return a.reshape(nb, h, s // ATTN_TILE, 1, ATTN_TILE)


def _local_step(x, positions, target, mods, wts, ln_g, ln_b, mla_g_q, mla_g_kv, fox_b_f):
    nb, s, d = x.shape
    t = nb * s
    x0 = x.reshape(t, d)
    tgt = target.reshape(t, d)
    perm, inv_perm = _uq_perm()

    half = MLA_ROPE // 2
    inv_freq = ROPE_THETA ** (-jnp.arange(half, dtype=F32) / half)
    ang = positions.astype(F32).reshape(t, 1) * inv_freq
    cos, sin = jnp.cos(ang), jnp.sin(ang)
    cos8, sin8 = jnp.tile(cos, (1, MLA_HEADS)), jnp.tile(sin, (1, MLA_HEADS))
    cos64 = jnp.concatenate([cos, cos], axis=1)
    sin64s = jnp.concatenate([-sin, sin], axis=1)
    swap64 = jnp.asarray(np.roll(np.eye(MLA_ROPE, dtype=np.float32), half, axis=1))
    head_sum = jnp.asarray(np.tile(np.eye(MLA_ROPE, dtype=np.float32), (MLA_HEADS, 1)), dtype=BF16)
    sel_mla = jnp.asarray(np.pad(np.kron(np.eye(MLA_HEADS, dtype=np.float32), np.ones((MLA_V, 1), np.float32)),
                                 ((0, 0), (0, 128 - MLA_HEADS))))
    sel_fox = jnp.asarray(np.pad(np.kron(np.eye(FOX_HEADS, dtype=np.float32), np.ones((FOX_HD, 1), np.float32)),
                                 ((0, 0), (0, 128 - FOX_HEADS))))
    tri = jnp.asarray(np.tril(np.ones((128, 128), np.float32)))
    triu = jnp.asarray(np.triu(np.ones((128, 128), np.float32)))

    def vec(a):
        return a.reshape(1, -1)

    w_uq_p = wts["mla_w_uq"][:, perm]
    fox_w_qkv = wts["fox_w_in"][:, :3 * d]
    fox_w_f = jnp.pad(wts["fox_w_in"][:, 3 * d:], ((0, 0), (0, 128 - FOX_HEADS)))
    b_f_pad = jnp.pad(fox_b_f.reshape(1, -1), ((0, 0), (0, 128 - FOX_HEADS)))

    sh_a, sc_a, gt_a, sh_f, sc_f, gt_f = mods[0]
    h_in, u_m = mod_linear(x0, sh_a, sc_a, wts["mla_w_in"], F32, "mla_in", emit_u=True)
    q_m, kn_m, v_m, kr_m, cq_m, ckv_m = mla_mid_fwd(
        h_in, vec(mla_g_q), vec(mla_g_kv), w_uq_p, wts["mla_w_uk"], wts["mla_w_uv"], cos8, sin8, cos64, sin64s, swap64,
        "mla_mid")
    hq = MLA_HEADS * MLA_NOPE
    hr = MLA_HEADS * half
    q_cat = jnp.concatenate([q_m[:, :hq].reshape(nb, s, MLA_HEADS, MLA_NOPE),
                             q_m[:, hq:hq + hr].reshape(nb, s, MLA_HEADS, half),
                             q_m[:, hq + hr:].reshape(nb, s, MLA_HEADS, half)], axis=-1)
    q_cat = jnp.transpose(q_cat, (0, 2, 1, 3))
    k_cat = jnp.concatenate([kn_m.reshape(nb, s, MLA_HEADS, MLA_NOPE),
                             jnp.broadcast_to(kr_m.reshape(nb, s, 1, MLA_ROPE), (nb, s, MLA_HEADS, MLA_ROPE))], axis=-1)
    k_cat = jnp.transpose(k_cat, (0, 2, 1, 3))
    v_h = _heads(v_m, nb, MLA_HEADS)
    scale_m = (MLA_NOPE + MLA_ROPE) ** -0.5
    o_m, lse_m = attn_fwd(q_cat, k_cat, v_h, None, None, scale_m, "mla_attn")
    o_m2 = _unheads(o_m)
    y0, x1 = linear_resid_ln(o_m2, wts["mla_w_o"], x0, gt_a, vec(ln_g[0, 0]), vec(ln_b[0, 0]), "mla_out")
    u_f0, hg0, hu0, y1, x2 = ffn_fwd(x1, sh_f, sc_f, gt_f, wts["ffn_w_gate"][:, 0], wts["ffn_w_up"][:, 0],
                                     wts["ffn_w_down"][:, 0], vec(ln_g[0, 1]), vec(ln_b[0, 1]), "ffn0")
    sh_a1, sc_a1, gt_a1, sh_f1, sc_f1, gt_f1 = mods[1]
    qkv, u_x = mod_linear(x2, sh_a1, sc_a1, fox_w_qkv, BF16, "fox_qkv", tn=1024, emit_u=True)
    hf = mod_linear(x2, sh_a1, sc_a1, fox_w_f, F32, "fox_f")
    cum = fox_gate_fwd(hf, b_f_pad, tri, nb, "fox_gate")
    cum_h = jnp.transpose(cum[:, :FOX_HEADS].reshape(nb, s, FOX_HEADS), (0, 2, 1))[..., None]
    q_x = _heads(qkv[:, :d], nb, FOX_HEADS)
    k_x = _heads(qkv[:, d:2 * d], nb, FOX_HEADS)
    v_x = _heads(qkv[:, 2 * d:], nb, FOX_HEADS)
    scale_x = FOX_HD ** -0.5
    o_x, lse_x = attn_fwd(q_x, k_x, v_x, cum_h, _row_tiles(cum_h), scale_x, "fox_attn")
    o_x2 = _unheads(o_x)
    y2, x3 = linear_resid_ln(o_x2, wts["fox_w_o"], x2, gt_a1, vec(ln_g[1, 0]), vec(ln_b[1, 0]), "fox_out")
    u_f1, hg1, hu1, y3, x4 = ffn_fwd(x3, sh_f1, sc_f1, gt_f1, wts["ffn_w_gate"][:, 1], wts["ffn_w_up"][:, 1],
                                     wts["ffn_w_down"][:, 1], vec(ln_g[1, 1]), vec(ln_b[1, 1]), "ffn1")
    dx4, sq_err = loss_grad(x4, tgt, "loss")
    loss_part = 0.5 * jnp.sum(sq_err) / d

    grads = {}
    dz3, dy3, dg11, db11, dgt_f1 = ln_bwd(dx4, x3, y3, gt_f1, vec(ln_g[1, 1]), "ffn1_ln_bwd")
    dhg1, dhu1, act1, dx3, dsc_f1, dsh_f1 = ffn_bwd(dy3, hg1, hu1, wts["ffn_w_gate"][:, 1], wts["ffn_w_up"][:, 1],
                                                    wts["ffn_w_down"][:, 1], dz3, x3, sc_f1, "ffn1_bwd")
    dwg1 = wgrad(u_f1[None], dhg1, "ffn1_dwg")
    dwu1 = wgrad(u_f1[None], dhu1, "ffn1_dwu")
    dwd1 = wgrad(act1, dy3[None], "ffn1_dwd")
    dz2, dy2, dg10, db10, dgt_a1 = ln_bwd(dx3, x2, y2, gt_a1, vec(ln_g[1, 0]), "fox_ln_bwd")
    do_x, delta_x = linear_nt_delta(dy2, wts["fox_w_o"], o_x2, sel_fox, "fox_out_bwd")
    grads["fox_w_o"] = wgrad(o_x2[None], dy2[None], "fox_dwo")[0]
    delta_xh = jnp.transpose(delta_x[:, :FOX_HEADS].reshape(nb, s, FOX_HEADS), (0, 2, 1))[..., None]
    dq_x, dk_x, dv_x, dfk_x, dfq_x = attn_bwd(q_x, k_x, v_x, _heads(do_x, nb, FOX_HEADS), _row_tiles(lse_x),
                                              _row_tiles(delta_xh), _row_tiles(cum_h), cum_h, scale_x, "fox_attn_bwd")
    dcum_h = dfk_x[..., 0] + dfq_x.reshape(nb, FOX_HEADS, s)
    dcum = jnp.pad(jnp.transpose(dcum_h, (0, 2, 1)).reshape(t, FOX_HEADS), ((0, 0), (0, 128 - FOX_HEADS)))
    dhf, dbf = fox_gate_bwd(dcum, hf, b_f_pad, triu, nb, "fox_gate_bwd")
    dqkv = jnp.concatenate([_unheads(dq_x).astype(BF16), _unheads(dk_x), _unheads(dv_x)], axis=1)
    dx2, dsc_a1, dsh_a1 = linear_nt_mod_bwd([(dqkv, fox_w_qkv), (dhf, fox_w_f)], dz2, x2, sc_a1, "fox_in_bwd")
    dw_qkv = wgrad(u_x[None], dqkv[None], "fox_dwqkv")[0]
    dw_f = wgrad(u_x[None], dhf[None], "fox_dwf")[0]
    grads["fox_w_in"] = jnp.concatenate([dw_qkv, dw_f[:, :FOX_HEADS]], axis=1)
    dz1, dy1, dg01, db01, dgt_f0 = ln_bwd(dx2, x1, y1, gt_f, vec(ln_g[0, 1]), "ffn0_ln_bwd")
    dhg0, dhu0, act0, dx1, dsc_f0, dsh_f0 = ffn_bwd(dy1, hg0, hu0, wts["ffn_w_gate"][:, 0], wts["ffn_w_up"][:, 0],
                                                    wts["ffn_w_down"][:, 0], dz1, x1, sc_f, "ffn0_bwd")
    dwg0 = wgrad(u_f0[None], dhg0, "ffn0_dwg")
    dwu0 = wgrad(u_f0[None], dhu0, "ffn0_dwu")
    dwd0 = wgrad(act0, dy1[None], "ffn0_dwd")
    grads["ffn_w_gate"] = jnp.stack([dwg0, dwg1], axis=1)
    grads["ffn_w_up"] = jnp.stack([dwu0, dwu1], axis=1)
    grads["ffn_w_down"] = jnp.stack([dwd0, dwd1], axis=1)
    dz0, dy0, dg00, db00, dgt_a0 = ln_bwd(dx1, x0, y0, gt_a, vec(ln_g[0, 0]), "mla_ln_bwd")
    do_m, delta_m = linear_nt_delta(dy0, wts["mla_w_o"], o_m2, sel_mla, "mla_out_bwd")
    grads["mla_w_o"] = wgrad(o_m2[None], dy0[None], "mla_dwo")[0]
    delta_mh = jnp.transpose(delta_m[:, :MLA_HEADS].reshape(nb, s, MLA_HEADS), (0, 2, 1))[..., None]
    dq_c, dk_c, dv_c = attn_bwd(q_cat, k_cat, v_h, _heads(do_m, nb, MLA_HEADS), _row_tiles(lse_m), _row_tiles(delta_mh),
                                None, None, scale_m, "mla_attn_bwd")
    dq_t = jnp.transpose(dq_c, (0, 2, 1, 3))
    dq_m = jnp.concatenate([dq_t[..., :MLA_NOPE].reshape(t, hq), dq_t[..., MLA_NOPE:MLA_NOPE + half].reshape(t, hr),
                            dq_t[..., MLA_NOPE + half:].reshape(t, hr)], axis=1).astype(BF16)
    dk_t = jnp.transpose(dk_c, (0, 2, 1, 3))
    dkn_m = dk_t[..., :MLA_NOPE].reshape(t, hq)
    dkr_heads = dk_t[..., MLA_NOPE:].reshape(t, MLA_HEADS * MLA_ROPE)
    dh_in, dq_pre, dgq, dgkv = mla_mid_bwd(
        dq_m, dkn_m, _unheads(dv_c), dkr_heads, h_in, vec(mla_g_q), vec(mla_g_kv), w_uq_p, wts["mla_w_uk"],
        wts["mla_w_uv"], cos8, sin8, cos64, sin64s, swap64, head_sum, "mla_mid_bwd")
    grads["mla_w_uq"] = wgrad(cq_m[None], dq_pre[None], "mla_dwuq")[0][:, inv_perm]
    grads["mla_w_uk"] = wgrad(ckv_m[None], dkn_m[None], "mla_dwuk")[0]
    grads["mla_w_uv"] = wgrad(ckv_m[None], _unheads(dv_c)[None], "mla_dwuv")[0]
    grads["mla_w_in"] = wgrad(u_m[None], dh_in[None], "mla_dwin")[0]
    dx0, dsc_a0, dsh_a0 = linear_nt_mod_bwd([(dh_in, wts["mla_w_in"])], dz0, x0, sc_a, "mla_in_bwd")

    dmods = [(dsh_a0, dsc_a0, dgt_a0, dsh_f0, dsc_f0, dgt_f0), (dsh_a1, dsc_a1, dgt_a1, dsh_f1, dsc_f1, dgt_f1)]
    d_ln_g = jnp.stack([jnp.concatenate([dg00, dg01], axis=0), jnp.concatenate([dg10, dg11], axis=0)])
    d_ln_b = jnp.stack([jnp.concatenate([db00, db01], axis=0), jnp.concatenate([db10, db11], axis=0)])
    return loss_part, dx0.reshape(nb, s, d), grads, dmods, d_ln_g, d_ln_b, dgq, dgkv, dbf[:, :FOX_HEADS]


def _pad_rows(a, rows):
    return jnp.pad(a, ((0, rows - a.shape[0]), (0, 0)))


def kernel(x, c, positions, mla_w_in, mla_g_q, mla_w_uq, mla_g_kv, mla_w_uk, mla_w_uv, mla_w_o, fox_w_in, fox_b_f, fox_w_o, ada_w, ada_b, ffn_w_gate, ffn_w_up, ffn_w_down, ln_g, ln_b, loss_target, m_mla_w_in, m_mla_g_q, m_mla_w_uq, m_mla_g_kv, m_mla_w_uk, m_mla_w_uv, m_mla_w_o, m_fox_w_in, m_fox_b_f, m_fox_w_o, m_ada_w, m_ada_b, m_ffn_w_gate, m_ffn_w_up, m_ffn_w_down, m_ln_g, m_ln_b, v_mla_w_in, v_mla_g_q, v_mla_w_uq, v_mla_g_kv, v_mla_w_uk, v_mla_w_uv, v_mla_w_o, v_fox_w_in, v_fox_b_f, v_fox_w_o, v_ada_w, v_ada_b, v_ffn_w_gate, v_ffn_w_up, v_ffn_w_down, v_ln_g, v_ln_b):
    args = dict(locals())
    nb, s, d = x.shape
    ax, ay, ac = lax.axis_index("x"), lax.axis_index("y"), lax.axis_index("c")
    chip = 2 * ax + ay
    dev = 2 * chip + ac
    n_dev = 2 * N_CHIPS
    n_all = nb * n_dev

    shard_shapes = [args[n].shape[1:] if n.startswith("mla") or n.startswith("fox") else args[n].shape for n in _PACKED]
    w_pack = _pack([args[n].reshape(1, *shp) for n, shp in zip(_PACKED, shard_shapes)], BF16)[0]
    w_all = all_gather_chips(w_pack, "gather_weights")
    parts = dict(zip(_PACKED, _unpack(w_all, shard_shapes)))
    wts = {
        "mla_w_in": parts["mla_w_in"].reshape(-1, parts["mla_w_in"].shape[-1]),
        "mla_w_uq": _cols_to_full(parts["mla_w_uq"]),
        "mla_w_uk": _cols_to_full(parts["mla_w_uk"]),
        "mla_w_uv": _cols_to_full(parts["mla_w_uv"]),
        "mla_w_o": parts["mla_w_o"].reshape(-1, d),
        "fox_w_in": _cols_to_full(parts["fox_w_in"]),
        "fox_w_o": parts["fox_w_o"].reshape(-1, d),
        "ffn_w_gate": parts["ffn_w_gate"], "ffn_w_up": parts["ffn_w_up"], "ffn_w_down": parts["ffn_w_down"],
    }

    c_all = all_gather8(_pad_rows(c, 8), "gather_c").reshape(n_dev, 8, d)[:, :nb].reshape(n_all, d)
    mod_part = ada_mod_part(c_all, ada_w, "ada_mod")
    ncol = mod_part.shape[-1]
    mod_g = all_gather8(mod_part.reshape(DEPTH * n_all, ncol), "gather_mod")
    mod_g = mod_g.reshape(N_CHIPS, 2, DEPTH, n_all, ncol)[:, 0]
    mod_full = jnp.transpose(mod_g, (1, 2, 0, 3)).reshape(DEPTH, n_all, N_CHIPS * ncol) + ada_b[:, None, :]
    mod_loc = lax.dynamic_slice_in_dim(mod_full, dev * nb, nb, axis=1)
    mods = [tuple(mod_loc[i, :, k * d:(k + 1) * d].reshape(nb, 1, d) for k in range(6)) for i in range(DEPTH)]

    ln_cols = ln_g.shape[-1]
    ln_blk = jnp.concatenate([ln_g.reshape(2 * DEPTH, ln_cols), ln_b.reshape(2 * DEPTH, ln_cols)], axis=0)
    ln_all = all_gather8(ln_blk, "gather_ln").reshape(N_CHIPS, 2, 4 * DEPTH, ln_cols)[:, 0]
    ln_all = jnp.transpose(ln_all, (1, 0, 2)).reshape(4 * DEPTH, d)
    ln_g_full = ln_all[:2 * DEPTH].reshape(DEPTH, 2, d)
    ln_b_full = ln_all[2 * DEPTH:].reshape(DEPTH, 2, d)

    loss_part, grad_x, grads, dmods, d_ln_g, d_ln_b, dgq, dgkv, dbf = _local_step(
        x, positions, loss_target, mods, wts, ln_g_full, ln_b_full, mla_g_q[0], mla_g_kv[0], fox_b_f[0])
    loss = lax.psum(loss_part, ("x", "y", "c"))

    dmod_rows = jnp.stack([jnp.concatenate([v_.reshape(nb, d) for v_ in dm], axis=1) for dm in dmods])
    small = jnp.concatenate([
        d_ln_g.reshape(2 * DEPTH, d), d_ln_b.reshape(2 * DEPTH, d),
        jnp.pad(jnp.concatenate([dgq, dgkv, dbf], axis=1), ((0, 0), (0, d - 2 * MLA_QR - FOX_HEADS))),
        dmod_rows.reshape(DEPTH * nb * 6, d)], axis=0)
    n_small = small.shape[0]
    small_rows = -(-n_small // 8) * 8
    small_all = all_gather8(_pad_rows(small, small_rows), "gather_stats").reshape(n_dev, small_rows, d)
    stat_sum = sum_leading(small_all, "sum_stats")
    g_ln_g = lax.dynamic_slice_in_dim(stat_sum[:2 * DEPTH], chip * ln_cols, ln_cols, axis=1).reshape(DEPTH, 2, ln_cols)
    g_ln_b = lax.dynamic_slice_in_dim(stat_sum[2 * DEPTH:4 * DEPTH], chip * ln_cols, ln_cols, axis=1).reshape(DEPTH, 2, ln_cols)
    row = stat_sum[4 * DEPTH]
    g_gq = row[:MLA_QR].reshape(1, MLA_QR)
    g_gkv = row[MLA_QR:2 * MLA_QR].reshape(1, MLA_KVR)
    g_bf = row[2 * MLA_QR:2 * MLA_QR + FOX_HEADS].reshape(1, FOX_HEADS)
    base = 4 * DEPTH + 1
    dmod_all = small_all[:, base:base + DEPTH * nb * 6].reshape(n_dev, DEPTH, nb, 6 * d)
    dmod_all = jnp.transpose(dmod_all, (1, 0, 2, 3)).reshape(DEPTH, n_all, 6 * d)
    g_ada_b = sum_leading(jnp.transpose(dmod_all, (1, 0, 2)), "sum_ada_b")
    dmod_mine = lax.dynamic_slice_in_dim(dmod_all, chip * ncol, ncol, axis=2)
    g_ada_w = ada_grad(c_all.T, dmod_mine, "ada_grad")

    g_blocks = [
        grads["mla_w_in"].reshape(N_CHIPS, -1), _full_to_cols(grads["mla_w_uq"]), _full_to_cols(grads["mla_w_uk"]),
        _full_to_cols(grads["mla_w_uv"]), grads["mla_w_o"].reshape(N_CHIPS, -1), _full_to_cols(grads["fox_w_in"]),
        grads["fox_w_o"].reshape(N_CHIPS, -1), grads["ffn_w_gate"], grads["ffn_w_up"], grads["ffn_w_down"]]
    g_pack = _pack(g_blocks, F32)
    rows = g_pack.shape[1]
    g_pack = g_pack.reshape(N_CHIPS, 2, rows // 2, PACK_COLS)
    recv = sibling_swap_halves(g_pack, "rs_swap")
    chip_sum = add_own_half(g_pack, recv, ac.reshape(1).astype(jnp.int32), "rs_add")
    pieces = chip_exchange(chip_sum, "rs_exchange")
    my_half = sum_pieces(pieces, "rs_sum")
    g_mine = sibling_join_halves(my_half, "rs_join").reshape(rows, PACK_COLS)
    g_big = dict(zip(_PACKED, _unpack(g_mine, shard_shapes)))

    g_out = {
        "mla_w_in": g_big["mla_w_in"][None], "mla_g_q": g_gq, "mla_w_uq": g_big["mla_w_uq"][None], "mla_g_kv": g_gkv,
        "mla_w_uk": g_big["mla_w_uk"][None], "mla_w_uv": g_big["mla_w_uv"][None], "mla_w_o": g_big["mla_w_o"][None],
        "fox_w_in": g_big["fox_w_in"][None], "fox_b_f": g_bf, "fox_w_o": g_big["fox_w_o"][None],
        "ada_w": g_ada_w, "ada_b": g_ada_b, "ffn_w_gate": g_big["ffn_w_gate"], "ffn_w_up": g_big["ffn_w_up"],
        "ffn_w_down": g_big["ffn_w_down"], "ln_g": g_ln_g, "ln_b": g_ln_b}
    names = ["mla_w_in", "mla_g_q", "mla_w_uq", "mla_g_kv", "mla_w_uk", "mla_w_uv", "mla_w_o", "fox_w_in", "fox_b_f",
             "fox_w_o", "ada_w", "ada_b", "ffn_w_gate", "ffn_w_up", "ffn_w_down", "ln_g", "ln_b"]
    small_names = ["mla_g_q", "mla_g_kv", "fox_b_f", "ada_b", "ln_g", "ln_b"]
    deltas, new_m, new_v = {}, {}, {}
    for n in names:
        if n in small_names:
            continue
        shp = args[n].shape
        two_d = (-1, shp[-1])
        dl, mn, vn = adamw(args[n].reshape(two_d), g_out[n].reshape(two_d), args["m_" + n].reshape(two_d),
                           args["v_" + n].reshape(two_d), "adamw_" + n)
        deltas[n], new_m[n], new_v[n] = dl.reshape(shp), mn.reshape(shp), vn.reshape(shp)

    def small_pack(prefix, src):
        flat = jnp.concatenate([src[prefix + n].reshape(-1) for n in small_names])
        size = -(-flat.shape[0] // (8 * 128)) * 8 * 128
        return jnp.pad(flat, (0, size - flat.shape[0])).reshape(-1, 128)

    sd, sm, sv = adamw(small_pack("", args), small_pack("", g_out), small_pack("m_", args), small_pack("v_", args),
                       "adamw_small")
    off = 0
    for n in small_names:
        shp = args[n].shape
        size = math.prod(shp)
        deltas[n] = sd.reshape(-1)[off:off + size].reshape(shp)
        new_m[n] = sm.reshape(-1)[off:off + size].reshape(shp)
        new_v[n] = sv.reshape(-1)[off:off + size].reshape(shp)
        off += size

    outs = [loss, grad_x]
    outs += [g_out[n].reshape(args[n].shape) for n in names]
    outs += [deltas[n] for n in names] + [new_m[n] for n in names] + [new_v[n] for n in names]
    return tuple(outs)
```

```python
import functools
import math

import numpy as np
import jax
import jax.numpy as jnp
from jax import lax
from jax.experimental import pallas as pl
from jax.experimental.pallas import tpu as pltpu

F32 = jnp.float32
BF16 = jnp.bfloat16
MESH = pl.DeviceIdType.MESH

D_MODEL = 1024
DEPTH = 2
MLA_HEADS = 8
MLA_NOPE = 128
MLA_ROPE = 64
MLA_V = 128
MLA_QR = 256
MLA_KVR = 256
ROPE_THETA = 10000.0
FOX_HEADS = 16
FOX_HD = 64
D_FF = 2816
N_CHIPS = 4
FF_CHUNK = D_FF // N_CHIPS
ALPHA = (2.0 * DEPTH) ** 0.25
EPS = 1e-5
ADAM_LR = 0.001
ADAM_B1 = 0.9
ADAM_B2 = 0.999
ADAM_EPS = 1e-08
ADAM_WD = 0.01
ADAM_STEP = 10

VMEM_LIMIT = 56 * 1024 * 1024
TOKEN_TILE = 256
ATTN_TILE = 512
COMM_BLOCK_BYTES = 2 * 1024 * 1024
ADAMW_BLOCK_BYTES = 1024 * 1024


def _cp(n_axes):
    return pltpu.CompilerParams(dimension_semantics=("arbitrary",) * n_axes, vmem_limit_bytes=VMEM_LIMIT)


def _dot(a, b):
    return jnp.dot(a, b, preferred_element_type=F32)


def _dot_nt(a, b):
    return lax.dot_general(a, b, (((1,), (1,)), ((), ())), preferred_element_type=F32)


def _dot_tn(a, b):
    return lax.dot_general(a, b, (((0,), (0,)), ((), ())), preferred_element_type=F32)


def _dot_f32(a, b):
    return jnp.dot(a, b, preferred_element_type=F32, precision=lax.Precision.HIGHEST)


def _sds(shape, dtype):
    return jax.ShapeDtypeStruct(shape, dtype)


def mod_linear(x, shift, scale, w, out_dtype, name, tn=None, emit_u=False):
    t, d = x.shape
    n = w.shape[1]
    tn = n if tn is None else tn
    tm = TOKEN_TILE
    tps = (t // shift.shape[0]) // tm

    def body(x_ref, sh_ref, sc_ref, w_ref, o_ref, *rest):
        u = (x_ref[...] * (1.0 + sc_ref[...]) + sh_ref[...]).astype(BF16)
        o_ref[...] = _dot(u, w_ref[...]).astype(out_dtype)
        if emit_u:
            @pl.when(pl.program_id(1) == 0)
            def _():
                rest[0][...] = u

    vec = pl.BlockSpec((None, 1, d), lambda i, j: (i // tps, 0, 0))
    out_shape = [_sds((t, n), out_dtype)]
    out_specs = [pl.BlockSpec((tm, tn), lambda i, j: (i, j))]
    if emit_u:
        out_shape.append(_sds((t, d), BF16))
        out_specs.append(pl.BlockSpec((tm, d), lambda i, j: (i, 0)))
    res = pl.pallas_call(
        body, name=name, grid=(t // tm, n // tn),
        in_specs=[pl.BlockSpec((tm, d), lambda i, j: (i, 0)), vec, vec,
                  pl.BlockSpec((d, tn), lambda i, j: (0, j))],
        out_specs=out_specs, out_shape=out_shape, compiler_params=_cp(2),
    )(x, shift, scale, w)
    return res if emit_u else res[0]


def _rms(h, g):
    rstd = lax.rsqrt(jnp.mean(h * h, axis=-1, keepdims=True) + EPS)
    return h * rstd, rstd


def mla_mid_fwd(h, g_q, g_kv, w_uq, w_uk, w_uv, cos8, sin8, cos64, sin64s, swap64, name):
    t = h.shape[0]
    tm = TOKEN_TILE
    hq = MLA_HEADS * MLA_NOPE
    hr = MLA_HEADS * MLA_ROPE // 2

    def body(h_ref, gq_ref, gkv_ref, wuq_ref, wuk_ref, wuv_ref, c8_ref, s8_ref, c64_ref, s64_ref, sw_ref,
             q_ref, kn_ref, v_ref, kr_ref, cq_ref, ckv_ref):
        hh = h_ref[...]
        cq = (_rms(hh[:, :MLA_QR], None)[0] * gq_ref[...]).astype(BF16)
        ckv = (_rms(hh[:, MLA_QR:MLA_QR + MLA_KVR], None)[0] * gkv_ref[...]).astype(BF16)
        cq_ref[...] = cq
        ckv_ref[...] = ckv
        q = _dot(cq, wuq_ref[...])
        x1 = q[:, hq:hq + hr]
        x2 = q[:, hq + hr:]
        cs = c8_ref[...]
        sn = s8_ref[...]
        q_ref[...] = jnp.concatenate([q[:, :hq], x1 * cs - x2 * sn, x2 * cs + x1 * sn], axis=1).astype(BF16)
        kn_ref[...] = _dot(ckv, wuk_ref[...]).astype(BF16)
        v_ref[...] = _dot(ckv, wuv_ref[...]).astype(BF16)
        kr = hh[:, MLA_QR + MLA_KVR:]
        kr_ref[...] = (kr * c64_ref[...] + _dot_f32(kr, sw_ref[...]) * s64_ref[...]).astype(BF16)

    def rows(n):
        return pl.BlockSpec((tm, n), lambda i: (i, 0))

    def whole(a):
        return pl.BlockSpec(a.shape, lambda i: (0,) * a.ndim)

    nq = w_uq.shape[1]
    return pl.pallas_call(
        body, name=name, grid=(t // tm,),
        in_specs=[rows(h.shape[1]), whole(g_q), whole(g_kv), whole(w_uq), whole(w_uk), whole(w_uv),
                  rows(hr), rows(hr), rows(MLA_ROPE), rows(MLA_ROPE), whole(swap64)],
        out_specs=[rows(nq), rows(hq), rows(hq), rows(MLA_ROPE), rows(MLA_QR), rows(MLA_KVR)],
        out_shape=[_sds((t, nq), BF16), _sds((t, hq), BF16), _sds((t, hq), BF16), _sds((t, MLA_ROPE), BF16),
                   _sds((t, MLA_QR), BF16), _sds((t, MLA_KVR), BF16)],
        compiler_params=_cp(1),
    )(h, g_q, g_kv, w_uq, w_uk, w_uv, cos8, sin8, cos64, sin64s, swap64)


def attn_fwd(q, k, v, fq, fk, scale, name):
    b, h, s, dk = q.shape
    dv = v.shape[-1]
    tq = ATTN_TILE
    nq = s // tq
    has_bias = fq is not None

    def body(*refs):
        if has_bias:
            q_ref, k_ref, v_ref, fq_ref, fk_ref, o_ref, lse_ref = refs
        else:
            q_ref, k_ref, v_ref, o_ref, lse_ref = refs
        i = pl.program_id(2)
        qb = q_ref[...]

        def block(j, carry, masked):
            m, l, acc = carry
            start = pl.multiple_of(j * tq, tq)
            kb = k_ref[pl.ds(start, tq), :]
            vb = v_ref[pl.ds(start, tq), :]
            sc = _dot_nt(qb, kb) * scale
            if has_bias:
                sc = sc + fq_ref[...] - fk_ref[j]
            if masked:
                keep = lax.broadcasted_iota(jnp.int32, (tq, tq), 0) >= lax.broadcasted_iota(jnp.int32, (tq, tq), 1)
                sc = jnp.where(keep, sc, -1e30)
            m_new = jnp.maximum(m, jnp.max(sc, axis=1, keepdims=True))
            a = jnp.exp(m - m_new)
            p = jnp.exp(sc - m_new)
            l = a * l + jnp.sum(p, axis=1, keepdims=True)
            acc = a * acc + _dot(p.astype(BF16), vb)
            return m_new, l, acc

        init = (jnp.full((tq, 1), -1e30, F32), jnp.zeros((tq, 1), F32), jnp.zeros((tq, dv), F32))
        carry = lax.fori_loop(0, i, lambda j, c: block(j, c, False), init)
        m, l, acc = block(i, carry, True)
        o_ref[...] = (acc / l).astype(BF16)
        lse_ref[...] = m + jnp.log(l)

    def qspec(n):
        return pl.BlockSpec((None, None, tq, n), lambda bb, hh, i: (bb, hh, i, 0))

    def full(n):
        return pl.BlockSpec((None, None, s, n), lambda bb, hh, i: (bb, hh, 0, 0))

    in_specs = [qspec(dk), full(dk), full(dv)]
    args = [q, k, v]
    if has_bias:
        in_specs += [qspec(1), pl.BlockSpec((None, None, nq, 1, tq), lambda bb, hh, i: (bb, hh, 0, 0, 0))]
        args += [fq, fk]
    return pl.pallas_call(
        body, name=name, grid=(b, h, nq), in_specs=in_specs,
        out_specs=[qspec(dv), qspec(1)],
        out_shape=[_sds((b, h, s, dv), BF16), _sds((b, h, s, 1), F32)],
        compiler_params=_cp(3),
    )(*args)


def _layer_norm(z, g, b):
    mu = jnp.mean(z, axis=-1, keepdims=True)
    zc = z - mu
    rstd = lax.rsqrt(jnp.mean(zc * zc, axis=-1, keepdims=True) + EPS)
    xhat = zc * rstd
    return xhat * g + b, xhat, rstd


def linear_resid_ln(a, w, x_in, gate, ln_g, ln_b, name):
    t, kdim = a.shape
    d = w.shape[1]
    tm = TOKEN_TILE
    tps = (t // gate.shape[0]) // tm

    def body(a_ref, w_ref, x_ref, gt_ref, g_ref, b_ref, y_ref, xo_ref):
        y = _dot(a_ref[...], w_ref[...])
        y_ref[...] = y
        z = ALPHA * x_ref[...] + (1.0 + gt_ref[...]) * y
        xo_ref[...] = _layer_norm(z, g_ref[...], b_ref[...])[0]

    rows = pl.BlockSpec((tm, d), lambda i: (i, 0))
    vec = pl.BlockSpec((1, d), lambda i: (0, 0))
    return pl.pallas_call(
        body, name=name, grid=(t // tm,),
        in_specs=[pl.BlockSpec((tm, kdim), lambda i: (i, 0)), pl.BlockSpec((kdim, d), lambda i: (0, 0)), rows,
                  pl.BlockSpec((None, 1, d), lambda i: (i // tps, 0, 0)), vec, vec],
        out_specs=[rows, rows], out_shape=[_sds((t, d), F32), _sds((t, d), F32)],
        compiler_params=_cp(1),
    )(a, w, x_in, gate, ln_g, ln_b)


def ffn_fwd(x_in, shift, scale, gate, wg, wu, wd, layer, ln_g, ln_b, name):
    t, d = x_in.shape
    c, _, _, fc = wg.shape
    tm = TOKEN_TILE
    tps = (t // gate.shape[0]) // tm

    def body(x_ref, sh_ref, sc_ref, gt_ref, wg_ref, wu_ref, wd_ref, g_ref, b_ref,
             u_ref, hg_ref, hu_ref, y_ref, xo_ref, acc_ref):
        cc = pl.program_id(1)

        @pl.when(cc == 0)
        def _():
            u_ref[...] = (x_ref[...] * (1.0 + sc_ref[...]) + sh_ref[...]).astype(BF16)
            acc_ref[...] = jnp.zeros_like(acc_ref)

        u = u_ref[...]
        hg = _dot(u, wg_ref[...])
        hu = _dot(u, wu_ref[...])
        hg_ref[...] = hg.astype(BF16)
        hu_ref[...] = hu.astype(BF16)
        act = (hg * jax.nn.sigmoid(hg) * hu).astype(BF16)
        acc_ref[...] += _dot(act, wd_ref[...])

        @pl.when(cc == c - 1)
        def _():
            y = acc_ref[...]
            y_ref[...] = y
            z = ALPHA * x_ref[...] + (1.0 + gt_ref[...]) * y
            xo_ref[...] = _layer_norm(z, g_ref[...], b_ref[...])[0]

    rows = pl.BlockSpec((tm, d), lambda i, cc: (i, 0))
    bvec = pl.BlockSpec((None, 1, d), lambda i, cc: (i // tps, 0, 0))
    vec = pl.BlockSpec((1, d), lambda i, cc: (0, 0))
    hspec = pl.BlockSpec((None, tm, fc), lambda i, cc: (cc, i, 0))
    return pl.pallas_call(
        body, name=name, grid=(t // tm, c),
        in_specs=[rows, bvec, bvec, bvec,
                  pl.BlockSpec((None, None, d, fc), lambda i, cc: (cc, layer, 0, 0)),
                  pl.BlockSpec((None, None, d, fc), lambda i, cc: (cc, layer, 0, 0)),
                  pl.BlockSpec((None, None, fc, d), lambda i, cc: (cc, layer, 0, 0)), vec, vec],
        out_specs=[rows, hspec, hspec, rows, rows],
        out_shape=[_sds((t, d), BF16), _sds((c, t, fc), BF16), _sds((c, t, fc), BF16),
                   _sds((t, d), F32), _sds((t, d), F32)],
        scratch_shapes=[pltpu.VMEM((tm, d), F32)],
        compiler_params=_cp(2),
    )(x_in, shift, scale, gate, wg, wu, wd, ln_g, ln_b)


def fox_gate_fwd(hf, b_f, tri, n_batch, name):
    t, n = hf.shape
    blk = tri.shape[0]
    nb = (t // n_batch) // blk

    def body(hf_ref, b_ref, tri_ref, o_ref, carry_ref):
        @pl.when(pl.program_id(1) == 0)
        def _():
            carry_ref[...] = jnp.zeros_like(carry_ref)

        xx = hf_ref[...] + b_ref[...]
        lf = jnp.minimum(xx, 0.0) - jnp.log(1.0 + jnp.exp(-jnp.abs(xx)))
        cum = _dot_f32(tri_ref[...], lf) + carry_ref[...]
        o_ref[...] = cum
        carry_ref[...] = cum[blk - 1:blk, :]

    return pl.pallas_call(
        body, name=name, grid=(n_batch, nb),
        in_specs=[pl.BlockSpec((blk, n), lambda bb, i: (bb * nb + i, 0)), pl.BlockSpec((1, n), lambda bb, i: (0, 0)),
                  pl.BlockSpec((blk, blk), lambda bb, i: (0, 0))],
        out_specs=pl.BlockSpec((blk, n), lambda bb, i: (bb * nb + i, 0)),
        out_shape=_sds((t, n), F32), scratch_shapes=[pltpu.VMEM((1, n), F32)],
        compiler_params=_cp(2),
    )(hf, b_f, tri)


def loss_grad(x_out, target, name):
    t, d = x_out.shape
    tm = TOKEN_TILE

    def body(x_ref, t_ref, g_ref, l_ref):
        @pl.when(pl.program_id(0) == 0)
        def _():
            l_ref[...] = jnp.zeros_like(l_ref)

        err = x_ref[...] - t_ref[...]
        g_ref[...] = err / d
        l_ref[...] += jnp.sum(err * err, axis=0, keepdims=True)

    rows = pl.BlockSpec((tm, d), lambda i: (i, 0))
    return pl.pallas_call(
        body, name=name, grid=(t // tm,), in_specs=[rows, rows],
        out_specs=[rows, pl.BlockSpec((1, d), lambda i: (0, 0))],
        out_shape=[_sds((t, d), F32), _sds((1, d), F32)], compiler_params=_cp(1),
    )(x_out, target)


def ln_bwd(dxo, x_in, y, gate, ln_g, name):
    t, d = dxo.shape
    nb = gate.shape[0]
    tm = TOKEN_TILE
    tps = (t // nb) // tm

    def body(dxo_ref, x_ref, y_ref, gt_ref, g_ref, dz_ref, dy_ref, dg_ref, db_ref, dgt_ref):
        i = pl.program_id(0)

        @pl.when(i == 0)
        def _():
            dg_ref[...] = jnp.zeros_like(dg_ref)
            db_ref[...] = jnp.zeros_like(db_ref)

        @pl.when(i % tps == 0)
        def _():
            dgt_ref[...] = jnp.zeros_like(dgt_ref)

        yy = y_ref[...]
        g1 = 1.0 + gt_ref[...]
        z = ALPHA * x_ref[...] + g1 * yy
        _, xhat, rstd = _layer_norm(z, 1.0, 0.0)
        dxo_v = dxo_ref[...]
        dg_ref[...] += jnp.sum(dxo_v * xhat, axis=0, keepdims=True)
        db_ref[...] += jnp.sum(dxo_v, axis=0, keepdims=True)
        dxh = dxo_v * g_ref[...]
        dz = rstd * (dxh - jnp.mean(dxh, axis=-1, keepdims=True) - xhat * jnp.mean(dxh * xhat, axis=-1, keepdims=True))
        dz_ref[...] = dz
        dy_ref[...] = (g1 * dz).astype(BF16)
        dgt_ref[...] += jnp.sum(dz * yy, axis=0, keepdims=True)

    rows = pl.BlockSpec((tm, d), lambda i: (i, 0))
    vec = pl.BlockSpec((1, d), lambda i: (0, 0))
    bvec = pl.BlockSpec((None, 1, d), lambda i: (i // tps, 0, 0))
    return pl.pallas_call(
        body, name=name, grid=(t // tm,), in_specs=[rows, rows, rows, bvec, vec],
        out_specs=[rows, rows, vec, vec, bvec],
        out_shape=[_sds((t, d), F32), _sds((t, d), BF16), _sds((1, d), F32), _sds((1, d), F32), _sds((nb, 1, d), F32)],
        compiler_params=_cp(1),
    )(dxo, x_in, y, gate, ln_g)


def _mod_bwd_tail(du, dz_ref, x_ref, sc_ref, dx_ref, dsc_ref, dsh_ref, first):
    @pl.when(first)
    def _():
        dsc_ref[...] = jnp.zeros_like(dsc_ref)
        dsh_ref[...] = jnp.zeros_like(dsh_ref)

    dx_ref[...] = ALPHA * dz_ref[...] + du * (1.0 + sc_ref[...])
    dsc_ref[...] += jnp.sum(du * x_ref[...], axis=0, keepdims=True)
    dsh_ref[...] += jnp.sum(du, axis=0, keepdims=True)


def ffn_bwd(dy, hg, hu, wg, wu, wd, layer, dz, x_in, scale, name):
    t, d = dy.shape
    c, _, _, fc = wg.shape
    nb = scale.shape[0]
    tm = TOKEN_TILE
    tps = (t // nb) // tm

    def body(dy_ref, hg_ref, hu_ref, wg_ref, wu_ref, wd_ref, dz_ref, x_ref, sc_ref,
             dhg_ref, dhu_ref, act_ref, dx_ref, dsc_ref, dsh_ref, acc_ref):
        i = pl.program_id(0)
        cc = pl.program_id(1)

        @pl.when(cc == 0)
        def _():
            acc_ref[...] = jnp.zeros_like(acc_ref)

        hgv = hg_ref[...].astype(F32)
        huv = hu_ref[...].astype(F32)
        da = _dot_nt(dy_ref[...], wd_ref[...])
        sg = jax.nn.sigmoid(hgv)
        sl = hgv * sg
        act_ref[...] = (sl * huv).astype(BF16)
        dhu = (da * sl).astype(BF16)
        dhg = (da * huv * (sg * (1.0 + hgv * (1.0 - sg)))).astype(BF16)
        dhu_ref[...] = dhu
        dhg_ref[...] = dhg
        acc_ref[...] += _dot_nt(dhg, wg_ref[...]) + _dot_nt(dhu, wu_ref[...])

        @pl.when(cc == c - 1)
        def _():
            _mod_bwd_tail(acc_ref[...], dz_ref, x_ref, sc_ref, dx_ref, dsc_ref, dsh_ref, i % tps == 0)

    rows = pl.BlockSpec((tm, d), lambda i, cc: (i, 0))
    bvec = pl.BlockSpec((None, 1, d), lambda i, cc: (i // tps, 0, 0))
    hspec = pl.BlockSpec((None, tm, fc), lambda i, cc: (cc, i, 0))
    wcol = pl.BlockSpec((None, None, d, fc), lambda i, cc: (cc, layer, 0, 0))
    return pl.pallas_call(
        body, name=name, grid=(t // tm, c),
        in_specs=[rows, hspec, hspec, wcol, wcol, pl.BlockSpec((None, None, fc, d), lambda i, cc: (cc, layer, 0, 0)),
                  rows, rows, bvec],
        out_specs=[hspec, hspec, hspec, rows, bvec, bvec],
        out_shape=[_sds((c, t, fc), BF16), _sds((c, t, fc), BF16), _sds((c, t, fc), BF16), _sds((t, d), F32),
                   _sds((nb, 1, d), F32), _sds((nb, 1, d), F32)],
        scratch_shapes=[pltpu.VMEM((tm, d), F32)],
        compiler_params=_cp(2),
    )(dy, hg, hu, wg, wu, wd, dz, x_in, scale)


def linear_nt_mod_bwd(pairs, dz, x_in, scale, name):
    t, d = dz.shape
    nb = scale.shape[0]
    tm = TOKEN_TILE
    tps = (t // nb) // tm
    npairs = len(pairs)

    def body(*refs):
        dh_refs = refs[:npairs]
        w_refs = refs[npairs:2 * npairs]
        dz_ref, x_ref, sc_ref, dx_ref, dsc_ref, dsh_ref = refs[2 * npairs:]
        du = _dot_nt(dh_refs[0][...], w_refs[0][...])
        for kk in range(1, npairs):
            du = du + _dot_nt(dh_refs[kk][...], w_refs[kk][...])
        _mod_bwd_tail(du, dz_ref, x_ref, sc_ref, dx_ref, dsc_ref, dsh_ref, pl.program_id(0) % tps == 0)

    rows = pl.BlockSpec((tm, d), lambda i: (i, 0))
    bvec = pl.BlockSpec((None, 1, d), lambda i: (i // tps, 0, 0))
    in_specs = [pl.BlockSpec((tm, dh.shape[1]), lambda i: (i, 0)) for dh, _ in pairs]
    in_specs += [pl.BlockSpec(w.shape, lambda i: (0, 0)) for _, w in pairs]
    in_specs += [rows, rows, bvec]
    return pl.pallas_call(
        body, name=name, grid=(t // tm,), in_specs=in_specs,
        out_specs=[rows, bvec, bvec],
        out_shape=[_sds((t, d), F32), _sds((nb, 1, d), F32), _sds((nb, 1, d), F32)],
        compiler_params=_cp(1),
    )(*[dh for dh, _ in pairs], *[w for _, w in pairs], dz, x_in, scale)


def linear_nt_delta(dy, w_o, o, head_sel, name):
    t, d = dy.shape
    hdv = w_o.shape[0]
    tm = TOKEN_TILE

    def body(dy_ref, w_ref, o_ref, sel_ref, do_ref, dl_ref):
        do = _dot_nt(dy_ref[...], w_ref[...])
        do_ref[...] = do.astype(BF16)
        dl_ref[...] = _dot_f32(do * o_ref[...].astype(F32), sel_ref[...])

    return pl.pallas_call(
        body, name=name, grid=(t // tm,),
        in_specs=[pl.BlockSpec((tm, d), lambda i: (i, 0)), pl.BlockSpec((hdv, d), lambda i: (0, 0)),
                  pl.BlockSpec((tm, hdv), lambda i: (i, 0)), pl.BlockSpec(head_sel.shape, lambda i: (0, 0))],
        out_specs=[pl.BlockSpec((tm, hdv), lambda i: (i, 0)), pl.BlockSpec((tm, 128), lambda i: (i, 0))],
        out_shape=[_sds((t, hdv), BF16), _sds((t, 128), F32)], compiler_params=_cp(1),
    )(dy, w_o, o, head_sel)


def attn_bwd(q, k, v, do, lse_r, delta_r, fq_r, fk_c, scale, name):
    b, h, s, dk = q.shape
    dv = v.shape[-1]
    tk = ATTN_TILE
    nk = s // tk
    has_bias = fq_r is not None

    def body(*refs):
        if has_bias:
            (q_ref, k_ref, v_ref, do_ref, lse_ref, dl_ref, fq_ref, fk_ref,
             dq_ref, dk_ref, dv_ref, dfk_ref, dfq_ref) = refs
        else:
            q_ref, k_ref, v_ref, do_ref, lse_ref, dl_ref, dq_ref, dk_ref, dv_ref = refs
        j = pl.program_id(2)
        kb = k_ref[...]
        vb = v_ref[...]

        @pl.when(j == 0)
        def _():
            dq_ref[...] = jnp.zeros_like(dq_ref)
            if has_bias:
                dfq_ref[...] = jnp.zeros_like(dfq_ref)

        def block(i, carry, masked):
            dk_acc, dv_acc, dfk_acc = carry
            start = pl.multiple_of(i * tk, tk)
            qb = q_ref[pl.ds(start, tk), :]
            dob = do_ref[pl.ds(start, tk), :]
            st = _dot_nt(kb, qb) * scale
            if has_bias:
                st = st + fq_ref[i] - fk_ref[...]
            if masked:
                keep = lax.broadcasted_iota(jnp.int32, (tk, tk), 1) >= lax.broadcasted_iota(jnp.int32, (tk, tk), 0)
                st = jnp.where(keep, st, -1e30)
            pt = jnp.exp(st - lse_ref[i])
            dv_acc = dv_acc + _dot(pt.astype(BF16), dob)
            dpt = _dot_nt(vb, dob)
            dst = pt * (dpt - dl_ref[i])
            if has_bias:
                dfk_acc = dfk_acc - jnp.sum(dst, axis=1, keepdims=True)
                dfq_ref[i] += jnp.sum(dst, axis=0, keepdims=True)
            dsb = (dst * scale).astype(BF16)
            dk_acc = dk_acc + _dot(dsb, qb)
            dq_ref[pl.ds(start, tk), :] += _dot_tn(dsb, kb)
            return dk_acc, dv_acc, dfk_acc

        init = (jnp.zeros((tk, dk), F32), jnp.zeros((tk, dv), F32), jnp.zeros((tk, 1), F32))
        carry = block(j, init, True)
        dk_acc, dv_acc, dfk_acc = lax.fori_loop(j + 1, nk, lambda i, c: block(i, c, False), carry)
        dk_ref[...] = dk_acc.astype(BF16)
        dv_ref[...] = dv_acc.astype(BF16)
        if has_bias:
            dfk_ref[...] = dfk_acc

    def full(n):
        return pl.BlockSpec((None, None, s, n), lambda bb, hh, j: (bb, hh, 0, 0))

    def kspec(n):
        return pl.BlockSpec((None, None, tk, n), lambda bb, hh, j: (bb, hh, j, 0))

    rowv = pl.BlockSpec((None, None, nk, 1, tk), lambda bb, hh, j: (bb, hh, 0, 0, 0))
    in_specs = [full(dk), kspec(dk), kspec(dv), full(dv), rowv, rowv]
    args = [q, k, v, do, lse_r, delta_r]
    out_specs = [full(dk), kspec(dk), kspec(dv)]
    out_shape = [_sds((b, h, s, dk), F32), _sds((b, h, s, dk), BF16), _sds((b, h, s, dv), BF16)]
    if has_bias:
        in_specs += [rowv, kspec(1)]
        args += [fq_r, fk_c]
        out_specs += [kspec(1), rowv]
        out_shape += [_sds((b, h, s, 1), F32), _sds((b, h, nk, 1, tk), F32)]
    return pl.pallas_call(
        body, name=name, grid=(b, h, nk), in_specs=in_specs, out_specs=out_specs, out_shape=out_shape,
        compiler_params=_cp(3),
    )(*args)


def mla_mid_bwd(dq, dkn, dv, dkr_heads, h, g_q, g_kv, w_uq, w_uk, w_uv, cos8, sin8, cos64, sin64s, swap64, head_sum, name):
    t = h.shape[0]
    tm = TOKEN_TILE
    hq = MLA_HEADS * MLA_NOPE
    hr = MLA_HEADS * MLA_ROPE // 2
    nq = w_uq.shape[1]

    def body(dq_ref, dkn_ref, dv_ref, dkr_ref, h_ref, gq_ref, gkv_ref, wuq_ref, wuk_ref, wuv_ref,
             c8_ref, s8_ref, c64_ref, s64_ref, sw_ref, hs_ref, dh_ref, dqp_ref, dgq_ref, dgkv_ref):
        @pl.when(pl.program_id(0) == 0)
        def _():
            dgq_ref[...] = jnp.zeros_like(dgq_ref)
            dgkv_ref[...] = jnp.zeros_like(dgkv_ref)

        dqv = dq_ref[...].astype(F32)
        o1 = dqv[:, hq:hq + hr]
        o2 = dqv[:, hq + hr:]
        cs = c8_ref[...]
        sn = s8_ref[...]
        dqp = jnp.concatenate([dqv[:, :hq], o1 * cs + o2 * sn, o2 * cs - o1 * sn], axis=1).astype(BF16)
        dqp_ref[...] = dqp
        dcq = _dot_nt(dqp, wuq_ref[...])
        dckv = _dot_nt(dkn_ref[...], wuk_ref[...]) + _dot_nt(dv_ref[...], wuv_ref[...])
        hh = h_ref[...]

        def rms_bwd(hpart, g, dc, dg_ref):
            hhat, rstd = _rms(hpart, None)
            dg_ref[...] += jnp.sum(dc * hhat, axis=0, keepdims=True)
            dcg = dc * g
            return rstd * (dcg - hhat * jnp.mean(dcg * hhat, axis=-1, keepdims=True))

        dhq = rms_bwd(hh[:, :MLA_QR], gq_ref[...], dcq, dgq_ref)
        dhkv = rms_bwd(hh[:, MLA_QR:MLA_QR + MLA_KVR], gkv_ref[...], dckv, dgkv_ref)
        dkr = _dot(dkr_ref[...], hs_ref[...])
        dkr_pre = dkr * c64_ref[...] + _dot_f32(dkr * s64_ref[...], sw_ref[...])
        dh_ref[...] = jnp.concatenate([dhq, dhkv, dkr_pre], axis=1).astype(BF16)

    def rows(n):
        return pl.BlockSpec((tm, n), lambda i: (i, 0))

    def whole(a):
        return pl.BlockSpec(a.shape, lambda i: (0,) * a.ndim)

    return pl.pallas_call(
        body, name=name, grid=(t // tm,),
        in_specs=[rows(nq), rows(hq), rows(hq), rows(MLA_HEADS * MLA_ROPE), rows(h.shape[1]), whole(g_q), whole(g_kv),
                  whole(w_uq), whole(w_uk), whole(w_uv), rows(hr), rows(hr), rows(MLA_ROPE), rows(MLA_ROPE),
                  whole(swap64), whole(head_sum)],
        out_specs=[rows(h.shape[1]), rows(nq), pl.BlockSpec((1, MLA_QR), lambda i: (0, 0)),
                   pl.BlockSpec((1, MLA_KVR), lambda i: (0, 0))],
        out_shape=[_sds((t, h.shape[1]), BF16), _sds((t, nq), BF16), _sds((1, MLA_QR), F32), _sds((1, MLA_KVR), F32)],
        compiler_params=_cp(1),
    )(dq, dkn, dv, dkr_heads, h, g_q, g_kv, w_uq, w_uk, w_uv, cos8, sin8, cos64, sin64s, swap64, head_sum)


def fox_gate_bwd(dcum, hf, b_f, triu, n_batch, name):
    t, n = hf.shape
    blk = triu.shape[0]
    nb = (t // n_batch) // blk

    def body(dc_ref, hf_ref, b_ref, tri_ref, o_ref, db_ref, carry_ref):
        @pl.when(pl.program_id(1) == 0)
        def _():
            carry_ref[...] = jnp.zeros_like(carry_ref)

        @pl.when((pl.program_id(0) == 0) & (pl.program_id(1) == 0))
        def _():
            db_ref[...] = jnp.zeros_like(db_ref)

        rc = _dot_f32(tri_ref[...], dc_ref[...]) + carry_ref[...]
        carry_ref[...] = rc[0:1, :]
        dhf = rc * jax.nn.sigmoid(-(hf_ref[...] + b_ref[...]))
        o_ref[...] = dhf.astype(BF16)
        db_ref[...] += jnp.sum(dhf, axis=0, keepdims=True)

    rev = pl.BlockSpec((blk, n), lambda bb, i: (bb * nb + nb - 1 - i, 0))
    return pl.pallas_call(
        body, name=name, grid=(n_batch, nb),
        in_specs=[rev, rev, pl.BlockSpec((1, n), lambda bb, i: (0, 0)), pl.BlockSpec((blk, blk), lambda bb, i: (0, 0))],
        out_specs=[rev, pl.BlockSpec((1, n), lambda bb, i: (0, 0))],
        out_shape=[_sds((t, n), BF16), _sds((1, n), F32)], scratch_shapes=[pltpu.VMEM((1, n), F32)],
        compiler_params=_cp(2),
    )(dcum, hf, b_f, triu)


def wgrad(a, bm, name, slot=None, bt=512):
    ca, t, kd = a.shape
    cb, _, nd = bm.shape
    c = max(ca, cb)
    bn = nd
    if nd > 1024 and nd % 1024 == 0:
        bn = 1024

    def body(*refs):
        a_ref, b_ref, o_ref = refs[0], refs[1], refs[-1]

        @pl.when(pl.program_id(2) == 0)
        def _():
            o_ref[...] = jnp.zeros_like(o_ref)

        o_ref[...] += _dot_tn(a_ref[...], b_ref[...])

    in_specs = [pl.BlockSpec((None, bt, kd), lambda cc, n, tt: (cc if ca > 1 else 0, tt, 0)),
                pl.BlockSpec((None, bt, bn), lambda cc, n, tt: (cc if cb > 1 else 0, tt, n))]
    args = [a, bm]
    aliases = {}
    if slot is None:
        out_spec = pl.BlockSpec((None, kd, bn), lambda cc, n, tt: (cc, 0, n))
        out_shape = _sds((c, kd, nd), F32)
    else:
        layer, n_layers, buf = slot
        out_spec = pl.BlockSpec((None, None, kd, bn), lambda cc, n, tt: (cc, layer, 0, n))
        out_shape = _sds((c, n_layers, kd, nd), F32)
        if buf is not None:
            in_specs.append(pl.BlockSpec(memory_space=pl.ANY))
            args.append(buf)
            aliases = {2: 0}
    return pl.pallas_call(
        body, name=name, grid=(c, nd // bn, t // bt), in_specs=in_specs, out_specs=out_spec, out_shape=out_shape,
        input_output_aliases=aliases, compiler_params=_cp(3),
    )(*args)


def ada_mod_part(c_all, ada_w, name):
    nl, d, n = ada_w.shape
    rows = c_all.shape[0]
    tn = 512

    def body(c_ref, w_ref, o_ref):
        cv = c_ref[...]
        act = (cv * jax.nn.sigmoid(cv)).astype(BF16)
        o_ref[...] = _dot(act, w_ref[...].astype(BF16))

    return pl.pallas_call(
        body, name=name, grid=(nl, n // tn),
        in_specs=[pl.BlockSpec((rows, d), lambda l, j: (0, 0)), pl.BlockSpec((None, d, tn), lambda l, j: (l, 0, j))],
        out_specs=pl.BlockSpec((None, rows, tn), lambda l, j: (l, 0, j)),
        out_shape=_sds((nl, rows, n), F32), compiler_params=_cp(2),
    )(c_all, ada_w)


def ada_grad(c_all_t, dmod, name):
    nl, rows, n = dmod.shape
    d = c_all_t.shape[0]
    tn = 512

    def body(c_ref, dm_ref, o_ref):
        cv = c_ref[...]
        act = (cv * jax.nn.sigmoid(cv)).astype(BF16)
        o_ref[...] = _dot(act, dm_ref[...].astype(BF16))

    return pl.pallas_call(
        body, name=name, grid=(nl, n // tn),
        in_specs=[pl.BlockSpec((d, rows), lambda l, j: (0, 0)), pl.BlockSpec((None, rows, tn), lambda l, j: (l, 0, j))],
        out_specs=pl.BlockSpec((None, d, tn), lambda l, j: (l, 0, j)),
        out_shape=_sds((nl, d, n), F32), compiler_params=_cp(2),
    )(c_all_t, dmod)


def sum_leading(a, name):
    g, r, n = a.shape

    def body(a_ref, o_ref):
        acc = a_ref[0]
        for kk in range(1, g):
            acc = acc + a_ref[kk]
        o_ref[...] = acc

    return pl.pallas_call(
        body, name=name, grid=(1,), in_specs=[pl.BlockSpec((g, r, n), lambda i: (0, 0, 0))],
        out_specs=pl.BlockSpec((r, n), lambda i: (0, 0)), out_shape=_sds((r, n), F32), compiler_params=_cp(1),
    )(a)


def adamw(w, g, m, v, name):
    r, n = w.shape
    br = r
    for cand in (512, 256, 128, 64, 32, 16, 8):
        if r % cand == 0 and r > cand and cand * n * 4 <= ADAMW_BLOCK_BYTES:
            br = cand
            break
    c1 = 1.0 - ADAM_B1 ** ADAM_STEP
    c2 = 1.0 - ADAM_B2 ** ADAM_STEP

    def body(w_ref, g_ref, m_ref, v_ref, d_ref, mo_ref, vo_ref):
        gv = g_ref[...]
        mn = ADAM_B1 * m_ref[...] + (1.0 - ADAM_B1) * gv
        vn = ADAM_B2 * v_ref[...] + (1.0 - ADAM_B2) * (gv * gv)
        mo_ref[...] = mn
        vo_ref[...] = vn
        d_ref[...] = -ADAM_LR * ((mn / c1) / (jnp.sqrt(vn / c2) + ADAM_EPS) + ADAM_WD * w_ref[...])

    spec = pl.BlockSpec((br, n), lambda i: (i, 0))
    return pl.pallas_call(
        body, name=name, grid=(r // br,), in_specs=[spec] * 4, out_specs=[spec] * 3,
        out_shape=[_sds((r, n), F32)] * 3, compiler_params=_cp(1),
    )(w, g, m, v)


def _place():
    return lax.axis_index("x"), lax.axis_index("y"), lax.axis_index("c")


def all_gather8(x_blk, name):
    m_per, n = x_blk.shape

    def body(x_ref, out_ref, send_sems, recv_sems, local_sem):
        x, y, c = _place()
        me, sibling = (x, y, c), (x, y, 1 - c)
        chips = [(1 - x, y), (x, 1 - y), (1 - x, 1 - y)]

        def rows(px, py, pc):
            return out_ref.at[pl.ds((4 * px + 2 * py + pc) * m_per, m_per), :]

        def copy(k, block, to, src=None):
            return pltpu.make_async_remote_copy(
                src_ref=rows(*block) if src is None else src, dst_ref=rows(*block),
                send_sem=send_sems.at[k], recv_sem=recv_sems.at[k], device_id=to, device_id_type=MESH)

        mine = pltpu.make_async_copy(x_ref, rows(*me), local_sem)
        mine.start()
        first = [copy(0, me, sibling, src=x_ref)]
        first += [copy(1 + j, me, (*chip, c), src=x_ref) for j, chip in enumerate(chips)]
        for cp in first:
            cp.start()
        passed = [copy(4 + j, (*chip, c), sibling) for j, chip in enumerate(chips)]
        for j, chip in enumerate(chips):
            copy(1 + j, (*chip, c), me).wait_recv()
            passed[j].start()
        copy(0, sibling, me).wait_recv()
        for j, chip in enumerate(chips):
            copy(4 + j, (*chip, 1 - c), me).wait_recv()
        for cp in first + passed:
            cp.wait_send()
        mine.wait()

    return pl.pallas_call(
        body, name=name, out_shape=_sds((8 * m_per, n), x_blk.dtype),
        in_specs=[pl.BlockSpec(memory_space=pltpu.VMEM)], out_specs=pl.BlockSpec(memory_space=pltpu.VMEM),
        scratch_shapes=[pltpu.SemaphoreType.DMA((7,)), pltpu.SemaphoreType.DMA((7,)), pltpu.SemaphoreType.DMA],
        compiler_params=pltpu.CompilerParams(vmem_limit_bytes=VMEM_LIMIT),
    )(x_blk)


def all_gather_chips(shards, name):
    nt = len(shards)

    def body(*refs):
        w_refs, out_refs = refs[:nt], refs[nt:2 * nt]
        send_sems, recv_sems, local_sems = refs[2 * nt:]
        x, y, c = _place()
        sibling = (x, y, 1 - c)
        chips = [(1 - x, y), (x, 1 - y), (1 - x, 1 - y)]

        def copy(t, k, block, to, src=None):
            px, py, hh = block
            dst = out_refs[t].at[2 * px + py, hh]
            return pltpu.make_async_remote_copy(
                src_ref=dst if src is None else src, dst_ref=dst,
                send_sem=send_sems.at[6 * t + k], recv_sem=recv_sems.at[6 * t + k], device_id=to, device_id_type=MESH)

        mine = [pltpu.make_async_copy(w_refs[t], out_refs[t].at[2 * x + y], local_sems.at[t]) for t in range(nt)]
        for cp in mine:
            cp.start()
        first = [copy(t, j, (x, y, c), (*chip, c), src=w_refs[t].at[c]) for t in range(nt) for j, chip in enumerate(chips)]
        for cp in first:
            cp.start()
        passed = []
        for t in range(nt):
            for j, chip in enumerate(chips):
                copy(t, j, (*chip, c), (x, y, c)).wait_recv()
                fwd = copy(t, 3 + j, (*chip, c), sibling)
                fwd.start()
                passed.append(fwd)
        for t in range(nt):
            for j, chip in enumerate(chips):
                copy(t, 3 + j, (*chip, 1 - c), (x, y, c)).wait_recv()
        for cp in first + passed:
            cp.wait_send()
        for cp in mine:
            cp.wait()

    hbm = pl.BlockSpec(memory_space=pl.ANY)
    return pl.pallas_call(
        body, name=name, out_shape=[_sds((N_CHIPS, *w.shape), w.dtype) for w in shards],
        in_specs=[hbm] * nt, out_specs=[hbm] * nt,
        scratch_shapes=[pltpu.SemaphoreType.DMA((6 * nt,)), pltpu.SemaphoreType.DMA((6 * nt,)),
                        pltpu.SemaphoreType.DMA((nt,))],
    )(*shards)


def sibling_swap_halves(grads, name):
    nt = len(grads)

    def body(*refs):
        g_refs, a_refs = refs[:nt], refs[nt:2 * nt]
        send_sems, recv_sems = refs[2 * nt:]
        x, y, c = _place()
        cps = [pltpu.make_async_remote_copy(
            src_ref=g_refs[t].at[j, 1 - c], dst_ref=a_refs[t].at[j], send_sem=send_sems.at[N_CHIPS * t + j],
            recv_sem=recv_sems.at[N_CHIPS * t + j], device_id=(x, y, 1 - c), device_id_type=MESH)
            for t in range(nt) for j in range(N_CHIPS)]
        for cp in cps:
            cp.start()
        for cp in cps:
            cp.wait()

    hbm = pl.BlockSpec(memory_space=pl.ANY)
    return pl.pallas_call(
        body, name=name, out_shape=[_sds((N_CHIPS, *g.shape[2:]), g.dtype) for g in grads],
        in_specs=[hbm] * nt, out_specs=[hbm] * nt,
        scratch_shapes=[pltpu.SemaphoreType.DMA((N_CHIPS * nt,)), pltpu.SemaphoreType.DMA((N_CHIPS * nt,))],
    )(*grads)


def _row_block(r, n, itemsize):
    best = None
    for br in range(16, r + 1, 16):
        if r % br == 0 and br * n * itemsize <= COMM_BLOCK_BYTES:
            best = br
    assert best is not None, (r, n)
    return best


def add_own_half(g, recv, core, name):
    nch, _, r, n = g.shape
    br = _row_block(r, n, 4)

    def body(c_ref, g_ref, a_ref, o_ref):
        o_ref[...] = (g_ref[...] + a_ref[...]).astype(BF16)

    return pl.pallas_call(
        body, name=name,
        grid_spec=pltpu.PrefetchScalarGridSpec(
            num_scalar_prefetch=1, grid=(nch, r // br),
            in_specs=[pl.BlockSpec((None, None, br, n), lambda j, rr, cref: (j, cref[0], rr, 0)),
                      pl.BlockSpec((None, br, n), lambda j, rr, cref: (j, rr, 0))],
            out_specs=pl.BlockSpec((None, br, n), lambda j, rr, cref: (j, rr, 0))),
        out_shape=_sds((nch, r, n), BF16), compiler_params=_cp(2),
    )(core, g, recv)


def chip_exchange(sums, name):
    nt = len(sums)

    def body(*refs):
        s_refs, b_refs = refs[:nt], refs[nt:2 * nt]
        send_sems, recv_sems, local_sems = refs[2 * nt:]
        x, y, c = _place()
        k = 2 * x + y
        chips = [(1 - x, y), (x, 1 - y), (1 - x, 1 - y)]
        mine = [pltpu.make_async_copy(s_refs[t].at[k], b_refs[t].at[k], local_sems.at[t]) for t in range(nt)]
        for cp in mine:
            cp.start()

        def copy(t, j, src_idx, dst_idx):
            px, py = chips[j]
            return pltpu.make_async_remote_copy(
                src_ref=s_refs[t].at[src_idx], dst_ref=b_refs[t].at[dst_idx], send_sem=send_sems.at[3 * t + j],
                recv_sem=recv_sems.at[3 * t + j], device_id=(px, py, c), device_id_type=MESH)

        cps = [copy(t, j, 2 * chips[j][0] + chips[j][1], k) for t in range(nt) for j in range(3)]
        for cp in cps:
            cp.start()
        for t in range(nt):
            for j in range(3):
                copy(t, j, k, 2 * chips[j][0] + chips[j][1]).wait_recv()
        for cp in cps:
            cp.wait_send()
        for cp in mine:
            cp.wait()

    hbm = pl.BlockSpec(memory_space=pl.ANY)
    return pl.pallas_call(
        body, name=name, out_shape=[_sds(sm.shape, sm.dtype) for sm in sums],
        in_specs=[hbm] * nt, out_specs=[hbm] * nt,
        scratch_shapes=[pltpu.SemaphoreType.DMA((3 * nt,)), pltpu.SemaphoreType.DMA((3 * nt,)),
                        pltpu.SemaphoreType.DMA((nt,))],
    )(*sums)


def sum_pieces(b, name):
    nch, r, n = b.shape
    br = _row_block(r, n, 4 * nch)

    def body(b_ref, o_ref):
        acc = b_ref[0].astype(F32)
        for kk in range(1, nch):
            acc = acc + b_ref[kk].astype(F32)
        o_ref[...] = acc

    return pl.pallas_call(
        body, name=name, grid=(r // br,), in_specs=[pl.BlockSpec((nch, br, n), lambda i: (0, i, 0))],
        out_specs=pl.BlockSpec((br, n), lambda i: (i, 0)), out_shape=_sds((r, n), F32), compiler_params=_cp(1),
    )(b)


def sibling_join_halves(halves, name):
    nt = len(halves)

    def body(*refs):
        f_refs, o_refs = refs[:nt], refs[nt:2 * nt]
        send_sems, recv_sems, local_sems = refs[2 * nt:]
        x, y, c = _place()
        mine = [pltpu.make_async_copy(f_refs[t], o_refs[t].at[c], local_sems.at[t]) for t in range(nt)]
        for cp in mine:
            cp.start()

        def copy(t, hh):
            return pltpu.make_async_remote_copy(
                src_ref=f_refs[t], dst_ref=o_refs[t].at[hh], send_sem=send_sems.at[t], recv_sem=recv_sems.at[t],
                device_id=(x, y, 1 - c), device_id_type=MESH)

        cps = [copy(t, c) for t in range(nt)]
        for cp in cps:
            cp.start()
        for t in range(nt):
            copy(t, 1 - c).wait_recv()
        for cp in cps:
            cp.wait_send()
        for cp in mine:
            cp.wait()

    hbm = pl.BlockSpec(memory_space=pl.ANY)
    return pl.pallas_call(
        body, name=name, out_shape=[_sds((2, *f.shape), f.dtype) for f in halves],
        in_specs=[hbm] * nt, out_specs=[hbm] * nt,
        scratch_shapes=[pltpu.SemaphoreType.DMA((nt,)), pltpu.SemaphoreType.DMA((nt,)), pltpu.SemaphoreType.DMA((nt,))],
    )(*halves)


_SHARD_KIND = {"mla_w_in": "rows", "mla_w_uq": "cols", "mla_w_uk": "cols", "mla_w_uv": "cols", "mla_w_o": "rows",
               "fox_w_in": "cols", "fox_w_o": "rows", "ffn_w_gate": "chunk", "ffn_w_up": "chunk", "ffn_w_down": "chunk"}
_PACKED = tuple(_SHARD_KIND)


def _halves(shard):
    if shard.ndim == 3 and shard.shape[0] == 2:
        return shard
    r, n = shard.shape[-2:]
    return shard.reshape(2, r // 2, n)


def _cols_to_full(g):
    return jnp.transpose(g, (1, 0, 2)).reshape(g.shape[1], -1)


def _full_to_cols(w):
    k, n4 = w.shape
    return jnp.transpose(w.reshape(k, N_CHIPS, n4 // N_CHIPS), (1, 0, 2))


def _uq_perm():
    per = MLA_NOPE + MLA_ROPE
    half = MLA_ROPE // 2
    nope = [h * per + d for h in range(MLA_HEADS) for d in range(MLA_NOPE)]
    r1 = [h * per + MLA_NOPE + r for h in range(MLA_HEADS) for r in range(half)]
    r2 = [h * per + MLA_NOPE + half + r for h in range(MLA_HEADS) for r in range(half)]
    perm = np.array(nope + r1 + r2, dtype=np.int32)
    return perm, np.argsort(perm).astype(np.int32)


def _heads(a, nb, h):
    t = a.shape[0]
    return jnp.transpose(a.reshape(nb, t // nb, h, -1), (0, 2, 1, 3))


def _unheads(a):
    nb, h, s, dd = a.shape
    return jnp.transpose(a, (0, 2, 1, 3)).reshape(nb * s, h * dd)


def _row_tiles(a):
    nb, h, s, _ = a.shape
    return a.reshape(nb, h, s // ATTN_TILE, 1, ATTN_TILE)


def _local_step(x, positions, target, mods, wts, ln_g, ln_b, mla_g_q, mla_g_kv, fox_b_f):
    nb, s, d = x.shape
    t = nb * s
    x0 = x.reshape(t, d)
    tgt = target.reshape(t, d)
    perm, inv_perm = _uq_perm()

    half = MLA_ROPE // 2
    inv_freq = ROPE_THETA ** (-jnp.arange(half, dtype=F32) / half)
    ang = positions.astype(F32).reshape(t, 1) * inv_freq
    cos, sin = jnp.cos(ang), jnp.sin(ang)
    cos8, sin8 = jnp.tile(cos, (1, MLA_HEADS)), jnp.tile(sin, (1, MLA_HEADS))
    cos64 = jnp.concatenate([cos, cos], axis=1)
    sin64s = jnp.concatenate([-sin, sin], axis=1)
    swap64 = jnp.asarray(np.roll(np.eye(MLA_ROPE, dtype=np.float32), half, axis=1))
    head_sum = jnp.asarray(np.tile(np.eye(MLA_ROPE, dtype=np.float32), (MLA_HEADS, 1)), dtype=BF16)
    sel_mla = jnp.asarray(np.pad(np.kron(np.eye(MLA_HEADS, dtype=np.float32), np.ones((MLA_V, 1), np.float32)),
                                 ((0, 0), (0, 128 - MLA_HEADS))))
    sel_fox = jnp.asarray(np.pad(np.kron(np.eye(FOX_HEADS, dtype=np.float32), np.ones((FOX_HD, 1), np.float32)),
                                 ((0, 0), (0, 128 - FOX_HEADS))))
    tri = jnp.asarray(np.tril(np.ones((128, 128), np.float32)))
    triu = jnp.asarray(np.triu(np.ones((128, 128), np.float32)))

    def vec(a):
        return a.reshape(1, -1)

    w_uq_p = wts["mla_w_uq"][:, perm]
    fox_w_qkv = wts["fox_w_in"][:, :3 * d]
    fox_w_f = jnp.pad(wts["fox_w_in"][:, 3 * d:], ((0, 0), (0, 128 - FOX_HEADS)))
    b_f_pad = jnp.pad(fox_b_f.reshape(1, -1), ((0, 0), (0, 128 - FOX_HEADS)))

    sh_a, sc_a, gt_a, sh_f, sc_f, gt_f = mods[0]
    h_in, u_m = mod_linear(x0, sh_a, sc_a, wts["mla_w_in"], F32, "mla_in", emit_u=True)
    q_m, kn_m, v_m, kr_m, cq_m, ckv_m = mla_mid_fwd(
        h_in, vec(mla_g_q), vec(mla_g_kv), w_uq_p, wts["mla_w_uk"], wts["mla_w_uv"], cos8, sin8, cos64, sin64s, swap64,
        "mla_mid")
    hq = MLA_HEADS * MLA_NOPE
    hr = MLA_HEADS * half
    q_cat = jnp.concatenate([q_m[:, :hq].reshape(nb, s, MLA_HEADS, MLA_NOPE),
                             q_m[:, hq:hq + hr].reshape(nb, s, MLA_HEADS, half),
                             q_m[:, hq + hr:].reshape(nb, s, MLA_HEADS, half)], axis=-1)
    q_cat = jnp.transpose(q_cat, (0, 2, 1, 3))
    k_cat = jnp.concatenate([kn_m.reshape(nb, s, MLA_HEADS, MLA_NOPE),
                             jnp.broadcast_to(kr_m.reshape(nb, s, 1, MLA_ROPE), (nb, s, MLA_HEADS, MLA_ROPE))], axis=-1)
    k_cat = jnp.transpose(k_cat, (0, 2, 1, 3))
    v_h = _heads(v_m, nb, MLA_HEADS)
    scale_m = (MLA_NOPE + MLA_ROPE) ** -0.5
    o_m, lse_m = attn_fwd(q_cat, k_cat, v_h, None, None, scale_m, "mla_attn")
    o_m2 = _unheads(o_m)
    y0, x1 = linear_resid_ln(o_m2, wts["mla_w_o"], x0, gt_a, vec(ln_g[0, 0]), vec(ln_b[0, 0]), "mla_out")
    u_f0, hg0, hu0, y1, x2 = ffn_fwd(x1, sh_f, sc_f, gt_f, wts["ffn_w_gate"], wts["ffn_w_up"], wts["ffn_w_down"], 0,
                                     vec(ln_g[0, 1]), vec(ln_b[0, 1]), "ffn0")
    sh_a1, sc_a1, gt_a1, sh_f1, sc_f1, gt_f1 = mods[1]
    qkv, u_x = mod_linear(x2, sh_a1, sc_a1, fox_w_qkv, BF16, "fox_qkv", tn=1024, emit_u=True)
    hf = mod_linear(x2, sh_a1, sc_a1, fox_w_f, F32, "fox_f")
    cum = fox_gate_fwd(hf, b_f_pad, tri, nb, "fox_gate")
    cum_h = jnp.transpose(cum[:, :FOX_HEADS].reshape(nb, s, FOX_HEADS), (0, 2, 1))[..., None]
    q_x = _heads(qkv[:, :d], nb, FOX_HEADS)
    k_x = _heads(qkv[:, d:2 * d], nb, FOX_HEADS)
    v_x = _heads(qkv[:, 2 * d:], nb, FOX_HEADS)
    scale_x = FOX_HD ** -0.5
    o_x, lse_x = attn_fwd(q_x, k_x, v_x, cum_h, _row_tiles(cum_h), scale_x, "fox_attn")
    o_x2 = _unheads(o_x)
    y2, x3 = linear_resid_ln(o_x2, wts["fox_w_o"], x2, gt_a1, vec(ln_g[1, 0]), vec(ln_b[1, 0]), "fox_out")
    u_f1, hg1, hu1, y3, x4 = ffn_fwd(x3, sh_f1, sc_f1, gt_f1, wts["ffn_w_gate"], wts["ffn_w_up"], wts["ffn_w_down"], 1,
                                     vec(ln_g[1, 1]), vec(ln_b[1, 1]), "ffn1")
    dx4, sq_err = loss_grad(x4, tgt, "loss")
    loss_part = 0.5 * jnp.sum(sq_err) / d

    grads = {}
    dz3, dy3, dg11, db11, dgt_f1 = ln_bwd(dx4, x3, y3, gt_f1, vec(ln_g[1, 1]), "ffn1_ln_bwd")
    dhg1, dhu1, act1, dx3, dsc_f1, dsh_f1 = ffn_bwd(dy3, hg1, hu1, wts["ffn_w_gate"], wts["ffn_w_up"],
                                                    wts["ffn_w_down"], 1, dz3, x3, sc_f1, "ffn1_bwd")
    dwg1 = wgrad(u_f1[None], dhg1, "ffn1_dwg", slot=(1, DEPTH, None))
    dwu1 = wgrad(u_f1[None], dhu1, "ffn1_dwu", slot=(1, DEPTH, None))
    dwd1 = wgrad(act1, dy3[None], "ffn1_dwd", slot=(1, DEPTH, None))
    dz2, dy2, dg10, db10, dgt_a1 = ln_bwd(dx3, x2, y2, gt_a1, vec(ln_g[1, 0]), "fox_ln_bwd")
    do_x, delta_x = linear_nt_delta(dy2, wts["fox_w_o"], o_x2, sel_fox, "fox_out_bwd")
    grads["fox_w_o"] = wgrad(o_x2[None], dy2[None], "fox_dwo")[0]
    delta_xh = jnp.transpose(delta_x[:, :FOX_HEADS].reshape(nb, s, FOX_HEADS), (0, 2, 1))[..., None]
    dq_x, dk_x, dv_x, dfk_x, dfq_x = attn_bwd(q_x, k_x, v_x, _heads(do_x, nb, FOX_HEADS), _row_tiles(lse_x),
                                              _row_tiles(delta_xh), _row_tiles(cum_h), cum_h, scale_x, "fox_attn_bwd")
    dcum_h = dfk_x[..., 0] + dfq_x.reshape(nb, FOX_HEADS, s)
    dcum = jnp.pad(jnp.transpose(dcum_h, (0, 2, 1)).reshape(t, FOX_HEADS), ((0, 0), (0, 128 - FOX_HEADS)))
    dhf, dbf = fox_gate_bwd(dcum, hf, b_f_pad, triu, nb, "fox_gate_bwd")
    dqkv = jnp.concatenate([_unheads(dq_x).astype(BF16), _unheads(dk_x), _unheads(dv_x)], axis=1)
    dx2, dsc_a1, dsh_a1 = linear_nt_mod_bwd([(dqkv, fox_w_qkv), (dhf, fox_w_f)], dz2, x2, sc_a1, "fox_in_bwd")
    dw_qkv = wgrad(u_x[None], dqkv[None], "fox_dwqkv")[0]
    dw_f = wgrad(u_x[None], dhf[None], "fox_dwf")[0]
    grads["fox_w_in"] = jnp.concatenate([dw_qkv, dw_f[:, :FOX_HEADS]], axis=1)
    dz1, dy1, dg01, db01, dgt_f0 = ln_bwd(dx2, x1, y1, gt_f, vec(ln_g[0, 1]), "ffn0_ln_bwd")
    dhg0, dhu0, act0, dx1, dsc_f0, dsh_f0 = ffn_bwd(dy1, hg0, hu0, wts["ffn_w_gate"], wts["ffn_w_up"],
                                                    wts["ffn_w_down"], 0, dz1, x1, sc_f, "ffn0_bwd")
    grads["ffn_w_gate"] = wgrad(u_f0[None], dhg0, "ffn0_dwg", slot=(0, DEPTH, dwg1))
    grads["ffn_w_up"] = wgrad(u_f0[None], dhu0, "ffn0_dwu", slot=(0, DEPTH, dwu1))
    grads["ffn_w_down"] = wgrad(act0, dy1[None], "ffn0_dwd", slot=(0, DEPTH, dwd1))
    dz0, dy0, dg00, db00, dgt_a0 = ln_bwd(dx1, x0, y0, gt_a, vec(ln_g[0, 0]), "mla_ln_bwd")
    do_m, delta_m = linear_nt_delta(dy0, wts["mla_w_o"], o_m2, sel_mla, "mla_out_bwd")
    grads["mla_w_o"] = wgrad(o_m2[None], dy0[None], "mla_dwo")[0]
    delta_mh = jnp.transpose(delta_m[:, :MLA_HEADS].reshape(nb, s, MLA_HEADS), (0, 2, 1))[..., None]
    dq_c, dk_c, dv_c = attn_bwd(q_cat, k_cat, v_h, _heads(do_m, nb, MLA_HEADS), _row_tiles(lse_m), _row_tiles(delta_mh),
                                None, None, scale_m, "mla_attn_bwd")
    dq_t = jnp.transpose(dq_c, (0, 2, 1, 3))
    dq_m = jnp.concatenate([dq_t[..., :MLA_NOPE].reshape(t, hq), dq_t[..., MLA_NOPE:MLA_NOPE + half].reshape(t, hr),
                            dq_t[..., MLA_NOPE + half:].reshape(t, hr)], axis=1).astype(BF16)
    dk_t = jnp.transpose(dk_c, (0, 2, 1, 3))
    dkn_m = dk_t[..., :MLA_NOPE].reshape(t, hq)
    dkr_heads = dk_t[..., MLA_NOPE:].reshape(t, MLA_HEADS * MLA_ROPE)
    dh_in, dq_pre, dgq, dgkv = mla_mid_bwd(
        dq_m, dkn_m, _unheads(dv_c), dkr_heads, h_in, vec(mla_g_q), vec(mla_g_kv), w_uq_p, wts["mla_w_uk"],
        wts["mla_w_uv"], cos8, sin8, cos64, sin64s, swap64, head_sum, "mla_mid_bwd")
    grads["mla_w_uq"] = wgrad(cq_m[None], dq_pre[None], "mla_dwuq")[0][:, inv_perm]
    grads["mla_w_uk"] = wgrad(ckv_m[None], dkn_m[None], "mla_dwuk")[0]
    grads["mla_w_uv"] = wgrad(ckv_m[None], _unheads(dv_c)[None], "mla_dwuv")[0]
    grads["mla_w_in"] = wgrad(u_m[None], dh_in[None], "mla_dwin")[0]
    dx0, dsc_a0, dsh_a0 = linear_nt_mod_bwd([(dh_in, wts["mla_w_in"])], dz0, x0, sc_a, "mla_in_bwd")

    dmods = [(dsh_a0, dsc_a0, dgt_a0, dsh_f0, dsc_f0, dgt_f0), (dsh_a1, dsc_a1, dgt_a1, dsh_f1, dsc_f1, dgt_f1)]
    d_ln_g = jnp.stack([jnp.concatenate([dg00, dg01], axis=0), jnp.concatenate([dg10, dg11], axis=0)])
    d_ln_b = jnp.stack([jnp.concatenate([db00, db01], axis=0), jnp.concatenate([db10, db11], axis=0)])
    return loss_part, dx0.reshape(nb, s, d), grads, dmods, d_ln_g, d_ln_b, dgq, dgkv, dbf[:, :FOX_HEADS]


def _pad_rows(a, rows):
    return jnp.pad(a, ((0, rows - a.shape[0]), (0, 0)))


def kernel(x, c, positions, mla_w_in, mla_g_q, mla_w_uq, mla_g_kv, mla_w_uk, mla_w_uv, mla_w_o, fox_w_in, fox_b_f, fox_w_o, ada_w, ada_b, ffn_w_gate, ffn_w_up, ffn_w_down, ln_g, ln_b, loss_target, m_mla_w_in, m_mla_g_q, m_mla_w_uq, m_mla_g_kv, m_mla_w_uk, m_mla_w_uv, m_mla_w_o, m_fox_w_in, m_fox_b_f, m_fox_w_o, m_ada_w, m_ada_b, m_ffn_w_gate, m_ffn_w_up, m_ffn_w_down, m_ln_g, m_ln_b, v_mla_w_in, v_mla_g_q, v_mla_w_uq, v_mla_g_kv, v_mla_w_uk, v_mla_w_uv, v_mla_w_o, v_fox_w_in, v_fox_b_f, v_fox_w_o, v_ada_w, v_ada_b, v_ffn_w_gate, v_ffn_w_up, v_ffn_w_down, v_ln_g, v_ln_b):
    args = dict(locals())
    nb, s, d = x.shape
    ax, ay, ac = lax.axis_index("x"), lax.axis_index("y"), lax.axis_index("c")
    chip = 2 * ax + ay
    dev = 2 * chip + ac
    n_dev = 2 * N_CHIPS
    n_all = nb * n_dev

    shard_shapes = {n: (args[n].shape if _SHARD_KIND[n] == "chunk" else args[n].shape[1:]) for n in _PACKED}
    w_all = all_gather_chips([_halves(args[n].reshape(shard_shapes[n]).astype(BF16)) for n in _PACKED], "gather_weights")
    wts = {}
    for n, g in zip(_PACKED, w_all):
        g = g.reshape(N_CHIPS, *shard_shapes[n])
        if _SHARD_KIND[n] == "rows":
            g = g.reshape(-1, g.shape[-1])
        elif _SHARD_KIND[n] == "cols":
            g = _cols_to_full(g)
        wts[n] = g

    c_all = all_gather8(_pad_rows(c, 8), "gather_c").reshape(n_dev, 8, d)[:, :nb].reshape(n_all, d)
    mod_part = ada_mod_part(c_all, ada_w, "ada_mod")
    ncol = mod_part.shape[-1]
    mod_g = all_gather8(mod_part.reshape(DEPTH * n_all, ncol), "gather_mod")
    mod_g = mod_g.reshape(N_CHIPS, 2, DEPTH, n_all, ncol)[:, 0]
    mod_full = jnp.transpose(mod_g, (1, 2, 0, 3)).reshape(DEPTH, n_all, N_CHIPS * ncol) + ada_b[:, None, :]
    mod_loc = lax.dynamic_slice_in_dim(mod_full, dev * nb, nb, axis=1)
    mods = [tuple(mod_loc[i, :, k * d:(k + 1) * d].reshape(nb, 1, d) for k in range(6)) for i in range(DEPTH)]

    ln_cols = ln_g.shape[-1]
    ln_blk = jnp.concatenate([ln_g.reshape(2 * DEPTH, ln_cols), ln_b.reshape(2 * DEPTH, ln_cols)], axis=0)
    ln_all = all_gather8(ln_blk, "gather_ln").reshape(N_CHIPS, 2, 4 * DEPTH, ln_cols)[:, 0]
    ln_all = jnp.transpose(ln_all, (1, 0, 2)).reshape(4 * DEPTH, d)
    ln_g_full = ln_all[:2 * DEPTH].reshape(DEPTH, 2, d)
    ln_b_full = ln_all[2 * DEPTH:].reshape(DEPTH, 2, d)

    loss_part, grad_x, grads, dmods, d_ln_g, d_ln_b, dgq, dgkv, dbf = _local_step(
        x, positions, loss_target, mods, wts, ln_g_full, ln_b_full, mla_g_q[0], mla_g_kv[0], fox_b_f[0])
    loss = lax.psum(loss_part, ("x", "y", "c"))

    dmod_rows = jnp.stack([jnp.concatenate([v_.reshape(nb, d) for v_ in dm], axis=1) for dm in dmods])
    small = jnp.concatenate([
        d_ln_g.reshape(2 * DEPTH, d), d_ln_b.reshape(2 * DEPTH, d),
        jnp.pad(jnp.concatenate([dgq, dgkv, dbf], axis=1), ((0, 0), (0, d - 2 * MLA_QR - FOX_HEADS))),
        dmod_rows.reshape(DEPTH * nb * 6, d)], axis=0)
    n_small = small.shape[0]
    small_rows = -(-n_small // 8) * 8
    small_all = all_gather8(_pad_rows(small, small_rows), "gather_stats").reshape(n_dev, small_rows, d)
    stat_sum = sum_leading(small_all, "sum_stats")
    g_ln_g = lax.dynamic_slice_in_dim(stat_sum[:2 * DEPTH], chip * ln_cols, ln_cols, axis=1).reshape(DEPTH, 2, ln_cols)
    g_ln_b = lax.dynamic_slice_in_dim(stat_sum[2 * DEPTH:4 * DEPTH], chip * ln_cols, ln_cols, axis=1).reshape(DEPTH, 2, ln_cols)
    row = stat_sum[4 * DEPTH]
    g_gq = row[:MLA_QR].reshape(1, MLA_QR)
    g_gkv = row[MLA_QR:2 * MLA_QR].reshape(1, MLA_KVR)
    g_bf = row[2 * MLA_QR:2 * MLA_QR + FOX_HEADS].reshape(1, FOX_HEADS)
    base = 4 * DEPTH + 1
    dmod_all = small_all[:, base:base + DEPTH * nb * 6].reshape(n_dev, DEPTH, nb, 6 * d)
    dmod_all = jnp.transpose(dmod_all, (1, 0, 2, 3)).reshape(DEPTH, n_all, 6 * d)
    g_ada_b = sum_leading(jnp.transpose(dmod_all, (1, 0, 2)), "sum_ada_b")
    dmod_mine = lax.dynamic_slice_in_dim(dmod_all, chip * ncol, ncol, axis=2)
    g_ada_w = ada_grad(c_all.T, dmod_mine, "ada_grad")

    g_blocks = []
    for n in _PACKED:
        g = grads[n]
        if _SHARD_KIND[n] == "rows":
            g = g.reshape(N_CHIPS, 2, g.shape[0] // (2 * N_CHIPS), g.shape[1])
        elif _SHARD_KIND[n] == "cols":
            g = _full_to_cols(g)
            g = g.reshape(N_CHIPS, 2, g.shape[1] // 2, g.shape[2])
        g_blocks.append(g)
    core = ac.reshape(1).astype(jnp.int32)
    recv = sibling_swap_halves(g_blocks, "rs_swap")
    chip_sums = [add_own_half(g, a, core, "rs_add_" + n) for n, g, a in zip(_PACKED, g_blocks, recv)]
    pieces = chip_exchange(chip_sums, "rs_exchange")
    my_halves = [sum_pieces(b, "rs_sum_" + n) for n, b in zip(_PACKED, pieces)]
    joined = sibling_join_halves(my_halves, "rs_join")
    g_big = {n: j.reshape(shard_shapes[n]) for n, j in zip(_PACKED, joined)}

    g_out = {
        "mla_w_in": g_big["mla_w_in"], "mla_g_q": g_gq, "mla_w_uq": g_big["mla_w_uq"], "mla_g_kv": g_gkv,
        "mla_w_uk": g_big["mla_w_uk"], "mla_w_uv": g_big["mla_w_uv"], "mla_w_o": g_big["mla_w_o"],
        "fox_w_in": g_big["fox_w_in"], "fox_b_f": g_bf, "fox_w_o": g_big["fox_w_o"],
        "ada_w": g_ada_w, "ada_b": g_ada_b, "ffn_w_gate": g_big["ffn_w_gate"], "ffn_w_up": g_big["ffn_w_up"],
        "ffn_w_down": g_big["ffn_w_down"], "ln_g": g_ln_g, "ln_b": g_ln_b}
    names = ["mla_w_in", "mla_g_q", "mla_w_uq", "mla_g_kv", "mla_w_uk", "mla_w_uv", "mla_w_o", "fox_w_in", "fox_b_f",
             "fox_w_o", "ada_w", "ada_b", "ffn_w_gate", "ffn_w_up", "ffn_w_down", "ln_g", "ln_b"]
    small_names = ["mla_g_q", "mla_g_kv", "fox_b_f", "ada_b", "ln_g", "ln_b"]
    deltas, new_m, new_v = {}, {}, {}
    for n in names:
        if n in small_names:
            continue
        shp = args[n].shape
        two_d = (-1, shp[-1])
        dl, mn, vn = adamw(args[n].reshape(two_d), g_out[n].reshape(two_d), args["m_" + n].reshape(two_d),
                           args["v_" + n].reshape(two_d), "adamw_" + n)
        deltas[n], new_m[n], new_v[n] = dl.reshape(shp), mn.reshape(shp), vn.reshape(shp)

    def small_pack(prefix, src):
        flat = jnp.concatenate([src[prefix + n].reshape(-1) for n in small_names])
        size = -(-flat.shape[0] // (8 * 128)) * 8 * 128
        return jnp.pad(flat, (0, size - flat.shape[0])).reshape(-1, 128)

    sd, sm, sv = adamw(small_pack("", args), small_pack("", g_out), small_pack("m_", args), small_pack("v_", args),
                       "adamw_small")
    off = 0
    for n in small_names:
        shp = args[n].shape
        size = math.prod(shp)
        deltas[n] = sd.reshape(-1)[off:off + size].reshape(shp)
        new_m[n] = sm.reshape(-1)[off:off + size].reshape(shp)
        new_v[n] = sv.reshape(-1)[off:off + size].reshape(shp)
        off += size

    outs = [loss, grad_x]
    outs += [g_out[n].reshape(args[n].shape) for n in names]
    outs += [deltas[n] for n in names] + [new_m[n] for n in names] + [new_v[n] for n in names]
    return tuple(outs)
```

```python
import functools
import math

import numpy as np
import jax
import jax.numpy as jnp
from jax import lax
from jax.experimental import pallas as pl
from jax.experimental.pallas import tpu as pltpu

F32 = jnp.float32
BF16 = jnp.bfloat16
MESH = pl.DeviceIdType.MESH

D_MODEL = 1024
DEPTH = 2
MLA_HEADS = 8
MLA_NOPE = 128
MLA_ROPE = 64
MLA_V = 128
MLA_QR = 256
MLA_KVR = 256
ROPE_THETA = 10000.0
FOX_HEADS = 16
FOX_HD = 64
D_FF = 2816
N_CHIPS = 4
FF_CHUNK = D_FF // N_CHIPS
ALPHA = (2.0 * DEPTH) ** 0.25
EPS = 1e-5
ADAM_LR = 0.001
ADAM_B1 = 0.9
ADAM_B2 = 0.999
ADAM_EPS = 1e-08
ADAM_WD = 0.01
ADAM_STEP = 10

VMEM_LIMIT = 56 * 1024 * 1024
TOKEN_TILE = 256
ATTN_TILE = 512
COMM_BLOCK_BYTES = 2 * 1024 * 1024
ADAMW_BLOCK_BYTES = 1024 * 1024


def _cp(n_axes):
    return pltpu.CompilerParams(dimension_semantics=("arbitrary",) * n_axes, vmem_limit_bytes=VMEM_LIMIT)


def _dot(a, b):
    return jnp.dot(a, b, preferred_element_type=F32)


def _dot_nt(a, b):
    return lax.dot_general(a, b, (((1,), (1,)), ((), ())), preferred_element_type=F32)


def _dot_tn(a, b):
    return lax.dot_general(a, b, (((0,), (0,)), ((), ())), preferred_element_type=F32)


def _dot_f32(a, b):
    return jnp.dot(a, b, preferred_element_type=F32, precision=lax.Precision.HIGHEST)


def _sds(shape, dtype):
    return jax.ShapeDtypeStruct(shape, dtype)


def mod_linear(x, shift, scale, w, out_dtype, name, tn=None, emit_u=False):
    t, d = x.shape
    n = w.shape[1]
    tn = n if tn is None else tn
    tm = TOKEN_TILE
    tps = (t // shift.shape[0]) // tm

    def body(x_ref, sh_ref, sc_ref, w_ref, o_ref, *rest):
        u = (x_ref[...] * (1.0 + sc_ref[...]) + sh_ref[...]).astype(BF16)
        o_ref[...] = _dot(u, w_ref[...]).astype(out_dtype)
        if emit_u:
            @pl.when(pl.program_id(1) == 0)
            def _():
                rest[0][...] = u

    vec = pl.BlockSpec((None, 1, d), lambda i, j: (i // tps, 0, 0))
    out_shape = [_sds((t, n), out_dtype)]
    out_specs = [pl.BlockSpec((tm, tn), lambda i, j: (i, j))]
    if emit_u:
        out_shape.append(_sds((t, d), BF16))
        out_specs.append(pl.BlockSpec((tm, d), lambda i, j: (i, 0)))
    res = pl.pallas_call(
        body, name=name, grid=(t // tm, n // tn),
        in_specs=[pl.BlockSpec((tm, d), lambda i, j: (i, 0)), vec, vec,
                  pl.BlockSpec((d, tn), lambda i, j: (0, j))],
        out_specs=out_specs, out_shape=out_shape, compiler_params=_cp(2),
    )(x, shift, scale, w)
    return res if emit_u else res[0]


def _rms(h, g):
    rstd = lax.rsqrt(jnp.mean(h * h, axis=-1, keepdims=True) + EPS)
    return h * rstd, rstd


def mla_mid_fwd(h, g_q, g_kv, w_uq, w_uk, w_uv, cos8, sin8, cos64, sin64s, swap64, name):
    t = h.shape[0]
    tm = TOKEN_TILE
    hq = MLA_HEADS * MLA_NOPE
    hr = MLA_HEADS * MLA_ROPE // 2

    def body(h_ref, gq_ref, gkv_ref, wuq_ref, wuk_ref, wuv_ref, c8_ref, s8_ref, c64_ref, s64_ref, sw_ref,
             q_ref, kn_ref, v_ref, kr_ref, cq_ref, ckv_ref):
        hh = h_ref[...]
        cq = (_rms(hh[:, :MLA_QR], None)[0] * gq_ref[...]).astype(BF16)
        ckv = (_rms(hh[:, MLA_QR:MLA_QR + MLA_KVR], None)[0] * gkv_ref[...]).astype(BF16)
        cq_ref[...] = cq
        ckv_ref[...] = ckv
        q = _dot(cq, wuq_ref[...])
        x1 = q[:, hq:hq + hr]
        x2 = q[:, hq + hr:]
        cs = c8_ref[...]
        sn = s8_ref[...]
        q_ref[...] = jnp.concatenate([q[:, :hq], x1 * cs - x2 * sn, x2 * cs + x1 * sn], axis=1).astype(BF16)
        kn_ref[...] = _dot(ckv, wuk_ref[...]).astype(BF16)
        v_ref[...] = _dot(ckv, wuv_ref[...]).astype(BF16)
        kr = hh[:, MLA_QR + MLA_KVR:]
        kr_ref[...] = (kr * c64_ref[...] + _dot_f32(kr, sw_ref[...]) * s64_ref[...]).astype(BF16)

    def rows(n):
        return pl.BlockSpec((tm, n), lambda i: (i, 0))

    def whole(a):
        return pl.BlockSpec(a.shape, lambda i: (0,) * a.ndim)

    nq = w_uq.shape[1]
    return pl.pallas_call(
        body, name=name, grid=(t // tm,),
        in_specs=[rows(h.shape[1]), whole(g_q), whole(g_kv), whole(w_uq), whole(w_uk), whole(w_uv),
                  rows(hr), rows(hr), rows(MLA_ROPE), rows(MLA_ROPE), whole(swap64)],
        out_specs=[rows(nq), rows(hq), rows(hq), rows(MLA_ROPE), rows(MLA_QR), rows(MLA_KVR)],
        out_shape=[_sds((t, nq), BF16), _sds((t, hq), BF16), _sds((t, hq), BF16), _sds((t, MLA_ROPE), BF16),
                   _sds((t, MLA_QR), BF16), _sds((t, MLA_KVR), BF16)],
        compiler_params=_cp(1),
    )(h, g_q, g_kv, w_uq, w_uk, w_uv, cos8, sin8, cos64, sin64s, swap64)


def attn_fwd(q, k, v, fq, fk, scale, name):
    b, h, s, dk = q.shape
    dv = v.shape[-1]
    tq = ATTN_TILE
    nq = s // tq
    has_bias = fq is not None

    def body(*refs):
        if has_bias:
            q_ref, k_ref, v_ref, fq_ref, fk_ref, o_ref, lse_ref = refs
        else:
            q_ref, k_ref, v_ref, o_ref, lse_ref = refs
        i = pl.program_id(2)
        qb = q_ref[...]

        def block(j, carry, masked):
            m, l, acc = carry
            start = pl.multiple_of(j * tq, tq)
            kb = k_ref[pl.ds(start, tq), :]
            vb = v_ref[pl.ds(start, tq), :]
            sc = _dot_nt(qb, kb) * scale
            if has_bias:
                sc = sc + fq_ref[...] - fk_ref[j]
            if masked:
                keep = lax.broadcasted_iota(jnp.int32, (tq, tq), 0) >= lax.broadcasted_iota(jnp.int32, (tq, tq), 1)
                sc = jnp.where(keep, sc, -1e30)
            m_new = jnp.maximum(m, jnp.max(sc, axis=1, keepdims=True))
            a = jnp.exp(m - m_new)
            p = jnp.exp(sc - m_new)
            l = a * l + jnp.sum(p, axis=1, keepdims=True)
            acc = a * acc + _dot(p.astype(BF16), vb)
            return m_new, l, acc

        init = (jnp.full((tq, 1), -1e30, F32), jnp.zeros((tq, 1), F32), jnp.zeros((tq, dv), F32))
        carry = lax.fori_loop(0, i, lambda j, c: block(j, c, False), init)
        m, l, acc = block(i, carry, True)
        o_ref[...] = (acc / l).astype(BF16)
        lse_ref[...] = m + jnp.log(l)

    def qspec(n):
        return pl.BlockSpec((None, None, tq, n), lambda bb, hh, i: (bb, hh, i, 0))

    def full(n):
        return pl.BlockSpec((None, None, s, n), lambda bb, hh, i: (bb, hh, 0, 0))

    in_specs = [qspec(dk), full(dk), full(dv)]
    args = [q, k, v]
    if has_bias:
        in_specs += [qspec(1), pl.BlockSpec((None, None, nq, 1, tq), lambda bb, hh, i: (bb, hh, 0, 0, 0))]
        args += [fq, fk]
    return pl.pallas_call(
        body, name=name, grid=(b, h, nq), in_specs=in_specs,
        out_specs=[qspec(dv), qspec(1)],
        out_shape=[_sds((b, h, s, dv), BF16), _sds((b, h, s, 1), F32)],
        compiler_params=_cp(3),
    )(*args)


def _layer_norm(z, g, b):
    mu = jnp.mean(z, axis=-1, keepdims=True)
    zc = z - mu
    rstd = lax.rsqrt(jnp.mean(zc * zc, axis=-1, keepdims=True) + EPS)
    xhat = zc * rstd
    return xhat * g + b, xhat, rstd


def linear_resid_ln(a, w, x_in, gate, ln_g, ln_b, name):
    t, kdim = a.shape
    d = w.shape[1]
    tm = TOKEN_TILE
    tps = (t // gate.shape[0]) // tm

    def body(a_ref, w_ref, x_ref, gt_ref, g_ref, b_ref, y_ref, xo_ref):
        y = _dot(a_ref[...], w_ref[...])
        y_ref[...] = y
        z = ALPHA * x_ref[...] + (1.0 + gt_ref[...]) * y
        xo_ref[...] = _layer_norm(z, g_ref[...], b_ref[...])[0]

    rows = pl.BlockSpec((tm, d), lambda i: (i, 0))
    vec = pl.BlockSpec((1, d), lambda i: (0, 0))
    return pl.pallas_call(
        body, name=name, grid=(t // tm,),
        in_specs=[pl.BlockSpec((tm, kdim), lambda i: (i, 0)), pl.BlockSpec((kdim, d), lambda i: (0, 0)), rows,
                  pl.BlockSpec((None, 1, d), lambda i: (i // tps, 0, 0)), vec, vec],
        out_specs=[rows, rows], out_shape=[_sds((t, d), F32), _sds((t, d), F32)],
        compiler_params=_cp(1),
    )(a, w, x_in, gate, ln_g, ln_b)


def ffn_fwd(x_in, shift, scale, gate, wg, wu, wd, layer, ln_g, ln_b, name):
    t, d = x_in.shape
    c, _, _, fc = wg.shape
    tm = TOKEN_TILE
    tps = (t // gate.shape[0]) // tm

    def body(x_ref, sh_ref, sc_ref, gt_ref, wg_ref, wu_ref, wd_ref, g_ref, b_ref,
             u_ref, hg_ref, hu_ref, y_ref, xo_ref, acc_ref):
        cc = pl.program_id(1)

        @pl.when(cc == 0)
        def _():
            u_ref[...] = (x_ref[...] * (1.0 + sc_ref[...]) + sh_ref[...]).astype(BF16)
            acc_ref[...] = jnp.zeros_like(acc_ref)

        u = u_ref[...]
        hg = _dot(u, wg_ref[...])
        hu = _dot(u, wu_ref[...])
        hg_ref[...] = hg.astype(BF16)
        hu_ref[...] = hu.astype(BF16)
        act = (hg * jax.nn.sigmoid(hg) * hu).astype(BF16)
        acc_ref[...] += _dot(act, wd_ref[...])

        @pl.when(cc == c - 1)
        def _():
            y = acc_ref[...]
            y_ref[...] = y
            z = ALPHA * x_ref[...] + (1.0 + gt_ref[...]) * y
            xo_ref[...] = _layer_norm(z, g_ref[...], b_ref[...])[0]

    rows = pl.BlockSpec((tm, d), lambda i, cc: (i, 0))
    bvec = pl.BlockSpec((None, 1, d), lambda i, cc: (i // tps, 0, 0))
    vec = pl.BlockSpec((1, d), lambda i, cc: (0, 0))
    hspec = pl.BlockSpec((None, tm, fc), lambda i, cc: (cc, i, 0))
    return pl.pallas_call(
        body, name=name, grid=(t // tm, c),
        in_specs=[rows, bvec, bvec, bvec,
                  pl.BlockSpec((None, None, d, fc), lambda i, cc: (cc, layer, 0, 0)),
                  pl.BlockSpec((None, None, d, fc), lambda i, cc: (cc, layer, 0, 0)),
                  pl.BlockSpec((None, None, fc, d), lambda i, cc: (cc, layer, 0, 0)), vec, vec],
        out_specs=[rows, hspec, hspec, rows, rows],
        out_shape=[_sds((t, d), BF16), _sds((c, t, fc), BF16), _sds((c, t, fc), BF16),
                   _sds((t, d), F32), _sds((t, d), F32)],
        scratch_shapes=[pltpu.VMEM((tm, d), F32)],
        compiler_params=_cp(2),
    )(x_in, shift, scale, gate, wg, wu, wd, ln_g, ln_b)


def fox_gate_fwd(hf, b_f, tri, n_batch, name):
    t, n = hf.shape
    blk = tri.shape[0]
    nb = (t // n_batch) // blk

    def body(hf_ref, b_ref, tri_ref, o_ref, carry_ref):
        @pl.when(pl.program_id(1) == 0)
        def _():
            carry_ref[...] = jnp.zeros_like(carry_ref)

        xx = hf_ref[...] + b_ref[...]
        lf = jnp.minimum(xx, 0.0) - jnp.log(1.0 + jnp.exp(-jnp.abs(xx)))
        cum = _dot_f32(tri_ref[...], lf) + carry_ref[...]
        o_ref[...] = cum
        carry_ref[...] = cum[blk - 1:blk, :]

    return pl.pallas_call(
        body, name=name, grid=(n_batch, nb),
        in_specs=[pl.BlockSpec((blk, n), lambda bb, i: (bb * nb + i, 0)), pl.BlockSpec((1, n), lambda bb, i: (0, 0)),
                  pl.BlockSpec((blk, blk), lambda bb, i: (0, 0))],
        out_specs=pl.BlockSpec((blk, n), lambda bb, i: (bb * nb + i, 0)),
        out_shape=_sds((t, n), F32), scratch_shapes=[pltpu.VMEM((1, n), F32)],
        compiler_params=_cp(2),
    )(hf, b_f, tri)


def loss_grad(x_out, target, name):
    t, d = x_out.shape
    tm = TOKEN_TILE

    def body(x_ref, t_ref, g_ref, l_ref):
        @pl.when(pl.program_id(0) == 0)
        def _():
            l_ref[...] = jnp.zeros_like(l_ref)

        err = x_ref[...] - t_ref[...]
        g_ref[...] = err / d
        l_ref[...] += jnp.sum(err * err, axis=0, keepdims=True)

    rows = pl.BlockSpec((tm, d), lambda i: (i, 0))
    return pl.pallas_call(
        body, name=name, grid=(t // tm,), in_specs=[rows, rows],
        out_specs=[rows, pl.BlockSpec((1, d), lambda i: (0, 0))],
        out_shape=[_sds((t, d), F32), _sds((1, d), F32)], compiler_params=_cp(1),
    )(x_out, target)


def ln_bwd(dxo, x_in, y, gate, ln_g, name):
    t, d = dxo.shape
    nb = gate.shape[0]
    tm = TOKEN_TILE
    tps = (t // nb) // tm

    def body(dxo_ref, x_ref, y_ref, gt_ref, g_ref, dz_ref, dy_ref, dg_ref, db_ref, dgt_ref):
        i = pl.program_id(0)

        @pl.when(i == 0)
        def _():
            dg_ref[...] = jnp.zeros_like(dg_ref)
            db_ref[...] = jnp.zeros_like(db_ref)

        @pl.when(i % tps == 0)
        def _():
            dgt_ref[...] = jnp.zeros_like(dgt_ref)

        yy = y_ref[...]
        g1 = 1.0 + gt_ref[...]
        z = ALPHA * x_ref[...] + g1 * yy
        _, xhat, rstd = _layer_norm(z, 1.0, 0.0)
        dxo_v = dxo_ref[...]
        dg_ref[...] += jnp.sum(dxo_v * xhat, axis=0, keepdims=True)
        db_ref[...] += jnp.sum(dxo_v, axis=0, keepdims=True)
        dxh = dxo_v * g_ref[...]
        dz = rstd * (dxh - jnp.mean(dxh, axis=-1, keepdims=True) - xhat * jnp.mean(dxh * xhat, axis=-1, keepdims=True))
        dz_ref[...] = dz
        dy_ref[...] = (g1 * dz).astype(BF16)
        dgt_ref[...] += jnp.sum(dz * yy, axis=0, keepdims=True)

    rows = pl.BlockSpec((tm, d), lambda i: (i, 0))
    vec = pl.BlockSpec((1, d), lambda i: (0, 0))
    bvec = pl.BlockSpec((None, 1, d), lambda i: (i // tps, 0, 0))
    return pl.pallas_call(
        body, name=name, grid=(t // tm,), in_specs=[rows, rows, rows, bvec, vec],
        out_specs=[rows, rows, vec, vec, bvec],
        out_shape=[_sds((t, d), F32), _sds((t, d), BF16), _sds((1, d), F32), _sds((1, d), F32), _sds((nb, 1, d), F32)],
        compiler_params=_cp(1),
    )(dxo, x_in, y, gate, ln_g)


def _mod_bwd_tail(du, dz_ref, x_ref, sc_ref, dx_ref, dsc_ref, dsh_ref, first):
    @pl.when(first)
    def _():
        dsc_ref[...] = jnp.zeros_like(dsc_ref)
        dsh_ref[...] = jnp.zeros_like(dsh_ref)

    dx_ref[...] = ALPHA * dz_ref[...] + du * (1.0 + sc_ref[...])
    dsc_ref[...] += jnp.sum(du * x_ref[...], axis=0, keepdims=True)
    dsh_ref[...] += jnp.sum(du, axis=0, keepdims=True)


def ffn_bwd(dy, hg, hu, wg, wu, wd, layer, dz, x_in, scale, name):
    t, d = dy.shape
    c, _, _, fc = wg.shape
    nb = scale.shape[0]
    tm = TOKEN_TILE
    tps = (t // nb) // tm

    def body(dy_ref, hg_ref, hu_ref, wg_ref, wu_ref, wd_ref, dz_ref, x_ref, sc_ref,
             dhg_ref, dhu_ref, act_ref, dx_ref, dsc_ref, dsh_ref, acc_ref):
        i = pl.program_id(0)
        cc = pl.program_id(1)

        @pl.when(cc == 0)
        def _():
            acc_ref[...] = jnp.zeros_like(acc_ref)

        hgv = hg_ref[...].astype(F32)
        huv = hu_ref[...].astype(F32)
        da = _dot_nt(dy_ref[...], wd_ref[...])
        sg = jax.nn.sigmoid(hgv)
        sl = hgv * sg
        act_ref[...] = (sl * huv).astype(BF16)
        dhu = (da * sl).astype(BF16)
        dhg = (da * huv * (sg * (1.0 + hgv * (1.0 - sg)))).astype(BF16)
        dhu_ref[...] = dhu
        dhg_ref[...] = dhg
        acc_ref[...] += _dot_nt(dhg, wg_ref[...]) + _dot_nt(dhu, wu_ref[...])

        @pl.when(cc == c - 1)
        def _():
            _mod_bwd_tail(acc_ref[...], dz_ref, x_ref, sc_ref, dx_ref, dsc_ref, dsh_ref, i % tps == 0)

    rows = pl.BlockSpec((tm, d), lambda i, cc: (i, 0))
    bvec = pl.BlockSpec((None, 1, d), lambda i, cc: (i // tps, 0, 0))
    hspec = pl.BlockSpec((None, tm, fc), lambda i, cc: (cc, i, 0))
    wcol = pl.BlockSpec((None, None, d, fc), lambda i, cc: (cc, layer, 0, 0))
    return pl.pallas_call(
        body, name=name, grid=(t // tm, c),
        in_specs=[rows, hspec, hspec, wcol, wcol, pl.BlockSpec((None, None, fc, d), lambda i, cc: (cc, layer, 0, 0)),
                  rows, rows, bvec],
        out_specs=[hspec, hspec, hspec, rows, bvec, bvec],
        out_shape=[_sds((c, t, fc), BF16), _sds((c, t, fc), BF16), _sds((c, t, fc), BF16), _sds((t, d), F32),
                   _sds((nb, 1, d), F32), _sds((nb, 1, d), F32)],
        scratch_shapes=[pltpu.VMEM((tm, d), F32)],
        compiler_params=_cp(2),
    )(dy, hg, hu, wg, wu, wd, dz, x_in, scale)


def linear_nt_mod_bwd(pairs, dz, x_in, scale, name):
    t, d = dz.shape
    nb = scale.shape[0]
    tm = TOKEN_TILE
    tps = (t // nb) // tm
    npairs = len(pairs)

    def body(*refs):
        dh_refs = refs[:npairs]
        w_refs = refs[npairs:2 * npairs]
        dz_ref, x_ref, sc_ref, dx_ref, dsc_ref, dsh_ref = refs[2 * npairs:]
        du = _dot_nt(dh_refs[0][...], w_refs[0][...])
        for kk in range(1, npairs):
            du = du + _dot_nt(dh_refs[kk][...], w_refs[kk][...])
        _mod_bwd_tail(du, dz_ref, x_ref, sc_ref, dx_ref, dsc_ref, dsh_ref, pl.program_id(0) % tps == 0)

    rows = pl.BlockSpec((tm, d), lambda i: (i, 0))
    bvec = pl.BlockSpec((None, 1, d), lambda i: (i // tps, 0, 0))
    in_specs = [pl.BlockSpec((tm, dh.shape[1]), lambda i: (i, 0)) for dh, _ in pairs]
    in_specs += [pl.BlockSpec(w.shape, lambda i: (0, 0)) for _, w in pairs]
    in_specs += [rows, rows, bvec]
    return pl.pallas_call(
        body, name=name, grid=(t // tm,), in_specs=in_specs,
        out_specs=[rows, bvec, bvec],
        out_shape=[_sds((t, d), F32), _sds((nb, 1, d), F32), _sds((nb, 1, d), F32)],
        compiler_params=_cp(1),
    )(*[dh for dh, _ in pairs], *[w for _, w in pairs], dz, x_in, scale)


def linear_nt_delta(dy, w_o, o, head_sel, name):
    t, d = dy.shape
    hdv = w_o.shape[0]
    tm = TOKEN_TILE

    def body(dy_ref, w_ref, o_ref, sel_ref, do_ref, dl_ref):
        do = _dot_nt(dy_ref[...], w_ref[...])
        do_ref[...] = do.astype(BF16)
        dl_ref[...] = _dot_f32(do * o_ref[...].astype(F32), sel_ref[...])

    return pl.pallas_call(
        body, name=name, grid=(t // tm,),
        in_specs=[pl.BlockSpec((tm, d), lambda i: (i, 0)), pl.BlockSpec((hdv, d), lambda i: (0, 0)),
                  pl.BlockSpec((tm, hdv), lambda i: (i, 0)), pl.BlockSpec(head_sel.shape, lambda i: (0, 0))],
        out_specs=[pl.BlockSpec((tm, hdv), lambda i: (i, 0)), pl.BlockSpec((tm, 128), lambda i: (i, 0))],
        out_shape=[_sds((t, hdv), BF16), _sds((t, 128), F32)], compiler_params=_cp(1),
    )(dy, w_o, o, head_sel)


def attn_bwd(q, k, v, do, lse_r, delta_r, fq_r, fk_c, scale, name):
    b, h, s, dk = q.shape
    dv = v.shape[-1]
    tk = ATTN_TILE
    nk = s // tk
    has_bias = fq_r is not None

    def body(*refs):
        if has_bias:
            (q_ref, k_ref, v_ref, do_ref, lse_ref, dl_ref, fq_ref, fk_ref,
             dq_ref, dk_ref, dv_ref, dfk_ref, dfq_ref) = refs
        else:
            q_ref, k_ref, v_ref, do_ref, lse_ref, dl_ref, dq_ref, dk_ref, dv_ref = refs
        j = pl.program_id(2)
        kb = k_ref[...]
        vb = v_ref[...]

        @pl.when(j == 0)
        def _():
            dq_ref[...] = jnp.zeros_like(dq_ref)
            if has_bias:
                dfq_ref[...] = jnp.zeros_like(dfq_ref)

        def block(i, carry, masked):
            dk_acc, dv_acc, dfk_acc = carry
            start = pl.multiple_of(i * tk, tk)
            qb = q_ref[pl.ds(start, tk), :]
            dob = do_ref[pl.ds(start, tk), :]
            st = _dot_nt(kb, qb) * scale
            if has_bias:
                st = st + fq_ref[i] - fk_ref[...]
            if masked:
                keep = lax.broadcasted_iota(jnp.int32, (tk, tk), 1) >= lax.broadcasted_iota(jnp.int32, (tk, tk), 0)
                st = jnp.where(keep, st, -1e30)
            pt = jnp.exp(st - lse_ref[i])
            dv_acc = dv_acc + _dot(pt.astype(BF16), dob)
            dpt = _dot_nt(vb, dob)
            dst = pt * (dpt - dl_ref[i])
            if has_bias:
                dfk_acc = dfk_acc - jnp.sum(dst, axis=1, keepdims=True)
                dfq_ref[i] += jnp.sum(dst, axis=0, keepdims=True)
            dsb = (dst * scale).astype(BF16)
            dk_acc = dk_acc + _dot(dsb, qb)
            dq_ref[pl.ds(start, tk), :] += _dot_tn(dsb, kb)
            return dk_acc, dv_acc, dfk_acc

        init = (jnp.zeros((tk, dk), F32), jnp.zeros((tk, dv), F32), jnp.zeros((tk, 1), F32))
        carry = block(j, init, True)
        dk_acc, dv_acc, dfk_acc = lax.fori_loop(j + 1, nk, lambda i, c: block(i, c, False), carry)
        dk_ref[...] = dk_acc.astype(BF16)
        dv_ref[...] = dv_acc.astype(BF16)
        if has_bias:
            dfk_ref[...] = dfk_acc

    def full(n):
        return pl.BlockSpec((None, None, s, n), lambda bb, hh, j: (bb, hh, 0, 0))

    def kspec(n):
        return pl.BlockSpec((None, None, tk, n), lambda bb, hh, j: (bb, hh, j, 0))

    rowv = pl.BlockSpec((None, None, nk, 1, tk), lambda bb, hh, j: (bb, hh, 0, 0, 0))
    in_specs = [full(dk), kspec(dk), kspec(dv), full(dv), rowv, rowv]
    args = [q, k, v, do, lse_r, delta_r]
    out_specs = [full(dk), kspec(dk), kspec(dv)]
    out_shape = [_sds((b, h, s, dk), F32), _sds((b, h, s, dk), BF16), _sds((b, h, s, dv), BF16)]
    if has_bias:
        in_specs += [rowv, kspec(1)]
        args += [fq_r, fk_c]
        out_specs += [kspec(1), rowv]
        out_shape += [_sds((b, h, s, 1), F32), _sds((b, h, nk, 1, tk), F32)]
    return pl.pallas_call(
        body, name=name, grid=(b, h, nk), in_specs=in_specs, out_specs=out_specs, out_shape=out_shape,
        compiler_params=_cp(3),
    )(*args)


def mla_mid_bwd(dq, dkn, dv, dkr_heads, h, g_q, g_kv, w_uq, w_uk, w_uv, cos8, sin8, cos64, sin64s, swap64, head_sum, name):
    t = h.shape[0]
    tm = TOKEN_TILE
    hq = MLA_HEADS * MLA_NOPE
    hr = MLA_HEADS * MLA_ROPE // 2
    nq = w_uq.shape[1]

    def body(dq_ref, dkn_ref, dv_ref, dkr_ref, h_ref, gq_ref, gkv_ref, wuq_ref, wuk_ref, wuv_ref,
             c8_ref, s8_ref, c64_ref, s64_ref, sw_ref, hs_ref, dh_ref, dqp_ref, dgq_ref, dgkv_ref):
        @pl.when(pl.program_id(0) == 0)
        def _():
            dgq_ref[...] = jnp.zeros_like(dgq_ref)
            dgkv_ref[...] = jnp.zeros_like(dgkv_ref)

        dqv = dq_ref[...].astype(F32)
        o1 = dqv[:, hq:hq + hr]
        o2 = dqv[:, hq + hr:]
        cs = c8_ref[...]
        sn = s8_ref[...]
        dqp = jnp.concatenate([dqv[:, :hq], o1 * cs + o2 * sn, o2 * cs - o1 * sn], axis=1).astype(BF16)
        dqp_ref[...] = dqp
        dcq = _dot_nt(dqp, wuq_ref[...])
        dckv = _dot_nt(dkn_ref[...], wuk_ref[...]) + _dot_nt(dv_ref[...], wuv_ref[...])
        hh = h_ref[...]

        def rms_bwd(hpart, g, dc, dg_ref):
            hhat, rstd = _rms(hpart, None)
            dg_ref[...] += jnp.sum(dc * hhat, axis=0, keepdims=True)
            dcg = dc * g
            return rstd * (dcg - hhat * jnp.mean(dcg * hhat, axis=-1, keepdims=True))

        dhq = rms_bwd(hh[:, :MLA_QR], gq_ref[...], dcq, dgq_ref)
        dhkv = rms_bwd(hh[:, MLA_QR:MLA_QR + MLA_KVR], gkv_ref[...], dckv, dgkv_ref)
        dkr = _dot(dkr_ref[...], hs_ref[...])
        dkr_pre = dkr * c64_ref[...] + _dot_f32(dkr * s64_ref[...], sw_ref[...])
        dh_ref[...] = jnp.concatenate([dhq, dhkv, dkr_pre], axis=1).astype(BF16)

    def rows(n):
        return pl.BlockSpec((tm, n), lambda i: (i, 0))

    def whole(a):
        return pl.BlockSpec(a.shape, lambda i: (0,) * a.ndim)

    return pl.pallas_call(
        body, name=name, grid=(t // tm,),
        in_specs=[rows(nq), rows(hq), rows(hq), rows(MLA_HEADS * MLA_ROPE), rows(h.shape[1]), whole(g_q), whole(g_kv),
                  whole(w_uq), whole(w_uk), whole(w_uv), rows(hr), rows(hr), rows(MLA_ROPE), rows(MLA_ROPE),
                  whole(swap64), whole(head_sum)],
        out_specs=[rows(h.shape[1]), rows(nq), pl.BlockSpec((1, MLA_QR), lambda i: (0, 0)),
                   pl.BlockSpec((1, MLA_KVR), lambda i: (0, 0))],
        out_shape=[_sds((t, h.shape[1]), BF16), _sds((t, nq), BF16), _sds((1, MLA_QR), F32), _sds((1, MLA_KVR), F32)],
        compiler_params=_cp(1),
    )(dq, dkn, dv, dkr_heads, h, g_q, g_kv, w_uq, w_uk, w_uv, cos8, sin8, cos64, sin64s, swap64, head_sum)


def fox_gate_bwd(dcum, hf, b_f, triu, n_batch, name):
    t, n = hf.shape
    blk = triu.shape[0]
    nb = (t // n_batch) // blk

    def body(dc_ref, hf_ref, b_ref, tri_ref, o_ref, db_ref, carry_ref):
        @pl.when(pl.program_id(1) == 0)
        def _():
            carry_ref[...] = jnp.zeros_like(carry_ref)

        @pl.when((pl.program_id(0) == 0) & (pl.program_id(1) == 0))
        def _():
            db_ref[...] = jnp.zeros_like(db_ref)

        rc = _dot_f32(tri_ref[...], dc_ref[...]) + carry_ref[...]
        carry_ref[...] = rc[0:1, :]
        dhf = rc * jax.nn.sigmoid(-(hf_ref[...] + b_ref[...]))
        o_ref[...] = dhf.astype(BF16)
        db_ref[...] += jnp.sum(dhf, axis=0, keepdims=True)

    rev = pl.BlockSpec((blk, n), lambda bb, i: (bb * nb + nb - 1 - i, 0))
    return pl.pallas_call(
        body, name=name, grid=(n_batch, nb),
        in_specs=[rev, rev, pl.BlockSpec((1, n), lambda bb, i: (0, 0)), pl.BlockSpec((blk, blk), lambda bb, i: (0, 0))],
        out_specs=[rev, pl.BlockSpec((1, n), lambda bb, i: (0, 0))],
        out_shape=[_sds((t, n), BF16), _sds((1, n), F32)], scratch_shapes=[pltpu.VMEM((1, n), F32)],
        compiler_params=_cp(2),
    )(dcum, hf, b_f, triu)


def wgrad(a, bm, name, slot=None, bt=512):
    ca, t, kd = a.shape
    cb, _, nd = bm.shape
    c = max(ca, cb)
    bn = nd
    if nd > 1024 and nd % 1024 == 0:
        bn = 1024

    def body(*refs):
        a_ref, b_ref, o_ref = refs[0], refs[1], refs[-1]

        @pl.when(pl.program_id(2) == 0)
        def _():
            o_ref[...] = jnp.zeros_like(o_ref)

        o_ref[...] += _dot_tn(a_ref[...], b_ref[...])

    in_specs = [pl.BlockSpec((None, bt, kd), lambda cc, n, tt: (cc if ca > 1 else 0, tt, 0)),
                pl.BlockSpec((None, bt, bn), lambda cc, n, tt: (cc if cb > 1 else 0, tt, n))]
    args = [a, bm]
    aliases = {}
    if slot is None:
        out_spec = pl.BlockSpec((None, kd, bn), lambda cc, n, tt: (cc, 0, n))
        out_shape = _sds((c, kd, nd), F32)
    else:
        layer, n_layers, buf = slot
        out_spec = pl.BlockSpec((None, None, kd, bn), lambda cc, n, tt: (cc, layer, 0, n))
        out_shape = _sds((c, n_layers, kd, nd), F32)
        if buf is not None:
            in_specs.append(pl.BlockSpec(memory_space=pl.ANY))
            args.append(buf)
            aliases = {2: 0}
    return pl.pallas_call(
        body, name=name, grid=(c, nd // bn, t // bt), in_specs=in_specs, out_specs=out_spec, out_shape=out_shape,
        input_output_aliases=aliases, compiler_params=_cp(3),
    )(*args)


def ada_mod_part(c_all, ada_w, name):
    nl, d, n = ada_w.shape
    rows = c_all.shape[0]
    tn = 512

    def body(c_ref, w_ref, o_ref):
        cv = c_ref[...]
        act = (cv * jax.nn.sigmoid(cv)).astype(BF16)
        o_ref[...] = _dot(act, w_ref[...].astype(BF16))

    return pl.pallas_call(
        body, name=name, grid=(nl, n // tn),
        in_specs=[pl.BlockSpec((rows, d), lambda l, j: (0, 0)), pl.BlockSpec((None, d, tn), lambda l, j: (l, 0, j))],
        out_specs=pl.BlockSpec((None, rows, tn), lambda l, j: (l, 0, j)),
        out_shape=_sds((nl, rows, n), F32), compiler_params=_cp(2),
    )(c_all, ada_w)


def ada_grad(c_all_t, dmod, name):
    nl, rows, n = dmod.shape
    d = c_all_t.shape[0]
    tn = 512

    def body(c_ref, dm_ref, o_ref):
        cv = c_ref[...]
        act = (cv * jax.nn.sigmoid(cv)).astype(BF16)
        o_ref[...] = _dot(act, dm_ref[...].astype(BF16))

    return pl.pallas_call(
        body, name=name, grid=(nl, n // tn),
        in_specs=[pl.BlockSpec((d, rows), lambda l, j: (0, 0)), pl.BlockSpec((None, rows, tn), lambda l, j: (l, 0, j))],
        out_specs=pl.BlockSpec((None, d, tn), lambda l, j: (l, 0, j)),
        out_shape=_sds((nl, d, n), F32), compiler_params=_cp(2),
    )(c_all_t, dmod)


def sum_leading(a, name):
    g, r, n = a.shape

    def body(a_ref, o_ref):
        acc = a_ref[0]
        for kk in range(1, g):
            acc = acc + a_ref[kk]
        o_ref[...] = acc

    return pl.pallas_call(
        body, name=name, grid=(1,), in_specs=[pl.BlockSpec((g, r, n), lambda i: (0, 0, 0))],
        out_specs=pl.BlockSpec((r, n), lambda i: (0, 0)), out_shape=_sds((r, n), F32), compiler_params=_cp(1),
    )(a)


def adamw(w, g, m, v, name):
    r, n = w.shape
    br = r
    for cand in (512, 256, 128, 64, 32, 16, 8):
        if r % cand == 0 and r > cand and cand * n * 4 <= ADAMW_BLOCK_BYTES:
            br = cand
            break
    c1 = 1.0 - ADAM_B1 ** ADAM_STEP
    c2 = 1.0 - ADAM_B2 ** ADAM_STEP

    def body(w_ref, g_ref, m_ref, v_ref, d_ref, mo_ref, vo_ref):
        gv = g_ref[...]
        mn = ADAM_B1 * m_ref[...] + (1.0 - ADAM_B1) * gv
        vn = ADAM_B2 * v_ref[...] + (1.0 - ADAM_B2) * (gv * gv)
        mo_ref[...] = mn
        vo_ref[...] = vn
        d_ref[...] = -ADAM_LR * ((mn / c1) / (jnp.sqrt(vn / c2) + ADAM_EPS) + ADAM_WD * w_ref[...])

    spec = pl.BlockSpec((br, n), lambda i: (i, 0))
    return pl.pallas_call(
        body, name=name, grid=(r // br,), in_specs=[spec] * 4, out_specs=[spec] * 3,
        out_shape=[_sds((r, n), F32)] * 3, compiler_params=_cp(1),
    )(w, g, m, v)


def _place():
    return lax.axis_index("x"), lax.axis_index("y"), lax.axis_index("c")


def all_gather8(x_blk, name):
    m_per, n = x_blk.shape

    def body(x_ref, out_ref, send_sems, recv_sems, local_sem):
        x, y, c = _place()
        me, sibling = (x, y, c), (x, y, 1 - c)
        chips = [(1 - x, y), (x, 1 - y), (1 - x, 1 - y)]

        def rows(px, py, pc):
            return out_ref.at[pl.ds((4 * px + 2 * py + pc) * m_per, m_per), :]

        def copy(k, block, to, src=None):
            return pltpu.make_async_remote_copy(
                src_ref=rows(*block) if src is None else src, dst_ref=rows(*block),
                send_sem=send_sems.at[k], recv_sem=recv_sems.at[k], device_id=to, device_id_type=MESH)

        mine = pltpu.make_async_copy(x_ref, rows(*me), local_sem)
        mine.start()
        first = [copy(0, me, sibling, src=x_ref)]
        first += [copy(1 + j, me, (*chip, c), src=x_ref) for j, chip in enumerate(chips)]
        for cp in first:
            cp.start()
        passed = [copy(4 + j, (*chip, c), sibling) for j, chip in enumerate(chips)]
        for j, chip in enumerate(chips):
            copy(1 + j, (*chip, c), me).wait_recv()
            passed[j].start()
        copy(0, sibling, me).wait_recv()
        for j, chip in enumerate(chips):
            copy(4 + j, (*chip, 1 - c), me).wait_recv()
        for cp in first + passed:
            cp.wait_send()
        mine.wait()

    return pl.pallas_call(
        body, name=name, out_shape=_sds((8 * m_per, n), x_blk.dtype),
        in_specs=[pl.BlockSpec(memory_space=pltpu.VMEM)], out_specs=pl.BlockSpec(memory_space=pltpu.VMEM),
        scratch_shapes=[pltpu.SemaphoreType.DMA((7,)), pltpu.SemaphoreType.DMA((7,)), pltpu.SemaphoreType.DMA],
        compiler_params=pltpu.CompilerParams(vmem_limit_bytes=VMEM_LIMIT),
    )(x_blk)


def all_gather_chips(shards, name):
    nt = len(shards)

    def body(*refs):
        w_refs, out_refs = refs[:nt], refs[nt:2 * nt]
        send_sems, recv_sems, own_send, own_recv = refs[2 * nt:]
        x, y, c = _place()
        sibling = (x, y, 1 - c)
        chips = [(1 - x, y), (x, 1 - y), (1 - x, 1 - y)]

        def copy(t, k, block, to, src=None):
            px, py, hh = block
            dst = out_refs[t].at[2 * px + py, hh]
            return pltpu.make_async_remote_copy(
                src_ref=dst if src is None else src, dst_ref=dst,
                send_sem=send_sems.at[6 * t + k], recv_sem=recv_sems.at[6 * t + k], device_id=to, device_id_type=MESH)

        def own(t):
            return pltpu.make_async_remote_copy(
                src_ref=w_refs[t], dst_ref=out_refs[t].at[2 * x + y], send_sem=own_send.at[t], recv_sem=own_recv.at[t],
                device_id=sibling, device_id_type=MESH)

        mine = [own(t) for t in range(nt)]
        for cp in mine:
            cp.start()
        first = [copy(t, j, (x, y, c), (*chip, c), src=w_refs[t].at[c]) for t in range(nt) for j, chip in enumerate(chips)]
        for cp in first:
            cp.start()
        passed = []
        for t in range(nt):
            for j, chip in enumerate(chips):
                copy(t, j, (*chip, c), (x, y, c)).wait_recv()
                fwd = copy(t, 3 + j, (*chip, c), sibling)
                fwd.start()
                passed.append(fwd)
        for t in range(nt):
            for j, chip in enumerate(chips):
                copy(t, 3 + j, (*chip, 1 - c), (x, y, c)).wait_recv()
        for cp in first + passed:
            cp.wait_send()
        for cp in mine:
            cp.wait()

    hbm = pl.BlockSpec(memory_space=pl.ANY)
    return pl.pallas_call(
        body, name=name, out_shape=[_sds((N_CHIPS, *w.shape), w.dtype) for w in shards],
        in_specs=[hbm] * nt, out_specs=[hbm] * nt,
        scratch_shapes=[pltpu.SemaphoreType.DMA((6 * nt,)), pltpu.SemaphoreType.DMA((6 * nt,)),
                        pltpu.SemaphoreType.DMA((nt,)), pltpu.SemaphoreType.DMA((nt,))],
    )(*shards)


def sibling_swap_halves(grads, name):
    nt = len(grads)

    def body(*refs):
        g_refs, a_refs = refs[:nt], refs[nt:2 * nt]
        send_sems, recv_sems = refs[2 * nt:]
        x, y, c = _place()
        cps = [pltpu.make_async_remote_copy(
            src_ref=g_refs[t].at[j, 1 - c], dst_ref=a_refs[t].at[j], send_sem=send_sems.at[N_CHIPS * t + j],
            recv_sem=recv_sems.at[N_CHIPS * t + j], device_id=(x, y, 1 - c), device_id_type=MESH)
            for t in range(nt) for j in range(N_CHIPS)]
        for cp in cps:
            cp.start()
        for cp in cps:
            cp.wait()

    hbm = pl.BlockSpec(memory_space=pl.ANY)
    return pl.pallas_call(
        body, name=name, out_shape=[_sds((N_CHIPS, *g.shape[2:]), g.dtype) for g in grads],
        in_specs=[hbm] * nt, out_specs=[hbm] * nt,
        scratch_shapes=[pltpu.SemaphoreType.DMA((N_CHIPS * nt,)), pltpu.SemaphoreType.DMA((N_CHIPS * nt,))],
    )(*grads)


def _row_block(r, n, itemsize):
    best = None
    for br in range(16, r + 1, 16):
        if r % br == 0 and br * n * itemsize <= COMM_BLOCK_BYTES:
            best = br
    assert best is not None, (r, n)
    return best


def add_own_half(g, recv, core, name):
    nch, _, r, n = g.shape
    br = _row_block(r, n, 4)

    def body(c_ref, g_ref, a_ref, o_ref):
        o_ref[...] = (g_ref[...] + a_ref[...]).astype(BF16)

    return pl.pallas_call(
        body, name=name,
        grid_spec=pltpu.PrefetchScalarGridSpec(
            num_scalar_prefetch=1, grid=(nch, r // br),
            in_specs=[pl.BlockSpec((None, None, br, n), lambda j, rr, cref: (j, cref[0], rr, 0)),
                      pl.BlockSpec((None, br, n), lambda j, rr, cref: (j, rr, 0))],
            out_specs=pl.BlockSpec((None, br, n), lambda j, rr, cref: (j, rr, 0))),
        out_shape=_sds((nch, r, n), BF16), compiler_params=_cp(2),
    )(core, g, recv)


def chip_exchange(sums, name):
    nt = len(sums)

    def body(*refs):
        s_refs, b_refs = refs[:nt], refs[nt:2 * nt]
        send_sems, recv_sems = refs[2 * nt:]
        x, y, c = _place()
        k = 2 * x + y
        chips = [(1 - x, y), (x, 1 - y), (1 - x, 1 - y)]

        def copy(t, j, src_idx, dst_idx):
            px, py = chips[j]
            return pltpu.make_async_remote_copy(
                src_ref=s_refs[t].at[src_idx], dst_ref=b_refs[t].at[dst_idx], send_sem=send_sems.at[3 * t + j],
                recv_sem=recv_sems.at[3 * t + j], device_id=(px, py, c), device_id_type=MESH)

        cps = [copy(t, j, 2 * chips[j][0] + chips[j][1], k) for t in range(nt) for j in range(3)]
        for cp in cps:
            cp.start()
        for t in range(nt):
            for j in range(3):
                copy(t, j, k, 2 * chips[j][0] + chips[j][1]).wait_recv()
        for cp in cps:
            cp.wait_send()

    hbm = pl.BlockSpec(memory_space=pl.ANY)
    return pl.pallas_call(
        body, name=name, out_shape=[_sds(sm.shape, sm.dtype) for sm in sums],
        in_specs=[hbm] * nt, out_specs=[hbm] * nt,
        scratch_shapes=[pltpu.SemaphoreType.DMA((3 * nt,)), pltpu.SemaphoreType.DMA((3 * nt,))],
    )(*sums)


def sum_pieces(own, recv, place, name):
    nch, r, n = own.shape
    br = _row_block(r, n, 4 * nch)

    def body(p_ref, o_ref, r1_ref, r2_ref, r3_ref, out_ref):
        out_ref[...] = ((o_ref[...].astype(F32) + r1_ref[...].astype(F32)) + r2_ref[...].astype(F32)) + r3_ref[...].astype(F32)

    def piece(step):
        return pl.BlockSpec((None, br, n), lambda i, pref: ((pref[1] + step) % nch, i, 0))

    return pl.pallas_call(
        body, name=name,
        grid_spec=pltpu.PrefetchScalarGridSpec(
            num_scalar_prefetch=1, grid=(r // br,), in_specs=[piece(0), piece(1), piece(2), piece(3)],
            out_specs=pl.BlockSpec((None, br, n), lambda i, pref: (pref[0], i, 0))),
        out_shape=_sds((2, r, n), F32), compiler_params=_cp(1),
    )(place, own, recv, recv, recv)


def sibling_join_halves(halves, name):
    nt = len(halves)

    def body(*refs):
        o_refs = refs[nt:2 * nt]
        send_sems, recv_sems = refs[2 * nt:]
        x, y, c = _place()

        def copy(t, hh):
            return pltpu.make_async_remote_copy(
                src_ref=o_refs[t].at[hh], dst_ref=o_refs[t].at[hh], send_sem=send_sems.at[t], recv_sem=recv_sems.at[t],
                device_id=(x, y, 1 - c), device_id_type=MESH)

        cps = [copy(t, c) for t in range(nt)]
        for cp in cps:
            cp.start()
        for t in range(nt):
            copy(t, 1 - c).wait_recv()
        for cp in cps:
            cp.wait_send()

    hbm = pl.BlockSpec(memory_space=pl.ANY)
    return pl.pallas_call(
        body, name=name, out_shape=[_sds(f.shape, f.dtype) for f in halves],
        in_specs=[hbm] * nt, out_specs=[hbm] * nt, input_output_aliases={t: t for t in range(nt)},
        scratch_shapes=[pltpu.SemaphoreType.DMA((nt,)), pltpu.SemaphoreType.DMA((nt,))],
    )(*halves)


_SHARD_KIND = {"mla_w_in": "rows", "mla_w_uq": "cols", "mla_w_uk": "cols", "mla_w_uv": "cols", "mla_w_o": "rows",
               "fox_w_in": "cols", "fox_w_o": "rows", "ffn_w_gate": "chunk", "ffn_w_up": "chunk", "ffn_w_down": "chunk"}
_PACKED = tuple(_SHARD_KIND)


def _halves(shard):
    if shard.ndim == 3 and shard.shape[0] == 2:
        return shard
    r, n = shard.shape[-2:]
    return shard.reshape(2, r // 2, n)


def _cols_to_full(g):
    return jnp.transpose(g, (1, 0, 2)).reshape(g.shape[1], -1)


def _full_to_cols(w):
    k, n4 = w.shape
    return jnp.transpose(w.reshape(k, N_CHIPS, n4 // N_CHIPS), (1, 0, 2))


def _uq_perm():
    per = MLA_NOPE + MLA_ROPE
    half = MLA_ROPE // 2
    nope = [h * per + d for h in range(MLA_HEADS) for d in range(MLA_NOPE)]
    r1 = [h * per + MLA_NOPE + r for h in range(MLA_HEADS) for r in range(half)]
    r2 = [h * per + MLA_NOPE + half + r for h in range(MLA_HEADS) for r in range(half)]
    perm = np.array(nope + r1 + r2, dtype=np.int32)
    return perm, np.argsort(perm).astype(np.int32)


def _heads(a, nb, h):
    t = a.shape[0]
    return jnp.transpose(a.reshape(nb, t // nb, h, -1), (0, 2, 1, 3))


def _unheads(a):
    nb, h, s, dd = a.shape
    return jnp.transpose(a, (0, 2, 1, 3)).reshape(nb * s, h * dd)


def _row_tiles(a):
    nb, h, s, _ = a.shape
    return a.reshape(nb, h, s // ATTN_TILE, 1, ATTN_TILE)


def _local_step(x, positions, target, mods, wts, ln_g, ln_b, mla_g_q, mla_g_kv, fox_b_f):
    nb, s, d = x.shape
    t = nb * s
    x0 = x.reshape(t, d)
    tgt = target.reshape(t, d)
    perm, inv_perm = _uq_perm()

    half = MLA_ROPE // 2
    inv_freq = ROPE_THETA ** (-jnp.arange(half, dtype=F32) / half)
    ang = positions.astype(F32).reshape(t, 1) * inv_freq
    cos, sin = jnp.cos(ang), jnp.sin(ang)
    cos8, sin8 = jnp.tile(cos, (1, MLA_HEADS)), jnp.tile(sin, (1, MLA_HEADS))
    cos64 = jnp.concatenate([cos, cos], axis=1)
    sin64s = jnp.concatenate([-sin, sin], axis=1)
    swap64 = jnp.asarray(np.roll(np.eye(MLA_ROPE, dtype=np.float32), half, axis=1))
    head_sum = jnp.asarray(np.tile(np.eye(MLA_ROPE, dtype=np.float32), (MLA_HEADS, 1)), dtype=BF16)
    sel_mla = jnp.asarray(np.pad(np.kron(np.eye(MLA_HEADS, dtype=np.float32), np.ones((MLA_V, 1), np.float32)),
                                 ((0, 0), (0, 128 - MLA_HEADS))))
    sel_fox = jnp.asarray(np.pad(np.kron(np.eye(FOX_HEADS, dtype=np.float32), np.ones((FOX_HD, 1), np.float32)),
                                 ((0, 0), (0, 128 - FOX_HEADS))))
    tri = jnp.asarray(np.tril(np.ones((128, 128), np.float32)))
    triu = jnp.asarray(np.triu(np.ones((128, 128), np.float32)))

    def vec(a):
        return a.reshape(1, -1)

    w_uq_p = wts["mla_w_uq"][:, perm]
    fox_w_qkv = wts["fox_w_in"][:, :3 * d]
    fox_w_f = jnp.pad(wts["fox_w_in"][:, 3 * d:], ((0, 0), (0, 128 - FOX_HEADS)))
    b_f_pad = jnp.pad(fox_b_f.reshape(1, -1), ((0, 0), (0, 128 - FOX_HEADS)))

    sh_a, sc_a, gt_a, sh_f, sc_f, gt_f = mods[0]
    h_in, u_m = mod_linear(x0, sh_a, sc_a, wts["mla_w_in"], F32, "mla_in", emit_u=True)
    q_m, kn_m, v_m, kr_m, cq_m, ckv_m = mla_mid_fwd(
        h_in, vec(mla_g_q), vec(mla_g_kv), w_uq_p, wts["mla_w_uk"], wts["mla_w_uv"], cos8, sin8, cos64, sin64s, swap64,
        "mla_mid")
    hq = MLA_HEADS * MLA_NOPE
    hr = MLA_HEADS * half
    q_cat = jnp.concatenate([q_m[:, :hq].reshape(nb, s, MLA_HEADS, MLA_NOPE),
                             q_m[:, hq:hq + hr].reshape(nb, s, MLA_HEADS, half),
                             q_m[:, hq + hr:].reshape(nb, s, MLA_HEADS, half)], axis=-1)
    q_cat = jnp.transpose(q_cat, (0, 2, 1, 3))
    k_cat = jnp.concatenate([kn_m.reshape(nb, s, MLA_HEADS, MLA_NOPE),
                             jnp.broadcast_to(kr_m.reshape(nb, s, 1, MLA_ROPE), (nb, s, MLA_HEADS, MLA_ROPE))], axis=-1)
    k_cat = jnp.transpose(k_cat, (0, 2, 1, 3))
    v_h = _heads(v_m, nb, MLA_HEADS)
    scale_m = (MLA_NOPE + MLA_ROPE) ** -0.5
    o_m, lse_m = attn_fwd(q_cat, k_cat, v_h, None, None, scale_m, "mla_attn")
    o_m2 = _unheads(o_m)
    y0, x1 = linear_resid_ln(o_m2, wts["mla_w_o"], x0, gt_a, vec(ln_g[0, 0]), vec(ln_b[0, 0]), "mla_out")
    u_f0, hg0, hu0, y1, x2 = ffn_fwd(x1, sh_f, sc_f, gt_f, wts["ffn_w_gate"], wts["ffn_w_up"], wts["ffn_w_down"], 0,
                                     vec(ln_g[0, 1]), vec(ln_b[0, 1]), "ffn0")
    sh_a1, sc_a1, gt_a1, sh_f1, sc_f1, gt_f1 = mods[1]
    qkv, u_x = mod_linear(x2, sh_a1, sc_a1, fox_w_qkv, BF16, "fox_qkv", tn=1024, emit_u=True)
    hf = mod_linear(x2, sh_a1, sc_a1, fox_w_f, F32, "fox_f")
    cum = fox_gate_fwd(hf, b_f_pad, tri, nb, "fox_gate")
    cum_h = jnp.transpose(cum[:, :FOX_HEADS].reshape(nb, s, FOX_HEADS), (0, 2, 1))[..., None]
    q_x = _heads(qkv[:, :d], nb, FOX_HEADS)
    k_x = _heads(qkv[:, d:2 * d], nb, FOX_HEADS)
    v_x = _heads(qkv[:, 2 * d:], nb, FOX_HEADS)
    scale_x = FOX_HD ** -0.5
    o_x, lse_x = attn_fwd(q_x, k_x, v_x, cum_h, _row_tiles(cum_h), scale_x, "fox_attn")
    o_x2 = _unheads(o_x)
    y2, x3 = linear_resid_ln(o_x2, wts["fox_w_o"], x2, gt_a1, vec(ln_g[1, 0]), vec(ln_b[1, 0]), "fox_out")
    u_f1, hg1, hu1, y3, x4 = ffn_fwd(x3, sh_f1, sc_f1, gt_f1, wts["ffn_w_gate"], wts["ffn_w_up"], wts["ffn_w_down"], 1,
                                     vec(ln_g[1, 1]), vec(ln_b[1, 1]), "ffn1")
    dx4, sq_err = loss_grad(x4, tgt, "loss")
    loss_part = 0.5 * jnp.sum(sq_err) / d

    grads = {}
    dz3, dy3, dg11, db11, dgt_f1 = ln_bwd(dx4, x3, y3, gt_f1, vec(ln_g[1, 1]), "ffn1_ln_bwd")
    dhg1, dhu1, act1, dx3, dsc_f1, dsh_f1 = ffn_bwd(dy3, hg1, hu1, wts["ffn_w_gate"], wts["ffn_w_up"],
                                                    wts["ffn_w_down"], 1, dz3, x3, sc_f1, "ffn1_bwd")
    dwg1 = wgrad(u_f1[None], dhg1, "ffn1_dwg", slot=(1, DEPTH, None))
    dwu1 = wgrad(u_f1[None], dhu1, "ffn1_dwu", slot=(1, DEPTH, None))
    dwd1 = wgrad(act1, dy3[None], "ffn1_dwd", slot=(1, DEPTH, None))
    dz2, dy2, dg10, db10, dgt_a1 = ln_bwd(dx3, x2, y2, gt_a1, vec(ln_g[1, 0]), "fox_ln_bwd")
    do_x, delta_x = linear_nt_delta(dy2, wts["fox_w_o"], o_x2, sel_fox, "fox_out_bwd")
    grads["fox_w_o"] = wgrad(o_x2[None], dy2[None], "fox_dwo")[0]
    delta_xh = jnp.transpose(delta_x[:, :FOX_HEADS].reshape(nb, s, FOX_HEADS), (0, 2, 1))[..., None]
    dq_x, dk_x, dv_x, dfk_x, dfq_x = attn_bwd(q_x, k_x, v_x, _heads(do_x, nb, FOX_HEADS), _row_tiles(lse_x),
                                              _row_tiles(delta_xh), _row_tiles(cum_h), cum_h, scale_x, "fox_attn_bwd")
    dcum_h = dfk_x[..., 0] + dfq_x.reshape(nb, FOX_HEADS, s)
    dcum = jnp.pad(jnp.transpose(dcum_h, (0, 2, 1)).reshape(t, FOX_HEADS), ((0, 0), (0, 128 - FOX_HEADS)))
    dhf, dbf = fox_gate_bwd(dcum, hf, b_f_pad, triu, nb, "fox_gate_bwd")
    dqkv = jnp.concatenate([_unheads(dq_x).astype(BF16), _unheads(dk_x), _unheads(dv_x)], axis=1)
    dx2, dsc_a1, dsh_a1 = linear_nt_mod_bwd([(dqkv, fox_w_qkv), (dhf, fox_w_f)], dz2, x2, sc_a1, "fox_in_bwd")
    dw_qkv = wgrad(u_x[None], dqkv[None], "fox_dwqkv")[0]
    dw_f = wgrad(u_x[None], dhf[None], "fox_dwf")[0]
    grads["fox_w_in"] = jnp.concatenate([dw_qkv, dw_f[:, :FOX_HEADS]], axis=1)
    dz1, dy1, dg01, db01, dgt_f0 = ln_bwd(dx2, x1, y1, gt_f, vec(ln_g[0, 1]), "ffn0_ln_bwd")
    dhg0, dhu0, act0, dx1, dsc_f0, dsh_f0 = ffn_bwd(dy1, hg0, hu0, wts["ffn_w_gate"], wts["ffn_w_up"],
                                                    wts["ffn_w_down"], 0, dz1, x1, sc_f, "ffn0_bwd")
    grads["ffn_w_gate"] = wgrad(u_f0[None], dhg0, "ffn0_dwg", slot=(0, DEPTH, dwg1))
    grads["ffn_w_up"] = wgrad(u_f0[None], dhu0, "ffn0_dwu", slot=(0, DEPTH, dwu1))
    grads["ffn_w_down"] = wgrad(act0, dy1[None], "ffn0_dwd", slot=(0, DEPTH, dwd1))
    dz0, dy0, dg00, db00, dgt_a0 = ln_bwd(dx1, x0, y0, gt_a, vec(ln_g[0, 0]), "mla_ln_bwd")
    do_m, delta_m = linear_nt_delta(dy0, wts["mla_w_o"], o_m2, sel_mla, "mla_out_bwd")
    grads["mla_w_o"] = wgrad(o_m2[None], dy0[None], "mla_dwo")[0]
    delta_mh = jnp.transpose(delta_m[:, :MLA_HEADS].reshape(nb, s, MLA_HEADS), (0, 2, 1))[..., None]
    dq_c, dk_c, dv_c = attn_bwd(q_cat, k_cat, v_h, _heads(do_m, nb, MLA_HEADS), _row_tiles(lse_m), _row_tiles(delta_mh),
                                None, None, scale_m, "mla_attn_bwd")
    dq_t = jnp.transpose(dq_c, (0, 2, 1, 3))
    dq_m = jnp.concatenate([dq_t[..., :MLA_NOPE].reshape(t, hq), dq_t[..., MLA_NOPE:MLA_NOPE + half].reshape(t, hr),
                            dq_t[..., MLA_NOPE + half:].reshape(t, hr)], axis=1).astype(BF16)
    dk_t = jnp.transpose(dk_c, (0, 2, 1, 3))
    dkn_m = dk_t[..., :MLA_NOPE].reshape(t, hq)
    dkr_heads = dk_t[..., MLA_NOPE:].reshape(t, MLA_HEADS * MLA_ROPE)
    dh_in, dq_pre, dgq, dgkv = mla_mid_bwd(
        dq_m, dkn_m, _unheads(dv_c), dkr_heads, h_in, vec(mla_g_q), vec(mla_g_kv), w_uq_p, wts["mla_w_uk"],
        wts["mla_w_uv"], cos8, sin8, cos64, sin64s, swap64, head_sum, "mla_mid_bwd")
    grads["mla_w_uq"] = wgrad(cq_m[None], dq_pre[None], "mla_dwuq")[0][:, inv_perm]
    grads["mla_w_uk"] = wgrad(ckv_m[None], dkn_m[None], "mla_dwuk")[0]
    grads["mla_w_uv"] = wgrad(ckv_m[None], _unheads(dv_c)[None], "mla_dwuv")[0]
    grads["mla_w_in"] = wgrad(u_m[None], dh_in[None], "mla_dwin")[0]
    dx0, dsc_a0, dsh_a0 = linear_nt_mod_bwd([(dh_in, wts["mla_w_in"])], dz0, x0, sc_a, "mla_in_bwd")

    dmods = [(dsh_a0, dsc_a0, dgt_a0, dsh_f0, dsc_f0, dgt_f0), (dsh_a1, dsc_a1, dgt_a1, dsh_f1, dsc_f1, dgt_f1)]
    d_ln_g = jnp.stack([jnp.concatenate([dg00, dg01], axis=0), jnp.concatenate([dg10, dg11], axis=0)])
    d_ln_b = jnp.stack([jnp.concatenate([db00, db01], axis=0), jnp.concatenate([db10, db11], axis=0)])
    return loss_part, dx0.reshape(nb, s, d), grads, dmods, d_ln_g, d_ln_b, dgq, dgkv, dbf[:, :FOX_HEADS]


def _pad_rows(a, rows):
    return jnp.pad(a, ((0, rows - a.shape[0]), (0, 0)))


def kernel(x, c, positions, mla_w_in, mla_g_q, mla_w_uq, mla_g_kv, mla_w_uk, mla_w_uv, mla_w_o, fox_w_in, fox_b_f, fox_w_o, ada_w, ada_b, ffn_w_gate, ffn_w_up, ffn_w_down, ln_g, ln_b, loss_target, m_mla_w_in, m_mla_g_q, m_mla_w_uq, m_mla_g_kv, m_mla_w_uk, m_mla_w_uv, m_mla_w_o, m_fox_w_in, m_fox_b_f, m_fox_w_o, m_ada_w, m_ada_b, m_ffn_w_gate, m_ffn_w_up, m_ffn_w_down, m_ln_g, m_ln_b, v_mla_w_in, v_mla_g_q, v_mla_w_uq, v_mla_g_kv, v_mla_w_uk, v_mla_w_uv, v_mla_w_o, v_fox_w_in, v_fox_b_f, v_fox_w_o, v_ada_w, v_ada_b, v_ffn_w_gate, v_ffn_w_up, v_ffn_w_down, v_ln_g, v_ln_b):
    args = dict(locals())
    nb, s, d = x.shape
    ax, ay, ac = lax.axis_index("x"), lax.axis_index("y"), lax.axis_index("c")
    chip = 2 * ax + ay
    dev = 2 * chip + ac
    n_dev = 2 * N_CHIPS
    n_all = nb * n_dev

    shard_shapes = {n: (args[n].shape if _SHARD_KIND[n] == "chunk" else args[n].shape[1:]) for n in _PACKED}
    w_all = all_gather_chips([_halves(args[n].reshape(shard_shapes[n]).astype(BF16)) for n in _PACKED], "gather_weights")
    wts = {}
    for n, g in zip(_PACKED, w_all):
        g = g.reshape(N_CHIPS, *shard_shapes[n])
        if _SHARD_KIND[n] == "rows":
            g = g.reshape(-1, g.shape[-1])
        elif _SHARD_KIND[n] == "cols":
            g = _cols_to_full(g)
        wts[n] = g

    c_all = all_gather8(_pad_rows(c, 8), "gather_c").reshape(n_dev, 8, d)[:, :nb].reshape(n_all, d)
    mod_part = ada_mod_part(c_all, ada_w, "ada_mod")
    ncol = mod_part.shape[-1]
    mod_g = all_gather8(mod_part.reshape(DEPTH * n_all, ncol), "gather_mod")
    mod_g = mod_g.reshape(N_CHIPS, 2, DEPTH, n_all, ncol)[:, 0]
    mod_full = jnp.transpose(mod_g, (1, 2, 0, 3)).reshape(DEPTH, n_all, N_CHIPS * ncol) + ada_b[:, None, :]
    mod_loc = lax.dynamic_slice_in_dim(mod_full, dev * nb, nb, axis=1)
    mods = [tuple(mod_loc[i, :, k * d:(k + 1) * d].reshape(nb, 1, d) for k in range(6)) for i in range(DEPTH)]

    ln_cols = ln_g.shape[-1]
    ln_blk = jnp.concatenate([ln_g.reshape(2 * DEPTH, ln_cols), ln_b.reshape(2 * DEPTH, ln_cols)], axis=0)
    ln_all = all_gather8(ln_blk, "gather_ln").reshape(N_CHIPS, 2, 4 * DEPTH, ln_cols)[:, 0]
    ln_all = jnp.transpose(ln_all, (1, 0, 2)).reshape(4 * DEPTH, d)
    ln_g_full = ln_all[:2 * DEPTH].reshape(DEPTH, 2, d)
    ln_b_full = ln_all[2 * DEPTH:].reshape(DEPTH, 2, d)

    loss_part, grad_x, grads, dmods, d_ln_g, d_ln_b, dgq, dgkv, dbf = _local_step(
        x, positions, loss_target, mods, wts, ln_g_full, ln_b_full, mla_g_q[0], mla_g_kv[0], fox_b_f[0])
    loss = lax.psum(loss_part, ("x", "y", "c"))

    dmod_rows = jnp.stack([jnp.concatenate([v_.reshape(nb, d) for v_ in dm], axis=1) for dm in dmods])
    small = jnp.concatenate([
        d_ln_g.reshape(2 * DEPTH, d), d_ln_b.reshape(2 * DEPTH, d),
        jnp.pad(jnp.concatenate([dgq, dgkv, dbf], axis=1), ((0, 0), (0, d - 2 * MLA_QR - FOX_HEADS))),
        dmod_rows.reshape(DEPTH * nb * 6, d)], axis=0)
    n_small = small.shape[0]
    small_rows = -(-n_small // 8) * 8
    small_all = all_gather8(_pad_rows(small, small_rows), "gather_stats").reshape(n_dev, small_rows, d)
    stat_sum = sum_leading(small_all, "sum_stats")
    g_ln_g = lax.dynamic_slice_in_dim(stat_sum[:2 * DEPTH], chip * ln_cols, ln_cols, axis=1).reshape(DEPTH, 2, ln_cols)
    g_ln_b = lax.dynamic_slice_in_dim(stat_sum[2 * DEPTH:4 * DEPTH], chip * ln_cols, ln_cols, axis=1).reshape(DEPTH, 2, ln_cols)
    row = stat_sum[4 * DEPTH]
    g_gq = row[:MLA_QR].reshape(1, MLA_QR)
    g_gkv = row[MLA_QR:2 * MLA_QR].reshape(1, MLA_KVR)
    g_bf = row[2 * MLA_QR:2 * MLA_QR + FOX_HEADS].reshape(1, FOX_HEADS)
    base = 4 * DEPTH + 1
    dmod_all = small_all[:, base:base + DEPTH * nb * 6].reshape(n_dev, DEPTH, nb, 6 * d)
    dmod_all = jnp.transpose(dmod_all, (1, 0, 2, 3)).reshape(DEPTH, n_all, 6 * d)
    g_ada_b = sum_leading(jnp.transpose(dmod_all, (1, 0, 2)), "sum_ada_b")
    dmod_mine = lax.dynamic_slice_in_dim(dmod_all, chip * ncol, ncol, axis=2)
    g_ada_w = ada_grad(c_all.T, dmod_mine, "ada_grad")

    g_blocks = []
    for n in _PACKED:
        g = grads[n]
        if _SHARD_KIND[n] == "rows":
            g = g.reshape(N_CHIPS, 2, g.shape[0] // (2 * N_CHIPS), g.shape[1])
        elif _SHARD_KIND[n] == "cols":
            g = _full_to_cols(g)
            g = g.reshape(N_CHIPS, 2, g.shape[1] // 2, g.shape[2])
        g_blocks.append(g)
    core = ac.reshape(1).astype(jnp.int32)
    recv = sibling_swap_halves(g_blocks, "rs_swap")
    chip_sums = [add_own_half(g, a, core, "rs_add_" + n) for n, g, a in zip(_PACKED, g_blocks, recv)]
    pieces = chip_exchange(chip_sums, "rs_exchange")
    place = jnp.stack([ac, chip]).astype(jnp.int32)
    my_halves = [sum_pieces(sm, b, place, "rs_sum_" + n) for n, sm, b in zip(_PACKED, chip_sums, pieces)]
    joined = sibling_join_halves(my_halves, "rs_join")
    g_big = {n: j.reshape(shard_shapes[n]) for n, j in zip(_PACKED, joined)}

    g_out = {
        "mla_w_in": g_big["mla_w_in"], "mla_g_q": g_gq, "mla_w_uq": g_big["mla_w_uq"], "mla_g_kv": g_gkv,
        "mla_w_uk": g_big["mla_w_uk"], "mla_w_uv": g_big["mla_w_uv"], "mla_w_o": g_big["mla_w_o"],
        "fox_w_in": g_big["fox_w_in"], "fox_b_f": g_bf, "fox_w_o": g_big["fox_w_o"],
        "ada_w": g_ada_w, "ada_b": g_ada_b, "ffn_w_gate": g_big["ffn_w_gate"], "ffn_w_up": g_big["ffn_w_up"],
        "ffn_w_down": g_big["ffn_w_down"], "ln_g": g_ln_g, "ln_b": g_ln_b}
    names = ["mla_w_in", "mla_g_q", "mla_w_uq", "mla_g_kv", "mla_w_uk", "mla_w_uv", "mla_w_o", "fox_w_in", "fox_b_f",
             "fox_w_o", "ada_w", "ada_b", "ffn_w_gate", "ffn_w_up", "ffn_w_down", "ln_g", "ln_b"]
    small_names = ["mla_g_q", "mla_g_kv", "fox_b_f", "ada_b", "ln_g", "ln_b"]
    deltas, new_m, new_v = {}, {}, {}
    for n in names:
        if n in small_names:
            continue
        shp = args[n].shape
        two_d = (-1, shp[-1])
        dl, mn, vn = adamw(args[n].reshape(two_d), g_out[n].reshape(two_d), args["m_" + n].reshape(two_d),
                           args["v_" + n].reshape(two_d), "adamw_" + n)
        deltas[n], new_m[n], new_v[n] = dl.reshape(shp), mn.reshape(shp), vn.reshape(shp)

    def small_pack(prefix, src):
        flat = jnp.concatenate([src[prefix + n].reshape(-1) for n in small_names])
        size = -(-flat.shape[0] // (8 * 128)) * 8 * 128
        return jnp.pad(flat, (0, size - flat.shape[0])).reshape(-1, 128)

    sd, sm, sv = adamw(small_pack("", args), small_pack("", g_out), small_pack("m_", args), small_pack("v_", args),
                       "adamw_small")
    off = 0
    for n in small_names:
        shp = args[n].shape
        size = math.prod(shp)
        deltas[n] = sd.reshape(-1)[off:off + size].reshape(shp)
        new_m[n] = sm.reshape(-1)[off:off + size].reshape(shp)
        new_v[n] = sv.reshape(-1)[off:off + size].reshape(shp)
        off += size

    outs = [loss, grad_x]
    outs += [g_out[n].reshape(args[n].shape) for n in names]
    outs += [deltas[n] for n in names] + [new_m[n] for n in names] + [new_v[n] for n in names]
    return tuple(outs)
```

```python
import functools
import math

import numpy as np
import jax
import jax.numpy as jnp
from jax import lax
from jax.experimental import pallas as pl
from jax.experimental.pallas import tpu as pltpu

F32 = jnp.float32
BF16 = jnp.bfloat16
MESH = pl.DeviceIdType.MESH

D_MODEL = 1024
DEPTH = 2
MLA_HEADS = 8
MLA_NOPE = 128
MLA_ROPE = 64
MLA_V = 128
MLA_QR = 256
MLA_KVR = 256
ROPE_THETA = 10000.0
FOX_HEADS = 16
FOX_HD = 64
D_FF = 2816
N_CHIPS = 4
FF_CHUNK = D_FF // N_CHIPS
ALPHA = (2.0 * DEPTH) ** 0.25
EPS = 1e-5
ADAM_LR = 0.001
ADAM_B1 = 0.9
ADAM_B2 = 0.999
ADAM_EPS = 1e-08
ADAM_WD = 0.01
ADAM_STEP = 10

VMEM_LIMIT = 56 * 1024 * 1024
TOKEN_TILE = 512
ATTN_TILE = 512
COMM_BLOCK_BYTES = 2 * 1024 * 1024
ADAMW_BLOCK_BYTES = 1024 * 1024


def _cp(n_axes):
    return pltpu.CompilerParams(dimension_semantics=("arbitrary",) * n_axes, vmem_limit_bytes=VMEM_LIMIT)


def _dot(a, b):
    return jnp.dot(a, b, preferred_element_type=F32)


def _dot_nt(a, b):
    return lax.dot_general(a, b, (((1,), (1,)), ((), ())), preferred_element_type=F32)


def _dot_tn(a, b):
    return lax.dot_general(a, b, (((0,), (0,)), ((), ())), preferred_element_type=F32)


def _dot_f32(a, b):
    return jnp.dot(a, b, preferred_element_type=F32, precision=lax.Precision.HIGHEST)


def _sds(shape, dtype):
    return jax.ShapeDtypeStruct(shape, dtype)


def mod_linear(x, shift, scale, w, out_dtype, name, tn=None, emit_u=False):
    t, d = x.shape
    n = w.shape[1]
    tn = n if tn is None else tn
    tm = TOKEN_TILE
    tps = (t // shift.shape[0]) // tm

    def body(x_ref, sh_ref, sc_ref, w_ref, o_ref, *rest):
        u = (x_ref[...] * (1.0 + sc_ref[...]) + sh_ref[...]).astype(BF16)
        o_ref[...] = _dot(u, w_ref[...]).astype(out_dtype)
        if emit_u:
            @pl.when(pl.program_id(1) == 0)
            def _():
                rest[0][...] = u

    vec = pl.BlockSpec((None, 1, d), lambda i, j: (i // tps, 0, 0))
    out_shape = [_sds((t, n), out_dtype)]
    out_specs = [pl.BlockSpec((tm, tn), lambda i, j: (i, j))]
    if emit_u:
        out_shape.append(_sds((t, d), BF16))
        out_specs.append(pl.BlockSpec((tm, d), lambda i, j: (i, 0)))
    res = pl.pallas_call(
        body, name=name, grid=(t // tm, n // tn),
        in_specs=[pl.BlockSpec((tm, d), lambda i, j: (i, 0)), vec, vec,
                  pl.BlockSpec((d, tn), lambda i, j: (0, j))],
        out_specs=out_specs, out_shape=out_shape, compiler_params=_cp(2),
    )(x, shift, scale, w)
    return res if emit_u else res[0]


def _rms(h, g):
    rstd = lax.rsqrt(jnp.mean(h * h, axis=-1, keepdims=True) + EPS)
    return h * rstd, rstd


def mla_mid_fwd(h, g_q, g_kv, w_uq, w_uk, w_uv, cos8, sin8, cos64, sin64s, swap64, rope_to_heads, dup64, name):
    t = h.shape[0]
    tm = TOKEN_TILE
    hq = MLA_HEADS * MLA_NOPE
    hr = MLA_HEADS * MLA_ROPE // 2

    def body(h_ref, gq_ref, gkv_ref, wuq_ref, wuk_ref, wuv_ref, c8_ref, s8_ref, c64_ref, s64_ref, sw_ref, p_ref, d_ref,
             q_ref, kn_ref, v_ref, kr_ref, cq_ref, ckv_ref):
        hh = h_ref[...]
        cq = (_rms(hh[:, :MLA_QR], None)[0] * gq_ref[...]).astype(BF16)
        ckv = (_rms(hh[:, MLA_QR:MLA_QR + MLA_KVR], None)[0] * gkv_ref[...]).astype(BF16)
        cq_ref[...] = cq
        ckv_ref[...] = ckv
        q = _dot(cq, wuq_ref[...])
        x1 = q[:, hq:hq + hr]
        x2 = q[:, hq + hr:]
        cs = c8_ref[...]
        sn = s8_ref[...]
        rot = jnp.concatenate([x1 * cs - x2 * sn, x2 * cs + x1 * sn], axis=1).astype(BF16)
        q_ref[...] = jnp.concatenate([q[:, :hq].astype(BF16), _dot(rot, p_ref[...]).astype(BF16)], axis=1)
        kn_ref[...] = _dot(ckv, wuk_ref[...]).astype(BF16)
        v_ref[...] = _dot(ckv, wuv_ref[...]).astype(BF16)
        kr = hh[:, MLA_QR + MLA_KVR:]
        kr = (kr * c64_ref[...] + _dot_f32(kr, sw_ref[...]) * s64_ref[...]).astype(BF16)
        kr_ref[...] = _dot(kr, d_ref[...]).astype(BF16)

    def rows(n):
        return pl.BlockSpec((tm, n), lambda i: (i, 0))

    def whole(a):
        return pl.BlockSpec(a.shape, lambda i: (0,) * a.ndim)

    nq = w_uq.shape[1]
    return pl.pallas_call(
        body, name=name, grid=(t // tm,),
        in_specs=[rows(h.shape[1]), whole(g_q), whole(g_kv), whole(w_uq), whole(w_uk), whole(w_uv),
                  rows(hr), rows(hr), rows(MLA_ROPE), rows(MLA_ROPE), whole(swap64), whole(rope_to_heads), whole(dup64)],
        out_specs=[rows(nq), rows(hq), rows(hq), rows(2 * MLA_ROPE), rows(MLA_QR), rows(MLA_KVR)],
        out_shape=[_sds((t, nq), BF16), _sds((t, hq), BF16), _sds((t, hq), BF16), _sds((t, 2 * MLA_ROPE), BF16),
                   _sds((t, MLA_QR), BF16), _sds((t, MLA_KVR), BF16)],
        compiler_params=_cp(1),
    )(h, g_q, g_kv, w_uq, w_uk, w_uv, cos8, sin8, cos64, sin64s, swap64, rope_to_heads, dup64)


def _pick_lane(tile, idx):
    lane = lax.broadcasted_iota(jnp.int32, tile.shape, 1)
    return jnp.sum(jnp.where(lane == idx, tile, 0.0), axis=1, keepdims=True)


def _pick_row(tile, idx):
    row = lax.broadcasted_iota(jnp.int32, tile.shape, 0)
    return jnp.sum(jnp.where(row == idx, tile, 0.0), axis=0, keepdims=True)


def _put_lane(tile, idx, col):
    lane = lax.broadcasted_iota(jnp.int32, tile.shape, 1)
    return jnp.where(lane == idx, col, tile)


def _put_row(tile, idx, row):
    r = lax.broadcasted_iota(jnp.int32, tile.shape, 0)
    return tile + jnp.where(r == idx, row, 0.0)


def _causal_softmax_blocks(i, nblk_rows, score_fn, pv_fn, dv):
    tq = nblk_rows

    def block(j, carry, masked):
        m, l, acc = carry
        sc = score_fn(j)
        if masked:
            keep = lax.broadcasted_iota(jnp.int32, sc.shape, 0) >= lax.broadcasted_iota(jnp.int32, sc.shape, 1)
            sc = jnp.where(keep, sc, -1e30)
        m_new = jnp.maximum(m, jnp.max(sc, axis=1, keepdims=True))
        a = jnp.exp(m - m_new)
        p = jnp.exp(sc - m_new)
        l = a * l + jnp.sum(p, axis=1, keepdims=True)
        acc = a * acc + pv_fn(j, p.astype(BF16))
        return m_new, l, acc

    init = (jnp.full((tq, 1), -1e30, F32), jnp.zeros((tq, 1), F32), jnp.zeros((tq, dv), F32))
    carry = lax.fori_loop(0, i, lambda j, c: block(j, c, False), init)
    m, l, acc = block(i, carry, True)
    return acc / l, m + jnp.log(l)


def fox_attn_fwd(qkv, cum, cum_rows, nb, name):
    t = qkv.shape[0]
    s = t // nb
    tq = ATTN_TILE
    nq = s // tq
    npairs = FOX_HEADS // 2
    scale = FOX_HD ** -0.5

    def body(q_ref, k_ref, v_ref, cum_ref, cr_ref, o_ref, lse_ref):
        i = pl.program_id(1)
        hp = pl.program_id(2)

        @pl.when(hp == 0)
        def _():
            lse_ref[...] = jnp.zeros_like(lse_ref)

        q = q_ref[...]
        low = lax.broadcasted_iota(jnp.int32, q.shape, 1) < FOX_HD
        cum_t = cum_ref[...]
        outs = []
        lse_t = lse_ref[...]
        for a in (0, 1):
            head = 2 * hp + a
            qa = jnp.where(low if a == 0 else jnp.logical_not(low), q, jnp.zeros_like(q))
            fq = _pick_lane(cum_t, head)

            def score(j, qa=qa, fq=fq, head=head):
                kb = k_ref[pl.ds(pl.multiple_of(j * tq, tq), tq), :]
                return _dot_nt(qa, kb) * scale + fq - _pick_row(cr_ref[j], head)

            def pv(j, p):
                return _dot(p, v_ref[pl.ds(pl.multiple_of(j * tq, tq), tq), :])

            o_a, lse_a = _causal_softmax_blocks(i, tq, score, pv, 2 * FOX_HD)
            outs.append(o_a)
            lse_t = _put_lane(lse_t, head, lse_a)
        o_ref[...] = jnp.where(low, outs[0], outs[1]).astype(BF16)
        lse_ref[...] = lse_t

    return pl.pallas_call(
        body, name=name, grid=(nb, nq, npairs),
        in_specs=[pl.BlockSpec((tq, 128), lambda b, i, hp: (b * nq + i, hp)),
                  pl.BlockSpec((s, 128), lambda b, i, hp: (b, npairs + hp)),
                  pl.BlockSpec((s, 128), lambda b, i, hp: (b, 2 * npairs + hp)),
                  pl.BlockSpec((tq, 128), lambda b, i, hp: (b * nq + i, 0)),
                  pl.BlockSpec((nq, 16, tq), lambda b, i, hp: (b, 0, 0))],
        out_specs=[pl.BlockSpec((tq, 128), lambda b, i, hp: (b * nq + i, hp)),
                   pl.BlockSpec((tq, 128), lambda b, i, hp: (b * nq + i, 0))],
        out_shape=[_sds((t, D_MODEL), BF16), _sds((t, 128), F32)],
        compiler_params=_cp(3),
    )(qkv, qkv, qkv, cum, cum_rows)


def mla_attn_fwd(q, kn, kr2, v, nb, name):
    t = q.shape[0]
    s = t // nb
    tq = ATTN_TILE
    nq = s // tq
    scale = (MLA_NOPE + MLA_ROPE) ** -0.5

    def body(qn_ref, qr_ref, kn_ref, kr_ref, v_ref, o_ref, lse_ref):
        i = pl.program_id(1)
        h = pl.program_id(2)

        @pl.when(h == 0)
        def _():
            lse_ref[...] = jnp.zeros_like(lse_ref)

        qr = qr_ref[...]
        mine = (lax.broadcasted_iota(jnp.int32, qr.shape, 1) // MLA_ROPE) == (h % 2)
        q_cat = jnp.concatenate([qn_ref[...], jnp.where(mine, qr, jnp.zeros_like(qr))], axis=1)

        def score(j):
            rows = pl.ds(pl.multiple_of(j * tq, tq), tq)
            k_cat = jnp.concatenate([kn_ref[rows, :], kr_ref[rows, :]], axis=1)
            return _dot_nt(q_cat, k_cat) * scale

        def pv(j, p):
            return _dot(p, v_ref[pl.ds(pl.multiple_of(j * tq, tq), tq), :])

        o, lse = _causal_softmax_blocks(i, tq, score, pv, MLA_V)
        o_ref[...] = o.astype(BF16)
        lse_ref[...] = _put_lane(lse_ref[...], h, lse)

    nrope0 = MLA_HEADS
    return pl.pallas_call(
        body, name=name, grid=(nb, nq, MLA_HEADS),
        in_specs=[pl.BlockSpec((tq, 128), lambda b, i, h: (b * nq + i, h)),
                  pl.BlockSpec((tq, 128), lambda b, i, h: (b * nq + i, nrope0 + h // 2)),
                  pl.BlockSpec((s, 128), lambda b, i, h: (b, h)),
                  pl.BlockSpec((s, 128), lambda b, i, h: (b, 0)),
                  pl.BlockSpec((s, 128), lambda b, i, h: (b, h))],
        out_specs=[pl.BlockSpec((tq, 128), lambda b, i, h: (b * nq + i, h)),
                   pl.BlockSpec((tq, 128), lambda b, i, h: (b * nq + i, 0))],
        out_shape=[_sds((t, MLA_HEADS * MLA_V), BF16), _sds((t, 128), F32)],
        compiler_params=_cp(3),
    )(q, q, kn, kr2, v)


def rows16(a, name):
    t = a.shape[0]
    tq = ATTN_TILE

    def body(a_ref, o_ref):
        o_ref[...] = a_ref[...].T[:16, :]

    return pl.pallas_call(
        body, name=name, grid=(t // tq,), in_specs=[pl.BlockSpec((tq, 128), lambda n: (n, 0))],
        out_specs=pl.BlockSpec((None, 16, tq), lambda n: (n, 0, 0)), out_shape=_sds((t // tq, 16, tq), F32),
        compiler_params=_cp(1),
    )(a)


def tokens128(rows, onehot, name):
    nblk, _, tq = rows.shape

    def body(r_ref, e_ref, o_ref):
        o_ref[...] = lax.dot_general(r_ref[...], e_ref[...], (((0,), (0,)), ((), ())), preferred_element_type=F32,
                                     precision=lax.Precision.HIGHEST)

    return pl.pallas_call(
        body, name=name, grid=(nblk,),
        in_specs=[pl.BlockSpec((None, 16, tq), lambda n: (n, 0, 0)), pl.BlockSpec((16, 128), lambda n: (0, 0))],
        out_specs=pl.BlockSpec((tq, 128), lambda n: (n, 0)), out_shape=_sds((nblk * tq, 128), F32),
        compiler_params=_cp(1),
    )(rows, onehot)


def _layer_norm(z, g, b):
    mu = jnp.mean(z, axis=-1, keepdims=True)
    zc = z - mu
    rstd = lax.rsqrt(jnp.mean(zc * zc, axis=-1, keepdims=True) + EPS)
    xhat = zc * rstd
    return xhat * g + b, xhat, rstd


def linear_resid_ln(a, w, x_in, gate, ln_g, ln_b, name):
    t, kdim = a.shape
    d = w.shape[1]
    tm = TOKEN_TILE
    tps = (t // gate.shape[0]) // tm

    def body(a_ref, w_ref, x_ref, gt_ref, g_ref, b_ref, y_ref, xo_ref):
        y = _dot(a_ref[...], w_ref[...])
        y_ref[...] = y
        z = ALPHA * x_ref[...] + (1.0 + gt_ref[...]) * y
        xo_ref[...] = _layer_norm(z, g_ref[...], b_ref[...])[0]

    rows = pl.BlockSpec((tm, d), lambda i: (i, 0))
    vec = pl.BlockSpec((1, d), lambda i: (0, 0))
    return pl.pallas_call(
        body, name=name, grid=(t // tm,),
        in_specs=[pl.BlockSpec((tm, kdim), lambda i: (i, 0)), pl.BlockSpec((kdim, d), lambda i: (0, 0)), rows,
                  pl.BlockSpec((None, 1, d), lambda i: (i // tps, 0, 0)), vec, vec],
        out_specs=[rows, rows], out_shape=[_sds((t, d), F32), _sds((t, d), F32)],
        compiler_params=_cp(1),
    )(a, w, x_in, gate, ln_g, ln_b)


def ffn_fwd(x_in, shift, scale, gate, wg, wu, wd, layer, ln_g, ln_b, name):
    t, d = x_in.shape
    c, _, _, fc = wg.shape
    tm = TOKEN_TILE
    tps = (t // gate.shape[0]) // tm

    def body(x_ref, sh_ref, sc_ref, gt_ref, wg_ref, wu_ref, wd_ref, g_ref, b_ref,
             u_ref, hg_ref, hu_ref, y_ref, xo_ref, acc_ref):
        cc = pl.program_id(1)

        @pl.when(cc == 0)
        def _():
            u_ref[...] = (x_ref[...] * (1.0 + sc_ref[...]) + sh_ref[...]).astype(BF16)
            acc_ref[...] = jnp.zeros_like(acc_ref)

        u = u_ref[...]
        hg = _dot(u, wg_ref[...])
        hu = _dot(u, wu_ref[...])
        hg_ref[...] = hg.astype(BF16)
        hu_ref[...] = hu.astype(BF16)
        act = (hg * jax.nn.sigmoid(hg) * hu).astype(BF16)
        acc_ref[...] += _dot(act, wd_ref[...])

        @pl.when(cc == c - 1)
        def _():
            y = acc_ref[...]
            y_ref[...] = y
            z = ALPHA * x_ref[...] + (1.0 + gt_ref[...]) * y
            xo_ref[...] = _layer_norm(z, g_ref[...], b_ref[...])[0]

    rows = pl.BlockSpec((tm, d), lambda i, cc: (i, 0))
    bvec = pl.BlockSpec((None, 1, d), lambda i, cc: (i // tps, 0, 0))
    vec = pl.BlockSpec((1, d), lambda i, cc: (0, 0))
    hspec = pl.BlockSpec((None, tm, fc), lambda i, cc: (cc, i, 0))
    return pl.pallas_call(
        body, name=name, grid=(t // tm, c),
        in_specs=[rows, bvec, bvec, bvec,
                  pl.BlockSpec((None, None, d, fc), lambda i, cc: (cc, layer, 0, 0)),
                  pl.BlockSpec((None, None, d, fc), lambda i, cc: (cc, layer, 0, 0)),
                  pl.BlockSpec((None, None, fc, d), lambda i, cc: (cc, layer, 0, 0)), vec, vec],
        out_specs=[rows, hspec, hspec, rows, rows],
        out_shape=[_sds((t, d), BF16), _sds((c, t, fc), BF16), _sds((c, t, fc), BF16),
                   _sds((t, d), F32), _sds((t, d), F32)],
        scratch_shapes=[pltpu.VMEM((tm, d), F32)],
        compiler_params=_cp(2),
    )(x_in, shift, scale, gate, wg, wu, wd, ln_g, ln_b)


def fox_gate_fwd(hf, b_f, tri, n_batch, name):
    t, n = hf.shape
    blk = tri.shape[0]
    nb = (t // n_batch) // blk

    def body(hf_ref, b_ref, tri_ref, o_ref, carry_ref):
        @pl.when(pl.program_id(1) == 0)
        def _():
            carry_ref[...] = jnp.zeros_like(carry_ref)

        xx = hf_ref[...] + b_ref[...]
        lf = jnp.minimum(xx, 0.0) - jnp.log(1.0 + jnp.exp(-jnp.abs(xx)))
        cum = _dot_f32(tri_ref[...], lf) + carry_ref[...]
        o_ref[...] = cum
        carry_ref[...] = cum[blk - 1:blk, :]

    return pl.pallas_call(
        body, name=name, grid=(n_batch, nb),
        in_specs=[pl.BlockSpec((blk, n), lambda bb, i: (bb * nb + i, 0)), pl.BlockSpec((1, n), lambda bb, i: (0, 0)),
                  pl.BlockSpec((blk, blk), lambda bb, i: (0, 0))],
        out_specs=pl.BlockSpec((blk, n), lambda bb, i: (bb * nb + i, 0)),
        out_shape=_sds((t, n), F32), scratch_shapes=[pltpu.VMEM((1, n), F32)],
        compiler_params=_cp(2),
    )(hf, b_f, tri)


def loss_grad(x_out, target, name):
    t, d = x_out.shape
    tm = TOKEN_TILE

    def body(x_ref, t_ref, g_ref, l_ref):
        @pl.when(pl.program_id(0) == 0)
        def _():
            l_ref[...] = jnp.zeros_like(l_ref)

        err = x_ref[...] - t_ref[...]
        g_ref[...] = err / d
        l_ref[...] += jnp.sum(err * err, axis=0, keepdims=True)

    rows = pl.BlockSpec((tm, d), lambda i: (i, 0))
    return pl.pallas_call(
        body, name=name, grid=(t // tm,), in_specs=[rows, rows],
        out_specs=[rows, pl.BlockSpec((1, d), lambda i: (0, 0))],
        out_shape=[_sds((t, d), F32), _sds((1, d), F32)], compiler_params=_cp(1),
    )(x_out, target)


def ln_bwd(dxo, x_in, y, gate, ln_g, name):
    t, d = dxo.shape
    nb = gate.shape[0]
    tm = TOKEN_TILE
    tps = (t // nb) // tm

    def body(dxo_ref, x_ref, y_ref, gt_ref, g_ref, dz_ref, dy_ref, dg_ref, db_ref, dgt_ref):
        i = pl.program_id(0)

        @pl.when(i == 0)
        def _():
            dg_ref[...] = jnp.zeros_like(dg_ref)
            db_ref[...] = jnp.zeros_like(db_ref)

        @pl.when(i % tps == 0)
        def _():
            dgt_ref[...] = jnp.zeros_like(dgt_ref)

        yy = y_ref[...]
        g1 = 1.0 + gt_ref[...]
        z = ALPHA * x_ref[...] + g1 * yy
        _, xhat, rstd = _layer_norm(z, 1.0, 0.0)
        dxo_v = dxo_ref[...]
        dg_ref[...] += jnp.sum(dxo_v * xhat, axis=0, keepdims=True)
        db_ref[...] += jnp.sum(dxo_v, axis=0, keepdims=True)
        dxh = dxo_v * g_ref[...]
        dz = rstd * (dxh - jnp.mean(dxh, axis=-1, keepdims=True) - xhat * jnp.mean(dxh * xhat, axis=-1, keepdims=True))
        dz_ref[...] = dz
        dy_ref[...] = (g1 * dz).astype(BF16)
        dgt_ref[...] += jnp.sum(dz * yy, axis=0, keepdims=True)

    rows = pl.BlockSpec((tm, d), lambda i: (i, 0))
    vec = pl.BlockSpec((1, d), lambda i: (0, 0))
    bvec = pl.BlockSpec((None, 1, d), lambda i: (i // tps, 0, 0))
    return pl.pallas_call(
        body, name=name, grid=(t // tm,), in_specs=[rows, rows, rows, bvec, vec],
        out_specs=[rows, rows, vec, vec, bvec],
        out_shape=[_sds((t, d), F32), _sds((t, d), BF16), _sds((1, d), F32), _sds((1, d), F32), _sds((nb, 1, d), F32)],
        compiler_params=_cp(1),
    )(dxo, x_in, y, gate, ln_g)


def _mod_bwd_tail(du, dz_ref, x_ref, sc_ref, dx_ref, dsc_ref, dsh_ref, first):
    @pl.when(first)
    def _():
        dsc_ref[...] = jnp.zeros_like(dsc_ref)
        dsh_ref[...] = jnp.zeros_like(dsh_ref)

    dx_ref[...] = ALPHA * dz_ref[...] + du * (1.0 + sc_ref[...])
    dsc_ref[...] += jnp.sum(du * x_ref[...], axis=0, keepdims=True)
    dsh_ref[...] += jnp.sum(du, axis=0, keepdims=True)


def ffn_bwd(dy, hg, hu, wg, wu, wd, layer, dz, x_in, scale, name):
    t, d = dy.shape
    c, _, _, fc = wg.shape
    nb = scale.shape[0]
    tm = TOKEN_TILE
    tps = (t // nb) // tm

    def body(dy_ref, hg_ref, hu_ref, wg_ref, wu_ref, wd_ref, dz_ref, x_ref, sc_ref,
             dhg_ref, dhu_ref, act_ref, dx_ref, dsc_ref, dsh_ref, acc_ref):
        i = pl.program_id(0)
        cc = pl.program_id(1)

        @pl.when(cc == 0)
        def _():
            acc_ref[...] = jnp.zeros_like(acc_ref)

        hgv = hg_ref[...].astype(F32)
        huv = hu_ref[...].astype(F32)
        da = _dot_nt(dy_ref[...], wd_ref[...])
        sg = jax.nn.sigmoid(hgv)
        sl = hgv * sg
        act_ref[...] = (sl * huv).astype(BF16)
        dhu = (da * sl).astype(BF16)
        dhg = (da * huv * (sg * (1.0 + hgv * (1.0 - sg)))).astype(BF16)
        dhu_ref[...] = dhu
        dhg_ref[...] = dhg
        acc_ref[...] += _dot_nt(dhg, wg_ref[...]) + _dot_nt(dhu, wu_ref[...])

        @pl.when(cc == c - 1)
        def _():
            _mod_bwd_tail(acc_ref[...], dz_ref, x_ref, sc_ref, dx_ref, dsc_ref, dsh_ref, i % tps == 0)

    rows = pl.BlockSpec((tm, d), lambda i, cc: (i, 0))
    bvec = pl.BlockSpec((None, 1, d), lambda i, cc: (i // tps, 0, 0))
    hspec = pl.BlockSpec((None, tm, fc), lambda i, cc: (cc, i, 0))
    wcol = pl.BlockSpec((None, None, d, fc), lambda i, cc: (cc, layer, 0, 0))
    return pl.pallas_call(
        body, name=name, grid=(t // tm, c),
        in_specs=[rows, hspec, hspec, wcol, wcol, pl.BlockSpec((None, None, fc, d), lambda i, cc: (cc, layer, 0, 0)),
                  rows, rows, bvec],
        out_specs=[hspec, hspec, hspec, rows, bvec, bvec],
        out_shape=[_sds((c, t, fc), BF16), _sds((c, t, fc), BF16), _sds((c, t, fc), BF16), _sds((t, d), F32),
                   _sds((nb, 1, d), F32), _sds((nb, 1, d), F32)],
        scratch_shapes=[pltpu.VMEM((tm, d), F32)],
        compiler_params=_cp(2),
    )(dy, hg, hu, wg, wu, wd, dz, x_in, scale)


def linear_nt_mod_bwd(pairs, dz, x_in, scale, name):
    t, d = dz.shape
    nb = scale.shape[0]
    tm = TOKEN_TILE
    tps = (t // nb) // tm
    npairs = len(pairs)

    def body(*refs):
        dh_refs = refs[:npairs]
        w_refs = refs[npairs:2 * npairs]
        dz_ref, x_ref, sc_ref, dx_ref, dsc_ref, dsh_ref = refs[2 * npairs:]
        du = _dot_nt(dh_refs[0][...], w_refs[0][...])
        for kk in range(1, npairs):
            du = du + _dot_nt(dh_refs[kk][...], w_refs[kk][...])
        _mod_bwd_tail(du, dz_ref, x_ref, sc_ref, dx_ref, dsc_ref, dsh_ref, pl.program_id(0) % tps == 0)

    rows = pl.BlockSpec((tm, d), lambda i: (i, 0))
    bvec = pl.BlockSpec((None, 1, d), lambda i: (i // tps, 0, 0))
    in_specs = [pl.BlockSpec((tm, dh.shape[1]), lambda i: (i, 0)) for dh, _ in pairs]
    in_specs += [pl.BlockSpec(w.shape, lambda i: (0, 0)) for _, w in pairs]
    in_specs += [rows, rows, bvec]
    return pl.pallas_call(
        body, name=name, grid=(t // tm,), in_specs=in_specs,
        out_specs=[rows, bvec, bvec],
        out_shape=[_sds((t, d), F32), _sds((nb, 1, d), F32), _sds((nb, 1, d), F32)],
        compiler_params=_cp(1),
    )(*[dh for dh, _ in pairs], *[w for _, w in pairs], dz, x_in, scale)


def linear_nt_delta(dy, w_o, o, head_sel, name):
    t, d = dy.shape
    hdv = w_o.shape[0]
    tm = TOKEN_TILE

    def body(dy_ref, w_ref, o_ref, sel_ref, do_ref, dl_ref):
        do = _dot_nt(dy_ref[...], w_ref[...])
        do_ref[...] = do.astype(BF16)
        dl_ref[...] = _dot_f32(do * o_ref[...].astype(F32), sel_ref[...])

    return pl.pallas_call(
        body, name=name, grid=(t // tm,),
        in_specs=[pl.BlockSpec((tm, d), lambda i: (i, 0)), pl.BlockSpec((hdv, d), lambda i: (0, 0)),
                  pl.BlockSpec((tm, hdv), lambda i: (i, 0)), pl.BlockSpec(head_sel.shape, lambda i: (0, 0))],
        out_specs=[pl.BlockSpec((tm, hdv), lambda i: (i, 0)), pl.BlockSpec((tm, 128), lambda i: (i, 0))],
        out_shape=[_sds((t, hdv), BF16), _sds((t, 128), F32)], compiler_params=_cp(1),
    )(dy, w_o, o, head_sel)


def _attn_bwd_blocks(j, nk, tk, q_fn, do_fn, k_cat, vb, row_fn, scale, bias_fn, sinks):
    add_dq, add_dfq = sinks
    dk_dim = k_cat.shape[1]
    dv_dim = vb.shape[1]

    def block(i, carry, masked):
        dk_acc, dv_acc, dfk_acc = carry
        qb = q_fn(i)
        dob = do_fn(i)
        lse_row, dl_row = row_fn(i)
        st = _dot_nt(k_cat, qb) * scale
        if bias_fn is not None:
            fq_row, fk_col = bias_fn(i)
            st = st + fq_row - fk_col
        if masked:
            keep = lax.broadcasted_iota(jnp.int32, st.shape, 1) >= lax.broadcasted_iota(jnp.int32, st.shape, 0)
            st = jnp.where(keep, st, -1e30)
        pt = jnp.exp(st - lse_row)
        dv_acc = dv_acc + _dot(pt.astype(BF16), dob)
        dst = pt * (_dot_nt(vb, dob) - dl_row)
        if add_dfq is not None:
            dfk_acc = dfk_acc - jnp.sum(dst, axis=1, keepdims=True)
            add_dfq(i, jnp.sum(dst, axis=0, keepdims=True))
        dsb = (dst * scale).astype(BF16)
        dk_acc = dk_acc + _dot(dsb, qb)
        add_dq(i, _dot_tn(dsb, k_cat))
        return dk_acc, dv_acc, dfk_acc

    init = (jnp.zeros((tk, dk_dim), F32), jnp.zeros((tk, dv_dim), F32), jnp.zeros((tk, 1), F32))
    carry = block(j, init, True)
    return lax.fori_loop(j + 1, nk, lambda i, c: block(i, c, False), carry)


def fox_attn_bwd(qkv, do, cum, cum_rows, lse_rows, delta_rows, nb, name):
    t = qkv.shape[0]
    s = t // nb
    tk = ATTN_TILE
    nk = s // tk
    npairs = FOX_HEADS // 2
    scale = FOX_HD ** -0.5

    def body(q_ref, k_ref, v_ref, do_ref, cum_ref, cr_ref, lr_ref, dr_ref, dq_ref, dk_ref, dv_ref, dfq_ref, dfk_ref):
        hp = pl.program_id(1)
        j = pl.program_id(2)

        @pl.when(j == 0)
        def _():
            dq_ref[...] = jnp.zeros_like(dq_ref)

        @pl.when((j == 0) & (hp == 0))
        def _():
            dfq_ref[...] = jnp.zeros_like(dfq_ref)
            dfk_ref[...] = jnp.zeros_like(dfk_ref)

        kb = k_ref[...]
        vb = v_ref[...]
        low = lax.broadcasted_iota(jnp.int32, kb.shape, 1) < FOX_HD
        cum_t = cum_ref[...]
        dks, dvs = [], []
        for a in (0, 1):
            head = 2 * hp + a
            half = low if a == 0 else jnp.logical_not(low)
            ka = jnp.where(half, kb, jnp.zeros_like(kb))
            va = jnp.where(half, vb, jnp.zeros_like(vb))
            fk = _pick_lane(cum_t, head)

            def rows_of(i):
                return pl.ds(pl.multiple_of(i * tk, tk), tk)

            def add_dq(i, val):
                dq_ref[rows_of(i), :] += val

            def add_dfq(i, val, head=head):
                dfq_ref[i] = _put_row(dfq_ref[i], head, val)

            dk_a, dv_a, dfk_a = _attn_bwd_blocks(
                j, nk, tk, lambda i: q_ref[rows_of(i), :], lambda i: do_ref[rows_of(i), :], ka, va,
                lambda i, head=head: (_pick_row(lr_ref[i], head), _pick_row(dr_ref[i], head)), scale,
                lambda i, head=head, fk=fk: (_pick_row(cr_ref[i], head), fk), (add_dq, add_dfq))
            dks.append(dk_a)
            dvs.append(dv_a)
            dfk_row = jnp.broadcast_to(dfk_a, (tk, 128)).T[0:1, :]
            dfk_ref[j] = _put_row(dfk_ref[j], head, dfk_row)
        dk_ref[...] = jnp.where(low, dks[0], dks[1]).astype(BF16)
        dv_ref[...] = jnp.where(low, dvs[0], dvs[1]).astype(BF16)

    rowsp = pl.BlockSpec((nk, 16, tk), lambda b, hp, j: (b, 0, 0))
    return pl.pallas_call(
        body, name=name, grid=(nb, npairs, nk),
        in_specs=[pl.BlockSpec((s, 128), lambda b, hp, j: (b, hp)),
                  pl.BlockSpec((tk, 128), lambda b, hp, j: (b * nk + j, npairs + hp)),
                  pl.BlockSpec((tk, 128), lambda b, hp, j: (b * nk + j, 2 * npairs + hp)),
                  pl.BlockSpec((s, 128), lambda b, hp, j: (b, hp)),
                  pl.BlockSpec((tk, 128), lambda b, hp, j: (b * nk + j, 0)),
                  rowsp, rowsp, rowsp],
        out_specs=[pl.BlockSpec((s, 128), lambda b, hp, j: (b, hp)),
                   pl.BlockSpec((tk, 128), lambda b, hp, j: (b * nk + j, hp)),
                   pl.BlockSpec((tk, 128), lambda b, hp, j: (b * nk + j, hp)),
                   rowsp, rowsp],
        out_shape=[_sds((t, D_MODEL), F32), _sds((t, D_MODEL), BF16), _sds((t, D_MODEL), BF16),
                   _sds((t // tk, 16, tk), F32), _sds((t // tk, 16, tk), F32)],
        compiler_params=_cp(3),
    )(qkv, qkv, qkv, do, cum, cum_rows, lse_rows, delta_rows)


def mla_attn_bwd(q, kn, kr2, v, do, lse_rows, delta_rows, nb, name):
    t = q.shape[0]
    s = t // nb
    tk = ATTN_TILE
    nk = s // tk
    scale = (MLA_NOPE + MLA_ROPE) ** -0.5

    def body(qn_ref, qr_ref, kn_ref, kr_ref, v_ref, do_ref, lr_ref, dr_ref, dqn_ref, dqr_ref, dkn_ref, dkr_ref, dv_ref):
        h = pl.program_id(1)
        j = pl.program_id(2)

        @pl.when(j == 0)
        def _():
            dqn_ref[...] = jnp.zeros_like(dqn_ref)
            dqr_ref[...] = jnp.zeros_like(dqr_ref)

        k_cat = jnp.concatenate([kn_ref[...], kr_ref[...]], axis=1)
        mine = (lax.broadcasted_iota(jnp.int32, (tk, 128), 1) // MLA_ROPE) == (h % 2)

        def rows_of(i):
            return pl.ds(pl.multiple_of(i * tk, tk), tk)

        def q_fn(i):
            qr = qr_ref[rows_of(i), :]
            return jnp.concatenate([qn_ref[rows_of(i), :], jnp.where(mine, qr, jnp.zeros_like(qr))], axis=1)

        def add_dq(i, val):
            dqn_ref[rows_of(i), :] += val[:, :MLA_NOPE]
            dqr_ref[rows_of(i), :] += val[:, MLA_NOPE:]

        dk_acc, dv_acc, _ = _attn_bwd_blocks(
            j, nk, tk, q_fn, lambda i: do_ref[rows_of(i), :], k_cat, v_ref[...],
            lambda i: (_pick_row(lr_ref[i], h), _pick_row(dr_ref[i], h)), scale, None, (add_dq, None))
        dkn_ref[...] = dk_acc[:, :MLA_NOPE].astype(BF16)
        dkr_ref[...] = dk_acc[:, MLA_NOPE:].astype(BF16)
        dv_ref[...] = dv_acc.astype(BF16)

    full = pl.BlockSpec((s, 128), lambda b, h, j: (b, h))
    blk = pl.BlockSpec((tk, 128), lambda b, h, j: (b * nk + j, h))
    rowsp = pl.BlockSpec((nk, 16, tk), lambda b, h, j: (b, 0, 0))
    wide = MLA_HEADS * MLA_V
    return pl.pallas_call(
        body, name=name, grid=(nb, MLA_HEADS, nk),
        in_specs=[full, pl.BlockSpec((s, 128), lambda b, h, j: (b, MLA_HEADS + h // 2)), blk,
                  pl.BlockSpec((tk, 128), lambda b, h, j: (b * nk + j, 0)), blk, full, rowsp, rowsp],
        out_specs=[full, full, blk, blk, blk],
        out_shape=[_sds((t, wide), F32), _sds((t, wide), F32), _sds((t, wide), BF16), _sds((t, wide), BF16),
                   _sds((t, wide), BF16)],
        compiler_params=_cp(3),
    )(q, q, kn, kr2, v, do, lse_rows, delta_rows)


def mla_mid_bwd(dqn, dqr, dkn, dv, dkr_heads, h, g_q, g_kv, w_uq, w_uk, w_uv, cos8, sin8, cos64, sin64s, swap64,
                heads_to_rope, head_sum, name):
    t = h.shape[0]
    tm = TOKEN_TILE
    hq = MLA_HEADS * MLA_NOPE
    hr = MLA_HEADS * MLA_ROPE // 2
    nq = w_uq.shape[1]

    def body(dqn_ref, dqr_ref, dkn_ref, dv_ref, dkr_ref, h_ref, gq_ref, gkv_ref, wuq_ref, wuk_ref, wuv_ref,
             c8_ref, s8_ref, c64_ref, s64_ref, sw_ref, hp_ref, hs_ref, dh_ref, dqp_ref, dgq_ref, dgkv_ref):
        @pl.when(pl.program_id(0) == 0)
        def _():
            dgq_ref[...] = jnp.zeros_like(dgq_ref)
            dgkv_ref[...] = jnp.zeros_like(dgkv_ref)

        drot = _dot(dqr_ref[...].astype(BF16), hp_ref[...])
        o1 = drot[:, :hr]
        o2 = drot[:, hr:]
        cs = c8_ref[...]
        sn = s8_ref[...]
        dqp = jnp.concatenate([dqn_ref[...].astype(BF16), (o1 * cs + o2 * sn).astype(BF16),
                               (o2 * cs - o1 * sn).astype(BF16)], axis=1)
        dqp_ref[...] = dqp
        dcq = _dot_nt(dqp, wuq_ref[...])
        dckv = _dot_nt(dkn_ref[...], wuk_ref[...]) + _dot_nt(dv_ref[...], wuv_ref[...])
        hh = h_ref[...]

        def rms_bwd(hpart, g, dc, dg_ref):
            hhat, rstd = _rms(hpart, None)
            dg_ref[...] += jnp.sum(dc * hhat, axis=0, keepdims=True)
            dcg = dc * g
            return rstd * (dcg - hhat * jnp.mean(dcg * hhat, axis=-1, keepdims=True))

        dhq = rms_bwd(hh[:, :MLA_QR], gq_ref[...], dcq, dgq_ref)
        dhkv = rms_bwd(hh[:, MLA_QR:MLA_QR + MLA_KVR], gkv_ref[...], dckv, dgkv_ref)
        dkr = _dot(dkr_ref[...], hs_ref[...])
        dkr_pre = dkr * c64_ref[...] + _dot_f32(dkr * s64_ref[...], sw_ref[...])
        dh_ref[...] = jnp.concatenate([dhq, dhkv, dkr_pre], axis=1).astype(BF16)

    def rows(n):
        return pl.BlockSpec((tm, n), lambda i: (i, 0))

    def whole(a):
        return pl.BlockSpec(a.shape, lambda i: (0,) * a.ndim)

    return pl.pallas_call(
        body, name=name, grid=(t // tm,),
        in_specs=[rows(hq), rows(hq), rows(hq), rows(hq), rows(hq), rows(h.shape[1]), whole(g_q), whole(g_kv),
                  whole(w_uq), whole(w_uk), whole(w_uv), rows(hr), rows(hr), rows(MLA_ROPE), rows(MLA_ROPE),
                  whole(swap64), whole(heads_to_rope), whole(head_sum)],
        out_specs=[rows(h.shape[1]), rows(nq), pl.BlockSpec((1, MLA_QR), lambda i: (0, 0)),
                   pl.BlockSpec((1, MLA_KVR), lambda i: (0, 0))],
        out_shape=[_sds((t, h.shape[1]), BF16), _sds((t, nq), BF16), _sds((1, MLA_QR), F32), _sds((1, MLA_KVR), F32)],
        compiler_params=_cp(1),
    )(dqn, dqr, dkn, dv, dkr_heads, h, g_q, g_kv, w_uq, w_uk, w_uv, cos8, sin8, cos64, sin64s, swap64,
      heads_to_rope, head_sum)


def fox_gate_bwd(dcum, hf, b_f, triu, n_batch, name):
    t, n = hf.shape
    blk = triu.shape[0]
    nb = (t // n_batch) // blk

    def body(dc_ref, hf_ref, b_ref, tri_ref, o_ref, db_ref, carry_ref):
        @pl.when(pl.program_id(1) == 0)
        def _():
            carry_ref[...] = jnp.zeros_like(carry_ref)

        @pl.when((pl.program_id(0) == 0) & (pl.program_id(1) == 0))
        def _():
            db_ref[...] = jnp.zeros_like(db_ref)

        rc = _dot_f32(tri_ref[...], dc_ref[...]) + carry_ref[...]
        carry_ref[...] = rc[0:1, :]
        dhf = rc * jax.nn.sigmoid(-(hf_ref[...] + b_ref[...]))
        o_ref[...] = dhf.astype(BF16)
        db_ref[...] += jnp.sum(dhf, axis=0, keepdims=True)

    rev = pl.BlockSpec((blk, n), lambda bb, i: (bb * nb + nb - 1 - i, 0))
    return pl.pallas_call(
        body, name=name, grid=(n_batch, nb),
        in_specs=[rev, rev, pl.BlockSpec((1, n), lambda bb, i: (0, 0)), pl.BlockSpec((blk, blk), lambda bb, i: (0, 0))],
        out_specs=[rev, pl.BlockSpec((1, n), lambda bb, i: (0, 0))],
        out_shape=[_sds((t, n), BF16), _sds((1, n), F32)], scratch_shapes=[pltpu.VMEM((1, n), F32)],
        compiler_params=_cp(2),
    )(dcum, hf, b_f, triu)


def wgrad(a, bm, name, slot=None, bt=512):
    ca, t, kd = a.shape
    cb, _, nd = bm.shape
    c = max(ca, cb)
    bn = nd
    if nd > 1024 and nd % 1024 == 0:
        bn = 1024

    def body(*refs):
        a_ref, b_ref, o_ref = refs[0], refs[1], refs[-1]

        @pl.when(pl.program_id(2) == 0)
        def _():
            o_ref[...] = jnp.zeros_like(o_ref)

        o_ref[...] += _dot_tn(a_ref[...].astype(BF16), b_ref[...].astype(BF16))

    in_specs = [pl.BlockSpec((None, bt, kd), lambda cc, n, tt: (cc if ca > 1 else 0, tt, 0)),
                pl.BlockSpec((None, bt, bn), lambda cc, n, tt: (cc if cb > 1 else 0, tt, n))]
    args = [a, bm]
    aliases = {}
    if slot is None:
        out_spec = pl.BlockSpec((None, kd, bn), lambda cc, n, tt: (cc, 0, n))
        out_shape = _sds((c, kd, nd), F32)
    else:
        layer, n_layers, buf = slot
        out_spec = pl.BlockSpec((None, None, kd, bn), lambda cc, n, tt: (cc, layer, 0, n))
        out_shape = _sds((c, n_layers, kd, nd), F32)
        if buf is not None:
            in_specs.append(pl.BlockSpec(memory_space=pl.ANY))
            args.append(buf)
            aliases = {2: 0}
    return pl.pallas_call(
        body, name=name, grid=(c, nd // bn, t // bt), in_specs=in_specs, out_specs=out_spec, out_shape=out_shape,
        input_output_aliases=aliases, compiler_params=_cp(3),
    )(*args)


def ada_mod_part(c_all, ada_w, name):
    nl, d, n = ada_w.shape
    rows = c_all.shape[0]
    tn = 512

    def body(c_ref, w_ref, o_ref):
        cv = c_ref[...]
        act = (cv * jax.nn.sigmoid(cv)).astype(BF16)
        o_ref[...] = _dot(act, w_ref[...].astype(BF16))

    return pl.pallas_call(
        body, name=name, grid=(nl, n // tn),
        in_specs=[pl.BlockSpec((rows, d), lambda l, j: (0, 0)), pl.BlockSpec((None, d, tn), lambda l, j: (l, 0, j))],
        out_specs=pl.BlockSpec((None, rows, tn), lambda l, j: (l, 0, j)),
        out_shape=_sds((nl, rows, n), F32), compiler_params=_cp(2),
    )(c_all, ada_w)


def ada_grad(c_all_t, dmod, name):
    nl, rows, n = dmod.shape
    d = c_all_t.shape[0]
    tn = 512

    def body(c_ref, dm_ref, o_ref):
        cv = c_ref[...]
        act = (cv * jax.nn.sigmoid(cv)).astype(BF16)
        o_ref[...] = _dot(act, dm_ref[...].astype(BF16))

    return pl.pallas_call(
        body, name=name, grid=(nl, n // tn),
        in_specs=[pl.BlockSpec((d, rows), lambda l, j: (0, 0)), pl.BlockSpec((None, rows, tn), lambda l, j: (l, 0, j))],
        out_specs=pl.BlockSpec((None, d, tn), lambda l, j: (l, 0, j)),
        out_shape=_sds((nl, d, n), F32), compiler_params=_cp(2),
    )(c_all_t, dmod)


def sum_leading(a, name):
    g, r, n = a.shape

    def body(a_ref, o_ref):
        acc = a_ref[0]
        for kk in range(1, g):
            acc = acc + a_ref[kk]
        o_ref[...] = acc

    return pl.pallas_call(
        body, name=name, grid=(1,), in_specs=[pl.BlockSpec((g, r, n), lambda i: (0, 0, 0))],
        out_specs=pl.BlockSpec((r, n), lambda i: (0, 0)), out_shape=_sds((r, n), F32), compiler_params=_cp(1),
    )(a)


def adamw(w, g, m, v, name):
    r, n = w.shape
    br = r
    for cand in (512, 256, 128, 64, 32, 16, 8):
        if r % cand == 0 and r > cand and cand * n * 4 <= ADAMW_BLOCK_BYTES:
            br = cand
            break
    c1 = 1.0 - ADAM_B1 ** ADAM_STEP
    c2 = 1.0 - ADAM_B2 ** ADAM_STEP

    def body(w_ref, g_ref, m_ref, v_ref, d_ref, mo_ref, vo_ref):
        gv = g_ref[...]
        mn = ADAM_B1 * m_ref[...] + (1.0 - ADAM_B1) * gv
        vn = ADAM_B2 * v_ref[...] + (1.0 - ADAM_B2) * (gv * gv)
        mo_ref[...] = mn
        vo_ref[...] = vn
        d_ref[...] = -ADAM_LR * ((mn / c1) / (jnp.sqrt(vn / c2) + ADAM_EPS) + ADAM_WD * w_ref[...])

    spec = pl.BlockSpec((br, n), lambda i: (i, 0))
    return pl.pallas_call(
        body, name=name, grid=(r // br,), in_specs=[spec] * 4, out_specs=[spec] * 3,
        out_shape=[_sds((r, n), F32)] * 3, compiler_params=_cp(1),
    )(w, g, m, v)


def _place():
    return lax.axis_index("x"), lax.axis_index("y"), lax.axis_index("c")


def all_gather8(x_blk, name):
    m_per, n = x_blk.shape

    def body(x_ref, out_ref, send_sems, recv_sems, local_sem):
        x, y, c = _place()
        me, sibling = (x, y, c), (x, y, 1 - c)
        chips = [(1 - x, y), (x, 1 - y), (1 - x, 1 - y)]

        def rows(px, py, pc):
            return out_ref.at[pl.ds((4 * px + 2 * py + pc) * m_per, m_per), :]

        def copy(k, block, to, src=None):
            return pltpu.make_async_remote_copy(
                src_ref=rows(*block) if src is None else src, dst_ref=rows(*block),
                send_sem=send_sems.at[k], recv_sem=recv_sems.at[k], device_id=to, device_id_type=MESH)

        mine = pltpu.make_async_copy(x_ref, rows(*me), local_sem)
        mine.start()
        first = [copy(0, me, sibling, src=x_ref)]
        first += [copy(1 + j, me, (*chip, c), src=x_ref) for j, chip in enumerate(chips)]
        for cp in first:
            cp.start()
        passed = [copy(4 + j, (*chip, c), sibling) for j, chip in enumerate(chips)]
        for j, chip in enumerate(chips):
            copy(1 + j, (*chip, c), me).wait_recv()
            passed[j].start()
        copy(0, sibling, me).wait_recv()
        for j, chip in enumerate(chips):
            copy(4 + j, (*chip, 1 - c), me).wait_recv()
        for cp in first + passed:
            cp.wait_send()
        mine.wait()

    return pl.pallas_call(
        body, name=name, out_shape=_sds((8 * m_per, n), x_blk.dtype),
        in_specs=[pl.BlockSpec(memory_space=pltpu.VMEM)], out_specs=pl.BlockSpec(memory_space=pltpu.VMEM),
        scratch_shapes=[pltpu.SemaphoreType.DMA((7,)), pltpu.SemaphoreType.DMA((7,)), pltpu.SemaphoreType.DMA],
        compiler_params=pltpu.CompilerParams(vmem_limit_bytes=VMEM_LIMIT),
    )(x_blk)


def all_gather_chips(shards, name):
    nt = len(shards)

    def body(*refs):
        w_refs, out_refs = refs[:nt], refs[nt:2 * nt]
        send_sems, recv_sems, own_send, own_recv = refs[2 * nt:]
        x, y, c = _place()
        sibling = (x, y, 1 - c)
        chips = [(1 - x, y), (x, 1 - y), (1 - x, 1 - y)]

        def copy(t, k, block, to, src=None):
            px, py, hh = block
            dst = out_refs[t].at[2 * px + py, hh]
            return pltpu.make_async_remote_copy(
                src_ref=dst if src is None else src, dst_ref=dst,
                send_sem=send_sems.at[6 * t + k], recv_sem=recv_sems.at[6 * t + k], device_id=to, device_id_type=MESH)

        def own(t):
            return pltpu.make_async_remote_copy(
                src_ref=w_refs[t], dst_ref=out_refs[t].at[2 * x + y], send_sem=own_send.at[t], recv_sem=own_recv.at[t],
                device_id=sibling, device_id_type=MESH)

        mine = [own(t) for t in range(nt)]
        for cp in mine:
            cp.start()
        first = [copy(t, j, (x, y, c), (*chip, c), src=w_refs[t].at[c]) for t in range(nt) for j, chip in enumerate(chips)]
        for cp in first:
            cp.start()
        passed = []
        for t in range(nt):
            for j, chip in enumerate(chips):
                copy(t, j, (*chip, c), (x, y, c)).wait_recv()
                fwd = copy(t, 3 + j, (*chip, c), sibling)
                fwd.start()
                passed.append(fwd)
        for t in range(nt):
            for j, chip in enumerate(chips):
                copy(t, 3 + j, (*chip, 1 - c), (x, y, c)).wait_recv()
        for cp in first + passed:
            cp.wait_send()
        for cp in mine:
            cp.wait()

    hbm = pl.BlockSpec(memory_space=pl.ANY)
    return pl.pallas_call(
        body, name=name, out_shape=[_sds((N_CHIPS, *w.shape), w.dtype) for w in shards],
        in_specs=[hbm] * nt, out_specs=[hbm] * nt,
        scratch_shapes=[pltpu.SemaphoreType.DMA((6 * nt,)), pltpu.SemaphoreType.DMA((6 * nt,)),
                        pltpu.SemaphoreType.DMA((nt,)), pltpu.SemaphoreType.DMA((nt,))],
    )(*shards)


def sibling_swap_halves(grads, name):
    nt = len(grads)

    def body(*refs):
        g_refs, a_refs = refs[:nt], refs[nt:2 * nt]
        send_sems, recv_sems = refs[2 * nt:]
        x, y, c = _place()
        cps = [pltpu.make_async_remote_copy(
            src_ref=g_refs[t].at[j, 1 - c], dst_ref=a_refs[t].at[j], send_sem=send_sems.at[N_CHIPS * t + j],
            recv_sem=recv_sems.at[N_CHIPS * t + j], device_id=(x, y, 1 - c), device_id_type=MESH)
            for t in range(nt) for j in range(N_CHIPS)]
        for cp in cps:
            cp.start()
        for cp in cps:
            cp.wait()

    hbm = pl.BlockSpec(memory_space=pl.ANY)
    return pl.pallas_call(
        body, name=name, out_shape=[_sds((N_CHIPS, *g.shape[2:]), g.dtype) for g in grads],
        in_specs=[hbm] * nt, out_specs=[hbm] * nt,
        scratch_shapes=[pltpu.SemaphoreType.DMA((N_CHIPS * nt,)), pltpu.SemaphoreType.DMA((N_CHIPS * nt,))],
    )(*grads)


def _row_block(r, n, itemsize):
    best = None
    for br in range(16, r + 1, 16):
        if r % br == 0 and br * n * itemsize <= COMM_BLOCK_BYTES:
            best = br
    assert best is not None, (r, n)
    return best


def add_own_half(g, recv, core, name):
    nch, _, r, n = g.shape
    br = _row_block(r, n, 4)

    def body(c_ref, g_ref, a_ref, o_ref):
        o_ref[...] = (g_ref[...] + a_ref[...]).astype(BF16)

    return pl.pallas_call(
        body, name=name,
        grid_spec=pltpu.PrefetchScalarGridSpec(
            num_scalar_prefetch=1, grid=(nch, r // br),
            in_specs=[pl.BlockSpec((None, None, br, n), lambda j, rr, cref: (j, cref[0], rr, 0)),
                      pl.BlockSpec((None, br, n), lambda j, rr, cref: (j, rr, 0))],
            out_specs=pl.BlockSpec((None, br, n), lambda j, rr, cref: (j, rr, 0))),
        out_shape=_sds((nch, r, n), BF16), compiler_params=_cp(2),
    )(core, g, recv)


def chip_exchange(sums, name):
    nt = len(sums)

    def body(*refs):
        s_refs, b_refs = refs[:nt], refs[nt:2 * nt]
        send_sems, recv_sems = refs[2 * nt:]
        x, y, c = _place()
        k = 2 * x + y
        chips = [(1 - x, y), (x, 1 - y), (1 - x, 1 - y)]

        def copy(t, j, src_idx, dst_idx):
            px, py = chips[j]
            return pltpu.make_async_remote_copy(
                src_ref=s_refs[t].at[src_idx], dst_ref=b_refs[t].at[dst_idx], send_sem=send_sems.at[3 * t + j],
                recv_sem=recv_sems.at[3 * t + j], device_id=(px, py, c), device_id_type=MESH)

        cps = [copy(t, j, 2 * chips[j][0] + chips[j][1], k) for t in range(nt) for j in range(3)]
        for cp in cps:
            cp.start()
        for t in range(nt):
            for j in range(3):
                copy(t, j, k, 2 * chips[j][0] + chips[j][1]).wait_recv()
        for cp in cps:
            cp.wait_send()

    hbm = pl.BlockSpec(memory_space=pl.ANY)
    return pl.pallas_call(
        body, name=name, out_shape=[_sds(sm.shape, sm.dtype) for sm in sums],
        in_specs=[hbm] * nt, out_specs=[hbm] * nt,
        scratch_shapes=[pltpu.SemaphoreType.DMA((3 * nt,)), pltpu.SemaphoreType.DMA((3 * nt,))],
    )(*sums)


def sum_pieces(own, recv, place, name):
    nch, r, n = own.shape
    br = _row_block(r, n, 4 * nch)

    def body(p_ref, o_ref, r1_ref, r2_ref, r3_ref, out_ref):
        out_ref[...] = ((o_ref[...].astype(F32) + r1_ref[...].astype(F32)) + r2_ref[...].astype(F32)) + r3_ref[...].astype(F32)

    def piece(step):
        return pl.BlockSpec((None, br, n), lambda i, pref: ((pref[1] + step) % nch, i, 0))

    return pl.pallas_call(
        body, name=name,
        grid_spec=pltpu.PrefetchScalarGridSpec(
            num_scalar_prefetch=1, grid=(r // br,), in_specs=[piece(0), piece(1), piece(2), piece(3)],
            out_specs=pl.BlockSpec((None, br, n), lambda i, pref: (pref[0], i, 0))),
        out_shape=_sds((2, r, n), F32), compiler_params=_cp(1),
    )(place, own, recv, recv, recv)


def sibling_join_halves(halves, name):
    nt = len(halves)

    def body(*refs):
        o_refs = refs[nt:2 * nt]
        send_sems, recv_sems = refs[2 * nt:]
        x, y, c = _place()

        def copy(t, hh):
            return pltpu.make_async_remote_copy(
                src_ref=o_refs[t].at[hh], dst_ref=o_refs[t].at[hh], send_sem=send_sems.at[t], recv_sem=recv_sems.at[t],
                device_id=(x, y, 1 - c), device_id_type=MESH)

        cps = [copy(t, c) for t in range(nt)]
        for cp in cps:
            cp.start()
        for t in range(nt):
            copy(t, 1 - c).wait_recv()
        for cp in cps:
            cp.wait_send()

    hbm = pl.BlockSpec(memory_space=pl.ANY)
    return pl.pallas_call(
        body, name=name, out_shape=[_sds(f.shape, f.dtype) for f in halves],
        in_specs=[hbm] * nt, out_specs=[hbm] * nt, input_output_aliases={t: t for t in range(nt)},
        scratch_shapes=[pltpu.SemaphoreType.DMA((nt,)), pltpu.SemaphoreType.DMA((nt,))],
    )(*halves)


_SHARD_KIND = {"mla_w_in": "rows", "mla_w_uq": "cols", "mla_w_uk": "cols", "mla_w_uv": "cols", "mla_w_o": "rows",
               "fox_w_in": "cols", "fox_w_o": "rows", "ffn_w_gate": "chunk", "ffn_w_up": "chunk", "ffn_w_down": "chunk"}
_PACKED = tuple(_SHARD_KIND)


def _halves(shard):
    if shard.ndim == 3 and shard.shape[0] == 2:
        return shard
    r, n = shard.shape[-2:]
    return shard.reshape(2, r // 2, n)


def _cols_to_full(g):
    return jnp.transpose(g, (1, 0, 2)).reshape(g.shape[1], -1)


def _full_to_cols(w):
    k, n4 = w.shape
    return jnp.transpose(w.reshape(k, N_CHIPS, n4 // N_CHIPS), (1, 0, 2))


def _uq_perm():
    per = MLA_NOPE + MLA_ROPE
    half = MLA_ROPE // 2
    nope = [h * per + d for h in range(MLA_HEADS) for d in range(MLA_NOPE)]
    r1 = [h * per + MLA_NOPE + r for h in range(MLA_HEADS) for r in range(half)]
    r2 = [h * per + MLA_NOPE + half + r for h in range(MLA_HEADS) for r in range(half)]
    perm = np.array(nope + r1 + r2, dtype=np.int32)
    return perm, np.argsort(perm).astype(np.int32)


def _rope_matrices():
    half = MLA_ROPE // 2
    nr = MLA_HEADS * MLA_ROPE
    to_heads = np.zeros((nr, nr), np.float32)
    from_heads = np.zeros((MLA_HEADS * 128, nr), np.float32)
    for e in range(2):
        for h in range(MLA_HEADS):
            for r in range(half):
                to_heads[e * MLA_HEADS * half + h * half + r, h * MLA_ROPE + e * half + r] = 1.0
                from_heads[h * 128 + e * half + r, e * MLA_HEADS * half + h * half + r] = 1.0
    head_sum = np.tile(np.eye(MLA_ROPE, dtype=np.float32), (2 * MLA_HEADS, 1))
    dup = np.concatenate([np.eye(MLA_ROPE, dtype=np.float32)] * 2, axis=1)
    return to_heads, from_heads, head_sum, dup


def _local_step(x, positions, target, mods, wts, ln_g, ln_b, mla_g_q, mla_g_kv, fox_b_f):
    nb, s, d = x.shape
    t = nb * s
    x0 = x.reshape(t, d)
    tgt = target.reshape(t, d)
    perm, inv_perm = _uq_perm()

    half = MLA_ROPE // 2
    inv_freq = ROPE_THETA ** (-jnp.arange(half, dtype=F32) / half)
    ang = positions.astype(F32).reshape(t, 1) * inv_freq
    cos, sin = jnp.cos(ang), jnp.sin(ang)
    cos8, sin8 = jnp.tile(cos, (1, MLA_HEADS)), jnp.tile(sin, (1, MLA_HEADS))
    cos64 = jnp.concatenate([cos, cos], axis=1)
    sin64s = jnp.concatenate([-sin, sin], axis=1)
    swap64 = jnp.asarray(np.roll(np.eye(MLA_ROPE, dtype=np.float32), half, axis=1))
    to_heads, from_heads, head_sum, dup = _rope_matrices()
    to_heads, from_heads = jnp.asarray(to_heads, dtype=BF16), jnp.asarray(from_heads, dtype=BF16)
    head_sum, dup = jnp.asarray(head_sum, dtype=BF16), jnp.asarray(dup, dtype=BF16)
    sel_mla = jnp.asarray(np.pad(np.kron(np.eye(MLA_HEADS, dtype=np.float32), np.ones((MLA_V, 1), np.float32)),
                                 ((0, 0), (0, 128 - MLA_HEADS))))
    sel_fox = jnp.asarray(np.pad(np.kron(np.eye(FOX_HEADS, dtype=np.float32), np.ones((FOX_HD, 1), np.float32)),
                                 ((0, 0), (0, 128 - FOX_HEADS))))
    tri = jnp.asarray(np.tril(np.ones((128, 128), np.float32)))
    triu = jnp.asarray(np.triu(np.ones((128, 128), np.float32)))
    onehot16 = jnp.asarray(np.eye(16, 128, dtype=np.float32))

    def vec(a):
        return a.reshape(1, -1)

    w_uq_p = wts["mla_w_uq"][:, perm]
    fox_w_qkv = wts["fox_w_in"][:, :3 * d]
    fox_w_f = jnp.pad(wts["fox_w_in"][:, 3 * d:], ((0, 0), (0, 128 - FOX_HEADS)))
    b_f_pad = jnp.pad(fox_b_f.reshape(1, -1), ((0, 0), (0, 128 - FOX_HEADS)))

    sh_a, sc_a, gt_a, sh_f, sc_f, gt_f = mods[0]
    h_in, u_m = mod_linear(x0, sh_a, sc_a, wts["mla_w_in"], F32, "mla_in", emit_u=True)
    q_m, kn_m, v_m, kr2_m, cq_m, ckv_m = mla_mid_fwd(
        h_in, vec(mla_g_q), vec(mla_g_kv), w_uq_p, wts["mla_w_uk"], wts["mla_w_uv"], cos8, sin8, cos64, sin64s, swap64,
        to_heads, dup, "mla_mid")
    o_m, lse_m = mla_attn_fwd(q_m, kn_m, kr2_m, v_m, nb, "mla_attn")
    y0, x1 = linear_resid_ln(o_m, wts["mla_w_o"], x0, gt_a, vec(ln_g[0, 0]), vec(ln_b[0, 0]), "mla_out")
    u_f0, hg0, hu0, y1, x2 = ffn_fwd(x1, sh_f, sc_f, gt_f, wts["ffn_w_gate"], wts["ffn_w_up"], wts["ffn_w_down"], 0,
                                     vec(ln_g[0, 1]), vec(ln_b[0, 1]), "ffn0")
    sh_a1, sc_a1, gt_a1, sh_f1, sc_f1, gt_f1 = mods[1]
    qkv, u_x = mod_linear(x2, sh_a1, sc_a1, fox_w_qkv, BF16, "fox_qkv", tn=1024, emit_u=True)
    hf = mod_linear(x2, sh_a1, sc_a1, fox_w_f, F32, "fox_f")
    cum = fox_gate_fwd(hf, b_f_pad, tri, nb, "fox_gate")
    cum_rows = rows16(cum, "fox_cum_rows")
    o_x, lse_x = fox_attn_fwd(qkv, cum, cum_rows, nb, "fox_attn")
    y2, x3 = linear_resid_ln(o_x, wts["fox_w_o"], x2, gt_a1, vec(ln_g[1, 0]), vec(ln_b[1, 0]), "fox_out")
    u_f1, hg1, hu1, y3, x4 = ffn_fwd(x3, sh_f1, sc_f1, gt_f1, wts["ffn_w_gate"], wts["ffn_w_up"], wts["ffn_w_down"], 1,
                                     vec(ln_g[1, 1]), vec(ln_b[1, 1]), "ffn1")
    dx4, sq_err = loss_grad(x4, tgt, "loss")
    loss_part = 0.5 * jnp.sum(sq_err) / d

    grads = {}
    dz3, dy3, dg11, db11, dgt_f1 = ln_bwd(dx4, x3, y3, gt_f1, vec(ln_g[1, 1]), "ffn1_ln_bwd")
    dhg1, dhu1, act1, dx3, dsc_f1, dsh_f1 = ffn_bwd(dy3, hg1, hu1, wts["ffn_w_gate"], wts["ffn_w_up"],
                                                    wts["ffn_w_down"], 1, dz3, x3, sc_f1, "ffn1_bwd")
    dwg1 = wgrad(u_f1[None], dhg1, "ffn1_dwg", slot=(1, DEPTH, None))
    dwu1 = wgrad(u_f1[None], dhu1, "ffn1_dwu", slot=(1, DEPTH, None))
    dwd1 = wgrad(act1, dy3[None], "ffn1_dwd", slot=(1, DEPTH, None))
    dz2, dy2, dg10, db10, dgt_a1 = ln_bwd(dx3, x2, y2, gt_a1, vec(ln_g[1, 0]), "fox_ln_bwd")
    do_x, delta_x = linear_nt_delta(dy2, wts["fox_w_o"], o_x, sel_fox, "fox_out_bwd")
    grads["fox_w_o"] = wgrad(o_x[None], dy2[None], "fox_dwo")[0]
    dq_x, dk_x, dv_x, dfq_x, dfk_x = fox_attn_bwd(qkv, do_x, cum, cum_rows, rows16(lse_x, "fox_lse_rows"),
                                                  rows16(delta_x, "fox_delta_rows"), nb, "fox_attn_bwd")
    dcum = tokens128(dfq_x + dfk_x, onehot16, "fox_dcum")
    dhf, dbf = fox_gate_bwd(dcum, hf, b_f_pad, triu, nb, "fox_gate_bwd")
    dqkv = jnp.concatenate([dq_x.astype(BF16), dk_x, dv_x], axis=1)
    dx2, dsc_a1, dsh_a1 = linear_nt_mod_bwd([(dqkv, fox_w_qkv), (dhf, fox_w_f)], dz2, x2, sc_a1, "fox_in_bwd")
    dw_qkv = wgrad(u_x[None], dqkv[None], "fox_dwqkv")[0]
    dw_f = wgrad(u_x[None], dhf[None], "fox_dwf")[0]
    grads["fox_w_in"] = jnp.concatenate([dw_qkv, dw_f[:, :FOX_HEADS]], axis=1)
    dz1, dy1, dg01, db01, dgt_f0 = ln_bwd(dx2, x1, y1, gt_f, vec(ln_g[0, 1]), "ffn0_ln_bwd")
    dhg0, dhu0, act0, dx1, dsc_f0, dsh_f0 = ffn_bwd(dy1, hg0, hu0, wts["ffn_w_gate"], wts["ffn_w_up"],
                                                    wts["ffn_w_down"], 0, dz1, x1, sc_f, "ffn0_bwd")
    grads["ffn_w_gate"] = wgrad(u_f0[None], dhg0, "ffn0_dwg", slot=(0, DEPTH, dwg1))
    grads["ffn_w_up"] = wgrad(u_f0[None], dhu0, "ffn0_dwu", slot=(0, DEPTH, dwu1))
    grads["ffn_w_down"] = wgrad(act0, dy1[None], "ffn0_dwd", slot=(0, DEPTH, dwd1))
    dz0, dy0, dg00, db00, dgt_a0 = ln_bwd(dx1, x0, y0, gt_a, vec(ln_g[0, 0]), "mla_ln_bwd")
    do_m, delta_m = linear_nt_delta(dy0, wts["mla_w_o"], o_m, sel_mla, "mla_out_bwd")
    grads["mla_w_o"] = wgrad(o_m[None], dy0[None], "mla_dwo")[0]
    dqn_m, dqr_m, dkn_m, dkr_m, dv_m = mla_attn_bwd(q_m, kn_m, kr2_m, v_m, do_m, rows16(lse_m, "mla_lse_rows"),
                                                    rows16(delta_m, "mla_delta_rows"), nb, "mla_attn_bwd")
    dh_in, dq_pre, dgq, dgkv = mla_mid_bwd(
        dqn_m, dqr_m, dkn_m, dv_m, dkr_m, h_in, vec(mla_g_q), vec(mla_g_kv), w_uq_p, wts["mla_w_uk"],
        wts["mla_w_uv"], cos8, sin8, cos64, sin64s, swap64, from_heads, head_sum, "mla_mid_bwd")
    grads["mla_w_uq"] = wgrad(cq_m[None], dq_pre[None], "mla_dwuq")[0][:, inv_perm]
    grads["mla_w_uk"] = wgrad(ckv_m[None], dkn_m[None], "mla_dwuk")[0]
    grads["mla_w_uv"] = wgrad(ckv_m[None], dv_m[None], "mla_dwuv")[0]
    grads["mla_w_in"] = wgrad(u_m[None], dh_in[None], "mla_dwin")[0]
    dx0, dsc_a0, dsh_a0 = linear_nt_mod_bwd([(dh_in, wts["mla_w_in"])], dz0, x0, sc_a, "mla_in_bwd")

    dmods = [(dsh_a0, dsc_a0, dgt_a0, dsh_f0, dsc_f0, dgt_f0), (dsh_a1, dsc_a1, dgt_a1, dsh_f1, dsc_f1, dgt_f1)]
    d_ln_g = jnp.stack([jnp.concatenate([dg00, dg01], axis=0), jnp.concatenate([dg10, dg11], axis=0)])
    d_ln_b = jnp.stack([jnp.concatenate([db00, db01], axis=0), jnp.concatenate([db10, db11], axis=0)])
    return loss_part, dx0.reshape(nb, s, d), grads, dmods, d_ln_g, d_ln_b, dgq, dgkv, dbf[:, :FOX_HEADS]


def _pad_rows(a, rows):
    return jnp.pad(a, ((0, rows - a.shape[0]), (0, 0)))


def kernel(x, c, positions, mla_w_in, mla_g_q, mla_w_uq, mla_g_kv, mla_w_uk, mla_w_uv, mla_w_o, fox_w_in, fox_b_f, fox_w_o, ada_w, ada_b, ffn_w_gate, ffn_w_up, ffn_w_down, ln_g, ln_b, loss_target, m_mla_w_in, m_mla_g_q, m_mla_w_uq, m_mla_g_kv, m_mla_w_uk, m_mla_w_uv, m_mla_w_o, m_fox_w_in, m_fox_b_f, m_fox_w_o, m_ada_w, m_ada_b, m_ffn_w_gate, m_ffn_w_up, m_ffn_w_down, m_ln_g, m_ln_b, v_mla_w_in, v_mla_g_q, v_mla_w_uq, v_mla_g_kv, v_mla_w_uk, v_mla_w_uv, v_mla_w_o, v_fox_w_in, v_fox_b_f, v_fox_w_o, v_ada_w, v_ada_b, v_ffn_w_gate, v_ffn_w_up, v_ffn_w_down, v_ln_g, v_ln_b):
    args = dict(locals())
    nb, s, d = x.shape
    ax, ay, ac = lax.axis_index("x"), lax.axis_index("y"), lax.axis_index("c")
    chip = 2 * ax + ay
    dev = 2 * chip + ac
    n_dev = 2 * N_CHIPS
    n_all = nb * n_dev

    shard_shapes = {n: (args[n].shape if _SHARD_KIND[n] == "chunk" else args[n].shape[1:]) for n in _PACKED}
    w_all = all_gather_chips([_halves(args[n].reshape(shard_shapes[n]).astype(BF16)) for n in _PACKED], "gather_weights")
    wts = {}
    for n, g in zip(_PACKED, w_all):
        g = g.reshape(N_CHIPS, *shard_shapes[n])
        if _SHARD_KIND[n] == "rows":
            g = g.reshape(-1, g.shape[-1])
        elif _SHARD_KIND[n] == "cols":
            g = _cols_to_full(g)
        wts[n] = g

    c_all = all_gather8(_pad_rows(c, 8), "gather_c").reshape(n_dev, 8, d)[:, :nb].reshape(n_all, d)
    mod_part = ada_mod_part(c_all, ada_w, "ada_mod")
    ncol = mod_part.shape[-1]
    mod_g = all_gather8(mod_part.reshape(DEPTH * n_all, ncol), "gather_mod")
    mod_g = mod_g.reshape(N_CHIPS, 2, DEPTH, n_all, ncol)[:, 0]
    mod_full = jnp.transpose(mod_g, (1, 2, 0, 3)).reshape(DEPTH, n_all, N_CHIPS * ncol) + ada_b[:, None, :]
    mod_loc = lax.dynamic_slice_in_dim(mod_full, dev * nb, nb, axis=1)
    mods = [tuple(mod_loc[i, :, k * d:(k + 1) * d].reshape(nb, 1, d) for k in range(6)) for i in range(DEPTH)]

    ln_cols = ln_g.shape[-1]
    ln_blk = jnp.concatenate([ln_g.reshape(2 * DEPTH, ln_cols), ln_b.reshape(2 * DEPTH, ln_cols)], axis=0)
    ln_all = all_gather8(ln_blk, "gather_ln").reshape(N_CHIPS, 2, 4 * DEPTH, ln_cols)[:, 0]
    ln_all = jnp.transpose(ln_all, (1, 0, 2)).reshape(4 * DEPTH, d)
    ln_g_full = ln_all[:2 * DEPTH].reshape(DEPTH, 2, d)
    ln_b_full = ln_all[2 * DEPTH:].reshape(DEPTH, 2, d)

    loss_part, grad_x, grads, dmods, d_ln_g, d_ln_b, dgq, dgkv, dbf = _local_step(
        x, positions, loss_target, mods, wts, ln_g_full, ln_b_full, mla_g_q[0], mla_g_kv[0], fox_b_f[0])
    loss = lax.psum(loss_part, ("x", "y", "c"))

    dmod_rows = jnp.stack([jnp.concatenate([v_.reshape(nb, d) for v_ in dm], axis=1) for dm in dmods])
    small = jnp.concatenate([
        d_ln_g.reshape(2 * DEPTH, d), d_ln_b.reshape(2 * DEPTH, d),
        jnp.pad(jnp.concatenate([dgq, dgkv, dbf], axis=1), ((0, 0), (0, d - 2 * MLA_QR - FOX_HEADS))),
        dmod_rows.reshape(DEPTH * nb * 6, d)], axis=0)
    n_small = small.shape[0]
    small_rows = -(-n_small // 8) * 8
    small_all = all_gather8(_pad_rows(small, small_rows), "gather_stats").reshape(n_dev, small_rows, d)
    stat_sum = sum_leading(small_all, "sum_stats")
    g_ln_g = lax.dynamic_slice_in_dim(stat_sum[:2 * DEPTH], chip * ln_cols, ln_cols, axis=1).reshape(DEPTH, 2, ln_cols)
    g_ln_b = lax.dynamic_slice_in_dim(stat_sum[2 * DEPTH:4 * DEPTH], chip * ln_cols, ln_cols, axis=1).reshape(DEPTH, 2, ln_cols)
    row = stat_sum[4 * DEPTH]
    g_gq = row[:MLA_QR].reshape(1, MLA_QR)
    g_gkv = row[MLA_QR:2 * MLA_QR].reshape(1, MLA_KVR)
    g_bf = row[2 * MLA_QR:2 * MLA_QR + FOX_HEADS].reshape(1, FOX_HEADS)
    base = 4 * DEPTH + 1
    dmod_all = small_all[:, base:base + DEPTH * nb * 6].reshape(n_dev, DEPTH, nb, 6 * d)
    dmod_all = jnp.transpose(dmod_all, (1, 0, 2, 3)).reshape(DEPTH, n_all, 6 * d)
    g_ada_b = sum_leading(jnp.transpose(dmod_all, (1, 0, 2)), "sum_ada_b")
    dmod_mine = lax.dynamic_slice_in_dim(dmod_all, chip * ncol, ncol, axis=2)
    g_ada_w = ada_grad(c_all.T, dmod_mine, "ada_grad")

    g_blocks = []
    for n in _PACKED:
        g = grads[n]
        if _SHARD_KIND[n] == "rows":
            g = g.reshape(N_CHIPS, 2, g.shape[0] // (2 * N_CHIPS), g.shape[1])
        elif _SHARD_KIND[n] == "cols":
            g = _full_to_cols(g)
            g = g.reshape(N_CHIPS, 2, g.shape[1] // 2, g.shape[2])
        g_blocks.append(g)
    core = ac.reshape(1).astype(jnp.int32)
    recv = sibling_swap_halves(g_blocks, "rs_swap")
    chip_sums = [add_own_half(g, a, core, "rs_add_" + n) for n, g, a in zip(_PACKED, g_blocks, recv)]
    pieces = chip_exchange(chip_sums, "rs_exchange")
    place = jnp.stack([ac, chip]).astype(jnp.int32)
    my_halves = [sum_pieces(sm, b, place, "rs_sum_" + n) for n, sm, b in zip(_PACKED, chip_sums, pieces)]
    joined = sibling_join_halves(my_halves, "rs_join")
    g_big = {n: j.reshape(shard_shapes[n]) for n, j in zip(_PACKED, joined)}

    g_out = {
        "mla_w_in": g_big["mla_w_in"], "mla_g_q": g_gq, "mla_w_uq": g_big["mla_w_uq"], "mla_g_kv": g_gkv,
        "mla_w_uk": g_big["mla_w_uk"], "mla_w_uv": g_big["mla_w_uv"], "mla_w_o": g_big["mla_w_o"],
        "fox_w_in": g_big["fox_w_in"], "fox_b_f": g_bf, "fox_w_o": g_big["fox_w_o"],
        "ada_w": g_ada_w, "ada_b": g_ada_b, "ffn_w_gate": g_big["ffn_w_gate"], "ffn_w_up": g_big["ffn_w_up"],
        "ffn_w_down": g_big["ffn_w_down"], "ln_g": g_ln_g, "ln_b": g_ln_b}
    names = ["mla_w_in", "mla_g_q", "mla_w_uq", "mla_g_kv", "mla_w_uk", "mla_w_uv", "mla_w_o", "fox_w_in", "fox_b_f",
             "fox_w_o", "ada_w", "ada_b", "ffn_w_gate", "ffn_w_up", "ffn_w_down", "ln_g", "ln_b"]
    small_names = ["mla_g_q", "mla_g_kv", "fox_b_f", "ada_b", "ln_g", "ln_b"]
    deltas, new_m, new_v = {}, {}, {}
    for n in names:
        if n in small_names:
            continue
        shp = args[n].shape
        two_d = (-1, shp[-1])
        dl, mn, vn = adamw(args[n].reshape(two_d), g_out[n].reshape(two_d), args["m_" + n].reshape(two_d),
                           args["v_" + n].reshape(two_d), "adamw_" + n)
        deltas[n], new_m[n], new_v[n] = dl.reshape(shp), mn.reshape(shp), vn.reshape(shp)

    def small_pack(prefix, src):
        flat = jnp.concatenate([src[prefix + n].reshape(-1) for n in small_names])
        size = -(-flat.shape[0] // (8 * 128)) * 8 * 128
        return jnp.pad(flat, (0, size - flat.shape[0])).reshape(-1, 128)

    sd, sm, sv = adamw(small_pack("", args), small_pack("", g_out), small_pack("m_", args), small_pack("v_", args),
                       "adamw_small")
    off = 0
    for n in small_names:
        shp = args[n].shape
        size = math.prod(shp)
        deltas[n] = sd.reshape(-1)[off:off + size].reshape(shp)
        new_m[n] = sm.reshape(-1)[off:off + size].reshape(shp)
        new_v[n] = sv.reshape(-1)[off:off + size].reshape(shp)
        off += size

    outs = [loss, grad_x]
    outs += [g_out[n].reshape(args[n].shape) for n in names]
    outs += [deltas[n] for n in names] + [new_m[n] for n in names] + [new_v[n] for n in names]
    return tuple(outs)
```

```python
import functools
import math

import numpy as np
import jax
import jax.numpy as jnp
from jax import lax
from jax.experimental import pallas as pl
from jax.experimental.pallas import tpu as pltpu

F32 = jnp.float32
BF16 = jnp.bfloat16
MESH = pl.DeviceIdType.MESH

D_MODEL = 1024
DEPTH = 2
MLA_HEADS = 8
MLA_NOPE = 128
MLA_ROPE = 64
MLA_V = 128
MLA_QR = 256
MLA_KVR = 256
ROPE_THETA = 10000.0
FOX_HEADS = 16
FOX_HD = 64
D_FF = 2816
N_CHIPS = 4
FF_CHUNK = D_FF // N_CHIPS
ALPHA = (2.0 * DEPTH) ** 0.25
EPS = 1e-5
ADAM_LR = 0.001
ADAM_B1 = 0.9
ADAM_B2 = 0.999
ADAM_EPS = 1e-08
ADAM_WD = 0.01
ADAM_STEP = 10

VMEM_LIMIT = 56 * 1024 * 1024
TOKEN_TILE = 512
ATTN_TILE = 512
COMM_BLOCK_BYTES = 2 * 1024 * 1024
ADAMW_BLOCK_BYTES = 1024 * 1024


def _cp(n_axes):
    return pltpu.CompilerParams(dimension_semantics=("arbitrary",) * n_axes, vmem_limit_bytes=VMEM_LIMIT)


def _dot(a, b):
    return jnp.dot(a, b, preferred_element_type=F32)


def _dot_nt(a, b):
    return lax.dot_general(a, b, (((1,), (1,)), ((), ())), preferred_element_type=F32)


def _dot_tn(a, b):
    return lax.dot_general(a, b, (((0,), (0,)), ((), ())), preferred_element_type=F32)


def _dot_f32(a, b):
    return jnp.dot(a, b, preferred_element_type=F32, precision=lax.Precision.HIGHEST)


def _sds(shape, dtype):
    return jax.ShapeDtypeStruct(shape, dtype)


def _place():
    return lax.axis_index("x"), lax.axis_index("y"), lax.axis_index("c")


class _Hosted:
    def __init__(self, inputs, out_shape, sems, start, finish):
        self.inputs, self.out_shape, self.sems, self.start, self.finish = inputs, out_shape, sems, start, finish


def _call(body, name, grid, in_specs, out_specs, out_shape, args, scratch_shapes=(), hosted=None):
    in_specs, out_specs, out_shape, scratch_shapes = list(in_specs), list(out_specs), list(out_shape), list(scratch_shapes)
    if hosted is None:
        return pl.pallas_call(body, name=name, grid=grid, in_specs=in_specs, out_specs=out_specs, out_shape=out_shape,
                              scratch_shapes=scratch_shapes, compiler_params=_cp(len(grid)))(*args)
    n_in, n_out, n_scr = len(in_specs), len(out_specs), len(scratch_shapes)
    h_in, h_out = len(hosted.inputs), len(hosted.out_shape)

    def carried(*refs):
        o0 = n_in + h_in
        s0 = o0 + n_out + h_out
        c_in, c_out, c_sem = refs[n_in:o0], refs[o0 + n_out:s0], refs[s0 + n_scr:]
        ids = [pl.program_id(a) for a in range(len(grid))]
        first = functools.reduce(jnp.logical_and, [i == 0 for i in ids])
        last = functools.reduce(jnp.logical_and, [i == g - 1 for i, g in zip(ids, grid)])

        @pl.when(first)
        def _():
            hosted.start(c_in, c_out, c_sem)

        body(*refs[:n_in], *refs[o0:o0 + n_out], *refs[s0:s0 + n_scr])

        @pl.when(last)
        def _():
            hosted.finish(c_in, c_out, c_sem)

    hbm = pl.BlockSpec(memory_space=pl.ANY)
    res = pl.pallas_call(
        carried, name=name, grid=grid, in_specs=in_specs + [hbm] * h_in, out_specs=out_specs + [hbm] * h_out,
        out_shape=out_shape + list(hosted.out_shape), scratch_shapes=scratch_shapes + list(hosted.sems),
        compiler_params=_cp(len(grid)))(*args, *hosted.inputs)
    return res[:n_out], res[n_out:]


def mod_linear(x, shift, scale, w, out_dtype, name, tn=None, emit_u=False):
    t, d = x.shape
    n = w.shape[1]
    tn = n if tn is None else tn
    tm = TOKEN_TILE
    tps = (t // shift.shape[0]) // tm

    def body(x_ref, sh_ref, sc_ref, w_ref, o_ref, *rest):
        u = (x_ref[...] * (1.0 + sc_ref[...]) + sh_ref[...]).astype(BF16)
        o_ref[...] = _dot(u, w_ref[...]).astype(out_dtype)
        if emit_u:
            @pl.when(pl.program_id(1) == 0)
            def _():
                rest[0][...] = u

    vec = pl.BlockSpec((None, 1, d), lambda i, j: (i // tps, 0, 0))
    out_shape = [_sds((t, n), out_dtype)]
    out_specs = [pl.BlockSpec((tm, tn), lambda i, j: (i, j))]
    if emit_u:
        out_shape.append(_sds((t, d), BF16))
        out_specs.append(pl.BlockSpec((tm, d), lambda i, j: (i, 0)))
    res = pl.pallas_call(
        body, name=name, grid=(t // tm, n // tn),
        in_specs=[pl.BlockSpec((tm, d), lambda i, j: (i, 0)), vec, vec,
                  pl.BlockSpec((d, tn), lambda i, j: (0, j))],
        out_specs=out_specs, out_shape=out_shape, compiler_params=_cp(2),
    )(x, shift, scale, w)
    return res if emit_u else res[0]


def _rms(h, g):
    rstd = lax.rsqrt(jnp.mean(h * h, axis=-1, keepdims=True) + EPS)
    return h * rstd, rstd


def mla_mid_fwd(h, g_q, g_kv, w_uq, w_uk, w_uv, cos8, sin8, cos64, sin64s, swap64, rope_to_heads, dup64, name):
    t = h.shape[0]
    tm = TOKEN_TILE
    hq = MLA_HEADS * MLA_NOPE
    hr = MLA_HEADS * MLA_ROPE // 2

    def body(h_ref, gq_ref, gkv_ref, wuq_ref, wuk_ref, wuv_ref, c8_ref, s8_ref, c64_ref, s64_ref, sw_ref, p_ref, d_ref,
             q_ref, kn_ref, v_ref, kr_ref, cq_ref, ckv_ref):
        hh = h_ref[...]
        cq = (_rms(hh[:, :MLA_QR], None)[0] * gq_ref[...]).astype(BF16)
        ckv = (_rms(hh[:, MLA_QR:MLA_QR + MLA_KVR], None)[0] * gkv_ref[...]).astype(BF16)
        cq_ref[...] = cq
        ckv_ref[...] = ckv
        q = _dot(cq, wuq_ref[...])
        x1 = q[:, hq:hq + hr]
        x2 = q[:, hq + hr:]
        cs = c8_ref[...]
        sn = s8_ref[...]
        rot = jnp.concatenate([x1 * cs - x2 * sn, x2 * cs + x1 * sn], axis=1).astype(BF16)
        q_ref[...] = jnp.concatenate([q[:, :hq].astype(BF16), _dot(rot, p_ref[...]).astype(BF16)], axis=1)
        kn_ref[...] = _dot(ckv, wuk_ref[...]).astype(BF16)
        v_ref[...] = _dot(ckv, wuv_ref[...]).astype(BF16)
        kr = hh[:, MLA_QR + MLA_KVR:]
        kr = (kr * c64_ref[...] + _dot_f32(kr, sw_ref[...]) * s64_ref[...]).astype(BF16)
        kr_ref[...] = _dot(kr, d_ref[...]).astype(BF16)

    def rows(n):
        return pl.BlockSpec((tm, n), lambda i: (i, 0))

    def whole(a):
        return pl.BlockSpec(a.shape, lambda i: (0,) * a.ndim)

    nq = w_uq.shape[1]
    return pl.pallas_call(
        body, name=name, grid=(t // tm,),
        in_specs=[rows(h.shape[1]), whole(g_q), whole(g_kv), whole(w_uq), whole(w_uk), whole(w_uv),
                  rows(hr), rows(hr), rows(MLA_ROPE), rows(MLA_ROPE), whole(swap64), whole(rope_to_heads), whole(dup64)],
        out_specs=[rows(nq), rows(hq), rows(hq), rows(2 * MLA_ROPE), rows(MLA_QR), rows(MLA_KVR)],
        out_shape=[_sds((t, nq), BF16), _sds((t, hq), BF16), _sds((t, hq), BF16), _sds((t, 2 * MLA_ROPE), BF16),
                   _sds((t, MLA_QR), BF16), _sds((t, MLA_KVR), BF16)],
        compiler_params=_cp(1),
    )(h, g_q, g_kv, w_uq, w_uk, w_uv, cos8, sin8, cos64, sin64s, swap64, rope_to_heads, dup64)


def _pick_lane(tile, idx):
    lane = lax.broadcasted_iota(jnp.int32, tile.shape, 1)
    return jnp.sum(jnp.where(lane == idx, tile, 0.0), axis=1, keepdims=True)


def _pick_row(tile, idx):
    row = lax.broadcasted_iota(jnp.int32, tile.shape, 0)
    return jnp.sum(jnp.where(row == idx, tile, 0.0), axis=0, keepdims=True)


def _put_lane(tile, idx, col):
    lane = lax.broadcasted_iota(jnp.int32, tile.shape, 1)
    return jnp.where(lane == idx, col, tile)


def _put_row(tile, idx, row):
    r = lax.broadcasted_iota(jnp.int32, tile.shape, 0)
    return tile + jnp.where(r == idx, row, 0.0)


def _causal_softmax_blocks(i, nblk_rows, score_fn, pv_fn, dv):
    tq = nblk_rows

    def block(j, carry, masked):
        m, l, acc = carry
        sc = score_fn(j)
        if masked:
            keep = lax.broadcasted_iota(jnp.int32, sc.shape, 0) >= lax.broadcasted_iota(jnp.int32, sc.shape, 1)
            sc = jnp.where(keep, sc, -1e30)
        m_new = jnp.maximum(m, jnp.max(sc, axis=1, keepdims=True))
        a = jnp.exp(m - m_new)
        p = jnp.exp(sc - m_new)
        l = a * l + jnp.sum(p, axis=1, keepdims=True)
        acc = a * acc + pv_fn(j, p.astype(BF16))
        return m_new, l, acc

    init = (jnp.full((tq, 1), -1e30, F32), jnp.zeros((tq, 1), F32), jnp.zeros((tq, dv), F32))
    carry = lax.fori_loop(0, i, lambda j, c: block(j, c, False), init)
    m, l, acc = block(i, carry, True)
    return acc / l, m + jnp.log(l)


def fox_attn_fwd(qkv, cum, cum_rows, nb, name, hosted=None):
    t = qkv.shape[0]
    s = t // nb
    tq = ATTN_TILE
    nq = s // tq
    npairs = FOX_HEADS // 2
    scale = FOX_HD ** -0.5

    def body(q_ref, k_ref, v_ref, cum_ref, cr_ref, o_ref, lse_ref):
        i = pl.program_id(1)
        hp = pl.program_id(2)

        @pl.when(hp == 0)
        def _():
            lse_ref[...] = jnp.zeros_like(lse_ref)

        q = q_ref[...]
        low = lax.broadcasted_iota(jnp.int32, q.shape, 1) < FOX_HD
        cum_t = cum_ref[...]
        outs = []
        lse_t = lse_ref[...]
        for a in (0, 1):
            head = 2 * hp + a
            qa = jnp.where(low if a == 0 else jnp.logical_not(low), q, jnp.zeros_like(q))
            fq = _pick_lane(cum_t, head)

            def score(j, qa=qa, fq=fq, head=head):
                kb = k_ref[pl.ds(pl.multiple_of(j * tq, tq), tq), :]
                return _dot_nt(qa, kb) * scale + fq - _pick_row(cr_ref[j], head)

            def pv(j, p):
                return _dot(p, v_ref[pl.ds(pl.multiple_of(j * tq, tq), tq), :])

            o_a, lse_a = _causal_softmax_blocks(i, tq, score, pv, 2 * FOX_HD)
            outs.append(o_a)
            lse_t = _put_lane(lse_t, head, lse_a)
        o_ref[...] = jnp.where(low, outs[0], outs[1]).astype(BF16)
        lse_ref[...] = lse_t

    return _call(
        body, name, (nb, nq, npairs),
        [pl.BlockSpec((tq, 128), lambda b, i, hp: (b * nq + i, hp)),
         pl.BlockSpec((s, 128), lambda b, i, hp: (b, npairs + hp)),
         pl.BlockSpec((s, 128), lambda b, i, hp: (b, 2 * npairs + hp)),
         pl.BlockSpec((tq, 128), lambda b, i, hp: (b * nq + i, 0)),
         pl.BlockSpec((nq, 16, tq), lambda b, i, hp: (b, 0, 0))],
        [pl.BlockSpec((tq, 128), lambda b, i, hp: (b * nq + i, hp)),
         pl.BlockSpec((tq, 128), lambda b, i, hp: (b * nq + i, 0))],
        [_sds((t, D_MODEL), BF16), _sds((t, 128), F32)], (qkv, qkv, qkv, cum, cum_rows), hosted=hosted)


def mla_attn_fwd(q, kn, kr2, v, nb, name, hosted=None):
    t = q.shape[0]
    s = t // nb
    tq = ATTN_TILE
    nq = s // tq
    scale = (MLA_NOPE + MLA_ROPE) ** -0.5

    def body(qn_ref, qr_ref, kn_ref, kr_ref, v_ref, o_ref, lse_ref):
        i = pl.program_id(1)
        h = pl.program_id(2)

        @pl.when(h == 0)
        def _():
            lse_ref[...] = jnp.zeros_like(lse_ref)

        qr = qr_ref[...]
        mine = (lax.broadcasted_iota(jnp.int32, qr.shape, 1) // MLA_ROPE) == (h % 2)
        q_cat = jnp.concatenate([qn_ref[...], jnp.where(mine, qr, jnp.zeros_like(qr))], axis=1)

        def score(j):
            rows = pl.ds(pl.multiple_of(j * tq, tq), tq)
            k_cat = jnp.concatenate([kn_ref[rows, :], kr_ref[rows, :]], axis=1)
            return _dot_nt(q_cat, k_cat) * scale

        def pv(j, p):
            return _dot(p, v_ref[pl.ds(pl.multiple_of(j * tq, tq), tq), :])

        o, lse = _causal_softmax_blocks(i, tq, score, pv, MLA_V)
        o_ref[...] = o.astype(BF16)
        lse_ref[...] = _put_lane(lse_ref[...], h, lse)

    nrope0 = MLA_HEADS
    return _call(
        body, name, (nb, nq, MLA_HEADS),
        [pl.BlockSpec((tq, 128), lambda b, i, h: (b * nq + i, h)),
         pl.BlockSpec((tq, 128), lambda b, i, h: (b * nq + i, nrope0 + h // 2)),
         pl.BlockSpec((s, 128), lambda b, i, h: (b, h)),
         pl.BlockSpec((s, 128), lambda b, i, h: (b, 0)),
         pl.BlockSpec((s, 128), lambda b, i, h: (b, h))],
        [pl.BlockSpec((tq, 128), lambda b, i, h: (b * nq + i, h)),
         pl.BlockSpec((tq, 128), lambda b, i, h: (b * nq + i, 0))],
        [_sds((t, MLA_HEADS * MLA_V), BF16), _sds((t, 128), F32)], (q, q, kn, kr2, v), hosted=hosted)


def rows16(a, name):
    t = a.shape[0]
    tq = ATTN_TILE

    def body(a_ref, o_ref):
        o_ref[...] = a_ref[...].T[:16, :]

    return pl.pallas_call(
        body, name=name, grid=(t // tq,), in_specs=[pl.BlockSpec((tq, 128), lambda n: (n, 0))],
        out_specs=pl.BlockSpec((None, 16, tq), lambda n: (n, 0, 0)), out_shape=_sds((t // tq, 16, tq), F32),
        compiler_params=_cp(1),
    )(a)


def tokens128(rows, onehot, name):
    nblk, _, tq = rows.shape

    def body(r_ref, e_ref, o_ref):
        o_ref[...] = lax.dot_general(r_ref[...], e_ref[...], (((0,), (0,)), ((), ())), preferred_element_type=F32,
                                     precision=lax.Precision.HIGHEST)

    return pl.pallas_call(
        body, name=name, grid=(nblk,),
        in_specs=[pl.BlockSpec((None, 16, tq), lambda n: (n, 0, 0)), pl.BlockSpec((16, 128), lambda n: (0, 0))],
        out_specs=pl.BlockSpec((tq, 128), lambda n: (n, 0)), out_shape=_sds((nblk * tq, 128), F32),
        compiler_params=_cp(1),
    )(rows, onehot)


def _layer_norm(z, g, b):
    mu = jnp.mean(z, axis=-1, keepdims=True)
    zc = z - mu
    rstd = lax.rsqrt(jnp.mean(zc * zc, axis=-1, keepdims=True) + EPS)
    xhat = zc * rstd
    return xhat * g + b, xhat, rstd


def linear_resid_ln(a, w, x_in, gate, ln_g, ln_b, name):
    t, kdim = a.shape
    d = w.shape[1]
    tm = TOKEN_TILE
    tps = (t // gate.shape[0]) // tm

    def body(a_ref, w_ref, x_ref, gt_ref, g_ref, b_ref, y_ref, xo_ref):
        y = _dot(a_ref[...], w_ref[...])
        y_ref[...] = y
        z = ALPHA * x_ref[...] + (1.0 + gt_ref[...]) * y
        xo_ref[...] = _layer_norm(z, g_ref[...], b_ref[...])[0]

    rows = pl.BlockSpec((tm, d), lambda i: (i, 0))
    vec = pl.BlockSpec((1, d), lambda i: (0, 0))
    return pl.pallas_call(
        body, name=name, grid=(t // tm,),
        in_specs=[pl.BlockSpec((tm, kdim), lambda i: (i, 0)), pl.BlockSpec((kdim, d), lambda i: (0, 0)), rows,
                  pl.BlockSpec((None, 1, d), lambda i: (i // tps, 0, 0)), vec, vec],
        out_specs=[rows, rows], out_shape=[_sds((t, d), F32), _sds((t, d), F32)],
        compiler_params=_cp(1),
    )(a, w, x_in, gate, ln_g, ln_b)


def ffn_fwd(x_in, shift, scale, gate, wg, wu, wd, ln_g, ln_b, name, hosted=None):
    t, d = x_in.shape
    c, _, fc = wg.shape
    tm = TOKEN_TILE
    tps = (t // gate.shape[0]) // tm

    def body(x_ref, sh_ref, sc_ref, gt_ref, wg_ref, wu_ref, wd_ref, g_ref, b_ref,
             u_ref, hg_ref, hu_ref, y_ref, xo_ref, acc_ref):
        cc = pl.program_id(1)

        @pl.when(cc == 0)
        def _():
            u_ref[...] = (x_ref[...] * (1.0 + sc_ref[...]) + sh_ref[...]).astype(BF16)
            acc_ref[...] = jnp.zeros_like(acc_ref)

        u = u_ref[...]
        hg = _dot(u, wg_ref[...])
        hu = _dot(u, wu_ref[...])
        hg_ref[...] = hg.astype(BF16)
        hu_ref[...] = hu.astype(BF16)
        act = (hg * jax.nn.sigmoid(hg) * hu).astype(BF16)
        acc_ref[...] += _dot(act, wd_ref[...])

        @pl.when(cc == c - 1)
        def _():
            y = acc_ref[...]
            y_ref[...] = y
            z = ALPHA * x_ref[...] + (1.0 + gt_ref[...]) * y
            xo_ref[...] = _layer_norm(z, g_ref[...], b_ref[...])[0]

    rows = pl.BlockSpec((tm, d), lambda i, cc: (i, 0))
    bvec = pl.BlockSpec((None, 1, d), lambda i, cc: (i // tps, 0, 0))
    vec = pl.BlockSpec((1, d), lambda i, cc: (0, 0))
    hspec = pl.BlockSpec((None, tm, fc), lambda i, cc: (cc, i, 0))
    wcol = pl.BlockSpec((None, d, fc), lambda i, cc: (cc, 0, 0))
    return _call(
        body, name, (t // tm, c),
        [rows, bvec, bvec, bvec, wcol, wcol, pl.BlockSpec((None, fc, d), lambda i, cc: (cc, 0, 0)), vec, vec],
        [rows, hspec, hspec, rows, rows],
        [_sds((t, d), BF16), _sds((c, t, fc), BF16), _sds((c, t, fc), BF16), _sds((t, d), F32), _sds((t, d), F32)],
        (x_in, shift, scale, gate, wg, wu, wd, ln_g, ln_b), scratch_shapes=[pltpu.VMEM((tm, d), F32)], hosted=hosted)


def fox_gate_fwd(hf, b_f, tri, n_batch, name):
    t, n = hf.shape
    blk = tri.shape[0]
    nb = (t // n_batch) // blk

    def body(hf_ref, b_ref, tri_ref, o_ref, carry_ref):
        @pl.when(pl.program_id(1) == 0)
        def _():
            carry_ref[...] = jnp.zeros_like(carry_ref)

        xx = hf_ref[...] + b_ref[...]
        lf = jnp.minimum(xx, 0.0) - jnp.log(1.0 + jnp.exp(-jnp.abs(xx)))
        cum = _dot_f32(tri_ref[...], lf) + carry_ref[...]
        o_ref[...] = cum
        carry_ref[...] = cum[blk - 1:blk, :]

    return pl.pallas_call(
        body, name=name, grid=(n_batch, nb),
        in_specs=[pl.BlockSpec((blk, n), lambda bb, i: (bb * nb + i, 0)), pl.BlockSpec((1, n), lambda bb, i: (0, 0)),
                  pl.BlockSpec((blk, blk), lambda bb, i: (0, 0))],
        out_specs=pl.BlockSpec((blk, n), lambda bb, i: (bb * nb + i, 0)),
        out_shape=_sds((t, n), F32), scratch_shapes=[pltpu.VMEM((1, n), F32)],
        compiler_params=_cp(2),
    )(hf, b_f, tri)


def loss_grad(x_out, target, name):
    t, d = x_out.shape
    tm = TOKEN_TILE

    def body(x_ref, t_ref, g_ref, l_ref):
        @pl.when(pl.program_id(0) == 0)
        def _():
            l_ref[...] = jnp.zeros_like(l_ref)

        err = x_ref[...] - t_ref[...]
        g_ref[...] = err / d
        l_ref[...] += jnp.sum(err * err, axis=0, keepdims=True)

    rows = pl.BlockSpec((tm, d), lambda i: (i, 0))
    return pl.pallas_call(
        body, name=name, grid=(t // tm,), in_specs=[rows, rows],
        out_specs=[rows, pl.BlockSpec((1, d), lambda i: (0, 0))],
        out_shape=[_sds((t, d), F32), _sds((1, d), F32)], compiler_params=_cp(1),
    )(x_out, target)


def ln_bwd(dxo, x_in, y, gate, ln_g, name):
    t, d = dxo.shape
    nb = gate.shape[0]
    tm = TOKEN_TILE
    tps = (t // nb) // tm

    def body(dxo_ref, x_ref, y_ref, gt_ref, g_ref, dz_ref, dy_ref, dg_ref, db_ref, dgt_ref):
        i = pl.program_id(0)

        @pl.when(i == 0)
        def _():
            dg_ref[...] = jnp.zeros_like(dg_ref)
            db_ref[...] = jnp.zeros_like(db_ref)

        @pl.when(i % tps == 0)
        def _():
            dgt_ref[...] = jnp.zeros_like(dgt_ref)

        yy = y_ref[...]
        g1 = 1.0 + gt_ref[...]
        z = ALPHA * x_ref[...] + g1 * yy
        _, xhat, rstd = _layer_norm(z, 1.0, 0.0)
        dxo_v = dxo_ref[...]
        dg_ref[...] += jnp.sum(dxo_v * xhat, axis=0, keepdims=True)
        db_ref[...] += jnp.sum(dxo_v, axis=0, keepdims=True)
        dxh = dxo_v * g_ref[...]
        dz = rstd * (dxh - jnp.mean(dxh, axis=-1, keepdims=True) - xhat * jnp.mean(dxh * xhat, axis=-1, keepdims=True))
        dz_ref[...] = dz
        dy_ref[...] = (g1 * dz).astype(BF16)
        dgt_ref[...] += jnp.sum(dz * yy, axis=0, keepdims=True)

    rows = pl.BlockSpec((tm, d), lambda i: (i, 0))
    vec = pl.BlockSpec((1, d), lambda i: (0, 0))
    bvec = pl.BlockSpec((None, 1, d), lambda i: (i // tps, 0, 0))
    return pl.pallas_call(
        body, name=name, grid=(t // tm,), in_specs=[rows, rows, rows, bvec, vec],
        out_specs=[rows, rows, vec, vec, bvec],
        out_shape=[_sds((t, d), F32), _sds((t, d), BF16), _sds((1, d), F32), _sds((1, d), F32), _sds((nb, 1, d), F32)],
        compiler_params=_cp(1),
    )(dxo, x_in, y, gate, ln_g)


def _mod_bwd_tail(du, dz_ref, x_ref, sc_ref, dx_ref, dsc_ref, dsh_ref, first):
    @pl.when(first)
    def _():
        dsc_ref[...] = jnp.zeros_like(dsc_ref)
        dsh_ref[...] = jnp.zeros_like(dsh_ref)

    dx_ref[...] = ALPHA * dz_ref[...] + du * (1.0 + sc_ref[...])
    dsc_ref[...] += jnp.sum(du * x_ref[...], axis=0, keepdims=True)
    dsh_ref[...] += jnp.sum(du, axis=0, keepdims=True)


def ffn_bwd(dy, hg, hu, wg, wu, wd, dz, x_in, scale, name):
    t, d = dy.shape
    c, _, fc = wg.shape
    nb = scale.shape[0]
    tm = TOKEN_TILE
    tps = (t // nb) // tm

    def body(dy_ref, hg_ref, hu_ref, wg_ref, wu_ref, wd_ref, dz_ref, x_ref, sc_ref,
             dhg_ref, dhu_ref, act_ref, dx_ref, dsc_ref, dsh_ref, acc_ref):
        i = pl.program_id(0)
        cc = pl.program_id(1)

        @pl.when(cc == 0)
        def _():
            acc_ref[...] = jnp.zeros_like(acc_ref)

        hgv = hg_ref[...].astype(F32)
        huv = hu_ref[...].astype(F32)
        da = _dot_nt(dy_ref[...], wd_ref[...])
        sg = jax.nn.sigmoid(hgv)
        sl = hgv * sg
        act_ref[...] = (sl * huv).astype(BF16)
        dhu = (da * sl).astype(BF16)
        dhg = (da * huv * (sg * (1.0 + hgv * (1.0 - sg)))).astype(BF16)
        dhu_ref[...] = dhu
        dhg_ref[...] = dhg
        acc_ref[...] += _dot_nt(dhg, wg_ref[...]) + _dot_nt(dhu, wu_ref[...])

        @pl.when(cc == c - 1)
        def _():
            _mod_bwd_tail(acc_ref[...], dz_ref, x_ref, sc_ref, dx_ref, dsc_ref, dsh_ref, i % tps == 0)

    rows = pl.BlockSpec((tm, d), lambda i, cc: (i, 0))
    bvec = pl.BlockSpec((None, 1, d), lambda i, cc: (i // tps, 0, 0))
    hspec = pl.BlockSpec((None, tm, fc), lambda i, cc: (cc, i, 0))
    wcol = pl.BlockSpec((None, d, fc), lambda i, cc: (cc, 0, 0))
    return pl.pallas_call(
        body, name=name, grid=(t // tm, c),
        in_specs=[rows, hspec, hspec, wcol, wcol, pl.BlockSpec((None, fc, d), lambda i, cc: (cc, 0, 0)),
                  rows, rows, bvec],
        out_specs=[hspec, hspec, hspec, rows, bvec, bvec],
        out_shape=[_sds((c, t, fc), BF16), _sds((c, t, fc), BF16), _sds((c, t, fc), BF16), _sds((t, d), F32),
                   _sds((nb, 1, d), F32), _sds((nb, 1, d), F32)],
        scratch_shapes=[pltpu.VMEM((tm, d), F32)],
        compiler_params=_cp(2),
    )(dy, hg, hu, wg, wu, wd, dz, x_in, scale)


def linear_nt_mod_bwd(pairs, dz, x_in, scale, name):
    t, d = dz.shape
    nb = scale.shape[0]
    tm = TOKEN_TILE
    tps = (t // nb) // tm
    npairs = len(pairs)

    def body(*refs):
        dh_refs = refs[:npairs]
        w_refs = refs[npairs:2 * npairs]
        dz_ref, x_ref, sc_ref, dx_ref, dsc_ref, dsh_ref = refs[2 * npairs:]
        du = _dot_nt(dh_refs[0][...], w_refs[0][...])
        for kk in range(1, npairs):
            du = du + _dot_nt(dh_refs[kk][...], w_refs[kk][...])
        _mod_bwd_tail(du, dz_ref, x_ref, sc_ref, dx_ref, dsc_ref, dsh_ref, pl.program_id(0) % tps == 0)

    rows = pl.BlockSpec((tm, d), lambda i: (i, 0))
    bvec = pl.BlockSpec((None, 1, d), lambda i: (i // tps, 0, 0))
    in_specs = [pl.BlockSpec((tm, dh.shape[1]), lambda i: (i, 0)) for dh, _ in pairs]
    in_specs += [pl.BlockSpec(w.shape, lambda i: (0, 0)) for _, w in pairs]
    in_specs += [rows, rows, bvec]
    return pl.pallas_call(
        body, name=name, grid=(t // tm,), in_specs=in_specs,
        out_specs=[rows, bvec, bvec],
        out_shape=[_sds((t, d), F32), _sds((nb, 1, d), F32), _sds((nb, 1, d), F32)],
        compiler_params=_cp(1),
    )(*[dh for dh, _ in pairs], *[w for _, w in pairs], dz, x_in, scale)


def linear_nt_delta(dy, w_o, o, head_sel, name):
    t, d = dy.shape
    hdv = w_o.shape[0]
    tm = TOKEN_TILE

    def body(dy_ref, w_ref, o_ref, sel_ref, do_ref, dl_ref):
        do = _dot_nt(dy_ref[...], w_ref[...])
        do_ref[...] = do.astype(BF16)
        dl_ref[...] = _dot_f32(do * o_ref[...].astype(F32), sel_ref[...])

    return pl.pallas_call(
        body, name=name, grid=(t // tm,),
        in_specs=[pl.BlockSpec((tm, d), lambda i: (i, 0)), pl.BlockSpec((hdv, d), lambda i: (0, 0)),
                  pl.BlockSpec((tm, hdv), lambda i: (i, 0)), pl.BlockSpec(head_sel.shape, lambda i: (0, 0))],
        out_specs=[pl.BlockSpec((tm, hdv), lambda i: (i, 0)), pl.BlockSpec((tm, 128), lambda i: (i, 0))],
        out_shape=[_sds((t, hdv), BF16), _sds((t, 128), F32)], compiler_params=_cp(1),
    )(dy, w_o, o, head_sel)


def _attn_bwd_blocks(j, nk, tk, q_fn, do_fn, k_cat, vb, row_fn, scale, bias_fn, sinks):
    add_dq, add_dfq = sinks
    dk_dim = k_cat.shape[1]
    dv_dim = vb.shape[1]

    def block(i, carry, masked):
        dk_acc, dv_acc, dfk_acc = carry
        qb = q_fn(i)
        dob = do_fn(i)
        lse_row, dl_row = row_fn(i)
        st = _dot_nt(k_cat, qb) * scale
        if bias_fn is not None:
            fq_row, fk_col = bias_fn(i)
            st = st + fq_row - fk_col
        if masked:
            keep = lax.broadcasted_iota(jnp.int32, st.shape, 1) >= lax.broadcasted_iota(jnp.int32, st.shape, 0)
            st = jnp.where(keep, st, -1e30)
        pt = jnp.exp(st - lse_row)
        dv_acc = dv_acc + _dot(pt.astype(BF16), dob)
        dst = pt * (_dot_nt(vb, dob) - dl_row)
        if add_dfq is not None:
            dfk_acc = dfk_acc - jnp.sum(dst, axis=1, keepdims=True)
            add_dfq(i, jnp.sum(dst, axis=0, keepdims=True))
        dsb = (dst * scale).astype(BF16)
        dk_acc = dk_acc + _dot(dsb, qb)
        add_dq(i, _dot_tn(dsb, k_cat))
        return dk_acc, dv_acc, dfk_acc

    init = (jnp.zeros((tk, dk_dim), F32), jnp.zeros((tk, dv_dim), F32), jnp.zeros((tk, 1), F32))
    carry = block(j, init, True)
    return lax.fori_loop(j + 1, nk, lambda i, c: block(i, c, False), carry)


def fox_attn_bwd(qkv, do, cum, cum_rows, lse_rows, delta_rows, nb, name):
    t = qkv.shape[0]
    s = t // nb
    tk = ATTN_TILE
    nk = s // tk
    npairs = FOX_HEADS // 2
    scale = FOX_HD ** -0.5

    def body(q_ref, k_ref, v_ref, do_ref, cum_ref, cr_ref, lr_ref, dr_ref, dq_ref, dk_ref, dv_ref, dfq_ref, dfk_ref):
        hp = pl.program_id(1)
        j = pl.program_id(2)

        @pl.when(j == 0)
        def _():
            dq_ref[...] = jnp.zeros_like(dq_ref)

        @pl.when((j == 0) & (hp == 0))
        def _():
            dfq_ref[...] = jnp.zeros_like(dfq_ref)
            dfk_ref[...] = jnp.zeros_like(dfk_ref)

        kb = k_ref[...]
        vb = v_ref[...]
        low = lax.broadcasted_iota(jnp.int32, kb.shape, 1) < FOX_HD
        cum_t = cum_ref[...]
        dks, dvs = [], []
        for a in (0, 1):
            head = 2 * hp + a
            half = low if a == 0 else jnp.logical_not(low)
            ka = jnp.where(half, kb, jnp.zeros_like(kb))
            va = jnp.where(half, vb, jnp.zeros_like(vb))
            fk = _pick_lane(cum_t, head)

            def rows_of(i):
                return pl.ds(pl.multiple_of(i * tk, tk), tk)

            def add_dq(i, val):
                dq_ref[rows_of(i), :] += val

            def add_dfq(i, val, head=head):
                dfq_ref[i] = _put_row(dfq_ref[i], head, val)

            dk_a, dv_a, dfk_a = _attn_bwd_blocks(
                j, nk, tk, lambda i: q_ref[rows_of(i), :], lambda i: do_ref[rows_of(i), :], ka, va,
                lambda i, head=head: (_pick_row(lr_ref[i], head), _pick_row(dr_ref[i], head)), scale,
                lambda i, head=head, fk=fk: (_pick_row(cr_ref[i], head), fk), (add_dq, add_dfq))
            dks.append(dk_a)
            dvs.append(dv_a)
            dfk_row = jnp.broadcast_to(dfk_a, (tk, 128)).T[0:1, :]
            dfk_ref[j] = _put_row(dfk_ref[j], head, dfk_row)
        dk_ref[...] = jnp.where(low, dks[0], dks[1]).astype(BF16)
        dv_ref[...] = jnp.where(low, dvs[0], dvs[1]).astype(BF16)

    rowsp = pl.BlockSpec((nk, 16, tk), lambda b, hp, j: (b, 0, 0))
    return pl.pallas_call(
        body, name=name, grid=(nb, npairs, nk),
        in_specs=[pl.BlockSpec((s, 128), lambda b, hp, j: (b, hp)),
                  pl.BlockSpec((tk, 128), lambda b, hp, j: (b * nk + j, npairs + hp)),
                  pl.BlockSpec((tk, 128), lambda b, hp, j: (b * nk + j, 2 * npairs + hp)),
                  pl.BlockSpec((s, 128), lambda b, hp, j: (b, hp)),
                  pl.BlockSpec((tk, 128), lambda b, hp, j: (b * nk + j, 0)),
                  rowsp, rowsp, rowsp],
        out_specs=[pl.BlockSpec((s, 128), lambda b, hp, j: (b, hp)),
                   pl.BlockSpec((tk, 128), lambda b, hp, j: (b * nk + j, hp)),
                   pl.BlockSpec((tk, 128), lambda b, hp, j: (b * nk + j, hp)),
                   rowsp, rowsp],
        out_shape=[_sds((t, D_MODEL), F32), _sds((t, D_MODEL), BF16), _sds((t, D_MODEL), BF16),
                   _sds((t // tk, 16, tk), F32), _sds((t // tk, 16, tk), F32)],
        compiler_params=_cp(3),
    )(qkv, qkv, qkv, do, cum, cum_rows, lse_rows, delta_rows)


def mla_attn_bwd(q, kn, kr2, v, do, lse_rows, delta_rows, nb, name):
    t = q.shape[0]
    s = t // nb
    tk = ATTN_TILE
    nk = s // tk
    scale = (MLA_NOPE + MLA_ROPE) ** -0.5

    def body(qn_ref, qr_ref, kn_ref, kr_ref, v_ref, do_ref, lr_ref, dr_ref, dqn_ref, dqr_ref, dkn_ref, dkr_ref, dv_ref):
        h = pl.program_id(1)
        j = pl.program_id(2)

        @pl.when(j == 0)
        def _():
            dqn_ref[...] = jnp.zeros_like(dqn_ref)
            dqr_ref[...] = jnp.zeros_like(dqr_ref)

        k_cat = jnp.concatenate([kn_ref[...], kr_ref[...]], axis=1)
        mine = (lax.broadcasted_iota(jnp.int32, (tk, 128), 1) // MLA_ROPE) == (h % 2)

        def rows_of(i):
            return pl.ds(pl.multiple_of(i * tk, tk), tk)

        def q_fn(i):
            qr = qr_ref[rows_of(i), :]
            return jnp.concatenate([qn_ref[rows_of(i), :], jnp.where(mine, qr, jnp.zeros_like(qr))], axis=1)

        def add_dq(i, val):
            dqn_ref[rows_of(i), :] += val[:, :MLA_NOPE]
            dqr_ref[rows_of(i), :] += val[:, MLA_NOPE:]

        dk_acc, dv_acc, _ = _attn_bwd_blocks(
            j, nk, tk, q_fn, lambda i: do_ref[rows_of(i), :], k_cat, v_ref[...],
            lambda i: (_pick_row(lr_ref[i], h), _pick_row(dr_ref[i], h)), scale, None, (add_dq, None))
        dkn_ref[...] = dk_acc[:, :MLA_NOPE].astype(BF16)
        dkr_ref[...] = dk_acc[:, MLA_NOPE:].astype(BF16)
        dv_ref[...] = dv_acc.astype(BF16)

    full = pl.BlockSpec((s, 128), lambda b, h, j: (b, h))
    blk = pl.BlockSpec((tk, 128), lambda b, h, j: (b * nk + j, h))
    rowsp = pl.BlockSpec((nk, 16, tk), lambda b, h, j: (b, 0, 0))
    wide = MLA_HEADS * MLA_V
    return pl.pallas_call(
        body, name=name, grid=(nb, MLA_HEADS, nk),
        in_specs=[full, pl.BlockSpec((s, 128), lambda b, h, j: (b, MLA_HEADS + h // 2)), blk,
                  pl.BlockSpec((tk, 128), lambda b, h, j: (b * nk + j, 0)), blk, full, rowsp, rowsp],
        out_specs=[full, full, blk, blk, blk],
        out_shape=[_sds((t, wide), F32), _sds((t, wide), F32), _sds((t, wide), BF16), _sds((t, wide), BF16),
                   _sds((t, wide), BF16)],
        compiler_params=_cp(3),
    )(q, q, kn, kr2, v, do, lse_rows, delta_rows)


def mla_mid_bwd(dqn, dqr, dkn, dv, dkr_heads, h, g_q, g_kv, w_uq, w_uk, w_uv, cos8, sin8, cos64, sin64s, swap64,
                heads_to_rope, head_sum, name):
    t = h.shape[0]
    tm = TOKEN_TILE
    hq = MLA_HEADS * MLA_NOPE
    hr = MLA_HEADS * MLA_ROPE // 2
    nq = w_uq.shape[1]

    def body(dqn_ref, dqr_ref, dkn_ref, dv_ref, dkr_ref, h_ref, gq_ref, gkv_ref, wuq_ref, wuk_ref, wuv_ref,
             c8_ref, s8_ref, c64_ref, s64_ref, sw_ref, hp_ref, hs_ref, dh_ref, dqp_ref, dgq_ref, dgkv_ref):
        @pl.when(pl.program_id(0) == 0)
        def _():
            dgq_ref[...] = jnp.zeros_like(dgq_ref)
            dgkv_ref[...] = jnp.zeros_like(dgkv_ref)

        drot = _dot(dqr_ref[...].astype(BF16), hp_ref[...])
        o1 = drot[:, :hr]
        o2 = drot[:, hr:]
        cs = c8_ref[...]
        sn = s8_ref[...]
        dqp = jnp.concatenate([dqn_ref[...].astype(BF16), (o1 * cs + o2 * sn).astype(BF16),
                               (o2 * cs - o1 * sn).astype(BF16)], axis=1)
        dqp_ref[...] = dqp
        dcq = _dot_nt(dqp, wuq_ref[...])
        dckv = _dot_nt(dkn_ref[...], wuk_ref[...]) + _dot_nt(dv_ref[...], wuv_ref[...])
        hh = h_ref[...]

        def rms_bwd(hpart, g, dc, dg_ref):
            hhat, rstd = _rms(hpart, None)
            dg_ref[...] += jnp.sum(dc * hhat, axis=0, keepdims=True)
            dcg = dc * g
            return rstd * (dcg - hhat * jnp.mean(dcg * hhat, axis=-1, keepdims=True))

        dhq = rms_bwd(hh[:, :MLA_QR], gq_ref[...], dcq, dgq_ref)
        dhkv = rms_bwd(hh[:, MLA_QR:MLA_QR + MLA_KVR], gkv_ref[...], dckv, dgkv_ref)
        dkr = _dot(dkr_ref[...], hs_ref[...])
        dkr_pre = dkr * c64_ref[...] + _dot_f32(dkr * s64_ref[...], sw_ref[...])
        dh_ref[...] = jnp.concatenate([dhq, dhkv, dkr_pre], axis=1).astype(BF16)

    def rows(n):
        return pl.BlockSpec((tm, n), lambda i: (i, 0))

    def whole(a):
        return pl.BlockSpec(a.shape, lambda i: (0,) * a.ndim)

    return pl.pallas_call(
        body, name=name, grid=(t // tm,),
        in_specs=[rows(hq), rows(hq), rows(hq), rows(hq), rows(hq), rows(h.shape[1]), whole(g_q), whole(g_kv),
                  whole(w_uq), whole(w_uk), whole(w_uv), rows(hr), rows(hr), rows(MLA_ROPE), rows(MLA_ROPE),
                  whole(swap64), whole(heads_to_rope), whole(head_sum)],
        out_specs=[rows(h.shape[1]), rows(nq), pl.BlockSpec((1, MLA_QR), lambda i: (0, 0)),
                   pl.BlockSpec((1, MLA_KVR), lambda i: (0, 0))],
        out_shape=[_sds((t, h.shape[1]), BF16), _sds((t, nq), BF16), _sds((1, MLA_QR), F32), _sds((1, MLA_KVR), F32)],
        compiler_params=_cp(1),
    )(dqn, dqr, dkn, dv, dkr_heads, h, g_q, g_kv, w_uq, w_uk, w_uv, cos8, sin8, cos64, sin64s, swap64,
      heads_to_rope, head_sum)


def fox_gate_bwd(dcum, hf, b_f, triu, n_batch, name):
    t, n = hf.shape
    blk = triu.shape[0]
    nb = (t // n_batch) // blk

    def body(dc_ref, hf_ref, b_ref, tri_ref, o_ref, db_ref, carry_ref):
        @pl.when(pl.program_id(1) == 0)
        def _():
            carry_ref[...] = jnp.zeros_like(carry_ref)

        @pl.when((pl.program_id(0) == 0) & (pl.program_id(1) == 0))
        def _():
            db_ref[...] = jnp.zeros_like(db_ref)

        rc = _dot_f32(tri_ref[...], dc_ref[...]) + carry_ref[...]
        carry_ref[...] = rc[0:1, :]
        dhf = rc * jax.nn.sigmoid(-(hf_ref[...] + b_ref[...]))
        o_ref[...] = dhf.astype(BF16)
        db_ref[...] += jnp.sum(dhf, axis=0, keepdims=True)

    rev = pl.BlockSpec((blk, n), lambda bb, i: (bb * nb + nb - 1 - i, 0))
    return pl.pallas_call(
        body, name=name, grid=(n_batch, nb),
        in_specs=[rev, rev, pl.BlockSpec((1, n), lambda bb, i: (0, 0)), pl.BlockSpec((blk, blk), lambda bb, i: (0, 0))],
        out_specs=[rev, pl.BlockSpec((1, n), lambda bb, i: (0, 0))],
        out_shape=[_sds((t, n), BF16), _sds((1, n), F32)], scratch_shapes=[pltpu.VMEM((1, n), F32)],
        compiler_params=_cp(2),
    )(dcum, hf, b_f, triu)


def wgrad(a, bm, name, slot=None, bt=512):
    ca, t, kd = a.shape
    cb, _, nd = bm.shape
    c = max(ca, cb)
    bn = nd
    if nd > 1024 and nd % 1024 == 0:
        bn = 1024

    def body(*refs):
        a_ref, b_ref, o_ref = refs[0], refs[1], refs[-1]

        @pl.when(pl.program_id(2) == 0)
        def _():
            o_ref[...] = jnp.zeros_like(o_ref)

        o_ref[...] += _dot_tn(a_ref[...].astype(BF16), b_ref[...].astype(BF16))

    in_specs = [pl.BlockSpec((None, bt, kd), lambda cc, n, tt: (cc if ca > 1 else 0, tt, 0)),
                pl.BlockSpec((None, bt, bn), lambda cc, n, tt: (cc if cb > 1 else 0, tt, n))]
    args = [a, bm]
    aliases = {}
    if slot is None:
        out_spec = pl.BlockSpec((None, kd, bn), lambda cc, n, tt: (cc, 0, n))
        out_shape = _sds((c, kd, nd), F32)
    else:
        layer, n_layers, buf = slot
        out_spec = pl.BlockSpec((None, None, kd, bn), lambda cc, n, tt: (cc, layer, 0, n))
        out_shape = _sds((c, n_layers, kd, nd), F32)
        if buf is not None:
            in_specs.append(pl.BlockSpec(memory_space=pl.ANY))
            args.append(buf)
            aliases = {2: 0}
    return pl.pallas_call(
        body, name=name, grid=(c, nd // bn, t // bt), in_specs=in_specs, out_specs=out_spec, out_shape=out_shape,
        input_output_aliases=aliases, compiler_params=_cp(3),
    )(*args)


def ada_mod_part(c_all, ada_w, name):
    nl, d, n = ada_w.shape
    rows = c_all.shape[0]
    tn = 512

    def body(c_ref, w_ref, o_ref):
        cv = c_ref[...]
        act = (cv * jax.nn.sigmoid(cv)).astype(BF16)
        o_ref[...] = _dot(act, w_ref[...].astype(BF16))

    return pl.pallas_call(
        body, name=name, grid=(nl, n // tn),
        in_specs=[pl.BlockSpec((rows, d), lambda l, j: (0, 0)), pl.BlockSpec((None, d, tn), lambda l, j: (l, 0, j))],
        out_specs=pl.BlockSpec((None, rows, tn), lambda l, j: (l, 0, j)),
        out_shape=_sds((nl, rows, n), F32), compiler_params=_cp(2),
    )(c_all, ada_w)


def ada_grad(c_all_t, dmod, name):
    nl, rows, n = dmod.shape
    d = c_all_t.shape[0]
    tn = 512

    def body(c_ref, dm_ref, o_ref):
        cv = c_ref[...]
        act = (cv * jax.nn.sigmoid(cv)).astype(BF16)
        o_ref[...] = _dot(act, dm_ref[...].astype(BF16))

    return pl.pallas_call(
        body, name=name, grid=(nl, n // tn),
        in_specs=[pl.BlockSpec((d, rows), lambda l, j: (0, 0)), pl.BlockSpec((None, rows, tn), lambda l, j: (l, 0, j))],
        out_specs=pl.BlockSpec((None, d, tn), lambda l, j: (l, 0, j)),
        out_shape=_sds((nl, d, n), F32), compiler_params=_cp(2),
    )(c_all_t, dmod)


def sum_leading(a, name):
    g, r, n = a.shape

    def body(a_ref, o_ref):
        acc = a_ref[0]
        for kk in range(1, g):
            acc = acc + a_ref[kk]
        o_ref[...] = acc

    return pl.pallas_call(
        body, name=name, grid=(1,), in_specs=[pl.BlockSpec((g, r, n), lambda i: (0, 0, 0))],
        out_specs=pl.BlockSpec((r, n), lambda i: (0, 0)), out_shape=_sds((r, n), F32), compiler_params=_cp(1),
    )(a)


def adamw(w, g, m, v, name):
    r, n = w.shape
    br = r
    for cand in (512, 256, 128, 64, 32, 16, 8):
        if r % cand == 0 and r > cand and cand * n * 4 <= ADAMW_BLOCK_BYTES:
            br = cand
            break
    c1 = 1.0 - ADAM_B1 ** ADAM_STEP
    c2 = 1.0 - ADAM_B2 ** ADAM_STEP

    def body(w_ref, g_ref, m_ref, v_ref, d_ref, mo_ref, vo_ref):
        gv = g_ref[...]
        mn = ADAM_B1 * m_ref[...] + (1.0 - ADAM_B1) * gv
        vn = ADAM_B2 * v_ref[...] + (1.0 - ADAM_B2) * (gv * gv)
        mo_ref[...] = mn
        vo_ref[...] = vn
        d_ref[...] = -ADAM_LR * ((mn / c1) / (jnp.sqrt(vn / c2) + ADAM_EPS) + ADAM_WD * w_ref[...])

    spec = pl.BlockSpec((br, n), lambda i: (i, 0))
    return pl.pallas_call(
        body, name=name, grid=(r // br,), in_specs=[spec] * 4, out_specs=[spec] * 3,
        out_shape=[_sds((r, n), F32)] * 3, compiler_params=_cp(1),
    )(w, g, m, v)


def all_gather8(x_blk, name):
    m_per, n = x_blk.shape

    def body(x_ref, out_ref, send_sems, recv_sems, local_sem):
        x, y, c = _place()
        me, sibling = (x, y, c), (x, y, 1 - c)
        chips = [(1 - x, y), (x, 1 - y), (1 - x, 1 - y)]

        def rows(px, py, pc):
            return out_ref.at[pl.ds((4 * px + 2 * py + pc) * m_per, m_per), :]

        def copy(k, block, to, src=None):
            return pltpu.make_async_remote_copy(
                src_ref=rows(*block) if src is None else src, dst_ref=rows(*block),
                send_sem=send_sems.at[k], recv_sem=recv_sems.at[k], device_id=to, device_id_type=MESH)

        mine = pltpu.make_async_copy(x_ref, rows(*me), local_sem)
        mine.start()
        first = [copy(0, me, sibling, src=x_ref)]
        first += [copy(1 + j, me, (*chip, c), src=x_ref) for j, chip in enumerate(chips)]
        for cp in first:
            cp.start()
        passed = [copy(4 + j, (*chip, c), sibling) for j, chip in enumerate(chips)]
        for j, chip in enumerate(chips):
            copy(1 + j, (*chip, c), me).wait_recv()
            passed[j].start()
        copy(0, sibling, me).wait_recv()
        for j, chip in enumerate(chips):
            copy(4 + j, (*chip, 1 - c), me).wait_recv()
        for cp in first + passed:
            cp.wait_send()
        mine.wait()

    return pl.pallas_call(
        body, name=name, out_shape=_sds((8 * m_per, n), x_blk.dtype),
        in_specs=[pl.BlockSpec(memory_space=pltpu.VMEM)], out_specs=pl.BlockSpec(memory_space=pltpu.VMEM),
        scratch_shapes=[pltpu.SemaphoreType.DMA((7,)), pltpu.SemaphoreType.DMA((7,)), pltpu.SemaphoreType.DMA],
        compiler_params=pltpu.CompilerParams(vmem_limit_bytes=VMEM_LIMIT),
    )(x_blk)


def _gather_comm(shards):
    nt = len(shards)

    def parts(w_refs, out_refs, sems):
        send_sems, recv_sems, own_send, own_recv = sems
        x, y, c = _place()
        sibling = (x, y, 1 - c)
        chips = [(1 - x, y), (x, 1 - y), (1 - x, 1 - y)]

        def copy(t, k, block, to, src=None):
            px, py, hh = block
            dst = out_refs[t].at[2 * px + py, hh]
            return pltpu.make_async_remote_copy(
                src_ref=dst if src is None else src, dst_ref=dst,
                send_sem=send_sems.at[6 * t + k], recv_sem=recv_sems.at[6 * t + k], device_id=to, device_id_type=MESH)

        own = [pltpu.make_async_remote_copy(
            src_ref=w_refs[t], dst_ref=out_refs[t].at[2 * x + y], send_sem=own_send.at[t], recv_sem=own_recv.at[t],
            device_id=sibling, device_id_type=MESH) for t in range(nt)]
        first = [copy(t, j, (x, y, c), (*chip, c), src=w_refs[t].at[c]) for t in range(nt) for j, chip in enumerate(chips)]
        landed = [copy(t, j, (*chip, c), (x, y, c)) for t in range(nt) for j, chip in enumerate(chips)]
        passed = [copy(t, 3 + j, (*chip, c), sibling) for t in range(nt) for j, chip in enumerate(chips)]
        from_sibling = [copy(t, 3 + j, (*chip, 1 - c), (x, y, c)) for t in range(nt) for j, chip in enumerate(chips)]
        return own, first, landed, passed, from_sibling

    def start(w_refs, out_refs, sems):
        own, first, _, _, _ = parts(w_refs, out_refs, sems)
        for cp in own + first:
            cp.start()

    def finish(w_refs, out_refs, sems):
        own, first, landed, passed, from_sibling = parts(w_refs, out_refs, sems)
        for arrived, fwd in zip(landed, passed):
            arrived.wait_recv()
            fwd.start()
        for cp in from_sibling:
            cp.wait_recv()
        for cp in first + passed:
            cp.wait_send()
        for cp in own:
            cp.wait()

    sems = [pltpu.SemaphoreType.DMA((6 * nt,)), pltpu.SemaphoreType.DMA((6 * nt,)),
            pltpu.SemaphoreType.DMA((nt,)), pltpu.SemaphoreType.DMA((nt,))]
    return _Hosted(list(shards), [_sds((N_CHIPS, *w.shape), w.dtype) for w in shards], sems, start, finish)


def all_gather_chips(shards, name):
    comm = _gather_comm(shards)
    nt = len(shards)

    def body(*refs):
        comm.start(refs[:nt], refs[nt:2 * nt], refs[2 * nt:])
        comm.finish(refs[:nt], refs[nt:2 * nt], refs[2 * nt:])

    hbm = pl.BlockSpec(memory_space=pl.ANY)
    return pl.pallas_call(body, name=name, out_shape=comm.out_shape, in_specs=[hbm] * nt, out_specs=[hbm] * nt,
                          scratch_shapes=comm.sems)(*shards)


def sibling_swap_halves(grads, name):
    nt = len(grads)

    def body(*refs):
        g_refs, a_refs = refs[:nt], refs[nt:2 * nt]
        send_sems, recv_sems = refs[2 * nt:]
        x, y, c = _place()
        cps = [pltpu.make_async_remote_copy(
            src_ref=g_refs[t].at[j, 1 - c], dst_ref=a_refs[t].at[j], send_sem=send_sems.at[N_CHIPS * t + j],
            recv_sem=recv_sems.at[N_CHIPS * t + j], device_id=(x, y, 1 - c), device_id_type=MESH)
            for t in range(nt) for j in range(N_CHIPS)]
        for cp in cps:
            cp.start()
        for cp in cps:
            cp.wait()

    hbm = pl.BlockSpec(memory_space=pl.ANY)
    return pl.pallas_call(
        body, name=name, out_shape=[_sds((N_CHIPS, *g.shape[2:]), g.dtype) for g in grads],
        in_specs=[hbm] * nt, out_specs=[hbm] * nt,
        scratch_shapes=[pltpu.SemaphoreType.DMA((N_CHIPS * nt,)), pltpu.SemaphoreType.DMA((N_CHIPS * nt,))],
    )(*grads)


def _row_block(r, n, itemsize):
    best = None
    for br in range(16, r + 1, 16):
        if r % br == 0 and br * n * itemsize <= COMM_BLOCK_BYTES:
            best = br
    assert best is not None, (r, n)
    return best


def add_own_half(g, recv, core, name):
    nch, _, r, n = g.shape
    br = _row_block(r, n, 4)

    def body(c_ref, g_ref, a_ref, o_ref):
        o_ref[...] = (g_ref[...] + a_ref[...]).astype(BF16)

    return pl.pallas_call(
        body, name=name,
        grid_spec=pltpu.PrefetchScalarGridSpec(
            num_scalar_prefetch=1, grid=(nch, r // br),
            in_specs=[pl.BlockSpec((None, None, br, n), lambda j, rr, cref: (j, cref[0], rr, 0)),
                      pl.BlockSpec((None, br, n), lambda j, rr, cref: (j, rr, 0))],
            out_specs=pl.BlockSpec((None, br, n), lambda j, rr, cref: (j, rr, 0))),
        out_shape=_sds((nch, r, n), BF16), compiler_params=_cp(2),
    )(core, g, recv)


def chip_exchange(sums, name):
    nt = len(sums)

    def body(*refs):
        s_refs, b_refs = refs[:nt], refs[nt:2 * nt]
        send_sems, recv_sems = refs[2 * nt:]
        x, y, c = _place()
        k = 2 * x + y
        chips = [(1 - x, y), (x, 1 - y), (1 - x, 1 - y)]

        def copy(t, j, src_idx, dst_idx):
            px, py = chips[j]
            return pltpu.make_async_remote_copy(
                src_ref=s_refs[t].at[src_idx], dst_ref=b_refs[t].at[dst_idx], send_sem=send_sems.at[3 * t + j],
                recv_sem=recv_sems.at[3 * t + j], device_id=(px, py, c), device_id_type=MESH)

        cps = [copy(t, j, 2 * chips[j][0] + chips[j][1], k) for t in range(nt) for j in range(3)]
        for cp in cps:
            cp.start()
        for t in range(nt):
            for j in range(3):
                copy(t, j, k, 2 * chips[j][0] + chips[j][1]).wait_recv()
        for cp in cps:
            cp.wait_send()

    hbm = pl.BlockSpec(memory_space=pl.ANY)
    return pl.pallas_call(
        body, name=name, out_shape=[_sds(sm.shape, sm.dtype) for sm in sums],
        in_specs=[hbm] * nt, out_specs=[hbm] * nt,
        scratch_shapes=[pltpu.SemaphoreType.DMA((3 * nt,)), pltpu.SemaphoreType.DMA((3 * nt,))],
    )(*sums)


def sum_pieces(own, recv, place, name):
    nch, r, n = own.shape
    br = _row_block(r, n, 4 * nch)

    def body(p_ref, o_ref, r1_ref, r2_ref, r3_ref, out_ref):
        out_ref[...] = ((o_ref[...].astype(F32) + r1_ref[...].astype(F32)) + r2_ref[...].astype(F32)) + r3_ref[...].astype(F32)

    def piece(step):
        return pl.BlockSpec((None, br, n), lambda i, pref: ((pref[1] + step) % nch, i, 0))

    return pl.pallas_call(
        body, name=name,
        grid_spec=pltpu.PrefetchScalarGridSpec(
            num_scalar_prefetch=1, grid=(r // br,), in_specs=[piece(0), piece(1), piece(2), piece(3)],
            out_specs=pl.BlockSpec((None, br, n), lambda i, pref: (pref[0], i, 0))),
        out_shape=_sds((2, r, n), F32), compiler_params=_cp(1),
    )(place, own, recv, recv, recv)


def sibling_join_halves(halves, name):
    nt = len(halves)

    def body(*refs):
        o_refs = refs[nt:2 * nt]
        send_sems, recv_sems = refs[2 * nt:]
        x, y, c = _place()

        def copy(t, hh):
            return pltpu.make_async_remote_copy(
                src_ref=o_refs[t].at[hh], dst_ref=o_refs[t].at[hh], send_sem=send_sems.at[t], recv_sem=recv_sems.at[t],
                device_id=(x, y, 1 - c), device_id_type=MESH)

        cps = [copy(t, c) for t in range(nt)]
        for cp in cps:
            cp.start()
        for t in range(nt):
            copy(t, 1 - c).wait_recv()
        for cp in cps:
            cp.wait_send()

    hbm = pl.BlockSpec(memory_space=pl.ANY)
    return pl.pallas_call(
        body, name=name, out_shape=[_sds(f.shape, f.dtype) for f in halves],
        in_specs=[hbm] * nt, out_specs=[hbm] * nt, input_output_aliases={t: t for t in range(nt)},
        scratch_shapes=[pltpu.SemaphoreType.DMA((nt,)), pltpu.SemaphoreType.DMA((nt,))],
    )(*halves)


_SHARD_KIND = {"mla_w_in": "rows", "mla_w_uq": "cols", "mla_w_uk": "cols", "mla_w_uv": "cols", "mla_w_o": "rows",
               "fox_w_in": "cols", "fox_w_o": "rows", "ffn_w_gate": "chunk", "ffn_w_up": "chunk", "ffn_w_down": "chunk"}
_PACKED = tuple(_SHARD_KIND)


def _halves(shard):
    if shard.ndim == 3 and shard.shape[0] == 2:
        return shard
    r, n = shard.shape[-2:]
    return shard.reshape(2, r // 2, n)


def _cols_to_full(g):
    return jnp.transpose(g, (1, 0, 2)).reshape(g.shape[1], -1)


def _full_to_cols(w):
    k, n4 = w.shape
    return jnp.transpose(w.reshape(k, N_CHIPS, n4 // N_CHIPS), (1, 0, 2))


def _uq_perm():
    per = MLA_NOPE + MLA_ROPE
    half = MLA_ROPE // 2
    nope = [h * per + d for h in range(MLA_HEADS) for d in range(MLA_NOPE)]
    r1 = [h * per + MLA_NOPE + r for h in range(MLA_HEADS) for r in range(half)]
    r2 = [h * per + MLA_NOPE + half + r for h in range(MLA_HEADS) for r in range(half)]
    perm = np.array(nope + r1 + r2, dtype=np.int32)
    return perm, np.argsort(perm).astype(np.int32)


def _rope_matrices():
    half = MLA_ROPE // 2
    nr = MLA_HEADS * MLA_ROPE
    to_heads = np.zeros((nr, nr), np.float32)
    from_heads = np.zeros((MLA_HEADS * 128, nr), np.float32)
    for e in range(2):
        for h in range(MLA_HEADS):
            for r in range(half):
                to_heads[e * MLA_HEADS * half + h * half + r, h * MLA_ROPE + e * half + r] = 1.0
                from_heads[h * 128 + e * half + r, e * MLA_HEADS * half + h * half + r] = 1.0
    head_sum = np.tile(np.eye(MLA_ROPE, dtype=np.float32), (2 * MLA_HEADS, 1))
    dup = np.concatenate([np.eye(MLA_ROPE, dtype=np.float32)] * 2, axis=1)
    return to_heads, from_heads, head_sum, dup


def _ffn_weights(gathered):
    return tuple(g.reshape(N_CHIPS, 2 * g.shape[2], g.shape[3]) for g in gathered)


def _fox_weights(gathered):
    w_in, w_o = gathered
    w_in = _cols_to_full(w_in.reshape(N_CHIPS, 2 * w_in.shape[2], w_in.shape[3]))
    return w_in, w_o.reshape(-1, w_o.shape[-1])


def _local_step(x, positions, target, mods, wts, ln_g, ln_b, mla_g_q, mla_g_kv, fox_b_f, shards=None):
    nb, s, d = x.shape
    t = nb * s
    x0 = x.reshape(t, d)
    tgt = target.reshape(t, d)
    perm, inv_perm = _uq_perm()

    half = MLA_ROPE // 2
    inv_freq = ROPE_THETA ** (-jnp.arange(half, dtype=F32) / half)
    ang = positions.astype(F32).reshape(t, 1) * inv_freq
    cos, sin = jnp.cos(ang), jnp.sin(ang)
    cos8, sin8 = jnp.tile(cos, (1, MLA_HEADS)), jnp.tile(sin, (1, MLA_HEADS))
    cos64 = jnp.concatenate([cos, cos], axis=1)
    sin64s = jnp.concatenate([-sin, sin], axis=1)
    swap64 = jnp.asarray(np.roll(np.eye(MLA_ROPE, dtype=np.float32), half, axis=1))
    to_heads, from_heads, head_sum, dup = _rope_matrices()
    to_heads, from_heads = jnp.asarray(to_heads, dtype=BF16), jnp.asarray(from_heads, dtype=BF16)
    head_sum, dup = jnp.asarray(head_sum, dtype=BF16), jnp.asarray(dup, dtype=BF16)
    sel_mla = jnp.asarray(np.pad(np.kron(np.eye(MLA_HEADS, dtype=np.float32), np.ones((MLA_V, 1), np.float32)),
                                 ((0, 0), (0, 128 - MLA_HEADS))))
    sel_fox = jnp.asarray(np.pad(np.kron(np.eye(FOX_HEADS, dtype=np.float32), np.ones((FOX_HD, 1), np.float32)),
                                 ((0, 0), (0, 128 - FOX_HEADS))))
    tri = jnp.asarray(np.tril(np.ones((128, 128), np.float32)))
    triu = jnp.asarray(np.triu(np.ones((128, 128), np.float32)))
    onehot16 = jnp.asarray(np.eye(16, 128, dtype=np.float32))

    def vec(a):
        return a.reshape(1, -1)

    def carried(key):
        return None if shards is None else _gather_comm(shards[key])

    def split(res):
        return (res, None) if shards is None else res

    w_uq_p = wts["mla_w_uq"][:, perm]
    b_f_pad = jnp.pad(fox_b_f.reshape(1, -1), ((0, 0), (0, 128 - FOX_HEADS)))

    sh_a, sc_a, gt_a, sh_f, sc_f, gt_f = mods[0]
    h_in, u_m = mod_linear(x0, sh_a, sc_a, wts["mla_w_in"], F32, "mla_in", emit_u=True)
    q_m, kn_m, v_m, kr2_m, cq_m, ckv_m = mla_mid_fwd(
        h_in, vec(mla_g_q), vec(mla_g_kv), w_uq_p, wts["mla_w_uk"], wts["mla_w_uv"], cos8, sin8, cos64, sin64s, swap64,
        to_heads, dup, "mla_mid")
    (o_m, lse_m), got = split(mla_attn_fwd(q_m, kn_m, kr2_m, v_m, nb, "mla_attn", hosted=carried("ffn0")))
    ffn0_w = wts["ffn"][0] if got is None else _ffn_weights(got)
    y0, x1 = linear_resid_ln(o_m, wts["mla_w_o"], x0, gt_a, vec(ln_g[0, 0]), vec(ln_b[0, 0]), "mla_out")
    (u_f0, hg0, hu0, y1, x2), got = split(ffn_fwd(x1, sh_f, sc_f, gt_f, *ffn0_w, vec(ln_g[0, 1]), vec(ln_b[0, 1]), "ffn0",
                                                  hosted=carried("fox")))
    fox_w_in, fox_w_o = (wts["fox_w_in"], wts["fox_w_o"]) if got is None else _fox_weights(got)
    fox_w_qkv = fox_w_in[:, :3 * d]
    fox_w_f = jnp.pad(fox_w_in[:, 3 * d:], ((0, 0), (0, 128 - FOX_HEADS)))
    sh_a1, sc_a1, gt_a1, sh_f1, sc_f1, gt_f1 = mods[1]
    qkv, u_x = mod_linear(x2, sh_a1, sc_a1, fox_w_qkv, BF16, "fox_qkv", tn=1024, emit_u=True)
    hf = mod_linear(x2, sh_a1, sc_a1, fox_w_f, F32, "fox_f")
    cum = fox_gate_fwd(hf, b_f_pad, tri, nb, "fox_gate")
    cum_rows = rows16(cum, "fox_cum_rows")
    (o_x, lse_x), got = split(fox_attn_fwd(qkv, cum, cum_rows, nb, "fox_attn", hosted=carried("ffn1")))
    ffn1_w = wts["ffn"][1] if got is None else _ffn_weights(got)
    y2, x3 = linear_resid_ln(o_x, fox_w_o, x2, gt_a1, vec(ln_g[1, 0]), vec(ln_b[1, 0]), "fox_out")
    u_f1, hg1, hu1, y3, x4 = ffn_fwd(x3, sh_f1, sc_f1, gt_f1, *ffn1_w, vec(ln_g[1, 1]), vec(ln_b[1, 1]), "ffn1")
    dx4, sq_err = loss_grad(x4, tgt, "loss")
    loss_part = 0.5 * jnp.sum(sq_err) / d

    grads = {}
    dz3, dy3, dg11, db11, dgt_f1 = ln_bwd(dx4, x3, y3, gt_f1, vec(ln_g[1, 1]), "ffn1_ln_bwd")
    dhg1, dhu1, act1, dx3, dsc_f1, dsh_f1 = ffn_bwd(dy3, hg1, hu1, *ffn1_w, dz3, x3, sc_f1, "ffn1_bwd")
    dwg1 = wgrad(u_f1[None], dhg1, "ffn1_dwg", slot=(1, DEPTH, None))
    dwu1 = wgrad(u_f1[None], dhu1, "ffn1_dwu", slot=(1, DEPTH, None))
    dwd1 = wgrad(act1, dy3[None], "ffn1_dwd", slot=(1, DEPTH, None))
    dz2, dy2, dg10, db10, dgt_a1 = ln_bwd(dx3, x2, y2, gt_a1, vec(ln_g[1, 0]), "fox_ln_bwd")
    do_x, delta_x = linear_nt_delta(dy2, fox_w_o, o_x, sel_fox, "fox_out_bwd")
    grads["fox_w_o"] = wgrad(o_x[None], dy2[None], "fox_dwo")[0]
    dq_x, dk_x, dv_x, dfq_x, dfk_x = fox_attn_bwd(qkv, do_x, cum, cum_rows, rows16(lse_x, "fox_lse_rows"),
                                                  rows16(delta_x, "fox_delta_rows"), nb, "fox_attn_bwd")
    dcum = tokens128(dfq_x + dfk_x, onehot16, "fox_dcum")
    dhf, dbf = fox_gate_bwd(dcum, hf, b_f_pad, triu, nb, "fox_gate_bwd")
    dqkv = jnp.concatenate([dq_x.astype(BF16), dk_x, dv_x], axis=1)
    dx2, dsc_a1, dsh_a1 = linear_nt_mod_bwd([(dqkv, fox_w_qkv), (dhf, fox_w_f)], dz2, x2, sc_a1, "fox_in_bwd")
    dw_qkv = wgrad(u_x[None], dqkv[None], "fox_dwqkv")[0]
    dw_f = wgrad(u_x[None], dhf[None], "fox_dwf")[0]
    grads["fox_w_in"] = jnp.concatenate([dw_qkv, dw_f[:, :FOX_HEADS]], axis=1)
    dz1, dy1, dg01, db01, dgt_f0 = ln_bwd(dx2, x1, y1, gt_f, vec(ln_g[0, 1]), "ffn0_ln_bwd")
    dhg0, dhu0, act0, dx1, dsc_f0, dsh_f0 = ffn_bwd(dy1, hg0, hu0, *ffn0_w, dz1, x1, sc_f, "ffn0_bwd")
    grads["ffn_w_gate"] = wgrad(u_f0[None], dhg0, "ffn0_dwg", slot=(0, DEPTH, dwg1))
    grads["ffn_w_up"] = wgrad(u_f0[None], dhu0, "ffn0_dwu", slot=(0, DEPTH, dwu1))
    grads["ffn_w_down"] = wgrad(act0, dy1[None], "ffn0_dwd", slot=(0, DEPTH, dwd1))
    dz0, dy0, dg00, db00, dgt_a0 = ln_bwd(dx1, x0, y0, gt_a, vec(ln_g[0, 0]), "mla_ln_bwd")
    do_m, delta_m = linear_nt_delta(dy0, wts["mla_w_o"], o_m, sel_mla, "mla_out_bwd")
    grads["mla_w_o"] = wgrad(o_m[None], dy0[None], "mla_dwo")[0]
    dqn_m, dqr_m, dkn_m, dkr_m, dv_m = mla_attn_bwd(q_m, kn_m, kr2_m, v_m, do_m, rows16(lse_m, "mla_lse_rows"),
                                                    rows16(delta_m, "mla_delta_rows"), nb, "mla_attn_bwd")
    dh_in, dq_pre, dgq, dgkv = mla_mid_bwd(
        dqn_m, dqr_m, dkn_m, dv_m, dkr_m, h_in, vec(mla_g_q), vec(mla_g_kv), w_uq_p, wts["mla_w_uk"],
        wts["mla_w_uv"], cos8, sin8, cos64, sin64s, swap64, from_heads, head_sum, "mla_mid_bwd")
    grads["mla_w_uq"] = wgrad(cq_m[None], dq_pre[None], "mla_dwuq")[0][:, inv_perm]
    grads["mla_w_uk"] = wgrad(ckv_m[None], dkn_m[None], "mla_dwuk")[0]
    grads["mla_w_uv"] = wgrad(ckv_m[None], dv_m[None], "mla_dwuv")[0]
    grads["mla_w_in"] = wgrad(u_m[None], dh_in[None], "mla_dwin")[0]
    dx0, dsc_a0, dsh_a0 = linear_nt_mod_bwd([(dh_in, wts["mla_w_in"])], dz0, x0, sc_a, "mla_in_bwd")

    dmods = [(dsh_a0, dsc_a0, dgt_a0, dsh_f0, dsc_f0, dgt_f0), (dsh_a1, dsc_a1, dgt_a1, dsh_f1, dsc_f1, dgt_f1)]
    d_ln_g = jnp.stack([jnp.concatenate([dg00, dg01], axis=0), jnp.concatenate([dg10, dg11], axis=0)])
    d_ln_b = jnp.stack([jnp.concatenate([db00, db01], axis=0), jnp.concatenate([db10, db11], axis=0)])
    return loss_part, dx0.reshape(nb, s, d), grads, dmods, d_ln_g, d_ln_b, dgq, dgkv, dbf[:, :FOX_HEADS]


def _pad_rows(a, rows):
    return jnp.pad(a, ((0, rows - a.shape[0]), (0, 0)))


def kernel(x, c, positions, mla_w_in, mla_g_q, mla_w_uq, mla_g_kv, mla_w_uk, mla_w_uv, mla_w_o, fox_w_in, fox_b_f, fox_w_o, ada_w, ada_b, ffn_w_gate, ffn_w_up, ffn_w_down, ln_g, ln_b, loss_target, m_mla_w_in, m_mla_g_q, m_mla_w_uq, m_mla_g_kv, m_mla_w_uk, m_mla_w_uv, m_mla_w_o, m_fox_w_in, m_fox_b_f, m_fox_w_o, m_ada_w, m_ada_b, m_ffn_w_gate, m_ffn_w_up, m_ffn_w_down, m_ln_g, m_ln_b, v_mla_w_in, v_mla_g_q, v_mla_w_uq, v_mla_g_kv, v_mla_w_uk, v_mla_w_uv, v_mla_w_o, v_fox_w_in, v_fox_b_f, v_fox_w_o, v_ada_w, v_ada_b, v_ffn_w_gate, v_ffn_w_up, v_ffn_w_down, v_ln_g, v_ln_b):
    args = dict(locals())
    nb, s, d = x.shape
    ax, ay, ac = lax.axis_index("x"), lax.axis_index("y"), lax.axis_index("c")
    chip = 2 * ax + ay
    dev = 2 * chip + ac
    n_dev = 2 * N_CHIPS
    n_all = nb * n_dev

    shard_shapes = {n: (args[n].shape if _SHARD_KIND[n] == "chunk" else args[n].shape[1:]) for n in _PACKED}

    def block(n, layer=None):
        w = args[n].reshape(shard_shapes[n]) if layer is None else args[n][layer]
        return _halves(w.astype(BF16))

    mla_names = [n for n in _PACKED if n.startswith("mla")]
    wts = {}
    for n, g in zip(mla_names, all_gather_chips([block(n) for n in mla_names], "gather_mla")):
        g = g.reshape(N_CHIPS, *shard_shapes[n])
        wts[n] = g.reshape(-1, g.shape[-1]) if _SHARD_KIND[n] == "rows" else _cols_to_full(g)
    ffn_names = ("ffn_w_gate", "ffn_w_up", "ffn_w_down")
    shards = {"ffn0": [block(n, 0) for n in ffn_names], "fox": [block("fox_w_in"), block("fox_w_o")],
              "ffn1": [block(n, 1) for n in ffn_names]}

    c_all = all_gather8(_pad_rows(c, 8), "gather_c").reshape(n_dev, 8, d)[:, :nb].reshape(n_all, d)
    mod_part = ada_mod_part(c_all, ada_w, "ada_mod")
    ncol = mod_part.shape[-1]
    mod_g = all_gather8(mod_part.reshape(DEPTH * n_all, ncol), "gather_mod")
    mod_g = mod_g.reshape(N_CHIPS, 2, DEPTH, n_all, ncol)[:, 0]
    mod_full = jnp.transpose(mod_g, (1, 2, 0, 3)).reshape(DEPTH, n_all, N_CHIPS * ncol) + ada_b[:, None, :]
    mod_loc = lax.dynamic_slice_in_dim(mod_full, dev * nb, nb, axis=1)
    mods = [tuple(mod_loc[i, :, k * d:(k + 1) * d].reshape(nb, 1, d) for k in range(6)) for i in range(DEPTH)]

    ln_cols = ln_g.shape[-1]
    ln_blk = jnp.concatenate([ln_g.reshape(2 * DEPTH, ln_cols), ln_b.reshape(2 * DEPTH, ln_cols)], axis=0)
    ln_all = all_gather8(ln_blk, "gather_ln").reshape(N_CHIPS, 2, 4 * DEPTH, ln_cols)[:, 0]
    ln_all = jnp.transpose(ln_all, (1, 0, 2)).reshape(4 * DEPTH, d)
    ln_g_full = ln_all[:2 * DEPTH].reshape(DEPTH, 2, d)
    ln_b_full = ln_all[2 * DEPTH:].reshape(DEPTH, 2, d)

    loss_part, grad_x, grads, dmods, d_ln_g, d_ln_b, dgq, dgkv, dbf = _local_step(
        x, positions, loss_target, mods, wts, ln_g_full, ln_b_full, mla_g_q[0], mla_g_kv[0], fox_b_f[0], shards)
    loss = lax.psum(loss_part, ("x", "y", "c"))

    dmod_rows = jnp.stack([jnp.concatenate([v_.reshape(nb, d) for v_ in dm], axis=1) for dm in dmods])
    small = jnp.concatenate([
        d_ln_g.reshape(2 * DEPTH, d), d_ln_b.reshape(2 * DEPTH, d),
        jnp.pad(jnp.concatenate([dgq, dgkv, dbf], axis=1), ((0, 0), (0, d - 2 * MLA_QR - FOX_HEADS))),
        dmod_rows.reshape(DEPTH * nb * 6, d)], axis=0)
    n_small = small.shape[0]
    small_rows = -(-n_small // 8) * 8
    small_all = all_gather8(_pad_rows(small, small_rows), "gather_stats").reshape(n_dev, small_rows, d)
    stat_sum = sum_leading(small_all, "sum_stats")
    g_ln_g = lax.dynamic_slice_in_dim(stat_sum[:2 * DEPTH], chip * ln_cols, ln_cols, axis=1).reshape(DEPTH, 2, ln_cols)
    g_ln_b = lax.dynamic_slice_in_dim(stat_sum[2 * DEPTH:4 * DEPTH], chip * ln_cols, ln_cols, axis=1).reshape(DEPTH, 2, ln_cols)
    row = stat_sum[4 * DEPTH]
    g_gq = row[:MLA_QR].reshape(1, MLA_QR)
    g_gkv = row[MLA_QR:2 * MLA_QR].reshape(1, MLA_KVR)
    g_bf = row[2 * MLA_QR:2 * MLA_QR + FOX_HEADS].reshape(1, FOX_HEADS)
    base = 4 * DEPTH + 1
    dmod_all = small_all[:, base:base + DEPTH * nb * 6].reshape(n_dev, DEPTH, nb, 6 * d)
    dmod_all = jnp.transpose(dmod_all, (1, 0, 2, 3)).reshape(DEPTH, n_all, 6 * d)
    g_ada_b = sum_leading(jnp.transpose(dmod_all, (1, 0, 2)), "sum_ada_b")
    dmod_mine = lax.dynamic_slice_in_dim(dmod_all, chip * ncol, ncol, axis=2)
    g_ada_w = ada_grad(c_all.T, dmod_mine, "ada_grad")

    g_blocks = []
    for n in _PACKED:
        g = grads[n]
        if _SHARD_KIND[n] == "rows":
            g = g.reshape(N_CHIPS, 2, g.shape[0] // (2 * N_CHIPS), g.shape[1])
        elif _SHARD_KIND[n] == "cols":
            g = _full_to_cols(g)
            g = g.reshape(N_CHIPS, 2, g.shape[1] // 2, g.shape[2])
        g_blocks.append(g)
    core = ac.reshape(1).astype(jnp.int32)
    recv = sibling_swap_halves(g_blocks, "rs_swap")
    chip_sums = [add_own_half(g, a, core, "rs_add_" + n) for n, g, a in zip(_PACKED, g_blocks, recv)]
    pieces = chip_exchange(chip_sums, "rs_exchange")
    place = jnp.stack([ac, chip]).astype(jnp.int32)
    my_halves = [sum_pieces(sm, b, place, "rs_sum_" + n) for n, sm, b in zip(_PACKED, chip_sums, pieces)]
    joined = sibling_join_halves(my_halves, "rs_join")
    g_big = {n: j.reshape(shard_shapes[n]) for n, j in zip(_PACKED, joined)}

    g_out = {
        "mla_w_in": g_big["mla_w_in"], "mla_g_q": g_gq, "mla_w_uq": g_big["mla_w_uq"], "mla_g_kv": g_gkv,
        "mla_w_uk": g_big["mla_w_uk"], "mla_w_uv": g_big["mla_w_uv"], "mla_w_o": g_big["mla_w_o"],
        "fox_w_in": g_big["fox_w_in"], "fox_b_f": g_bf, "fox_w_o": g_big["fox_w_o"],
        "ada_w": g_ada_w, "ada_b": g_ada_b, "ffn_w_gate": g_big["ffn_w_gate"], "ffn_w_up": g_big["ffn_w_up"],
        "ffn_w_down": g_big["ffn_w_down"], "ln_g": g_ln_g, "ln_b": g_ln_b}
    names = ["mla_w_in", "mla_g_q", "mla_w_uq", "mla_g_kv", "mla_w_uk", "mla_w_uv", "mla_w_o", "fox_w_in", "fox_b_f",
             "fox_w_o", "ada_w", "ada_b", "ffn_w_gate", "ffn_w_up", "ffn_w_down", "ln_g", "ln_b"]
    small_names = ["mla_g_q", "mla_g_kv", "fox_b_f", "ada_b", "ln_g", "ln_b"]
    deltas, new_m, new_v = {}, {}, {}
    for n in names:
        if n in small_names:
            continue
        shp = args[n].shape
        two_d = (-1, shp[-1])
        dl, mn, vn = adamw(args[n].reshape(two_d), g_out[n].reshape(two_d), args["m_" + n].reshape(two_d),
                           args["v_" + n].reshape(two_d), "adamw_" + n)
        deltas[n], new_m[n], new_v[n] = dl.reshape(shp), mn.reshape(shp), vn.reshape(shp)

    def small_pack(prefix, src):
        flat = jnp.concatenate([src[prefix + n].reshape(-1) for n in small_names])
        size = -(-flat.shape[0] // (8 * 128)) * 8 * 128
        return jnp.pad(flat, (0, size - flat.shape[0])).reshape(-1, 128)

    sd, sm, sv = adamw(small_pack("", args), small_pack("", g_out), small_pack("m_", args), small_pack("v_", args),
                       "adamw_small")
    off = 0
    for n in small_names:
        shp = args[n].shape
        size = math.prod(shp)
        deltas[n] = sd.reshape(-1)[off:off + size].reshape(shp)
        new_m[n] = sm.reshape(-1)[off:off + size].reshape(shp)
        new_v[n] = sv.reshape(-1)[off:off + size].reshape(shp)
        off += size

    outs = [loss, grad_x]
    outs += [g_out[n].reshape(args[n].shape) for n in names]
    outs += [deltas[n] for n in names] + [new_m[n] for n in names] + [new_v[n] for n in names]
    return tuple(outs)
```

```python
import functools
import math

import numpy as np
import jax
import jax.numpy as jnp
from jax import lax
from jax.experimental import pallas as pl
from jax.experimental.pallas import tpu as pltpu

F32 = jnp.float32
BF16 = jnp.bfloat16
MESH = pl.DeviceIdType.MESH

D_MODEL = 1024
DEPTH = 2
MLA_HEADS = 8
MLA_NOPE = 128
MLA_ROPE = 64
MLA_V = 128
MLA_QR = 256
MLA_KVR = 256
ROPE_THETA = 10000.0
FOX_HEADS = 16
FOX_HD = 64
D_FF = 2816
N_CHIPS = 4
FF_CHUNK = D_FF // N_CHIPS
ALPHA = (2.0 * DEPTH) ** 0.25
EPS = 1e-5
ADAM_LR = 0.001
ADAM_B1 = 0.9
ADAM_B2 = 0.999
ADAM_EPS = 1e-08
ADAM_WD = 0.01
ADAM_STEP = 10

VMEM_LIMIT = 56 * 1024 * 1024
TOKEN_TILE = 512
ATTN_TILE = 512
COMM_BLOCK_BYTES = 2 * 1024 * 1024
ADAMW_BLOCK_BYTES = 1024 * 1024


def _cp(n_axes):
    return pltpu.CompilerParams(dimension_semantics=("arbitrary",) * n_axes, vmem_limit_bytes=VMEM_LIMIT)


def _dot(a, b):
    return jnp.dot(a, b, preferred_element_type=F32)


def _dot_nt(a, b):
    return lax.dot_general(a, b, (((1,), (1,)), ((), ())), preferred_element_type=F32)


def _dot_tn(a, b):
    return lax.dot_general(a, b, (((0,), (0,)), ((), ())), preferred_element_type=F32)


def _dot_f32(a, b):
    return jnp.dot(a, b, preferred_element_type=F32, precision=lax.Precision.HIGHEST)


def _sds(shape, dtype):
    return jax.ShapeDtypeStruct(shape, dtype)


def _place():
    return lax.axis_index("x"), lax.axis_index("y"), lax.axis_index("c")


class _Hosted:
    def __init__(self, inputs, out_shape, sems, start, finish):
        self.inputs, self.out_shape, self.sems, self.start, self.finish = inputs, out_shape, sems, start, finish


def _call(body, name, grid, in_specs, out_specs, out_shape, args, scratch_shapes=(), hosted=None):
    in_specs, out_specs, out_shape, scratch_shapes = list(in_specs), list(out_specs), list(out_shape), list(scratch_shapes)
    if hosted is None:
        return pl.pallas_call(body, name=name, grid=grid, in_specs=in_specs, out_specs=out_specs, out_shape=out_shape,
                              scratch_shapes=scratch_shapes, compiler_params=_cp(len(grid)))(*args)
    n_in, n_out, n_scr = len(in_specs), len(out_specs), len(scratch_shapes)
    h_in, h_out = len(hosted.inputs), len(hosted.out_shape)

    def carried(*refs):
        o0 = n_in + h_in
        s0 = o0 + n_out + h_out
        c_in, c_out, c_sem = refs[n_in:o0], refs[o0 + n_out:s0], refs[s0 + n_scr:]
        ids = [pl.program_id(a) for a in range(len(grid))]
        first = functools.reduce(jnp.logical_and, [i == 0 for i in ids])
        last = functools.reduce(jnp.logical_and, [i == g - 1 for i, g in zip(ids, grid)])

        @pl.when(first)
        def _():
            hosted.start(c_in, c_out, c_sem)

        body(*refs[:n_in], *refs[o0:o0 + n_out], *refs[s0:s0 + n_scr])

        @pl.when(last)
        def _():
            hosted.finish(c_in, c_out, c_sem)

    hbm = pl.BlockSpec(memory_space=pl.ANY)
    res = pl.pallas_call(
        carried, name=name, grid=grid, in_specs=in_specs + [hbm] * h_in, out_specs=out_specs + [hbm] * h_out,
        out_shape=out_shape + list(hosted.out_shape), scratch_shapes=scratch_shapes + list(hosted.sems),
        compiler_params=_cp(len(grid)))(*args, *hosted.inputs)
    return res[:n_out], res[n_out:]


def mod_linear(x, shift, scale, w, out_dtype, name, tn=None, emit_u=False):
    t, d = x.shape
    n = w.shape[1]
    tn = n if tn is None else tn
    tm = TOKEN_TILE
    tps = (t // shift.shape[0]) // tm

    def body(x_ref, sh_ref, sc_ref, w_ref, o_ref, *rest):
        u = (x_ref[...] * (1.0 + sc_ref[...]) + sh_ref[...]).astype(BF16)
        o_ref[...] = _dot(u, w_ref[...]).astype(out_dtype)
        if emit_u:
            @pl.when(pl.program_id(1) == 0)
            def _():
                rest[0][...] = u

    vec = pl.BlockSpec((None, 1, d), lambda i, j: (i // tps, 0, 0))
    out_shape = [_sds((t, n), out_dtype)]
    out_specs = [pl.BlockSpec((tm, tn), lambda i, j: (i, j))]
    if emit_u:
        out_shape.append(_sds((t, d), BF16))
        out_specs.append(pl.BlockSpec((tm, d), lambda i, j: (i, 0)))
    res = pl.pallas_call(
        body, name=name, grid=(t // tm, n // tn),
        in_specs=[pl.BlockSpec((tm, d), lambda i, j: (i, 0)), vec, vec,
                  pl.BlockSpec((d, tn), lambda i, j: (0, j))],
        out_specs=out_specs, out_shape=out_shape, compiler_params=_cp(2),
    )(x, shift, scale, w)
    return res if emit_u else res[0]


def _rms(h, g):
    rstd = lax.rsqrt(jnp.mean(h * h, axis=-1, keepdims=True) + EPS)
    return h * rstd, rstd


def mla_mid_fwd(h, g_q, g_kv, w_uq, w_uk, w_uv, cos8, sin8, cos64, sin64s, swap64, rope_to_heads, dup64, name):
    t = h.shape[0]
    tm = TOKEN_TILE
    hq = MLA_HEADS * MLA_NOPE
    hr = MLA_HEADS * MLA_ROPE // 2

    def body(h_ref, gq_ref, gkv_ref, wuq_ref, wuk_ref, wuv_ref, c8_ref, s8_ref, c64_ref, s64_ref, sw_ref, p_ref, d_ref,
             q_ref, kn_ref, v_ref, kr_ref, cq_ref, ckv_ref):
        hh = h_ref[...]
        cq = (_rms(hh[:, :MLA_QR], None)[0] * gq_ref[...]).astype(BF16)
        ckv = (_rms(hh[:, MLA_QR:MLA_QR + MLA_KVR], None)[0] * gkv_ref[...]).astype(BF16)
        cq_ref[...] = cq
        ckv_ref[...] = ckv
        q = _dot(cq, wuq_ref[...])
        x1 = q[:, hq:hq + hr]
        x2 = q[:, hq + hr:]
        cs = c8_ref[...]
        sn = s8_ref[...]
        rot = jnp.concatenate([x1 * cs - x2 * sn, x2 * cs + x1 * sn], axis=1).astype(BF16)
        q_ref[...] = jnp.concatenate([q[:, :hq].astype(BF16), _dot(rot, p_ref[...]).astype(BF16)], axis=1)
        kn_ref[...] = _dot(ckv, wuk_ref[...]).astype(BF16)
        v_ref[...] = _dot(ckv, wuv_ref[...]).astype(BF16)
        kr = hh[:, MLA_QR + MLA_KVR:]
        kr = (kr * c64_ref[...] + _dot_f32(kr, sw_ref[...]) * s64_ref[...]).astype(BF16)
        kr_ref[...] = _dot(kr, d_ref[...]).astype(BF16)

    def rows(n):
        return pl.BlockSpec((tm, n), lambda i: (i, 0))

    def whole(a):
        return pl.BlockSpec(a.shape, lambda i: (0,) * a.ndim)

    nq = w_uq.shape[1]
    return pl.pallas_call(
        body, name=name, grid=(t // tm,),
        in_specs=[rows(h.shape[1]), whole(g_q), whole(g_kv), whole(w_uq), whole(w_uk), whole(w_uv),
                  rows(hr), rows(hr), rows(MLA_ROPE), rows(MLA_ROPE), whole(swap64), whole(rope_to_heads), whole(dup64)],
        out_specs=[rows(nq), rows(hq), rows(hq), rows(2 * MLA_ROPE), rows(MLA_QR), rows(MLA_KVR)],
        out_shape=[_sds((t, nq), BF16), _sds((t, hq), BF16), _sds((t, hq), BF16), _sds((t, 2 * MLA_ROPE), BF16),
                   _sds((t, MLA_QR), BF16), _sds((t, MLA_KVR), BF16)],
        compiler_params=_cp(1),
    )(h, g_q, g_kv, w_uq, w_uk, w_uv, cos8, sin8, cos64, sin64s, swap64, rope_to_heads, dup64)


def _pick_lane(tile, idx):
    lane = lax.broadcasted_iota(jnp.int32, tile.shape, 1)
    return jnp.sum(jnp.where(lane == idx, tile, 0.0), axis=1, keepdims=True)


def _pick_row(tile, idx):
    row = lax.broadcasted_iota(jnp.int32, tile.shape, 0)
    return jnp.sum(jnp.where(row == idx, tile, 0.0), axis=0, keepdims=True)


def _put_lane(tile, idx, col):
    lane = lax.broadcasted_iota(jnp.int32, tile.shape, 1)
    return jnp.where(lane == idx, col, tile)


def _put_row(tile, idx, row):
    r = lax.broadcasted_iota(jnp.int32, tile.shape, 0)
    return tile + jnp.where(r == idx, row, 0.0)


def _causal_softmax_blocks(i, nblk_rows, score_fn, pv_fn, dv):
    tq = nblk_rows

    def block(j, carry, masked):
        m, l, acc = carry
        sc = score_fn(j)
        if masked:
            keep = lax.broadcasted_iota(jnp.int32, sc.shape, 0) >= lax.broadcasted_iota(jnp.int32, sc.shape, 1)
            sc = jnp.where(keep, sc, -1e30)
        m_new = jnp.maximum(m, jnp.max(sc, axis=1, keepdims=True))
        a = jnp.exp(m - m_new)
        p = jnp.exp(sc - m_new)
        l = a * l + jnp.sum(p, axis=1, keepdims=True)
        acc = a * acc + pv_fn(j, p.astype(BF16))
        return m_new, l, acc

    init = (jnp.full((tq, 1), -1e30, F32), jnp.zeros((tq, 1), F32), jnp.zeros((tq, dv), F32))
    carry = lax.fori_loop(0, i, lambda j, c: block(j, c, False), init)
    m, l, acc = block(i, carry, True)
    return acc / l, m + jnp.log(l)


def fox_attn_fwd(qkv, cum, cum_rows, nb, name, hosted=None):
    t = qkv.shape[0]
    s = t // nb
    tq = ATTN_TILE
    nq = s // tq
    npairs = FOX_HEADS // 2
    scale = FOX_HD ** -0.5

    def body(q_ref, k_ref, v_ref, cum_ref, cr_ref, o_ref, lse_ref):
        i = pl.program_id(1)
        hp = pl.program_id(2)

        @pl.when(hp == 0)
        def _():
            lse_ref[...] = jnp.zeros_like(lse_ref)

        q = q_ref[...]
        low = lax.broadcasted_iota(jnp.int32, q.shape, 1) < FOX_HD
        cum_t = cum_ref[...]
        outs = []
        lse_t = lse_ref[...]
        for a in (0, 1):
            head = 2 * hp + a
            qa = jnp.where(low if a == 0 else jnp.logical_not(low), q, jnp.zeros_like(q))
            fq = _pick_lane(cum_t, head)

            def score(j, qa=qa, fq=fq, head=head):
                kb = k_ref[pl.ds(pl.multiple_of(j * tq, tq), tq), :]
                return _dot_nt(qa, kb) * scale + fq - _pick_row(cr_ref[j], head)

            def pv(j, p):
                return _dot(p, v_ref[pl.ds(pl.multiple_of(j * tq, tq), tq), :])

            o_a, lse_a = _causal_softmax_blocks(i, tq, score, pv, 2 * FOX_HD)
            outs.append(o_a)
            lse_t = _put_lane(lse_t, head, lse_a)
        o_ref[...] = jnp.where(low, outs[0], outs[1]).astype(BF16)
        lse_ref[...] = lse_t

    return _call(
        body, name, (nb, nq, npairs),
        [pl.BlockSpec((tq, 128), lambda b, i, hp: (b * nq + i, hp)),
         pl.BlockSpec((s, 128), lambda b, i, hp: (b, npairs + hp)),
         pl.BlockSpec((s, 128), lambda b, i, hp: (b, 2 * npairs + hp)),
         pl.BlockSpec((tq, 128), lambda b, i, hp: (b * nq + i, 0)),
         pl.BlockSpec((nq, 16, tq), lambda b, i, hp: (b, 0, 0))],
        [pl.BlockSpec((tq, 128), lambda b, i, hp: (b * nq + i, hp)),
         pl.BlockSpec((tq, 128), lambda b, i, hp: (b * nq + i, 0))],
        [_sds((t, D_MODEL), BF16), _sds((t, 128), F32)], (qkv, qkv, qkv, cum, cum_rows), hosted=hosted)


def mla_attn_fwd(q, kn, kr2, v, nb, name, hosted=None):
    t = q.shape[0]
    s = t // nb
    tq = ATTN_TILE
    nq = s // tq
    scale = (MLA_NOPE + MLA_ROPE) ** -0.5

    def body(qn_ref, qr_ref, kn_ref, kr_ref, v_ref, o_ref, lse_ref):
        i = pl.program_id(1)
        h = pl.program_id(2)

        @pl.when(h == 0)
        def _():
            lse_ref[...] = jnp.zeros_like(lse_ref)

        qr = qr_ref[...]
        mine = (lax.broadcasted_iota(jnp.int32, qr.shape, 1) // MLA_ROPE) == (h % 2)
        q_cat = jnp.concatenate([qn_ref[...], jnp.where(mine, qr, jnp.zeros_like(qr))], axis=1)

        def score(j):
            rows = pl.ds(pl.multiple_of(j * tq, tq), tq)
            k_cat = jnp.concatenate([kn_ref[rows, :], kr_ref[rows, :]], axis=1)
            return _dot_nt(q_cat, k_cat) * scale

        def pv(j, p):
            return _dot(p, v_ref[pl.ds(pl.multiple_of(j * tq, tq), tq), :])

        o, lse = _causal_softmax_blocks(i, tq, score, pv, MLA_V)
        o_ref[...] = o.astype(BF16)
        lse_ref[...] = _put_lane(lse_ref[...], h, lse)

    nrope0 = MLA_HEADS
    return _call(
        body, name, (nb, nq, MLA_HEADS),
        [pl.BlockSpec((tq, 128), lambda b, i, h: (b * nq + i, h)),
         pl.BlockSpec((tq, 128), lambda b, i, h: (b * nq + i, nrope0 + h // 2)),
         pl.BlockSpec((s, 128), lambda b, i, h: (b, h)),
         pl.BlockSpec((s, 128), lambda b, i, h: (b, 0)),
         pl.BlockSpec((s, 128), lambda b, i, h: (b, h))],
        [pl.BlockSpec((tq, 128), lambda b, i, h: (b * nq + i, h)),
         pl.BlockSpec((tq, 128), lambda b, i, h: (b * nq + i, 0))],
        [_sds((t, MLA_HEADS * MLA_V), BF16), _sds((t, 128), F32)], (q, q, kn, kr2, v), hosted=hosted)


def rows16(a, name):
    t = a.shape[0]
    tq = ATTN_TILE

    def body(a_ref, o_ref):
        o_ref[...] = a_ref[...].T[:16, :]

    return pl.pallas_call(
        body, name=name, grid=(t // tq,), in_specs=[pl.BlockSpec((tq, 128), lambda n: (n, 0))],
        out_specs=pl.BlockSpec((None, 16, tq), lambda n: (n, 0, 0)), out_shape=_sds((t // tq, 16, tq), F32),
        compiler_params=_cp(1),
    )(a)


def tokens128(rows, onehot, name):
    nblk, _, tq = rows.shape

    def body(r_ref, e_ref, o_ref):
        o_ref[...] = lax.dot_general(r_ref[...], e_ref[...], (((0,), (0,)), ((), ())), preferred_element_type=F32,
                                     precision=lax.Precision.HIGHEST)

    return pl.pallas_call(
        body, name=name, grid=(nblk,),
        in_specs=[pl.BlockSpec((None, 16, tq), lambda n: (n, 0, 0)), pl.BlockSpec((16, 128), lambda n: (0, 0))],
        out_specs=pl.BlockSpec((tq, 128), lambda n: (n, 0)), out_shape=_sds((nblk * tq, 128), F32),
        compiler_params=_cp(1),
    )(rows, onehot)


def _layer_norm(z, g, b):
    mu = jnp.mean(z, axis=-1, keepdims=True)
    zc = z - mu
    rstd = lax.rsqrt(jnp.mean(zc * zc, axis=-1, keepdims=True) + EPS)
    xhat = zc * rstd
    return xhat * g + b, xhat, rstd


def linear_resid_ln(a, w, x_in, gate, ln_g, ln_b, name):
    t, kdim = a.shape
    d = w.shape[1]
    tm = TOKEN_TILE
    tps = (t // gate.shape[0]) // tm

    def body(a_ref, w_ref, x_ref, gt_ref, g_ref, b_ref, y_ref, xo_ref):
        y = _dot(a_ref[...], w_ref[...])
        y_ref[...] = y
        z = ALPHA * x_ref[...] + (1.0 + gt_ref[...]) * y
        xo_ref[...] = _layer_norm(z, g_ref[...], b_ref[...])[0]

    rows = pl.BlockSpec((tm, d), lambda i: (i, 0))
    vec = pl.BlockSpec((1, d), lambda i: (0, 0))
    return pl.pallas_call(
        body, name=name, grid=(t // tm,),
        in_specs=[pl.BlockSpec((tm, kdim), lambda i: (i, 0)), pl.BlockSpec((kdim, d), lambda i: (0, 0)), rows,
                  pl.BlockSpec((None, 1, d), lambda i: (i // tps, 0, 0)), vec, vec],
        out_specs=[rows, rows], out_shape=[_sds((t, d), F32), _sds((t, d), F32)],
        compiler_params=_cp(1),
    )(a, w, x_in, gate, ln_g, ln_b)


def ffn_fwd(x_in, shift, scale, gate, wg, wu, wd, ln_g, ln_b, name, hosted=None):
    t, d = x_in.shape
    c, _, fc = wg.shape
    tm = TOKEN_TILE
    tps = (t // gate.shape[0]) // tm

    def body(x_ref, sh_ref, sc_ref, gt_ref, wg_ref, wu_ref, wd_ref, g_ref, b_ref,
             u_ref, hg_ref, hu_ref, y_ref, xo_ref, acc_ref):
        cc = pl.program_id(1)

        @pl.when(cc == 0)
        def _():
            u_ref[...] = (x_ref[...] * (1.0 + sc_ref[...]) + sh_ref[...]).astype(BF16)
            acc_ref[...] = jnp.zeros_like(acc_ref)

        u = u_ref[...]
        hg = _dot(u, wg_ref[...])
        hu = _dot(u, wu_ref[...])
        hg_ref[...] = hg.astype(BF16)
        hu_ref[...] = hu.astype(BF16)
        act = (hg * jax.nn.sigmoid(hg) * hu).astype(BF16)
        acc_ref[...] += _dot(act, wd_ref[...])

        @pl.when(cc == c - 1)
        def _():
            y = acc_ref[...]
            y_ref[...] = y
            z = ALPHA * x_ref[...] + (1.0 + gt_ref[...]) * y
            xo_ref[...] = _layer_norm(z, g_ref[...], b_ref[...])[0]

    rows = pl.BlockSpec((tm, d), lambda i, cc: (i, 0))
    bvec = pl.BlockSpec((None, 1, d), lambda i, cc: (i // tps, 0, 0))
    vec = pl.BlockSpec((1, d), lambda i, cc: (0, 0))
    hspec = pl.BlockSpec((None, tm, fc), lambda i, cc: (cc, i, 0))
    wcol = pl.BlockSpec((None, d, fc), lambda i, cc: (cc, 0, 0))
    return _call(
        body, name, (t // tm, c),
        [rows, bvec, bvec, bvec, wcol, wcol, pl.BlockSpec((None, fc, d), lambda i, cc: (cc, 0, 0)), vec, vec],
        [rows, hspec, hspec, rows, rows],
        [_sds((t, d), BF16), _sds((c, t, fc), BF16), _sds((c, t, fc), BF16), _sds((t, d), F32), _sds((t, d), F32)],
        (x_in, shift, scale, gate, wg, wu, wd, ln_g, ln_b), scratch_shapes=[pltpu.VMEM((tm, d), F32)], hosted=hosted)


def fox_gate_fwd(hf, b_f, tri, n_batch, name):
    t, n = hf.shape
    blk = tri.shape[0]
    nb = (t // n_batch) // blk

    def body(hf_ref, b_ref, tri_ref, o_ref, carry_ref):
        @pl.when(pl.program_id(1) == 0)
        def _():
            carry_ref[...] = jnp.zeros_like(carry_ref)

        xx = hf_ref[...] + b_ref[...]
        lf = jnp.minimum(xx, 0.0) - jnp.log(1.0 + jnp.exp(-jnp.abs(xx)))
        cum = _dot_f32(tri_ref[...], lf) + carry_ref[...]
        o_ref[...] = cum
        carry_ref[...] = cum[blk - 1:blk, :]

    return pl.pallas_call(
        body, name=name, grid=(n_batch, nb),
        in_specs=[pl.BlockSpec((blk, n), lambda bb, i: (bb * nb + i, 0)), pl.BlockSpec((1, n), lambda bb, i: (0, 0)),
                  pl.BlockSpec((blk, blk), lambda bb, i: (0, 0))],
        out_specs=pl.BlockSpec((blk, n), lambda bb, i: (bb * nb + i, 0)),
        out_shape=_sds((t, n), F32), scratch_shapes=[pltpu.VMEM((1, n), F32)],
        compiler_params=_cp(2),
    )(hf, b_f, tri)


def loss_grad(x_out, target, name):
    t, d = x_out.shape
    tm = TOKEN_TILE

    def body(x_ref, t_ref, g_ref, l_ref):
        @pl.when(pl.program_id(0) == 0)
        def _():
            l_ref[...] = jnp.zeros_like(l_ref)

        err = x_ref[...] - t_ref[...]
        g_ref[...] = err / d
        l_ref[...] += jnp.sum(err * err, axis=0, keepdims=True)

    rows = pl.BlockSpec((tm, d), lambda i: (i, 0))
    return pl.pallas_call(
        body, name=name, grid=(t // tm,), in_specs=[rows, rows],
        out_specs=[rows, pl.BlockSpec((1, d), lambda i: (0, 0))],
        out_shape=[_sds((t, d), F32), _sds((1, d), F32)], compiler_params=_cp(1),
    )(x_out, target)


def ln_bwd(dxo, x_in, y, gate, ln_g, name):
    t, d = dxo.shape
    nb = gate.shape[0]
    tm = TOKEN_TILE
    tps = (t // nb) // tm

    def body(dxo_ref, x_ref, y_ref, gt_ref, g_ref, dz_ref, dy_ref, dg_ref, db_ref, dgt_ref):
        i = pl.program_id(0)

        @pl.when(i == 0)
        def _():
            dg_ref[...] = jnp.zeros_like(dg_ref)
            db_ref[...] = jnp.zeros_like(db_ref)

        @pl.when(i % tps == 0)
        def _():
            dgt_ref[...] = jnp.zeros_like(dgt_ref)

        yy = y_ref[...]
        g1 = 1.0 + gt_ref[...]
        z = ALPHA * x_ref[...] + g1 * yy
        _, xhat, rstd = _layer_norm(z, 1.0, 0.0)
        dxo_v = dxo_ref[...]
        dg_ref[...] += jnp.sum(dxo_v * xhat, axis=0, keepdims=True)
        db_ref[...] += jnp.sum(dxo_v, axis=0, keepdims=True)
        dxh = dxo_v * g_ref[...]
        dz = rstd * (dxh - jnp.mean(dxh, axis=-1, keepdims=True) - xhat * jnp.mean(dxh * xhat, axis=-1, keepdims=True))
        dz_ref[...] = dz
        dy_ref[...] = (g1 * dz).astype(BF16)
        dgt_ref[...] += jnp.sum(dz * yy, axis=0, keepdims=True)

    rows = pl.BlockSpec((tm, d), lambda i: (i, 0))
    vec = pl.BlockSpec((1, d), lambda i: (0, 0))
    bvec = pl.BlockSpec((None, 1, d), lambda i: (i // tps, 0, 0))
    return pl.pallas_call(
        body, name=name, grid=(t // tm,), in_specs=[rows, rows, rows, bvec, vec],
        out_specs=[rows, rows, vec, vec, bvec],
        out_shape=[_sds((t, d), F32), _sds((t, d), BF16), _sds((1, d), F32), _sds((1, d), F32), _sds((nb, 1, d), F32)],
        compiler_params=_cp(1),
    )(dxo, x_in, y, gate, ln_g)


def _mod_bwd_tail(du, dz_ref, x_ref, sc_ref, dx_ref, dsc_ref, dsh_ref, first):
    @pl.when(first)
    def _():
        dsc_ref[...] = jnp.zeros_like(dsc_ref)
        dsh_ref[...] = jnp.zeros_like(dsh_ref)

    dx_ref[...] = ALPHA * dz_ref[...] + du * (1.0 + sc_ref[...])
    dsc_ref[...] += jnp.sum(du * x_ref[...], axis=0, keepdims=True)
    dsh_ref[...] += jnp.sum(du, axis=0, keepdims=True)


def ffn_bwd(dy, hg, hu, wg, wu, wd, dz, x_in, scale, name, hosted=None):
    t, d = dy.shape
    c, _, fc = wg.shape
    nb = scale.shape[0]
    tm = TOKEN_TILE
    tps = (t // nb) // tm

    def body(dy_ref, hg_ref, hu_ref, wg_ref, wu_ref, wd_ref, dz_ref, x_ref, sc_ref,
             dhg_ref, dhu_ref, act_ref, dx_ref, dsc_ref, dsh_ref, acc_ref):
        i = pl.program_id(0)
        cc = pl.program_id(1)

        @pl.when(cc == 0)
        def _():
            acc_ref[...] = jnp.zeros_like(acc_ref)

        hgv = hg_ref[...].astype(F32)
        huv = hu_ref[...].astype(F32)
        da = _dot_nt(dy_ref[...], wd_ref[...])
        sg = jax.nn.sigmoid(hgv)
        sl = hgv * sg
        act_ref[...] = (sl * huv).astype(BF16)
        dhu = (da * sl).astype(BF16)
        dhg = (da * huv * (sg * (1.0 + hgv * (1.0 - sg)))).astype(BF16)
        dhu_ref[...] = dhu
        dhg_ref[...] = dhg
        acc_ref[...] += _dot_nt(dhg, wg_ref[...]) + _dot_nt(dhu, wu_ref[...])

        @pl.when(cc == c - 1)
        def _():
            _mod_bwd_tail(acc_ref[...], dz_ref, x_ref, sc_ref, dx_ref, dsc_ref, dsh_ref, i % tps == 0)

    rows = pl.BlockSpec((tm, d), lambda i, cc: (i, 0))
    bvec = pl.BlockSpec((None, 1, d), lambda i, cc: (i // tps, 0, 0))
    hspec = pl.BlockSpec((None, tm, fc), lambda i, cc: (cc, i, 0))
    wcol = pl.BlockSpec((None, d, fc), lambda i, cc: (cc, 0, 0))
    return _call(
        body, name, (t // tm, c),
        [rows, hspec, hspec, wcol, wcol, pl.BlockSpec((None, fc, d), lambda i, cc: (cc, 0, 0)), rows, rows, bvec],
        [hspec, hspec, hspec, rows, bvec, bvec],
        [_sds((c, t, fc), BF16), _sds((c, t, fc), BF16), _sds((c, t, fc), BF16), _sds((t, d), F32),
         _sds((nb, 1, d), F32), _sds((nb, 1, d), F32)],
        (dy, hg, hu, wg, wu, wd, dz, x_in, scale), scratch_shapes=[pltpu.VMEM((tm, d), F32)], hosted=hosted)


def linear_nt_mod_bwd(pairs, dz, x_in, scale, name):
    t, d = dz.shape
    nb = scale.shape[0]
    tm = TOKEN_TILE
    tps = (t // nb) // tm
    npairs = len(pairs)

    def body(*refs):
        dh_refs = refs[:npairs]
        w_refs = refs[npairs:2 * npairs]
        dz_ref, x_ref, sc_ref, dx_ref, dsc_ref, dsh_ref = refs[2 * npairs:]
        du = _dot_nt(dh_refs[0][...], w_refs[0][...])
        for kk in range(1, npairs):
            du = du + _dot_nt(dh_refs[kk][...], w_refs[kk][...])
        _mod_bwd_tail(du, dz_ref, x_ref, sc_ref, dx_ref, dsc_ref, dsh_ref, pl.program_id(0) % tps == 0)

    rows = pl.BlockSpec((tm, d), lambda i: (i, 0))
    bvec = pl.BlockSpec((None, 1, d), lambda i: (i // tps, 0, 0))
    in_specs = [pl.BlockSpec((tm, dh.shape[1]), lambda i: (i, 0)) for dh, _ in pairs]
    in_specs += [pl.BlockSpec(w.shape, lambda i: (0, 0)) for _, w in pairs]
    in_specs += [rows, rows, bvec]
    return pl.pallas_call(
        body, name=name, grid=(t // tm,), in_specs=in_specs,
        out_specs=[rows, bvec, bvec],
        out_shape=[_sds((t, d), F32), _sds((nb, 1, d), F32), _sds((nb, 1, d), F32)],
        compiler_params=_cp(1),
    )(*[dh for dh, _ in pairs], *[w for _, w in pairs], dz, x_in, scale)


def linear_nt_delta(dy, w_o, o, head_sel, name):
    t, d = dy.shape
    hdv = w_o.shape[0]
    tm = TOKEN_TILE

    def body(dy_ref, w_ref, o_ref, sel_ref, do_ref, dl_ref):
        do = _dot_nt(dy_ref[...], w_ref[...])
        do_ref[...] = do.astype(BF16)
        dl_ref[...] = _dot_f32(do * o_ref[...].astype(F32), sel_ref[...])

    return pl.pallas_call(
        body, name=name, grid=(t // tm,),
        in_specs=[pl.BlockSpec((tm, d), lambda i: (i, 0)), pl.BlockSpec((hdv, d), lambda i: (0, 0)),
                  pl.BlockSpec((tm, hdv), lambda i: (i, 0)), pl.BlockSpec(head_sel.shape, lambda i: (0, 0))],
        out_specs=[pl.BlockSpec((tm, hdv), lambda i: (i, 0)), pl.BlockSpec((tm, 128), lambda i: (i, 0))],
        out_shape=[_sds((t, hdv), BF16), _sds((t, 128), F32)], compiler_params=_cp(1),
    )(dy, w_o, o, head_sel)


def _attn_bwd_blocks(j, nk, tk, q_fn, do_fn, k_cat, vb, row_fn, scale, bias_fn, sinks):
    add_dq, add_dfq = sinks
    dk_dim = k_cat.shape[1]
    dv_dim = vb.shape[1]

    def block(i, carry, masked):
        dk_acc, dv_acc, dfk_acc = carry
        qb = q_fn(i)
        dob = do_fn(i)
        lse_row, dl_row = row_fn(i)
        st = _dot_nt(k_cat, qb) * scale
        if bias_fn is not None:
            fq_row, fk_col = bias_fn(i)
            st = st + fq_row - fk_col
        if masked:
            keep = lax.broadcasted_iota(jnp.int32, st.shape, 1) >= lax.broadcasted_iota(jnp.int32, st.shape, 0)
            st = jnp.where(keep, st, -1e30)
        pt = jnp.exp(st - lse_row)
        dv_acc = dv_acc + _dot(pt.astype(BF16), dob)
        dst = pt * (_dot_nt(vb, dob) - dl_row)
        if add_dfq is not None:
            dfk_acc = dfk_acc - jnp.sum(dst, axis=1, keepdims=True)
            add_dfq(i, jnp.sum(dst, axis=0, keepdims=True))
        dsb = (dst * scale).astype(BF16)
        dk_acc = dk_acc + _dot(dsb, qb)
        add_dq(i, _dot_tn(dsb, k_cat))
        return dk_acc, dv_acc, dfk_acc

    init = (jnp.zeros((tk, dk_dim), F32), jnp.zeros((tk, dv_dim), F32), jnp.zeros((tk, 1), F32))
    carry = block(j, init, True)
    return lax.fori_loop(j + 1, nk, lambda i, c: block(i, c, False), carry)


def fox_attn_bwd(qkv, do, cum, cum_rows, lse_rows, delta_rows, nb, name, hosted=None):
    t = qkv.shape[0]
    s = t // nb
    tk = ATTN_TILE
    nk = s // tk
    npairs = FOX_HEADS // 2
    scale = FOX_HD ** -0.5

    def body(q_ref, k_ref, v_ref, do_ref, cum_ref, cr_ref, lr_ref, dr_ref, dq_ref, dk_ref, dv_ref, dfq_ref, dfk_ref):
        hp = pl.program_id(1)
        j = pl.program_id(2)

        @pl.when(j == 0)
        def _():
            dq_ref[...] = jnp.zeros_like(dq_ref)

        @pl.when((j == 0) & (hp == 0))
        def _():
            dfq_ref[...] = jnp.zeros_like(dfq_ref)
            dfk_ref[...] = jnp.zeros_like(dfk_ref)

        kb = k_ref[...]
        vb = v_ref[...]
        low = lax.broadcasted_iota(jnp.int32, kb.shape, 1) < FOX_HD
        cum_t = cum_ref[...]
        dks, dvs = [], []
        for a in (0, 1):
            head = 2 * hp + a
            half = low if a == 0 else jnp.logical_not(low)
            ka = jnp.where(half, kb, jnp.zeros_like(kb))
            va = jnp.where(half, vb, jnp.zeros_like(vb))
            fk = _pick_lane(cum_t, head)

            def rows_of(i):
                return pl.ds(pl.multiple_of(i * tk, tk), tk)

            def add_dq(i, val):
                dq_ref[rows_of(i), :] += val

            def add_dfq(i, val, head=head):
                dfq_ref[i] = _put_row(dfq_ref[i], head, val)

            dk_a, dv_a, dfk_a = _attn_bwd_blocks(
                j, nk, tk, lambda i: q_ref[rows_of(i), :], lambda i: do_ref[rows_of(i), :], ka, va,
                lambda i, head=head: (_pick_row(lr_ref[i], head), _pick_row(dr_ref[i], head)), scale,
                lambda i, head=head, fk=fk: (_pick_row(cr_ref[i], head), fk), (add_dq, add_dfq))
            dks.append(dk_a)
            dvs.append(dv_a)
            dfk_row = jnp.broadcast_to(dfk_a, (tk, 128)).T[0:1, :]
            dfk_ref[j] = _put_row(dfk_ref[j], head, dfk_row)
        dk_ref[...] = jnp.where(low, dks[0], dks[1]).astype(BF16)
        dv_ref[...] = jnp.where(low, dvs[0], dvs[1]).astype(BF16)

    rowsp = pl.BlockSpec((nk, 16, tk), lambda b, hp, j: (b, 0, 0))
    return _call(
        body, name, (nb, npairs, nk),
        [pl.BlockSpec((s, 128), lambda b, hp, j: (b, hp)),
         pl.BlockSpec((tk, 128), lambda b, hp, j: (b * nk + j, npairs + hp)),
         pl.BlockSpec((tk, 128), lambda b, hp, j: (b * nk + j, 2 * npairs + hp)),
         pl.BlockSpec((s, 128), lambda b, hp, j: (b, hp)),
         pl.BlockSpec((tk, 128), lambda b, hp, j: (b * nk + j, 0)),
         rowsp, rowsp, rowsp],
        [pl.BlockSpec((s, 128), lambda b, hp, j: (b, hp)),
         pl.BlockSpec((tk, 128), lambda b, hp, j: (b * nk + j, hp)),
         pl.BlockSpec((tk, 128), lambda b, hp, j: (b * nk + j, hp)),
         rowsp, rowsp],
        [_sds((t, D_MODEL), F32), _sds((t, D_MODEL), BF16), _sds((t, D_MODEL), BF16),
         _sds((t // tk, 16, tk), F32), _sds((t // tk, 16, tk), F32)],
        (qkv, qkv, qkv, do, cum, cum_rows, lse_rows, delta_rows), hosted=hosted)


def mla_attn_bwd(q, kn, kr2, v, do, lse_rows, delta_rows, nb, name, hosted=None):
    t = q.shape[0]
    s = t // nb
    tk = ATTN_TILE
    nk = s // tk
    scale = (MLA_NOPE + MLA_ROPE) ** -0.5

    def body(qn_ref, qr_ref, kn_ref, kr_ref, v_ref, do_ref, lr_ref, dr_ref, dqn_ref, dqr_ref, dkn_ref, dkr_ref, dv_ref):
        h = pl.program_id(1)
        j = pl.program_id(2)

        @pl.when(j == 0)
        def _():
            dqn_ref[...] = jnp.zeros_like(dqn_ref)
            dqr_ref[...] = jnp.zeros_like(dqr_ref)

        k_cat = jnp.concatenate([kn_ref[...], kr_ref[...]], axis=1)
        mine = (lax.broadcasted_iota(jnp.int32, (tk, 128), 1) // MLA_ROPE) == (h % 2)

        def rows_of(i):
            return pl.ds(pl.multiple_of(i * tk, tk), tk)

        def q_fn(i):
            qr = qr_ref[rows_of(i), :]
            return jnp.concatenate([qn_ref[rows_of(i), :], jnp.where(mine, qr, jnp.zeros_like(qr))], axis=1)

        def add_dq(i, val):
            dqn_ref[rows_of(i), :] += val[:, :MLA_NOPE]
            dqr_ref[rows_of(i), :] += val[:, MLA_NOPE:]

        dk_acc, dv_acc, _ = _attn_bwd_blocks(
            j, nk, tk, q_fn, lambda i: do_ref[rows_of(i), :], k_cat, v_ref[...],
            lambda i: (_pick_row(lr_ref[i], h), _pick_row(dr_ref[i], h)), scale, None, (add_dq, None))
        dkn_ref[...] = dk_acc[:, :MLA_NOPE].astype(BF16)
        dkr_ref[...] = dk_acc[:, MLA_NOPE:].astype(BF16)
        dv_ref[...] = dv_acc.astype(BF16)

    full = pl.BlockSpec((s, 128), lambda b, h, j: (b, h))
    blk = pl.BlockSpec((tk, 128), lambda b, h, j: (b * nk + j, h))
    rowsp = pl.BlockSpec((nk, 16, tk), lambda b, h, j: (b, 0, 0))
    wide = MLA_HEADS * MLA_V
    return _call(
        body, name, (nb, MLA_HEADS, nk),
        [full, pl.BlockSpec((s, 128), lambda b, h, j: (b, MLA_HEADS + h // 2)), blk,
         pl.BlockSpec((tk, 128), lambda b, h, j: (b * nk + j, 0)), blk, full, rowsp, rowsp],
        [full, full, blk, blk, blk],
        [_sds((t, wide), F32), _sds((t, wide), F32), _sds((t, wide), BF16), _sds((t, wide), BF16),
         _sds((t, wide), BF16)],
        (q, q, kn, kr2, v, do, lse_rows, delta_rows), hosted=hosted)


def mla_mid_bwd(dqn, dqr, dkn, dv, dkr_heads, h, g_q, g_kv, w_uq, w_uk, w_uv, cos8, sin8, cos64, sin64s, swap64,
                heads_to_rope, head_sum, name):
    t = h.shape[0]
    tm = TOKEN_TILE
    hq = MLA_HEADS * MLA_NOPE
    hr = MLA_HEADS * MLA_ROPE // 2
    nq = w_uq.shape[1]

    def body(dqn_ref, dqr_ref, dkn_ref, dv_ref, dkr_ref, h_ref, gq_ref, gkv_ref, wuq_ref, wuk_ref, wuv_ref,
             c8_ref, s8_ref, c64_ref, s64_ref, sw_ref, hp_ref, hs_ref, dh_ref, dqp_ref, dgq_ref, dgkv_ref):
        @pl.when(pl.program_id(0) == 0)
        def _():
            dgq_ref[...] = jnp.zeros_like(dgq_ref)
            dgkv_ref[...] = jnp.zeros_like(dgkv_ref)

        drot = _dot(dqr_ref[...].astype(BF16), hp_ref[...])
        o1 = drot[:, :hr]
        o2 = drot[:, hr:]
        cs = c8_ref[...]
        sn = s8_ref[...]
        dqp = jnp.concatenate([dqn_ref[...].astype(BF16), (o1 * cs + o2 * sn).astype(BF16),
                               (o2 * cs - o1 * sn).astype(BF16)], axis=1)
        dqp_ref[...] = dqp
        dcq = _dot_nt(dqp, wuq_ref[...])
        dckv = _dot_nt(dkn_ref[...], wuk_ref[...]) + _dot_nt(dv_ref[...], wuv_ref[...])
        hh = h_ref[...]

        def rms_bwd(hpart, g, dc, dg_ref):
            hhat, rstd = _rms(hpart, None)
            dg_ref[...] += jnp.sum(dc * hhat, axis=0, keepdims=True)
            dcg = dc * g
            return rstd * (dcg - hhat * jnp.mean(dcg * hhat, axis=-1, keepdims=True))

        dhq = rms_bwd(hh[:, :MLA_QR], gq_ref[...], dcq, dgq_ref)
        dhkv = rms_bwd(hh[:, MLA_QR:MLA_QR + MLA_KVR], gkv_ref[...], dckv, dgkv_ref)
        dkr = _dot(dkr_ref[...], hs_ref[...])
        dkr_pre = dkr * c64_ref[...] + _dot_f32(dkr * s64_ref[...], sw_ref[...])
        dh_ref[...] = jnp.concatenate([dhq, dhkv, dkr_pre], axis=1).astype(BF16)

    def rows(n):
        return pl.BlockSpec((tm, n), lambda i: (i, 0))

    def whole(a):
        return pl.BlockSpec(a.shape, lambda i: (0,) * a.ndim)

    return pl.pallas_call(
        body, name=name, grid=(t // tm,),
        in_specs=[rows(hq), rows(hq), rows(hq), rows(hq), rows(hq), rows(h.shape[1]), whole(g_q), whole(g_kv),
                  whole(w_uq), whole(w_uk), whole(w_uv), rows(hr), rows(hr), rows(MLA_ROPE), rows(MLA_ROPE),
                  whole(swap64), whole(heads_to_rope), whole(head_sum)],
        out_specs=[rows(h.shape[1]), rows(nq), pl.BlockSpec((1, MLA_QR), lambda i: (0, 0)),
                   pl.BlockSpec((1, MLA_KVR), lambda i: (0, 0))],
        out_shape=[_sds((t, h.shape[1]), BF16), _sds((t, nq), BF16), _sds((1, MLA_QR), F32), _sds((1, MLA_KVR), F32)],
        compiler_params=_cp(1),
    )(dqn, dqr, dkn, dv, dkr_heads, h, g_q, g_kv, w_uq, w_uk, w_uv, cos8, sin8, cos64, sin64s, swap64,
      heads_to_rope, head_sum)


def fox_gate_bwd(dcum, hf, b_f, triu, n_batch, name):
    t, n = hf.shape
    blk = triu.shape[0]
    nb = (t // n_batch) // blk

    def body(dc_ref, hf_ref, b_ref, tri_ref, o_ref, db_ref, carry_ref):
        @pl.when(pl.program_id(1) == 0)
        def _():
            carry_ref[...] = jnp.zeros_like(carry_ref)

        @pl.when((pl.program_id(0) == 0) & (pl.program_id(1) == 0))
        def _():
            db_ref[...] = jnp.zeros_like(db_ref)

        rc = _dot_f32(tri_ref[...], dc_ref[...]) + carry_ref[...]
        carry_ref[...] = rc[0:1, :]
        dhf = rc * jax.nn.sigmoid(-(hf_ref[...] + b_ref[...]))
        o_ref[...] = dhf.astype(BF16)
        db_ref[...] += jnp.sum(dhf, axis=0, keepdims=True)

    rev = pl.BlockSpec((blk, n), lambda bb, i: (bb * nb + nb - 1 - i, 0))
    return pl.pallas_call(
        body, name=name, grid=(n_batch, nb),
        in_specs=[rev, rev, pl.BlockSpec((1, n), lambda bb, i: (0, 0)), pl.BlockSpec((blk, blk), lambda bb, i: (0, 0))],
        out_specs=[rev, pl.BlockSpec((1, n), lambda bb, i: (0, 0))],
        out_shape=[_sds((t, n), BF16), _sds((1, n), F32)], scratch_shapes=[pltpu.VMEM((1, n), F32)],
        compiler_params=_cp(2),
    )(dcum, hf, b_f, triu)


def wgrad(a, bm, name, with_bf16=False, bt=512):
    ca, t, kd = a.shape
    cb, _, nd = bm.shape
    c = max(ca, cb)
    bn = nd
    if nd > 1024 and nd % 1024 == 0:
        bn = 1024
    nsteps = t // bt

    def body(a_ref, b_ref, o_ref, *rest):
        @pl.when(pl.program_id(2) == 0)
        def _():
            o_ref[...] = jnp.zeros_like(o_ref)

        o_ref[...] += _dot_tn(a_ref[...].astype(BF16), b_ref[...].astype(BF16))
        if with_bf16:
            @pl.when(pl.program_id(2) == nsteps - 1)
            def _():
                rest[0][...] = o_ref[...].astype(BF16)

    out_spec = pl.BlockSpec((None, kd, bn), lambda cc, n, tt: (cc, 0, n))
    res = pl.pallas_call(
        body, name=name, grid=(c, nd // bn, nsteps),
        in_specs=[pl.BlockSpec((None, bt, kd), lambda cc, n, tt: (cc if ca > 1 else 0, tt, 0)),
                  pl.BlockSpec((None, bt, bn), lambda cc, n, tt: (cc if cb > 1 else 0, tt, n))],
        out_specs=[out_spec, out_spec] if with_bf16 else out_spec,
        out_shape=[_sds((c, kd, nd), F32), _sds((c, kd, nd), BF16)] if with_bf16 else _sds((c, kd, nd), F32),
        compiler_params=_cp(3),
    )(a, bm)
    return res


def ada_mod_part(c_all, ada_w, name):
    nl, d, n = ada_w.shape
    rows = c_all.shape[0]
    tn = 512

    def body(c_ref, w_ref, o_ref):
        cv = c_ref[...]
        act = (cv * jax.nn.sigmoid(cv)).astype(BF16)
        o_ref[...] = _dot(act, w_ref[...].astype(BF16))

    return pl.pallas_call(
        body, name=name, grid=(nl, n // tn),
        in_specs=[pl.BlockSpec((rows, d), lambda l, j: (0, 0)), pl.BlockSpec((None, d, tn), lambda l, j: (l, 0, j))],
        out_specs=pl.BlockSpec((None, rows, tn), lambda l, j: (l, 0, j)),
        out_shape=_sds((nl, rows, n), F32), compiler_params=_cp(2),
    )(c_all, ada_w)


def ada_grad(c_all_t, dmod, name):
    nl, rows, n = dmod.shape
    d = c_all_t.shape[0]
    tn = 512

    def body(c_ref, dm_ref, o_ref):
        cv = c_ref[...]
        act = (cv * jax.nn.sigmoid(cv)).astype(BF16)
        o_ref[...] = _dot(act, dm_ref[...].astype(BF16))

    return pl.pallas_call(
        body, name=name, grid=(nl, n // tn),
        in_specs=[pl.BlockSpec((d, rows), lambda l, j: (0, 0)), pl.BlockSpec((None, rows, tn), lambda l, j: (l, 0, j))],
        out_specs=pl.BlockSpec((None, d, tn), lambda l, j: (l, 0, j)),
        out_shape=_sds((nl, d, n), F32), compiler_params=_cp(2),
    )(c_all_t, dmod)


def sum_leading(a, name):
    g, r, n = a.shape

    def body(a_ref, o_ref):
        acc = a_ref[0]
        for kk in range(1, g):
            acc = acc + a_ref[kk]
        o_ref[...] = acc

    return pl.pallas_call(
        body, name=name, grid=(1,), in_specs=[pl.BlockSpec((g, r, n), lambda i: (0, 0, 0))],
        out_specs=pl.BlockSpec((r, n), lambda i: (0, 0)), out_shape=_sds((r, n), F32), compiler_params=_cp(1),
    )(a)


def adamw(w, g, m, v, name):
    r, n = w.shape
    br = r
    for cand in (512, 256, 128, 64, 32, 16, 8):
        if r % cand == 0 and r > cand and cand * n * 4 <= ADAMW_BLOCK_BYTES:
            br = cand
            break
    c1 = 1.0 - ADAM_B1 ** ADAM_STEP
    c2 = 1.0 - ADAM_B2 ** ADAM_STEP

    def body(w_ref, g_ref, m_ref, v_ref, d_ref, mo_ref, vo_ref):
        gv = g_ref[...]
        mn = ADAM_B1 * m_ref[...] + (1.0 - ADAM_B1) * gv
        vn = ADAM_B2 * v_ref[...] + (1.0 - ADAM_B2) * (gv * gv)
        mo_ref[...] = mn
        vo_ref[...] = vn
        d_ref[...] = -ADAM_LR * ((mn / c1) / (jnp.sqrt(vn / c2) + ADAM_EPS) + ADAM_WD * w_ref[...])

    spec = pl.BlockSpec((br, n), lambda i: (i, 0))
    return pl.pallas_call(
        body, name=name, grid=(r // br,), in_specs=[spec] * 4, out_specs=[spec] * 3,
        out_shape=[_sds((r, n), F32)] * 3, compiler_params=_cp(1),
    )(w, g, m, v)


def all_gather8(x_blk, name):
    m_per, n = x_blk.shape

    def body(x_ref, out_ref, send_sems, recv_sems, local_sem):
        x, y, c = _place()
        me, sibling = (x, y, c), (x, y, 1 - c)
        chips = [(1 - x, y), (x, 1 - y), (1 - x, 1 - y)]

        def rows(px, py, pc):
            return out_ref.at[pl.ds((4 * px + 2 * py + pc) * m_per, m_per), :]

        def copy(k, block, to, src=None):
            return pltpu.make_async_remote_copy(
                src_ref=rows(*block) if src is None else src, dst_ref=rows(*block),
                send_sem=send_sems.at[k], recv_sem=recv_sems.at[k], device_id=to, device_id_type=MESH)

        mine = pltpu.make_async_copy(x_ref, rows(*me), local_sem)
        mine.start()
        first = [copy(0, me, sibling, src=x_ref)]
        first += [copy(1 + j, me, (*chip, c), src=x_ref) for j, chip in enumerate(chips)]
        for cp in first:
            cp.start()
        passed = [copy(4 + j, (*chip, c), sibling) for j, chip in enumerate(chips)]
        for j, chip in enumerate(chips):
            copy(1 + j, (*chip, c), me).wait_recv()
            passed[j].start()
        copy(0, sibling, me).wait_recv()
        for j, chip in enumerate(chips):
            copy(4 + j, (*chip, 1 - c), me).wait_recv()
        for cp in first + passed:
            cp.wait_send()
        mine.wait()

    return pl.pallas_call(
        body, name=name, out_shape=_sds((8 * m_per, n), x_blk.dtype),
        in_specs=[pl.BlockSpec(memory_space=pltpu.VMEM)], out_specs=pl.BlockSpec(memory_space=pltpu.VMEM),
        scratch_shapes=[pltpu.SemaphoreType.DMA((7,)), pltpu.SemaphoreType.DMA((7,)), pltpu.SemaphoreType.DMA],
        compiler_params=pltpu.CompilerParams(vmem_limit_bytes=VMEM_LIMIT),
    )(x_blk)


def _gather_comm(shards):
    nt = len(shards)

    def parts(w_refs, out_refs, sems, finishing):
        send_sems, recv_sems, own_send, own_recv = sems
        x, y, c = _place()
        sibling = (x, y, 1 - c)
        chips = [(1 - x, y), (x, 1 - y), (1 - x, 1 - y)]

        def copy(t, k, block, to, src=None):
            px, py, hh = block
            dst = out_refs[t].at[2 * px + py, hh]
            return pltpu.make_async_remote_copy(
                src_ref=dst if src is None else src, dst_ref=dst,
                send_sem=send_sems.at[6 * t + k], recv_sem=recv_sems.at[6 * t + k], device_id=to, device_id_type=MESH)

        own = [pltpu.make_async_remote_copy(
            src_ref=w_refs[t], dst_ref=out_refs[t].at[2 * x + y], send_sem=own_send.at[t], recv_sem=own_recv.at[t],
            device_id=sibling, device_id_type=MESH) for t in range(nt)]
        first = [copy(t, j, (x, y, c), (*chip, c), src=w_refs[t].at[c]) for t in range(nt) for j, chip in enumerate(chips)]
        if not finishing:
            return own, first
        landed = [copy(t, j, (*chip, c), (x, y, c)) for t in range(nt) for j, chip in enumerate(chips)]
        passed = [copy(t, 3 + j, (*chip, c), sibling) for t in range(nt) for j, chip in enumerate(chips)]
        from_sibling = [copy(t, 3 + j, (*chip, 1 - c), (x, y, c)) for t in range(nt) for j, chip in enumerate(chips)]
        return own, first, landed, passed, from_sibling

    def start(w_refs, out_refs, sems):
        own, first = parts(w_refs, out_refs, sems, False)
        for cp in own + first:
            cp.start()

    def finish(w_refs, out_refs, sems):
        own, first, landed, passed, from_sibling = parts(w_refs, out_refs, sems, True)
        for arrived, fwd in zip(landed, passed):
            arrived.wait_recv()
            fwd.start()
        for cp in from_sibling:
            cp.wait_recv()
        for cp in first + passed:
            cp.wait_send()
        for cp in own:
            cp.wait()

    sems = [pltpu.SemaphoreType.DMA((6 * nt,)), pltpu.SemaphoreType.DMA((6 * nt,)),
            pltpu.SemaphoreType.DMA((nt,)), pltpu.SemaphoreType.DMA((nt,))]
    return _Hosted(list(shards), [_sds((N_CHIPS, *w.shape), w.dtype) for w in shards], sems, start, finish)


def all_gather_chips(shards, name):
    comm = _gather_comm(shards)
    nt = len(shards)

    def body(*refs):
        comm.start(refs[:nt], refs[nt:2 * nt], refs[2 * nt:])
        comm.finish(refs[:nt], refs[nt:2 * nt], refs[2 * nt:])

    hbm = pl.BlockSpec(memory_space=pl.ANY)
    return pl.pallas_call(body, name=name, out_shape=comm.out_shape, in_specs=[hbm] * nt, out_specs=[hbm] * nt,
                          scratch_shapes=comm.sems)(*shards)


def _row_block(r, n, itemsize):
    best = None
    for br in range(16, r + 1, 16):
        if r % br == 0 and br * n * itemsize <= COMM_BLOCK_BYTES:
            best = br
    assert best is not None, (r, n)
    return best


def _scatter_comm(parts):
    nt = len(parts)

    def copies(p_refs, b_refs, sems, arriving):
        send_sems, recv_sems = sems
        x, y, c = _place()
        me = 4 * x + 2 * y + c
        cps = []
        for t in range(nt):
            for r in range(1, 8):
                tx = 1 - x if r & 4 else x
                ty = 1 - y if r & 2 else y
                tc = 1 - c if r & 1 else c
                src, dst = (2 * x + y, c), 4 * tx + 2 * ty + tc
                if not arriving:
                    src, dst = (2 * tx + ty, tc), me
                cps.append(pltpu.make_async_remote_copy(
                    src_ref=p_refs[t].at[src], dst_ref=b_refs[t].at[dst], send_sem=send_sems.at[7 * t + r - 1],
                    recv_sem=recv_sems.at[7 * t + r - 1], device_id=(tx, ty, tc), device_id_type=MESH))
        return cps

    def start(p_refs, b_refs, sems):
        for cp in copies(p_refs, b_refs, sems, False):
            cp.start()

    def finish(p_refs, b_refs, sems):
        for cp in copies(p_refs, b_refs, sems, True):
            cp.wait_recv()
        for cp in copies(p_refs, b_refs, sems, False):
            cp.wait_send()

    sems = [pltpu.SemaphoreType.DMA((7 * nt,)), pltpu.SemaphoreType.DMA((7 * nt,))]
    return _Hosted(list(parts), [_sds((2 * N_CHIPS, *p.shape[2:]), p.dtype) for p in parts], sems, start, finish)


def scatter_exchange(parts, name):
    comm = _scatter_comm(parts)
    nt = len(parts)

    def body(*refs):
        comm.start(refs[:nt], refs[nt:2 * nt], refs[2 * nt:])
        comm.finish(refs[:nt], refs[nt:2 * nt], refs[2 * nt:])

    hbm = pl.BlockSpec(memory_space=pl.ANY)
    return pl.pallas_call(body, name=name, out_shape=comm.out_shape, in_specs=[hbm] * nt, out_specs=[hbm] * nt,
                          scratch_shapes=comm.sems)(*parts)


def sum_devices(own, recv, place, name, slot=(0, 1, None)):
    _, _, r, n = own.shape
    layer, n_layers, buf = slot
    br = _row_block(r, n, 4 * 8)

    def body(p_ref, o_ref, *rest):
        acc = o_ref[...]
        for kk in range(7):
            acc = acc + rest[kk][...].astype(F32)
        rest[-1][...] = acc

    def arrived(rel):
        return pl.BlockSpec((None, br, n), lambda i, pref: (jnp.bitwise_xor(pref[0], rel), i, 0))

    in_specs = [pl.BlockSpec((None, None, br, n), lambda i, pref: (pref[2], pref[1], i, 0))]
    in_specs += [arrived(rel) for rel in range(1, 8)]
    args = [own] + [recv] * 7
    aliases = {}
    if buf is not None:
        in_specs.append(pl.BlockSpec(memory_space=pl.ANY))
        args.append(buf)
        aliases = {9: 0}
    return pl.pallas_call(
        body, name=name,
        grid_spec=pltpu.PrefetchScalarGridSpec(
            num_scalar_prefetch=1, grid=(r // br,), in_specs=in_specs,
            out_specs=pl.BlockSpec((None, None, br, n), lambda i, pref: (layer, pref[1], i, 0))),
        out_shape=_sds((n_layers, 2, r, n), F32), input_output_aliases=aliases, compiler_params=_cp(1),
    )(place, *args)


def sibling_join_halves(bufs, name):
    nt = len(bufs)
    layers = [bf.shape[0] for bf in bufs]
    first = [sum(layers[:t]) for t in range(nt)]

    def body(*refs):
        o_refs = refs[nt:2 * nt]
        send_sems, recv_sems = refs[2 * nt:]
        x, y, c = _place()

        def copy(t, l, hh):
            return pltpu.make_async_remote_copy(
                src_ref=o_refs[t].at[l, hh], dst_ref=o_refs[t].at[l, hh], send_sem=send_sems.at[first[t] + l],
                recv_sem=recv_sems.at[first[t] + l], device_id=(x, y, 1 - c), device_id_type=MESH)

        cps = [copy(t, l, c) for t in range(nt) for l in range(layers[t])]
        for cp in cps:
            cp.start()
        for t in range(nt):
            for l in range(layers[t]):
                copy(t, l, 1 - c).wait_recv()
        for cp in cps:
            cp.wait_send()

    hbm = pl.BlockSpec(memory_space=pl.ANY)
    return pl.pallas_call(
        body, name=name, out_shape=[_sds(bf.shape, bf.dtype) for bf in bufs],
        in_specs=[hbm] * nt, out_specs=[hbm] * nt, input_output_aliases={t: t for t in range(nt)},
        scratch_shapes=[pltpu.SemaphoreType.DMA((sum(layers),)), pltpu.SemaphoreType.DMA((sum(layers),))],
    )(*bufs)


_SHARD_KIND = {"mla_w_in": "rows", "mla_w_uq": "cols", "mla_w_uk": "cols", "mla_w_uv": "cols", "mla_w_o": "rows",
               "fox_w_in": "cols", "fox_w_o": "rows", "ffn_w_gate": "chunk", "ffn_w_up": "chunk", "ffn_w_down": "chunk"}
_PACKED = tuple(_SHARD_KIND)


def _halves(shard):
    if shard.ndim == 3 and shard.shape[0] == 2:
        return shard
    r, n = shard.shape[-2:]
    return shard.reshape(2, r // 2, n)


def _cols_to_full(g):
    return jnp.transpose(g, (1, 0, 2)).reshape(g.shape[1], -1)


def _full_to_cols(w):
    k, n4 = w.shape
    return jnp.transpose(w.reshape(k, N_CHIPS, n4 // N_CHIPS), (1, 0, 2))


def _uq_perm():
    per = MLA_NOPE + MLA_ROPE
    half = MLA_ROPE // 2
    nope = [h * per + d for h in range(MLA_HEADS) for d in range(MLA_NOPE)]
    r1 = [h * per + MLA_NOPE + r for h in range(MLA_HEADS) for r in range(half)]
    r2 = [h * per + MLA_NOPE + half + r for h in range(MLA_HEADS) for r in range(half)]
    perm = np.array(nope + r1 + r2, dtype=np.int32)
    return perm, np.argsort(perm).astype(np.int32)


def _rope_matrices():
    half = MLA_ROPE // 2
    nr = MLA_HEADS * MLA_ROPE
    to_heads = np.zeros((nr, nr), np.float32)
    from_heads = np.zeros((MLA_HEADS * 128, nr), np.float32)
    for e in range(2):
        for h in range(MLA_HEADS):
            for r in range(half):
                to_heads[e * MLA_HEADS * half + h * half + r, h * MLA_ROPE + e * half + r] = 1.0
                from_heads[h * 128 + e * half + r, e * MLA_HEADS * half + h * half + r] = 1.0
    head_sum = np.tile(np.eye(MLA_ROPE, dtype=np.float32), (2 * MLA_HEADS, 1))
    dup = np.concatenate([np.eye(MLA_ROPE, dtype=np.float32)] * 2, axis=1)
    return to_heads, from_heads, head_sum, dup


def _ffn_weights(gathered):
    return tuple(g.reshape(N_CHIPS, 2 * g.shape[2], g.shape[3]) for g in gathered)


def _fox_weights(gathered):
    w_in, w_o = gathered
    w_in = _cols_to_full(w_in.reshape(N_CHIPS, 2 * w_in.shape[2], w_in.shape[3]))
    return w_in, w_o.reshape(-1, w_o.shape[-1])


def _local_step(x, positions, target, mods, wts, ln_g, ln_b, mla_g_q, mla_g_kv, fox_b_f, shards=None):
    nb, s, d = x.shape
    t = nb * s
    x0 = x.reshape(t, d)
    tgt = target.reshape(t, d)
    perm, inv_perm = _uq_perm()

    half = MLA_ROPE // 2
    inv_freq = ROPE_THETA ** (-jnp.arange(half, dtype=F32) / half)
    ang = positions.astype(F32).reshape(t, 1) * inv_freq
    cos, sin = jnp.cos(ang), jnp.sin(ang)
    cos8, sin8 = jnp.tile(cos, (1, MLA_HEADS)), jnp.tile(sin, (1, MLA_HEADS))
    cos64 = jnp.concatenate([cos, cos], axis=1)
    sin64s = jnp.concatenate([-sin, sin], axis=1)
    swap64 = jnp.asarray(np.roll(np.eye(MLA_ROPE, dtype=np.float32), half, axis=1))
    to_heads, from_heads, head_sum, dup = _rope_matrices()
    to_heads, from_heads = jnp.asarray(to_heads, dtype=BF16), jnp.asarray(from_heads, dtype=BF16)
    head_sum, dup = jnp.asarray(head_sum, dtype=BF16), jnp.asarray(dup, dtype=BF16)
    sel_mla = jnp.asarray(np.pad(np.kron(np.eye(MLA_HEADS, dtype=np.float32), np.ones((MLA_V, 1), np.float32)),
                                 ((0, 0), (0, 128 - MLA_HEADS))))
    sel_fox = jnp.asarray(np.pad(np.kron(np.eye(FOX_HEADS, dtype=np.float32), np.ones((FOX_HD, 1), np.float32)),
                                 ((0, 0), (0, 128 - FOX_HEADS))))
    tri = jnp.asarray(np.tril(np.ones((128, 128), np.float32)))
    triu = jnp.asarray(np.triu(np.ones((128, 128), np.float32)))
    onehot16 = jnp.asarray(np.eye(16, 128, dtype=np.float32))

    def vec(a):
        return a.reshape(1, -1)

    def carried(key):
        return None if shards is None else _gather_comm(shards[key])

    def split(res):
        return (res, None) if shards is None else res

    w_uq_p = wts["mla_w_uq"][:, perm]
    b_f_pad = jnp.pad(fox_b_f.reshape(1, -1), ((0, 0), (0, 128 - FOX_HEADS)))

    sh_a, sc_a, gt_a, sh_f, sc_f, gt_f = mods[0]
    h_in, u_m = mod_linear(x0, sh_a, sc_a, wts["mla_w_in"], F32, "mla_in", emit_u=True)
    q_m, kn_m, v_m, kr2_m, cq_m, ckv_m = mla_mid_fwd(
        h_in, vec(mla_g_q), vec(mla_g_kv), w_uq_p, wts["mla_w_uk"], wts["mla_w_uv"], cos8, sin8, cos64, sin64s, swap64,
        to_heads, dup, "mla_mid")
    (o_m, lse_m), got = split(mla_attn_fwd(q_m, kn_m, kr2_m, v_m, nb, "mla_attn", hosted=carried("ffn0")))
    ffn0_w = wts["ffn"][0] if got is None else _ffn_weights(got)
    y0, x1 = linear_resid_ln(o_m, wts["mla_w_o"], x0, gt_a, vec(ln_g[0, 0]), vec(ln_b[0, 0]), "mla_out")
    (u_f0, hg0, hu0, y1, x2), got = split(ffn_fwd(x1, sh_f, sc_f, gt_f, *ffn0_w, vec(ln_g[0, 1]), vec(ln_b[0, 1]), "ffn0",
                                                  hosted=carried("fox")))
    fox_w_in, fox_w_o = (wts["fox_w_in"], wts["fox_w_o"]) if got is None else _fox_weights(got)
    fox_w_qkv = fox_w_in[:, :3 * d]
    fox_w_f = jnp.pad(fox_w_in[:, 3 * d:], ((0, 0), (0, 128 - FOX_HEADS)))
    sh_a1, sc_a1, gt_a1, sh_f1, sc_f1, gt_f1 = mods[1]
    qkv, u_x = mod_linear(x2, sh_a1, sc_a1, fox_w_qkv, BF16, "fox_qkv", tn=1024, emit_u=True)
    hf = mod_linear(x2, sh_a1, sc_a1, fox_w_f, F32, "fox_f")
    cum = fox_gate_fwd(hf, b_f_pad, tri, nb, "fox_gate")
    cum_rows = rows16(cum, "fox_cum_rows")
    (o_x, lse_x), got = split(fox_attn_fwd(qkv, cum, cum_rows, nb, "fox_attn", hosted=carried("ffn1")))
    ffn1_w = wts["ffn"][1] if got is None else _ffn_weights(got)
    y2, x3 = linear_resid_ln(o_x, fox_w_o, x2, gt_a1, vec(ln_g[1, 0]), vec(ln_b[1, 0]), "fox_out")
    u_f1, hg1, hu1, y3, x4 = ffn_fwd(x3, sh_f1, sc_f1, gt_f1, *ffn1_w, vec(ln_g[1, 1]), vec(ln_b[1, 1]), "ffn1")
    dx4, sq_err = loss_grad(x4, tgt, "loss")
    loss_part = 0.5 * jnp.sum(sq_err) / d

    parts, recv = {}, {}

    def halves_of(g):
        return g.reshape(N_CHIPS, 2, g.shape[1] // 2, g.shape[2])

    def scatter(keys, sent):
        return None if shards is None else _scatter_comm([sent[k] for k in keys])

    def landed(keys, got):
        if got is not None:
            recv.update(zip(keys, got))

    def ffn_grads(layer, u, dhg, dhu, act, dy):
        sent = {}
        for n, (a_op, b_op) in (("ffn_w_gate", (u[None], dhg)), ("ffn_w_up", (u[None], dhu)), ("ffn_w_down", (act, dy[None]))):
            g32, g16 = wgrad(a_op, b_op, "ffn%d_d%s" % (layer, n[4:]), with_bf16=True)
            parts["%s/%d" % (n, layer)], sent["%s/%d" % (n, layer)] = halves_of(g32), halves_of(g16)
        return sent

    dz3, dy3, dg11, db11, dgt_f1 = ln_bwd(dx4, x3, y3, gt_f1, vec(ln_g[1, 1]), "ffn1_ln_bwd")
    dhg1, dhu1, act1, dx3, dsc_f1, dsh_f1 = ffn_bwd(dy3, hg1, hu1, *ffn1_w, dz3, x3, sc_f1, "ffn1_bwd")
    sent = ffn_grads(1, u_f1, dhg1, dhu1, act1, dy3)
    dz2, dy2, dg10, db10, dgt_a1 = ln_bwd(dx3, x2, y2, gt_a1, vec(ln_g[1, 0]), "fox_ln_bwd")
    do_x, delta_x = linear_nt_delta(dy2, fox_w_o, o_x, sel_fox, "fox_out_bwd")
    (dq_x, dk_x, dv_x, dfq_x, dfk_x), got = split(fox_attn_bwd(
        qkv, do_x, cum, cum_rows, rows16(lse_x, "fox_lse_rows"), rows16(delta_x, "fox_delta_rows"), nb, "fox_attn_bwd",
        hosted=scatter(list(sent), sent)))
    landed(list(sent), got)
    dcum = tokens128(dfq_x + dfk_x, onehot16, "fox_dcum")
    dhf, dbf = fox_gate_bwd(dcum, hf, b_f_pad, triu, nb, "fox_gate_bwd")
    dqkv = jnp.concatenate([dq_x.astype(BF16), dk_x, dv_x], axis=1)
    dx2, dsc_a1, dsh_a1 = linear_nt_mod_bwd([(dqkv, fox_w_qkv), (dhf, fox_w_f)], dz2, x2, sc_a1, "fox_in_bwd")
    dw_qkv = wgrad(u_x[None], dqkv[None], "fox_dwqkv")[0]
    dw_f = wgrad(u_x[None], dhf[None], "fox_dwf")[0]
    parts["fox_w_in"] = halves_of(_full_to_cols(jnp.concatenate([dw_qkv, dw_f[:, :FOX_HEADS]], axis=1)))
    parts["fox_w_o"] = wgrad(o_x[None], dy2[None], "fox_dwo")[0].reshape(N_CHIPS, 2, -1, d)
    sent = {k: parts[k].astype(BF16) for k in ("fox_w_in", "fox_w_o")}
    dz1, dy1, dg01, db01, dgt_f0 = ln_bwd(dx2, x1, y1, gt_f, vec(ln_g[0, 1]), "ffn0_ln_bwd")
    (dhg0, dhu0, act0, dx1, dsc_f0, dsh_f0), got = split(ffn_bwd(dy1, hg0, hu0, *ffn0_w, dz1, x1, sc_f, "ffn0_bwd",
                                                                 hosted=scatter(list(sent), sent)))
    landed(list(sent), got)
    sent = ffn_grads(0, u_f0, dhg0, dhu0, act0, dy1)
    dz0, dy0, dg00, db00, dgt_a0 = ln_bwd(dx1, x0, y0, gt_a, vec(ln_g[0, 0]), "mla_ln_bwd")
    do_m, delta_m = linear_nt_delta(dy0, wts["mla_w_o"], o_m, sel_mla, "mla_out_bwd")
    (dqn_m, dqr_m, dkn_m, dkr_m, dv_m), got = split(mla_attn_bwd(
        q_m, kn_m, kr2_m, v_m, do_m, rows16(lse_m, "mla_lse_rows"), rows16(delta_m, "mla_delta_rows"), nb,
        "mla_attn_bwd", hosted=scatter(list(sent), sent)))
    landed(list(sent), got)
    dh_in, dq_pre, dgq, dgkv = mla_mid_bwd(
        dqn_m, dqr_m, dkn_m, dv_m, dkr_m, h_in, vec(mla_g_q), vec(mla_g_kv), w_uq_p, wts["mla_w_uk"],
        wts["mla_w_uv"], cos8, sin8, cos64, sin64s, swap64, from_heads, head_sum, "mla_mid_bwd")
    parts["mla_w_o"] = wgrad(o_m[None], dy0[None], "mla_dwo")[0].reshape(N_CHIPS, 2, -1, d)
    parts["mla_w_uq"] = halves_of(_full_to_cols(wgrad(cq_m[None], dq_pre[None], "mla_dwuq")[0][:, inv_perm]))
    parts["mla_w_uk"] = halves_of(_full_to_cols(wgrad(ckv_m[None], dkn_m[None], "mla_dwuk")[0]))
    parts["mla_w_uv"] = halves_of(_full_to_cols(wgrad(ckv_m[None], dv_m[None], "mla_dwuv")[0]))
    parts["mla_w_in"] = wgrad(u_m[None], dh_in[None], "mla_dwin")[0].reshape(N_CHIPS, 2, -1, h_in.shape[1])
    dx0, dsc_a0, dsh_a0 = linear_nt_mod_bwd([(dh_in, wts["mla_w_in"])], dz0, x0, sc_a, "mla_in_bwd")

    dmods = [(dsh_a0, dsc_a0, dgt_a0, dsh_f0, dsc_f0, dgt_f0), (dsh_a1, dsc_a1, dgt_a1, dsh_f1, dsc_f1, dgt_f1)]
    d_ln_g = jnp.stack([jnp.concatenate([dg00, dg01], axis=0), jnp.concatenate([dg10, dg11], axis=0)])
    d_ln_b = jnp.stack([jnp.concatenate([db00, db01], axis=0), jnp.concatenate([db10, db11], axis=0)])
    return loss_part, dx0.reshape(nb, s, d), (parts, recv), dmods, d_ln_g, d_ln_b, dgq, dgkv, dbf[:, :FOX_HEADS]


def _pad_rows(a, rows):
    return jnp.pad(a, ((0, rows - a.shape[0]), (0, 0)))


def kernel(x, c, positions, mla_w_in, mla_g_q, mla_w_uq, mla_g_kv, mla_w_uk, mla_w_uv, mla_w_o, fox_w_in, fox_b_f, fox_w_o, ada_w, ada_b, ffn_w_gate, ffn_w_up, ffn_w_down, ln_g, ln_b, loss_target, m_mla_w_in, m_mla_g_q, m_mla_w_uq, m_mla_g_kv, m_mla_w_uk, m_mla_w_uv, m_mla_w_o, m_fox_w_in, m_fox_b_f, m_fox_w_o, m_ada_w, m_ada_b, m_ffn_w_gate, m_ffn_w_up, m_ffn_w_down, m_ln_g, m_ln_b, v_mla_w_in, v_mla_g_q, v_mla_w_uq, v_mla_g_kv, v_mla_w_uk, v_mla_w_uv, v_mla_w_o, v_fox_w_in, v_fox_b_f, v_fox_w_o, v_ada_w, v_ada_b, v_ffn_w_gate, v_ffn_w_up, v_ffn_w_down, v_ln_g, v_ln_b):
    args = dict(locals())
    nb, s, d = x.shape
    ax, ay, ac = lax.axis_index("x"), lax.axis_index("y"), lax.axis_index("c")
    chip = 2 * ax + ay
    dev = 2 * chip + ac
    n_dev = 2 * N_CHIPS
    n_all = nb * n_dev

    shard_shapes = {n: (args[n].shape if _SHARD_KIND[n] == "chunk" else args[n].shape[1:]) for n in _PACKED}

    def block(n, layer=None):
        w = args[n].reshape(shard_shapes[n]) if layer is None else args[n][layer]
        return _halves(w.astype(BF16))

    mla_names = [n for n in _PACKED if n.startswith("mla")]
    wts = {}
    for n, g in zip(mla_names, all_gather_chips([block(n) for n in mla_names], "gather_mla")):
        g = g.reshape(N_CHIPS, *shard_shapes[n])
        wts[n] = g.reshape(-1, g.shape[-1]) if _SHARD_KIND[n] == "rows" else _cols_to_full(g)
    ffn_names = ("ffn_w_gate", "ffn_w_up", "ffn_w_down")
    shards = {"ffn0": [block(n, 0) for n in ffn_names], "fox": [block("fox_w_in"), block("fox_w_o")],
              "ffn1": [block(n, 1) for n in ffn_names]}

    c_all = all_gather8(_pad_rows(c, 8), "gather_c").reshape(n_dev, 8, d)[:, :nb].reshape(n_all, d)
    mod_part = ada_mod_part(c_all, ada_w, "ada_mod")
    ncol = mod_part.shape[-1]
    mod_g = all_gather8(mod_part.reshape(DEPTH * n_all, ncol), "gather_mod")
    mod_g = mod_g.reshape(N_CHIPS, 2, DEPTH, n_all, ncol)[:, 0]
    mod_full = jnp.transpose(mod_g, (1, 2, 0, 3)).reshape(DEPTH, n_all, N_CHIPS * ncol) + ada_b[:, None, :]
    mod_loc = lax.dynamic_slice_in_dim(mod_full, dev * nb, nb, axis=1)
    mods = [tuple(mod_loc[i, :, k * d:(k + 1) * d].reshape(nb, 1, d) for k in range(6)) for i in range(DEPTH)]

    ln_cols = ln_g.shape[-1]
    ln_blk = jnp.concatenate([ln_g.reshape(2 * DEPTH, ln_cols), ln_b.reshape(2 * DEPTH, ln_cols)], axis=0)
    ln_all = all_gather8(ln_blk, "gather_ln").reshape(N_CHIPS, 2, 4 * DEPTH, ln_cols)[:, 0]
    ln_all = jnp.transpose(ln_all, (1, 0, 2)).reshape(4 * DEPTH, d)
    ln_g_full = ln_all[:2 * DEPTH].reshape(DEPTH, 2, d)
    ln_b_full = ln_all[2 * DEPTH:].reshape(DEPTH, 2, d)

    loss_part, grad_x, (parts, recv), dmods, d_ln_g, d_ln_b, dgq, dgkv, dbf = _local_step(
        x, positions, loss_target, mods, wts, ln_g_full, ln_b_full, mla_g_q[0], mla_g_kv[0], fox_b_f[0], shards)
    loss = lax.psum(loss_part, ("x", "y", "c"))

    dmod_rows = jnp.stack([jnp.concatenate([v_.reshape(nb, d) for v_ in dm], axis=1) for dm in dmods])
    small = jnp.concatenate([
        d_ln_g.reshape(2 * DEPTH, d), d_ln_b.reshape(2 * DEPTH, d),
        jnp.pad(jnp.concatenate([dgq, dgkv, dbf], axis=1), ((0, 0), (0, d - 2 * MLA_QR - FOX_HEADS))),
        dmod_rows.reshape(DEPTH * nb * 6, d)], axis=0)
    n_small = small.shape[0]
    small_rows = -(-n_small // 8) * 8
    small_all = all_gather8(_pad_rows(small, small_rows), "gather_stats").reshape(n_dev, small_rows, d)
    stat_sum = sum_leading(small_all, "sum_stats")
    g_ln_g = lax.dynamic_slice_in_dim(stat_sum[:2 * DEPTH], chip * ln_cols, ln_cols, axis=1).reshape(DEPTH, 2, ln_cols)
    g_ln_b = lax.dynamic_slice_in_dim(stat_sum[2 * DEPTH:4 * DEPTH], chip * ln_cols, ln_cols, axis=1).reshape(DEPTH, 2, ln_cols)
    row = stat_sum[4 * DEPTH]
    g_gq = row[:MLA_QR].reshape(1, MLA_QR)
    g_gkv = row[MLA_QR:2 * MLA_QR].reshape(1, MLA_KVR)
    g_bf = row[2 * MLA_QR:2 * MLA_QR + FOX_HEADS].reshape(1, FOX_HEADS)
    base = 4 * DEPTH + 1
    dmod_all = small_all[:, base:base + DEPTH * nb * 6].reshape(n_dev, DEPTH, nb, 6 * d)
    dmod_all = jnp.transpose(dmod_all, (1, 0, 2, 3)).reshape(DEPTH, n_all, 6 * d)
    g_ada_b = sum_leading(jnp.transpose(dmod_all, (1, 0, 2)), "sum_ada_b")
    dmod_mine = lax.dynamic_slice_in_dim(dmod_all, chip * ncol, ncol, axis=2)
    g_ada_w = ada_grad(c_all.T, dmod_mine, "ada_grad")

    late = [k for k in parts if k not in recv]
    recv.update(zip(late, scatter_exchange([parts[k].astype(BF16) for k in late], "rs_exchange_mla")))
    place = jnp.stack([dev, ac, chip]).astype(jnp.int32)
    bufs = []
    for n in _PACKED:
        if _SHARD_KIND[n] == "chunk":
            buf = None
            for layer in range(DEPTH):
                key = "%s/%d" % (n, layer)
                buf = sum_devices(parts[key], recv[key], place, "rs_sum_%s%d" % (n, layer), slot=(layer, DEPTH, buf))
        else:
            buf = sum_devices(parts[n], recv[n], place, "rs_sum_" + n)
        bufs.append(buf)
    joined = sibling_join_halves(bufs, "rs_join")
    g_big = {n: j.reshape(shard_shapes[n]) for n, j in zip(_PACKED, joined)}

    g_out = {
        "mla_w_in": g_big["mla_w_in"], "mla_g_q": g_gq, "mla_w_uq": g_big["mla_w_uq"], "mla_g_kv": g_gkv,
        "mla_w_uk": g_big["mla_w_uk"], "mla_w_uv": g_big["mla_w_uv"], "mla_w_o": g_big["mla_w_o"],
        "fox_w_in": g_big["fox_w_in"], "fox_b_f": g_bf, "fox_w_o": g_big["fox_w_o"],
        "ada_w": g_ada_w, "ada_b": g_ada_b, "ffn_w_gate": g_big["ffn_w_gate"], "ffn_w_up": g_big["ffn_w_up"],
        "ffn_w_down": g_big["ffn_w_down"], "ln_g": g_ln_g, "ln_b": g_ln_b}
    names = ["mla_w_in", "mla_g_q", "mla_w_uq", "mla_g_kv", "mla_w_uk", "mla_w_uv", "mla_w_o", "fox_w_in", "fox_b_f",
             "fox_w_o", "ada_w", "ada_b", "ffn_w_gate", "ffn_w_up", "ffn_w_down", "ln_g", "ln_b"]
    small_names = ["mla_g_q", "mla_g_kv", "fox_b_f", "ada_b", "ln_g", "ln_b"]
    deltas, new_m, new_v = {}, {}, {}
    for n in names:
        if n in small_names:
            continue
        shp = args[n].shape
        two_d = (-1, shp[-1])
        dl, mn, vn = adamw(args[n].reshape(two_d), g_out[n].reshape(two_d), args["m_" + n].reshape(two_d),
                           args["v_" + n].reshape(two_d), "adamw_" + n)
        deltas[n], new_m[n], new_v[n] = dl.reshape(shp), mn.reshape(shp), vn.reshape(shp)

    def small_pack(prefix, src):
        flat = jnp.concatenate([src[prefix + n].reshape(-1) for n in small_names])
        size = -(-flat.shape[0] // (8 * 128)) * 8 * 128
        return jnp.pad(flat, (0, size - flat.shape[0])).reshape(-1, 128)

    sd, sm, sv = adamw(small_pack("", args), small_pack("", g_out), small_pack("m_", args), small_pack("v_", args),
                       "adamw_small")
    off = 0
    for n in small_names:
        shp = args[n].shape
        size = math.prod(shp)
        deltas[n] = sd.reshape(-1)[off:off + size].reshape(shp)
        new_m[n] = sm.reshape(-1)[off:off + size].reshape(shp)
        new_v[n] = sv.reshape(-1)[off:off + size].reshape(shp)
        off += size

    outs = [loss, grad_x]
    outs += [g_out[n].reshape(args[n].shape) for n in names]
    outs += [deltas[n] for n in names] + [new_m[n] for n in names] + [new_v[n] for n in names]
    return tuple(outs)
```

```python
import functools
import math

import numpy as np
import jax
import jax.numpy as jnp
from jax import lax
from jax.experimental import pallas as pl
from jax.experimental.pallas import tpu as pltpu

F32 = jnp.float32
BF16 = jnp.bfloat16
MESH = pl.DeviceIdType.MESH

D_MODEL = 1024
DEPTH = 2
MLA_HEADS = 8
MLA_NOPE = 128
MLA_ROPE = 64
MLA_V = 128
MLA_QR = 256
MLA_KVR = 256
ROPE_THETA = 10000.0
FOX_HEADS = 16
FOX_HD = 64
D_FF = 2816
N_CHIPS = 4
FF_CHUNK = D_FF // N_CHIPS
ALPHA = (2.0 * DEPTH) ** 0.25
EPS = 1e-5
ADAM_LR = 0.001
ADAM_B1 = 0.9
ADAM_B2 = 0.999
ADAM_EPS = 1e-08
ADAM_WD = 0.01
ADAM_STEP = 10

VMEM_LIMIT = 56 * 1024 * 1024
TOKEN_TILE = 512
ATTN_TILE = 512
COMM_BLOCK_BYTES = 2 * 1024 * 1024
ADAMW_BLOCK_BYTES = 1024 * 1024


def _cp(n_axes):
    return pltpu.CompilerParams(dimension_semantics=("arbitrary",) * n_axes, vmem_limit_bytes=VMEM_LIMIT)


def _dot(a, b):
    return jnp.dot(a, b, preferred_element_type=F32)


def _dot_nt(a, b):
    return lax.dot_general(a, b, (((1,), (1,)), ((), ())), preferred_element_type=F32)


def _dot_tn(a, b):
    return lax.dot_general(a, b, (((0,), (0,)), ((), ())), preferred_element_type=F32)


def _dot_f32(a, b):
    return jnp.dot(a, b, preferred_element_type=F32, precision=lax.Precision.HIGHEST)


def _sds(shape, dtype):
    return jax.ShapeDtypeStruct(shape, dtype)


def _place():
    return lax.axis_index("x"), lax.axis_index("y"), lax.axis_index("c")


class _Hosted:
    def __init__(self, inputs, out_shape, sems, start, finish):
        self.inputs, self.out_shape, self.sems, self.start, self.finish = inputs, out_shape, sems, start, finish


def _call(body, name, grid, in_specs, out_specs, out_shape, args, scratch_shapes=(), hosted=None):
    in_specs, out_specs, out_shape, scratch_shapes = list(in_specs), list(out_specs), list(out_shape), list(scratch_shapes)
    if hosted is None:
        return pl.pallas_call(body, name=name, grid=grid, in_specs=in_specs, out_specs=out_specs, out_shape=out_shape,
                              scratch_shapes=scratch_shapes, compiler_params=_cp(len(grid)))(*args)
    n_in, n_out, n_scr = len(in_specs), len(out_specs), len(scratch_shapes)
    h_in, h_out = len(hosted.inputs), len(hosted.out_shape)

    def carried(*refs):
        o0 = n_in + h_in
        s0 = o0 + n_out + h_out
        c_in, c_out, c_sem = refs[n_in:o0], refs[o0 + n_out:s0], refs[s0 + n_scr:]
        ids = [pl.program_id(a) for a in range(len(grid))]
        first = functools.reduce(jnp.logical_and, [i == 0 for i in ids])
        last = functools.reduce(jnp.logical_and, [i == g - 1 for i, g in zip(ids, grid)])

        @pl.when(first)
        def _():
            hosted.start(c_in, c_out, c_sem)

        body(*refs[:n_in], *refs[o0:o0 + n_out], *refs[s0:s0 + n_scr])

        @pl.when(last)
        def _():
            hosted.finish(c_in, c_out, c_sem)

    hbm = pl.BlockSpec(memory_space=pl.ANY)
    res = pl.pallas_call(
        carried, name=name, grid=grid, in_specs=in_specs + [hbm] * h_in, out_specs=out_specs + [hbm] * h_out,
        out_shape=out_shape + list(hosted.out_shape), scratch_shapes=scratch_shapes + list(hosted.sems),
        compiler_params=_cp(len(grid)))(*args, *hosted.inputs)
    return res[:n_out], res[n_out:]


def mod_linear(x, shift, scale, w, out_dtype, name, tn=None, emit_u=False):
    t, d = x.shape
    n = w.shape[1]
    tn = n if tn is None else tn
    tm = TOKEN_TILE
    tps = (t // shift.shape[0]) // tm

    def body(x_ref, sh_ref, sc_ref, w_ref, o_ref, *rest):
        u = (x_ref[...] * (1.0 + sc_ref[...]) + sh_ref[...]).astype(BF16)
        o_ref[...] = _dot(u, w_ref[...]).astype(out_dtype)
        if emit_u:
            @pl.when(pl.program_id(1) == 0)
            def _():
                rest[0][...] = u

    vec = pl.BlockSpec((None, 1, d), lambda i, j: (i // tps, 0, 0))
    out_shape = [_sds((t, n), out_dtype)]
    out_specs = [pl.BlockSpec((tm, tn), lambda i, j: (i, j))]
    if emit_u:
        out_shape.append(_sds((t, d), BF16))
        out_specs.append(pl.BlockSpec((tm, d), lambda i, j: (i, 0)))
    res = pl.pallas_call(
        body, name=name, grid=(t // tm, n // tn),
        in_specs=[pl.BlockSpec((tm, d), lambda i, j: (i, 0)), vec, vec,
                  pl.BlockSpec((d, tn), lambda i, j: (0, j))],
        out_specs=out_specs, out_shape=out_shape, compiler_params=_cp(2),
    )(x, shift, scale, w)
    return res if emit_u else res[0]


def _rms(h, g):
    rstd = lax.rsqrt(jnp.mean(h * h, axis=-1, keepdims=True) + EPS)
    return h * rstd, rstd


def mla_mid_fwd(h, g_q, g_kv, w_uq, w_uk, w_uv, cos8, sin8, cos64, sin64s, swap64, rope_to_heads, dup64, name):
    t = h.shape[0]
    tm = TOKEN_TILE
    hq = MLA_HEADS * MLA_NOPE
    hr = MLA_HEADS * MLA_ROPE // 2

    def body(h_ref, gq_ref, gkv_ref, wuq_ref, wuk_ref, wuv_ref, c8_ref, s8_ref, c64_ref, s64_ref, sw_ref, p_ref, d_ref,
             q_ref, kn_ref, v_ref, kr_ref, cq_ref, ckv_ref):
        hh = h_ref[...]
        cq = (_rms(hh[:, :MLA_QR], None)[0] * gq_ref[...]).astype(BF16)
        ckv = (_rms(hh[:, MLA_QR:MLA_QR + MLA_KVR], None)[0] * gkv_ref[...]).astype(BF16)
        cq_ref[...] = cq
        ckv_ref[...] = ckv
        q = _dot(cq, wuq_ref[...])
        x1 = q[:, hq:hq + hr]
        x2 = q[:, hq + hr:]
        cs = c8_ref[...]
        sn = s8_ref[...]
        rot = jnp.concatenate([x1 * cs - x2 * sn, x2 * cs + x1 * sn], axis=1).astype(BF16)
        q_ref[...] = jnp.concatenate([q[:, :hq].astype(BF16), _dot(rot, p_ref[...]).astype(BF16)], axis=1)
        kn_ref[...] = _dot(ckv, wuk_ref[...]).astype(BF16)
        v_ref[...] = _dot(ckv, wuv_ref[...]).astype(BF16)
        kr = hh[:, MLA_QR + MLA_KVR:]
        kr = (kr * c64_ref[...] + _dot_f32(kr, sw_ref[...]) * s64_ref[...]).astype(BF16)
        kr_ref[...] = _dot(kr, d_ref[...]).astype(BF16)

    def rows(n):
        return pl.BlockSpec((tm, n), lambda i: (i, 0))

    def whole(a):
        return pl.BlockSpec(a.shape, lambda i: (0,) * a.ndim)

    nq = w_uq.shape[1]
    return pl.pallas_call(
        body, name=name, grid=(t // tm,),
        in_specs=[rows(h.shape[1]), whole(g_q), whole(g_kv), whole(w_uq), whole(w_uk), whole(w_uv),
                  rows(hr), rows(hr), rows(MLA_ROPE), rows(MLA_ROPE), whole(swap64), whole(rope_to_heads), whole(dup64)],
        out_specs=[rows(nq), rows(hq), rows(hq), rows(2 * MLA_ROPE), rows(MLA_QR), rows(MLA_KVR)],
        out_shape=[_sds((t, nq), BF16), _sds((t, hq), BF16), _sds((t, hq), BF16), _sds((t, 2 * MLA_ROPE), BF16),
                   _sds((t, MLA_QR), BF16), _sds((t, MLA_KVR), BF16)],
        compiler_params=_cp(1),
    )(h, g_q, g_kv, w_uq, w_uk, w_uv, cos8, sin8, cos64, sin64s, swap64, rope_to_heads, dup64)


def _pick_lane(tile, idx):
    lane = lax.broadcasted_iota(jnp.int32, tile.shape, 1)
    return jnp.sum(jnp.where(lane == idx, tile, 0.0), axis=1, keepdims=True)


def _pick_row(tile, idx):
    row = lax.broadcasted_iota(jnp.int32, tile.shape, 0)
    return jnp.sum(jnp.where(row == idx, tile, 0.0), axis=0, keepdims=True)


def _put_lane(tile, idx, col):
    lane = lax.broadcasted_iota(jnp.int32, tile.shape, 1)
    return jnp.where(lane == idx, col, tile)


def _put_row(tile, idx, row):
    r = lax.broadcasted_iota(jnp.int32, tile.shape, 0)
    return tile + jnp.where(r == idx, row, 0.0)


def _causal_softmax_blocks(i, tq, heads):
    def block(j, carry, masked):
        new = []
        for (score_fn, pv_fn, _), (m, l, acc) in zip(heads, carry):
            sc = score_fn(j)
            if masked:
                keep = lax.broadcasted_iota(jnp.int32, sc.shape, 0) >= lax.broadcasted_iota(jnp.int32, sc.shape, 1)
                sc = jnp.where(keep, sc, -1e30)
            m_new = jnp.maximum(m, jnp.max(sc, axis=1, keepdims=True))
            a = jnp.exp(m - m_new)
            p = jnp.exp(sc - m_new)
            new.append((m_new, a * l + jnp.sum(p, axis=1, keepdims=True), a * acc + pv_fn(j, p.astype(BF16))))
        return tuple(new)

    init = tuple((jnp.full((tq, 1), -1e30, F32), jnp.zeros((tq, 1), F32), jnp.zeros((tq, dv), F32)) for _, _, dv in heads)
    carry = lax.fori_loop(0, i, lambda j, c: block(j, c, False), init)
    return [(acc / l, m + jnp.log(l)) for m, l, acc in block(i, carry, True)]


def fox_attn_fwd(qkv, cum, cum_rows, nb, name, hosted=None):
    t = qkv.shape[0]
    s = t // nb
    tq = ATTN_TILE
    nq = s // tq
    npairs = FOX_HEADS // 2
    scale = FOX_HD ** -0.5

    def body(q_ref, k_ref, v_ref, cum_ref, cr_ref, o_ref, lse_ref):
        i = pl.program_id(1)
        hp = pl.program_id(2)

        @pl.when(hp == 0)
        def _():
            lse_ref[...] = jnp.zeros_like(lse_ref)

        q = q_ref[...]
        low = lax.broadcasted_iota(jnp.int32, q.shape, 1) < FOX_HD
        cum_t = cum_ref[...]

        def rows_of(j):
            return pl.ds(pl.multiple_of(j * tq, tq), tq)

        def head(a):
            hd = 2 * hp + a
            qa = jnp.where(low if a == 0 else jnp.logical_not(low), q, jnp.zeros_like(q))
            fq = _pick_lane(cum_t, hd)
            return (lambda j: _dot_nt(qa, k_ref[rows_of(j), :]) * scale + fq - _pick_row(cr_ref[j], hd),
                    lambda j, p: _dot(p, v_ref[rows_of(j), :]), 2 * FOX_HD)

        (o_0, lse_0), (o_1, lse_1) = _causal_softmax_blocks(i, tq, [head(0), head(1)])
        o_ref[...] = jnp.where(low, o_0, o_1).astype(BF16)
        lse_ref[...] = _put_lane(_put_lane(lse_ref[...], 2 * hp, lse_0), 2 * hp + 1, lse_1)

    return _call(
        body, name, (nb, nq, npairs),
        [pl.BlockSpec((tq, 128), lambda b, i, hp: (b * nq + i, hp)),
         pl.BlockSpec((s, 128), lambda b, i, hp: (b, npairs + hp)),
         pl.BlockSpec((s, 128), lambda b, i, hp: (b, 2 * npairs + hp)),
         pl.BlockSpec((tq, 128), lambda b, i, hp: (b * nq + i, 0)),
         pl.BlockSpec((nq, 16, tq), lambda b, i, hp: (b, 0, 0))],
        [pl.BlockSpec((tq, 128), lambda b, i, hp: (b * nq + i, hp)),
         pl.BlockSpec((tq, 128), lambda b, i, hp: (b * nq + i, 0))],
        [_sds((t, D_MODEL), BF16), _sds((t, 128), F32)], (qkv, qkv, qkv, cum, cum_rows), hosted=hosted)


def mla_attn_fwd(q, kn, kr2, v, nb, name, hosted=None):
    t = q.shape[0]
    s = t // nb
    tq = ATTN_TILE
    nq = s // tq
    npairs = MLA_HEADS // 2
    scale = (MLA_NOPE + MLA_ROPE) ** -0.5

    def body(qn_ref, qr_ref, kn_ref, kr_ref, v_ref, o_ref, lse_ref):
        i = pl.program_id(1)
        hp = pl.program_id(2)

        @pl.when(hp == 0)
        def _():
            lse_ref[...] = jnp.zeros_like(lse_ref)

        qr = qr_ref[...]
        low = lax.broadcasted_iota(jnp.int32, qr.shape, 1) < MLA_ROPE

        def rows_of(j):
            return pl.ds(pl.multiple_of(j * tq, tq), tq)

        def head(a):
            cols = slice(a * MLA_NOPE, (a + 1) * MLA_NOPE)
            q_cat = jnp.concatenate([qn_ref[:, cols], jnp.where(low if a == 0 else jnp.logical_not(low), qr,
                                                                jnp.zeros_like(qr))], axis=1)
            return (lambda j: _dot_nt(q_cat, jnp.concatenate([kn_ref[rows_of(j), cols], kr_ref[rows_of(j), :]], axis=1)) * scale,
                    lambda j, p: _dot(p, v_ref[rows_of(j), cols]), MLA_V)

        (o_0, lse_0), (o_1, lse_1) = _causal_softmax_blocks(i, tq, [head(0), head(1)])
        o_ref[...] = jnp.concatenate([o_0, o_1], axis=1).astype(BF16)
        lse_ref[...] = _put_lane(_put_lane(lse_ref[...], 2 * hp, lse_0), 2 * hp + 1, lse_1)

    wide = 2 * MLA_NOPE
    return _call(
        body, name, (nb, nq, npairs),
        [pl.BlockSpec((tq, wide), lambda b, i, hp: (b * nq + i, hp)),
         pl.BlockSpec((tq, 128), lambda b, i, hp: (b * nq + i, MLA_HEADS + hp)),
         pl.BlockSpec((s, wide), lambda b, i, hp: (b, hp)),
         pl.BlockSpec((s, 128), lambda b, i, hp: (b, 0)),
         pl.BlockSpec((s, wide), lambda b, i, hp: (b, hp))],
        [pl.BlockSpec((tq, wide), lambda b, i, hp: (b * nq + i, hp)),
         pl.BlockSpec((tq, 128), lambda b, i, hp: (b * nq + i, 0))],
        [_sds((t, MLA_HEADS * MLA_V), BF16), _sds((t, 128), F32)], (q, q, kn, kr2, v), hosted=hosted)


def rows16(a, name):
    t = a.shape[0]
    tq = ATTN_TILE

    def body(a_ref, o_ref):
        o_ref[...] = a_ref[...].T[:16, :]

    return pl.pallas_call(
        body, name=name, grid=(t // tq,), in_specs=[pl.BlockSpec((tq, 128), lambda n: (n, 0))],
        out_specs=pl.BlockSpec((None, 16, tq), lambda n: (n, 0, 0)), out_shape=_sds((t // tq, 16, tq), F32),
        compiler_params=_cp(1),
    )(a)


def tokens128(rows, onehot, name):
    nblk, _, tq = rows.shape

    def body(r_ref, e_ref, o_ref):
        o_ref[...] = lax.dot_general(r_ref[...], e_ref[...], (((0,), (0,)), ((), ())), preferred_element_type=F32,
                                     precision=lax.Precision.HIGHEST)

    return pl.pallas_call(
        body, name=name, grid=(nblk,),
        in_specs=[pl.BlockSpec((None, 16, tq), lambda n: (n, 0, 0)), pl.BlockSpec((16, 128), lambda n: (0, 0))],
        out_specs=pl.BlockSpec((tq, 128), lambda n: (n, 0)), out_shape=_sds((nblk * tq, 128), F32),
        compiler_params=_cp(1),
    )(rows, onehot)


def _layer_norm(z, g, b):
    mu = jnp.mean(z, axis=-1, keepdims=True)
    zc = z - mu
    rstd = lax.rsqrt(jnp.mean(zc * zc, axis=-1, keepdims=True) + EPS)
    xhat = zc * rstd
    return xhat * g + b, xhat, rstd


def linear_resid_ln(a, w, x_in, gate, ln_g, ln_b, name):
    t, kdim = a.shape
    d = w.shape[1]
    tm = TOKEN_TILE
    tps = (t // gate.shape[0]) // tm

    def body(a_ref, w_ref, x_ref, gt_ref, g_ref, b_ref, y_ref, xo_ref):
        y = _dot(a_ref[...], w_ref[...])
        y_ref[...] = y
        z = ALPHA * x_ref[...] + (1.0 + gt_ref[...]) * y
        xo_ref[...] = _layer_norm(z, g_ref[...], b_ref[...])[0]

    rows = pl.BlockSpec((tm, d), lambda i: (i, 0))
    vec = pl.BlockSpec((1, d), lambda i: (0, 0))
    return pl.pallas_call(
        body, name=name, grid=(t // tm,),
        in_specs=[pl.BlockSpec((tm, kdim), lambda i: (i, 0)), pl.BlockSpec((kdim, d), lambda i: (0, 0)), rows,
                  pl.BlockSpec((None, 1, d), lambda i: (i // tps, 0, 0)), vec, vec],
        out_specs=[rows, rows], out_shape=[_sds((t, d), F32), _sds((t, d), F32)],
        compiler_params=_cp(1),
    )(a, w, x_in, gate, ln_g, ln_b)


def ffn_fwd(x_in, shift, scale, gate, wg, wu, wd, ln_g, ln_b, name, hosted=None):
    t, d = x_in.shape
    c, _, fc = wg.shape
    tm = TOKEN_TILE
    tps = (t // gate.shape[0]) // tm

    def body(x_ref, sh_ref, sc_ref, gt_ref, wg_ref, wu_ref, wd_ref, g_ref, b_ref,
             u_ref, hg_ref, hu_ref, y_ref, xo_ref, acc_ref):
        cc = pl.program_id(1)

        @pl.when(cc == 0)
        def _():
            u_ref[...] = (x_ref[...] * (1.0 + sc_ref[...]) + sh_ref[...]).astype(BF16)
            acc_ref[...] = jnp.zeros_like(acc_ref)

        u = u_ref[...]
        hg = _dot(u, wg_ref[...])
        hu = _dot(u, wu_ref[...])
        hg_ref[...] = hg.astype(BF16)
        hu_ref[...] = hu.astype(BF16)
        act = (hg * jax.nn.sigmoid(hg) * hu).astype(BF16)
        acc_ref[...] += _dot(act, wd_ref[...])

        @pl.when(cc == c - 1)
        def _():
            y = acc_ref[...]
            y_ref[...] = y
            z = ALPHA * x_ref[...] + (1.0 + gt_ref[...]) * y
            xo_ref[...] = _layer_norm(z, g_ref[...], b_ref[...])[0]

    rows = pl.BlockSpec((tm, d), lambda i, cc: (i, 0))
    bvec = pl.BlockSpec((None, 1, d), lambda i, cc: (i // tps, 0, 0))
    vec = pl.BlockSpec((1, d), lambda i, cc: (0, 0))
    hspec = pl.BlockSpec((None, tm, fc), lambda i, cc: (cc, i, 0))
    wcol = pl.BlockSpec((None, d, fc), lambda i, cc: (cc, 0, 0))
    return _call(
        body, name, (t // tm, c),
        [rows, bvec, bvec, bvec, wcol, wcol, pl.BlockSpec((None, fc, d), lambda i, cc: (cc, 0, 0)), vec, vec],
        [rows, hspec, hspec, rows, rows],
        [_sds((t, d), BF16), _sds((c, t, fc), BF16), _sds((c, t, fc), BF16), _sds((t, d), F32), _sds((t, d), F32)],
        (x_in, shift, scale, gate, wg, wu, wd, ln_g, ln_b), scratch_shapes=[pltpu.VMEM((tm, d), F32)], hosted=hosted)


def fox_gate_fwd(hf, b_f, tri, n_batch, name):
    t, n = hf.shape
    blk = tri.shape[0]
    nb = (t // n_batch) // blk

    def body(hf_ref, b_ref, tri_ref, o_ref, carry_ref):
        @pl.when(pl.program_id(1) == 0)
        def _():
            carry_ref[...] = jnp.zeros_like(carry_ref)

        xx = hf_ref[...] + b_ref[...]
        lf = jnp.minimum(xx, 0.0) - jnp.log(1.0 + jnp.exp(-jnp.abs(xx)))
        cum = _dot_f32(tri_ref[...], lf) + carry_ref[...]
        o_ref[...] = cum
        carry_ref[...] = cum[blk - 1:blk, :]

    return pl.pallas_call(
        body, name=name, grid=(n_batch, nb),
        in_specs=[pl.BlockSpec((blk, n), lambda bb, i: (bb * nb + i, 0)), pl.BlockSpec((1, n), lambda bb, i: (0, 0)),
                  pl.BlockSpec((blk, blk), lambda bb, i: (0, 0))],
        out_specs=pl.BlockSpec((blk, n), lambda bb, i: (bb * nb + i, 0)),
        out_shape=_sds((t, n), F32), scratch_shapes=[pltpu.VMEM((1, n), F32)],
        compiler_params=_cp(2),
    )(hf, b_f, tri)


def loss_grad(x_out, target, name):
    t, d = x_out.shape
    tm = TOKEN_TILE

    def body(x_ref, t_ref, g_ref, l_ref):
        @pl.when(pl.program_id(0) == 0)
        def _():
            l_ref[...] = jnp.zeros_like(l_ref)

        err = x_ref[...] - t_ref[...]
        g_ref[...] = err / d
        l_ref[...] += jnp.sum(err * err, axis=0, keepdims=True)

    rows = pl.BlockSpec((tm, d), lambda i: (i, 0))
    return pl.pallas_call(
        body, name=name, grid=(t // tm,), in_specs=[rows, rows],
        out_specs=[rows, pl.BlockSpec((1, d), lambda i: (0, 0))],
        out_shape=[_sds((t, d), F32), _sds((1, d), F32)], compiler_params=_cp(1),
    )(x_out, target)


def ln_bwd(dxo, x_in, y, gate, ln_g, name):
    t, d = dxo.shape
    nb = gate.shape[0]
    tm = TOKEN_TILE
    tps = (t // nb) // tm

    def body(dxo_ref, x_ref, y_ref, gt_ref, g_ref, dz_ref, dy_ref, dg_ref, db_ref, dgt_ref):
        i = pl.program_id(0)

        @pl.when(i == 0)
        def _():
            dg_ref[...] = jnp.zeros_like(dg_ref)
            db_ref[...] = jnp.zeros_like(db_ref)

        @pl.when(i % tps == 0)
        def _():
            dgt_ref[...] = jnp.zeros_like(dgt_ref)

        yy = y_ref[...]
        g1 = 1.0 + gt_ref[...]
        z = ALPHA * x_ref[...] + g1 * yy
        _, xhat, rstd = _layer_norm(z, 1.0, 0.0)
        dxo_v = dxo_ref[...]
        dg_ref[...] += jnp.sum(dxo_v * xhat, axis=0, keepdims=True)
        db_ref[...] += jnp.sum(dxo_v, axis=0, keepdims=True)
        dxh = dxo_v * g_ref[...]
        dz = rstd * (dxh - jnp.mean(dxh, axis=-1, keepdims=True) - xhat * jnp.mean(dxh * xhat, axis=-1, keepdims=True))
        dz_ref[...] = dz
        dy_ref[...] = (g1 * dz).astype(BF16)
        dgt_ref[...] += jnp.sum(dz * yy, axis=0, keepdims=True)

    rows = pl.BlockSpec((tm, d), lambda i: (i, 0))
    vec = pl.BlockSpec((1, d), lambda i: (0, 0))
    bvec = pl.BlockSpec((None, 1, d), lambda i: (i // tps, 0, 0))
    return pl.pallas_call(
        body, name=name, grid=(t // tm,), in_specs=[rows, rows, rows, bvec, vec],
        out_specs=[rows, rows, vec, vec, bvec],
        out_shape=[_sds((t, d), F32), _sds((t, d), BF16), _sds((1, d), F32), _sds((1, d), F32), _sds((nb, 1, d), F32)],
        compiler_params=_cp(1),
    )(dxo, x_in, y, gate, ln_g)


def _mod_bwd_tail(du, dz_ref, x_ref, sc_ref, dx_ref, dsc_ref, dsh_ref, first):
    @pl.when(first)
    def _():
        dsc_ref[...] = jnp.zeros_like(dsc_ref)
        dsh_ref[...] = jnp.zeros_like(dsh_ref)

    dx_ref[...] = ALPHA * dz_ref[...] + du * (1.0 + sc_ref[...])
    dsc_ref[...] += jnp.sum(du * x_ref[...], axis=0, keepdims=True)
    dsh_ref[...] += jnp.sum(du, axis=0, keepdims=True)


def ffn_bwd(dy, hg, hu, wg, wu, wd, dz, x_in, scale, name, hosted=None):
    t, d = dy.shape
    c, _, fc = wg.shape
    nb = scale.shape[0]
    tm = TOKEN_TILE
    tps = (t // nb) // tm

    def body(dy_ref, hg_ref, hu_ref, wg_ref, wu_ref, wd_ref, dz_ref, x_ref, sc_ref,
             dhg_ref, dhu_ref, act_ref, dx_ref, dsc_ref, dsh_ref, acc_ref):
        i = pl.program_id(0)
        cc = pl.program_id(1)

        @pl.when(cc == 0)
        def _():
            acc_ref[...] = jnp.zeros_like(acc_ref)

        hgv = hg_ref[...].astype(F32)
        huv = hu_ref[...].astype(F32)
        da = _dot_nt(dy_ref[...], wd_ref[...])
        sg = jax.nn.sigmoid(hgv)
        sl = hgv * sg
        act_ref[...] = (sl * huv).astype(BF16)
        dhu = (da * sl).astype(BF16)
        dhg = (da * huv * (sg * (1.0 + hgv * (1.0 - sg)))).astype(BF16)
        dhu_ref[...] = dhu
        dhg_ref[...] = dhg
        acc_ref[...] += _dot_nt(dhg, wg_ref[...]) + _dot_nt(dhu, wu_ref[...])

        @pl.when(cc == c - 1)
        def _():
            _mod_bwd_tail(acc_ref[...], dz_ref, x_ref, sc_ref, dx_ref, dsc_ref, dsh_ref, i % tps == 0)

    rows = pl.BlockSpec((tm, d), lambda i, cc: (i, 0))
    bvec = pl.BlockSpec((None, 1, d), lambda i, cc: (i // tps, 0, 0))
    hspec = pl.BlockSpec((None, tm, fc), lambda i, cc: (cc, i, 0))
    wcol = pl.BlockSpec((None, d, fc), lambda i, cc: (cc, 0, 0))
    return _call(
        body, name, (t // tm, c),
        [rows, hspec, hspec, wcol, wcol, pl.BlockSpec((None, fc, d), lambda i, cc: (cc, 0, 0)), rows, rows, bvec],
        [hspec, hspec, hspec, rows, bvec, bvec],
        [_sds((c, t, fc), BF16), _sds((c, t, fc), BF16), _sds((c, t, fc), BF16), _sds((t, d), F32),
         _sds((nb, 1, d), F32), _sds((nb, 1, d), F32)],
        (dy, hg, hu, wg, wu, wd, dz, x_in, scale), scratch_shapes=[pltpu.VMEM((tm, d), F32)], hosted=hosted)


def linear_nt_mod_bwd(pairs, dz, x_in, scale, name):
    t, d = dz.shape
    nb = scale.shape[0]
    tm = TOKEN_TILE
    tps = (t // nb) // tm
    npairs = len(pairs)

    def body(*refs):
        dh_refs = refs[:npairs]
        w_refs = refs[npairs:2 * npairs]
        dz_ref, x_ref, sc_ref, dx_ref, dsc_ref, dsh_ref = refs[2 * npairs:]
        du = _dot_nt(dh_refs[0][...], w_refs[0][...])
        for kk in range(1, npairs):
            du = du + _dot_nt(dh_refs[kk][...], w_refs[kk][...])
        _mod_bwd_tail(du, dz_ref, x_ref, sc_ref, dx_ref, dsc_ref, dsh_ref, pl.program_id(0) % tps == 0)

    rows = pl.BlockSpec((tm, d), lambda i: (i, 0))
    bvec = pl.BlockSpec((None, 1, d), lambda i: (i // tps, 0, 0))
    in_specs = [pl.BlockSpec((tm, dh.shape[1]), lambda i: (i, 0)) for dh, _ in pairs]
    in_specs += [pl.BlockSpec(w.shape, lambda i: (0, 0)) for _, w in pairs]
    in_specs += [rows, rows, bvec]
    return pl.pallas_call(
        body, name=name, grid=(t // tm,), in_specs=in_specs,
        out_specs=[rows, bvec, bvec],
        out_shape=[_sds((t, d), F32), _sds((nb, 1, d), F32), _sds((nb, 1, d), F32)],
        compiler_params=_cp(1),
    )(*[dh for dh, _ in pairs], *[w for _, w in pairs], dz, x_in, scale)


def linear_nt_delta(dy, w_o, o, head_sel, name):
    t, d = dy.shape
    hdv = w_o.shape[0]
    tm = TOKEN_TILE

    def body(dy_ref, w_ref, o_ref, sel_ref, do_ref, dl_ref):
        do = _dot_nt(dy_ref[...], w_ref[...])
        do_ref[...] = do.astype(BF16)
        dl_ref[...] = _dot_f32(do * o_ref[...].astype(F32), sel_ref[...])

    return pl.pallas_call(
        body, name=name, grid=(t // tm,),
        in_specs=[pl.BlockSpec((tm, d), lambda i: (i, 0)), pl.BlockSpec((hdv, d), lambda i: (0, 0)),
                  pl.BlockSpec((tm, hdv), lambda i: (i, 0)), pl.BlockSpec(head_sel.shape, lambda i: (0, 0))],
        out_specs=[pl.BlockSpec((tm, hdv), lambda i: (i, 0)), pl.BlockSpec((tm, 128), lambda i: (i, 0))],
        out_shape=[_sds((t, hdv), BF16), _sds((t, 128), F32)], compiler_params=_cp(1),
    )(dy, w_o, o, head_sel)


def _attn_bwd_blocks(j, nk, tk, scale, heads):
    def block(i, carry, masked):
        new = []
        for hd, (dk_acc, dv_acc, dfk_acc) in zip(heads, carry):
            qb = hd["q"](i)
            dob = hd["do"](i)
            lse_row, dl_row = hd["rows"](i)
            st = _dot_nt(hd["k"], qb) * scale
            if hd["bias"] is not None:
                fq_row, fk_col = hd["bias"](i)
                st = st + fq_row - fk_col
            if masked:
                keep = lax.broadcasted_iota(jnp.int32, st.shape, 1) >= lax.broadcasted_iota(jnp.int32, st.shape, 0)
                st = jnp.where(keep, st, -1e30)
            pt = jnp.exp(st - lse_row)
            dv_acc = dv_acc + _dot(pt.astype(BF16), dob)
            dst = pt * (_dot_nt(hd["v"], dob) - dl_row)
            if hd["add_dfq"] is not None:
                dfk_acc = dfk_acc - jnp.sum(dst, axis=1, keepdims=True)
                hd["add_dfq"](i, jnp.sum(dst, axis=0, keepdims=True))
            dsb = (dst * scale).astype(BF16)
            dk_acc = dk_acc + _dot(dsb, qb)
            hd["add_dq"](i, _dot_tn(dsb, hd["k"]))
            new.append((dk_acc, dv_acc, dfk_acc))
        return tuple(new)

    init = tuple((jnp.zeros((tk, hd["k"].shape[1]), F32), jnp.zeros((tk, hd["v"].shape[1]), F32), jnp.zeros((tk, 1), F32))
                 for hd in heads)
    carry = block(j, init, True)
    return lax.fori_loop(j + 1, nk, lambda i, c: block(i, c, False), carry)


def fox_attn_bwd(qkv, do, cum, cum_rows, lse_rows, delta_rows, nb, name, hosted=None):
    t = qkv.shape[0]
    s = t // nb
    tk = ATTN_TILE
    nk = s // tk
    npairs = FOX_HEADS // 2
    scale = FOX_HD ** -0.5

    def body(q_ref, k_ref, v_ref, do_ref, cum_ref, cr_ref, lr_ref, dr_ref, dq_ref, dk_ref, dv_ref, dfq_ref, dfk_ref):
        hp = pl.program_id(1)
        j = pl.program_id(2)

        @pl.when(j == 0)
        def _():
            dq_ref[...] = jnp.zeros_like(dq_ref)

        @pl.when((j == 0) & (hp == 0))
        def _():
            dfq_ref[...] = jnp.zeros_like(dfq_ref)
            dfk_ref[...] = jnp.zeros_like(dfk_ref)

        kb = k_ref[...]
        vb = v_ref[...]
        low = lax.broadcasted_iota(jnp.int32, kb.shape, 1) < FOX_HD
        cum_t = cum_ref[...]

        def rows_of(i):
            return pl.ds(pl.multiple_of(i * tk, tk), tk)

        def add_dq(i, val):
            dq_ref[rows_of(i), :] += val

        def head(a):
            hd = 2 * hp + a
            half = low if a == 0 else jnp.logical_not(low)
            fk = _pick_lane(cum_t, hd)

            def add_dfq(i, val):
                dfq_ref[i] = _put_row(dfq_ref[i], hd, val)

            return dict(q=lambda i: q_ref[rows_of(i), :], do=lambda i: do_ref[rows_of(i), :],
                        k=jnp.where(half, kb, jnp.zeros_like(kb)), v=jnp.where(half, vb, jnp.zeros_like(vb)),
                        rows=lambda i: (_pick_row(lr_ref[i], hd), _pick_row(dr_ref[i], hd)),
                        bias=lambda i: (_pick_row(cr_ref[i], hd), fk), add_dq=add_dq, add_dfq=add_dfq)

        (dk_0, dv_0, dfk_0), (dk_1, dv_1, dfk_1) = _attn_bwd_blocks(j, nk, tk, scale, [head(0), head(1)])
        dk_ref[...] = jnp.where(low, dk_0, dk_1).astype(BF16)
        dv_ref[...] = jnp.where(low, dv_0, dv_1).astype(BF16)
        for a, dfk_a in ((0, dfk_0), (1, dfk_1)):
            dfk_ref[j] = _put_row(dfk_ref[j], 2 * hp + a, jnp.broadcast_to(dfk_a, (tk, 128)).T[0:1, :])

    rowsp = pl.BlockSpec((nk, 16, tk), lambda b, hp, j: (b, 0, 0))
    return _call(
        body, name, (nb, npairs, nk),
        [pl.BlockSpec((s, 128), lambda b, hp, j: (b, hp)),
         pl.BlockSpec((tk, 128), lambda b, hp, j: (b * nk + j, npairs + hp)),
         pl.BlockSpec((tk, 128), lambda b, hp, j: (b * nk + j, 2 * npairs + hp)),
         pl.BlockSpec((s, 128), lambda b, hp, j: (b, hp)),
         pl.BlockSpec((tk, 128), lambda b, hp, j: (b * nk + j, 0)),
         rowsp, rowsp, rowsp],
        [pl.BlockSpec((s, 128), lambda b, hp, j: (b, hp)),
         pl.BlockSpec((tk, 128), lambda b, hp, j: (b * nk + j, hp)),
         pl.BlockSpec((tk, 128), lambda b, hp, j: (b * nk + j, hp)),
         rowsp, rowsp],
        [_sds((t, D_MODEL), F32), _sds((t, D_MODEL), BF16), _sds((t, D_MODEL), BF16),
         _sds((t // tk, 16, tk), F32), _sds((t // tk, 16, tk), F32)],
        (qkv, qkv, qkv, do, cum, cum_rows, lse_rows, delta_rows), hosted=hosted)


def mla_attn_bwd(q, kn, kr2, v, do, lse_rows, delta_rows, nb, name, hosted=None):
    t = q.shape[0]
    s = t // nb
    tk = ATTN_TILE
    nk = s // tk
    npairs = MLA_HEADS // 2
    scale = (MLA_NOPE + MLA_ROPE) ** -0.5

    def body(qn_ref, qr_ref, kn_ref, kr_ref, v_ref, do_ref, lr_ref, dr_ref, dqn_ref, dqr_ref, dkn_ref, dkr_ref, dv_ref):
        hp = pl.program_id(1)
        j = pl.program_id(2)

        @pl.when(j == 0)
        def _():
            dqn_ref[...] = jnp.zeros_like(dqn_ref)
            dqr_ref[...] = jnp.zeros_like(dqr_ref)

        low = lax.broadcasted_iota(jnp.int32, (tk, 128), 1) < MLA_ROPE
        kr = kr_ref[...]

        def rows_of(i):
            return pl.ds(pl.multiple_of(i * tk, tk), tk)

        def head(a):
            cols = slice(a * MLA_NOPE, (a + 1) * MLA_NOPE)
            mine = low if a == 0 else jnp.logical_not(low)

            def q_fn(i):
                qr = qr_ref[rows_of(i), :]
                return jnp.concatenate([qn_ref[rows_of(i), cols], jnp.where(mine, qr, jnp.zeros_like(qr))], axis=1)

            def add_dq(i, val):
                dqn_ref[rows_of(i), cols] += val[:, :MLA_NOPE]
                dqr_ref[rows_of(i), cols] += val[:, MLA_NOPE:]

            return dict(q=q_fn, do=lambda i: do_ref[rows_of(i), cols], k=jnp.concatenate([kn_ref[:, cols], kr], axis=1),
                        v=v_ref[:, cols], rows=lambda i: (_pick_row(lr_ref[i], 2 * hp + a), _pick_row(dr_ref[i], 2 * hp + a)),
                        bias=None, add_dq=add_dq, add_dfq=None)

        (dk_0, dv_0, _), (dk_1, dv_1, _) = _attn_bwd_blocks(j, nk, tk, scale, [head(0), head(1)])
        dkn_ref[...] = jnp.concatenate([dk_0[:, :MLA_NOPE], dk_1[:, :MLA_NOPE]], axis=1).astype(BF16)
        dkr_ref[...] = jnp.concatenate([dk_0[:, MLA_NOPE:], dk_1[:, MLA_NOPE:]], axis=1).astype(BF16)
        dv_ref[...] = jnp.concatenate([dv_0, dv_1], axis=1).astype(BF16)

    wide = 2 * MLA_NOPE
    full = pl.BlockSpec((s, wide), lambda b, hp, j: (b, hp))
    blk = pl.BlockSpec((tk, wide), lambda b, hp, j: (b * nk + j, hp))
    rowsp = pl.BlockSpec((nk, 16, tk), lambda b, hp, j: (b, 0, 0))
    total = MLA_HEADS * MLA_V
    return _call(
        body, name, (nb, npairs, nk),
        [full, pl.BlockSpec((s, 128), lambda b, hp, j: (b, MLA_HEADS + hp)), blk,
         pl.BlockSpec((tk, 128), lambda b, hp, j: (b * nk + j, 0)), blk, full, rowsp, rowsp],
        [full, full, blk, blk, blk],
        [_sds((t, total), F32), _sds((t, total), F32), _sds((t, total), BF16), _sds((t, total), BF16),
         _sds((t, total), BF16)],
        (q, q, kn, kr2, v, do, lse_rows, delta_rows), hosted=hosted)


def mla_mid_bwd(dqn, dqr, dkn, dv, dkr_heads, h, g_q, g_kv, w_uq, w_uk, w_uv, cos8, sin8, cos64, sin64s, swap64,
                heads_to_rope, head_sum, name):
    t = h.shape[0]
    tm = TOKEN_TILE
    hq = MLA_HEADS * MLA_NOPE
    hr = MLA_HEADS * MLA_ROPE // 2
    nq = w_uq.shape[1]

    def body(dqn_ref, dqr_ref, dkn_ref, dv_ref, dkr_ref, h_ref, gq_ref, gkv_ref, wuq_ref, wuk_ref, wuv_ref,
             c8_ref, s8_ref, c64_ref, s64_ref, sw_ref, hp_ref, hs_ref, dh_ref, dqp_ref, dgq_ref, dgkv_ref):
        @pl.when(pl.program_id(0) == 0)
        def _():
            dgq_ref[...] = jnp.zeros_like(dgq_ref)
            dgkv_ref[...] = jnp.zeros_like(dgkv_ref)

        drot = _dot(dqr_ref[...].astype(BF16), hp_ref[...])
        o1 = drot[:, :hr]
        o2 = drot[:, hr:]
        cs = c8_ref[...]
        sn = s8_ref[...]
        dqp = jnp.concatenate([dqn_ref[...].astype(BF16), (o1 * cs + o2 * sn).astype(BF16),
                               (o2 * cs - o1 * sn).astype(BF16)], axis=1)
        dqp_ref[...] = dqp
        dcq = _dot_nt(dqp, wuq_ref[...])
        dckv = _dot_nt(dkn_ref[...], wuk_ref[...]) + _dot_nt(dv_ref[...], wuv_ref[...])
        hh = h_ref[...]

        def rms_bwd(hpart, g, dc, dg_ref):
            hhat, rstd = _rms(hpart, None)
            dg_ref[...] += jnp.sum(dc * hhat, axis=0, keepdims=True)
            dcg = dc * g
            return rstd * (dcg - hhat * jnp.mean(dcg * hhat, axis=-1, keepdims=True))

        dhq = rms_bwd(hh[:, :MLA_QR], gq_ref[...], dcq, dgq_ref)
        dhkv = rms_bwd(hh[:, MLA_QR:MLA_QR + MLA_KVR], gkv_ref[...], dckv, dgkv_ref)
        dkr = _dot(dkr_ref[...], hs_ref[...])
        dkr_pre = dkr * c64_ref[...] + _dot_f32(dkr * s64_ref[...], sw_ref[...])
        dh_ref[...] = jnp.concatenate([dhq, dhkv, dkr_pre], axis=1).astype(BF16)

    def rows(n):
        return pl.BlockSpec((tm, n), lambda i: (i, 0))

    def whole(a):
        return pl.BlockSpec(a.shape, lambda i: (0,) * a.ndim)

    return pl.pallas_call(
        body, name=name, grid=(t // tm,),
        in_specs=[rows(hq), rows(hq), rows(hq), rows(hq), rows(hq), rows(h.shape[1]), whole(g_q), whole(g_kv),
                  whole(w_uq), whole(w_uk), whole(w_uv), rows(hr), rows(hr), rows(MLA_ROPE), rows(MLA_ROPE),
                  whole(swap64), whole(heads_to_rope), whole(head_sum)],
        out_specs=[rows(h.shape[1]), rows(nq), pl.BlockSpec((1, MLA_QR), lambda i: (0, 0)),
                   pl.BlockSpec((1, MLA_KVR), lambda i: (0, 0))],
        out_shape=[_sds((t, h.shape[1]), BF16), _sds((t, nq), BF16), _sds((1, MLA_QR), F32), _sds((1, MLA_KVR), F32)],
        compiler_params=_cp(1),
    )(dqn, dqr, dkn, dv, dkr_heads, h, g_q, g_kv, w_uq, w_uk, w_uv, cos8, sin8, cos64, sin64s, swap64,
      heads_to_rope, head_sum)


def fox_gate_bwd(dcum, hf, b_f, triu, n_batch, name):
    t, n = hf.shape
    blk = triu.shape[0]
    nb = (t // n_batch) // blk

    def body(dc_ref, hf_ref, b_ref, tri_ref, o_ref, db_ref, carry_ref):
        @pl.when(pl.program_id(1) == 0)
        def _():
            carry_ref[...] = jnp.zeros_like(carry_ref)

        @pl.when((pl.program_id(0) == 0) & (pl.program_id(1) == 0))
        def _():
            db_ref[...] = jnp.zeros_like(db_ref)

        rc = _dot_f32(tri_ref[...], dc_ref[...]) + carry_ref[...]
        carry_ref[...] = rc[0:1, :]
        dhf = rc * jax.nn.sigmoid(-(hf_ref[...] + b_ref[...]))
        o_ref[...] = dhf.astype(BF16)
        db_ref[...] += jnp.sum(dhf, axis=0, keepdims=True)

    rev = pl.BlockSpec((blk, n), lambda bb, i: (bb * nb + nb - 1 - i, 0))
    return pl.pallas_call(
        body, name=name, grid=(n_batch, nb),
        in_specs=[rev, rev, pl.BlockSpec((1, n), lambda bb, i: (0, 0)), pl.BlockSpec((blk, blk), lambda bb, i: (0, 0))],
        out_specs=[rev, pl.BlockSpec((1, n), lambda bb, i: (0, 0))],
        out_shape=[_sds((t, n), BF16), _sds((1, n), F32)], scratch_shapes=[pltpu.VMEM((1, n), F32)],
        compiler_params=_cp(2),
    )(dcum, hf, b_f, triu)


def wgrad(a, bm, name, with_bf16=False, bt=512):
    ca, t, kd = a.shape
    cb, _, nd = bm.shape
    c = max(ca, cb)
    bn = nd
    if nd > 1024 and nd % 1024 == 0:
        bn = 1024
    nsteps = t // bt

    def body(a_ref, b_ref, o_ref, *rest):
        @pl.when(pl.program_id(2) == 0)
        def _():
            o_ref[...] = jnp.zeros_like(o_ref)

        o_ref[...] += _dot_tn(a_ref[...].astype(BF16), b_ref[...].astype(BF16))
        if with_bf16:
            @pl.when(pl.program_id(2) == nsteps - 1)
            def _():
                rest[0][...] = o_ref[...].astype(BF16)

    out_spec = pl.BlockSpec((None, kd, bn), lambda cc, n, tt: (cc, 0, n))
    res = pl.pallas_call(
        body, name=name, grid=(c, nd // bn, nsteps),
        in_specs=[pl.BlockSpec((None, bt, kd), lambda cc, n, tt: (cc if ca > 1 else 0, tt, 0)),
                  pl.BlockSpec((None, bt, bn), lambda cc, n, tt: (cc if cb > 1 else 0, tt, n))],
        out_specs=[out_spec, out_spec] if with_bf16 else out_spec,
        out_shape=[_sds((c, kd, nd), F32), _sds((c, kd, nd), BF16)] if with_bf16 else _sds((c, kd, nd), F32),
        compiler_params=_cp(3),
    )(a, bm)
    return res


def ada_mod_part(c_all, ada_w, name):
    nl, d, n = ada_w.shape
    rows = c_all.shape[0]
    tn = 512

    def body(c_ref, w_ref, o_ref):
        cv = c_ref[...]
        act = (cv * jax.nn.sigmoid(cv)).astype(BF16)
        o_ref[...] = _dot(act, w_ref[...].astype(BF16))

    return pl.pallas_call(
        body, name=name, grid=(nl, n // tn),
        in_specs=[pl.BlockSpec((rows, d), lambda l, j: (0, 0)), pl.BlockSpec((None, d, tn), lambda l, j: (l, 0, j))],
        out_specs=pl.BlockSpec((None, rows, tn), lambda l, j: (l, 0, j)),
        out_shape=_sds((nl, rows, n), F32), compiler_params=_cp(2),
    )(c_all, ada_w)


def ada_grad(c_all_t, dmod, name):
    nl, rows, n = dmod.shape
    d = c_all_t.shape[0]
    tn = 512

    def body(c_ref, dm_ref, o_ref):
        cv = c_ref[...]
        act = (cv * jax.nn.sigmoid(cv)).astype(BF16)
        o_ref[...] = _dot(act, dm_ref[...].astype(BF16))

    return pl.pallas_call(
        body, name=name, grid=(nl, n // tn),
        in_specs=[pl.BlockSpec((d, rows), lambda l, j: (0, 0)), pl.BlockSpec((None, rows, tn), lambda l, j: (l, 0, j))],
        out_specs=pl.BlockSpec((None, d, tn), lambda l, j: (l, 0, j)),
        out_shape=_sds((nl, d, n), F32), compiler_params=_cp(2),
    )(c_all_t, dmod)


def sum_leading(a, name):
    g, r, n = a.shape

    def body(a_ref, o_ref):
        acc = a_ref[0]
        for kk in range(1, g):
            acc = acc + a_ref[kk]
        o_ref[...] = acc

    return pl.pallas_call(
        body, name=name, grid=(1,), in_specs=[pl.BlockSpec((g, r, n), lambda i: (0, 0, 0))],
        out_specs=pl.BlockSpec((r, n), lambda i: (0, 0)), out_shape=_sds((r, n), F32), compiler_params=_cp(1),
    )(a)


def adamw(w, g, m, v, name):
    r, n = w.shape
    br = r
    for cand in (512, 256, 128, 64, 32, 16, 8):
        if r % cand == 0 and r > cand and cand * n * 4 <= ADAMW_BLOCK_BYTES:
            br = cand
            break
    c1 = 1.0 - ADAM_B1 ** ADAM_STEP
    c2 = 1.0 - ADAM_B2 ** ADAM_STEP

    def body(w_ref, g_ref, m_ref, v_ref, d_ref, mo_ref, vo_ref):
        gv = g_ref[...]
        mn = ADAM_B1 * m_ref[...] + (1.0 - ADAM_B1) * gv
        vn = ADAM_B2 * v_ref[...] + (1.0 - ADAM_B2) * (gv * gv)
        mo_ref[...] = mn
        vo_ref[...] = vn
        d_ref[...] = -ADAM_LR * ((mn / c1) / (jnp.sqrt(vn / c2) + ADAM_EPS) + ADAM_WD * w_ref[...])

    spec = pl.BlockSpec((br, n), lambda i: (i, 0))
    return pl.pallas_call(
        body, name=name, grid=(r // br,), in_specs=[spec] * 4, out_specs=[spec] * 3,
        out_shape=[_sds((r, n), F32)] * 3, compiler_params=_cp(1),
    )(w, g, m, v)


def all_gather8(x_blk, name):
    m_per, n = x_blk.shape

    def body(x_ref, out_ref, send_sems, recv_sems, local_sem):
        x, y, c = _place()
        me, sibling = (x, y, c), (x, y, 1 - c)
        chips = [(1 - x, y), (x, 1 - y), (1 - x, 1 - y)]

        def rows(px, py, pc):
            return out_ref.at[pl.ds((4 * px + 2 * py + pc) * m_per, m_per), :]

        def copy(k, block, to, src=None):
            return pltpu.make_async_remote_copy(
                src_ref=rows(*block) if src is None else src, dst_ref=rows(*block),
                send_sem=send_sems.at[k], recv_sem=recv_sems.at[k], device_id=to, device_id_type=MESH)

        mine = pltpu.make_async_copy(x_ref, rows(*me), local_sem)
        mine.start()
        first = [copy(0, me, sibling, src=x_ref)]
        first += [copy(1 + j, me, (*chip, c), src=x_ref) for j, chip in enumerate(chips)]
        for cp in first:
            cp.start()
        passed = [copy(4 + j, (*chip, c), sibling) for j, chip in enumerate(chips)]
        for j, chip in enumerate(chips):
            copy(1 + j, (*chip, c), me).wait_recv()
            passed[j].start()
        copy(0, sibling, me).wait_recv()
        for j, chip in enumerate(chips):
            copy(4 + j, (*chip, 1 - c), me).wait_recv()
        for cp in first + passed:
            cp.wait_send()
        mine.wait()

    return pl.pallas_call(
        body, name=name, out_shape=_sds((8 * m_per, n), x_blk.dtype),
        in_specs=[pl.BlockSpec(memory_space=pltpu.VMEM)], out_specs=pl.BlockSpec(memory_space=pltpu.VMEM),
        scratch_shapes=[pltpu.SemaphoreType.DMA((7,)), pltpu.SemaphoreType.DMA((7,)), pltpu.SemaphoreType.DMA],
        compiler_params=pltpu.CompilerParams(vmem_limit_bytes=VMEM_LIMIT),
    )(x_blk)


def _gather_comm(shards):
    nt = len(shards)

    def parts(w_refs, out_refs, sems, finishing):
        send_sems, recv_sems, own_send, own_recv = sems
        x, y, c = _place()
        sibling = (x, y, 1 - c)
        chips = [(1 - x, y), (x, 1 - y), (1 - x, 1 - y)]

        def copy(t, k, block, to, src=None):
            px, py, hh = block
            dst = out_refs[t].at[2 * px + py, hh]
            return pltpu.make_async_remote_copy(
                src_ref=dst if src is None else src, dst_ref=dst,
                send_sem=send_sems.at[6 * t + k], recv_sem=recv_sems.at[6 * t + k], device_id=to, device_id_type=MESH)

        own = [pltpu.make_async_remote_copy(
            src_ref=w_refs[t], dst_ref=out_refs[t].at[2 * x + y], send_sem=own_send.at[t], recv_sem=own_recv.at[t],
            device_id=sibling, device_id_type=MESH) for t in range(nt)]
        first = [copy(t, j, (x, y, c), (*chip, c), src=w_refs[t].at[c]) for t in range(nt) for j, chip in enumerate(chips)]
        if not finishing:
            return own, first
        landed = [copy(t, j, (*chip, c), (x, y, c)) for t in range(nt) for j, chip in enumerate(chips)]
        passed = [copy(t, 3 + j, (*chip, c), sibling) for t in range(nt) for j, chip in enumerate(chips)]
        from_sibling = [copy(t, 3 + j, (*chip, 1 - c), (x, y, c)) for t in range(nt) for j, chip in enumerate(chips)]
        return own, first, landed, passed, from_sibling

    def start(w_refs, out_refs, sems):
        own, first = parts(w_refs, out_refs, sems, False)
        for cp in own + first:
            cp.start()

    def finish(w_refs, out_refs, sems):
        own, first, landed, passed, from_sibling = parts(w_refs, out_refs, sems, True)
        for arrived, fwd in zip(landed, passed):
            arrived.wait_recv()
            fwd.start()
        for cp in from_sibling:
            cp.wait_recv()
        for cp in first + passed:
            cp.wait_send()
        for cp in own:
            cp.wait()

    sems = [pltpu.SemaphoreType.DMA((6 * nt,)), pltpu.SemaphoreType.DMA((6 * nt,)),
            pltpu.SemaphoreType.DMA((nt,)), pltpu.SemaphoreType.DMA((nt,))]
    return _Hosted(list(shards), [_sds((N_CHIPS, *w.shape), w.dtype) for w in shards], sems, start, finish)


def all_gather_chips(shards, name):
    comm = _gather_comm(shards)
    nt = len(shards)

    def body(*refs):
        comm.start(refs[:nt], refs[nt:2 * nt], refs[2 * nt:])
        comm.finish(refs[:nt], refs[nt:2 * nt], refs[2 * nt:])

    hbm = pl.BlockSpec(memory_space=pl.ANY)
    return pl.pallas_call(body, name=name, out_shape=comm.out_shape, in_specs=[hbm] * nt, out_specs=[hbm] * nt,
                          scratch_shapes=comm.sems)(*shards)


def _row_block(r, n, itemsize):
    best = None
    for br in range(16, r + 1, 16):
        if r % br == 0 and br * n * itemsize <= COMM_BLOCK_BYTES:
            best = br
    assert best is not None, (r, n)
    return best


def _scatter_comm(parts):
    nt = len(parts)

    def copies(p_refs, b_refs, sems, arriving):
        send_sems, recv_sems = sems
        x, y, c = _place()
        me = 4 * x + 2 * y + c
        cps = []
        for t in range(nt):
            for r in range(1, 8):
                tx = 1 - x if r & 4 else x
                ty = 1 - y if r & 2 else y
                tc = 1 - c if r & 1 else c
                src, dst = (2 * x + y, c), 4 * tx + 2 * ty + tc
                if not arriving:
                    src, dst = (2 * tx + ty, tc), me
                cps.append(pltpu.make_async_remote_copy(
                    src_ref=p_refs[t].at[src], dst_ref=b_refs[t].at[dst], send_sem=send_sems.at[7 * t + r - 1],
                    recv_sem=recv_sems.at[7 * t + r - 1], device_id=(tx, ty, tc), device_id_type=MESH))
        return cps

    def start(p_refs, b_refs, sems):
        for cp in copies(p_refs, b_refs, sems, False):
            cp.start()

    def finish(p_refs, b_refs, sems):
        for cp in copies(p_refs, b_refs, sems, True):
            cp.wait_recv()
        for cp in copies(p_refs, b_refs, sems, False):
            cp.wait_send()

    sems = [pltpu.SemaphoreType.DMA((7 * nt,)), pltpu.SemaphoreType.DMA((7 * nt,))]
    return _Hosted(list(parts), [_sds((2 * N_CHIPS, *p.shape[2:]), p.dtype) for p in parts], sems, start, finish)


def scatter_exchange(parts, name):
    comm = _scatter_comm(parts)
    nt = len(parts)

    def body(*refs):
        comm.start(refs[:nt], refs[nt:2 * nt], refs[2 * nt:])
        comm.finish(refs[:nt], refs[nt:2 * nt], refs[2 * nt:])

    hbm = pl.BlockSpec(memory_space=pl.ANY)
    return pl.pallas_call(body, name=name, out_shape=comm.out_shape, in_specs=[hbm] * nt, out_specs=[hbm] * nt,
                          scratch_shapes=comm.sems)(*parts)


def sum_devices(own, recv, place, name, slot=(0, 1, None)):
    _, _, r, n = own.shape
    layer, n_layers, buf = slot
    br = _row_block(r, n, 4 * 8)

    def body(p_ref, o_ref, *rest):
        acc = o_ref[...]
        for kk in range(7):
            acc = acc + rest[kk][...].astype(F32)
        rest[-1][...] = acc

    def arrived(rel):
        return pl.BlockSpec((None, br, n), lambda i, pref: (jnp.bitwise_xor(pref[0], rel), i, 0))

    in_specs = [pl.BlockSpec((None, None, br, n), lambda i, pref: (pref[2], pref[1], i, 0))]
    in_specs += [arrived(rel) for rel in range(1, 8)]
    args = [own] + [recv] * 7
    aliases = {}
    if buf is not None:
        in_specs.append(pl.BlockSpec(memory_space=pl.ANY))
        args.append(buf)
        aliases = {9: 0}
    return pl.pallas_call(
        body, name=name,
        grid_spec=pltpu.PrefetchScalarGridSpec(
            num_scalar_prefetch=1, grid=(r // br,), in_specs=in_specs,
            out_specs=pl.BlockSpec((None, None, br, n), lambda i, pref: (layer, pref[1], i, 0))),
        out_shape=_sds((n_layers, 2, r, n), F32), input_output_aliases=aliases, compiler_params=_cp(1),
    )(place, *args)


def sibling_join_halves(bufs, name):
    nt = len(bufs)
    layers = [bf.shape[0] for bf in bufs]
    first = [sum(layers[:t]) for t in range(nt)]

    def body(*refs):
        o_refs = refs[nt:2 * nt]
        send_sems, recv_sems = refs[2 * nt:]
        x, y, c = _place()

        def copy(t, l, hh):
            return pltpu.make_async_remote_copy(
                src_ref=o_refs[t].at[l, hh], dst_ref=o_refs[t].at[l, hh], send_sem=send_sems.at[first[t] + l],
                recv_sem=recv_sems.at[first[t] + l], device_id=(x, y, 1 - c), device_id_type=MESH)

        cps = [copy(t, l, c) for t in range(nt) for l in range(layers[t])]
        for cp in cps:
            cp.start()
        for t in range(nt):
            for l in range(layers[t]):
                copy(t, l, 1 - c).wait_recv()
        for cp in cps:
            cp.wait_send()

    hbm = pl.BlockSpec(memory_space=pl.ANY)
    return pl.pallas_call(
        body, name=name, out_shape=[_sds(bf.shape, bf.dtype) for bf in bufs],
        in_specs=[hbm] * nt, out_specs=[hbm] * nt, input_output_aliases={t: t for t in range(nt)},
        scratch_shapes=[pltpu.SemaphoreType.DMA((sum(layers),)), pltpu.SemaphoreType.DMA((sum(layers),))],
    )(*bufs)


_SHARD_KIND = {"mla_w_in": "rows", "mla_w_uq": "cols", "mla_w_uk": "cols", "mla_w_uv": "cols", "mla_w_o": "rows",
               "fox_w_in": "cols", "fox_w_o": "rows", "ffn_w_gate": "chunk", "ffn_w_up": "chunk", "ffn_w_down": "chunk"}
_PACKED = tuple(_SHARD_KIND)


def _halves(shard):
    if shard.ndim == 3 and shard.shape[0] == 2:
        return shard
    r, n = shard.shape[-2:]
    return shard.reshape(2, r // 2, n)


def _cols_to_full(g):
    return jnp.transpose(g, (1, 0, 2)).reshape(g.shape[1], -1)


def _full_to_cols(w):
    k, n4 = w.shape
    return jnp.transpose(w.reshape(k, N_CHIPS, n4 // N_CHIPS), (1, 0, 2))


def _uq_perm():
    per = MLA_NOPE + MLA_ROPE
    half = MLA_ROPE // 2
    nope = [h * per + d for h in range(MLA_HEADS) for d in range(MLA_NOPE)]
    r1 = [h * per + MLA_NOPE + r for h in range(MLA_HEADS) for r in range(half)]
    r2 = [h * per + MLA_NOPE + half + r for h in range(MLA_HEADS) for r in range(half)]
    perm = np.array(nope + r1 + r2, dtype=np.int32)
    return perm, np.argsort(perm).astype(np.int32)


def _rope_matrices():
    half = MLA_ROPE // 2
    nr = MLA_HEADS * MLA_ROPE
    to_heads = np.zeros((nr, nr), np.float32)
    from_heads = np.zeros((MLA_HEADS * 128, nr), np.float32)
    for e in range(2):
        for h in range(MLA_HEADS):
            for r in range(half):
                to_heads[e * MLA_HEADS * half + h * half + r, h * MLA_ROPE + e * half + r] = 1.0
                from_heads[h * 128 + e * half + r, e * MLA_HEADS * half + h * half + r] = 1.0
    head_sum = np.tile(np.eye(MLA_ROPE, dtype=np.float32), (2 * MLA_HEADS, 1))
    dup = np.concatenate([np.eye(MLA_ROPE, dtype=np.float32)] * 2, axis=1)
    return to_heads, from_heads, head_sum, dup


def _ffn_weights(gathered):
    return tuple(g.reshape(N_CHIPS, 2 * g.shape[2], g.shape[3]) for g in gathered)


def _fox_weights(gathered):
    w_in, w_o = gathered
    w_in = _cols_to_full(w_in.reshape(N_CHIPS, 2 * w_in.shape[2], w_in.shape[3]))
    return w_in, w_o.reshape(-1, w_o.shape[-1])


def _local_step(x, positions, target, mods, wts, ln_g, ln_b, mla_g_q, mla_g_kv, fox_b_f, shards=None):
    nb, s, d = x.shape
    t = nb * s
    x0 = x.reshape(t, d)
    tgt = target.reshape(t, d)
    perm, inv_perm = _uq_perm()

    half = MLA_ROPE // 2
    inv_freq = ROPE_THETA ** (-jnp.arange(half, dtype=F32) / half)
    ang = positions.astype(F32).reshape(t, 1) * inv_freq
    cos, sin = jnp.cos(ang), jnp.sin(ang)
    cos8, sin8 = jnp.tile(cos, (1, MLA_HEADS)), jnp.tile(sin, (1, MLA_HEADS))
    cos64 = jnp.concatenate([cos, cos], axis=1)
    sin64s = jnp.concatenate([-sin, sin], axis=1)
    swap64 = jnp.asarray(np.roll(np.eye(MLA_ROPE, dtype=np.float32), half, axis=1))
    to_heads, from_heads, head_sum, dup = _rope_matrices()
    to_heads, from_heads = jnp.asarray(to_heads, dtype=BF16), jnp.asarray(from_heads, dtype=BF16)
    head_sum, dup = jnp.asarray(head_sum, dtype=BF16), jnp.asarray(dup, dtype=BF16)
    sel_mla = jnp.asarray(np.pad(np.kron(np.eye(MLA_HEADS, dtype=np.float32), np.ones((MLA_V, 1), np.float32)),
                                 ((0, 0), (0, 128 - MLA_HEADS))))
    sel_fox = jnp.asarray(np.pad(np.kron(np.eye(FOX_HEADS, dtype=np.float32), np.ones((FOX_HD, 1), np.float32)),
                                 ((0, 0), (0, 128 - FOX_HEADS))))
    tri = jnp.asarray(np.tril(np.ones((128, 128), np.float32)))
    triu = jnp.asarray(np.triu(np.ones((128, 128), np.float32)))
    onehot16 = jnp.asarray(np.eye(16, 128, dtype=np.float32))

    def vec(a):
        return a.reshape(1, -1)

    def carried(key):
        return None if shards is None else _gather_comm(shards[key])

    def split(res):
        return (res, None) if shards is None else res

    w_uq_p = wts["mla_w_uq"][:, perm]
    b_f_pad = jnp.pad(fox_b_f.reshape(1, -1), ((0, 0), (0, 128 - FOX_HEADS)))

    sh_a, sc_a, gt_a, sh_f, sc_f, gt_f = mods[0]
    h_in, u_m = mod_linear(x0, sh_a, sc_a, wts["mla_w_in"], F32, "mla_in", emit_u=True)
    q_m, kn_m, v_m, kr2_m, cq_m, ckv_m = mla_mid_fwd(
        h_in, vec(mla_g_q), vec(mla_g_kv), w_uq_p, wts["mla_w_uk"], wts["mla_w_uv"], cos8, sin8, cos64, sin64s, swap64,
        to_heads, dup, "mla_mid")
    (o_m, lse_m), got = split(mla_attn_fwd(q_m, kn_m, kr2_m, v_m, nb, "mla_attn", hosted=carried("ffn0")))
    ffn0_w = wts["ffn"][0] if got is None else _ffn_weights(got)
    y0, x1 = linear_resid_ln(o_m, wts["mla_w_o"], x0, gt_a, vec(ln_g[0, 0]), vec(ln_b[0, 0]), "mla_out")
    (u_f0, hg0, hu0, y1, x2), got = split(ffn_fwd(x1, sh_f, sc_f, gt_f, *ffn0_w, vec(ln_g[0, 1]), vec(ln_b[0, 1]), "ffn0",
                                                  hosted=carried("fox")))
    fox_w_in, fox_w_o = (wts["fox_w_in"], wts["fox_w_o"]) if got is None else _fox_weights(got)
    fox_w_qkv = fox_w_in[:, :3 * d]
    fox_w_f = jnp.pad(fox_w_in[:, 3 * d:], ((0, 0), (0, 128 - FOX_HEADS)))
    sh_a1, sc_a1, gt_a1, sh_f1, sc_f1, gt_f1 = mods[1]
    qkv, u_x = mod_linear(x2, sh_a1, sc_a1, fox_w_qkv, BF16, "fox_qkv", tn=1024, emit_u=True)
    hf = mod_linear(x2, sh_a1, sc_a1, fox_w_f, F32, "fox_f")
    cum = fox_gate_fwd(hf, b_f_pad, tri, nb, "fox_gate")
    cum_rows = rows16(cum, "fox_cum_rows")
    (o_x, lse_x), got = split(fox_attn_fwd(qkv, cum, cum_rows, nb, "fox_attn", hosted=carried("ffn1")))
    ffn1_w = wts["ffn"][1] if got is None else _ffn_weights(got)
    y2, x3 = linear_resid_ln(o_x, fox_w_o, x2, gt_a1, vec(ln_g[1, 0]), vec(ln_b[1, 0]), "fox_out")
    u_f1, hg1, hu1, y3, x4 = ffn_fwd(x3, sh_f1, sc_f1, gt_f1, *ffn1_w, vec(ln_g[1, 1]), vec(ln_b[1, 1]), "ffn1")
    dx4, sq_err = loss_grad(x4, tgt, "loss")
    loss_part = 0.5 * jnp.sum(sq_err) / d

    parts, recv = {}, {}

    def halves_of(g):
        return g.reshape(N_CHIPS, 2, g.shape[1] // 2, g.shape[2])

    def scatter(keys, sent):
        return None if shards is None else _scatter_comm([sent[k] for k in keys])

    def landed(keys, got):
        if got is not None:
            recv.update(zip(keys, got))

    def ffn_grads(layer, u, dhg, dhu, act, dy):
        sent = {}
        for n, (a_op, b_op) in (("ffn_w_gate", (u[None], dhg)), ("ffn_w_up", (u[None], dhu)), ("ffn_w_down", (act, dy[None]))):
            g32, g16 = wgrad(a_op, b_op, "ffn%d_d%s" % (layer, n[4:]), with_bf16=True)
            parts["%s/%d" % (n, layer)], sent["%s/%d" % (n, layer)] = halves_of(g32), halves_of(g16)
        return sent

    dz3, dy3, dg11, db11, dgt_f1 = ln_bwd(dx4, x3, y3, gt_f1, vec(ln_g[1, 1]), "ffn1_ln_bwd")
    dhg1, dhu1, act1, dx3, dsc_f1, dsh_f1 = ffn_bwd(dy3, hg1, hu1, *ffn1_w, dz3, x3, sc_f1, "ffn1_bwd")
    sent = ffn_grads(1, u_f1, dhg1, dhu1, act1, dy3)
    dz2, dy2, dg10, db10, dgt_a1 = ln_bwd(dx3, x2, y2, gt_a1, vec(ln_g[1, 0]), "fox_ln_bwd")
    do_x, delta_x = linear_nt_delta(dy2, fox_w_o, o_x, sel_fox, "fox_out_bwd")
    (dq_x, dk_x, dv_x, dfq_x, dfk_x), got = split(fox_attn_bwd(
        qkv, do_x, cum, cum_rows, rows16(lse_x, "fox_lse_rows"), rows16(delta_x, "fox_delta_rows"), nb, "fox_attn_bwd",
        hosted=scatter(list(sent), sent)))
    landed(list(sent), got)
    dcum = tokens128(dfq_x + dfk_x, onehot16, "fox_dcum")
    dhf, dbf = fox_gate_bwd(dcum, hf, b_f_pad, triu, nb, "fox_gate_bwd")
    dqkv = jnp.concatenate([dq_x.astype(BF16), dk_x, dv_x], axis=1)
    dx2, dsc_a1, dsh_a1 = linear_nt_mod_bwd([(dqkv, fox_w_qkv), (dhf, fox_w_f)], dz2, x2, sc_a1, "fox_in_bwd")
    dw_qkv = wgrad(u_x[None], dqkv[None], "fox_dwqkv")[0]
    dw_f = wgrad(u_x[None], dhf[None], "fox_dwf")[0]
    parts["fox_w_in"] = halves_of(_full_to_cols(jnp.concatenate([dw_qkv, dw_f[:, :FOX_HEADS]], axis=1)))
    parts["fox_w_o"] = wgrad(o_x[None], dy2[None], "fox_dwo")[0].reshape(N_CHIPS, 2, -1, d)
    sent = {k: parts[k].astype(BF16) for k in ("fox_w_in", "fox_w_o")}
    dz1, dy1, dg01, db01, dgt_f0 = ln_bwd(dx2, x1, y1, gt_f, vec(ln_g[0, 1]), "ffn0_ln_bwd")
    (dhg0, dhu0, act0, dx1, dsc_f0, dsh_f0), got = split(ffn_bwd(dy1, hg0, hu0, *ffn0_w, dz1, x1, sc_f, "ffn0_bwd",
                                                                 hosted=scatter(list(sent), sent)))
    landed(list(sent), got)
    sent = ffn_grads(0, u_f0, dhg0, dhu0, act0, dy1)
    dz0, dy0, dg00, db00, dgt_a0 = ln_bwd(dx1, x0, y0, gt_a, vec(ln_g[0, 0]), "mla_ln_bwd")
    do_m, delta_m = linear_nt_delta(dy0, wts["mla_w_o"], o_m, sel_mla, "mla_out_bwd")
    (dqn_m, dqr_m, dkn_m, dkr_m, dv_m), got = split(mla_attn_bwd(
        q_m, kn_m, kr2_m, v_m, do_m, rows16(lse_m, "mla_lse_rows"), rows16(delta_m, "mla_delta_rows"), nb,
        "mla_attn_bwd", hosted=scatter(list(sent), sent)))
    landed(list(sent), got)
    dh_in, dq_pre, dgq, dgkv = mla_mid_bwd(
        dqn_m, dqr_m, dkn_m, dv_m, dkr_m, h_in, vec(mla_g_q), vec(mla_g_kv), w_uq_p, wts["mla_w_uk"],
        wts["mla_w_uv"], cos8, sin8, cos64, sin64s, swap64, from_heads, head_sum, "mla_mid_bwd")
    parts["mla_w_o"] = wgrad(o_m[None], dy0[None], "mla_dwo")[0].reshape(N_CHIPS, 2, -1, d)
    parts["mla_w_uq"] = halves_of(_full_to_cols(wgrad(cq_m[None], dq_pre[None], "mla_dwuq")[0][:, inv_perm]))
    parts["mla_w_uk"] = halves_of(_full_to_cols(wgrad(ckv_m[None], dkn_m[None], "mla_dwuk")[0]))
    parts["mla_w_uv"] = halves_of(_full_to_cols(wgrad(ckv_m[None], dv_m[None], "mla_dwuv")[0]))
    parts["mla_w_in"] = wgrad(u_m[None], dh_in[None], "mla_dwin")[0].reshape(N_CHIPS, 2, -1, h_in.shape[1])
    dx0, dsc_a0, dsh_a0 = linear_nt_mod_bwd([(dh_in, wts["mla_w_in"])], dz0, x0, sc_a, "mla_in_bwd")

    dmods = [(dsh_a0, dsc_a0, dgt_a0, dsh_f0, dsc_f0, dgt_f0), (dsh_a1, dsc_a1, dgt_a1, dsh_f1, dsc_f1, dgt_f1)]
    d_ln_g = jnp.stack([jnp.concatenate([dg00, dg01], axis=0), jnp.concatenate([dg10, dg11], axis=0)])
    d_ln_b = jnp.stack([jnp.concatenate([db00, db01], axis=0), jnp.concatenate([db10, db11], axis=0)])
    return loss_part, dx0.reshape(nb, s, d), (parts, recv), dmods, d_ln_g, d_ln_b, dgq, dgkv, dbf[:, :FOX_HEADS]


def _pad_rows(a, rows):
    return jnp.pad(a, ((0, rows - a.shape[0]), (0, 0)))


def kernel(x, c, positions, mla_w_in, mla_g_q, mla_w_uq, mla_g_kv, mla_w_uk, mla_w_uv, mla_w_o, fox_w_in, fox_b_f, fox_w_o, ada_w, ada_b, ffn_w_gate, ffn_w_up, ffn_w_down, ln_g, ln_b, loss_target, m_mla_w_in, m_mla_g_q, m_mla_w_uq, m_mla_g_kv, m_mla_w_uk, m_mla_w_uv, m_mla_w_o, m_fox_w_in, m_fox_b_f, m_fox_w_o, m_ada_w, m_ada_b, m_ffn_w_gate, m_ffn_w_up, m_ffn_w_down, m_ln_g, m_ln_b, v_mla_w_in, v_mla_g_q, v_mla_w_uq, v_mla_g_kv, v_mla_w_uk, v_mla_w_uv, v_mla_w_o, v_fox_w_in, v_fox_b_f, v_fox_w_o, v_ada_w, v_ada_b, v_ffn_w_gate, v_ffn_w_up, v_ffn_w_down, v_ln_g, v_ln_b):
    args = dict(locals())
    nb, s, d = x.shape
    ax, ay, ac = lax.axis_index("x"), lax.axis_index("y"), lax.axis_index("c")
    chip = 2 * ax + ay
    dev = 2 * chip + ac
    n_dev = 2 * N_CHIPS
    n_all = nb * n_dev

    shard_shapes = {n: (args[n].shape if _SHARD_KIND[n] == "chunk" else args[n].shape[1:]) for n in _PACKED}

    def block(n, layer=None):
        w = args[n].reshape(shard_shapes[n]) if layer is None else args[n][layer]
        return _halves(w.astype(BF16))

    mla_names = [n for n in _PACKED if n.startswith("mla")]
    wts = {}
    for n, g in zip(mla_names, all_gather_chips([block(n) for n in mla_names], "gather_mla")):
        g = g.reshape(N_CHIPS, *shard_shapes[n])
        wts[n] = g.reshape(-1, g.shape[-1]) if _SHARD_KIND[n] == "rows" else _cols_to_full(g)
    ffn_names = ("ffn_w_gate", "ffn_w_up", "ffn_w_down")
    shards = {"ffn0": [block(n, 0) for n in ffn_names], "fox": [block("fox_w_in"), block("fox_w_o")],
              "ffn1": [block(n, 1) for n in ffn_names]}

    ln_cols = ln_g.shape[-1]
    ln_blk = jnp.concatenate([ln_g.reshape(2 * DEPTH, ln_cols), ln_b.reshape(2 * DEPTH, ln_cols)], axis=0)
    early = jnp.concatenate([_pad_rows(c, 8), jnp.pad(_pad_rows(ln_blk, 8), ((0, 0), (0, d - ln_cols)))], axis=0)
    early = all_gather8(early, "gather_c_ln").reshape(n_dev, 16, d)
    c_all = early[:, :nb].reshape(n_all, d)
    ln_all = early.reshape(N_CHIPS, 2, 16, d)[:, 0, 8:8 + 4 * DEPTH, :ln_cols]
    ln_all = jnp.transpose(ln_all, (1, 0, 2)).reshape(4 * DEPTH, d)
    ln_g_full = ln_all[:2 * DEPTH].reshape(DEPTH, 2, d)
    ln_b_full = ln_all[2 * DEPTH:].reshape(DEPTH, 2, d)
    mod_part = ada_mod_part(c_all, ada_w, "ada_mod")
    ncol = mod_part.shape[-1]
    mod_g = all_gather8(mod_part.reshape(DEPTH * n_all, ncol), "gather_mod")
    mod_g = mod_g.reshape(N_CHIPS, 2, DEPTH, n_all, ncol)[:, 0]
    mod_full = jnp.transpose(mod_g, (1, 2, 0, 3)).reshape(DEPTH, n_all, N_CHIPS * ncol) + ada_b[:, None, :]
    mod_loc = lax.dynamic_slice_in_dim(mod_full, dev * nb, nb, axis=1)
    mods = [tuple(mod_loc[i, :, k * d:(k + 1) * d].reshape(nb, 1, d) for k in range(6)) for i in range(DEPTH)]

    loss_part, grad_x, (parts, recv), dmods, d_ln_g, d_ln_b, dgq, dgkv, dbf = _local_step(
        x, positions, loss_target, mods, wts, ln_g_full, ln_b_full, mla_g_q[0], mla_g_kv[0], fox_b_f[0], shards)
    loss = lax.psum(loss_part, ("x", "y", "c"))

    dmod_rows = jnp.stack([jnp.concatenate([v_.reshape(nb, d) for v_ in dm], axis=1) for dm in dmods])
    small = jnp.concatenate([
        d_ln_g.reshape(2 * DEPTH, d), d_ln_b.reshape(2 * DEPTH, d),
        jnp.pad(jnp.concatenate([dgq, dgkv, dbf], axis=1), ((0, 0), (0, d - 2 * MLA_QR - FOX_HEADS))),
        dmod_rows.reshape(DEPTH * nb * 6, d)], axis=0)
    n_small = small.shape[0]
    small_rows = -(-n_small // 8) * 8
    small_all = all_gather8(_pad_rows(small, small_rows), "gather_stats").reshape(n_dev, small_rows, d)
    stat_sum = sum_leading(small_all, "sum_stats")
    g_ln_g = lax.dynamic_slice_in_dim(stat_sum[:2 * DEPTH], chip * ln_cols, ln_cols, axis=1).reshape(DEPTH, 2, ln_cols)
    g_ln_b = lax.dynamic_slice_in_dim(stat_sum[2 * DEPTH:4 * DEPTH], chip * ln_cols, ln_cols, axis=1).reshape(DEPTH, 2, ln_cols)
    row = stat_sum[4 * DEPTH]
    g_gq = row[:MLA_QR].reshape(1, MLA_QR)
    g_gkv = row[MLA_QR:2 * MLA_QR].reshape(1, MLA_KVR)
    g_bf = row[2 * MLA_QR:2 * MLA_QR + FOX_HEADS].reshape(1, FOX_HEADS)
    base = 4 * DEPTH + 1
    dmod_all = small_all[:, base:base + DEPTH * nb * 6].reshape(n_dev, DEPTH, nb, 6 * d)
    dmod_all = jnp.transpose(dmod_all, (1, 0, 2, 3)).reshape(DEPTH, n_all, 6 * d)
    g_ada_b = sum_leading(jnp.transpose(dmod_all, (1, 0, 2)), "sum_ada_b")
    dmod_mine = lax.dynamic_slice_in_dim(dmod_all, chip * ncol, ncol, axis=2)
    g_ada_w = ada_grad(c_all.T, dmod_mine, "ada_grad")

    late = [k for k in parts if k not in recv]
    recv.update(zip(late, scatter_exchange([parts[k].astype(BF16) for k in late], "rs_exchange_mla")))
    place = jnp.stack([dev, ac, chip]).astype(jnp.int32)
    bufs = []
    for n in _PACKED:
        if _SHARD_KIND[n] == "chunk":
            buf = None
            for layer in range(DEPTH):
                key = "%s/%d" % (n, layer)
                buf = sum_devices(parts[key], recv[key], place, "rs_sum_%s%d" % (n, layer), slot=(layer, DEPTH, buf))
        else:
            buf = sum_devices(parts[n], recv[n], place, "rs_sum_" + n)
        bufs.append(buf)
    joined = sibling_join_halves(bufs, "rs_join")
    g_big = {n: j.reshape(shard_shapes[n]) for n, j in zip(_PACKED, joined)}

    g_out = {
        "mla_w_in": g_big["mla_w_in"], "mla_g_q": g_gq, "mla_w_uq": g_big["mla_w_uq"], "mla_g_kv": g_gkv,
        "mla_w_uk": g_big["mla_w_uk"], "mla_w_uv": g_big["mla_w_uv"], "mla_w_o": g_big["mla_w_o"],
        "fox_w_in": g_big["fox_w_in"], "fox_b_f": g_bf, "fox_w_o": g_big["fox_w_o"],
        "ada_w": g_ada_w, "ada_b": g_ada_b, "ffn_w_gate": g_big["ffn_w_gate"], "ffn_w_up": g_big["ffn_w_up"],
        "ffn_w_down": g_big["ffn_w_down"], "ln_g": g_ln_g, "ln_b": g_ln_b}
    names = ["mla_w_in", "mla_g_q", "mla_w_uq", "mla_g_kv", "mla_w_uk", "mla_w_uv", "mla_w_o", "fox_w_in", "fox_b_f",
             "fox_w_o", "ada_w", "ada_b", "ffn_w_gate", "ffn_w_up", "ffn_w_down", "ln_g", "ln_b"]
    small_names = ["mla_g_q", "mla_g_kv", "fox_b_f", "ada_b", "ln_g", "ln_b"]
    deltas, new_m, new_v = {}, {}, {}
    for n in names:
        if n in small_names:
            continue
        shp = args[n].shape
        two_d = (-1, shp[-1])
        dl, mn, vn = adamw(args[n].reshape(two_d), g_out[n].reshape(two_d), args["m_" + n].reshape(two_d),
                           args["v_" + n].reshape(two_d), "adamw_" + n)
        deltas[n], new_m[n], new_v[n] = dl.reshape(shp), mn.reshape(shp), vn.reshape(shp)

    def small_pack(prefix, src):
        flat = jnp.concatenate([src[prefix + n].reshape(-1) for n in small_names])
        size = -(-flat.shape[0] // (8 * 128)) * 8 * 128
        return jnp.pad(flat, (0, size - flat.shape[0])).reshape(-1, 128)

    sd, sm, sv = adamw(small_pack("", args), small_pack("", g_out), small_pack("m_", args), small_pack("v_", args),
                       "adamw_small")
    off = 0
    for n in small_names:
        shp = args[n].shape
        size = math.prod(shp)
        deltas[n] = sd.reshape(-1)[off:off + size].reshape(shp)
        new_m[n] = sm.reshape(-1)[off:off + size].reshape(shp)
        new_v[n] = sv.reshape(-1)[off:off + size].reshape(shp)
        off += size

    outs = [loss, grad_x]
    outs += [g_out[n].reshape(args[n].shape) for n in names]
    outs += [deltas[n] for n in names] + [new_m[n] for n in names] + [new_v[n] for n in names]
    return tuple(outs)
```

```python
import functools
import math

import numpy as np
import jax
import jax.numpy as jnp
from jax import lax
from jax.experimental import pallas as pl
from jax.experimental.pallas import tpu as pltpu

F32 = jnp.float32
BF16 = jnp.bfloat16
MESH = pl.DeviceIdType.MESH

D_MODEL = 1024
DEPTH = 2
MLA_HEADS = 8
MLA_NOPE = 128
MLA_ROPE = 64
MLA_V = 128
MLA_QR = 256
MLA_KVR = 256
ROPE_THETA = 10000.0
FOX_HEADS = 16
FOX_HD = 64
D_FF = 2816
N_CHIPS = 4
FF_CHUNK = D_FF // N_CHIPS
ALPHA = (2.0 * DEPTH) ** 0.25
EPS = 1e-5
ADAM_LR = 0.001
ADAM_B1 = 0.9
ADAM_B2 = 0.999
ADAM_EPS = 1e-08
ADAM_WD = 0.01
ADAM_STEP = 10

VMEM_LIMIT = 56 * 1024 * 1024
TOKEN_TILE = 512
WGRAD_TOKENS = 2048
ATTN_TILE = 512
COMM_BLOCK_BYTES = 2 * 1024 * 1024
ADAMW_BLOCK_BYTES = 1024 * 1024


def _cp(n_axes):
    return pltpu.CompilerParams(dimension_semantics=("arbitrary",) * n_axes, vmem_limit_bytes=VMEM_LIMIT)


def _dot(a, b):
    return jnp.dot(a, b, preferred_element_type=F32)


def _dot_nt(a, b):
    return lax.dot_general(a, b, (((1,), (1,)), ((), ())), preferred_element_type=F32)


def _dot_tn(a, b):
    return lax.dot_general(a, b, (((0,), (0,)), ((), ())), preferred_element_type=F32)


def _dot_f32(a, b):
    return jnp.dot(a, b, preferred_element_type=F32, precision=lax.Precision.HIGHEST)


def _sds(shape, dtype):
    return jax.ShapeDtypeStruct(shape, dtype)


def _place():
    return lax.axis_index("x"), lax.axis_index("y"), lax.axis_index("c")


class _Hosted:
    def __init__(self, inputs, out_shape, sems, start, finish):
        self.inputs, self.out_shape, self.sems, self.start, self.finish = inputs, out_shape, sems, start, finish


def _call(body, name, grid, in_specs, out_specs, out_shape, args, scratch_shapes=(), hosted=None):
    in_specs, out_specs, out_shape, scratch_shapes = list(in_specs), list(out_specs), list(out_shape), list(scratch_shapes)
    if hosted is None:
        return pl.pallas_call(body, name=name, grid=grid, in_specs=in_specs, out_specs=out_specs, out_shape=out_shape,
                              scratch_shapes=scratch_shapes, compiler_params=_cp(len(grid)))(*args)
    n_in, n_out, n_scr = len(in_specs), len(out_specs), len(scratch_shapes)
    h_in, h_out = len(hosted.inputs), len(hosted.out_shape)

    def carried(*refs):
        o0 = n_in + h_in
        s0 = o0 + n_out + h_out
        c_in, c_out, c_sem = refs[n_in:o0], refs[o0 + n_out:s0], refs[s0 + n_scr:]
        ids = [pl.program_id(a) for a in range(len(grid))]
        first = functools.reduce(jnp.logical_and, [i == 0 for i in ids])
        last = functools.reduce(jnp.logical_and, [i == g - 1 for i, g in zip(ids, grid)])

        @pl.when(first)
        def _():
            hosted.start(c_in, c_out, c_sem)

        body(*refs[:n_in], *refs[o0:o0 + n_out], *refs[s0:s0 + n_scr])

        @pl.when(last)
        def _():
            hosted.finish(c_in, c_out, c_sem)

    hbm = pl.BlockSpec(memory_space=pl.ANY)
    res = pl.pallas_call(
        carried, name=name, grid=grid, in_specs=in_specs + [hbm] * h_in, out_specs=out_specs + [hbm] * h_out,
        out_shape=out_shape + list(hosted.out_shape), scratch_shapes=scratch_shapes + list(hosted.sems),
        compiler_params=_cp(len(grid)))(*args, *hosted.inputs)
    return res[:n_out], res[n_out:]


def mod_linear(x, shift, scale, w, out_dtype, name, tn=None, emit_u=False):
    t, d = x.shape
    n = w.shape[1]
    tn = n if tn is None else tn
    tm = TOKEN_TILE
    tps = (t // shift.shape[0]) // tm

    def body(x_ref, sh_ref, sc_ref, w_ref, o_ref, *rest):
        u = (x_ref[...] * (1.0 + sc_ref[...]) + sh_ref[...]).astype(BF16)
        o_ref[...] = _dot(u, w_ref[...]).astype(out_dtype)
        if emit_u:
            @pl.when(pl.program_id(1) == 0)
            def _():
                rest[0][...] = u

    vec = pl.BlockSpec((None, 1, d), lambda i, j: (i // tps, 0, 0))
    out_shape = [_sds((t, n), out_dtype)]
    out_specs = [pl.BlockSpec((tm, tn), lambda i, j: (i, j))]
    if emit_u:
        out_shape.append(_sds((t, d), BF16))
        out_specs.append(pl.BlockSpec((tm, d), lambda i, j: (i, 0)))
    res = pl.pallas_call(
        body, name=name, grid=(t // tm, n // tn),
        in_specs=[pl.BlockSpec((tm, d), lambda i, j: (i, 0)), vec, vec,
                  pl.BlockSpec((d, tn), lambda i, j: (0, j))],
        out_specs=out_specs, out_shape=out_shape, compiler_params=_cp(2),
    )(x, shift, scale, w)
    return res if emit_u else res[0]


def _rms(h, g):
    rstd = lax.rsqrt(jnp.mean(h * h, axis=-1, keepdims=True) + EPS)
    return h * rstd, rstd


def mla_mid_fwd(h, g_q, g_kv, w_uq, w_uk, w_uv, cos8, sin8, cos64, sin64s, swap64, rope_to_heads, dup64, name):
    t = h.shape[0]
    tm = TOKEN_TILE
    hq = MLA_HEADS * MLA_NOPE
    hr = MLA_HEADS * MLA_ROPE // 2

    def body(h_ref, gq_ref, gkv_ref, wuq_ref, wuk_ref, wuv_ref, c8_ref, s8_ref, c64_ref, s64_ref, sw_ref, p_ref, d_ref,
             q_ref, kn_ref, v_ref, kr_ref, cq_ref, ckv_ref):
        hh = h_ref[...]
        cq = (_rms(hh[:, :MLA_QR], None)[0] * gq_ref[...]).astype(BF16)
        ckv = (_rms(hh[:, MLA_QR:MLA_QR + MLA_KVR], None)[0] * gkv_ref[...]).astype(BF16)
        cq_ref[...] = cq
        ckv_ref[...] = ckv
        q = _dot(cq, wuq_ref[...])
        x1 = q[:, hq:hq + hr]
        x2 = q[:, hq + hr:]
        cs = c8_ref[...]
        sn = s8_ref[...]
        rot = jnp.concatenate([x1 * cs - x2 * sn, x2 * cs + x1 * sn], axis=1).astype(BF16)
        q_ref[...] = jnp.concatenate([q[:, :hq].astype(BF16), _dot(rot, p_ref[...]).astype(BF16)], axis=1)
        kn_ref[...] = _dot(ckv, wuk_ref[...]).astype(BF16)
        v_ref[...] = _dot(ckv, wuv_ref[...]).astype(BF16)
        kr = hh[:, MLA_QR + MLA_KVR:]
        kr = (kr * c64_ref[...] + _dot_f32(kr, sw_ref[...]) * s64_ref[...]).astype(BF16)
        kr_ref[...] = _dot(kr, d_ref[...]).astype(BF16)

    def rows(n):
        return pl.BlockSpec((tm, n), lambda i: (i, 0))

    def whole(a):
        return pl.BlockSpec(a.shape, lambda i: (0,) * a.ndim)

    nq = w_uq.shape[1]
    return pl.pallas_call(
        body, name=name, grid=(t // tm,),
        in_specs=[rows(h.shape[1]), whole(g_q), whole(g_kv), whole(w_uq), whole(w_uk), whole(w_uv),
                  rows(hr), rows(hr), rows(MLA_ROPE), rows(MLA_ROPE), whole(swap64), whole(rope_to_heads), whole(dup64)],
        out_specs=[rows(nq), rows(hq), rows(hq), rows(2 * MLA_ROPE), rows(MLA_QR), rows(MLA_KVR)],
        out_shape=[_sds((t, nq), BF16), _sds((t, hq), BF16), _sds((t, hq), BF16), _sds((t, 2 * MLA_ROPE), BF16),
                   _sds((t, MLA_QR), BF16), _sds((t, MLA_KVR), BF16)],
        compiler_params=_cp(1),
    )(h, g_q, g_kv, w_uq, w_uk, w_uv, cos8, sin8, cos64, sin64s, swap64, rope_to_heads, dup64)


def _pick_lane(tile, idx):
    lane = lax.broadcasted_iota(jnp.int32, tile.shape, 1)
    return jnp.sum(jnp.where(lane == idx, tile, 0.0), axis=1, keepdims=True)


def _pick_row(tile, idx):
    row = lax.broadcasted_iota(jnp.int32, tile.shape, 0)
    return jnp.sum(jnp.where(row == idx, tile, 0.0), axis=0, keepdims=True)


def _put_lane(tile, idx, col):
    lane = lax.broadcasted_iota(jnp.int32, tile.shape, 1)
    return jnp.where(lane == idx, col, tile)


def _put_row(tile, idx, row):
    r = lax.broadcasted_iota(jnp.int32, tile.shape, 0)
    return tile + jnp.where(r == idx, row, 0.0)


def _causal_softmax_blocks(i, tq, heads):
    def block(j, carry, masked):
        new = []
        for (score_fn, pv_fn, _), (m, l, acc) in zip(heads, carry):
            sc = score_fn(j)
            if masked:
                keep = lax.broadcasted_iota(jnp.int32, sc.shape, 0) >= lax.broadcasted_iota(jnp.int32, sc.shape, 1)
                sc = jnp.where(keep, sc, -1e30)
            m_new = jnp.maximum(m, jnp.max(sc, axis=1, keepdims=True))
            a = jnp.exp(m - m_new)
            p = jnp.exp(sc - m_new)
            new.append((m_new, a * l + jnp.sum(p, axis=1, keepdims=True), a * acc + pv_fn(j, p.astype(BF16))))
        return tuple(new)

    init = tuple((jnp.full((tq, 1), -1e30, F32), jnp.zeros((tq, 1), F32), jnp.zeros((tq, dv), F32)) for _, _, dv in heads)
    carry = lax.fori_loop(0, i, lambda j, c: block(j, c, False), init)
    return [(acc / l, m + jnp.log(l)) for m, l, acc in block(i, carry, True)]


def fox_attn_fwd(qkv, cum, cum_rows, nb, name, hosted=None):
    t = qkv.shape[0]
    s = t // nb
    tq = ATTN_TILE
    nq = s // tq
    npairs = FOX_HEADS // 2
    scale = FOX_HD ** -0.5

    def body(q_ref, k_ref, v_ref, cum_ref, cr_ref, o_ref, lse_ref):
        i = pl.program_id(1)
        hp = pl.program_id(2)

        @pl.when(hp == 0)
        def _():
            lse_ref[...] = jnp.zeros_like(lse_ref)

        q = q_ref[...]
        low = lax.broadcasted_iota(jnp.int32, q.shape, 1) < FOX_HD
        cum_t = cum_ref[...]

        def rows_of(j):
            return pl.ds(pl.multiple_of(j * tq, tq), tq)

        def head(a):
            hd = 2 * hp + a
            qa = jnp.where(low if a == 0 else jnp.logical_not(low), q, jnp.zeros_like(q))
            fq = _pick_lane(cum_t, hd)
            return (lambda j: _dot_nt(qa, k_ref[rows_of(j), :]) * scale + fq - _pick_row(cr_ref[j], hd),
                    lambda j, p: _dot(p, v_ref[rows_of(j), :]), 2 * FOX_HD)

        (o_0, lse_0), (o_1, lse_1) = _causal_softmax_blocks(i, tq, [head(0), head(1)])
        o_ref[...] = jnp.where(low, o_0, o_1).astype(BF16)
        lse_ref[...] = _put_lane(_put_lane(lse_ref[...], 2 * hp, lse_0), 2 * hp + 1, lse_1)

    return _call(
        body, name, (nb, nq, npairs),
        [pl.BlockSpec((tq, 128), lambda b, i, hp: (b * nq + i, hp)),
         pl.BlockSpec((s, 128), lambda b, i, hp: (b, npairs + hp)),
         pl.BlockSpec((s, 128), lambda b, i, hp: (b, 2 * npairs + hp)),
         pl.BlockSpec((tq, 128), lambda b, i, hp: (b * nq + i, 0)),
         pl.BlockSpec((nq, 16, tq), lambda b, i, hp: (b, 0, 0))],
        [pl.BlockSpec((tq, 128), lambda b, i, hp: (b * nq + i, hp)),
         pl.BlockSpec((tq, 128), lambda b, i, hp: (b * nq + i, 0))],
        [_sds((t, D_MODEL), BF16), _sds((t, 128), F32)], (qkv, qkv, qkv, cum, cum_rows), hosted=hosted)


def mla_attn_fwd(q, kn, kr2, v, nb, name, hosted=None):
    t = q.shape[0]
    s = t // nb
    tq = ATTN_TILE
    nq = s // tq
    npairs = MLA_HEADS // 2
    scale = (MLA_NOPE + MLA_ROPE) ** -0.5

    def body(qn_ref, qr_ref, kn_ref, kr_ref, v_ref, o_ref, lse_ref):
        i = pl.program_id(1)
        hp = pl.program_id(2)

        @pl.when(hp == 0)
        def _():
            lse_ref[...] = jnp.zeros_like(lse_ref)

        qr = qr_ref[...]
        low = lax.broadcasted_iota(jnp.int32, qr.shape, 1) < MLA_ROPE

        def rows_of(j):
            return pl.ds(pl.multiple_of(j * tq, tq), tq)

        def head(a):
            cols = slice(a * MLA_NOPE, (a + 1) * MLA_NOPE)
            q_cat = jnp.concatenate([qn_ref[:, cols], jnp.where(low if a == 0 else jnp.logical_not(low), qr,
                                                                jnp.zeros_like(qr))], axis=1)
            return (lambda j: _dot_nt(q_cat, jnp.concatenate([kn_ref[rows_of(j), cols], kr_ref[rows_of(j), :]], axis=1)) * scale,
                    lambda j, p: _dot(p, v_ref[rows_of(j), cols]), MLA_V)

        (o_0, lse_0), (o_1, lse_1) = _causal_softmax_blocks(i, tq, [head(0), head(1)])
        o_ref[...] = jnp.concatenate([o_0, o_1], axis=1).astype(BF16)
        lse_ref[...] = _put_lane(_put_lane(lse_ref[...], 2 * hp, lse_0), 2 * hp + 1, lse_1)

    wide = 2 * MLA_NOPE
    return _call(
        body, name, (nb, nq, npairs),
        [pl.BlockSpec((tq, wide), lambda b, i, hp: (b * nq + i, hp)),
         pl.BlockSpec((tq, 128), lambda b, i, hp: (b * nq + i, MLA_HEADS + hp)),
         pl.BlockSpec((s, wide), lambda b, i, hp: (b, hp)),
         pl.BlockSpec((s, 128), lambda b, i, hp: (b, 0)),
         pl.BlockSpec((s, wide), lambda b, i, hp: (b, hp))],
        [pl.BlockSpec((tq, wide), lambda b, i, hp: (b * nq + i, hp)),
         pl.BlockSpec((tq, 128), lambda b, i, hp: (b * nq + i, 0))],
        [_sds((t, MLA_HEADS * MLA_V), BF16), _sds((t, 128), F32)], (q, q, kn, kr2, v), hosted=hosted)


def rows16(a, name):
    t = a.shape[0]
    tq = ATTN_TILE

    def body(a_ref, o_ref):
        o_ref[...] = a_ref[...].T[:16, :]

    return pl.pallas_call(
        body, name=name, grid=(t // tq,), in_specs=[pl.BlockSpec((tq, 128), lambda n: (n, 0))],
        out_specs=pl.BlockSpec((None, 16, tq), lambda n: (n, 0, 0)), out_shape=_sds((t // tq, 16, tq), F32),
        compiler_params=_cp(1),
    )(a)


def tokens128(rows, onehot, name):
    nblk, _, tq = rows.shape

    def body(r_ref, e_ref, o_ref):
        o_ref[...] = lax.dot_general(r_ref[...], e_ref[...], (((0,), (0,)), ((), ())), preferred_element_type=F32,
                                     precision=lax.Precision.HIGHEST)

    return pl.pallas_call(
        body, name=name, grid=(nblk,),
        in_specs=[pl.BlockSpec((None, 16, tq), lambda n: (n, 0, 0)), pl.BlockSpec((16, 128), lambda n: (0, 0))],
        out_specs=pl.BlockSpec((tq, 128), lambda n: (n, 0)), out_shape=_sds((nblk * tq, 128), F32),
        compiler_params=_cp(1),
    )(rows, onehot)


def _layer_norm(z, g, b):
    mu = jnp.mean(z, axis=-1, keepdims=True)
    zc = z - mu
    rstd = lax.rsqrt(jnp.mean(zc * zc, axis=-1, keepdims=True) + EPS)
    xhat = zc * rstd
    return xhat * g + b, xhat, rstd


def linear_resid_ln(a, w, x_in, gate, ln_g, ln_b, name):
    t, kdim = a.shape
    d = w.shape[1]
    tm = TOKEN_TILE
    tps = (t // gate.shape[0]) // tm

    def body(a_ref, w_ref, x_ref, gt_ref, g_ref, b_ref, y_ref, xo_ref):
        y = _dot(a_ref[...], w_ref[...])
        y_ref[...] = y
        z = ALPHA * x_ref[...] + (1.0 + gt_ref[...]) * y
        xo_ref[...] = _layer_norm(z, g_ref[...], b_ref[...])[0]

    rows = pl.BlockSpec((tm, d), lambda i: (i, 0))
    vec = pl.BlockSpec((1, d), lambda i: (0, 0))
    return pl.pallas_call(
        body, name=name, grid=(t // tm,),
        in_specs=[pl.BlockSpec((tm, kdim), lambda i: (i, 0)), pl.BlockSpec((kdim, d), lambda i: (0, 0)), rows,
                  pl.BlockSpec((None, 1, d), lambda i: (i // tps, 0, 0)), vec, vec],
        out_specs=[rows, rows], out_shape=[_sds((t, d), F32), _sds((t, d), F32)],
        compiler_params=_cp(1),
    )(a, w, x_in, gate, ln_g, ln_b)


def ffn_fwd(x_in, shift, scale, gate, wg, wu, wd, ln_g, ln_b, name, hosted=None):
    t, d = x_in.shape
    c, _, fc = wg.shape
    tm = TOKEN_TILE
    tps = (t // gate.shape[0]) // tm

    def body(x_ref, sh_ref, sc_ref, gt_ref, wg_ref, wu_ref, wd_ref, g_ref, b_ref,
             u_ref, hg_ref, hu_ref, y_ref, xo_ref, acc_ref):
        cc = pl.program_id(1)

        @pl.when(cc == 0)
        def _():
            u_ref[...] = (x_ref[...] * (1.0 + sc_ref[...]) + sh_ref[...]).astype(BF16)
            acc_ref[...] = jnp.zeros_like(acc_ref)

        u = u_ref[...]
        hg = _dot(u, wg_ref[...])
        hu = _dot(u, wu_ref[...])
        hg_ref[...] = hg.astype(BF16)
        hu_ref[...] = hu.astype(BF16)
        act = (hg * jax.nn.sigmoid(hg) * hu).astype(BF16)
        acc_ref[...] += _dot(act, wd_ref[...])

        @pl.when(cc == c - 1)
        def _():
            y = acc_ref[...]
            y_ref[...] = y
            z = ALPHA * x_ref[...] + (1.0 + gt_ref[...]) * y
            xo_ref[...] = _layer_norm(z, g_ref[...], b_ref[...])[0]

    rows = pl.BlockSpec((tm, d), lambda i, cc: (i, 0))
    bvec = pl.BlockSpec((None, 1, d), lambda i, cc: (i // tps, 0, 0))
    vec = pl.BlockSpec((1, d), lambda i, cc: (0, 0))
    hspec = pl.BlockSpec((None, tm, fc), lambda i, cc: (cc, i, 0))
    wcol = pl.BlockSpec((None, d, fc), lambda i, cc: (cc, 0, 0))
    return _call(
        body, name, (t // tm, c),
        [rows, bvec, bvec, bvec, wcol, wcol, pl.BlockSpec((None, fc, d), lambda i, cc: (cc, 0, 0)), vec, vec],
        [rows, hspec, hspec, rows, rows],
        [_sds((t, d), BF16), _sds((c, t, fc), BF16), _sds((c, t, fc), BF16), _sds((t, d), F32), _sds((t, d), F32)],
        (x_in, shift, scale, gate, wg, wu, wd, ln_g, ln_b), scratch_shapes=[pltpu.VMEM((tm, d), F32)], hosted=hosted)


def fox_gate_fwd(hf, b_f, tri, n_batch, name):
    t, n = hf.shape
    blk = tri.shape[0]
    nb = (t // n_batch) // blk

    def body(hf_ref, b_ref, tri_ref, o_ref, carry_ref):
        @pl.when(pl.program_id(1) == 0)
        def _():
            carry_ref[...] = jnp.zeros_like(carry_ref)

        xx = hf_ref[...] + b_ref[...]
        lf = jnp.minimum(xx, 0.0) - jnp.log(1.0 + jnp.exp(-jnp.abs(xx)))
        cum = _dot_f32(tri_ref[...], lf) + carry_ref[...]
        o_ref[...] = cum
        carry_ref[...] = cum[blk - 1:blk, :]

    return pl.pallas_call(
        body, name=name, grid=(n_batch, nb),
        in_specs=[pl.BlockSpec((blk, n), lambda bb, i: (bb * nb + i, 0)), pl.BlockSpec((1, n), lambda bb, i: (0, 0)),
                  pl.BlockSpec((blk, blk), lambda bb, i: (0, 0))],
        out_specs=pl.BlockSpec((blk, n), lambda bb, i: (bb * nb + i, 0)),
        out_shape=_sds((t, n), F32), scratch_shapes=[pltpu.VMEM((1, n), F32)],
        compiler_params=_cp(2),
    )(hf, b_f, tri)


def loss_grad(x_out, target, name):
    t, d = x_out.shape
    tm = TOKEN_TILE

    def body(x_ref, t_ref, g_ref, l_ref):
        @pl.when(pl.program_id(0) == 0)
        def _():
            l_ref[...] = jnp.zeros_like(l_ref)

        err = x_ref[...] - t_ref[...]
        g_ref[...] = err / d
        l_ref[...] += jnp.sum(err * err, axis=0, keepdims=True)

    rows = pl.BlockSpec((tm, d), lambda i: (i, 0))
    return pl.pallas_call(
        body, name=name, grid=(t // tm,), in_specs=[rows, rows],
        out_specs=[rows, pl.BlockSpec((1, d), lambda i: (0, 0))],
        out_shape=[_sds((t, d), F32), _sds((1, d), F32)], compiler_params=_cp(1),
    )(x_out, target)


def ln_bwd(dxo, x_in, y, gate, ln_g, name):
    t, d = dxo.shape
    nb = gate.shape[0]
    tm = TOKEN_TILE
    tps = (t // nb) // tm

    def body(dxo_ref, x_ref, y_ref, gt_ref, g_ref, dz_ref, dy_ref, dg_ref, db_ref, dgt_ref):
        i = pl.program_id(0)

        @pl.when(i == 0)
        def _():
            dg_ref[...] = jnp.zeros_like(dg_ref)
            db_ref[...] = jnp.zeros_like(db_ref)

        @pl.when(i % tps == 0)
        def _():
            dgt_ref[...] = jnp.zeros_like(dgt_ref)

        yy = y_ref[...]
        g1 = 1.0 + gt_ref[...]
        z = ALPHA * x_ref[...] + g1 * yy
        _, xhat, rstd = _layer_norm(z, 1.0, 0.0)
        dxo_v = dxo_ref[...]
        dg_ref[...] += jnp.sum(dxo_v * xhat, axis=0, keepdims=True)
        db_ref[...] += jnp.sum(dxo_v, axis=0, keepdims=True)
        dxh = dxo_v * g_ref[...]
        dz = rstd * (dxh - jnp.mean(dxh, axis=-1, keepdims=True) - xhat * jnp.mean(dxh * xhat, axis=-1, keepdims=True))
        dz_ref[...] = dz
        dy_ref[...] = (g1 * dz).astype(BF16)
        dgt_ref[...] += jnp.sum(dz * yy, axis=0, keepdims=True)

    rows = pl.BlockSpec((tm, d), lambda i: (i, 0))
    vec = pl.BlockSpec((1, d), lambda i: (0, 0))
    bvec = pl.BlockSpec((None, 1, d), lambda i: (i // tps, 0, 0))
    return pl.pallas_call(
        body, name=name, grid=(t // tm,), in_specs=[rows, rows, rows, bvec, vec],
        out_specs=[rows, rows, vec, vec, bvec],
        out_shape=[_sds((t, d), F32), _sds((t, d), BF16), _sds((1, d), F32), _sds((1, d), F32), _sds((nb, 1, d), F32)],
        compiler_params=_cp(1),
    )(dxo, x_in, y, gate, ln_g)


def _mod_bwd_tail(du, dz_ref, x_ref, sc_ref, dx_ref, dsc_ref, dsh_ref, first):
    @pl.when(first)
    def _():
        dsc_ref[...] = jnp.zeros_like(dsc_ref)
        dsh_ref[...] = jnp.zeros_like(dsh_ref)

    dx_ref[...] = ALPHA * dz_ref[...] + du * (1.0 + sc_ref[...])
    dsc_ref[...] += jnp.sum(du * x_ref[...], axis=0, keepdims=True)
    dsh_ref[...] += jnp.sum(du, axis=0, keepdims=True)


def ffn_bwd(dy, hg, hu, wg, wu, wd, dz, x_in, scale, name, hosted=None):
    t, d = dy.shape
    c, _, fc = wg.shape
    nb = scale.shape[0]
    tm = TOKEN_TILE
    tps = (t // nb) // tm

    def body(dy_ref, hg_ref, hu_ref, wg_ref, wu_ref, wd_ref, dz_ref, x_ref, sc_ref,
             dhg_ref, dhu_ref, act_ref, dx_ref, dsc_ref, dsh_ref, acc_ref):
        i = pl.program_id(0)
        cc = pl.program_id(1)

        @pl.when(cc == 0)
        def _():
            acc_ref[...] = jnp.zeros_like(acc_ref)

        hgv = hg_ref[...].astype(F32)
        huv = hu_ref[...].astype(F32)
        da = _dot_nt(dy_ref[...], wd_ref[...])
        sg = jax.nn.sigmoid(hgv)
        sl = hgv * sg
        act_ref[...] = (sl * huv).astype(BF16)
        dhu = (da * sl).astype(BF16)
        dhg = (da * huv * (sg * (1.0 + hgv * (1.0 - sg)))).astype(BF16)
        dhu_ref[...] = dhu
        dhg_ref[...] = dhg
        acc_ref[...] += _dot_nt(dhg, wg_ref[...]) + _dot_nt(dhu, wu_ref[...])

        @pl.when(cc == c - 1)
        def _():
            _mod_bwd_tail(acc_ref[...], dz_ref, x_ref, sc_ref, dx_ref, dsc_ref, dsh_ref, i % tps == 0)

    rows = pl.BlockSpec((tm, d), lambda i, cc: (i, 0))
    bvec = pl.BlockSpec((None, 1, d), lambda i, cc: (i // tps, 0, 0))
    hspec = pl.BlockSpec((None, tm, fc), lambda i, cc: (cc, i, 0))
    wcol = pl.BlockSpec((None, d, fc), lambda i, cc: (cc, 0, 0))
    return _call(
        body, name, (t // tm, c),
        [rows, hspec, hspec, wcol, wcol, pl.BlockSpec((None, fc, d), lambda i, cc: (cc, 0, 0)), rows, rows, bvec],
        [hspec, hspec, hspec, rows, bvec, bvec],
        [_sds((c, t, fc), BF16), _sds((c, t, fc), BF16), _sds((c, t, fc), BF16), _sds((t, d), F32),
         _sds((nb, 1, d), F32), _sds((nb, 1, d), F32)],
        (dy, hg, hu, wg, wu, wd, dz, x_in, scale), scratch_shapes=[pltpu.VMEM((tm, d), F32)], hosted=hosted)


def linear_nt_mod_bwd(pairs, dz, x_in, scale, name):
    t, d = dz.shape
    nb = scale.shape[0]
    tm = TOKEN_TILE
    tps = (t // nb) // tm
    npairs = len(pairs)

    def body(*refs):
        dh_refs = refs[:npairs]
        w_refs = refs[npairs:2 * npairs]
        dz_ref, x_ref, sc_ref, dx_ref, dsc_ref, dsh_ref = refs[2 * npairs:]
        du = _dot_nt(dh_refs[0][...], w_refs[0][...])
        for kk in range(1, npairs):
            du = du + _dot_nt(dh_refs[kk][...], w_refs[kk][...])
        _mod_bwd_tail(du, dz_ref, x_ref, sc_ref, dx_ref, dsc_ref, dsh_ref, pl.program_id(0) % tps == 0)

    rows = pl.BlockSpec((tm, d), lambda i: (i, 0))
    bvec = pl.BlockSpec((None, 1, d), lambda i: (i // tps, 0, 0))
    in_specs = [pl.BlockSpec((tm, dh.shape[1]), lambda i: (i, 0)) for dh, _ in pairs]
    in_specs += [pl.BlockSpec(w.shape, lambda i: (0, 0)) for _, w in pairs]
    in_specs += [rows, rows, bvec]
    return pl.pallas_call(
        body, name=name, grid=(t // tm,), in_specs=in_specs,
        out_specs=[rows, bvec, bvec],
        out_shape=[_sds((t, d), F32), _sds((nb, 1, d), F32), _sds((nb, 1, d), F32)],
        compiler_params=_cp(1),
    )(*[dh for dh, _ in pairs], *[w for _, w in pairs], dz, x_in, scale)


def linear_nt_delta(dy, w_o, o, head_sel, name):
    t, d = dy.shape
    hdv = w_o.shape[0]
    tm = TOKEN_TILE

    def body(dy_ref, w_ref, o_ref, sel_ref, do_ref, dl_ref):
        do = _dot_nt(dy_ref[...], w_ref[...])
        do_ref[...] = do.astype(BF16)
        dl_ref[...] = _dot_f32(do * o_ref[...].astype(F32), sel_ref[...])

    return pl.pallas_call(
        body, name=name, grid=(t // tm,),
        in_specs=[pl.BlockSpec((tm, d), lambda i: (i, 0)), pl.BlockSpec((hdv, d), lambda i: (0, 0)),
                  pl.BlockSpec((tm, hdv), lambda i: (i, 0)), pl.BlockSpec(head_sel.shape, lambda i: (0, 0))],
        out_specs=[pl.BlockSpec((tm, hdv), lambda i: (i, 0)), pl.BlockSpec((tm, 128), lambda i: (i, 0))],
        out_shape=[_sds((t, hdv), BF16), _sds((t, 128), F32)], compiler_params=_cp(1),
    )(dy, w_o, o, head_sel)


def _attn_bwd_blocks(j, nk, tk, scale, heads):
    def block(i, carry, masked):
        new = []
        for hd, (dk_acc, dv_acc, dfk_acc) in zip(heads, carry):
            qb = hd["q"](i)
            dob = hd["do"](i)
            lse_row, dl_row = hd["rows"](i)
            st = _dot_nt(hd["k"], qb) * scale
            if hd["bias"] is not None:
                fq_row, fk_col = hd["bias"](i)
                st = st + fq_row - fk_col
            if masked:
                keep = lax.broadcasted_iota(jnp.int32, st.shape, 1) >= lax.broadcasted_iota(jnp.int32, st.shape, 0)
                st = jnp.where(keep, st, -1e30)
            pt = jnp.exp(st - lse_row)
            dv_acc = dv_acc + _dot(pt.astype(BF16), dob)
            dst = pt * (_dot_nt(hd["v"], dob) - dl_row)
            if hd["add_dfq"] is not None:
                dfk_acc = dfk_acc - jnp.sum(dst, axis=1, keepdims=True)
                hd["add_dfq"](i, jnp.sum(dst, axis=0, keepdims=True))
            dsb = (dst * scale).astype(BF16)
            dk_acc = dk_acc + _dot(dsb, qb)
            hd["add_dq"](i, _dot_tn(dsb, hd["k"]))
            new.append((dk_acc, dv_acc, dfk_acc))
        return tuple(new)

    init = tuple((jnp.zeros((tk, hd["k"].shape[1]), F32), jnp.zeros((tk, hd["v"].shape[1]), F32), jnp.zeros((tk, 1), F32))
                 for hd in heads)
    carry = block(j, init, True)
    return lax.fori_loop(j + 1, nk, lambda i, c: block(i, c, False), carry)


def fox_attn_bwd(qkv, do, cum, cum_rows, lse_rows, delta_rows, nb, name, hosted=None):
    t = qkv.shape[0]
    s = t // nb
    tk = ATTN_TILE
    nk = s // tk
    npairs = FOX_HEADS // 2
    scale = FOX_HD ** -0.5

    def body(q_ref, k_ref, v_ref, do_ref, cum_ref, cr_ref, lr_ref, dr_ref, dq_ref, dk_ref, dv_ref, dfq_ref, dfk_ref):
        hp = pl.program_id(1)
        j = pl.program_id(2)

        @pl.when(j == 0)
        def _():
            dq_ref[...] = jnp.zeros_like(dq_ref)

        @pl.when((j == 0) & (hp == 0))
        def _():
            dfq_ref[...] = jnp.zeros_like(dfq_ref)
            dfk_ref[...] = jnp.zeros_like(dfk_ref)

        kb = k_ref[...]
        vb = v_ref[...]
        low = lax.broadcasted_iota(jnp.int32, kb.shape, 1) < FOX_HD
        cum_t = cum_ref[...]

        def rows_of(i):
            return pl.ds(pl.multiple_of(i * tk, tk), tk)

        def add_dq(i, val):
            dq_ref[rows_of(i), :] += val

        def head(a):
            hd = 2 * hp + a
            half = low if a == 0 else jnp.logical_not(low)
            fk = _pick_lane(cum_t, hd)

            def add_dfq(i, val):
                dfq_ref[i] = _put_row(dfq_ref[i], hd, val)

            return dict(q=lambda i: q_ref[rows_of(i), :], do=lambda i: do_ref[rows_of(i), :],
                        k=jnp.where(half, kb, jnp.zeros_like(kb)), v=jnp.where(half, vb, jnp.zeros_like(vb)),
                        rows=lambda i: (_pick_row(lr_ref[i], hd), _pick_row(dr_ref[i], hd)),
                        bias=lambda i: (_pick_row(cr_ref[i], hd), fk), add_dq=add_dq, add_dfq=add_dfq)

        (dk_0, dv_0, dfk_0), (dk_1, dv_1, dfk_1) = _attn_bwd_blocks(j, nk, tk, scale, [head(0), head(1)])
        dk_ref[...] = jnp.where(low, dk_0, dk_1).astype(BF16)
        dv_ref[...] = jnp.where(low, dv_0, dv_1).astype(BF16)
        for a, dfk_a in ((0, dfk_0), (1, dfk_1)):
            dfk_ref[j] = _put_row(dfk_ref[j], 2 * hp + a, jnp.broadcast_to(dfk_a, (tk, 128)).T[0:1, :])

    rowsp = pl.BlockSpec((nk, 16, tk), lambda b, hp, j: (b, 0, 0))
    return _call(
        body, name, (nb, npairs, nk),
        [pl.BlockSpec((s, 128), lambda b, hp, j: (b, hp)),
         pl.BlockSpec((tk, 128), lambda b, hp, j: (b * nk + j, npairs + hp)),
         pl.BlockSpec((tk, 128), lambda b, hp, j: (b * nk + j, 2 * npairs + hp)),
         pl.BlockSpec((s, 128), lambda b, hp, j: (b, hp)),
         pl.BlockSpec((tk, 128), lambda b, hp, j: (b * nk + j, 0)),
         rowsp, rowsp, rowsp],
        [pl.BlockSpec((s, 128), lambda b, hp, j: (b, hp)),
         pl.BlockSpec((tk, 128), lambda b, hp, j: (b * nk + j, hp)),
         pl.BlockSpec((tk, 128), lambda b, hp, j: (b * nk + j, hp)),
         rowsp, rowsp],
        [_sds((t, D_MODEL), F32), _sds((t, D_MODEL), BF16), _sds((t, D_MODEL), BF16),
         _sds((t // tk, 16, tk), F32), _sds((t // tk, 16, tk), F32)],
        (qkv, qkv, qkv, do, cum, cum_rows, lse_rows, delta_rows), hosted=hosted)


def mla_attn_bwd(q, kn, kr2, v, do, lse_rows, delta_rows, nb, name, hosted=None):
    t = q.shape[0]
    s = t // nb
    tk = ATTN_TILE
    nk = s // tk
    npairs = MLA_HEADS // 2
    scale = (MLA_NOPE + MLA_ROPE) ** -0.5

    def body(qn_ref, qr_ref, kn_ref, kr_ref, v_ref, do_ref, lr_ref, dr_ref, dqn_ref, dqr_ref, dkn_ref, dkr_ref, dv_ref):
        hp = pl.program_id(1)
        j = pl.program_id(2)

        @pl.when(j == 0)
        def _():
            dqn_ref[...] = jnp.zeros_like(dqn_ref)
            dqr_ref[...] = jnp.zeros_like(dqr_ref)

        low = lax.broadcasted_iota(jnp.int32, (tk, 128), 1) < MLA_ROPE
        kr = kr_ref[...]

        def rows_of(i):
            return pl.ds(pl.multiple_of(i * tk, tk), tk)

        def head(a):
            cols = slice(a * MLA_NOPE, (a + 1) * MLA_NOPE)
            mine = low if a == 0 else jnp.logical_not(low)

            def q_fn(i):
                qr = qr_ref[rows_of(i), :]
                return jnp.concatenate([qn_ref[rows_of(i), cols], jnp.where(mine, qr, jnp.zeros_like(qr))], axis=1)

            def add_dq(i, val):
                dqn_ref[rows_of(i), cols] += val[:, :MLA_NOPE]
                dqr_ref[rows_of(i), cols] += val[:, MLA_NOPE:]

            return dict(q=q_fn, do=lambda i: do_ref[rows_of(i), cols], k=jnp.concatenate([kn_ref[:, cols], kr], axis=1),
                        v=v_ref[:, cols], rows=lambda i: (_pick_row(lr_ref[i], 2 * hp + a), _pick_row(dr_ref[i], 2 * hp + a)),
                        bias=None, add_dq=add_dq, add_dfq=None)

        (dk_0, dv_0, _), (dk_1, dv_1, _) = _attn_bwd_blocks(j, nk, tk, scale, [head(0), head(1)])
        dkn_ref[...] = jnp.concatenate([dk_0[:, :MLA_NOPE], dk_1[:, :MLA_NOPE]], axis=1).astype(BF16)
        dkr_ref[...] = jnp.concatenate([dk_0[:, MLA_NOPE:], dk_1[:, MLA_NOPE:]], axis=1).astype(BF16)
        dv_ref[...] = jnp.concatenate([dv_0, dv_1], axis=1).astype(BF16)

    wide = 2 * MLA_NOPE
    full = pl.BlockSpec((s, wide), lambda b, hp, j: (b, hp))
    blk = pl.BlockSpec((tk, wide), lambda b, hp, j: (b * nk + j, hp))
    rowsp = pl.BlockSpec((nk, 16, tk), lambda b, hp, j: (b, 0, 0))
    total = MLA_HEADS * MLA_V
    return _call(
        body, name, (nb, npairs, nk),
        [full, pl.BlockSpec((s, 128), lambda b, hp, j: (b, MLA_HEADS + hp)), blk,
         pl.BlockSpec((tk, 128), lambda b, hp, j: (b * nk + j, 0)), blk, full, rowsp, rowsp],
        [full, full, blk, blk, blk],
        [_sds((t, total), F32), _sds((t, total), F32), _sds((t, total), BF16), _sds((t, total), BF16),
         _sds((t, total), BF16)],
        (q, q, kn, kr2, v, do, lse_rows, delta_rows), hosted=hosted)


def mla_mid_bwd(dqn, dqr, dkn, dv, dkr_heads, h, g_q, g_kv, w_uq, w_uk, w_uv, cos8, sin8, cos64, sin64s, swap64,
                heads_to_rope, head_sum, name):
    t = h.shape[0]
    tm = TOKEN_TILE
    hq = MLA_HEADS * MLA_NOPE
    hr = MLA_HEADS * MLA_ROPE // 2
    nq = w_uq.shape[1]

    def body(dqn_ref, dqr_ref, dkn_ref, dv_ref, dkr_ref, h_ref, gq_ref, gkv_ref, wuq_ref, wuk_ref, wuv_ref,
             c8_ref, s8_ref, c64_ref, s64_ref, sw_ref, hp_ref, hs_ref, dh_ref, dqp_ref, dgq_ref, dgkv_ref):
        @pl.when(pl.program_id(0) == 0)
        def _():
            dgq_ref[...] = jnp.zeros_like(dgq_ref)
            dgkv_ref[...] = jnp.zeros_like(dgkv_ref)

        drot = _dot(dqr_ref[...].astype(BF16), hp_ref[...])
        o1 = drot[:, :hr]
        o2 = drot[:, hr:]
        cs = c8_ref[...]
        sn = s8_ref[...]
        dqp = jnp.concatenate([dqn_ref[...].astype(BF16), (o1 * cs + o2 * sn).astype(BF16),
                               (o2 * cs - o1 * sn).astype(BF16)], axis=1)
        dqp_ref[...] = dqp
        dcq = _dot_nt(dqp, wuq_ref[...])
        dckv = _dot_nt(dkn_ref[...], wuk_ref[...]) + _dot_nt(dv_ref[...], wuv_ref[...])
        hh = h_ref[...]

        def rms_bwd(hpart, g, dc, dg_ref):
            hhat, rstd = _rms(hpart, None)
            dg_ref[...] += jnp.sum(dc * hhat, axis=0, keepdims=True)
            dcg = dc * g
            return rstd * (dcg - hhat * jnp.mean(dcg * hhat, axis=-1, keepdims=True))

        dhq = rms_bwd(hh[:, :MLA_QR], gq_ref[...], dcq, dgq_ref)
        dhkv = rms_bwd(hh[:, MLA_QR:MLA_QR + MLA_KVR], gkv_ref[...], dckv, dgkv_ref)
        dkr = _dot(dkr_ref[...], hs_ref[...])
        dkr_pre = dkr * c64_ref[...] + _dot_f32(dkr * s64_ref[...], sw_ref[...])
        dh_ref[...] = jnp.concatenate([dhq, dhkv, dkr_pre], axis=1).astype(BF16)

    def rows(n):
        return pl.BlockSpec((tm, n), lambda i: (i, 0))

    def whole(a):
        return pl.BlockSpec(a.shape, lambda i: (0,) * a.ndim)

    return pl.pallas_call(
        body, name=name, grid=(t // tm,),
        in_specs=[rows(hq), rows(hq), rows(hq), rows(hq), rows(hq), rows(h.shape[1]), whole(g_q), whole(g_kv),
                  whole(w_uq), whole(w_uk), whole(w_uv), rows(hr), rows(hr), rows(MLA_ROPE), rows(MLA_ROPE),
                  whole(swap64), whole(heads_to_rope), whole(head_sum)],
        out_specs=[rows(h.shape[1]), rows(nq), pl.BlockSpec((1, MLA_QR), lambda i: (0, 0)),
                   pl.BlockSpec((1, MLA_KVR), lambda i: (0, 0))],
        out_shape=[_sds((t, h.shape[1]), BF16), _sds((t, nq), BF16), _sds((1, MLA_QR), F32), _sds((1, MLA_KVR), F32)],
        compiler_params=_cp(1),
    )(dqn, dqr, dkn, dv, dkr_heads, h, g_q, g_kv, w_uq, w_uk, w_uv, cos8, sin8, cos64, sin64s, swap64,
      heads_to_rope, head_sum)


def fox_gate_bwd(dcum, hf, b_f, triu, n_batch, name):
    t, n = hf.shape
    blk = triu.shape[0]
    nb = (t // n_batch) // blk

    def body(dc_ref, hf_ref, b_ref, tri_ref, o_ref, db_ref, carry_ref):
        @pl.when(pl.program_id(1) == 0)
        def _():
            carry_ref[...] = jnp.zeros_like(carry_ref)

        @pl.when((pl.program_id(0) == 0) & (pl.program_id(1) == 0))
        def _():
            db_ref[...] = jnp.zeros_like(db_ref)

        rc = _dot_f32(tri_ref[...], dc_ref[...]) + carry_ref[...]
        carry_ref[...] = rc[0:1, :]
        dhf = rc * jax.nn.sigmoid(-(hf_ref[...] + b_ref[...]))
        o_ref[...] = dhf.astype(BF16)
        db_ref[...] += jnp.sum(dhf, axis=0, keepdims=True)

    rev = pl.BlockSpec((blk, n), lambda bb, i: (bb * nb + nb - 1 - i, 0))
    return pl.pallas_call(
        body, name=name, grid=(n_batch, nb),
        in_specs=[rev, rev, pl.BlockSpec((1, n), lambda bb, i: (0, 0)), pl.BlockSpec((blk, blk), lambda bb, i: (0, 0))],
        out_specs=[rev, pl.BlockSpec((1, n), lambda bb, i: (0, 0))],
        out_shape=[_sds((t, n), BF16), _sds((1, n), F32)], scratch_shapes=[pltpu.VMEM((1, n), F32)],
        compiler_params=_cp(2),
    )(dcum, hf, b_f, triu)


def wgrad(a, bm, name, with_bf16=False, bt=WGRAD_TOKENS):
    ca, t, kd = a.shape
    cb, _, nd = bm.shape
    c = max(ca, cb)
    bn = nd
    if nd > 1024 and nd % 1024 == 0:
        bn = 1024
    nsteps = t // bt

    def body(a_ref, b_ref, o_ref, *rest):
        @pl.when(pl.program_id(2) == 0)
        def _():
            o_ref[...] = jnp.zeros_like(o_ref)

        o_ref[...] += _dot_tn(a_ref[...].astype(BF16), b_ref[...].astype(BF16))
        if with_bf16:
            @pl.when(pl.program_id(2) == nsteps - 1)
            def _():
                rest[0][...] = o_ref[...].astype(BF16)

    out_spec = pl.BlockSpec((None, kd, bn), lambda cc, n, tt: (cc, 0, n))
    res = pl.pallas_call(
        body, name=name, grid=(c, nd // bn, nsteps),
        in_specs=[pl.BlockSpec((None, bt, kd), lambda cc, n, tt: (cc if ca > 1 else 0, tt, 0)),
                  pl.BlockSpec((None, bt, bn), lambda cc, n, tt: (cc if cb > 1 else 0, tt, n))],
        out_specs=[out_spec, out_spec] if with_bf16 else out_spec,
        out_shape=[_sds((c, kd, nd), F32), _sds((c, kd, nd), BF16)] if with_bf16 else _sds((c, kd, nd), F32),
        compiler_params=_cp(3),
    )(a, bm)
    return res


def ada_mod_part(c_all, ada_w, name):
    nl, d, n = ada_w.shape
    rows = c_all.shape[0]
    tn = 512

    def body(c_ref, w_ref, o_ref):
        cv = c_ref[...]
        act = (cv * jax.nn.sigmoid(cv)).astype(BF16)
        o_ref[...] = _dot(act, w_ref[...].astype(BF16))

    return pl.pallas_call(
        body, name=name, grid=(nl, n // tn),
        in_specs=[pl.BlockSpec((rows, d), lambda l, j: (0, 0)), pl.BlockSpec((None, d, tn), lambda l, j: (l, 0, j))],
        out_specs=pl.BlockSpec((None, rows, tn), lambda l, j: (l, 0, j)),
        out_shape=_sds((nl, rows, n), F32), compiler_params=_cp(2),
    )(c_all, ada_w)


def ada_grad(c_all_t, dmod, name):
    nl, rows, n = dmod.shape
    d = c_all_t.shape[0]
    tn = 512

    def body(c_ref, dm_ref, o_ref):
        cv = c_ref[...]
        act = (cv * jax.nn.sigmoid(cv)).astype(BF16)
        o_ref[...] = _dot(act, dm_ref[...].astype(BF16))

    return pl.pallas_call(
        body, name=name, grid=(nl, n // tn),
        in_specs=[pl.BlockSpec((d, rows), lambda l, j: (0, 0)), pl.BlockSpec((None, rows, tn), lambda l, j: (l, 0, j))],
        out_specs=pl.BlockSpec((None, d, tn), lambda l, j: (l, 0, j)),
        out_shape=_sds((nl, d, n), F32), compiler_params=_cp(2),
    )(c_all_t, dmod)


def sum_leading(a, name):
    g, r, n = a.shape

    def body(a_ref, o_ref):
        acc = a_ref[0]
        for kk in range(1, g):
            acc = acc + a_ref[kk]
        o_ref[...] = acc

    return pl.pallas_call(
        body, name=name, grid=(1,), in_specs=[pl.BlockSpec((g, r, n), lambda i: (0, 0, 0))],
        out_specs=pl.BlockSpec((r, n), lambda i: (0, 0)), out_shape=_sds((r, n), F32), compiler_params=_cp(1),
    )(a)


def adamw(w, g, m, v, name):
    r, n = w.shape
    br = r
    for cand in (512, 256, 128, 64, 32, 16, 8):
        if r % cand == 0 and r > cand and cand * n * 4 <= ADAMW_BLOCK_BYTES:
            br = cand
            break
    c1 = 1.0 - ADAM_B1 ** ADAM_STEP
    c2 = 1.0 - ADAM_B2 ** ADAM_STEP

    def body(w_ref, g_ref, m_ref, v_ref, d_ref, mo_ref, vo_ref):
        gv = g_ref[...]
        mn = ADAM_B1 * m_ref[...] + (1.0 - ADAM_B1) * gv
        vn = ADAM_B2 * v_ref[...] + (1.0 - ADAM_B2) * (gv * gv)
        mo_ref[...] = mn
        vo_ref[...] = vn
        d_ref[...] = -ADAM_LR * ((mn / c1) / (jnp.sqrt(vn / c2) + ADAM_EPS) + ADAM_WD * w_ref[...])

    spec = pl.BlockSpec((br, n), lambda i: (i, 0))
    return pl.pallas_call(
        body, name=name, grid=(r // br,), in_specs=[spec] * 4, out_specs=[spec] * 3,
        out_shape=[_sds((r, n), F32)] * 3, compiler_params=_cp(1),
    )(w, g, m, v)


def all_gather8(x_blk, name):
    m_per, n = x_blk.shape

    def body(x_ref, out_ref, send_sems, recv_sems, local_sem):
        x, y, c = _place()
        me, sibling = (x, y, c), (x, y, 1 - c)
        chips = [(1 - x, y), (x, 1 - y), (1 - x, 1 - y)]

        def rows(px, py, pc):
            return out_ref.at[pl.ds((4 * px + 2 * py + pc) * m_per, m_per), :]

        def copy(k, block, to, src=None):
            return pltpu.make_async_remote_copy(
                src_ref=rows(*block) if src is None else src, dst_ref=rows(*block),
                send_sem=send_sems.at[k], recv_sem=recv_sems.at[k], device_id=to, device_id_type=MESH)

        mine = pltpu.make_async_copy(x_ref, rows(*me), local_sem)
        mine.start()
        first = [copy(0, me, sibling, src=x_ref)]
        first += [copy(1 + j, me, (*chip, c), src=x_ref) for j, chip in enumerate(chips)]
        for cp in first:
            cp.start()
        passed = [copy(4 + j, (*chip, c), sibling) for j, chip in enumerate(chips)]
        for j, chip in enumerate(chips):
            copy(1 + j, (*chip, c), me).wait_recv()
            passed[j].start()
        copy(0, sibling, me).wait_recv()
        for j, chip in enumerate(chips):
            copy(4 + j, (*chip, 1 - c), me).wait_recv()
        for cp in first + passed:
            cp.wait_send()
        mine.wait()

    return pl.pallas_call(
        body, name=name, out_shape=_sds((8 * m_per, n), x_blk.dtype),
        in_specs=[pl.BlockSpec(memory_space=pltpu.VMEM)], out_specs=pl.BlockSpec(memory_space=pltpu.VMEM),
        scratch_shapes=[pltpu.SemaphoreType.DMA((7,)), pltpu.SemaphoreType.DMA((7,)), pltpu.SemaphoreType.DMA],
        compiler_params=pltpu.CompilerParams(vmem_limit_bytes=VMEM_LIMIT),
    )(x_blk)


def _gather_comm(shards):
    nt = len(shards)

    def parts(w_refs, out_refs, sems, finishing):
        send_sems, recv_sems, own_send, own_recv = sems
        x, y, c = _place()
        sibling = (x, y, 1 - c)
        chips = [(1 - x, y), (x, 1 - y), (1 - x, 1 - y)]

        def copy(t, k, block, to, src=None):
            px, py, hh = block
            dst = out_refs[t].at[2 * px + py, hh]
            return pltpu.make_async_remote_copy(
                src_ref=dst if src is None else src, dst_ref=dst,
                send_sem=send_sems.at[6 * t + k], recv_sem=recv_sems.at[6 * t + k], device_id=to, device_id_type=MESH)

        own = [pltpu.make_async_remote_copy(
            src_ref=w_refs[t], dst_ref=out_refs[t].at[2 * x + y], send_sem=own_send.at[t], recv_sem=own_recv.at[t],
            device_id=sibling, device_id_type=MESH) for t in range(nt)]
        first = [copy(t, j, (x, y, c), (*chip, c), src=w_refs[t].at[c]) for t in range(nt) for j, chip in enumerate(chips)]
        if not finishing:
            return own, first
        landed = [copy(t, j, (*chip, c), (x, y, c)) for t in range(nt) for j, chip in enumerate(chips)]
        passed = [copy(t, 3 + j, (*chip, c), sibling) for t in range(nt) for j, chip in enumerate(chips)]
        from_sibling = [copy(t, 3 + j, (*chip, 1 - c), (x, y, c)) for t in range(nt) for j, chip in enumerate(chips)]
        return own, first, landed, passed, from_sibling

    def start(w_refs, out_refs, sems):
        own, first = parts(w_refs, out_refs, sems, False)
        for cp in own + first:
            cp.start()

    def finish(w_refs, out_refs, sems):
        own, first, landed, passed, from_sibling = parts(w_refs, out_refs, sems, True)
        for arrived, fwd in zip(landed, passed):
            arrived.wait_recv()
            fwd.start()
        for cp in from_sibling:
            cp.wait_recv()
        for cp in first + passed:
            cp.wait_send()
        for cp in own:
            cp.wait()

    sems = [pltpu.SemaphoreType.DMA((6 * nt,)), pltpu.SemaphoreType.DMA((6 * nt,)),
            pltpu.SemaphoreType.DMA((nt,)), pltpu.SemaphoreType.DMA((nt,))]
    return _Hosted(list(shards), [_sds((N_CHIPS, *w.shape), w.dtype) for w in shards], sems, start, finish)


def all_gather_chips(shards, name):
    comm = _gather_comm(shards)
    nt = len(shards)

    def body(*refs):
        comm.start(refs[:nt], refs[nt:2 * nt], refs[2 * nt:])
        comm.finish(refs[:nt], refs[nt:2 * nt], refs[2 * nt:])

    hbm = pl.BlockSpec(memory_space=pl.ANY)
    return pl.pallas_call(body, name=name, out_shape=comm.out_shape, in_specs=[hbm] * nt, out_specs=[hbm] * nt,
                          scratch_shapes=comm.sems)(*shards)


def _row_block(r, n, itemsize):
    best = None
    for br in range(16, r + 1, 16):
        if r % br == 0 and br * n * itemsize <= COMM_BLOCK_BYTES:
            best = br
    assert best is not None, (r, n)
    return best


def _scatter_comm(parts):
    nt = len(parts)

    def copies(p_refs, b_refs, sems, arriving):
        send_sems, recv_sems = sems
        x, y, c = _place()
        me = 4 * x + 2 * y + c
        cps = []
        for t in range(nt):
            for r in range(1, 8):
                tx = 1 - x if r & 4 else x
                ty = 1 - y if r & 2 else y
                tc = 1 - c if r & 1 else c
                src, dst = (2 * x + y, c), 4 * tx + 2 * ty + tc
                if not arriving:
                    src, dst = (2 * tx + ty, tc), me
                cps.append(pltpu.make_async_remote_copy(
                    src_ref=p_refs[t].at[src], dst_ref=b_refs[t].at[dst], send_sem=send_sems.at[7 * t + r - 1],
                    recv_sem=recv_sems.at[7 * t + r - 1], device_id=(tx, ty, tc), device_id_type=MESH))
        return cps

    def start(p_refs, b_refs, sems):
        for cp in copies(p_refs, b_refs, sems, False):
            cp.start()

    def finish(p_refs, b_refs, sems):
        for cp in copies(p_refs, b_refs, sems, True):
            cp.wait_recv()
        for cp in copies(p_refs, b_refs, sems, False):
            cp.wait_send()

    sems = [pltpu.SemaphoreType.DMA((7 * nt,)), pltpu.SemaphoreType.DMA((7 * nt,))]
    return _Hosted(list(parts), [_sds((2 * N_CHIPS, *p.shape[2:]), p.dtype) for p in parts], sems, start, finish)


def scatter_exchange(parts, name):
    comm = _scatter_comm(parts)
    nt = len(parts)

    def body(*refs):
        comm.start(refs[:nt], refs[nt:2 * nt], refs[2 * nt:])
        comm.finish(refs[:nt], refs[nt:2 * nt], refs[2 * nt:])

    hbm = pl.BlockSpec(memory_space=pl.ANY)
    return pl.pallas_call(body, name=name, out_shape=comm.out_shape, in_specs=[hbm] * nt, out_specs=[hbm] * nt,
                          scratch_shapes=comm.sems)(*parts)


def sum_devices(own, recv, place, name, slot=(0, 1, None)):
    _, _, r, n = own.shape
    layer, n_layers, buf = slot
    br = _row_block(r, n, 4 * 8)

    def body(p_ref, o_ref, *rest):
        acc = o_ref[...]
        for kk in range(7):
            acc = acc + rest[kk][...].astype(F32)
        rest[-1][...] = acc

    def arrived(rel):
        return pl.BlockSpec((None, br, n), lambda i, pref: (jnp.bitwise_xor(pref[0], rel), i, 0))

    in_specs = [pl.BlockSpec((None, None, br, n), lambda i, pref: (pref[2], pref[1], i, 0))]
    in_specs += [arrived(rel) for rel in range(1, 8)]
    args = [own] + [recv] * 7
    aliases = {}
    if buf is not None:
        in_specs.append(pl.BlockSpec(memory_space=pl.ANY))
        args.append(buf)
        aliases = {9: 0}
    return pl.pallas_call(
        body, name=name,
        grid_spec=pltpu.PrefetchScalarGridSpec(
            num_scalar_prefetch=1, grid=(r // br,), in_specs=in_specs,
            out_specs=pl.BlockSpec((None, None, br, n), lambda i, pref: (layer, pref[1], i, 0))),
        out_shape=_sds((n_layers, 2, r, n), F32), input_output_aliases=aliases, compiler_params=_cp(1),
    )(place, *args)


def sibling_join_halves(bufs, name):
    nt = len(bufs)
    layers = [bf.shape[0] for bf in bufs]
    first = [sum(layers[:t]) for t in range(nt)]

    def body(*refs):
        o_refs = refs[nt:2 * nt]
        send_sems, recv_sems = refs[2 * nt:]
        x, y, c = _place()

        def copy(t, l, hh):
            return pltpu.make_async_remote_copy(
                src_ref=o_refs[t].at[l, hh], dst_ref=o_refs[t].at[l, hh], send_sem=send_sems.at[first[t] + l],
                recv_sem=recv_sems.at[first[t] + l], device_id=(x, y, 1 - c), device_id_type=MESH)

        cps = [copy(t, l, c) for t in range(nt) for l in range(layers[t])]
        for cp in cps:
            cp.start()
        for t in range(nt):
            for l in range(layers[t]):
                copy(t, l, 1 - c).wait_recv()
        for cp in cps:
            cp.wait_send()

    hbm = pl.BlockSpec(memory_space=pl.ANY)
    return pl.pallas_call(
        body, name=name, out_shape=[_sds(bf.shape, bf.dtype) for bf in bufs],
        in_specs=[hbm] * nt, out_specs=[hbm] * nt, input_output_aliases={t: t for t in range(nt)},
        scratch_shapes=[pltpu.SemaphoreType.DMA((sum(layers),)), pltpu.SemaphoreType.DMA((sum(layers),))],
    )(*bufs)


_SHARD_KIND = {"mla_w_in": "rows", "mla_w_uq": "cols", "mla_w_uk": "cols", "mla_w_uv": "cols", "mla_w_o": "rows",
               "fox_w_in": "cols", "fox_w_o": "rows", "ffn_w_gate": "chunk", "ffn_w_up": "chunk", "ffn_w_down": "chunk"}
_PACKED = tuple(_SHARD_KIND)


def _halves(shard):
    if shard.ndim == 3 and shard.shape[0] == 2:
        return shard
    r, n = shard.shape[-2:]
    return shard.reshape(2, r // 2, n)


def _cols_to_full(g):
    return jnp.transpose(g, (1, 0, 2)).reshape(g.shape[1], -1)


def _full_to_cols(w):
    k, n4 = w.shape
    return jnp.transpose(w.reshape(k, N_CHIPS, n4 // N_CHIPS), (1, 0, 2))


def _uq_perm():
    per = MLA_NOPE + MLA_ROPE
    half = MLA_ROPE // 2
    nope = [h * per + d for h in range(MLA_HEADS) for d in range(MLA_NOPE)]
    r1 = [h * per + MLA_NOPE + r for h in range(MLA_HEADS) for r in range(half)]
    r2 = [h * per + MLA_NOPE + half + r for h in range(MLA_HEADS) for r in range(half)]
    perm = np.array(nope + r1 + r2, dtype=np.int32)
    return perm, np.argsort(perm).astype(np.int32)


def _rope_matrices():
    half = MLA_ROPE // 2
    nr = MLA_HEADS * MLA_ROPE
    to_heads = np.zeros((nr, nr), np.float32)
    from_heads = np.zeros((MLA_HEADS * 128, nr), np.float32)
    for e in range(2):
        for h in range(MLA_HEADS):
            for r in range(half):
                to_heads[e * MLA_HEADS * half + h * half + r, h * MLA_ROPE + e * half + r] = 1.0
                from_heads[h * 128 + e * half + r, e * MLA_HEADS * half + h * half + r] = 1.0
    head_sum = np.tile(np.eye(MLA_ROPE, dtype=np.float32), (2 * MLA_HEADS, 1))
    dup = np.concatenate([np.eye(MLA_ROPE, dtype=np.float32)] * 2, axis=1)
    return to_heads, from_heads, head_sum, dup


def _ffn_weights(gathered):
    return tuple(g.reshape(N_CHIPS, 2 * g.shape[2], g.shape[3]) for g in gathered)


def _fox_weights(gathered):
    w_in, w_o = gathered
    w_in = _cols_to_full(w_in.reshape(N_CHIPS, 2 * w_in.shape[2], w_in.shape[3]))
    return w_in, w_o.reshape(-1, w_o.shape[-1])


def _local_step(x, positions, target, mods, wts, ln_g, ln_b, mla_g_q, mla_g_kv, fox_b_f, shards=None):
    nb, s, d = x.shape
    t = nb * s
    x0 = x.reshape(t, d)
    tgt = target.reshape(t, d)
    perm, inv_perm = _uq_perm()

    half = MLA_ROPE // 2
    inv_freq = ROPE_THETA ** (-jnp.arange(half, dtype=F32) / half)
    ang = positions.astype(F32).reshape(t, 1) * inv_freq
    cos, sin = jnp.cos(ang), jnp.sin(ang)
    cos8, sin8 = jnp.tile(cos, (1, MLA_HEADS)), jnp.tile(sin, (1, MLA_HEADS))
    cos64 = jnp.concatenate([cos, cos], axis=1)
    sin64s = jnp.concatenate([-sin, sin], axis=1)
    swap64 = jnp.asarray(np.roll(np.eye(MLA_ROPE, dtype=np.float32), half, axis=1))
    to_heads, from_heads, head_sum, dup = _rope_matrices()
    to_heads, from_heads = jnp.asarray(to_heads, dtype=BF16), jnp.asarray(from_heads, dtype=BF16)
    head_sum, dup = jnp.asarray(head_sum, dtype=BF16), jnp.asarray(dup, dtype=BF16)
    sel_mla = jnp.asarray(np.pad(np.kron(np.eye(MLA_HEADS, dtype=np.float32), np.ones((MLA_V, 1), np.float32)),
                                 ((0, 0), (0, 128 - MLA_HEADS))))
    sel_fox = jnp.asarray(np.pad(np.kron(np.eye(FOX_HEADS, dtype=np.float32), np.ones((FOX_HD, 1), np.float32)),
                                 ((0, 0), (0, 128 - FOX_HEADS))))
    tri = jnp.asarray(np.tril(np.ones((128, 128), np.float32)))
    triu = jnp.asarray(np.triu(np.ones((128, 128), np.float32)))
    onehot16 = jnp.asarray(np.eye(16, 128, dtype=np.float32))

    def vec(a):
        return a.reshape(1, -1)

    def carried(key):
        return None if shards is None else _gather_comm(shards[key])

    def split(res):
        return (res, None) if shards is None else res

    w_uq_p = wts["mla_w_uq"][:, perm]
    b_f_pad = jnp.pad(fox_b_f.reshape(1, -1), ((0, 0), (0, 128 - FOX_HEADS)))

    sh_a, sc_a, gt_a, sh_f, sc_f, gt_f = mods[0]
    h_in, u_m = mod_linear(x0, sh_a, sc_a, wts["mla_w_in"], F32, "mla_in", emit_u=True)
    q_m, kn_m, v_m, kr2_m, cq_m, ckv_m = mla_mid_fwd(
        h_in, vec(mla_g_q), vec(mla_g_kv), w_uq_p, wts["mla_w_uk"], wts["mla_w_uv"], cos8, sin8, cos64, sin64s, swap64,
        to_heads, dup, "mla_mid")
    (o_m, lse_m), got = split(mla_attn_fwd(q_m, kn_m, kr2_m, v_m, nb, "mla_attn", hosted=carried("ffn0")))
    ffn0_w = wts["ffn"][0] if got is None else _ffn_weights(got)
    y0, x1 = linear_resid_ln(o_m, wts["mla_w_o"], x0, gt_a, vec(ln_g[0, 0]), vec(ln_b[0, 0]), "mla_out")
    (u_f0, hg0, hu0, y1, x2), got = split(ffn_fwd(x1, sh_f, sc_f, gt_f, *ffn0_w, vec(ln_g[0, 1]), vec(ln_b[0, 1]), "ffn0",
                                                  hosted=carried("fox")))
    fox_w_in, fox_w_o = (wts["fox_w_in"], wts["fox_w_o"]) if got is None else _fox_weights(got)
    fox_w_qkv = fox_w_in[:, :3 * d]
    fox_w_f = jnp.pad(fox_w_in[:, 3 * d:], ((0, 0), (0, 128 - FOX_HEADS)))
    sh_a1, sc_a1, gt_a1, sh_f1, sc_f1, gt_f1 = mods[1]
    qkv, u_x = mod_linear(x2, sh_a1, sc_a1, fox_w_qkv, BF16, "fox_qkv", tn=1024, emit_u=True)
    hf = mod_linear(x2, sh_a1, sc_a1, fox_w_f, F32, "fox_f")
    cum = fox_gate_fwd(hf, b_f_pad, tri, nb, "fox_gate")
    cum_rows = rows16(cum, "fox_cum_rows")
    (o_x, lse_x), got = split(fox_attn_fwd(qkv, cum, cum_rows, nb, "fox_attn", hosted=carried("ffn1")))
    ffn1_w = wts["ffn"][1] if got is None else _ffn_weights(got)
    y2, x3 = linear_resid_ln(o_x, fox_w_o, x2, gt_a1, vec(ln_g[1, 0]), vec(ln_b[1, 0]), "fox_out")
    u_f1, hg1, hu1, y3, x4 = ffn_fwd(x3, sh_f1, sc_f1, gt_f1, *ffn1_w, vec(ln_g[1, 1]), vec(ln_b[1, 1]), "ffn1")
    dx4, sq_err = loss_grad(x4, tgt, "loss")
    loss_part = 0.5 * jnp.sum(sq_err) / d

    parts, recv = {}, {}

    def halves_of(g):
        return g.reshape(N_CHIPS, 2, g.shape[1] // 2, g.shape[2])

    def scatter(keys, sent):
        return None if shards is None else _scatter_comm([sent[k] for k in keys])

    def landed(keys, got):
        if got is not None:
            recv.update(zip(keys, got))

    def ffn_grads(layer, u, dhg, dhu, act, dy):
        sent = {}
        for n, (a_op, b_op) in (("ffn_w_gate", (u[None], dhg)), ("ffn_w_up", (u[None], dhu)), ("ffn_w_down", (act, dy[None]))):
            g32, g16 = wgrad(a_op, b_op, "ffn%d_d%s" % (layer, n[4:]), with_bf16=True)
            parts["%s/%d" % (n, layer)], sent["%s/%d" % (n, layer)] = halves_of(g32), halves_of(g16)
        return sent

    dz3, dy3, dg11, db11, dgt_f1 = ln_bwd(dx4, x3, y3, gt_f1, vec(ln_g[1, 1]), "ffn1_ln_bwd")
    dhg1, dhu1, act1, dx3, dsc_f1, dsh_f1 = ffn_bwd(dy3, hg1, hu1, *ffn1_w, dz3, x3, sc_f1, "ffn1_bwd")
    sent = ffn_grads(1, u_f1, dhg1, dhu1, act1, dy3)
    dz2, dy2, dg10, db10, dgt_a1 = ln_bwd(dx3, x2, y2, gt_a1, vec(ln_g[1, 0]), "fox_ln_bwd")
    do_x, delta_x = linear_nt_delta(dy2, fox_w_o, o_x, sel_fox, "fox_out_bwd")
    (dq_x, dk_x, dv_x, dfq_x, dfk_x), got = split(fox_attn_bwd(
        qkv, do_x, cum, cum_rows, rows16(lse_x, "fox_lse_rows"), rows16(delta_x, "fox_delta_rows"), nb, "fox_attn_bwd",
        hosted=scatter(list(sent), sent)))
    landed(list(sent), got)
    dcum = tokens128(dfq_x + dfk_x, onehot16, "fox_dcum")
    dhf, dbf = fox_gate_bwd(dcum, hf, b_f_pad, triu, nb, "fox_gate_bwd")
    dqkv = jnp.concatenate([dq_x.astype(BF16), dk_x, dv_x], axis=1)
    dx2, dsc_a1, dsh_a1 = linear_nt_mod_bwd([(dqkv, fox_w_qkv), (dhf, fox_w_f)], dz2, x2, sc_a1, "fox_in_bwd")
    dw_qkv = wgrad(u_x[None], dqkv[None], "fox_dwqkv")[0]
    dw_f = wgrad(u_x[None], dhf[None], "fox_dwf")[0]
    parts["fox_w_in"] = halves_of(_full_to_cols(jnp.concatenate([dw_qkv, dw_f[:, :FOX_HEADS]], axis=1)))
    parts["fox_w_o"] = wgrad(o_x[None], dy2[None], "fox_dwo")[0].reshape(N_CHIPS, 2, -1, d)
    sent = {k: parts[k].astype(BF16) for k in ("fox_w_in", "fox_w_o")}
    dz1, dy1, dg01, db01, dgt_f0 = ln_bwd(dx2, x1, y1, gt_f, vec(ln_g[0, 1]), "ffn0_ln_bwd")
    (dhg0, dhu0, act0, dx1, dsc_f0, dsh_f0), got = split(ffn_bwd(dy1, hg0, hu0, *ffn0_w, dz1, x1, sc_f, "ffn0_bwd",
                                                                 hosted=scatter(list(sent), sent)))
    landed(list(sent), got)
    sent = ffn_grads(0, u_f0, dhg0, dhu0, act0, dy1)
    dz0, dy0, dg00, db00, dgt_a0 = ln_bwd(dx1, x0, y0, gt_a, vec(ln_g[0, 0]), "mla_ln_bwd")
    do_m, delta_m = linear_nt_delta(dy0, wts["mla_w_o"], o_m, sel_mla, "mla_out_bwd")
    parts["mla_w_o"] = wgrad(o_m[None], dy0[None], "mla_dwo")[0].reshape(N_CHIPS, 2, -1, d)
    sent["mla_w_o"] = parts["mla_w_o"].astype(BF16)
    (dqn_m, dqr_m, dkn_m, dkr_m, dv_m), got = split(mla_attn_bwd(
        q_m, kn_m, kr2_m, v_m, do_m, rows16(lse_m, "mla_lse_rows"), rows16(delta_m, "mla_delta_rows"), nb,
        "mla_attn_bwd", hosted=scatter(list(sent), sent)))
    landed(list(sent), got)
    dh_in, dq_pre, dgq, dgkv = mla_mid_bwd(
        dqn_m, dqr_m, dkn_m, dv_m, dkr_m, h_in, vec(mla_g_q), vec(mla_g_kv), w_uq_p, wts["mla_w_uk"],
        wts["mla_w_uv"], cos8, sin8, cos64, sin64s, swap64, from_heads, head_sum, "mla_mid_bwd")
    parts["mla_w_uq"] = halves_of(_full_to_cols(wgrad(cq_m[None], dq_pre[None], "mla_dwuq")[0][:, inv_perm]))
    parts["mla_w_uk"] = halves_of(_full_to_cols(wgrad(ckv_m[None], dkn_m[None], "mla_dwuk")[0]))
    parts["mla_w_uv"] = halves_of(_full_to_cols(wgrad(ckv_m[None], dv_m[None], "mla_dwuv")[0]))
    parts["mla_w_in"] = wgrad(u_m[None], dh_in[None], "mla_dwin")[0].reshape(N_CHIPS, 2, -1, h_in.shape[1])
    dx0, dsc_a0, dsh_a0 = linear_nt_mod_bwd([(dh_in, wts["mla_w_in"])], dz0, x0, sc_a, "mla_in_bwd")

    dmods = [(dsh_a0, dsc_a0, dgt_a0, dsh_f0, dsc_f0, dgt_f0), (dsh_a1, dsc_a1, dgt_a1, dsh_f1, dsc_f1, dgt_f1)]
    d_ln_g = jnp.stack([jnp.concatenate([dg00, dg01], axis=0), jnp.concatenate([dg10, dg11], axis=0)])
    d_ln_b = jnp.stack([jnp.concatenate([db00, db01], axis=0), jnp.concatenate([db10, db11], axis=0)])
    return loss_part, dx0.reshape(nb, s, d), (parts, recv), dmods, d_ln_g, d_ln_b, dgq, dgkv, dbf[:, :FOX_HEADS]


def _pad_rows(a, rows):
    return jnp.pad(a, ((0, rows - a.shape[0]), (0, 0)))


def kernel(x, c, positions, mla_w_in, mla_g_q, mla_w_uq, mla_g_kv, mla_w_uk, mla_w_uv, mla_w_o, fox_w_in, fox_b_f, fox_w_o, ada_w, ada_b, ffn_w_gate, ffn_w_up, ffn_w_down, ln_g, ln_b, loss_target, m_mla_w_in, m_mla_g_q, m_mla_w_uq, m_mla_g_kv, m_mla_w_uk, m_mla_w_uv, m_mla_w_o, m_fox_w_in, m_fox_b_f, m_fox_w_o, m_ada_w, m_ada_b, m_ffn_w_gate, m_ffn_w_up, m_ffn_w_down, m_ln_g, m_ln_b, v_mla_w_in, v_mla_g_q, v_mla_w_uq, v_mla_g_kv, v_mla_w_uk, v_mla_w_uv, v_mla_w_o, v_fox_w_in, v_fox_b_f, v_fox_w_o, v_ada_w, v_ada_b, v_ffn_w_gate, v_ffn_w_up, v_ffn_w_down, v_ln_g, v_ln_b):
    args = dict(locals())
    nb, s, d = x.shape
    ax, ay, ac = lax.axis_index("x"), lax.axis_index("y"), lax.axis_index("c")
    chip = 2 * ax + ay
    dev = 2 * chip + ac
    n_dev = 2 * N_CHIPS
    n_all = nb * n_dev

    shard_shapes = {n: (args[n].shape if _SHARD_KIND[n] == "chunk" else args[n].shape[1:]) for n in _PACKED}

    def block(n, layer=None):
        w = args[n].reshape(shard_shapes[n]) if layer is None else args[n][layer]
        return _halves(w.astype(BF16))

    mla_names = [n for n in _PACKED if n.startswith("mla")]
    wts = {}
    for n, g in zip(mla_names, all_gather_chips([block(n) for n in mla_names], "gather_mla")):
        g = g.reshape(N_CHIPS, *shard_shapes[n])
        wts[n] = g.reshape(-1, g.shape[-1]) if _SHARD_KIND[n] == "rows" else _cols_to_full(g)
    ffn_names = ("ffn_w_gate", "ffn_w_up", "ffn_w_down")
    shards = {"ffn0": [block(n, 0) for n in ffn_names], "fox": [block("fox_w_in"), block("fox_w_o")],
              "ffn1": [block(n, 1) for n in ffn_names]}

    ln_cols = ln_g.shape[-1]
    ln_blk = jnp.concatenate([ln_g.reshape(2 * DEPTH, ln_cols), ln_b.reshape(2 * DEPTH, ln_cols)], axis=0)
    early = jnp.concatenate([_pad_rows(c, 8), jnp.pad(_pad_rows(ln_blk, 8), ((0, 0), (0, d - ln_cols)))], axis=0)
    early = all_gather8(early, "gather_c_ln").reshape(n_dev, 16, d)
    c_all = early[:, :nb].reshape(n_all, d)
    ln_all = early.reshape(N_CHIPS, 2, 16, d)[:, 0, 8:8 + 4 * DEPTH, :ln_cols]
    ln_all = jnp.transpose(ln_all, (1, 0, 2)).reshape(4 * DEPTH, d)
    ln_g_full = ln_all[:2 * DEPTH].reshape(DEPTH, 2, d)
    ln_b_full = ln_all[2 * DEPTH:].reshape(DEPTH, 2, d)
    mod_part = ada_mod_part(c_all, ada_w, "ada_mod")
    ncol = mod_part.shape[-1]
    mod_g = all_gather8(mod_part.reshape(DEPTH * n_all, ncol), "gather_mod")
    mod_g = mod_g.reshape(N_CHIPS, 2, DEPTH, n_all, ncol)[:, 0]
    mod_full = jnp.transpose(mod_g, (1, 2, 0, 3)).reshape(DEPTH, n_all, N_CHIPS * ncol) + ada_b[:, None, :]
    mod_loc = lax.dynamic_slice_in_dim(mod_full, dev * nb, nb, axis=1)
    mods = [tuple(mod_loc[i, :, k * d:(k + 1) * d].reshape(nb, 1, d) for k in range(6)) for i in range(DEPTH)]

    loss_part, grad_x, (parts, recv), dmods, d_ln_g, d_ln_b, dgq, dgkv, dbf = _local_step(
        x, positions, loss_target, mods, wts, ln_g_full, ln_b_full, mla_g_q[0], mla_g_kv[0], fox_b_f[0], shards)
    loss = lax.psum(loss_part, ("x", "y", "c"))

    dmod_rows = jnp.stack([jnp.concatenate([v_.reshape(nb, d) for v_ in dm], axis=1) for dm in dmods])
    small = jnp.concatenate([
        d_ln_g.reshape(2 * DEPTH, d), d_ln_b.reshape(2 * DEPTH, d),
        jnp.pad(jnp.concatenate([dgq, dgkv, dbf], axis=1), ((0, 0), (0, d - 2 * MLA_QR - FOX_HEADS))),
        dmod_rows.reshape(DEPTH * nb * 6, d)], axis=0)
    n_small = small.shape[0]
    small_rows = -(-n_small // 8) * 8
    small_all = all_gather8(_pad_rows(small, small_rows), "gather_stats").reshape(n_dev, small_rows, d)
    stat_sum = sum_leading(small_all, "sum_stats")
    g_ln_g = lax.dynamic_slice_in_dim(stat_sum[:2 * DEPTH], chip * ln_cols, ln_cols, axis=1).reshape(DEPTH, 2, ln_cols)
    g_ln_b = lax.dynamic_slice_in_dim(stat_sum[2 * DEPTH:4 * DEPTH], chip * ln_cols, ln_cols, axis=1).reshape(DEPTH, 2, ln_cols)
    row = stat_sum[4 * DEPTH]
    g_gq = row[:MLA_QR].reshape(1, MLA_QR)
    g_gkv = row[MLA_QR:2 * MLA_QR].reshape(1, MLA_KVR)
    g_bf = row[2 * MLA_QR:2 * MLA_QR + FOX_HEADS].reshape(1, FOX_HEADS)
    base = 4 * DEPTH + 1
    dmod_all = small_all[:, base:base + DEPTH * nb * 6].reshape(n_dev, DEPTH, nb, 6 * d)
    dmod_all = jnp.transpose(dmod_all, (1, 0, 2, 3)).reshape(DEPTH, n_all, 6 * d)
    g_ada_b = sum_leading(jnp.transpose(dmod_all, (1, 0, 2)), "sum_ada_b")
    dmod_mine = lax.dynamic_slice_in_dim(dmod_all, chip * ncol, ncol, axis=2)
    g_ada_w = ada_grad(c_all.T, dmod_mine, "ada_grad")

    late = [k for k in parts if k not in recv]
    recv.update(zip(late, scatter_exchange([parts[k].astype(BF16) for k in late], "rs_exchange_mla")))
    place = jnp.stack([dev, ac, chip]).astype(jnp.int32)
    bufs = []
    for n in _PACKED:
        if _SHARD_KIND[n] == "chunk":
            buf = None
            for layer in range(DEPTH):
                key = "%s/%d" % (n, layer)
                buf = sum_devices(parts[key], recv[key], place, "rs_sum_%s%d" % (n, layer), slot=(layer, DEPTH, buf))
        else:
            buf = sum_devices(parts[n], recv[n], place, "rs_sum_" + n)
        bufs.append(buf)
    joined = sibling_join_halves(bufs, "rs_join")
    g_big = {n: j.reshape(shard_shapes[n]) for n, j in zip(_PACKED, joined)}

    g_out = {
        "mla_w_in": g_big["mla_w_in"], "mla_g_q": g_gq, "mla_w_uq": g_big["mla_w_uq"], "mla_g_kv": g_gkv,
        "mla_w_uk": g_big["mla_w_uk"], "mla_w_uv": g_big["mla_w_uv"], "mla_w_o": g_big["mla_w_o"],
        "fox_w_in": g_big["fox_w_in"], "fox_b_f": g_bf, "fox_w_o": g_big["fox_w_o"],
        "ada_w": g_ada_w, "ada_b": g_ada_b, "ffn_w_gate": g_big["ffn_w_gate"], "ffn_w_up": g_big["ffn_w_up"],
        "ffn_w_down": g_big["ffn_w_down"], "ln_g": g_ln_g, "ln_b": g_ln_b}
    names = ["mla_w_in", "mla_g_q", "mla_w_uq", "mla_g_kv", "mla_w_uk", "mla_w_uv", "mla_w_o", "fox_w_in", "fox_b_f",
             "fox_w_o", "ada_w", "ada_b", "ffn_w_gate", "ffn_w_up", "ffn_w_down", "ln_g", "ln_b"]
    small_names = ["mla_g_q", "mla_g_kv", "fox_b_f", "ada_b", "ln_g", "ln_b"]
    deltas, new_m, new_v = {}, {}, {}
    for n in names:
        if n in small_names:
            continue
        shp = args[n].shape
        two_d = (-1, shp[-1])
        dl, mn, vn = adamw(args[n].reshape(two_d), g_out[n].reshape(two_d), args["m_" + n].reshape(two_d),
                           args["v_" + n].reshape(two_d), "adamw_" + n)
        deltas[n], new_m[n], new_v[n] = dl.reshape(shp), mn.reshape(shp), vn.reshape(shp)

    def small_pack(prefix, src):
        flat = jnp.concatenate([src[prefix + n].reshape(-1) for n in small_names])
        size = -(-flat.shape[0] // (8 * 128)) * 8 * 128
        return jnp.pad(flat, (0, size - flat.shape[0])).reshape(-1, 128)

    sd, sm, sv = adamw(small_pack("", args), small_pack("", g_out), small_pack("m_", args), small_pack("v_", args),
                       "adamw_small")
    off = 0
    for n in small_names:
        shp = args[n].shape
        size = math.prod(shp)
        deltas[n] = sd.reshape(-1)[off:off + size].reshape(shp)
        new_m[n] = sm.reshape(-1)[off:off + size].reshape(shp)
        new_v[n] = sv.reshape(-1)[off:off + size].reshape(shp)
        off += size

    outs = [loss, grad_x]
    outs += [g_out[n].reshape(args[n].shape) for n in names]
    outs += [deltas[n] for n in names] + [new_m[n] for n in names] + [new_v[n] for n in names]
    return tuple(outs)
```

```python
import functools
import math

import numpy as np
import jax
import jax.numpy as jnp
from jax import lax
from jax.experimental import pallas as pl
from jax.experimental.pallas import tpu as pltpu

F32 = jnp.float32
BF16 = jnp.bfloat16
MESH = pl.DeviceIdType.MESH

D_MODEL = 1024
DEPTH = 2
MLA_HEADS = 8
MLA_NOPE = 128
MLA_ROPE = 64
MLA_V = 128
MLA_QR = 256
MLA_KVR = 256
ROPE_THETA = 10000.0
FOX_HEADS = 16
FOX_HD = 64
D_FF = 2816
N_CHIPS = 4
FF_CHUNK = D_FF // N_CHIPS
ALPHA = (2.0 * DEPTH) ** 0.25
EPS = 1e-5
ADAM_LR = 0.001
ADAM_B1 = 0.9
ADAM_B2 = 0.999
ADAM_EPS = 1e-08
ADAM_WD = 0.01
ADAM_STEP = 10

VMEM_LIMIT = 56 * 1024 * 1024
TOKEN_TILE = 512
WGRAD_TOKENS = 2048
ATTN_TILE = 512
COMM_BLOCK_BYTES = 2 * 1024 * 1024
ADAMW_BLOCK_BYTES = 1024 * 1024


def _cp(n_axes):
    return pltpu.CompilerParams(dimension_semantics=("arbitrary",) * n_axes, vmem_limit_bytes=VMEM_LIMIT)


def _dot(a, b):
    return jnp.dot(a, b, preferred_element_type=F32)


def _dot_nt(a, b):
    return lax.dot_general(a, b, (((1,), (1,)), ((), ())), preferred_element_type=F32)


def _dot_tn(a, b):
    return lax.dot_general(a, b, (((0,), (0,)), ((), ())), preferred_element_type=F32)


def _dot_f32(a, b):
    return jnp.dot(a, b, preferred_element_type=F32, precision=lax.Precision.HIGHEST)


def _sds(shape, dtype):
    return jax.ShapeDtypeStruct(shape, dtype)


def _place():
    return lax.axis_index("x"), lax.axis_index("y"), lax.axis_index("c")


class _Hosted:
    def __init__(self, inputs, out_shape, sems, start, finish):
        self.inputs, self.out_shape, self.sems, self.start, self.finish = inputs, out_shape, sems, start, finish


def _call(body, name, grid, in_specs, out_specs, out_shape, args, scratch_shapes=(), hosted=None):
    in_specs, out_specs, out_shape, scratch_shapes = list(in_specs), list(out_specs), list(out_shape), list(scratch_shapes)
    if hosted is None:
        return pl.pallas_call(body, name=name, grid=grid, in_specs=in_specs, out_specs=out_specs, out_shape=out_shape,
                              scratch_shapes=scratch_shapes, compiler_params=_cp(len(grid)))(*args)
    n_in, n_out, n_scr = len(in_specs), len(out_specs), len(scratch_shapes)
    h_in, h_out = len(hosted.inputs), len(hosted.out_shape)

    def carried(*refs):
        o0 = n_in + h_in
        s0 = o0 + n_out + h_out
        c_in, c_out, c_sem = refs[n_in:o0], refs[o0 + n_out:s0], refs[s0 + n_scr:]
        ids = [pl.program_id(a) for a in range(len(grid))]
        first = functools.reduce(jnp.logical_and, [i == 0 for i in ids])
        last = functools.reduce(jnp.logical_and, [i == g - 1 for i, g in zip(ids, grid)])

        @pl.when(first)
        def _():
            hosted.start(c_in, c_out, c_sem)

        body(*refs[:n_in], *refs[o0:o0 + n_out], *refs[s0:s0 + n_scr])

        @pl.when(last)
        def _():
            hosted.finish(c_in, c_out, c_sem)

    hbm = pl.BlockSpec(memory_space=pl.ANY)
    res = pl.pallas_call(
        carried, name=name, grid=grid, in_specs=in_specs + [hbm] * h_in, out_specs=out_specs + [hbm] * h_out,
        out_shape=out_shape + list(hosted.out_shape), scratch_shapes=scratch_shapes + list(hosted.sems),
        compiler_params=_cp(len(grid)))(*args, *hosted.inputs)
    return res[:n_out], res[n_out:]


def mod_linear(x, shift, scale, w, out_dtype, name, tn=None, emit_u=False):
    t, d = x.shape
    n = w.shape[1]
    tn = n if tn is None else tn
    tm = TOKEN_TILE
    tps = (t // shift.shape[0]) // tm

    def body(x_ref, sh_ref, sc_ref, w_ref, o_ref, *rest):
        u = (x_ref[...] * (1.0 + sc_ref[...]) + sh_ref[...]).astype(BF16)
        o_ref[...] = _dot(u, w_ref[...]).astype(out_dtype)
        if emit_u:
            @pl.when(pl.program_id(1) == 0)
            def _():
                rest[0][...] = u

    vec = pl.BlockSpec((None, 1, d), lambda i, j: (i // tps, 0, 0))
    out_shape = [_sds((t, n), out_dtype)]
    out_specs = [pl.BlockSpec((tm, tn), lambda i, j: (i, j))]
    if emit_u:
        out_shape.append(_sds((t, d), BF16))
        out_specs.append(pl.BlockSpec((tm, d), lambda i, j: (i, 0)))
    res = pl.pallas_call(
        body, name=name, grid=(t // tm, n // tn),
        in_specs=[pl.BlockSpec((tm, d), lambda i, j: (i, 0)), vec, vec,
                  pl.BlockSpec((d, tn), lambda i, j: (0, j))],
        out_specs=out_specs, out_shape=out_shape, compiler_params=_cp(2),
    )(x, shift, scale, w)
    return res if emit_u else res[0]


def _rms(h, g):
    rstd = lax.rsqrt(jnp.mean(h * h, axis=-1, keepdims=True) + EPS)
    return h * rstd, rstd


def mla_mid_fwd(h, g_q, g_kv, w_uq, w_uk, w_uv, cos8, sin8, cos64, sin64s, swap64, rope_to_heads, dup64, name):
    t = h.shape[0]
    tm = TOKEN_TILE
    hq = MLA_HEADS * MLA_NOPE
    hr = MLA_HEADS * MLA_ROPE // 2

    def body(h_ref, gq_ref, gkv_ref, wuq_ref, wuk_ref, wuv_ref, c8_ref, s8_ref, c64_ref, s64_ref, sw_ref, p_ref, d_ref,
             q_ref, kn_ref, v_ref, kr_ref, cq_ref, ckv_ref):
        hh = h_ref[...]
        cq = (_rms(hh[:, :MLA_QR], None)[0] * gq_ref[...]).astype(BF16)
        ckv = (_rms(hh[:, MLA_QR:MLA_QR + MLA_KVR], None)[0] * gkv_ref[...]).astype(BF16)
        cq_ref[...] = cq
        ckv_ref[...] = ckv
        q = _dot(cq, wuq_ref[...])
        x1 = q[:, hq:hq + hr]
        x2 = q[:, hq + hr:]
        cs = c8_ref[...]
        sn = s8_ref[...]
        rot = jnp.concatenate([x1 * cs - x2 * sn, x2 * cs + x1 * sn], axis=1).astype(BF16)
        q_ref[...] = jnp.concatenate([q[:, :hq].astype(BF16), _dot(rot, p_ref[...]).astype(BF16)], axis=1)
        kn_ref[...] = _dot(ckv, wuk_ref[...]).astype(BF16)
        v_ref[...] = _dot(ckv, wuv_ref[...]).astype(BF16)
        kr = hh[:, MLA_QR + MLA_KVR:]
        kr = (kr * c64_ref[...] + _dot_f32(kr, sw_ref[...]) * s64_ref[...]).astype(BF16)
        kr_ref[...] = _dot(kr, d_ref[...]).astype(BF16)

    def rows(n):
        return pl.BlockSpec((tm, n), lambda i: (i, 0))

    def whole(a):
        return pl.BlockSpec(a.shape, lambda i: (0,) * a.ndim)

    nq = w_uq.shape[1]
    return pl.pallas_call(
        body, name=name, grid=(t // tm,),
        in_specs=[rows(h.shape[1]), whole(g_q), whole(g_kv), whole(w_uq), whole(w_uk), whole(w_uv),
                  rows(hr), rows(hr), rows(MLA_ROPE), rows(MLA_ROPE), whole(swap64), whole(rope_to_heads), whole(dup64)],
        out_specs=[rows(nq), rows(hq), rows(hq), rows(2 * MLA_ROPE), rows(MLA_QR), rows(MLA_KVR)],
        out_shape=[_sds((t, nq), BF16), _sds((t, hq), BF16), _sds((t, hq), BF16), _sds((t, 2 * MLA_ROPE), BF16),
                   _sds((t, MLA_QR), BF16), _sds((t, MLA_KVR), BF16)],
        compiler_params=_cp(1),
    )(h, g_q, g_kv, w_uq, w_uk, w_uv, cos8, sin8, cos64, sin64s, swap64, rope_to_heads, dup64)


def _pick_lane(tile, idx):
    lane = lax.broadcasted_iota(jnp.int32, tile.shape, 1)
    return jnp.sum(jnp.where(lane == idx, tile, 0.0), axis=1, keepdims=True)


def _pick_row(tile, idx):
    row = lax.broadcasted_iota(jnp.int32, tile.shape, 0)
    return jnp.sum(jnp.where(row == idx, tile, 0.0), axis=0, keepdims=True)


def _put_lane(tile, idx, col):
    lane = lax.broadcasted_iota(jnp.int32, tile.shape, 1)
    return jnp.where(lane == idx, col, tile)


def _put_row(tile, idx, row):
    r = lax.broadcasted_iota(jnp.int32, tile.shape, 0)
    return tile + jnp.where(r == idx, row, 0.0)


def _causal_softmax_blocks(i, tq, heads):
    def block(j, carry, masked):
        new = []
        for (score_fn, pv_fn, _), (m, l, acc) in zip(heads, carry):
            sc = score_fn(j)
            if masked:
                keep = lax.broadcasted_iota(jnp.int32, sc.shape, 0) >= lax.broadcasted_iota(jnp.int32, sc.shape, 1)
                sc = jnp.where(keep, sc, -1e30)
            m_new = jnp.maximum(m, jnp.max(sc, axis=1, keepdims=True))
            a = jnp.exp(m - m_new)
            p = jnp.exp(sc - m_new)
            new.append((m_new, a * l + jnp.sum(p, axis=1, keepdims=True), a * acc + pv_fn(j, p.astype(BF16))))
        return tuple(new)

    init = tuple((jnp.full((tq, 1), -1e30, F32), jnp.zeros((tq, 1), F32), jnp.zeros((tq, dv), F32)) for _, _, dv in heads)
    carry = lax.fori_loop(0, i, lambda j, c: block(j, c, False), init)
    return [(acc / l, m + jnp.log(l)) for m, l, acc in block(i, carry, True)]


def fox_attn_fwd(qkv, cum, cum_rows, nb, name, hosted=None):
    t = qkv.shape[0]
    s = t // nb
    tq = ATTN_TILE
    nq = s // tq
    npairs = FOX_HEADS // 2
    scale = FOX_HD ** -0.5

    def body(q_ref, k_ref, v_ref, cum_ref, cr_ref, o_ref, lse_ref):
        i = pl.program_id(1)
        hp = pl.program_id(2)

        @pl.when(hp == 0)
        def _():
            lse_ref[...] = jnp.zeros_like(lse_ref)

        q = q_ref[...]
        low = lax.broadcasted_iota(jnp.int32, q.shape, 1) < FOX_HD
        cum_t = cum_ref[...]

        def rows_of(j):
            return pl.ds(pl.multiple_of(j * tq, tq), tq)

        def head(a):
            hd = 2 * hp + a
            qa = jnp.where(low if a == 0 else jnp.logical_not(low), q, jnp.zeros_like(q))
            fq = _pick_lane(cum_t, hd)
            return (lambda j: _dot_nt(qa, k_ref[rows_of(j), :]) * scale + fq - _pick_row(cr_ref[j], hd),
                    lambda j, p: _dot(p, v_ref[rows_of(j), :]), 2 * FOX_HD)

        (o_0, lse_0), (o_1, lse_1) = _causal_softmax_blocks(i, tq, [head(0), head(1)])
        o_ref[...] = jnp.where(low, o_0, o_1).astype(BF16)
        lse_ref[...] = _put_lane(_put_lane(lse_ref[...], 2 * hp, lse_0), 2 * hp + 1, lse_1)

    return _call(
        body, name, (nb, nq, npairs),
        [pl.BlockSpec((tq, 128), lambda b, i, hp: (b * nq + i, hp)),
         pl.BlockSpec((s, 128), lambda b, i, hp: (b, npairs + hp)),
         pl.BlockSpec((s, 128), lambda b, i, hp: (b, 2 * npairs + hp)),
         pl.BlockSpec((tq, 128), lambda b, i, hp: (b * nq + i, 0)),
         pl.BlockSpec((nq, 16, tq), lambda b, i, hp: (b, 0, 0))],
        [pl.BlockSpec((tq, 128), lambda b, i, hp: (b * nq + i, hp)),
         pl.BlockSpec((tq, 128), lambda b, i, hp: (b * nq + i, 0))],
        [_sds((t, D_MODEL), BF16), _sds((t, 128), F32)], (qkv, qkv, qkv, cum, cum_rows), hosted=hosted)


def mla_attn_fwd(q, kn, kr2, v, nb, name, hosted=None):
    t = q.shape[0]
    s = t // nb
    tq = ATTN_TILE
    nq = s // tq
    npairs = MLA_HEADS // 2
    scale = (MLA_NOPE + MLA_ROPE) ** -0.5

    def body(qn_ref, qr_ref, kn_ref, kr_ref, v_ref, o_ref, lse_ref):
        i = pl.program_id(1)
        hp = pl.program_id(2)

        @pl.when(hp == 0)
        def _():
            lse_ref[...] = jnp.zeros_like(lse_ref)

        qr = qr_ref[...]
        low = lax.broadcasted_iota(jnp.int32, qr.shape, 1) < MLA_ROPE

        def rows_of(j):
            return pl.ds(pl.multiple_of(j * tq, tq), tq)

        def head(a):
            cols = slice(a * MLA_NOPE, (a + 1) * MLA_NOPE)
            q_cat = jnp.concatenate([qn_ref[:, cols], jnp.where(low if a == 0 else jnp.logical_not(low), qr,
                                                                jnp.zeros_like(qr))], axis=1)
            return (lambda j: _dot_nt(q_cat, jnp.concatenate([kn_ref[rows_of(j), cols], kr_ref[rows_of(j), :]], axis=1)) * scale,
                    lambda j, p: _dot(p, v_ref[rows_of(j), cols]), MLA_V)

        (o_0, lse_0), (o_1, lse_1) = _causal_softmax_blocks(i, tq, [head(0), head(1)])
        o_ref[...] = jnp.concatenate([o_0, o_1], axis=1).astype(BF16)
        lse_ref[...] = _put_lane(_put_lane(lse_ref[...], 2 * hp, lse_0), 2 * hp + 1, lse_1)

    wide = 2 * MLA_NOPE
    return _call(
        body, name, (nb, nq, npairs),
        [pl.BlockSpec((tq, wide), lambda b, i, hp: (b * nq + i, hp)),
         pl.BlockSpec((tq, 128), lambda b, i, hp: (b * nq + i, MLA_HEADS + hp)),
         pl.BlockSpec((s, wide), lambda b, i, hp: (b, hp)),
         pl.BlockSpec((s, 128), lambda b, i, hp: (b, 0)),
         pl.BlockSpec((s, wide), lambda b, i, hp: (b, hp))],
        [pl.BlockSpec((tq, wide), lambda b, i, hp: (b * nq + i, hp)),
         pl.BlockSpec((tq, 128), lambda b, i, hp: (b * nq + i, 0))],
        [_sds((t, MLA_HEADS * MLA_V), BF16), _sds((t, 128), F32)], (q, q, kn, kr2, v), hosted=hosted)


def rows16(a, name):
    t = a.shape[0]
    tq = ATTN_TILE

    def body(a_ref, o_ref):
        o_ref[...] = a_ref[...].T[:16, :]

    return pl.pallas_call(
        body, name=name, grid=(t // tq,), in_specs=[pl.BlockSpec((tq, 128), lambda n: (n, 0))],
        out_specs=pl.BlockSpec((None, 16, tq), lambda n: (n, 0, 0)), out_shape=_sds((t // tq, 16, tq), F32),
        compiler_params=_cp(1),
    )(a)


def tokens128(rows, onehot, name):
    nblk, _, tq = rows.shape

    def body(r_ref, e_ref, o_ref):
        o_ref[...] = lax.dot_general(r_ref[...], e_ref[...], (((0,), (0,)), ((), ())), preferred_element_type=F32,
                                     precision=lax.Precision.HIGHEST)

    return pl.pallas_call(
        body, name=name, grid=(nblk,),
        in_specs=[pl.BlockSpec((None, 16, tq), lambda n: (n, 0, 0)), pl.BlockSpec((16, 128), lambda n: (0, 0))],
        out_specs=pl.BlockSpec((tq, 128), lambda n: (n, 0)), out_shape=_sds((nblk * tq, 128), F32),
        compiler_params=_cp(1),
    )(rows, onehot)


def _layer_norm(z, g, b):
    mu = jnp.mean(z, axis=-1, keepdims=True)
    zc = z - mu
    rstd = lax.rsqrt(jnp.mean(zc * zc, axis=-1, keepdims=True) + EPS)
    xhat = zc * rstd
    return xhat * g + b, xhat, rstd


def linear_resid_ln(a, w, x_in, gate, ln_g, ln_b, name):
    t, kdim = a.shape
    d = w.shape[1]
    tm = TOKEN_TILE
    tps = (t // gate.shape[0]) // tm

    def body(a_ref, w_ref, x_ref, gt_ref, g_ref, b_ref, y_ref, xo_ref):
        y = _dot(a_ref[...], w_ref[...])
        y_ref[...] = y
        z = ALPHA * x_ref[...] + (1.0 + gt_ref[...]) * y
        xo_ref[...] = _layer_norm(z, g_ref[...], b_ref[...])[0]

    rows = pl.BlockSpec((tm, d), lambda i: (i, 0))
    vec = pl.BlockSpec((1, d), lambda i: (0, 0))
    return pl.pallas_call(
        body, name=name, grid=(t // tm,),
        in_specs=[pl.BlockSpec((tm, kdim), lambda i: (i, 0)), pl.BlockSpec((kdim, d), lambda i: (0, 0)), rows,
                  pl.BlockSpec((None, 1, d), lambda i: (i // tps, 0, 0)), vec, vec],
        out_specs=[rows, rows], out_shape=[_sds((t, d), F32), _sds((t, d), F32)],
        compiler_params=_cp(1),
    )(a, w, x_in, gate, ln_g, ln_b)


def ffn_fwd(x_in, shift, scale, gate, wg, wu, wd, ln_g, ln_b, name, hosted=None):
    t, d = x_in.shape
    c, _, fc = wg.shape
    tm = TOKEN_TILE
    tps = (t // gate.shape[0]) // tm

    def body(x_ref, sh_ref, sc_ref, gt_ref, wg_ref, wu_ref, wd_ref, g_ref, b_ref,
             u_ref, hg_ref, hu_ref, y_ref, xo_ref, acc_ref):
        cc = pl.program_id(1)

        @pl.when(cc == 0)
        def _():
            u_ref[...] = (x_ref[...] * (1.0 + sc_ref[...]) + sh_ref[...]).astype(BF16)
            acc_ref[...] = jnp.zeros_like(acc_ref)

        u = u_ref[...]
        hg = _dot(u, wg_ref[...])
        hu = _dot(u, wu_ref[...])
        hg_ref[...] = hg.astype(BF16)
        hu_ref[...] = hu.astype(BF16)
        act = (hg * jax.nn.sigmoid(hg) * hu).astype(BF16)
        acc_ref[...] += _dot(act, wd_ref[...])

        @pl.when(cc == c - 1)
        def _():
            y = acc_ref[...]
            y_ref[...] = y
            z = ALPHA * x_ref[...] + (1.0 + gt_ref[...]) * y
            xo_ref[...] = _layer_norm(z, g_ref[...], b_ref[...])[0]

    rows = pl.BlockSpec((tm, d), lambda i, cc: (i, 0))
    bvec = pl.BlockSpec((None, 1, d), lambda i, cc: (i // tps, 0, 0))
    vec = pl.BlockSpec((1, d), lambda i, cc: (0, 0))
    hspec = pl.BlockSpec((None, tm, fc), lambda i, cc: (cc, i, 0))
    wcol = pl.BlockSpec((None, d, fc), lambda i, cc: (cc, 0, 0))
    return _call(
        body, name, (t // tm, c),
        [rows, bvec, bvec, bvec, wcol, wcol, pl.BlockSpec((None, fc, d), lambda i, cc: (cc, 0, 0)), vec, vec],
        [rows, hspec, hspec, rows, rows],
        [_sds((t, d), BF16), _sds((c, t, fc), BF16), _sds((c, t, fc), BF16), _sds((t, d), F32), _sds((t, d), F32)],
        (x_in, shift, scale, gate, wg, wu, wd, ln_g, ln_b), scratch_shapes=[pltpu.VMEM((tm, d), F32)], hosted=hosted)


def fox_gate_fwd(hf, b_f, tri, n_batch, name):
    t, n = hf.shape
    blk = tri.shape[0]
    nb = (t // n_batch) // blk

    def body(hf_ref, b_ref, tri_ref, o_ref, carry_ref):
        @pl.when(pl.program_id(1) == 0)
        def _():
            carry_ref[...] = jnp.zeros_like(carry_ref)

        xx = hf_ref[...] + b_ref[...]
        lf = jnp.minimum(xx, 0.0) - jnp.log(1.0 + jnp.exp(-jnp.abs(xx)))
        cum = _dot_f32(tri_ref[...], lf) + carry_ref[...]
        o_ref[...] = cum
        carry_ref[...] = cum[blk - 1:blk, :]

    return pl.pallas_call(
        body, name=name, grid=(n_batch, nb),
        in_specs=[pl.BlockSpec((blk, n), lambda bb, i: (bb * nb + i, 0)), pl.BlockSpec((1, n), lambda bb, i: (0, 0)),
                  pl.BlockSpec((blk, blk), lambda bb, i: (0, 0))],
        out_specs=pl.BlockSpec((blk, n), lambda bb, i: (bb * nb + i, 0)),
        out_shape=_sds((t, n), F32), scratch_shapes=[pltpu.VMEM((1, n), F32)],
        compiler_params=_cp(2),
    )(hf, b_f, tri)


def loss_grad(x_out, target, name):
    t, d = x_out.shape
    tm = TOKEN_TILE

    def body(x_ref, t_ref, g_ref, l_ref):
        @pl.when(pl.program_id(0) == 0)
        def _():
            l_ref[...] = jnp.zeros_like(l_ref)

        err = x_ref[...] - t_ref[...]
        g_ref[...] = err / d
        l_ref[...] += jnp.sum(err * err, axis=0, keepdims=True)

    rows = pl.BlockSpec((tm, d), lambda i: (i, 0))
    return pl.pallas_call(
        body, name=name, grid=(t // tm,), in_specs=[rows, rows],
        out_specs=[rows, pl.BlockSpec((1, d), lambda i: (0, 0))],
        out_shape=[_sds((t, d), F32), _sds((1, d), F32)], compiler_params=_cp(1),
    )(x_out, target)


def ln_bwd(dxo, x_in, y, gate, ln_g, name):
    t, d = dxo.shape
    nb = gate.shape[0]
    tm = TOKEN_TILE
    tps = (t // nb) // tm

    def body(dxo_ref, x_ref, y_ref, gt_ref, g_ref, dz_ref, dy_ref, dg_ref, db_ref, dgt_ref):
        i = pl.program_id(0)

        @pl.when(i == 0)
        def _():
            dg_ref[...] = jnp.zeros_like(dg_ref)
            db_ref[...] = jnp.zeros_like(db_ref)

        @pl.when(i % tps == 0)
        def _():
            dgt_ref[...] = jnp.zeros_like(dgt_ref)

        yy = y_ref[...]
        g1 = 1.0 + gt_ref[...]
        z = ALPHA * x_ref[...] + g1 * yy
        _, xhat, rstd = _layer_norm(z, 1.0, 0.0)
        dxo_v = dxo_ref[...]
        dg_ref[...] += jnp.sum(dxo_v * xhat, axis=0, keepdims=True)
        db_ref[...] += jnp.sum(dxo_v, axis=0, keepdims=True)
        dxh = dxo_v * g_ref[...]
        dz = rstd * (dxh - jnp.mean(dxh, axis=-1, keepdims=True) - xhat * jnp.mean(dxh * xhat, axis=-1, keepdims=True))
        dz_ref[...] = dz
        dy_ref[...] = (g1 * dz).astype(BF16)
        dgt_ref[...] += jnp.sum(dz * yy, axis=0, keepdims=True)

    rows = pl.BlockSpec((tm, d), lambda i: (i, 0))
    vec = pl.BlockSpec((1, d), lambda i: (0, 0))
    bvec = pl.BlockSpec((None, 1, d), lambda i: (i // tps, 0, 0))
    return pl.pallas_call(
        body, name=name, grid=(t // tm,), in_specs=[rows, rows, rows, bvec, vec],
        out_specs=[rows, rows, vec, vec, bvec],
        out_shape=[_sds((t, d), F32), _sds((t, d), BF16), _sds((1, d), F32), _sds((1, d), F32), _sds((nb, 1, d), F32)],
        compiler_params=_cp(1),
    )(dxo, x_in, y, gate, ln_g)


def _mod_bwd_tail(du, dz_ref, x_ref, sc_ref, dx_ref, dsc_ref, dsh_ref, first):
    @pl.when(first)
    def _():
        dsc_ref[...] = jnp.zeros_like(dsc_ref)
        dsh_ref[...] = jnp.zeros_like(dsh_ref)

    dx_ref[...] = ALPHA * dz_ref[...] + du * (1.0 + sc_ref[...])
    dsc_ref[...] += jnp.sum(du * x_ref[...], axis=0, keepdims=True)
    dsh_ref[...] += jnp.sum(du, axis=0, keepdims=True)


def ffn_bwd(dy, hg, hu, wg, wu, wd, dz, x_in, scale, name, hosted=None):
    t, d = dy.shape
    c, _, fc = wg.shape
    nb = scale.shape[0]
    tm = TOKEN_TILE
    tps = (t // nb) // tm

    def body(dy_ref, hg_ref, hu_ref, wg_ref, wu_ref, wd_ref, dz_ref, x_ref, sc_ref,
             dhg_ref, dhu_ref, act_ref, dx_ref, dsc_ref, dsh_ref, acc_ref):
        i = pl.program_id(0)
        cc = pl.program_id(1)

        @pl.when(cc == 0)
        def _():
            acc_ref[...] = jnp.zeros_like(acc_ref)

        hgv = hg_ref[...].astype(F32)
        huv = hu_ref[...].astype(F32)
        da = _dot_nt(dy_ref[...], wd_ref[...])
        sg = jax.nn.sigmoid(hgv)
        sl = hgv * sg
        act_ref[...] = (sl * huv).astype(BF16)
        dhu = (da * sl).astype(BF16)
        dhg = (da * huv * (sg * (1.0 + hgv * (1.0 - sg)))).astype(BF16)
        dhu_ref[...] = dhu
        dhg_ref[...] = dhg
        acc_ref[...] += _dot_nt(dhg, wg_ref[...]) + _dot_nt(dhu, wu_ref[...])

        @pl.when(cc == c - 1)
        def _():
            _mod_bwd_tail(acc_ref[...], dz_ref, x_ref, sc_ref, dx_ref, dsc_ref, dsh_ref, i % tps == 0)

    rows = pl.BlockSpec((tm, d), lambda i, cc: (i, 0))
    bvec = pl.BlockSpec((None, 1, d), lambda i, cc: (i // tps, 0, 0))
    hspec = pl.BlockSpec((None, tm, fc), lambda i, cc: (cc, i, 0))
    wcol = pl.BlockSpec((None, d, fc), lambda i, cc: (cc, 0, 0))
    return _call(
        body, name, (t // tm, c),
        [rows, hspec, hspec, wcol, wcol, pl.BlockSpec((None, fc, d), lambda i, cc: (cc, 0, 0)), rows, rows, bvec],
        [hspec, hspec, hspec, rows, bvec, bvec],
        [_sds((c, t, fc), BF16), _sds((c, t, fc), BF16), _sds((c, t, fc), BF16), _sds((t, d), F32),
         _sds((nb, 1, d), F32), _sds((nb, 1, d), F32)],
        (dy, hg, hu, wg, wu, wd, dz, x_in, scale), scratch_shapes=[pltpu.VMEM((tm, d), F32)], hosted=hosted)


def linear_nt_mod_bwd(pairs, dz, x_in, scale, name):
    t, d = dz.shape
    nb = scale.shape[0]
    tm = TOKEN_TILE
    tps = (t // nb) // tm
    npairs = len(pairs)

    def body(*refs):
        dh_refs = refs[:npairs]
        w_refs = refs[npairs:2 * npairs]
        dz_ref, x_ref, sc_ref, dx_ref, dsc_ref, dsh_ref = refs[2 * npairs:]
        du = _dot_nt(dh_refs[0][...].astype(BF16), w_refs[0][...])
        for kk in range(1, npairs):
            du = du + _dot_nt(dh_refs[kk][...].astype(BF16), w_refs[kk][...])
        _mod_bwd_tail(du, dz_ref, x_ref, sc_ref, dx_ref, dsc_ref, dsh_ref, pl.program_id(0) % tps == 0)

    rows = pl.BlockSpec((tm, d), lambda i: (i, 0))
    bvec = pl.BlockSpec((None, 1, d), lambda i: (i // tps, 0, 0))
    in_specs = [pl.BlockSpec((tm, dh.shape[1]), lambda i: (i, 0)) for dh, _ in pairs]
    in_specs += [pl.BlockSpec(w.shape, lambda i: (0, 0)) for _, w in pairs]
    in_specs += [rows, rows, bvec]
    return pl.pallas_call(
        body, name=name, grid=(t // tm,), in_specs=in_specs,
        out_specs=[rows, bvec, bvec],
        out_shape=[_sds((t, d), F32), _sds((nb, 1, d), F32), _sds((nb, 1, d), F32)],
        compiler_params=_cp(1),
    )(*[dh for dh, _ in pairs], *[w for _, w in pairs], dz, x_in, scale)


def linear_nt_delta(dy, w_o, o, head_sel, name):
    t, d = dy.shape
    hdv = w_o.shape[0]
    tm = TOKEN_TILE

    def body(dy_ref, w_ref, o_ref, sel_ref, do_ref, dl_ref):
        do = _dot_nt(dy_ref[...], w_ref[...])
        do_ref[...] = do.astype(BF16)
        dl_ref[...] = _dot_f32(do * o_ref[...].astype(F32), sel_ref[...])

    return pl.pallas_call(
        body, name=name, grid=(t // tm,),
        in_specs=[pl.BlockSpec((tm, d), lambda i: (i, 0)), pl.BlockSpec((hdv, d), lambda i: (0, 0)),
                  pl.BlockSpec((tm, hdv), lambda i: (i, 0)), pl.BlockSpec(head_sel.shape, lambda i: (0, 0))],
        out_specs=[pl.BlockSpec((tm, hdv), lambda i: (i, 0)), pl.BlockSpec((tm, 128), lambda i: (i, 0))],
        out_shape=[_sds((t, hdv), BF16), _sds((t, 128), F32)], compiler_params=_cp(1),
    )(dy, w_o, o, head_sel)


def _attn_bwd_blocks(j, nk, tk, scale, heads):
    def block(i, carry, masked):
        new = []
        for hd, (dk_acc, dv_acc, dfk_acc) in zip(heads, carry):
            qb = hd["q"](i)
            dob = hd["do"](i)
            lse_row, dl_row = hd["rows"](i)
            st = _dot_nt(hd["k"], qb) * scale
            if hd["bias"] is not None:
                fq_row, fk_col = hd["bias"](i)
                st = st + fq_row - fk_col
            if masked:
                keep = lax.broadcasted_iota(jnp.int32, st.shape, 1) >= lax.broadcasted_iota(jnp.int32, st.shape, 0)
                st = jnp.where(keep, st, -1e30)
            pt = jnp.exp(st - lse_row)
            dv_acc = dv_acc + _dot(pt.astype(BF16), dob)
            dst = pt * (_dot_nt(hd["v"], dob) - dl_row)
            if hd["add_dfq"] is not None:
                dfk_acc = dfk_acc - jnp.sum(dst, axis=1, keepdims=True)
                hd["add_dfq"](i, jnp.sum(dst, axis=0, keepdims=True))
            dsb = (dst * scale).astype(BF16)
            dk_acc = dk_acc + _dot(dsb, qb)
            hd["add_dq"](i, _dot_tn(dsb, hd["k"]))
            new.append((dk_acc, dv_acc, dfk_acc))
        return tuple(new)

    init = tuple((jnp.zeros((tk, hd["k"].shape[1]), F32), jnp.zeros((tk, hd["v"].shape[1]), F32), jnp.zeros((tk, 1), F32))
                 for hd in heads)
    carry = block(j, init, True)
    return lax.fori_loop(j + 1, nk, lambda i, c: block(i, c, False), carry)


def fox_attn_bwd(qkv, do, cum, cum_rows, lse_rows, delta_rows, nb, name, hosted=None):
    t = qkv.shape[0]
    s = t // nb
    tk = ATTN_TILE
    nk = s // tk
    npairs = FOX_HEADS // 2
    scale = FOX_HD ** -0.5

    def body(q_ref, k_ref, v_ref, do_ref, cum_ref, cr_ref, lr_ref, dr_ref, dq_ref, dk_ref, dv_ref, dfq_ref, dfk_ref):
        hp = pl.program_id(1)
        j = pl.program_id(2)

        @pl.when(j == 0)
        def _():
            dq_ref[...] = jnp.zeros_like(dq_ref)

        @pl.when((j == 0) & (hp == 0))
        def _():
            dfq_ref[...] = jnp.zeros_like(dfq_ref)
            dfk_ref[...] = jnp.zeros_like(dfk_ref)

        kb = k_ref[...]
        vb = v_ref[...]
        low = lax.broadcasted_iota(jnp.int32, kb.shape, 1) < FOX_HD
        cum_t = cum_ref[...]

        def rows_of(i):
            return pl.ds(pl.multiple_of(i * tk, tk), tk)

        def add_dq(i, val):
            dq_ref[rows_of(i), :] += val

        def head(a):
            hd = 2 * hp + a
            half = low if a == 0 else jnp.logical_not(low)
            fk = _pick_lane(cum_t, hd)

            def add_dfq(i, val):
                dfq_ref[i] = _put_row(dfq_ref[i], hd, val)

            return dict(q=lambda i: q_ref[rows_of(i), :], do=lambda i: do_ref[rows_of(i), :],
                        k=jnp.where(half, kb, jnp.zeros_like(kb)), v=jnp.where(half, vb, jnp.zeros_like(vb)),
                        rows=lambda i: (_pick_row(lr_ref[i], hd), _pick_row(dr_ref[i], hd)),
                        bias=lambda i: (_pick_row(cr_ref[i], hd), fk), add_dq=add_dq, add_dfq=add_dfq)

        (dk_0, dv_0, dfk_0), (dk_1, dv_1, dfk_1) = _attn_bwd_blocks(j, nk, tk, scale, [head(0), head(1)])
        dk_ref[...] = jnp.where(low, dk_0, dk_1).astype(BF16)
        dv_ref[...] = jnp.where(low, dv_0, dv_1).astype(BF16)
        for a, dfk_a in ((0, dfk_0), (1, dfk_1)):
            dfk_ref[j] = _put_row(dfk_ref[j], 2 * hp + a, jnp.broadcast_to(dfk_a, (tk, 128)).T[0:1, :])

    rowsp = pl.BlockSpec((nk, 16, tk), lambda b, hp, j: (b, 0, 0))
    return _call(
        body, name, (nb, npairs, nk),
        [pl.BlockSpec((s, 128), lambda b, hp, j: (b, hp)),
         pl.BlockSpec((tk, 128), lambda b, hp, j: (b * nk + j, npairs + hp)),
         pl.BlockSpec((tk, 128), lambda b, hp, j: (b * nk + j, 2 * npairs + hp)),
         pl.BlockSpec((s, 128), lambda b, hp, j: (b, hp)),
         pl.BlockSpec((tk, 128), lambda b, hp, j: (b * nk + j, 0)),
         rowsp, rowsp, rowsp],
        [pl.BlockSpec((s, 128), lambda b, hp, j: (b, hp)),
         pl.BlockSpec((tk, 128), lambda b, hp, j: (b * nk + j, hp)),
         pl.BlockSpec((tk, 128), lambda b, hp, j: (b * nk + j, hp)),
         rowsp, rowsp],
        [_sds((t, D_MODEL), F32), _sds((t, D_MODEL), BF16), _sds((t, D_MODEL), BF16),
         _sds((t // tk, 16, tk), F32), _sds((t // tk, 16, tk), F32)],
        (qkv, qkv, qkv, do, cum, cum_rows, lse_rows, delta_rows), hosted=hosted)


def mla_attn_bwd(q, kn, kr2, v, do, lse_rows, delta_rows, nb, name, hosted=None):
    t = q.shape[0]
    s = t // nb
    tk = ATTN_TILE
    nk = s // tk
    npairs = MLA_HEADS // 2
    scale = (MLA_NOPE + MLA_ROPE) ** -0.5

    def body(qn_ref, qr_ref, kn_ref, kr_ref, v_ref, do_ref, lr_ref, dr_ref, dqn_ref, dqr_ref, dkn_ref, dkr_ref, dv_ref):
        hp = pl.program_id(1)
        j = pl.program_id(2)

        @pl.when(j == 0)
        def _():
            dqn_ref[...] = jnp.zeros_like(dqn_ref)
            dqr_ref[...] = jnp.zeros_like(dqr_ref)

        low = lax.broadcasted_iota(jnp.int32, (tk, 128), 1) < MLA_ROPE
        kr = kr_ref[...]

        def rows_of(i):
            return pl.ds(pl.multiple_of(i * tk, tk), tk)

        def head(a):
            cols = slice(a * MLA_NOPE, (a + 1) * MLA_NOPE)
            mine = low if a == 0 else jnp.logical_not(low)

            def q_fn(i):
                qr = qr_ref[rows_of(i), :]
                return jnp.concatenate([qn_ref[rows_of(i), cols], jnp.where(mine, qr, jnp.zeros_like(qr))], axis=1)

            def add_dq(i, val):
                dqn_ref[rows_of(i), cols] += val[:, :MLA_NOPE]
                dqr_ref[rows_of(i), cols] += val[:, MLA_NOPE:]

            return dict(q=q_fn, do=lambda i: do_ref[rows_of(i), cols], k=jnp.concatenate([kn_ref[:, cols], kr], axis=1),
                        v=v_ref[:, cols], rows=lambda i: (_pick_row(lr_ref[i], 2 * hp + a), _pick_row(dr_ref[i], 2 * hp + a)),
                        bias=None, add_dq=add_dq, add_dfq=None)

        (dk_0, dv_0, _), (dk_1, dv_1, _) = _attn_bwd_blocks(j, nk, tk, scale, [head(0), head(1)])
        dkn_ref[...] = jnp.concatenate([dk_0[:, :MLA_NOPE], dk_1[:, :MLA_NOPE]], axis=1).astype(BF16)
        dkr_ref[...] = jnp.concatenate([dk_0[:, MLA_NOPE:], dk_1[:, MLA_NOPE:]], axis=1).astype(BF16)
        dv_ref[...] = jnp.concatenate([dv_0, dv_1], axis=1).astype(BF16)

    wide = 2 * MLA_NOPE
    full = pl.BlockSpec((s, wide), lambda b, hp, j: (b, hp))
    blk = pl.BlockSpec((tk, wide), lambda b, hp, j: (b * nk + j, hp))
    rowsp = pl.BlockSpec((nk, 16, tk), lambda b, hp, j: (b, 0, 0))
    total = MLA_HEADS * MLA_V
    return _call(
        body, name, (nb, npairs, nk),
        [full, pl.BlockSpec((s, 128), lambda b, hp, j: (b, MLA_HEADS + hp)), blk,
         pl.BlockSpec((tk, 128), lambda b, hp, j: (b * nk + j, 0)), blk, full, rowsp, rowsp],
        [full, full, blk, blk, blk],
        [_sds((t, total), F32), _sds((t, total), F32), _sds((t, total), BF16), _sds((t, total), BF16),
         _sds((t, total), BF16)],
        (q, q, kn, kr2, v, do, lse_rows, delta_rows), hosted=hosted)


def mla_mid_bwd(dqn, dqr, dkn, dv, dkr_heads, h, g_q, g_kv, w_uq, w_uk, w_uv, cos8, sin8, cos64, sin64s, swap64,
                heads_to_rope, head_sum, name):
    t = h.shape[0]
    tm = TOKEN_TILE
    hq = MLA_HEADS * MLA_NOPE
    hr = MLA_HEADS * MLA_ROPE // 2
    nq = w_uq.shape[1]

    def body(dqn_ref, dqr_ref, dkn_ref, dv_ref, dkr_ref, h_ref, gq_ref, gkv_ref, wuq_ref, wuk_ref, wuv_ref,
             c8_ref, s8_ref, c64_ref, s64_ref, sw_ref, hp_ref, hs_ref, dh_ref, dqp_ref, dgq_ref, dgkv_ref):
        @pl.when(pl.program_id(0) == 0)
        def _():
            dgq_ref[...] = jnp.zeros_like(dgq_ref)
            dgkv_ref[...] = jnp.zeros_like(dgkv_ref)

        drot = _dot(dqr_ref[...].astype(BF16), hp_ref[...])
        o1 = drot[:, :hr]
        o2 = drot[:, hr:]
        cs = c8_ref[...]
        sn = s8_ref[...]
        dqp = jnp.concatenate([dqn_ref[...].astype(BF16), (o1 * cs + o2 * sn).astype(BF16),
                               (o2 * cs - o1 * sn).astype(BF16)], axis=1)
        dqp_ref[...] = dqp
        dcq = _dot_nt(dqp, wuq_ref[...])
        dckv = _dot_nt(dkn_ref[...], wuk_ref[...]) + _dot_nt(dv_ref[...], wuv_ref[...])
        hh = h_ref[...]

        def rms_bwd(hpart, g, dc, dg_ref):
            hhat, rstd = _rms(hpart, None)
            dg_ref[...] += jnp.sum(dc * hhat, axis=0, keepdims=True)
            dcg = dc * g
            return rstd * (dcg - hhat * jnp.mean(dcg * hhat, axis=-1, keepdims=True))

        dhq = rms_bwd(hh[:, :MLA_QR], gq_ref[...], dcq, dgq_ref)
        dhkv = rms_bwd(hh[:, MLA_QR:MLA_QR + MLA_KVR], gkv_ref[...], dckv, dgkv_ref)
        dkr = _dot(dkr_ref[...], hs_ref[...])
        dkr_pre = dkr * c64_ref[...] + _dot_f32(dkr * s64_ref[...], sw_ref[...])
        dh_ref[...] = jnp.concatenate([dhq, dhkv, dkr_pre], axis=1).astype(BF16)

    def rows(n):
        return pl.BlockSpec((tm, n), lambda i: (i, 0))

    def whole(a):
        return pl.BlockSpec(a.shape, lambda i: (0,) * a.ndim)

    return pl.pallas_call(
        body, name=name, grid=(t // tm,),
        in_specs=[rows(hq), rows(hq), rows(hq), rows(hq), rows(hq), rows(h.shape[1]), whole(g_q), whole(g_kv),
                  whole(w_uq), whole(w_uk), whole(w_uv), rows(hr), rows(hr), rows(MLA_ROPE), rows(MLA_ROPE),
                  whole(swap64), whole(heads_to_rope), whole(head_sum)],
        out_specs=[rows(h.shape[1]), rows(nq), pl.BlockSpec((1, MLA_QR), lambda i: (0, 0)),
                   pl.BlockSpec((1, MLA_KVR), lambda i: (0, 0))],
        out_shape=[_sds((t, h.shape[1]), BF16), _sds((t, nq), BF16), _sds((1, MLA_QR), F32), _sds((1, MLA_KVR), F32)],
        compiler_params=_cp(1),
    )(dqn, dqr, dkn, dv, dkr_heads, h, g_q, g_kv, w_uq, w_uk, w_uv, cos8, sin8, cos64, sin64s, swap64,
      heads_to_rope, head_sum)


def fox_gate_bwd(dcum, hf, b_f, triu, n_batch, name):
    t, n = hf.shape
    blk = triu.shape[0]
    nb = (t // n_batch) // blk

    def body(dc_ref, hf_ref, b_ref, tri_ref, o_ref, db_ref, carry_ref):
        @pl.when(pl.program_id(1) == 0)
        def _():
            carry_ref[...] = jnp.zeros_like(carry_ref)

        @pl.when((pl.program_id(0) == 0) & (pl.program_id(1) == 0))
        def _():
            db_ref[...] = jnp.zeros_like(db_ref)

        rc = _dot_f32(tri_ref[...], dc_ref[...]) + carry_ref[...]
        carry_ref[...] = rc[0:1, :]
        dhf = rc * jax.nn.sigmoid(-(hf_ref[...] + b_ref[...]))
        o_ref[...] = dhf.astype(BF16)
        db_ref[...] += jnp.sum(dhf, axis=0, keepdims=True)

    rev = pl.BlockSpec((blk, n), lambda bb, i: (bb * nb + nb - 1 - i, 0))
    return pl.pallas_call(
        body, name=name, grid=(n_batch, nb),
        in_specs=[rev, rev, pl.BlockSpec((1, n), lambda bb, i: (0, 0)), pl.BlockSpec((blk, blk), lambda bb, i: (0, 0))],
        out_specs=[rev, pl.BlockSpec((1, n), lambda bb, i: (0, 0))],
        out_shape=[_sds((t, n), BF16), _sds((1, n), F32)], scratch_shapes=[pltpu.VMEM((1, n), F32)],
        compiler_params=_cp(2),
    )(dcum, hf, b_f, triu)


def wgrad(a, bm, name, with_bf16=False, bt=WGRAD_TOKENS):
    ca, t, kd = a.shape
    cb, _, nd = bm.shape
    c = max(ca, cb)
    bn = nd
    if nd > 1024 and nd % 1024 == 0:
        bn = 1024
    nsteps = t // bt

    def body(a_ref, b_ref, o_ref, *rest):
        @pl.when(pl.program_id(2) == 0)
        def _():
            o_ref[...] = jnp.zeros_like(o_ref)

        o_ref[...] += _dot_tn(a_ref[...].astype(BF16), b_ref[...].astype(BF16))
        if with_bf16:
            @pl.when(pl.program_id(2) == nsteps - 1)
            def _():
                rest[0][...] = o_ref[...].astype(BF16)

    out_spec = pl.BlockSpec((None, kd, bn), lambda cc, n, tt: (cc, 0, n))
    res = pl.pallas_call(
        body, name=name, grid=(c, nd // bn, nsteps),
        in_specs=[pl.BlockSpec((None, bt, kd), lambda cc, n, tt: (cc if ca > 1 else 0, tt, 0)),
                  pl.BlockSpec((None, bt, bn), lambda cc, n, tt: (cc if cb > 1 else 0, tt, n))],
        out_specs=[out_spec, out_spec] if with_bf16 else out_spec,
        out_shape=[_sds((c, kd, nd), F32), _sds((c, kd, nd), BF16)] if with_bf16 else _sds((c, kd, nd), F32),
        compiler_params=_cp(3),
    )(a, bm)
    return res


def ada_mod_part(c_all, ada_w, name):
    nl, d, n = ada_w.shape
    rows = c_all.shape[0]
    tn = 512

    def body(c_ref, w_ref, o_ref):
        cv = c_ref[...]
        act = (cv * jax.nn.sigmoid(cv)).astype(BF16)
        o_ref[...] = _dot(act, w_ref[...].astype(BF16))

    return pl.pallas_call(
        body, name=name, grid=(nl, n // tn),
        in_specs=[pl.BlockSpec((rows, d), lambda l, j: (0, 0)), pl.BlockSpec((None, d, tn), lambda l, j: (l, 0, j))],
        out_specs=pl.BlockSpec((None, rows, tn), lambda l, j: (l, 0, j)),
        out_shape=_sds((nl, rows, n), F32), compiler_params=_cp(2),
    )(c_all, ada_w)


def ada_grad(c_all_t, dmod, name):
    nl, rows, n = dmod.shape
    d = c_all_t.shape[0]
    tn = 512

    def body(c_ref, dm_ref, o_ref):
        cv = c_ref[...]
        act = (cv * jax.nn.sigmoid(cv)).astype(BF16)
        o_ref[...] = _dot(act, dm_ref[...].astype(BF16))

    return pl.pallas_call(
        body, name=name, grid=(nl, n // tn),
        in_specs=[pl.BlockSpec((d, rows), lambda l, j: (0, 0)), pl.BlockSpec((None, rows, tn), lambda l, j: (l, 0, j))],
        out_specs=pl.BlockSpec((None, d, tn), lambda l, j: (l, 0, j)),
        out_shape=_sds((nl, d, n), F32), compiler_params=_cp(2),
    )(c_all_t, dmod)


def sum_leading(a, name):
    g, r, n = a.shape

    def body(a_ref, o_ref):
        acc = a_ref[0]
        for kk in range(1, g):
            acc = acc + a_ref[kk]
        o_ref[...] = acc

    return pl.pallas_call(
        body, name=name, grid=(1,), in_specs=[pl.BlockSpec((g, r, n), lambda i: (0, 0, 0))],
        out_specs=pl.BlockSpec((r, n), lambda i: (0, 0)), out_shape=_sds((r, n), F32), compiler_params=_cp(1),
    )(a)


def adamw(w, g, m, v, name):
    nl, r, n = w.shape
    br = r
    for cand in (512, 256, 128, 64, 32, 16, 8):
        if r % cand == 0 and r > cand and cand * n * 4 <= ADAMW_BLOCK_BYTES:
            br = cand
            break
    c1 = 1.0 - ADAM_B1 ** ADAM_STEP
    c2 = 1.0 - ADAM_B2 ** ADAM_STEP

    def body(w_ref, g_ref, m_ref, v_ref, go_ref, d_ref, mo_ref, vo_ref):
        gv = g_ref[...]
        go_ref[...] = gv
        mn = ADAM_B1 * m_ref[...] + (1.0 - ADAM_B1) * gv
        vn = ADAM_B2 * v_ref[...] + (1.0 - ADAM_B2) * (gv * gv)
        mo_ref[...] = mn
        vo_ref[...] = vn
        d_ref[...] = -ADAM_LR * ((mn / c1) / (jnp.sqrt(vn / c2) + ADAM_EPS) + ADAM_WD * w_ref[...])

    spec = pl.BlockSpec((None, br, n), lambda a, i: (a, i, 0))
    return pl.pallas_call(
        body, name=name, grid=(nl, r // br), in_specs=[spec] * 4, out_specs=[spec] * 4,
        out_shape=[_sds((nl, r, n), F32)] * 4, compiler_params=_cp(2),
    )(w, g, m, v)


def all_gather8(x_blk, name):
    m_per, n = x_blk.shape

    def body(x_ref, out_ref, send_sems, recv_sems, local_sem):
        x, y, c = _place()
        me, sibling = (x, y, c), (x, y, 1 - c)
        chips = [(1 - x, y), (x, 1 - y), (1 - x, 1 - y)]

        def rows(px, py, pc):
            return out_ref.at[pl.ds((4 * px + 2 * py + pc) * m_per, m_per), :]

        def copy(k, block, to, src=None):
            return pltpu.make_async_remote_copy(
                src_ref=rows(*block) if src is None else src, dst_ref=rows(*block),
                send_sem=send_sems.at[k], recv_sem=recv_sems.at[k], device_id=to, device_id_type=MESH)

        mine = pltpu.make_async_copy(x_ref, rows(*me), local_sem)
        mine.start()
        first = [copy(0, me, sibling, src=x_ref)]
        first += [copy(1 + j, me, (*chip, c), src=x_ref) for j, chip in enumerate(chips)]
        for cp in first:
            cp.start()
        passed = [copy(4 + j, (*chip, c), sibling) for j, chip in enumerate(chips)]
        for j, chip in enumerate(chips):
            copy(1 + j, (*chip, c), me).wait_recv()
            passed[j].start()
        copy(0, sibling, me).wait_recv()
        for j, chip in enumerate(chips):
            copy(4 + j, (*chip, 1 - c), me).wait_recv()
        for cp in first + passed:
            cp.wait_send()
        mine.wait()

    return pl.pallas_call(
        body, name=name, out_shape=_sds((8 * m_per, n), x_blk.dtype),
        in_specs=[pl.BlockSpec(memory_space=pltpu.VMEM)], out_specs=pl.BlockSpec(memory_space=pltpu.VMEM),
        scratch_shapes=[pltpu.SemaphoreType.DMA((7,)), pltpu.SemaphoreType.DMA((7,)), pltpu.SemaphoreType.DMA],
        compiler_params=pltpu.CompilerParams(vmem_limit_bytes=VMEM_LIMIT),
    )(x_blk)


def _gather_comm(shards):
    nt = len(shards)

    def parts(w_refs, out_refs, sems, finishing):
        send_sems, recv_sems, own_send, own_recv = sems
        x, y, c = _place()
        sibling = (x, y, 1 - c)
        chips = [(1 - x, y), (x, 1 - y), (1 - x, 1 - y)]

        def copy(t, k, block, to, src=None):
            px, py, hh = block
            dst = out_refs[t].at[2 * px + py, hh]
            return pltpu.make_async_remote_copy(
                src_ref=dst if src is None else src, dst_ref=dst,
                send_sem=send_sems.at[6 * t + k], recv_sem=recv_sems.at[6 * t + k], device_id=to, device_id_type=MESH)

        own = [pltpu.make_async_remote_copy(
            src_ref=w_refs[t], dst_ref=out_refs[t].at[2 * x + y], send_sem=own_send.at[t], recv_sem=own_recv.at[t],
            device_id=sibling, device_id_type=MESH) for t in range(nt)]
        first = [copy(t, j, (x, y, c), (*chip, c), src=w_refs[t].at[c]) for t in range(nt) for j, chip in enumerate(chips)]
        if not finishing:
            return own, first
        landed = [copy(t, j, (*chip, c), (x, y, c)) for t in range(nt) for j, chip in enumerate(chips)]
        passed = [copy(t, 3 + j, (*chip, c), sibling) for t in range(nt) for j, chip in enumerate(chips)]
        from_sibling = [copy(t, 3 + j, (*chip, 1 - c), (x, y, c)) for t in range(nt) for j, chip in enumerate(chips)]
        return own, first, landed, passed, from_sibling

    def start(w_refs, out_refs, sems):
        own, first = parts(w_refs, out_refs, sems, False)
        for cp in own + first:
            cp.start()

    def finish(w_refs, out_refs, sems):
        own, first, landed, passed, from_sibling = parts(w_refs, out_refs, sems, True)
        for arrived, fwd in zip(landed, passed):
            arrived.wait_recv()
            fwd.start()
        for cp in from_sibling:
            cp.wait_recv()
        for cp in first + passed:
            cp.wait_send()
        for cp in own:
            cp.wait()

    sems = [pltpu.SemaphoreType.DMA((6 * nt,)), pltpu.SemaphoreType.DMA((6 * nt,)),
            pltpu.SemaphoreType.DMA((nt,)), pltpu.SemaphoreType.DMA((nt,))]
    return _Hosted(list(shards), [_sds((N_CHIPS, *w.shape), w.dtype) for w in shards], sems, start, finish)


def all_gather_chips(shards, name):
    comm = _gather_comm(shards)
    nt = len(shards)

    def body(*refs):
        comm.start(refs[:nt], refs[nt:2 * nt], refs[2 * nt:])
        comm.finish(refs[:nt], refs[nt:2 * nt], refs[2 * nt:])

    hbm = pl.BlockSpec(memory_space=pl.ANY)
    return pl.pallas_call(body, name=name, out_shape=comm.out_shape, in_specs=[hbm] * nt, out_specs=[hbm] * nt,
                          scratch_shapes=comm.sems)(*shards)


def _row_block(r, n, itemsize):
    best = None
    for br in range(16, r + 1, 16):
        if r % br == 0 and br * n * itemsize <= COMM_BLOCK_BYTES:
            best = br
    assert best is not None, (r, n)
    return best


def _scatter_comm(parts):
    nt = len(parts)

    def copies(p_refs, b_refs, sems, arriving):
        send_sems, recv_sems = sems
        x, y, c = _place()
        me = 4 * x + 2 * y + c
        cps = []
        for t in range(nt):
            for r in range(1, 8):
                tx = 1 - x if r & 4 else x
                ty = 1 - y if r & 2 else y
                tc = 1 - c if r & 1 else c
                src, dst = (2 * x + y, c), 4 * tx + 2 * ty + tc
                if not arriving:
                    src, dst = (2 * tx + ty, tc), me
                cps.append(pltpu.make_async_remote_copy(
                    src_ref=p_refs[t].at[src], dst_ref=b_refs[t].at[dst], send_sem=send_sems.at[7 * t + r - 1],
                    recv_sem=recv_sems.at[7 * t + r - 1], device_id=(tx, ty, tc), device_id_type=MESH))
        return cps

    def start(p_refs, b_refs, sems):
        for cp in copies(p_refs, b_refs, sems, False):
            cp.start()

    def finish(p_refs, b_refs, sems):
        for cp in copies(p_refs, b_refs, sems, True):
            cp.wait_recv()
        for cp in copies(p_refs, b_refs, sems, False):
            cp.wait_send()

    sems = [pltpu.SemaphoreType.DMA((7 * nt,)), pltpu.SemaphoreType.DMA((7 * nt,))]
    return _Hosted(list(parts), [_sds((2 * N_CHIPS, *p.shape[2:]), p.dtype) for p in parts], sems, start, finish)


def scatter_exchange(parts, name):
    comm = _scatter_comm(parts)
    nt = len(parts)

    def body(*refs):
        comm.start(refs[:nt], refs[nt:2 * nt], refs[2 * nt:])
        comm.finish(refs[:nt], refs[nt:2 * nt], refs[2 * nt:])

    hbm = pl.BlockSpec(memory_space=pl.ANY)
    return pl.pallas_call(body, name=name, out_shape=comm.out_shape, in_specs=[hbm] * nt, out_specs=[hbm] * nt,
                          scratch_shapes=comm.sems)(*parts)


def sum_devices(own, recv, place, name, slot=(0, 1, None)):
    _, _, r, n = own.shape
    layer, n_layers, buf = slot
    br = _row_block(r, n, 4 * 8)

    def body(p_ref, o_ref, *rest):
        acc = o_ref[...]
        for kk in range(7):
            acc = acc + rest[kk][...].astype(F32)
        rest[-1][...] = acc

    def arrived(rel):
        return pl.BlockSpec((None, br, n), lambda i, pref: (jnp.bitwise_xor(pref[0], rel), i, 0))

    in_specs = [pl.BlockSpec((None, None, br, n), lambda i, pref: (pref[2], pref[1], i, 0))]
    in_specs += [arrived(rel) for rel in range(1, 8)]
    args = [own] + [recv] * 7
    aliases = {}
    if buf is not None:
        in_specs.append(pl.BlockSpec(memory_space=pl.ANY))
        args.append(buf)
        aliases = {9: 0}
    return pl.pallas_call(
        body, name=name,
        grid_spec=pltpu.PrefetchScalarGridSpec(
            num_scalar_prefetch=1, grid=(r // br,), in_specs=in_specs,
            out_specs=pl.BlockSpec((None, None, br, n), lambda i, pref: (layer, pref[1], i, 0))),
        out_shape=_sds((n_layers, 2, r, n), F32), input_output_aliases=aliases, compiler_params=_cp(1),
    )(place, *args)


def sibling_join_halves(bufs, name):
    nt = len(bufs)
    layers = [bf.shape[0] for bf in bufs]
    first = [sum(layers[:t]) for t in range(nt)]

    def body(*refs):
        o_refs = refs[nt:2 * nt]
        send_sems, recv_sems = refs[2 * nt:]
        x, y, c = _place()

        def copy(t, l, hh):
            return pltpu.make_async_remote_copy(
                src_ref=o_refs[t].at[l, hh], dst_ref=o_refs[t].at[l, hh], send_sem=send_sems.at[first[t] + l],
                recv_sem=recv_sems.at[first[t] + l], device_id=(x, y, 1 - c), device_id_type=MESH)

        cps = [copy(t, l, c) for t in range(nt) for l in range(layers[t])]
        for cp in cps:
            cp.start()
        for t in range(nt):
            for l in range(layers[t]):
                copy(t, l, 1 - c).wait_recv()
        for cp in cps:
            cp.wait_send()

    hbm = pl.BlockSpec(memory_space=pl.ANY)
    return pl.pallas_call(
        body, name=name, out_shape=[_sds(bf.shape, bf.dtype) for bf in bufs],
        in_specs=[hbm] * nt, out_specs=[hbm] * nt, input_output_aliases={t: t for t in range(nt)},
        scratch_shapes=[pltpu.SemaphoreType.DMA((sum(layers),)), pltpu.SemaphoreType.DMA((sum(layers),))],
    )(*bufs)


_SHARD_KIND = {"mla_w_in": "rows", "mla_w_uq": "cols", "mla_w_uk": "cols", "mla_w_uv": "cols", "mla_w_o": "rows",
               "fox_w_in": "cols", "fox_w_o": "rows", "ffn_w_gate": "chunk", "ffn_w_up": "chunk", "ffn_w_down": "chunk"}
_PACKED = tuple(_SHARD_KIND)


def _halves(shard):
    if shard.ndim == 3 and shard.shape[0] == 2:
        return shard
    r, n = shard.shape[-2:]
    return shard.reshape(2, r // 2, n)


def _cols_to_full(g):
    return jnp.transpose(g, (1, 0, 2)).reshape(g.shape[1], -1)


def _full_to_cols(w):
    k, n4 = w.shape
    return jnp.transpose(w.reshape(k, N_CHIPS, n4 // N_CHIPS), (1, 0, 2))


def _uq_perm():
    per = MLA_NOPE + MLA_ROPE
    half = MLA_ROPE // 2
    nope = [h * per + d for h in range(MLA_HEADS) for d in range(MLA_NOPE)]
    r1 = [h * per + MLA_NOPE + r for h in range(MLA_HEADS) for r in range(half)]
    r2 = [h * per + MLA_NOPE + half + r for h in range(MLA_HEADS) for r in range(half)]
    perm = np.array(nope + r1 + r2, dtype=np.int32)
    return perm, np.argsort(perm).astype(np.int32)


def _rope_matrices():
    half = MLA_ROPE // 2
    nr = MLA_HEADS * MLA_ROPE
    to_heads = np.zeros((nr, nr), np.float32)
    from_heads = np.zeros((MLA_HEADS * 128, nr), np.float32)
    for e in range(2):
        for h in range(MLA_HEADS):
            for r in range(half):
                to_heads[e * MLA_HEADS * half + h * half + r, h * MLA_ROPE + e * half + r] = 1.0
                from_heads[h * 128 + e * half + r, e * MLA_HEADS * half + h * half + r] = 1.0
    head_sum = np.tile(np.eye(MLA_ROPE, dtype=np.float32), (2 * MLA_HEADS, 1))
    dup = np.concatenate([np.eye(MLA_ROPE, dtype=np.float32)] * 2, axis=1)
    return to_heads, from_heads, head_sum, dup


def _ffn_weights(gathered):
    return tuple(g.reshape(N_CHIPS, 2 * g.shape[2], g.shape[3]) for g in gathered)


def _fox_weights(gathered):
    w_in, w_o = gathered
    w_in = _cols_to_full(w_in.reshape(N_CHIPS, 2 * w_in.shape[2], w_in.shape[3]))
    return w_in, w_o.reshape(-1, w_o.shape[-1])


def _local_step(x, positions, target, mods, wts, ln_g, ln_b, mla_g_q, mla_g_kv, fox_b_f, shards=None):
    nb, s, d = x.shape
    t = nb * s
    x0 = x.reshape(t, d)
    tgt = target.reshape(t, d)
    perm, inv_perm = _uq_perm()

    half = MLA_ROPE // 2
    inv_freq = ROPE_THETA ** (-jnp.arange(half, dtype=F32) / half)
    ang = positions.astype(F32).reshape(t, 1) * inv_freq
    cos, sin = jnp.cos(ang), jnp.sin(ang)
    cos8, sin8 = jnp.tile(cos, (1, MLA_HEADS)), jnp.tile(sin, (1, MLA_HEADS))
    cos64 = jnp.concatenate([cos, cos], axis=1)
    sin64s = jnp.concatenate([-sin, sin], axis=1)
    swap64 = jnp.asarray(np.roll(np.eye(MLA_ROPE, dtype=np.float32), half, axis=1))
    to_heads, from_heads, head_sum, dup = _rope_matrices()
    to_heads, from_heads = jnp.asarray(to_heads, dtype=BF16), jnp.asarray(from_heads, dtype=BF16)
    head_sum, dup = jnp.asarray(head_sum, dtype=BF16), jnp.asarray(dup, dtype=BF16)
    sel_mla = jnp.asarray(np.pad(np.kron(np.eye(MLA_HEADS, dtype=np.float32), np.ones((MLA_V, 1), np.float32)),
                                 ((0, 0), (0, 128 - MLA_HEADS))))
    sel_fox = jnp.asarray(np.pad(np.kron(np.eye(FOX_HEADS, dtype=np.float32), np.ones((FOX_HD, 1), np.float32)),
                                 ((0, 0), (0, 128 - FOX_HEADS))))
    tri = jnp.asarray(np.tril(np.ones((128, 128), np.float32)))
    triu = jnp.asarray(np.triu(np.ones((128, 128), np.float32)))
    onehot16 = jnp.asarray(np.eye(16, 128, dtype=np.float32))

    def vec(a):
        return a.reshape(1, -1)

    def carried(key):
        return None if shards is None else _gather_comm(shards[key])

    def split(res):
        return (res, None) if shards is None else res

    w_uq_p = wts["mla_w_uq"][:, perm]
    b_f_pad = jnp.pad(fox_b_f.reshape(1, -1), ((0, 0), (0, 128 - FOX_HEADS)))

    sh_a, sc_a, gt_a, sh_f, sc_f, gt_f = mods[0]
    h_in, u_m = mod_linear(x0, sh_a, sc_a, wts["mla_w_in"], F32, "mla_in", emit_u=True)
    q_m, kn_m, v_m, kr2_m, cq_m, ckv_m = mla_mid_fwd(
        h_in, vec(mla_g_q), vec(mla_g_kv), w_uq_p, wts["mla_w_uk"], wts["mla_w_uv"], cos8, sin8, cos64, sin64s, swap64,
        to_heads, dup, "mla_mid")
    (o_m, lse_m), got = split(mla_attn_fwd(q_m, kn_m, kr2_m, v_m, nb, "mla_attn", hosted=carried("ffn0")))
    ffn0_w = wts["ffn"][0] if got is None else _ffn_weights(got)
    y0, x1 = linear_resid_ln(o_m, wts["mla_w_o"], x0, gt_a, vec(ln_g[0, 0]), vec(ln_b[0, 0]), "mla_out")
    (u_f0, hg0, hu0, y1, x2), got = split(ffn_fwd(x1, sh_f, sc_f, gt_f, *ffn0_w, vec(ln_g[0, 1]), vec(ln_b[0, 1]), "ffn0",
                                                  hosted=carried("fox")))
    fox_w_in, fox_w_o = (wts["fox_w_in"], wts["fox_w_o"]) if got is None else _fox_weights(got)
    fox_w_qkv = fox_w_in[:, :3 * d]
    fox_w_f = jnp.pad(fox_w_in[:, 3 * d:], ((0, 0), (0, 128 - FOX_HEADS)))
    sh_a1, sc_a1, gt_a1, sh_f1, sc_f1, gt_f1 = mods[1]
    qkv, u_x = mod_linear(x2, sh_a1, sc_a1, fox_w_qkv, BF16, "fox_qkv", tn=1024, emit_u=True)
    hf = mod_linear(x2, sh_a1, sc_a1, fox_w_f, F32, "fox_f")
    cum = fox_gate_fwd(hf, b_f_pad, tri, nb, "fox_gate")
    cum_rows = rows16(cum, "fox_cum_rows")
    (o_x, lse_x), got = split(fox_attn_fwd(qkv, cum, cum_rows, nb, "fox_attn", hosted=carried("ffn1")))
    ffn1_w = wts["ffn"][1] if got is None else _ffn_weights(got)
    y2, x3 = linear_resid_ln(o_x, fox_w_o, x2, gt_a1, vec(ln_g[1, 0]), vec(ln_b[1, 0]), "fox_out")
    u_f1, hg1, hu1, y3, x4 = ffn_fwd(x3, sh_f1, sc_f1, gt_f1, *ffn1_w, vec(ln_g[1, 1]), vec(ln_b[1, 1]), "ffn1")
    dx4, sq_err = loss_grad(x4, tgt, "loss")
    loss_part = 0.5 * jnp.sum(sq_err) / d

    parts, recv = {}, {}

    def halves_of(g):
        return g.reshape(N_CHIPS, 2, g.shape[1] // 2, g.shape[2])

    def scatter(keys, sent):
        return None if shards is None else _scatter_comm([sent[k] for k in keys])

    def landed(keys, got):
        if got is not None:
            recv.update(zip(keys, got))

    def ffn_grads(layer, u, dhg, dhu, act, dy):
        sent = {}
        for n, (a_op, b_op) in (("ffn_w_gate", (u[None], dhg)), ("ffn_w_up", (u[None], dhu)), ("ffn_w_down", (act, dy[None]))):
            g32, g16 = wgrad(a_op, b_op, "ffn%d_d%s" % (layer, n[4:]), with_bf16=True)
            parts["%s/%d" % (n, layer)], sent["%s/%d" % (n, layer)] = halves_of(g32), halves_of(g16)
        return sent

    dz3, dy3, dg11, db11, dgt_f1 = ln_bwd(dx4, x3, y3, gt_f1, vec(ln_g[1, 1]), "ffn1_ln_bwd")
    dhg1, dhu1, act1, dx3, dsc_f1, dsh_f1 = ffn_bwd(dy3, hg1, hu1, *ffn1_w, dz3, x3, sc_f1, "ffn1_bwd")
    sent = ffn_grads(1, u_f1, dhg1, dhu1, act1, dy3)
    dz2, dy2, dg10, db10, dgt_a1 = ln_bwd(dx3, x2, y2, gt_a1, vec(ln_g[1, 0]), "fox_ln_bwd")
    do_x, delta_x = linear_nt_delta(dy2, fox_w_o, o_x, sel_fox, "fox_out_bwd")
    (dq_x, dk_x, dv_x, dfq_x, dfk_x), got = split(fox_attn_bwd(
        qkv, do_x, cum, cum_rows, rows16(lse_x, "fox_lse_rows"), rows16(delta_x, "fox_delta_rows"), nb, "fox_attn_bwd",
        hosted=scatter(list(sent), sent)))
    landed(list(sent), got)
    dcum = tokens128(dfq_x + dfk_x, onehot16, "fox_dcum")
    dhf, dbf = fox_gate_bwd(dcum, hf, b_f_pad, triu, nb, "fox_gate_bwd")
    fox_d = [("q", dq_x), ("k", dk_x), ("v", dv_x)]
    dx2, dsc_a1, dsh_a1 = linear_nt_mod_bwd(
        [(dh, fox_w_in[:, i * d:(i + 1) * d]) for i, (_, dh) in enumerate(fox_d)] + [(dhf, fox_w_f)], dz2, x2, sc_a1,
        "fox_in_bwd")
    dw_in = [wgrad(u_x[None], dh[None], "fox_dw" + tag)[0] for tag, dh in fox_d]
    dw_f = wgrad(u_x[None], dhf[None], "fox_dwf")[0]
    parts["fox_w_in"] = halves_of(_full_to_cols(jnp.concatenate(dw_in + [dw_f[:, :FOX_HEADS]], axis=1)))
    parts["fox_w_o"] = wgrad(o_x[None], dy2[None], "fox_dwo")[0].reshape(N_CHIPS, 2, -1, d)
    sent = {k: parts[k].astype(BF16) for k in ("fox_w_in", "fox_w_o")}
    dz1, dy1, dg01, db01, dgt_f0 = ln_bwd(dx2, x1, y1, gt_f, vec(ln_g[0, 1]), "ffn0_ln_bwd")
    (dhg0, dhu0, act0, dx1, dsc_f0, dsh_f0), got = split(ffn_bwd(dy1, hg0, hu0, *ffn0_w, dz1, x1, sc_f, "ffn0_bwd",
                                                                 hosted=scatter(list(sent), sent)))
    landed(list(sent), got)
    sent = ffn_grads(0, u_f0, dhg0, dhu0, act0, dy1)
    dz0, dy0, dg00, db00, dgt_a0 = ln_bwd(dx1, x0, y0, gt_a, vec(ln_g[0, 0]), "mla_ln_bwd")
    do_m, delta_m = linear_nt_delta(dy0, wts["mla_w_o"], o_m, sel_mla, "mla_out_bwd")
    parts["mla_w_o"] = wgrad(o_m[None], dy0[None], "mla_dwo")[0].reshape(N_CHIPS, 2, -1, d)
    sent["mla_w_o"] = parts["mla_w_o"].astype(BF16)
    (dqn_m, dqr_m, dkn_m, dkr_m, dv_m), got = split(mla_attn_bwd(
        q_m, kn_m, kr2_m, v_m, do_m, rows16(lse_m, "mla_lse_rows"), rows16(delta_m, "mla_delta_rows"), nb,
        "mla_attn_bwd", hosted=scatter(list(sent), sent)))
    landed(list(sent), got)
    dh_in, dq_pre, dgq, dgkv = mla_mid_bwd(
        dqn_m, dqr_m, dkn_m, dv_m, dkr_m, h_in, vec(mla_g_q), vec(mla_g_kv), w_uq_p, wts["mla_w_uk"],
        wts["mla_w_uv"], cos8, sin8, cos64, sin64s, swap64, from_heads, head_sum, "mla_mid_bwd")
    parts["mla_w_uq"] = halves_of(_full_to_cols(wgrad(cq_m[None], dq_pre[None], "mla_dwuq")[0][:, inv_perm]))
    parts["mla_w_uk"] = halves_of(_full_to_cols(wgrad(ckv_m[None], dkn_m[None], "mla_dwuk")[0]))
    parts["mla_w_uv"] = halves_of(_full_to_cols(wgrad(ckv_m[None], dv_m[None], "mla_dwuv")[0]))
    parts["mla_w_in"] = wgrad(u_m[None], dh_in[None], "mla_dwin")[0].reshape(N_CHIPS, 2, -1, h_in.shape[1])
    dx0, dsc_a0, dsh_a0 = linear_nt_mod_bwd([(dh_in, wts["mla_w_in"])], dz0, x0, sc_a, "mla_in_bwd")

    dmods = [(dsh_a0, dsc_a0, dgt_a0, dsh_f0, dsc_f0, dgt_f0), (dsh_a1, dsc_a1, dgt_a1, dsh_f1, dsc_f1, dgt_f1)]
    d_ln_g = jnp.stack([jnp.concatenate([dg00, dg01], axis=0), jnp.concatenate([dg10, dg11], axis=0)])
    d_ln_b = jnp.stack([jnp.concatenate([db00, db01], axis=0), jnp.concatenate([db10, db11], axis=0)])
    return loss_part, dx0.reshape(nb, s, d), (parts, recv), dmods, d_ln_g, d_ln_b, dgq, dgkv, dbf[:, :FOX_HEADS]


def _pad_rows(a, rows):
    return jnp.pad(a, ((0, rows - a.shape[0]), (0, 0)))


def kernel(x, c, positions, mla_w_in, mla_g_q, mla_w_uq, mla_g_kv, mla_w_uk, mla_w_uv, mla_w_o, fox_w_in, fox_b_f, fox_w_o, ada_w, ada_b, ffn_w_gate, ffn_w_up, ffn_w_down, ln_g, ln_b, loss_target, m_mla_w_in, m_mla_g_q, m_mla_w_uq, m_mla_g_kv, m_mla_w_uk, m_mla_w_uv, m_mla_w_o, m_fox_w_in, m_fox_b_f, m_fox_w_o, m_ada_w, m_ada_b, m_ffn_w_gate, m_ffn_w_up, m_ffn_w_down, m_ln_g, m_ln_b, v_mla_w_in, v_mla_g_q, v_mla_w_uq, v_mla_g_kv, v_mla_w_uk, v_mla_w_uv, v_mla_w_o, v_fox_w_in, v_fox_b_f, v_fox_w_o, v_ada_w, v_ada_b, v_ffn_w_gate, v_ffn_w_up, v_ffn_w_down, v_ln_g, v_ln_b):
    args = dict(locals())
    nb, s, d = x.shape
    ax, ay, ac = lax.axis_index("x"), lax.axis_index("y"), lax.axis_index("c")
    chip = 2 * ax + ay
    dev = 2 * chip + ac
    n_dev = 2 * N_CHIPS
    n_all = nb * n_dev

    shard_shapes = {n: (args[n].shape if _SHARD_KIND[n] == "chunk" else args[n].shape[1:]) for n in _PACKED}

    def block(n, layer=None):
        w = args[n].reshape(shard_shapes[n]) if layer is None else args[n][layer]
        return _halves(w.astype(BF16))

    mla_names = [n for n in _PACKED if n.startswith("mla")]
    wts = {}
    for n, g in zip(mla_names, all_gather_chips([block(n) for n in mla_names], "gather_mla")):
        g = g.reshape(N_CHIPS, *shard_shapes[n])
        wts[n] = g.reshape(-1, g.shape[-1]) if _SHARD_KIND[n] == "rows" else _cols_to_full(g)
    ffn_names = ("ffn_w_gate", "ffn_w_up", "ffn_w_down")
    shards = {"ffn0": [block(n, 0) for n in ffn_names], "fox": [block("fox_w_in"), block("fox_w_o")],
              "ffn1": [block(n, 1) for n in ffn_names]}

    ln_cols = ln_g.shape[-1]
    ln_blk = jnp.concatenate([ln_g.reshape(2 * DEPTH, ln_cols), ln_b.reshape(2 * DEPTH, ln_cols)], axis=0)
    early = jnp.concatenate([_pad_rows(c, 8), jnp.pad(_pad_rows(ln_blk, 8), ((0, 0), (0, d - ln_cols)))], axis=0)
    early = all_gather8(early, "gather_c_ln").reshape(n_dev, 16, d)
    c_all = early[:, :nb].reshape(n_all, d)
    ln_all = early.reshape(N_CHIPS, 2, 16, d)[:, 0, 8:8 + 4 * DEPTH, :ln_cols]
    ln_all = jnp.transpose(ln_all, (1, 0, 2)).reshape(4 * DEPTH, d)
    ln_g_full = ln_all[:2 * DEPTH].reshape(DEPTH, 2, d)
    ln_b_full = ln_all[2 * DEPTH:].reshape(DEPTH, 2, d)
    mod_part = ada_mod_part(c_all, ada_w, "ada_mod")
    ncol = mod_part.shape[-1]
    mod_g = all_gather8(mod_part.reshape(DEPTH * n_all, ncol), "gather_mod")
    mod_g = mod_g.reshape(N_CHIPS, 2, DEPTH, n_all, ncol)[:, 0]
    mod_full = jnp.transpose(mod_g, (1, 2, 0, 3)).reshape(DEPTH, n_all, N_CHIPS * ncol) + ada_b[:, None, :]
    mod_loc = lax.dynamic_slice_in_dim(mod_full, dev * nb, nb, axis=1)
    mods = [tuple(mod_loc[i, :, k * d:(k + 1) * d].reshape(nb, 1, d) for k in range(6)) for i in range(DEPTH)]

    loss_part, grad_x, (parts, recv), dmods, d_ln_g, d_ln_b, dgq, dgkv, dbf = _local_step(
        x, positions, loss_target, mods, wts, ln_g_full, ln_b_full, mla_g_q[0], mla_g_kv[0], fox_b_f[0], shards)
    loss = lax.psum(loss_part, ("x", "y", "c"))

    dmod_rows = jnp.stack([jnp.concatenate([v_.reshape(nb, d) for v_ in dm], axis=1) for dm in dmods])
    small = jnp.concatenate([
        d_ln_g.reshape(2 * DEPTH, d), d_ln_b.reshape(2 * DEPTH, d),
        jnp.pad(jnp.concatenate([dgq, dgkv, dbf], axis=1), ((0, 0), (0, d - 2 * MLA_QR - FOX_HEADS))),
        dmod_rows.reshape(DEPTH * nb * 6, d)], axis=0)
    n_small = small.shape[0]
    small_rows = -(-n_small // 8) * 8
    small_all = all_gather8(_pad_rows(small, small_rows), "gather_stats").reshape(n_dev, small_rows, d)
    stat_sum = sum_leading(small_all, "sum_stats")
    g_ln_g = lax.dynamic_slice_in_dim(stat_sum[:2 * DEPTH], chip * ln_cols, ln_cols, axis=1).reshape(DEPTH, 2, ln_cols)
    g_ln_b = lax.dynamic_slice_in_dim(stat_sum[2 * DEPTH:4 * DEPTH], chip * ln_cols, ln_cols, axis=1).reshape(DEPTH, 2, ln_cols)
    row = stat_sum[4 * DEPTH]
    g_gq = row[:MLA_QR].reshape(1, MLA_QR)
    g_gkv = row[MLA_QR:2 * MLA_QR].reshape(1, MLA_KVR)
    g_bf = row[2 * MLA_QR:2 * MLA_QR + FOX_HEADS].reshape(1, FOX_HEADS)
    base = 4 * DEPTH + 1
    dmod_all = small_all[:, base:base + DEPTH * nb * 6].reshape(n_dev, DEPTH, nb, 6 * d)
    dmod_all = jnp.transpose(dmod_all, (1, 0, 2, 3)).reshape(DEPTH, n_all, 6 * d)
    g_ada_b = sum_leading(jnp.transpose(dmod_all, (1, 0, 2)), "sum_ada_b")
    dmod_mine = lax.dynamic_slice_in_dim(dmod_all, chip * ncol, ncol, axis=2)
    g_ada_w = ada_grad(c_all.T, dmod_mine, "ada_grad")

    late = [k for k in parts if k not in recv]
    recv.update(zip(late, scatter_exchange([parts[k].astype(BF16) for k in late], "rs_exchange_mla")))
    place = jnp.stack([dev, ac, chip]).astype(jnp.int32)
    bufs = []
    for n in _PACKED:
        if _SHARD_KIND[n] == "chunk":
            buf = None
            for layer in range(DEPTH):
                key = "%s/%d" % (n, layer)
                buf = sum_devices(parts[key], recv[key], place, "rs_sum_%s%d" % (n, layer), slot=(layer, DEPTH, buf))
        else:
            buf = sum_devices(parts[n], recv[n], place, "rs_sum_" + n)
        bufs.append(buf)
    joined = sibling_join_halves(bufs, "rs_join")
    g_big = {n: j.reshape(shard_shapes[n]) for n, j in zip(_PACKED, joined)}

    g_out = {
        "mla_w_in": g_big["mla_w_in"], "mla_g_q": g_gq, "mla_w_uq": g_big["mla_w_uq"], "mla_g_kv": g_gkv,
        "mla_w_uk": g_big["mla_w_uk"], "mla_w_uv": g_big["mla_w_uv"], "mla_w_o": g_big["mla_w_o"],
        "fox_w_in": g_big["fox_w_in"], "fox_b_f": g_bf, "fox_w_o": g_big["fox_w_o"],
        "ada_w": g_ada_w, "ada_b": g_ada_b, "ffn_w_gate": g_big["ffn_w_gate"], "ffn_w_up": g_big["ffn_w_up"],
        "ffn_w_down": g_big["ffn_w_down"], "ln_g": g_ln_g, "ln_b": g_ln_b}
    names = ["mla_w_in", "mla_g_q", "mla_w_uq", "mla_g_kv", "mla_w_uk", "mla_w_uv", "mla_w_o", "fox_w_in", "fox_b_f",
             "fox_w_o", "ada_w", "ada_b", "ffn_w_gate", "ffn_w_up", "ffn_w_down", "ln_g", "ln_b"]
    small_names = ["mla_g_q", "mla_g_kv", "fox_b_f", "ada_b", "ln_g", "ln_b"]
    deltas, new_m, new_v = {}, {}, {}
    for n in names:
        if n in small_names:
            continue
        g_out[n], deltas[n], new_m[n], new_v[n] = adamw(args[n], g_out[n].reshape(args[n].shape), args["m_" + n],
                                                        args["v_" + n], "adamw_" + n)

    def small_pack(prefix, src):
        flat = jnp.concatenate([src[prefix + n].reshape(-1) for n in small_names])
        size = -(-flat.shape[0] // (8 * 128)) * 8 * 128
        return jnp.pad(flat, (0, size - flat.shape[0])).reshape(1, -1, 128)

    _, sd, sm, sv = adamw(small_pack("", args), small_pack("", g_out), small_pack("m_", args), small_pack("v_", args),
                       "adamw_small")
    off = 0
    for n in small_names:
        shp = args[n].shape
        size = math.prod(shp)
        deltas[n] = sd.reshape(-1)[off:off + size].reshape(shp)
        new_m[n] = sm.reshape(-1)[off:off + size].reshape(shp)
        new_v[n] = sv.reshape(-1)[off:off + size].reshape(shp)
        off += size

    outs = [loss, grad_x]
    outs += [g_out[n].reshape(args[n].shape) for n in names]
    outs += [deltas[n] for n in names] + [new_m[n] for n in names] + [new_v[n] for n in names]
    return tuple(outs)
```

```python
import functools
import math

import numpy as np
import jax
import jax.numpy as jnp
from jax import lax
from jax.experimental import pallas as pl
from jax.experimental.pallas import tpu as pltpu

F32 = jnp.float32
BF16 = jnp.bfloat16
MESH = pl.DeviceIdType.MESH

D_MODEL = 1024
DEPTH = 2
MLA_HEADS = 8
MLA_NOPE = 128
MLA_ROPE = 64
MLA_V = 128
MLA_QR = 256
MLA_KVR = 256
ROPE_THETA = 10000.0
FOX_HEADS = 16
FOX_HD = 64
D_FF = 2816
N_CHIPS = 4
FF_CHUNK = D_FF // N_CHIPS
ALPHA = (2.0 * DEPTH) ** 0.25
EPS = 1e-5
ADAM_LR = 0.001
ADAM_B1 = 0.9
ADAM_B2 = 0.999
ADAM_EPS = 1e-08
ADAM_WD = 0.01
ADAM_STEP = 10

VMEM_LIMIT = 56 * 1024 * 1024
TOKEN_TILE = 512
WGRAD_TOKENS = 2048
ATTN_TILE = 512
COMM_BLOCK_BYTES = 2 * 1024 * 1024
ADAMW_BLOCK_BYTES = 1024 * 1024


def _cp(n_axes):
    return pltpu.CompilerParams(dimension_semantics=("arbitrary",) * n_axes, vmem_limit_bytes=VMEM_LIMIT)


def _dot(a, b):
    return jnp.dot(a, b, preferred_element_type=F32)


def _dot_nt(a, b):
    return lax.dot_general(a, b, (((1,), (1,)), ((), ())), preferred_element_type=F32)


def _dot_tn(a, b):
    return lax.dot_general(a, b, (((0,), (0,)), ((), ())), preferred_element_type=F32)


def _dot_f32(a, b):
    return jnp.dot(a, b, preferred_element_type=F32, precision=lax.Precision.HIGHEST)


def _sds(shape, dtype):
    return jax.ShapeDtypeStruct(shape, dtype)


def _place():
    return lax.axis_index("x"), lax.axis_index("y"), lax.axis_index("c")


class _Hosted:
    def __init__(self, inputs, out_shape, sems, start, finish):
        self.inputs, self.out_shape, self.sems, self.start, self.finish = inputs, out_shape, sems, start, finish


def _call(body, name, grid, in_specs, out_specs, out_shape, args, scratch_shapes=(), hosted=None):
    in_specs, out_specs, out_shape, scratch_shapes = list(in_specs), list(out_specs), list(out_shape), list(scratch_shapes)
    if hosted is None:
        return pl.pallas_call(body, name=name, grid=grid, in_specs=in_specs, out_specs=out_specs, out_shape=out_shape,
                              scratch_shapes=scratch_shapes, compiler_params=_cp(len(grid)))(*args)
    n_in, n_out, n_scr = len(in_specs), len(out_specs), len(scratch_shapes)
    h_in, h_out = len(hosted.inputs), len(hosted.out_shape)

    def carried(*refs):
        o0 = n_in + h_in
        s0 = o0 + n_out + h_out
        c_in, c_out, c_sem = refs[n_in:o0], refs[o0 + n_out:s0], refs[s0 + n_scr:]
        ids = [pl.program_id(a) for a in range(len(grid))]
        first = functools.reduce(jnp.logical_and, [i == 0 for i in ids])
        last = functools.reduce(jnp.logical_and, [i == g - 1 for i, g in zip(ids, grid)])

        @pl.when(first)
        def _():
            hosted.start(c_in, c_out, c_sem)

        body(*refs[:n_in], *refs[o0:o0 + n_out], *refs[s0:s0 + n_scr])

        @pl.when(last)
        def _():
            hosted.finish(c_in, c_out, c_sem)

    hbm = pl.BlockSpec(memory_space=pl.ANY)
    res = pl.pallas_call(
        carried, name=name, grid=grid, in_specs=in_specs + [hbm] * h_in, out_specs=out_specs + [hbm] * h_out,
        out_shape=out_shape + list(hosted.out_shape), scratch_shapes=scratch_shapes + list(hosted.sems),
        compiler_params=_cp(len(grid)))(*args, *hosted.inputs)
    return res[:n_out], res[n_out:]


def mod_linear(x, shift, scale, w, out_dtype, name, tn=None, emit_u=False, w_rows=None):
    t, d = x.shape
    n = w.shape[1] if w_rows is None else w_rows
    tn = n if tn is None else tn
    tm = TOKEN_TILE
    tps = (t // shift.shape[0]) // tm

    def body(x_ref, sh_ref, sc_ref, w_ref, o_ref, *rest):
        u = (x_ref[...] * (1.0 + sc_ref[...]) + sh_ref[...]).astype(BF16)
        o_ref[...] = (_dot(u, w_ref[...]) if w_rows is None else _dot_nt(u, w_ref[...])).astype(out_dtype)
        if emit_u:
            @pl.when(pl.program_id(1) == 0)
            def _():
                rest[0][...] = u

    vec = pl.BlockSpec((None, 1, d), lambda i, j: (i // tps, 0, 0))
    out_shape = [_sds((t, n), out_dtype)]
    out_specs = [pl.BlockSpec((tm, tn), lambda i, j: (i, j))]
    if emit_u:
        out_shape.append(_sds((t, d), BF16))
        out_specs.append(pl.BlockSpec((tm, d), lambda i, j: (i, 0)))
    w_spec = pl.BlockSpec((d, tn), lambda i, j: (0, j)) if w_rows is None else pl.BlockSpec((tn, d), lambda i, j: (j, 0))
    res = pl.pallas_call(
        body, name=name, grid=(t // tm, n // tn),
        in_specs=[pl.BlockSpec((tm, d), lambda i, j: (i, 0)), vec, vec, w_spec],
        out_specs=out_specs, out_shape=out_shape, compiler_params=_cp(2),
    )(x, shift, scale, w)
    return res if emit_u else res[0]


def _rms(h, g):
    rstd = lax.rsqrt(jnp.mean(h * h, axis=-1, keepdims=True) + EPS)
    return h * rstd, rstd


def mla_mid_fwd(h, g_q, g_kv, w_uq, w_uk, w_uv, cos8, sin8, cos64, sin64s, swap64, rope_to_heads, dup64, name):
    t = h.shape[0]
    tm = TOKEN_TILE
    hq = MLA_HEADS * MLA_NOPE
    hr = MLA_HEADS * MLA_ROPE // 2

    def body(h_ref, gq_ref, gkv_ref, wuq_ref, wuk_ref, wuv_ref, c8_ref, s8_ref, c64_ref, s64_ref, sw_ref, p_ref, d_ref,
             q_ref, kn_ref, v_ref, kr_ref, cq_ref, ckv_ref):
        hh = h_ref[...]
        cq = (_rms(hh[:, :MLA_QR], None)[0] * gq_ref[...]).astype(BF16)
        ckv = (_rms(hh[:, MLA_QR:MLA_QR + MLA_KVR], None)[0] * gkv_ref[...]).astype(BF16)
        cq_ref[...] = cq
        ckv_ref[...] = ckv
        q = _dot(cq, wuq_ref[...])
        x1 = q[:, hq:hq + hr]
        x2 = q[:, hq + hr:]
        cs = c8_ref[...]
        sn = s8_ref[...]
        rot = jnp.concatenate([x1 * cs - x2 * sn, x2 * cs + x1 * sn], axis=1).astype(BF16)
        q_ref[...] = jnp.concatenate([q[:, :hq].astype(BF16), _dot(rot, p_ref[...]).astype(BF16)], axis=1)
        kn_ref[...] = _dot(ckv, wuk_ref[...]).astype(BF16)
        v_ref[...] = _dot(ckv, wuv_ref[...]).astype(BF16)
        kr = hh[:, MLA_QR + MLA_KVR:]
        kr = (kr * c64_ref[...] + _dot_f32(kr, sw_ref[...]) * s64_ref[...]).astype(BF16)
        kr_ref[...] = _dot(kr, d_ref[...]).astype(BF16)

    def rows(n):
        return pl.BlockSpec((tm, n), lambda i: (i, 0))

    def whole(a):
        return pl.BlockSpec(a.shape, lambda i: (0,) * a.ndim)

    nq = w_uq.shape[1]
    return pl.pallas_call(
        body, name=name, grid=(t // tm,),
        in_specs=[rows(h.shape[1]), whole(g_q), whole(g_kv), whole(w_uq), whole(w_uk), whole(w_uv),
                  rows(hr), rows(hr), rows(MLA_ROPE), rows(MLA_ROPE), whole(swap64), whole(rope_to_heads), whole(dup64)],
        out_specs=[rows(nq), rows(hq), rows(hq), rows(2 * MLA_ROPE), rows(MLA_QR), rows(MLA_KVR)],
        out_shape=[_sds((t, nq), BF16), _sds((t, hq), BF16), _sds((t, hq), BF16), _sds((t, 2 * MLA_ROPE), BF16),
                   _sds((t, MLA_QR), BF16), _sds((t, MLA_KVR), BF16)],
        compiler_params=_cp(1),
    )(h, g_q, g_kv, w_uq, w_uk, w_uv, cos8, sin8, cos64, sin64s, swap64, rope_to_heads, dup64)


def _pick_lane(tile, idx):
    lane = lax.broadcasted_iota(jnp.int32, tile.shape, 1)
    return jnp.sum(jnp.where(lane == idx, tile, 0.0), axis=1, keepdims=True)


def _pick_row(tile, idx):
    row = lax.broadcasted_iota(jnp.int32, tile.shape, 0)
    return jnp.sum(jnp.where(row == idx, tile, 0.0), axis=0, keepdims=True)


def _put_lane(tile, idx, col):
    lane = lax.broadcasted_iota(jnp.int32, tile.shape, 1)
    return jnp.where(lane == idx, col, tile)


def _put_row(tile, idx, row):
    r = lax.broadcasted_iota(jnp.int32, tile.shape, 0)
    return tile + jnp.where(r == idx, row, 0.0)


def _causal_softmax_blocks(i, tq, heads):
    def block(j, carry, masked):
        new = []
        for (score_fn, pv_fn, _), (m, l, acc) in zip(heads, carry):
            sc = score_fn(j)
            if masked:
                keep = lax.broadcasted_iota(jnp.int32, sc.shape, 0) >= lax.broadcasted_iota(jnp.int32, sc.shape, 1)
                sc = jnp.where(keep, sc, -1e30)
            m_new = jnp.maximum(m, jnp.max(sc, axis=1, keepdims=True))
            a = jnp.exp(m - m_new)
            p = jnp.exp(sc - m_new)
            new.append((m_new, a * l + jnp.sum(p, axis=1, keepdims=True), a * acc + pv_fn(j, p.astype(BF16))))
        return tuple(new)

    init = tuple((jnp.full((tq, 1), -1e30, F32), jnp.zeros((tq, 1), F32), jnp.zeros((tq, dv), F32)) for _, _, dv in heads)
    carry = lax.fori_loop(0, i, lambda j, c: block(j, c, False), init)
    return [(acc / l, m + jnp.log(l)) for m, l, acc in block(i, carry, True)]


def fox_attn_fwd(qkv, cum, cum_rows, nb, name, hosted=None):
    t = qkv.shape[0]
    s = t // nb
    tq = ATTN_TILE
    nq = s // tq
    npairs = FOX_HEADS // 2
    scale = FOX_HD ** -0.5

    def body(q_ref, k_ref, v_ref, cum_ref, cr_ref, o_ref, lse_ref):
        i = pl.program_id(1)
        hp = pl.program_id(2)

        @pl.when(hp == 0)
        def _():
            lse_ref[...] = jnp.zeros_like(lse_ref)

        q = q_ref[...]
        low = lax.broadcasted_iota(jnp.int32, q.shape, 1) < FOX_HD
        cum_t = cum_ref[...]

        def rows_of(j):
            return pl.ds(pl.multiple_of(j * tq, tq), tq)

        def head(a):
            hd = 2 * hp + a
            qa = jnp.where(low if a == 0 else jnp.logical_not(low), q, jnp.zeros_like(q))
            fq = _pick_lane(cum_t, hd)
            return (lambda j: _dot_nt(qa, k_ref[rows_of(j), :]) * scale + fq - _pick_row(cr_ref[j], hd),
                    lambda j, p: _dot(p, v_ref[rows_of(j), :]), 2 * FOX_HD)

        (o_0, lse_0), (o_1, lse_1) = _causal_softmax_blocks(i, tq, [head(0), head(1)])
        o_ref[...] = jnp.where(low, o_0, o_1).astype(BF16)
        lse_ref[...] = _put_lane(_put_lane(lse_ref[...], 2 * hp, lse_0), 2 * hp + 1, lse_1)

    return _call(
        body, name, (nb, nq, npairs),
        [pl.BlockSpec((tq, 128), lambda b, i, hp: (b * nq + i, hp)),
         pl.BlockSpec((s, 128), lambda b, i, hp: (b, npairs + hp)),
         pl.BlockSpec((s, 128), lambda b, i, hp: (b, 2 * npairs + hp)),
         pl.BlockSpec((tq, 128), lambda b, i, hp: (b * nq + i, 0)),
         pl.BlockSpec((nq, 16, tq), lambda b, i, hp: (b, 0, 0))],
        [pl.BlockSpec((tq, 128), lambda b, i, hp: (b * nq + i, hp)),
         pl.BlockSpec((tq, 128), lambda b, i, hp: (b * nq + i, 0))],
        [_sds((t, D_MODEL), BF16), _sds((t, 128), F32)], (qkv, qkv, qkv, cum, cum_rows), hosted=hosted)


def mla_attn_fwd(q, kn, kr2, v, nb, name, hosted=None):
    t = q.shape[0]
    s = t // nb
    tq = ATTN_TILE
    nq = s // tq
    npairs = MLA_HEADS // 2
    scale = (MLA_NOPE + MLA_ROPE) ** -0.5

    def body(qn_ref, qr_ref, kn_ref, kr_ref, v_ref, o_ref, lse_ref):
        i = pl.program_id(1)
        hp = pl.program_id(2)

        @pl.when(hp == 0)
        def _():
            lse_ref[...] = jnp.zeros_like(lse_ref)

        qr = qr_ref[...]
        low = lax.broadcasted_iota(jnp.int32, qr.shape, 1) < MLA_ROPE

        def rows_of(j):
            return pl.ds(pl.multiple_of(j * tq, tq), tq)

        def head(a):
            cols = slice(a * MLA_NOPE, (a + 1) * MLA_NOPE)
            q_cat = jnp.concatenate([qn_ref[:, cols], jnp.where(low if a == 0 else jnp.logical_not(low), qr,
                                                                jnp.zeros_like(qr))], axis=1)
            return (lambda j: _dot_nt(q_cat, jnp.concatenate([kn_ref[rows_of(j), cols], kr_ref[rows_of(j), :]], axis=1)) * scale,
                    lambda j, p: _dot(p, v_ref[rows_of(j), cols]), MLA_V)

        (o_0, lse_0), (o_1, lse_1) = _causal_softmax_blocks(i, tq, [head(0), head(1)])
        o_ref[...] = jnp.concatenate([o_0, o_1], axis=1).astype(BF16)
        lse_ref[...] = _put_lane(_put_lane(lse_ref[...], 2 * hp, lse_0), 2 * hp + 1, lse_1)

    wide = 2 * MLA_NOPE
    return _call(
        body, name, (nb, nq, npairs),
        [pl.BlockSpec((tq, wide), lambda b, i, hp: (b * nq + i, hp)),
         pl.BlockSpec((tq, 128), lambda b, i, hp: (b * nq + i, MLA_HEADS + hp)),
         pl.BlockSpec((s, wide), lambda b, i, hp: (b, hp)),
         pl.BlockSpec((s, 128), lambda b, i, hp: (b, 0)),
         pl.BlockSpec((s, wide), lambda b, i, hp: (b, hp))],
        [pl.BlockSpec((tq, wide), lambda b, i, hp: (b * nq + i, hp)),
         pl.BlockSpec((tq, 128), lambda b, i, hp: (b * nq + i, 0))],
        [_sds((t, MLA_HEADS * MLA_V), BF16), _sds((t, 128), F32)], (q, q, kn, kr2, v), hosted=hosted)


def rows16(a, name):
    t = a.shape[0]
    tq = ATTN_TILE

    def body(a_ref, o_ref):
        o_ref[...] = a_ref[...].T[:16, :]

    return pl.pallas_call(
        body, name=name, grid=(t // tq,), in_specs=[pl.BlockSpec((tq, 128), lambda n: (n, 0))],
        out_specs=pl.BlockSpec((None, 16, tq), lambda n: (n, 0, 0)), out_shape=_sds((t // tq, 16, tq), F32),
        compiler_params=_cp(1),
    )(a)


def tokens128(rows, onehot, name):
    nblk, _, tq = rows.shape

    def body(r_ref, e_ref, o_ref):
        o_ref[...] = lax.dot_general(r_ref[...], e_ref[...], (((0,), (0,)), ((), ())), preferred_element_type=F32,
                                     precision=lax.Precision.HIGHEST)

    return pl.pallas_call(
        body, name=name, grid=(nblk,),
        in_specs=[pl.BlockSpec((None, 16, tq), lambda n: (n, 0, 0)), pl.BlockSpec((16, 128), lambda n: (0, 0))],
        out_specs=pl.BlockSpec((tq, 128), lambda n: (n, 0)), out_shape=_sds((nblk * tq, 128), F32),
        compiler_params=_cp(1),
    )(rows, onehot)


def _layer_norm(z, g, b):
    mu = jnp.mean(z, axis=-1, keepdims=True)
    zc = z - mu
    rstd = lax.rsqrt(jnp.mean(zc * zc, axis=-1, keepdims=True) + EPS)
    xhat = zc * rstd
    return xhat * g + b, xhat, rstd


def linear_resid_ln(a, w, x_in, gate, ln_g, ln_b, name):
    t, kdim = a.shape
    d = w.shape[1]
    tm = TOKEN_TILE
    tps = (t // gate.shape[0]) // tm

    def body(a_ref, w_ref, x_ref, gt_ref, g_ref, b_ref, y_ref, xo_ref):
        y = _dot(a_ref[...], w_ref[...])
        y_ref[...] = y
        z = ALPHA * x_ref[...] + (1.0 + gt_ref[...]) * y
        xo_ref[...] = _layer_norm(z, g_ref[...], b_ref[...])[0]

    rows = pl.BlockSpec((tm, d), lambda i: (i, 0))
    vec = pl.BlockSpec((1, d), lambda i: (0, 0))
    return pl.pallas_call(
        body, name=name, grid=(t // tm,),
        in_specs=[pl.BlockSpec((tm, kdim), lambda i: (i, 0)), pl.BlockSpec((kdim, d), lambda i: (0, 0)), rows,
                  pl.BlockSpec((None, 1, d), lambda i: (i // tps, 0, 0)), vec, vec],
        out_specs=[rows, rows], out_shape=[_sds((t, d), F32), _sds((t, d), F32)],
        compiler_params=_cp(1),
    )(a, w, x_in, gate, ln_g, ln_b)


def ffn_fwd(x_in, shift, scale, gate, wg, wu, wd, ln_g, ln_b, name, hosted=None):
    t, d = x_in.shape
    c, _, fc = wg.shape
    tm = TOKEN_TILE
    tps = (t // gate.shape[0]) // tm

    def body(x_ref, sh_ref, sc_ref, gt_ref, wg_ref, wu_ref, wd_ref, g_ref, b_ref,
             u_ref, hg_ref, hu_ref, y_ref, xo_ref, acc_ref):
        cc = pl.program_id(1)

        @pl.when(cc == 0)
        def _():
            u_ref[...] = (x_ref[...] * (1.0 + sc_ref[...]) + sh_ref[...]).astype(BF16)
            acc_ref[...] = jnp.zeros_like(acc_ref)

        u = u_ref[...]
        hg = _dot(u, wg_ref[...])
        hu = _dot(u, wu_ref[...])
        hg_ref[...] = hg.astype(BF16)
        hu_ref[...] = hu.astype(BF16)
        act = (hg * jax.nn.sigmoid(hg) * hu).astype(BF16)
        acc_ref[...] += _dot(act, wd_ref[...])

        @pl.when(cc == c - 1)
        def _():
            y = acc_ref[...]
            y_ref[...] = y
            z = ALPHA * x_ref[...] + (1.0 + gt_ref[...]) * y
            xo_ref[...] = _layer_norm(z, g_ref[...], b_ref[...])[0]

    rows = pl.BlockSpec((tm, d), lambda i, cc: (i, 0))
    bvec = pl.BlockSpec((None, 1, d), lambda i, cc: (i // tps, 0, 0))
    vec = pl.BlockSpec((1, d), lambda i, cc: (0, 0))
    hspec = pl.BlockSpec((None, tm, fc), lambda i, cc: (cc, i, 0))
    wcol = pl.BlockSpec((None, d, fc), lambda i, cc: (cc, 0, 0))
    return _call(
        body, name, (t // tm, c),
        [rows, bvec, bvec, bvec, wcol, wcol, pl.BlockSpec((None, fc, d), lambda i, cc: (cc, 0, 0)), vec, vec],
        [rows, hspec, hspec, rows, rows],
        [_sds((t, d), BF16), _sds((c, t, fc), BF16), _sds((c, t, fc), BF16), _sds((t, d), F32), _sds((t, d), F32)],
        (x_in, shift, scale, gate, wg, wu, wd, ln_g, ln_b), scratch_shapes=[pltpu.VMEM((tm, d), F32)], hosted=hosted)


def fox_gate_fwd(hf, b_f, tri, n_batch, name):
    t, n = hf.shape
    blk = tri.shape[0]
    nb = (t // n_batch) // blk

    def body(hf_ref, b_ref, tri_ref, o_ref, carry_ref):
        @pl.when(pl.program_id(1) == 0)
        def _():
            carry_ref[...] = jnp.zeros_like(carry_ref)

        xx = hf_ref[...] + b_ref[...]
        lf = jnp.minimum(xx, 0.0) - jnp.log(1.0 + jnp.exp(-jnp.abs(xx)))
        cum = _dot_f32(tri_ref[...], lf) + carry_ref[...]
        o_ref[...] = cum
        carry_ref[...] = cum[blk - 1:blk, :]

    return pl.pallas_call(
        body, name=name, grid=(n_batch, nb),
        in_specs=[pl.BlockSpec((blk, n), lambda bb, i: (bb * nb + i, 0)), pl.BlockSpec((1, n), lambda bb, i: (0, 0)),
                  pl.BlockSpec((blk, blk), lambda bb, i: (0, 0))],
        out_specs=pl.BlockSpec((blk, n), lambda bb, i: (bb * nb + i, 0)),
        out_shape=_sds((t, n), F32), scratch_shapes=[pltpu.VMEM((1, n), F32)],
        compiler_params=_cp(2),
    )(hf, b_f, tri)


def loss_grad(x_out, target, name):
    t, d = x_out.shape
    tm = TOKEN_TILE

    def body(x_ref, t_ref, g_ref, l_ref):
        @pl.when(pl.program_id(0) == 0)
        def _():
            l_ref[...] = jnp.zeros_like(l_ref)

        err = x_ref[...] - t_ref[...]
        g_ref[...] = err / d
        l_ref[...] += jnp.sum(err * err, axis=0, keepdims=True)

    rows = pl.BlockSpec((tm, d), lambda i: (i, 0))
    return pl.pallas_call(
        body, name=name, grid=(t // tm,), in_specs=[rows, rows],
        out_specs=[rows, pl.BlockSpec((1, d), lambda i: (0, 0))],
        out_shape=[_sds((t, d), F32), _sds((1, d), F32)], compiler_params=_cp(1),
    )(x_out, target)


def ln_bwd(dxo, x_in, y, gate, ln_g, name):
    t, d = dxo.shape
    nb = gate.shape[0]
    tm = TOKEN_TILE
    tps = (t // nb) // tm

    def body(dxo_ref, x_ref, y_ref, gt_ref, g_ref, dz_ref, dy_ref, dg_ref, db_ref, dgt_ref):
        i = pl.program_id(0)

        @pl.when(i == 0)
        def _():
            dg_ref[...] = jnp.zeros_like(dg_ref)
            db_ref[...] = jnp.zeros_like(db_ref)

        @pl.when(i % tps == 0)
        def _():
            dgt_ref[...] = jnp.zeros_like(dgt_ref)

        yy = y_ref[...]
        g1 = 1.0 + gt_ref[...]
        z = ALPHA * x_ref[...] + g1 * yy
        _, xhat, rstd = _layer_norm(z, 1.0, 0.0)
        dxo_v = dxo_ref[...]
        dg_ref[...] += jnp.sum(dxo_v * xhat, axis=0, keepdims=True)
        db_ref[...] += jnp.sum(dxo_v, axis=0, keepdims=True)
        dxh = dxo_v * g_ref[...]
        dz = rstd * (dxh - jnp.mean(dxh, axis=-1, keepdims=True) - xhat * jnp.mean(dxh * xhat, axis=-1, keepdims=True))
        dz_ref[...] = dz
        dy_ref[...] = (g1 * dz).astype(BF16)
        dgt_ref[...] += jnp.sum(dz * yy, axis=0, keepdims=True)

    rows = pl.BlockSpec((tm, d), lambda i: (i, 0))
    vec = pl.BlockSpec((1, d), lambda i: (0, 0))
    bvec = pl.BlockSpec((None, 1, d), lambda i: (i // tps, 0, 0))
    return pl.pallas_call(
        body, name=name, grid=(t // tm,), in_specs=[rows, rows, rows, bvec, vec],
        out_specs=[rows, rows, vec, vec, bvec],
        out_shape=[_sds((t, d), F32), _sds((t, d), BF16), _sds((1, d), F32), _sds((1, d), F32), _sds((nb, 1, d), F32)],
        compiler_params=_cp(1),
    )(dxo, x_in, y, gate, ln_g)


def _mod_bwd_tail(du, dz_ref, x_ref, sc_ref, dx_ref, dsc_ref, dsh_ref, first):
    @pl.when(first)
    def _():
        dsc_ref[...] = jnp.zeros_like(dsc_ref)
        dsh_ref[...] = jnp.zeros_like(dsh_ref)

    dx_ref[...] = ALPHA * dz_ref[...] + du * (1.0 + sc_ref[...])
    dsc_ref[...] += jnp.sum(du * x_ref[...], axis=0, keepdims=True)
    dsh_ref[...] += jnp.sum(du, axis=0, keepdims=True)


def ffn_bwd(dy, hg, hu, wg, wu, wd, dz, x_in, scale, name, hosted=None):
    t, d = dy.shape
    c, _, fc = wg.shape
    nb = scale.shape[0]
    tm = TOKEN_TILE
    tps = (t // nb) // tm

    def body(dy_ref, hg_ref, hu_ref, wg_ref, wu_ref, wd_ref, dz_ref, x_ref, sc_ref,
             dhg_ref, dhu_ref, act_ref, dx_ref, dsc_ref, dsh_ref, acc_ref):
        i = pl.program_id(0)
        cc = pl.program_id(1)

        @pl.when(cc == 0)
        def _():
            acc_ref[...] = jnp.zeros_like(acc_ref)

        hgv = hg_ref[...].astype(F32)
        huv = hu_ref[...].astype(F32)
        da = _dot_nt(dy_ref[...], wd_ref[...])
        sg = jax.nn.sigmoid(hgv)
        sl = hgv * sg
        act_ref[...] = (sl * huv).astype(BF16)
        dhu = (da * sl).astype(BF16)
        dhg = (da * huv * (sg * (1.0 + hgv * (1.0 - sg)))).astype(BF16)
        dhu_ref[...] = dhu
        dhg_ref[...] = dhg
        acc_ref[...] += _dot_nt(dhg, wg_ref[...]) + _dot_nt(dhu, wu_ref[...])

        @pl.when(cc == c - 1)
        def _():
            _mod_bwd_tail(acc_ref[...], dz_ref, x_ref, sc_ref, dx_ref, dsc_ref, dsh_ref, i % tps == 0)

    rows = pl.BlockSpec((tm, d), lambda i, cc: (i, 0))
    bvec = pl.BlockSpec((None, 1, d), lambda i, cc: (i // tps, 0, 0))
    hspec = pl.BlockSpec((None, tm, fc), lambda i, cc: (cc, i, 0))
    wcol = pl.BlockSpec((None, d, fc), lambda i, cc: (cc, 0, 0))
    return _call(
        body, name, (t // tm, c),
        [rows, hspec, hspec, wcol, wcol, pl.BlockSpec((None, fc, d), lambda i, cc: (cc, 0, 0)), rows, rows, bvec],
        [hspec, hspec, hspec, rows, bvec, bvec],
        [_sds((c, t, fc), BF16), _sds((c, t, fc), BF16), _sds((c, t, fc), BF16), _sds((t, d), F32),
         _sds((nb, 1, d), F32), _sds((nb, 1, d), F32)],
        (dy, hg, hu, wg, wu, wd, dz, x_in, scale), scratch_shapes=[pltpu.VMEM((tm, d), F32)], hosted=hosted)


def linear_nt_mod_bwd(pairs, dz, x_in, scale, name):
    t, d = dz.shape
    nb = scale.shape[0]
    tm = TOKEN_TILE
    tps = (t // nb) // tm
    npairs = len(pairs)

    def body(*refs):
        dh_refs = refs[:npairs]
        w_refs = refs[npairs:2 * npairs]
        dz_ref, x_ref, sc_ref, dx_ref, dsc_ref, dsh_ref = refs[2 * npairs:]
        du = None
        for (_, _, blk), dh_ref, w_ref in zip(pairs, dh_refs, w_refs):
            dh = dh_ref[...].astype(BF16)
            term = _dot_nt(dh, w_ref[...]) if blk is None else _dot(dh, w_ref[...])
            du = term if du is None else du + term
        _mod_bwd_tail(du, dz_ref, x_ref, sc_ref, dx_ref, dsc_ref, dsh_ref, pl.program_id(0) % tps == 0)

    rows = pl.BlockSpec((tm, d), lambda i: (i, 0))
    bvec = pl.BlockSpec((None, 1, d), lambda i: (i // tps, 0, 0))
    in_specs = [pl.BlockSpec((tm, dh.shape[1]), lambda i: (i, 0)) for dh, _, _ in pairs]
    for dh, w, blk in pairs:
        if blk is None:
            in_specs.append(pl.BlockSpec(w.shape, lambda i: (0, 0)))
        else:
            in_specs.append(pl.BlockSpec((dh.shape[1], d), lambda i, blk=blk: (blk, 0)))
    in_specs += [rows, rows, bvec]
    return pl.pallas_call(
        body, name=name, grid=(t // tm,), in_specs=in_specs,
        out_specs=[rows, bvec, bvec],
        out_shape=[_sds((t, d), F32), _sds((nb, 1, d), F32), _sds((nb, 1, d), F32)],
        compiler_params=_cp(1),
    )(*[dh for dh, _, _ in pairs], *[w for _, w, _ in pairs], dz, x_in, scale)


def linear_nt_delta(dy, w_o, o, head_sel, name):
    t, d = dy.shape
    hdv = w_o.shape[0]
    tm = TOKEN_TILE

    def body(dy_ref, w_ref, o_ref, sel_ref, do_ref, dl_ref):
        do = _dot_nt(dy_ref[...], w_ref[...])
        do_ref[...] = do.astype(BF16)
        dl_ref[...] = _dot_f32(do * o_ref[...].astype(F32), sel_ref[...])

    return pl.pallas_call(
        body, name=name, grid=(t // tm,),
        in_specs=[pl.BlockSpec((tm, d), lambda i: (i, 0)), pl.BlockSpec((hdv, d), lambda i: (0, 0)),
                  pl.BlockSpec((tm, hdv), lambda i: (i, 0)), pl.BlockSpec(head_sel.shape, lambda i: (0, 0))],
        out_specs=[pl.BlockSpec((tm, hdv), lambda i: (i, 0)), pl.BlockSpec((tm, 128), lambda i: (i, 0))],
        out_shape=[_sds((t, hdv), BF16), _sds((t, 128), F32)], compiler_params=_cp(1),
    )(dy, w_o, o, head_sel)


def _attn_bwd_blocks(j, nk, tk, scale, heads):
    def block(i, carry, masked):
        new = []
        for hd, (dk_acc, dv_acc, dfk_acc) in zip(heads, carry):
            qb = hd["q"](i)
            dob = hd["do"](i)
            lse_row, dl_row = hd["rows"](i)
            st = _dot_nt(hd["k"], qb) * scale
            if hd["bias"] is not None:
                fq_row, fk_col = hd["bias"](i)
                st = st + fq_row - fk_col
            if masked:
                keep = lax.broadcasted_iota(jnp.int32, st.shape, 1) >= lax.broadcasted_iota(jnp.int32, st.shape, 0)
                st = jnp.where(keep, st, -1e30)
            pt = jnp.exp(st - lse_row)
            dv_acc = dv_acc + _dot(pt.astype(BF16), dob)
            dst = pt * (_dot_nt(hd["v"], dob) - dl_row)
            if hd["add_dfq"] is not None:
                dfk_acc = dfk_acc - jnp.sum(dst, axis=1, keepdims=True)
                hd["add_dfq"](i, jnp.sum(dst, axis=0, keepdims=True))
            dsb = (dst * scale).astype(BF16)
            dk_acc = dk_acc + _dot(dsb, qb)
            hd["add_dq"](i, _dot_tn(dsb, hd["k"]))
            new.append((dk_acc, dv_acc, dfk_acc))
        return tuple(new)

    init = tuple((jnp.zeros((tk, hd["k"].shape[1]), F32), jnp.zeros((tk, hd["v"].shape[1]), F32), jnp.zeros((tk, 1), F32))
                 for hd in heads)
    carry = block(j, init, True)
    return lax.fori_loop(j + 1, nk, lambda i, c: block(i, c, False), carry)


def fox_attn_bwd(qkv, do, cum, cum_rows, lse_rows, delta_rows, nb, name, hosted=None):
    t = qkv.shape[0]
    s = t // nb
    tk = ATTN_TILE
    nk = s // tk
    npairs = FOX_HEADS // 2
    scale = FOX_HD ** -0.5

    def body(q_ref, k_ref, v_ref, do_ref, cum_ref, cr_ref, lr_ref, dr_ref, dq_ref, dk_ref, dv_ref, dfq_ref, dfk_ref):
        hp = pl.program_id(1)
        j = pl.program_id(2)

        @pl.when(j == 0)
        def _():
            dq_ref[...] = jnp.zeros_like(dq_ref)

        @pl.when((j == 0) & (hp == 0))
        def _():
            dfq_ref[...] = jnp.zeros_like(dfq_ref)
            dfk_ref[...] = jnp.zeros_like(dfk_ref)

        kb = k_ref[...]
        vb = v_ref[...]
        low = lax.broadcasted_iota(jnp.int32, kb.shape, 1) < FOX_HD
        cum_t = cum_ref[...]

        def rows_of(i):
            return pl.ds(pl.multiple_of(i * tk, tk), tk)

        def add_dq(i, val):
            dq_ref[rows_of(i), :] += val

        def head(a):
            hd = 2 * hp + a
            half = low if a == 0 else jnp.logical_not(low)
            fk = _pick_lane(cum_t, hd)

            def add_dfq(i, val):
                dfq_ref[i] = _put_row(dfq_ref[i], hd, val)

            return dict(q=lambda i: q_ref[rows_of(i), :], do=lambda i: do_ref[rows_of(i), :],
                        k=jnp.where(half, kb, jnp.zeros_like(kb)), v=jnp.where(half, vb, jnp.zeros_like(vb)),
                        rows=lambda i: (_pick_row(lr_ref[i], hd), _pick_row(dr_ref[i], hd)),
                        bias=lambda i: (_pick_row(cr_ref[i], hd), fk), add_dq=add_dq, add_dfq=add_dfq)

        (dk_0, dv_0, dfk_0), (dk_1, dv_1, dfk_1) = _attn_bwd_blocks(j, nk, tk, scale, [head(0), head(1)])
        dk_ref[...] = jnp.where(low, dk_0, dk_1).astype(BF16)
        dv_ref[...] = jnp.where(low, dv_0, dv_1).astype(BF16)
        for a, dfk_a in ((0, dfk_0), (1, dfk_1)):
            dfk_ref[j] = _put_row(dfk_ref[j], 2 * hp + a, jnp.broadcast_to(dfk_a, (tk, 128)).T[0:1, :])

    rowsp = pl.BlockSpec((nk, 16, tk), lambda b, hp, j: (b, 0, 0))
    return _call(
        body, name, (nb, npairs, nk),
        [pl.BlockSpec((s, 128), lambda b, hp, j: (b, hp)),
         pl.BlockSpec((tk, 128), lambda b, hp, j: (b * nk + j, npairs + hp)),
         pl.BlockSpec((tk, 128), lambda b, hp, j: (b * nk + j, 2 * npairs + hp)),
         pl.BlockSpec((s, 128), lambda b, hp, j: (b, hp)),
         pl.BlockSpec((tk, 128), lambda b, hp, j: (b * nk + j, 0)),
         rowsp, rowsp, rowsp],
        [pl.BlockSpec((s, 128), lambda b, hp, j: (b, hp)),
         pl.BlockSpec((tk, 128), lambda b, hp, j: (b * nk + j, hp)),
         pl.BlockSpec((tk, 128), lambda b, hp, j: (b * nk + j, hp)),
         rowsp, rowsp],
        [_sds((t, D_MODEL), F32), _sds((t, D_MODEL), BF16), _sds((t, D_MODEL), BF16),
         _sds((t // tk, 16, tk), F32), _sds((t // tk, 16, tk), F32)],
        (qkv, qkv, qkv, do, cum, cum_rows, lse_rows, delta_rows), hosted=hosted)


def mla_attn_bwd(q, kn, kr2, v, do, lse_rows, delta_rows, nb, name, hosted=None):
    t = q.shape[0]
    s = t // nb
    tk = ATTN_TILE
    nk = s // tk
    npairs = MLA_HEADS // 2
    scale = (MLA_NOPE + MLA_ROPE) ** -0.5

    def body(qn_ref, qr_ref, kn_ref, kr_ref, v_ref, do_ref, lr_ref, dr_ref, dqn_ref, dqr_ref, dkn_ref, dkr_ref, dv_ref):
        hp = pl.program_id(1)
        j = pl.program_id(2)

        @pl.when(j == 0)
        def _():
            dqn_ref[...] = jnp.zeros_like(dqn_ref)
            dqr_ref[...] = jnp.zeros_like(dqr_ref)

        low = lax.broadcasted_iota(jnp.int32, (tk, 128), 1) < MLA_ROPE
        kr = kr_ref[...]

        def rows_of(i):
            return pl.ds(pl.multiple_of(i * tk, tk), tk)

        def head(a):
            cols = slice(a * MLA_NOPE, (a + 1) * MLA_NOPE)
            mine = low if a == 0 else jnp.logical_not(low)

            def q_fn(i):
                qr = qr_ref[rows_of(i), :]
                return jnp.concatenate([qn_ref[rows_of(i), cols], jnp.where(mine, qr, jnp.zeros_like(qr))], axis=1)

            def add_dq(i, val):
                dqn_ref[rows_of(i), cols] += val[:, :MLA_NOPE]
                dqr_ref[rows_of(i), cols] += val[:, MLA_NOPE:]

            return dict(q=q_fn, do=lambda i: do_ref[rows_of(i), cols], k=jnp.concatenate([kn_ref[:, cols], kr], axis=1),
                        v=v_ref[:, cols], rows=lambda i: (_pick_row(lr_ref[i], 2 * hp + a), _pick_row(dr_ref[i], 2 * hp + a)),
                        bias=None, add_dq=add_dq, add_dfq=None)

        (dk_0, dv_0, _), (dk_1, dv_1, _) = _attn_bwd_blocks(j, nk, tk, scale, [head(0), head(1)])
        dkn_ref[...] = jnp.concatenate([dk_0[:, :MLA_NOPE], dk_1[:, :MLA_NOPE]], axis=1).astype(BF16)
        dkr_ref[...] = jnp.concatenate([dk_0[:, MLA_NOPE:], dk_1[:, MLA_NOPE:]], axis=1).astype(BF16)
        dv_ref[...] = jnp.concatenate([dv_0, dv_1], axis=1).astype(BF16)

    wide = 2 * MLA_NOPE
    full = pl.BlockSpec((s, wide), lambda b, hp, j: (b, hp))
    blk = pl.BlockSpec((tk, wide), lambda b, hp, j: (b * nk + j, hp))
    rowsp = pl.BlockSpec((nk, 16, tk), lambda b, hp, j: (b, 0, 0))
    total = MLA_HEADS * MLA_V
    return _call(
        body, name, (nb, npairs, nk),
        [full, pl.BlockSpec((s, 128), lambda b, hp, j: (b, MLA_HEADS + hp)), blk,
         pl.BlockSpec((tk, 128), lambda b, hp, j: (b * nk + j, 0)), blk, full, rowsp, rowsp],
        [full, full, blk, blk, blk],
        [_sds((t, total), F32), _sds((t, total), F32), _sds((t, total), BF16), _sds((t, total), BF16),
         _sds((t, total), BF16)],
        (q, q, kn, kr2, v, do, lse_rows, delta_rows), hosted=hosted)


def mla_mid_bwd(dqn, dqr, dkn, dv, dkr_heads, h, g_q, g_kv, w_uq, w_uk, w_uv, cos8, sin8, cos64, sin64s, swap64,
                heads_to_rope, head_sum, name):
    t = h.shape[0]
    tm = TOKEN_TILE
    hq = MLA_HEADS * MLA_NOPE
    hr = MLA_HEADS * MLA_ROPE // 2
    nq = w_uq.shape[1]

    def body(dqn_ref, dqr_ref, dkn_ref, dv_ref, dkr_ref, h_ref, gq_ref, gkv_ref, wuq_ref, wuk_ref, wuv_ref,
             c8_ref, s8_ref, c64_ref, s64_ref, sw_ref, hp_ref, hs_ref, dh_ref, dqp_ref, dgq_ref, dgkv_ref):
        @pl.when(pl.program_id(0) == 0)
        def _():
            dgq_ref[...] = jnp.zeros_like(dgq_ref)
            dgkv_ref[...] = jnp.zeros_like(dgkv_ref)

        drot = _dot(dqr_ref[...].astype(BF16), hp_ref[...])
        o1 = drot[:, :hr]
        o2 = drot[:, hr:]
        cs = c8_ref[...]
        sn = s8_ref[...]
        dqp = jnp.concatenate([dqn_ref[...].astype(BF16), (o1 * cs + o2 * sn).astype(BF16),
                               (o2 * cs - o1 * sn).astype(BF16)], axis=1)
        dqp_ref[...] = dqp
        dcq = _dot_nt(dqp, wuq_ref[...])
        dckv = _dot_nt(dkn_ref[...], wuk_ref[...]) + _dot_nt(dv_ref[...], wuv_ref[...])
        hh = h_ref[...]

        def rms_bwd(hpart, g, dc, dg_ref):
            hhat, rstd = _rms(hpart, None)
            dg_ref[...] += jnp.sum(dc * hhat, axis=0, keepdims=True)
            dcg = dc * g
            return rstd * (dcg - hhat * jnp.mean(dcg * hhat, axis=-1, keepdims=True))

        dhq = rms_bwd(hh[:, :MLA_QR], gq_ref[...], dcq, dgq_ref)
        dhkv = rms_bwd(hh[:, MLA_QR:MLA_QR + MLA_KVR], gkv_ref[...], dckv, dgkv_ref)
        dkr = _dot(dkr_ref[...], hs_ref[...])
        dkr_pre = dkr * c64_ref[...] + _dot_f32(dkr * s64_ref[...], sw_ref[...])
        dh_ref[...] = jnp.concatenate([dhq, dhkv, dkr_pre], axis=1).astype(BF16)

    def rows(n):
        return pl.BlockSpec((tm, n), lambda i: (i, 0))

    def whole(a):
        return pl.BlockSpec(a.shape, lambda i: (0,) * a.ndim)

    return pl.pallas_call(
        body, name=name, grid=(t // tm,),
        in_specs=[rows(hq), rows(hq), rows(hq), rows(hq), rows(hq), rows(h.shape[1]), whole(g_q), whole(g_kv),
                  whole(w_uq), whole(w_uk), whole(w_uv), rows(hr), rows(hr), rows(MLA_ROPE), rows(MLA_ROPE),
                  whole(swap64), whole(heads_to_rope), whole(head_sum)],
        out_specs=[rows(h.shape[1]), rows(nq), pl.BlockSpec((1, MLA_QR), lambda i: (0, 0)),
                   pl.BlockSpec((1, MLA_KVR), lambda i: (0, 0))],
        out_shape=[_sds((t, h.shape[1]), BF16), _sds((t, nq), BF16), _sds((1, MLA_QR), F32), _sds((1, MLA_KVR), F32)],
        compiler_params=_cp(1),
    )(dqn, dqr, dkn, dv, dkr_heads, h, g_q, g_kv, w_uq, w_uk, w_uv, cos8, sin8, cos64, sin64s, swap64,
      heads_to_rope, head_sum)


def fox_gate_bwd(dcum, hf, b_f, triu, n_batch, name):
    t, n = hf.shape
    blk = triu.shape[0]
    nb = (t // n_batch) // blk

    def body(dc_ref, hf_ref, b_ref, tri_ref, o_ref, db_ref, carry_ref):
        @pl.when(pl.program_id(1) == 0)
        def _():
            carry_ref[...] = jnp.zeros_like(carry_ref)

        @pl.when((pl.program_id(0) == 0) & (pl.program_id(1) == 0))
        def _():
            db_ref[...] = jnp.zeros_like(db_ref)

        rc = _dot_f32(tri_ref[...], dc_ref[...]) + carry_ref[...]
        carry_ref[...] = rc[0:1, :]
        dhf = rc * jax.nn.sigmoid(-(hf_ref[...] + b_ref[...]))
        o_ref[...] = dhf.astype(BF16)
        db_ref[...] += jnp.sum(dhf, axis=0, keepdims=True)

    rev = pl.BlockSpec((blk, n), lambda bb, i: (bb * nb + nb - 1 - i, 0))
    return pl.pallas_call(
        body, name=name, grid=(n_batch, nb),
        in_specs=[rev, rev, pl.BlockSpec((1, n), lambda bb, i: (0, 0)), pl.BlockSpec((blk, blk), lambda bb, i: (0, 0))],
        out_specs=[rev, pl.BlockSpec((1, n), lambda bb, i: (0, 0))],
        out_shape=[_sds((t, n), BF16), _sds((1, n), F32)], scratch_shapes=[pltpu.VMEM((1, n), F32)],
        compiler_params=_cp(2),
    )(dcum, hf, b_f, triu)


def wgrad(a, bm, name, with_bf16=False, bt=WGRAD_TOKENS):
    ca, t, kd = a.shape
    cb, _, nd = bm.shape
    c = max(ca, cb)
    bn = nd
    if nd > 1024 and nd % 1024 == 0:
        bn = 1024
    nsteps = t // bt

    def body(a_ref, b_ref, o_ref, *rest):
        @pl.when(pl.program_id(2) == 0)
        def _():
            o_ref[...] = jnp.zeros_like(o_ref)

        o_ref[...] += _dot_tn(a_ref[...].astype(BF16), b_ref[...].astype(BF16))
        if with_bf16:
            @pl.when(pl.program_id(2) == nsteps - 1)
            def _():
                rest[0][...] = o_ref[...].astype(BF16)

    out_spec = pl.BlockSpec((None, kd, bn), lambda cc, n, tt: (cc, 0, n))
    res = pl.pallas_call(
        body, name=name, grid=(c, nd // bn, nsteps),
        in_specs=[pl.BlockSpec((None, bt, kd), lambda cc, n, tt: (cc if ca > 1 else 0, tt, 0)),
                  pl.BlockSpec((None, bt, bn), lambda cc, n, tt: (cc if cb > 1 else 0, tt, n))],
        out_specs=[out_spec, out_spec] if with_bf16 else out_spec,
        out_shape=[_sds((c, kd, nd), F32), _sds((c, kd, nd), BF16)] if with_bf16 else _sds((c, kd, nd), F32),
        compiler_params=_cp(3),
    )(a, bm)
    return res


def ada_mod_part(c_all, ada_w, name):
    nl, d, n = ada_w.shape
    rows = c_all.shape[0]
    tn = 512

    def body(c_ref, w_ref, o_ref):
        cv = c_ref[...]
        act = (cv * jax.nn.sigmoid(cv)).astype(BF16)
        o_ref[...] = _dot(act, w_ref[...].astype(BF16))

    return pl.pallas_call(
        body, name=name, grid=(nl, n // tn),
        in_specs=[pl.BlockSpec((rows, d), lambda l, j: (0, 0)), pl.BlockSpec((None, d, tn), lambda l, j: (l, 0, j))],
        out_specs=pl.BlockSpec((None, rows, tn), lambda l, j: (l, 0, j)),
        out_shape=_sds((nl, rows, n), F32), compiler_params=_cp(2),
    )(c_all, ada_w)


def ada_grad(c_all_t, dmod, name):
    nl, rows, n = dmod.shape
    d = c_all_t.shape[0]
    tn = 512

    def body(c_ref, dm_ref, o_ref):
        cv = c_ref[...]
        act = (cv * jax.nn.sigmoid(cv)).astype(BF16)
        o_ref[...] = _dot(act, dm_ref[...].astype(BF16))

    return pl.pallas_call(
        body, name=name, grid=(nl, n // tn),
        in_specs=[pl.BlockSpec((d, rows), lambda l, j: (0, 0)), pl.BlockSpec((None, rows, tn), lambda l, j: (l, 0, j))],
        out_specs=pl.BlockSpec((None, d, tn), lambda l, j: (l, 0, j)),
        out_shape=_sds((nl, d, n), F32), compiler_params=_cp(2),
    )(c_all_t, dmod)


def sum_leading(a, name):
    g, r, n = a.shape

    def body(a_ref, o_ref):
        acc = a_ref[0]
        for kk in range(1, g):
            acc = acc + a_ref[kk]
        o_ref[...] = acc

    return pl.pallas_call(
        body, name=name, grid=(1,), in_specs=[pl.BlockSpec((g, r, n), lambda i: (0, 0, 0))],
        out_specs=pl.BlockSpec((r, n), lambda i: (0, 0)), out_shape=_sds((r, n), F32), compiler_params=_cp(1),
    )(a)


def adamw(w, g, m, v, name):
    r, n = w.shape
    br = r
    for cand in (512, 256, 128, 64, 32, 16, 8):
        if r % cand == 0 and r > cand and cand * n * 4 <= ADAMW_BLOCK_BYTES:
            br = cand
            break
    c1 = 1.0 - ADAM_B1 ** ADAM_STEP
    c2 = 1.0 - ADAM_B2 ** ADAM_STEP

    def body(w_ref, g_ref, m_ref, v_ref, d_ref, mo_ref, vo_ref):
        gv = g_ref[...]
        mn = ADAM_B1 * m_ref[...] + (1.0 - ADAM_B1) * gv
        vn = ADAM_B2 * v_ref[...] + (1.0 - ADAM_B2) * (gv * gv)
        mo_ref[...] = mn
        vo_ref[...] = vn
        d_ref[...] = -ADAM_LR * ((mn / c1) / (jnp.sqrt(vn / c2) + ADAM_EPS) + ADAM_WD * w_ref[...])

    spec = pl.BlockSpec((br, n), lambda i: (i, 0))
    return pl.pallas_call(
        body, name=name, grid=(r // br,), in_specs=[spec] * 4, out_specs=[spec] * 3,
        out_shape=[_sds((r, n), F32)] * 3, compiler_params=_cp(1),
    )(w, g, m, v)


def all_gather8(x_blk, name):
    m_per, n = x_blk.shape

    def body(x_ref, out_ref, send_sems, recv_sems, local_sem):
        x, y, c = _place()
        me, sibling = (x, y, c), (x, y, 1 - c)
        chips = [(1 - x, y), (x, 1 - y), (1 - x, 1 - y)]

        def rows(px, py, pc):
            return out_ref.at[pl.ds((4 * px + 2 * py + pc) * m_per, m_per), :]

        def copy(k, block, to, src=None):
            return pltpu.make_async_remote_copy(
                src_ref=rows(*block) if src is None else src, dst_ref=rows(*block),
                send_sem=send_sems.at[k], recv_sem=recv_sems.at[k], device_id=to, device_id_type=MESH)

        mine = pltpu.make_async_copy(x_ref, rows(*me), local_sem)
        mine.start()
        first = [copy(0, me, sibling, src=x_ref)]
        first += [copy(1 + j, me, (*chip, c), src=x_ref) for j, chip in enumerate(chips)]
        for cp in first:
            cp.start()
        passed = [copy(4 + j, (*chip, c), sibling) for j, chip in enumerate(chips)]
        for j, chip in enumerate(chips):
            copy(1 + j, (*chip, c), me).wait_recv()
            passed[j].start()
        copy(0, sibling, me).wait_recv()
        for j, chip in enumerate(chips):
            copy(4 + j, (*chip, 1 - c), me).wait_recv()
        for cp in first + passed:
            cp.wait_send()
        mine.wait()

    return pl.pallas_call(
        body, name=name, out_shape=_sds((8 * m_per, n), x_blk.dtype),
        in_specs=[pl.BlockSpec(memory_space=pltpu.VMEM)], out_specs=pl.BlockSpec(memory_space=pltpu.VMEM),
        scratch_shapes=[pltpu.SemaphoreType.DMA((7,)), pltpu.SemaphoreType.DMA((7,)), pltpu.SemaphoreType.DMA],
        compiler_params=pltpu.CompilerParams(vmem_limit_bytes=VMEM_LIMIT),
    )(x_blk)


def _gather_comm(shards):
    nt = len(shards)

    def parts(w_refs, out_refs, sems, finishing):
        send_sems, recv_sems, own_send, own_recv = sems
        x, y, c = _place()
        sibling = (x, y, 1 - c)
        chips = [(1 - x, y), (x, 1 - y), (1 - x, 1 - y)]

        def copy(t, k, block, to, src=None):
            px, py, hh = block
            dst = out_refs[t].at[2 * px + py, hh]
            return pltpu.make_async_remote_copy(
                src_ref=dst if src is None else src, dst_ref=dst,
                send_sem=send_sems.at[6 * t + k], recv_sem=recv_sems.at[6 * t + k], device_id=to, device_id_type=MESH)

        own = [pltpu.make_async_remote_copy(
            src_ref=w_refs[t], dst_ref=out_refs[t].at[2 * x + y], send_sem=own_send.at[t], recv_sem=own_recv.at[t],
            device_id=sibling, device_id_type=MESH) for t in range(nt)]
        first = [copy(t, j, (x, y, c), (*chip, c), src=w_refs[t].at[c]) for t in range(nt) for j, chip in enumerate(chips)]
        if not finishing:
            return own, first
        landed = [copy(t, j, (*chip, c), (x, y, c)) for t in range(nt) for j, chip in enumerate(chips)]
        passed = [copy(t, 3 + j, (*chip, c), sibling) for t in range(nt) for j, chip in enumerate(chips)]
        from_sibling = [copy(t, 3 + j, (*chip, 1 - c), (x, y, c)) for t in range(nt) for j, chip in enumerate(chips)]
        return own, first, landed, passed, from_sibling

    def start(w_refs, out_refs, sems):
        own, first = parts(w_refs, out_refs, sems, False)
        for cp in own + first:
            cp.start()

    def finish(w_refs, out_refs, sems):
        own, first, landed, passed, from_sibling = parts(w_refs, out_refs, sems, True)
        for arrived, fwd in zip(landed, passed):
            arrived.wait_recv()
            fwd.start()
        for cp in from_sibling:
            cp.wait_recv()
        for cp in first + passed:
            cp.wait_send()
        for cp in own:
            cp.wait()

    sems = [pltpu.SemaphoreType.DMA((6 * nt,)), pltpu.SemaphoreType.DMA((6 * nt,)),
            pltpu.SemaphoreType.DMA((nt,)), pltpu.SemaphoreType.DMA((nt,))]
    return _Hosted(list(shards), [_sds((N_CHIPS, *w.shape), w.dtype) for w in shards], sems, start, finish)


def all_gather_chips(shards, name):
    comm = _gather_comm(shards)
    nt = len(shards)

    def body(*refs):
        comm.start(refs[:nt], refs[nt:2 * nt], refs[2 * nt:])
        comm.finish(refs[:nt], refs[nt:2 * nt], refs[2 * nt:])

    hbm = pl.BlockSpec(memory_space=pl.ANY)
    return pl.pallas_call(body, name=name, out_shape=comm.out_shape, in_specs=[hbm] * nt, out_specs=[hbm] * nt,
                          scratch_shapes=comm.sems)(*shards)


def _row_block(r, n, itemsize):
    best = None
    for br in range(16, r + 1, 16):
        if r % br == 0 and br * n * itemsize <= COMM_BLOCK_BYTES:
            best = br
    return r if best is None else best


def _scatter_comm(parts):
    nt = len(parts)

    def copies(p_refs, b_refs, sems, arriving):
        send_sems, recv_sems = sems
        x, y, c = _place()
        me = 4 * x + 2 * y + c
        cps = []
        for t in range(nt):
            for r in range(1, 8):
                tx = 1 - x if r & 4 else x
                ty = 1 - y if r & 2 else y
                tc = 1 - c if r & 1 else c
                src, dst = (2 * x + y, c), 4 * tx + 2 * ty + tc
                if not arriving:
                    src, dst = (2 * tx + ty, tc), me
                cps.append(pltpu.make_async_remote_copy(
                    src_ref=p_refs[t].at[src], dst_ref=b_refs[t].at[dst], send_sem=send_sems.at[7 * t + r - 1],
                    recv_sem=recv_sems.at[7 * t + r - 1], device_id=(tx, ty, tc), device_id_type=MESH))
        return cps

    def start(p_refs, b_refs, sems):
        for cp in copies(p_refs, b_refs, sems, False):
            cp.start()

    def finish(p_refs, b_refs, sems):
        for cp in copies(p_refs, b_refs, sems, True):
            cp.wait_recv()
        for cp in copies(p_refs, b_refs, sems, False):
            cp.wait_send()

    sems = [pltpu.SemaphoreType.DMA((7 * nt,)), pltpu.SemaphoreType.DMA((7 * nt,))]
    return _Hosted(list(parts), [_sds((2 * N_CHIPS, *p.shape[2:]), p.dtype) for p in parts], sems, start, finish)


def scatter_exchange(parts, name):
    comm = _scatter_comm(parts)
    nt = len(parts)

    def body(*refs):
        comm.start(refs[:nt], refs[nt:2 * nt], refs[2 * nt:])
        comm.finish(refs[:nt], refs[nt:2 * nt], refs[2 * nt:])

    hbm = pl.BlockSpec(memory_space=pl.ANY)
    return pl.pallas_call(body, name=name, out_shape=comm.out_shape, in_specs=[hbm] * nt, out_specs=[hbm] * nt,
                          scratch_shapes=comm.sems)(*parts)


def sum_devices(own, recv, place, name, slot=(0, 1, None)):
    _, _, r, n = own.shape
    layer, n_layers, buf = slot
    br = _row_block(r, n, 4 * 8)

    def body(p_ref, o_ref, *rest):
        acc = o_ref[...]
        for kk in range(7):
            acc = acc + rest[kk][...].astype(F32)
        rest[-1][...] = acc

    def arrived(rel):
        return pl.BlockSpec((None, br, n), lambda i, pref: (jnp.bitwise_xor(pref[0], rel), i, 0))

    in_specs = [pl.BlockSpec((None, None, br, n), lambda i, pref: (pref[2], pref[1], i, 0))]
    in_specs += [arrived(rel) for rel in range(1, 8)]
    args = [own] + [recv] * 7
    aliases = {}
    if buf is not None:
        in_specs.append(pl.BlockSpec(memory_space=pl.ANY))
        args.append(buf)
        aliases = {9: 0}
    return pl.pallas_call(
        body, name=name,
        grid_spec=pltpu.PrefetchScalarGridSpec(
            num_scalar_prefetch=1, grid=(r // br,), in_specs=in_specs,
            out_specs=pl.BlockSpec((None, None, br, n), lambda i, pref: (layer, pref[1], i, 0))),
        out_shape=_sds((n_layers, 2, r, n), F32), input_output_aliases=aliases, compiler_params=_cp(1),
    )(place, *args)


def sibling_join_halves(bufs, name):
    nt = len(bufs)
    layers = [bf.shape[0] for bf in bufs]
    first = [sum(layers[:t]) for t in range(nt)]

    def body(*refs):
        o_refs = refs[nt:2 * nt]
        send_sems, recv_sems = refs[2 * nt:]
        x, y, c = _place()

        def copy(t, l, hh):
            return pltpu.make_async_remote_copy(
                src_ref=o_refs[t].at[l, hh], dst_ref=o_refs[t].at[l, hh], send_sem=send_sems.at[first[t] + l],
                recv_sem=recv_sems.at[first[t] + l], device_id=(x, y, 1 - c), device_id_type=MESH)

        cps = [copy(t, l, c) for t in range(nt) for l in range(layers[t])]
        for cp in cps:
            cp.start()
        for t in range(nt):
            for l in range(layers[t]):
                copy(t, l, 1 - c).wait_recv()
        for cp in cps:
            cp.wait_send()

    hbm = pl.BlockSpec(memory_space=pl.ANY)
    return pl.pallas_call(
        body, name=name, out_shape=[_sds(bf.shape, bf.dtype) for bf in bufs],
        in_specs=[hbm] * nt, out_specs=[hbm] * nt, input_output_aliases={t: t for t in range(nt)},
        scratch_shapes=[pltpu.SemaphoreType.DMA((sum(layers),)), pltpu.SemaphoreType.DMA((sum(layers),))],
    )(*bufs)


_SHARD_KIND = {"mla_w_in": "rows", "mla_w_uq": "cols", "mla_w_uk": "cols", "mla_w_uv": "cols", "mla_w_o": "rows",
               "fox_w_in": "cols", "fox_w_o": "rows", "ffn_w_gate": "chunk", "ffn_w_up": "chunk", "ffn_w_down": "chunk"}
_PACKED = tuple(_SHARD_KIND)
_TRANSPOSED = ("ffn_w_gate", "ffn_w_up", "fox_w_in")


def _halves(shard):
    if shard.ndim == 3 and shard.shape[0] == 2:
        return shard
    r, n = shard.shape[-2:]
    return shard.reshape(2, r // 2, n)


def _cols_to_full(g):
    return jnp.transpose(g, (1, 0, 2)).reshape(g.shape[1], -1)


def _full_to_cols(w):
    k, n4 = w.shape
    return jnp.transpose(w.reshape(k, N_CHIPS, n4 // N_CHIPS), (1, 0, 2))


def _uq_perm():
    per = MLA_NOPE + MLA_ROPE
    half = MLA_ROPE // 2
    nope = [h * per + d for h in range(MLA_HEADS) for d in range(MLA_NOPE)]
    r1 = [h * per + MLA_NOPE + r for h in range(MLA_HEADS) for r in range(half)]
    r2 = [h * per + MLA_NOPE + half + r for h in range(MLA_HEADS) for r in range(half)]
    perm = np.array(nope + r1 + r2, dtype=np.int32)
    return perm, np.argsort(perm).astype(np.int32)


def _rope_matrices():
    half = MLA_ROPE // 2
    nr = MLA_HEADS * MLA_ROPE
    to_heads = np.zeros((nr, nr), np.float32)
    from_heads = np.zeros((MLA_HEADS * 128, nr), np.float32)
    for e in range(2):
        for h in range(MLA_HEADS):
            for r in range(half):
                to_heads[e * MLA_HEADS * half + h * half + r, h * MLA_ROPE + e * half + r] = 1.0
                from_heads[h * 128 + e * half + r, e * MLA_HEADS * half + h * half + r] = 1.0
    head_sum = np.tile(np.eye(MLA_ROPE, dtype=np.float32), (2 * MLA_HEADS, 1))
    dup = np.concatenate([np.eye(MLA_ROPE, dtype=np.float32)] * 2, axis=1)
    return to_heads, from_heads, head_sum, dup


def _ffn_weights(gathered):
    return tuple(g.reshape(N_CHIPS, 2 * g.shape[2], g.shape[3]) for g in gathered)


def _fox_weights(gathered):
    w_in, w_o = gathered
    w_in = jnp.transpose(w_in, (0, 2, 1, 3)).reshape(N_CHIPS * w_in.shape[2], 2 * w_in.shape[3])
    return w_in, w_o.reshape(-1, w_o.shape[-1])


def _local_step(x, positions, target, mods, wts, ln_g, ln_b, mla_g_q, mla_g_kv, fox_b_f, shards=None):
    nb, s, d = x.shape
    t = nb * s
    x0 = x.reshape(t, d)
    tgt = target.reshape(t, d)
    perm, inv_perm = _uq_perm()

    half = MLA_ROPE // 2
    inv_freq = ROPE_THETA ** (-jnp.arange(half, dtype=F32) / half)
    ang = positions.astype(F32).reshape(t, 1) * inv_freq
    cos, sin = jnp.cos(ang), jnp.sin(ang)
    cos8, sin8 = jnp.tile(cos, (1, MLA_HEADS)), jnp.tile(sin, (1, MLA_HEADS))
    cos64 = jnp.concatenate([cos, cos], axis=1)
    sin64s = jnp.concatenate([-sin, sin], axis=1)
    swap64 = jnp.asarray(np.roll(np.eye(MLA_ROPE, dtype=np.float32), half, axis=1))
    to_heads, from_heads, head_sum, dup = _rope_matrices()
    to_heads, from_heads = jnp.asarray(to_heads, dtype=BF16), jnp.asarray(from_heads, dtype=BF16)
    head_sum, dup = jnp.asarray(head_sum, dtype=BF16), jnp.asarray(dup, dtype=BF16)
    sel_mla = jnp.asarray(np.pad(np.kron(np.eye(MLA_HEADS, dtype=np.float32), np.ones((MLA_V, 1), np.float32)),
                                 ((0, 0), (0, 128 - MLA_HEADS))))
    sel_fox = jnp.asarray(np.pad(np.kron(np.eye(FOX_HEADS, dtype=np.float32), np.ones((FOX_HD, 1), np.float32)),
                                 ((0, 0), (0, 128 - FOX_HEADS))))
    tri = jnp.asarray(np.tril(np.ones((128, 128), np.float32)))
    triu = jnp.asarray(np.triu(np.ones((128, 128), np.float32)))
    onehot16 = jnp.asarray(np.eye(16, 128, dtype=np.float32))

    def vec(a):
        return a.reshape(1, -1)

    def carried(key):
        return None if shards is None else _gather_comm(shards[key])

    def split(res):
        return (res, None) if shards is None else res

    w_uq_p = wts["mla_w_uq"][:, perm]
    b_f_pad = jnp.pad(fox_b_f.reshape(1, -1), ((0, 0), (0, 128 - FOX_HEADS)))

    sh_a, sc_a, gt_a, sh_f, sc_f, gt_f = mods[0]
    h_in, u_m = mod_linear(x0, sh_a, sc_a, wts["mla_w_in"], F32, "mla_in", emit_u=True)
    q_m, kn_m, v_m, kr2_m, cq_m, ckv_m = mla_mid_fwd(
        h_in, vec(mla_g_q), vec(mla_g_kv), w_uq_p, wts["mla_w_uk"], wts["mla_w_uv"], cos8, sin8, cos64, sin64s, swap64,
        to_heads, dup, "mla_mid")
    (o_m, lse_m), got = split(mla_attn_fwd(q_m, kn_m, kr2_m, v_m, nb, "mla_attn", hosted=carried("ffn0")))
    ffn0_w = wts["ffn"][0] if got is None else _ffn_weights(got)
    y0, x1 = linear_resid_ln(o_m, wts["mla_w_o"], x0, gt_a, vec(ln_g[0, 0]), vec(ln_b[0, 0]), "mla_out")
    (u_f0, hg0, hu0, y1, x2), got = split(ffn_fwd(x1, sh_f, sc_f, gt_f, *ffn0_w, vec(ln_g[0, 1]), vec(ln_b[0, 1]), "ffn0",
                                                  hosted=carried("fox")))
    fox_w_in_t, fox_w_o = (wts["fox_w_in"].T, wts["fox_w_o"]) if got is None else _fox_weights(got)
    fox_w_f_t = jnp.pad(fox_w_in_t[3 * d:], ((0, 128 - FOX_HEADS), (0, 0)))
    sh_a1, sc_a1, gt_a1, sh_f1, sc_f1, gt_f1 = mods[1]
    qkv, u_x = mod_linear(x2, sh_a1, sc_a1, fox_w_in_t, BF16, "fox_qkv", tn=1024, emit_u=True, w_rows=3 * d)
    hf = mod_linear(x2, sh_a1, sc_a1, fox_w_f_t, F32, "fox_f", w_rows=128)
    cum = fox_gate_fwd(hf, b_f_pad, tri, nb, "fox_gate")
    cum_rows = rows16(cum, "fox_cum_rows")
    (o_x, lse_x), got = split(fox_attn_fwd(qkv, cum, cum_rows, nb, "fox_attn", hosted=carried("ffn1")))
    ffn1_w = wts["ffn"][1] if got is None else _ffn_weights(got)
    y2, x3 = linear_resid_ln(o_x, fox_w_o, x2, gt_a1, vec(ln_g[1, 0]), vec(ln_b[1, 0]), "fox_out")
    u_f1, hg1, hu1, y3, x4 = ffn_fwd(x3, sh_f1, sc_f1, gt_f1, *ffn1_w, vec(ln_g[1, 1]), vec(ln_b[1, 1]), "ffn1")
    dx4, sq_err = loss_grad(x4, tgt, "loss")
    loss_part = 0.5 * jnp.sum(sq_err) / d

    parts, recv = {}, {}

    def halves_of(g):
        return g.reshape(N_CHIPS, 2, g.shape[1] // 2, g.shape[2])

    def scatter(keys, sent):
        return None if shards is None else _scatter_comm([sent[k] for k in keys])

    def landed(keys, got):
        if got is not None:
            recv.update(zip(keys, got))

    def ffn_grads(layer, u, dhg, dhu, act, dy):
        sent = {}
        for n, (a_op, b_op) in (("ffn_w_gate", (dhg, u[None])), ("ffn_w_up", (dhu, u[None])), ("ffn_w_down", (act, dy[None]))):
            g32, g16 = wgrad(a_op, b_op, "ffn%d_d%s" % (layer, n[4:]), with_bf16=True)
            parts["%s/%d" % (n, layer)], sent["%s/%d" % (n, layer)] = halves_of(g32), halves_of(g16)
        return sent

    dz3, dy3, dg11, db11, dgt_f1 = ln_bwd(dx4, x3, y3, gt_f1, vec(ln_g[1, 1]), "ffn1_ln_bwd")
    dhg1, dhu1, act1, dx3, dsc_f1, dsh_f1 = ffn_bwd(dy3, hg1, hu1, *ffn1_w, dz3, x3, sc_f1, "ffn1_bwd")
    sent = ffn_grads(1, u_f1, dhg1, dhu1, act1, dy3)
    dz2, dy2, dg10, db10, dgt_a1 = ln_bwd(dx3, x2, y2, gt_a1, vec(ln_g[1, 0]), "fox_ln_bwd")
    do_x, delta_x = linear_nt_delta(dy2, fox_w_o, o_x, sel_fox, "fox_out_bwd")
    (dq_x, dk_x, dv_x, dfq_x, dfk_x), got = split(fox_attn_bwd(
        qkv, do_x, cum, cum_rows, rows16(lse_x, "fox_lse_rows"), rows16(delta_x, "fox_delta_rows"), nb, "fox_attn_bwd",
        hosted=scatter(list(sent), sent)))
    landed(list(sent), got)
    dcum = tokens128(dfq_x + dfk_x, onehot16, "fox_dcum")
    dhf, dbf = fox_gate_bwd(dcum, hf, b_f_pad, triu, nb, "fox_gate_bwd")
    fox_d = [("q", dq_x), ("k", dk_x), ("v", dv_x)]
    dx2, dsc_a1, dsh_a1 = linear_nt_mod_bwd(
        [(dh, fox_w_in_t, i) for i, (_, dh) in enumerate(fox_d)] + [(dhf, fox_w_f_t, 0)], dz2, x2, sc_a1, "fox_in_bwd")
    dw_in_t = [wgrad(dh[None], u_x[None], "fox_dw" + tag)[0] for tag, dh in fox_d]
    dw_in_t.append(wgrad(dhf[None], u_x[None], "fox_dwf")[0][:FOX_HEADS])
    dw_in_t = jnp.concatenate(dw_in_t, axis=0).reshape(N_CHIPS, -1, 2, d // 2)
    parts["fox_w_in"] = jnp.transpose(dw_in_t, (0, 2, 1, 3))
    parts["fox_w_o"] = wgrad(o_x[None], dy2[None], "fox_dwo")[0].reshape(N_CHIPS, 2, -1, d)
    sent = {k: parts[k].astype(BF16) for k in ("fox_w_in", "fox_w_o")}
    dz1, dy1, dg01, db01, dgt_f0 = ln_bwd(dx2, x1, y1, gt_f, vec(ln_g[0, 1]), "ffn0_ln_bwd")
    (dhg0, dhu0, act0, dx1, dsc_f0, dsh_f0), got = split(ffn_bwd(dy1, hg0, hu0, *ffn0_w, dz1, x1, sc_f, "ffn0_bwd",
                                                                 hosted=scatter(list(sent), sent)))
    landed(list(sent), got)
    sent = ffn_grads(0, u_f0, dhg0, dhu0, act0, dy1)
    dz0, dy0, dg00, db00, dgt_a0 = ln_bwd(dx1, x0, y0, gt_a, vec(ln_g[0, 0]), "mla_ln_bwd")
    do_m, delta_m = linear_nt_delta(dy0, wts["mla_w_o"], o_m, sel_mla, "mla_out_bwd")
    parts["mla_w_o"] = wgrad(o_m[None], dy0[None], "mla_dwo")[0].reshape(N_CHIPS, 2, -1, d)
    sent["mla_w_o"] = parts["mla_w_o"].astype(BF16)
    (dqn_m, dqr_m, dkn_m, dkr_m, dv_m), got = split(mla_attn_bwd(
        q_m, kn_m, kr2_m, v_m, do_m, rows16(lse_m, "mla_lse_rows"), rows16(delta_m, "mla_delta_rows"), nb,
        "mla_attn_bwd", hosted=scatter(list(sent), sent)))
    landed(list(sent), got)
    dh_in, dq_pre, dgq, dgkv = mla_mid_bwd(
        dqn_m, dqr_m, dkn_m, dv_m, dkr_m, h_in, vec(mla_g_q), vec(mla_g_kv), w_uq_p, wts["mla_w_uk"],
        wts["mla_w_uv"], cos8, sin8, cos64, sin64s, swap64, from_heads, head_sum, "mla_mid_bwd")
    parts["mla_w_uq"] = halves_of(_full_to_cols(wgrad(cq_m[None], dq_pre[None], "mla_dwuq")[0][:, inv_perm]))
    parts["mla_w_uk"] = halves_of(_full_to_cols(wgrad(ckv_m[None], dkn_m[None], "mla_dwuk")[0]))
    parts["mla_w_uv"] = halves_of(_full_to_cols(wgrad(ckv_m[None], dv_m[None], "mla_dwuv")[0]))
    parts["mla_w_in"] = wgrad(u_m[None], dh_in[None], "mla_dwin")[0].reshape(N_CHIPS, 2, -1, h_in.shape[1])
    dx0, dsc_a0, dsh_a0 = linear_nt_mod_bwd([(dh_in, wts["mla_w_in"], None)], dz0, x0, sc_a, "mla_in_bwd")

    dmods = [(dsh_a0, dsc_a0, dgt_a0, dsh_f0, dsc_f0, dgt_f0), (dsh_a1, dsc_a1, dgt_a1, dsh_f1, dsc_f1, dgt_f1)]
    d_ln_g = jnp.stack([jnp.concatenate([dg00, dg01], axis=0), jnp.concatenate([dg10, dg11], axis=0)])
    d_ln_b = jnp.stack([jnp.concatenate([db00, db01], axis=0), jnp.concatenate([db10, db11], axis=0)])
    return loss_part, dx0.reshape(nb, s, d), (parts, recv), dmods, d_ln_g, d_ln_b, dgq, dgkv, dbf[:, :FOX_HEADS]


def _pad_rows(a, rows):
    return jnp.pad(a, ((0, rows - a.shape[0]), (0, 0)))


def kernel(x, c, positions, mla_w_in, mla_g_q, mla_w_uq, mla_g_kv, mla_w_uk, mla_w_uv, mla_w_o, fox_w_in, fox_b_f, fox_w_o, ada_w, ada_b, ffn_w_gate, ffn_w_up, ffn_w_down, ln_g, ln_b, loss_target, m_mla_w_in, m_mla_g_q, m_mla_w_uq, m_mla_g_kv, m_mla_w_uk, m_mla_w_uv, m_mla_w_o, m_fox_w_in, m_fox_b_f, m_fox_w_o, m_ada_w, m_ada_b, m_ffn_w_gate, m_ffn_w_up, m_ffn_w_down, m_ln_g, m_ln_b, v_mla_w_in, v_mla_g_q, v_mla_w_uq, v_mla_g_kv, v_mla_w_uk, v_mla_w_uv, v_mla_w_o, v_fox_w_in, v_fox_b_f, v_fox_w_o, v_ada_w, v_ada_b, v_ffn_w_gate, v_ffn_w_up, v_ffn_w_down, v_ln_g, v_ln_b):
    args = dict(locals())
    nb, s, d = x.shape
    ax, ay, ac = lax.axis_index("x"), lax.axis_index("y"), lax.axis_index("c")
    chip = 2 * ax + ay
    dev = 2 * chip + ac
    n_dev = 2 * N_CHIPS
    n_all = nb * n_dev

    shard_shapes = {n: (args[n].shape if _SHARD_KIND[n] == "chunk" else args[n].shape[1:]) for n in _PACKED}

    def block(n, layer=None):
        w = args[n].reshape(shard_shapes[n]) if layer is None else args[n][layer]
        return _halves(w.astype(BF16))

    mla_names = [n for n in _PACKED if n.startswith("mla")]
    wts = {}
    for n, g in zip(mla_names, all_gather_chips([block(n) for n in mla_names], "gather_mla")):
        g = g.reshape(N_CHIPS, *shard_shapes[n])
        wts[n] = g.reshape(-1, g.shape[-1]) if _SHARD_KIND[n] == "rows" else _cols_to_full(g)
    ffn_names = ("ffn_w_gate", "ffn_w_up", "ffn_w_down")
    fox_in_t = jnp.swapaxes(fox_w_in, 1, 2)[0].astype(BF16)
    fox_in_t = jnp.stack([fox_in_t[:, :d // 2], fox_in_t[:, d // 2:]])
    shards = {"ffn0": [block(n, 0) for n in ffn_names], "fox": [fox_in_t, block("fox_w_o")],
              "ffn1": [block(n, 1) for n in ffn_names]}

    ln_cols = ln_g.shape[-1]
    ln_blk = jnp.concatenate([ln_g.reshape(2 * DEPTH, ln_cols), ln_b.reshape(2 * DEPTH, ln_cols)], axis=0)
    early = jnp.concatenate([_pad_rows(c, 8), jnp.pad(_pad_rows(ln_blk, 8), ((0, 0), (0, d - ln_cols)))], axis=0)
    early = all_gather8(early, "gather_c_ln").reshape(n_dev, 16, d)
    c_all = early[:, :nb].reshape(n_all, d)
    ln_all = early.reshape(N_CHIPS, 2, 16, d)[:, 0, 8:8 + 4 * DEPTH, :ln_cols]
    ln_all = jnp.transpose(ln_all, (1, 0, 2)).reshape(4 * DEPTH, d)
    ln_g_full = ln_all[:2 * DEPTH].reshape(DEPTH, 2, d)
    ln_b_full = ln_all[2 * DEPTH:].reshape(DEPTH, 2, d)
    mod_part = ada_mod_part(c_all, ada_w, "ada_mod")
    ncol = mod_part.shape[-1]
    mod_g = all_gather8(mod_part.reshape(DEPTH * n_all, ncol), "gather_mod")
    mod_g = mod_g.reshape(N_CHIPS, 2, DEPTH, n_all, ncol)[:, 0]
    mod_full = jnp.transpose(mod_g, (1, 2, 0, 3)).reshape(DEPTH, n_all, N_CHIPS * ncol) + ada_b[:, None, :]
    mod_loc = lax.dynamic_slice_in_dim(mod_full, dev * nb, nb, axis=1)
    mods = [tuple(mod_loc[i, :, k * d:(k + 1) * d].reshape(nb, 1, d) for k in range(6)) for i in range(DEPTH)]

    loss_part, grad_x, (parts, recv), dmods, d_ln_g, d_ln_b, dgq, dgkv, dbf = _local_step(
        x, positions, loss_target, mods, wts, ln_g_full, ln_b_full, mla_g_q[0], mla_g_kv[0], fox_b_f[0], shards)
    loss = lax.psum(loss_part, ("x", "y", "c"))

    dmod_rows = jnp.stack([jnp.concatenate([v_.reshape(nb, d) for v_ in dm], axis=1) for dm in dmods])
    small = jnp.concatenate([
        d_ln_g.reshape(2 * DEPTH, d), d_ln_b.reshape(2 * DEPTH, d),
        jnp.pad(jnp.concatenate([dgq, dgkv, dbf], axis=1), ((0, 0), (0, d - 2 * MLA_QR - FOX_HEADS))),
        dmod_rows.reshape(DEPTH * nb * 6, d)], axis=0)
    n_small = small.shape[0]
    small_rows = -(-n_small // 8) * 8
    small_all = all_gather8(_pad_rows(small, small_rows), "gather_stats").reshape(n_dev, small_rows, d)
    stat_sum = sum_leading(small_all, "sum_stats")
    g_ln_g = lax.dynamic_slice_in_dim(stat_sum[:2 * DEPTH], chip * ln_cols, ln_cols, axis=1).reshape(DEPTH, 2, ln_cols)
    g_ln_b = lax.dynamic_slice_in_dim(stat_sum[2 * DEPTH:4 * DEPTH], chip * ln_cols, ln_cols, axis=1).reshape(DEPTH, 2, ln_cols)
    row = stat_sum[4 * DEPTH]
    g_gq = row[:MLA_QR].reshape(1, MLA_QR)
    g_gkv = row[MLA_QR:2 * MLA_QR].reshape(1, MLA_KVR)
    g_bf = row[2 * MLA_QR:2 * MLA_QR + FOX_HEADS].reshape(1, FOX_HEADS)
    base = 4 * DEPTH + 1
    dmod_all = small_all[:, base:base + DEPTH * nb * 6].reshape(n_dev, DEPTH, nb, 6 * d)
    dmod_all = jnp.transpose(dmod_all, (1, 0, 2, 3)).reshape(DEPTH, n_all, 6 * d)
    g_ada_b = sum_leading(jnp.transpose(dmod_all, (1, 0, 2)), "sum_ada_b")
    dmod_mine = lax.dynamic_slice_in_dim(dmod_all, chip * ncol, ncol, axis=2)
    g_ada_w = ada_grad(c_all.T, dmod_mine, "ada_grad")

    late = [k for k in parts if k not in recv]
    recv.update(zip(late, scatter_exchange([parts[k].astype(BF16) for k in late], "rs_exchange_mla")))
    place = jnp.stack([dev, ac, chip]).astype(jnp.int32)
    bufs = []
    for n in _PACKED:
        if _SHARD_KIND[n] == "chunk":
            buf = None
            for layer in range(DEPTH):
                key = "%s/%d" % (n, layer)
                buf = sum_devices(parts[key], recv[key], place, "rs_sum_%s%d" % (n, layer), slot=(layer, DEPTH, buf))
        else:
            buf = sum_devices(parts[n], recv[n], place, "rs_sum_" + n)
        bufs.append(buf)
    joined = sibling_join_halves(bufs, "rs_join")
    g_big = {n: j.reshape(j.shape[0], 2 * j.shape[2], j.shape[3]) for n, j in zip(_PACKED, joined)}
    j = joined[_PACKED.index("fox_w_in")]
    g_big["fox_w_in"] = jnp.transpose(j, (0, 2, 1, 3)).reshape(1, j.shape[2], 2 * j.shape[3])

    g_out = {
        "mla_w_in": g_big["mla_w_in"], "mla_g_q": g_gq, "mla_w_uq": g_big["mla_w_uq"], "mla_g_kv": g_gkv,
        "mla_w_uk": g_big["mla_w_uk"], "mla_w_uv": g_big["mla_w_uv"], "mla_w_o": g_big["mla_w_o"],
        "fox_w_in": g_big["fox_w_in"], "fox_b_f": g_bf, "fox_w_o": g_big["fox_w_o"],
        "ada_w": g_ada_w, "ada_b": g_ada_b, "ffn_w_gate": g_big["ffn_w_gate"], "ffn_w_up": g_big["ffn_w_up"],
        "ffn_w_down": g_big["ffn_w_down"], "ln_g": g_ln_g, "ln_b": g_ln_b}
    names = ["mla_w_in", "mla_g_q", "mla_w_uq", "mla_g_kv", "mla_w_uk", "mla_w_uv", "mla_w_o", "fox_w_in", "fox_b_f",
             "fox_w_o", "ada_w", "ada_b", "ffn_w_gate", "ffn_w_up", "ffn_w_down", "ln_g", "ln_b"]
    small_names = ["mla_g_q", "mla_g_kv", "fox_b_f", "ada_b", "ln_g", "ln_b"]
    deltas, new_m, new_v = {}, {}, {}
    for n in names:
        if n in small_names:
            continue
        shp = args[n].shape
        if n in _TRANSPOSED:
            view = lambda a: jnp.swapaxes(a, 1, 2).reshape(-1, shp[1])
            back = lambda a: jnp.swapaxes(a.reshape(shp[0], shp[2], shp[1]), 1, 2)
        else:
            view = lambda a: a.reshape(-1, shp[-1])
            back = lambda a: a.reshape(shp)
        dl, mn, vn = adamw(view(args[n]), g_out[n].reshape(view(args[n]).shape), view(args["m_" + n]),
                           view(args["v_" + n]), "adamw_" + n)
        g_out[n], deltas[n], new_m[n], new_v[n] = back(g_out[n].reshape(view(args[n]).shape)), back(dl), back(mn), back(vn)

    def small_pack(prefix, src):
        flat = jnp.concatenate([src[prefix + n].reshape(-1) for n in small_names])
        size = -(-flat.shape[0] // (8 * 128)) * 8 * 128
        return jnp.pad(flat, (0, size - flat.shape[0])).reshape(-1, 128)

    sd, sm, sv = adamw(small_pack("", args), small_pack("", g_out), small_pack("m_", args), small_pack("v_", args),
                       "adamw_small")
    off = 0
    for n in small_names:
        shp = args[n].shape
        size = math.prod(shp)
        deltas[n] = sd.reshape(-1)[off:off + size].reshape(shp)
        new_m[n] = sm.reshape(-1)[off:off + size].reshape(shp)
        new_v[n] = sv.reshape(-1)[off:off + size].reshape(shp)
        off += size

    outs = [loss, grad_x]
    outs += [g_out[n].reshape(args[n].shape) for n in names]
    outs += [deltas[n] for n in names] + [new_m[n] for n in names] + [new_v[n] for n in names]
    return tuple(outs)
```

```python
import functools
import math

import numpy as np
import jax
import jax.numpy as jnp
from jax import lax
from jax.experimental import pallas as pl
from jax.experimental.pallas import tpu as pltpu

F32 = jnp.float32
BF16 = jnp.bfloat16
MESH = pl.DeviceIdType.MESH

D_MODEL = 1024
DEPTH = 2
MLA_HEADS = 8
MLA_NOPE = 128
MLA_ROPE = 64
MLA_V = 128
MLA_QR = 256
MLA_KVR = 256
ROPE_THETA = 10000.0
FOX_HEADS = 16
FOX_HD = 64
D_FF = 2816
N_CHIPS = 4
FF_CHUNK = D_FF // N_CHIPS
ALPHA = (2.0 * DEPTH) ** 0.25
EPS = 1e-5
ADAM_LR = 0.001
ADAM_B1 = 0.9
ADAM_B2 = 0.999
ADAM_EPS = 1e-08
ADAM_WD = 0.01
ADAM_STEP = 10

VMEM_LIMIT = 56 * 1024 * 1024
TOKEN_TILE = 512
WGRAD_TOKENS = 2048
ATTN_TILE = 512
COMM_BLOCK_BYTES = 2 * 1024 * 1024
ADAMW_BLOCK_BYTES = 1024 * 1024


def _cp(n_axes):
    return pltpu.CompilerParams(dimension_semantics=("arbitrary",) * n_axes, vmem_limit_bytes=VMEM_LIMIT)


def _dot(a, b):
    return jnp.dot(a, b, preferred_element_type=F32)


def _dot_nt(a, b):
    return lax.dot_general(a, b, (((1,), (1,)), ((), ())), preferred_element_type=F32)


def _dot_tn(a, b):
    return lax.dot_general(a, b, (((0,), (0,)), ((), ())), preferred_element_type=F32)


def _dot_f32(a, b):
    return jnp.dot(a, b, preferred_element_type=F32, precision=lax.Precision.HIGHEST)


def _sds(shape, dtype):
    return jax.ShapeDtypeStruct(shape, dtype)


def _place():
    return lax.axis_index("x"), lax.axis_index("y"), lax.axis_index("c")


class _Hosted:
    def __init__(self, inputs, out_shape, sems, start, finish):
        self.inputs, self.out_shape, self.sems, self.start, self.finish = inputs, out_shape, sems, start, finish


def _call(body, name, grid, in_specs, out_specs, out_shape, args, scratch_shapes=(), hosted=None):
    in_specs, out_specs, out_shape, scratch_shapes = list(in_specs), list(out_specs), list(out_shape), list(scratch_shapes)
    if hosted is None:
        return pl.pallas_call(body, name=name, grid=grid, in_specs=in_specs, out_specs=out_specs, out_shape=out_shape,
                              scratch_shapes=scratch_shapes, compiler_params=_cp(len(grid)))(*args)
    n_in, n_out, n_scr = len(in_specs), len(out_specs), len(scratch_shapes)
    h_in, h_out = len(hosted.inputs), len(hosted.out_shape)

    def carried(*refs):
        o0 = n_in + h_in
        s0 = o0 + n_out + h_out
        c_in, c_out, c_sem = refs[n_in:o0], refs[o0 + n_out:s0], refs[s0 + n_scr:]
        ids = [pl.program_id(a) for a in range(len(grid))]
        first = functools.reduce(jnp.logical_and, [i == 0 for i in ids])
        last = functools.reduce(jnp.logical_and, [i == g - 1 for i, g in zip(ids, grid)])

        @pl.when(first)
        def _():
            hosted.start(c_in, c_out, c_sem)

        body(*refs[:n_in], *refs[o0:o0 + n_out], *refs[s0:s0 + n_scr])

        @pl.when(last)
        def _():
            hosted.finish(c_in, c_out, c_sem)

    hbm = pl.BlockSpec(memory_space=pl.ANY)
    res = pl.pallas_call(
        carried, name=name, grid=grid, in_specs=in_specs + [hbm] * h_in, out_specs=out_specs + [hbm] * h_out,
        out_shape=out_shape + list(hosted.out_shape), scratch_shapes=scratch_shapes + list(hosted.sems),
        compiler_params=_cp(len(grid)))(*args, *hosted.inputs)
    return res[:n_out], res[n_out:]


def mod_linear(x, shift, scale, w, out_dtype, name, tn=None, emit_u=False, w_rows=None):
    t, d = x.shape
    n = w.shape[1] if w_rows is None else w_rows
    tn = n if tn is None else tn
    tm = TOKEN_TILE
    tps = (t // shift.shape[0]) // tm

    def body(x_ref, sh_ref, sc_ref, w_ref, o_ref, *rest):
        u = (x_ref[...] * (1.0 + sc_ref[...]) + sh_ref[...]).astype(BF16)
        o_ref[...] = (_dot(u, w_ref[...]) if w_rows is None else _dot_nt(u, w_ref[...])).astype(out_dtype)
        if emit_u:
            @pl.when(pl.program_id(1) == 0)
            def _():
                rest[0][...] = u

    vec = pl.BlockSpec((None, 1, d), lambda i, j: (i // tps, 0, 0))
    out_shape = [_sds((t, n), out_dtype)]
    out_specs = [pl.BlockSpec((tm, tn), lambda i, j: (i, j))]
    if emit_u:
        out_shape.append(_sds((t, d), BF16))
        out_specs.append(pl.BlockSpec((tm, d), lambda i, j: (i, 0)))
    w_spec = pl.BlockSpec((d, tn), lambda i, j: (0, j)) if w_rows is None else pl.BlockSpec((tn, d), lambda i, j: (j, 0))
    res = pl.pallas_call(
        body, name=name, grid=(t // tm, n // tn),
        in_specs=[pl.BlockSpec((tm, d), lambda i, j: (i, 0)), vec, vec, w_spec],
        out_specs=out_specs, out_shape=out_shape, compiler_params=_cp(2),
    )(x, shift, scale, w)
    return res if emit_u else res[0]


def _rms(h, g):
    rstd = lax.rsqrt(jnp.mean(h * h, axis=-1, keepdims=True) + EPS)
    return h * rstd, rstd


def mla_mid_fwd(h, g_q, g_kv, w_uq, w_uk, w_uv, cos8, sin8, cos64, sin64s, swap64, rope_to_heads, dup64, name):
    t = h.shape[0]
    tm = TOKEN_TILE
    hq = MLA_HEADS * MLA_NOPE
    hr = MLA_HEADS * MLA_ROPE // 2

    def body(h_ref, gq_ref, gkv_ref, wuq_ref, wuk_ref, wuv_ref, c8_ref, s8_ref, c64_ref, s64_ref, sw_ref, p_ref, d_ref,
             q_ref, kn_ref, v_ref, kr_ref, cq_ref, ckv_ref):
        hh = h_ref[...]
        cq = (_rms(hh[:, :MLA_QR], None)[0] * gq_ref[...]).astype(BF16)
        ckv = (_rms(hh[:, MLA_QR:MLA_QR + MLA_KVR], None)[0] * gkv_ref[...]).astype(BF16)
        cq_ref[...] = cq
        ckv_ref[...] = ckv
        q = _dot(cq, wuq_ref[...])
        x1 = q[:, hq:hq + hr]
        x2 = q[:, hq + hr:]
        cs = c8_ref[...]
        sn = s8_ref[...]
        rot = jnp.concatenate([x1 * cs - x2 * sn, x2 * cs + x1 * sn], axis=1).astype(BF16)
        q_ref[...] = jnp.concatenate([q[:, :hq].astype(BF16), _dot(rot, p_ref[...]).astype(BF16)], axis=1)
        kn_ref[...] = _dot(ckv, wuk_ref[...]).astype(BF16)
        v_ref[...] = _dot(ckv, wuv_ref[...]).astype(BF16)
        kr = hh[:, MLA_QR + MLA_KVR:]
        kr = (kr * c64_ref[...] + _dot_f32(kr, sw_ref[...]) * s64_ref[...]).astype(BF16)
        kr_ref[...] = _dot(kr, d_ref[...]).astype(BF16)

    def rows(n):
        return pl.BlockSpec((tm, n), lambda i: (i, 0))

    def whole(a):
        return pl.BlockSpec(a.shape, lambda i: (0,) * a.ndim)

    nq = w_uq.shape[1]
    return pl.pallas_call(
        body, name=name, grid=(t // tm,),
        in_specs=[rows(h.shape[1]), whole(g_q), whole(g_kv), whole(w_uq), whole(w_uk), whole(w_uv),
                  rows(hr), rows(hr), rows(MLA_ROPE), rows(MLA_ROPE), whole(swap64), whole(rope_to_heads), whole(dup64)],
        out_specs=[rows(nq), rows(hq), rows(hq), rows(2 * MLA_ROPE), rows(MLA_QR), rows(MLA_KVR)],
        out_shape=[_sds((t, nq), BF16), _sds((t, hq), BF16), _sds((t, hq), BF16), _sds((t, 2 * MLA_ROPE), BF16),
                   _sds((t, MLA_QR), BF16), _sds((t, MLA_KVR), BF16)],
        compiler_params=_cp(1),
    )(h, g_q, g_kv, w_uq, w_uk, w_uv, cos8, sin8, cos64, sin64s, swap64, rope_to_heads, dup64)


def _pick_lane(tile, idx):
    lane = lax.broadcasted_iota(jnp.int32, tile.shape, 1)
    return jnp.sum(jnp.where(lane == idx, tile, 0.0), axis=1, keepdims=True)


def _pick_row(tile, idx):
    row = lax.broadcasted_iota(jnp.int32, tile.shape, 0)
    return jnp.sum(jnp.where(row == idx, tile, 0.0), axis=0, keepdims=True)


def _put_lane(tile, idx, col):
    lane = lax.broadcasted_iota(jnp.int32, tile.shape, 1)
    return jnp.where(lane == idx, col, tile)


def _put_row(tile, idx, row):
    r = lax.broadcasted_iota(jnp.int32, tile.shape, 0)
    return tile + jnp.where(r == idx, row, 0.0)


def _causal_softmax_blocks(i, tq, heads):
    def block(j, carry, masked):
        new = []
        for (score_fn, pv_fn, _), (m, l, acc) in zip(heads, carry):
            sc = score_fn(j)
            if masked:
                keep = lax.broadcasted_iota(jnp.int32, sc.shape, 0) >= lax.broadcasted_iota(jnp.int32, sc.shape, 1)
                sc = jnp.where(keep, sc, -1e30)
            m_new = jnp.maximum(m, jnp.max(sc, axis=1, keepdims=True))
            a = jnp.exp(m - m_new)
            p = jnp.exp(sc - m_new)
            new.append((m_new, a * l + jnp.sum(p, axis=1, keepdims=True), a * acc + pv_fn(j, p.astype(BF16))))
        return tuple(new)

    init = tuple((jnp.full((tq, 1), -1e30, F32), jnp.zeros((tq, 1), F32), jnp.zeros((tq, dv), F32)) for _, _, dv in heads)
    carry = lax.fori_loop(0, i, lambda j, c: block(j, c, False), init)
    return [(acc / l, m + jnp.log(l)) for m, l, acc in block(i, carry, True)]


def fox_attn_fwd(qkv, cum, cum_rows, nb, name, hosted=None):
    t = qkv.shape[0]
    s = t // nb
    tq = ATTN_TILE
    nq = s // tq
    npairs = FOX_HEADS // 2
    scale = FOX_HD ** -0.5

    def body(q_ref, k_ref, v_ref, cum_ref, cr_ref, o_ref, lse_ref):
        i = pl.program_id(1)
        hp = pl.program_id(2)

        @pl.when(hp == 0)
        def _():
            lse_ref[...] = jnp.zeros_like(lse_ref)

        q = q_ref[...]
        low = lax.broadcasted_iota(jnp.int32, q.shape, 1) < FOX_HD
        cum_t = cum_ref[...]

        def rows_of(j):
            return pl.ds(pl.multiple_of(j * tq, tq), tq)

        def head(a):
            hd = 2 * hp + a
            qa = jnp.where(low if a == 0 else jnp.logical_not(low), q, jnp.zeros_like(q))
            fq = _pick_lane(cum_t, hd)
            return (lambda j: _dot_nt(qa, k_ref[rows_of(j), :]) * scale + fq - _pick_row(cr_ref[j], hd),
                    lambda j, p: _dot(p, v_ref[rows_of(j), :]), 2 * FOX_HD)

        (o_0, lse_0), (o_1, lse_1) = _causal_softmax_blocks(i, tq, [head(0), head(1)])
        o_ref[...] = jnp.where(low, o_0, o_1).astype(BF16)
        lse_ref[...] = _put_lane(_put_lane(lse_ref[...], 2 * hp, lse_0), 2 * hp + 1, lse_1)

    return _call(
        body, name, (nb, nq, npairs),
        [pl.BlockSpec((tq, 128), lambda b, i, hp: (b * nq + i, hp)),
         pl.BlockSpec((s, 128), lambda b, i, hp: (b, npairs + hp)),
         pl.BlockSpec((s, 128), lambda b, i, hp: (b, 2 * npairs + hp)),
         pl.BlockSpec((tq, 128), lambda b, i, hp: (b * nq + i, 0)),
         pl.BlockSpec((nq, 16, tq), lambda b, i, hp: (b, 0, 0))],
        [pl.BlockSpec((tq, 128), lambda b, i, hp: (b * nq + i, hp)),
         pl.BlockSpec((tq, 128), lambda b, i, hp: (b * nq + i, 0))],
        [_sds((t, D_MODEL), BF16), _sds((t, 128), F32)], (qkv, qkv, qkv, cum, cum_rows), hosted=hosted)


def mla_attn_fwd(q, kn, kr2, v, nb, name, hosted=None):
    t = q.shape[0]
    s = t // nb
    tq = ATTN_TILE
    nq = s // tq
    npairs = MLA_HEADS // 2
    scale = (MLA_NOPE + MLA_ROPE) ** -0.5

    def body(qn_ref, qr_ref, kn_ref, kr_ref, v_ref, o_ref, lse_ref):
        i = pl.program_id(1)
        hp = pl.program_id(2)

        @pl.when(hp == 0)
        def _():
            lse_ref[...] = jnp.zeros_like(lse_ref)

        qr = qr_ref[...]
        low = lax.broadcasted_iota(jnp.int32, qr.shape, 1) < MLA_ROPE

        def rows_of(j):
            return pl.ds(pl.multiple_of(j * tq, tq), tq)

        def head(a):
            cols = slice(a * MLA_NOPE, (a + 1) * MLA_NOPE)
            q_cat = jnp.concatenate([qn_ref[:, cols], jnp.where(low if a == 0 else jnp.logical_not(low), qr,
                                                                jnp.zeros_like(qr))], axis=1)
            return (lambda j: _dot_nt(q_cat, jnp.concatenate([kn_ref[rows_of(j), cols], kr_ref[rows_of(j), :]], axis=1)) * scale,
                    lambda j, p: _dot(p, v_ref[rows_of(j), cols]), MLA_V)

        (o_0, lse_0), (o_1, lse_1) = _causal_softmax_blocks(i, tq, [head(0), head(1)])
        o_ref[...] = jnp.concatenate([o_0, o_1], axis=1).astype(BF16)
        lse_ref[...] = _put_lane(_put_lane(lse_ref[...], 2 * hp, lse_0), 2 * hp + 1, lse_1)

    wide = 2 * MLA_NOPE
    return _call(
        body, name, (nb, nq, npairs),
        [pl.BlockSpec((tq, wide), lambda b, i, hp: (b * nq + i, hp)),
         pl.BlockSpec((tq, 128), lambda b, i, hp: (b * nq + i, MLA_HEADS + hp)),
         pl.BlockSpec((s, wide), lambda b, i, hp: (b, hp)),
         pl.BlockSpec((s, 128), lambda b, i, hp: (b, 0)),
         pl.BlockSpec((s, wide), lambda b, i, hp: (b, hp))],
        [pl.BlockSpec((tq, wide), lambda b, i, hp: (b * nq + i, hp)),
         pl.BlockSpec((tq, 128), lambda b, i, hp: (b * nq + i, 0))],
        [_sds((t, MLA_HEADS * MLA_V), BF16), _sds((t, 128), F32)], (q, q, kn, kr2, v), hosted=hosted)


def rows16(a, name):
    t = a.shape[0]
    tq = ATTN_TILE

    def body(a_ref, o_ref):
        o_ref[...] = a_ref[...].T[:16, :]

    return pl.pallas_call(
        body, name=name, grid=(t // tq,), in_specs=[pl.BlockSpec((tq, 128), lambda n: (n, 0))],
        out_specs=pl.BlockSpec((None, 16, tq), lambda n: (n, 0, 0)), out_shape=_sds((t // tq, 16, tq), F32),
        compiler_params=_cp(1),
    )(a)


def tokens128(rows, onehot, name):
    nblk, _, tq = rows.shape

    def body(r_ref, e_ref, o_ref):
        o_ref[...] = lax.dot_general(r_ref[...], e_ref[...], (((0,), (0,)), ((), ())), preferred_element_type=F32,
                                     precision=lax.Precision.HIGHEST)

    return pl.pallas_call(
        body, name=name, grid=(nblk,),
        in_specs=[pl.BlockSpec((None, 16, tq), lambda n: (n, 0, 0)), pl.BlockSpec((16, 128), lambda n: (0, 0))],
        out_specs=pl.BlockSpec((tq, 128), lambda n: (n, 0)), out_shape=_sds((nblk * tq, 128), F32),
        compiler_params=_cp(1),
    )(rows, onehot)


def _layer_norm(z, g, b):
    mu = jnp.mean(z, axis=-1, keepdims=True)
    zc = z - mu
    rstd = lax.rsqrt(jnp.mean(zc * zc, axis=-1, keepdims=True) + EPS)
    xhat = zc * rstd
    return xhat * g + b, xhat, rstd


def linear_resid_ln(a, w, x_in, gate, ln_g, ln_b, name):
    t, kdim = a.shape
    d = w.shape[1]
    tm = TOKEN_TILE
    tps = (t // gate.shape[0]) // tm

    def body(a_ref, w_ref, x_ref, gt_ref, g_ref, b_ref, y_ref, xo_ref):
        y = _dot(a_ref[...], w_ref[...])
        y_ref[...] = y
        z = ALPHA * x_ref[...] + (1.0 + gt_ref[...]) * y
        xo_ref[...] = _layer_norm(z, g_ref[...], b_ref[...])[0]

    rows = pl.BlockSpec((tm, d), lambda i: (i, 0))
    vec = pl.BlockSpec((1, d), lambda i: (0, 0))
    return pl.pallas_call(
        body, name=name, grid=(t // tm,),
        in_specs=[pl.BlockSpec((tm, kdim), lambda i: (i, 0)), pl.BlockSpec((kdim, d), lambda i: (0, 0)), rows,
                  pl.BlockSpec((None, 1, d), lambda i: (i // tps, 0, 0)), vec, vec],
        out_specs=[rows, rows], out_shape=[_sds((t, d), F32), _sds((t, d), F32)],
        compiler_params=_cp(1),
    )(a, w, x_in, gate, ln_g, ln_b)


def ffn_fwd(x_in, shift, scale, gate, wg, wu, wd, ln_g, ln_b, name, hosted=None):
    t, d = x_in.shape
    c, _, fc = wg.shape
    tm = TOKEN_TILE
    tps = (t // gate.shape[0]) // tm

    def body(x_ref, sh_ref, sc_ref, gt_ref, wg_ref, wu_ref, wd_ref, g_ref, b_ref,
             u_ref, hg_ref, hu_ref, y_ref, xo_ref, acc_ref):
        cc = pl.program_id(1)

        @pl.when(cc == 0)
        def _():
            u_ref[...] = (x_ref[...] * (1.0 + sc_ref[...]) + sh_ref[...]).astype(BF16)
            acc_ref[...] = jnp.zeros_like(acc_ref)

        u = u_ref[...]
        hg = _dot(u, wg_ref[...])
        hu = _dot(u, wu_ref[...])
        hg_ref[...] = hg.astype(BF16)
        hu_ref[...] = hu.astype(BF16)
        act = (hg * jax.nn.sigmoid(hg) * hu).astype(BF16)
        acc_ref[...] += _dot(act, wd_ref[...])

        @pl.when(cc == c - 1)
        def _():
            y = acc_ref[...]
            y_ref[...] = y
            z = ALPHA * x_ref[...] + (1.0 + gt_ref[...]) * y
            xo_ref[...] = _layer_norm(z, g_ref[...], b_ref[...])[0]

    rows = pl.BlockSpec((tm, d), lambda i, cc: (i, 0))
    bvec = pl.BlockSpec((None, 1, d), lambda i, cc: (i // tps, 0, 0))
    vec = pl.BlockSpec((1, d), lambda i, cc: (0, 0))
    hspec = pl.BlockSpec((None, tm, fc), lambda i, cc: (cc, i, 0))
    wcol = pl.BlockSpec((None, d, fc), lambda i, cc: (cc, 0, 0))
    return _call(
        body, name, (t // tm, c),
        [rows, bvec, bvec, bvec, wcol, wcol, pl.BlockSpec((None, fc, d), lambda i, cc: (cc, 0, 0)), vec, vec],
        [rows, hspec, hspec, rows, rows],
        [_sds((t, d), BF16), _sds((c, t, fc), BF16), _sds((c, t, fc), BF16), _sds((t, d), F32), _sds((t, d), F32)],
        (x_in, shift, scale, gate, wg, wu, wd, ln_g, ln_b), scratch_shapes=[pltpu.VMEM((tm, d), F32)], hosted=hosted)


def fox_gate_fwd(hf, b_f, tri, n_batch, name):
    t, n = hf.shape
    blk = tri.shape[0]
    nb = (t // n_batch) // blk

    def body(hf_ref, b_ref, tri_ref, o_ref, carry_ref):
        @pl.when(pl.program_id(1) == 0)
        def _():
            carry_ref[...] = jnp.zeros_like(carry_ref)

        xx = hf_ref[...] + b_ref[...]
        lf = jnp.minimum(xx, 0.0) - jnp.log(1.0 + jnp.exp(-jnp.abs(xx)))
        cum = _dot_f32(tri_ref[...], lf) + carry_ref[...]
        o_ref[...] = cum
        carry_ref[...] = cum[blk - 1:blk, :]

    return pl.pallas_call(
        body, name=name, grid=(n_batch, nb),
        in_specs=[pl.BlockSpec((blk, n), lambda bb, i: (bb * nb + i, 0)), pl.BlockSpec((1, n), lambda bb, i: (0, 0)),
                  pl.BlockSpec((blk, blk), lambda bb, i: (0, 0))],
        out_specs=pl.BlockSpec((blk, n), lambda bb, i: (bb * nb + i, 0)),
        out_shape=_sds((t, n), F32), scratch_shapes=[pltpu.VMEM((1, n), F32)],
        compiler_params=_cp(2),
    )(hf, b_f, tri)


def loss_grad(x_out, target, name):
    t, d = x_out.shape
    tm = TOKEN_TILE

    def body(x_ref, t_ref, g_ref, l_ref):
        @pl.when(pl.program_id(0) == 0)
        def _():
            l_ref[...] = jnp.zeros_like(l_ref)

        err = x_ref[...] - t_ref[...]
        g_ref[...] = err / d
        l_ref[...] += jnp.sum(err * err, axis=0, keepdims=True)

    rows = pl.BlockSpec((tm, d), lambda i: (i, 0))
    return pl.pallas_call(
        body, name=name, grid=(t // tm,), in_specs=[rows, rows],
        out_specs=[rows, pl.BlockSpec((1, d), lambda i: (0, 0))],
        out_shape=[_sds((t, d), F32), _sds((1, d), F32)], compiler_params=_cp(1),
    )(x_out, target)


def ln_bwd(dxo, x_in, y, gate, ln_g, name):
    t, d = dxo.shape
    nb = gate.shape[0]
    tm = TOKEN_TILE
    tps = (t // nb) // tm

    def body(dxo_ref, x_ref, y_ref, gt_ref, g_ref, dz_ref, dy_ref, dg_ref, db_ref, dgt_ref):
        i = pl.program_id(0)

        @pl.when(i == 0)
        def _():
            dg_ref[...] = jnp.zeros_like(dg_ref)
            db_ref[...] = jnp.zeros_like(db_ref)

        @pl.when(i % tps == 0)
        def _():
            dgt_ref[...] = jnp.zeros_like(dgt_ref)

        yy = y_ref[...]
        g1 = 1.0 + gt_ref[...]
        z = ALPHA * x_ref[...] + g1 * yy
        _, xhat, rstd = _layer_norm(z, 1.0, 0.0)
        dxo_v = dxo_ref[...]
        dg_ref[...] += jnp.sum(dxo_v * xhat, axis=0, keepdims=True)
        db_ref[...] += jnp.sum(dxo_v, axis=0, keepdims=True)
        dxh = dxo_v * g_ref[...]
        dz = rstd * (dxh - jnp.mean(dxh, axis=-1, keepdims=True) - xhat * jnp.mean(dxh * xhat, axis=-1, keepdims=True))
        dz_ref[...] = dz
        dy_ref[...] = (g1 * dz).astype(BF16)
        dgt_ref[...] += jnp.sum(dz * yy, axis=0, keepdims=True)

    rows = pl.BlockSpec((tm, d), lambda i: (i, 0))
    vec = pl.BlockSpec((1, d), lambda i: (0, 0))
    bvec = pl.BlockSpec((None, 1, d), lambda i: (i // tps, 0, 0))
    return pl.pallas_call(
        body, name=name, grid=(t // tm,), in_specs=[rows, rows, rows, bvec, vec],
        out_specs=[rows, rows, vec, vec, bvec],
        out_shape=[_sds((t, d), F32), _sds((t, d), BF16), _sds((1, d), F32), _sds((1, d), F32), _sds((nb, 1, d), F32)],
        compiler_params=_cp(1),
    )(dxo, x_in, y, gate, ln_g)


def _mod_bwd_tail(du, dz_ref, x_ref, sc_ref, dx_ref, dsc_ref, dsh_ref, first):
    @pl.when(first)
    def _():
        dsc_ref[...] = jnp.zeros_like(dsc_ref)
        dsh_ref[...] = jnp.zeros_like(dsh_ref)

    dx_ref[...] = ALPHA * dz_ref[...] + du * (1.0 + sc_ref[...])
    dsc_ref[...] += jnp.sum(du * x_ref[...], axis=0, keepdims=True)
    dsh_ref[...] += jnp.sum(du, axis=0, keepdims=True)


def ffn_bwd(dy, hg, hu, wg, wu, wd, dz, x_in, scale, name, hosted=None):
    t, d = dy.shape
    c, _, fc = wg.shape
    nb = scale.shape[0]
    tm = TOKEN_TILE
    tps = (t // nb) // tm

    def act_body(dy_ref, hg_ref, hu_ref, wd_ref, dhg_ref, dhu_ref, act_ref):
        hgv = hg_ref[...].astype(F32)
        huv = hu_ref[...].astype(F32)
        da = _dot_nt(dy_ref[...], wd_ref[...])
        sg = jax.nn.sigmoid(hgv)
        sl = hgv * sg
        act_ref[...] = (sl * huv).astype(BF16)
        dhu_ref[...] = (da * sl).astype(BF16)
        dhg_ref[...] = (da * huv * (sg * (1.0 + hgv * (1.0 - sg)))).astype(BF16)

    def in_body(dhg_ref, dhu_ref, wg_ref, wu_ref, dz_ref, x_ref, sc_ref, dx_ref, dsc_ref, dsh_ref, acc_ref):
        i = pl.program_id(0)
        cc = pl.program_id(1)

        @pl.when(cc == 0)
        def _():
            acc_ref[...] = jnp.zeros_like(acc_ref)

        acc_ref[...] += _dot_nt(dhg_ref[...], wg_ref[...]) + _dot_nt(dhu_ref[...], wu_ref[...])

        @pl.when(cc == c - 1)
        def _():
            _mod_bwd_tail(acc_ref[...], dz_ref, x_ref, sc_ref, dx_ref, dsc_ref, dsh_ref, i % tps == 0)

    rows = pl.BlockSpec((tm, d), lambda i, cc: (i, 0))
    bvec = pl.BlockSpec((None, 1, d), lambda i, cc: (i // tps, 0, 0))
    hspec = pl.BlockSpec((None, tm, fc), lambda i, cc: (cc, i, 0))
    wcol = pl.BlockSpec((None, d, fc), lambda i, cc: (cc, 0, 0))
    dhg, dhu, act = _call(
        act_body, name + "_act", (t // tm, c),
        [rows, hspec, hspec, pl.BlockSpec((None, fc, d), lambda i, cc: (cc, 0, 0))], [hspec, hspec, hspec],
        [_sds((c, t, fc), BF16)] * 3, (dy, hg, hu, wd))
    res = _call(
        in_body, name, (t // tm, c), [hspec, hspec, wcol, wcol, rows, rows, bvec], [rows, bvec, bvec],
        [_sds((t, d), F32), _sds((nb, 1, d), F32), _sds((nb, 1, d), F32)], (dhg, dhu, wg, wu, dz, x_in, scale),
        scratch_shapes=[pltpu.VMEM((tm, d), F32)], hosted=hosted)
    if hosted is None:
        return [dhg, dhu, act] + list(res)
    return [dhg, dhu, act] + list(res[0]), res[1]


def linear_nt_mod_bwd(pairs, dz, x_in, scale, name, hosted=None):
    t, d = dz.shape
    nb = scale.shape[0]
    tm = TOKEN_TILE
    tps = (t // nb) // tm
    npairs = len(pairs)

    def body(*refs):
        dh_refs = refs[:npairs]
        w_refs = refs[npairs:2 * npairs]
        dz_ref, x_ref, sc_ref, dx_ref, dsc_ref, dsh_ref = refs[2 * npairs:]
        du = None
        for (_, _, blk), dh_ref, w_ref in zip(pairs, dh_refs, w_refs):
            dh = dh_ref[...].astype(BF16)
            term = _dot_nt(dh, w_ref[...]) if blk is None else _dot(dh, w_ref[...])
            du = term if du is None else du + term
        _mod_bwd_tail(du, dz_ref, x_ref, sc_ref, dx_ref, dsc_ref, dsh_ref, pl.program_id(0) % tps == 0)

    rows = pl.BlockSpec((tm, d), lambda i: (i, 0))
    bvec = pl.BlockSpec((None, 1, d), lambda i: (i // tps, 0, 0))
    in_specs = [pl.BlockSpec((tm, dh.shape[1]), lambda i: (i, 0)) for dh, _, _ in pairs]
    for dh, w, blk in pairs:
        if blk is None:
            in_specs.append(pl.BlockSpec(w.shape, lambda i: (0, 0)))
        else:
            in_specs.append(pl.BlockSpec((dh.shape[1], d), lambda i, blk=blk: (blk, 0)))
    in_specs += [rows, rows, bvec]
    return _call(
        body, name, (t // tm,), in_specs, [rows, bvec, bvec],
        [_sds((t, d), F32), _sds((nb, 1, d), F32), _sds((nb, 1, d), F32)],
        (*[dh for dh, _, _ in pairs], *[w for _, w, _ in pairs], dz, x_in, scale), hosted=hosted)


def linear_nt_delta(dy, w_o, o, head_sel, name):
    t, d = dy.shape
    hdv = w_o.shape[0]
    tm = TOKEN_TILE

    def body(dy_ref, w_ref, o_ref, sel_ref, do_ref, dl_ref):
        do = _dot_nt(dy_ref[...], w_ref[...])
        do_ref[...] = do.astype(BF16)
        dl_ref[...] = _dot_f32(do * o_ref[...].astype(F32), sel_ref[...])

    return pl.pallas_call(
        body, name=name, grid=(t // tm,),
        in_specs=[pl.BlockSpec((tm, d), lambda i: (i, 0)), pl.BlockSpec((hdv, d), lambda i: (0, 0)),
                  pl.BlockSpec((tm, hdv), lambda i: (i, 0)), pl.BlockSpec(head_sel.shape, lambda i: (0, 0))],
        out_specs=[pl.BlockSpec((tm, hdv), lambda i: (i, 0)), pl.BlockSpec((tm, 128), lambda i: (i, 0))],
        out_shape=[_sds((t, hdv), BF16), _sds((t, 128), F32)], compiler_params=_cp(1),
    )(dy, w_o, o, head_sel)


def _attn_bwd_blocks(j, nk, tk, scale, heads):
    def block(i, carry, masked):
        new = []
        for hd, (dk_acc, dv_acc, dfk_acc) in zip(heads, carry):
            qb = hd["q"](i)
            dob = hd["do"](i)
            lse_row, dl_row = hd["rows"](i)
            st = _dot_nt(hd["k"], qb) * scale
            if hd["bias"] is not None:
                fq_row, fk_col = hd["bias"](i)
                st = st + fq_row - fk_col
            if masked:
                keep = lax.broadcasted_iota(jnp.int32, st.shape, 1) >= lax.broadcasted_iota(jnp.int32, st.shape, 0)
                st = jnp.where(keep, st, -1e30)
            pt = jnp.exp(st - lse_row)
            dv_acc = dv_acc + _dot(pt.astype(BF16), dob)
            dst = pt * (_dot_nt(hd["v"], dob) - dl_row)
            if hd["add_dfq"] is not None:
                dfk_acc = dfk_acc - jnp.sum(dst, axis=1, keepdims=True)
                hd["add_dfq"](i, jnp.sum(dst, axis=0, keepdims=True))
            dsb = (dst * scale).astype(BF16)
            dk_acc = dk_acc + _dot(dsb, qb)
            hd["add_dq"](i, _dot_tn(dsb, hd["k"]))
            new.append((dk_acc, dv_acc, dfk_acc))
        return tuple(new)

    init = tuple((jnp.zeros((tk, hd["k"].shape[1]), F32), jnp.zeros((tk, hd["v"].shape[1]), F32), jnp.zeros((tk, 1), F32))
                 for hd in heads)
    carry = block(j, init, True)
    return lax.fori_loop(j + 1, nk, lambda i, c: block(i, c, False), carry)


def fox_attn_bwd(qkv, do, cum, cum_rows, lse_rows, delta_rows, nb, name, hosted=None):
    t = qkv.shape[0]
    s = t // nb
    tk = ATTN_TILE
    nk = s // tk
    npairs = FOX_HEADS // 2
    scale = FOX_HD ** -0.5

    def body(q_ref, k_ref, v_ref, do_ref, cum_ref, cr_ref, lr_ref, dr_ref, dq_ref, dk_ref, dv_ref, dfq_ref, dfk_ref):
        hp = pl.program_id(1)
        j = pl.program_id(2)

        @pl.when(j == 0)
        def _():
            dq_ref[...] = jnp.zeros_like(dq_ref)

        @pl.when((j == 0) & (hp == 0))
        def _():
            dfq_ref[...] = jnp.zeros_like(dfq_ref)
            dfk_ref[...] = jnp.zeros_like(dfk_ref)

        kb = k_ref[...]
        vb = v_ref[...]
        low = lax.broadcasted_iota(jnp.int32, kb.shape, 1) < FOX_HD
        cum_t = cum_ref[...]

        def rows_of(i):
            return pl.ds(pl.multiple_of(i * tk, tk), tk)

        def add_dq(i, val):
            dq_ref[rows_of(i), :] += val

        def head(a):
            hd = 2 * hp + a
            half = low if a == 0 else jnp.logical_not(low)
            fk = _pick_lane(cum_t, hd)

            def add_dfq(i, val):
                dfq_ref[i] = _put_row(dfq_ref[i], hd, val)

            return dict(q=lambda i: q_ref[rows_of(i), :], do=lambda i: do_ref[rows_of(i), :],
                        k=jnp.where(half, kb, jnp.zeros_like(kb)), v=jnp.where(half, vb, jnp.zeros_like(vb)),
                        rows=lambda i: (_pick_row(lr_ref[i], hd), _pick_row(dr_ref[i], hd)),
                        bias=lambda i: (_pick_row(cr_ref[i], hd), fk), add_dq=add_dq, add_dfq=add_dfq)

        (dk_0, dv_0, dfk_0), (dk_1, dv_1, dfk_1) = _attn_bwd_blocks(j, nk, tk, scale, [head(0), head(1)])
        dk_ref[...] = jnp.where(low, dk_0, dk_1).astype(BF16)
        dv_ref[...] = jnp.where(low, dv_0, dv_1).astype(BF16)
        for a, dfk_a in ((0, dfk_0), (1, dfk_1)):
            dfk_ref[j] = _put_row(dfk_ref[j], 2 * hp + a, jnp.broadcast_to(dfk_a, (tk, 128)).T[0:1, :])

    rowsp = pl.BlockSpec((nk, 16, tk), lambda b, hp, j: (b, 0, 0))
    return _call(
        body, name, (nb, npairs, nk),
        [pl.BlockSpec((s, 128), lambda b, hp, j: (b, hp)),
         pl.BlockSpec((tk, 128), lambda b, hp, j: (b * nk + j, npairs + hp)),
         pl.BlockSpec((tk, 128), lambda b, hp, j: (b * nk + j, 2 * npairs + hp)),
         pl.BlockSpec((s, 128), lambda b, hp, j: (b, hp)),
         pl.BlockSpec((tk, 128), lambda b, hp, j: (b * nk + j, 0)),
         rowsp, rowsp, rowsp],
        [pl.BlockSpec((s, 128), lambda b, hp, j: (b, hp)),
         pl.BlockSpec((tk, 128), lambda b, hp, j: (b * nk + j, hp)),
         pl.BlockSpec((tk, 128), lambda b, hp, j: (b * nk + j, hp)),
         rowsp, rowsp],
        [_sds((t, D_MODEL), F32), _sds((t, D_MODEL), BF16), _sds((t, D_MODEL), BF16),
         _sds((t // tk, 16, tk), F32), _sds((t // tk, 16, tk), F32)],
        (qkv, qkv, qkv, do, cum, cum_rows, lse_rows, delta_rows), hosted=hosted)


def mla_attn_bwd(q, kn, kr2, v, do, lse_rows, delta_rows, nb, name, hosted=None):
    t = q.shape[0]
    s = t // nb
    tk = ATTN_TILE
    nk = s // tk
    npairs = MLA_HEADS // 2
    scale = (MLA_NOPE + MLA_ROPE) ** -0.5

    def body(qn_ref, qr_ref, kn_ref, kr_ref, v_ref, do_ref, lr_ref, dr_ref, dqn_ref, dqr_ref, dkn_ref, dkr_ref, dv_ref):
        hp = pl.program_id(1)
        j = pl.program_id(2)

        @pl.when(j == 0)
        def _():
            dqn_ref[...] = jnp.zeros_like(dqn_ref)
            dqr_ref[...] = jnp.zeros_like(dqr_ref)

        low = lax.broadcasted_iota(jnp.int32, (tk, 128), 1) < MLA_ROPE
        kr = kr_ref[...]

        def rows_of(i):
            return pl.ds(pl.multiple_of(i * tk, tk), tk)

        def head(a):
            cols = slice(a * MLA_NOPE, (a + 1) * MLA_NOPE)
            mine = low if a == 0 else jnp.logical_not(low)

            def q_fn(i):
                qr = qr_ref[rows_of(i), :]
                return jnp.concatenate([qn_ref[rows_of(i), cols], jnp.where(mine, qr, jnp.zeros_like(qr))], axis=1)

            def add_dq(i, val):
                dqn_ref[rows_of(i), cols] += val[:, :MLA_NOPE]
                dqr_ref[rows_of(i), cols] += val[:, MLA_NOPE:]

            return dict(q=q_fn, do=lambda i: do_ref[rows_of(i), cols], k=jnp.concatenate([kn_ref[:, cols], kr], axis=1),
                        v=v_ref[:, cols], rows=lambda i: (_pick_row(lr_ref[i], 2 * hp + a), _pick_row(dr_ref[i], 2 * hp + a)),
                        bias=None, add_dq=add_dq, add_dfq=None)

        (dk_0, dv_0, _), (dk_1, dv_1, _) = _attn_bwd_blocks(j, nk, tk, scale, [head(0), head(1)])
        dkn_ref[...] = jnp.concatenate([dk_0[:, :MLA_NOPE], dk_1[:, :MLA_NOPE]], axis=1).astype(BF16)
        dkr_ref[...] = jnp.concatenate([dk_0[:, MLA_NOPE:], dk_1[:, MLA_NOPE:]], axis=1).astype(BF16)
        dv_ref[...] = jnp.concatenate([dv_0, dv_1], axis=1).astype(BF16)

    wide = 2 * MLA_NOPE
    full = pl.BlockSpec((s, wide), lambda b, hp, j: (b, hp))
    blk = pl.BlockSpec((tk, wide), lambda b, hp, j: (b * nk + j, hp))
    rowsp = pl.BlockSpec((nk, 16, tk), lambda b, hp, j: (b, 0, 0))
    total = MLA_HEADS * MLA_V
    return _call(
        body, name, (nb, npairs, nk),
        [full, pl.BlockSpec((s, 128), lambda b, hp, j: (b, MLA_HEADS + hp)), blk,
         pl.BlockSpec((tk, 128), lambda b, hp, j: (b * nk + j, 0)), blk, full, rowsp, rowsp],
        [full, full, blk, blk, blk],
        [_sds((t, total), F32), _sds((t, total), F32), _sds((t, total), BF16), _sds((t, total), BF16),
         _sds((t, total), BF16)],
        (q, q, kn, kr2, v, do, lse_rows, delta_rows), hosted=hosted)


def mla_mid_bwd(dqn, dqr, dkn, dv, dkr_heads, h, g_q, g_kv, w_uq, w_uk, w_uv, cos8, sin8, cos64, sin64s, swap64,
                heads_to_rope, head_sum, name):
    t = h.shape[0]
    tm = TOKEN_TILE
    hq = MLA_HEADS * MLA_NOPE
    hr = MLA_HEADS * MLA_ROPE // 2
    nq = w_uq.shape[1]

    def body(dqn_ref, dqr_ref, dkn_ref, dv_ref, dkr_ref, h_ref, gq_ref, gkv_ref, wuq_ref, wuk_ref, wuv_ref,
             c8_ref, s8_ref, c64_ref, s64_ref, sw_ref, hp_ref, hs_ref, dh_ref, dqp_ref, dgq_ref, dgkv_ref):
        @pl.when(pl.program_id(0) == 0)
        def _():
            dgq_ref[...] = jnp.zeros_like(dgq_ref)
            dgkv_ref[...] = jnp.zeros_like(dgkv_ref)

        drot = _dot(dqr_ref[...].astype(BF16), hp_ref[...])
        o1 = drot[:, :hr]
        o2 = drot[:, hr:]
        cs = c8_ref[...]
        sn = s8_ref[...]
        dqp = jnp.concatenate([dqn_ref[...].astype(BF16), (o1 * cs + o2 * sn).astype(BF16),
                               (o2 * cs - o1 * sn).astype(BF16)], axis=1)
        dqp_ref[...] = dqp
        dcq = _dot_nt(dqp, wuq_ref[...])
        dckv = _dot_nt(dkn_ref[...], wuk_ref[...]) + _dot_nt(dv_ref[...], wuv_ref[...])
        hh = h_ref[...]

        def rms_bwd(hpart, g, dc, dg_ref):
            hhat, rstd = _rms(hpart, None)
            dg_ref[...] += jnp.sum(dc * hhat, axis=0, keepdims=True)
            dcg = dc * g
            return rstd * (dcg - hhat * jnp.mean(dcg * hhat, axis=-1, keepdims=True))

        dhq = rms_bwd(hh[:, :MLA_QR], gq_ref[...], dcq, dgq_ref)
        dhkv = rms_bwd(hh[:, MLA_QR:MLA_QR + MLA_KVR], gkv_ref[...], dckv, dgkv_ref)
        dkr = _dot(dkr_ref[...], hs_ref[...])
        dkr_pre = dkr * c64_ref[...] + _dot_f32(dkr * s64_ref[...], sw_ref[...])
        dh_ref[...] = jnp.concatenate([dhq, dhkv, dkr_pre], axis=1).astype(BF16)

    def rows(n):
        return pl.BlockSpec((tm, n), lambda i: (i, 0))

    def whole(a):
        return pl.BlockSpec(a.shape, lambda i: (0,) * a.ndim)

    return pl.pallas_call(
        body, name=name, grid=(t // tm,),
        in_specs=[rows(hq), rows(hq), rows(hq), rows(hq), rows(hq), rows(h.shape[1]), whole(g_q), whole(g_kv),
                  whole(w_uq), whole(w_uk), whole(w_uv), rows(hr), rows(hr), rows(MLA_ROPE), rows(MLA_ROPE),
                  whole(swap64), whole(heads_to_rope), whole(head_sum)],
        out_specs=[rows(h.shape[1]), rows(nq), pl.BlockSpec((1, MLA_QR), lambda i: (0, 0)),
                   pl.BlockSpec((1, MLA_KVR), lambda i: (0, 0))],
        out_shape=[_sds((t, h.shape[1]), BF16), _sds((t, nq), BF16), _sds((1, MLA_QR), F32), _sds((1, MLA_KVR), F32)],
        compiler_params=_cp(1),
    )(dqn, dqr, dkn, dv, dkr_heads, h, g_q, g_kv, w_uq, w_uk, w_uv, cos8, sin8, cos64, sin64s, swap64,
      heads_to_rope, head_sum)


def fox_gate_bwd(dcum, hf, b_f, triu, n_batch, name):
    t, n = hf.shape
    blk = triu.shape[0]
    nb = (t // n_batch) // blk

    def body(dc_ref, hf_ref, b_ref, tri_ref, o_ref, db_ref, carry_ref):
        @pl.when(pl.program_id(1) == 0)
        def _():
            carry_ref[...] = jnp.zeros_like(carry_ref)

        @pl.when((pl.program_id(0) == 0) & (pl.program_id(1) == 0))
        def _():
            db_ref[...] = jnp.zeros_like(db_ref)

        rc = _dot_f32(tri_ref[...], dc_ref[...]) + carry_ref[...]
        carry_ref[...] = rc[0:1, :]
        dhf = rc * jax.nn.sigmoid(-(hf_ref[...] + b_ref[...]))
        o_ref[...] = dhf.astype(BF16)
        db_ref[...] += jnp.sum(dhf, axis=0, keepdims=True)

    rev = pl.BlockSpec((blk, n), lambda bb, i: (bb * nb + nb - 1 - i, 0))
    return pl.pallas_call(
        body, name=name, grid=(n_batch, nb),
        in_specs=[rev, rev, pl.BlockSpec((1, n), lambda bb, i: (0, 0)), pl.BlockSpec((blk, blk), lambda bb, i: (0, 0))],
        out_specs=[rev, pl.BlockSpec((1, n), lambda bb, i: (0, 0))],
        out_shape=[_sds((t, n), BF16), _sds((1, n), F32)], scratch_shapes=[pltpu.VMEM((1, n), F32)],
        compiler_params=_cp(2),
    )(dcum, hf, b_f, triu)


def wgrad(a, bm, name, with_bf16=False, bt=WGRAD_TOKENS):
    ca, t, kd = a.shape
    cb, _, nd = bm.shape
    c = max(ca, cb)
    bn = nd
    if nd > 1024 and nd % 1024 == 0:
        bn = 1024
    nsteps = t // bt

    def body(a_ref, b_ref, o_ref, *rest):
        @pl.when(pl.program_id(2) == 0)
        def _():
            o_ref[...] = jnp.zeros_like(o_ref)

        o_ref[...] += _dot_tn(a_ref[...].astype(BF16), b_ref[...].astype(BF16))
        if with_bf16:
            @pl.when(pl.program_id(2) == nsteps - 1)
            def _():
                rest[0][...] = o_ref[...].astype(BF16)

    out_spec = pl.BlockSpec((None, kd, bn), lambda cc, n, tt: (cc, 0, n))
    res = pl.pallas_call(
        body, name=name, grid=(c, nd // bn, nsteps),
        in_specs=[pl.BlockSpec((None, bt, kd), lambda cc, n, tt: (cc if ca > 1 else 0, tt, 0)),
                  pl.BlockSpec((None, bt, bn), lambda cc, n, tt: (cc if cb > 1 else 0, tt, n))],
        out_specs=[out_spec, out_spec] if with_bf16 else out_spec,
        out_shape=[_sds((c, kd, nd), F32), _sds((c, kd, nd), BF16)] if with_bf16 else _sds((c, kd, nd), F32),
        compiler_params=_cp(3),
    )(a, bm)
    return res


def ada_mod_part(c_all, ada_w, name):
    nl, d, n = ada_w.shape
    rows = c_all.shape[0]
    tn = 512

    def body(c_ref, w_ref, o_ref):
        cv = c_ref[...]
        act = (cv * jax.nn.sigmoid(cv)).astype(BF16)
        o_ref[...] = _dot(act, w_ref[...].astype(BF16))

    return pl.pallas_call(
        body, name=name, grid=(nl, n // tn),
        in_specs=[pl.BlockSpec((rows, d), lambda l, j: (0, 0)), pl.BlockSpec((None, d, tn), lambda l, j: (l, 0, j))],
        out_specs=pl.BlockSpec((None, rows, tn), lambda l, j: (l, 0, j)),
        out_shape=_sds((nl, rows, n), F32), compiler_params=_cp(2),
    )(c_all, ada_w)


def ada_grad(c_all_t, dmod, name):
    nl, rows, n = dmod.shape
    d = c_all_t.shape[0]
    tn = 512

    def body(c_ref, dm_ref, o_ref):
        cv = c_ref[...]
        act = (cv * jax.nn.sigmoid(cv)).astype(BF16)
        o_ref[...] = _dot(act, dm_ref[...].astype(BF16))

    return pl.pallas_call(
        body, name=name, grid=(nl, n // tn),
        in_specs=[pl.BlockSpec((d, rows), lambda l, j: (0, 0)), pl.BlockSpec((None, rows, tn), lambda l, j: (l, 0, j))],
        out_specs=pl.BlockSpec((None, d, tn), lambda l, j: (l, 0, j)),
        out_shape=_sds((nl, d, n), F32), compiler_params=_cp(2),
    )(c_all_t, dmod)


def sum_leading(a, name):
    g, r, n = a.shape

    def body(a_ref, o_ref):
        acc = a_ref[0]
        for kk in range(1, g):
            acc = acc + a_ref[kk]
        o_ref[...] = acc

    return pl.pallas_call(
        body, name=name, grid=(1,), in_specs=[pl.BlockSpec((g, r, n), lambda i: (0, 0, 0))],
        out_specs=pl.BlockSpec((r, n), lambda i: (0, 0)), out_shape=_sds((r, n), F32), compiler_params=_cp(1),
    )(a)


def adamw(w, g, m, v, name):
    r, n = w.shape
    br = r
    for cand in (512, 256, 128, 64, 32, 16, 8):
        if r % cand == 0 and r > cand and cand * n * 4 <= ADAMW_BLOCK_BYTES:
            br = cand
            break
    c1 = 1.0 - ADAM_B1 ** ADAM_STEP
    c2 = 1.0 - ADAM_B2 ** ADAM_STEP

    def body(w_ref, g_ref, m_ref, v_ref, d_ref, mo_ref, vo_ref):
        gv = g_ref[...]
        mn = ADAM_B1 * m_ref[...] + (1.0 - ADAM_B1) * gv
        vn = ADAM_B2 * v_ref[...] + (1.0 - ADAM_B2) * (gv * gv)
        mo_ref[...] = mn
        vo_ref[...] = vn
        d_ref[...] = -ADAM_LR * ((mn / c1) / (jnp.sqrt(vn / c2) + ADAM_EPS) + ADAM_WD * w_ref[...])

    spec = pl.BlockSpec((br, n), lambda i: (i, 0))
    return pl.pallas_call(
        body, name=name, grid=(r // br,), in_specs=[spec] * 4, out_specs=[spec] * 3,
        out_shape=[_sds((r, n), F32)] * 3, compiler_params=_cp(1),
    )(w, g, m, v)


def all_gather8(x_blk, name):
    m_per, n = x_blk.shape

    def body(x_ref, out_ref, send_sems, recv_sems, local_sem):
        x, y, c = _place()
        me, sibling = (x, y, c), (x, y, 1 - c)
        chips = [(1 - x, y), (x, 1 - y), (1 - x, 1 - y)]

        def rows(px, py, pc):
            return out_ref.at[pl.ds((4 * px + 2 * py + pc) * m_per, m_per), :]

        def copy(k, block, to, src=None):
            return pltpu.make_async_remote_copy(
                src_ref=rows(*block) if src is None else src, dst_ref=rows(*block),
                send_sem=send_sems.at[k], recv_sem=recv_sems.at[k], device_id=to, device_id_type=MESH)

        mine = pltpu.make_async_copy(x_ref, rows(*me), local_sem)
        mine.start()
        first = [copy(0, me, sibling, src=x_ref)]
        first += [copy(1 + j, me, (*chip, c), src=x_ref) for j, chip in enumerate(chips)]
        for cp in first:
            cp.start()
        passed = [copy(4 + j, (*chip, c), sibling) for j, chip in enumerate(chips)]
        for j, chip in enumerate(chips):
            copy(1 + j, (*chip, c), me).wait_recv()
            passed[j].start()
        copy(0, sibling, me).wait_recv()
        for j, chip in enumerate(chips):
            copy(4 + j, (*chip, 1 - c), me).wait_recv()
        for cp in first + passed:
            cp.wait_send()
        mine.wait()

    return pl.pallas_call(
        body, name=name, out_shape=_sds((8 * m_per, n), x_blk.dtype),
        in_specs=[pl.BlockSpec(memory_space=pltpu.VMEM)], out_specs=pl.BlockSpec(memory_space=pltpu.VMEM),
        scratch_shapes=[pltpu.SemaphoreType.DMA((7,)), pltpu.SemaphoreType.DMA((7,)), pltpu.SemaphoreType.DMA],
        compiler_params=pltpu.CompilerParams(vmem_limit_bytes=VMEM_LIMIT),
    )(x_blk)


def _gather_comm(shards):
    nt = len(shards)

    def parts(w_refs, out_refs, sems, finishing):
        send_sems, recv_sems, own_send, own_recv = sems
        x, y, c = _place()
        sibling = (x, y, 1 - c)
        chips = [(1 - x, y), (x, 1 - y), (1 - x, 1 - y)]

        def copy(t, k, block, to, src=None):
            px, py, hh = block
            dst = out_refs[t].at[2 * px + py, hh]
            return pltpu.make_async_remote_copy(
                src_ref=dst if src is None else src, dst_ref=dst,
                send_sem=send_sems.at[6 * t + k], recv_sem=recv_sems.at[6 * t + k], device_id=to, device_id_type=MESH)

        own = [pltpu.make_async_remote_copy(
            src_ref=w_refs[t], dst_ref=out_refs[t].at[2 * x + y], send_sem=own_send.at[t], recv_sem=own_recv.at[t],
            device_id=sibling, device_id_type=MESH) for t in range(nt)]
        first = [copy(t, j, (x, y, c), (*chip, c), src=w_refs[t].at[c]) for t in range(nt) for j, chip in enumerate(chips)]
        if not finishing:
            return own, first
        landed = [copy(t, j, (*chip, c), (x, y, c)) for t in range(nt) for j, chip in enumerate(chips)]
        passed = [copy(t, 3 + j, (*chip, c), sibling) for t in range(nt) for j, chip in enumerate(chips)]
        from_sibling = [copy(t, 3 + j, (*chip, 1 - c), (x, y, c)) for t in range(nt) for j, chip in enumerate(chips)]
        return own, first, landed, passed, from_sibling

    def start(w_refs, out_refs, sems):
        own, first = parts(w_refs, out_refs, sems, False)
        for cp in own + first:
            cp.start()

    def finish(w_refs, out_refs, sems):
        own, first, landed, passed, from_sibling = parts(w_refs, out_refs, sems, True)
        for arrived, fwd in zip(landed, passed):
            arrived.wait_recv()
            fwd.start()
        for cp in from_sibling:
            cp.wait_recv()
        for cp in first + passed:
            cp.wait_send()
        for cp in own:
            cp.wait()

    sems = [pltpu.SemaphoreType.DMA((6 * nt,)), pltpu.SemaphoreType.DMA((6 * nt,)),
            pltpu.SemaphoreType.DMA((nt,)), pltpu.SemaphoreType.DMA((nt,))]
    return _Hosted(list(shards), [_sds((N_CHIPS, *w.shape), w.dtype) for w in shards], sems, start, finish)


def all_gather_chips(shards, name):
    comm = _gather_comm(shards)
    nt = len(shards)

    def body(*refs):
        comm.start(refs[:nt], refs[nt:2 * nt], refs[2 * nt:])
        comm.finish(refs[:nt], refs[nt:2 * nt], refs[2 * nt:])

    hbm = pl.BlockSpec(memory_space=pl.ANY)
    return pl.pallas_call(body, name=name, out_shape=comm.out_shape, in_specs=[hbm] * nt, out_specs=[hbm] * nt,
                          scratch_shapes=comm.sems)(*shards)


def _row_block(r, n, itemsize):
    best = None
    for br in range(16, r + 1, 16):
        if r % br == 0 and br * n * itemsize <= COMM_BLOCK_BYTES:
            best = br
    return r if best is None else best


def _scatter_comm(parts):
    nt = len(parts)

    def copies(p_refs, b_refs, sems, arriving):
        send_sems, recv_sems = sems
        x, y, c = _place()
        me = 4 * x + 2 * y + c
        cps = []
        for t in range(nt):
            for r in range(1, 8):
                tx = 1 - x if r & 4 else x
                ty = 1 - y if r & 2 else y
                tc = 1 - c if r & 1 else c
                src, dst = (2 * x + y, c), 4 * tx + 2 * ty + tc
                if not arriving:
                    src, dst = (2 * tx + ty, tc), me
                cps.append(pltpu.make_async_remote_copy(
                    src_ref=p_refs[t].at[src], dst_ref=b_refs[t].at[dst], send_sem=send_sems.at[7 * t + r - 1],
                    recv_sem=recv_sems.at[7 * t + r - 1], device_id=(tx, ty, tc), device_id_type=MESH))
        return cps

    def start(p_refs, b_refs, sems):
        for cp in copies(p_refs, b_refs, sems, False):
            cp.start()

    def finish(p_refs, b_refs, sems):
        for cp in copies(p_refs, b_refs, sems, True):
            cp.wait_recv()
        for cp in copies(p_refs, b_refs, sems, False):
            cp.wait_send()

    sems = [pltpu.SemaphoreType.DMA((7 * nt,)), pltpu.SemaphoreType.DMA((7 * nt,))]
    return _Hosted(list(parts), [_sds((2 * N_CHIPS, *p.shape[2:]), p.dtype) for p in parts], sems, start, finish)


def sum_devices(own, recv, place, name, slot=(0, 1, None)):
    _, _, r, n = own.shape
    layer, n_layers, buf = slot
    br = _row_block(r, n, 4 * 8)

    def body(p_ref, o_ref, *rest):
        acc = o_ref[...]
        for kk in range(7):
            acc = acc + rest[kk][...].astype(F32)
        rest[-1][...] = acc

    def arrived(rel):
        return pl.BlockSpec((None, br, n), lambda i, pref: (jnp.bitwise_xor(pref[0], rel), i, 0))

    in_specs = [pl.BlockSpec((None, None, br, n), lambda i, pref: (pref[2], pref[1], i, 0))]
    in_specs += [arrived(rel) for rel in range(1, 8)]
    args = [own] + [recv] * 7
    aliases = {}
    if buf is not None:
        in_specs.append(pl.BlockSpec(memory_space=pl.ANY))
        args.append(buf)
        aliases = {9: 0}
    return pl.pallas_call(
        body, name=name,
        grid_spec=pltpu.PrefetchScalarGridSpec(
            num_scalar_prefetch=1, grid=(r // br,), in_specs=in_specs,
            out_specs=pl.BlockSpec((None, None, br, n), lambda i, pref: (layer, pref[1], i, 0))),
        out_shape=_sds((n_layers, 2, r, n), F32), input_output_aliases=aliases, compiler_params=_cp(1),
    )(place, *args)


def sibling_join_halves(bufs, name):
    nt = len(bufs)
    layers = [bf.shape[0] for bf in bufs]
    first = [sum(layers[:t]) for t in range(nt)]

    def body(*refs):
        o_refs = refs[nt:2 * nt]
        send_sems, recv_sems = refs[2 * nt:]
        x, y, c = _place()

        def copy(t, l, hh):
            return pltpu.make_async_remote_copy(
                src_ref=o_refs[t].at[l, hh], dst_ref=o_refs[t].at[l, hh], send_sem=send_sems.at[first[t] + l],
                recv_sem=recv_sems.at[first[t] + l], device_id=(x, y, 1 - c), device_id_type=MESH)

        cps = [copy(t, l, c) for t in range(nt) for l in range(layers[t])]
        for cp in cps:
            cp.start()
        for t in range(nt):
            for l in range(layers[t]):
                copy(t, l, 1 - c).wait_recv()
        for cp in cps:
            cp.wait_send()

    hbm = pl.BlockSpec(memory_space=pl.ANY)
    return pl.pallas_call(
        body, name=name, out_shape=[_sds(bf.shape, bf.dtype) for bf in bufs],
        in_specs=[hbm] * nt, out_specs=[hbm] * nt, input_output_aliases={t: t for t in range(nt)},
        scratch_shapes=[pltpu.SemaphoreType.DMA((sum(layers),)), pltpu.SemaphoreType.DMA((sum(layers),))],
    )(*bufs)


_SHARD_KIND = {"mla_w_in": "rows", "mla_w_uq": "cols", "mla_w_uk": "cols", "mla_w_uv": "cols", "mla_w_o": "rows",
               "fox_w_in": "cols", "fox_w_o": "rows", "ffn_w_gate": "chunk", "ffn_w_up": "chunk", "ffn_w_down": "chunk"}
_PACKED = tuple(_SHARD_KIND)
_TRANSPOSED = ("ffn_w_gate", "ffn_w_up", "fox_w_in")


def _halves(shard):
    if shard.ndim == 3 and shard.shape[0] == 2:
        return shard
    r, n = shard.shape[-2:]
    return shard.reshape(2, r // 2, n)


def _cols_to_full(g):
    return jnp.transpose(g, (1, 0, 2)).reshape(g.shape[1], -1)


def _full_to_cols(w):
    k, n4 = w.shape
    return jnp.transpose(w.reshape(k, N_CHIPS, n4 // N_CHIPS), (1, 0, 2))


def _uq_perm():
    per = MLA_NOPE + MLA_ROPE
    half = MLA_ROPE // 2
    nope = [h * per + d for h in range(MLA_HEADS) for d in range(MLA_NOPE)]
    r1 = [h * per + MLA_NOPE + r for h in range(MLA_HEADS) for r in range(half)]
    r2 = [h * per + MLA_NOPE + half + r for h in range(MLA_HEADS) for r in range(half)]
    perm = np.array(nope + r1 + r2, dtype=np.int32)
    return perm, np.argsort(perm).astype(np.int32)


def _rope_matrices():
    half = MLA_ROPE // 2
    nr = MLA_HEADS * MLA_ROPE
    to_heads = np.zeros((nr, nr), np.float32)
    from_heads = np.zeros((MLA_HEADS * 128, nr), np.float32)
    for e in range(2):
        for h in range(MLA_HEADS):
            for r in range(half):
                to_heads[e * MLA_HEADS * half + h * half + r, h * MLA_ROPE + e * half + r] = 1.0
                from_heads[h * 128 + e * half + r, e * MLA_HEADS * half + h * half + r] = 1.0
    head_sum = np.tile(np.eye(MLA_ROPE, dtype=np.float32), (2 * MLA_HEADS, 1))
    dup = np.concatenate([np.eye(MLA_ROPE, dtype=np.float32)] * 2, axis=1)
    return to_heads, from_heads, head_sum, dup


def _ffn_weights(gathered):
    return tuple(g.reshape(N_CHIPS, 2 * g.shape[2], g.shape[3]) for g in gathered)


def _fox_weights(gathered):
    w_in, w_o = gathered
    w_in = jnp.transpose(w_in, (0, 2, 1, 3)).reshape(N_CHIPS * w_in.shape[2], 2 * w_in.shape[3])
    return w_in, w_o.reshape(-1, w_o.shape[-1])


def _local_step(x, positions, target, mods, wts, ln_g, ln_b, mla_g_q, mla_g_kv, fox_b_f, shards=None):
    nb, s, d = x.shape
    t = nb * s
    x0 = x.reshape(t, d)
    tgt = target.reshape(t, d)
    perm, inv_perm = _uq_perm()

    half = MLA_ROPE // 2
    inv_freq = ROPE_THETA ** (-jnp.arange(half, dtype=F32) / half)
    ang = positions.astype(F32).reshape(t, 1) * inv_freq
    cos, sin = jnp.cos(ang), jnp.sin(ang)
    cos8, sin8 = jnp.tile(cos, (1, MLA_HEADS)), jnp.tile(sin, (1, MLA_HEADS))
    cos64 = jnp.concatenate([cos, cos], axis=1)
    sin64s = jnp.concatenate([-sin, sin], axis=1)
    swap64 = jnp.asarray(np.roll(np.eye(MLA_ROPE, dtype=np.float32), half, axis=1))
    to_heads, from_heads, head_sum, dup = _rope_matrices()
    to_heads, from_heads = jnp.asarray(to_heads, dtype=BF16), jnp.asarray(from_heads, dtype=BF16)
    head_sum, dup = jnp.asarray(head_sum, dtype=BF16), jnp.asarray(dup, dtype=BF16)
    sel_mla = jnp.asarray(np.pad(np.kron(np.eye(MLA_HEADS, dtype=np.float32), np.ones((MLA_V, 1), np.float32)),
                                 ((0, 0), (0, 128 - MLA_HEADS))))
    sel_fox = jnp.asarray(np.pad(np.kron(np.eye(FOX_HEADS, dtype=np.float32), np.ones((FOX_HD, 1), np.float32)),
                                 ((0, 0), (0, 128 - FOX_HEADS))))
    tri = jnp.asarray(np.tril(np.ones((128, 128), np.float32)))
    triu = jnp.asarray(np.triu(np.ones((128, 128), np.float32)))
    onehot16 = jnp.asarray(np.eye(16, 128, dtype=np.float32))

    def vec(a):
        return a.reshape(1, -1)

    def carried(key):
        return None if shards is None else _gather_comm(shards[key])

    def split(res):
        return (res, None) if shards is None else res

    w_uq_p = wts["mla_w_uq"][:, perm]
    b_f_pad = jnp.pad(fox_b_f.reshape(1, -1), ((0, 0), (0, 128 - FOX_HEADS)))

    sh_a, sc_a, gt_a, sh_f, sc_f, gt_f = mods[0]
    h_in, u_m = mod_linear(x0, sh_a, sc_a, wts["mla_w_in"], F32, "mla_in", emit_u=True)
    q_m, kn_m, v_m, kr2_m, cq_m, ckv_m = mla_mid_fwd(
        h_in, vec(mla_g_q), vec(mla_g_kv), w_uq_p, wts["mla_w_uk"], wts["mla_w_uv"], cos8, sin8, cos64, sin64s, swap64,
        to_heads, dup, "mla_mid")
    (o_m, lse_m), got = split(mla_attn_fwd(q_m, kn_m, kr2_m, v_m, nb, "mla_attn", hosted=carried("ffn0")))
    ffn0_w = wts["ffn"][0] if got is None else _ffn_weights(got)
    y0, x1 = linear_resid_ln(o_m, wts["mla_w_o"], x0, gt_a, vec(ln_g[0, 0]), vec(ln_b[0, 0]), "mla_out")
    (u_f0, hg0, hu0, y1, x2), got = split(ffn_fwd(x1, sh_f, sc_f, gt_f, *ffn0_w, vec(ln_g[0, 1]), vec(ln_b[0, 1]), "ffn0",
                                                  hosted=carried("fox")))
    fox_w_in_t, fox_w_o = (wts["fox_w_in"].T, wts["fox_w_o"]) if got is None else _fox_weights(got)
    fox_w_f_t = jnp.pad(fox_w_in_t[3 * d:], ((0, 128 - FOX_HEADS), (0, 0)))
    sh_a1, sc_a1, gt_a1, sh_f1, sc_f1, gt_f1 = mods[1]
    qkv, u_x = mod_linear(x2, sh_a1, sc_a1, fox_w_in_t, BF16, "fox_qkv", tn=1024, emit_u=True, w_rows=3 * d)
    hf = mod_linear(x2, sh_a1, sc_a1, fox_w_f_t, F32, "fox_f", w_rows=128)
    cum = fox_gate_fwd(hf, b_f_pad, tri, nb, "fox_gate")
    cum_rows = rows16(cum, "fox_cum_rows")
    (o_x, lse_x), got = split(fox_attn_fwd(qkv, cum, cum_rows, nb, "fox_attn", hosted=carried("ffn1")))
    ffn1_w = wts["ffn"][1] if got is None else _ffn_weights(got)
    y2, x3 = linear_resid_ln(o_x, fox_w_o, x2, gt_a1, vec(ln_g[1, 0]), vec(ln_b[1, 0]), "fox_out")
    u_f1, hg1, hu1, y3, x4 = ffn_fwd(x3, sh_f1, sc_f1, gt_f1, *ffn1_w, vec(ln_g[1, 1]), vec(ln_b[1, 1]), "ffn1")
    dx4, sq_err = loss_grad(x4, tgt, "loss")
    loss_part = 0.5 * jnp.sum(sq_err) / d

    parts, recv = {}, {}

    def halves_of(g):
        return g.reshape(N_CHIPS, 2, g.shape[1] // 2, g.shape[2])

    def scatter(keys, sent):
        return None if shards is None else _scatter_comm([sent[k] for k in keys])

    def landed(keys, got):
        if got is not None:
            recv.update(zip(keys, got))

    def ffn_grads(layer, u, dhg, dhu, act, dy):
        sent = {}
        for n, (a_op, b_op) in (("ffn_w_gate", (dhg, u[None])), ("ffn_w_up", (dhu, u[None])), ("ffn_w_down", (act, dy[None]))):
            g32, g16 = wgrad(a_op, b_op, "ffn%d_d%s" % (layer, n[4:]), with_bf16=True)
            parts["%s/%d" % (n, layer)], sent["%s/%d" % (n, layer)] = halves_of(g32), halves_of(g16)
        return sent

    dz3, dy3, dg11, db11, dgt_f1 = ln_bwd(dx4, x3, y3, gt_f1, vec(ln_g[1, 1]), "ffn1_ln_bwd")
    dhg1, dhu1, act1, dx3, dsc_f1, dsh_f1 = ffn_bwd(dy3, hg1, hu1, *ffn1_w, dz3, x3, sc_f1, "ffn1_bwd")
    sent = ffn_grads(1, u_f1, dhg1, dhu1, act1, dy3)
    dz2, dy2, dg10, db10, dgt_a1 = ln_bwd(dx3, x2, y2, gt_a1, vec(ln_g[1, 0]), "fox_ln_bwd")
    do_x, delta_x = linear_nt_delta(dy2, fox_w_o, o_x, sel_fox, "fox_out_bwd")
    (dq_x, dk_x, dv_x, dfq_x, dfk_x), got = split(fox_attn_bwd(
        qkv, do_x, cum, cum_rows, rows16(lse_x, "fox_lse_rows"), rows16(delta_x, "fox_delta_rows"), nb, "fox_attn_bwd",
        hosted=scatter(list(sent), sent)))
    landed(list(sent), got)
    dcum = tokens128(dfq_x + dfk_x, onehot16, "fox_dcum")
    dhf, dbf = fox_gate_bwd(dcum, hf, b_f_pad, triu, nb, "fox_gate_bwd")
    fox_d = [("q", dq_x), ("k", dk_x), ("v", dv_x)]
    dx2, dsc_a1, dsh_a1 = linear_nt_mod_bwd(
        [(dh, fox_w_in_t, i) for i, (_, dh) in enumerate(fox_d)] + [(dhf, fox_w_f_t, 0)], dz2, x2, sc_a1, "fox_in_bwd")
    dw_in_t = [wgrad(dh[None], u_x[None], "fox_dw" + tag)[0] for tag, dh in fox_d]
    dw_in_t.append(wgrad(dhf[None], u_x[None], "fox_dwf")[0][:FOX_HEADS])
    dw_in_t = jnp.concatenate(dw_in_t, axis=0).reshape(N_CHIPS, -1, 2, d // 2)
    parts["fox_w_in"] = jnp.transpose(dw_in_t, (0, 2, 1, 3))
    parts["fox_w_o"] = wgrad(o_x[None], dy2[None], "fox_dwo")[0].reshape(N_CHIPS, 2, -1, d)
    sent = {k: parts[k].astype(BF16) for k in ("fox_w_in", "fox_w_o")}
    dz1, dy1, dg01, db01, dgt_f0 = ln_bwd(dx2, x1, y1, gt_f, vec(ln_g[0, 1]), "ffn0_ln_bwd")
    (dhg0, dhu0, act0, dx1, dsc_f0, dsh_f0), got = split(ffn_bwd(dy1, hg0, hu0, *ffn0_w, dz1, x1, sc_f, "ffn0_bwd",
                                                                 hosted=scatter(list(sent), sent)))
    landed(list(sent), got)
    sent = ffn_grads(0, u_f0, dhg0, dhu0, act0, dy1)
    dz0, dy0, dg00, db00, dgt_a0 = ln_bwd(dx1, x0, y0, gt_a, vec(ln_g[0, 0]), "mla_ln_bwd")
    do_m, delta_m = linear_nt_delta(dy0, wts["mla_w_o"], o_m, sel_mla, "mla_out_bwd")
    parts["mla_w_o"] = wgrad(o_m[None], dy0[None], "mla_dwo")[0].reshape(N_CHIPS, 2, -1, d)
    sent["mla_w_o"] = parts["mla_w_o"].astype(BF16)
    (dqn_m, dqr_m, dkn_m, dkr_m, dv_m), got = split(mla_attn_bwd(
        q_m, kn_m, kr2_m, v_m, do_m, rows16(lse_m, "mla_lse_rows"), rows16(delta_m, "mla_delta_rows"), nb,
        "mla_attn_bwd", hosted=scatter(list(sent), sent)))
    landed(list(sent), got)
    dh_in, dq_pre, dgq, dgkv = mla_mid_bwd(
        dqn_m, dqr_m, dkn_m, dv_m, dkr_m, h_in, vec(mla_g_q), vec(mla_g_kv), w_uq_p, wts["mla_w_uk"],
        wts["mla_w_uv"], cos8, sin8, cos64, sin64s, swap64, from_heads, head_sum, "mla_mid_bwd")
    parts["mla_w_uq"] = halves_of(_full_to_cols(wgrad(cq_m[None], dq_pre[None], "mla_dwuq")[0][:, inv_perm]))
    parts["mla_w_uk"] = halves_of(_full_to_cols(wgrad(ckv_m[None], dkn_m[None], "mla_dwuk")[0]))
    parts["mla_w_uv"] = halves_of(_full_to_cols(wgrad(ckv_m[None], dv_m[None], "mla_dwuv")[0]))
    parts["mla_w_in"] = wgrad(u_m[None], dh_in[None], "mla_dwin")[0].reshape(N_CHIPS, 2, -1, h_in.shape[1])
    sent = {k: parts[k].astype(BF16) for k in ("mla_w_in", "mla_w_uq", "mla_w_uk", "mla_w_uv")}
    (dx0, dsc_a0, dsh_a0), got = split(linear_nt_mod_bwd([(dh_in, wts["mla_w_in"], None)], dz0, x0, sc_a, "mla_in_bwd",
                                                         hosted=scatter(list(sent), sent)))
    landed(list(sent), got)

    dmods = [(dsh_a0, dsc_a0, dgt_a0, dsh_f0, dsc_f0, dgt_f0), (dsh_a1, dsc_a1, dgt_a1, dsh_f1, dsc_f1, dgt_f1)]
    d_ln_g = jnp.stack([jnp.concatenate([dg00, dg01], axis=0), jnp.concatenate([dg10, dg11], axis=0)])
    d_ln_b = jnp.stack([jnp.concatenate([db00, db01], axis=0), jnp.concatenate([db10, db11], axis=0)])
    return loss_part, dx0.reshape(nb, s, d), (parts, recv), dmods, d_ln_g, d_ln_b, dgq, dgkv, dbf[:, :FOX_HEADS]


def _pad_rows(a, rows):
    return jnp.pad(a, ((0, rows - a.shape[0]), (0, 0)))


def kernel(x, c, positions, mla_w_in, mla_g_q, mla_w_uq, mla_g_kv, mla_w_uk, mla_w_uv, mla_w_o, fox_w_in, fox_b_f, fox_w_o, ada_w, ada_b, ffn_w_gate, ffn_w_up, ffn_w_down, ln_g, ln_b, loss_target, m_mla_w_in, m_mla_g_q, m_mla_w_uq, m_mla_g_kv, m_mla_w_uk, m_mla_w_uv, m_mla_w_o, m_fox_w_in, m_fox_b_f, m_fox_w_o, m_ada_w, m_ada_b, m_ffn_w_gate, m_ffn_w_up, m_ffn_w_down, m_ln_g, m_ln_b, v_mla_w_in, v_mla_g_q, v_mla_w_uq, v_mla_g_kv, v_mla_w_uk, v_mla_w_uv, v_mla_w_o, v_fox_w_in, v_fox_b_f, v_fox_w_o, v_ada_w, v_ada_b, v_ffn_w_gate, v_ffn_w_up, v_ffn_w_down, v_ln_g, v_ln_b):
    args = dict(locals())
    nb, s, d = x.shape
    ax, ay, ac = lax.axis_index("x"), lax.axis_index("y"), lax.axis_index("c")
    chip = 2 * ax + ay
    dev = 2 * chip + ac
    n_dev = 2 * N_CHIPS
    n_all = nb * n_dev

    shard_shapes = {n: (args[n].shape if _SHARD_KIND[n] == "chunk" else args[n].shape[1:]) for n in _PACKED}

    def block(n, layer=None):
        w = args[n].reshape(shard_shapes[n]) if layer is None else args[n][layer]
        return _halves(w.astype(BF16))

    mla_names = [n for n in _PACKED if n.startswith("mla")]
    wts = {}
    for n, g in zip(mla_names, all_gather_chips([block(n) for n in mla_names], "gather_mla")):
        g = g.reshape(N_CHIPS, *shard_shapes[n])
        wts[n] = g.reshape(-1, g.shape[-1]) if _SHARD_KIND[n] == "rows" else _cols_to_full(g)
    ffn_names = ("ffn_w_gate", "ffn_w_up", "ffn_w_down")
    fox_in_t = jnp.swapaxes(fox_w_in, 1, 2)[0].astype(BF16)
    fox_in_t = jnp.stack([fox_in_t[:, :d // 2], fox_in_t[:, d // 2:]])
    shards = {"ffn0": [block(n, 0) for n in ffn_names], "fox": [fox_in_t, block("fox_w_o")],
              "ffn1": [block(n, 1) for n in ffn_names]}

    ln_cols = ln_g.shape[-1]
    ln_blk = jnp.concatenate([ln_g.reshape(2 * DEPTH, ln_cols), ln_b.reshape(2 * DEPTH, ln_cols)], axis=0)
    early = jnp.concatenate([_pad_rows(c, 8), jnp.pad(_pad_rows(ln_blk, 8), ((0, 0), (0, d - ln_cols)))], axis=0)
    early = all_gather8(early, "gather_c_ln").reshape(n_dev, 16, d)
    c_all = early[:, :nb].reshape(n_all, d)
    ln_all = early.reshape(N_CHIPS, 2, 16, d)[:, 0, 8:8 + 4 * DEPTH, :ln_cols]
    ln_all = jnp.transpose(ln_all, (1, 0, 2)).reshape(4 * DEPTH, d)
    ln_g_full = ln_all[:2 * DEPTH].reshape(DEPTH, 2, d)
    ln_b_full = ln_all[2 * DEPTH:].reshape(DEPTH, 2, d)
    mod_part = ada_mod_part(c_all, ada_w, "ada_mod")
    ncol = mod_part.shape[-1]
    mod_g = all_gather8(mod_part.reshape(DEPTH * n_all, ncol), "gather_mod")
    mod_g = mod_g.reshape(N_CHIPS, 2, DEPTH, n_all, ncol)[:, 0]
    mod_full = jnp.transpose(mod_g, (1, 2, 0, 3)).reshape(DEPTH, n_all, N_CHIPS * ncol) + ada_b[:, None, :]
    mod_loc = lax.dynamic_slice_in_dim(mod_full, dev * nb, nb, axis=1)
    mods = [tuple(mod_loc[i, :, k * d:(k + 1) * d].reshape(nb, 1, d) for k in range(6)) for i in range(DEPTH)]

    loss_part, grad_x, (parts, recv), dmods, d_ln_g, d_ln_b, dgq, dgkv, dbf = _local_step(
        x, positions, loss_target, mods, wts, ln_g_full, ln_b_full, mla_g_q[0], mla_g_kv[0], fox_b_f[0], shards)
    loss = lax.psum(loss_part, ("x", "y", "c"))

    dmod_rows = jnp.stack([jnp.concatenate([v_.reshape(nb, d) for v_ in dm], axis=1) for dm in dmods])
    small = jnp.concatenate([
        d_ln_g.reshape(2 * DEPTH, d), d_ln_b.reshape(2 * DEPTH, d),
        jnp.pad(jnp.concatenate([dgq, dgkv, dbf], axis=1), ((0, 0), (0, d - 2 * MLA_QR - FOX_HEADS))),
        dmod_rows.reshape(DEPTH * nb * 6, d)], axis=0)
    n_small = small.shape[0]
    small_rows = -(-n_small // 8) * 8
    small_all = all_gather8(_pad_rows(small, small_rows), "gather_stats").reshape(n_dev, small_rows, d)
    stat_sum = sum_leading(small_all, "sum_stats")
    g_ln_g = lax.dynamic_slice_in_dim(stat_sum[:2 * DEPTH], chip * ln_cols, ln_cols, axis=1).reshape(DEPTH, 2, ln_cols)
    g_ln_b = lax.dynamic_slice_in_dim(stat_sum[2 * DEPTH:4 * DEPTH], chip * ln_cols, ln_cols, axis=1).reshape(DEPTH, 2, ln_cols)
    row = stat_sum[4 * DEPTH]
    g_gq = row[:MLA_QR].reshape(1, MLA_QR)
    g_gkv = row[MLA_QR:2 * MLA_QR].reshape(1, MLA_KVR)
    g_bf = row[2 * MLA_QR:2 * MLA_QR + FOX_HEADS].reshape(1, FOX_HEADS)
    base = 4 * DEPTH + 1
    dmod_all = small_all[:, base:base + DEPTH * nb * 6].reshape(n_dev, DEPTH, nb, 6 * d)
    dmod_all = jnp.transpose(dmod_all, (1, 0, 2, 3)).reshape(DEPTH, n_all, 6 * d)
    g_ada_b = sum_leading(jnp.transpose(dmod_all, (1, 0, 2)), "sum_ada_b")
    dmod_mine = lax.dynamic_slice_in_dim(dmod_all, chip * ncol, ncol, axis=2)
    g_ada_w = ada_grad(c_all.T, dmod_mine, "ada_grad")

    place = jnp.stack([dev, ac, chip]).astype(jnp.int32)
    bufs = []
    for n in _PACKED:
        if _SHARD_KIND[n] == "chunk":
            buf = None
            for layer in range(DEPTH):
                key = "%s/%d" % (n, layer)
                buf = sum_devices(parts[key], recv[key], place, "rs_sum_%s%d" % (n, layer), slot=(layer, DEPTH, buf))
        else:
            buf = sum_devices(parts[n], recv[n], place, "rs_sum_" + n)
        bufs.append(buf)
    joined = sibling_join_halves(bufs, "rs_join")
    g_big = {n: j.reshape(j.shape[0], 2 * j.shape[2], j.shape[3]) for n, j in zip(_PACKED, joined)}
    j = joined[_PACKED.index("fox_w_in")]
    g_big["fox_w_in"] = jnp.transpose(j, (0, 2, 1, 3)).reshape(1, j.shape[2], 2 * j.shape[3])

    g_out = {
        "mla_w_in": g_big["mla_w_in"], "mla_g_q": g_gq, "mla_w_uq": g_big["mla_w_uq"], "mla_g_kv": g_gkv,
        "mla_w_uk": g_big["mla_w_uk"], "mla_w_uv": g_big["mla_w_uv"], "mla_w_o": g_big["mla_w_o"],
        "fox_w_in": g_big["fox_w_in"], "fox_b_f": g_bf, "fox_w_o": g_big["fox_w_o"],
        "ada_w": g_ada_w, "ada_b": g_ada_b, "ffn_w_gate": g_big["ffn_w_gate"], "ffn_w_up": g_big["ffn_w_up"],
        "ffn_w_down": g_big["ffn_w_down"], "ln_g": g_ln_g, "ln_b": g_ln_b}
    names = ["mla_w_in", "mla_g_q", "mla_w_uq", "mla_g_kv", "mla_w_uk", "mla_w_uv", "mla_w_o", "fox_w_in", "fox_b_f",
             "fox_w_o", "ada_w", "ada_b", "ffn_w_gate", "ffn_w_up", "ffn_w_down", "ln_g", "ln_b"]
    small_names = ["mla_g_q", "mla_g_kv", "fox_b_f", "ada_b", "ln_g", "ln_b"]
    deltas, new_m, new_v = {}, {}, {}
    for n in names:
        if n in small_names:
            continue
        shp = args[n].shape
        if n in _TRANSPOSED:
            view = lambda a: jnp.swapaxes(a, 1, 2).reshape(-1, shp[1])
            back = lambda a: jnp.swapaxes(a.reshape(shp[0], shp[2], shp[1]), 1, 2)
        else:
            view = lambda a: a.reshape(-1, shp[-1])
            back = lambda a: a.reshape(shp)
        dl, mn, vn = adamw(view(args[n]), g_out[n].reshape(view(args[n]).shape), view(args["m_" + n]),
                           view(args["v_" + n]), "adamw_" + n)
        g_out[n], deltas[n], new_m[n], new_v[n] = back(g_out[n].reshape(view(args[n]).shape)), back(dl), back(mn), back(vn)

    def small_pack(prefix, src):
        flat = jnp.concatenate([src[prefix + n].reshape(-1) for n in small_names])
        size = -(-flat.shape[0] // (8 * 128)) * 8 * 128
        return jnp.pad(flat, (0, size - flat.shape[0])).reshape(-1, 128)

    sd, sm, sv = adamw(small_pack("", args), small_pack("", g_out), small_pack("m_", args), small_pack("v_", args),
                       "adamw_small")
    off = 0
    for n in small_names:
        shp = args[n].shape
        size = math.prod(shp)
        deltas[n] = sd.reshape(-1)[off:off + size].reshape(shp)
        new_m[n] = sm.reshape(-1)[off:off + size].reshape(shp)
        new_v[n] = sv.reshape(-1)[off:off + size].reshape(shp)
        off += size

    outs = [loss, grad_x]
    outs += [g_out[n].reshape(args[n].shape) for n in names]
    outs += [deltas[n] for n in names] + [new_m[n] for n in names] + [new_v[n] for n in names]
    return tuple(outs)
```

```python
import functools
import math

import numpy as np
import jax
import jax.numpy as jnp
from jax import lax
from jax.experimental import pallas as pl
from jax.experimental.pallas import tpu as pltpu

F32 = jnp.float32
BF16 = jnp.bfloat16
MESH = pl.DeviceIdType.MESH

D_MODEL = 1024
DEPTH = 2
MLA_HEADS = 8
MLA_NOPE = 128
MLA_ROPE = 64
MLA_V = 128
MLA_QR = 256
MLA_KVR = 256
ROPE_THETA = 10000.0
FOX_HEADS = 16
FOX_HD = 64
D_FF = 2816
N_CHIPS = 4
FF_CHUNK = D_FF // N_CHIPS
ALPHA = (2.0 * DEPTH) ** 0.25
EPS = 1e-5
ADAM_LR = 0.001
ADAM_B1 = 0.9
ADAM_B2 = 0.999
ADAM_EPS = 1e-08
ADAM_WD = 0.01
ADAM_STEP = 10

VMEM_LIMIT = 56 * 1024 * 1024
TOKEN_TILE = 512
WGRAD_TOKENS = 2048
ATTN_TILE = 512
COMM_BLOCK_BYTES = 2 * 1024 * 1024
ADAMW_BLOCK_BYTES = 1024 * 1024


def _cp(n_axes):
    return pltpu.CompilerParams(dimension_semantics=("arbitrary",) * n_axes, vmem_limit_bytes=VMEM_LIMIT)


def _dot(a, b):
    return jnp.dot(a, b, preferred_element_type=F32)


def _dot_nt(a, b):
    return lax.dot_general(a, b, (((1,), (1,)), ((), ())), preferred_element_type=F32)


def _dot_tn(a, b):
    return lax.dot_general(a, b, (((0,), (0,)), ((), ())), preferred_element_type=F32)


def _dot_f32(a, b):
    return jnp.dot(a, b, preferred_element_type=F32, precision=lax.Precision.HIGHEST)


def _sds(shape, dtype):
    return jax.ShapeDtypeStruct(shape, dtype)


def _place():
    return lax.axis_index("x"), lax.axis_index("y"), lax.axis_index("c")


class _Hosted:
    def __init__(self, inputs, out_shape, sems, start, finish):
        self.inputs, self.out_shape, self.sems, self.start, self.finish = inputs, out_shape, sems, start, finish


def _call(body, name, grid, in_specs, out_specs, out_shape, args, scratch_shapes=(), hosted=None):
    in_specs, out_specs, out_shape, scratch_shapes = list(in_specs), list(out_specs), list(out_shape), list(scratch_shapes)
    if hosted is None:
        return pl.pallas_call(body, name=name, grid=grid, in_specs=in_specs, out_specs=out_specs, out_shape=out_shape,
                              scratch_shapes=scratch_shapes, compiler_params=_cp(len(grid)))(*args)
    n_in, n_out, n_scr = len(in_specs), len(out_specs), len(scratch_shapes)
    h_in, h_out = len(hosted.inputs), len(hosted.out_shape)

    def carried(*refs):
        o0 = n_in + h_in
        s0 = o0 + n_out + h_out
        c_in, c_out, c_sem = refs[n_in:o0], refs[o0 + n_out:s0], refs[s0 + n_scr:]
        ids = [pl.program_id(a) for a in range(len(grid))]
        first = functools.reduce(jnp.logical_and, [i == 0 for i in ids])
        last = functools.reduce(jnp.logical_and, [i == g - 1 for i, g in zip(ids, grid)])

        @pl.when(first)
        def _():
            hosted.start(c_in, c_out, c_sem)

        body(*refs[:n_in], *refs[o0:o0 + n_out], *refs[s0:s0 + n_scr])

        @pl.when(last)
        def _():
            hosted.finish(c_in, c_out, c_sem)

    hbm = pl.BlockSpec(memory_space=pl.ANY)
    res = pl.pallas_call(
        carried, name=name, grid=grid, in_specs=in_specs + [hbm] * h_in, out_specs=out_specs + [hbm] * h_out,
        out_shape=out_shape + list(hosted.out_shape), scratch_shapes=scratch_shapes + list(hosted.sems),
        compiler_params=_cp(len(grid)))(*args, *hosted.inputs)
    return res[:n_out], res[n_out:]


def mod_linear(x, shift, scale, w, out_dtype, name, tn=None, emit_u=False, w_rows=None):
    t, d = x.shape
    n = w.shape[1] if w_rows is None else w_rows
    tn = n if tn is None else tn
    tm = TOKEN_TILE
    tps = (t // shift.shape[0]) // tm

    def body(x_ref, sh_ref, sc_ref, w_ref, o_ref, *rest):
        u = (x_ref[...] * (1.0 + sc_ref[...]) + sh_ref[...]).astype(BF16)
        o_ref[...] = (_dot(u, w_ref[...]) if w_rows is None else _dot_nt(u, w_ref[...])).astype(out_dtype)
        if emit_u:
            @pl.when(pl.program_id(1) == 0)
            def _():
                rest[0][...] = u

    vec = pl.BlockSpec((None, 1, d), lambda i, j: (i // tps, 0, 0))
    out_shape = [_sds((t, n), out_dtype)]
    out_specs = [pl.BlockSpec((tm, tn), lambda i, j: (i, j))]
    if emit_u:
        out_shape.append(_sds((t, d), BF16))
        out_specs.append(pl.BlockSpec((tm, d), lambda i, j: (i, 0)))
    w_spec = pl.BlockSpec((d, tn), lambda i, j: (0, j)) if w_rows is None else pl.BlockSpec((tn, d), lambda i, j: (j, 0))
    res = pl.pallas_call(
        body, name=name, grid=(t // tm, n // tn),
        in_specs=[pl.BlockSpec((tm, d), lambda i, j: (i, 0)), vec, vec, w_spec],
        out_specs=out_specs, out_shape=out_shape, compiler_params=_cp(2),
    )(x, shift, scale, w)
    return res if emit_u else res[0]


def _rms(h, g):
    rstd = lax.rsqrt(jnp.mean(h * h, axis=-1, keepdims=True) + EPS)
    return h * rstd, rstd


def mla_mid_fwd(h, g_q, g_kv, w_uq, w_uk, w_uv, cos8, sin8, cos64, sin64s, swap64, rope_to_heads, dup64, name):
    t = h.shape[0]
    tm = TOKEN_TILE
    hq = MLA_HEADS * MLA_NOPE
    hr = MLA_HEADS * MLA_ROPE // 2

    def body(h_ref, gq_ref, gkv_ref, wuq_ref, wuk_ref, wuv_ref, c8_ref, s8_ref, c64_ref, s64_ref, sw_ref, p_ref, d_ref,
             q_ref, kn_ref, v_ref, kr_ref, cq_ref, ckv_ref):
        hh = h_ref[...]
        cq = (_rms(hh[:, :MLA_QR], None)[0] * gq_ref[...]).astype(BF16)
        ckv = (_rms(hh[:, MLA_QR:MLA_QR + MLA_KVR], None)[0] * gkv_ref[...]).astype(BF16)
        cq_ref[...] = cq
        ckv_ref[...] = ckv
        q = _dot(cq, wuq_ref[...])
        x1 = q[:, hq:hq + hr]
        x2 = q[:, hq + hr:]
        cs = c8_ref[...]
        sn = s8_ref[...]
        rot = jnp.concatenate([x1 * cs - x2 * sn, x2 * cs + x1 * sn], axis=1).astype(BF16)
        q_ref[...] = jnp.concatenate([q[:, :hq].astype(BF16), _dot(rot, p_ref[...]).astype(BF16)], axis=1)
        kn_ref[...] = _dot(ckv, wuk_ref[...]).astype(BF16)
        v_ref[...] = _dot(ckv, wuv_ref[...]).astype(BF16)
        kr = hh[:, MLA_QR + MLA_KVR:]
        kr = (kr * c64_ref[...] + _dot_f32(kr, sw_ref[...]) * s64_ref[...]).astype(BF16)
        kr_ref[...] = _dot(kr, d_ref[...]).astype(BF16)

    def rows(n):
        return pl.BlockSpec((tm, n), lambda i: (i, 0))

    def whole(a):
        return pl.BlockSpec(a.shape, lambda i: (0,) * a.ndim)

    nq = w_uq.shape[1]
    return pl.pallas_call(
        body, name=name, grid=(t // tm,),
        in_specs=[rows(h.shape[1]), whole(g_q), whole(g_kv), whole(w_uq), whole(w_uk), whole(w_uv),
                  rows(hr), rows(hr), rows(MLA_ROPE), rows(MLA_ROPE), whole(swap64), whole(rope_to_heads), whole(dup64)],
        out_specs=[rows(nq), rows(hq), rows(hq), rows(2 * MLA_ROPE), rows(MLA_QR), rows(MLA_KVR)],
        out_shape=[_sds((t, nq), BF16), _sds((t, hq), BF16), _sds((t, hq), BF16), _sds((t, 2 * MLA_ROPE), BF16),
                   _sds((t, MLA_QR), BF16), _sds((t, MLA_KVR), BF16)],
        compiler_params=_cp(1),
    )(h, g_q, g_kv, w_uq, w_uk, w_uv, cos8, sin8, cos64, sin64s, swap64, rope_to_heads, dup64)


def _pick_lane(tile, idx):
    lane = lax.broadcasted_iota(jnp.int32, tile.shape, 1)
    return jnp.sum(jnp.where(lane == idx, tile, 0.0), axis=1, keepdims=True)


def _pick_row(tile, idx):
    row = lax.broadcasted_iota(jnp.int32, tile.shape, 0)
    return jnp.sum(jnp.where(row == idx, tile, 0.0), axis=0, keepdims=True)


def _put_lane(tile, idx, col):
    lane = lax.broadcasted_iota(jnp.int32, tile.shape, 1)
    return jnp.where(lane == idx, col, tile)


def _put_row(tile, idx, row):
    r = lax.broadcasted_iota(jnp.int32, tile.shape, 0)
    return tile + jnp.where(r == idx, row, 0.0)


def _causal_softmax_blocks(i, tq, heads):
    def block(j, carry, masked):
        new = []
        for (score_fn, pv_fn, _), (m, l, acc) in zip(heads, carry):
            sc = score_fn(j)
            if masked:
                keep = lax.broadcasted_iota(jnp.int32, sc.shape, 0) >= lax.broadcasted_iota(jnp.int32, sc.shape, 1)
                sc = jnp.where(keep, sc, -1e30)
            m_new = jnp.maximum(m, jnp.max(sc, axis=1, keepdims=True))
            a = jnp.exp(m - m_new)
            p = jnp.exp(sc - m_new)
            new.append((m_new, a * l + jnp.sum(p, axis=1, keepdims=True), a * acc + pv_fn(j, p.astype(BF16))))
        return tuple(new)

    init = tuple((jnp.full((tq, 1), -1e30, F32), jnp.zeros((tq, 1), F32), jnp.zeros((tq, dv), F32)) for _, _, dv in heads)
    carry = lax.fori_loop(0, i, lambda j, c: block(j, c, False), init)
    return [(acc / l, m + jnp.log(l)) for m, l, acc in block(i, carry, True)]


def fox_attn_fwd(qkv, cum, cum_rows, nb, name, hosted=None):
    t = qkv.shape[0]
    s = t // nb
    tq = ATTN_TILE
    nq = s // tq
    npairs = FOX_HEADS // 2
    scale = FOX_HD ** -0.5

    def body(q_ref, k_ref, v_ref, cum_ref, cr_ref, o_ref, lse_ref):
        i = pl.program_id(1)
        hp = pl.program_id(2)

        @pl.when(hp == 0)
        def _():
            lse_ref[...] = jnp.zeros_like(lse_ref)

        q = q_ref[...]
        low = lax.broadcasted_iota(jnp.int32, q.shape, 1) < FOX_HD
        cum_t = cum_ref[...]

        def rows_of(j):
            return pl.ds(pl.multiple_of(j * tq, tq), tq)

        def head(a):
            hd = 2 * hp + a
            qa = jnp.where(low if a == 0 else jnp.logical_not(low), q, jnp.zeros_like(q))
            fq = _pick_lane(cum_t, hd)
            return (lambda j: _dot_nt(qa, k_ref[rows_of(j), :]) * scale + fq - _pick_row(cr_ref[j], hd),
                    lambda j, p: _dot(p, v_ref[rows_of(j), :]), 2 * FOX_HD)

        (o_0, lse_0), (o_1, lse_1) = _causal_softmax_blocks(i, tq, [head(0), head(1)])
        o_ref[...] = jnp.where(low, o_0, o_1).astype(BF16)
        lse_ref[...] = _put_lane(_put_lane(lse_ref[...], 2 * hp, lse_0), 2 * hp + 1, lse_1)

    return _call(
        body, name, (nb, nq, npairs),
        [pl.BlockSpec((tq, 128), lambda b, i, hp: (b * nq + i, hp)),
         pl.BlockSpec((s, 128), lambda b, i, hp: (b, npairs + hp)),
         pl.BlockSpec((s, 128), lambda b, i, hp: (b, 2 * npairs + hp)),
         pl.BlockSpec((tq, 128), lambda b, i, hp: (b * nq + i, 0)),
         pl.BlockSpec((nq, 16, tq), lambda b, i, hp: (b, 0, 0))],
        [pl.BlockSpec((tq, 128), lambda b, i, hp: (b * nq + i, hp)),
         pl.BlockSpec((tq, 128), lambda b, i, hp: (b * nq + i, 0))],
        [_sds((t, D_MODEL), BF16), _sds((t, 128), F32)], (qkv, qkv, qkv, cum, cum_rows), hosted=hosted)


def mla_attn_fwd(q, kn, kr2, v, nb, name, hosted=None):
    t = q.shape[0]
    s = t // nb
    tq = ATTN_TILE
    nq = s // tq
    npairs = MLA_HEADS // 2
    scale = (MLA_NOPE + MLA_ROPE) ** -0.5

    def body(qn_ref, qr_ref, kn_ref, kr_ref, v_ref, o_ref, lse_ref):
        i = pl.program_id(1)
        hp = pl.program_id(2)

        @pl.when(hp == 0)
        def _():
            lse_ref[...] = jnp.zeros_like(lse_ref)

        qr = qr_ref[...]
        low = lax.broadcasted_iota(jnp.int32, qr.shape, 1) < MLA_ROPE

        def rows_of(j):
            return pl.ds(pl.multiple_of(j * tq, tq), tq)

        def head(a):
            cols = slice(a * MLA_NOPE, (a + 1) * MLA_NOPE)
            q_cat = jnp.concatenate([qn_ref[:, cols], jnp.where(low if a == 0 else jnp.logical_not(low), qr,
                                                                jnp.zeros_like(qr))], axis=1)
            return (lambda j: _dot_nt(q_cat, jnp.concatenate([kn_ref[rows_of(j), cols], kr_ref[rows_of(j), :]], axis=1)) * scale,
                    lambda j, p: _dot(p, v_ref[rows_of(j), cols]), MLA_V)

        (o_0, lse_0), (o_1, lse_1) = _causal_softmax_blocks(i, tq, [head(0), head(1)])
        o_ref[...] = jnp.concatenate([o_0, o_1], axis=1).astype(BF16)
        lse_ref[...] = _put_lane(_put_lane(lse_ref[...], 2 * hp, lse_0), 2 * hp + 1, lse_1)

    wide = 2 * MLA_NOPE
    return _call(
        body, name, (nb, nq, npairs),
        [pl.BlockSpec((tq, wide), lambda b, i, hp: (b * nq + i, hp)),
         pl.BlockSpec((tq, 128), lambda b, i, hp: (b * nq + i, MLA_HEADS + hp)),
         pl.BlockSpec((s, wide), lambda b, i, hp: (b, hp)),
         pl.BlockSpec((s, 128), lambda b, i, hp: (b, 0)),
         pl.BlockSpec((s, wide), lambda b, i, hp: (b, hp))],
        [pl.BlockSpec((tq, wide), lambda b, i, hp: (b * nq + i, hp)),
         pl.BlockSpec((tq, 128), lambda b, i, hp: (b * nq + i, 0))],
        [_sds((t, MLA_HEADS * MLA_V), BF16), _sds((t, 128), F32)], (q, q, kn, kr2, v), hosted=hosted)


def rows16(a, name):
    t = a.shape[0]
    tq = ATTN_TILE

    def body(a_ref, o_ref):
        o_ref[...] = a_ref[...].T[:16, :]

    return pl.pallas_call(
        body, name=name, grid=(t // tq,), in_specs=[pl.BlockSpec((tq, 128), lambda n: (n, 0))],
        out_specs=pl.BlockSpec((None, 16, tq), lambda n: (n, 0, 0)), out_shape=_sds((t // tq, 16, tq), F32),
        compiler_params=_cp(1),
    )(a)


def tokens128(rows, onehot, name):
    nblk, _, tq = rows.shape

    def body(r_ref, e_ref, o_ref):
        o_ref[...] = lax.dot_general(r_ref[...], e_ref[...], (((0,), (0,)), ((), ())), preferred_element_type=F32,
                                     precision=lax.Precision.HIGHEST)

    return pl.pallas_call(
        body, name=name, grid=(nblk,),
        in_specs=[pl.BlockSpec((None, 16, tq), lambda n: (n, 0, 0)), pl.BlockSpec((16, 128), lambda n: (0, 0))],
        out_specs=pl.BlockSpec((tq, 128), lambda n: (n, 0)), out_shape=_sds((nblk * tq, 128), F32),
        compiler_params=_cp(1),
    )(rows, onehot)


def _layer_norm(z, g, b):
    mu = jnp.mean(z, axis=-1, keepdims=True)
    zc = z - mu
    rstd = lax.rsqrt(jnp.mean(zc * zc, axis=-1, keepdims=True) + EPS)
    xhat = zc * rstd
    return xhat * g + b, xhat, rstd


def linear_resid_ln(a, w, x_in, gate, ln_g, ln_b, name):
    t, kdim = a.shape
    d = w.shape[1]
    tm = TOKEN_TILE
    tps = (t // gate.shape[0]) // tm

    def body(a_ref, w_ref, x_ref, gt_ref, g_ref, b_ref, y_ref, xo_ref):
        y = _dot(a_ref[...], w_ref[...])
        y_ref[...] = y
        z = ALPHA * x_ref[...] + (1.0 + gt_ref[...]) * y
        xo_ref[...] = _layer_norm(z, g_ref[...], b_ref[...])[0]

    rows = pl.BlockSpec((tm, d), lambda i: (i, 0))
    vec = pl.BlockSpec((1, d), lambda i: (0, 0))
    return pl.pallas_call(
        body, name=name, grid=(t // tm,),
        in_specs=[pl.BlockSpec((tm, kdim), lambda i: (i, 0)), pl.BlockSpec((kdim, d), lambda i: (0, 0)), rows,
                  pl.BlockSpec((None, 1, d), lambda i: (i // tps, 0, 0)), vec, vec],
        out_specs=[rows, rows], out_shape=[_sds((t, d), F32), _sds((t, d), F32)],
        compiler_params=_cp(1),
    )(a, w, x_in, gate, ln_g, ln_b)


def _resident(a):
    return pl.BlockSpec(a.shape, lambda *_: (0,) * a.ndim, pipeline_mode=pl.Buffered(1))


def ffn_fwd(x_in, shift, scale, gate, wg, wu, wd, ln_g, ln_b, name, hosted=None):
    t, d = x_in.shape
    c, _, fc = wg.shape
    tm = TOKEN_TILE
    tps = (t // gate.shape[0]) // tm

    def body(x_ref, sh_ref, sc_ref, gt_ref, wg_ref, wu_ref, wd_ref, g_ref, b_ref,
             u_ref, hg_ref, hu_ref, y_ref, xo_ref, acc_ref):
        cc = pl.program_id(1)

        @pl.when(cc == 0)
        def _():
            u_ref[...] = (x_ref[...] * (1.0 + sc_ref[...]) + sh_ref[...]).astype(BF16)
            acc_ref[...] = jnp.zeros_like(acc_ref)

        u = u_ref[...]
        hg = _dot(u, wg_ref[cc])
        hu = _dot(u, wu_ref[cc])
        hg_ref[...] = hg.astype(BF16)
        hu_ref[...] = hu.astype(BF16)
        act = (hg * jax.nn.sigmoid(hg) * hu).astype(BF16)
        acc_ref[...] += _dot(act, wd_ref[cc])

        @pl.when(cc == c - 1)
        def _():
            y = acc_ref[...]
            y_ref[...] = y
            z = ALPHA * x_ref[...] + (1.0 + gt_ref[...]) * y
            xo_ref[...] = _layer_norm(z, g_ref[...], b_ref[...])[0]

    rows = pl.BlockSpec((tm, d), lambda i, cc: (i, 0))
    bvec = pl.BlockSpec((None, 1, d), lambda i, cc: (i // tps, 0, 0))
    vec = pl.BlockSpec((1, d), lambda i, cc: (0, 0))
    hspec = pl.BlockSpec((None, tm, fc), lambda i, cc: (cc, i, 0))
    wcol = _resident(wg)
    return _call(
        body, name, (t // tm, c),
        [rows, bvec, bvec, bvec, wcol, wcol, _resident(wd), vec, vec],
        [rows, hspec, hspec, rows, rows],
        [_sds((t, d), BF16), _sds((c, t, fc), BF16), _sds((c, t, fc), BF16), _sds((t, d), F32), _sds((t, d), F32)],
        (x_in, shift, scale, gate, wg, wu, wd, ln_g, ln_b), scratch_shapes=[pltpu.VMEM((tm, d), F32)], hosted=hosted)


def fox_gate_fwd(hf, b_f, tri, n_batch, name):
    t, n = hf.shape
    blk = tri.shape[0]
    nb = (t // n_batch) // blk

    def body(hf_ref, b_ref, tri_ref, o_ref, carry_ref):
        @pl.when(pl.program_id(1) == 0)
        def _():
            carry_ref[...] = jnp.zeros_like(carry_ref)

        xx = hf_ref[...] + b_ref[...]
        lf = jnp.minimum(xx, 0.0) - jnp.log(1.0 + jnp.exp(-jnp.abs(xx)))
        cum = _dot_f32(tri_ref[...], lf) + carry_ref[...]
        o_ref[...] = cum
        carry_ref[...] = cum[blk - 1:blk, :]

    return pl.pallas_call(
        body, name=name, grid=(n_batch, nb),
        in_specs=[pl.BlockSpec((blk, n), lambda bb, i: (bb * nb + i, 0)), pl.BlockSpec((1, n), lambda bb, i: (0, 0)),
                  pl.BlockSpec((blk, blk), lambda bb, i: (0, 0))],
        out_specs=pl.BlockSpec((blk, n), lambda bb, i: (bb * nb + i, 0)),
        out_shape=_sds((t, n), F32), scratch_shapes=[pltpu.VMEM((1, n), F32)],
        compiler_params=_cp(2),
    )(hf, b_f, tri)


def loss_grad(x_out, target, name):
    t, d = x_out.shape
    tm = TOKEN_TILE

    def body(x_ref, t_ref, g_ref, l_ref):
        @pl.when(pl.program_id(0) == 0)
        def _():
            l_ref[...] = jnp.zeros_like(l_ref)

        err = x_ref[...] - t_ref[...]
        g_ref[...] = err / d
        l_ref[...] += jnp.sum(err * err, axis=0, keepdims=True)

    rows = pl.BlockSpec((tm, d), lambda i: (i, 0))
    return pl.pallas_call(
        body, name=name, grid=(t // tm,), in_specs=[rows, rows],
        out_specs=[rows, pl.BlockSpec((1, d), lambda i: (0, 0))],
        out_shape=[_sds((t, d), F32), _sds((1, d), F32)], compiler_params=_cp(1),
    )(x_out, target)


def ln_bwd(dxo, x_in, y, gate, ln_g, name):
    t, d = dxo.shape
    nb = gate.shape[0]
    tm = TOKEN_TILE
    tps = (t // nb) // tm

    def body(dxo_ref, x_ref, y_ref, gt_ref, g_ref, dz_ref, dy_ref, dg_ref, db_ref, dgt_ref):
        i = pl.program_id(0)

        @pl.when(i == 0)
        def _():
            dg_ref[...] = jnp.zeros_like(dg_ref)
            db_ref[...] = jnp.zeros_like(db_ref)

        @pl.when(i % tps == 0)
        def _():
            dgt_ref[...] = jnp.zeros_like(dgt_ref)

        yy = y_ref[...]
        g1 = 1.0 + gt_ref[...]
        z = ALPHA * x_ref[...] + g1 * yy
        _, xhat, rstd = _layer_norm(z, 1.0, 0.0)
        dxo_v = dxo_ref[...]
        dg_ref[...] += jnp.sum(dxo_v * xhat, axis=0, keepdims=True)
        db_ref[...] += jnp.sum(dxo_v, axis=0, keepdims=True)
        dxh = dxo_v * g_ref[...]
        dz = rstd * (dxh - jnp.mean(dxh, axis=-1, keepdims=True) - xhat * jnp.mean(dxh * xhat, axis=-1, keepdims=True))
        dz_ref[...] = dz
        dy_ref[...] = (g1 * dz).astype(BF16)
        dgt_ref[...] += jnp.sum(dz * yy, axis=0, keepdims=True)

    rows = pl.BlockSpec((tm, d), lambda i: (i, 0))
    vec = pl.BlockSpec((1, d), lambda i: (0, 0))
    bvec = pl.BlockSpec((None, 1, d), lambda i: (i // tps, 0, 0))
    return pl.pallas_call(
        body, name=name, grid=(t // tm,), in_specs=[rows, rows, rows, bvec, vec],
        out_specs=[rows, rows, vec, vec, bvec],
        out_shape=[_sds((t, d), F32), _sds((t, d), BF16), _sds((1, d), F32), _sds((1, d), F32), _sds((nb, 1, d), F32)],
        compiler_params=_cp(1),
    )(dxo, x_in, y, gate, ln_g)


def _mod_bwd_tail(du, dz_ref, x_ref, sc_ref, dx_ref, dsc_ref, dsh_ref, first):
    @pl.when(first)
    def _():
        dsc_ref[...] = jnp.zeros_like(dsc_ref)
        dsh_ref[...] = jnp.zeros_like(dsh_ref)

    dx_ref[...] = ALPHA * dz_ref[...] + du * (1.0 + sc_ref[...])
    dsc_ref[...] += jnp.sum(du * x_ref[...], axis=0, keepdims=True)
    dsh_ref[...] += jnp.sum(du, axis=0, keepdims=True)


def ffn_bwd(dy, hg, hu, wg, wu, wd, dz, x_in, scale, name, hosted=None):
    t, d = dy.shape
    c, _, fc = wg.shape
    nb = scale.shape[0]
    tm = TOKEN_TILE
    tps = (t // nb) // tm

    def body(dy_ref, hg_ref, hu_ref, wg_ref, wu_ref, wd_ref, dz_ref, x_ref, sc_ref,
             dhg_ref, dhu_ref, act_ref, dx_ref, dsc_ref, dsh_ref, acc_ref):
        i = pl.program_id(0)
        cc = pl.program_id(1)

        @pl.when(cc == 0)
        def _():
            acc_ref[...] = jnp.zeros_like(acc_ref)

        hgv = hg_ref[...].astype(F32)
        huv = hu_ref[...].astype(F32)
        da = _dot_nt(dy_ref[...], wd_ref[cc])
        sg = jax.nn.sigmoid(hgv)
        sl = hgv * sg
        act_ref[...] = (sl * huv).astype(BF16)
        dhu = (da * sl).astype(BF16)
        dhg = (da * huv * (sg * (1.0 + hgv * (1.0 - sg)))).astype(BF16)
        dhu_ref[...] = dhu
        dhg_ref[...] = dhg
        acc_ref[...] += _dot_nt(dhg, wg_ref[cc]) + _dot_nt(dhu, wu_ref[cc])

        @pl.when(cc == c - 1)
        def _():
            _mod_bwd_tail(acc_ref[...], dz_ref, x_ref, sc_ref, dx_ref, dsc_ref, dsh_ref, i % tps == 0)

    rows = pl.BlockSpec((tm, d), lambda i, cc: (i, 0))
    bvec = pl.BlockSpec((None, 1, d), lambda i, cc: (i // tps, 0, 0))
    hspec = pl.BlockSpec((None, tm, fc), lambda i, cc: (cc, i, 0))
    wcol = _resident(wg)
    return _call(
        body, name, (t // tm, c),
        [rows, hspec, hspec, wcol, wcol, _resident(wd), rows, rows, bvec],
        [hspec, hspec, hspec, rows, bvec, bvec],
        [_sds((c, t, fc), BF16), _sds((c, t, fc), BF16), _sds((c, t, fc), BF16), _sds((t, d), F32),
         _sds((nb, 1, d), F32), _sds((nb, 1, d), F32)],
        (dy, hg, hu, wg, wu, wd, dz, x_in, scale), scratch_shapes=[pltpu.VMEM((tm, d), F32)], hosted=hosted)


def linear_nt_mod_bwd(pairs, dz, x_in, scale, name, hosted=None):
    t, d = dz.shape
    nb = scale.shape[0]
    tm = TOKEN_TILE
    tps = (t // nb) // tm
    npairs = len(pairs)

    def body(*refs):
        dh_refs = refs[:npairs]
        w_refs = refs[npairs:2 * npairs]
        dz_ref, x_ref, sc_ref, dx_ref, dsc_ref, dsh_ref = refs[2 * npairs:]
        du = None
        for (_, _, blk), dh_ref, w_ref in zip(pairs, dh_refs, w_refs):
            dh = dh_ref[...].astype(BF16)
            term = _dot_nt(dh, w_ref[...]) if blk is None else _dot(dh, w_ref[...])
            du = term if du is None else du + term
        _mod_bwd_tail(du, dz_ref, x_ref, sc_ref, dx_ref, dsc_ref, dsh_ref, pl.program_id(0) % tps == 0)

    rows = pl.BlockSpec((tm, d), lambda i: (i, 0))
    bvec = pl.BlockSpec((None, 1, d), lambda i: (i // tps, 0, 0))
    in_specs = [pl.BlockSpec((tm, dh.shape[1]), lambda i: (i, 0)) for dh, _, _ in pairs]
    for dh, w, blk in pairs:
        if blk is None:
            in_specs.append(pl.BlockSpec(w.shape, lambda i: (0, 0)))
        else:
            in_specs.append(pl.BlockSpec((dh.shape[1], d), lambda i, blk=blk: (blk, 0)))
    in_specs += [rows, rows, bvec]
    return _call(
        body, name, (t // tm,), in_specs, [rows, bvec, bvec],
        [_sds((t, d), F32), _sds((nb, 1, d), F32), _sds((nb, 1, d), F32)],
        (*[dh for dh, _, _ in pairs], *[w for _, w, _ in pairs], dz, x_in, scale), hosted=hosted)


def linear_nt_delta(dy, w_o, o, head_sel, name):
    t, d = dy.shape
    hdv = w_o.shape[0]
    tm = TOKEN_TILE

    def body(dy_ref, w_ref, o_ref, sel_ref, do_ref, dl_ref):
        do = _dot_nt(dy_ref[...], w_ref[...])
        do_ref[...] = do.astype(BF16)
        dl_ref[...] = _dot_f32(do * o_ref[...].astype(F32), sel_ref[...])

    return pl.pallas_call(
        body, name=name, grid=(t // tm,),
        in_specs=[pl.BlockSpec((tm, d), lambda i: (i, 0)), pl.BlockSpec((hdv, d), lambda i: (0, 0)),
                  pl.BlockSpec((tm, hdv), lambda i: (i, 0)), pl.BlockSpec(head_sel.shape, lambda i: (0, 0))],
        out_specs=[pl.BlockSpec((tm, hdv), lambda i: (i, 0)), pl.BlockSpec((tm, 128), lambda i: (i, 0))],
        out_shape=[_sds((t, hdv), BF16), _sds((t, 128), F32)], compiler_params=_cp(1),
    )(dy, w_o, o, head_sel)


def _attn_bwd_blocks(j, nk, tk, scale, heads):
    def block(i, carry, masked):
        new = []
        for hd, (dk_acc, dv_acc, dfk_acc) in zip(heads, carry):
            qb = hd["q"](i)
            dob = hd["do"](i)
            lse_row, dl_row = hd["rows"](i)
            st = _dot_nt(hd["k"], qb) * scale
            if hd["bias"] is not None:
                fq_row, fk_col = hd["bias"](i)
                st = st + fq_row - fk_col
            if masked:
                keep = lax.broadcasted_iota(jnp.int32, st.shape, 1) >= lax.broadcasted_iota(jnp.int32, st.shape, 0)
                st = jnp.where(keep, st, -1e30)
            pt = jnp.exp(st - lse_row)
            dv_acc = dv_acc + _dot(pt.astype(BF16), dob)
            dst = pt * (_dot_nt(hd["v"], dob) - dl_row)
            if hd["add_dfq"] is not None:
                dfk_acc = dfk_acc - jnp.sum(dst, axis=1, keepdims=True)
                hd["add_dfq"](i, jnp.sum(dst, axis=0, keepdims=True))
            dsb = (dst * scale).astype(BF16)
            dk_acc = dk_acc + _dot(dsb, qb)
            hd["add_dq"](i, _dot_tn(dsb, hd["k"]))
            new.append((dk_acc, dv_acc, dfk_acc))
        return tuple(new)

    init = tuple((jnp.zeros((tk, hd["k"].shape[1]), F32), jnp.zeros((tk, hd["v"].shape[1]), F32), jnp.zeros((tk, 1), F32))
                 for hd in heads)
    carry = block(j, init, True)
    return lax.fori_loop(j + 1, nk, lambda i, c: block(i, c, False), carry)


def fox_attn_bwd(qkv, do, cum, cum_rows, lse_rows, delta_rows, nb, name, hosted=None):
    t = qkv.shape[0]
    s = t // nb
    tk = ATTN_TILE
    nk = s // tk
    npairs = FOX_HEADS // 2
    scale = FOX_HD ** -0.5

    def body(q_ref, k_ref, v_ref, do_ref, cum_ref, cr_ref, lr_ref, dr_ref, dq_ref, dk_ref, dv_ref, dfq_ref, dfk_ref):
        hp = pl.program_id(1)
        j = pl.program_id(2)

        @pl.when(j == 0)
        def _():
            dq_ref[...] = jnp.zeros_like(dq_ref)

        @pl.when((j == 0) & (hp == 0))
        def _():
            dfq_ref[...] = jnp.zeros_like(dfq_ref)
            dfk_ref[...] = jnp.zeros_like(dfk_ref)

        kb = k_ref[...]
        vb = v_ref[...]
        low = lax.broadcasted_iota(jnp.int32, kb.shape, 1) < FOX_HD
        cum_t = cum_ref[...]

        def rows_of(i):
            return pl.ds(pl.multiple_of(i * tk, tk), tk)

        def add_dq(i, val):
            dq_ref[rows_of(i), :] += val

        def head(a):
            hd = 2 * hp + a
            half = low if a == 0 else jnp.logical_not(low)
            fk = _pick_lane(cum_t, hd)

            def add_dfq(i, val):
                dfq_ref[i] = _put_row(dfq_ref[i], hd, val)

            return dict(q=lambda i: q_ref[rows_of(i), :], do=lambda i: do_ref[rows_of(i), :],
                        k=jnp.where(half, kb, jnp.zeros_like(kb)), v=jnp.where(half, vb, jnp.zeros_like(vb)),
                        rows=lambda i: (_pick_row(lr_ref[i], hd), _pick_row(dr_ref[i], hd)),
                        bias=lambda i: (_pick_row(cr_ref[i], hd), fk), add_dq=add_dq, add_dfq=add_dfq)

        (dk_0, dv_0, dfk_0), (dk_1, dv_1, dfk_1) = _attn_bwd_blocks(j, nk, tk, scale, [head(0), head(1)])
        dk_ref[...] = jnp.where(low, dk_0, dk_1).astype(BF16)
        dv_ref[...] = jnp.where(low, dv_0, dv_1).astype(BF16)
        for a, dfk_a in ((0, dfk_0), (1, dfk_1)):
            dfk_ref[j] = _put_row(dfk_ref[j], 2 * hp + a, jnp.broadcast_to(dfk_a, (tk, 128)).T[0:1, :])

    rowsp = pl.BlockSpec((nk, 16, tk), lambda b, hp, j: (b, 0, 0))
    return _call(
        body, name, (nb, npairs, nk),
        [pl.BlockSpec((s, 128), lambda b, hp, j: (b, hp)),
         pl.BlockSpec((tk, 128), lambda b, hp, j: (b * nk + j, npairs + hp)),
         pl.BlockSpec((tk, 128), lambda b, hp, j: (b * nk + j, 2 * npairs + hp)),
         pl.BlockSpec((s, 128), lambda b, hp, j: (b, hp)),
         pl.BlockSpec((tk, 128), lambda b, hp, j: (b * nk + j, 0)),
         rowsp, rowsp, rowsp],
        [pl.BlockSpec((s, 128), lambda b, hp, j: (b, hp)),
         pl.BlockSpec((tk, 128), lambda b, hp, j: (b * nk + j, hp)),
         pl.BlockSpec((tk, 128), lambda b, hp, j: (b * nk + j, hp)),
         rowsp, rowsp],
        [_sds((t, D_MODEL), F32), _sds((t, D_MODEL), BF16), _sds((t, D_MODEL), BF16),
         _sds((t // tk, 16, tk), F32), _sds((t // tk, 16, tk), F32)],
        (qkv, qkv, qkv, do, cum, cum_rows, lse_rows, delta_rows), hosted=hosted)


def mla_attn_bwd(q, kn, kr2, v, do, lse_rows, delta_rows, nb, name, hosted=None):
    t = q.shape[0]
    s = t // nb
    tk = ATTN_TILE
    nk = s // tk
    npairs = MLA_HEADS // 2
    scale = (MLA_NOPE + MLA_ROPE) ** -0.5

    def body(qn_ref, qr_ref, kn_ref, kr_ref, v_ref, do_ref, lr_ref, dr_ref, dqn_ref, dqr_ref, dkn_ref, dkr_ref, dv_ref):
        hp = pl.program_id(1)
        j = pl.program_id(2)

        @pl.when(j == 0)
        def _():
            dqn_ref[...] = jnp.zeros_like(dqn_ref)
            dqr_ref[...] = jnp.zeros_like(dqr_ref)

        low = lax.broadcasted_iota(jnp.int32, (tk, 128), 1) < MLA_ROPE
        kr = kr_ref[...]

        def rows_of(i):
            return pl.ds(pl.multiple_of(i * tk, tk), tk)

        def head(a):
            cols = slice(a * MLA_NOPE, (a + 1) * MLA_NOPE)
            mine = low if a == 0 else jnp.logical_not(low)

            def q_fn(i):
                qr = qr_ref[rows_of(i), :]
                return jnp.concatenate([qn_ref[rows_of(i), cols], jnp.where(mine, qr, jnp.zeros_like(qr))], axis=1)

            def add_dq(i, val):
                dqn_ref[rows_of(i), cols] += val[:, :MLA_NOPE]
                dqr_ref[rows_of(i), cols] += val[:, MLA_NOPE:]

            return dict(q=q_fn, do=lambda i: do_ref[rows_of(i), cols], k=jnp.concatenate([kn_ref[:, cols], kr], axis=1),
                        v=v_ref[:, cols], rows=lambda i: (_pick_row(lr_ref[i], 2 * hp + a), _pick_row(dr_ref[i], 2 * hp + a)),
                        bias=None, add_dq=add_dq, add_dfq=None)

        (dk_0, dv_0, _), (dk_1, dv_1, _) = _attn_bwd_blocks(j, nk, tk, scale, [head(0), head(1)])
        dkn_ref[...] = jnp.concatenate([dk_0[:, :MLA_NOPE], dk_1[:, :MLA_NOPE]], axis=1).astype(BF16)
        dkr_ref[...] = jnp.concatenate([dk_0[:, MLA_NOPE:], dk_1[:, MLA_NOPE:]], axis=1).astype(BF16)
        dv_ref[...] = jnp.concatenate([dv_0, dv_1], axis=1).astype(BF16)

    wide = 2 * MLA_NOPE
    full = pl.BlockSpec((s, wide), lambda b, hp, j: (b, hp))
    blk = pl.BlockSpec((tk, wide), lambda b, hp, j: (b * nk + j, hp))
    rowsp = pl.BlockSpec((nk, 16, tk), lambda b, hp, j: (b, 0, 0))
    total = MLA_HEADS * MLA_V
    return _call(
        body, name, (nb, npairs, nk),
        [full, pl.BlockSpec((s, 128), lambda b, hp, j: (b, MLA_HEADS + hp)), blk,
         pl.BlockSpec((tk, 128), lambda b, hp, j: (b * nk + j, 0)), blk, full, rowsp, rowsp],
        [full, full, blk, blk, blk],
        [_sds((t, total), F32), _sds((t, total), F32), _sds((t, total), BF16), _sds((t, total), BF16),
         _sds((t, total), BF16)],
        (q, q, kn, kr2, v, do, lse_rows, delta_rows), hosted=hosted)


def mla_mid_bwd(dqn, dqr, dkn, dv, dkr_heads, h, g_q, g_kv, w_uq, w_uk, w_uv, cos8, sin8, cos64, sin64s, swap64,
                heads_to_rope, head_sum, name):
    t = h.shape[0]
    tm = TOKEN_TILE
    hq = MLA_HEADS * MLA_NOPE
    hr = MLA_HEADS * MLA_ROPE // 2
    nq = w_uq.shape[1]

    def body(dqn_ref, dqr_ref, dkn_ref, dv_ref, dkr_ref, h_ref, gq_ref, gkv_ref, wuq_ref, wuk_ref, wuv_ref,
             c8_ref, s8_ref, c64_ref, s64_ref, sw_ref, hp_ref, hs_ref, dh_ref, dqp_ref, dgq_ref, dgkv_ref):
        @pl.when(pl.program_id(0) == 0)
        def _():
            dgq_ref[...] = jnp.zeros_like(dgq_ref)
            dgkv_ref[...] = jnp.zeros_like(dgkv_ref)

        drot = _dot(dqr_ref[...].astype(BF16), hp_ref[...])
        o1 = drot[:, :hr]
        o2 = drot[:, hr:]
        cs = c8_ref[...]
        sn = s8_ref[...]
        dqp = jnp.concatenate([dqn_ref[...].astype(BF16), (o1 * cs + o2 * sn).astype(BF16),
                               (o2 * cs - o1 * sn).astype(BF16)], axis=1)
        dqp_ref[...] = dqp
        dcq = _dot_nt(dqp, wuq_ref[...])
        dckv = _dot_nt(dkn_ref[...], wuk_ref[...]) + _dot_nt(dv_ref[...], wuv_ref[...])
        hh = h_ref[...]

        def rms_bwd(hpart, g, dc, dg_ref):
            hhat, rstd = _rms(hpart, None)
            dg_ref[...] += jnp.sum(dc * hhat, axis=0, keepdims=True)
            dcg = dc * g
            return rstd * (dcg - hhat * jnp.mean(dcg * hhat, axis=-1, keepdims=True))

        dhq = rms_bwd(hh[:, :MLA_QR], gq_ref[...], dcq, dgq_ref)
        dhkv = rms_bwd(hh[:, MLA_QR:MLA_QR + MLA_KVR], gkv_ref[...], dckv, dgkv_ref)
        dkr = _dot(dkr_ref[...], hs_ref[...])
        dkr_pre = dkr * c64_ref[...] + _dot_f32(dkr * s64_ref[...], sw_ref[...])
        dh_ref[...] = jnp.concatenate([dhq, dhkv, dkr_pre], axis=1).astype(BF16)

    def rows(n):
        return pl.BlockSpec((tm, n), lambda i: (i, 0))

    def whole(a):
        return pl.BlockSpec(a.shape, lambda i: (0,) * a.ndim)

    return pl.pallas_call(
        body, name=name, grid=(t // tm,),
        in_specs=[rows(hq), rows(hq), rows(hq), rows(hq), rows(hq), rows(h.shape[1]), whole(g_q), whole(g_kv),
                  whole(w_uq), whole(w_uk), whole(w_uv), rows(hr), rows(hr), rows(MLA_ROPE), rows(MLA_ROPE),
                  whole(swap64), whole(heads_to_rope), whole(head_sum)],
        out_specs=[rows(h.shape[1]), rows(nq), pl.BlockSpec((1, MLA_QR), lambda i: (0, 0)),
                   pl.BlockSpec((1, MLA_KVR), lambda i: (0, 0))],
        out_shape=[_sds((t, h.shape[1]), BF16), _sds((t, nq), BF16), _sds((1, MLA_QR), F32), _sds((1, MLA_KVR), F32)],
        compiler_params=_cp(1),
    )(dqn, dqr, dkn, dv, dkr_heads, h, g_q, g_kv, w_uq, w_uk, w_uv, cos8, sin8, cos64, sin64s, swap64,
      heads_to_rope, head_sum)


def fox_gate_bwd(dcum, hf, b_f, triu, n_batch, name):
    t, n = hf.shape
    blk = triu.shape[0]
    nb = (t // n_batch) // blk

    def body(dc_ref, hf_ref, b_ref, tri_ref, o_ref, db_ref, carry_ref):
        @pl.when(pl.program_id(1) == 0)
        def _():
            carry_ref[...] = jnp.zeros_like(carry_ref)

        @pl.when((pl.program_id(0) == 0) & (pl.program_id(1) == 0))
        def _():
            db_ref[...] = jnp.zeros_like(db_ref)

        rc = _dot_f32(tri_ref[...], dc_ref[...]) + carry_ref[...]
        carry_ref[...] = rc[0:1, :]
        dhf = rc * jax.nn.sigmoid(-(hf_ref[...] + b_ref[...]))
        o_ref[...] = dhf.astype(BF16)
        db_ref[...] += jnp.sum(dhf, axis=0, keepdims=True)

    rev = pl.BlockSpec((blk, n), lambda bb, i: (bb * nb + nb - 1 - i, 0))
    return pl.pallas_call(
        body, name=name, grid=(n_batch, nb),
        in_specs=[rev, rev, pl.BlockSpec((1, n), lambda bb, i: (0, 0)), pl.BlockSpec((blk, blk), lambda bb, i: (0, 0))],
        out_specs=[rev, pl.BlockSpec((1, n), lambda bb, i: (0, 0))],
        out_shape=[_sds((t, n), BF16), _sds((1, n), F32)], scratch_shapes=[pltpu.VMEM((1, n), F32)],
        compiler_params=_cp(2),
    )(dcum, hf, b_f, triu)


def wgrad(a, bm, name, with_bf16=False, bt=WGRAD_TOKENS):
    ca, t, kd = a.shape
    cb, _, nd = bm.shape
    c = max(ca, cb)
    bn = nd
    if nd > 1024 and nd % 1024 == 0:
        bn = 1024
    nsteps = t // bt

    def body(a_ref, b_ref, o_ref, *rest):
        @pl.when(pl.program_id(2) == 0)
        def _():
            o_ref[...] = jnp.zeros_like(o_ref)

        o_ref[...] += _dot_tn(a_ref[...].astype(BF16), b_ref[...].astype(BF16))
        if with_bf16:
            @pl.when(pl.program_id(2) == nsteps - 1)
            def _():
                rest[0][...] = o_ref[...].astype(BF16)

    out_spec = pl.BlockSpec((None, kd, bn), lambda cc, n, tt: (cc, 0, n))
    res = pl.pallas_call(
        body, name=name, grid=(c, nd // bn, nsteps),
        in_specs=[pl.BlockSpec((None, bt, kd), lambda cc, n, tt: (cc if ca > 1 else 0, tt, 0)),
                  pl.BlockSpec((None, bt, bn), lambda cc, n, tt: (cc if cb > 1 else 0, tt, n))],
        out_specs=[out_spec, out_spec] if with_bf16 else out_spec,
        out_shape=[_sds((c, kd, nd), F32), _sds((c, kd, nd), BF16)] if with_bf16 else _sds((c, kd, nd), F32),
        compiler_params=_cp(3),
    )(a, bm)
    return res


def ada_mod_part(c_all, ada_w, name):
    nl, d, n = ada_w.shape
    rows = c_all.shape[0]
    tn = 512

    def body(c_ref, w_ref, o_ref):
        cv = c_ref[...]
        act = (cv * jax.nn.sigmoid(cv)).astype(BF16)
        o_ref[...] = _dot(act, w_ref[...].astype(BF16))

    return pl.pallas_call(
        body, name=name, grid=(nl, n // tn),
        in_specs=[pl.BlockSpec((rows, d), lambda l, j: (0, 0)), pl.BlockSpec((None, d, tn), lambda l, j: (l, 0, j))],
        out_specs=pl.BlockSpec((None, rows, tn), lambda l, j: (l, 0, j)),
        out_shape=_sds((nl, rows, n), F32), compiler_params=_cp(2),
    )(c_all, ada_w)


def ada_grad(c_all_t, dmod, name):
    nl, rows, n = dmod.shape
    d = c_all_t.shape[0]
    tn = 512

    def body(c_ref, dm_ref, o_ref):
        cv = c_ref[...]
        act = (cv * jax.nn.sigmoid(cv)).astype(BF16)
        o_ref[...] = _dot(act, dm_ref[...].astype(BF16))

    return pl.pallas_call(
        body, name=name, grid=(nl, n // tn),
        in_specs=[pl.BlockSpec((d, rows), lambda l, j: (0, 0)), pl.BlockSpec((None, rows, tn), lambda l, j: (l, 0, j))],
        out_specs=pl.BlockSpec((None, d, tn), lambda l, j: (l, 0, j)),
        out_shape=_sds((nl, d, n), F32), compiler_params=_cp(2),
    )(c_all_t, dmod)


def sum_leading(a, name):
    g, r, n = a.shape

    def body(a_ref, o_ref):
        acc = a_ref[0]
        for kk in range(1, g):
            acc = acc + a_ref[kk]
        o_ref[...] = acc

    return pl.pallas_call(
        body, name=name, grid=(1,), in_specs=[pl.BlockSpec((g, r, n), lambda i: (0, 0, 0))],
        out_specs=pl.BlockSpec((r, n), lambda i: (0, 0)), out_shape=_sds((r, n), F32), compiler_params=_cp(1),
    )(a)


def adamw(w, g, m, v, name):
    r, n = w.shape
    br = r
    for cand in (512, 256, 128, 64, 32, 16, 8):
        if r % cand == 0 and r > cand and cand * n * 4 <= ADAMW_BLOCK_BYTES:
            br = cand
            break
    c1 = 1.0 - ADAM_B1 ** ADAM_STEP
    c2 = 1.0 - ADAM_B2 ** ADAM_STEP

    def body(w_ref, g_ref, m_ref, v_ref, d_ref, mo_ref, vo_ref):
        gv = g_ref[...]
        mn = ADAM_B1 * m_ref[...] + (1.0 - ADAM_B1) * gv
        vn = ADAM_B2 * v_ref[...] + (1.0 - ADAM_B2) * (gv * gv)
        mo_ref[...] = mn
        vo_ref[...] = vn
        d_ref[...] = -ADAM_LR * ((mn / c1) / (jnp.sqrt(vn / c2) + ADAM_EPS) + ADAM_WD * w_ref[...])

    spec = pl.BlockSpec((br, n), lambda i: (i, 0))
    return pl.pallas_call(
        body, name=name, grid=(r // br,), in_specs=[spec] * 4, out_specs=[spec] * 3,
        out_shape=[_sds((r, n), F32)] * 3, compiler_params=_cp(1),
    )(w, g, m, v)


def all_gather8(x_blk, name):
    m_per, n = x_blk.shape

    def body(x_ref, out_ref, send_sems, recv_sems, local_sem):
        x, y, c = _place()
        me, sibling = (x, y, c), (x, y, 1 - c)
        chips = [(1 - x, y), (x, 1 - y), (1 - x, 1 - y)]

        def rows(px, py, pc):
            return out_ref.at[pl.ds((4 * px + 2 * py + pc) * m_per, m_per), :]

        def copy(k, block, to, src=None):
            return pltpu.make_async_remote_copy(
                src_ref=rows(*block) if src is None else src, dst_ref=rows(*block),
                send_sem=send_sems.at[k], recv_sem=recv_sems.at[k], device_id=to, device_id_type=MESH)

        mine = pltpu.make_async_copy(x_ref, rows(*me), local_sem)
        mine.start()
        first = [copy(0, me, sibling, src=x_ref)]
        first += [copy(1 + j, me, (*chip, c), src=x_ref) for j, chip in enumerate(chips)]
        for cp in first:
            cp.start()
        passed = [copy(4 + j, (*chip, c), sibling) for j, chip in enumerate(chips)]
        for j, chip in enumerate(chips):
            copy(1 + j, (*chip, c), me).wait_recv()
            passed[j].start()
        copy(0, sibling, me).wait_recv()
        for j, chip in enumerate(chips):
            copy(4 + j, (*chip, 1 - c), me).wait_recv()
        for cp in first + passed:
            cp.wait_send()
        mine.wait()

    return pl.pallas_call(
        body, name=name, out_shape=_sds((8 * m_per, n), x_blk.dtype),
        in_specs=[pl.BlockSpec(memory_space=pltpu.VMEM)], out_specs=pl.BlockSpec(memory_space=pltpu.VMEM),
        scratch_shapes=[pltpu.SemaphoreType.DMA((7,)), pltpu.SemaphoreType.DMA((7,)), pltpu.SemaphoreType.DMA],
        compiler_params=pltpu.CompilerParams(vmem_limit_bytes=VMEM_LIMIT),
    )(x_blk)


def _gather_comm(shards):
    nt = len(shards)

    def parts(w_refs, out_refs, sems, finishing):
        send_sems, recv_sems, own_send, own_recv = sems
        x, y, c = _place()
        sibling = (x, y, 1 - c)
        chips = [(1 - x, y), (x, 1 - y), (1 - x, 1 - y)]

        def copy(t, k, block, to, src=None):
            px, py, hh = block
            dst = out_refs[t].at[2 * px + py, hh]
            return pltpu.make_async_remote_copy(
                src_ref=dst if src is None else src, dst_ref=dst,
                send_sem=send_sems.at[6 * t + k], recv_sem=recv_sems.at[6 * t + k], device_id=to, device_id_type=MESH)

        own = [pltpu.make_async_remote_copy(
            src_ref=w_refs[t], dst_ref=out_refs[t].at[2 * x + y], send_sem=own_send.at[t], recv_sem=own_recv.at[t],
            device_id=sibling, device_id_type=MESH) for t in range(nt)]
        first = [copy(t, j, (x, y, c), (*chip, c), src=w_refs[t].at[c]) for t in range(nt) for j, chip in enumerate(chips)]
        if not finishing:
            return own, first
        landed = [copy(t, j, (*chip, c), (x, y, c)) for t in range(nt) for j, chip in enumerate(chips)]
        passed = [copy(t, 3 + j, (*chip, c), sibling) for t in range(nt) for j, chip in enumerate(chips)]
        from_sibling = [copy(t, 3 + j, (*chip, 1 - c), (x, y, c)) for t in range(nt) for j, chip in enumerate(chips)]
        return own, first, landed, passed, from_sibling

    def start(w_refs, out_refs, sems):
        own, first = parts(w_refs, out_refs, sems, False)
        for cp in own + first:
            cp.start()

    def finish(w_refs, out_refs, sems):
        own, first, landed, passed, from_sibling = parts(w_refs, out_refs, sems, True)
        for arrived, fwd in zip(landed, passed):
            arrived.wait_recv()
            fwd.start()
        for cp in from_sibling:
            cp.wait_recv()
        for cp in first + passed:
            cp.wait_send()
        for cp in own:
            cp.wait()

    sems = [pltpu.SemaphoreType.DMA((6 * nt,)), pltpu.SemaphoreType.DMA((6 * nt,)),
            pltpu.SemaphoreType.DMA((nt,)), pltpu.SemaphoreType.DMA((nt,))]
    return _Hosted(list(shards), [_sds((N_CHIPS, *w.shape), w.dtype) for w in shards], sems, start, finish)


def all_gather_chips(shards, name):
    comm = _gather_comm(shards)
    nt = len(shards)

    def body(*refs):
        comm.start(refs[:nt], refs[nt:2 * nt], refs[2 * nt:])
        comm.finish(refs[:nt], refs[nt:2 * nt], refs[2 * nt:])

    hbm = pl.BlockSpec(memory_space=pl.ANY)
    return pl.pallas_call(body, name=name, out_shape=comm.out_shape, in_specs=[hbm] * nt, out_specs=[hbm] * nt,
                          scratch_shapes=comm.sems)(*shards)


def _row_block(r, n, itemsize):
    best = None
    for br in range(16, r + 1, 16):
        if r % br == 0 and br * n * itemsize <= COMM_BLOCK_BYTES:
            best = br
    return r if best is None else best


def _scatter_comm(parts):
    nt = len(parts)

    def copies(p_refs, b_refs, sems, arriving):
        send_sems, recv_sems = sems
        x, y, c = _place()
        me = 4 * x + 2 * y + c
        cps = []
        for t in range(nt):
            for r in range(1, 8):
                tx = 1 - x if r & 4 else x
                ty = 1 - y if r & 2 else y
                tc = 1 - c if r & 1 else c
                src, dst = (2 * x + y, c), 4 * tx + 2 * ty + tc
                if not arriving:
                    src, dst = (2 * tx + ty, tc), me
                cps.append(pltpu.make_async_remote_copy(
                    src_ref=p_refs[t].at[src], dst_ref=b_refs[t].at[dst], send_sem=send_sems.at[7 * t + r - 1],
                    recv_sem=recv_sems.at[7 * t + r - 1], device_id=(tx, ty, tc), device_id_type=MESH))
        return cps

    def start(p_refs, b_refs, sems):
        for cp in copies(p_refs, b_refs, sems, False):
            cp.start()

    def finish(p_refs, b_refs, sems):
        for cp in copies(p_refs, b_refs, sems, True):
            cp.wait_recv()
        for cp in copies(p_refs, b_refs, sems, False):
            cp.wait_send()

    sems = [pltpu.SemaphoreType.DMA((7 * nt,)), pltpu.SemaphoreType.DMA((7 * nt,))]
    return _Hosted(list(parts), [_sds((2 * N_CHIPS, *p.shape[2:]), p.dtype) for p in parts], sems, start, finish)


def sum_devices(own, recv, place, name, slot=(0, 1, None)):
    _, _, r, n = own.shape
    layer, n_layers, buf = slot
    br = _row_block(r, n, 4 * 8)

    def body(p_ref, o_ref, *rest):
        acc = o_ref[...]
        for kk in range(7):
            acc = acc + rest[kk][...].astype(F32)
        rest[-1][...] = acc

    def arrived(rel):
        return pl.BlockSpec((None, br, n), lambda i, pref: (jnp.bitwise_xor(pref[0], rel), i, 0))

    in_specs = [pl.BlockSpec((None, None, br, n), lambda i, pref: (pref[2], pref[1], i, 0))]
    in_specs += [arrived(rel) for rel in range(1, 8)]
    args = [own] + [recv] * 7
    aliases = {}
    if buf is not None:
        in_specs.append(pl.BlockSpec(memory_space=pl.ANY))
        args.append(buf)
        aliases = {9: 0}
    return pl.pallas_call(
        body, name=name,
        grid_spec=pltpu.PrefetchScalarGridSpec(
            num_scalar_prefetch=1, grid=(r // br,), in_specs=in_specs,
            out_specs=pl.BlockSpec((None, None, br, n), lambda i, pref: (layer, pref[1], i, 0))),
        out_shape=_sds((n_layers, 2, r, n), F32), input_output_aliases=aliases, compiler_params=_cp(1),
    )(place, *args)


def sibling_join_halves(bufs, name):
    nt = len(bufs)
    layers = [bf.shape[0] for bf in bufs]
    first = [sum(layers[:t]) for t in range(nt)]

    def body(*refs):
        o_refs = refs[nt:2 * nt]
        send_sems, recv_sems = refs[2 * nt:]
        x, y, c = _place()

        def copy(t, l, hh):
            return pltpu.make_async_remote_copy(
                src_ref=o_refs[t].at[l, hh], dst_ref=o_refs[t].at[l, hh], send_sem=send_sems.at[first[t] + l],
                recv_sem=recv_sems.at[first[t] + l], device_id=(x, y, 1 - c), device_id_type=MESH)

        cps = [copy(t, l, c) for t in range(nt) for l in range(layers[t])]
        for cp in cps:
            cp.start()
        for t in range(nt):
            for l in range(layers[t]):
                copy(t, l, 1 - c).wait_recv()
        for cp in cps:
            cp.wait_send()

    hbm = pl.BlockSpec(memory_space=pl.ANY)
    return pl.pallas_call(
        body, name=name, out_shape=[_sds(bf.shape, bf.dtype) for bf in bufs],
        in_specs=[hbm] * nt, out_specs=[hbm] * nt, input_output_aliases={t: t for t in range(nt)},
        scratch_shapes=[pltpu.SemaphoreType.DMA((sum(layers),)), pltpu.SemaphoreType.DMA((sum(layers),))],
    )(*bufs)


_SHARD_KIND = {"mla_w_in": "rows", "mla_w_uq": "cols", "mla_w_uk": "cols", "mla_w_uv": "cols", "mla_w_o": "rows",
               "fox_w_in": "cols", "fox_w_o": "rows", "ffn_w_gate": "chunk", "ffn_w_up": "chunk", "ffn_w_down": "chunk"}
_PACKED = tuple(_SHARD_KIND)
_TRANSPOSED = ("ffn_w_gate", "ffn_w_up", "fox_w_in")


def _halves(shard):
    if shard.ndim == 3 and shard.shape[0] == 2:
        return shard
    r, n = shard.shape[-2:]
    return shard.reshape(2, r // 2, n)


def _cols_to_full(g):
    return jnp.transpose(g, (1, 0, 2)).reshape(g.shape[1], -1)


def _full_to_cols(w):
    k, n4 = w.shape
    return jnp.transpose(w.reshape(k, N_CHIPS, n4 // N_CHIPS), (1, 0, 2))


def _uq_perm():
    per = MLA_NOPE + MLA_ROPE
    half = MLA_ROPE // 2
    nope = [h * per + d for h in range(MLA_HEADS) for d in range(MLA_NOPE)]
    r1 = [h * per + MLA_NOPE + r for h in range(MLA_HEADS) for r in range(half)]
    r2 = [h * per + MLA_NOPE + half + r for h in range(MLA_HEADS) for r in range(half)]
    perm = np.array(nope + r1 + r2, dtype=np.int32)
    return perm, np.argsort(perm).astype(np.int32)


def _rope_matrices():
    half = MLA_ROPE // 2
    nr = MLA_HEADS * MLA_ROPE
    to_heads = np.zeros((nr, nr), np.float32)
    from_heads = np.zeros((MLA_HEADS * 128, nr), np.float32)
    for e in range(2):
        for h in range(MLA_HEADS):
            for r in range(half):
                to_heads[e * MLA_HEADS * half + h * half + r, h * MLA_ROPE + e * half + r] = 1.0
                from_heads[h * 128 + e * half + r, e * MLA_HEADS * half + h * half + r] = 1.0
    head_sum = np.tile(np.eye(MLA_ROPE, dtype=np.float32), (2 * MLA_HEADS, 1))
    dup = np.concatenate([np.eye(MLA_ROPE, dtype=np.float32)] * 2, axis=1)
    return to_heads, from_heads, head_sum, dup


def _ffn_weights(gathered):
    return tuple(g.reshape(N_CHIPS, 2 * g.shape[2], g.shape[3]) for g in gathered)


def _fox_weights(gathered):
    w_in, w_o = gathered
    w_in = jnp.transpose(w_in, (0, 2, 1, 3)).reshape(N_CHIPS * w_in.shape[2], 2 * w_in.shape[3])
    return w_in, w_o.reshape(-1, w_o.shape[-1])


def _local_step(x, positions, target, mods, wts, ln_g, ln_b, mla_g_q, mla_g_kv, fox_b_f, shards=None):
    nb, s, d = x.shape
    t = nb * s
    x0 = x.reshape(t, d)
    tgt = target.reshape(t, d)
    perm, inv_perm = _uq_perm()

    half = MLA_ROPE // 2
    inv_freq = ROPE_THETA ** (-jnp.arange(half, dtype=F32) / half)
    ang = positions.astype(F32).reshape(t, 1) * inv_freq
    cos, sin = jnp.cos(ang), jnp.sin(ang)
    cos8, sin8 = jnp.tile(cos, (1, MLA_HEADS)), jnp.tile(sin, (1, MLA_HEADS))
    cos64 = jnp.concatenate([cos, cos], axis=1)
    sin64s = jnp.concatenate([-sin, sin], axis=1)
    swap64 = jnp.asarray(np.roll(np.eye(MLA_ROPE, dtype=np.float32), half, axis=1))
    to_heads, from_heads, head_sum, dup = _rope_matrices()
    to_heads, from_heads = jnp.asarray(to_heads, dtype=BF16), jnp.asarray(from_heads, dtype=BF16)
    head_sum, dup = jnp.asarray(head_sum, dtype=BF16), jnp.asarray(dup, dtype=BF16)
    sel_mla = jnp.asarray(np.pad(np.kron(np.eye(MLA_HEADS, dtype=np.float32), np.ones((MLA_V, 1), np.float32)),
                                 ((0, 0), (0, 128 - MLA_HEADS))))
    sel_fox = jnp.asarray(np.pad(np.kron(np.eye(FOX_HEADS, dtype=np.float32), np.ones((FOX_HD, 1), np.float32)),
                                 ((0, 0), (0, 128 - FOX_HEADS))))
    tri = jnp.asarray(np.tril(np.ones((128, 128), np.float32)))
    triu = jnp.asarray(np.triu(np.ones((128, 128), np.float32)))
    onehot16 = jnp.asarray(np.eye(16, 128, dtype=np.float32))

    def vec(a):
        return a.reshape(1, -1)

    def carried(key):
        return None if shards is None else _gather_comm(shards[key])

    def split(res):
        return (res, None) if shards is None else res

    w_uq_p = wts["mla_w_uq"][:, perm]
    b_f_pad = jnp.pad(fox_b_f.reshape(1, -1), ((0, 0), (0, 128 - FOX_HEADS)))

    sh_a, sc_a, gt_a, sh_f, sc_f, gt_f = mods[0]
    h_in, u_m = mod_linear(x0, sh_a, sc_a, wts["mla_w_in"], F32, "mla_in", emit_u=True)
    q_m, kn_m, v_m, kr2_m, cq_m, ckv_m = mla_mid_fwd(
        h_in, vec(mla_g_q), vec(mla_g_kv), w_uq_p, wts["mla_w_uk"], wts["mla_w_uv"], cos8, sin8, cos64, sin64s, swap64,
        to_heads, dup, "mla_mid")
    (o_m, lse_m), got = split(mla_attn_fwd(q_m, kn_m, kr2_m, v_m, nb, "mla_attn", hosted=carried("ffn0")))
    ffn0_w = wts["ffn"][0] if got is None else _ffn_weights(got)
    y0, x1 = linear_resid_ln(o_m, wts["mla_w_o"], x0, gt_a, vec(ln_g[0, 0]), vec(ln_b[0, 0]), "mla_out")
    (u_f0, hg0, hu0, y1, x2), got = split(ffn_fwd(x1, sh_f, sc_f, gt_f, *ffn0_w, vec(ln_g[0, 1]), vec(ln_b[0, 1]), "ffn0",
                                                  hosted=carried("fox")))
    fox_w_in_t, fox_w_o = (wts["fox_w_in"].T, wts["fox_w_o"]) if got is None else _fox_weights(got)
    fox_w_f_t = jnp.pad(fox_w_in_t[3 * d:], ((0, 128 - FOX_HEADS), (0, 0)))
    sh_a1, sc_a1, gt_a1, sh_f1, sc_f1, gt_f1 = mods[1]
    qkv, u_x = mod_linear(x2, sh_a1, sc_a1, fox_w_in_t, BF16, "fox_qkv", tn=1024, emit_u=True, w_rows=3 * d)
    hf = mod_linear(x2, sh_a1, sc_a1, fox_w_f_t, F32, "fox_f", w_rows=128)
    cum = fox_gate_fwd(hf, b_f_pad, tri, nb, "fox_gate")
    cum_rows = rows16(cum, "fox_cum_rows")
    (o_x, lse_x), got = split(fox_attn_fwd(qkv, cum, cum_rows, nb, "fox_attn", hosted=carried("ffn1")))
    ffn1_w = wts["ffn"][1] if got is None else _ffn_weights(got)
    y2, x3 = linear_resid_ln(o_x, fox_w_o, x2, gt_a1, vec(ln_g[1, 0]), vec(ln_b[1, 0]), "fox_out")
    u_f1, hg1, hu1, y3, x4 = ffn_fwd(x3, sh_f1, sc_f1, gt_f1, *ffn1_w, vec(ln_g[1, 1]), vec(ln_b[1, 1]), "ffn1")
    dx4, sq_err = loss_grad(x4, tgt, "loss")
    loss_part = 0.5 * jnp.sum(sq_err) / d

    parts, recv = {}, {}

    def halves_of(g):
        return g.reshape(N_CHIPS, 2, g.shape[1] // 2, g.shape[2])

    def scatter(keys, sent):
        return None if shards is None else _scatter_comm([sent[k] for k in keys])

    def landed(keys, got):
        if got is not None:
            recv.update(zip(keys, got))

    def ffn_grads(layer, u, dhg, dhu, act, dy):
        sent = {}
        for n, (a_op, b_op) in (("ffn_w_gate", (dhg, u[None])), ("ffn_w_up", (dhu, u[None])), ("ffn_w_down", (act, dy[None]))):
            g32, g16 = wgrad(a_op, b_op, "ffn%d_d%s" % (layer, n[4:]), with_bf16=True)
            parts["%s/%d" % (n, layer)], sent["%s/%d" % (n, layer)] = halves_of(g32), halves_of(g16)
        return sent

    dz3, dy3, dg11, db11, dgt_f1 = ln_bwd(dx4, x3, y3, gt_f1, vec(ln_g[1, 1]), "ffn1_ln_bwd")
    dhg1, dhu1, act1, dx3, dsc_f1, dsh_f1 = ffn_bwd(dy3, hg1, hu1, *ffn1_w, dz3, x3, sc_f1, "ffn1_bwd")
    sent = ffn_grads(1, u_f1, dhg1, dhu1, act1, dy3)
    dz2, dy2, dg10, db10, dgt_a1 = ln_bwd(dx3, x2, y2, gt_a1, vec(ln_g[1, 0]), "fox_ln_bwd")
    do_x, delta_x = linear_nt_delta(dy2, fox_w_o, o_x, sel_fox, "fox_out_bwd")
    (dq_x, dk_x, dv_x, dfq_x, dfk_x), got = split(fox_attn_bwd(
        qkv, do_x, cum, cum_rows, rows16(lse_x, "fox_lse_rows"), rows16(delta_x, "fox_delta_rows"), nb, "fox_attn_bwd",
        hosted=scatter(list(sent), sent)))
    landed(list(sent), got)
    dcum = tokens128(dfq_x + dfk_x, onehot16, "fox_dcum")
    dhf, dbf = fox_gate_bwd(dcum, hf, b_f_pad, triu, nb, "fox_gate_bwd")
    fox_d = [("q", dq_x), ("k", dk_x), ("v", dv_x)]
    dx2, dsc_a1, dsh_a1 = linear_nt_mod_bwd(
        [(dh, fox_w_in_t, i) for i, (_, dh) in enumerate(fox_d)] + [(dhf, fox_w_f_t, 0)], dz2, x2, sc_a1, "fox_in_bwd")
    dw_in_t = [wgrad(dh[None], u_x[None], "fox_dw" + tag)[0] for tag, dh in fox_d]
    dw_in_t.append(wgrad(dhf[None], u_x[None], "fox_dwf")[0][:FOX_HEADS])
    dw_in_t = jnp.concatenate(dw_in_t, axis=0).reshape(N_CHIPS, -1, 2, d // 2)
    parts["fox_w_in"] = jnp.transpose(dw_in_t, (0, 2, 1, 3))
    parts["fox_w_o"] = wgrad(o_x[None], dy2[None], "fox_dwo")[0].reshape(N_CHIPS, 2, -1, d)
    sent = {k: parts[k].astype(BF16) for k in ("fox_w_in", "fox_w_o")}
    dz1, dy1, dg01, db01, dgt_f0 = ln_bwd(dx2, x1, y1, gt_f, vec(ln_g[0, 1]), "ffn0_ln_bwd")
    (dhg0, dhu0, act0, dx1, dsc_f0, dsh_f0), got = split(ffn_bwd(dy1, hg0, hu0, *ffn0_w, dz1, x1, sc_f, "ffn0_bwd",
                                                                 hosted=scatter(list(sent), sent)))
    landed(list(sent), got)
    sent = ffn_grads(0, u_f0, dhg0, dhu0, act0, dy1)
    dz0, dy0, dg00, db00, dgt_a0 = ln_bwd(dx1, x0, y0, gt_a, vec(ln_g[0, 0]), "mla_ln_bwd")
    do_m, delta_m = linear_nt_delta(dy0, wts["mla_w_o"], o_m, sel_mla, "mla_out_bwd")
    parts["mla_w_o"] = wgrad(o_m[None], dy0[None], "mla_dwo")[0].reshape(N_CHIPS, 2, -1, d)
    sent["mla_w_o"] = parts["mla_w_o"].astype(BF16)
    (dqn_m, dqr_m, dkn_m, dkr_m, dv_m), got = split(mla_attn_bwd(
        q_m, kn_m, kr2_m, v_m, do_m, rows16(lse_m, "mla_lse_rows"), rows16(delta_m, "mla_delta_rows"), nb,
        "mla_attn_bwd", hosted=scatter(list(sent), sent)))
    landed(list(sent), got)
    dh_in, dq_pre, dgq, dgkv = mla_mid_bwd(
        dqn_m, dqr_m, dkn_m, dv_m, dkr_m, h_in, vec(mla_g_q), vec(mla_g_kv), w_uq_p, wts["mla_w_uk"],
        wts["mla_w_uv"], cos8, sin8, cos64, sin64s, swap64, from_heads, head_sum, "mla_mid_bwd")
    parts["mla_w_uq"] = halves_of(_full_to_cols(wgrad(cq_m[None], dq_pre[None], "mla_dwuq")[0][:, inv_perm]))
    parts["mla_w_uk"] = halves_of(_full_to_cols(wgrad(ckv_m[None], dkn_m[None], "mla_dwuk")[0]))
    parts["mla_w_uv"] = halves_of(_full_to_cols(wgrad(ckv_m[None], dv_m[None], "mla_dwuv")[0]))
    parts["mla_w_in"] = wgrad(u_m[None], dh_in[None], "mla_dwin")[0].reshape(N_CHIPS, 2, -1, h_in.shape[1])
    sent = {k: parts[k].astype(BF16) for k in ("mla_w_in", "mla_w_uq", "mla_w_uk", "mla_w_uv")}
    (dx0, dsc_a0, dsh_a0), got = split(linear_nt_mod_bwd([(dh_in, wts["mla_w_in"], None)], dz0, x0, sc_a, "mla_in_bwd",
                                                         hosted=scatter(list(sent), sent)))
    landed(list(sent), got)

    dmods = [(dsh_a0, dsc_a0, dgt_a0, dsh_f0, dsc_f0, dgt_f0), (dsh_a1, dsc_a1, dgt_a1, dsh_f1, dsc_f1, dgt_f1)]
    d_ln_g = jnp.stack([jnp.concatenate([dg00, dg01], axis=0), jnp.concatenate([dg10, dg11], axis=0)])
    d_ln_b = jnp.stack([jnp.concatenate([db00, db01], axis=0), jnp.concatenate([db10, db11], axis=0)])
    return loss_part, dx0.reshape(nb, s, d), (parts, recv), dmods, d_ln_g, d_ln_b, dgq, dgkv, dbf[:, :FOX_HEADS]


def _pad_rows(a, rows):
    return jnp.pad(a, ((0, rows - a.shape[0]), (0, 0)))


def kernel(x, c, positions, mla_w_in, mla_g_q, mla_w_uq, mla_g_kv, mla_w_uk, mla_w_uv, mla_w_o, fox_w_in, fox_b_f, fox_w_o, ada_w, ada_b, ffn_w_gate, ffn_w_up, ffn_w_down, ln_g, ln_b, loss_target, m_mla_w_in, m_mla_g_q, m_mla_w_uq, m_mla_g_kv, m_mla_w_uk, m_mla_w_uv, m_mla_w_o, m_fox_w_in, m_fox_b_f, m_fox_w_o, m_ada_w, m_ada_b, m_ffn_w_gate, m_ffn_w_up, m_ffn_w_down, m_ln_g, m_ln_b, v_mla_w_in, v_mla_g_q, v_mla_w_uq, v_mla_g_kv, v_mla_w_uk, v_mla_w_uv, v_mla_w_o, v_fox_w_in, v_fox_b_f, v_fox_w_o, v_ada_w, v_ada_b, v_ffn_w_gate, v_ffn_w_up, v_ffn_w_down, v_ln_g, v_ln_b):
    args = dict(locals())
    nb, s, d = x.shape
    ax, ay, ac = lax.axis_index("x"), lax.axis_index("y"), lax.axis_index("c")
    chip = 2 * ax + ay
    dev = 2 * chip + ac
    n_dev = 2 * N_CHIPS
    n_all = nb * n_dev

    shard_shapes = {n: (args[n].shape if _SHARD_KIND[n] == "chunk" else args[n].shape[1:]) for n in _PACKED}

    def block(n, layer=None):
        w = args[n].reshape(shard_shapes[n]) if layer is None else args[n][layer]
        return _halves(w.astype(BF16))

    mla_names = [n for n in _PACKED if n.startswith("mla")]
    wts = {}
    for n, g in zip(mla_names, all_gather_chips([block(n) for n in mla_names], "gather_mla")):
        g = g.reshape(N_CHIPS, *shard_shapes[n])
        wts[n] = g.reshape(-1, g.shape[-1]) if _SHARD_KIND[n] == "rows" else _cols_to_full(g)
    ffn_names = ("ffn_w_gate", "ffn_w_up", "ffn_w_down")
    fox_in_t = jnp.swapaxes(fox_w_in, 1, 2)[0].astype(BF16)
    fox_in_t = jnp.stack([fox_in_t[:, :d // 2], fox_in_t[:, d // 2:]])
    shards = {"ffn0": [block(n, 0) for n in ffn_names], "fox": [fox_in_t, block("fox_w_o")],
              "ffn1": [block(n, 1) for n in ffn_names]}

    ln_cols = ln_g.shape[-1]
    ln_blk = jnp.concatenate([ln_g.reshape(2 * DEPTH, ln_cols), ln_b.reshape(2 * DEPTH, ln_cols)], axis=0)
    early = jnp.concatenate([_pad_rows(c, 8), jnp.pad(_pad_rows(ln_blk, 8), ((0, 0), (0, d - ln_cols)))], axis=0)
    early = all_gather8(early, "gather_c_ln").reshape(n_dev, 16, d)
    c_all = early[:, :nb].reshape(n_all, d)
    ln_all = early.reshape(N_CHIPS, 2, 16, d)[:, 0, 8:8 + 4 * DEPTH, :ln_cols]
    ln_all = jnp.transpose(ln_all, (1, 0, 2)).reshape(4 * DEPTH, d)
    ln_g_full = ln_all[:2 * DEPTH].reshape(DEPTH, 2, d)
    ln_b_full = ln_all[2 * DEPTH:].reshape(DEPTH, 2, d)
    mod_part = ada_mod_part(c_all, ada_w, "ada_mod")
    ncol = mod_part.shape[-1]
    mod_g = all_gather8(mod_part.reshape(DEPTH * n_all, ncol), "gather_mod")
    mod_g = mod_g.reshape(N_CHIPS, 2, DEPTH, n_all, ncol)[:, 0]
    mod_full = jnp.transpose(mod_g, (1, 2, 0, 3)).reshape(DEPTH, n_all, N_CHIPS * ncol) + ada_b[:, None, :]
    mod_loc = lax.dynamic_slice_in_dim(mod_full, dev * nb, nb, axis=1)
    mods = [tuple(mod_loc[i, :, k * d:(k + 1) * d].reshape(nb, 1, d) for k in range(6)) for i in range(DEPTH)]

    loss_part, grad_x, (parts, recv), dmods, d_ln_g, d_ln_b, dgq, dgkv, dbf = _local_step(
        x, positions, loss_target, mods, wts, ln_g_full, ln_b_full, mla_g_q[0], mla_g_kv[0], fox_b_f[0], shards)
    loss = lax.psum(loss_part, ("x", "y", "c"))

    dmod_rows = jnp.stack([jnp.concatenate([v_.reshape(nb, d) for v_ in dm], axis=1) for dm in dmods])
    small = jnp.concatenate([
        d_ln_g.reshape(2 * DEPTH, d), d_ln_b.reshape(2 * DEPTH, d),
        jnp.pad(jnp.concatenate([dgq, dgkv, dbf], axis=1), ((0, 0), (0, d - 2 * MLA_QR - FOX_HEADS))),
        dmod_rows.reshape(DEPTH * nb * 6, d)], axis=0)
    n_small = small.shape[0]
    small_rows = -(-n_small // 8) * 8
    small_all = all_gather8(_pad_rows(small, small_rows), "gather_stats").reshape(n_dev, small_rows, d)
    stat_sum = sum_leading(small_all, "sum_stats")
    g_ln_g = lax.dynamic_slice_in_dim(stat_sum[:2 * DEPTH], chip * ln_cols, ln_cols, axis=1).reshape(DEPTH, 2, ln_cols)
    g_ln_b = lax.dynamic_slice_in_dim(stat_sum[2 * DEPTH:4 * DEPTH], chip * ln_cols, ln_cols, axis=1).reshape(DEPTH, 2, ln_cols)
    row = stat_sum[4 * DEPTH]
    g_gq = row[:MLA_QR].reshape(1, MLA_QR)
    g_gkv = row[MLA_QR:2 * MLA_QR].reshape(1, MLA_KVR)
    g_bf = row[2 * MLA_QR:2 * MLA_QR + FOX_HEADS].reshape(1, FOX_HEADS)
    base = 4 * DEPTH + 1
    dmod_all = small_all[:, base:base + DEPTH * nb * 6].reshape(n_dev, DEPTH, nb, 6 * d)
    dmod_all = jnp.transpose(dmod_all, (1, 0, 2, 3)).reshape(DEPTH, n_all, 6 * d)
    g_ada_b = sum_leading(jnp.transpose(dmod_all, (1, 0, 2)), "sum_ada_b")
    dmod_mine = lax.dynamic_slice_in_dim(dmod_all, chip * ncol, ncol, axis=2)
    g_ada_w = ada_grad(c_all.T, dmod_mine, "ada_grad")

    place = jnp.stack([dev, ac, chip]).astype(jnp.int32)
    bufs = []
    for n in _PACKED:
        if _SHARD_KIND[n] == "chunk":
            buf = None
            for layer in range(DEPTH):
                key = "%s/%d" % (n, layer)
                buf = sum_devices(parts[key], recv[key], place, "rs_sum_%s%d" % (n, layer), slot=(layer, DEPTH, buf))
        else:
            buf = sum_devices(parts[n], recv[n], place, "rs_sum_" + n)
        bufs.append(buf)
    joined = sibling_join_halves(bufs, "rs_join")
    g_big = {n: j.reshape(j.shape[0], 2 * j.shape[2], j.shape[3]) for n, j in zip(_PACKED, joined)}
    j = joined[_PACKED.index("fox_w_in")]
    g_big["fox_w_in"] = jnp.transpose(j, (0, 2, 1, 3)).reshape(1, j.shape[2], 2 * j.shape[3])

    g_out = {
        "mla_w_in": g_big["mla_w_in"], "mla_g_q": g_gq, "mla_w_uq": g_big["mla_w_uq"], "mla_g_kv": g_gkv,
        "mla_w_uk": g_big["mla_w_uk"], "mla_w_uv": g_big["mla_w_uv"], "mla_w_o": g_big["mla_w_o"],
        "fox_w_in": g_big["fox_w_in"], "fox_b_f": g_bf, "fox_w_o": g_big["fox_w_o"],
        "ada_w": g_ada_w, "ada_b": g_ada_b, "ffn_w_gate": g_big["ffn_w_gate"], "ffn_w_up": g_big["ffn_w_up"],
        "ffn_w_down": g_big["ffn_w_down"], "ln_g": g_ln_g, "ln_b": g_ln_b}
    names = ["mla_w_in", "mla_g_q", "mla_w_uq", "mla_g_kv", "mla_w_uk", "mla_w_uv", "mla_w_o", "fox_w_in", "fox_b_f",
             "fox_w_o", "ada_w", "ada_b", "ffn_w_gate", "ffn_w_up", "ffn_w_down", "ln_g", "ln_b"]
    small_names = ["mla_g_q", "mla_g_kv", "fox_b_f", "ada_b", "ln_g", "ln_b"]
    deltas, new_m, new_v = {}, {}, {}
    for n in names:
        if n in small_names:
            continue
        shp = args[n].shape
        if n in _TRANSPOSED:
            view = lambda a: jnp.swapaxes(a, 1, 2).reshape(-1, shp[1])
            back = lambda a: jnp.swapaxes(a.reshape(shp[0], shp[2], shp[1]), 1, 2)
        else:
            view = lambda a: a.reshape(-1, shp[-1])
            back = lambda a: a.reshape(shp)
        dl, mn, vn = adamw(view(args[n]), g_out[n].reshape(view(args[n]).shape), view(args["m_" + n]),
                           view(args["v_" + n]), "adamw_" + n)
        g_out[n], deltas[n], new_m[n], new_v[n] = back(g_out[n].reshape(view(args[n]).shape)), back(dl), back(mn), back(vn)

    def small_pack(prefix, src):
        flat = jnp.concatenate([src[prefix + n].reshape(-1) for n in small_names])
        size = -(-flat.shape[0] // (8 * 128)) * 8 * 128
        return jnp.pad(flat, (0, size - flat.shape[0])).reshape(-1, 128)

    sd, sm, sv = adamw(small_pack("", args), small_pack("", g_out), small_pack("m_", args), small_pack("v_", args),
                       "adamw_small")
    off = 0
    for n in small_names:
        shp = args[n].shape
        size = math.prod(shp)
        deltas[n] = sd.reshape(-1)[off:off + size].reshape(shp)
        new_m[n] = sm.reshape(-1)[off:off + size].reshape(shp)
        new_v[n] = sv.reshape(-1)[off:off + size].reshape(shp)
        off += size

    outs = [loss, grad_x]
    outs += [g_out[n].reshape(args[n].shape) for n in names]
    outs += [deltas[n] for n in names] + [new_m[n] for n in names] + [new_v[n] for n in names]
    return tuple(outs)
```

```python
import functools
import math

import numpy as np
import jax
import jax.numpy as jnp
from jax import lax
from jax.experimental import pallas as pl
from jax.experimental.pallas import tpu as pltpu

F32 = jnp.float32
BF16 = jnp.bfloat16
MESH = pl.DeviceIdType.MESH

D_MODEL = 1024
DEPTH = 2
MLA_HEADS = 8
MLA_NOPE = 128
MLA_ROPE = 64
MLA_V = 128
MLA_QR = 256
MLA_KVR = 256
ROPE_THETA = 10000.0
FOX_HEADS = 16
FOX_HD = 64
D_FF = 2816
N_CHIPS = 4
FF_CHUNK = D_FF // N_CHIPS
ALPHA = (2.0 * DEPTH) ** 0.25
EPS = 1e-5
ADAM_LR = 0.001
ADAM_B1 = 0.9
ADAM_B2 = 0.999
ADAM_EPS = 1e-08
ADAM_WD = 0.01
ADAM_STEP = 10

VMEM_LIMIT = 56 * 1024 * 1024
TOKEN_TILE = 512
WGRAD_TOKENS = 2048
ATTN_TILE = 512
FOX_GROUP = 4
MLA_GROUP = 4
COMM_BLOCK_BYTES = 2 * 1024 * 1024
ADAMW_BLOCK_BYTES = 1024 * 1024


def _cp(n_axes):
    return pltpu.CompilerParams(dimension_semantics=("arbitrary",) * n_axes, vmem_limit_bytes=VMEM_LIMIT)


def _dot(a, b):
    return jnp.dot(a, b, preferred_element_type=F32)


def _dot_nt(a, b):
    return lax.dot_general(a, b, (((1,), (1,)), ((), ())), preferred_element_type=F32)


def _dot_tn(a, b):
    return lax.dot_general(a, b, (((0,), (0,)), ((), ())), preferred_element_type=F32)


def _dot_f32(a, b):
    return jnp.dot(a, b, preferred_element_type=F32, precision=lax.Precision.HIGHEST)


def _sds(shape, dtype):
    return jax.ShapeDtypeStruct(shape, dtype)


def _place():
    return lax.axis_index("x"), lax.axis_index("y"), lax.axis_index("c")


class _Hosted:
    def __init__(self, inputs, out_shape, sems, start, finish):
        self.inputs, self.out_shape, self.sems, self.start, self.finish = inputs, out_shape, sems, start, finish


def _call(body, name, grid, in_specs, out_specs, out_shape, args, scratch_shapes=(), hosted=None):
    in_specs, out_specs, out_shape, scratch_shapes = list(in_specs), list(out_specs), list(out_shape), list(scratch_shapes)
    if hosted is None:
        return pl.pallas_call(body, name=name, grid=grid, in_specs=in_specs, out_specs=out_specs, out_shape=out_shape,
                              scratch_shapes=scratch_shapes, compiler_params=_cp(len(grid)))(*args)
    n_in, n_out, n_scr = len(in_specs), len(out_specs), len(scratch_shapes)
    h_in, h_out = len(hosted.inputs), len(hosted.out_shape)

    def carried(*refs):
        o0 = n_in + h_in
        s0 = o0 + n_out + h_out
        c_in, c_out, c_sem = refs[n_in:o0], refs[o0 + n_out:s0], refs[s0 + n_scr:]
        ids = [pl.program_id(a) for a in range(len(grid))]
        first = functools.reduce(jnp.logical_and, [i == 0 for i in ids])
        last = functools.reduce(jnp.logical_and, [i == g - 1 for i, g in zip(ids, grid)])

        @pl.when(first)
        def _():
            hosted.start(c_in, c_out, c_sem)

        body(*refs[:n_in], *refs[o0:o0 + n_out], *refs[s0:s0 + n_scr])

        @pl.when(last)
        def _():
            hosted.finish(c_in, c_out, c_sem)

    hbm = pl.BlockSpec(memory_space=pl.ANY)
    res = pl.pallas_call(
        carried, name=name, grid=grid, in_specs=in_specs + [hbm] * h_in, out_specs=out_specs + [hbm] * h_out,
        out_shape=out_shape + list(hosted.out_shape), scratch_shapes=scratch_shapes + list(hosted.sems),
        compiler_params=_cp(len(grid)))(*args, *hosted.inputs)
    return res[:n_out], res[n_out:]


def mod_linear(x, shift, scale, w, out_dtype, name, tn=None, emit_u=False, w_rows=None):
    t, d = x.shape
    n = w.shape[1] if w_rows is None else w_rows
    tn = n if tn is None else tn
    tm = TOKEN_TILE
    tps = (t // shift.shape[0]) // tm

    def body(x_ref, sh_ref, sc_ref, w_ref, o_ref, *rest):
        u = (x_ref[...] * (1.0 + sc_ref[...]) + sh_ref[...]).astype(BF16)
        o_ref[...] = (_dot(u, w_ref[...]) if w_rows is None else _dot_nt(u, w_ref[...])).astype(out_dtype)
        if emit_u:
            @pl.when(pl.program_id(1) == 0)
            def _():
                rest[0][...] = u

    vec = pl.BlockSpec((None, 1, d), lambda i, j: (i // tps, 0, 0))
    out_shape = [_sds((t, n), out_dtype)]
    out_specs = [pl.BlockSpec((tm, tn), lambda i, j: (i, j))]
    if emit_u:
        out_shape.append(_sds((t, d), BF16))
        out_specs.append(pl.BlockSpec((tm, d), lambda i, j: (i, 0)))
    w_spec = pl.BlockSpec((d, tn), lambda i, j: (0, j)) if w_rows is None else pl.BlockSpec((tn, d), lambda i, j: (j, 0))
    res = pl.pallas_call(
        body, name=name, grid=(t // tm, n // tn),
        in_specs=[pl.BlockSpec((tm, d), lambda i, j: (i, 0)), vec, vec, w_spec],
        out_specs=out_specs, out_shape=out_shape, compiler_params=_cp(2),
    )(x, shift, scale, w)
    return res if emit_u else res[0]


def _rms(h, g):
    rstd = lax.rsqrt(jnp.mean(h * h, axis=-1, keepdims=True) + EPS)
    return h * rstd, rstd


def mla_mid_fwd(h, g_q, g_kv, w_uq, w_uk, w_uv, cos8, sin8, cos64, sin64s, swap64, rope_to_heads, dup64, name):
    t = h.shape[0]
    tm = TOKEN_TILE
    hq = MLA_HEADS * MLA_NOPE
    hr = MLA_HEADS * MLA_ROPE // 2

    def body(h_ref, gq_ref, gkv_ref, wuq_ref, wuk_ref, wuv_ref, c8_ref, s8_ref, c64_ref, s64_ref, sw_ref, p_ref, d_ref,
             q_ref, kn_ref, v_ref, kr_ref, cq_ref, ckv_ref):
        hh = h_ref[...]
        cq = (_rms(hh[:, :MLA_QR], None)[0] * gq_ref[...]).astype(BF16)
        ckv = (_rms(hh[:, MLA_QR:MLA_QR + MLA_KVR], None)[0] * gkv_ref[...]).astype(BF16)
        cq_ref[...] = cq
        ckv_ref[...] = ckv
        q = _dot(cq, wuq_ref[...])
        x1 = q[:, hq:hq + hr]
        x2 = q[:, hq + hr:]
        cs = c8_ref[...]
        sn = s8_ref[...]
        rot = jnp.concatenate([x1 * cs - x2 * sn, x2 * cs + x1 * sn], axis=1).astype(BF16)
        q_ref[...] = jnp.concatenate([q[:, :hq].astype(BF16), _dot(rot, p_ref[...]).astype(BF16)], axis=1)
        kn_ref[...] = _dot(ckv, wuk_ref[...]).astype(BF16)
        v_ref[...] = _dot(ckv, wuv_ref[...]).astype(BF16)
        kr = hh[:, MLA_QR + MLA_KVR:]
        kr = (kr * c64_ref[...] + _dot_f32(kr, sw_ref[...]) * s64_ref[...]).astype(BF16)
        kr_ref[...] = _dot(kr, d_ref[...]).astype(BF16)

    def rows(n):
        return pl.BlockSpec((tm, n), lambda i: (i, 0))

    def whole(a):
        return pl.BlockSpec(a.shape, lambda i: (0,) * a.ndim)

    nq = w_uq.shape[1]
    return pl.pallas_call(
        body, name=name, grid=(t // tm,),
        in_specs=[rows(h.shape[1]), whole(g_q), whole(g_kv), whole(w_uq), whole(w_uk), whole(w_uv),
                  rows(hr), rows(hr), rows(MLA_ROPE), rows(MLA_ROPE), whole(swap64), whole(rope_to_heads), whole(dup64)],
        out_specs=[rows(nq), rows(hq), rows(hq), rows(2 * MLA_ROPE), rows(MLA_QR), rows(MLA_KVR)],
        out_shape=[_sds((t, nq), BF16), _sds((t, hq), BF16), _sds((t, hq), BF16), _sds((t, 2 * MLA_ROPE), BF16),
                   _sds((t, MLA_QR), BF16), _sds((t, MLA_KVR), BF16)],
        compiler_params=_cp(1),
    )(h, g_q, g_kv, w_uq, w_uk, w_uv, cos8, sin8, cos64, sin64s, swap64, rope_to_heads, dup64)


def _pick_lane(tile, idx):
    lane = lax.broadcasted_iota(jnp.int32, tile.shape, 1)
    return jnp.sum(jnp.where(lane == idx, tile, 0.0), axis=1, keepdims=True)


def _pick_row(tile, idx):
    row = lax.broadcasted_iota(jnp.int32, tile.shape, 0)
    return jnp.sum(jnp.where(row == idx, tile, 0.0), axis=0, keepdims=True)


def _put_lane(tile, idx, col):
    lane = lax.broadcasted_iota(jnp.int32, tile.shape, 1)
    return jnp.where(lane == idx, col, tile)


def _put_row(tile, idx, row):
    r = lax.broadcasted_iota(jnp.int32, tile.shape, 0)
    return tile + jnp.where(r == idx, row, 0.0)


def _causal_softmax_blocks(i, tq, heads):
    def block(j, carry, masked):
        new = []
        for (score_fn, pv_fn, _), (m, l, acc) in zip(heads, carry):
            sc = score_fn(j)
            if masked:
                keep = lax.broadcasted_iota(jnp.int32, sc.shape, 0) >= lax.broadcasted_iota(jnp.int32, sc.shape, 1)
                sc = jnp.where(keep, sc, -1e30)
            m_new = jnp.maximum(m, jnp.max(sc, axis=1, keepdims=True))
            a = jnp.exp(m - m_new)
            p = jnp.exp(sc - m_new)
            new.append((m_new, a * l + jnp.sum(p, axis=1, keepdims=True), a * acc + pv_fn(j, p.astype(BF16))))
        return tuple(new)

    init = tuple((jnp.full((tq, 1), -1e30, F32), jnp.zeros((tq, 1), F32), jnp.zeros((tq, dv), F32)) for _, _, dv in heads)
    carry = lax.fori_loop(0, i, lambda j, c: block(j, c, False), init)
    return [(acc / l, m + jnp.log(l)) for m, l, acc in block(i, carry, True)]


def fox_attn_fwd(qkv, cum, cum_rows, nb, name, hosted=None):
    t = qkv.shape[0]
    s = t // nb
    tq = ATTN_TILE
    nq = s // tq
    wide = FOX_GROUP * FOX_HD
    ngroups = FOX_HEADS // FOX_GROUP
    scale = FOX_HD ** -0.5

    def body(q_ref, k_ref, v_ref, cum_ref, cr_ref, o_ref, lse_ref):
        i = pl.program_id(1)
        hg = pl.program_id(2)

        @pl.when(hg == 0)
        def _():
            lse_ref[...] = jnp.zeros_like(lse_ref)

        low = lax.broadcasted_iota(jnp.int32, (tq, 128), 1) < FOX_HD
        cum_t = cum_ref[...]

        def rows_of(j):
            return pl.ds(pl.multiple_of(j * tq, tq), tq)

        def head(a):
            hd = FOX_GROUP * hg + a
            cols = slice(128 * (a // 2), 128 * (a // 2) + 128)
            q = q_ref[:, cols]
            qa = jnp.where(low if a % 2 == 0 else jnp.logical_not(low), q, jnp.zeros_like(q))
            fq = _pick_lane(cum_t, hd)
            return (lambda j: _dot_nt(qa, k_ref[rows_of(j), cols]) * scale + fq - _pick_row(cr_ref[j], hd),
                    lambda j, p: _dot(p, v_ref[rows_of(j), cols]), 2 * FOX_HD)

        res = _causal_softmax_blocks(i, tq, [head(a) for a in range(FOX_GROUP)])
        o_ref[...] = jnp.concatenate([jnp.where(low, res[a][0], res[a + 1][0]) for a in range(0, FOX_GROUP, 2)],
                                     axis=1).astype(BF16)
        lse_t = lse_ref[...]
        for a in range(FOX_GROUP):
            lse_t = _put_lane(lse_t, FOX_GROUP * hg + a, res[a][1])
        lse_ref[...] = lse_t

    return _call(
        body, name, (nb, nq, ngroups),
        [pl.BlockSpec((tq, wide), lambda b, i, hg: (b * nq + i, hg)),
         pl.BlockSpec((s, wide), lambda b, i, hg: (b, ngroups + hg)),
         pl.BlockSpec((s, wide), lambda b, i, hg: (b, 2 * ngroups + hg)),
         pl.BlockSpec((tq, 128), lambda b, i, hg: (b * nq + i, 0)),
         pl.BlockSpec((nq, 16, tq), lambda b, i, hg: (b, 0, 0))],
        [pl.BlockSpec((tq, wide), lambda b, i, hg: (b * nq + i, hg)),
         pl.BlockSpec((tq, 128), lambda b, i, hg: (b * nq + i, 0))],
        [_sds((t, D_MODEL), BF16), _sds((t, 128), F32)], (qkv, qkv, qkv, cum, cum_rows), hosted=hosted)


def mla_attn_fwd(q, kn, kr2, v, nb, name, hosted=None):
    t = q.shape[0]
    s = t // nb
    tq = ATTN_TILE
    nq = s // tq
    ngroups = MLA_HEADS // MLA_GROUP
    wide = MLA_GROUP * MLA_NOPE
    rwide = MLA_GROUP * MLA_ROPE
    scale = (MLA_NOPE + MLA_ROPE) ** -0.5

    def body(qn_ref, qr_ref, kn_ref, kr_ref, v_ref, o_ref, lse_ref):
        i = pl.program_id(1)
        hg = pl.program_id(2)

        @pl.when(hg == 0)
        def _():
            lse_ref[...] = jnp.zeros_like(lse_ref)

        low = lax.broadcasted_iota(jnp.int32, (tq, 128), 1) < MLA_ROPE

        def rows_of(j):
            return pl.ds(pl.multiple_of(j * tq, tq), tq)

        def head(a):
            cols = slice(a * MLA_NOPE, (a + 1) * MLA_NOPE)
            qr = qr_ref[:, 128 * (a // 2):128 * (a // 2) + 128]
            q_cat = jnp.concatenate([qn_ref[:, cols], jnp.where(low if a % 2 == 0 else jnp.logical_not(low), qr,
                                                                jnp.zeros_like(qr))], axis=1)
            return (lambda j: _dot_nt(q_cat, jnp.concatenate([kn_ref[rows_of(j), cols], kr_ref[rows_of(j), :]], axis=1)) * scale,
                    lambda j, p: _dot(p, v_ref[rows_of(j), cols]), MLA_V)

        res = _causal_softmax_blocks(i, tq, [head(a) for a in range(MLA_GROUP)])
        o_ref[...] = jnp.concatenate([r[0] for r in res], axis=1).astype(BF16)
        lse_t = lse_ref[...]
        for a in range(MLA_GROUP):
            lse_t = _put_lane(lse_t, MLA_GROUP * hg + a, res[a][1])
        lse_ref[...] = lse_t

    rope0 = MLA_HEADS * MLA_NOPE // rwide
    return _call(
        body, name, (nb, nq, ngroups),
        [pl.BlockSpec((tq, wide), lambda b, i, hg: (b * nq + i, hg)),
         pl.BlockSpec((tq, rwide), lambda b, i, hg: (b * nq + i, rope0 + hg)),
         pl.BlockSpec((s, wide), lambda b, i, hg: (b, hg)),
         pl.BlockSpec((s, 128), lambda b, i, hg: (b, 0)),
         pl.BlockSpec((s, wide), lambda b, i, hg: (b, hg))],
        [pl.BlockSpec((tq, wide), lambda b, i, hg: (b * nq + i, hg)),
         pl.BlockSpec((tq, 128), lambda b, i, hg: (b * nq + i, 0))],
        [_sds((t, MLA_HEADS * MLA_V), BF16), _sds((t, 128), F32)], (q, q, kn, kr2, v), hosted=hosted)


def rows16(a, name):
    t = a.shape[0]
    tq = ATTN_TILE

    def body(a_ref, o_ref):
        o_ref[...] = a_ref[...].T[:16, :]

    return pl.pallas_call(
        body, name=name, grid=(t // tq,), in_specs=[pl.BlockSpec((tq, 128), lambda n: (n, 0))],
        out_specs=pl.BlockSpec((None, 16, tq), lambda n: (n, 0, 0)), out_shape=_sds((t // tq, 16, tq), F32),
        compiler_params=_cp(1),
    )(a)


def tokens128(rows, onehot, name):
    nblk, _, tq = rows.shape

    def body(r_ref, e_ref, o_ref):
        o_ref[...] = lax.dot_general(r_ref[...], e_ref[...], (((0,), (0,)), ((), ())), preferred_element_type=F32,
                                     precision=lax.Precision.HIGHEST)

    return pl.pallas_call(
        body, name=name, grid=(nblk,),
        in_specs=[pl.BlockSpec((None, 16, tq), lambda n: (n, 0, 0)), pl.BlockSpec((16, 128), lambda n: (0, 0))],
        out_specs=pl.BlockSpec((tq, 128), lambda n: (n, 0)), out_shape=_sds((nblk * tq, 128), F32),
        compiler_params=_cp(1),
    )(rows, onehot)


def _layer_norm(z, g, b):
    mu = jnp.mean(z, axis=-1, keepdims=True)
    zc = z - mu
    rstd = lax.rsqrt(jnp.mean(zc * zc, axis=-1, keepdims=True) + EPS)
    xhat = zc * rstd
    return xhat * g + b, xhat, rstd


def linear_resid_ln(a, w, x_in, gate, ln_g, ln_b, name):
    t, kdim = a.shape
    d = w.shape[1]
    tm = TOKEN_TILE
    tps = (t // gate.shape[0]) // tm

    def body(a_ref, w_ref, x_ref, gt_ref, g_ref, b_ref, y_ref, xo_ref):
        y = _dot(a_ref[...], w_ref[...])
        y_ref[...] = y
        z = ALPHA * x_ref[...] + (1.0 + gt_ref[...]) * y
        xo_ref[...] = _layer_norm(z, g_ref[...], b_ref[...])[0]

    rows = pl.BlockSpec((tm, d), lambda i: (i, 0))
    vec = pl.BlockSpec((1, d), lambda i: (0, 0))
    return pl.pallas_call(
        body, name=name, grid=(t // tm,),
        in_specs=[pl.BlockSpec((tm, kdim), lambda i: (i, 0)), pl.BlockSpec((kdim, d), lambda i: (0, 0)), rows,
                  pl.BlockSpec((None, 1, d), lambda i: (i // tps, 0, 0)), vec, vec],
        out_specs=[rows, rows], out_shape=[_sds((t, d), F32), _sds((t, d), F32)],
        compiler_params=_cp(1),
    )(a, w, x_in, gate, ln_g, ln_b)


def _resident(a):
    return pl.BlockSpec(a.shape, lambda *_: (0,) * a.ndim, pipeline_mode=pl.Buffered(1))


def ffn_fwd(x_in, shift, scale, gate, wg, wu, wd, ln_g, ln_b, name, hosted=None):
    t, d = x_in.shape
    c, _, fc = wg.shape
    tm = TOKEN_TILE
    tps = (t // gate.shape[0]) // tm

    def body(x_ref, sh_ref, sc_ref, gt_ref, wg_ref, wu_ref, wd_ref, g_ref, b_ref,
             u_ref, hg_ref, hu_ref, y_ref, xo_ref, acc_ref):
        cc = pl.program_id(1)

        @pl.when(cc == 0)
        def _():
            u_ref[...] = (x_ref[...] * (1.0 + sc_ref[...]) + sh_ref[...]).astype(BF16)
            acc_ref[...] = jnp.zeros_like(acc_ref)

        u = u_ref[...]
        hg = _dot(u, wg_ref[cc])
        hu = _dot(u, wu_ref[cc])
        hg_ref[...] = hg.astype(BF16)
        hu_ref[...] = hu.astype(BF16)
        act = (hg * jax.nn.sigmoid(hg) * hu).astype(BF16)
        acc_ref[...] += _dot(act, wd_ref[cc])

        @pl.when(cc == c - 1)
        def _():
            y = acc_ref[...]
            y_ref[...] = y
            z = ALPHA * x_ref[...] + (1.0 + gt_ref[...]) * y
            xo_ref[...] = _layer_norm(z, g_ref[...], b_ref[...])[0]

    rows = pl.BlockSpec((tm, d), lambda i, cc: (i, 0))
    bvec = pl.BlockSpec((None, 1, d), lambda i, cc: (i // tps, 0, 0))
    vec = pl.BlockSpec((1, d), lambda i, cc: (0, 0))
    hspec = pl.BlockSpec((None, tm, fc), lambda i, cc: (cc, i, 0))
    wcol = _resident(wg)
    return _call(
        body, name, (t // tm, c),
        [rows, bvec, bvec, bvec, wcol, wcol, _resident(wd), vec, vec],
        [rows, hspec, hspec, rows, rows],
        [_sds((t, d), BF16), _sds((c, t, fc), BF16), _sds((c, t, fc), BF16), _sds((t, d), F32), _sds((t, d), F32)],
        (x_in, shift, scale, gate, wg, wu, wd, ln_g, ln_b), scratch_shapes=[pltpu.VMEM((tm, d), F32)], hosted=hosted)


def fox_gate_fwd(hf, b_f, tri, n_batch, name):
    t, n = hf.shape
    blk = tri.shape[0]
    nb = (t // n_batch) // blk

    def body(hf_ref, b_ref, tri_ref, o_ref, carry_ref):
        @pl.when(pl.program_id(1) == 0)
        def _():
            carry_ref[...] = jnp.zeros_like(carry_ref)

        xx = hf_ref[...] + b_ref[...]
        lf = jnp.minimum(xx, 0.0) - jnp.log(1.0 + jnp.exp(-jnp.abs(xx)))
        cum = _dot_f32(tri_ref[...], lf) + carry_ref[...]
        o_ref[...] = cum
        carry_ref[...] = cum[blk - 1:blk, :]

    return pl.pallas_call(
        body, name=name, grid=(n_batch, nb),
        in_specs=[pl.BlockSpec((blk, n), lambda bb, i: (bb * nb + i, 0)), pl.BlockSpec((1, n), lambda bb, i: (0, 0)),
                  pl.BlockSpec((blk, blk), lambda bb, i: (0, 0))],
        out_specs=pl.BlockSpec((blk, n), lambda bb, i: (bb * nb + i, 0)),
        out_shape=_sds((t, n), F32), scratch_shapes=[pltpu.VMEM((1, n), F32)],
        compiler_params=_cp(2),
    )(hf, b_f, tri)


def loss_grad(x_out, target, name):
    t, d = x_out.shape
    tm = TOKEN_TILE

    def body(x_ref, t_ref, g_ref, l_ref):
        @pl.when(pl.program_id(0) == 0)
        def _():
            l_ref[...] = jnp.zeros_like(l_ref)

        err = x_ref[...] - t_ref[...]
        g_ref[...] = err / d
        l_ref[...] += jnp.sum(err * err, axis=0, keepdims=True)

    rows = pl.BlockSpec((tm, d), lambda i: (i, 0))
    return pl.pallas_call(
        body, name=name, grid=(t // tm,), in_specs=[rows, rows],
        out_specs=[rows, pl.BlockSpec((1, d), lambda i: (0, 0))],
        out_shape=[_sds((t, d), F32), _sds((1, d), F32)], compiler_params=_cp(1),
    )(x_out, target)


def ln_bwd(dxo, x_in, y, gate, ln_g, name):
    t, d = dxo.shape
    nb = gate.shape[0]
    tm = TOKEN_TILE
    tps = (t // nb) // tm

    def body(dxo_ref, x_ref, y_ref, gt_ref, g_ref, dz_ref, dy_ref, dg_ref, db_ref, dgt_ref):
        i = pl.program_id(0)

        @pl.when(i == 0)
        def _():
            dg_ref[...] = jnp.zeros_like(dg_ref)
            db_ref[...] = jnp.zeros_like(db_ref)

        @pl.when(i % tps == 0)
        def _():
            dgt_ref[...] = jnp.zeros_like(dgt_ref)

        yy = y_ref[...]
        g1 = 1.0 + gt_ref[...]
        z = ALPHA * x_ref[...] + g1 * yy
        _, xhat, rstd = _layer_norm(z, 1.0, 0.0)
        dxo_v = dxo_ref[...]
        dg_ref[...] += jnp.sum(dxo_v * xhat, axis=0, keepdims=True)
        db_ref[...] += jnp.sum(dxo_v, axis=0, keepdims=True)
        dxh = dxo_v * g_ref[...]
        dz = rstd * (dxh - jnp.mean(dxh, axis=-1, keepdims=True) - xhat * jnp.mean(dxh * xhat, axis=-1, keepdims=True))
        dz_ref[...] = dz
        dy_ref[...] = (g1 * dz).astype(BF16)
        dgt_ref[...] += jnp.sum(dz * yy, axis=0, keepdims=True)

    rows = pl.BlockSpec((tm, d), lambda i: (i, 0))
    vec = pl.BlockSpec((1, d), lambda i: (0, 0))
    bvec = pl.BlockSpec((None, 1, d), lambda i: (i // tps, 0, 0))
    return pl.pallas_call(
        body, name=name, grid=(t // tm,), in_specs=[rows, rows, rows, bvec, vec],
        out_specs=[rows, rows, vec, vec, bvec],
        out_shape=[_sds((t, d), F32), _sds((t, d), BF16), _sds((1, d), F32), _sds((1, d), F32), _sds((nb, 1, d), F32)],
        compiler_params=_cp(1),
    )(dxo, x_in, y, gate, ln_g)


def _mod_bwd_tail(du, dz_ref, x_ref, sc_ref, dx_ref, dsc_ref, dsh_ref, first):
    @pl.when(first)
    def _():
        dsc_ref[...] = jnp.zeros_like(dsc_ref)
        dsh_ref[...] = jnp.zeros_like(dsh_ref)

    dx_ref[...] = ALPHA * dz_ref[...] + du * (1.0 + sc_ref[...])
    dsc_ref[...] += jnp.sum(du * x_ref[...], axis=0, keepdims=True)
    dsh_ref[...] += jnp.sum(du, axis=0, keepdims=True)


def ffn_bwd(dy, hg, hu, wg, wu, wd, dz, x_in, scale, name, hosted=None):
    t, d = dy.shape
    c, _, fc = wg.shape
    nb = scale.shape[0]
    tm = TOKEN_TILE
    tps = (t // nb) // tm

    def body(dy_ref, hg_ref, hu_ref, wg_ref, wu_ref, wd_ref, dz_ref, x_ref, sc_ref,
             dhg_ref, dhu_ref, act_ref, dx_ref, dsc_ref, dsh_ref, acc_ref):
        i = pl.program_id(0)
        cc = pl.program_id(1)

        @pl.when(cc == 0)
        def _():
            acc_ref[...] = jnp.zeros_like(acc_ref)

        hgv = hg_ref[...].astype(F32)
        huv = hu_ref[...].astype(F32)
        da = _dot_nt(dy_ref[...], wd_ref[cc])
        sg = jax.nn.sigmoid(hgv)
        sl = hgv * sg
        act_ref[...] = (sl * huv).astype(BF16)
        dhu = (da * sl).astype(BF16)
        dhg = (da * huv * (sg * (1.0 + hgv * (1.0 - sg)))).astype(BF16)
        dhu_ref[...] = dhu
        dhg_ref[...] = dhg
        acc_ref[...] += _dot_nt(dhg, wg_ref[cc]) + _dot_nt(dhu, wu_ref[cc])

        @pl.when(cc == c - 1)
        def _():
            _mod_bwd_tail(acc_ref[...], dz_ref, x_ref, sc_ref, dx_ref, dsc_ref, dsh_ref, i % tps == 0)

    rows = pl.BlockSpec((tm, d), lambda i, cc: (i, 0))
    bvec = pl.BlockSpec((None, 1, d), lambda i, cc: (i // tps, 0, 0))
    hspec = pl.BlockSpec((None, tm, fc), lambda i, cc: (cc, i, 0))
    wcol = _resident(wg)
    return _call(
        body, name, (t // tm, c),
        [rows, hspec, hspec, wcol, wcol, _resident(wd), rows, rows, bvec],
        [hspec, hspec, hspec, rows, bvec, bvec],
        [_sds((c, t, fc), BF16), _sds((c, t, fc), BF16), _sds((c, t, fc), BF16), _sds((t, d), F32),
         _sds((nb, 1, d), F32), _sds((nb, 1, d), F32)],
        (dy, hg, hu, wg, wu, wd, dz, x_in, scale), scratch_shapes=[pltpu.VMEM((tm, d), F32)], hosted=hosted)


def linear_nt_mod_bwd(pairs, dz, x_in, scale, name, hosted=None):
    t, d = dz.shape
    nb = scale.shape[0]
    tm = TOKEN_TILE
    tps = (t // nb) // tm
    npairs = len(pairs)

    def body(*refs):
        dh_refs = refs[:npairs]
        w_refs = refs[npairs:2 * npairs]
        dz_ref, x_ref, sc_ref, dx_ref, dsc_ref, dsh_ref = refs[2 * npairs:]
        du = None
        for (_, _, blk), dh_ref, w_ref in zip(pairs, dh_refs, w_refs):
            dh = dh_ref[...].astype(BF16)
            term = _dot_nt(dh, w_ref[...]) if blk is None else _dot(dh, w_ref[...])
            du = term if du is None else du + term
        _mod_bwd_tail(du, dz_ref, x_ref, sc_ref, dx_ref, dsc_ref, dsh_ref, pl.program_id(0) % tps == 0)

    rows = pl.BlockSpec((tm, d), lambda i: (i, 0))
    bvec = pl.BlockSpec((None, 1, d), lambda i: (i // tps, 0, 0))
    in_specs = [pl.BlockSpec((tm, dh.shape[1]), lambda i: (i, 0)) for dh, _, _ in pairs]
    for dh, w, blk in pairs:
        if blk is None:
            in_specs.append(pl.BlockSpec(w.shape, lambda i: (0, 0)))
        else:
            in_specs.append(pl.BlockSpec((dh.shape[1], d), lambda i, blk=blk: (blk, 0)))
    in_specs += [rows, rows, bvec]
    return _call(
        body, name, (t // tm,), in_specs, [rows, bvec, bvec],
        [_sds((t, d), F32), _sds((nb, 1, d), F32), _sds((nb, 1, d), F32)],
        (*[dh for dh, _, _ in pairs], *[w for _, w, _ in pairs], dz, x_in, scale), hosted=hosted)


def linear_nt_delta(dy, w_o, o, head_sel, name):
    t, d = dy.shape
    hdv = w_o.shape[0]
    tm = TOKEN_TILE

    def body(dy_ref, w_ref, o_ref, sel_ref, do_ref, dl_ref):
        do = _dot_nt(dy_ref[...], w_ref[...])
        do_ref[...] = do.astype(BF16)
        dl_ref[...] = _dot_f32(do * o_ref[...].astype(F32), sel_ref[...])

    return pl.pallas_call(
        body, name=name, grid=(t // tm,),
        in_specs=[pl.BlockSpec((tm, d), lambda i: (i, 0)), pl.BlockSpec((hdv, d), lambda i: (0, 0)),
                  pl.BlockSpec((tm, hdv), lambda i: (i, 0)), pl.BlockSpec(head_sel.shape, lambda i: (0, 0))],
        out_specs=[pl.BlockSpec((tm, hdv), lambda i: (i, 0)), pl.BlockSpec((tm, 128), lambda i: (i, 0))],
        out_shape=[_sds((t, hdv), BF16), _sds((t, 128), F32)], compiler_params=_cp(1),
    )(dy, w_o, o, head_sel)


def _attn_bwd_blocks(j, nk, tk, scale, heads):
    def block(i, carry, masked):
        new = []
        for hd, (dk_acc, dv_acc, dfk_acc) in zip(heads, carry):
            qb = hd["q"](i)
            dob = hd["do"](i)
            lse_row, dl_row = hd["rows"](i)
            st = _dot_nt(hd["k"], qb) * scale
            if hd["bias"] is not None:
                fq_row, fk_col = hd["bias"](i)
                st = st + fq_row - fk_col
            if masked:
                keep = lax.broadcasted_iota(jnp.int32, st.shape, 1) >= lax.broadcasted_iota(jnp.int32, st.shape, 0)
                st = jnp.where(keep, st, -1e30)
            pt = jnp.exp(st - lse_row)
            dv_acc = dv_acc + _dot(pt.astype(BF16), dob)
            dst = pt * (_dot_nt(hd["v"], dob) - dl_row)
            if hd["add_dfq"] is not None:
                dfk_acc = dfk_acc - jnp.sum(dst, axis=1, keepdims=True)
                hd["add_dfq"](i, jnp.sum(dst, axis=0, keepdims=True))
            dsb = (dst * scale).astype(BF16)
            dk_acc = dk_acc + _dot(dsb, qb)
            hd["add_dq"](i, _dot_tn(dsb, hd["k"]))
            new.append((dk_acc, dv_acc, dfk_acc))
        return tuple(new)

    init = tuple((jnp.zeros((tk, hd["k"].shape[1]), F32), jnp.zeros((tk, hd["v"].shape[1]), F32), jnp.zeros((tk, 1), F32))
                 for hd in heads)
    carry = block(j, init, True)
    return lax.fori_loop(j + 1, nk, lambda i, c: block(i, c, False), carry)


def fox_attn_bwd(qkv, do, cum, cum_rows, lse_rows, delta_rows, nb, name, hosted=None):
    t = qkv.shape[0]
    s = t // nb
    tk = ATTN_TILE
    nk = s // tk
    scale = FOX_HD ** -0.5

    def body(q_ref, k_ref, v_ref, do_ref, cum_ref, cr_ref, lr_ref, dr_ref, dq_ref, dk_ref, dv_ref, dfq_ref, dfk_ref):
        hg = pl.program_id(1)
        j = pl.program_id(2)

        @pl.when(j == 0)
        def _():
            dq_ref[...] = jnp.zeros_like(dq_ref)

        @pl.when((j == 0) & (hg == 0))
        def _():
            dfq_ref[...] = jnp.zeros_like(dfq_ref)
            dfk_ref[...] = jnp.zeros_like(dfk_ref)

        low = lax.broadcasted_iota(jnp.int32, (tk, 128), 1) < FOX_HD
        cum_t = cum_ref[...]

        def rows_of(i):
            return pl.ds(pl.multiple_of(i * tk, tk), tk)

        def head(a):
            hd = FOX_GROUP * hg + a
            cols = slice(128 * (a // 2), 128 * (a // 2) + 128)
            half = low if a % 2 == 0 else jnp.logical_not(low)
            kb = k_ref[:, cols]
            vb = v_ref[:, cols]
            fk = _pick_lane(cum_t, hd)

            def add_dq(i, val):
                dq_ref[rows_of(i), cols] += val

            def add_dfq(i, val):
                dfq_ref[i] = _put_row(dfq_ref[i], hd, val)

            return dict(q=lambda i: q_ref[rows_of(i), cols], do=lambda i: do_ref[rows_of(i), cols],
                        k=jnp.where(half, kb, jnp.zeros_like(kb)), v=jnp.where(half, vb, jnp.zeros_like(vb)),
                        rows=lambda i: (_pick_row(lr_ref[i], hd), _pick_row(dr_ref[i], hd)),
                        bias=lambda i: (_pick_row(cr_ref[i], hd), fk), add_dq=add_dq, add_dfq=add_dfq)

        res = _attn_bwd_blocks(j, nk, tk, scale, [head(a) for a in range(FOX_GROUP)])
        dk_ref[...] = jnp.concatenate([jnp.where(low, res[a][0], res[a + 1][0]) for a in range(0, FOX_GROUP, 2)],
                                      axis=1).astype(BF16)
        dv_ref[...] = jnp.concatenate([jnp.where(low, res[a][1], res[a + 1][1]) for a in range(0, FOX_GROUP, 2)],
                                      axis=1).astype(BF16)
        for a in range(FOX_GROUP):
            dfk_ref[j] = _put_row(dfk_ref[j], FOX_GROUP * hg + a, jnp.broadcast_to(res[a][2], (tk, 128)).T[0:1, :])

    wide = FOX_GROUP * FOX_HD
    ngroups = FOX_HEADS // FOX_GROUP
    rowsp = pl.BlockSpec((nk, 16, tk), lambda b, hg, j: (b, 0, 0))
    return _call(
        body, name, (nb, ngroups, nk),
        [pl.BlockSpec((s, wide), lambda b, hg, j: (b, hg)),
         pl.BlockSpec((tk, wide), lambda b, hg, j: (b * nk + j, ngroups + hg)),
         pl.BlockSpec((tk, wide), lambda b, hg, j: (b * nk + j, 2 * ngroups + hg)),
         pl.BlockSpec((s, wide), lambda b, hg, j: (b, hg)),
         pl.BlockSpec((tk, 128), lambda b, hg, j: (b * nk + j, 0)),
         rowsp, rowsp, rowsp],
        [pl.BlockSpec((s, wide), lambda b, hg, j: (b, hg)),
         pl.BlockSpec((tk, wide), lambda b, hg, j: (b * nk + j, hg)),
         pl.BlockSpec((tk, wide), lambda b, hg, j: (b * nk + j, hg)),
         rowsp, rowsp],
        [_sds((t, D_MODEL), F32), _sds((t, D_MODEL), BF16), _sds((t, D_MODEL), BF16),
         _sds((t // tk, 16, tk), F32), _sds((t // tk, 16, tk), F32)],
        (qkv, qkv, qkv, do, cum, cum_rows, lse_rows, delta_rows), hosted=hosted)


def mla_attn_bwd(q, kn, kr2, v, do, lse_rows, delta_rows, nb, name, hosted=None):
    t = q.shape[0]
    s = t // nb
    tk = ATTN_TILE
    nk = s // tk
    ngroups = MLA_HEADS // MLA_GROUP
    wide = MLA_GROUP * MLA_NOPE
    rwide = MLA_GROUP * MLA_ROPE
    scale = (MLA_NOPE + MLA_ROPE) ** -0.5

    def body(qn_ref, qr_ref, kn_ref, kr_ref, v_ref, do_ref, lr_ref, dr_ref, dqn_ref, dqr_ref, dkn_ref, dkr_ref, dv_ref):
        hg = pl.program_id(1)
        j = pl.program_id(2)

        @pl.when(j == 0)
        def _():
            dqn_ref[...] = jnp.zeros_like(dqn_ref)
            dqr_ref[...] = jnp.zeros_like(dqr_ref)

        low = lax.broadcasted_iota(jnp.int32, (tk, 128), 1) < MLA_ROPE
        kr = kr_ref[...]

        def rows_of(i):
            return pl.ds(pl.multiple_of(i * tk, tk), tk)

        def head(a):
            cols = slice(a * MLA_NOPE, (a + 1) * MLA_NOPE)
            rcols = slice(128 * (a // 2), 128 * (a // 2) + 128)
            mine = low if a % 2 == 0 else jnp.logical_not(low)
            hd = MLA_GROUP * hg + a

            def q_fn(i):
                qr = qr_ref[rows_of(i), rcols]
                return jnp.concatenate([qn_ref[rows_of(i), cols], jnp.where(mine, qr, jnp.zeros_like(qr))], axis=1)

            def add_dq(i, val):
                dqn_ref[rows_of(i), cols] += val[:, :MLA_NOPE]
                dqr_ref[rows_of(i), cols] += val[:, MLA_NOPE:]

            return dict(q=q_fn, do=lambda i: do_ref[rows_of(i), cols], k=jnp.concatenate([kn_ref[:, cols], kr], axis=1),
                        v=v_ref[:, cols], rows=lambda i: (_pick_row(lr_ref[i], hd), _pick_row(dr_ref[i], hd)),
                        bias=None, add_dq=add_dq, add_dfq=None)

        res = _attn_bwd_blocks(j, nk, tk, scale, [head(a) for a in range(MLA_GROUP)])
        dkn_ref[...] = jnp.concatenate([r[0][:, :MLA_NOPE] for r in res], axis=1).astype(BF16)
        dkr_ref[...] = jnp.concatenate([r[0][:, MLA_NOPE:] for r in res], axis=1).astype(BF16)
        dv_ref[...] = jnp.concatenate([r[1] for r in res], axis=1).astype(BF16)

    full = pl.BlockSpec((s, wide), lambda b, hg, j: (b, hg))
    blk = pl.BlockSpec((tk, wide), lambda b, hg, j: (b * nk + j, hg))
    rowsp = pl.BlockSpec((nk, 16, tk), lambda b, hg, j: (b, 0, 0))
    total = MLA_HEADS * MLA_V
    rope0 = MLA_HEADS * MLA_NOPE // rwide
    return _call(
        body, name, (nb, ngroups, nk),
        [full, pl.BlockSpec((s, rwide), lambda b, hg, j: (b, rope0 + hg)), blk,
         pl.BlockSpec((tk, 128), lambda b, hg, j: (b * nk + j, 0)), blk, full, rowsp, rowsp],
        [full, full, blk, blk, blk],
        [_sds((t, total), F32), _sds((t, total), F32), _sds((t, total), BF16), _sds((t, total), BF16),
         _sds((t, total), BF16)],
        (q, q, kn, kr2, v, do, lse_rows, delta_rows), hosted=hosted)


def mla_mid_bwd(dqn, dqr, dkn, dv, dkr_heads, h, g_q, g_kv, w_uq, w_uk, w_uv, cos8, sin8, cos64, sin64s, swap64,
                heads_to_rope, head_sum, name):
    t = h.shape[0]
    tm = TOKEN_TILE
    hq = MLA_HEADS * MLA_NOPE
    hr = MLA_HEADS * MLA_ROPE // 2
    nq = w_uq.shape[1]

    def body(dqn_ref, dqr_ref, dkn_ref, dv_ref, dkr_ref, h_ref, gq_ref, gkv_ref, wuq_ref, wuk_ref, wuv_ref,
             c8_ref, s8_ref, c64_ref, s64_ref, sw_ref, hp_ref, hs_ref, dh_ref, dqp_ref, dgq_ref, dgkv_ref):
        @pl.when(pl.program_id(0) == 0)
        def _():
            dgq_ref[...] = jnp.zeros_like(dgq_ref)
            dgkv_ref[...] = jnp.zeros_like(dgkv_ref)

        drot = _dot(dqr_ref[...].astype(BF16), hp_ref[...])
        o1 = drot[:, :hr]
        o2 = drot[:, hr:]
        cs = c8_ref[...]
        sn = s8_ref[...]
        dqp = jnp.concatenate([dqn_ref[...].astype(BF16), (o1 * cs + o2 * sn).astype(BF16),
                               (o2 * cs - o1 * sn).astype(BF16)], axis=1)
        dqp_ref[...] = dqp
        dcq = _dot_nt(dqp, wuq_ref[...])
        dckv = _dot_nt(dkn_ref[...], wuk_ref[...]) + _dot_nt(dv_ref[...], wuv_ref[...])
        hh = h_ref[...]

        def rms_bwd(hpart, g, dc, dg_ref):
            hhat, rstd = _rms(hpart, None)
            dg_ref[...] += jnp.sum(dc * hhat, axis=0, keepdims=True)
            dcg = dc * g
            return rstd * (dcg - hhat * jnp.mean(dcg * hhat, axis=-1, keepdims=True))

        dhq = rms_bwd(hh[:, :MLA_QR], gq_ref[...], dcq, dgq_ref)
        dhkv = rms_bwd(hh[:, MLA_QR:MLA_QR + MLA_KVR], gkv_ref[...], dckv, dgkv_ref)
        dkr = _dot(dkr_ref[...], hs_ref[...])
        dkr_pre = dkr * c64_ref[...] + _dot_f32(dkr * s64_ref[...], sw_ref[...])
        dh_ref[...] = jnp.concatenate([dhq, dhkv, dkr_pre], axis=1).astype(BF16)

    def rows(n):
        return pl.BlockSpec((tm, n), lambda i: (i, 0))

    def whole(a):
        return pl.BlockSpec(a.shape, lambda i: (0,) * a.ndim)

    return pl.pallas_call(
        body, name=name, grid=(t // tm,),
        in_specs=[rows(hq), rows(hq), rows(hq), rows(hq), rows(hq), rows(h.shape[1]), whole(g_q), whole(g_kv),
                  whole(w_uq), whole(w_uk), whole(w_uv), rows(hr), rows(hr), rows(MLA_ROPE), rows(MLA_ROPE),
                  whole(swap64), whole(heads_to_rope), whole(head_sum)],
        out_specs=[rows(h.shape[1]), rows(nq), pl.BlockSpec((1, MLA_QR), lambda i: (0, 0)),
                   pl.BlockSpec((1, MLA_KVR), lambda i: (0, 0))],
        out_shape=[_sds((t, h.shape[1]), BF16), _sds((t, nq), BF16), _sds((1, MLA_QR), F32), _sds((1, MLA_KVR), F32)],
        compiler_params=_cp(1),
    )(dqn, dqr, dkn, dv, dkr_heads, h, g_q, g_kv, w_uq, w_uk, w_uv, cos8, sin8, cos64, sin64s, swap64,
      heads_to_rope, head_sum)


def fox_gate_bwd(dcum, hf, b_f, triu, n_batch, name):
    t, n = hf.shape
    blk = triu.shape[0]
    nb = (t // n_batch) // blk

    def body(dc_ref, hf_ref, b_ref, tri_ref, o_ref, db_ref, carry_ref):
        @pl.when(pl.program_id(1) == 0)
        def _():
            carry_ref[...] = jnp.zeros_like(carry_ref)

        @pl.when((pl.program_id(0) == 0) & (pl.program_id(1) == 0))
        def _():
            db_ref[...] = jnp.zeros_like(db_ref)

        rc = _dot_f32(tri_ref[...], dc_ref[...]) + carry_ref[...]
        carry_ref[...] = rc[0:1, :]
        dhf = rc * jax.nn.sigmoid(-(hf_ref[...] + b_ref[...]))
        o_ref[...] = dhf.astype(BF16)
        db_ref[...] += jnp.sum(dhf, axis=0, keepdims=True)

    rev = pl.BlockSpec((blk, n), lambda bb, i: (bb * nb + nb - 1 - i, 0))
    return pl.pallas_call(
        body, name=name, grid=(n_batch, nb),
        in_specs=[rev, rev, pl.BlockSpec((1, n), lambda bb, i: (0, 0)), pl.BlockSpec((blk, blk), lambda bb, i: (0, 0))],
        out_specs=[rev, pl.BlockSpec((1, n), lambda bb, i: (0, 0))],
        out_shape=[_sds((t, n), BF16), _sds((1, n), F32)], scratch_shapes=[pltpu.VMEM((1, n), F32)],
        compiler_params=_cp(2),
    )(dcum, hf, b_f, triu)


def wgrad(a, bm, name, with_bf16=False, bt=WGRAD_TOKENS):
    ca, t, kd = a.shape
    cb, _, nd = bm.shape
    c = max(ca, cb)
    bn = nd
    if nd > 1024 and nd % 1024 == 0:
        bn = 1024
    nsteps = t // bt

    def body(a_ref, b_ref, o_ref, *rest):
        @pl.when(pl.program_id(2) == 0)
        def _():
            o_ref[...] = jnp.zeros_like(o_ref)

        o_ref[...] += _dot_tn(a_ref[...].astype(BF16), b_ref[...].astype(BF16))
        if with_bf16:
            @pl.when(pl.program_id(2) == nsteps - 1)
            def _():
                rest[0][...] = o_ref[...].astype(BF16)

    out_spec = pl.BlockSpec((None, kd, bn), lambda cc, n, tt: (cc, 0, n))
    res = pl.pallas_call(
        body, name=name, grid=(c, nd // bn, nsteps),
        in_specs=[pl.BlockSpec((None, bt, kd), lambda cc, n, tt: (cc if ca > 1 else 0, tt, 0)),
                  pl.BlockSpec((None, bt, bn), lambda cc, n, tt: (cc if cb > 1 else 0, tt, n))],
        out_specs=[out_spec, out_spec] if with_bf16 else out_spec,
        out_shape=[_sds((c, kd, nd), F32), _sds((c, kd, nd), BF16)] if with_bf16 else _sds((c, kd, nd), F32),
        compiler_params=_cp(3),
    )(a, bm)
    return res


def ada_mod_part(c_all, ada_w, name):
    nl, d, n = ada_w.shape
    rows = c_all.shape[0]
    tn = 512

    def body(c_ref, w_ref, o_ref):
        cv = c_ref[...]
        act = (cv * jax.nn.sigmoid(cv)).astype(BF16)
        o_ref[...] = _dot(act, w_ref[...].astype(BF16))

    return pl.pallas_call(
        body, name=name, grid=(nl, n // tn),
        in_specs=[pl.BlockSpec((rows, d), lambda l, j: (0, 0)), pl.BlockSpec((None, d, tn), lambda l, j: (l, 0, j))],
        out_specs=pl.BlockSpec((None, rows, tn), lambda l, j: (l, 0, j)),
        out_shape=_sds((nl, rows, n), F32), compiler_params=_cp(2),
    )(c_all, ada_w)


def ada_grad(c_all_t, dmod, name):
    nl, rows, n = dmod.shape
    d = c_all_t.shape[0]
    tn = 512

    def body(c_ref, dm_ref, o_ref):
        cv = c_ref[...]
        act = (cv * jax.nn.sigmoid(cv)).astype(BF16)
        o_ref[...] = _dot(act, dm_ref[...].astype(BF16))

    return pl.pallas_call(
        body, name=name, grid=(nl, n // tn),
        in_specs=[pl.BlockSpec((d, rows), lambda l, j: (0, 0)), pl.BlockSpec((None, rows, tn), lambda l, j: (l, 0, j))],
        out_specs=pl.BlockSpec((None, d, tn), lambda l, j: (l, 0, j)),
        out_shape=_sds((nl, d, n), F32), compiler_params=_cp(2),
    )(c_all_t, dmod)


def sum_leading(a, name):
    g, r, n = a.shape

    def body(a_ref, o_ref):
        acc = a_ref[0]
        for kk in range(1, g):
            acc = acc + a_ref[kk]
        o_ref[...] = acc

    return pl.pallas_call(
        body, name=name, grid=(1,), in_specs=[pl.BlockSpec((g, r, n), lambda i: (0, 0, 0))],
        out_specs=pl.BlockSpec((r, n), lambda i: (0, 0)), out_shape=_sds((r, n), F32), compiler_params=_cp(1),
    )(a)


def adamw(w, g, m, v, name):
    r, n = w.shape
    br = r
    for cand in (512, 256, 128, 64, 32, 16, 8):
        if r % cand == 0 and r > cand and cand * n * 4 <= ADAMW_BLOCK_BYTES:
            br = cand
            break
    c1 = 1.0 - ADAM_B1 ** ADAM_STEP
    c2 = 1.0 - ADAM_B2 ** ADAM_STEP

    def body(w_ref, g_ref, m_ref, v_ref, d_ref, mo_ref, vo_ref):
        gv = g_ref[...]
        mn = ADAM_B1 * m_ref[...] + (1.0 - ADAM_B1) * gv
        vn = ADAM_B2 * v_ref[...] + (1.0 - ADAM_B2) * (gv * gv)
        mo_ref[...] = mn
        vo_ref[...] = vn
        d_ref[...] = -ADAM_LR * ((mn / c1) / (jnp.sqrt(vn / c2) + ADAM_EPS) + ADAM_WD * w_ref[...])

    spec = pl.BlockSpec((br, n), lambda i: (i, 0))
    return pl.pallas_call(
        body, name=name, grid=(r // br,), in_specs=[spec] * 4, out_specs=[spec] * 3,
        out_shape=[_sds((r, n), F32)] * 3, compiler_params=_cp(1),
    )(w, g, m, v)


def all_gather8(x_blk, name):
    m_per, n = x_blk.shape

    def body(x_ref, out_ref, send_sems, recv_sems, local_sem):
        x, y, c = _place()
        me, sibling = (x, y, c), (x, y, 1 - c)
        chips = [(1 - x, y), (x, 1 - y), (1 - x, 1 - y)]

        def rows(px, py, pc):
            return out_ref.at[pl.ds((4 * px + 2 * py + pc) * m_per, m_per), :]

        def copy(k, block, to, src=None):
            return pltpu.make_async_remote_copy(
                src_ref=rows(*block) if src is None else src, dst_ref=rows(*block),
                send_sem=send_sems.at[k], recv_sem=recv_sems.at[k], device_id=to, device_id_type=MESH)

        mine = pltpu.make_async_copy(x_ref, rows(*me), local_sem)
        mine.start()
        first = [copy(0, me, sibling, src=x_ref)]
        first += [copy(1 + j, me, (*chip, c), src=x_ref) for j, chip in enumerate(chips)]
        for cp in first:
            cp.start()
        passed = [copy(4 + j, (*chip, c), sibling) for j, chip in enumerate(chips)]
        for j, chip in enumerate(chips):
            copy(1 + j, (*chip, c), me).wait_recv()
            passed[j].start()
        copy(0, sibling, me).wait_recv()
        for j, chip in enumerate(chips):
            copy(4 + j, (*chip, 1 - c), me).wait_recv()
        for cp in first + passed:
            cp.wait_send()
        mine.wait()

    return pl.pallas_call(
        body, name=name, out_shape=_sds((8 * m_per, n), x_blk.dtype),
        in_specs=[pl.BlockSpec(memory_space=pltpu.VMEM)], out_specs=pl.BlockSpec(memory_space=pltpu.VMEM),
        scratch_shapes=[pltpu.SemaphoreType.DMA((7,)), pltpu.SemaphoreType.DMA((7,)), pltpu.SemaphoreType.DMA],
        compiler_params=pltpu.CompilerParams(vmem_limit_bytes=VMEM_LIMIT),
    )(x_blk)


def _gather_comm(shards):
    nt = len(shards)

    def parts(w_refs, out_refs, sems, finishing):
        send_sems, recv_sems, own_send, own_recv = sems
        x, y, c = _place()
        sibling = (x, y, 1 - c)
        chips = [(1 - x, y), (x, 1 - y), (1 - x, 1 - y)]

        def copy(t, k, block, to, src=None):
            px, py, hh = block
            dst = out_refs[t].at[2 * px + py, hh]
            return pltpu.make_async_remote_copy(
                src_ref=dst if src is None else src, dst_ref=dst,
                send_sem=send_sems.at[6 * t + k], recv_sem=recv_sems.at[6 * t + k], device_id=to, device_id_type=MESH)

        own = [pltpu.make_async_remote_copy(
            src_ref=w_refs[t], dst_ref=out_refs[t].at[2 * x + y], send_sem=own_send.at[t], recv_sem=own_recv.at[t],
            device_id=sibling, device_id_type=MESH) for t in range(nt)]
        first = [copy(t, j, (x, y, c), (*chip, c), src=w_refs[t].at[c]) for t in range(nt) for j, chip in enumerate(chips)]
        if not finishing:
            return own, first
        landed = [copy(t, j, (*chip, c), (x, y, c)) for t in range(nt) for j, chip in enumerate(chips)]
        passed = [copy(t, 3 + j, (*chip, c), sibling) for t in range(nt) for j, chip in enumerate(chips)]
        from_sibling = [copy(t, 3 + j, (*chip, 1 - c), (x, y, c)) for t in range(nt) for j, chip in enumerate(chips)]
        return own, first, landed, passed, from_sibling

    def start(w_refs, out_refs, sems):
        own, first = parts(w_refs, out_refs, sems, False)
        for cp in own + first:
            cp.start()

    def finish(w_refs, out_refs, sems):
        own, first, landed, passed, from_sibling = parts(w_refs, out_refs, sems, True)
        for arrived, fwd in zip(landed, passed):
            arrived.wait_recv()
            fwd.start()
        for cp in from_sibling:
            cp.wait_recv()
        for cp in first + passed:
            cp.wait_send()
        for cp in own:
            cp.wait()

    sems = [pltpu.SemaphoreType.DMA((6 * nt,)), pltpu.SemaphoreType.DMA((6 * nt,)),
            pltpu.SemaphoreType.DMA((nt,)), pltpu.SemaphoreType.DMA((nt,))]
    return _Hosted(list(shards), [_sds((N_CHIPS, *w.shape), w.dtype) for w in shards], sems, start, finish)


def all_gather_chips(shards, name):
    comm = _gather_comm(shards)
    nt = len(shards)

    def body(*refs):
        comm.start(refs[:nt], refs[nt:2 * nt], refs[2 * nt:])
        comm.finish(refs[:nt], refs[nt:2 * nt], refs[2 * nt:])

    hbm = pl.BlockSpec(memory_space=pl.ANY)
    return pl.pallas_call(body, name=name, out_shape=comm.out_shape, in_specs=[hbm] * nt, out_specs=[hbm] * nt,
                          scratch_shapes=comm.sems)(*shards)


def _row_block(r, n, itemsize):
    best = None
    for br in range(16, r + 1, 16):
        if r % br == 0 and br * n * itemsize <= COMM_BLOCK_BYTES:
            best = br
    return r if best is None else best


def _scatter_comm(parts):
    nt = len(parts)

    def copies(p_refs, b_refs, sems, arriving):
        send_sems, recv_sems = sems
        x, y, c = _place()
        me = 4 * x + 2 * y + c
        cps = []
        for t in range(nt):
            for r in range(1, 8):
                tx = 1 - x if r & 4 else x
                ty = 1 - y if r & 2 else y
                tc = 1 - c if r & 1 else c
                src, dst = (2 * x + y, c), 4 * tx + 2 * ty + tc
                if not arriving:
                    src, dst = (2 * tx + ty, tc), me
                cps.append(pltpu.make_async_remote_copy(
                    src_ref=p_refs[t].at[src], dst_ref=b_refs[t].at[dst], send_sem=send_sems.at[7 * t + r - 1],
                    recv_sem=recv_sems.at[7 * t + r - 1], device_id=(tx, ty, tc), device_id_type=MESH))
        return cps

    def start(p_refs, b_refs, sems):
        for cp in copies(p_refs, b_refs, sems, False):
            cp.start()

    def finish(p_refs, b_refs, sems):
        for cp in copies(p_refs, b_refs, sems, True):
            cp.wait_recv()
        for cp in copies(p_refs, b_refs, sems, False):
            cp.wait_send()

    sems = [pltpu.SemaphoreType.DMA((7 * nt,)), pltpu.SemaphoreType.DMA((7 * nt,))]
    return _Hosted(list(parts), [_sds((2 * N_CHIPS, *p.shape[2:]), p.dtype) for p in parts], sems, start, finish)


def sum_devices(own, recv, place, name, slot=(0, 1, None)):
    _, _, r, n = own.shape
    layer, n_layers, buf = slot
    br = _row_block(r, n, 4 * 8)

    def body(p_ref, o_ref, *rest):
        acc = o_ref[...]
        for kk in range(7):
            acc = acc + rest[kk][...].astype(F32)
        rest[-1][...] = acc

    def arrived(rel):
        return pl.BlockSpec((None, br, n), lambda i, pref: (jnp.bitwise_xor(pref[0], rel), i, 0))

    in_specs = [pl.BlockSpec((None, None, br, n), lambda i, pref: (pref[2], pref[1], i, 0))]
    in_specs += [arrived(rel) for rel in range(1, 8)]
    args = [own] + [recv] * 7
    aliases = {}
    if buf is not None:
        in_specs.append(pl.BlockSpec(memory_space=pl.ANY))
        args.append(buf)
        aliases = {9: 0}
    return pl.pallas_call(
        body, name=name,
        grid_spec=pltpu.PrefetchScalarGridSpec(
            num_scalar_prefetch=1, grid=(r // br,), in_specs=in_specs,
            out_specs=pl.BlockSpec((None, None, br, n), lambda i, pref: (layer, pref[1], i, 0))),
        out_shape=_sds((n_layers, 2, r, n), F32), input_output_aliases=aliases, compiler_params=_cp(1),
    )(place, *args)


def sibling_join_halves(bufs, name):
    nt = len(bufs)
    layers = [bf.shape[0] for bf in bufs]
    first = [sum(layers[:t]) for t in range(nt)]

    def body(*refs):
        o_refs = refs[nt:2 * nt]
        send_sems, recv_sems = refs[2 * nt:]
        x, y, c = _place()

        def copy(t, l, hh):
            return pltpu.make_async_remote_copy(
                src_ref=o_refs[t].at[l, hh], dst_ref=o_refs[t].at[l, hh], send_sem=send_sems.at[first[t] + l],
                recv_sem=recv_sems.at[first[t] + l], device_id=(x, y, 1 - c), device_id_type=MESH)

        cps = [copy(t, l, c) for t in range(nt) for l in range(layers[t])]
        for cp in cps:
            cp.start()
        for t in range(nt):
            for l in range(layers[t]):
                copy(t, l, 1 - c).wait_recv()
        for cp in cps:
            cp.wait_send()

    hbm = pl.BlockSpec(memory_space=pl.ANY)
    return pl.pallas_call(
        body, name=name, out_shape=[_sds(bf.shape, bf.dtype) for bf in bufs],
        in_specs=[hbm] * nt, out_specs=[hbm] * nt, input_output_aliases={t: t for t in range(nt)},
        scratch_shapes=[pltpu.SemaphoreType.DMA((sum(layers),)), pltpu.SemaphoreType.DMA((sum(layers),))],
    )(*bufs)


_SHARD_KIND = {"mla_w_in": "rows", "mla_w_uq": "cols", "mla_w_uk": "cols", "mla_w_uv": "cols", "mla_w_o": "rows",
               "fox_w_in": "cols", "fox_w_o": "rows", "ffn_w_gate": "chunk", "ffn_w_up": "chunk", "ffn_w_down": "chunk"}
_PACKED = tuple(_SHARD_KIND)
_TRANSPOSED = ("ffn_w_gate", "ffn_w_up", "fox_w_in")


def _halves(shard):
    if shard.ndim == 3 and shard.shape[0] == 2:
        return shard
    r, n = shard.shape[-2:]
    return shard.reshape(2, r // 2, n)


def _cols_to_full(g):
    return jnp.transpose(g, (1, 0, 2)).reshape(g.shape[1], -1)


def _full_to_cols(w):
    k, n4 = w.shape
    return jnp.transpose(w.reshape(k, N_CHIPS, n4 // N_CHIPS), (1, 0, 2))


def _uq_perm():
    per = MLA_NOPE + MLA_ROPE
    half = MLA_ROPE // 2
    nope = [h * per + d for h in range(MLA_HEADS) for d in range(MLA_NOPE)]
    r1 = [h * per + MLA_NOPE + r for h in range(MLA_HEADS) for r in range(half)]
    r2 = [h * per + MLA_NOPE + half + r for h in range(MLA_HEADS) for r in range(half)]
    perm = np.array(nope + r1 + r2, dtype=np.int32)
    return perm, np.argsort(perm).astype(np.int32)


def _rope_matrices():
    half = MLA_ROPE // 2
    nr = MLA_HEADS * MLA_ROPE
    to_heads = np.zeros((nr, nr), np.float32)
    from_heads = np.zeros((MLA_HEADS * 128, nr), np.float32)
    for e in range(2):
        for h in range(MLA_HEADS):
            for r in range(half):
                to_heads[e * MLA_HEADS * half + h * half + r, h * MLA_ROPE + e * half + r] = 1.0
                from_heads[h * 128 + e * half + r, e * MLA_HEADS * half + h * half + r] = 1.0
    head_sum = np.tile(np.eye(MLA_ROPE, dtype=np.float32), (2 * MLA_HEADS, 1))
    dup = np.concatenate([np.eye(MLA_ROPE, dtype=np.float32)] * 2, axis=1)
    return to_heads, from_heads, head_sum, dup


def _ffn_weights(gathered):
    return tuple(g.reshape(N_CHIPS, 2 * g.shape[2], g.shape[3]) for g in gathered)


def _fox_weights(gathered):
    w_in, w_o = gathered
    w_in = jnp.transpose(w_in, (0, 2, 1, 3)).reshape(N_CHIPS * w_in.shape[2], 2 * w_in.shape[3])
    return w_in, w_o.reshape(-1, w_o.shape[-1])


def _local_step(x, positions, target, mods, wts, ln_g, ln_b, mla_g_q, mla_g_kv, fox_b_f, shards=None):
    nb, s, d = x.shape
    t = nb * s
    x0 = x.reshape(t, d)
    tgt = target.reshape(t, d)
    perm, inv_perm = _uq_perm()

    half = MLA_ROPE // 2
    inv_freq = ROPE_THETA ** (-jnp.arange(half, dtype=F32) / half)
    ang = positions.astype(F32).reshape(t, 1) * inv_freq
    cos, sin = jnp.cos(ang), jnp.sin(ang)
    cos8, sin8 = jnp.tile(cos, (1, MLA_HEADS)), jnp.tile(sin, (1, MLA_HEADS))
    cos64 = jnp.concatenate([cos, cos], axis=1)
    sin64s = jnp.concatenate([-sin, sin], axis=1)
    swap64 = jnp.asarray(np.roll(np.eye(MLA_ROPE, dtype=np.float32), half, axis=1))
    to_heads, from_heads, head_sum, dup = _rope_matrices()
    to_heads, from_heads = jnp.asarray(to_heads, dtype=BF16), jnp.asarray(from_heads, dtype=BF16)
    head_sum, dup = jnp.asarray(head_sum, dtype=BF16), jnp.asarray(dup, dtype=BF16)
    sel_mla = jnp.asarray(np.pad(np.kron(np.eye(MLA_HEADS, dtype=np.float32), np.ones((MLA_V, 1), np.float32)),
                                 ((0, 0), (0, 128 - MLA_HEADS))))
    sel_fox = jnp.asarray(np.pad(np.kron(np.eye(FOX_HEADS, dtype=np.float32), np.ones((FOX_HD, 1), np.float32)),
                                 ((0, 0), (0, 128 - FOX_HEADS))))
    tri = jnp.asarray(np.tril(np.ones((128, 128), np.float32)))
    triu = jnp.asarray(np.triu(np.ones((128, 128), np.float32)))
    onehot16 = jnp.asarray(np.eye(16, 128, dtype=np.float32))

    def vec(a):
        return a.reshape(1, -1)

    def carried(key):
        return None if shards is None else _gather_comm(shards[key])

    def split(res):
        return (res, None) if shards is None else res

    w_uq_p = wts["mla_w_uq"][:, perm]
    b_f_pad = jnp.pad(fox_b_f.reshape(1, -1), ((0, 0), (0, 128 - FOX_HEADS)))

    sh_a, sc_a, gt_a, sh_f, sc_f, gt_f = mods[0]
    h_in, u_m = mod_linear(x0, sh_a, sc_a, wts["mla_w_in"], F32, "mla_in", emit_u=True)
    q_m, kn_m, v_m, kr2_m, cq_m, ckv_m = mla_mid_fwd(
        h_in, vec(mla_g_q), vec(mla_g_kv), w_uq_p, wts["mla_w_uk"], wts["mla_w_uv"], cos8, sin8, cos64, sin64s, swap64,
        to_heads, dup, "mla_mid")
    (o_m, lse_m), got = split(mla_attn_fwd(q_m, kn_m, kr2_m, v_m, nb, "mla_attn", hosted=carried("ffn0")))
    ffn0_w = wts["ffn"][0] if got is None else _ffn_weights(got)
    y0, x1 = linear_resid_ln(o_m, wts["mla_w_o"], x0, gt_a, vec(ln_g[0, 0]), vec(ln_b[0, 0]), "mla_out")
    (u_f0, hg0, hu0, y1, x2), got = split(ffn_fwd(x1, sh_f, sc_f, gt_f, *ffn0_w, vec(ln_g[0, 1]), vec(ln_b[0, 1]), "ffn0",
                                                  hosted=carried("fox")))
    fox_w_in_t, fox_w_o = (wts["fox_w_in"].T, wts["fox_w_o"]) if got is None else _fox_weights(got)
    fox_w_f_t = jnp.pad(fox_w_in_t[3 * d:], ((0, 128 - FOX_HEADS), (0, 0)))
    sh_a1, sc_a1, gt_a1, sh_f1, sc_f1, gt_f1 = mods[1]
    qkv, u_x = mod_linear(x2, sh_a1, sc_a1, fox_w_in_t, BF16, "fox_qkv", tn=1024, emit_u=True, w_rows=3 * d)
    hf = mod_linear(x2, sh_a1, sc_a1, fox_w_f_t, F32, "fox_f", w_rows=128)
    cum = fox_gate_fwd(hf, b_f_pad, tri, nb, "fox_gate")
    cum_rows = rows16(cum, "fox_cum_rows")
    (o_x, lse_x), got = split(fox_attn_fwd(qkv, cum, cum_rows, nb, "fox_attn", hosted=carried("ffn1")))
    ffn1_w = wts["ffn"][1] if got is None else _ffn_weights(got)
    y2, x3 = linear_resid_ln(o_x, fox_w_o, x2, gt_a1, vec(ln_g[1, 0]), vec(ln_b[1, 0]), "fox_out")
    u_f1, hg1, hu1, y3, x4 = ffn_fwd(x3, sh_f1, sc_f1, gt_f1, *ffn1_w, vec(ln_g[1, 1]), vec(ln_b[1, 1]), "ffn1")
    dx4, sq_err = loss_grad(x4, tgt, "loss")
    loss_part = 0.5 * jnp.sum(sq_err) / d

    parts, recv = {}, {}

    def halves_of(g):
        return g.reshape(N_CHIPS, 2, g.shape[1] // 2, g.shape[2])

    def scatter(keys, sent):
        return None if shards is None else _scatter_comm([sent[k] for k in keys])

    def landed(keys, got):
        if got is not None:
            recv.update(zip(keys, got))

    def ffn_grads(layer, u, dhg, dhu, act, dy):
        sent = {}
        for n, (a_op, b_op) in (("ffn_w_gate", (dhg, u[None])), ("ffn_w_up", (dhu, u[None])), ("ffn_w_down", (act, dy[None]))):
            g32, g16 = wgrad(a_op, b_op, "ffn%d_d%s" % (layer, n[4:]), with_bf16=True)
            parts["%s/%d" % (n, layer)], sent["%s/%d" % (n, layer)] = halves_of(g32), halves_of(g16)
        return sent

    dz3, dy3, dg11, db11, dgt_f1 = ln_bwd(dx4, x3, y3, gt_f1, vec(ln_g[1, 1]), "ffn1_ln_bwd")
    dhg1, dhu1, act1, dx3, dsc_f1, dsh_f1 = ffn_bwd(dy3, hg1, hu1, *ffn1_w, dz3, x3, sc_f1, "ffn1_bwd")
    sent = ffn_grads(1, u_f1, dhg1, dhu1, act1, dy3)
    dz2, dy2, dg10, db10, dgt_a1 = ln_bwd(dx3, x2, y2, gt_a1, vec(ln_g[1, 0]), "fox_ln_bwd")
    do_x, delta_x = linear_nt_delta(dy2, fox_w_o, o_x, sel_fox, "fox_out_bwd")
    (dq_x, dk_x, dv_x, dfq_x, dfk_x), got = split(fox_attn_bwd(
        qkv, do_x, cum, cum_rows, rows16(lse_x, "fox_lse_rows"), rows16(delta_x, "fox_delta_rows"), nb, "fox_attn_bwd",
        hosted=scatter(list(sent), sent)))
    landed(list(sent), got)
    dcum = tokens128(dfq_x + dfk_x, onehot16, "fox_dcum")
    dhf, dbf = fox_gate_bwd(dcum, hf, b_f_pad, triu, nb, "fox_gate_bwd")
    fox_d = [("q", dq_x), ("k", dk_x), ("v", dv_x)]
    dx2, dsc_a1, dsh_a1 = linear_nt_mod_bwd(
        [(dh, fox_w_in_t, i) for i, (_, dh) in enumerate(fox_d)] + [(dhf, fox_w_f_t, 0)], dz2, x2, sc_a1, "fox_in_bwd")
    dw_in_t = [wgrad(dh[None], u_x[None], "fox_dw" + tag)[0] for tag, dh in fox_d]
    dw_in_t.append(wgrad(dhf[None], u_x[None], "fox_dwf")[0][:FOX_HEADS])
    dw_in_t = jnp.concatenate(dw_in_t, axis=0).reshape(N_CHIPS, -1, 2, d // 2)
    parts["fox_w_in"] = jnp.transpose(dw_in_t, (0, 2, 1, 3))
    parts["fox_w_o"] = wgrad(o_x[None], dy2[None], "fox_dwo")[0].reshape(N_CHIPS, 2, -1, d)
    sent = {k: parts[k].astype(BF16) for k in ("fox_w_in", "fox_w_o")}
    dz1, dy1, dg01, db01, dgt_f0 = ln_bwd(dx2, x1, y1, gt_f, vec(ln_g[0, 1]), "ffn0_ln_bwd")
    (dhg0, dhu0, act0, dx1, dsc_f0, dsh_f0), got = split(ffn_bwd(dy1, hg0, hu0, *ffn0_w, dz1, x1, sc_f, "ffn0_bwd",
                                                                 hosted=scatter(list(sent), sent)))
    landed(list(sent), got)
    sent = ffn_grads(0, u_f0, dhg0, dhu0, act0, dy1)
    dz0, dy0, dg00, db00, dgt_a0 = ln_bwd(dx1, x0, y0, gt_a, vec(ln_g[0, 0]), "mla_ln_bwd")
    do_m, delta_m = linear_nt_delta(dy0, wts["mla_w_o"], o_m, sel_mla, "mla_out_bwd")
    parts["mla_w_o"] = wgrad(o_m[None], dy0[None], "mla_dwo")[0].reshape(N_CHIPS, 2, -1, d)
    sent["mla_w_o"] = parts["mla_w_o"].astype(BF16)
    (dqn_m, dqr_m, dkn_m, dkr_m, dv_m), got = split(mla_attn_bwd(
        q_m, kn_m, kr2_m, v_m, do_m, rows16(lse_m, "mla_lse_rows"), rows16(delta_m, "mla_delta_rows"), nb,
        "mla_attn_bwd", hosted=scatter(list(sent), sent)))
    landed(list(sent), got)
    dh_in, dq_pre, dgq, dgkv = mla_mid_bwd(
        dqn_m, dqr_m, dkn_m, dv_m, dkr_m, h_in, vec(mla_g_q), vec(mla_g_kv), w_uq_p, wts["mla_w_uk"],
        wts["mla_w_uv"], cos8, sin8, cos64, sin64s, swap64, from_heads, head_sum, "mla_mid_bwd")
    parts["mla_w_uq"] = halves_of(_full_to_cols(wgrad(cq_m[None], dq_pre[None], "mla_dwuq")[0][:, inv_perm]))
    parts["mla_w_uk"] = halves_of(_full_to_cols(wgrad(ckv_m[None], dkn_m[None], "mla_dwuk")[0]))
    parts["mla_w_uv"] = halves_of(_full_to_cols(wgrad(ckv_m[None], dv_m[None], "mla_dwuv")[0]))
    parts["mla_w_in"] = wgrad(u_m[None], dh_in[None], "mla_dwin")[0].reshape(N_CHIPS, 2, -1, h_in.shape[1])
    sent = {k: parts[k].astype(BF16) for k in ("mla_w_in", "mla_w_uq", "mla_w_uk", "mla_w_uv")}
    (dx0, dsc_a0, dsh_a0), got = split(linear_nt_mod_bwd([(dh_in, wts["mla_w_in"], None)], dz0, x0, sc_a, "mla_in_bwd",
                                                         hosted=scatter(list(sent), sent)))
    landed(list(sent), got)

    dmods = [(dsh_a0, dsc_a0, dgt_a0, dsh_f0, dsc_f0, dgt_f0), (dsh_a1, dsc_a1, dgt_a1, dsh_f1, dsc_f1, dgt_f1)]
    d_ln_g = jnp.stack([jnp.concatenate([dg00, dg01], axis=0), jnp.concatenate([dg10, dg11], axis=0)])
    d_ln_b = jnp.stack([jnp.concatenate([db00, db01], axis=0), jnp.concatenate([db10, db11], axis=0)])
    return loss_part, dx0.reshape(nb, s, d), (parts, recv), dmods, d_ln_g, d_ln_b, dgq, dgkv, dbf[:, :FOX_HEADS]


def _pad_rows(a, rows):
    return jnp.pad(a, ((0, rows - a.shape[0]), (0, 0)))


def kernel(x, c, positions, mla_w_in, mla_g_q, mla_w_uq, mla_g_kv, mla_w_uk, mla_w_uv, mla_w_o, fox_w_in, fox_b_f, fox_w_o, ada_w, ada_b, ffn_w_gate, ffn_w_up, ffn_w_down, ln_g, ln_b, loss_target, m_mla_w_in, m_mla_g_q, m_mla_w_uq, m_mla_g_kv, m_mla_w_uk, m_mla_w_uv, m_mla_w_o, m_fox_w_in, m_fox_b_f, m_fox_w_o, m_ada_w, m_ada_b, m_ffn_w_gate, m_ffn_w_up, m_ffn_w_down, m_ln_g, m_ln_b, v_mla_w_in, v_mla_g_q, v_mla_w_uq, v_mla_g_kv, v_mla_w_uk, v_mla_w_uv, v_mla_w_o, v_fox_w_in, v_fox_b_f, v_fox_w_o, v_ada_w, v_ada_b, v_ffn_w_gate, v_ffn_w_up, v_ffn_w_down, v_ln_g, v_ln_b):
    args = dict(locals())
    nb, s, d = x.shape
    ax, ay, ac = lax.axis_index("x"), lax.axis_index("y"), lax.axis_index("c")
    chip = 2 * ax + ay
    dev = 2 * chip + ac
    n_dev = 2 * N_CHIPS
    n_all = nb * n_dev

    shard_shapes = {n: (args[n].shape if _SHARD_KIND[n] == "chunk" else args[n].shape[1:]) for n in _PACKED}

    def block(n, layer=None):
        w = args[n].reshape(shard_shapes[n]) if layer is None else args[n][layer]
        return _halves(w.astype(BF16))

    mla_names = [n for n in _PACKED if n.startswith("mla")]
    wts = {}
    for n, g in zip(mla_names, all_gather_chips([block(n) for n in mla_names], "gather_mla")):
        g = g.reshape(N_CHIPS, *shard_shapes[n])
        wts[n] = g.reshape(-1, g.shape[-1]) if _SHARD_KIND[n] == "rows" else _cols_to_full(g)
    ffn_names = ("ffn_w_gate", "ffn_w_up", "ffn_w_down")
    fox_in_t = jnp.swapaxes(fox_w_in, 1, 2)[0].astype(BF16)
    fox_in_t = jnp.stack([fox_in_t[:, :d // 2], fox_in_t[:, d // 2:]])
    shards = {"ffn0": [block(n, 0) for n in ffn_names], "fox": [fox_in_t, block("fox_w_o")],
              "ffn1": [block(n, 1) for n in ffn_names]}

    ln_cols = ln_g.shape[-1]
    ln_blk = jnp.concatenate([ln_g.reshape(2 * DEPTH, ln_cols), ln_b.reshape(2 * DEPTH, ln_cols)], axis=0)
    early = jnp.concatenate([_pad_rows(c, 8), jnp.pad(_pad_rows(ln_blk, 8), ((0, 0), (0, d - ln_cols)))], axis=0)
    early = all_gather8(early, "gather_c_ln").reshape(n_dev, 16, d)
    c_all = early[:, :nb].reshape(n_all, d)
    ln_all = early.reshape(N_CHIPS, 2, 16, d)[:, 0, 8:8 + 4 * DEPTH, :ln_cols]
    ln_all = jnp.transpose(ln_all, (1, 0, 2)).reshape(4 * DEPTH, d)
    ln_g_full = ln_all[:2 * DEPTH].reshape(DEPTH, 2, d)
    ln_b_full = ln_all[2 * DEPTH:].reshape(DEPTH, 2, d)
    mod_part = ada_mod_part(c_all, ada_w, "ada_mod")
    ncol = mod_part.shape[-1]
    mod_g = all_gather8(mod_part.reshape(DEPTH * n_all, ncol), "gather_mod")
    mod_g = mod_g.reshape(N_CHIPS, 2, DEPTH, n_all, ncol)[:, 0]
    mod_full = jnp.transpose(mod_g, (1, 2, 0, 3)).reshape(DEPTH, n_all, N_CHIPS * ncol) + ada_b[:, None, :]
    mod_loc = lax.dynamic_slice_in_dim(mod_full, dev * nb, nb, axis=1)
    mods = [tuple(mod_loc[i, :, k * d:(k + 1) * d].reshape(nb, 1, d) for k in range(6)) for i in range(DEPTH)]

    loss_part, grad_x, (parts, recv), dmods, d_ln_g, d_ln_b, dgq, dgkv, dbf = _local_step(
        x, positions, loss_target, mods, wts, ln_g_full, ln_b_full, mla_g_q[0], mla_g_kv[0], fox_b_f[0], shards)
    loss = lax.psum(loss_part, ("x", "y", "c"))

    dmod_rows = jnp.stack([jnp.concatenate([v_.reshape(nb, d) for v_ in dm], axis=1) for dm in dmods])
    small = jnp.concatenate([
        d_ln_g.reshape(2 * DEPTH, d), d_ln_b.reshape(2 * DEPTH, d),
        jnp.pad(jnp.concatenate([dgq, dgkv, dbf], axis=1), ((0, 0), (0, d - 2 * MLA_QR - FOX_HEADS))),
        dmod_rows.reshape(DEPTH * nb * 6, d)], axis=0)
    n_small = small.shape[0]
    small_rows = -(-n_small // 8) * 8
    small_all = all_gather8(_pad_rows(small, small_rows), "gather_stats").reshape(n_dev, small_rows, d)
    stat_sum = sum_leading(small_all, "sum_stats")
    g_ln_g = lax.dynamic_slice_in_dim(stat_sum[:2 * DEPTH], chip * ln_cols, ln_cols, axis=1).reshape(DEPTH, 2, ln_cols)
    g_ln_b = lax.dynamic_slice_in_dim(stat_sum[2 * DEPTH:4 * DEPTH], chip * ln_cols, ln_cols, axis=1).reshape(DEPTH, 2, ln_cols)
    row = stat_sum[4 * DEPTH]
    g_gq = row[:MLA_QR].reshape(1, MLA_QR)
    g_gkv = row[MLA_QR:2 * MLA_QR].reshape(1, MLA_KVR)
    g_bf = row[2 * MLA_QR:2 * MLA_QR + FOX_HEADS].reshape(1, FOX_HEADS)
    base = 4 * DEPTH + 1
    dmod_all = small_all[:, base:base + DEPTH * nb * 6].reshape(n_dev, DEPTH, nb, 6 * d)
    dmod_all = jnp.transpose(dmod_all, (1, 0, 2, 3)).reshape(DEPTH, n_all, 6 * d)
    g_ada_b = sum_leading(jnp.transpose(dmod_all, (1, 0, 2)), "sum_ada_b")
    dmod_mine = lax.dynamic_slice_in_dim(dmod_all, chip * ncol, ncol, axis=2)
    g_ada_w = ada_grad(c_all.T, dmod_mine, "ada_grad")

    place = jnp.stack([dev, ac, chip]).astype(jnp.int32)
    bufs = []
    for n in _PACKED:
        if _SHARD_KIND[n] == "chunk":
            buf = None
            for layer in range(DEPTH):
                key = "%s/%d" % (n, layer)
                buf = sum_devices(parts[key], recv[key], place, "rs_sum_%s%d" % (n, layer), slot=(layer, DEPTH, buf))
        else:
            buf = sum_devices(parts[n], recv[n], place, "rs_sum_" + n)
        bufs.append(buf)
    joined = sibling_join_halves(bufs, "rs_join")
    g_big = {n: j.reshape(j.shape[0], 2 * j.shape[2], j.shape[3]) for n, j in zip(_PACKED, joined)}
    j = joined[_PACKED.index("fox_w_in")]
    g_big["fox_w_in"] = jnp.transpose(j, (0, 2, 1, 3)).reshape(1, j.shape[2], 2 * j.shape[3])

    g_out = {
        "mla_w_in": g_big["mla_w_in"], "mla_g_q": g_gq, "mla_w_uq": g_big["mla_w_uq"], "mla_g_kv": g_gkv,
        "mla_w_uk": g_big["mla_w_uk"], "mla_w_uv": g_big["mla_w_uv"], "mla_w_o": g_big["mla_w_o"],
        "fox_w_in": g_big["fox_w_in"], "fox_b_f": g_bf, "fox_w_o": g_big["fox_w_o"],
        "ada_w": g_ada_w, "ada_b": g_ada_b, "ffn_w_gate": g_big["ffn_w_gate"], "ffn_w_up": g_big["ffn_w_up"],
        "ffn_w_down": g_big["ffn_w_down"], "ln_g": g_ln_g, "ln_b": g_ln_b}
    names = ["mla_w_in", "mla_g_q", "mla_w_uq", "mla_g_kv", "mla_w_uk", "mla_w_uv", "mla_w_o", "fox_w_in", "fox_b_f",
             "fox_w_o", "ada_w", "ada_b", "ffn_w_gate", "ffn_w_up", "ffn_w_down", "ln_g", "ln_b"]
    small_names = ["mla_g_q", "mla_g_kv", "fox_b_f", "ada_b", "ln_g", "ln_b"]
    deltas, new_m, new_v = {}, {}, {}
    for n in names:
        if n in small_names:
            continue
        shp = args[n].shape
        if n in _TRANSPOSED:
            view = lambda a: jnp.swapaxes(a, 1, 2).reshape(-1, shp[1])
            back = lambda a: jnp.swapaxes(a.reshape(shp[0], shp[2], shp[1]), 1, 2)
        else:
            view = lambda a: a.reshape(-1, shp[-1])
            back = lambda a: a.reshape(shp)
        dl, mn, vn = adamw(view(args[n]), g_out[n].reshape(view(args[n]).shape), view(args["m_" + n]),
                           view(args["v_" + n]), "adamw_" + n)
        g_out[n], deltas[n], new_m[n], new_v[n] = back(g_out[n].reshape(view(args[n]).shape)), back(dl), back(mn), back(vn)

    def small_pack(prefix, src):
        flat = jnp.concatenate([src[prefix + n].reshape(-1) for n in small_names])
        size = -(-flat.shape[0] // (8 * 128)) * 8 * 128
        return jnp.pad(flat, (0, size - flat.shape[0])).reshape(-1, 128)

    sd, sm, sv = adamw(small_pack("", args), small_pack("", g_out), small_pack("m_", args), small_pack("v_", args),
                       "adamw_small")
    off = 0
    for n in small_names:
        shp = args[n].shape
        size = math.prod(shp)
        deltas[n] = sd.reshape(-1)[off:off + size].reshape(shp)
        new_m[n] = sm.reshape(-1)[off:off + size].reshape(shp)
        new_v[n] = sv.reshape(-1)[off:off + size].reshape(shp)
        off += size

    outs = [loss, grad_x]
    outs += [g_out[n].reshape(args[n].shape) for n in names]
    outs += [deltas[n] for n in names] + [new_m[n] for n in names] + [new_v[n] for n in names]
    return tuple(outs)
```

```python
import functools
import math

import numpy as np
import jax
import jax.numpy as jnp
from jax import lax
from jax.experimental import pallas as pl
from jax.experimental.pallas import tpu as pltpu

F32 = jnp.float32
BF16 = jnp.bfloat16
MESH = pl.DeviceIdType.MESH

D_MODEL = 1024
DEPTH = 2
MLA_HEADS = 8
MLA_NOPE = 128
MLA_ROPE = 64
MLA_V = 128
MLA_QR = 256
MLA_KVR = 256
ROPE_THETA = 10000.0
FOX_HEADS = 16
FOX_HD = 64
D_FF = 2816
N_CHIPS = 4
FF_CHUNK = D_FF // N_CHIPS
ALPHA = (2.0 * DEPTH) ** 0.25
EPS = 1e-5
ADAM_LR = 0.001
ADAM_B1 = 0.9
ADAM_B2 = 0.999
ADAM_EPS = 1e-08
ADAM_WD = 0.01
ADAM_STEP = 10

VMEM_LIMIT = 56 * 1024 * 1024
TOKEN_TILE = 512
WGRAD_TOKENS = 2048
ATTN_TILE = 512
FOX_GROUP = 8
MLA_GROUP = 4
COMM_BLOCK_BYTES = 2 * 1024 * 1024
ADAMW_BLOCK_BYTES = 1024 * 1024


def _cp(n_axes):
    return pltpu.CompilerParams(dimension_semantics=("arbitrary",) * n_axes, vmem_limit_bytes=VMEM_LIMIT)


def _dot(a, b):
    return jnp.dot(a, b, preferred_element_type=F32)


def _dot_nt(a, b):
    return lax.dot_general(a, b, (((1,), (1,)), ((), ())), preferred_element_type=F32)


def _dot_tn(a, b):
    return lax.dot_general(a, b, (((0,), (0,)), ((), ())), preferred_element_type=F32)


def _dot_f32(a, b):
    return jnp.dot(a, b, preferred_element_type=F32, precision=lax.Precision.HIGHEST)


def _sds(shape, dtype):
    return jax.ShapeDtypeStruct(shape, dtype)


def _place():
    return lax.axis_index("x"), lax.axis_index("y"), lax.axis_index("c")


class _Hosted:
    def __init__(self, inputs, out_shape, sems, start, finish, in_place=False):
        self.inputs, self.out_shape, self.sems, self.start, self.finish = inputs, out_shape, sems, start, finish
        self.in_place = in_place


def _call(body, name, grid, in_specs, out_specs, out_shape, args, scratch_shapes=(), hosted=None):
    in_specs, out_specs, out_shape, scratch_shapes = list(in_specs), list(out_specs), list(out_shape), list(scratch_shapes)
    if hosted is None:
        return pl.pallas_call(body, name=name, grid=grid, in_specs=in_specs, out_specs=out_specs, out_shape=out_shape,
                              scratch_shapes=scratch_shapes, compiler_params=_cp(len(grid)))(*args)
    n_in, n_out, n_scr = len(in_specs), len(out_specs), len(scratch_shapes)
    h_in, h_out = len(hosted.inputs), len(hosted.out_shape)

    def carried(*refs):
        o0 = n_in + h_in
        s0 = o0 + n_out + h_out
        c_in, c_out, c_sem = refs[n_in:o0], refs[o0 + n_out:s0], refs[s0 + n_scr:]
        ids = [pl.program_id(a) for a in range(len(grid))]
        first = functools.reduce(jnp.logical_and, [i == 0 for i in ids])
        last = functools.reduce(jnp.logical_and, [i == g - 1 for i, g in zip(ids, grid)])

        @pl.when(first)
        def _():
            hosted.start(c_in, c_out, c_sem)

        body(*refs[:n_in], *refs[o0:o0 + n_out], *refs[s0:s0 + n_scr])

        @pl.when(last)
        def _():
            hosted.finish(c_in, c_out, c_sem)

    hbm = pl.BlockSpec(memory_space=pl.ANY)
    aliases = {n_in + k: n_out + k for k in range(h_in)} if hosted.in_place else {}
    res = pl.pallas_call(
        carried, name=name, grid=grid, in_specs=in_specs + [hbm] * h_in, out_specs=out_specs + [hbm] * h_out,
        out_shape=out_shape + list(hosted.out_shape), scratch_shapes=scratch_shapes + list(hosted.sems),
        input_output_aliases=aliases, compiler_params=_cp(len(grid)))(*args, *hosted.inputs)
    return res[:n_out], res[n_out:]


def mod_linear(x, shift, scale, w, out_dtype, name, tn=None, emit_u=False, w_rows=None):
    t, d = x.shape
    n = w.shape[1] if w_rows is None else w_rows
    tn = n if tn is None else tn
    tm = TOKEN_TILE
    tps = (t // shift.shape[0]) // tm

    def body(x_ref, sh_ref, sc_ref, w_ref, o_ref, *rest):
        u = (x_ref[...] * (1.0 + sc_ref[...]) + sh_ref[...]).astype(BF16)
        o_ref[...] = (_dot(u, w_ref[...]) if w_rows is None else _dot_nt(u, w_ref[...])).astype(out_dtype)
        if emit_u:
            @pl.when(pl.program_id(1) == 0)
            def _():
                rest[0][...] = u

    vec = pl.BlockSpec((None, 1, d), lambda i, j: (i // tps, 0, 0))
    out_shape = [_sds((t, n), out_dtype)]
    out_specs = [pl.BlockSpec((tm, tn), lambda i, j: (i, j))]
    if emit_u:
        out_shape.append(_sds((t, d), BF16))
        out_specs.append(pl.BlockSpec((tm, d), lambda i, j: (i, 0)))
    w_spec = pl.BlockSpec((d, tn), lambda i, j: (0, j)) if w_rows is None else pl.BlockSpec((tn, d), lambda i, j: (j, 0))
    res = pl.pallas_call(
        body, name=name, grid=(t // tm, n // tn),
        in_specs=[pl.BlockSpec((tm, d), lambda i, j: (i, 0)), vec, vec, w_spec],
        out_specs=out_specs, out_shape=out_shape, compiler_params=_cp(2),
    )(x, shift, scale, w)
    return res if emit_u else res[0]


def _rms(h, g):
    rstd = lax.rsqrt(jnp.mean(h * h, axis=-1, keepdims=True) + EPS)
    return h * rstd, rstd


def mla_mid_fwd(h, g_q, g_kv, w_uq, w_uk, w_uv, cos8, sin8, cos64, sin64s, swap64, rope_to_heads, dup64, name):
    t = h.shape[0]
    tm = TOKEN_TILE
    hq = MLA_HEADS * MLA_NOPE
    hr = MLA_HEADS * MLA_ROPE // 2

    def body(h_ref, gq_ref, gkv_ref, wuq_ref, wuk_ref, wuv_ref, c8_ref, s8_ref, c64_ref, s64_ref, sw_ref, p_ref, d_ref,
             q_ref, kn_ref, v_ref, kr_ref, cq_ref, ckv_ref):
        hh = h_ref[...]
        cq = (_rms(hh[:, :MLA_QR], None)[0] * gq_ref[...]).astype(BF16)
        ckv = (_rms(hh[:, MLA_QR:MLA_QR + MLA_KVR], None)[0] * gkv_ref[...]).astype(BF16)
        cq_ref[...] = cq
        ckv_ref[...] = ckv
        q = _dot(cq, wuq_ref[...])
        x1 = q[:, hq:hq + hr]
        x2 = q[:, hq + hr:]
        cs = c8_ref[...]
        sn = s8_ref[...]
        rot = jnp.concatenate([x1 * cs - x2 * sn, x2 * cs + x1 * sn], axis=1).astype(BF16)
        q_ref[...] = jnp.concatenate([q[:, :hq].astype(BF16), _dot(rot, p_ref[...]).astype(BF16)], axis=1)
        kn_ref[...] = _dot(ckv, wuk_ref[...]).astype(BF16)
        v_ref[...] = _dot(ckv, wuv_ref[...]).astype(BF16)
        kr = hh[:, MLA_QR + MLA_KVR:]
        kr = (kr * c64_ref[...] + _dot_f32(kr, sw_ref[...]) * s64_ref[...]).astype(BF16)
        kr_ref[...] = _dot(kr, d_ref[...]).astype(BF16)

    def rows(n):
        return pl.BlockSpec((tm, n), lambda i: (i, 0))

    def whole(a):
        return pl.BlockSpec(a.shape, lambda i: (0,) * a.ndim)

    nq = w_uq.shape[1]
    return pl.pallas_call(
        body, name=name, grid=(t // tm,),
        in_specs=[rows(h.shape[1]), whole(g_q), whole(g_kv), whole(w_uq), whole(w_uk), whole(w_uv),
                  rows(hr), rows(hr), rows(MLA_ROPE), rows(MLA_ROPE), whole(swap64), whole(rope_to_heads), whole(dup64)],
        out_specs=[rows(nq), rows(hq), rows(hq), rows(2 * MLA_ROPE), rows(MLA_QR), rows(MLA_KVR)],
        out_shape=[_sds((t, nq), BF16), _sds((t, hq), BF16), _sds((t, hq), BF16), _sds((t, 2 * MLA_ROPE), BF16),
                   _sds((t, MLA_QR), BF16), _sds((t, MLA_KVR), BF16)],
        compiler_params=_cp(1),
    )(h, g_q, g_kv, w_uq, w_uk, w_uv, cos8, sin8, cos64, sin64s, swap64, rope_to_heads, dup64)


def _pick_lane(tile, idx):
    lane = lax.broadcasted_iota(jnp.int32, tile.shape, 1)
    return jnp.sum(jnp.where(lane == idx, tile, 0.0), axis=1, keepdims=True)


def _pick_row(tile, idx):
    row = lax.broadcasted_iota(jnp.int32, tile.shape, 0)
    return jnp.sum(jnp.where(row == idx, tile, 0.0), axis=0, keepdims=True)


def _put_lane(tile, idx, col):
    lane = lax.broadcasted_iota(jnp.int32, tile.shape, 1)
    return jnp.where(lane == idx, col, tile)


def _put_row(tile, idx, row):
    r = lax.broadcasted_iota(jnp.int32, tile.shape, 0)
    return tile + jnp.where(r == idx, row, 0.0)


def _causal_softmax_blocks(i, tq, heads):
    def block(j, carry, masked):
        new = []
        for (score_fn, pv_fn, _), (m, l, acc) in zip(heads, carry):
            sc = score_fn(j)
            if masked:
                keep = lax.broadcasted_iota(jnp.int32, sc.shape, 0) >= lax.broadcasted_iota(jnp.int32, sc.shape, 1)
                sc = jnp.where(keep, sc, -1e30)
            m_new = jnp.maximum(m, jnp.max(sc, axis=1, keepdims=True))
            a = jnp.exp(m - m_new)
            p = jnp.exp(sc - m_new)
            new.append((m_new, a * l + jnp.sum(p, axis=1, keepdims=True), a * acc + pv_fn(j, p.astype(BF16))))
        return tuple(new)

    init = tuple((jnp.full((tq, 1), -1e30, F32), jnp.zeros((tq, 1), F32), jnp.zeros((tq, dv), F32)) for _, _, dv in heads)
    carry = lax.fori_loop(0, i, lambda j, c: block(j, c, False), init)
    return [(acc / l, m + jnp.log(l)) for m, l, acc in block(i, carry, True)]


def fox_attn_fwd(qkv, cum, cum_rows, nb, name, hosted=None):
    t = qkv.shape[0]
    s = t // nb
    tq = ATTN_TILE
    nq = s // tq
    wide = FOX_GROUP * FOX_HD
    ngroups = FOX_HEADS // FOX_GROUP
    scale = FOX_HD ** -0.5

    def body(q_ref, k_ref, v_ref, cum_ref, cr_ref, o_ref, lse_ref):
        i = pl.program_id(1)
        hg = pl.program_id(2)

        @pl.when(hg == 0)
        def _():
            lse_ref[...] = jnp.zeros_like(lse_ref)

        low = lax.broadcasted_iota(jnp.int32, (tq, 128), 1) < FOX_HD
        cum_t = cum_ref[...]

        def rows_of(j):
            return pl.ds(pl.multiple_of(j * tq, tq), tq)

        def head(a):
            hd = FOX_GROUP * hg + a
            cols = slice(128 * (a // 2), 128 * (a // 2) + 128)
            q = q_ref[:, cols]
            qa = jnp.where(low if a % 2 == 0 else jnp.logical_not(low), q, jnp.zeros_like(q))
            fq = _pick_lane(cum_t, hd)
            return (lambda j: _dot_nt(qa, k_ref[rows_of(j), cols]) * scale + fq - _pick_row(cr_ref[j], hd),
                    lambda j, p: _dot(p, v_ref[rows_of(j), cols]), 2 * FOX_HD)

        res = _causal_softmax_blocks(i, tq, [head(a) for a in range(FOX_GROUP)])
        o_ref[...] = jnp.concatenate([jnp.where(low, res[a][0], res[a + 1][0]) for a in range(0, FOX_GROUP, 2)],
                                     axis=1).astype(BF16)
        lse_t = lse_ref[...]
        for a in range(FOX_GROUP):
            lse_t = _put_lane(lse_t, FOX_GROUP * hg + a, res[a][1])
        lse_ref[...] = lse_t

    return _call(
        body, name, (nb, nq, ngroups),
        [pl.BlockSpec((tq, wide), lambda b, i, hg: (b * nq + i, hg)),
         pl.BlockSpec((s, wide), lambda b, i, hg: (b, ngroups + hg)),
         pl.BlockSpec((s, wide), lambda b, i, hg: (b, 2 * ngroups + hg)),
         pl.BlockSpec((tq, 128), lambda b, i, hg: (b * nq + i, 0)),
         pl.BlockSpec((nq, 16, tq), lambda b, i, hg: (b, 0, 0))],
        [pl.BlockSpec((tq, wide), lambda b, i, hg: (b * nq + i, hg)),
         pl.BlockSpec((tq, 128), lambda b, i, hg: (b * nq + i, 0))],
        [_sds((t, D_MODEL), BF16), _sds((t, 128), F32)], (qkv, qkv, qkv, cum, cum_rows), hosted=hosted)


def mla_attn_fwd(q, kn, kr2, v, nb, name, hosted=None):
    t = q.shape[0]
    s = t // nb
    tq = ATTN_TILE
    nq = s // tq
    ngroups = MLA_HEADS // MLA_GROUP
    wide = MLA_GROUP * MLA_NOPE
    rwide = MLA_GROUP * MLA_ROPE
    scale = (MLA_NOPE + MLA_ROPE) ** -0.5

    def body(qn_ref, qr_ref, kn_ref, kr_ref, v_ref, o_ref, lse_ref):
        i = pl.program_id(1)
        hg = pl.program_id(2)

        @pl.when(hg == 0)
        def _():
            lse_ref[...] = jnp.zeros_like(lse_ref)

        low = lax.broadcasted_iota(jnp.int32, (tq, 128), 1) < MLA_ROPE

        def rows_of(j):
            return pl.ds(pl.multiple_of(j * tq, tq), tq)

        def head(a):
            cols = slice(a * MLA_NOPE, (a + 1) * MLA_NOPE)
            qr = qr_ref[:, 128 * (a // 2):128 * (a // 2) + 128]
            q_cat = jnp.concatenate([qn_ref[:, cols], jnp.where(low if a % 2 == 0 else jnp.logical_not(low), qr,
                                                                jnp.zeros_like(qr))], axis=1)
            return (lambda j: _dot_nt(q_cat, jnp.concatenate([kn_ref[rows_of(j), cols], kr_ref[rows_of(j), :]], axis=1)) * scale,
                    lambda j, p: _dot(p, v_ref[rows_of(j), cols]), MLA_V)

        res = _causal_softmax_blocks(i, tq, [head(a) for a in range(MLA_GROUP)])
        o_ref[...] = jnp.concatenate([r[0] for r in res], axis=1).astype(BF16)
        lse_t = lse_ref[...]
        for a in range(MLA_GROUP):
            lse_t = _put_lane(lse_t, MLA_GROUP * hg + a, res[a][1])
        lse_ref[...] = lse_t

    rope0 = MLA_HEADS * MLA_NOPE // rwide
    return _call(
        body, name, (nb, nq, ngroups),
        [pl.BlockSpec((tq, wide), lambda b, i, hg: (b * nq + i, hg)),
         pl.BlockSpec((tq, rwide), lambda b, i, hg: (b * nq + i, rope0 + hg)),
         pl.BlockSpec((s, wide), lambda b, i, hg: (b, hg)),
         pl.BlockSpec((s, 128), lambda b, i, hg: (b, 0)),
         pl.BlockSpec((s, wide), lambda b, i, hg: (b, hg))],
        [pl.BlockSpec((tq, wide), lambda b, i, hg: (b * nq + i, hg)),
         pl.BlockSpec((tq, 128), lambda b, i, hg: (b * nq + i, 0))],
        [_sds((t, MLA_HEADS * MLA_V), BF16), _sds((t, 128), F32)], (q, q, kn, kr2, v), hosted=hosted)


def rows16(a, name):
    t = a.shape[0]
    tq = ATTN_TILE

    def body(a_ref, o_ref):
        o_ref[...] = a_ref[...].T[:16, :]

    return pl.pallas_call(
        body, name=name, grid=(t // tq,), in_specs=[pl.BlockSpec((tq, 128), lambda n: (n, 0))],
        out_specs=pl.BlockSpec((None, 16, tq), lambda n: (n, 0, 0)), out_shape=_sds((t // tq, 16, tq), F32),
        compiler_params=_cp(1),
    )(a)


def tokens128(rows, onehot, name):
    nblk, _, tq = rows.shape

    def body(r_ref, e_ref, o_ref):
        o_ref[...] = lax.dot_general(r_ref[...], e_ref[...], (((0,), (0,)), ((), ())), preferred_element_type=F32,
                                     precision=lax.Precision.HIGHEST)

    return pl.pallas_call(
        body, name=name, grid=(nblk,),
        in_specs=[pl.BlockSpec((None, 16, tq), lambda n: (n, 0, 0)), pl.BlockSpec((16, 128), lambda n: (0, 0))],
        out_specs=pl.BlockSpec((tq, 128), lambda n: (n, 0)), out_shape=_sds((nblk * tq, 128), F32),
        compiler_params=_cp(1),
    )(rows, onehot)


def _layer_norm(z, g, b):
    mu = jnp.mean(z, axis=-1, keepdims=True)
    zc = z - mu
    rstd = lax.rsqrt(jnp.mean(zc * zc, axis=-1, keepdims=True) + EPS)
    xhat = zc * rstd
    return xhat * g + b, xhat, rstd


def linear_resid_ln(a, w, x_in, gate, ln_g, ln_b, name):
    t, kdim = a.shape
    d = w.shape[1]
    tm = TOKEN_TILE
    tps = (t // gate.shape[0]) // tm

    def body(a_ref, w_ref, x_ref, gt_ref, g_ref, b_ref, y_ref, xo_ref):
        y = _dot(a_ref[...], w_ref[...])
        y_ref[...] = y
        z = ALPHA * x_ref[...] + (1.0 + gt_ref[...]) * y
        xo_ref[...] = _layer_norm(z, g_ref[...], b_ref[...])[0]

    rows = pl.BlockSpec((tm, d), lambda i: (i, 0))
    vec = pl.BlockSpec((1, d), lambda i: (0, 0))
    return pl.pallas_call(
        body, name=name, grid=(t // tm,),
        in_specs=[pl.BlockSpec((tm, kdim), lambda i: (i, 0)), pl.BlockSpec((kdim, d), lambda i: (0, 0)), rows,
                  pl.BlockSpec((None, 1, d), lambda i: (i // tps, 0, 0)), vec, vec],
        out_specs=[rows, rows], out_shape=[_sds((t, d), F32), _sds((t, d), F32)],
        compiler_params=_cp(1),
    )(a, w, x_in, gate, ln_g, ln_b)


def _resident(a):
    return pl.BlockSpec(a.shape, lambda *_: (0,) * a.ndim, pipeline_mode=pl.Buffered(1))


def ffn_fwd(x_in, shift, scale, gate, wg, wu, wd, ln_g, ln_b, name, hosted=None):
    t, d = x_in.shape
    c, _, fc = wg.shape
    tm = TOKEN_TILE
    tps = (t // gate.shape[0]) // tm

    def body(x_ref, sh_ref, sc_ref, gt_ref, wg_ref, wu_ref, wd_ref, g_ref, b_ref,
             u_ref, hg_ref, hu_ref, y_ref, xo_ref, acc_ref):
        cc = pl.program_id(1)

        @pl.when(cc == 0)
        def _():
            u_ref[...] = (x_ref[...] * (1.0 + sc_ref[...]) + sh_ref[...]).astype(BF16)
            acc_ref[...] = jnp.zeros_like(acc_ref)

        u = u_ref[...]
        hg = _dot(u, wg_ref[cc])
        hu = _dot(u, wu_ref[cc])
        hg_ref[...] = hg.astype(BF16)
        hu_ref[...] = hu.astype(BF16)
        act = (hg * jax.nn.sigmoid(hg) * hu).astype(BF16)
        acc_ref[...] += _dot(act, wd_ref[cc])

        @pl.when(cc == c - 1)
        def _():
            y = acc_ref[...]
            y_ref[...] = y
            z = ALPHA * x_ref[...] + (1.0 + gt_ref[...]) * y
            xo_ref[...] = _layer_norm(z, g_ref[...], b_ref[...])[0]

    rows = pl.BlockSpec((tm, d), lambda i, cc: (i, 0))
    bvec = pl.BlockSpec((None, 1, d), lambda i, cc: (i // tps, 0, 0))
    vec = pl.BlockSpec((1, d), lambda i, cc: (0, 0))
    hspec = pl.BlockSpec((None, tm, fc), lambda i, cc: (cc, i, 0))
    wcol = _resident(wg)
    return _call(
        body, name, (t // tm, c),
        [rows, bvec, bvec, bvec, wcol, wcol, _resident(wd), vec, vec],
        [rows, hspec, hspec, rows, rows],
        [_sds((t, d), BF16), _sds((c, t, fc), BF16), _sds((c, t, fc), BF16), _sds((t, d), F32), _sds((t, d), F32)],
        (x_in, shift, scale, gate, wg, wu, wd, ln_g, ln_b), scratch_shapes=[pltpu.VMEM((tm, d), F32)], hosted=hosted)


def fox_gate_fwd(hf, b_f, tri, n_batch, name):
    t, n = hf.shape
    blk = tri.shape[0]
    nb = (t // n_batch) // blk

    def body(hf_ref, b_ref, tri_ref, o_ref, carry_ref):
        @pl.when(pl.program_id(1) == 0)
        def _():
            carry_ref[...] = jnp.zeros_like(carry_ref)

        xx = hf_ref[...] + b_ref[...]
        lf = jnp.minimum(xx, 0.0) - jnp.log(1.0 + jnp.exp(-jnp.abs(xx)))
        cum = _dot_f32(tri_ref[...], lf) + carry_ref[...]
        o_ref[...] = cum
        carry_ref[...] = cum[blk - 1:blk, :]

    return pl.pallas_call(
        body, name=name, grid=(n_batch, nb),
        in_specs=[pl.BlockSpec((blk, n), lambda bb, i: (bb * nb + i, 0)), pl.BlockSpec((1, n), lambda bb, i: (0, 0)),
                  pl.BlockSpec((blk, blk), lambda bb, i: (0, 0))],
        out_specs=pl.BlockSpec((blk, n), lambda bb, i: (bb * nb + i, 0)),
        out_shape=_sds((t, n), F32), scratch_shapes=[pltpu.VMEM((1, n), F32)],
        compiler_params=_cp(2),
    )(hf, b_f, tri)


def loss_grad(x_out, target, name):
    t, d = x_out.shape
    tm = TOKEN_TILE

    def body(x_ref, t_ref, g_ref, l_ref):
        @pl.when(pl.program_id(0) == 0)
        def _():
            l_ref[...] = jnp.zeros_like(l_ref)

        err = x_ref[...] - t_ref[...]
        g_ref[...] = err / d
        l_ref[...] += jnp.sum(err * err, axis=0, keepdims=True)

    rows = pl.BlockSpec((tm, d), lambda i: (i, 0))
    return pl.pallas_call(
        body, name=name, grid=(t // tm,), in_specs=[rows, rows],
        out_specs=[rows, pl.BlockSpec((1, d), lambda i: (0, 0))],
        out_shape=[_sds((t, d), F32), _sds((1, d), F32)], compiler_params=_cp(1),
    )(x_out, target)


def ln_bwd(dxo, x_in, y, gate, ln_g, name):
    t, d = dxo.shape
    nb = gate.shape[0]
    tm = TOKEN_TILE
    tps = (t // nb) // tm

    def body(dxo_ref, x_ref, y_ref, gt_ref, g_ref, dz_ref, dy_ref, dg_ref, db_ref, dgt_ref):
        i = pl.program_id(0)

        @pl.when(i == 0)
        def _():
            dg_ref[...] = jnp.zeros_like(dg_ref)
            db_ref[...] = jnp.zeros_like(db_ref)

        @pl.when(i % tps == 0)
        def _():
            dgt_ref[...] = jnp.zeros_like(dgt_ref)

        yy = y_ref[...]
        g1 = 1.0 + gt_ref[...]
        z = ALPHA * x_ref[...] + g1 * yy
        _, xhat, rstd = _layer_norm(z, 1.0, 0.0)
        dxo_v = dxo_ref[...]
        dg_ref[...] += jnp.sum(dxo_v * xhat, axis=0, keepdims=True)
        db_ref[...] += jnp.sum(dxo_v, axis=0, keepdims=True)
        dxh = dxo_v * g_ref[...]
        dz = rstd * (dxh - jnp.mean(dxh, axis=-1, keepdims=True) - xhat * jnp.mean(dxh * xhat, axis=-1, keepdims=True))
        dz_ref[...] = dz
        dy_ref[...] = (g1 * dz).astype(BF16)
        dgt_ref[...] += jnp.sum(dz * yy, axis=0, keepdims=True)

    rows = pl.BlockSpec((tm, d), lambda i: (i, 0))
    vec = pl.BlockSpec((1, d), lambda i: (0, 0))
    bvec = pl.BlockSpec((None, 1, d), lambda i: (i // tps, 0, 0))
    return pl.pallas_call(
        body, name=name, grid=(t // tm,), in_specs=[rows, rows, rows, bvec, vec],
        out_specs=[rows, rows, vec, vec, bvec],
        out_shape=[_sds((t, d), F32), _sds((t, d), BF16), _sds((1, d), F32), _sds((1, d), F32), _sds((nb, 1, d), F32)],
        compiler_params=_cp(1),
    )(dxo, x_in, y, gate, ln_g)


def _mod_bwd_tail(du, dz_ref, x_ref, sc_ref, dx_ref, dsc_ref, dsh_ref, first):
    @pl.when(first)
    def _():
        dsc_ref[...] = jnp.zeros_like(dsc_ref)
        dsh_ref[...] = jnp.zeros_like(dsh_ref)

    dx_ref[...] = ALPHA * dz_ref[...] + du * (1.0 + sc_ref[...])
    dsc_ref[...] += jnp.sum(du * x_ref[...], axis=0, keepdims=True)
    dsh_ref[...] += jnp.sum(du, axis=0, keepdims=True)


def ffn_bwd(dy, hg, hu, wg, wu, wd, dz, x_in, scale, name, hosted=None):
    t, d = dy.shape
    c, _, fc = wg.shape
    nb = scale.shape[0]
    tm = TOKEN_TILE
    tps = (t // nb) // tm

    def body(dy_ref, hg_ref, hu_ref, wg_ref, wu_ref, wd_ref, dz_ref, x_ref, sc_ref,
             dhg_ref, dhu_ref, act_ref, dx_ref, dsc_ref, dsh_ref, acc_ref):
        i = pl.program_id(0)
        cc = pl.program_id(1)

        @pl.when(cc == 0)
        def _():
            acc_ref[...] = jnp.zeros_like(acc_ref)

        hgv = hg_ref[...].astype(F32)
        huv = hu_ref[...].astype(F32)
        da = _dot_nt(dy_ref[...], wd_ref[cc])
        sg = jax.nn.sigmoid(hgv)
        sl = hgv * sg
        act_ref[...] = (sl * huv).astype(BF16)
        dhu = (da * sl).astype(BF16)
        dhg = (da * huv * (sg * (1.0 + hgv * (1.0 - sg)))).astype(BF16)
        dhu_ref[...] = dhu
        dhg_ref[...] = dhg
        acc_ref[...] += _dot_nt(dhg, wg_ref[cc]) + _dot_nt(dhu, wu_ref[cc])

        @pl.when(cc == c - 1)
        def _():
            _mod_bwd_tail(acc_ref[...], dz_ref, x_ref, sc_ref, dx_ref, dsc_ref, dsh_ref, i % tps == 0)

    rows = pl.BlockSpec((tm, d), lambda i, cc: (i, 0))
    bvec = pl.BlockSpec((None, 1, d), lambda i, cc: (i // tps, 0, 0))
    hspec = pl.BlockSpec((None, tm, fc), lambda i, cc: (cc, i, 0))
    wcol = _resident(wg)
    return _call(
        body, name, (t // tm, c),
        [rows, hspec, hspec, wcol, wcol, _resident(wd), rows, rows, bvec],
        [hspec, hspec, hspec, rows, bvec, bvec],
        [_sds((c, t, fc), BF16), _sds((c, t, fc), BF16), _sds((c, t, fc), BF16), _sds((t, d), F32),
         _sds((nb, 1, d), F32), _sds((nb, 1, d), F32)],
        (dy, hg, hu, wg, wu, wd, dz, x_in, scale), scratch_shapes=[pltpu.VMEM((tm, d), F32)], hosted=hosted)


def linear_nt_mod_bwd(pairs, dz, x_in, scale, name, hosted=None):
    t, d = dz.shape
    nb = scale.shape[0]
    tm = TOKEN_TILE
    tps = (t // nb) // tm
    npairs = len(pairs)

    def body(*refs):
        dh_refs = refs[:npairs]
        w_refs = refs[npairs:2 * npairs]
        dz_ref, x_ref, sc_ref, dx_ref, dsc_ref, dsh_ref = refs[2 * npairs:]
        du = None
        for (_, _, blk), dh_ref, w_ref in zip(pairs, dh_refs, w_refs):
            dh = dh_ref[...].astype(BF16)
            term = _dot_nt(dh, w_ref[...]) if blk is None else _dot(dh, w_ref[...])
            du = term if du is None else du + term
        _mod_bwd_tail(du, dz_ref, x_ref, sc_ref, dx_ref, dsc_ref, dsh_ref, pl.program_id(0) % tps == 0)

    rows = pl.BlockSpec((tm, d), lambda i: (i, 0))
    bvec = pl.BlockSpec((None, 1, d), lambda i: (i // tps, 0, 0))
    in_specs = [pl.BlockSpec((tm, dh.shape[1]), lambda i: (i, 0)) for dh, _, _ in pairs]
    for dh, w, blk in pairs:
        if blk is None:
            in_specs.append(pl.BlockSpec(w.shape, lambda i: (0, 0)))
        else:
            in_specs.append(pl.BlockSpec((dh.shape[1], d), lambda i, blk=blk: (blk, 0)))
    in_specs += [rows, rows, bvec]
    return _call(
        body, name, (t // tm,), in_specs, [rows, bvec, bvec],
        [_sds((t, d), F32), _sds((nb, 1, d), F32), _sds((nb, 1, d), F32)],
        (*[dh for dh, _, _ in pairs], *[w for _, w, _ in pairs], dz, x_in, scale), hosted=hosted)


def linear_nt_delta(dy, w_o, o, head_sel, name):
    t, d = dy.shape
    hdv = w_o.shape[0]
    tm = TOKEN_TILE

    def body(dy_ref, w_ref, o_ref, sel_ref, do_ref, dl_ref):
        do = _dot_nt(dy_ref[...], w_ref[...])
        do_ref[...] = do.astype(BF16)
        dl_ref[...] = _dot_f32(do * o_ref[...].astype(F32), sel_ref[...])

    return pl.pallas_call(
        body, name=name, grid=(t // tm,),
        in_specs=[pl.BlockSpec((tm, d), lambda i: (i, 0)), pl.BlockSpec((hdv, d), lambda i: (0, 0)),
                  pl.BlockSpec((tm, hdv), lambda i: (i, 0)), pl.BlockSpec(head_sel.shape, lambda i: (0, 0))],
        out_specs=[pl.BlockSpec((tm, hdv), lambda i: (i, 0)), pl.BlockSpec((tm, 128), lambda i: (i, 0))],
        out_shape=[_sds((t, hdv), BF16), _sds((t, 128), F32)], compiler_params=_cp(1),
    )(dy, w_o, o, head_sel)


def _attn_bwd_blocks(j, nk, tk, scale, heads):
    def block(i, carry, masked):
        new = []
        for hd, (dk_acc, dv_acc, dfk_acc) in zip(heads, carry):
            qb = hd["q"](i)
            dob = hd["do"](i)
            lse_row, dl_row = hd["rows"](i)
            st = _dot_nt(hd["k"], qb) * scale
            if hd["bias"] is not None:
                fq_row, fk_col = hd["bias"](i)
                st = st + fq_row - fk_col
            if masked:
                keep = lax.broadcasted_iota(jnp.int32, st.shape, 1) >= lax.broadcasted_iota(jnp.int32, st.shape, 0)
                st = jnp.where(keep, st, -1e30)
            pt = jnp.exp(st - lse_row)
            dv_acc = dv_acc + _dot(pt.astype(BF16), dob)
            dst = pt * (_dot_nt(hd["v"], dob) - dl_row)
            if hd["add_dfq"] is not None:
                dfk_acc = dfk_acc - jnp.sum(dst, axis=1, keepdims=True)
                hd["add_dfq"](i, jnp.sum(dst, axis=0, keepdims=True))
            dsb = (dst * scale).astype(BF16)
            dk_acc = dk_acc + _dot(dsb, qb)
            hd["add_dq"](i, _dot_tn(dsb, hd["k"]))
            new.append((dk_acc, dv_acc, dfk_acc))
        return tuple(new)

    init = tuple((jnp.zeros((tk, hd["k"].shape[1]), F32), jnp.zeros((tk, hd["v"].shape[1]), F32), jnp.zeros((tk, 1), F32))
                 for hd in heads)
    carry = block(j, init, True)
    return lax.fori_loop(j + 1, nk, lambda i, c: block(i, c, False), carry)


def fox_attn_bwd(qkv, do, cum, cum_rows, lse_rows, delta_rows, nb, name, hosted=None):
    t = qkv.shape[0]
    s = t // nb
    tk = ATTN_TILE
    nk = s // tk
    scale = FOX_HD ** -0.5

    def body(q_ref, k_ref, v_ref, do_ref, cum_ref, cr_ref, lr_ref, dr_ref, dq_ref, dk_ref, dv_ref, dfq_ref, dfk_ref):
        hg = pl.program_id(1)
        j = pl.program_id(2)

        @pl.when(j == 0)
        def _():
            dq_ref[...] = jnp.zeros_like(dq_ref)

        @pl.when((j == 0) & (hg == 0))
        def _():
            dfq_ref[...] = jnp.zeros_like(dfq_ref)
            dfk_ref[...] = jnp.zeros_like(dfk_ref)

        low = lax.broadcasted_iota(jnp.int32, (tk, 128), 1) < FOX_HD
        cum_t = cum_ref[...]

        def rows_of(i):
            return pl.ds(pl.multiple_of(i * tk, tk), tk)

        def head(a):
            hd = FOX_GROUP * hg + a
            cols = slice(128 * (a // 2), 128 * (a // 2) + 128)
            half = low if a % 2 == 0 else jnp.logical_not(low)
            kb = k_ref[:, cols]
            vb = v_ref[:, cols]
            fk = _pick_lane(cum_t, hd)

            def add_dq(i, val):
                dq_ref[rows_of(i), cols] += val

            def add_dfq(i, val):
                dfq_ref[i] = _put_row(dfq_ref[i], hd, val)

            return dict(q=lambda i: q_ref[rows_of(i), cols], do=lambda i: do_ref[rows_of(i), cols],
                        k=jnp.where(half, kb, jnp.zeros_like(kb)), v=jnp.where(half, vb, jnp.zeros_like(vb)),
                        rows=lambda i: (_pick_row(lr_ref[i], hd), _pick_row(dr_ref[i], hd)),
                        bias=lambda i: (_pick_row(cr_ref[i], hd), fk), add_dq=add_dq, add_dfq=add_dfq)

        res = _attn_bwd_blocks(j, nk, tk, scale, [head(a) for a in range(FOX_GROUP)])
        dk_ref[...] = jnp.concatenate([jnp.where(low, res[a][0], res[a + 1][0]) for a in range(0, FOX_GROUP, 2)],
                                      axis=1).astype(BF16)
        dv_ref[...] = jnp.concatenate([jnp.where(low, res[a][1], res[a + 1][1]) for a in range(0, FOX_GROUP, 2)],
                                      axis=1).astype(BF16)
        for a in range(FOX_GROUP):
            dfk_ref[j] = _put_row(dfk_ref[j], FOX_GROUP * hg + a, jnp.broadcast_to(res[a][2], (tk, 128)).T[0:1, :])

    wide = FOX_GROUP * FOX_HD
    ngroups = FOX_HEADS // FOX_GROUP
    rowsp = pl.BlockSpec((nk, 16, tk), lambda b, hg, j: (b, 0, 0))
    return _call(
        body, name, (nb, ngroups, nk),
        [pl.BlockSpec((s, wide), lambda b, hg, j: (b, hg)),
         pl.BlockSpec((tk, wide), lambda b, hg, j: (b * nk + j, ngroups + hg)),
         pl.BlockSpec((tk, wide), lambda b, hg, j: (b * nk + j, 2 * ngroups + hg)),
         pl.BlockSpec((s, wide), lambda b, hg, j: (b, hg)),
         pl.BlockSpec((tk, 128), lambda b, hg, j: (b * nk + j, 0)),
         rowsp, rowsp, rowsp],
        [pl.BlockSpec((s, wide), lambda b, hg, j: (b, hg)),
         pl.BlockSpec((tk, wide), lambda b, hg, j: (b * nk + j, hg)),
         pl.BlockSpec((tk, wide), lambda b, hg, j: (b * nk + j, hg)),
         rowsp, rowsp],
        [_sds((t, D_MODEL), F32), _sds((t, D_MODEL), BF16), _sds((t, D_MODEL), BF16),
         _sds((t // tk, 16, tk), F32), _sds((t // tk, 16, tk), F32)],
        (qkv, qkv, qkv, do, cum, cum_rows, lse_rows, delta_rows), hosted=hosted)


def mla_attn_bwd(q, kn, kr2, v, do, lse_rows, delta_rows, nb, name, hosted=None):
    t = q.shape[0]
    s = t // nb
    tk = ATTN_TILE
    nk = s // tk
    ngroups = MLA_HEADS // MLA_GROUP
    wide = MLA_GROUP * MLA_NOPE
    rwide = MLA_GROUP * MLA_ROPE
    scale = (MLA_NOPE + MLA_ROPE) ** -0.5

    def body(qn_ref, qr_ref, kn_ref, kr_ref, v_ref, do_ref, lr_ref, dr_ref, dqn_ref, dqr_ref, dkn_ref, dkr_ref, dv_ref):
        hg = pl.program_id(1)
        j = pl.program_id(2)

        @pl.when(j == 0)
        def _():
            dqn_ref[...] = jnp.zeros_like(dqn_ref)
            dqr_ref[...] = jnp.zeros_like(dqr_ref)

        low = lax.broadcasted_iota(jnp.int32, (tk, 128), 1) < MLA_ROPE
        kr = kr_ref[...]

        def rows_of(i):
            return pl.ds(pl.multiple_of(i * tk, tk), tk)

        def head(a):
            cols = slice(a * MLA_NOPE, (a + 1) * MLA_NOPE)
            rcols = slice(128 * (a // 2), 128 * (a // 2) + 128)
            mine = low if a % 2 == 0 else jnp.logical_not(low)
            hd = MLA_GROUP * hg + a

            def q_fn(i):
                qr = qr_ref[rows_of(i), rcols]
                return jnp.concatenate([qn_ref[rows_of(i), cols], jnp.where(mine, qr, jnp.zeros_like(qr))], axis=1)

            def add_dq(i, val):
                dqn_ref[rows_of(i), cols] += val[:, :MLA_NOPE]
                dqr_ref[rows_of(i), cols] += val[:, MLA_NOPE:]

            return dict(q=q_fn, do=lambda i: do_ref[rows_of(i), cols], k=jnp.concatenate([kn_ref[:, cols], kr], axis=1),
                        v=v_ref[:, cols], rows=lambda i: (_pick_row(lr_ref[i], hd), _pick_row(dr_ref[i], hd)),
                        bias=None, add_dq=add_dq, add_dfq=None)

        res = _attn_bwd_blocks(j, nk, tk, scale, [head(a) for a in range(MLA_GROUP)])
        dkn_ref[...] = jnp.concatenate([r[0][:, :MLA_NOPE] for r in res], axis=1).astype(BF16)
        dkr_ref[...] = jnp.concatenate([r[0][:, MLA_NOPE:] for r in res], axis=1).astype(BF16)
        dv_ref[...] = jnp.concatenate([r[1] for r in res], axis=1).astype(BF16)

    full = pl.BlockSpec((s, wide), lambda b, hg, j: (b, hg))
    blk = pl.BlockSpec((tk, wide), lambda b, hg, j: (b * nk + j, hg))
    rowsp = pl.BlockSpec((nk, 16, tk), lambda b, hg, j: (b, 0, 0))
    total = MLA_HEADS * MLA_V
    rope0 = MLA_HEADS * MLA_NOPE // rwide
    return _call(
        body, name, (nb, ngroups, nk),
        [full, pl.BlockSpec((s, rwide), lambda b, hg, j: (b, rope0 + hg)), blk,
         pl.BlockSpec((tk, 128), lambda b, hg, j: (b * nk + j, 0)), blk, full, rowsp, rowsp],
        [full, full, blk, blk, blk],
        [_sds((t, total), F32), _sds((t, total), F32), _sds((t, total), BF16), _sds((t, total), BF16),
         _sds((t, total), BF16)],
        (q, q, kn, kr2, v, do, lse_rows, delta_rows), hosted=hosted)


def mla_mid_bwd(dqn, dqr, dkn, dv, dkr_heads, h, g_q, g_kv, w_uq, w_uk, w_uv, cos8, sin8, cos64, sin64s, swap64,
                heads_to_rope, head_sum, name):
    t = h.shape[0]
    tm = TOKEN_TILE
    hq = MLA_HEADS * MLA_NOPE
    hr = MLA_HEADS * MLA_ROPE // 2
    nq = w_uq.shape[1]

    def body(dqn_ref, dqr_ref, dkn_ref, dv_ref, dkr_ref, h_ref, gq_ref, gkv_ref, wuq_ref, wuk_ref, wuv_ref,
             c8_ref, s8_ref, c64_ref, s64_ref, sw_ref, hp_ref, hs_ref, dh_ref, dqp_ref, dgq_ref, dgkv_ref):
        @pl.when(pl.program_id(0) == 0)
        def _():
            dgq_ref[...] = jnp.zeros_like(dgq_ref)
            dgkv_ref[...] = jnp.zeros_like(dgkv_ref)

        drot = _dot(dqr_ref[...].astype(BF16), hp_ref[...])
        o1 = drot[:, :hr]
        o2 = drot[:, hr:]
        cs = c8_ref[...]
        sn = s8_ref[...]
        dqp = jnp.concatenate([dqn_ref[...].astype(BF16), (o1 * cs + o2 * sn).astype(BF16),
                               (o2 * cs - o1 * sn).astype(BF16)], axis=1)
        dqp_ref[...] = dqp
        dcq = _dot_nt(dqp, wuq_ref[...])
        dckv = _dot_nt(dkn_ref[...], wuk_ref[...]) + _dot_nt(dv_ref[...], wuv_ref[...])
        hh = h_ref[...]

        def rms_bwd(hpart, g, dc, dg_ref):
            hhat, rstd = _rms(hpart, None)
            dg_ref[...] += jnp.sum(dc * hhat, axis=0, keepdims=True)
            dcg = dc * g
            return rstd * (dcg - hhat * jnp.mean(dcg * hhat, axis=-1, keepdims=True))

        dhq = rms_bwd(hh[:, :MLA_QR], gq_ref[...], dcq, dgq_ref)
        dhkv = rms_bwd(hh[:, MLA_QR:MLA_QR + MLA_KVR], gkv_ref[...], dckv, dgkv_ref)
        dkr = _dot(dkr_ref[...], hs_ref[...])
        dkr_pre = dkr * c64_ref[...] + _dot_f32(dkr * s64_ref[...], sw_ref[...])
        dh_ref[...] = jnp.concatenate([dhq, dhkv, dkr_pre], axis=1).astype(BF16)

    def rows(n):
        return pl.BlockSpec((tm, n), lambda i: (i, 0))

    def whole(a):
        return pl.BlockSpec(a.shape, lambda i: (0,) * a.ndim)

    return pl.pallas_call(
        body, name=name, grid=(t // tm,),
        in_specs=[rows(hq), rows(hq), rows(hq), rows(hq), rows(hq), rows(h.shape[1]), whole(g_q), whole(g_kv),
                  whole(w_uq), whole(w_uk), whole(w_uv), rows(hr), rows(hr), rows(MLA_ROPE), rows(MLA_ROPE),
                  whole(swap64), whole(heads_to_rope), whole(head_sum)],
        out_specs=[rows(h.shape[1]), rows(nq), pl.BlockSpec((1, MLA_QR), lambda i: (0, 0)),
                   pl.BlockSpec((1, MLA_KVR), lambda i: (0, 0))],
        out_shape=[_sds((t, h.shape[1]), BF16), _sds((t, nq), BF16), _sds((1, MLA_QR), F32), _sds((1, MLA_KVR), F32)],
        compiler_params=_cp(1),
    )(dqn, dqr, dkn, dv, dkr_heads, h, g_q, g_kv, w_uq, w_uk, w_uv, cos8, sin8, cos64, sin64s, swap64,
      heads_to_rope, head_sum)


def fox_gate_bwd(dcum, hf, b_f, triu, n_batch, name):
    t, n = hf.shape
    blk = triu.shape[0]
    nb = (t // n_batch) // blk

    def body(dc_ref, hf_ref, b_ref, tri_ref, o_ref, db_ref, carry_ref):
        @pl.when(pl.program_id(1) == 0)
        def _():
            carry_ref[...] = jnp.zeros_like(carry_ref)

        @pl.when((pl.program_id(0) == 0) & (pl.program_id(1) == 0))
        def _():
            db_ref[...] = jnp.zeros_like(db_ref)

        rc = _dot_f32(tri_ref[...], dc_ref[...]) + carry_ref[...]
        carry_ref[...] = rc[0:1, :]
        dhf = rc * jax.nn.sigmoid(-(hf_ref[...] + b_ref[...]))
        o_ref[...] = dhf.astype(BF16)
        db_ref[...] += jnp.sum(dhf, axis=0, keepdims=True)

    rev = pl.BlockSpec((blk, n), lambda bb, i: (bb * nb + nb - 1 - i, 0))
    return pl.pallas_call(
        body, name=name, grid=(n_batch, nb),
        in_specs=[rev, rev, pl.BlockSpec((1, n), lambda bb, i: (0, 0)), pl.BlockSpec((blk, blk), lambda bb, i: (0, 0))],
        out_specs=[rev, pl.BlockSpec((1, n), lambda bb, i: (0, 0))],
        out_shape=[_sds((t, n), BF16), _sds((1, n), F32)], scratch_shapes=[pltpu.VMEM((1, n), F32)],
        compiler_params=_cp(2),
    )(dcum, hf, b_f, triu)


def wgrad(a, bm, name, with_bf16=False, bt=WGRAD_TOKENS):
    ca, t, kd = a.shape
    cb, _, nd = bm.shape
    c = max(ca, cb)
    bn = nd
    if nd > 1024 and nd % 1024 == 0:
        bn = 1024
    nsteps = t // bt

    def body(a_ref, b_ref, o_ref, *rest):
        @pl.when(pl.program_id(2) == 0)
        def _():
            o_ref[...] = jnp.zeros_like(o_ref)

        o_ref[...] += _dot_tn(a_ref[...].astype(BF16), b_ref[...].astype(BF16))
        if with_bf16:
            @pl.when(pl.program_id(2) == nsteps - 1)
            def _():
                rest[0][...] = o_ref[...].astype(BF16)

    out_spec = pl.BlockSpec((None, kd, bn), lambda cc, n, tt: (cc, 0, n))
    res = pl.pallas_call(
        body, name=name, grid=(c, nd // bn, nsteps),
        in_specs=[pl.BlockSpec((None, bt, kd), lambda cc, n, tt: (cc if ca > 1 else 0, tt, 0)),
                  pl.BlockSpec((None, bt, bn), lambda cc, n, tt: (cc if cb > 1 else 0, tt, n))],
        out_specs=[out_spec, out_spec] if with_bf16 else out_spec,
        out_shape=[_sds((c, kd, nd), F32), _sds((c, kd, nd), BF16)] if with_bf16 else _sds((c, kd, nd), F32),
        compiler_params=_cp(3),
    )(a, bm)
    return res


def ada_mod_part(c_all, ada_w, name):
    nl, d, n = ada_w.shape
    rows = c_all.shape[0]
    tn = 512

    def body(c_ref, w_ref, o_ref):
        cv = c_ref[...]
        act = (cv * jax.nn.sigmoid(cv)).astype(BF16)
        o_ref[...] = _dot(act, w_ref[...].astype(BF16))

    return pl.pallas_call(
        body, name=name, grid=(nl, n // tn),
        in_specs=[pl.BlockSpec((rows, d), lambda l, j: (0, 0)), pl.BlockSpec((None, d, tn), lambda l, j: (l, 0, j))],
        out_specs=pl.BlockSpec((None, rows, tn), lambda l, j: (l, 0, j)),
        out_shape=_sds((nl, rows, n), F32), compiler_params=_cp(2),
    )(c_all, ada_w)


def ada_grad(c_all_t, dmod, name):
    nl, rows, n = dmod.shape
    d = c_all_t.shape[0]
    tn = 512

    def body(c_ref, dm_ref, o_ref):
        cv = c_ref[...]
        act = (cv * jax.nn.sigmoid(cv)).astype(BF16)
        o_ref[...] = _dot(act, dm_ref[...].astype(BF16))

    return pl.pallas_call(
        body, name=name, grid=(nl, n // tn),
        in_specs=[pl.BlockSpec((d, rows), lambda l, j: (0, 0)), pl.BlockSpec((None, rows, tn), lambda l, j: (l, 0, j))],
        out_specs=pl.BlockSpec((None, d, tn), lambda l, j: (l, 0, j)),
        out_shape=_sds((nl, d, n), F32), compiler_params=_cp(2),
    )(c_all_t, dmod)


def sum_leading(a, name):
    g, r, n = a.shape

    def body(a_ref, o_ref):
        acc = a_ref[0]
        for kk in range(1, g):
            acc = acc + a_ref[kk]
        o_ref[...] = acc

    return pl.pallas_call(
        body, name=name, grid=(1,), in_specs=[pl.BlockSpec((g, r, n), lambda i: (0, 0, 0))],
        out_specs=pl.BlockSpec((r, n), lambda i: (0, 0)), out_shape=_sds((r, n), F32), compiler_params=_cp(1),
    )(a)


def adamw(w, g, m, v, name, hosted=None):
    r, n = w.shape
    br = r
    for cand in (512, 256, 128, 64, 32, 16, 8):
        if r % cand == 0 and r > cand and cand * n * 4 <= ADAMW_BLOCK_BYTES:
            br = cand
            break
    c1 = 1.0 - ADAM_B1 ** ADAM_STEP
    c2 = 1.0 - ADAM_B2 ** ADAM_STEP

    def body(w_ref, g_ref, m_ref, v_ref, d_ref, mo_ref, vo_ref):
        gv = g_ref[...]
        mn = ADAM_B1 * m_ref[...] + (1.0 - ADAM_B1) * gv
        vn = ADAM_B2 * v_ref[...] + (1.0 - ADAM_B2) * (gv * gv)
        mo_ref[...] = mn
        vo_ref[...] = vn
        d_ref[...] = -ADAM_LR * ((mn / c1) / (jnp.sqrt(vn / c2) + ADAM_EPS) + ADAM_WD * w_ref[...])

    spec = pl.BlockSpec((br, n), lambda i: (i, 0))
    return _call(body, name, (r // br,), [spec] * 4, [spec] * 3, [_sds((r, n), F32)] * 3, (w, g, m, v), hosted=hosted)


def all_gather8(x_blk, name, hosted=None):
    m_per, n = x_blk.shape
    h_in = 0 if hosted is None else len(hosted.inputs)
    h_out = 0 if hosted is None else len(hosted.out_shape)

    def body(x_ref, *refs):
        c_in, (out_ref, *c_out), (send_sems, recv_sems, local_sem, *c_sem) = (
            refs[:h_in], refs[h_in:h_in + 1 + h_out], refs[h_in + 1 + h_out:])
        if hosted is not None:
            hosted.start(c_in, c_out, c_sem)
        gather(x_ref, out_ref, send_sems, recv_sems, local_sem)
        if hosted is not None:
            hosted.finish(c_in, c_out, c_sem)

    def gather(x_ref, out_ref, send_sems, recv_sems, local_sem):
        x, y, c = _place()
        me, sibling = (x, y, c), (x, y, 1 - c)
        chips = [(1 - x, y), (x, 1 - y), (1 - x, 1 - y)]

        def rows(px, py, pc):
            return out_ref.at[pl.ds((4 * px + 2 * py + pc) * m_per, m_per), :]

        def copy(k, block, to, src=None):
            return pltpu.make_async_remote_copy(
                src_ref=rows(*block) if src is None else src, dst_ref=rows(*block),
                send_sem=send_sems.at[k], recv_sem=recv_sems.at[k], device_id=to, device_id_type=MESH)

        mine = pltpu.make_async_copy(x_ref, rows(*me), local_sem)
        mine.start()
        first = [copy(0, me, sibling, src=x_ref)]
        first += [copy(1 + j, me, (*chip, c), src=x_ref) for j, chip in enumerate(chips)]
        for cp in first:
            cp.start()
        passed = [copy(4 + j, (*chip, c), sibling) for j, chip in enumerate(chips)]
        for j, chip in enumerate(chips):
            copy(1 + j, (*chip, c), me).wait_recv()
            passed[j].start()
        copy(0, sibling, me).wait_recv()
        for j, chip in enumerate(chips):
            copy(4 + j, (*chip, 1 - c), me).wait_recv()
        for cp in first + passed:
            cp.wait_send()
        mine.wait()

    hbm = pl.BlockSpec(memory_space=pl.ANY)
    vmem = pl.BlockSpec(memory_space=pltpu.VMEM)
    res = pl.pallas_call(
        body, name=name,
        out_shape=[_sds((8 * m_per, n), x_blk.dtype)] + ([] if hosted is None else list(hosted.out_shape)),
        in_specs=[vmem] + [hbm] * h_in, out_specs=[vmem] + [hbm] * h_out,
        scratch_shapes=[pltpu.SemaphoreType.DMA((7,)), pltpu.SemaphoreType.DMA((7,)), pltpu.SemaphoreType.DMA]
        + ([] if hosted is None else list(hosted.sems)),
        compiler_params=pltpu.CompilerParams(vmem_limit_bytes=VMEM_LIMIT),
    )(x_blk, *([] if hosted is None else hosted.inputs))
    return res[0] if hosted is None else (res[0], res[1:])


def _gather_comm(shards):
    nt = len(shards)

    def parts(w_refs, out_refs, sems, finishing):
        send_sems, recv_sems, own_send, own_recv = sems
        x, y, c = _place()
        sibling = (x, y, 1 - c)
        chips = [(1 - x, y), (x, 1 - y), (1 - x, 1 - y)]

        def copy(t, k, block, to, src=None):
            px, py, hh = block
            dst = out_refs[t].at[2 * px + py, hh]
            return pltpu.make_async_remote_copy(
                src_ref=dst if src is None else src, dst_ref=dst,
                send_sem=send_sems.at[6 * t + k], recv_sem=recv_sems.at[6 * t + k], device_id=to, device_id_type=MESH)

        own = [pltpu.make_async_remote_copy(
            src_ref=w_refs[t], dst_ref=out_refs[t].at[2 * x + y], send_sem=own_send.at[t], recv_sem=own_recv.at[t],
            device_id=sibling, device_id_type=MESH) for t in range(nt)]
        first = [copy(t, j, (x, y, c), (*chip, c), src=w_refs[t].at[c]) for t in range(nt) for j, chip in enumerate(chips)]
        if not finishing:
            return own, first
        landed = [copy(t, j, (*chip, c), (x, y, c)) for t in range(nt) for j, chip in enumerate(chips)]
        passed = [copy(t, 3 + j, (*chip, c), sibling) for t in range(nt) for j, chip in enumerate(chips)]
        from_sibling = [copy(t, 3 + j, (*chip, 1 - c), (x, y, c)) for t in range(nt) for j, chip in enumerate(chips)]
        return own, first, landed, passed, from_sibling

    def start(w_refs, out_refs, sems):
        own, first = parts(w_refs, out_refs, sems, False)
        for cp in own + first:
            cp.start()

    def finish(w_refs, out_refs, sems):
        own, first, landed, passed, from_sibling = parts(w_refs, out_refs, sems, True)
        for arrived, fwd in zip(landed, passed):
            arrived.wait_recv()
            fwd.start()
        for cp in from_sibling:
            cp.wait_recv()
        for cp in first + passed:
            cp.wait_send()
        for cp in own:
            cp.wait()

    sems = [pltpu.SemaphoreType.DMA((6 * nt,)), pltpu.SemaphoreType.DMA((6 * nt,)),
            pltpu.SemaphoreType.DMA((nt,)), pltpu.SemaphoreType.DMA((nt,))]
    return _Hosted(list(shards), [_sds((N_CHIPS, *w.shape), w.dtype) for w in shards], sems, start, finish)


def _row_block(r, n, itemsize):
    best = None
    for br in range(16, r + 1, 16):
        if r % br == 0 and br * n * itemsize <= COMM_BLOCK_BYTES:
            best = br
    return r if best is None else best


def _scatter_comm(parts):
    nt = len(parts)

    def copies(p_refs, b_refs, sems, arriving):
        send_sems, recv_sems = sems
        x, y, c = _place()
        me = 4 * x + 2 * y + c
        cps = []
        for t in range(nt):
            for r in range(1, 8):
                tx = 1 - x if r & 4 else x
                ty = 1 - y if r & 2 else y
                tc = 1 - c if r & 1 else c
                src, dst = (2 * x + y, c), 4 * tx + 2 * ty + tc
                if not arriving:
                    src, dst = (2 * tx + ty, tc), me
                cps.append(pltpu.make_async_remote_copy(
                    src_ref=p_refs[t].at[src], dst_ref=b_refs[t].at[dst], send_sem=send_sems.at[7 * t + r - 1],
                    recv_sem=recv_sems.at[7 * t + r - 1], device_id=(tx, ty, tc), device_id_type=MESH))
        return cps

    def start(p_refs, b_refs, sems):
        for cp in copies(p_refs, b_refs, sems, False):
            cp.start()

    def finish(p_refs, b_refs, sems):
        for cp in copies(p_refs, b_refs, sems, True):
            cp.wait_recv()
        for cp in copies(p_refs, b_refs, sems, False):
            cp.wait_send()

    sems = [pltpu.SemaphoreType.DMA((7 * nt,)), pltpu.SemaphoreType.DMA((7 * nt,))]
    return _Hosted(list(parts), [_sds((2 * N_CHIPS, *p.shape[2:]), p.dtype) for p in parts], sems, start, finish)


def sum_devices(own, recv, place, name, slot=(0, 1, None)):
    _, _, r, n = own.shape
    layer, n_layers, buf = slot
    br = _row_block(r, n, 4 * 8)

    def body(p_ref, o_ref, *rest):
        acc = o_ref[...]
        for kk in range(7):
            acc = acc + rest[kk][...].astype(F32)
        rest[-1][...] = acc

    def arrived(rel):
        return pl.BlockSpec((None, br, n), lambda i, pref: (jnp.bitwise_xor(pref[0], rel), i, 0))

    in_specs = [pl.BlockSpec((None, None, br, n), lambda i, pref: (pref[2], pref[1], i, 0))]
    in_specs += [arrived(rel) for rel in range(1, 8)]
    args = [own] + [recv] * 7
    aliases = {}
    if buf is not None:
        in_specs.append(pl.BlockSpec(memory_space=pl.ANY))
        args.append(buf)
        aliases = {9: 0}
    return pl.pallas_call(
        body, name=name,
        grid_spec=pltpu.PrefetchScalarGridSpec(
            num_scalar_prefetch=1, grid=(r // br,), in_specs=in_specs,
            out_specs=pl.BlockSpec((None, None, br, n), lambda i, pref: (layer, pref[1], i, 0))),
        out_shape=_sds((n_layers, 2, r, n), F32), input_output_aliases=aliases, compiler_params=_cp(1),
    )(place, *args)


def _join_comm(bufs):
    nt = len(bufs)
    layers = [bf.shape[0] for bf in bufs]
    first = [sum(layers[:t]) for t in range(nt)]

    def copies(o_refs, sems, own):
        send_sems, recv_sems = sems
        x, y, c = _place()
        hh = c if own else 1 - c
        return [pltpu.make_async_remote_copy(
            src_ref=o_refs[t].at[l, hh], dst_ref=o_refs[t].at[l, hh], send_sem=send_sems.at[first[t] + l],
            recv_sem=recv_sems.at[first[t] + l], device_id=(x, y, 1 - c), device_id_type=MESH)
            for t in range(nt) for l in range(layers[t])]

    def start(_, o_refs, sems):
        for cp in copies(o_refs, sems, True):
            cp.start()

    def finish(_, o_refs, sems):
        for cp in copies(o_refs, sems, False):
            cp.wait_recv()
        for cp in copies(o_refs, sems, True):
            cp.wait_send()

    sems = [pltpu.SemaphoreType.DMA((sum(layers),)), pltpu.SemaphoreType.DMA((sum(layers),))]
    return _Hosted(list(bufs), [_sds(bf.shape, bf.dtype) for bf in bufs], sems, start, finish, in_place=True)


_SHARD_KIND = {"mla_w_in": "rows", "mla_w_uq": "cols", "mla_w_uk": "cols", "mla_w_uv": "cols", "mla_w_o": "rows",
               "fox_w_in": "cols", "fox_w_o": "rows", "ffn_w_gate": "chunk", "ffn_w_up": "chunk", "ffn_w_down": "chunk"}
_PACKED = tuple(_SHARD_KIND)
_TRANSPOSED = ("ffn_w_gate", "ffn_w_up", "fox_w_in")


def _halves(shard):
    if shard.ndim == 3 and shard.shape[0] == 2:
        return shard
    r, n = shard.shape[-2:]
    return shard.reshape(2, r // 2, n)


def _cols_to_full(g):
    return jnp.transpose(g, (1, 0, 2)).reshape(g.shape[1], -1)


def _full_to_cols(w):
    k, n4 = w.shape
    return jnp.transpose(w.reshape(k, N_CHIPS, n4 // N_CHIPS), (1, 0, 2))


def _uq_perm():
    per = MLA_NOPE + MLA_ROPE
    half = MLA_ROPE // 2
    nope = [h * per + d for h in range(MLA_HEADS) for d in range(MLA_NOPE)]
    r1 = [h * per + MLA_NOPE + r for h in range(MLA_HEADS) for r in range(half)]
    r2 = [h * per + MLA_NOPE + half + r for h in range(MLA_HEADS) for r in range(half)]
    perm = np.array(nope + r1 + r2, dtype=np.int32)
    return perm, np.argsort(perm).astype(np.int32)


def _rope_matrices():
    half = MLA_ROPE // 2
    nr = MLA_HEADS * MLA_ROPE
    to_heads = np.zeros((nr, nr), np.float32)
    from_heads = np.zeros((MLA_HEADS * 128, nr), np.float32)
    for e in range(2):
        for h in range(MLA_HEADS):
            for r in range(half):
                to_heads[e * MLA_HEADS * half + h * half + r, h * MLA_ROPE + e * half + r] = 1.0
                from_heads[h * 128 + e * half + r, e * MLA_HEADS * half + h * half + r] = 1.0
    head_sum = np.tile(np.eye(MLA_ROPE, dtype=np.float32), (2 * MLA_HEADS, 1))
    dup = np.concatenate([np.eye(MLA_ROPE, dtype=np.float32)] * 2, axis=1)
    return to_heads, from_heads, head_sum, dup


def _ffn_weights(gathered):
    return tuple(g.reshape(N_CHIPS, 2 * g.shape[2], g.shape[3]) for g in gathered)


def _fox_weights(gathered):
    w_in, w_o = gathered
    w_in = jnp.transpose(w_in, (0, 2, 1, 3)).reshape(N_CHIPS * w_in.shape[2], 2 * w_in.shape[3])
    return w_in, w_o.reshape(-1, w_o.shape[-1])


def _local_step(x, positions, target, mods, wts, ln_g, ln_b, mla_g_q, mla_g_kv, fox_b_f, shards=None):
    nb, s, d = x.shape
    t = nb * s
    x0 = x.reshape(t, d)
    tgt = target.reshape(t, d)
    perm, inv_perm = _uq_perm()

    half = MLA_ROPE // 2
    inv_freq = ROPE_THETA ** (-jnp.arange(half, dtype=F32) / half)
    ang = positions.astype(F32).reshape(t, 1) * inv_freq
    cos, sin = jnp.cos(ang), jnp.sin(ang)
    cos8, sin8 = jnp.tile(cos, (1, MLA_HEADS)), jnp.tile(sin, (1, MLA_HEADS))
    cos64 = jnp.concatenate([cos, cos], axis=1)
    sin64s = jnp.concatenate([-sin, sin], axis=1)
    swap64 = jnp.asarray(np.roll(np.eye(MLA_ROPE, dtype=np.float32), half, axis=1))
    to_heads, from_heads, head_sum, dup = _rope_matrices()
    to_heads, from_heads = jnp.asarray(to_heads, dtype=BF16), jnp.asarray(from_heads, dtype=BF16)
    head_sum, dup = jnp.asarray(head_sum, dtype=BF16), jnp.asarray(dup, dtype=BF16)
    sel_mla = jnp.asarray(np.pad(np.kron(np.eye(MLA_HEADS, dtype=np.float32), np.ones((MLA_V, 1), np.float32)),
                                 ((0, 0), (0, 128 - MLA_HEADS))))
    sel_fox = jnp.asarray(np.pad(np.kron(np.eye(FOX_HEADS, dtype=np.float32), np.ones((FOX_HD, 1), np.float32)),
                                 ((0, 0), (0, 128 - FOX_HEADS))))
    tri = jnp.asarray(np.tril(np.ones((128, 128), np.float32)))
    triu = jnp.asarray(np.triu(np.ones((128, 128), np.float32)))
    onehot16 = jnp.asarray(np.eye(16, 128, dtype=np.float32))

    def vec(a):
        return a.reshape(1, -1)

    def carried(key):
        return None if shards is None else _gather_comm(shards[key])

    def split(res):
        return (res, None) if shards is None else res

    w_uq_p = wts["mla_w_uq"][:, perm]
    b_f_pad = jnp.pad(fox_b_f.reshape(1, -1), ((0, 0), (0, 128 - FOX_HEADS)))

    sh_a, sc_a, gt_a, sh_f, sc_f, gt_f = mods[0]
    h_in, u_m = mod_linear(x0, sh_a, sc_a, wts["mla_w_in"], F32, "mla_in", emit_u=True)
    q_m, kn_m, v_m, kr2_m, cq_m, ckv_m = mla_mid_fwd(
        h_in, vec(mla_g_q), vec(mla_g_kv), w_uq_p, wts["mla_w_uk"], wts["mla_w_uv"], cos8, sin8, cos64, sin64s, swap64,
        to_heads, dup, "mla_mid")
    (o_m, lse_m), got = split(mla_attn_fwd(q_m, kn_m, kr2_m, v_m, nb, "mla_attn", hosted=carried("ffn0")))
    ffn0_w = wts["ffn"][0] if got is None else _ffn_weights(got)
    y0, x1 = linear_resid_ln(o_m, wts["mla_w_o"], x0, gt_a, vec(ln_g[0, 0]), vec(ln_b[0, 0]), "mla_out")
    (u_f0, hg0, hu0, y1, x2), got = split(ffn_fwd(x1, sh_f, sc_f, gt_f, *ffn0_w, vec(ln_g[0, 1]), vec(ln_b[0, 1]), "ffn0",
                                                  hosted=carried("fox")))
    fox_w_in_t, fox_w_o = (wts["fox_w_in"].T, wts["fox_w_o"]) if got is None else _fox_weights(got)
    fox_w_f_t = jnp.pad(fox_w_in_t[3 * d:], ((0, 128 - FOX_HEADS), (0, 0)))
    sh_a1, sc_a1, gt_a1, sh_f1, sc_f1, gt_f1 = mods[1]
    qkv, u_x = mod_linear(x2, sh_a1, sc_a1, fox_w_in_t, BF16, "fox_qkv", tn=1024, emit_u=True, w_rows=3 * d)
    hf = mod_linear(x2, sh_a1, sc_a1, fox_w_f_t, F32, "fox_f", w_rows=128)
    cum = fox_gate_fwd(hf, b_f_pad, tri, nb, "fox_gate")
    cum_rows = rows16(cum, "fox_cum_rows")
    (o_x, lse_x), got = split(fox_attn_fwd(qkv, cum, cum_rows, nb, "fox_attn", hosted=carried("ffn1")))
    ffn1_w = wts["ffn"][1] if got is None else _ffn_weights(got)
    y2, x3 = linear_resid_ln(o_x, fox_w_o, x2, gt_a1, vec(ln_g[1, 0]), vec(ln_b[1, 0]), "fox_out")
    u_f1, hg1, hu1, y3, x4 = ffn_fwd(x3, sh_f1, sc_f1, gt_f1, *ffn1_w, vec(ln_g[1, 1]), vec(ln_b[1, 1]), "ffn1")
    dx4, sq_err = loss_grad(x4, tgt, "loss")
    loss_part = 0.5 * jnp.sum(sq_err) / d

    parts, recv = {}, {}

    def halves_of(g):
        return g.reshape(N_CHIPS, 2, g.shape[1] // 2, g.shape[2])

    def scatter(keys, sent):
        return None if shards is None else _scatter_comm([sent[k] for k in keys])

    def landed(keys, got):
        if got is not None:
            recv.update(zip(keys, got))

    def ffn_grads(layer, u, dhg, dhu, act, dy):
        sent = {}
        for n, (a_op, b_op) in (("ffn_w_gate", (dhg, u[None])), ("ffn_w_up", (dhu, u[None])), ("ffn_w_down", (act, dy[None]))):
            g32, g16 = wgrad(a_op, b_op, "ffn%d_d%s" % (layer, n[4:]), with_bf16=True)
            parts["%s/%d" % (n, layer)], sent["%s/%d" % (n, layer)] = halves_of(g32), halves_of(g16)
        return sent

    dz3, dy3, dg11, db11, dgt_f1 = ln_bwd(dx4, x3, y3, gt_f1, vec(ln_g[1, 1]), "ffn1_ln_bwd")
    dhg1, dhu1, act1, dx3, dsc_f1, dsh_f1 = ffn_bwd(dy3, hg1, hu1, *ffn1_w, dz3, x3, sc_f1, "ffn1_bwd")
    sent = ffn_grads(1, u_f1, dhg1, dhu1, act1, dy3)
    dz2, dy2, dg10, db10, dgt_a1 = ln_bwd(dx3, x2, y2, gt_a1, vec(ln_g[1, 0]), "fox_ln_bwd")
    do_x, delta_x = linear_nt_delta(dy2, fox_w_o, o_x, sel_fox, "fox_out_bwd")
    (dq_x, dk_x, dv_x, dfq_x, dfk_x), got = split(fox_attn_bwd(
        qkv, do_x, cum, cum_rows, rows16(lse_x, "fox_lse_rows"), rows16(delta_x, "fox_delta_rows"), nb, "fox_attn_bwd",
        hosted=scatter(list(sent), sent)))
    landed(list(sent), got)
    dcum = tokens128(dfq_x + dfk_x, onehot16, "fox_dcum")
    dhf, dbf = fox_gate_bwd(dcum, hf, b_f_pad, triu, nb, "fox_gate_bwd")
    fox_d = [("q", dq_x), ("k", dk_x), ("v", dv_x)]
    dx2, dsc_a1, dsh_a1 = linear_nt_mod_bwd(
        [(dh, fox_w_in_t, i) for i, (_, dh) in enumerate(fox_d)] + [(dhf, fox_w_f_t, 0)], dz2, x2, sc_a1, "fox_in_bwd")
    dw_in_t = [wgrad(dh[None], u_x[None], "fox_dw" + tag)[0] for tag, dh in fox_d]
    dw_in_t.append(wgrad(dhf[None], u_x[None], "fox_dwf")[0][:FOX_HEADS])
    dw_in_t = jnp.concatenate(dw_in_t, axis=0).reshape(N_CHIPS, -1, 2, d // 2)
    parts["fox_w_in"] = jnp.transpose(dw_in_t, (0, 2, 1, 3))
    parts["fox_w_o"] = wgrad(o_x[None], dy2[None], "fox_dwo")[0].reshape(N_CHIPS, 2, -1, d)
    sent = {k: parts[k].astype(BF16) for k in ("fox_w_in", "fox_w_o")}
    dz1, dy1, dg01, db01, dgt_f0 = ln_bwd(dx2, x1, y1, gt_f, vec(ln_g[0, 1]), "ffn0_ln_bwd")
    (dhg0, dhu0, act0, dx1, dsc_f0, dsh_f0), got = split(ffn_bwd(dy1, hg0, hu0, *ffn0_w, dz1, x1, sc_f, "ffn0_bwd",
                                                                 hosted=scatter(list(sent), sent)))
    landed(list(sent), got)
    sent = ffn_grads(0, u_f0, dhg0, dhu0, act0, dy1)
    dz0, dy0, dg00, db00, dgt_a0 = ln_bwd(dx1, x0, y0, gt_a, vec(ln_g[0, 0]), "mla_ln_bwd")
    do_m, delta_m = linear_nt_delta(dy0, wts["mla_w_o"], o_m, sel_mla, "mla_out_bwd")
    parts["mla_w_o"] = wgrad(o_m[None], dy0[None], "mla_dwo")[0].reshape(N_CHIPS, 2, -1, d)
    sent["mla_w_o"] = parts["mla_w_o"].astype(BF16)
    (dqn_m, dqr_m, dkn_m, dkr_m, dv_m), got = split(mla_attn_bwd(
        q_m, kn_m, kr2_m, v_m, do_m, rows16(lse_m, "mla_lse_rows"), rows16(delta_m, "mla_delta_rows"), nb,
        "mla_attn_bwd", hosted=scatter(list(sent), sent)))
    landed(list(sent), got)
    dh_in, dq_pre, dgq, dgkv = mla_mid_bwd(
        dqn_m, dqr_m, dkn_m, dv_m, dkr_m, h_in, vec(mla_g_q), vec(mla_g_kv), w_uq_p, wts["mla_w_uk"],
        wts["mla_w_uv"], cos8, sin8, cos64, sin64s, swap64, from_heads, head_sum, "mla_mid_bwd")
    parts["mla_w_uq"] = halves_of(_full_to_cols(wgrad(cq_m[None], dq_pre[None], "mla_dwuq")[0][:, inv_perm]))
    parts["mla_w_uk"] = halves_of(_full_to_cols(wgrad(ckv_m[None], dkn_m[None], "mla_dwuk")[0]))
    parts["mla_w_uv"] = halves_of(_full_to_cols(wgrad(ckv_m[None], dv_m[None], "mla_dwuv")[0]))
    parts["mla_w_in"] = wgrad(u_m[None], dh_in[None], "mla_dwin")[0].reshape(N_CHIPS, 2, -1, h_in.shape[1])
    sent = {k: parts[k].astype(BF16) for k in ("mla_w_in", "mla_w_uq", "mla_w_uk", "mla_w_uv")}
    (dx0, dsc_a0, dsh_a0), got = split(linear_nt_mod_bwd([(dh_in, wts["mla_w_in"], None)], dz0, x0, sc_a, "mla_in_bwd",
                                                         hosted=scatter(list(sent), sent)))
    landed(list(sent), got)

    dmods = [(dsh_a0, dsc_a0, dgt_a0, dsh_f0, dsc_f0, dgt_f0), (dsh_a1, dsc_a1, dgt_a1, dsh_f1, dsc_f1, dgt_f1)]
    d_ln_g = jnp.stack([jnp.concatenate([dg00, dg01], axis=0), jnp.concatenate([dg10, dg11], axis=0)])
    d_ln_b = jnp.stack([jnp.concatenate([db00, db01], axis=0), jnp.concatenate([db10, db11], axis=0)])
    return loss_part, dx0.reshape(nb, s, d), (parts, recv), dmods, d_ln_g, d_ln_b, dgq, dgkv, dbf[:, :FOX_HEADS]


def _pad_rows(a, rows):
    return jnp.pad(a, ((0, rows - a.shape[0]), (0, 0)))


def kernel(x, c, positions, mla_w_in, mla_g_q, mla_w_uq, mla_g_kv, mla_w_uk, mla_w_uv, mla_w_o, fox_w_in, fox_b_f, fox_w_o, ada_w, ada_b, ffn_w_gate, ffn_w_up, ffn_w_down, ln_g, ln_b, loss_target, m_mla_w_in, m_mla_g_q, m_mla_w_uq, m_mla_g_kv, m_mla_w_uk, m_mla_w_uv, m_mla_w_o, m_fox_w_in, m_fox_b_f, m_fox_w_o, m_ada_w, m_ada_b, m_ffn_w_gate, m_ffn_w_up, m_ffn_w_down, m_ln_g, m_ln_b, v_mla_w_in, v_mla_g_q, v_mla_w_uq, v_mla_g_kv, v_mla_w_uk, v_mla_w_uv, v_mla_w_o, v_fox_w_in, v_fox_b_f, v_fox_w_o, v_ada_w, v_ada_b, v_ffn_w_gate, v_ffn_w_up, v_ffn_w_down, v_ln_g, v_ln_b):
    args = dict(locals())
    nb, s, d = x.shape
    ax, ay, ac = lax.axis_index("x"), lax.axis_index("y"), lax.axis_index("c")
    chip = 2 * ax + ay
    dev = 2 * chip + ac
    n_dev = 2 * N_CHIPS
    n_all = nb * n_dev

    shard_shapes = {n: (args[n].shape if _SHARD_KIND[n] == "chunk" else args[n].shape[1:]) for n in _PACKED}

    def block(n, layer=None):
        w = args[n].reshape(shard_shapes[n]) if layer is None else args[n][layer]
        return _halves(w.astype(BF16))

    mla_names = [n for n in _PACKED if n.startswith("mla")]
    ffn_names = ("ffn_w_gate", "ffn_w_up", "ffn_w_down")
    fox_in_t = jnp.swapaxes(fox_w_in, 1, 2)[0].astype(BF16)
    fox_in_t = jnp.stack([fox_in_t[:, :d // 2], fox_in_t[:, d // 2:]])
    shards = {"ffn0": [block(n, 0) for n in ffn_names], "fox": [fox_in_t, block("fox_w_o")],
              "ffn1": [block(n, 1) for n in ffn_names]}

    ln_cols = ln_g.shape[-1]
    ln_blk = jnp.concatenate([ln_g.reshape(2 * DEPTH, ln_cols), ln_b.reshape(2 * DEPTH, ln_cols)], axis=0)
    early = jnp.concatenate([_pad_rows(c, 8), jnp.pad(_pad_rows(ln_blk, 8), ((0, 0), (0, d - ln_cols)))], axis=0)
    early, mla_all = all_gather8(early, "gather_c_ln_mla", hosted=_gather_comm([block(n) for n in mla_names]))
    wts = {}
    for n, g in zip(mla_names, mla_all):
        g = g.reshape(N_CHIPS, *shard_shapes[n])
        wts[n] = g.reshape(-1, g.shape[-1]) if _SHARD_KIND[n] == "rows" else _cols_to_full(g)
    early = early.reshape(n_dev, 16, d)
    c_all = early[:, :nb].reshape(n_all, d)
    ln_all = early.reshape(N_CHIPS, 2, 16, d)[:, 0, 8:8 + 4 * DEPTH, :ln_cols]
    ln_all = jnp.transpose(ln_all, (1, 0, 2)).reshape(4 * DEPTH, d)
    ln_g_full = ln_all[:2 * DEPTH].reshape(DEPTH, 2, d)
    ln_b_full = ln_all[2 * DEPTH:].reshape(DEPTH, 2, d)
    mod_part = ada_mod_part(c_all, ada_w, "ada_mod")
    ncol = mod_part.shape[-1]
    mod_g = all_gather8(mod_part.reshape(DEPTH * n_all, ncol), "gather_mod")
    mod_g = mod_g.reshape(N_CHIPS, 2, DEPTH, n_all, ncol)[:, 0]
    mod_full = jnp.transpose(mod_g, (1, 2, 0, 3)).reshape(DEPTH, n_all, N_CHIPS * ncol) + ada_b[:, None, :]
    mod_loc = lax.dynamic_slice_in_dim(mod_full, dev * nb, nb, axis=1)
    mods = [tuple(mod_loc[i, :, k * d:(k + 1) * d].reshape(nb, 1, d) for k in range(6)) for i in range(DEPTH)]

    loss_part, grad_x, (parts, recv), dmods, d_ln_g, d_ln_b, dgq, dgkv, dbf = _local_step(
        x, positions, loss_target, mods, wts, ln_g_full, ln_b_full, mla_g_q[0], mla_g_kv[0], fox_b_f[0], shards)
    loss = lax.psum(loss_part, ("x", "y", "c"))

    dmod_rows = jnp.stack([jnp.concatenate([v_.reshape(nb, d) for v_ in dm], axis=1) for dm in dmods])
    small = jnp.concatenate([
        d_ln_g.reshape(2 * DEPTH, d), d_ln_b.reshape(2 * DEPTH, d),
        jnp.pad(jnp.concatenate([dgq, dgkv, dbf], axis=1), ((0, 0), (0, d - 2 * MLA_QR - FOX_HEADS))),
        dmod_rows.reshape(DEPTH * nb * 6, d)], axis=0)
    n_small = small.shape[0]
    small_rows = -(-n_small // 8) * 8
    small_all = all_gather8(_pad_rows(small, small_rows), "gather_stats").reshape(n_dev, small_rows, d)
    stat_sum = sum_leading(small_all, "sum_stats")
    g_ln_g = lax.dynamic_slice_in_dim(stat_sum[:2 * DEPTH], chip * ln_cols, ln_cols, axis=1).reshape(DEPTH, 2, ln_cols)
    g_ln_b = lax.dynamic_slice_in_dim(stat_sum[2 * DEPTH:4 * DEPTH], chip * ln_cols, ln_cols, axis=1).reshape(DEPTH, 2, ln_cols)
    row = stat_sum[4 * DEPTH]
    g_gq = row[:MLA_QR].reshape(1, MLA_QR)
    g_gkv = row[MLA_QR:2 * MLA_QR].reshape(1, MLA_KVR)
    g_bf = row[2 * MLA_QR:2 * MLA_QR + FOX_HEADS].reshape(1, FOX_HEADS)
    base = 4 * DEPTH + 1
    dmod_all = small_all[:, base:base + DEPTH * nb * 6].reshape(n_dev, DEPTH, nb, 6 * d)
    dmod_all = jnp.transpose(dmod_all, (1, 0, 2, 3)).reshape(DEPTH, n_all, 6 * d)
    g_ada_b = sum_leading(jnp.transpose(dmod_all, (1, 0, 2)), "sum_ada_b")
    dmod_mine = lax.dynamic_slice_in_dim(dmod_all, chip * ncol, ncol, axis=2)
    g_ada_w = ada_grad(c_all.T, dmod_mine, "ada_grad")

    place = jnp.stack([dev, ac, chip]).astype(jnp.int32)
    bufs = []
    for n in _PACKED:
        if _SHARD_KIND[n] == "chunk":
            buf = None
            for layer in range(DEPTH):
                key = "%s/%d" % (n, layer)
                buf = sum_devices(parts[key], recv[key], place, "rs_sum_%s%d" % (n, layer), slot=(layer, DEPTH, buf))
        else:
            buf = sum_devices(parts[n], recv[n], place, "rs_sum_" + n)
        bufs.append(buf)
    shp = ada_w.shape
    (dl, mn, vn), joined = adamw(ada_w.reshape(-1, shp[-1]), g_ada_w.reshape(-1, shp[-1]), m_ada_w.reshape(-1, shp[-1]),
                                 v_ada_w.reshape(-1, shp[-1]), "adamw_ada_w", hosted=_join_comm(bufs))
    done = {"ada_w": (dl.reshape(shp), mn.reshape(shp), vn.reshape(shp))}
    g_big = {n: j.reshape(j.shape[0], 2 * j.shape[2], j.shape[3]) for n, j in zip(_PACKED, joined)}
    j = joined[_PACKED.index("fox_w_in")]
    g_big["fox_w_in"] = jnp.transpose(j, (0, 2, 1, 3)).reshape(1, j.shape[2], 2 * j.shape[3])

    g_out = {
        "mla_w_in": g_big["mla_w_in"], "mla_g_q": g_gq, "mla_w_uq": g_big["mla_w_uq"], "mla_g_kv": g_gkv,
        "mla_w_uk": g_big["mla_w_uk"], "mla_w_uv": g_big["mla_w_uv"], "mla_w_o": g_big["mla_w_o"],
        "fox_w_in": g_big["fox_w_in"], "fox_b_f": g_bf, "fox_w_o": g_big["fox_w_o"],
        "ada_w": g_ada_w, "ada_b": g_ada_b, "ffn_w_gate": g_big["ffn_w_gate"], "ffn_w_up": g_big["ffn_w_up"],
        "ffn_w_down": g_big["ffn_w_down"], "ln_g": g_ln_g, "ln_b": g_ln_b}
    names = ["mla_w_in", "mla_g_q", "mla_w_uq", "mla_g_kv", "mla_w_uk", "mla_w_uv", "mla_w_o", "fox_w_in", "fox_b_f",
             "fox_w_o", "ada_w", "ada_b", "ffn_w_gate", "ffn_w_up", "ffn_w_down", "ln_g", "ln_b"]
    small_names = ["mla_g_q", "mla_g_kv", "fox_b_f", "ada_b", "ln_g", "ln_b"]
    deltas, new_m, new_v = {}, {}, {}
    for n in names:
        if n in small_names:
            continue
        if n in done:
            deltas[n], new_m[n], new_v[n] = done[n]
            continue
        shp = args[n].shape
        if n in _TRANSPOSED:
            view = lambda a: jnp.swapaxes(a, 1, 2).reshape(-1, shp[1])
            back = lambda a: jnp.swapaxes(a.reshape(shp[0], shp[2], shp[1]), 1, 2)
        else:
            view = lambda a: a.reshape(-1, shp[-1])
            back = lambda a: a.reshape(shp)
        dl, mn, vn = adamw(view(args[n]), g_out[n].reshape(view(args[n]).shape), view(args["m_" + n]),
                           view(args["v_" + n]), "adamw_" + n)
        g_out[n], deltas[n], new_m[n], new_v[n] = back(g_out[n].reshape(view(args[n]).shape)), back(dl), back(mn), back(vn)

    def small_pack(prefix, src):
        flat = jnp.concatenate([src[prefix + n].reshape(-1) for n in small_names])
        size = -(-flat.shape[0] // (8 * 128)) * 8 * 128
        return jnp.pad(flat, (0, size - flat.shape[0])).reshape(-1, 128)

    sd, sm, sv = adamw(small_pack("", args), small_pack("", g_out), small_pack("m_", args), small_pack("v_", args),
                       "adamw_small")
    off = 0
    for n in small_names:
        shp = args[n].shape
        size = math.prod(shp)
        deltas[n] = sd.reshape(-1)[off:off + size].reshape(shp)
        new_m[n] = sm.reshape(-1)[off:off + size].reshape(shp)
        new_v[n] = sv.reshape(-1)[off:off + size].reshape(shp)
        off += size

    outs = [loss, grad_x]
    outs += [g_out[n].reshape(args[n].shape) for n in names]
    outs += [deltas[n] for n in names] + [new_m[n] for n in names] + [new_v[n] for n in names]
    return tuple(outs)
```

```python
import functools
import math

import numpy as np
import jax
import jax.numpy as jnp
from jax import lax
from jax.experimental import pallas as pl
from jax.experimental.pallas import tpu as pltpu

F32 = jnp.float32
BF16 = jnp.bfloat16
MESH = pl.DeviceIdType.MESH

D_MODEL = 1024
DEPTH = 2
MLA_HEADS = 8
MLA_NOPE = 128
MLA_ROPE = 64
MLA_V = 128
MLA_QR = 256
MLA_KVR = 256
ROPE_THETA = 10000.0
FOX_HEADS = 16
FOX_HD = 64
D_FF = 2816
N_CHIPS = 4
FF_CHUNK = D_FF // N_CHIPS
ALPHA = (2.0 * DEPTH) ** 0.25
EPS = 1e-5
ADAM_LR = 0.001
ADAM_B1 = 0.9
ADAM_B2 = 0.999
ADAM_EPS = 1e-08
ADAM_WD = 0.01
ADAM_STEP = 10

VMEM_LIMIT = 56 * 1024 * 1024
TOKEN_TILE = 512
WGRAD_TOKENS = 2048
ATTN_TILE = 512
FOX_GROUP = 8
MLA_GROUP = 4
COMM_BLOCK_BYTES = 2 * 1024 * 1024
ADAMW_BLOCK_BYTES = 1024 * 1024


def _cp(n_axes):
    return pltpu.CompilerParams(dimension_semantics=("arbitrary",) * n_axes, vmem_limit_bytes=VMEM_LIMIT)


def _dot(a, b):
    return jnp.dot(a, b, preferred_element_type=F32)


def _dot_nt(a, b):
    return lax.dot_general(a, b, (((1,), (1,)), ((), ())), preferred_element_type=F32)


def _dot_tn(a, b):
    return lax.dot_general(a, b, (((0,), (0,)), ((), ())), preferred_element_type=F32)


def _dot_f32(a, b):
    return jnp.dot(a, b, preferred_element_type=F32, precision=lax.Precision.HIGHEST)


def _sds(shape, dtype):
    return jax.ShapeDtypeStruct(shape, dtype)


def _place():
    return lax.axis_index("x"), lax.axis_index("y"), lax.axis_index("c")


class _Hosted:
    def __init__(self, inputs, out_shape, sems, start, finish, in_place=False):
        self.inputs, self.out_shape, self.sems, self.start, self.finish = inputs, out_shape, sems, start, finish
        self.in_place = in_place


def _call(body, name, grid, in_specs, out_specs, out_shape, args, scratch_shapes=(), hosted=None):
    in_specs, out_specs, out_shape, scratch_shapes = list(in_specs), list(out_specs), list(out_shape), list(scratch_shapes)
    if hosted is None:
        return pl.pallas_call(body, name=name, grid=grid, in_specs=in_specs, out_specs=out_specs, out_shape=out_shape,
                              scratch_shapes=scratch_shapes, compiler_params=_cp(len(grid)))(*args)
    n_in, n_out, n_scr = len(in_specs), len(out_specs), len(scratch_shapes)
    h_in, h_out = len(hosted.inputs), len(hosted.out_shape)

    def carried(*refs):
        o0 = n_in + h_in
        s0 = o0 + n_out + h_out
        c_in, c_out, c_sem = refs[n_in:o0], refs[o0 + n_out:s0], refs[s0 + n_scr:]
        ids = [pl.program_id(a) for a in range(len(grid))]
        first = functools.reduce(jnp.logical_and, [i == 0 for i in ids])
        last = functools.reduce(jnp.logical_and, [i == g - 1 for i, g in zip(ids, grid)])

        @pl.when(first)
        def _():
            hosted.start(c_in, c_out, c_sem)

        body(*refs[:n_in], *refs[o0:o0 + n_out], *refs[s0:s0 + n_scr])

        @pl.when(last)
        def _():
            hosted.finish(c_in, c_out, c_sem)

    hbm = pl.BlockSpec(memory_space=pl.ANY)
    aliases = {n_in + k: n_out + k for k in range(h_in)} if hosted.in_place else {}
    res = pl.pallas_call(
        carried, name=name, grid=grid, in_specs=in_specs + [hbm] * h_in, out_specs=out_specs + [hbm] * h_out,
        out_shape=out_shape + list(hosted.out_shape), scratch_shapes=scratch_shapes + list(hosted.sems),
        input_output_aliases=aliases, compiler_params=_cp(len(grid)))(*args, *hosted.inputs)
    return res[:n_out], res[n_out:]


def mod_linear(x, shift, scale, w, out_dtype, name, tn=None, emit_u=False, w_rows=None):
    t, d = x.shape
    n = w.shape[1] if w_rows is None else w_rows
    tn = n if tn is None else tn
    tm = TOKEN_TILE
    tps = (t // shift.shape[0]) // tm

    def body(x_ref, sh_ref, sc_ref, w_ref, o_ref, *rest):
        u = (x_ref[...] * (1.0 + sc_ref[...]) + sh_ref[...]).astype(BF16)
        o_ref[...] = (_dot(u, w_ref[...]) if w_rows is None else _dot_nt(u, w_ref[...])).astype(out_dtype)
        if emit_u:
            @pl.when(pl.program_id(1) == 0)
            def _():
                rest[0][...] = u

    vec = pl.BlockSpec((None, 1, d), lambda i, j: (i // tps, 0, 0))
    out_shape = [_sds((t, n), out_dtype)]
    out_specs = [pl.BlockSpec((tm, tn), lambda i, j: (i, j))]
    if emit_u:
        out_shape.append(_sds((t, d), BF16))
        out_specs.append(pl.BlockSpec((tm, d), lambda i, j: (i, 0)))
    w_spec = pl.BlockSpec((d, tn), lambda i, j: (0, j)) if w_rows is None else pl.BlockSpec((tn, d), lambda i, j: (j, 0))
    res = pl.pallas_call(
        body, name=name, grid=(t // tm, n // tn),
        in_specs=[pl.BlockSpec((tm, d), lambda i, j: (i, 0)), vec, vec, w_spec],
        out_specs=out_specs, out_shape=out_shape, compiler_params=_cp(2),
    )(x, shift, scale, w)
    return res if emit_u else res[0]


def _rms(h, g):
    rstd = lax.rsqrt(jnp.mean(h * h, axis=-1, keepdims=True) + EPS)
    return h * rstd, rstd


def mla_mid_fwd(h, g_q, g_kv, w_uq, w_uk, w_uv, cos8, sin8, cos64, sin64s, swap64, rope_to_heads, dup64, name):
    t = h.shape[0]
    tm = TOKEN_TILE
    hq = MLA_HEADS * MLA_NOPE
    hr = MLA_HEADS * MLA_ROPE // 2

    def body(h_ref, gq_ref, gkv_ref, wuq_ref, wuk_ref, wuv_ref, c8_ref, s8_ref, c64_ref, s64_ref, sw_ref, p_ref, d_ref,
             q_ref, kn_ref, v_ref, kr_ref, cq_ref, ckv_ref):
        hh = h_ref[...]
        cq = (_rms(hh[:, :MLA_QR], None)[0] * gq_ref[...]).astype(BF16)
        ckv = (_rms(hh[:, MLA_QR:MLA_QR + MLA_KVR], None)[0] * gkv_ref[...]).astype(BF16)
        cq_ref[...] = cq
        ckv_ref[...] = ckv
        q = _dot(cq, wuq_ref[...])
        x1 = q[:, hq:hq + hr]
        x2 = q[:, hq + hr:]
        cs = c8_ref[...]
        sn = s8_ref[...]
        rot = jnp.concatenate([x1 * cs - x2 * sn, x2 * cs + x1 * sn], axis=1).astype(BF16)
        q_ref[...] = jnp.concatenate([q[:, :hq].astype(BF16), _dot(rot, p_ref[...]).astype(BF16)], axis=1)
        kn_ref[...] = _dot(ckv, wuk_ref[...]).astype(BF16)
        v_ref[...] = _dot(ckv, wuv_ref[...]).astype(BF16)
        kr = hh[:, MLA_QR + MLA_KVR:]
        kr = (kr * c64_ref[...] + _dot_f32(kr, sw_ref[...]) * s64_ref[...]).astype(BF16)
        kr_ref[...] = _dot(kr, d_ref[...]).astype(BF16)

    def rows(n):
        return pl.BlockSpec((tm, n), lambda i: (i, 0))

    def whole(a):
        return pl.BlockSpec(a.shape, lambda i: (0,) * a.ndim)

    nq = w_uq.shape[1]
    return pl.pallas_call(
        body, name=name, grid=(t // tm,),
        in_specs=[rows(h.shape[1]), whole(g_q), whole(g_kv), whole(w_uq), whole(w_uk), whole(w_uv),
                  rows(hr), rows(hr), rows(MLA_ROPE), rows(MLA_ROPE), whole(swap64), whole(rope_to_heads), whole(dup64)],
        out_specs=[rows(nq), rows(hq), rows(hq), rows(2 * MLA_ROPE), rows(MLA_QR), rows(MLA_KVR)],
        out_shape=[_sds((t, nq), BF16), _sds((t, hq), BF16), _sds((t, hq), BF16), _sds((t, 2 * MLA_ROPE), BF16),
                   _sds((t, MLA_QR), BF16), _sds((t, MLA_KVR), BF16)],
        compiler_params=_cp(1),
    )(h, g_q, g_kv, w_uq, w_uk, w_uv, cos8, sin8, cos64, sin64s, swap64, rope_to_heads, dup64)


def _pick_lane(tile, idx):
    lane = lax.broadcasted_iota(jnp.int32, tile.shape, 1)
    return jnp.sum(jnp.where(lane == idx, tile, 0.0), axis=1, keepdims=True)


def _pick_row(tile, idx):
    row = lax.broadcasted_iota(jnp.int32, tile.shape, 0)
    return jnp.sum(jnp.where(row == idx, tile, 0.0), axis=0, keepdims=True)


def _put_lane(tile, idx, col):
    lane = lax.broadcasted_iota(jnp.int32, tile.shape, 1)
    return jnp.where(lane == idx, col, tile)


def _put_row(tile, idx, row):
    r = lax.broadcasted_iota(jnp.int32, tile.shape, 0)
    return tile + jnp.where(r == idx, row, 0.0)


def _causal_softmax_blocks(i, tq, heads):
    def block(j, carry, masked):
        new = []
        for (score_fn, pv_fn, _), (m, l, acc) in zip(heads, carry):
            sc = score_fn(j)
            if masked:
                keep = lax.broadcasted_iota(jnp.int32, sc.shape, 0) >= lax.broadcasted_iota(jnp.int32, sc.shape, 1)
                sc = jnp.where(keep, sc, -1e30)
            m_new = jnp.maximum(m, jnp.max(sc, axis=1, keepdims=True))
            a = jnp.exp(m - m_new)
            p = jnp.exp(sc - m_new)
            new.append((m_new, a * l + jnp.sum(p, axis=1, keepdims=True), a * acc + pv_fn(j, p.astype(BF16))))
        return tuple(new)

    init = tuple((jnp.full((tq, 1), -1e30, F32), jnp.zeros((tq, 1), F32), jnp.zeros((tq, dv), F32)) for _, _, dv in heads)
    carry = lax.fori_loop(0, i, lambda j, c: block(j, c, False), init)
    return [(acc / l, m + jnp.log(l)) for m, l, acc in block(i, carry, True)]


def fox_attn_fwd(qkv, cum, cum_rows, nb, name, hosted=None):
    t = qkv.shape[0]
    s = t // nb
    tq = ATTN_TILE
    nq = s // tq
    wide = FOX_GROUP * FOX_HD
    ngroups = FOX_HEADS // FOX_GROUP
    scale = FOX_HD ** -0.5

    def body(q_ref, k_ref, v_ref, cum_ref, cr_ref, o_ref, lse_ref):
        i = pl.program_id(1)
        hg = pl.program_id(2)

        @pl.when(hg == 0)
        def _():
            lse_ref[...] = jnp.zeros_like(lse_ref)

        low = lax.broadcasted_iota(jnp.int32, (tq, 128), 1) < FOX_HD
        cum_t = cum_ref[...]

        def rows_of(j):
            return pl.ds(pl.multiple_of(j * tq, tq), tq)

        def head(a):
            hd = FOX_GROUP * hg + a
            cols = slice(128 * (a // 2), 128 * (a // 2) + 128)
            q = q_ref[:, cols]
            qa = jnp.where(low if a % 2 == 0 else jnp.logical_not(low), q, jnp.zeros_like(q)) * scale
            fq = _pick_lane(cum_t, hd)
            return (lambda j: _dot_nt(qa, k_ref[rows_of(j), cols]) + fq - _pick_row(cr_ref[j], hd),
                    lambda j, p: _dot(p, v_ref[rows_of(j), cols]), 2 * FOX_HD)

        res = _causal_softmax_blocks(i, tq, [head(a) for a in range(FOX_GROUP)])
        o_ref[...] = jnp.concatenate([jnp.where(low, res[a][0], res[a + 1][0]) for a in range(0, FOX_GROUP, 2)],
                                     axis=1).astype(BF16)
        lse_t = lse_ref[...]
        for a in range(FOX_GROUP):
            lse_t = _put_lane(lse_t, FOX_GROUP * hg + a, res[a][1])
        lse_ref[...] = lse_t

    return _call(
        body, name, (nb, nq, ngroups),
        [pl.BlockSpec((tq, wide), lambda b, i, hg: (b * nq + i, hg)),
         pl.BlockSpec((s, wide), lambda b, i, hg: (b, ngroups + hg)),
         pl.BlockSpec((s, wide), lambda b, i, hg: (b, 2 * ngroups + hg)),
         pl.BlockSpec((tq, 128), lambda b, i, hg: (b * nq + i, 0)),
         pl.BlockSpec((nq, 16, tq), lambda b, i, hg: (b, 0, 0))],
        [pl.BlockSpec((tq, wide), lambda b, i, hg: (b * nq + i, hg)),
         pl.BlockSpec((tq, 128), lambda b, i, hg: (b * nq + i, 0))],
        [_sds((t, D_MODEL), BF16), _sds((t, 128), F32)], (qkv, qkv, qkv, cum, cum_rows), hosted=hosted)


def mla_attn_fwd(q, kn, kr2, v, nb, name, hosted=None):
    t = q.shape[0]
    s = t // nb
    tq = ATTN_TILE
    nq = s // tq
    ngroups = MLA_HEADS // MLA_GROUP
    wide = MLA_GROUP * MLA_NOPE
    rwide = MLA_GROUP * MLA_ROPE
    scale = (MLA_NOPE + MLA_ROPE) ** -0.5

    def body(qn_ref, qr_ref, kn_ref, kr_ref, v_ref, o_ref, lse_ref):
        i = pl.program_id(1)
        hg = pl.program_id(2)

        @pl.when(hg == 0)
        def _():
            lse_ref[...] = jnp.zeros_like(lse_ref)

        low = lax.broadcasted_iota(jnp.int32, (tq, 128), 1) < MLA_ROPE

        def rows_of(j):
            return pl.ds(pl.multiple_of(j * tq, tq), tq)

        def head(a):
            cols = slice(a * MLA_NOPE, (a + 1) * MLA_NOPE)
            qr = qr_ref[:, 128 * (a // 2):128 * (a // 2) + 128]
            q_cat = jnp.concatenate([qn_ref[:, cols], jnp.where(low if a % 2 == 0 else jnp.logical_not(low), qr,
                                                                jnp.zeros_like(qr))], axis=1)
            return (lambda j: _dot_nt(q_cat, jnp.concatenate([kn_ref[rows_of(j), cols], kr_ref[rows_of(j), :]], axis=1)) * scale,
                    lambda j, p: _dot(p, v_ref[rows_of(j), cols]), MLA_V)

        res = _causal_softmax_blocks(i, tq, [head(a) for a in range(MLA_GROUP)])
        o_ref[...] = jnp.concatenate([r[0] for r in res], axis=1).astype(BF16)
        lse_t = lse_ref[...]
        for a in range(MLA_GROUP):
            lse_t = _put_lane(lse_t, MLA_GROUP * hg + a, res[a][1])
        lse_ref[...] = lse_t

    rope0 = MLA_HEADS * MLA_NOPE // rwide
    return _call(
        body, name, (nb, nq, ngroups),
        [pl.BlockSpec((tq, wide), lambda b, i, hg: (b * nq + i, hg)),
         pl.BlockSpec((tq, rwide), lambda b, i, hg: (b * nq + i, rope0 + hg)),
         pl.BlockSpec((s, wide), lambda b, i, hg: (b, hg)),
         pl.BlockSpec((s, 128), lambda b, i, hg: (b, 0)),
         pl.BlockSpec((s, wide), lambda b, i, hg: (b, hg))],
        [pl.BlockSpec((tq, wide), lambda b, i, hg: (b * nq + i, hg)),
         pl.BlockSpec((tq, 128), lambda b, i, hg: (b * nq + i, 0))],
        [_sds((t, MLA_HEADS * MLA_V), BF16), _sds((t, 128), F32)], (q, q, kn, kr2, v), hosted=hosted)


def rows16(a, name):
    t = a.shape[0]
    tq = ATTN_TILE

    def body(a_ref, o_ref):
        o_ref[...] = a_ref[...].T[:16, :]

    return pl.pallas_call(
        body, name=name, grid=(t // tq,), in_specs=[pl.BlockSpec((tq, 128), lambda n: (n, 0))],
        out_specs=pl.BlockSpec((None, 16, tq), lambda n: (n, 0, 0)), out_shape=_sds((t // tq, 16, tq), F32),
        compiler_params=_cp(1),
    )(a)


def tokens128(rows, onehot, name):
    nblk, _, tq = rows.shape

    def body(r_ref, e_ref, o_ref):
        o_ref[...] = lax.dot_general(r_ref[...], e_ref[...], (((0,), (0,)), ((), ())), preferred_element_type=F32,
                                     precision=lax.Precision.HIGHEST)

    return pl.pallas_call(
        body, name=name, grid=(nblk,),
        in_specs=[pl.BlockSpec((None, 16, tq), lambda n: (n, 0, 0)), pl.BlockSpec((16, 128), lambda n: (0, 0))],
        out_specs=pl.BlockSpec((tq, 128), lambda n: (n, 0)), out_shape=_sds((nblk * tq, 128), F32),
        compiler_params=_cp(1),
    )(rows, onehot)


def _layer_norm(z, g, b):
    mu = jnp.mean(z, axis=-1, keepdims=True)
    zc = z - mu
    rstd = lax.rsqrt(jnp.mean(zc * zc, axis=-1, keepdims=True) + EPS)
    xhat = zc * rstd
    return xhat * g + b, xhat, rstd


def linear_resid_ln(a, w, x_in, gate, ln_g, ln_b, name):
    t, kdim = a.shape
    d = w.shape[1]
    tm = TOKEN_TILE
    tps = (t // gate.shape[0]) // tm

    def body(a_ref, w_ref, x_ref, gt_ref, g_ref, b_ref, y_ref, xo_ref):
        y = _dot(a_ref[...], w_ref[...])
        y_ref[...] = y
        z = ALPHA * x_ref[...] + (1.0 + gt_ref[...]) * y
        xo_ref[...] = _layer_norm(z, g_ref[...], b_ref[...])[0]

    rows = pl.BlockSpec((tm, d), lambda i: (i, 0))
    vec = pl.BlockSpec((1, d), lambda i: (0, 0))
    return pl.pallas_call(
        body, name=name, grid=(t // tm,),
        in_specs=[pl.BlockSpec((tm, kdim), lambda i: (i, 0)), pl.BlockSpec((kdim, d), lambda i: (0, 0)), rows,
                  pl.BlockSpec((None, 1, d), lambda i: (i // tps, 0, 0)), vec, vec],
        out_specs=[rows, rows], out_shape=[_sds((t, d), F32), _sds((t, d), F32)],
        compiler_params=_cp(1),
    )(a, w, x_in, gate, ln_g, ln_b)


def _resident(a):
    return pl.BlockSpec(a.shape, lambda *_: (0,) * a.ndim, pipeline_mode=pl.Buffered(1))


def ffn_fwd(x_in, shift, scale, gate, wg, wu, wd, ln_g, ln_b, name, hosted=None):
    t, d = x_in.shape
    c, _, fc = wg.shape
    tm = TOKEN_TILE
    tps = (t // gate.shape[0]) // tm

    def body(x_ref, sh_ref, sc_ref, gt_ref, wg_ref, wu_ref, wd_ref, g_ref, b_ref,
             u_ref, hg_ref, hu_ref, y_ref, xo_ref, acc_ref):
        cc = pl.program_id(1)

        @pl.when(cc == 0)
        def _():
            u_ref[...] = (x_ref[...] * (1.0 + sc_ref[...]) + sh_ref[...]).astype(BF16)
            acc_ref[...] = jnp.zeros_like(acc_ref)

        u = u_ref[...]
        hg = _dot(u, wg_ref[cc])
        hu = _dot(u, wu_ref[cc])
        hg_ref[...] = hg.astype(BF16)
        hu_ref[...] = hu.astype(BF16)
        act = (hg * jax.nn.sigmoid(hg) * hu).astype(BF16)
        acc_ref[...] += _dot(act, wd_ref[cc])

        @pl.when(cc == c - 1)
        def _():
            y = acc_ref[...]
            y_ref[...] = y
            z = ALPHA * x_ref[...] + (1.0 + gt_ref[...]) * y
            xo_ref[...] = _layer_norm(z, g_ref[...], b_ref[...])[0]

    rows = pl.BlockSpec((tm, d), lambda i, cc: (i, 0))
    bvec = pl.BlockSpec((None, 1, d), lambda i, cc: (i // tps, 0, 0))
    vec = pl.BlockSpec((1, d), lambda i, cc: (0, 0))
    hspec = pl.BlockSpec((None, tm, fc), lambda i, cc: (cc, i, 0))
    wcol = _resident(wg)
    return _call(
        body, name, (t // tm, c),
        [rows, bvec, bvec, bvec, wcol, wcol, _resident(wd), vec, vec],
        [rows, hspec, hspec, rows, rows],
        [_sds((t, d), BF16), _sds((c, t, fc), BF16), _sds((c, t, fc), BF16), _sds((t, d), F32), _sds((t, d), F32)],
        (x_in, shift, scale, gate, wg, wu, wd, ln_g, ln_b), scratch_shapes=[pltpu.VMEM((tm, d), F32)], hosted=hosted)


def fox_gate_fwd(hf, b_f, tri, n_batch, name):
    t, n = hf.shape
    blk = tri.shape[0]
    nb = (t // n_batch) // blk

    def body(hf_ref, b_ref, tri_ref, o_ref, carry_ref):
        @pl.when(pl.program_id(1) == 0)
        def _():
            carry_ref[...] = jnp.zeros_like(carry_ref)

        xx = hf_ref[...] + b_ref[...]
        lf = jnp.minimum(xx, 0.0) - jnp.log(1.0 + jnp.exp(-jnp.abs(xx)))
        cum = _dot_f32(tri_ref[...], lf) + carry_ref[...]
        o_ref[...] = cum
        carry_ref[...] = cum[blk - 1:blk, :]

    return pl.pallas_call(
        body, name=name, grid=(n_batch, nb),
        in_specs=[pl.BlockSpec((blk, n), lambda bb, i: (bb * nb + i, 0)), pl.BlockSpec((1, n), lambda bb, i: (0, 0)),
                  pl.BlockSpec((blk, blk), lambda bb, i: (0, 0))],
        out_specs=pl.BlockSpec((blk, n), lambda bb, i: (bb * nb + i, 0)),
        out_shape=_sds((t, n), F32), scratch_shapes=[pltpu.VMEM((1, n), F32)],
        compiler_params=_cp(2),
    )(hf, b_f, tri)


def loss_grad(x_out, target, name):
    t, d = x_out.shape
    tm = TOKEN_TILE

    def body(x_ref, t_ref, g_ref, l_ref):
        @pl.when(pl.program_id(0) == 0)
        def _():
            l_ref[...] = jnp.zeros_like(l_ref)

        err = x_ref[...] - t_ref[...]
        g_ref[...] = err / d
        l_ref[...] += jnp.sum(err * err, axis=0, keepdims=True)

    rows = pl.BlockSpec((tm, d), lambda i: (i, 0))
    return pl.pallas_call(
        body, name=name, grid=(t // tm,), in_specs=[rows, rows],
        out_specs=[rows, pl.BlockSpec((1, d), lambda i: (0, 0))],
        out_shape=[_sds((t, d), F32), _sds((1, d), F32)], compiler_params=_cp(1),
    )(x_out, target)


def ln_bwd(dxo, x_in, y, gate, ln_g, name):
    t, d = dxo.shape
    nb = gate.shape[0]
    tm = TOKEN_TILE
    tps = (t // nb) // tm

    def body(dxo_ref, x_ref, y_ref, gt_ref, g_ref, dz_ref, dy_ref, dg_ref, db_ref, dgt_ref):
        i = pl.program_id(0)

        @pl.when(i == 0)
        def _():
            dg_ref[...] = jnp.zeros_like(dg_ref)
            db_ref[...] = jnp.zeros_like(db_ref)

        @pl.when(i % tps == 0)
        def _():
            dgt_ref[...] = jnp.zeros_like(dgt_ref)

        yy = y_ref[...]
        g1 = 1.0 + gt_ref[...]
        z = ALPHA * x_ref[...] + g1 * yy
        _, xhat, rstd = _layer_norm(z, 1.0, 0.0)
        dxo_v = dxo_ref[...]
        dg_ref[...] += jnp.sum(dxo_v * xhat, axis=0, keepdims=True)
        db_ref[...] += jnp.sum(dxo_v, axis=0, keepdims=True)
        dxh = dxo_v * g_ref[...]
        dz = rstd * (dxh - jnp.mean(dxh, axis=-1, keepdims=True) - xhat * jnp.mean(dxh * xhat, axis=-1, keepdims=True))
        dz_ref[...] = dz
        dy_ref[...] = (g1 * dz).astype(BF16)
        dgt_ref[...] += jnp.sum(dz * yy, axis=0, keepdims=True)

    rows = pl.BlockSpec((tm, d), lambda i: (i, 0))
    vec = pl.BlockSpec((1, d), lambda i: (0, 0))
    bvec = pl.BlockSpec((None, 1, d), lambda i: (i // tps, 0, 0))
    return pl.pallas_call(
        body, name=name, grid=(t // tm,), in_specs=[rows, rows, rows, bvec, vec],
        out_specs=[rows, rows, vec, vec, bvec],
        out_shape=[_sds((t, d), F32), _sds((t, d), BF16), _sds((1, d), F32), _sds((1, d), F32), _sds((nb, 1, d), F32)],
        compiler_params=_cp(1),
    )(dxo, x_in, y, gate, ln_g)


def _mod_bwd_tail(du, dz_ref, x_ref, sc_ref, dx_ref, dsc_ref, dsh_ref, first):
    @pl.when(first)
    def _():
        dsc_ref[...] = jnp.zeros_like(dsc_ref)
        dsh_ref[...] = jnp.zeros_like(dsh_ref)

    dx_ref[...] = ALPHA * dz_ref[...] + du * (1.0 + sc_ref[...])
    dsc_ref[...] += jnp.sum(du * x_ref[...], axis=0, keepdims=True)
    dsh_ref[...] += jnp.sum(du, axis=0, keepdims=True)


def ffn_bwd(dy, hg, hu, wg, wu, wd, dz, x_in, scale, name, hosted=None):
    t, d = dy.shape
    c, _, fc = wg.shape
    nb = scale.shape[0]
    tm = TOKEN_TILE
    tps = (t // nb) // tm

    def body(dy_ref, hg_ref, hu_ref, wg_ref, wu_ref, wd_ref, dz_ref, x_ref, sc_ref,
             dhg_ref, dhu_ref, act_ref, dx_ref, dsc_ref, dsh_ref, acc_ref):
        i = pl.program_id(0)
        cc = pl.program_id(1)

        @pl.when(cc == 0)
        def _():
            acc_ref[...] = jnp.zeros_like(acc_ref)

        hgv = hg_ref[...].astype(F32)
        huv = hu_ref[...].astype(F32)
        da = _dot_nt(dy_ref[...], wd_ref[cc])
        sg = jax.nn.sigmoid(hgv)
        sl = hgv * sg
        act_ref[...] = (sl * huv).astype(BF16)
        dhu = (da * sl).astype(BF16)
        dhg = (da * huv * (sg * (1.0 + hgv * (1.0 - sg)))).astype(BF16)
        dhu_ref[...] = dhu
        dhg_ref[...] = dhg
        acc_ref[...] += _dot_nt(dhg, wg_ref[cc]) + _dot_nt(dhu, wu_ref[cc])

        @pl.when(cc == c - 1)
        def _():
            _mod_bwd_tail(acc_ref[...], dz_ref, x_ref, sc_ref, dx_ref, dsc_ref, dsh_ref, i % tps == 0)

    rows = pl.BlockSpec((tm, d), lambda i, cc: (i, 0))
    bvec = pl.BlockSpec((None, 1, d), lambda i, cc: (i // tps, 0, 0))
    hspec = pl.BlockSpec((None, tm, fc), lambda i, cc: (cc, i, 0))
    wcol = _resident(wg)
    return _call(
        body, name, (t // tm, c),
        [rows, hspec, hspec, wcol, wcol, _resident(wd), rows, rows, bvec],
        [hspec, hspec, hspec, rows, bvec, bvec],
        [_sds((c, t, fc), BF16), _sds((c, t, fc), BF16), _sds((c, t, fc), BF16), _sds((t, d), F32),
         _sds((nb, 1, d), F32), _sds((nb, 1, d), F32)],
        (dy, hg, hu, wg, wu, wd, dz, x_in, scale), scratch_shapes=[pltpu.VMEM((tm, d), F32)], hosted=hosted)


def linear_nt_mod_bwd(pairs, dz, x_in, scale, name, hosted=None):
    t, d = dz.shape
    nb = scale.shape[0]
    tm = TOKEN_TILE
    tps = (t // nb) // tm
    npairs = len(pairs)

    def body(*refs):
        dh_refs = refs[:npairs]
        w_refs = refs[npairs:2 * npairs]
        dz_ref, x_ref, sc_ref, dx_ref, dsc_ref, dsh_ref = refs[2 * npairs:]
        du = None
        for (_, _, blk), dh_ref, w_ref in zip(pairs, dh_refs, w_refs):
            dh = dh_ref[...].astype(BF16)
            term = _dot_nt(dh, w_ref[...]) if blk is None else _dot(dh, w_ref[...])
            du = term if du is None else du + term
        _mod_bwd_tail(du, dz_ref, x_ref, sc_ref, dx_ref, dsc_ref, dsh_ref, pl.program_id(0) % tps == 0)

    rows = pl.BlockSpec((tm, d), lambda i: (i, 0))
    bvec = pl.BlockSpec((None, 1, d), lambda i: (i // tps, 0, 0))
    in_specs = [pl.BlockSpec((tm, dh.shape[1]), lambda i: (i, 0)) for dh, _, _ in pairs]
    for dh, w, blk in pairs:
        if blk is None:
            in_specs.append(pl.BlockSpec(w.shape, lambda i: (0, 0)))
        else:
            in_specs.append(pl.BlockSpec((dh.shape[1], d), lambda i, blk=blk: (blk, 0)))
    in_specs += [rows, rows, bvec]
    return _call(
        body, name, (t // tm,), in_specs, [rows, bvec, bvec],
        [_sds((t, d), F32), _sds((nb, 1, d), F32), _sds((nb, 1, d), F32)],
        (*[dh for dh, _, _ in pairs], *[w for _, w, _ in pairs], dz, x_in, scale), hosted=hosted)


def linear_nt_delta(dy, w_o, o, head_sel, name):
    t, d = dy.shape
    hdv = w_o.shape[0]
    tm = TOKEN_TILE

    def body(dy_ref, w_ref, o_ref, sel_ref, do_ref, dl_ref):
        do = _dot_nt(dy_ref[...], w_ref[...])
        do_ref[...] = do.astype(BF16)
        dl_ref[...] = _dot_f32(do * o_ref[...].astype(F32), sel_ref[...])

    return pl.pallas_call(
        body, name=name, grid=(t // tm,),
        in_specs=[pl.BlockSpec((tm, d), lambda i: (i, 0)), pl.BlockSpec((hdv, d), lambda i: (0, 0)),
                  pl.BlockSpec((tm, hdv), lambda i: (i, 0)), pl.BlockSpec(head_sel.shape, lambda i: (0, 0))],
        out_specs=[pl.BlockSpec((tm, hdv), lambda i: (i, 0)), pl.BlockSpec((tm, 128), lambda i: (i, 0))],
        out_shape=[_sds((t, hdv), BF16), _sds((t, 128), F32)], compiler_params=_cp(1),
    )(dy, w_o, o, head_sel)


def _attn_bwd_blocks(j, nk, tk, scale, heads):
    def block(i, carry, masked):
        new = []
        for hd, (dk_acc, dv_acc, dfk_acc) in zip(heads, carry):
            qb = hd["q"](i)
            dob = hd["do"](i)
            lse_row, dl_row = hd["rows"](i)
            st = _dot_nt(hd["k"], qb)
            if scale is not None:
                st = st * scale
            if hd["bias"] is not None:
                fq_row, fk_col = hd["bias"](i)
                st = st + fq_row - fk_col
            if masked:
                keep = lax.broadcasted_iota(jnp.int32, st.shape, 1) >= lax.broadcasted_iota(jnp.int32, st.shape, 0)
                st = jnp.where(keep, st, -1e30)
            pt = jnp.exp(st - lse_row)
            dv_acc = dv_acc + _dot(pt.astype(BF16), dob)
            dst = pt * (_dot_nt(hd["v"], dob) - dl_row)
            if hd["add_dfq"] is not None:
                dfk_acc = dfk_acc - jnp.sum(dst, axis=1, keepdims=True)
                hd["add_dfq"](i, jnp.sum(dst, axis=0, keepdims=True))
            dsb = (dst if scale is None else dst * scale).astype(BF16)
            dk_acc = dk_acc + _dot(dsb, qb)
            hd["add_dq"](i, _dot_tn(dsb, hd["k"] if scale is not None else hd["k_scaled"]))
            new.append((dk_acc, dv_acc, dfk_acc))
        return tuple(new)

    init = tuple((jnp.zeros((tk, hd["k"].shape[1]), F32), jnp.zeros((tk, hd["v"].shape[1]), F32), jnp.zeros((tk, 1), F32))
                 for hd in heads)
    carry = block(j, init, True)
    return lax.fori_loop(j + 1, nk, lambda i, c: block(i, c, False), carry)


def fox_attn_bwd(qkv, do, cum, cum_rows, lse_rows, delta_rows, nb, name, hosted=None):
    t = qkv.shape[0]
    s = t // nb
    tk = ATTN_TILE
    nk = s // tk
    scale = FOX_HD ** -0.5

    def body(q_ref, k_ref, v_ref, do_ref, cum_ref, cr_ref, lr_ref, dr_ref, dq_ref, dk_ref, dv_ref, dfq_ref, dfk_ref):
        hg = pl.program_id(1)
        j = pl.program_id(2)

        @pl.when(j == 0)
        def _():
            dq_ref[...] = jnp.zeros_like(dq_ref)

        @pl.when((j == 0) & (hg == 0))
        def _():
            dfq_ref[...] = jnp.zeros_like(dfq_ref)
            dfk_ref[...] = jnp.zeros_like(dfk_ref)

        low = lax.broadcasted_iota(jnp.int32, (tk, 128), 1) < FOX_HD
        cum_t = cum_ref[...]

        def rows_of(i):
            return pl.ds(pl.multiple_of(i * tk, tk), tk)

        def head(a):
            hd = FOX_GROUP * hg + a
            cols = slice(128 * (a // 2), 128 * (a // 2) + 128)
            half = low if a % 2 == 0 else jnp.logical_not(low)
            kb = k_ref[:, cols]
            vb = v_ref[:, cols]
            fk = _pick_lane(cum_t, hd)

            def add_dq(i, val):
                dq_ref[rows_of(i), cols] += val

            def add_dfq(i, val):
                dfq_ref[i] = _put_row(dfq_ref[i], hd, val)

            ka = jnp.where(half, kb, jnp.zeros_like(kb))
            return dict(q=lambda i: q_ref[rows_of(i), cols] * scale, do=lambda i: do_ref[rows_of(i), cols],
                        k=ka, k_scaled=ka * scale, v=jnp.where(half, vb, jnp.zeros_like(vb)),
                        rows=lambda i: (_pick_row(lr_ref[i], hd), _pick_row(dr_ref[i], hd)),
                        bias=lambda i: (_pick_row(cr_ref[i], hd), fk), add_dq=add_dq, add_dfq=add_dfq)

        res = _attn_bwd_blocks(j, nk, tk, None, [head(a) for a in range(FOX_GROUP)])
        dk_ref[...] = jnp.concatenate([jnp.where(low, res[a][0], res[a + 1][0]) for a in range(0, FOX_GROUP, 2)],
                                      axis=1).astype(BF16)
        dv_ref[...] = jnp.concatenate([jnp.where(low, res[a][1], res[a + 1][1]) for a in range(0, FOX_GROUP, 2)],
                                      axis=1).astype(BF16)
        for a in range(FOX_GROUP):
            dfk_ref[j] = _put_row(dfk_ref[j], FOX_GROUP * hg + a, jnp.broadcast_to(res[a][2], (tk, 128)).T[0:1, :])

    wide = FOX_GROUP * FOX_HD
    ngroups = FOX_HEADS // FOX_GROUP
    rowsp = pl.BlockSpec((nk, 16, tk), lambda b, hg, j: (b, 0, 0))
    return _call(
        body, name, (nb, ngroups, nk),
        [pl.BlockSpec((s, wide), lambda b, hg, j: (b, hg)),
         pl.BlockSpec((tk, wide), lambda b, hg, j: (b * nk + j, ngroups + hg)),
         pl.BlockSpec((tk, wide), lambda b, hg, j: (b * nk + j, 2 * ngroups + hg)),
         pl.BlockSpec((s, wide), lambda b, hg, j: (b, hg)),
         pl.BlockSpec((tk, 128), lambda b, hg, j: (b * nk + j, 0)),
         rowsp, rowsp, rowsp],
        [pl.BlockSpec((s, wide), lambda b, hg, j: (b, hg)),
         pl.BlockSpec((tk, wide), lambda b, hg, j: (b * nk + j, hg)),
         pl.BlockSpec((tk, wide), lambda b, hg, j: (b * nk + j, hg)),
         rowsp, rowsp],
        [_sds((t, D_MODEL), F32), _sds((t, D_MODEL), BF16), _sds((t, D_MODEL), BF16),
         _sds((t // tk, 16, tk), F32), _sds((t // tk, 16, tk), F32)],
        (qkv, qkv, qkv, do, cum, cum_rows, lse_rows, delta_rows), hosted=hosted)


def mla_attn_bwd(q, kn, kr2, v, do, lse_rows, delta_rows, nb, name, hosted=None):
    t = q.shape[0]
    s = t // nb
    tk = ATTN_TILE
    nk = s // tk
    ngroups = MLA_HEADS // MLA_GROUP
    wide = MLA_GROUP * MLA_NOPE
    rwide = MLA_GROUP * MLA_ROPE
    scale = (MLA_NOPE + MLA_ROPE) ** -0.5

    def body(qn_ref, qr_ref, kn_ref, kr_ref, v_ref, do_ref, lr_ref, dr_ref, dqn_ref, dqr_ref, dkn_ref, dkr_ref, dv_ref):
        hg = pl.program_id(1)
        j = pl.program_id(2)

        @pl.when(j == 0)
        def _():
            dqn_ref[...] = jnp.zeros_like(dqn_ref)
            dqr_ref[...] = jnp.zeros_like(dqr_ref)

        low = lax.broadcasted_iota(jnp.int32, (tk, 128), 1) < MLA_ROPE
        kr = kr_ref[...]

        def rows_of(i):
            return pl.ds(pl.multiple_of(i * tk, tk), tk)

        def head(a):
            cols = slice(a * MLA_NOPE, (a + 1) * MLA_NOPE)
            rcols = slice(128 * (a // 2), 128 * (a // 2) + 128)
            mine = low if a % 2 == 0 else jnp.logical_not(low)
            hd = MLA_GROUP * hg + a

            def q_fn(i):
                qr = qr_ref[rows_of(i), rcols]
                return jnp.concatenate([qn_ref[rows_of(i), cols], jnp.where(mine, qr, jnp.zeros_like(qr))], axis=1)

            def add_dq(i, val):
                dqn_ref[rows_of(i), cols] += val[:, :MLA_NOPE]
                dqr_ref[rows_of(i), cols] += val[:, MLA_NOPE:]

            return dict(q=q_fn, do=lambda i: do_ref[rows_of(i), cols], k=jnp.concatenate([kn_ref[:, cols], kr], axis=1),
                        v=v_ref[:, cols], rows=lambda i: (_pick_row(lr_ref[i], hd), _pick_row(dr_ref[i], hd)),
                        bias=None, add_dq=add_dq, add_dfq=None)

        res = _attn_bwd_blocks(j, nk, tk, scale, [head(a) for a in range(MLA_GROUP)])
        dkn_ref[...] = jnp.concatenate([r[0][:, :MLA_NOPE] for r in res], axis=1).astype(BF16)
        dkr_ref[...] = jnp.concatenate([r[0][:, MLA_NOPE:] for r in res], axis=1).astype(BF16)
        dv_ref[...] = jnp.concatenate([r[1] for r in res], axis=1).astype(BF16)

    full = pl.BlockSpec((s, wide), lambda b, hg, j: (b, hg))
    blk = pl.BlockSpec((tk, wide), lambda b, hg, j: (b * nk + j, hg))
    rowsp = pl.BlockSpec((nk, 16, tk), lambda b, hg, j: (b, 0, 0))
    total = MLA_HEADS * MLA_V
    rope0 = MLA_HEADS * MLA_NOPE // rwide
    return _call(
        body, name, (nb, ngroups, nk),
        [full, pl.BlockSpec((s, rwide), lambda b, hg, j: (b, rope0 + hg)), blk,
         pl.BlockSpec((tk, 128), lambda b, hg, j: (b * nk + j, 0)), blk, full, rowsp, rowsp],
        [full, full, blk, blk, blk],
        [_sds((t, total), F32), _sds((t, total), F32), _sds((t, total), BF16), _sds((t, total), BF16),
         _sds((t, total), BF16)],
        (q, q, kn, kr2, v, do, lse_rows, delta_rows), hosted=hosted)


def mla_mid_bwd(dqn, dqr, dkn, dv, dkr_heads, h, g_q, g_kv, w_uq, w_uk, w_uv, cos8, sin8, cos64, sin64s, swap64,
                heads_to_rope, head_sum, name):
    t = h.shape[0]
    tm = TOKEN_TILE
    hq = MLA_HEADS * MLA_NOPE
    hr = MLA_HEADS * MLA_ROPE // 2
    nq = w_uq.shape[1]

    def body(dqn_ref, dqr_ref, dkn_ref, dv_ref, dkr_ref, h_ref, gq_ref, gkv_ref, wuq_ref, wuk_ref, wuv_ref,
             c8_ref, s8_ref, c64_ref, s64_ref, sw_ref, hp_ref, hs_ref, dh_ref, dqp_ref, dgq_ref, dgkv_ref):
        @pl.when(pl.program_id(0) == 0)
        def _():
            dgq_ref[...] = jnp.zeros_like(dgq_ref)
            dgkv_ref[...] = jnp.zeros_like(dgkv_ref)

        drot = _dot(dqr_ref[...].astype(BF16), hp_ref[...])
        o1 = drot[:, :hr]
        o2 = drot[:, hr:]
        cs = c8_ref[...]
        sn = s8_ref[...]
        dqp = jnp.concatenate([dqn_ref[...].astype(BF16), (o1 * cs + o2 * sn).astype(BF16),
                               (o2 * cs - o1 * sn).astype(BF16)], axis=1)
        dqp_ref[...] = dqp
        dcq = _dot_nt(dqp, wuq_ref[...])
        dckv = _dot_nt(dkn_ref[...], wuk_ref[...]) + _dot_nt(dv_ref[...], wuv_ref[...])
        hh = h_ref[...]

        def rms_bwd(hpart, g, dc, dg_ref):
            hhat, rstd = _rms(hpart, None)
            dg_ref[...] += jnp.sum(dc * hhat, axis=0, keepdims=True)
            dcg = dc * g
            return rstd * (dcg - hhat * jnp.mean(dcg * hhat, axis=-1, keepdims=True))

        dhq = rms_bwd(hh[:, :MLA_QR], gq_ref[...], dcq, dgq_ref)
        dhkv = rms_bwd(hh[:, MLA_QR:MLA_QR + MLA_KVR], gkv_ref[...], dckv, dgkv_ref)
        dkr = _dot(dkr_ref[...], hs_ref[...])
        dkr_pre = dkr * c64_ref[...] + _dot_f32(dkr * s64_ref[...], sw_ref[...])
        dh_ref[...] = jnp.concatenate([dhq, dhkv, dkr_pre], axis=1).astype(BF16)

    def rows(n):
        return pl.BlockSpec((tm, n), lambda i: (i, 0))

    def whole(a):
        return pl.BlockSpec(a.shape, lambda i: (0,) * a.ndim)

    return pl.pallas_call(
        body, name=name, grid=(t // tm,),
        in_specs=[rows(hq), rows(hq), rows(hq), rows(hq), rows(hq), rows(h.shape[1]), whole(g_q), whole(g_kv),
                  whole(w_uq), whole(w_uk), whole(w_uv), rows(hr), rows(hr), rows(MLA_ROPE), rows(MLA_ROPE),
                  whole(swap64), whole(heads_to_rope), whole(head_sum)],
        out_specs=[rows(h.shape[1]), rows(nq), pl.BlockSpec((1, MLA_QR), lambda i: (0, 0)),
                   pl.BlockSpec((1, MLA_KVR), lambda i: (0, 0))],
        out_shape=[_sds((t, h.shape[1]), BF16), _sds((t, nq), BF16), _sds((1, MLA_QR), F32), _sds((1, MLA_KVR), F32)],
        compiler_params=_cp(1),
    )(dqn, dqr, dkn, dv, dkr_heads, h, g_q, g_kv, w_uq, w_uk, w_uv, cos8, sin8, cos64, sin64s, swap64,
      heads_to_rope, head_sum)


def fox_gate_bwd(dcum, hf, b_f, triu, n_batch, name):
    t, n = hf.shape
    blk = triu.shape[0]
    nb = (t // n_batch) // blk

    def body(dc_ref, hf_ref, b_ref, tri_ref, o_ref, db_ref, carry_ref):
        @pl.when(pl.program_id(1) == 0)
        def _():
            carry_ref[...] = jnp.zeros_like(carry_ref)

        @pl.when((pl.program_id(0) == 0) & (pl.program_id(1) == 0))
        def _():
            db_ref[...] = jnp.zeros_like(db_ref)

        rc = _dot_f32(tri_ref[...], dc_ref[...]) + carry_ref[...]
        carry_ref[...] = rc[0:1, :]
        dhf = rc * jax.nn.sigmoid(-(hf_ref[...] + b_ref[...]))
        o_ref[...] = dhf.astype(BF16)
        db_ref[...] += jnp.sum(dhf, axis=0, keepdims=True)

    rev = pl.BlockSpec((blk, n), lambda bb, i: (bb * nb + nb - 1 - i, 0))
    return pl.pallas_call(
        body, name=name, grid=(n_batch, nb),
        in_specs=[rev, rev, pl.BlockSpec((1, n), lambda bb, i: (0, 0)), pl.BlockSpec((blk, blk), lambda bb, i: (0, 0))],
        out_specs=[rev, pl.BlockSpec((1, n), lambda bb, i: (0, 0))],
        out_shape=[_sds((t, n), BF16), _sds((1, n), F32)], scratch_shapes=[pltpu.VMEM((1, n), F32)],
        compiler_params=_cp(2),
    )(dcum, hf, b_f, triu)


def wgrad(a, bm, name, with_bf16=False, bt=WGRAD_TOKENS):
    ca, t, kd = a.shape
    cb, _, nd = bm.shape
    c = max(ca, cb)
    bn = nd
    if nd > 1024 and nd % 1024 == 0:
        bn = 1024
    nsteps = t // bt

    def body(a_ref, b_ref, o_ref, *rest):
        @pl.when(pl.program_id(2) == 0)
        def _():
            o_ref[...] = jnp.zeros_like(o_ref)

        o_ref[...] += _dot_tn(a_ref[...].astype(BF16), b_ref[...].astype(BF16))
        if with_bf16:
            @pl.when(pl.program_id(2) == nsteps - 1)
            def _():
                rest[0][...] = o_ref[...].astype(BF16)

    out_spec = pl.BlockSpec((None, kd, bn), lambda cc, n, tt: (cc, 0, n))
    res = pl.pallas_call(
        body, name=name, grid=(c, nd // bn, nsteps),
        in_specs=[pl.BlockSpec((None, bt, kd), lambda cc, n, tt: (cc if ca > 1 else 0, tt, 0)),
                  pl.BlockSpec((None, bt, bn), lambda cc, n, tt: (cc if cb > 1 else 0, tt, n))],
        out_specs=[out_spec, out_spec] if with_bf16 else out_spec,
        out_shape=[_sds((c, kd, nd), F32), _sds((c, kd, nd), BF16)] if with_bf16 else _sds((c, kd, nd), F32),
        compiler_params=_cp(3),
    )(a, bm)
    return res


def ada_mod_part(c_all, ada_w, name):
    nl, d, n = ada_w.shape
    rows = c_all.shape[0]
    tn = 512

    def body(c_ref, w_ref, o_ref):
        cv = c_ref[...]
        act = (cv * jax.nn.sigmoid(cv)).astype(BF16)
        o_ref[...] = _dot(act, w_ref[...].astype(BF16))

    return pl.pallas_call(
        body, name=name, grid=(nl, n // tn),
        in_specs=[pl.BlockSpec((rows, d), lambda l, j: (0, 0)), pl.BlockSpec((None, d, tn), lambda l, j: (l, 0, j))],
        out_specs=pl.BlockSpec((None, rows, tn), lambda l, j: (l, 0, j)),
        out_shape=_sds((nl, rows, n), F32), compiler_params=_cp(2),
    )(c_all, ada_w)


def ada_grad(c_all_t, dmod, name):
    nl, rows, n = dmod.shape
    d = c_all_t.shape[0]
    tn = 512

    def body(c_ref, dm_ref, o_ref):
        cv = c_ref[...]
        act = (cv * jax.nn.sigmoid(cv)).astype(BF16)
        o_ref[...] = _dot(act, dm_ref[...].astype(BF16))

    return pl.pallas_call(
        body, name=name, grid=(nl, n // tn),
        in_specs=[pl.BlockSpec((d, rows), lambda l, j: (0, 0)), pl.BlockSpec((None, rows, tn), lambda l, j: (l, 0, j))],
        out_specs=pl.BlockSpec((None, d, tn), lambda l, j: (l, 0, j)),
        out_shape=_sds((nl, d, n), F32), compiler_params=_cp(2),
    )(c_all_t, dmod)


def sum_leading(a, name):
    g, r, n = a.shape

    def body(a_ref, o_ref):
        acc = a_ref[0]
        for kk in range(1, g):
            acc = acc + a_ref[kk]
        o_ref[...] = acc

    return pl.pallas_call(
        body, name=name, grid=(1,), in_specs=[pl.BlockSpec((g, r, n), lambda i: (0, 0, 0))],
        out_specs=pl.BlockSpec((r, n), lambda i: (0, 0)), out_shape=_sds((r, n), F32), compiler_params=_cp(1),
    )(a)


def adamw(w, g, m, v, name):
    r, n = w.shape
    br = r
    for cand in (512, 256, 128, 64, 32, 16, 8):
        if r % cand == 0 and r > cand and cand * n * 4 <= ADAMW_BLOCK_BYTES:
            br = cand
            break
    c1 = 1.0 - ADAM_B1 ** ADAM_STEP
    c2 = 1.0 - ADAM_B2 ** ADAM_STEP

    def body(w_ref, g_ref, m_ref, v_ref, d_ref, mo_ref, vo_ref):
        gv = g_ref[...]
        mn = ADAM_B1 * m_ref[...] + (1.0 - ADAM_B1) * gv
        vn = ADAM_B2 * v_ref[...] + (1.0 - ADAM_B2) * (gv * gv)
        mo_ref[...] = mn
        vo_ref[...] = vn
        d_ref[...] = -ADAM_LR * ((mn / c1) / (jnp.sqrt(vn / c2) + ADAM_EPS) + ADAM_WD * w_ref[...])

    spec = pl.BlockSpec((br, n), lambda i: (i, 0))
    return _call(body, name, (r // br,), [spec] * 4, [spec] * 3, [_sds((r, n), F32)] * 3, (w, g, m, v))


def all_gather8(x_blk, name, hosted=None):
    m_per, n = x_blk.shape
    h_in = 0 if hosted is None else len(hosted.inputs)
    h_out = 0 if hosted is None else len(hosted.out_shape)

    def body(x_ref, *refs):
        c_in, (out_ref, *c_out), (send_sems, recv_sems, local_sem, *c_sem) = (
            refs[:h_in], refs[h_in:h_in + 1 + h_out], refs[h_in + 1 + h_out:])
        if hosted is not None:
            hosted.start(c_in, c_out, c_sem)
        gather(x_ref, out_ref, send_sems, recv_sems, local_sem)
        if hosted is not None:
            hosted.finish(c_in, c_out, c_sem)

    def gather(x_ref, out_ref, send_sems, recv_sems, local_sem):
        x, y, c = _place()
        me, sibling = (x, y, c), (x, y, 1 - c)
        chips = [(1 - x, y), (x, 1 - y), (1 - x, 1 - y)]

        def rows(px, py, pc):
            return out_ref.at[pl.ds((4 * px + 2 * py + pc) * m_per, m_per), :]

        def copy(k, block, to, src=None):
            return pltpu.make_async_remote_copy(
                src_ref=rows(*block) if src is None else src, dst_ref=rows(*block),
                send_sem=send_sems.at[k], recv_sem=recv_sems.at[k], device_id=to, device_id_type=MESH)

        mine = pltpu.make_async_copy(x_ref, rows(*me), local_sem)
        mine.start()
        first = [copy(0, me, sibling, src=x_ref)]
        first += [copy(1 + j, me, (*chip, c), src=x_ref) for j, chip in enumerate(chips)]
        for cp in first:
            cp.start()
        passed = [copy(4 + j, (*chip, c), sibling) for j, chip in enumerate(chips)]
        for j, chip in enumerate(chips):
            copy(1 + j, (*chip, c), me).wait_recv()
            passed[j].start()
        copy(0, sibling, me).wait_recv()
        for j, chip in enumerate(chips):
            copy(4 + j, (*chip, 1 - c), me).wait_recv()
        for cp in first + passed:
            cp.wait_send()
        mine.wait()

    hbm = pl.BlockSpec(memory_space=pl.ANY)
    vmem = pl.BlockSpec(memory_space=pltpu.VMEM)
    res = pl.pallas_call(
        body, name=name,
        out_shape=[_sds((8 * m_per, n), x_blk.dtype)] + ([] if hosted is None else list(hosted.out_shape)),
        in_specs=[vmem] + [hbm] * h_in, out_specs=[vmem] + [hbm] * h_out,
        scratch_shapes=[pltpu.SemaphoreType.DMA((7,)), pltpu.SemaphoreType.DMA((7,)), pltpu.SemaphoreType.DMA]
        + ([] if hosted is None else list(hosted.sems)),
        compiler_params=pltpu.CompilerParams(vmem_limit_bytes=VMEM_LIMIT),
    )(x_blk, *([] if hosted is None else hosted.inputs))
    return res[0] if hosted is None else (res[0], res[1:])


def _gather_comm(shards):
    nt = len(shards)

    def parts(w_refs, out_refs, sems, finishing):
        send_sems, recv_sems, own_send, own_recv = sems
        x, y, c = _place()
        sibling = (x, y, 1 - c)
        chips = [(1 - x, y), (x, 1 - y), (1 - x, 1 - y)]

        def copy(t, k, block, to, src=None):
            px, py, hh = block
            dst = out_refs[t].at[2 * px + py, hh]
            return pltpu.make_async_remote_copy(
                src_ref=dst if src is None else src, dst_ref=dst,
                send_sem=send_sems.at[6 * t + k], recv_sem=recv_sems.at[6 * t + k], device_id=to, device_id_type=MESH)

        own = [pltpu.make_async_remote_copy(
            src_ref=w_refs[t], dst_ref=out_refs[t].at[2 * x + y], send_sem=own_send.at[t], recv_sem=own_recv.at[t],
            device_id=sibling, device_id_type=MESH) for t in range(nt)]
        first = [copy(t, j, (x, y, c), (*chip, c), src=w_refs[t].at[c]) for t in range(nt) for j, chip in enumerate(chips)]
        if not finishing:
            return own, first
        landed = [copy(t, j, (*chip, c), (x, y, c)) for t in range(nt) for j, chip in enumerate(chips)]
        passed = [copy(t, 3 + j, (*chip, c), sibling) for t in range(nt) for j, chip in enumerate(chips)]
        from_sibling = [copy(t, 3 + j, (*chip, 1 - c), (x, y, c)) for t in range(nt) for j, chip in enumerate(chips)]
        return own, first, landed, passed, from_sibling

    def start(w_refs, out_refs, sems):
        own, first = parts(w_refs, out_refs, sems, False)
        for cp in own + first:
            cp.start()

    def finish(w_refs, out_refs, sems):
        own, first, landed, passed, from_sibling = parts(w_refs, out_refs, sems, True)
        for arrived, fwd in zip(landed, passed):
            arrived.wait_recv()
            fwd.start()
        for cp in from_sibling:
            cp.wait_recv()
        for cp in first + passed:
            cp.wait_send()
        for cp in own:
            cp.wait()

    sems = [pltpu.SemaphoreType.DMA((6 * nt,)), pltpu.SemaphoreType.DMA((6 * nt,)),
            pltpu.SemaphoreType.DMA((nt,)), pltpu.SemaphoreType.DMA((nt,))]
    return _Hosted(list(shards), [_sds((N_CHIPS, *w.shape), w.dtype) for w in shards], sems, start, finish)


def _row_block(r, n, itemsize):
    best = None
    for br in range(16, r + 1, 16):
        if r % br == 0 and br * n * itemsize <= COMM_BLOCK_BYTES:
            best = br
    return r if best is None else best


def _scatter_comm(parts):
    nt = len(parts)

    def copies(p_refs, b_refs, sems, arriving):
        send_sems, recv_sems = sems
        x, y, c = _place()
        me = 4 * x + 2 * y + c
        cps = []
        for t in range(nt):
            for r in range(1, 8):
                tx = 1 - x if r & 4 else x
                ty = 1 - y if r & 2 else y
                tc = 1 - c if r & 1 else c
                src, dst = (2 * x + y, c), 4 * tx + 2 * ty + tc
                if not arriving:
                    src, dst = (2 * tx + ty, tc), me
                cps.append(pltpu.make_async_remote_copy(
                    src_ref=p_refs[t].at[src], dst_ref=b_refs[t].at[dst], send_sem=send_sems.at[7 * t + r - 1],
                    recv_sem=recv_sems.at[7 * t + r - 1], device_id=(tx, ty, tc), device_id_type=MESH))
        return cps

    def start(p_refs, b_refs, sems):
        for cp in copies(p_refs, b_refs, sems, False):
            cp.start()

    def finish(p_refs, b_refs, sems):
        for cp in copies(p_refs, b_refs, sems, True):
            cp.wait_recv()
        for cp in copies(p_refs, b_refs, sems, False):
            cp.wait_send()

    sems = [pltpu.SemaphoreType.DMA((7 * nt,)), pltpu.SemaphoreType.DMA((7 * nt,))]
    return _Hosted(list(parts), [_sds((2 * N_CHIPS, *p.shape[2:]), p.dtype) for p in parts], sems, start, finish)


def sum_devices(own, recv, place, name, slot=(0, 1, None)):
    _, _, r, n = own.shape
    layer, n_layers, buf = slot
    br = _row_block(r, n, 4 * 8)

    def body(p_ref, o_ref, *rest):
        acc = o_ref[...]
        for kk in range(7):
            acc = acc + rest[kk][...].astype(F32)
        rest[-1][...] = acc

    def arrived(rel):
        return pl.BlockSpec((None, br, n), lambda i, pref: (jnp.bitwise_xor(pref[0], rel), i, 0))

    in_specs = [pl.BlockSpec((None, None, br, n), lambda i, pref: (pref[2], pref[1], i, 0))]
    in_specs += [arrived(rel) for rel in range(1, 8)]
    args = [own] + [recv] * 7
    aliases = {}
    if buf is not None:
        in_specs.append(pl.BlockSpec(memory_space=pl.ANY))
        args.append(buf)
        aliases = {9: 0}
    return pl.pallas_call(
        body, name=name,
        grid_spec=pltpu.PrefetchScalarGridSpec(
            num_scalar_prefetch=1, grid=(r // br,), in_specs=in_specs,
            out_specs=pl.BlockSpec((None, None, br, n), lambda i, pref: (layer, pref[1], i, 0))),
        out_shape=_sds((n_layers, 2, r, n), F32), input_output_aliases=aliases, compiler_params=_cp(1),
    )(place, *args)


def _join_comm(bufs):
    nt = len(bufs)
    layers = [bf.shape[0] for bf in bufs]
    first = [sum(layers[:t]) for t in range(nt)]

    def copies(o_refs, sems, own):
        send_sems, recv_sems = sems
        x, y, c = _place()
        hh = c if own else 1 - c
        return [pltpu.make_async_remote_copy(
            src_ref=o_refs[t].at[l, hh], dst_ref=o_refs[t].at[l, hh], send_sem=send_sems.at[first[t] + l],
            recv_sem=recv_sems.at[first[t] + l], device_id=(x, y, 1 - c), device_id_type=MESH)
            for t in range(nt) for l in range(layers[t])]

    def start(_, o_refs, sems):
        for cp in copies(o_refs, sems, True):
            cp.start()

    def finish(_, o_refs, sems):
        for cp in copies(o_refs, sems, False):
            cp.wait_recv()
        for cp in copies(o_refs, sems, True):
            cp.wait_send()

    sems = [pltpu.SemaphoreType.DMA((sum(layers),)), pltpu.SemaphoreType.DMA((sum(layers),))]
    return _Hosted(list(bufs), [_sds(bf.shape, bf.dtype) for bf in bufs], sems, start, finish, in_place=True)


def sibling_join_halves(bufs, name):
    comm = _join_comm(bufs)
    nt = len(bufs)

    def body(*refs):
        comm.start(refs[:nt], refs[nt:2 * nt], refs[2 * nt:])
        comm.finish(refs[:nt], refs[nt:2 * nt], refs[2 * nt:])

    hbm = pl.BlockSpec(memory_space=pl.ANY)
    return pl.pallas_call(body, name=name, out_shape=comm.out_shape, in_specs=[hbm] * nt, out_specs=[hbm] * nt,
                          input_output_aliases={k: k for k in range(nt)}, scratch_shapes=comm.sems)(*bufs)


_SHARD_KIND = {"mla_w_in": "rows", "mla_w_uq": "cols", "mla_w_uk": "cols", "mla_w_uv": "cols", "mla_w_o": "rows",
               "fox_w_in": "cols", "fox_w_o": "rows", "ffn_w_gate": "chunk", "ffn_w_up": "chunk", "ffn_w_down": "chunk"}
_PACKED = tuple(_SHARD_KIND)
_TRANSPOSED = ("ffn_w_gate", "ffn_w_up", "fox_w_in")


def _halves(shard):
    if shard.ndim == 3 and shard.shape[0] == 2:
        return shard
    r, n = shard.shape[-2:]
    return shard.reshape(2, r // 2, n)


def _cols_to_full(g):
    return jnp.transpose(g, (1, 0, 2)).reshape(g.shape[1], -1)


def _full_to_cols(w):
    k, n4 = w.shape
    return jnp.transpose(w.reshape(k, N_CHIPS, n4 // N_CHIPS), (1, 0, 2))


def _uq_perm():
    per = MLA_NOPE + MLA_ROPE
    half = MLA_ROPE // 2
    nope = [h * per + d for h in range(MLA_HEADS) for d in range(MLA_NOPE)]
    r1 = [h * per + MLA_NOPE + r for h in range(MLA_HEADS) for r in range(half)]
    r2 = [h * per + MLA_NOPE + half + r for h in range(MLA_HEADS) for r in range(half)]
    perm = np.array(nope + r1 + r2, dtype=np.int32)
    return perm, np.argsort(perm).astype(np.int32)


def _rope_matrices():
    half = MLA_ROPE // 2
    nr = MLA_HEADS * MLA_ROPE
    to_heads = np.zeros((nr, nr), np.float32)
    from_heads = np.zeros((MLA_HEADS * 128, nr), np.float32)
    for e in range(2):
        for h in range(MLA_HEADS):
            for r in range(half):
                to_heads[e * MLA_HEADS * half + h * half + r, h * MLA_ROPE + e * half + r] = 1.0
                from_heads[h * 128 + e * half + r, e * MLA_HEADS * half + h * half + r] = 1.0
    head_sum = np.tile(np.eye(MLA_ROPE, dtype=np.float32), (2 * MLA_HEADS, 1))
    dup = np.concatenate([np.eye(MLA_ROPE, dtype=np.float32)] * 2, axis=1)
    return to_heads, from_heads, head_sum, dup


def _ffn_weights(gathered):
    return tuple(g.reshape(N_CHIPS, 2 * g.shape[2], g.shape[3]) for g in gathered)


def _fox_weights(gathered):
    w_in, w_o = gathered
    w_in = jnp.transpose(w_in, (0, 2, 1, 3)).reshape(N_CHIPS * w_in.shape[2], 2 * w_in.shape[3])
    return w_in, w_o.reshape(-1, w_o.shape[-1])


def _local_step(x, positions, target, mods, wts, ln_g, ln_b, mla_g_q, mla_g_kv, fox_b_f, shards=None):
    nb, s, d = x.shape
    t = nb * s
    x0 = x.reshape(t, d)
    tgt = target.reshape(t, d)
    perm, inv_perm = _uq_perm()

    half = MLA_ROPE // 2
    inv_freq = ROPE_THETA ** (-jnp.arange(half, dtype=F32) / half)
    ang = positions.astype(F32).reshape(t, 1) * inv_freq
    cos, sin = jnp.cos(ang), jnp.sin(ang)
    cos8, sin8 = jnp.tile(cos, (1, MLA_HEADS)), jnp.tile(sin, (1, MLA_HEADS))
    cos64 = jnp.concatenate([cos, cos], axis=1)
    sin64s = jnp.concatenate([-sin, sin], axis=1)
    swap64 = jnp.asarray(np.roll(np.eye(MLA_ROPE, dtype=np.float32), half, axis=1))
    to_heads, from_heads, head_sum, dup = _rope_matrices()
    to_heads, from_heads = jnp.asarray(to_heads, dtype=BF16), jnp.asarray(from_heads, dtype=BF16)
    head_sum, dup = jnp.asarray(head_sum, dtype=BF16), jnp.asarray(dup, dtype=BF16)
    sel_mla = jnp.asarray(np.pad(np.kron(np.eye(MLA_HEADS, dtype=np.float32), np.ones((MLA_V, 1), np.float32)),
                                 ((0, 0), (0, 128 - MLA_HEADS))))
    sel_fox = jnp.asarray(np.pad(np.kron(np.eye(FOX_HEADS, dtype=np.float32), np.ones((FOX_HD, 1), np.float32)),
                                 ((0, 0), (0, 128 - FOX_HEADS))))
    tri = jnp.asarray(np.tril(np.ones((128, 128), np.float32)))
    triu = jnp.asarray(np.triu(np.ones((128, 128), np.float32)))
    onehot16 = jnp.asarray(np.eye(16, 128, dtype=np.float32))

    def vec(a):
        return a.reshape(1, -1)

    def carried(key):
        return None if shards is None else _gather_comm(shards[key])

    def split(res):
        return (res, None) if shards is None else res

    w_uq_p = wts["mla_w_uq"][:, perm]
    b_f_pad = jnp.pad(fox_b_f.reshape(1, -1), ((0, 0), (0, 128 - FOX_HEADS)))

    sh_a, sc_a, gt_a, sh_f, sc_f, gt_f = mods[0]
    h_in, u_m = mod_linear(x0, sh_a, sc_a, wts["mla_w_in"], F32, "mla_in", emit_u=True)
    q_m, kn_m, v_m, kr2_m, cq_m, ckv_m = mla_mid_fwd(
        h_in, vec(mla_g_q), vec(mla_g_kv), w_uq_p, wts["mla_w_uk"], wts["mla_w_uv"], cos8, sin8, cos64, sin64s, swap64,
        to_heads, dup, "mla_mid")
    (o_m, lse_m), got = split(mla_attn_fwd(q_m, kn_m, kr2_m, v_m, nb, "mla_attn", hosted=carried("ffn0")))
    ffn0_w = wts["ffn"][0] if got is None else _ffn_weights(got)
    y0, x1 = linear_resid_ln(o_m, wts["mla_w_o"], x0, gt_a, vec(ln_g[0, 0]), vec(ln_b[0, 0]), "mla_out")
    (u_f0, hg0, hu0, y1, x2), got = split(ffn_fwd(x1, sh_f, sc_f, gt_f, *ffn0_w, vec(ln_g[0, 1]), vec(ln_b[0, 1]), "ffn0",
                                                  hosted=carried("fox")))
    fox_w_in_t, fox_w_o = (wts["fox_w_in"].T, wts["fox_w_o"]) if got is None else _fox_weights(got)
    fox_w_f_t = jnp.pad(fox_w_in_t[3 * d:], ((0, 128 - FOX_HEADS), (0, 0)))
    sh_a1, sc_a1, gt_a1, sh_f1, sc_f1, gt_f1 = mods[1]
    qkv, u_x = mod_linear(x2, sh_a1, sc_a1, fox_w_in_t, BF16, "fox_qkv", tn=1024, emit_u=True, w_rows=3 * d)
    hf = mod_linear(x2, sh_a1, sc_a1, fox_w_f_t, F32, "fox_f", w_rows=128)
    cum = fox_gate_fwd(hf, b_f_pad, tri, nb, "fox_gate")
    cum_rows = rows16(cum, "fox_cum_rows")
    (o_x, lse_x), got = split(fox_attn_fwd(qkv, cum, cum_rows, nb, "fox_attn", hosted=carried("ffn1")))
    ffn1_w = wts["ffn"][1] if got is None else _ffn_weights(got)
    y2, x3 = linear_resid_ln(o_x, fox_w_o, x2, gt_a1, vec(ln_g[1, 0]), vec(ln_b[1, 0]), "fox_out")
    u_f1, hg1, hu1, y3, x4 = ffn_fwd(x3, sh_f1, sc_f1, gt_f1, *ffn1_w, vec(ln_g[1, 1]), vec(ln_b[1, 1]), "ffn1")
    dx4, sq_err = loss_grad(x4, tgt, "loss")
    loss_part = 0.5 * jnp.sum(sq_err) / d

    parts, recv = {}, {}

    def halves_of(g):
        return g.reshape(N_CHIPS, 2, g.shape[1] // 2, g.shape[2])

    def scatter(keys, sent):
        return None if shards is None else _scatter_comm([sent[k] for k in keys])

    def landed(keys, got):
        if got is not None:
            recv.update(zip(keys, got))

    def ffn_grads(layer, u, dhg, dhu, act, dy):
        sent = {}
        for n, (a_op, b_op) in (("ffn_w_gate", (dhg, u[None])), ("ffn_w_up", (dhu, u[None])), ("ffn_w_down", (act, dy[None]))):
            g32, g16 = wgrad(a_op, b_op, "ffn%d_d%s" % (layer, n[4:]), with_bf16=True)
            parts["%s/%d" % (n, layer)], sent["%s/%d" % (n, layer)] = halves_of(g32), halves_of(g16)
        return sent

    dz3, dy3, dg11, db11, dgt_f1 = ln_bwd(dx4, x3, y3, gt_f1, vec(ln_g[1, 1]), "ffn1_ln_bwd")
    dhg1, dhu1, act1, dx3, dsc_f1, dsh_f1 = ffn_bwd(dy3, hg1, hu1, *ffn1_w, dz3, x3, sc_f1, "ffn1_bwd")
    sent = ffn_grads(1, u_f1, dhg1, dhu1, act1, dy3)
    dz2, dy2, dg10, db10, dgt_a1 = ln_bwd(dx3, x2, y2, gt_a1, vec(ln_g[1, 0]), "fox_ln_bwd")
    do_x, delta_x = linear_nt_delta(dy2, fox_w_o, o_x, sel_fox, "fox_out_bwd")
    (dq_x, dk_x, dv_x, dfq_x, dfk_x), got = split(fox_attn_bwd(
        qkv, do_x, cum, cum_rows, rows16(lse_x, "fox_lse_rows"), rows16(delta_x, "fox_delta_rows"), nb, "fox_attn_bwd",
        hosted=scatter(list(sent), sent)))
    landed(list(sent), got)
    dcum = tokens128(dfq_x + dfk_x, onehot16, "fox_dcum")
    dhf, dbf = fox_gate_bwd(dcum, hf, b_f_pad, triu, nb, "fox_gate_bwd")
    fox_d = [("q", dq_x), ("k", dk_x), ("v", dv_x)]
    dx2, dsc_a1, dsh_a1 = linear_nt_mod_bwd(
        [(dh, fox_w_in_t, i) for i, (_, dh) in enumerate(fox_d)] + [(dhf, fox_w_f_t, 0)], dz2, x2, sc_a1, "fox_in_bwd")
    dw_in_t = [wgrad(dh[None], u_x[None], "fox_dw" + tag)[0] for tag, dh in fox_d]
    dw_in_t.append(wgrad(dhf[None], u_x[None], "fox_dwf")[0][:FOX_HEADS])
    dw_in_t = jnp.concatenate(dw_in_t, axis=0).reshape(N_CHIPS, -1, 2, d // 2)
    parts["fox_w_in"] = jnp.transpose(dw_in_t, (0, 2, 1, 3))
    parts["fox_w_o"] = wgrad(o_x[None], dy2[None], "fox_dwo")[0].reshape(N_CHIPS, 2, -1, d)
    sent = {k: parts[k].astype(BF16) for k in ("fox_w_in", "fox_w_o")}
    dz1, dy1, dg01, db01, dgt_f0 = ln_bwd(dx2, x1, y1, gt_f, vec(ln_g[0, 1]), "ffn0_ln_bwd")
    (dhg0, dhu0, act0, dx1, dsc_f0, dsh_f0), got = split(ffn_bwd(dy1, hg0, hu0, *ffn0_w, dz1, x1, sc_f, "ffn0_bwd",
                                                                 hosted=scatter(list(sent), sent)))
    landed(list(sent), got)
    sent = ffn_grads(0, u_f0, dhg0, dhu0, act0, dy1)
    dz0, dy0, dg00, db00, dgt_a0 = ln_bwd(dx1, x0, y0, gt_a, vec(ln_g[0, 0]), "mla_ln_bwd")
    do_m, delta_m = linear_nt_delta(dy0, wts["mla_w_o"], o_m, sel_mla, "mla_out_bwd")
    parts["mla_w_o"] = wgrad(o_m[None], dy0[None], "mla_dwo")[0].reshape(N_CHIPS, 2, -1, d)
    sent["mla_w_o"] = parts["mla_w_o"].astype(BF16)
    (dqn_m, dqr_m, dkn_m, dkr_m, dv_m), got = split(mla_attn_bwd(
        q_m, kn_m, kr2_m, v_m, do_m, rows16(lse_m, "mla_lse_rows"), rows16(delta_m, "mla_delta_rows"), nb,
        "mla_attn_bwd", hosted=scatter(list(sent), sent)))
    landed(list(sent), got)
    dh_in, dq_pre, dgq, dgkv = mla_mid_bwd(
        dqn_m, dqr_m, dkn_m, dv_m, dkr_m, h_in, vec(mla_g_q), vec(mla_g_kv), w_uq_p, wts["mla_w_uk"],
        wts["mla_w_uv"], cos8, sin8, cos64, sin64s, swap64, from_heads, head_sum, "mla_mid_bwd")
    parts["mla_w_uq"] = halves_of(_full_to_cols(wgrad(cq_m[None], dq_pre[None], "mla_dwuq")[0][:, inv_perm]))
    parts["mla_w_uk"] = halves_of(_full_to_cols(wgrad(ckv_m[None], dkn_m[None], "mla_dwuk")[0]))
    parts["mla_w_uv"] = halves_of(_full_to_cols(wgrad(ckv_m[None], dv_m[None], "mla_dwuv")[0]))
    parts["mla_w_in"] = wgrad(u_m[None], dh_in[None], "mla_dwin")[0].reshape(N_CHIPS, 2, -1, h_in.shape[1])
    sent = {k: parts[k].astype(BF16) for k in ("mla_w_in", "mla_w_uq", "mla_w_uk", "mla_w_uv")}
    (dx0, dsc_a0, dsh_a0), got = split(linear_nt_mod_bwd([(dh_in, wts["mla_w_in"], None)], dz0, x0, sc_a, "mla_in_bwd",
                                                         hosted=scatter(list(sent), sent)))
    landed(list(sent), got)

    dmods = [(dsh_a0, dsc_a0, dgt_a0, dsh_f0, dsc_f0, dgt_f0), (dsh_a1, dsc_a1, dgt_a1, dsh_f1, dsc_f1, dgt_f1)]
    d_ln_g = jnp.stack([jnp.concatenate([dg00, dg01], axis=0), jnp.concatenate([dg10, dg11], axis=0)])
    d_ln_b = jnp.stack([jnp.concatenate([db00, db01], axis=0), jnp.concatenate([db10, db11], axis=0)])
    return loss_part, dx0.reshape(nb, s, d), (parts, recv), dmods, d_ln_g, d_ln_b, dgq, dgkv, dbf[:, :FOX_HEADS]


def _pad_rows(a, rows):
    return jnp.pad(a, ((0, rows - a.shape[0]), (0, 0)))


def kernel(x, c, positions, mla_w_in, mla_g_q, mla_w_uq, mla_g_kv, mla_w_uk, mla_w_uv, mla_w_o, fox_w_in, fox_b_f, fox_w_o, ada_w, ada_b, ffn_w_gate, ffn_w_up, ffn_w_down, ln_g, ln_b, loss_target, m_mla_w_in, m_mla_g_q, m_mla_w_uq, m_mla_g_kv, m_mla_w_uk, m_mla_w_uv, m_mla_w_o, m_fox_w_in, m_fox_b_f, m_fox_w_o, m_ada_w, m_ada_b, m_ffn_w_gate, m_ffn_w_up, m_ffn_w_down, m_ln_g, m_ln_b, v_mla_w_in, v_mla_g_q, v_mla_w_uq, v_mla_g_kv, v_mla_w_uk, v_mla_w_uv, v_mla_w_o, v_fox_w_in, v_fox_b_f, v_fox_w_o, v_ada_w, v_ada_b, v_ffn_w_gate, v_ffn_w_up, v_ffn_w_down, v_ln_g, v_ln_b):
    args = dict(locals())
    nb, s, d = x.shape
    ax, ay, ac = lax.axis_index("x"), lax.axis_index("y"), lax.axis_index("c")
    chip = 2 * ax + ay
    dev = 2 * chip + ac
    n_dev = 2 * N_CHIPS
    n_all = nb * n_dev

    shard_shapes = {n: (args[n].shape if _SHARD_KIND[n] == "chunk" else args[n].shape[1:]) for n in _PACKED}

    def block(n, layer=None):
        w = args[n].reshape(shard_shapes[n]) if layer is None else args[n][layer]
        return _halves(w.astype(BF16))

    mla_names = [n for n in _PACKED if n.startswith("mla")]
    ffn_names = ("ffn_w_gate", "ffn_w_up", "ffn_w_down")
    fox_in_t = jnp.swapaxes(fox_w_in, 1, 2)[0].astype(BF16)
    fox_in_t = jnp.stack([fox_in_t[:, :d // 2], fox_in_t[:, d // 2:]])
    shards = {"ffn0": [block(n, 0) for n in ffn_names], "fox": [fox_in_t, block("fox_w_o")],
              "ffn1": [block(n, 1) for n in ffn_names]}

    ln_cols = ln_g.shape[-1]
    ln_blk = jnp.concatenate([ln_g.reshape(2 * DEPTH, ln_cols), ln_b.reshape(2 * DEPTH, ln_cols)], axis=0)
    early = jnp.concatenate([_pad_rows(c, 8), jnp.pad(_pad_rows(ln_blk, 8), ((0, 0), (0, d - ln_cols)))], axis=0)
    early, mla_all = all_gather8(early, "gather_c_ln_mla", hosted=_gather_comm([block(n) for n in mla_names]))
    wts = {}
    for n, g in zip(mla_names, mla_all):
        g = g.reshape(N_CHIPS, *shard_shapes[n])
        wts[n] = g.reshape(-1, g.shape[-1]) if _SHARD_KIND[n] == "rows" else _cols_to_full(g)
    early = early.reshape(n_dev, 16, d)
    c_all = early[:, :nb].reshape(n_all, d)
    ln_all = early.reshape(N_CHIPS, 2, 16, d)[:, 0, 8:8 + 4 * DEPTH, :ln_cols]
    ln_all = jnp.transpose(ln_all, (1, 0, 2)).reshape(4 * DEPTH, d)
    ln_g_full = ln_all[:2 * DEPTH].reshape(DEPTH, 2, d)
    ln_b_full = ln_all[2 * DEPTH:].reshape(DEPTH, 2, d)
    mod_part = ada_mod_part(c_all, ada_w, "ada_mod")
    ncol = mod_part.shape[-1]
    mod_g = all_gather8(mod_part.reshape(DEPTH * n_all, ncol), "gather_mod")
    mod_g = mod_g.reshape(N_CHIPS, 2, DEPTH, n_all, ncol)[:, 0]
    mod_full = jnp.transpose(mod_g, (1, 2, 0, 3)).reshape(DEPTH, n_all, N_CHIPS * ncol) + ada_b[:, None, :]
    mod_loc = lax.dynamic_slice_in_dim(mod_full, dev * nb, nb, axis=1)
    mods = [tuple(mod_loc[i, :, k * d:(k + 1) * d].reshape(nb, 1, d) for k in range(6)) for i in range(DEPTH)]

    loss_part, grad_x, (parts, recv), dmods, d_ln_g, d_ln_b, dgq, dgkv, dbf = _local_step(
        x, positions, loss_target, mods, wts, ln_g_full, ln_b_full, mla_g_q[0], mla_g_kv[0], fox_b_f[0], shards)
    loss = lax.psum(loss_part, ("x", "y", "c"))

    dmod_rows = jnp.stack([jnp.concatenate([v_.reshape(nb, d) for v_ in dm], axis=1) for dm in dmods])
    small = jnp.concatenate([
        d_ln_g.reshape(2 * DEPTH, d), d_ln_b.reshape(2 * DEPTH, d),
        jnp.pad(jnp.concatenate([dgq, dgkv, dbf], axis=1), ((0, 0), (0, d - 2 * MLA_QR - FOX_HEADS))),
        dmod_rows.reshape(DEPTH * nb * 6, d)], axis=0)
    n_small = small.shape[0]
    small_rows = -(-n_small // 8) * 8
    small_all = all_gather8(_pad_rows(small, small_rows), "gather_stats").reshape(n_dev, small_rows, d)
    stat_sum = sum_leading(small_all, "sum_stats")
    g_ln_g = lax.dynamic_slice_in_dim(stat_sum[:2 * DEPTH], chip * ln_cols, ln_cols, axis=1).reshape(DEPTH, 2, ln_cols)
    g_ln_b = lax.dynamic_slice_in_dim(stat_sum[2 * DEPTH:4 * DEPTH], chip * ln_cols, ln_cols, axis=1).reshape(DEPTH, 2, ln_cols)
    row = stat_sum[4 * DEPTH]
    g_gq = row[:MLA_QR].reshape(1, MLA_QR)
    g_gkv = row[MLA_QR:2 * MLA_QR].reshape(1, MLA_KVR)
    g_bf = row[2 * MLA_QR:2 * MLA_QR + FOX_HEADS].reshape(1, FOX_HEADS)
    base = 4 * DEPTH + 1
    dmod_all = small_all[:, base:base + DEPTH * nb * 6].reshape(n_dev, DEPTH, nb, 6 * d)
    dmod_all = jnp.transpose(dmod_all, (1, 0, 2, 3)).reshape(DEPTH, n_all, 6 * d)
    g_ada_b = sum_leading(jnp.transpose(dmod_all, (1, 0, 2)), "sum_ada_b")
    dmod_mine = lax.dynamic_slice_in_dim(dmod_all, chip * ncol, ncol, axis=2)
    g_ada_w = ada_grad(c_all.T, dmod_mine, "ada_grad")

    place = jnp.stack([dev, ac, chip]).astype(jnp.int32)
    bufs = []
    for n in _PACKED:
        if _SHARD_KIND[n] == "chunk":
            buf = None
            for layer in range(DEPTH):
                key = "%s/%d" % (n, layer)
                buf = sum_devices(parts[key], recv[key], place, "rs_sum_%s%d" % (n, layer), slot=(layer, DEPTH, buf))
        else:
            buf = sum_devices(parts[n], recv[n], place, "rs_sum_" + n)
        bufs.append(buf)
    joined = sibling_join_halves(bufs, "rs_join")
    g_big = {n: j.reshape(j.shape[0], 2 * j.shape[2], j.shape[3]) for n, j in zip(_PACKED, joined)}
    j = joined[_PACKED.index("fox_w_in")]
    g_big["fox_w_in"] = jnp.transpose(j, (0, 2, 1, 3)).reshape(1, j.shape[2], 2 * j.shape[3])

    g_out = {
        "mla_w_in": g_big["mla_w_in"], "mla_g_q": g_gq, "mla_w_uq": g_big["mla_w_uq"], "mla_g_kv": g_gkv,
        "mla_w_uk": g_big["mla_w_uk"], "mla_w_uv": g_big["mla_w_uv"], "mla_w_o": g_big["mla_w_o"],
        "fox_w_in": g_big["fox_w_in"], "fox_b_f": g_bf, "fox_w_o": g_big["fox_w_o"],
        "ada_w": g_ada_w, "ada_b": g_ada_b, "ffn_w_gate": g_big["ffn_w_gate"], "ffn_w_up": g_big["ffn_w_up"],
        "ffn_w_down": g_big["ffn_w_down"], "ln_g": g_ln_g, "ln_b": g_ln_b}
    names = ["mla_w_in", "mla_g_q", "mla_w_uq", "mla_g_kv", "mla_w_uk", "mla_w_uv", "mla_w_o", "fox_w_in", "fox_b_f",
             "fox_w_o", "ada_w", "ada_b", "ffn_w_gate", "ffn_w_up", "ffn_w_down", "ln_g", "ln_b"]
    small_names = ["mla_g_q", "mla_g_kv", "fox_b_f", "ada_b", "ln_g", "ln_b"]
    deltas, new_m, new_v = {}, {}, {}
    for n in names:
        if n in small_names:
            continue
        shp = args[n].shape
        if n in _TRANSPOSED:
            view = lambda a: jnp.swapaxes(a, 1, 2).reshape(-1, shp[1])
            back = lambda a: jnp.swapaxes(a.reshape(shp[0], shp[2], shp[1]), 1, 2)
        else:
            view = lambda a: a.reshape(-1, shp[-1])
            back = lambda a: a.reshape(shp)
        dl, mn, vn = adamw(view(args[n]), g_out[n].reshape(view(args[n]).shape), view(args["m_" + n]),
                           view(args["v_" + n]), "adamw_" + n)
        g_out[n], deltas[n], new_m[n], new_v[n] = back(g_out[n].reshape(view(args[n]).shape)), back(dl), back(mn), back(vn)

    def small_pack(prefix, src):
        flat = jnp.concatenate([src[prefix + n].reshape(-1) for n in small_names])
        size = -(-flat.shape[0] // (8 * 128)) * 8 * 128
        return jnp.pad(flat, (0, size - flat.shape[0])).reshape(-1, 128)

    sd, sm, sv = adamw(small_pack("", args), small_pack("", g_out), small_pack("m_", args), small_pack("v_", args),
                       "adamw_small")
    off = 0
    for n in small_names:
        shp = args[n].shape
        size = math.prod(shp)
        deltas[n] = sd.reshape(-1)[off:off + size].reshape(shp)
        new_m[n] = sm.reshape(-1)[off:off + size].reshape(shp)
        new_v[n] = sv.reshape(-1)[off:off + size].reshape(shp)
        off += size

    outs = [loss, grad_x]
    outs += [g_out[n].reshape(args[n].shape) for n in names]
    outs += [deltas[n] for n in names] + [new_m[n] for n in names] + [new_v[n] for n in names]
    return tuple(outs)
```

```python
import functools
import math

import numpy as np
import jax
import jax.numpy as jnp
from jax import lax
from jax.experimental import pallas as pl
from jax.experimental.pallas import tpu as pltpu

F32 = jnp.float32
BF16 = jnp.bfloat16
MESH = pl.DeviceIdType.MESH

D_MODEL = 1024
DEPTH = 2
MLA_HEADS = 8
MLA_NOPE = 128
MLA_ROPE = 64
MLA_V = 128
MLA_QR = 256
MLA_KVR = 256
ROPE_THETA = 10000.0
FOX_HEADS = 16
FOX_HD = 64
D_FF = 2816
N_CHIPS = 4
FF_CHUNK = D_FF // N_CHIPS
ALPHA = (2.0 * DEPTH) ** 0.25
EPS = 1e-5
ADAM_LR = 0.001
ADAM_B1 = 0.9
ADAM_B2 = 0.999
ADAM_EPS = 1e-08
ADAM_WD = 0.01
ADAM_STEP = 10

VMEM_LIMIT = 56 * 1024 * 1024
TOKEN_TILE = 512
WGRAD_TOKENS = 2048
ATTN_TILE = 512
FOX_GROUP = 8
MLA_GROUP = 4
COMM_BLOCK_BYTES = 2 * 1024 * 1024
ADAMW_BLOCK_BYTES = 1024 * 1024


def _cp(n_axes):
    return pltpu.CompilerParams(dimension_semantics=("arbitrary",) * n_axes, vmem_limit_bytes=VMEM_LIMIT)


def _dot(a, b):
    return jnp.dot(a, b, preferred_element_type=F32)


def _dot_nt(a, b):
    return lax.dot_general(a, b, (((1,), (1,)), ((), ())), preferred_element_type=F32)


def _dot_tn(a, b):
    return lax.dot_general(a, b, (((0,), (0,)), ((), ())), preferred_element_type=F32)


def _dot_f32(a, b):
    return jnp.dot(a, b, preferred_element_type=F32, precision=lax.Precision.HIGHEST)


def _sds(shape, dtype):
    return jax.ShapeDtypeStruct(shape, dtype)


def _place():
    return lax.axis_index("x"), lax.axis_index("y"), lax.axis_index("c")


class _Hosted:
    def __init__(self, inputs, out_shape, sems, start, finish, in_place=False):
        self.inputs, self.out_shape, self.sems, self.start, self.finish = inputs, out_shape, sems, start, finish
        self.in_place = in_place


def _call(body, name, grid, in_specs, out_specs, out_shape, args, scratch_shapes=(), hosted=None):
    in_specs, out_specs, out_shape, scratch_shapes = list(in_specs), list(out_specs), list(out_shape), list(scratch_shapes)
    if hosted is None:
        return pl.pallas_call(body, name=name, grid=grid, in_specs=in_specs, out_specs=out_specs, out_shape=out_shape,
                              scratch_shapes=scratch_shapes, compiler_params=_cp(len(grid)))(*args)
    n_in, n_out, n_scr = len(in_specs), len(out_specs), len(scratch_shapes)
    h_in, h_out = len(hosted.inputs), len(hosted.out_shape)

    def carried(*refs):
        o0 = n_in + h_in
        s0 = o0 + n_out + h_out
        c_in, c_out, c_sem = refs[n_in:o0], refs[o0 + n_out:s0], refs[s0 + n_scr:]
        ids = [pl.program_id(a) for a in range(len(grid))]
        first = functools.reduce(jnp.logical_and, [i == 0 for i in ids])
        last = functools.reduce(jnp.logical_and, [i == g - 1 for i, g in zip(ids, grid)])

        @pl.when(first)
        def _():
            hosted.start(c_in, c_out, c_sem)

        body(*refs[:n_in], *refs[o0:o0 + n_out], *refs[s0:s0 + n_scr])

        @pl.when(last)
        def _():
            hosted.finish(c_in, c_out, c_sem)

    hbm = pl.BlockSpec(memory_space=pl.ANY)
    aliases = {n_in + k: n_out + k for k in range(h_in)} if hosted.in_place else {}
    res = pl.pallas_call(
        carried, name=name, grid=grid, in_specs=in_specs + [hbm] * h_in, out_specs=out_specs + [hbm] * h_out,
        out_shape=out_shape + list(hosted.out_shape), scratch_shapes=scratch_shapes + list(hosted.sems),
        input_output_aliases=aliases, compiler_params=_cp(len(grid)))(*args, *hosted.inputs)
    return res[:n_out], res[n_out:]


def mod_linear(x, shift, scale, w, out_dtype, name, tn=None, emit_u=False, w_rows=None):
    t, d = x.shape
    n = w.shape[1] if w_rows is None else w_rows
    tn = n if tn is None else tn
    tm = TOKEN_TILE
    tps = (t // shift.shape[0]) // tm

    def body(x_ref, sh_ref, sc_ref, w_ref, o_ref, *rest):
        u = (x_ref[...] * (1.0 + sc_ref[...]) + sh_ref[...]).astype(BF16)
        o_ref[...] = (_dot(u, w_ref[...]) if w_rows is None else _dot_nt(u, w_ref[...])).astype(out_dtype)
        if emit_u:
            @pl.when(pl.program_id(1) == 0)
            def _():
                rest[0][...] = u

    vec = pl.BlockSpec((None, 1, d), lambda i, j: (i // tps, 0, 0))
    out_shape = [_sds((t, n), out_dtype)]
    out_specs = [pl.BlockSpec((tm, tn), lambda i, j: (i, j))]
    if emit_u:
        out_shape.append(_sds((t, d), BF16))
        out_specs.append(pl.BlockSpec((tm, d), lambda i, j: (i, 0)))
    w_spec = pl.BlockSpec((d, tn), lambda i, j: (0, j)) if w_rows is None else pl.BlockSpec((tn, d), lambda i, j: (j, 0))
    res = pl.pallas_call(
        body, name=name, grid=(t // tm, n // tn),
        in_specs=[pl.BlockSpec((tm, d), lambda i, j: (i, 0)), vec, vec, w_spec],
        out_specs=out_specs, out_shape=out_shape, compiler_params=_cp(2),
    )(x, shift, scale, w)
    return res if emit_u else res[0]


def _rms(h, g):
    rstd = lax.rsqrt(jnp.mean(h * h, axis=-1, keepdims=True) + EPS)
    return h * rstd, rstd


def mla_mid_fwd(h, g_q, g_kv, w_uq, w_uk, w_uv, cos8, sin8, cos64, sin64s, swap64, rope_to_heads, dup64, name):
    t = h.shape[0]
    tm = TOKEN_TILE
    hq = MLA_HEADS * MLA_NOPE
    hr = MLA_HEADS * MLA_ROPE // 2

    def body(h_ref, gq_ref, gkv_ref, wuq_ref, wuk_ref, wuv_ref, c8_ref, s8_ref, c64_ref, s64_ref, sw_ref, p_ref, d_ref,
             q_ref, kn_ref, v_ref, kr_ref, cq_ref, ckv_ref):
        hh = h_ref[...]
        cq = (_rms(hh[:, :MLA_QR], None)[0] * gq_ref[...]).astype(BF16)
        ckv = (_rms(hh[:, MLA_QR:MLA_QR + MLA_KVR], None)[0] * gkv_ref[...]).astype(BF16)
        cq_ref[...] = cq
        ckv_ref[...] = ckv
        q = _dot(cq, wuq_ref[...])
        x1 = q[:, hq:hq + hr]
        x2 = q[:, hq + hr:]
        cs = c8_ref[...]
        sn = s8_ref[...]
        rot = jnp.concatenate([x1 * cs - x2 * sn, x2 * cs + x1 * sn], axis=1).astype(BF16)
        q_ref[...] = jnp.concatenate([q[:, :hq].astype(BF16), _dot(rot, p_ref[...]).astype(BF16)], axis=1)
        kn_ref[...] = _dot(ckv, wuk_ref[...]).astype(BF16)
        v_ref[...] = _dot(ckv, wuv_ref[...]).astype(BF16)
        kr = hh[:, MLA_QR + MLA_KVR:]
        kr = (kr * c64_ref[...] + _dot_f32(kr, sw_ref[...]) * s64_ref[...]).astype(BF16)
        kr_ref[...] = _dot(kr, d_ref[...]).astype(BF16)

    def rows(n):
        return pl.BlockSpec((tm, n), lambda i: (i, 0))

    def whole(a):
        return pl.BlockSpec(a.shape, lambda i: (0,) * a.ndim)

    nq = w_uq.shape[1]
    return pl.pallas_call(
        body, name=name, grid=(t // tm,),
        in_specs=[rows(h.shape[1]), whole(g_q), whole(g_kv), whole(w_uq), whole(w_uk), whole(w_uv),
                  rows(hr), rows(hr), rows(MLA_ROPE), rows(MLA_ROPE), whole(swap64), whole(rope_to_heads), whole(dup64)],
        out_specs=[rows(nq), rows(hq), rows(hq), rows(2 * MLA_ROPE), rows(MLA_QR), rows(MLA_KVR)],
        out_shape=[_sds((t, nq), BF16), _sds((t, hq), BF16), _sds((t, hq), BF16), _sds((t, 2 * MLA_ROPE), BF16),
                   _sds((t, MLA_QR), BF16), _sds((t, MLA_KVR), BF16)],
        compiler_params=_cp(1),
    )(h, g_q, g_kv, w_uq, w_uk, w_uv, cos8, sin8, cos64, sin64s, swap64, rope_to_heads, dup64)


def _pick_lane(tile, idx):
    lane = lax.broadcasted_iota(jnp.int32, tile.shape, 1)
    return jnp.sum(jnp.where(lane == idx, tile, 0.0), axis=1, keepdims=True)


def _pick_row(tile, idx):
    row = lax.broadcasted_iota(jnp.int32, tile.shape, 0)
    return jnp.sum(jnp.where(row == idx, tile, 0.0), axis=0, keepdims=True)


def _put_lane(tile, idx, col):
    lane = lax.broadcasted_iota(jnp.int32, tile.shape, 1)
    return jnp.where(lane == idx, col, tile)


def _put_row(tile, idx, row):
    r = lax.broadcasted_iota(jnp.int32, tile.shape, 0)
    return tile + jnp.where(r == idx, row, 0.0)


def _causal_softmax_blocks(i, tq, heads):
    def block(j, carry, masked):
        new = []
        for (score_fn, pv_fn, _), (m, l, acc) in zip(heads, carry):
            sc = score_fn(j)
            if masked:
                keep = lax.broadcasted_iota(jnp.int32, sc.shape, 0) >= lax.broadcasted_iota(jnp.int32, sc.shape, 1)
                sc = jnp.where(keep, sc, -1e30)
            m_new = jnp.maximum(m, jnp.max(sc, axis=1, keepdims=True))
            a = jnp.exp(m - m_new)
            p = jnp.exp(sc - m_new)
            new.append((m_new, a * l + jnp.sum(p, axis=1, keepdims=True), a * acc + pv_fn(j, p.astype(BF16))))
        return tuple(new)

    init = tuple((jnp.full((tq, 1), -1e30, F32), jnp.zeros((tq, 1), F32), jnp.zeros((tq, dv), F32)) for _, _, dv in heads)
    carry = lax.fori_loop(0, i, lambda j, c: block(j, c, False), init)
    return [(acc / l, m + jnp.log(l)) for m, l, acc in block(i, carry, True)]


def fox_attn_fwd(qkv, cum, cum_rows, nb, name, hosted=None):
    t = qkv.shape[0]
    s = t // nb
    tq = ATTN_TILE
    nq = s // tq
    wide = FOX_GROUP * FOX_HD
    ngroups = FOX_HEADS // FOX_GROUP
    scale = FOX_HD ** -0.5

    def body(q_ref, k_ref, v_ref, cum_ref, cr_ref, o_ref, lse_ref):
        i = pl.program_id(1)
        hg = pl.program_id(2)

        @pl.when(hg == 0)
        def _():
            lse_ref[...] = jnp.zeros_like(lse_ref)

        low = lax.broadcasted_iota(jnp.int32, (tq, 128), 1) < FOX_HD
        cum_t = cum_ref[...]

        def rows_of(j):
            return pl.ds(pl.multiple_of(j * tq, tq), tq)

        def head(a):
            hd = FOX_GROUP * hg + a
            cols = slice(128 * (a // 2), 128 * (a // 2) + 128)
            q = q_ref[:, cols]
            qa = jnp.where(low if a % 2 == 0 else jnp.logical_not(low), q, jnp.zeros_like(q)) * scale
            fq = _pick_lane(cum_t, hd)
            return (lambda j: _dot_nt(qa, k_ref[rows_of(j), cols]) + fq - _pick_row(cr_ref[j], hd),
                    lambda j, p: _dot(p, v_ref[rows_of(j), cols]), 2 * FOX_HD)

        res = _causal_softmax_blocks(i, tq, [head(a) for a in range(FOX_GROUP)])
        o_ref[...] = jnp.concatenate([jnp.where(low, res[a][0], res[a + 1][0]) for a in range(0, FOX_GROUP, 2)],
                                     axis=1).astype(BF16)
        lse_t = lse_ref[...]
        for a in range(FOX_GROUP):
            lse_t = _put_lane(lse_t, FOX_GROUP * hg + a, res[a][1])
        lse_ref[...] = lse_t

    return _call(
        body, name, (nb, nq, ngroups),
        [pl.BlockSpec((tq, wide), lambda b, i, hg: (b * nq + i, hg)),
         pl.BlockSpec((s, wide), lambda b, i, hg: (b, ngroups + hg)),
         pl.BlockSpec((s, wide), lambda b, i, hg: (b, 2 * ngroups + hg)),
         pl.BlockSpec((tq, 128), lambda b, i, hg: (b * nq + i, 0)),
         pl.BlockSpec((nq, 16, tq), lambda b, i, hg: (b, 0, 0))],
        [pl.BlockSpec((tq, wide), lambda b, i, hg: (b * nq + i, hg)),
         pl.BlockSpec((tq, 128), lambda b, i, hg: (b * nq + i, 0))],
        [_sds((t, D_MODEL), BF16), _sds((t, 128), F32)], (qkv, qkv, qkv, cum, cum_rows), hosted=hosted)


def mla_attn_fwd(q, kn, kr2, v, nb, name, hosted=None):
    t = q.shape[0]
    s = t // nb
    tq = ATTN_TILE
    nq = s // tq
    ngroups = MLA_HEADS // MLA_GROUP
    wide = MLA_GROUP * MLA_NOPE
    rwide = MLA_GROUP * MLA_ROPE
    scale = (MLA_NOPE + MLA_ROPE) ** -0.5

    def body(qn_ref, qr_ref, kn_ref, kr_ref, v_ref, o_ref, lse_ref):
        i = pl.program_id(1)
        hg = pl.program_id(2)

        @pl.when(hg == 0)
        def _():
            lse_ref[...] = jnp.zeros_like(lse_ref)

        low = lax.broadcasted_iota(jnp.int32, (tq, 128), 1) < MLA_ROPE

        def rows_of(j):
            return pl.ds(pl.multiple_of(j * tq, tq), tq)

        def head(a):
            cols = slice(a * MLA_NOPE, (a + 1) * MLA_NOPE)
            qr = qr_ref[:, 128 * (a // 2):128 * (a // 2) + 128]
            q_cat = jnp.concatenate([qn_ref[:, cols], jnp.where(low if a % 2 == 0 else jnp.logical_not(low), qr,
                                                                jnp.zeros_like(qr))], axis=1)
            return (lambda j: _dot_nt(q_cat, jnp.concatenate([kn_ref[rows_of(j), cols], kr_ref[rows_of(j), :]], axis=1)) * scale,
                    lambda j, p: _dot(p, v_ref[rows_of(j), cols]), MLA_V)

        res = _causal_softmax_blocks(i, tq, [head(a) for a in range(MLA_GROUP)])
        o_ref[...] = jnp.concatenate([r[0] for r in res], axis=1).astype(BF16)
        lse_t = lse_ref[...]
        for a in range(MLA_GROUP):
            lse_t = _put_lane(lse_t, MLA_GROUP * hg + a, res[a][1])
        lse_ref[...] = lse_t

    rope0 = MLA_HEADS * MLA_NOPE // rwide
    return _call(
        body, name, (nb, nq, ngroups),
        [pl.BlockSpec((tq, wide), lambda b, i, hg: (b * nq + i, hg)),
         pl.BlockSpec((tq, rwide), lambda b, i, hg: (b * nq + i, rope0 + hg)),
         pl.BlockSpec((s, wide), lambda b, i, hg: (b, hg)),
         pl.BlockSpec((s, 128), lambda b, i, hg: (b, 0)),
         pl.BlockSpec((s, wide), lambda b, i, hg: (b, hg))],
        [pl.BlockSpec((tq, wide), lambda b, i, hg: (b * nq + i, hg)),
         pl.BlockSpec((tq, 128), lambda b, i, hg: (b * nq + i, 0))],
        [_sds((t, MLA_HEADS * MLA_V), BF16), _sds((t, 128), F32)], (q, q, kn, kr2, v), hosted=hosted)


def rows16(a, name):
    t = a.shape[0]
    tq = ATTN_TILE

    def body(a_ref, o_ref):
        o_ref[...] = a_ref[...].T[:16, :]

    return pl.pallas_call(
        body, name=name, grid=(t // tq,), in_specs=[pl.BlockSpec((tq, 128), lambda n: (n, 0))],
        out_specs=pl.BlockSpec((None, 16, tq), lambda n: (n, 0, 0)), out_shape=_sds((t // tq, 16, tq), F32),
        compiler_params=_cp(1),
    )(a)


def tokens128(rows, onehot, name):
    nblk, _, tq = rows.shape

    def body(r_ref, e_ref, o_ref):
        o_ref[...] = lax.dot_general(r_ref[...], e_ref[...], (((0,), (0,)), ((), ())), preferred_element_type=F32,
                                     precision=lax.Precision.HIGHEST)

    return pl.pallas_call(
        body, name=name, grid=(nblk,),
        in_specs=[pl.BlockSpec((None, 16, tq), lambda n: (n, 0, 0)), pl.BlockSpec((16, 128), lambda n: (0, 0))],
        out_specs=pl.BlockSpec((tq, 128), lambda n: (n, 0)), out_shape=_sds((nblk * tq, 128), F32),
        compiler_params=_cp(1),
    )(rows, onehot)


def _layer_norm(z, g, b):
    mu = jnp.mean(z, axis=-1, keepdims=True)
    zc = z - mu
    rstd = lax.rsqrt(jnp.mean(zc * zc, axis=-1, keepdims=True) + EPS)
    xhat = zc * rstd
    return xhat * g + b, xhat, rstd


def linear_resid_ln(a, w, x_in, gate, ln_g, ln_b, name):
    t, kdim = a.shape
    d = w.shape[1]
    tm = TOKEN_TILE
    tps = (t // gate.shape[0]) // tm

    def body(a_ref, w_ref, x_ref, gt_ref, g_ref, b_ref, y_ref, xo_ref):
        y = _dot(a_ref[...], w_ref[...])
        y_ref[...] = y
        z = ALPHA * x_ref[...] + (1.0 + gt_ref[...]) * y
        xo_ref[...] = _layer_norm(z, g_ref[...], b_ref[...])[0]

    rows = pl.BlockSpec((tm, d), lambda i: (i, 0))
    vec = pl.BlockSpec((1, d), lambda i: (0, 0))
    return pl.pallas_call(
        body, name=name, grid=(t // tm,),
        in_specs=[pl.BlockSpec((tm, kdim), lambda i: (i, 0)), pl.BlockSpec((kdim, d), lambda i: (0, 0)), rows,
                  pl.BlockSpec((None, 1, d), lambda i: (i // tps, 0, 0)), vec, vec],
        out_specs=[rows, rows], out_shape=[_sds((t, d), F32), _sds((t, d), F32)],
        compiler_params=_cp(1),
    )(a, w, x_in, gate, ln_g, ln_b)


def _resident(a):
    return pl.BlockSpec(a.shape, lambda *_: (0,) * a.ndim, pipeline_mode=pl.Buffered(1))


def ffn_fwd(x_in, shift, scale, gate, wg, wu, wd, ln_g, ln_b, name, hosted=None):
    t, d = x_in.shape
    c, _, fc = wg.shape
    tm = TOKEN_TILE
    tps = (t // gate.shape[0]) // tm

    def body(x_ref, sh_ref, sc_ref, gt_ref, wg_ref, wu_ref, wd_ref, g_ref, b_ref,
             u_ref, hg_ref, hu_ref, y_ref, xo_ref, acc_ref):
        cc = pl.program_id(1)

        @pl.when(cc == 0)
        def _():
            u_ref[...] = (x_ref[...] * (1.0 + sc_ref[...]) + sh_ref[...]).astype(BF16)
            acc_ref[...] = jnp.zeros_like(acc_ref)

        u = u_ref[...]
        hg = _dot(u, wg_ref[cc])
        hu = _dot(u, wu_ref[cc])
        hg_ref[...] = hg.astype(BF16)
        hu_ref[...] = hu.astype(BF16)
        act = (hg * jax.nn.sigmoid(hg) * hu).astype(BF16)
        acc_ref[...] += _dot(act, wd_ref[cc])

        @pl.when(cc == c - 1)
        def _():
            y = acc_ref[...]
            y_ref[...] = y
            z = ALPHA * x_ref[...] + (1.0 + gt_ref[...]) * y
            xo_ref[...] = _layer_norm(z, g_ref[...], b_ref[...])[0]

    rows = pl.BlockSpec((tm, d), lambda i, cc: (i, 0))
    bvec = pl.BlockSpec((None, 1, d), lambda i, cc: (i // tps, 0, 0))
    vec = pl.BlockSpec((1, d), lambda i, cc: (0, 0))
    hspec = pl.BlockSpec((None, tm, fc), lambda i, cc: (cc, i, 0))
    wcol = _resident(wg)
    return _call(
        body, name, (t // tm, c),
        [rows, bvec, bvec, bvec, wcol, wcol, _resident(wd), vec, vec],
        [rows, hspec, hspec, rows, rows],
        [_sds((t, d), BF16), _sds((c, t, fc), BF16), _sds((c, t, fc), BF16), _sds((t, d), F32), _sds((t, d), F32)],
        (x_in, shift, scale, gate, wg, wu, wd, ln_g, ln_b), scratch_shapes=[pltpu.VMEM((tm, d), F32)], hosted=hosted)


def fox_gate_fwd(hf, b_f, tri, n_batch, name):
    t, n = hf.shape
    blk = tri.shape[0]
    nb = (t // n_batch) // blk

    def body(hf_ref, b_ref, tri_ref, o_ref, carry_ref):
        @pl.when(pl.program_id(1) == 0)
        def _():
            carry_ref[...] = jnp.zeros_like(carry_ref)

        xx = hf_ref[...] + b_ref[...]
        lf = jnp.minimum(xx, 0.0) - jnp.log(1.0 + jnp.exp(-jnp.abs(xx)))
        cum = _dot_f32(tri_ref[...], lf) + carry_ref[...]
        o_ref[...] = cum
        carry_ref[...] = cum[blk - 1:blk, :]

    return pl.pallas_call(
        body, name=name, grid=(n_batch, nb),
        in_specs=[pl.BlockSpec((blk, n), lambda bb, i: (bb * nb + i, 0)), pl.BlockSpec((1, n), lambda bb, i: (0, 0)),
                  pl.BlockSpec((blk, blk), lambda bb, i: (0, 0))],
        out_specs=pl.BlockSpec((blk, n), lambda bb, i: (bb * nb + i, 0)),
        out_shape=_sds((t, n), F32), scratch_shapes=[pltpu.VMEM((1, n), F32)],
        compiler_params=_cp(2),
    )(hf, b_f, tri)


def ln_bwd(dxo, x_in, y, gate, ln_g, name, target=None):
    t, d = dxo.shape
    nb = gate.shape[0]
    tm = TOKEN_TILE
    tps = (t // nb) // tm
    with_loss = target is not None

    def body(dxo_ref, *refs):
        if with_loss:
            t_ref, x_ref, y_ref, gt_ref, g_ref, dz_ref, dy_ref, dg_ref, db_ref, dgt_ref, l_ref = refs
        else:
            x_ref, y_ref, gt_ref, g_ref, dz_ref, dy_ref, dg_ref, db_ref, dgt_ref = refs
        i = pl.program_id(0)

        @pl.when(i == 0)
        def _():
            dg_ref[...] = jnp.zeros_like(dg_ref)
            db_ref[...] = jnp.zeros_like(db_ref)
            if with_loss:
                l_ref[...] = jnp.zeros_like(l_ref)

        @pl.when(i % tps == 0)
        def _():
            dgt_ref[...] = jnp.zeros_like(dgt_ref)

        yy = y_ref[...]
        g1 = 1.0 + gt_ref[...]
        z = ALPHA * x_ref[...] + g1 * yy
        _, xhat, rstd = _layer_norm(z, 1.0, 0.0)
        dxo_v = dxo_ref[...]
        if with_loss:
            err = dxo_v - t_ref[...]
            l_ref[...] += jnp.sum(err * err, axis=0, keepdims=True)
            dxo_v = err / d
        dg_ref[...] += jnp.sum(dxo_v * xhat, axis=0, keepdims=True)
        db_ref[...] += jnp.sum(dxo_v, axis=0, keepdims=True)
        dxh = dxo_v * g_ref[...]
        dz = rstd * (dxh - jnp.mean(dxh, axis=-1, keepdims=True) - xhat * jnp.mean(dxh * xhat, axis=-1, keepdims=True))
        dz_ref[...] = dz
        dy_ref[...] = (g1 * dz).astype(BF16)
        dgt_ref[...] += jnp.sum(dz * yy, axis=0, keepdims=True)

    rows = pl.BlockSpec((tm, d), lambda i: (i, 0))
    vec = pl.BlockSpec((1, d), lambda i: (0, 0))
    bvec = pl.BlockSpec((None, 1, d), lambda i: (i // tps, 0, 0))
    return pl.pallas_call(
        body, name=name, grid=(t // tm,), in_specs=[rows] * (4 if with_loss else 3) + [bvec, vec],
        out_specs=[rows, rows, vec, vec, bvec] + ([vec] if with_loss else []),
        out_shape=[_sds((t, d), F32), _sds((t, d), BF16), _sds((1, d), F32), _sds((1, d), F32), _sds((nb, 1, d), F32)]
        + ([_sds((1, d), F32)] if with_loss else []),
        compiler_params=_cp(1),
    )(dxo, *([target] if with_loss else []), x_in, y, gate, ln_g)


def _mod_bwd_tail(du, dz_ref, x_ref, sc_ref, dx_ref, dsc_ref, dsh_ref, first):
    @pl.when(first)
    def _():
        dsc_ref[...] = jnp.zeros_like(dsc_ref)
        dsh_ref[...] = jnp.zeros_like(dsh_ref)

    dx_ref[...] = ALPHA * dz_ref[...] + du * (1.0 + sc_ref[...])
    dsc_ref[...] += jnp.sum(du * x_ref[...], axis=0, keepdims=True)
    dsh_ref[...] += jnp.sum(du, axis=0, keepdims=True)


def ffn_bwd(dy, hg, hu, wg, wu, wd, dz, x_in, scale, name, hosted=None):
    t, d = dy.shape
    c, _, fc = wg.shape
    nb = scale.shape[0]
    tm = TOKEN_TILE
    tps = (t // nb) // tm

    def body(dy_ref, hg_ref, hu_ref, wg_ref, wu_ref, wd_ref, dz_ref, x_ref, sc_ref,
             dhg_ref, dhu_ref, act_ref, dx_ref, dsc_ref, dsh_ref, acc_ref):
        i = pl.program_id(0)
        cc = pl.program_id(1)

        @pl.when(cc == 0)
        def _():
            acc_ref[...] = jnp.zeros_like(acc_ref)

        hgv = hg_ref[...].astype(F32)
        huv = hu_ref[...].astype(F32)
        da = _dot_nt(dy_ref[...], wd_ref[cc])
        sg = jax.nn.sigmoid(hgv)
        sl = hgv * sg
        act_ref[...] = (sl * huv).astype(BF16)
        dhu = (da * sl).astype(BF16)
        dhg = (da * huv * (sg * (1.0 + hgv * (1.0 - sg)))).astype(BF16)
        dhu_ref[...] = dhu
        dhg_ref[...] = dhg
        acc_ref[...] += _dot_nt(dhg, wg_ref[cc]) + _dot_nt(dhu, wu_ref[cc])

        @pl.when(cc == c - 1)
        def _():
            _mod_bwd_tail(acc_ref[...], dz_ref, x_ref, sc_ref, dx_ref, dsc_ref, dsh_ref, i % tps == 0)

    rows = pl.BlockSpec((tm, d), lambda i, cc: (i, 0))
    bvec = pl.BlockSpec((None, 1, d), lambda i, cc: (i // tps, 0, 0))
    hspec = pl.BlockSpec((None, tm, fc), lambda i, cc: (cc, i, 0))
    wcol = _resident(wg)
    return _call(
        body, name, (t // tm, c),
        [rows, hspec, hspec, wcol, wcol, _resident(wd), rows, rows, bvec],
        [hspec, hspec, hspec, rows, bvec, bvec],
        [_sds((c, t, fc), BF16), _sds((c, t, fc), BF16), _sds((c, t, fc), BF16), _sds((t, d), F32),
         _sds((nb, 1, d), F32), _sds((nb, 1, d), F32)],
        (dy, hg, hu, wg, wu, wd, dz, x_in, scale), scratch_shapes=[pltpu.VMEM((tm, d), F32)], hosted=hosted)


def linear_nt_mod_bwd(pairs, dz, x_in, scale, name, hosted=None):
    t, d = dz.shape
    nb = scale.shape[0]
    tm = TOKEN_TILE
    tps = (t // nb) // tm
    npairs = len(pairs)

    def body(*refs):
        dh_refs = refs[:npairs]
        w_refs = refs[npairs:2 * npairs]
        dz_ref, x_ref, sc_ref, dx_ref, dsc_ref, dsh_ref = refs[2 * npairs:]
        du = None
        for (_, _, blk), dh_ref, w_ref in zip(pairs, dh_refs, w_refs):
            dh = dh_ref[...].astype(BF16)
            term = _dot_nt(dh, w_ref[...]) if blk is None else _dot(dh, w_ref[...])
            du = term if du is None else du + term
        _mod_bwd_tail(du, dz_ref, x_ref, sc_ref, dx_ref, dsc_ref, dsh_ref, pl.program_id(0) % tps == 0)

    rows = pl.BlockSpec((tm, d), lambda i: (i, 0))
    bvec = pl.BlockSpec((None, 1, d), lambda i: (i // tps, 0, 0))
    in_specs = [pl.BlockSpec((tm, dh.shape[1]), lambda i: (i, 0)) for dh, _, _ in pairs]
    for dh, w, blk in pairs:
        if blk is None:
            in_specs.append(pl.BlockSpec(w.shape, lambda i: (0, 0)))
        else:
            in_specs.append(pl.BlockSpec((dh.shape[1], d), lambda i, blk=blk: (blk, 0)))
    in_specs += [rows, rows, bvec]
    return _call(
        body, name, (t // tm,), in_specs, [rows, bvec, bvec],
        [_sds((t, d), F32), _sds((nb, 1, d), F32), _sds((nb, 1, d), F32)],
        (*[dh for dh, _, _ in pairs], *[w for _, w, _ in pairs], dz, x_in, scale), hosted=hosted)


def linear_nt_delta(dy, w_o, o, head_sel, name):
    t, d = dy.shape
    hdv = w_o.shape[0]
    tm = TOKEN_TILE

    def body(dy_ref, w_ref, o_ref, sel_ref, do_ref, dl_ref):
        do = _dot_nt(dy_ref[...], w_ref[...])
        do_ref[...] = do.astype(BF16)
        dl_ref[...] = _dot_f32(do * o_ref[...].astype(F32), sel_ref[...])

    return pl.pallas_call(
        body, name=name, grid=(t // tm,),
        in_specs=[pl.BlockSpec((tm, d), lambda i: (i, 0)), pl.BlockSpec((hdv, d), lambda i: (0, 0)),
                  pl.BlockSpec((tm, hdv), lambda i: (i, 0)), pl.BlockSpec(head_sel.shape, lambda i: (0, 0))],
        out_specs=[pl.BlockSpec((tm, hdv), lambda i: (i, 0)), pl.BlockSpec((tm, 128), lambda i: (i, 0))],
        out_shape=[_sds((t, hdv), BF16), _sds((t, 128), F32)], compiler_params=_cp(1),
    )(dy, w_o, o, head_sel)


def _attn_bwd_blocks(j, nk, tk, scale, heads):
    def block(i, carry, masked):
        new = []
        for hd, (dk_acc, dv_acc, dfk_acc) in zip(heads, carry):
            qb = hd["q"](i)
            dob = hd["do"](i)
            lse_row, dl_row = hd["rows"](i)
            st = _dot_nt(hd["k"], qb)
            if scale is not None:
                st = st * scale
            if hd["bias"] is not None:
                fq_row, fk_col = hd["bias"](i)
                st = st + fq_row - fk_col
            if masked:
                keep = lax.broadcasted_iota(jnp.int32, st.shape, 1) >= lax.broadcasted_iota(jnp.int32, st.shape, 0)
                st = jnp.where(keep, st, -1e30)
            pt = jnp.exp(st - lse_row)
            dv_acc = dv_acc + _dot(pt.astype(BF16), dob)
            dst = pt * (_dot_nt(hd["v"], dob) - dl_row)
            if hd["add_dfq"] is not None:
                dfk_acc = dfk_acc - jnp.sum(dst, axis=1, keepdims=True)
                hd["add_dfq"](i, jnp.sum(dst, axis=0, keepdims=True))
            dsb = (dst if scale is None else dst * scale).astype(BF16)
            dk_acc = dk_acc + _dot(dsb, qb)
            hd["add_dq"](i, _dot_tn(dsb, hd["k"] if scale is not None else hd["k_scaled"]))
            new.append((dk_acc, dv_acc, dfk_acc))
        return tuple(new)

    init = tuple((jnp.zeros((tk, hd["k"].shape[1]), F32), jnp.zeros((tk, hd["v"].shape[1]), F32), jnp.zeros((tk, 1), F32))
                 for hd in heads)
    carry = block(j, init, True)
    return lax.fori_loop(j + 1, nk, lambda i, c: block(i, c, False), carry)


def fox_attn_bwd(qkv, do, cum, cum_rows, lse_rows, delta_rows, nb, name, hosted=None):
    t = qkv.shape[0]
    s = t // nb
    tk = ATTN_TILE
    nk = s // tk
    scale = FOX_HD ** -0.5

    def body(q_ref, k_ref, v_ref, do_ref, cum_ref, cr_ref, lr_ref, dr_ref, dq_ref, dk_ref, dv_ref, dfq_ref, dfk_ref):
        hg = pl.program_id(1)
        j = pl.program_id(2)

        @pl.when(j == 0)
        def _():
            dq_ref[...] = jnp.zeros_like(dq_ref)

        @pl.when((j == 0) & (hg == 0))
        def _():
            dfq_ref[...] = jnp.zeros_like(dfq_ref)
            dfk_ref[...] = jnp.zeros_like(dfk_ref)

        low = lax.broadcasted_iota(jnp.int32, (tk, 128), 1) < FOX_HD
        cum_t = cum_ref[...]

        def rows_of(i):
            return pl.ds(pl.multiple_of(i * tk, tk), tk)

        def head(a):
            hd = FOX_GROUP * hg + a
            cols = slice(128 * (a // 2), 128 * (a // 2) + 128)
            half = low if a % 2 == 0 else jnp.logical_not(low)
            kb = k_ref[:, cols]
            vb = v_ref[:, cols]
            fk = _pick_lane(cum_t, hd)

            def add_dq(i, val):
                dq_ref[rows_of(i), cols] += val

            def add_dfq(i, val):
                dfq_ref[i] = _put_row(dfq_ref[i], hd, val)

            ka = jnp.where(half, kb, jnp.zeros_like(kb))
            return dict(q=lambda i: q_ref[rows_of(i), cols] * scale, do=lambda i: do_ref[rows_of(i), cols],
                        k=ka, k_scaled=ka * scale, v=jnp.where(half, vb, jnp.zeros_like(vb)),
                        rows=lambda i: (_pick_row(lr_ref[i], hd), _pick_row(dr_ref[i], hd)),
                        bias=lambda i: (_pick_row(cr_ref[i], hd), fk), add_dq=add_dq, add_dfq=add_dfq)

        res = _attn_bwd_blocks(j, nk, tk, None, [head(a) for a in range(FOX_GROUP)])
        dk_ref[...] = jnp.concatenate([jnp.where(low, res[a][0], res[a + 1][0]) for a in range(0, FOX_GROUP, 2)],
                                      axis=1).astype(BF16)
        dv_ref[...] = jnp.concatenate([jnp.where(low, res[a][1], res[a + 1][1]) for a in range(0, FOX_GROUP, 2)],
                                      axis=1).astype(BF16)
        for a in range(FOX_GROUP):
            dfk_ref[j] = _put_row(dfk_ref[j], FOX_GROUP * hg + a, jnp.broadcast_to(res[a][2], (tk, 128)).T[0:1, :])

    wide = FOX_GROUP * FOX_HD
    ngroups = FOX_HEADS // FOX_GROUP
    rowsp = pl.BlockSpec((nk, 16, tk), lambda b, hg, j: (b, 0, 0))
    return _call(
        body, name, (nb, ngroups, nk),
        [pl.BlockSpec((s, wide), lambda b, hg, j: (b, hg)),
         pl.BlockSpec((tk, wide), lambda b, hg, j: (b * nk + j, ngroups + hg)),
         pl.BlockSpec((tk, wide), lambda b, hg, j: (b * nk + j, 2 * ngroups + hg)),
         pl.BlockSpec((s, wide), lambda b, hg, j: (b, hg)),
         pl.BlockSpec((tk, 128), lambda b, hg, j: (b * nk + j, 0)),
         rowsp, rowsp, rowsp],
        [pl.BlockSpec((s, wide), lambda b, hg, j: (b, hg)),
         pl.BlockSpec((tk, wide), lambda b, hg, j: (b * nk + j, hg)),
         pl.BlockSpec((tk, wide), lambda b, hg, j: (b * nk + j, hg)),
         rowsp, rowsp],
        [_sds((t, D_MODEL), F32), _sds((t, D_MODEL), BF16), _sds((t, D_MODEL), BF16),
         _sds((t // tk, 16, tk), F32), _sds((t // tk, 16, tk), F32)],
        (qkv, qkv, qkv, do, cum, cum_rows, lse_rows, delta_rows), hosted=hosted)


def mla_attn_bwd(q, kn, kr2, v, do, lse_rows, delta_rows, nb, name, hosted=None):
    t = q.shape[0]
    s = t // nb
    tk = ATTN_TILE
    nk = s // tk
    ngroups = MLA_HEADS // MLA_GROUP
    wide = MLA_GROUP * MLA_NOPE
    rwide = MLA_GROUP * MLA_ROPE
    scale = (MLA_NOPE + MLA_ROPE) ** -0.5

    def body(qn_ref, qr_ref, kn_ref, kr_ref, v_ref, do_ref, lr_ref, dr_ref, dqn_ref, dqr_ref, dkn_ref, dkr_ref, dv_ref):
        hg = pl.program_id(1)
        j = pl.program_id(2)

        @pl.when(j == 0)
        def _():
            dqn_ref[...] = jnp.zeros_like(dqn_ref)
            dqr_ref[...] = jnp.zeros_like(dqr_ref)

        low = lax.broadcasted_iota(jnp.int32, (tk, 128), 1) < MLA_ROPE
        kr = kr_ref[...]

        def rows_of(i):
            return pl.ds(pl.multiple_of(i * tk, tk), tk)

        def head(a):
            cols = slice(a * MLA_NOPE, (a + 1) * MLA_NOPE)
            rcols = slice(128 * (a // 2), 128 * (a // 2) + 128)
            mine = low if a % 2 == 0 else jnp.logical_not(low)
            hd = MLA_GROUP * hg + a

            def q_fn(i):
                qr = qr_ref[rows_of(i), rcols]
                return jnp.concatenate([qn_ref[rows_of(i), cols], jnp.where(mine, qr, jnp.zeros_like(qr))], axis=1)

            def add_dq(i, val):
                dqn_ref[rows_of(i), cols] += val[:, :MLA_NOPE]
                dqr_ref[rows_of(i), cols] += val[:, MLA_NOPE:]

            return dict(q=q_fn, do=lambda i: do_ref[rows_of(i), cols], k=jnp.concatenate([kn_ref[:, cols], kr], axis=1),
                        v=v_ref[:, cols], rows=lambda i: (_pick_row(lr_ref[i], hd), _pick_row(dr_ref[i], hd)),
                        bias=None, add_dq=add_dq, add_dfq=None)

        res = _attn_bwd_blocks(j, nk, tk, scale, [head(a) for a in range(MLA_GROUP)])
        dkn_ref[...] = jnp.concatenate([r[0][:, :MLA_NOPE] for r in res], axis=1).astype(BF16)
        dkr_ref[...] = jnp.concatenate([r[0][:, MLA_NOPE:] for r in res], axis=1).astype(BF16)
        dv_ref[...] = jnp.concatenate([r[1] for r in res], axis=1).astype(BF16)

    full = pl.BlockSpec((s, wide), lambda b, hg, j: (b, hg))
    blk = pl.BlockSpec((tk, wide), lambda b, hg, j: (b * nk + j, hg))
    rowsp = pl.BlockSpec((nk, 16, tk), lambda b, hg, j: (b, 0, 0))
    total = MLA_HEADS * MLA_V
    rope0 = MLA_HEADS * MLA_NOPE // rwide
    return _call(
        body, name, (nb, ngroups, nk),
        [full, pl.BlockSpec((s, rwide), lambda b, hg, j: (b, rope0 + hg)), blk,
         pl.BlockSpec((tk, 128), lambda b, hg, j: (b * nk + j, 0)), blk, full, rowsp, rowsp],
        [full, full, blk, blk, blk],
        [_sds((t, total), F32), _sds((t, total), F32), _sds((t, total), BF16), _sds((t, total), BF16),
         _sds((t, total), BF16)],
        (q, q, kn, kr2, v, do, lse_rows, delta_rows), hosted=hosted)


def mla_mid_bwd(dqn, dqr, dkn, dv, dkr_heads, h, g_q, g_kv, w_uq, w_uk, w_uv, cos8, sin8, cos64, sin64s, swap64,
                heads_to_rope, head_sum, name):
    t = h.shape[0]
    tm = TOKEN_TILE
    hq = MLA_HEADS * MLA_NOPE
    hr = MLA_HEADS * MLA_ROPE // 2
    nq = w_uq.shape[1]

    def body(dqn_ref, dqr_ref, dkn_ref, dv_ref, dkr_ref, h_ref, gq_ref, gkv_ref, wuq_ref, wuk_ref, wuv_ref,
             c8_ref, s8_ref, c64_ref, s64_ref, sw_ref, hp_ref, hs_ref, dh_ref, dqp_ref, dgq_ref, dgkv_ref):
        @pl.when(pl.program_id(0) == 0)
        def _():
            dgq_ref[...] = jnp.zeros_like(dgq_ref)
            dgkv_ref[...] = jnp.zeros_like(dgkv_ref)

        drot = _dot(dqr_ref[...].astype(BF16), hp_ref[...])
        o1 = drot[:, :hr]
        o2 = drot[:, hr:]
        cs = c8_ref[...]
        sn = s8_ref[...]
        dqp = jnp.concatenate([dqn_ref[...].astype(BF16), (o1 * cs + o2 * sn).astype(BF16),
                               (o2 * cs - o1 * sn).astype(BF16)], axis=1)
        dqp_ref[...] = dqp
        dcq = _dot_nt(dqp, wuq_ref[...])
        dckv = _dot_nt(dkn_ref[...], wuk_ref[...]) + _dot_nt(dv_ref[...], wuv_ref[...])
        hh = h_ref[...]

        def rms_bwd(hpart, g, dc, dg_ref):
            hhat, rstd = _rms(hpart, None)
            dg_ref[...] += jnp.sum(dc * hhat, axis=0, keepdims=True)
            dcg = dc * g
            return rstd * (dcg - hhat * jnp.mean(dcg * hhat, axis=-1, keepdims=True))

        dhq = rms_bwd(hh[:, :MLA_QR], gq_ref[...], dcq, dgq_ref)
        dhkv = rms_bwd(hh[:, MLA_QR:MLA_QR + MLA_KVR], gkv_ref[...], dckv, dgkv_ref)
        dkr = _dot(dkr_ref[...], hs_ref[...])
        dkr_pre = dkr * c64_ref[...] + _dot_f32(dkr * s64_ref[...], sw_ref[...])
        dh_ref[...] = jnp.concatenate([dhq, dhkv, dkr_pre], axis=1).astype(BF16)

    def rows(n):
        return pl.BlockSpec((tm, n), lambda i: (i, 0))

    def whole(a):
        return pl.BlockSpec(a.shape, lambda i: (0,) * a.ndim)

    return pl.pallas_call(
        body, name=name, grid=(t // tm,),
        in_specs=[rows(hq), rows(hq), rows(hq), rows(hq), rows(hq), rows(h.shape[1]), whole(g_q), whole(g_kv),
                  whole(w_uq), whole(w_uk), whole(w_uv), rows(hr), rows(hr), rows(MLA_ROPE), rows(MLA_ROPE),
                  whole(swap64), whole(heads_to_rope), whole(head_sum)],
        out_specs=[rows(h.shape[1]), rows(nq), pl.BlockSpec((1, MLA_QR), lambda i: (0, 0)),
                   pl.BlockSpec((1, MLA_KVR), lambda i: (0, 0))],
        out_shape=[_sds((t, h.shape[1]), BF16), _sds((t, nq), BF16), _sds((1, MLA_QR), F32), _sds((1, MLA_KVR), F32)],
        compiler_params=_cp(1),
    )(dqn, dqr, dkn, dv, dkr_heads, h, g_q, g_kv, w_uq, w_uk, w_uv, cos8, sin8, cos64, sin64s, swap64,
      heads_to_rope, head_sum)


def fox_gate_bwd(dcum, hf, b_f, triu, n_batch, name):
    t, n = hf.shape
    blk = triu.shape[0]
    nb = (t // n_batch) // blk

    def body(dc_ref, hf_ref, b_ref, tri_ref, o_ref, db_ref, carry_ref):
        @pl.when(pl.program_id(1) == 0)
        def _():
            carry_ref[...] = jnp.zeros_like(carry_ref)

        @pl.when((pl.program_id(0) == 0) & (pl.program_id(1) == 0))
        def _():
            db_ref[...] = jnp.zeros_like(db_ref)

        rc = _dot_f32(tri_ref[...], dc_ref[...]) + carry_ref[...]
        carry_ref[...] = rc[0:1, :]
        dhf = rc * jax.nn.sigmoid(-(hf_ref[...] + b_ref[...]))
        o_ref[...] = dhf.astype(BF16)
        db_ref[...] += jnp.sum(dhf, axis=0, keepdims=True)

    rev = pl.BlockSpec((blk, n), lambda bb, i: (bb * nb + nb - 1 - i, 0))
    return pl.pallas_call(
        body, name=name, grid=(n_batch, nb),
        in_specs=[rev, rev, pl.BlockSpec((1, n), lambda bb, i: (0, 0)), pl.BlockSpec((blk, blk), lambda bb, i: (0, 0))],
        out_specs=[rev, pl.BlockSpec((1, n), lambda bb, i: (0, 0))],
        out_shape=[_sds((t, n), BF16), _sds((1, n), F32)], scratch_shapes=[pltpu.VMEM((1, n), F32)],
        compiler_params=_cp(2),
    )(dcum, hf, b_f, triu)


def wgrad(a, bm, name, with_bf16=False, bt=WGRAD_TOKENS):
    ca, t, kd = a.shape
    cb, _, nd = bm.shape
    c = max(ca, cb)
    bn = nd
    if nd > 1024 and nd % 1024 == 0:
        bn = 1024
    nsteps = t // bt

    def body(a_ref, b_ref, o_ref, *rest):
        @pl.when(pl.program_id(2) == 0)
        def _():
            o_ref[...] = jnp.zeros_like(o_ref)

        o_ref[...] += _dot_tn(a_ref[...].astype(BF16), b_ref[...].astype(BF16))
        if with_bf16:
            @pl.when(pl.program_id(2) == nsteps - 1)
            def _():
                rest[0][...] = o_ref[...].astype(BF16)

    out_spec = pl.BlockSpec((None, kd, bn), lambda cc, n, tt: (cc, 0, n))
    res = pl.pallas_call(
        body, name=name, grid=(c, nd // bn, nsteps),
        in_specs=[pl.BlockSpec((None, bt, kd), lambda cc, n, tt: (cc if ca > 1 else 0, tt, 0)),
                  pl.BlockSpec((None, bt, bn), lambda cc, n, tt: (cc if cb > 1 else 0, tt, n))],
        out_specs=[out_spec, out_spec] if with_bf16 else out_spec,
        out_shape=[_sds((c, kd, nd), F32), _sds((c, kd, nd), BF16)] if with_bf16 else _sds((c, kd, nd), F32),
        compiler_params=_cp(3),
    )(a, bm)
    return res


def ada_mod_part(c_all, ada_w, name):
    nl, d, n = ada_w.shape
    rows = c_all.shape[0]
    tn = 512

    def body(c_ref, w_ref, o_ref):
        cv = c_ref[...]
        act = (cv * jax.nn.sigmoid(cv)).astype(BF16)
        o_ref[...] = _dot(act, w_ref[...].astype(BF16))

    return pl.pallas_call(
        body, name=name, grid=(nl, n // tn),
        in_specs=[pl.BlockSpec((rows, d), lambda l, j: (0, 0)), pl.BlockSpec((None, d, tn), lambda l, j: (l, 0, j))],
        out_specs=pl.BlockSpec((None, rows, tn), lambda l, j: (l, 0, j)),
        out_shape=_sds((nl, rows, n), F32), compiler_params=_cp(2),
    )(c_all, ada_w)


def ada_grad(c_all_t, dmod, name):
    nl, rows, n = dmod.shape
    d = c_all_t.shape[0]
    tn = 512

    def body(c_ref, dm_ref, o_ref):
        cv = c_ref[...]
        act = (cv * jax.nn.sigmoid(cv)).astype(BF16)
        o_ref[...] = _dot(act, dm_ref[...].astype(BF16))

    return pl.pallas_call(
        body, name=name, grid=(nl, n // tn),
        in_specs=[pl.BlockSpec((d, rows), lambda l, j: (0, 0)), pl.BlockSpec((None, rows, tn), lambda l, j: (l, 0, j))],
        out_specs=pl.BlockSpec((None, d, tn), lambda l, j: (l, 0, j)),
        out_shape=_sds((nl, d, n), F32), compiler_params=_cp(2),
    )(c_all_t, dmod)


def sum_leading(a, name):
    g, r, n = a.shape

    def body(a_ref, o_ref):
        acc = a_ref[0]
        for kk in range(1, g):
            acc = acc + a_ref[kk]
        o_ref[...] = acc

    return pl.pallas_call(
        body, name=name, grid=(1,), in_specs=[pl.BlockSpec((g, r, n), lambda i: (0, 0, 0))],
        out_specs=pl.BlockSpec((r, n), lambda i: (0, 0)), out_shape=_sds((r, n), F32), compiler_params=_cp(1),
    )(a)


def adamw(w, g, m, v, name):
    r, n = w.shape
    br = r
    for cand in (512, 256, 128, 64, 32, 16, 8):
        if r % cand == 0 and r > cand and cand * n * 4 <= ADAMW_BLOCK_BYTES:
            br = cand
            break
    c1 = 1.0 - ADAM_B1 ** ADAM_STEP
    c2 = 1.0 - ADAM_B2 ** ADAM_STEP

    def body(w_ref, g_ref, m_ref, v_ref, d_ref, mo_ref, vo_ref):
        gv = g_ref[...]
        mn = ADAM_B1 * m_ref[...] + (1.0 - ADAM_B1) * gv
        vn = ADAM_B2 * v_ref[...] + (1.0 - ADAM_B2) * (gv * gv)
        mo_ref[...] = mn
        vo_ref[...] = vn
        d_ref[...] = -ADAM_LR * ((mn / c1) / (jnp.sqrt(vn / c2) + ADAM_EPS) + ADAM_WD * w_ref[...])

    spec = pl.BlockSpec((br, n), lambda i: (i, 0))
    return _call(body, name, (r // br,), [spec] * 4, [spec] * 3, [_sds((r, n), F32)] * 3, (w, g, m, v))


def all_gather8(x_blk, name, hosted=None):
    m_per, n = x_blk.shape
    h_in = 0 if hosted is None else len(hosted.inputs)
    h_out = 0 if hosted is None else len(hosted.out_shape)

    def body(x_ref, *refs):
        c_in, (out_ref, *c_out), (send_sems, recv_sems, local_sem, *c_sem) = (
            refs[:h_in], refs[h_in:h_in + 1 + h_out], refs[h_in + 1 + h_out:])
        if hosted is not None:
            hosted.start(c_in, c_out, c_sem)
        gather(x_ref, out_ref, send_sems, recv_sems, local_sem)
        if hosted is not None:
            hosted.finish(c_in, c_out, c_sem)

    def gather(x_ref, out_ref, send_sems, recv_sems, local_sem):
        x, y, c = _place()
        me, sibling = (x, y, c), (x, y, 1 - c)
        chips = [(1 - x, y), (x, 1 - y), (1 - x, 1 - y)]

        def rows(px, py, pc):
            return out_ref.at[pl.ds((4 * px + 2 * py + pc) * m_per, m_per), :]

        def copy(k, block, to, src=None):
            return pltpu.make_async_remote_copy(
                src_ref=rows(*block) if src is None else src, dst_ref=rows(*block),
                send_sem=send_sems.at[k], recv_sem=recv_sems.at[k], device_id=to, device_id_type=MESH)

        mine = pltpu.make_async_copy(x_ref, rows(*me), local_sem)
        mine.start()
        first = [copy(0, me, sibling, src=x_ref)]
        first += [copy(1 + j, me, (*chip, c), src=x_ref) for j, chip in enumerate(chips)]
        for cp in first:
            cp.start()
        passed = [copy(4 + j, (*chip, c), sibling) for j, chip in enumerate(chips)]
        for j, chip in enumerate(chips):
            copy(1 + j, (*chip, c), me).wait_recv()
            passed[j].start()
        copy(0, sibling, me).wait_recv()
        for j, chip in enumerate(chips):
            copy(4 + j, (*chip, 1 - c), me).wait_recv()
        for cp in first + passed:
            cp.wait_send()
        mine.wait()

    hbm = pl.BlockSpec(memory_space=pl.ANY)
    vmem = pl.BlockSpec(memory_space=pltpu.VMEM)
    res = pl.pallas_call(
        body, name=name,
        out_shape=[_sds((8 * m_per, n), x_blk.dtype)] + ([] if hosted is None else list(hosted.out_shape)),
        in_specs=[vmem] + [hbm] * h_in, out_specs=[vmem] + [hbm] * h_out,
        scratch_shapes=[pltpu.SemaphoreType.DMA((7,)), pltpu.SemaphoreType.DMA((7,)), pltpu.SemaphoreType.DMA]
        + ([] if hosted is None else list(hosted.sems)),
        compiler_params=pltpu.CompilerParams(vmem_limit_bytes=VMEM_LIMIT),
    )(x_blk, *([] if hosted is None else hosted.inputs))
    return res[0] if hosted is None else (res[0], res[1:])


def _gather_comm(shards):
    nt = len(shards)

    def parts(w_refs, out_refs, sems, finishing):
        send_sems, recv_sems, own_send, own_recv = sems
        x, y, c = _place()
        sibling = (x, y, 1 - c)
        chips = [(1 - x, y), (x, 1 - y), (1 - x, 1 - y)]

        def copy(t, k, block, to, src=None):
            px, py, hh = block
            dst = out_refs[t].at[2 * px + py, hh]
            return pltpu.make_async_remote_copy(
                src_ref=dst if src is None else src, dst_ref=dst,
                send_sem=send_sems.at[6 * t + k], recv_sem=recv_sems.at[6 * t + k], device_id=to, device_id_type=MESH)

        own = [pltpu.make_async_remote_copy(
            src_ref=w_refs[t], dst_ref=out_refs[t].at[2 * x + y], send_sem=own_send.at[t], recv_sem=own_recv.at[t],
            device_id=sibling, device_id_type=MESH) for t in range(nt)]
        first = [copy(t, j, (x, y, c), (*chip, c), src=w_refs[t].at[c]) for t in range(nt) for j, chip in enumerate(chips)]
        if not finishing:
            return own, first
        landed = [copy(t, j, (*chip, c), (x, y, c)) for t in range(nt) for j, chip in enumerate(chips)]
        passed = [copy(t, 3 + j, (*chip, c), sibling) for t in range(nt) for j, chip in enumerate(chips)]
        from_sibling = [copy(t, 3 + j, (*chip, 1 - c), (x, y, c)) for t in range(nt) for j, chip in enumerate(chips)]
        return own, first, landed, passed, from_sibling

    def start(w_refs, out_refs, sems):
        own, first = parts(w_refs, out_refs, sems, False)
        for cp in own + first:
            cp.start()

    def finish(w_refs, out_refs, sems):
        own, first, landed, passed, from_sibling = parts(w_refs, out_refs, sems, True)
        for arrived, fwd in zip(landed, passed):
            arrived.wait_recv()
            fwd.start()
        for cp in from_sibling:
            cp.wait_recv()
        for cp in first + passed:
            cp.wait_send()
        for cp in own:
            cp.wait()

    sems = [pltpu.SemaphoreType.DMA((6 * nt,)), pltpu.SemaphoreType.DMA((6 * nt,)),
            pltpu.SemaphoreType.DMA((nt,)), pltpu.SemaphoreType.DMA((nt,))]
    return _Hosted(list(shards), [_sds((N_CHIPS, *w.shape), w.dtype) for w in shards], sems, start, finish)


def _row_block(r, n, itemsize):
    best = None
    for br in range(16, r + 1, 16):
        if r % br == 0 and br * n * itemsize <= COMM_BLOCK_BYTES:
            best = br
    return r if best is None else best


def _scatter_comm(parts):
    nt = len(parts)

    def copies(p_refs, b_refs, sems, arriving):
        send_sems, recv_sems = sems
        x, y, c = _place()
        me = 4 * x + 2 * y + c
        cps = []
        for t in range(nt):
            for r in range(1, 8):
                tx = 1 - x if r & 4 else x
                ty = 1 - y if r & 2 else y
                tc = 1 - c if r & 1 else c
                src, dst = (2 * x + y, c), 4 * tx + 2 * ty + tc
                if not arriving:
                    src, dst = (2 * tx + ty, tc), me
                cps.append(pltpu.make_async_remote_copy(
                    src_ref=p_refs[t].at[src], dst_ref=b_refs[t].at[dst], send_sem=send_sems.at[7 * t + r - 1],
                    recv_sem=recv_sems.at[7 * t + r - 1], device_id=(tx, ty, tc), device_id_type=MESH))
        return cps

    def start(p_refs, b_refs, sems):
        for cp in copies(p_refs, b_refs, sems, False):
            cp.start()

    def finish(p_refs, b_refs, sems):
        for cp in copies(p_refs, b_refs, sems, True):
            cp.wait_recv()
        for cp in copies(p_refs, b_refs, sems, False):
            cp.wait_send()

    sems = [pltpu.SemaphoreType.DMA((7 * nt,)), pltpu.SemaphoreType.DMA((7 * nt,))]
    return _Hosted(list(parts), [_sds((2 * N_CHIPS, *p.shape[2:]), p.dtype) for p in parts], sems, start, finish)


def sum_devices(own, recv, place, name, slot=(0, 1, None)):
    _, _, r, n = own.shape
    layer, n_layers, buf = slot
    br = _row_block(r, n, 4 * 8)

    def body(p_ref, o_ref, *rest):
        acc = o_ref[...]
        for kk in range(7):
            acc = acc + rest[kk][...].astype(F32)
        rest[-1][...] = acc

    def arrived(rel):
        return pl.BlockSpec((None, br, n), lambda i, pref: (jnp.bitwise_xor(pref[0], rel), i, 0))

    in_specs = [pl.BlockSpec((None, None, br, n), lambda i, pref: (pref[2], pref[1], i, 0))]
    in_specs += [arrived(rel) for rel in range(1, 8)]
    args = [own] + [recv] * 7
    aliases = {}
    if buf is not None:
        in_specs.append(pl.BlockSpec(memory_space=pl.ANY))
        args.append(buf)
        aliases = {9: 0}
    return pl.pallas_call(
        body, name=name,
        grid_spec=pltpu.PrefetchScalarGridSpec(
            num_scalar_prefetch=1, grid=(r // br,), in_specs=in_specs,
            out_specs=pl.BlockSpec((None, None, br, n), lambda i, pref: (layer, pref[1], i, 0))),
        out_shape=_sds((n_layers, 2, r, n), F32), input_output_aliases=aliases, compiler_params=_cp(1),
    )(place, *args)


def _join_comm(bufs):
    nt = len(bufs)
    layers = [bf.shape[0] for bf in bufs]
    first = [sum(layers[:t]) for t in range(nt)]

    def copies(o_refs, sems, own):
        send_sems, recv_sems = sems
        x, y, c = _place()
        hh = c if own else 1 - c
        return [pltpu.make_async_remote_copy(
            src_ref=o_refs[t].at[l, hh], dst_ref=o_refs[t].at[l, hh], send_sem=send_sems.at[first[t] + l],
            recv_sem=recv_sems.at[first[t] + l], device_id=(x, y, 1 - c), device_id_type=MESH)
            for t in range(nt) for l in range(layers[t])]

    def start(_, o_refs, sems):
        for cp in copies(o_refs, sems, True):
            cp.start()

    def finish(_, o_refs, sems):
        for cp in copies(o_refs, sems, False):
            cp.wait_recv()
        for cp in copies(o_refs, sems, True):
            cp.wait_send()

    sems = [pltpu.SemaphoreType.DMA((sum(layers),)), pltpu.SemaphoreType.DMA((sum(layers),))]
    return _Hosted(list(bufs), [_sds(bf.shape, bf.dtype) for bf in bufs], sems, start, finish, in_place=True)


def sibling_join_halves(bufs, name):
    comm = _join_comm(bufs)
    nt = len(bufs)

    def body(*refs):
        comm.start(refs[:nt], refs[nt:2 * nt], refs[2 * nt:])
        comm.finish(refs[:nt], refs[nt:2 * nt], refs[2 * nt:])

    hbm = pl.BlockSpec(memory_space=pl.ANY)
    return pl.pallas_call(body, name=name, out_shape=comm.out_shape, in_specs=[hbm] * nt, out_specs=[hbm] * nt,
                          input_output_aliases={k: k for k in range(nt)}, scratch_shapes=comm.sems)(*bufs)


_SHARD_KIND = {"mla_w_in": "rows", "mla_w_uq": "cols", "mla_w_uk": "cols", "mla_w_uv": "cols", "mla_w_o": "rows",
               "fox_w_in": "cols", "fox_w_o": "rows", "ffn_w_gate": "chunk", "ffn_w_up": "chunk", "ffn_w_down": "chunk"}
_PACKED = tuple(_SHARD_KIND)
_TRANSPOSED = ("ffn_w_gate", "ffn_w_up", "fox_w_in")


def _halves(shard):
    if shard.ndim == 3 and shard.shape[0] == 2:
        return shard
    r, n = shard.shape[-2:]
    return shard.reshape(2, r // 2, n)


def _cols_to_full(g):
    return jnp.transpose(g, (1, 0, 2)).reshape(g.shape[1], -1)


def _full_to_cols(w):
    k, n4 = w.shape
    return jnp.transpose(w.reshape(k, N_CHIPS, n4 // N_CHIPS), (1, 0, 2))


def _uq_perm():
    per = MLA_NOPE + MLA_ROPE
    half = MLA_ROPE // 2
    nope = [h * per + d for h in range(MLA_HEADS) for d in range(MLA_NOPE)]
    r1 = [h * per + MLA_NOPE + r for h in range(MLA_HEADS) for r in range(half)]
    r2 = [h * per + MLA_NOPE + half + r for h in range(MLA_HEADS) for r in range(half)]
    perm = np.array(nope + r1 + r2, dtype=np.int32)
    return perm, np.argsort(perm).astype(np.int32)


def _rope_matrices():
    half = MLA_ROPE // 2
    nr = MLA_HEADS * MLA_ROPE
    to_heads = np.zeros((nr, nr), np.float32)
    from_heads = np.zeros((MLA_HEADS * 128, nr), np.float32)
    for e in range(2):
        for h in range(MLA_HEADS):
            for r in range(half):
                to_heads[e * MLA_HEADS * half + h * half + r, h * MLA_ROPE + e * half + r] = 1.0
                from_heads[h * 128 + e * half + r, e * MLA_HEADS * half + h * half + r] = 1.0
    head_sum = np.tile(np.eye(MLA_ROPE, dtype=np.float32), (2 * MLA_HEADS, 1))
    dup = np.concatenate([np.eye(MLA_ROPE, dtype=np.float32)] * 2, axis=1)
    return to_heads, from_heads, head_sum, dup


def _ffn_weights(gathered):
    return tuple(g.reshape(N_CHIPS, 2 * g.shape[2], g.shape[3]) for g in gathered)


def _fox_weights(gathered):
    w_in, w_o = gathered
    w_in = jnp.transpose(w_in, (0, 2, 1, 3)).reshape(N_CHIPS * w_in.shape[2], 2 * w_in.shape[3])
    return w_in, w_o.reshape(-1, w_o.shape[-1])


def _local_step(x, positions, target, mods, wts, ln_g, ln_b, mla_g_q, mla_g_kv, fox_b_f, shards=None):
    nb, s, d = x.shape
    t = nb * s
    x0 = x.reshape(t, d)
    tgt = target.reshape(t, d)
    perm, inv_perm = _uq_perm()

    half = MLA_ROPE // 2
    inv_freq = ROPE_THETA ** (-jnp.arange(half, dtype=F32) / half)
    ang = positions.astype(F32).reshape(t, 1) * inv_freq
    cos, sin = jnp.cos(ang), jnp.sin(ang)
    cos8, sin8 = jnp.tile(cos, (1, MLA_HEADS)), jnp.tile(sin, (1, MLA_HEADS))
    cos64 = jnp.concatenate([cos, cos], axis=1)
    sin64s = jnp.concatenate([-sin, sin], axis=1)
    swap64 = jnp.asarray(np.roll(np.eye(MLA_ROPE, dtype=np.float32), half, axis=1))
    to_heads, from_heads, head_sum, dup = _rope_matrices()
    to_heads, from_heads = jnp.asarray(to_heads, dtype=BF16), jnp.asarray(from_heads, dtype=BF16)
    head_sum, dup = jnp.asarray(head_sum, dtype=BF16), jnp.asarray(dup, dtype=BF16)
    sel_mla = jnp.asarray(np.pad(np.kron(np.eye(MLA_HEADS, dtype=np.float32), np.ones((MLA_V, 1), np.float32)),
                                 ((0, 0), (0, 128 - MLA_HEADS))))
    sel_fox = jnp.asarray(np.pad(np.kron(np.eye(FOX_HEADS, dtype=np.float32), np.ones((FOX_HD, 1), np.float32)),
                                 ((0, 0), (0, 128 - FOX_HEADS))))
    tri = jnp.asarray(np.tril(np.ones((128, 128), np.float32)))
    triu = jnp.asarray(np.triu(np.ones((128, 128), np.float32)))
    onehot16 = jnp.asarray(np.eye(16, 128, dtype=np.float32))

    def vec(a):
        return a.reshape(1, -1)

    def carried(key):
        return None if shards is None else _gather_comm(shards[key])

    def split(res):
        return (res, None) if shards is None else res

    w_uq_p = wts["mla_w_uq"][:, perm]
    b_f_pad = jnp.pad(fox_b_f.reshape(1, -1), ((0, 0), (0, 128 - FOX_HEADS)))

    sh_a, sc_a, gt_a, sh_f, sc_f, gt_f = mods[0]
    h_in, u_m = mod_linear(x0, sh_a, sc_a, wts["mla_w_in"], F32, "mla_in", emit_u=True)
    q_m, kn_m, v_m, kr2_m, cq_m, ckv_m = mla_mid_fwd(
        h_in, vec(mla_g_q), vec(mla_g_kv), w_uq_p, wts["mla_w_uk"], wts["mla_w_uv"], cos8, sin8, cos64, sin64s, swap64,
        to_heads, dup, "mla_mid")
    (o_m, lse_m), got = split(mla_attn_fwd(q_m, kn_m, kr2_m, v_m, nb, "mla_attn", hosted=carried("ffn0")))
    ffn0_w = wts["ffn"][0] if got is None else _ffn_weights(got)
    y0, x1 = linear_resid_ln(o_m, wts["mla_w_o"], x0, gt_a, vec(ln_g[0, 0]), vec(ln_b[0, 0]), "mla_out")
    (u_f0, hg0, hu0, y1, x2), got = split(ffn_fwd(x1, sh_f, sc_f, gt_f, *ffn0_w, vec(ln_g[0, 1]), vec(ln_b[0, 1]), "ffn0",
                                                  hosted=carried("fox")))
    fox_w_in_t, fox_w_o = (wts["fox_w_in"].T, wts["fox_w_o"]) if got is None else _fox_weights(got)
    fox_w_f_t = jnp.pad(fox_w_in_t[3 * d:], ((0, 128 - FOX_HEADS), (0, 0)))
    sh_a1, sc_a1, gt_a1, sh_f1, sc_f1, gt_f1 = mods[1]
    qkv, u_x = mod_linear(x2, sh_a1, sc_a1, fox_w_in_t, BF16, "fox_qkv", tn=1024, emit_u=True, w_rows=3 * d)
    hf = mod_linear(x2, sh_a1, sc_a1, fox_w_f_t, F32, "fox_f", w_rows=128)
    cum = fox_gate_fwd(hf, b_f_pad, tri, nb, "fox_gate")
    cum_rows = rows16(cum, "fox_cum_rows")
    (o_x, lse_x), got = split(fox_attn_fwd(qkv, cum, cum_rows, nb, "fox_attn", hosted=carried("ffn1")))
    ffn1_w = wts["ffn"][1] if got is None else _ffn_weights(got)
    y2, x3 = linear_resid_ln(o_x, fox_w_o, x2, gt_a1, vec(ln_g[1, 0]), vec(ln_b[1, 0]), "fox_out")
    u_f1, hg1, hu1, y3, x4 = ffn_fwd(x3, sh_f1, sc_f1, gt_f1, *ffn1_w, vec(ln_g[1, 1]), vec(ln_b[1, 1]), "ffn1")

    parts, recv = {}, {}

    def halves_of(g):
        return g.reshape(N_CHIPS, 2, g.shape[1] // 2, g.shape[2])

    def scatter(keys, sent):
        return None if shards is None else _scatter_comm([sent[k] for k in keys])

    def landed(keys, got):
        if got is not None:
            recv.update(zip(keys, got))

    def ffn_grads(layer, u, dhg, dhu, act, dy):
        sent = {}
        for n, (a_op, b_op) in (("ffn_w_gate", (dhg, u[None])), ("ffn_w_up", (dhu, u[None])), ("ffn_w_down", (act, dy[None]))):
            g32, g16 = wgrad(a_op, b_op, "ffn%d_d%s" % (layer, n[4:]), with_bf16=True)
            parts["%s/%d" % (n, layer)], sent["%s/%d" % (n, layer)] = halves_of(g32), halves_of(g16)
        return sent

    dz3, dy3, dg11, db11, dgt_f1, sq_err = ln_bwd(x4, x3, y3, gt_f1, vec(ln_g[1, 1]), "ffn1_ln_bwd", target=tgt)
    loss_part = 0.5 * jnp.sum(sq_err) / d
    dhg1, dhu1, act1, dx3, dsc_f1, dsh_f1 = ffn_bwd(dy3, hg1, hu1, *ffn1_w, dz3, x3, sc_f1, "ffn1_bwd")
    sent = ffn_grads(1, u_f1, dhg1, dhu1, act1, dy3)
    dz2, dy2, dg10, db10, dgt_a1 = ln_bwd(dx3, x2, y2, gt_a1, vec(ln_g[1, 0]), "fox_ln_bwd")
    do_x, delta_x = linear_nt_delta(dy2, fox_w_o, o_x, sel_fox, "fox_out_bwd")
    (dq_x, dk_x, dv_x, dfq_x, dfk_x), got = split(fox_attn_bwd(
        qkv, do_x, cum, cum_rows, rows16(lse_x, "fox_lse_rows"), rows16(delta_x, "fox_delta_rows"), nb, "fox_attn_bwd",
        hosted=scatter(list(sent), sent)))
    landed(list(sent), got)
    dcum = tokens128(dfq_x + dfk_x, onehot16, "fox_dcum")
    dhf, dbf = fox_gate_bwd(dcum, hf, b_f_pad, triu, nb, "fox_gate_bwd")
    fox_d = [("q", dq_x), ("k", dk_x), ("v", dv_x)]
    dx2, dsc_a1, dsh_a1 = linear_nt_mod_bwd(
        [(dh, fox_w_in_t, i) for i, (_, dh) in enumerate(fox_d)] + [(dhf, fox_w_f_t, 0)], dz2, x2, sc_a1, "fox_in_bwd")
    dw_in_t = [wgrad(dh[None], u_x[None], "fox_dw" + tag)[0] for tag, dh in fox_d]
    dw_in_t.append(wgrad(dhf[None], u_x[None], "fox_dwf")[0][:FOX_HEADS])
    dw_in_t = jnp.concatenate(dw_in_t, axis=0).reshape(N_CHIPS, -1, 2, d // 2)
    parts["fox_w_in"] = jnp.transpose(dw_in_t, (0, 2, 1, 3))
    parts["fox_w_o"] = wgrad(o_x[None], dy2[None], "fox_dwo")[0].reshape(N_CHIPS, 2, -1, d)
    sent = {k: parts[k].astype(BF16) for k in ("fox_w_in", "fox_w_o")}
    dz1, dy1, dg01, db01, dgt_f0 = ln_bwd(dx2, x1, y1, gt_f, vec(ln_g[0, 1]), "ffn0_ln_bwd")
    (dhg0, dhu0, act0, dx1, dsc_f0, dsh_f0), got = split(ffn_bwd(dy1, hg0, hu0, *ffn0_w, dz1, x1, sc_f, "ffn0_bwd",
                                                                 hosted=scatter(list(sent), sent)))
    landed(list(sent), got)
    sent = ffn_grads(0, u_f0, dhg0, dhu0, act0, dy1)
    dz0, dy0, dg00, db00, dgt_a0 = ln_bwd(dx1, x0, y0, gt_a, vec(ln_g[0, 0]), "mla_ln_bwd")
    do_m, delta_m = linear_nt_delta(dy0, wts["mla_w_o"], o_m, sel_mla, "mla_out_bwd")
    parts["mla_w_o"] = wgrad(o_m[None], dy0[None], "mla_dwo")[0].reshape(N_CHIPS, 2, -1, d)
    (dqn_m, dqr_m, dkn_m, dkr_m, dv_m), got = split(mla_attn_bwd(
        q_m, kn_m, kr2_m, v_m, do_m, rows16(lse_m, "mla_lse_rows"), rows16(delta_m, "mla_delta_rows"), nb,
        "mla_attn_bwd", hosted=scatter(list(sent), sent)))
    landed(list(sent), got)
    dh_in, dq_pre, dgq, dgkv = mla_mid_bwd(
        dqn_m, dqr_m, dkn_m, dv_m, dkr_m, h_in, vec(mla_g_q), vec(mla_g_kv), w_uq_p, wts["mla_w_uk"],
        wts["mla_w_uv"], cos8, sin8, cos64, sin64s, swap64, from_heads, head_sum, "mla_mid_bwd")
    parts["mla_w_uq"] = halves_of(_full_to_cols(wgrad(cq_m[None], dq_pre[None], "mla_dwuq")[0][:, inv_perm]))
    parts["mla_w_uk"] = halves_of(_full_to_cols(wgrad(ckv_m[None], dkn_m[None], "mla_dwuk")[0]))
    parts["mla_w_uv"] = halves_of(_full_to_cols(wgrad(ckv_m[None], dv_m[None], "mla_dwuv")[0]))
    parts["mla_w_in"] = wgrad(u_m[None], dh_in[None], "mla_dwin")[0].reshape(N_CHIPS, 2, -1, h_in.shape[1])
    sent = {k: parts[k].astype(BF16) for k in ("mla_w_in", "mla_w_uq", "mla_w_uk", "mla_w_uv", "mla_w_o")}
    (dx0, dsc_a0, dsh_a0), got = split(linear_nt_mod_bwd([(dh_in, wts["mla_w_in"], None)], dz0, x0, sc_a, "mla_in_bwd",
                                                         hosted=scatter(list(sent), sent)))
    landed(list(sent), got)

    dmods = [(dsh_a0, dsc_a0, dgt_a0, dsh_f0, dsc_f0, dgt_f0), (dsh_a1, dsc_a1, dgt_a1, dsh_f1, dsc_f1, dgt_f1)]
    d_ln_g = jnp.stack([jnp.concatenate([dg00, dg01], axis=0), jnp.concatenate([dg10, dg11], axis=0)])
    d_ln_b = jnp.stack([jnp.concatenate([db00, db01], axis=0), jnp.concatenate([db10, db11], axis=0)])
    return loss_part, dx0.reshape(nb, s, d), (parts, recv), dmods, d_ln_g, d_ln_b, dgq, dgkv, dbf[:, :FOX_HEADS]


def _pad_rows(a, rows):
    return jnp.pad(a, ((0, rows - a.shape[0]), (0, 0)))


def kernel(x, c, positions, mla_w_in, mla_g_q, mla_w_uq, mla_g_kv, mla_w_uk, mla_w_uv, mla_w_o, fox_w_in, fox_b_f, fox_w_o, ada_w, ada_b, ffn_w_gate, ffn_w_up, ffn_w_down, ln_g, ln_b, loss_target, m_mla_w_in, m_mla_g_q, m_mla_w_uq, m_mla_g_kv, m_mla_w_uk, m_mla_w_uv, m_mla_w_o, m_fox_w_in, m_fox_b_f, m_fox_w_o, m_ada_w, m_ada_b, m_ffn_w_gate, m_ffn_w_up, m_ffn_w_down, m_ln_g, m_ln_b, v_mla_w_in, v_mla_g_q, v_mla_w_uq, v_mla_g_kv, v_mla_w_uk, v_mla_w_uv, v_mla_w_o, v_fox_w_in, v_fox_b_f, v_fox_w_o, v_ada_w, v_ada_b, v_ffn_w_gate, v_ffn_w_up, v_ffn_w_down, v_ln_g, v_ln_b):
    args = dict(locals())
    nb, s, d = x.shape
    ax, ay, ac = lax.axis_index("x"), lax.axis_index("y"), lax.axis_index("c")
    chip = 2 * ax + ay
    dev = 2 * chip + ac
    n_dev = 2 * N_CHIPS
    n_all = nb * n_dev

    shard_shapes = {n: (args[n].shape if _SHARD_KIND[n] == "chunk" else args[n].shape[1:]) for n in _PACKED}

    def block(n, layer=None):
        w = args[n].reshape(shard_shapes[n]) if layer is None else args[n][layer]
        return _halves(w.astype(BF16))

    mla_names = [n for n in _PACKED if n.startswith("mla")]
    ffn_names = ("ffn_w_gate", "ffn_w_up", "ffn_w_down")
    fox_in_t = jnp.swapaxes(fox_w_in, 1, 2)[0].astype(BF16)
    fox_in_t = jnp.stack([fox_in_t[:, :d // 2], fox_in_t[:, d // 2:]])
    shards = {"ffn0": [block(n, 0) for n in ffn_names], "fox": [fox_in_t, block("fox_w_o")],
              "ffn1": [block(n, 1) for n in ffn_names]}

    ln_cols = ln_g.shape[-1]
    ln_blk = jnp.concatenate([ln_g.reshape(2 * DEPTH, ln_cols), ln_b.reshape(2 * DEPTH, ln_cols)], axis=0)
    early = jnp.concatenate([_pad_rows(c, 8), jnp.pad(_pad_rows(ln_blk, 8), ((0, 0), (0, d - ln_cols)))], axis=0)
    early, mla_all = all_gather8(early, "gather_c_ln_mla", hosted=_gather_comm([block(n) for n in mla_names]))
    wts = {}
    for n, g in zip(mla_names, mla_all):
        g = g.reshape(N_CHIPS, *shard_shapes[n])
        wts[n] = g.reshape(-1, g.shape[-1]) if _SHARD_KIND[n] == "rows" else _cols_to_full(g)
    early = early.reshape(n_dev, 16, d)
    c_all = early[:, :nb].reshape(n_all, d)
    ln_all = early.reshape(N_CHIPS, 2, 16, d)[:, 0, 8:8 + 4 * DEPTH, :ln_cols]
    ln_all = jnp.transpose(ln_all, (1, 0, 2)).reshape(4 * DEPTH, d)
    ln_g_full = ln_all[:2 * DEPTH].reshape(DEPTH, 2, d)
    ln_b_full = ln_all[2 * DEPTH:].reshape(DEPTH, 2, d)
    mod_part = ada_mod_part(c_all, ada_w, "ada_mod")
    ncol = mod_part.shape[-1]
    mod_g = all_gather8(mod_part.reshape(DEPTH * n_all, ncol), "gather_mod")
    mod_g = mod_g.reshape(N_CHIPS, 2, DEPTH, n_all, ncol)[:, 0]
    mod_full = jnp.transpose(mod_g, (1, 2, 0, 3)).reshape(DEPTH, n_all, N_CHIPS * ncol) + ada_b[:, None, :]
    mod_loc = lax.dynamic_slice_in_dim(mod_full, dev * nb, nb, axis=1)
    mods = [tuple(mod_loc[i, :, k * d:(k + 1) * d].reshape(nb, 1, d) for k in range(6)) for i in range(DEPTH)]

    loss_part, grad_x, (parts, recv), dmods, d_ln_g, d_ln_b, dgq, dgkv, dbf = _local_step(
        x, positions, loss_target, mods, wts, ln_g_full, ln_b_full, mla_g_q[0], mla_g_kv[0], fox_b_f[0], shards)
    loss = lax.psum(loss_part, ("x", "y", "c"))

    dmod_rows = jnp.stack([jnp.concatenate([v_.reshape(nb, d) for v_ in dm], axis=1) for dm in dmods])
    small = jnp.concatenate([
        d_ln_g.reshape(2 * DEPTH, d), d_ln_b.reshape(2 * DEPTH, d),
        jnp.pad(jnp.concatenate([dgq, dgkv, dbf], axis=1), ((0, 0), (0, d - 2 * MLA_QR - FOX_HEADS))),
        dmod_rows.reshape(DEPTH * nb * 6, d)], axis=0)
    n_small = small.shape[0]
    small_rows = -(-n_small // 8) * 8
    small_all = all_gather8(_pad_rows(small, small_rows), "gather_stats").reshape(n_dev, small_rows, d)
    stat_sum = sum_leading(small_all, "sum_stats")
    g_ln_g = lax.dynamic_slice_in_dim(stat_sum[:2 * DEPTH], chip * ln_cols, ln_cols, axis=1).reshape(DEPTH, 2, ln_cols)
    g_ln_b = lax.dynamic_slice_in_dim(stat_sum[2 * DEPTH:4 * DEPTH], chip * ln_cols, ln_cols, axis=1).reshape(DEPTH, 2, ln_cols)
    row = stat_sum[4 * DEPTH]
    g_gq = row[:MLA_QR].reshape(1, MLA_QR)
    g_gkv = row[MLA_QR:2 * MLA_QR].reshape(1, MLA_KVR)
    g_bf = row[2 * MLA_QR:2 * MLA_QR + FOX_HEADS].reshape(1, FOX_HEADS)
    base = 4 * DEPTH + 1
    dmod_all = small_all[:, base:base + DEPTH * nb * 6].reshape(n_dev, DEPTH, nb, 6 * d)
    dmod_all = jnp.transpose(dmod_all, (1, 0, 2, 3)).reshape(DEPTH, n_all, 6 * d)
    g_ada_b = sum_leading(jnp.transpose(dmod_all, (1, 0, 2)), "sum_ada_b")
    dmod_mine = lax.dynamic_slice_in_dim(dmod_all, chip * ncol, ncol, axis=2)
    g_ada_w = ada_grad(c_all.T, dmod_mine, "ada_grad")

    place = jnp.stack([dev, ac, chip]).astype(jnp.int32)
    bufs = []
    for n in _PACKED:
        if _SHARD_KIND[n] == "chunk":
            buf = None
            for layer in range(DEPTH):
                key = "%s/%d" % (n, layer)
                buf = sum_devices(parts[key], recv[key], place, "rs_sum_%s%d" % (n, layer), slot=(layer, DEPTH, buf))
        else:
            buf = sum_devices(parts[n], recv[n], place, "rs_sum_" + n)
        bufs.append(buf)
    joined = sibling_join_halves(bufs, "rs_join")
    g_big = {n: j.reshape(j.shape[0], 2 * j.shape[2], j.shape[3]) for n, j in zip(_PACKED, joined)}
    j = joined[_PACKED.index("fox_w_in")]
    g_big["fox_w_in"] = jnp.transpose(j, (0, 2, 1, 3)).reshape(1, j.shape[2], 2 * j.shape[3])

    g_out = {
        "mla_w_in": g_big["mla_w_in"], "mla_g_q": g_gq, "mla_w_uq": g_big["mla_w_uq"], "mla_g_kv": g_gkv,
        "mla_w_uk": g_big["mla_w_uk"], "mla_w_uv": g_big["mla_w_uv"], "mla_w_o": g_big["mla_w_o"],
        "fox_w_in": g_big["fox_w_in"], "fox_b_f": g_bf, "fox_w_o": g_big["fox_w_o"],
        "ada_w": g_ada_w, "ada_b": g_ada_b, "ffn_w_gate": g_big["ffn_w_gate"], "ffn_w_up": g_big["ffn_w_up"],
        "ffn_w_down": g_big["ffn_w_down"], "ln_g": g_ln_g, "ln_b": g_ln_b}
    names = ["mla_w_in", "mla_g_q", "mla_w_uq", "mla_g_kv", "mla_w_uk", "mla_w_uv", "mla_w_o", "fox_w_in", "fox_b_f",
             "fox_w_o", "ada_w", "ada_b", "ffn_w_gate", "ffn_w_up", "ffn_w_down", "ln_g", "ln_b"]
    small_names = ["mla_g_q", "mla_g_kv", "fox_b_f", "ada_b", "ln_g", "ln_b"]
    deltas, new_m, new_v = {}, {}, {}
    for n in names:
        if n in small_names:
            continue
        shp = args[n].shape
        if n in _TRANSPOSED:
            view = lambda a: jnp.swapaxes(a, 1, 2).reshape(-1, shp[1])
            back = lambda a: jnp.swapaxes(a.reshape(shp[0], shp[2], shp[1]), 1, 2)
        else:
            view = lambda a: a.reshape(-1, shp[-1])
            back = lambda a: a.reshape(shp)
        dl, mn, vn = adamw(view(args[n]), g_out[n].reshape(view(args[n]).shape), view(args["m_" + n]),
                           view(args["v_" + n]), "adamw_" + n)
        g_out[n], deltas[n], new_m[n], new_v[n] = back(g_out[n].reshape(view(args[n]).shape)), back(dl), back(mn), back(vn)

    def small_pack(prefix, src):
        flat = jnp.concatenate([src[prefix + n].reshape(-1) for n in small_names])
        size = -(-flat.shape[0] // (8 * 128)) * 8 * 128
        return jnp.pad(flat, (0, size - flat.shape[0])).reshape(-1, 128)

    sd, sm, sv = adamw(small_pack("", args), small_pack("", g_out), small_pack("m_", args), small_pack("v_", args),
                       "adamw_small")
    off = 0
    for n in small_names:
        shp = args[n].shape
        size = math.prod(shp)
        deltas[n] = sd.reshape(-1)[off:off + size].reshape(shp)
        new_m[n] = sm.reshape(-1)[off:off + size].reshape(shp)
        new_v[n] = sv.reshape(-1)[off:off + size].reshape(shp)
        off += size

    outs = [loss, grad_x]
    outs += [g_out[n].reshape(args[n].shape) for n in names]
    outs += [deltas[n] for n in names] + [new_m[n] for n in names] + [new_v[n] for n in names]
    return tuple(outs)
```

```python
import functools
import math

import numpy as np
import jax
import jax.numpy as jnp
from jax import lax
from jax.experimental import pallas as pl
from jax.experimental.pallas import tpu as pltpu

F32 = jnp.float32
BF16 = jnp.bfloat16
MESH = pl.DeviceIdType.MESH

D_MODEL = 1024
DEPTH = 2
MLA_HEADS = 8
MLA_NOPE = 128
MLA_ROPE = 64
MLA_V = 128
MLA_QR = 256
MLA_KVR = 256
ROPE_THETA = 10000.0
FOX_HEADS = 16
FOX_HD = 64
D_FF = 2816
N_CHIPS = 4
FF_CHUNK = D_FF // N_CHIPS
ALPHA = (2.0 * DEPTH) ** 0.25
EPS = 1e-5
ADAM_LR = 0.001
ADAM_B1 = 0.9
ADAM_B2 = 0.999
ADAM_EPS = 1e-08
ADAM_WD = 0.01
ADAM_STEP = 10

VMEM_LIMIT = 56 * 1024 * 1024
TOKEN_TILE = 512
WGRAD_TOKENS = 2048
ATTN_TILE = 512
FOX_GROUP = 8
MLA_GROUP = 4
COMM_BLOCK_BYTES = 2 * 1024 * 1024
ADAMW_BLOCK_BYTES = 1024 * 1024


def _cp(n_axes):
    return pltpu.CompilerParams(dimension_semantics=("arbitrary",) * n_axes, vmem_limit_bytes=VMEM_LIMIT)


def _dot(a, b):
    return jnp.dot(a, b, preferred_element_type=F32)


def _dot_nt(a, b):
    return lax.dot_general(a, b, (((1,), (1,)), ((), ())), preferred_element_type=F32)


def _dot_tn(a, b):
    return lax.dot_general(a, b, (((0,), (0,)), ((), ())), preferred_element_type=F32)


def _dot_f32(a, b):
    return jnp.dot(a, b, preferred_element_type=F32, precision=lax.Precision.HIGHEST)


def _sds(shape, dtype):
    return jax.ShapeDtypeStruct(shape, dtype)


def _place():
    return lax.axis_index("x"), lax.axis_index("y"), lax.axis_index("c")


class _Hosted:
    def __init__(self, inputs, out_shape, sems, start, finish, in_place=False):
        self.inputs, self.out_shape, self.sems, self.start, self.finish = inputs, out_shape, sems, start, finish
        self.in_place = in_place


def _call(body, name, grid, in_specs, out_specs, out_shape, args, scratch_shapes=(), hosted=None):
    in_specs, out_specs, out_shape, scratch_shapes = list(in_specs), list(out_specs), list(out_shape), list(scratch_shapes)
    if hosted is None:
        return pl.pallas_call(body, name=name, grid=grid, in_specs=in_specs, out_specs=out_specs, out_shape=out_shape,
                              scratch_shapes=scratch_shapes, compiler_params=_cp(len(grid)))(*args)
    n_in, n_out, n_scr = len(in_specs), len(out_specs), len(scratch_shapes)
    h_in, h_out = len(hosted.inputs), len(hosted.out_shape)

    def carried(*refs):
        o0 = n_in + h_in
        s0 = o0 + n_out + h_out
        c_in, c_out, c_sem = refs[n_in:o0], refs[o0 + n_out:s0], refs[s0 + n_scr:]
        ids = [pl.program_id(a) for a in range(len(grid))]
        first = functools.reduce(jnp.logical_and, [i == 0 for i in ids])
        last = functools.reduce(jnp.logical_and, [i == g - 1 for i, g in zip(ids, grid)])

        @pl.when(first)
        def _():
            hosted.start(c_in, c_out, c_sem)

        body(*refs[:n_in], *refs[o0:o0 + n_out], *refs[s0:s0 + n_scr])

        @pl.when(last)
        def _():
            hosted.finish(c_in, c_out, c_sem)

    hbm = pl.BlockSpec(memory_space=pl.ANY)
    aliases = {n_in + k: n_out + k for k in range(h_in)} if hosted.in_place else {}
    res = pl.pallas_call(
        carried, name=name, grid=grid, in_specs=in_specs + [hbm] * h_in, out_specs=out_specs + [hbm] * h_out,
        out_shape=out_shape + list(hosted.out_shape), scratch_shapes=scratch_shapes + list(hosted.sems),
        input_output_aliases=aliases, compiler_params=_cp(len(grid)))(*args, *hosted.inputs)
    return res[:n_out], res[n_out:]


def mod_linear(x, shift, scale, w, out_dtype, name, tn=None, emit_u=False, w_rows=None):
    t, d = x.shape
    n = w.shape[1] if w_rows is None else w_rows
    tn = n if tn is None else tn
    tm = TOKEN_TILE
    tps = (t // shift.shape[0]) // tm

    def body(x_ref, sh_ref, sc_ref, w_ref, o_ref, *rest):
        u = (x_ref[...] * (1.0 + sc_ref[...]) + sh_ref[...]).astype(BF16)
        o_ref[...] = (_dot(u, w_ref[...]) if w_rows is None else _dot_nt(u, w_ref[...])).astype(out_dtype)
        if emit_u:
            @pl.when(pl.program_id(1) == 0)
            def _():
                rest[0][...] = u

    vec = pl.BlockSpec((None, 1, d), lambda i, j: (i // tps, 0, 0))
    out_shape = [_sds((t, n), out_dtype)]
    out_specs = [pl.BlockSpec((tm, tn), lambda i, j: (i, j))]
    if emit_u:
        out_shape.append(_sds((t, d), BF16))
        out_specs.append(pl.BlockSpec((tm, d), lambda i, j: (i, 0)))
    w_spec = pl.BlockSpec((d, tn), lambda i, j: (0, j)) if w_rows is None else pl.BlockSpec((tn, d), lambda i, j: (j, 0))
    res = pl.pallas_call(
        body, name=name, grid=(t // tm, n // tn),
        in_specs=[pl.BlockSpec((tm, d), lambda i, j: (i, 0)), vec, vec, w_spec],
        out_specs=out_specs, out_shape=out_shape, compiler_params=_cp(2),
    )(x, shift, scale, w)
    return res if emit_u else res[0]


def _rms(h, g):
    rstd = lax.rsqrt(jnp.mean(h * h, axis=-1, keepdims=True) + EPS)
    return h * rstd, rstd


def mla_mid_fwd(h, g_q, g_kv, w_uq, w_uk, w_uv, cos8, sin8, cos64, sin64s, swap64, rope_to_heads, dup64, name):
    t = h.shape[0]
    tm = TOKEN_TILE
    hq = MLA_HEADS * MLA_NOPE
    hr = MLA_HEADS * MLA_ROPE // 2

    def body(h_ref, gq_ref, gkv_ref, wuq_ref, wuk_ref, wuv_ref, c8_ref, s8_ref, c64_ref, s64_ref, sw_ref, p_ref, d_ref,
             q_ref, kn_ref, v_ref, kr_ref, cq_ref, ckv_ref):
        hh = h_ref[...]
        cq = (_rms(hh[:, :MLA_QR], None)[0] * gq_ref[...]).astype(BF16)
        ckv = (_rms(hh[:, MLA_QR:MLA_QR + MLA_KVR], None)[0] * gkv_ref[...]).astype(BF16)
        cq_ref[...] = cq
        ckv_ref[...] = ckv
        q = _dot(cq, wuq_ref[...])
        x1 = q[:, hq:hq + hr]
        x2 = q[:, hq + hr:]
        cs = c8_ref[...]
        sn = s8_ref[...]
        rot = jnp.concatenate([x1 * cs - x2 * sn, x2 * cs + x1 * sn], axis=1).astype(BF16)
        q_ref[...] = jnp.concatenate([q[:, :hq].astype(BF16), _dot(rot, p_ref[...]).astype(BF16)], axis=1)
        kn_ref[...] = _dot(ckv, wuk_ref[...]).astype(BF16)
        v_ref[...] = _dot(ckv, wuv_ref[...]).astype(BF16)
        kr = hh[:, MLA_QR + MLA_KVR:]
        kr = (kr * c64_ref[...] + _dot_f32(kr, sw_ref[...]) * s64_ref[...]).astype(BF16)
        kr_ref[...] = _dot(kr, d_ref[...]).astype(BF16)

    def rows(n):
        return pl.BlockSpec((tm, n), lambda i: (i, 0))

    def whole(a):
        return pl.BlockSpec(a.shape, lambda i: (0,) * a.ndim)

    nq = w_uq.shape[1]
    return pl.pallas_call(
        body, name=name, grid=(t // tm,),
        in_specs=[rows(h.shape[1]), whole(g_q), whole(g_kv), whole(w_uq), whole(w_uk), whole(w_uv),
                  rows(hr), rows(hr), rows(MLA_ROPE), rows(MLA_ROPE), whole(swap64), whole(rope_to_heads), whole(dup64)],
        out_specs=[rows(nq), rows(hq), rows(hq), rows(2 * MLA_ROPE), rows(MLA_QR), rows(MLA_KVR)],
        out_shape=[_sds((t, nq), BF16), _sds((t, hq), BF16), _sds((t, hq), BF16), _sds((t, 2 * MLA_ROPE), BF16),
                   _sds((t, MLA_QR), BF16), _sds((t, MLA_KVR), BF16)],
        compiler_params=_cp(1),
    )(h, g_q, g_kv, w_uq, w_uk, w_uv, cos8, sin8, cos64, sin64s, swap64, rope_to_heads, dup64)


def _pick_lane(tile, idx):
    lane = lax.broadcasted_iota(jnp.int32, tile.shape, 1)
    return jnp.sum(jnp.where(lane == idx, tile, 0.0), axis=1, keepdims=True)


def _pick_row(tile, idx):
    row = lax.broadcasted_iota(jnp.int32, tile.shape, 0)
    return jnp.sum(jnp.where(row == idx, tile, 0.0), axis=0, keepdims=True)


def _put_lane(tile, idx, col):
    lane = lax.broadcasted_iota(jnp.int32, tile.shape, 1)
    return jnp.where(lane == idx, col, tile)


def _put_row(tile, idx, row):
    r = lax.broadcasted_iota(jnp.int32, tile.shape, 0)
    return tile + jnp.where(r == idx, row, 0.0)


def _causal_softmax_blocks(i, tq, heads):
    def block(j, carry, masked):
        new = []
        for (score_fn, pv_fn, _), (m, l, acc) in zip(heads, carry):
            sc = score_fn(j)
            if masked:
                keep = lax.broadcasted_iota(jnp.int32, sc.shape, 0) >= lax.broadcasted_iota(jnp.int32, sc.shape, 1)
                sc = jnp.where(keep, sc, -1e30)
            m_new = jnp.maximum(m, jnp.max(sc, axis=1, keepdims=True))
            a = jnp.exp(m - m_new)
            p = jnp.exp(sc - m_new)
            new.append((m_new, a * l + jnp.sum(p, axis=1, keepdims=True), a * acc + pv_fn(j, p.astype(BF16))))
        return tuple(new)

    init = tuple((jnp.full((tq, 1), -1e30, F32), jnp.zeros((tq, 1), F32), jnp.zeros((tq, dv), F32)) for _, _, dv in heads)
    carry = lax.fori_loop(0, i, lambda j, c: block(j, c, False), init)
    return [(acc / l, m + jnp.log(l)) for m, l, acc in block(i, carry, True)]


def fox_attn_fwd(qkv, cum, cum_rows, nb, name, hosted=None):
    t = qkv.shape[0]
    s = t // nb
    tq = ATTN_TILE
    nq = s // tq
    wide = FOX_GROUP * FOX_HD
    ngroups = FOX_HEADS // FOX_GROUP
    scale = FOX_HD ** -0.5

    def body(q_ref, k_ref, v_ref, cum_ref, cr_ref, o_ref, lse_ref):
        i = pl.program_id(1)
        hg = pl.program_id(2)

        @pl.when(hg == 0)
        def _():
            lse_ref[...] = jnp.zeros_like(lse_ref)

        low = lax.broadcasted_iota(jnp.int32, (tq, 128), 1) < FOX_HD
        cum_t = cum_ref[...]

        def rows_of(j):
            return pl.ds(pl.multiple_of(j * tq, tq), tq)

        def head(a):
            hd = FOX_GROUP * hg + a
            cols = slice(128 * (a // 2), 128 * (a // 2) + 128)
            q = q_ref[:, cols]
            qa = jnp.where(low if a % 2 == 0 else jnp.logical_not(low), q, jnp.zeros_like(q)) * scale
            fq = _pick_lane(cum_t, hd)
            return (lambda j: _dot_nt(qa, k_ref[rows_of(j), cols]) + fq - _pick_row(cr_ref[j], hd),
                    lambda j, p: _dot(p, v_ref[rows_of(j), cols]), 2 * FOX_HD)

        res = _causal_softmax_blocks(i, tq, [head(a) for a in range(FOX_GROUP)])
        o_ref[...] = jnp.concatenate([jnp.where(low, res[a][0], res[a + 1][0]) for a in range(0, FOX_GROUP, 2)],
                                     axis=1).astype(BF16)
        lse_t = lse_ref[...]
        for a in range(FOX_GROUP):
            lse_t = _put_lane(lse_t, FOX_GROUP * hg + a, res[a][1])
        lse_ref[...] = lse_t

    return _call(
        body, name, (nb, nq, ngroups),
        [pl.BlockSpec((tq, wide), lambda b, i, hg: (b * nq + i, hg)),
         pl.BlockSpec((s, wide), lambda b, i, hg: (b, ngroups + hg)),
         pl.BlockSpec((s, wide), lambda b, i, hg: (b, 2 * ngroups + hg)),
         pl.BlockSpec((tq, 128), lambda b, i, hg: (b * nq + i, 0)),
         pl.BlockSpec((nq, 16, tq), lambda b, i, hg: (b, 0, 0))],
        [pl.BlockSpec((tq, wide), lambda b, i, hg: (b * nq + i, hg)),
         pl.BlockSpec((tq, 128), lambda b, i, hg: (b * nq + i, 0))],
        [_sds((t, D_MODEL), BF16), _sds((t, 128), F32)], (qkv, qkv, qkv, cum, cum_rows), hosted=hosted)


def mla_attn_fwd(q, kn, kr2, v, nb, name, hosted=None):
    t = q.shape[0]
    s = t // nb
    tq = ATTN_TILE
    nq = s // tq
    ngroups = MLA_HEADS // MLA_GROUP
    wide = MLA_GROUP * MLA_NOPE
    rwide = MLA_GROUP * MLA_ROPE
    scale = (MLA_NOPE + MLA_ROPE) ** -0.5

    def body(qn_ref, qr_ref, kn_ref, kr_ref, v_ref, o_ref, lse_ref):
        i = pl.program_id(1)
        hg = pl.program_id(2)

        @pl.when(hg == 0)
        def _():
            lse_ref[...] = jnp.zeros_like(lse_ref)

        low = lax.broadcasted_iota(jnp.int32, (tq, 128), 1) < MLA_ROPE

        def rows_of(j):
            return pl.ds(pl.multiple_of(j * tq, tq), tq)

        def head(a):
            cols = slice(a * MLA_NOPE, (a + 1) * MLA_NOPE)
            qr = qr_ref[:, 128 * (a // 2):128 * (a // 2) + 128]
            q_cat = jnp.concatenate([qn_ref[:, cols], jnp.where(low if a % 2 == 0 else jnp.logical_not(low), qr,
                                                                jnp.zeros_like(qr))], axis=1)
            return (lambda j: _dot_nt(q_cat, jnp.concatenate([kn_ref[rows_of(j), cols], kr_ref[rows_of(j), :]], axis=1)) * scale,
                    lambda j, p: _dot(p, v_ref[rows_of(j), cols]), MLA_V)

        res = _causal_softmax_blocks(i, tq, [head(a) for a in range(MLA_GROUP)])
        o_ref[...] = jnp.concatenate([r[0] for r in res], axis=1).astype(BF16)
        lse_t = lse_ref[...]
        for a in range(MLA_GROUP):
            lse_t = _put_lane(lse_t, MLA_GROUP * hg + a, res[a][1])
        lse_ref[...] = lse_t

    rope0 = MLA_HEADS * MLA_NOPE // rwide
    return _call(
        body, name, (nb, nq, ngroups),
        [pl.BlockSpec((tq, wide), lambda b, i, hg: (b * nq + i, hg)),
         pl.BlockSpec((tq, rwide), lambda b, i, hg: (b * nq + i, rope0 + hg)),
         pl.BlockSpec((s, wide), lambda b, i, hg: (b, hg)),
         pl.BlockSpec((s, 128), lambda b, i, hg: (b, 0)),
         pl.BlockSpec((s, wide), lambda b, i, hg: (b, hg))],
        [pl.BlockSpec((tq, wide), lambda b, i, hg: (b * nq + i, hg)),
         pl.BlockSpec((tq, 128), lambda b, i, hg: (b * nq + i, 0))],
        [_sds((t, MLA_HEADS * MLA_V), BF16), _sds((t, 128), F32)], (q, q, kn, kr2, v), hosted=hosted)


def rows16(a, name):
    t = a.shape[0]
    tq = ATTN_TILE

    def body(a_ref, o_ref):
        o_ref[...] = a_ref[...].T[:16, :]

    return pl.pallas_call(
        body, name=name, grid=(t // tq,), in_specs=[pl.BlockSpec((tq, 128), lambda n: (n, 0))],
        out_specs=pl.BlockSpec((None, 16, tq), lambda n: (n, 0, 0)), out_shape=_sds((t // tq, 16, tq), F32),
        compiler_params=_cp(1),
    )(a)


def tokens128(rows, onehot, name):
    nblk, _, tq = rows.shape

    def body(r_ref, e_ref, o_ref):
        o_ref[...] = lax.dot_general(r_ref[...], e_ref[...], (((0,), (0,)), ((), ())), preferred_element_type=F32,
                                     precision=lax.Precision.HIGHEST)

    return pl.pallas_call(
        body, name=name, grid=(nblk,),
        in_specs=[pl.BlockSpec((None, 16, tq), lambda n: (n, 0, 0)), pl.BlockSpec((16, 128), lambda n: (0, 0))],
        out_specs=pl.BlockSpec((tq, 128), lambda n: (n, 0)), out_shape=_sds((nblk * tq, 128), F32),
        compiler_params=_cp(1),
    )(rows, onehot)


def _layer_norm(z, g, b):
    mu = jnp.mean(z, axis=-1, keepdims=True)
    zc = z - mu
    rstd = lax.rsqrt(jnp.mean(zc * zc, axis=-1, keepdims=True) + EPS)
    xhat = zc * rstd
    return xhat * g + b, xhat, rstd


def linear_resid_ln(a, w, x_in, gate, ln_g, ln_b, name):
    t, kdim = a.shape
    d = w.shape[1]
    tm = TOKEN_TILE
    tps = (t // gate.shape[0]) // tm

    def body(a_ref, w_ref, x_ref, gt_ref, g_ref, b_ref, y_ref, xo_ref):
        y = _dot(a_ref[...], w_ref[...])
        y_ref[...] = y
        z = ALPHA * x_ref[...] + (1.0 + gt_ref[...]) * y
        xo_ref[...] = _layer_norm(z, g_ref[...], b_ref[...])[0]

    rows = pl.BlockSpec((tm, d), lambda i: (i, 0))
    vec = pl.BlockSpec((1, d), lambda i: (0, 0))
    return pl.pallas_call(
        body, name=name, grid=(t // tm,),
        in_specs=[pl.BlockSpec((tm, kdim), lambda i: (i, 0)), pl.BlockSpec((kdim, d), lambda i: (0, 0)), rows,
                  pl.BlockSpec((None, 1, d), lambda i: (i // tps, 0, 0)), vec, vec],
        out_specs=[rows, rows], out_shape=[_sds((t, d), F32), _sds((t, d), F32)],
        compiler_params=_cp(1),
    )(a, w, x_in, gate, ln_g, ln_b)


def _resident(a):
    return pl.BlockSpec(a.shape, lambda *_: (0,) * a.ndim, pipeline_mode=pl.Buffered(1))


def ffn_fwd(x_in, shift, scale, gate, wg, wu, wd, ln_g, ln_b, name, hosted=None):
    t, d = x_in.shape
    c, _, fc = wg.shape
    tm = TOKEN_TILE
    tps = (t // gate.shape[0]) // tm

    def body(x_ref, sh_ref, sc_ref, gt_ref, wg_ref, wu_ref, wd_ref, g_ref, b_ref,
             u_ref, hg_ref, hu_ref, y_ref, xo_ref, acc_ref):
        cc = pl.program_id(1)

        @pl.when(cc == 0)
        def _():
            u_ref[...] = (x_ref[...] * (1.0 + sc_ref[...]) + sh_ref[...]).astype(BF16)
            acc_ref[...] = jnp.zeros_like(acc_ref)

        u = u_ref[...]
        hg = _dot(u, wg_ref[cc])
        hu = _dot(u, wu_ref[cc])
        hg_ref[...] = hg.astype(BF16)
        hu_ref[...] = hu.astype(BF16)
        act = (hg * jax.nn.sigmoid(hg) * hu).astype(BF16)
        acc_ref[...] += _dot(act, wd_ref[cc])

        @pl.when(cc == c - 1)
        def _():
            y = acc_ref[...]
            y_ref[...] = y
            z = ALPHA * x_ref[...] + (1.0 + gt_ref[...]) * y
            xo_ref[...] = _layer_norm(z, g_ref[...], b_ref[...])[0]

    rows = pl.BlockSpec((tm, d), lambda i, cc: (i, 0))
    bvec = pl.BlockSpec((None, 1, d), lambda i, cc: (i // tps, 0, 0))
    vec = pl.BlockSpec((1, d), lambda i, cc: (0, 0))
    hspec = pl.BlockSpec((None, tm, fc), lambda i, cc: (cc, i, 0))
    wcol = _resident(wg)
    return _call(
        body, name, (t // tm, c),
        [rows, bvec, bvec, bvec, wcol, wcol, _resident(wd), vec, vec],
        [rows, hspec, hspec, rows, rows],
        [_sds((t, d), BF16), _sds((c, t, fc), BF16), _sds((c, t, fc), BF16), _sds((t, d), F32), _sds((t, d), F32)],
        (x_in, shift, scale, gate, wg, wu, wd, ln_g, ln_b), scratch_shapes=[pltpu.VMEM((tm, d), F32)], hosted=hosted)


def fox_gate_fwd(hf, b_f, tri, n_batch, name):
    t, n = hf.shape
    blk = tri.shape[0]
    nb = (t // n_batch) // blk

    def body(hf_ref, b_ref, tri_ref, o_ref, carry_ref):
        @pl.when(pl.program_id(1) == 0)
        def _():
            carry_ref[...] = jnp.zeros_like(carry_ref)

        xx = hf_ref[...] + b_ref[...]
        lf = jnp.minimum(xx, 0.0) - jnp.log(1.0 + jnp.exp(-jnp.abs(xx)))
        cum = _dot_f32(tri_ref[...], lf) + carry_ref[...]
        o_ref[...] = cum
        carry_ref[...] = cum[blk - 1:blk, :]

    return pl.pallas_call(
        body, name=name, grid=(n_batch, nb),
        in_specs=[pl.BlockSpec((blk, n), lambda bb, i: (bb * nb + i, 0)), pl.BlockSpec((1, n), lambda bb, i: (0, 0)),
                  pl.BlockSpec((blk, blk), lambda bb, i: (0, 0))],
        out_specs=pl.BlockSpec((blk, n), lambda bb, i: (bb * nb + i, 0)),
        out_shape=_sds((t, n), F32), scratch_shapes=[pltpu.VMEM((1, n), F32)],
        compiler_params=_cp(2),
    )(hf, b_f, tri)


def ln_bwd(dxo, x_in, y, gate, ln_g, name, target=None):
    t, d = dxo.shape
    nb = gate.shape[0]
    tm = TOKEN_TILE
    tps = (t // nb) // tm
    with_loss = target is not None

    def body(dxo_ref, *refs):
        if with_loss:
            t_ref, x_ref, y_ref, gt_ref, g_ref, dz_ref, dy_ref, dg_ref, db_ref, dgt_ref, l_ref = refs
        else:
            x_ref, y_ref, gt_ref, g_ref, dz_ref, dy_ref, dg_ref, db_ref, dgt_ref = refs
        i = pl.program_id(0)

        @pl.when(i == 0)
        def _():
            dg_ref[...] = jnp.zeros_like(dg_ref)
            db_ref[...] = jnp.zeros_like(db_ref)
            if with_loss:
                l_ref[...] = jnp.zeros_like(l_ref)

        @pl.when(i % tps == 0)
        def _():
            dgt_ref[...] = jnp.zeros_like(dgt_ref)

        yy = y_ref[...]
        g1 = 1.0 + gt_ref[...]
        z = ALPHA * x_ref[...] + g1 * yy
        _, xhat, rstd = _layer_norm(z, 1.0, 0.0)
        dxo_v = dxo_ref[...]
        if with_loss:
            err = dxo_v - t_ref[...]
            l_ref[...] += jnp.sum(err * err, axis=0, keepdims=True)
            dxo_v = err / d
        dg_ref[...] += jnp.sum(dxo_v * xhat, axis=0, keepdims=True)
        db_ref[...] += jnp.sum(dxo_v, axis=0, keepdims=True)
        dxh = dxo_v * g_ref[...]
        dz = rstd * (dxh - jnp.mean(dxh, axis=-1, keepdims=True) - xhat * jnp.mean(dxh * xhat, axis=-1, keepdims=True))
        dz_ref[...] = dz
        dy_ref[...] = (g1 * dz).astype(BF16)
        dgt_ref[...] += jnp.sum(dz * yy, axis=0, keepdims=True)

    rows = pl.BlockSpec((tm, d), lambda i: (i, 0))
    vec = pl.BlockSpec((1, d), lambda i: (0, 0))
    bvec = pl.BlockSpec((None, 1, d), lambda i: (i // tps, 0, 0))
    return pl.pallas_call(
        body, name=name, grid=(t // tm,), in_specs=[rows] * (4 if with_loss else 3) + [bvec, vec],
        out_specs=[rows, rows, vec, vec, bvec] + ([vec] if with_loss else []),
        out_shape=[_sds((t, d), F32), _sds((t, d), BF16), _sds((1, d), F32), _sds((1, d), F32), _sds((nb, 1, d), F32)]
        + ([_sds((1, d), F32)] if with_loss else []),
        compiler_params=_cp(1),
    )(dxo, *([target] if with_loss else []), x_in, y, gate, ln_g)


def _mod_bwd_tail(du, dz_ref, x_ref, sc_ref, dx_ref, dsc_ref, dsh_ref, first):
    @pl.when(first)
    def _():
        dsc_ref[...] = jnp.zeros_like(dsc_ref)
        dsh_ref[...] = jnp.zeros_like(dsh_ref)

    dx_ref[...] = ALPHA * dz_ref[...] + du * (1.0 + sc_ref[...])
    dsc_ref[...] += jnp.sum(du * x_ref[...], axis=0, keepdims=True)
    dsh_ref[...] += jnp.sum(du, axis=0, keepdims=True)


def ffn_bwd(dy, hg, hu, wg, wu, wd, dz, x_in, scale, name, hosted=None):
    t, d = dy.shape
    c, _, fc = wg.shape
    nb = scale.shape[0]
    tm = TOKEN_TILE
    tps = (t // nb) // tm

    def body(dy_ref, hg_ref, hu_ref, wg_ref, wu_ref, wd_ref, dz_ref, x_ref, sc_ref,
             dhg_ref, dhu_ref, act_ref, dx_ref, dsc_ref, dsh_ref, acc_ref):
        i = pl.program_id(0)
        cc = pl.program_id(1)

        @pl.when(cc == 0)
        def _():
            acc_ref[...] = jnp.zeros_like(acc_ref)

        hgv = hg_ref[...].astype(F32)
        huv = hu_ref[...].astype(F32)
        da = _dot_nt(dy_ref[...], wd_ref[cc])
        sg = jax.nn.sigmoid(hgv)
        sl = hgv * sg
        act_ref[...] = (sl * huv).astype(BF16)
        dhu = (da * sl).astype(BF16)
        dhg = (da * huv * (sg * (1.0 + hgv * (1.0 - sg)))).astype(BF16)
        dhu_ref[...] = dhu
        dhg_ref[...] = dhg
        acc_ref[...] += _dot_nt(dhg, wg_ref[cc]) + _dot_nt(dhu, wu_ref[cc])

        @pl.when(cc == c - 1)
        def _():
            _mod_bwd_tail(acc_ref[...], dz_ref, x_ref, sc_ref, dx_ref, dsc_ref, dsh_ref, i % tps == 0)

    rows = pl.BlockSpec((tm, d), lambda i, cc: (i, 0))
    bvec = pl.BlockSpec((None, 1, d), lambda i, cc: (i // tps, 0, 0))
    hspec = pl.BlockSpec((None, tm, fc), lambda i, cc: (cc, i, 0))
    wcol = _resident(wg)
    return _call(
        body, name, (t // tm, c),
        [rows, hspec, hspec, wcol, wcol, _resident(wd), rows, rows, bvec],
        [hspec, hspec, hspec, rows, bvec, bvec],
        [_sds((c, t, fc), BF16), _sds((c, t, fc), BF16), _sds((c, t, fc), BF16), _sds((t, d), F32),
         _sds((nb, 1, d), F32), _sds((nb, 1, d), F32)],
        (dy, hg, hu, wg, wu, wd, dz, x_in, scale), scratch_shapes=[pltpu.VMEM((tm, d), F32)], hosted=hosted)


def linear_nt_mod_bwd(pairs, dz, x_in, scale, name, hosted=None):
    t, d = dz.shape
    nb = scale.shape[0]
    tm = TOKEN_TILE
    tps = (t // nb) // tm
    npairs = len(pairs)

    def body(*refs):
        dh_refs = refs[:npairs]
        w_refs = refs[npairs:2 * npairs]
        dz_ref, x_ref, sc_ref, dx_ref, dsc_ref, dsh_ref = refs[2 * npairs:]
        du = None
        for (_, _, blk), dh_ref, w_ref in zip(pairs, dh_refs, w_refs):
            dh = dh_ref[...].astype(BF16)
            term = _dot_nt(dh, w_ref[...]) if blk is None else _dot(dh, w_ref[...])
            du = term if du is None else du + term
        _mod_bwd_tail(du, dz_ref, x_ref, sc_ref, dx_ref, dsc_ref, dsh_ref, pl.program_id(0) % tps == 0)

    rows = pl.BlockSpec((tm, d), lambda i: (i, 0))
    bvec = pl.BlockSpec((None, 1, d), lambda i: (i // tps, 0, 0))
    in_specs = [pl.BlockSpec((tm, dh.shape[1]), lambda i: (i, 0)) for dh, _, _ in pairs]
    for dh, w, blk in pairs:
        if blk is None:
            in_specs.append(pl.BlockSpec(w.shape, lambda i: (0, 0)))
        else:
            in_specs.append(pl.BlockSpec((dh.shape[1], d), lambda i, blk=blk: (blk, 0)))
    in_specs += [rows, rows, bvec]
    return _call(
        body, name, (t // tm,), in_specs, [rows, bvec, bvec],
        [_sds((t, d), F32), _sds((nb, 1, d), F32), _sds((nb, 1, d), F32)],
        (*[dh for dh, _, _ in pairs], *[w for _, w, _ in pairs], dz, x_in, scale), hosted=hosted)


def linear_nt_delta(dy, w_o, o, head_sel, name):
    t, d = dy.shape
    hdv = w_o.shape[0]
    tm = TOKEN_TILE

    def body(dy_ref, w_ref, o_ref, sel_ref, do_ref, dl_ref):
        do = _dot_nt(dy_ref[...], w_ref[...])
        do_ref[...] = do.astype(BF16)
        dl_ref[...] = _dot_f32(do * o_ref[...].astype(F32), sel_ref[...])

    return pl.pallas_call(
        body, name=name, grid=(t // tm,),
        in_specs=[pl.BlockSpec((tm, d), lambda i: (i, 0)), pl.BlockSpec((hdv, d), lambda i: (0, 0)),
                  pl.BlockSpec((tm, hdv), lambda i: (i, 0)), pl.BlockSpec(head_sel.shape, lambda i: (0, 0))],
        out_specs=[pl.BlockSpec((tm, hdv), lambda i: (i, 0)), pl.BlockSpec((tm, 128), lambda i: (i, 0))],
        out_shape=[_sds((t, hdv), BF16), _sds((t, 128), F32)], compiler_params=_cp(1),
    )(dy, w_o, o, head_sel)


def _attn_bwd_blocks(j, nk, tk, scale, heads):
    def block(i, carry, masked):
        new = []
        for hd, (dk_acc, dv_acc, dfk_acc) in zip(heads, carry):
            qb = hd["q"](i)
            dob = hd["do"](i)
            lse_row, dl_row = hd["rows"](i)
            st = _dot_nt(hd["k"], qb)
            if scale is not None:
                st = st * scale
            if hd["bias"] is not None:
                fq_row, fk_col = hd["bias"](i)
                st = st + fq_row - fk_col
            if masked:
                keep = lax.broadcasted_iota(jnp.int32, st.shape, 1) >= lax.broadcasted_iota(jnp.int32, st.shape, 0)
                st = jnp.where(keep, st, -1e30)
            pt = jnp.exp(st - lse_row)
            dv_acc = dv_acc + _dot(pt.astype(BF16), dob)
            dst = pt * (_dot_nt(hd["v"], dob) - dl_row)
            if hd["add_dfq"] is not None:
                dfk_acc = dfk_acc - jnp.sum(dst, axis=1, keepdims=True)
                hd["add_dfq"](i, jnp.sum(dst, axis=0, keepdims=True))
            dsb = (dst if scale is None else dst * scale).astype(BF16)
            dk_acc = dk_acc + _dot(dsb, qb)
            hd["add_dq"](i, _dot_tn(dsb, hd["k"] if scale is not None else hd["k_scaled"]))
            new.append((dk_acc, dv_acc, dfk_acc))
        return tuple(new)

    init = tuple((jnp.zeros((tk, hd["k"].shape[1]), F32), jnp.zeros((tk, hd["v"].shape[1]), F32), jnp.zeros((tk, 1), F32))
                 for hd in heads)
    carry = block(j, init, True)
    return lax.fori_loop(j + 1, nk, lambda i, c: block(i, c, False), carry)


def fox_attn_bwd(qkv, do, cum, cum_rows, lse_rows, delta_rows, nb, name, hosted=None):
    t = qkv.shape[0]
    s = t // nb
    tk = ATTN_TILE
    nk = s // tk
    scale = FOX_HD ** -0.5

    def body(q_ref, k_ref, v_ref, do_ref, cum_ref, cr_ref, lr_ref, dr_ref, dq_ref, dk_ref, dv_ref, dfq_ref, dfk_ref):
        hg = pl.program_id(1)
        j = pl.program_id(2)

        @pl.when(j == 0)
        def _():
            dq_ref[...] = jnp.zeros_like(dq_ref)

        @pl.when((j == 0) & (hg == 0))
        def _():
            dfq_ref[...] = jnp.zeros_like(dfq_ref)
            dfk_ref[...] = jnp.zeros_like(dfk_ref)

        low = lax.broadcasted_iota(jnp.int32, (tk, 128), 1) < FOX_HD
        cum_t = cum_ref[...]

        def rows_of(i):
            return pl.ds(pl.multiple_of(i * tk, tk), tk)

        def head(a):
            hd = FOX_GROUP * hg + a
            cols = slice(128 * (a // 2), 128 * (a // 2) + 128)
            half = low if a % 2 == 0 else jnp.logical_not(low)
            kb = k_ref[:, cols]
            vb = v_ref[:, cols]
            fk = _pick_lane(cum_t, hd)

            def add_dq(i, val):
                dq_ref[rows_of(i), cols] += val

            def add_dfq(i, val):
                dfq_ref[i] = _put_row(dfq_ref[i], hd, val)

            ka = jnp.where(half, kb, jnp.zeros_like(kb))
            return dict(q=lambda i: q_ref[rows_of(i), cols] * scale, do=lambda i: do_ref[rows_of(i), cols],
                        k=ka, k_scaled=ka * scale, v=jnp.where(half, vb, jnp.zeros_like(vb)),
                        rows=lambda i: (_pick_row(lr_ref[i], hd), _pick_row(dr_ref[i], hd)),
                        bias=lambda i: (_pick_row(cr_ref[i], hd), fk), add_dq=add_dq, add_dfq=add_dfq)

        res = _attn_bwd_blocks(j, nk, tk, None, [head(a) for a in range(FOX_GROUP)])
        dk_ref[...] = jnp.concatenate([jnp.where(low, res[a][0], res[a + 1][0]) for a in range(0, FOX_GROUP, 2)],
                                      axis=1).astype(BF16)
        dv_ref[...] = jnp.concatenate([jnp.where(low, res[a][1], res[a + 1][1]) for a in range(0, FOX_GROUP, 2)],
                                      axis=1).astype(BF16)
        for a in range(FOX_GROUP):
            dfk_ref[j] = _put_row(dfk_ref[j], FOX_GROUP * hg + a, jnp.broadcast_to(res[a][2], (tk, 128)).T[0:1, :])

    wide = FOX_GROUP * FOX_HD
    ngroups = FOX_HEADS // FOX_GROUP
    rowsp = pl.BlockSpec((nk, 16, tk), lambda b, hg, j: (b, 0, 0))
    return _call(
        body, name, (nb, ngroups, nk),
        [pl.BlockSpec((s, wide), lambda b, hg, j: (b, hg)),
         pl.BlockSpec((tk, wide), lambda b, hg, j: (b * nk + j, ngroups + hg)),
         pl.BlockSpec((tk, wide), lambda b, hg, j: (b * nk + j, 2 * ngroups + hg)),
         pl.BlockSpec((s, wide), lambda b, hg, j: (b, hg)),
         pl.BlockSpec((tk, 128), lambda b, hg, j: (b * nk + j, 0)),
         rowsp, rowsp, rowsp],
        [pl.BlockSpec((s, wide), lambda b, hg, j: (b, hg)),
         pl.BlockSpec((tk, wide), lambda b, hg, j: (b * nk + j, hg)),
         pl.BlockSpec((tk, wide), lambda b, hg, j: (b * nk + j, hg)),
         rowsp, rowsp],
        [_sds((t, D_MODEL), F32), _sds((t, D_MODEL), BF16), _sds((t, D_MODEL), BF16),
         _sds((t // tk, 16, tk), F32), _sds((t // tk, 16, tk), F32)],
        (qkv, qkv, qkv, do, cum, cum_rows, lse_rows, delta_rows), hosted=hosted)


def mla_attn_bwd(q, kn, kr2, v, do, lse_rows, delta_rows, nb, name, hosted=None):
    t = q.shape[0]
    s = t // nb
    tk = ATTN_TILE
    nk = s // tk
    ngroups = MLA_HEADS // MLA_GROUP
    wide = MLA_GROUP * MLA_NOPE
    rwide = MLA_GROUP * MLA_ROPE
    scale = (MLA_NOPE + MLA_ROPE) ** -0.5

    def body(qn_ref, qr_ref, kn_ref, kr_ref, v_ref, do_ref, lr_ref, dr_ref, dqn_ref, dqr_ref, dkn_ref, dkr_ref, dv_ref):
        hg = pl.program_id(1)
        j = pl.program_id(2)

        @pl.when(j == 0)
        def _():
            dqn_ref[...] = jnp.zeros_like(dqn_ref)
            dqr_ref[...] = jnp.zeros_like(dqr_ref)

        low = lax.broadcasted_iota(jnp.int32, (tk, 128), 1) < MLA_ROPE
        kr = kr_ref[...]

        def rows_of(i):
            return pl.ds(pl.multiple_of(i * tk, tk), tk)

        def head(a):
            cols = slice(a * MLA_NOPE, (a + 1) * MLA_NOPE)
            rcols = slice(128 * (a // 2), 128 * (a // 2) + 128)
            mine = low if a % 2 == 0 else jnp.logical_not(low)
            hd = MLA_GROUP * hg + a

            def q_fn(i):
                qr = qr_ref[rows_of(i), rcols]
                return jnp.concatenate([qn_ref[rows_of(i), cols], jnp.where(mine, qr, jnp.zeros_like(qr))], axis=1)

            def add_dq(i, val):
                dqn_ref[rows_of(i), cols] += val[:, :MLA_NOPE]
                dqr_ref[rows_of(i), cols] += val[:, MLA_NOPE:]

            return dict(q=q_fn, do=lambda i: do_ref[rows_of(i), cols], k=jnp.concatenate([kn_ref[:, cols], kr], axis=1),
                        v=v_ref[:, cols], rows=lambda i: (_pick_row(lr_ref[i], hd), _pick_row(dr_ref[i], hd)),
                        bias=None, add_dq=add_dq, add_dfq=None)

        res = _attn_bwd_blocks(j, nk, tk, scale, [head(a) for a in range(MLA_GROUP)])
        dkn_ref[...] = jnp.concatenate([r[0][:, :MLA_NOPE] for r in res], axis=1).astype(BF16)
        dkr_ref[...] = jnp.concatenate([r[0][:, MLA_NOPE:] for r in res], axis=1).astype(BF16)
        dv_ref[...] = jnp.concatenate([r[1] for r in res], axis=1).astype(BF16)

    full = pl.BlockSpec((s, wide), lambda b, hg, j: (b, hg))
    blk = pl.BlockSpec((tk, wide), lambda b, hg, j: (b * nk + j, hg))
    rowsp = pl.BlockSpec((nk, 16, tk), lambda b, hg, j: (b, 0, 0))
    total = MLA_HEADS * MLA_V
    rope0 = MLA_HEADS * MLA_NOPE // rwide
    return _call(
        body, name, (nb, ngroups, nk),
        [full, pl.BlockSpec((s, rwide), lambda b, hg, j: (b, rope0 + hg)), blk,
         pl.BlockSpec((tk, 128), lambda b, hg, j: (b * nk + j, 0)), blk, full, rowsp, rowsp],
        [full, full, blk, blk, blk],
        [_sds((t, total), F32), _sds((t, total), F32), _sds((t, total), BF16), _sds((t, total), BF16),
         _sds((t, total), BF16)],
        (q, q, kn, kr2, v, do, lse_rows, delta_rows), hosted=hosted)


def mla_mid_bwd(dqn, dqr, dkn, dv, dkr_heads, h, g_q, g_kv, w_uq, w_uk, w_uv, cos8, sin8, cos64, sin64s, swap64,
                heads_to_rope, head_sum, name, hosted=None):
    t = h.shape[0]
    tm = TOKEN_TILE
    hq = MLA_HEADS * MLA_NOPE
    hr = MLA_HEADS * MLA_ROPE // 2
    nq = w_uq.shape[1]

    def body(dqn_ref, dqr_ref, dkn_ref, dv_ref, dkr_ref, h_ref, gq_ref, gkv_ref, wuq_ref, wuk_ref, wuv_ref,
             c8_ref, s8_ref, c64_ref, s64_ref, sw_ref, hp_ref, hs_ref, dh_ref, dqp_ref, dgq_ref, dgkv_ref):
        @pl.when(pl.program_id(0) == 0)
        def _():
            dgq_ref[...] = jnp.zeros_like(dgq_ref)
            dgkv_ref[...] = jnp.zeros_like(dgkv_ref)

        drot = _dot(dqr_ref[...].astype(BF16), hp_ref[...])
        o1 = drot[:, :hr]
        o2 = drot[:, hr:]
        cs = c8_ref[...]
        sn = s8_ref[...]
        dqp = jnp.concatenate([dqn_ref[...].astype(BF16), (o1 * cs + o2 * sn).astype(BF16),
                               (o2 * cs - o1 * sn).astype(BF16)], axis=1)
        dqp_ref[...] = dqp
        dcq = _dot_nt(dqp, wuq_ref[...])
        dckv = _dot_nt(dkn_ref[...], wuk_ref[...]) + _dot_nt(dv_ref[...], wuv_ref[...])
        hh = h_ref[...]

        def rms_bwd(hpart, g, dc, dg_ref):
            hhat, rstd = _rms(hpart, None)
            dg_ref[...] += jnp.sum(dc * hhat, axis=0, keepdims=True)
            dcg = dc * g
            return rstd * (dcg - hhat * jnp.mean(dcg * hhat, axis=-1, keepdims=True))

        dhq = rms_bwd(hh[:, :MLA_QR], gq_ref[...], dcq, dgq_ref)
        dhkv = rms_bwd(hh[:, MLA_QR:MLA_QR + MLA_KVR], gkv_ref[...], dckv, dgkv_ref)
        dkr = _dot(dkr_ref[...], hs_ref[...])
        dkr_pre = dkr * c64_ref[...] + _dot_f32(dkr * s64_ref[...], sw_ref[...])
        dh_ref[...] = jnp.concatenate([dhq, dhkv, dkr_pre], axis=1).astype(BF16)

    def rows(n):
        return pl.BlockSpec((tm, n), lambda i: (i, 0))

    def whole(a):
        return pl.BlockSpec(a.shape, lambda i: (0,) * a.ndim)

    return _call(
        body, name, (t // tm,),
        [rows(hq), rows(hq), rows(hq), rows(hq), rows(hq), rows(h.shape[1]), whole(g_q), whole(g_kv),
         whole(w_uq), whole(w_uk), whole(w_uv), rows(hr), rows(hr), rows(MLA_ROPE), rows(MLA_ROPE),
         whole(swap64), whole(heads_to_rope), whole(head_sum)],
        [rows(h.shape[1]), rows(nq), pl.BlockSpec((1, MLA_QR), lambda i: (0, 0)),
         pl.BlockSpec((1, MLA_KVR), lambda i: (0, 0))],
        [_sds((t, h.shape[1]), BF16), _sds((t, nq), BF16), _sds((1, MLA_QR), F32), _sds((1, MLA_KVR), F32)],
        (dqn, dqr, dkn, dv, dkr_heads, h, g_q, g_kv, w_uq, w_uk, w_uv, cos8, sin8, cos64, sin64s, swap64,
         heads_to_rope, head_sum), hosted=hosted)


def fox_gate_bwd(dcum, hf, b_f, triu, n_batch, name):
    t, n = hf.shape
    blk = triu.shape[0]
    nb = (t // n_batch) // blk

    def body(dc_ref, hf_ref, b_ref, tri_ref, o_ref, db_ref, carry_ref):
        @pl.when(pl.program_id(1) == 0)
        def _():
            carry_ref[...] = jnp.zeros_like(carry_ref)

        @pl.when((pl.program_id(0) == 0) & (pl.program_id(1) == 0))
        def _():
            db_ref[...] = jnp.zeros_like(db_ref)

        rc = _dot_f32(tri_ref[...], dc_ref[...]) + carry_ref[...]
        carry_ref[...] = rc[0:1, :]
        dhf = rc * jax.nn.sigmoid(-(hf_ref[...] + b_ref[...]))
        o_ref[...] = dhf.astype(BF16)
        db_ref[...] += jnp.sum(dhf, axis=0, keepdims=True)

    rev = pl.BlockSpec((blk, n), lambda bb, i: (bb * nb + nb - 1 - i, 0))
    return pl.pallas_call(
        body, name=name, grid=(n_batch, nb),
        in_specs=[rev, rev, pl.BlockSpec((1, n), lambda bb, i: (0, 0)), pl.BlockSpec((blk, blk), lambda bb, i: (0, 0))],
        out_specs=[rev, pl.BlockSpec((1, n), lambda bb, i: (0, 0))],
        out_shape=[_sds((t, n), BF16), _sds((1, n), F32)], scratch_shapes=[pltpu.VMEM((1, n), F32)],
        compiler_params=_cp(2),
    )(dcum, hf, b_f, triu)


def wgrad(a, bm, name, with_bf16=False, bt=WGRAD_TOKENS):
    ca, t, kd = a.shape
    cb, _, nd = bm.shape
    c = max(ca, cb)
    bn = nd
    if nd > 1024 and nd % 1024 == 0:
        bn = 1024
    nsteps = t // bt

    def body(a_ref, b_ref, o_ref, *rest):
        @pl.when(pl.program_id(2) == 0)
        def _():
            o_ref[...] = jnp.zeros_like(o_ref)

        o_ref[...] += _dot_tn(a_ref[...].astype(BF16), b_ref[...].astype(BF16))
        if with_bf16:
            @pl.when(pl.program_id(2) == nsteps - 1)
            def _():
                rest[0][...] = o_ref[...].astype(BF16)

    out_spec = pl.BlockSpec((None, kd, bn), lambda cc, n, tt: (cc, 0, n))
    res = pl.pallas_call(
        body, name=name, grid=(c, nd // bn, nsteps),
        in_specs=[pl.BlockSpec((None, bt, kd), lambda cc, n, tt: (cc if ca > 1 else 0, tt, 0)),
                  pl.BlockSpec((None, bt, bn), lambda cc, n, tt: (cc if cb > 1 else 0, tt, n))],
        out_specs=[out_spec, out_spec] if with_bf16 else out_spec,
        out_shape=[_sds((c, kd, nd), F32), _sds((c, kd, nd), BF16)] if with_bf16 else _sds((c, kd, nd), F32),
        compiler_params=_cp(3),
    )(a, bm)
    return res


def ada_mod_part(c_all, ada_w, name):
    nl, d, n = ada_w.shape
    rows = c_all.shape[0]
    tn = 512

    def body(c_ref, w_ref, o_ref):
        cv = c_ref[...]
        act = (cv * jax.nn.sigmoid(cv)).astype(BF16)
        o_ref[...] = _dot(act, w_ref[...].astype(BF16))

    return pl.pallas_call(
        body, name=name, grid=(nl, n // tn),
        in_specs=[pl.BlockSpec((rows, d), lambda l, j: (0, 0)), pl.BlockSpec((None, d, tn), lambda l, j: (l, 0, j))],
        out_specs=pl.BlockSpec((None, rows, tn), lambda l, j: (l, 0, j)),
        out_shape=_sds((nl, rows, n), F32), compiler_params=_cp(2),
    )(c_all, ada_w)


def ada_grad(c_all_t, dmod, name):
    nl, rows, n = dmod.shape
    d = c_all_t.shape[0]
    tn = 512

    def body(c_ref, dm_ref, o_ref):
        cv = c_ref[...]
        act = (cv * jax.nn.sigmoid(cv)).astype(BF16)
        o_ref[...] = _dot(act, dm_ref[...].astype(BF16))

    return pl.pallas_call(
        body, name=name, grid=(nl, n // tn),
        in_specs=[pl.BlockSpec((d, rows), lambda l, j: (0, 0)), pl.BlockSpec((None, rows, tn), lambda l, j: (l, 0, j))],
        out_specs=pl.BlockSpec((None, d, tn), lambda l, j: (l, 0, j)),
        out_shape=_sds((nl, d, n), F32), compiler_params=_cp(2),
    )(c_all_t, dmod)


def sum_leading(a, name):
    g, r, n = a.shape

    def body(a_ref, o_ref):
        acc = a_ref[0]
        for kk in range(1, g):
            acc = acc + a_ref[kk]
        o_ref[...] = acc

    return pl.pallas_call(
        body, name=name, grid=(1,), in_specs=[pl.BlockSpec((g, r, n), lambda i: (0, 0, 0))],
        out_specs=pl.BlockSpec((r, n), lambda i: (0, 0)), out_shape=_sds((r, n), F32), compiler_params=_cp(1),
    )(a)


def adamw(w, g, m, v, name):
    r, n = w.shape
    br = r
    for cand in (512, 256, 128, 64, 32, 16, 8):
        if r % cand == 0 and r > cand and cand * n * 4 <= ADAMW_BLOCK_BYTES:
            br = cand
            break
    c1 = 1.0 - ADAM_B1 ** ADAM_STEP
    c2 = 1.0 - ADAM_B2 ** ADAM_STEP

    def body(w_ref, g_ref, m_ref, v_ref, d_ref, mo_ref, vo_ref):
        gv = g_ref[...]
        mn = ADAM_B1 * m_ref[...] + (1.0 - ADAM_B1) * gv
        vn = ADAM_B2 * v_ref[...] + (1.0 - ADAM_B2) * (gv * gv)
        mo_ref[...] = mn
        vo_ref[...] = vn
        d_ref[...] = -ADAM_LR * ((mn / c1) / (jnp.sqrt(vn / c2) + ADAM_EPS) + ADAM_WD * w_ref[...])

    spec = pl.BlockSpec((br, n), lambda i: (i, 0))
    return _call(body, name, (r // br,), [spec] * 4, [spec] * 3, [_sds((r, n), F32)] * 3, (w, g, m, v))


def all_gather8(x_blk, name, hosted=None):
    m_per, n = x_blk.shape
    h_in = 0 if hosted is None else len(hosted.inputs)
    h_out = 0 if hosted is None else len(hosted.out_shape)

    def body(x_ref, *refs):
        c_in, (out_ref, *c_out), (send_sems, recv_sems, local_sem, *c_sem) = (
            refs[:h_in], refs[h_in:h_in + 1 + h_out], refs[h_in + 1 + h_out:])
        if hosted is not None:
            hosted.start(c_in, c_out, c_sem)
        gather(x_ref, out_ref, send_sems, recv_sems, local_sem)
        if hosted is not None:
            hosted.finish(c_in, c_out, c_sem)

    def gather(x_ref, out_ref, send_sems, recv_sems, local_sem):
        x, y, c = _place()
        me, sibling = (x, y, c), (x, y, 1 - c)
        chips = [(1 - x, y), (x, 1 - y), (1 - x, 1 - y)]

        def rows(px, py, pc):
            return out_ref.at[pl.ds((4 * px + 2 * py + pc) * m_per, m_per), :]

        def copy(k, block, to, src=None):
            return pltpu.make_async_remote_copy(
                src_ref=rows(*block) if src is None else src, dst_ref=rows(*block),
                send_sem=send_sems.at[k], recv_sem=recv_sems.at[k], device_id=to, device_id_type=MESH)

        mine = pltpu.make_async_copy(x_ref, rows(*me), local_sem)
        mine.start()
        first = [copy(0, me, sibling, src=x_ref)]
        first += [copy(1 + j, me, (*chip, c), src=x_ref) for j, chip in enumerate(chips)]
        for cp in first:
            cp.start()
        passed = [copy(4 + j, (*chip, c), sibling) for j, chip in enumerate(chips)]
        for j, chip in enumerate(chips):
            copy(1 + j, (*chip, c), me).wait_recv()
            passed[j].start()
        copy(0, sibling, me).wait_recv()
        for j, chip in enumerate(chips):
            copy(4 + j, (*chip, 1 - c), me).wait_recv()
        for cp in first + passed:
            cp.wait_send()
        mine.wait()

    hbm = pl.BlockSpec(memory_space=pl.ANY)
    vmem = pl.BlockSpec(memory_space=pltpu.VMEM)
    res = pl.pallas_call(
        body, name=name,
        out_shape=[_sds((8 * m_per, n), x_blk.dtype)] + ([] if hosted is None else list(hosted.out_shape)),
        in_specs=[vmem] + [hbm] * h_in, out_specs=[vmem] + [hbm] * h_out,
        scratch_shapes=[pltpu.SemaphoreType.DMA((7,)), pltpu.SemaphoreType.DMA((7,)), pltpu.SemaphoreType.DMA]
        + ([] if hosted is None else list(hosted.sems)),
        compiler_params=pltpu.CompilerParams(vmem_limit_bytes=VMEM_LIMIT),
    )(x_blk, *([] if hosted is None else hosted.inputs))
    return res[0] if hosted is None else (res[0], res[1:])


def _gather_comm(shards):
    nt = len(shards)

    def parts(w_refs, out_refs, sems, finishing):
        send_sems, recv_sems, own_send, own_recv = sems
        x, y, c = _place()
        sibling = (x, y, 1 - c)
        chips = [(1 - x, y), (x, 1 - y), (1 - x, 1 - y)]

        def copy(t, k, block, to, src=None):
            px, py, hh = block
            dst = out_refs[t].at[2 * px + py, hh]
            return pltpu.make_async_remote_copy(
                src_ref=dst if src is None else src, dst_ref=dst,
                send_sem=send_sems.at[6 * t + k], recv_sem=recv_sems.at[6 * t + k], device_id=to, device_id_type=MESH)

        own = [pltpu.make_async_remote_copy(
            src_ref=w_refs[t], dst_ref=out_refs[t].at[2 * x + y], send_sem=own_send.at[t], recv_sem=own_recv.at[t],
            device_id=sibling, device_id_type=MESH) for t in range(nt)]
        first = [copy(t, j, (x, y, c), (*chip, c), src=w_refs[t].at[c]) for t in range(nt) for j, chip in enumerate(chips)]
        if not finishing:
            return own, first
        landed = [copy(t, j, (*chip, c), (x, y, c)) for t in range(nt) for j, chip in enumerate(chips)]
        passed = [copy(t, 3 + j, (*chip, c), sibling) for t in range(nt) for j, chip in enumerate(chips)]
        from_sibling = [copy(t, 3 + j, (*chip, 1 - c), (x, y, c)) for t in range(nt) for j, chip in enumerate(chips)]
        return own, first, landed, passed, from_sibling

    def start(w_refs, out_refs, sems):
        own, first = parts(w_refs, out_refs, sems, False)
        for cp in own + first:
            cp.start()

    def finish(w_refs, out_refs, sems):
        own, first, landed, passed, from_sibling = parts(w_refs, out_refs, sems, True)
        for arrived, fwd in zip(landed, passed):
            arrived.wait_recv()
            fwd.start()
        for cp in from_sibling:
            cp.wait_recv()
        for cp in first + passed:
            cp.wait_send()
        for cp in own:
            cp.wait()

    sems = [pltpu.SemaphoreType.DMA((6 * nt,)), pltpu.SemaphoreType.DMA((6 * nt,)),
            pltpu.SemaphoreType.DMA((nt,)), pltpu.SemaphoreType.DMA((nt,))]
    return _Hosted(list(shards), [_sds((N_CHIPS, *w.shape), w.dtype) for w in shards], sems, start, finish)


def _row_block(r, n, itemsize):
    best = None
    for br in range(16, r + 1, 16):
        if r % br == 0 and br * n * itemsize <= COMM_BLOCK_BYTES:
            best = br
    return r if best is None else best


def _scatter_comm(parts):
    nt = len(parts)

    def copies(p_refs, b_refs, sems, arriving):
        send_sems, recv_sems = sems
        x, y, c = _place()
        me = 4 * x + 2 * y + c
        cps = []
        for t in range(nt):
            for r in range(1, 8):
                tx = 1 - x if r & 4 else x
                ty = 1 - y if r & 2 else y
                tc = 1 - c if r & 1 else c
                src, dst = (2 * x + y, c), 4 * tx + 2 * ty + tc
                if not arriving:
                    src, dst = (2 * tx + ty, tc), me
                cps.append(pltpu.make_async_remote_copy(
                    src_ref=p_refs[t].at[src], dst_ref=b_refs[t].at[dst], send_sem=send_sems.at[7 * t + r - 1],
                    recv_sem=recv_sems.at[7 * t + r - 1], device_id=(tx, ty, tc), device_id_type=MESH))
        return cps

    def start(p_refs, b_refs, sems):
        for cp in copies(p_refs, b_refs, sems, False):
            cp.start()

    def finish(p_refs, b_refs, sems):
        for cp in copies(p_refs, b_refs, sems, True):
            cp.wait_recv()
        for cp in copies(p_refs, b_refs, sems, False):
            cp.wait_send()

    sems = [pltpu.SemaphoreType.DMA((7 * nt,)), pltpu.SemaphoreType.DMA((7 * nt,))]
    return _Hosted(list(parts), [_sds((2 * N_CHIPS, *p.shape[2:]), p.dtype) for p in parts], sems, start, finish)


def sum_devices(own, recv, place, name, slot=(0, 1, None)):
    _, _, r, n = own.shape
    layer, n_layers, buf = slot
    br = _row_block(r, n, 4 * 8)

    def body(p_ref, o_ref, *rest):
        acc = o_ref[...]
        for kk in range(7):
            acc = acc + rest[kk][...].astype(F32)
        rest[-1][...] = acc

    def arrived(rel):
        return pl.BlockSpec((None, br, n), lambda i, pref: (jnp.bitwise_xor(pref[0], rel), i, 0))

    in_specs = [pl.BlockSpec((None, None, br, n), lambda i, pref: (pref[2], pref[1], i, 0))]
    in_specs += [arrived(rel) for rel in range(1, 8)]
    args = [own] + [recv] * 7
    aliases = {}
    if buf is not None:
        in_specs.append(pl.BlockSpec(memory_space=pl.ANY))
        args.append(buf)
        aliases = {9: 0}
    return pl.pallas_call(
        body, name=name,
        grid_spec=pltpu.PrefetchScalarGridSpec(
            num_scalar_prefetch=1, grid=(r // br,), in_specs=in_specs,
            out_specs=pl.BlockSpec((None, None, br, n), lambda i, pref: (layer, pref[1], i, 0))),
        out_shape=_sds((n_layers, 2, r, n), F32), input_output_aliases=aliases, compiler_params=_cp(1),
    )(place, *args)


def _join_comm(bufs):
    nt = len(bufs)
    layers = [bf.shape[0] for bf in bufs]
    first = [sum(layers[:t]) for t in range(nt)]

    def copies(o_refs, sems, own):
        send_sems, recv_sems = sems
        x, y, c = _place()
        hh = c if own else 1 - c
        return [pltpu.make_async_remote_copy(
            src_ref=o_refs[t].at[l, hh], dst_ref=o_refs[t].at[l, hh], send_sem=send_sems.at[first[t] + l],
            recv_sem=recv_sems.at[first[t] + l], device_id=(x, y, 1 - c), device_id_type=MESH)
            for t in range(nt) for l in range(layers[t])]

    def start(_, o_refs, sems):
        for cp in copies(o_refs, sems, True):
            cp.start()

    def finish(_, o_refs, sems):
        for cp in copies(o_refs, sems, False):
            cp.wait_recv()
        for cp in copies(o_refs, sems, True):
            cp.wait_send()

    sems = [pltpu.SemaphoreType.DMA((sum(layers),)), pltpu.SemaphoreType.DMA((sum(layers),))]
    return _Hosted(list(bufs), [_sds(bf.shape, bf.dtype) for bf in bufs], sems, start, finish, in_place=True)


def sibling_join_halves(bufs, name):
    comm = _join_comm(bufs)
    nt = len(bufs)

    def body(*refs):
        comm.start(refs[:nt], refs[nt:2 * nt], refs[2 * nt:])
        comm.finish(refs[:nt], refs[nt:2 * nt], refs[2 * nt:])

    hbm = pl.BlockSpec(memory_space=pl.ANY)
    return pl.pallas_call(body, name=name, out_shape=comm.out_shape, in_specs=[hbm] * nt, out_specs=[hbm] * nt,
                          input_output_aliases={k: k for k in range(nt)}, scratch_shapes=comm.sems)(*bufs)


_SHARD_KIND = {"mla_w_in": "rows", "mla_w_uq": "cols", "mla_w_uk": "cols", "mla_w_uv": "cols", "mla_w_o": "rows",
               "fox_w_in": "cols", "fox_w_o": "rows", "ffn_w_gate": "chunk", "ffn_w_up": "chunk", "ffn_w_down": "chunk"}
_PACKED = tuple(_SHARD_KIND)
_TRANSPOSED = ("ffn_w_gate", "ffn_w_up", "fox_w_in")


def _halves(shard):
    if shard.ndim == 3 and shard.shape[0] == 2:
        return shard
    r, n = shard.shape[-2:]
    return shard.reshape(2, r // 2, n)


def _cols_to_full(g):
    return jnp.transpose(g, (1, 0, 2)).reshape(g.shape[1], -1)


def _full_to_cols(w):
    k, n4 = w.shape
    return jnp.transpose(w.reshape(k, N_CHIPS, n4 // N_CHIPS), (1, 0, 2))


def _uq_perm():
    per = MLA_NOPE + MLA_ROPE
    half = MLA_ROPE // 2
    nope = [h * per + d for h in range(MLA_HEADS) for d in range(MLA_NOPE)]
    r1 = [h * per + MLA_NOPE + r for h in range(MLA_HEADS) for r in range(half)]
    r2 = [h * per + MLA_NOPE + half + r for h in range(MLA_HEADS) for r in range(half)]
    perm = np.array(nope + r1 + r2, dtype=np.int32)
    return perm, np.argsort(perm).astype(np.int32)


def _rope_matrices():
    half = MLA_ROPE // 2
    nr = MLA_HEADS * MLA_ROPE
    to_heads = np.zeros((nr, nr), np.float32)
    from_heads = np.zeros((MLA_HEADS * 128, nr), np.float32)
    for e in range(2):
        for h in range(MLA_HEADS):
            for r in range(half):
                to_heads[e * MLA_HEADS * half + h * half + r, h * MLA_ROPE + e * half + r] = 1.0
                from_heads[h * 128 + e * half + r, e * MLA_HEADS * half + h * half + r] = 1.0
    head_sum = np.tile(np.eye(MLA_ROPE, dtype=np.float32), (2 * MLA_HEADS, 1))
    dup = np.concatenate([np.eye(MLA_ROPE, dtype=np.float32)] * 2, axis=1)
    return to_heads, from_heads, head_sum, dup


def _ffn_weights(gathered):
    return tuple(g.reshape(N_CHIPS, 2 * g.shape[2], g.shape[3]) for g in gathered)


def _fox_weights(gathered):
    w_in, w_o = gathered
    w_in = jnp.transpose(w_in, (0, 2, 1, 3)).reshape(N_CHIPS * w_in.shape[2], 2 * w_in.shape[3])
    return w_in, w_o.reshape(-1, w_o.shape[-1])


def _local_step(x, positions, target, mods, wts, ln_g, ln_b, mla_g_q, mla_g_kv, fox_b_f, shards=None):
    nb, s, d = x.shape
    t = nb * s
    x0 = x.reshape(t, d)
    tgt = target.reshape(t, d)
    perm, inv_perm = _uq_perm()

    half = MLA_ROPE // 2
    inv_freq = ROPE_THETA ** (-jnp.arange(half, dtype=F32) / half)
    ang = positions.astype(F32).reshape(t, 1) * inv_freq
    cos, sin = jnp.cos(ang), jnp.sin(ang)
    cos8, sin8 = jnp.tile(cos, (1, MLA_HEADS)), jnp.tile(sin, (1, MLA_HEADS))
    cos64 = jnp.concatenate([cos, cos], axis=1)
    sin64s = jnp.concatenate([-sin, sin], axis=1)
    swap64 = jnp.asarray(np.roll(np.eye(MLA_ROPE, dtype=np.float32), half, axis=1))
    to_heads, from_heads, head_sum, dup = _rope_matrices()
    to_heads, from_heads = jnp.asarray(to_heads, dtype=BF16), jnp.asarray(from_heads, dtype=BF16)
    head_sum, dup = jnp.asarray(head_sum, dtype=BF16), jnp.asarray(dup, dtype=BF16)
    sel_mla = jnp.asarray(np.pad(np.kron(np.eye(MLA_HEADS, dtype=np.float32), np.ones((MLA_V, 1), np.float32)),
                                 ((0, 0), (0, 128 - MLA_HEADS))))
    sel_fox = jnp.asarray(np.pad(np.kron(np.eye(FOX_HEADS, dtype=np.float32), np.ones((FOX_HD, 1), np.float32)),
                                 ((0, 0), (0, 128 - FOX_HEADS))))
    tri = jnp.asarray(np.tril(np.ones((128, 128), np.float32)))
    triu = jnp.asarray(np.triu(np.ones((128, 128), np.float32)))
    onehot16 = jnp.asarray(np.eye(16, 128, dtype=np.float32))

    def vec(a):
        return a.reshape(1, -1)

    def carried(key):
        return None if shards is None else _gather_comm(shards[key])

    def split(res):
        return (res, None) if shards is None else res

    w_uq_p = wts["mla_w_uq"][:, perm]
    b_f_pad = jnp.pad(fox_b_f.reshape(1, -1), ((0, 0), (0, 128 - FOX_HEADS)))

    sh_a, sc_a, gt_a, sh_f, sc_f, gt_f = mods[0]
    h_in, u_m = mod_linear(x0, sh_a, sc_a, wts["mla_w_in"], F32, "mla_in", emit_u=True)
    q_m, kn_m, v_m, kr2_m, cq_m, ckv_m = mla_mid_fwd(
        h_in, vec(mla_g_q), vec(mla_g_kv), w_uq_p, wts["mla_w_uk"], wts["mla_w_uv"], cos8, sin8, cos64, sin64s, swap64,
        to_heads, dup, "mla_mid")
    (o_m, lse_m), got = split(mla_attn_fwd(q_m, kn_m, kr2_m, v_m, nb, "mla_attn", hosted=carried("ffn0")))
    ffn0_w = wts["ffn"][0] if got is None else _ffn_weights(got)
    y0, x1 = linear_resid_ln(o_m, wts["mla_w_o"], x0, gt_a, vec(ln_g[0, 0]), vec(ln_b[0, 0]), "mla_out")
    (u_f0, hg0, hu0, y1, x2), got = split(ffn_fwd(x1, sh_f, sc_f, gt_f, *ffn0_w, vec(ln_g[0, 1]), vec(ln_b[0, 1]), "ffn0",
                                                  hosted=carried("fox")))
    fox_w_in_t, fox_w_o = (wts["fox_w_in"].T, wts["fox_w_o"]) if got is None else _fox_weights(got)
    fox_w_f_t = jnp.pad(fox_w_in_t[3 * d:], ((0, 128 - FOX_HEADS), (0, 0)))
    sh_a1, sc_a1, gt_a1, sh_f1, sc_f1, gt_f1 = mods[1]
    qkv, u_x = mod_linear(x2, sh_a1, sc_a1, fox_w_in_t, BF16, "fox_qkv", tn=1024, emit_u=True, w_rows=3 * d)
    hf = mod_linear(x2, sh_a1, sc_a1, fox_w_f_t, F32, "fox_f", w_rows=128)
    cum = fox_gate_fwd(hf, b_f_pad, tri, nb, "fox_gate")
    cum_rows = rows16(cum, "fox_cum_rows")
    (o_x, lse_x), got = split(fox_attn_fwd(qkv, cum, cum_rows, nb, "fox_attn", hosted=carried("ffn1")))
    ffn1_w = wts["ffn"][1] if got is None else _ffn_weights(got)
    y2, x3 = linear_resid_ln(o_x, fox_w_o, x2, gt_a1, vec(ln_g[1, 0]), vec(ln_b[1, 0]), "fox_out")
    u_f1, hg1, hu1, y3, x4 = ffn_fwd(x3, sh_f1, sc_f1, gt_f1, *ffn1_w, vec(ln_g[1, 1]), vec(ln_b[1, 1]), "ffn1")

    parts, recv = {}, {}

    def halves_of(g):
        return g.reshape(N_CHIPS, 2, g.shape[1] // 2, g.shape[2])

    def scatter(keys, sent):
        return None if shards is None else _scatter_comm([sent[k] for k in keys])

    def landed(keys, got):
        if got is not None:
            recv.update(zip(keys, got))

    def ffn_grads(layer, u, dhg, dhu, act, dy):
        sent = {}
        for n, (a_op, b_op) in (("ffn_w_gate", (dhg, u[None])), ("ffn_w_up", (dhu, u[None])), ("ffn_w_down", (act, dy[None]))):
            g32, g16 = wgrad(a_op, b_op, "ffn%d_d%s" % (layer, n[4:]), with_bf16=True)
            parts["%s/%d" % (n, layer)], sent["%s/%d" % (n, layer)] = halves_of(g32), halves_of(g16)
        return sent

    dz3, dy3, dg11, db11, dgt_f1, sq_err = ln_bwd(x4, x3, y3, gt_f1, vec(ln_g[1, 1]), "ffn1_ln_bwd", target=tgt)
    loss_part = 0.5 * jnp.sum(sq_err) / d
    dhg1, dhu1, act1, dx3, dsc_f1, dsh_f1 = ffn_bwd(dy3, hg1, hu1, *ffn1_w, dz3, x3, sc_f1, "ffn1_bwd")
    sent = ffn_grads(1, u_f1, dhg1, dhu1, act1, dy3)
    dz2, dy2, dg10, db10, dgt_a1 = ln_bwd(dx3, x2, y2, gt_a1, vec(ln_g[1, 0]), "fox_ln_bwd")
    do_x, delta_x = linear_nt_delta(dy2, fox_w_o, o_x, sel_fox, "fox_out_bwd")
    (dq_x, dk_x, dv_x, dfq_x, dfk_x), got = split(fox_attn_bwd(
        qkv, do_x, cum, cum_rows, rows16(lse_x, "fox_lse_rows"), rows16(delta_x, "fox_delta_rows"), nb, "fox_attn_bwd",
        hosted=scatter(list(sent), sent)))
    landed(list(sent), got)
    dcum = tokens128(dfq_x + dfk_x, onehot16, "fox_dcum")
    dhf, dbf = fox_gate_bwd(dcum, hf, b_f_pad, triu, nb, "fox_gate_bwd")
    fox_d = [("q", dq_x), ("k", dk_x), ("v", dv_x)]
    dx2, dsc_a1, dsh_a1 = linear_nt_mod_bwd(
        [(dh, fox_w_in_t, i) for i, (_, dh) in enumerate(fox_d)] + [(dhf, fox_w_f_t, 0)], dz2, x2, sc_a1, "fox_in_bwd")
    dw_in_t = [wgrad(dh[None], u_x[None], "fox_dw" + tag)[0] for tag, dh in fox_d]
    dw_in_t.append(wgrad(dhf[None], u_x[None], "fox_dwf")[0][:FOX_HEADS])
    dw_in_t = jnp.concatenate(dw_in_t, axis=0).reshape(N_CHIPS, -1, 2, d // 2)
    parts["fox_w_in"] = jnp.transpose(dw_in_t, (0, 2, 1, 3))
    parts["fox_w_o"] = wgrad(o_x[None], dy2[None], "fox_dwo")[0].reshape(N_CHIPS, 2, -1, d)
    sent = {k: parts[k].astype(BF16) for k in ("fox_w_in", "fox_w_o")}
    dz1, dy1, dg01, db01, dgt_f0 = ln_bwd(dx2, x1, y1, gt_f, vec(ln_g[0, 1]), "ffn0_ln_bwd")
    (dhg0, dhu0, act0, dx1, dsc_f0, dsh_f0), got = split(ffn_bwd(dy1, hg0, hu0, *ffn0_w, dz1, x1, sc_f, "ffn0_bwd",
                                                                 hosted=scatter(list(sent), sent)))
    landed(list(sent), got)
    sent = ffn_grads(0, u_f0, dhg0, dhu0, act0, dy1)
    dz0, dy0, dg00, db00, dgt_a0 = ln_bwd(dx1, x0, y0, gt_a, vec(ln_g[0, 0]), "mla_ln_bwd")
    do_m, delta_m = linear_nt_delta(dy0, wts["mla_w_o"], o_m, sel_mla, "mla_out_bwd")
    parts["mla_w_o"] = wgrad(o_m[None], dy0[None], "mla_dwo")[0].reshape(N_CHIPS, 2, -1, d)
    (dqn_m, dqr_m, dkn_m, dkr_m, dv_m), got = split(mla_attn_bwd(
        q_m, kn_m, kr2_m, v_m, do_m, rows16(lse_m, "mla_lse_rows"), rows16(delta_m, "mla_delta_rows"), nb,
        "mla_attn_bwd", hosted=scatter(list(sent), sent)))
    landed(list(sent), got)
    sent = {"mla_w_o": parts["mla_w_o"].astype(BF16)}
    (dh_in, dq_pre, dgq, dgkv), got = split(mla_mid_bwd(
        dqn_m, dqr_m, dkn_m, dv_m, dkr_m, h_in, vec(mla_g_q), vec(mla_g_kv), w_uq_p, wts["mla_w_uk"],
        wts["mla_w_uv"], cos8, sin8, cos64, sin64s, swap64, from_heads, head_sum, "mla_mid_bwd",
        hosted=scatter(list(sent), sent)))
    landed(list(sent), got)
    parts["mla_w_uq"] = halves_of(_full_to_cols(wgrad(cq_m[None], dq_pre[None], "mla_dwuq")[0][:, inv_perm]))
    parts["mla_w_uk"] = halves_of(_full_to_cols(wgrad(ckv_m[None], dkn_m[None], "mla_dwuk")[0]))
    parts["mla_w_uv"] = halves_of(_full_to_cols(wgrad(ckv_m[None], dv_m[None], "mla_dwuv")[0]))
    parts["mla_w_in"] = wgrad(u_m[None], dh_in[None], "mla_dwin")[0].reshape(N_CHIPS, 2, -1, h_in.shape[1])
    sent = {k: parts[k].astype(BF16) for k in ("mla_w_in", "mla_w_uq", "mla_w_uk", "mla_w_uv")}
    (dx0, dsc_a0, dsh_a0), got = split(linear_nt_mod_bwd([(dh_in, wts["mla_w_in"], None)], dz0, x0, sc_a, "mla_in_bwd",
                                                         hosted=scatter(list(sent), sent)))
    landed(list(sent), got)

    dmods = [(dsh_a0, dsc_a0, dgt_a0, dsh_f0, dsc_f0, dgt_f0), (dsh_a1, dsc_a1, dgt_a1, dsh_f1, dsc_f1, dgt_f1)]
    d_ln_g = jnp.stack([jnp.concatenate([dg00, dg01], axis=0), jnp.concatenate([dg10, dg11], axis=0)])
    d_ln_b = jnp.stack([jnp.concatenate([db00, db01], axis=0), jnp.concatenate([db10, db11], axis=0)])
    return loss_part, dx0.reshape(nb, s, d), (parts, recv), dmods, d_ln_g, d_ln_b, dgq, dgkv, dbf[:, :FOX_HEADS]


def _pad_rows(a, rows):
    return jnp.pad(a, ((0, rows - a.shape[0]), (0, 0)))


def kernel(x, c, positions, mla_w_in, mla_g_q, mla_w_uq, mla_g_kv, mla_w_uk, mla_w_uv, mla_w_o, fox_w_in, fox_b_f, fox_w_o, ada_w, ada_b, ffn_w_gate, ffn_w_up, ffn_w_down, ln_g, ln_b, loss_target, m_mla_w_in, m_mla_g_q, m_mla_w_uq, m_mla_g_kv, m_mla_w_uk, m_mla_w_uv, m_mla_w_o, m_fox_w_in, m_fox_b_f, m_fox_w_o, m_ada_w, m_ada_b, m_ffn_w_gate, m_ffn_w_up, m_ffn_w_down, m_ln_g, m_ln_b, v_mla_w_in, v_mla_g_q, v_mla_w_uq, v_mla_g_kv, v_mla_w_uk, v_mla_w_uv, v_mla_w_o, v_fox_w_in, v_fox_b_f, v_fox_w_o, v_ada_w, v_ada_b, v_ffn_w_gate, v_ffn_w_up, v_ffn_w_down, v_ln_g, v_ln_b):
    args = dict(locals())
    nb, s, d = x.shape
    ax, ay, ac = lax.axis_index("x"), lax.axis_index("y"), lax.axis_index("c")
    chip = 2 * ax + ay
    dev = 2 * chip + ac
    n_dev = 2 * N_CHIPS
    n_all = nb * n_dev

    shard_shapes = {n: (args[n].shape if _SHARD_KIND[n] == "chunk" else args[n].shape[1:]) for n in _PACKED}

    def block(n, layer=None):
        w = args[n].reshape(shard_shapes[n]) if layer is None else args[n][layer]
        return _halves(w.astype(BF16))

    mla_names = [n for n in _PACKED if n.startswith("mla")]
    ffn_names = ("ffn_w_gate", "ffn_w_up", "ffn_w_down")
    fox_in_t = jnp.swapaxes(fox_w_in, 1, 2)[0].astype(BF16)
    fox_in_t = jnp.stack([fox_in_t[:, :d // 2], fox_in_t[:, d // 2:]])
    shards = {"ffn0": [block(n, 0) for n in ffn_names], "fox": [fox_in_t, block("fox_w_o")],
              "ffn1": [block(n, 1) for n in ffn_names]}

    ln_cols = ln_g.shape[-1]
    ln_blk = jnp.concatenate([ln_g.reshape(2 * DEPTH, ln_cols), ln_b.reshape(2 * DEPTH, ln_cols)], axis=0)
    early = jnp.concatenate([_pad_rows(c, 8), jnp.pad(_pad_rows(ln_blk, 8), ((0, 0), (0, d - ln_cols)))], axis=0)
    early, mla_all = all_gather8(early, "gather_c_ln_mla", hosted=_gather_comm([block(n) for n in mla_names]))
    wts = {}
    for n, g in zip(mla_names, mla_all):
        g = g.reshape(N_CHIPS, *shard_shapes[n])
        wts[n] = g.reshape(-1, g.shape[-1]) if _SHARD_KIND[n] == "rows" else _cols_to_full(g)
    early = early.reshape(n_dev, 16, d)
    c_all = early[:, :nb].reshape(n_all, d)
    ln_all = early.reshape(N_CHIPS, 2, 16, d)[:, 0, 8:8 + 4 * DEPTH, :ln_cols]
    ln_all = jnp.transpose(ln_all, (1, 0, 2)).reshape(4 * DEPTH, d)
    ln_g_full = ln_all[:2 * DEPTH].reshape(DEPTH, 2, d)
    ln_b_full = ln_all[2 * DEPTH:].reshape(DEPTH, 2, d)
    mod_part = ada_mod_part(c_all, ada_w, "ada_mod")
    ncol = mod_part.shape[-1]
    mod_g = all_gather8(mod_part.reshape(DEPTH * n_all, ncol), "gather_mod")
    mod_g = mod_g.reshape(N_CHIPS, 2, DEPTH, n_all, ncol)[:, 0]
    mod_full = jnp.transpose(mod_g, (1, 2, 0, 3)).reshape(DEPTH, n_all, N_CHIPS * ncol) + ada_b[:, None, :]
    mod_loc = lax.dynamic_slice_in_dim(mod_full, dev * nb, nb, axis=1)
    mods = [tuple(mod_loc[i, :, k * d:(k + 1) * d].reshape(nb, 1, d) for k in range(6)) for i in range(DEPTH)]

    loss_part, grad_x, (parts, recv), dmods, d_ln_g, d_ln_b, dgq, dgkv, dbf = _local_step(
        x, positions, loss_target, mods, wts, ln_g_full, ln_b_full, mla_g_q[0], mla_g_kv[0], fox_b_f[0], shards)
    loss = lax.psum(loss_part, ("x", "y", "c"))

    dmod_rows = jnp.stack([jnp.concatenate([v_.reshape(nb, d) for v_ in dm], axis=1) for dm in dmods])
    small = jnp.concatenate([
        d_ln_g.reshape(2 * DEPTH, d), d_ln_b.reshape(2 * DEPTH, d),
        jnp.pad(jnp.concatenate([dgq, dgkv, dbf], axis=1), ((0, 0), (0, d - 2 * MLA_QR - FOX_HEADS))),
        dmod_rows.reshape(DEPTH * nb * 6, d)], axis=0)
    n_small = small.shape[0]
    small_rows = -(-n_small // 8) * 8
    small_all = all_gather8(_pad_rows(small, small_rows), "gather_stats").reshape(n_dev, small_rows, d)
    stat_sum = sum_leading(small_all, "sum_stats")
    g_ln_g = lax.dynamic_slice_in_dim(stat_sum[:2 * DEPTH], chip * ln_cols, ln_cols, axis=1).reshape(DEPTH, 2, ln_cols)
    g_ln_b = lax.dynamic_slice_in_dim(stat_sum[2 * DEPTH:4 * DEPTH], chip * ln_cols, ln_cols, axis=1).reshape(DEPTH, 2, ln_cols)
    row = stat_sum[4 * DEPTH]
    g_gq = row[:MLA_QR].reshape(1, MLA_QR)
    g_gkv = row[MLA_QR:2 * MLA_QR].reshape(1, MLA_KVR)
    g_bf = row[2 * MLA_QR:2 * MLA_QR + FOX_HEADS].reshape(1, FOX_HEADS)
    base = 4 * DEPTH + 1
    dmod_all = small_all[:, base:base + DEPTH * nb * 6].reshape(n_dev, DEPTH, nb, 6 * d)
    dmod_all = jnp.transpose(dmod_all, (1, 0, 2, 3)).reshape(DEPTH, n_all, 6 * d)
    g_ada_b = sum_leading(jnp.transpose(dmod_all, (1, 0, 2)), "sum_ada_b")
    dmod_mine = lax.dynamic_slice_in_dim(dmod_all, chip * ncol, ncol, axis=2)
    g_ada_w = ada_grad(c_all.T, dmod_mine, "ada_grad")

    place = jnp.stack([dev, ac, chip]).astype(jnp.int32)
    bufs = []
    for n in _PACKED:
        if _SHARD_KIND[n] == "chunk":
            buf = None
            for layer in range(DEPTH):
                key = "%s/%d" % (n, layer)
                buf = sum_devices(parts[key], recv[key], place, "rs_sum_%s%d" % (n, layer), slot=(layer, DEPTH, buf))
        else:
            buf = sum_devices(parts[n], recv[n], place, "rs_sum_" + n)
        bufs.append(buf)
    joined = sibling_join_halves(bufs, "rs_join")
    g_big = {n: j.reshape(j.shape[0], 2 * j.shape[2], j.shape[3]) for n, j in zip(_PACKED, joined)}
    j = joined[_PACKED.index("fox_w_in")]
    g_big["fox_w_in"] = jnp.transpose(j, (0, 2, 1, 3)).reshape(1, j.shape[2], 2 * j.shape[3])

    g_out = {
        "mla_w_in": g_big["mla_w_in"], "mla_g_q": g_gq, "mla_w_uq": g_big["mla_w_uq"], "mla_g_kv": g_gkv,
        "mla_w_uk": g_big["mla_w_uk"], "mla_w_uv": g_big["mla_w_uv"], "mla_w_o": g_big["mla_w_o"],
        "fox_w_in": g_big["fox_w_in"], "fox_b_f": g_bf, "fox_w_o": g_big["fox_w_o"],
        "ada_w": g_ada_w, "ada_b": g_ada_b, "ffn_w_gate": g_big["ffn_w_gate"], "ffn_w_up": g_big["ffn_w_up"],
        "ffn_w_down": g_big["ffn_w_down"], "ln_g": g_ln_g, "ln_b": g_ln_b}
    names = ["mla_w_in", "mla_g_q", "mla_w_uq", "mla_g_kv", "mla_w_uk", "mla_w_uv", "mla_w_o", "fox_w_in", "fox_b_f",
             "fox_w_o", "ada_w", "ada_b", "ffn_w_gate", "ffn_w_up", "ffn_w_down", "ln_g", "ln_b"]
    small_names = ["mla_g_q", "mla_g_kv", "fox_b_f", "ada_b", "ln_g", "ln_b"]
    deltas, new_m, new_v = {}, {}, {}
    for n in names:
        if n in small_names:
            continue
        shp = args[n].shape
        if n in _TRANSPOSED:
            view = lambda a: jnp.swapaxes(a, 1, 2).reshape(-1, shp[1])
            back = lambda a: jnp.swapaxes(a.reshape(shp[0], shp[2], shp[1]), 1, 2)
        else:
            view = lambda a: a.reshape(-1, shp[-1])
            back = lambda a: a.reshape(shp)
        dl, mn, vn = adamw(view(args[n]), g_out[n].reshape(view(args[n]).shape), view(args["m_" + n]),
                           view(args["v_" + n]), "adamw_" + n)
        g_out[n], deltas[n], new_m[n], new_v[n] = back(g_out[n].reshape(view(args[n]).shape)), back(dl), back(mn), back(vn)

    def small_pack(prefix, src):
        flat = jnp.concatenate([src[prefix + n].reshape(-1) for n in small_names])
        size = -(-flat.shape[0] // (8 * 128)) * 8 * 128
        return jnp.pad(flat, (0, size - flat.shape[0])).reshape(-1, 128)

    sd, sm, sv = adamw(small_pack("", args), small_pack("", g_out), small_pack("m_", args), small_pack("v_", args),
                       "adamw_small")
    off = 0
    for n in small_names:
        shp = args[n].shape
        size = math.prod(shp)
        deltas[n] = sd.reshape(-1)[off:off + size].reshape(shp)
        new_m[n] = sm.reshape(-1)[off:off + size].reshape(shp)
        new_v[n] = sv.reshape(-1)[off:off + size].reshape(shp)
        off += size

    outs = [loss, grad_x]
    outs += [g_out[n].reshape(args[n].shape) for n in names]
    outs += [deltas[n] for n in names] + [new_m[n] for n in names] + [new_v[n] for n in names]
    return tuple(outs)
```

```python
import functools
import math

import numpy as np
import jax
import jax.numpy as jnp
from jax import lax
from jax.experimental import pallas as pl
from jax.experimental.pallas import tpu as pltpu

F32 = jnp.float32
BF16 = jnp.bfloat16
MESH = pl.DeviceIdType.MESH

D_MODEL = 1024
DEPTH = 2
MLA_HEADS = 8
MLA_NOPE = 128
MLA_ROPE = 64
MLA_V = 128
MLA_QR = 256
MLA_KVR = 256
ROPE_THETA = 10000.0
FOX_HEADS = 16
FOX_HD = 64
D_FF = 2816
N_CHIPS = 4
FF_CHUNK = D_FF // N_CHIPS
ALPHA = (2.0 * DEPTH) ** 0.25
EPS = 1e-5
ADAM_LR = 0.001
ADAM_B1 = 0.9
ADAM_B2 = 0.999
ADAM_EPS = 1e-08
ADAM_WD = 0.01
ADAM_STEP = 10

VMEM_LIMIT = 56 * 1024 * 1024
TOKEN_TILE = 512
WGRAD_TOKENS = 2048
ATTN_TILE = 512
FOX_GROUP = 8
MLA_GROUP = 4
COMM_BLOCK_BYTES = 8 * 1024 * 1024
ADAMW_BLOCK_BYTES = 1024 * 1024


def _cp(n_axes):
    return pltpu.CompilerParams(dimension_semantics=("arbitrary",) * n_axes, vmem_limit_bytes=VMEM_LIMIT)


def _dot(a, b):
    return jnp.dot(a, b, preferred_element_type=F32)


def _dot_nt(a, b):
    return lax.dot_general(a, b, (((1,), (1,)), ((), ())), preferred_element_type=F32)


def _dot_tn(a, b):
    return lax.dot_general(a, b, (((0,), (0,)), ((), ())), preferred_element_type=F32)


def _dot_f32(a, b):
    return jnp.dot(a, b, preferred_element_type=F32, precision=lax.Precision.HIGHEST)


def _sds(shape, dtype):
    return jax.ShapeDtypeStruct(shape, dtype)


def _place():
    return lax.axis_index("x"), lax.axis_index("y"), lax.axis_index("c")


class _Hosted:
    def __init__(self, inputs, out_shape, sems, start, finish, in_place=False):
        self.inputs, self.out_shape, self.sems, self.start, self.finish = inputs, out_shape, sems, start, finish
        self.in_place = in_place


def _call(body, name, grid, in_specs, out_specs, out_shape, args, scratch_shapes=(), hosted=None):
    in_specs, out_specs, out_shape, scratch_shapes = list(in_specs), list(out_specs), list(out_shape), list(scratch_shapes)
    if hosted is None:
        return pl.pallas_call(body, name=name, grid=grid, in_specs=in_specs, out_specs=out_specs, out_shape=out_shape,
                              scratch_shapes=scratch_shapes, compiler_params=_cp(len(grid)))(*args)
    n_in, n_out, n_scr = len(in_specs), len(out_specs), len(scratch_shapes)
    h_in, h_out = len(hosted.inputs), len(hosted.out_shape)

    def carried(*refs):
        o0 = n_in + h_in
        s0 = o0 + n_out + h_out
        c_in, c_out, c_sem = refs[n_in:o0], refs[o0 + n_out:s0], refs[s0 + n_scr:]
        ids = [pl.program_id(a) for a in range(len(grid))]
        first = functools.reduce(jnp.logical_and, [i == 0 for i in ids])
        last = functools.reduce(jnp.logical_and, [i == g - 1 for i, g in zip(ids, grid)])

        @pl.when(first)
        def _():
            hosted.start(c_in, c_out, c_sem)

        body(*refs[:n_in], *refs[o0:o0 + n_out], *refs[s0:s0 + n_scr])

        @pl.when(last)
        def _():
            hosted.finish(c_in, c_out, c_sem)

    hbm = pl.BlockSpec(memory_space=pl.ANY)
    aliases = {n_in + k: n_out + k for k in range(h_in)} if hosted.in_place else {}
    res = pl.pallas_call(
        carried, name=name, grid=grid, in_specs=in_specs + [hbm] * h_in, out_specs=out_specs + [hbm] * h_out,
        out_shape=out_shape + list(hosted.out_shape), scratch_shapes=scratch_shapes + list(hosted.sems),
        input_output_aliases=aliases, compiler_params=_cp(len(grid)))(*args, *hosted.inputs)
    return res[:n_out], res[n_out:]


def mod_linear(x, shift, scale, w, out_dtype, name, tn=None, emit_u=False, w_rows=None):
    t, d = x.shape
    n = w.shape[1] if w_rows is None else w_rows
    tn = n if tn is None else tn
    tm = TOKEN_TILE
    tps = (t // shift.shape[0]) // tm

    def body(x_ref, sh_ref, sc_ref, w_ref, o_ref, *rest):
        u = (x_ref[...] * (1.0 + sc_ref[...]) + sh_ref[...]).astype(BF16)
        o_ref[...] = (_dot(u, w_ref[...]) if w_rows is None else _dot_nt(u, w_ref[...])).astype(out_dtype)
        if emit_u:
            @pl.when(pl.program_id(1) == 0)
            def _():
                rest[0][...] = u

    vec = pl.BlockSpec((None, 1, d), lambda i, j: (i // tps, 0, 0))
    out_shape = [_sds((t, n), out_dtype)]
    out_specs = [pl.BlockSpec((tm, tn), lambda i, j: (i, j))]
    if emit_u:
        out_shape.append(_sds((t, d), BF16))
        out_specs.append(pl.BlockSpec((tm, d), lambda i, j: (i, 0)))
    w_spec = pl.BlockSpec((d, tn), lambda i, j: (0, j)) if w_rows is None else pl.BlockSpec((tn, d), lambda i, j: (j, 0))
    res = pl.pallas_call(
        body, name=name, grid=(t // tm, n // tn),
        in_specs=[pl.BlockSpec((tm, d), lambda i, j: (i, 0)), vec, vec, w_spec],
        out_specs=out_specs, out_shape=out_shape, compiler_params=_cp(2),
    )(x, shift, scale, w)
    return res if emit_u else res[0]


def _rms(h, g):
    rstd = lax.rsqrt(jnp.mean(h * h, axis=-1, keepdims=True) + EPS)
    return h * rstd, rstd


def mla_mid_fwd(h, g_q, g_kv, w_uq, w_uk, w_uv, cos8, sin8, cos64, sin64s, swap64, rope_to_heads, dup64, name):
    t = h.shape[0]
    tm = TOKEN_TILE
    hq = MLA_HEADS * MLA_NOPE
    hr = MLA_HEADS * MLA_ROPE // 2

    def body(h_ref, gq_ref, gkv_ref, wuq_ref, wuk_ref, wuv_ref, c8_ref, s8_ref, c64_ref, s64_ref, sw_ref, p_ref, d_ref,
             q_ref, kn_ref, v_ref, kr_ref, cq_ref, ckv_ref):
        hh = h_ref[...]
        cq = (_rms(hh[:, :MLA_QR], None)[0] * gq_ref[...]).astype(BF16)
        ckv = (_rms(hh[:, MLA_QR:MLA_QR + MLA_KVR], None)[0] * gkv_ref[...]).astype(BF16)
        cq_ref[...] = cq
        ckv_ref[...] = ckv
        q = _dot(cq, wuq_ref[...])
        x1 = q[:, hq:hq + hr]
        x2 = q[:, hq + hr:]
        cs = c8_ref[...]
        sn = s8_ref[...]
        rot = jnp.concatenate([x1 * cs - x2 * sn, x2 * cs + x1 * sn], axis=1).astype(BF16)
        q_ref[...] = jnp.concatenate([q[:, :hq].astype(BF16), _dot(rot, p_ref[...]).astype(BF16)], axis=1)
        kn_ref[...] = _dot(ckv, wuk_ref[...]).astype(BF16)
        v_ref[...] = _dot(ckv, wuv_ref[...]).astype(BF16)
        kr = hh[:, MLA_QR + MLA_KVR:]
        kr = (kr * c64_ref[...] + _dot_f32(kr, sw_ref[...]) * s64_ref[...]).astype(BF16)
        kr_ref[...] = _dot(kr, d_ref[...]).astype(BF16)

    def rows(n):
        return pl.BlockSpec((tm, n), lambda i: (i, 0))

    def whole(a):
        return pl.BlockSpec(a.shape, lambda i: (0,) * a.ndim)

    nq = w_uq.shape[1]
    return pl.pallas_call(
        body, name=name, grid=(t // tm,),
        in_specs=[rows(h.shape[1]), whole(g_q), whole(g_kv), whole(w_uq), whole(w_uk), whole(w_uv),
                  rows(hr), rows(hr), rows(MLA_ROPE), rows(MLA_ROPE), whole(swap64), whole(rope_to_heads), whole(dup64)],
        out_specs=[rows(nq), rows(hq), rows(hq), rows(2 * MLA_ROPE), rows(MLA_QR), rows(MLA_KVR)],
        out_shape=[_sds((t, nq), BF16), _sds((t, hq), BF16), _sds((t, hq), BF16), _sds((t, 2 * MLA_ROPE), BF16),
                   _sds((t, MLA_QR), BF16), _sds((t, MLA_KVR), BF16)],
        compiler_params=_cp(1),
    )(h, g_q, g_kv, w_uq, w_uk, w_uv, cos8, sin8, cos64, sin64s, swap64, rope_to_heads, dup64)


def _pick_lane(tile, idx):
    lane = lax.broadcasted_iota(jnp.int32, tile.shape, 1)
    return jnp.sum(jnp.where(lane == idx, tile, 0.0), axis=1, keepdims=True)


def _pick_row(tile, idx):
    row = lax.broadcasted_iota(jnp.int32, tile.shape, 0)
    return jnp.sum(jnp.where(row == idx, tile, 0.0), axis=0, keepdims=True)


def _put_lane(tile, idx, col):
    lane = lax.broadcasted_iota(jnp.int32, tile.shape, 1)
    return jnp.where(lane == idx, col, tile)


def _put_row(tile, idx, row):
    r = lax.broadcasted_iota(jnp.int32, tile.shape, 0)
    return tile + jnp.where(r == idx, row, 0.0)


def _causal_softmax_blocks(i, tq, heads):
    def block(j, carry, masked):
        new = []
        for (score_fn, pv_fn, _), (m, l, acc) in zip(heads, carry):
            sc = score_fn(j)
            if masked:
                keep = lax.broadcasted_iota(jnp.int32, sc.shape, 0) >= lax.broadcasted_iota(jnp.int32, sc.shape, 1)
                sc = jnp.where(keep, sc, -1e30)
            m_new = jnp.maximum(m, jnp.max(sc, axis=1, keepdims=True))
            a = jnp.exp(m - m_new)
            p = jnp.exp(sc - m_new)
            new.append((m_new, a * l + jnp.sum(p, axis=1, keepdims=True), a * acc + pv_fn(j, p.astype(BF16))))
        return tuple(new)

    init = tuple((jnp.full((tq, 1), -1e30, F32), jnp.zeros((tq, 1), F32), jnp.zeros((tq, dv), F32)) for _, _, dv in heads)
    carry = lax.fori_loop(0, i, lambda j, c: block(j, c, False), init)
    return [(acc / l, m + jnp.log(l)) for m, l, acc in block(i, carry, True)]


def fox_attn_fwd(qkv, cum, cum_rows, nb, name, hosted=None):
    t = qkv.shape[0]
    s = t // nb
    tq = ATTN_TILE
    nq = s // tq
    wide = FOX_GROUP * FOX_HD
    ngroups = FOX_HEADS // FOX_GROUP
    scale = FOX_HD ** -0.5

    def body(q_ref, k_ref, v_ref, cum_ref, cr_ref, o_ref, lse_ref):
        i = pl.program_id(1)
        hg = pl.program_id(2)

        @pl.when(hg == 0)
        def _():
            lse_ref[...] = jnp.zeros_like(lse_ref)

        low = lax.broadcasted_iota(jnp.int32, (tq, 128), 1) < FOX_HD
        cum_t = cum_ref[...]

        def rows_of(j):
            return pl.ds(pl.multiple_of(j * tq, tq), tq)

        def head(a):
            hd = FOX_GROUP * hg + a
            cols = slice(128 * (a // 2), 128 * (a // 2) + 128)
            q = q_ref[:, cols]
            qa = jnp.where(low if a % 2 == 0 else jnp.logical_not(low), q, jnp.zeros_like(q)) * scale
            fq = _pick_lane(cum_t, hd)
            return (lambda j: _dot_nt(qa, k_ref[rows_of(j), cols]) + fq - _pick_row(cr_ref[j], hd),
                    lambda j, p: _dot(p, v_ref[rows_of(j), cols]), 2 * FOX_HD)

        res = _causal_softmax_blocks(i, tq, [head(a) for a in range(FOX_GROUP)])
        o_ref[...] = jnp.concatenate([jnp.where(low, res[a][0], res[a + 1][0]) for a in range(0, FOX_GROUP, 2)],
                                     axis=1).astype(BF16)
        lse_t = lse_ref[...]
        for a in range(FOX_GROUP):
            lse_t = _put_lane(lse_t, FOX_GROUP * hg + a, res[a][1])
        lse_ref[...] = lse_t

    return _call(
        body, name, (nb, nq, ngroups),
        [pl.BlockSpec((tq, wide), lambda b, i, hg: (b * nq + i, hg)),
         pl.BlockSpec((s, wide), lambda b, i, hg: (b, ngroups + hg)),
         pl.BlockSpec((s, wide), lambda b, i, hg: (b, 2 * ngroups + hg)),
         pl.BlockSpec((tq, 128), lambda b, i, hg: (b * nq + i, 0)),
         pl.BlockSpec((nq, 16, tq), lambda b, i, hg: (b, 0, 0))],
        [pl.BlockSpec((tq, wide), lambda b, i, hg: (b * nq + i, hg)),
         pl.BlockSpec((tq, 128), lambda b, i, hg: (b * nq + i, 0))],
        [_sds((t, D_MODEL), BF16), _sds((t, 128), F32)], (qkv, qkv, qkv, cum, cum_rows), hosted=hosted)


def mla_attn_fwd(q, kn, kr2, v, nb, name, hosted=None):
    t = q.shape[0]
    s = t // nb
    tq = ATTN_TILE
    nq = s // tq
    ngroups = MLA_HEADS // MLA_GROUP
    wide = MLA_GROUP * MLA_NOPE
    rwide = MLA_GROUP * MLA_ROPE
    scale = (MLA_NOPE + MLA_ROPE) ** -0.5

    def body(qn_ref, qr_ref, kn_ref, kr_ref, v_ref, o_ref, lse_ref):
        i = pl.program_id(1)
        hg = pl.program_id(2)

        @pl.when(hg == 0)
        def _():
            lse_ref[...] = jnp.zeros_like(lse_ref)

        low = lax.broadcasted_iota(jnp.int32, (tq, 128), 1) < MLA_ROPE

        def rows_of(j):
            return pl.ds(pl.multiple_of(j * tq, tq), tq)

        def head(a):
            cols = slice(a * MLA_NOPE, (a + 1) * MLA_NOPE)
            qr = qr_ref[:, 128 * (a // 2):128 * (a // 2) + 128]
            q_cat = jnp.concatenate([qn_ref[:, cols], jnp.where(low if a % 2 == 0 else jnp.logical_not(low), qr,
                                                                jnp.zeros_like(qr))], axis=1)
            return (lambda j: _dot_nt(q_cat, jnp.concatenate([kn_ref[rows_of(j), cols], kr_ref[rows_of(j), :]], axis=1)) * scale,
                    lambda j, p: _dot(p, v_ref[rows_of(j), cols]), MLA_V)

        res = _causal_softmax_blocks(i, tq, [head(a) for a in range(MLA_GROUP)])
        o_ref[...] = jnp.concatenate([r[0] for r in res], axis=1).astype(BF16)
        lse_t = lse_ref[...]
        for a in range(MLA_GROUP):
            lse_t = _put_lane(lse_t, MLA_GROUP * hg + a, res[a][1])
        lse_ref[...] = lse_t

    rope0 = MLA_HEADS * MLA_NOPE // rwide
    return _call(
        body, name, (nb, nq, ngroups),
        [pl.BlockSpec((tq, wide), lambda b, i, hg: (b * nq + i, hg)),
         pl.BlockSpec((tq, rwide), lambda b, i, hg: (b * nq + i, rope0 + hg)),
         pl.BlockSpec((s, wide), lambda b, i, hg: (b, hg)),
         pl.BlockSpec((s, 128), lambda b, i, hg: (b, 0)),
         pl.BlockSpec((s, wide), lambda b, i, hg: (b, hg))],
        [pl.BlockSpec((tq, wide), lambda b, i, hg: (b * nq + i, hg)),
         pl.BlockSpec((tq, 128), lambda b, i, hg: (b * nq + i, 0))],
        [_sds((t, MLA_HEADS * MLA_V), BF16), _sds((t, 128), F32)], (q, q, kn, kr2, v), hosted=hosted)


def rows16(a, name):
    t = a.shape[0]
    tq = ATTN_TILE

    def body(a_ref, o_ref):
        o_ref[...] = a_ref[...].T[:16, :]

    return pl.pallas_call(
        body, name=name, grid=(t // tq,), in_specs=[pl.BlockSpec((tq, 128), lambda n: (n, 0))],
        out_specs=pl.BlockSpec((None, 16, tq), lambda n: (n, 0, 0)), out_shape=_sds((t // tq, 16, tq), F32),
        compiler_params=_cp(1),
    )(a)


def tokens128(rows, onehot, name):
    nblk, _, tq = rows.shape

    def body(r_ref, e_ref, o_ref):
        o_ref[...] = lax.dot_general(r_ref[...], e_ref[...], (((0,), (0,)), ((), ())), preferred_element_type=F32,
                                     precision=lax.Precision.HIGHEST)

    return pl.pallas_call(
        body, name=name, grid=(nblk,),
        in_specs=[pl.BlockSpec((None, 16, tq), lambda n: (n, 0, 0)), pl.BlockSpec((16, 128), lambda n: (0, 0))],
        out_specs=pl.BlockSpec((tq, 128), lambda n: (n, 0)), out_shape=_sds((nblk * tq, 128), F32),
        compiler_params=_cp(1),
    )(rows, onehot)


def _layer_norm(z, g, b):
    mu = jnp.mean(z, axis=-1, keepdims=True)
    zc = z - mu
    rstd = lax.rsqrt(jnp.mean(zc * zc, axis=-1, keepdims=True) + EPS)
    xhat = zc * rstd
    return xhat * g + b, xhat, rstd


def linear_resid_ln(a, w, x_in, gate, ln_g, ln_b, name):
    t, kdim = a.shape
    d = w.shape[1]
    tm = TOKEN_TILE
    tps = (t // gate.shape[0]) // tm

    def body(a_ref, w_ref, x_ref, gt_ref, g_ref, b_ref, y_ref, xo_ref):
        y = _dot(a_ref[...], w_ref[...])
        y_ref[...] = y
        z = ALPHA * x_ref[...] + (1.0 + gt_ref[...]) * y
        xo_ref[...] = _layer_norm(z, g_ref[...], b_ref[...])[0]

    rows = pl.BlockSpec((tm, d), lambda i: (i, 0))
    vec = pl.BlockSpec((1, d), lambda i: (0, 0))
    return pl.pallas_call(
        body, name=name, grid=(t // tm,),
        in_specs=[pl.BlockSpec((tm, kdim), lambda i: (i, 0)), pl.BlockSpec((kdim, d), lambda i: (0, 0)), rows,
                  pl.BlockSpec((None, 1, d), lambda i: (i // tps, 0, 0)), vec, vec],
        out_specs=[rows, rows], out_shape=[_sds((t, d), F32), _sds((t, d), F32)],
        compiler_params=_cp(1),
    )(a, w, x_in, gate, ln_g, ln_b)


def _resident(a):
    return pl.BlockSpec(a.shape, lambda *_: (0,) * a.ndim, pipeline_mode=pl.Buffered(1))


def ffn_fwd(x_in, shift, scale, gate, wg, wu, wd, ln_g, ln_b, name, hosted=None):
    t, d = x_in.shape
    c, _, fc = wg.shape
    tm = TOKEN_TILE
    tps = (t // gate.shape[0]) // tm

    def body(x_ref, sh_ref, sc_ref, gt_ref, wg_ref, wu_ref, wd_ref, g_ref, b_ref,
             u_ref, hg_ref, hu_ref, y_ref, xo_ref, acc_ref):
        cc = pl.program_id(1)

        @pl.when(cc == 0)
        def _():
            u_ref[...] = (x_ref[...] * (1.0 + sc_ref[...]) + sh_ref[...]).astype(BF16)
            acc_ref[...] = jnp.zeros_like(acc_ref)

        u = u_ref[...]
        hg = _dot(u, wg_ref[cc])
        hu = _dot(u, wu_ref[cc])
        hg_ref[...] = hg.astype(BF16)
        hu_ref[...] = hu.astype(BF16)
        act = (hg * jax.nn.sigmoid(hg) * hu).astype(BF16)
        acc_ref[...] += _dot(act, wd_ref[cc])

        @pl.when(cc == c - 1)
        def _():
            y = acc_ref[...]
            y_ref[...] = y
            z = ALPHA * x_ref[...] + (1.0 + gt_ref[...]) * y
            xo_ref[...] = _layer_norm(z, g_ref[...], b_ref[...])[0]

    rows = pl.BlockSpec((tm, d), lambda i, cc: (i, 0))
    bvec = pl.BlockSpec((None, 1, d), lambda i, cc: (i // tps, 0, 0))
    vec = pl.BlockSpec((1, d), lambda i, cc: (0, 0))
    hspec = pl.BlockSpec((None, tm, fc), lambda i, cc: (cc, i, 0))
    wcol = _resident(wg)
    return _call(
        body, name, (t // tm, c),
        [rows, bvec, bvec, bvec, wcol, wcol, _resident(wd), vec, vec],
        [rows, hspec, hspec, rows, rows],
        [_sds((t, d), BF16), _sds((c, t, fc), BF16), _sds((c, t, fc), BF16), _sds((t, d), F32), _sds((t, d), F32)],
        (x_in, shift, scale, gate, wg, wu, wd, ln_g, ln_b), scratch_shapes=[pltpu.VMEM((tm, d), F32)], hosted=hosted)


def fox_gate_fwd(hf, b_f, tri, n_batch, name):
    t, n = hf.shape
    blk = tri.shape[0]
    nb = (t // n_batch) // blk

    def body(hf_ref, b_ref, tri_ref, o_ref, carry_ref):
        @pl.when(pl.program_id(1) == 0)
        def _():
            carry_ref[...] = jnp.zeros_like(carry_ref)

        xx = hf_ref[...] + b_ref[...]
        lf = jnp.minimum(xx, 0.0) - jnp.log(1.0 + jnp.exp(-jnp.abs(xx)))
        cum = _dot_f32(tri_ref[...], lf) + carry_ref[...]
        o_ref[...] = cum
        carry_ref[...] = cum[blk - 1:blk, :]

    return pl.pallas_call(
        body, name=name, grid=(n_batch, nb),
        in_specs=[pl.BlockSpec((blk, n), lambda bb, i: (bb * nb + i, 0)), pl.BlockSpec((1, n), lambda bb, i: (0, 0)),
                  pl.BlockSpec((blk, blk), lambda bb, i: (0, 0))],
        out_specs=pl.BlockSpec((blk, n), lambda bb, i: (bb * nb + i, 0)),
        out_shape=_sds((t, n), F32), scratch_shapes=[pltpu.VMEM((1, n), F32)],
        compiler_params=_cp(2),
    )(hf, b_f, tri)


def ln_bwd(dxo, x_in, y, gate, ln_g, name, target=None):
    t, d = dxo.shape
    nb = gate.shape[0]
    tm = TOKEN_TILE
    tps = (t // nb) // tm
    with_loss = target is not None

    def body(dxo_ref, *refs):
        if with_loss:
            t_ref, x_ref, y_ref, gt_ref, g_ref, dz_ref, dy_ref, dg_ref, db_ref, dgt_ref, l_ref = refs
        else:
            x_ref, y_ref, gt_ref, g_ref, dz_ref, dy_ref, dg_ref, db_ref, dgt_ref = refs
        i = pl.program_id(0)

        @pl.when(i == 0)
        def _():
            dg_ref[...] = jnp.zeros_like(dg_ref)
            db_ref[...] = jnp.zeros_like(db_ref)
            if with_loss:
                l_ref[...] = jnp.zeros_like(l_ref)

        @pl.when(i % tps == 0)
        def _():
            dgt_ref[...] = jnp.zeros_like(dgt_ref)

        yy = y_ref[...]
        g1 = 1.0 + gt_ref[...]
        z = ALPHA * x_ref[...] + g1 * yy
        _, xhat, rstd = _layer_norm(z, 1.0, 0.0)
        dxo_v = dxo_ref[...]
        if with_loss:
            err = dxo_v - t_ref[...]
            l_ref[...] += jnp.sum(err * err, axis=0, keepdims=True)
            dxo_v = err / d
        dg_ref[...] += jnp.sum(dxo_v * xhat, axis=0, keepdims=True)
        db_ref[...] += jnp.sum(dxo_v, axis=0, keepdims=True)
        dxh = dxo_v * g_ref[...]
        dz = rstd * (dxh - jnp.mean(dxh, axis=-1, keepdims=True) - xhat * jnp.mean(dxh * xhat, axis=-1, keepdims=True))
        dz_ref[...] = dz
        dy_ref[...] = (g1 * dz).astype(BF16)
        dgt_ref[...] += jnp.sum(dz * yy, axis=0, keepdims=True)

    rows = pl.BlockSpec((tm, d), lambda i: (i, 0))
    vec = pl.BlockSpec((1, d), lambda i: (0, 0))
    bvec = pl.BlockSpec((None, 1, d), lambda i: (i // tps, 0, 0))
    return pl.pallas_call(
        body, name=name, grid=(t // tm,), in_specs=[rows] * (4 if with_loss else 3) + [bvec, vec],
        out_specs=[rows, rows, vec, vec, bvec] + ([vec] if with_loss else []),
        out_shape=[_sds((t, d), F32), _sds((t, d), BF16), _sds((1, d), F32), _sds((1, d), F32), _sds((nb, 1, d), F32)]
        + ([_sds((1, d), F32)] if with_loss else []),
        compiler_params=_cp(1),
    )(dxo, *([target] if with_loss else []), x_in, y, gate, ln_g)


def _mod_bwd_tail(du, dz_ref, x_ref, sc_ref, dx_ref, dsc_ref, dsh_ref, first):
    @pl.when(first)
    def _():
        dsc_ref[...] = jnp.zeros_like(dsc_ref)
        dsh_ref[...] = jnp.zeros_like(dsh_ref)

    dx_ref[...] = ALPHA * dz_ref[...] + du * (1.0 + sc_ref[...])
    dsc_ref[...] += jnp.sum(du * x_ref[...], axis=0, keepdims=True)
    dsh_ref[...] += jnp.sum(du, axis=0, keepdims=True)


def ffn_bwd(dy, hg, hu, wg, wu, wd, dz, x_in, scale, name, hosted=None):
    t, d = dy.shape
    c, _, fc = wg.shape
    nb = scale.shape[0]
    tm = TOKEN_TILE
    tps = (t // nb) // tm

    def body(dy_ref, hg_ref, hu_ref, wg_ref, wu_ref, wd_ref, dz_ref, x_ref, sc_ref,
             dhg_ref, dhu_ref, act_ref, dx_ref, dsc_ref, dsh_ref, acc_ref):
        i = pl.program_id(0)
        cc = pl.program_id(1)

        @pl.when(cc == 0)
        def _():
            acc_ref[...] = jnp.zeros_like(acc_ref)

        hgv = hg_ref[...].astype(F32)
        huv = hu_ref[...].astype(F32)
        da = _dot_nt(dy_ref[...], wd_ref[cc])
        sg = jax.nn.sigmoid(hgv)
        sl = hgv * sg
        act_ref[...] = (sl * huv).astype(BF16)
        dhu = (da * sl).astype(BF16)
        dhg = (da * huv * (sg * (1.0 + hgv * (1.0 - sg)))).astype(BF16)
        dhu_ref[...] = dhu
        dhg_ref[...] = dhg
        acc_ref[...] += _dot_nt(dhg, wg_ref[cc]) + _dot_nt(dhu, wu_ref[cc])

        @pl.when(cc == c - 1)
        def _():
            _mod_bwd_tail(acc_ref[...], dz_ref, x_ref, sc_ref, dx_ref, dsc_ref, dsh_ref, i % tps == 0)

    rows = pl.BlockSpec((tm, d), lambda i, cc: (i, 0))
    bvec = pl.BlockSpec((None, 1, d), lambda i, cc: (i // tps, 0, 0))
    hspec = pl.BlockSpec((None, tm, fc), lambda i, cc: (cc, i, 0))
    wcol = _resident(wg)
    return _call(
        body, name, (t // tm, c),
        [rows, hspec, hspec, wcol, wcol, _resident(wd), rows, rows, bvec],
        [hspec, hspec, hspec, rows, bvec, bvec],
        [_sds((c, t, fc), BF16), _sds((c, t, fc), BF16), _sds((c, t, fc), BF16), _sds((t, d), F32),
         _sds((nb, 1, d), F32), _sds((nb, 1, d), F32)],
        (dy, hg, hu, wg, wu, wd, dz, x_in, scale), scratch_shapes=[pltpu.VMEM((tm, d), F32)], hosted=hosted)


def linear_nt_mod_bwd(pairs, dz, x_in, scale, name, hosted=None):
    t, d = dz.shape
    nb = scale.shape[0]
    tm = TOKEN_TILE
    tps = (t // nb) // tm
    npairs = len(pairs)

    def body(*refs):
        dh_refs = refs[:npairs]
        w_refs = refs[npairs:2 * npairs]
        dz_ref, x_ref, sc_ref, dx_ref, dsc_ref, dsh_ref = refs[2 * npairs:]
        du = None
        for (_, _, blk), dh_ref, w_ref in zip(pairs, dh_refs, w_refs):
            dh = dh_ref[...].astype(BF16)
            term = _dot_nt(dh, w_ref[...]) if blk is None else _dot(dh, w_ref[...])
            du = term if du is None else du + term
        _mod_bwd_tail(du, dz_ref, x_ref, sc_ref, dx_ref, dsc_ref, dsh_ref, pl.program_id(0) % tps == 0)

    rows = pl.BlockSpec((tm, d), lambda i: (i, 0))
    bvec = pl.BlockSpec((None, 1, d), lambda i: (i // tps, 0, 0))
    in_specs = [pl.BlockSpec((tm, dh.shape[1]), lambda i: (i, 0)) for dh, _, _ in pairs]
    for dh, w, blk in pairs:
        if blk is None:
            in_specs.append(pl.BlockSpec(w.shape, lambda i: (0, 0)))
        else:
            in_specs.append(pl.BlockSpec((dh.shape[1], d), lambda i, blk=blk: (blk, 0)))
    in_specs += [rows, rows, bvec]
    return _call(
        body, name, (t // tm,), in_specs, [rows, bvec, bvec],
        [_sds((t, d), F32), _sds((nb, 1, d), F32), _sds((nb, 1, d), F32)],
        (*[dh for dh, _, _ in pairs], *[w for _, w, _ in pairs], dz, x_in, scale), hosted=hosted)


def linear_nt_delta(dy, w_o, o, head_sel, name):
    t, d = dy.shape
    hdv = w_o.shape[0]
    tm = TOKEN_TILE

    def body(dy_ref, w_ref, o_ref, sel_ref, do_ref, dl_ref):
        do = _dot_nt(dy_ref[...], w_ref[...])
        do_ref[...] = do.astype(BF16)
        dl_ref[...] = _dot_f32(do * o_ref[...].astype(F32), sel_ref[...])

    return pl.pallas_call(
        body, name=name, grid=(t // tm,),
        in_specs=[pl.BlockSpec((tm, d), lambda i: (i, 0)), pl.BlockSpec((hdv, d), lambda i: (0, 0)),
                  pl.BlockSpec((tm, hdv), lambda i: (i, 0)), pl.BlockSpec(head_sel.shape, lambda i: (0, 0))],
        out_specs=[pl.BlockSpec((tm, hdv), lambda i: (i, 0)), pl.BlockSpec((tm, 128), lambda i: (i, 0))],
        out_shape=[_sds((t, hdv), BF16), _sds((t, 128), F32)], compiler_params=_cp(1),
    )(dy, w_o, o, head_sel)


def _attn_bwd_blocks(j, nk, tk, scale, heads):
    def block(i, carry, masked):
        new = []
        for hd, (dk_acc, dv_acc, dfk_acc) in zip(heads, carry):
            qb = hd["q"](i)
            dob = hd["do"](i)
            lse_row, dl_row = hd["rows"](i)
            st = _dot_nt(hd["k"], qb)
            if scale is not None:
                st = st * scale
            if hd["bias"] is not None:
                fq_row, fk_col = hd["bias"](i)
                st = st + fq_row - fk_col
            if masked:
                keep = lax.broadcasted_iota(jnp.int32, st.shape, 1) >= lax.broadcasted_iota(jnp.int32, st.shape, 0)
                st = jnp.where(keep, st, -1e30)
            pt = jnp.exp(st - lse_row)
            dv_acc = dv_acc + _dot(pt.astype(BF16), dob)
            dst = pt * (_dot_nt(hd["v"], dob) - dl_row)
            if hd["add_dfq"] is not None:
                dfk_acc = dfk_acc - jnp.sum(dst, axis=1, keepdims=True)
                hd["add_dfq"](i, jnp.sum(dst, axis=0, keepdims=True))
            dsb = (dst if scale is None else dst * scale).astype(BF16)
            dk_acc = dk_acc + _dot(dsb, qb)
            hd["add_dq"](i, _dot_tn(dsb, hd["k"] if scale is not None else hd["k_scaled"]))
            new.append((dk_acc, dv_acc, dfk_acc))
        return tuple(new)

    init = tuple((jnp.zeros((tk, hd["k"].shape[1]), F32), jnp.zeros((tk, hd["v"].shape[1]), F32), jnp.zeros((tk, 1), F32))
                 for hd in heads)
    carry = block(j, init, True)
    return lax.fori_loop(j + 1, nk, lambda i, c: block(i, c, False), carry)


def fox_attn_bwd(qkv, do, cum, cum_rows, lse_rows, delta_rows, nb, name, hosted=None):
    t = qkv.shape[0]
    s = t // nb
    tk = ATTN_TILE
    nk = s // tk
    scale = FOX_HD ** -0.5

    def body(q_ref, k_ref, v_ref, do_ref, cum_ref, cr_ref, lr_ref, dr_ref, dq_ref, dk_ref, dv_ref, dfq_ref, dfk_ref):
        hg = pl.program_id(1)
        j = pl.program_id(2)

        @pl.when(j == 0)
        def _():
            dq_ref[...] = jnp.zeros_like(dq_ref)

        @pl.when((j == 0) & (hg == 0))
        def _():
            dfq_ref[...] = jnp.zeros_like(dfq_ref)
            dfk_ref[...] = jnp.zeros_like(dfk_ref)

        low = lax.broadcasted_iota(jnp.int32, (tk, 128), 1) < FOX_HD
        cum_t = cum_ref[...]

        def rows_of(i):
            return pl.ds(pl.multiple_of(i * tk, tk), tk)

        def head(a):
            hd = FOX_GROUP * hg + a
            cols = slice(128 * (a // 2), 128 * (a // 2) + 128)
            half = low if a % 2 == 0 else jnp.logical_not(low)
            kb = k_ref[:, cols]
            vb = v_ref[:, cols]
            fk = _pick_lane(cum_t, hd)

            def add_dq(i, val):
                dq_ref[rows_of(i), cols] += val

            def add_dfq(i, val):
                dfq_ref[i] = _put_row(dfq_ref[i], hd, val)

            ka = jnp.where(half, kb, jnp.zeros_like(kb))
            return dict(q=lambda i: q_ref[rows_of(i), cols] * scale, do=lambda i: do_ref[rows_of(i), cols],
                        k=ka, k_scaled=ka * scale, v=jnp.where(half, vb, jnp.zeros_like(vb)),
                        rows=lambda i: (_pick_row(lr_ref[i], hd), _pick_row(dr_ref[i], hd)),
                        bias=lambda i: (_pick_row(cr_ref[i], hd), fk), add_dq=add_dq, add_dfq=add_dfq)

        res = _attn_bwd_blocks(j, nk, tk, None, [head(a) for a in range(FOX_GROUP)])
        dk_ref[...] = jnp.concatenate([jnp.where(low, res[a][0], res[a + 1][0]) for a in range(0, FOX_GROUP, 2)],
                                      axis=1).astype(BF16)
        dv_ref[...] = jnp.concatenate([jnp.where(low, res[a][1], res[a + 1][1]) for a in range(0, FOX_GROUP, 2)],
                                      axis=1).astype(BF16)
        for a in range(FOX_GROUP):
            dfk_ref[j] = _put_row(dfk_ref[j], FOX_GROUP * hg + a, jnp.broadcast_to(res[a][2], (tk, 128)).T[0:1, :])

    wide = FOX_GROUP * FOX_HD
    ngroups = FOX_HEADS // FOX_GROUP
    rowsp = pl.BlockSpec((nk, 16, tk), lambda b, hg, j: (b, 0, 0))
    return _call(
        body, name, (nb, ngroups, nk),
        [pl.BlockSpec((s, wide), lambda b, hg, j: (b, hg)),
         pl.BlockSpec((tk, wide), lambda b, hg, j: (b * nk + j, ngroups + hg)),
         pl.BlockSpec((tk, wide), lambda b, hg, j: (b * nk + j, 2 * ngroups + hg)),
         pl.BlockSpec((s, wide), lambda b, hg, j: (b, hg)),
         pl.BlockSpec((tk, 128), lambda b, hg, j: (b * nk + j, 0)),
         rowsp, rowsp, rowsp],
        [pl.BlockSpec((s, wide), lambda b, hg, j: (b, hg)),
         pl.BlockSpec((tk, wide), lambda b, hg, j: (b * nk + j, hg)),
         pl.BlockSpec((tk, wide), lambda b, hg, j: (b * nk + j, hg)),
         rowsp, rowsp],
        [_sds((t, D_MODEL), F32), _sds((t, D_MODEL), BF16), _sds((t, D_MODEL), BF16),
         _sds((t // tk, 16, tk), F32), _sds((t // tk, 16, tk), F32)],
        (qkv, qkv, qkv, do, cum, cum_rows, lse_rows, delta_rows), hosted=hosted)


def mla_attn_bwd(q, kn, kr2, v, do, lse_rows, delta_rows, nb, name, hosted=None):
    t = q.shape[0]
    s = t // nb
    tk = ATTN_TILE
    nk = s // tk
    ngroups = MLA_HEADS // MLA_GROUP
    wide = MLA_GROUP * MLA_NOPE
    rwide = MLA_GROUP * MLA_ROPE
    scale = (MLA_NOPE + MLA_ROPE) ** -0.5

    def body(qn_ref, qr_ref, kn_ref, kr_ref, v_ref, do_ref, lr_ref, dr_ref, dqn_ref, dqr_ref, dkn_ref, dkr_ref, dv_ref):
        hg = pl.program_id(1)
        j = pl.program_id(2)

        @pl.when(j == 0)
        def _():
            dqn_ref[...] = jnp.zeros_like(dqn_ref)
            dqr_ref[...] = jnp.zeros_like(dqr_ref)

        low = lax.broadcasted_iota(jnp.int32, (tk, 128), 1) < MLA_ROPE
        kr = kr_ref[...]

        def rows_of(i):
            return pl.ds(pl.multiple_of(i * tk, tk), tk)

        def head(a):
            cols = slice(a * MLA_NOPE, (a + 1) * MLA_NOPE)
            rcols = slice(128 * (a // 2), 128 * (a // 2) + 128)
            mine = low if a % 2 == 0 else jnp.logical_not(low)
            hd = MLA_GROUP * hg + a

            def q_fn(i):
                qr = qr_ref[rows_of(i), rcols]
                return jnp.concatenate([qn_ref[rows_of(i), cols], jnp.where(mine, qr, jnp.zeros_like(qr))], axis=1)

            def add_dq(i, val):
                dqn_ref[rows_of(i), cols] += val[:, :MLA_NOPE]
                dqr_ref[rows_of(i), cols] += val[:, MLA_NOPE:]

            return dict(q=q_fn, do=lambda i: do_ref[rows_of(i), cols], k=jnp.concatenate([kn_ref[:, cols], kr], axis=1),
                        v=v_ref[:, cols], rows=lambda i: (_pick_row(lr_ref[i], hd), _pick_row(dr_ref[i], hd)),
                        bias=None, add_dq=add_dq, add_dfq=None)

        res = _attn_bwd_blocks(j, nk, tk, scale, [head(a) for a in range(MLA_GROUP)])
        dkn_ref[...] = jnp.concatenate([r[0][:, :MLA_NOPE] for r in res], axis=1).astype(BF16)
        dkr_ref[...] = jnp.concatenate([r[0][:, MLA_NOPE:] for r in res], axis=1).astype(BF16)
        dv_ref[...] = jnp.concatenate([r[1] for r in res], axis=1).astype(BF16)

    full = pl.BlockSpec((s, wide), lambda b, hg, j: (b, hg))
    blk = pl.BlockSpec((tk, wide), lambda b, hg, j: (b * nk + j, hg))
    rowsp = pl.BlockSpec((nk, 16, tk), lambda b, hg, j: (b, 0, 0))
    total = MLA_HEADS * MLA_V
    rope0 = MLA_HEADS * MLA_NOPE // rwide
    return _call(
        body, name, (nb, ngroups, nk),
        [full, pl.BlockSpec((s, rwide), lambda b, hg, j: (b, rope0 + hg)), blk,
         pl.BlockSpec((tk, 128), lambda b, hg, j: (b * nk + j, 0)), blk, full, rowsp, rowsp],
        [full, full, blk, blk, blk],
        [_sds((t, total), F32), _sds((t, total), F32), _sds((t, total), BF16), _sds((t, total), BF16),
         _sds((t, total), BF16)],
        (q, q, kn, kr2, v, do, lse_rows, delta_rows), hosted=hosted)


def mla_mid_bwd(dqn, dqr, dkn, dv, dkr_heads, h, g_q, g_kv, w_uq, w_uk, w_uv, cos8, sin8, cos64, sin64s, swap64,
                heads_to_rope, head_sum, name, hosted=None):
    t = h.shape[0]
    tm = TOKEN_TILE
    hq = MLA_HEADS * MLA_NOPE
    hr = MLA_HEADS * MLA_ROPE // 2
    nq = w_uq.shape[1]

    def body(dqn_ref, dqr_ref, dkn_ref, dv_ref, dkr_ref, h_ref, gq_ref, gkv_ref, wuq_ref, wuk_ref, wuv_ref,
             c8_ref, s8_ref, c64_ref, s64_ref, sw_ref, hp_ref, hs_ref, dh_ref, dqp_ref, dgq_ref, dgkv_ref):
        @pl.when(pl.program_id(0) == 0)
        def _():
            dgq_ref[...] = jnp.zeros_like(dgq_ref)
            dgkv_ref[...] = jnp.zeros_like(dgkv_ref)

        drot = _dot(dqr_ref[...].astype(BF16), hp_ref[...])
        o1 = drot[:, :hr]
        o2 = drot[:, hr:]
        cs = c8_ref[...]
        sn = s8_ref[...]
        dqp = jnp.concatenate([dqn_ref[...].astype(BF16), (o1 * cs + o2 * sn).astype(BF16),
                               (o2 * cs - o1 * sn).astype(BF16)], axis=1)
        dqp_ref[...] = dqp
        dcq = _dot_nt(dqp, wuq_ref[...])
        dckv = _dot_nt(dkn_ref[...], wuk_ref[...]) + _dot_nt(dv_ref[...], wuv_ref[...])
        hh = h_ref[...]

        def rms_bwd(hpart, g, dc, dg_ref):
            hhat, rstd = _rms(hpart, None)
            dg_ref[...] += jnp.sum(dc * hhat, axis=0, keepdims=True)
            dcg = dc * g
            return rstd * (dcg - hhat * jnp.mean(dcg * hhat, axis=-1, keepdims=True))

        dhq = rms_bwd(hh[:, :MLA_QR], gq_ref[...], dcq, dgq_ref)
        dhkv = rms_bwd(hh[:, MLA_QR:MLA_QR + MLA_KVR], gkv_ref[...], dckv, dgkv_ref)
        dkr = _dot(dkr_ref[...], hs_ref[...])
        dkr_pre = dkr * c64_ref[...] + _dot_f32(dkr * s64_ref[...], sw_ref[...])
        dh_ref[...] = jnp.concatenate([dhq, dhkv, dkr_pre], axis=1).astype(BF16)

    def rows(n):
        return pl.BlockSpec((tm, n), lambda i: (i, 0))

    def whole(a):
        return pl.BlockSpec(a.shape, lambda i: (0,) * a.ndim)

    return _call(
        body, name, (t // tm,),
        [rows(hq), rows(hq), rows(hq), rows(hq), rows(hq), rows(h.shape[1]), whole(g_q), whole(g_kv),
         whole(w_uq), whole(w_uk), whole(w_uv), rows(hr), rows(hr), rows(MLA_ROPE), rows(MLA_ROPE),
         whole(swap64), whole(heads_to_rope), whole(head_sum)],
        [rows(h.shape[1]), rows(nq), pl.BlockSpec((1, MLA_QR), lambda i: (0, 0)),
         pl.BlockSpec((1, MLA_KVR), lambda i: (0, 0))],
        [_sds((t, h.shape[1]), BF16), _sds((t, nq), BF16), _sds((1, MLA_QR), F32), _sds((1, MLA_KVR), F32)],
        (dqn, dqr, dkn, dv, dkr_heads, h, g_q, g_kv, w_uq, w_uk, w_uv, cos8, sin8, cos64, sin64s, swap64,
         heads_to_rope, head_sum), hosted=hosted)


def fox_gate_bwd(dcum, hf, b_f, triu, n_batch, name):
    t, n = hf.shape
    blk = triu.shape[0]
    nb = (t // n_batch) // blk

    def body(dc_ref, hf_ref, b_ref, tri_ref, o_ref, db_ref, carry_ref):
        @pl.when(pl.program_id(1) == 0)
        def _():
            carry_ref[...] = jnp.zeros_like(carry_ref)

        @pl.when((pl.program_id(0) == 0) & (pl.program_id(1) == 0))
        def _():
            db_ref[...] = jnp.zeros_like(db_ref)

        rc = _dot_f32(tri_ref[...], dc_ref[...]) + carry_ref[...]
        carry_ref[...] = rc[0:1, :]
        dhf = rc * jax.nn.sigmoid(-(hf_ref[...] + b_ref[...]))
        o_ref[...] = dhf.astype(BF16)
        db_ref[...] += jnp.sum(dhf, axis=0, keepdims=True)

    rev = pl.BlockSpec((blk, n), lambda bb, i: (bb * nb + nb - 1 - i, 0))
    return pl.pallas_call(
        body, name=name, grid=(n_batch, nb),
        in_specs=[rev, rev, pl.BlockSpec((1, n), lambda bb, i: (0, 0)), pl.BlockSpec((blk, blk), lambda bb, i: (0, 0))],
        out_specs=[rev, pl.BlockSpec((1, n), lambda bb, i: (0, 0))],
        out_shape=[_sds((t, n), BF16), _sds((1, n), F32)], scratch_shapes=[pltpu.VMEM((1, n), F32)],
        compiler_params=_cp(2),
    )(dcum, hf, b_f, triu)


def wgrad(a, bm, name, with_bf16=False, bt=WGRAD_TOKENS):
    ca, t, kd = a.shape
    cb, _, nd = bm.shape
    c = max(ca, cb)
    bn = nd
    if nd > 1024 and nd % 1024 == 0:
        bn = 1024
    nsteps = t // bt

    def body(a_ref, b_ref, o_ref, *rest):
        @pl.when(pl.program_id(2) == 0)
        def _():
            o_ref[...] = jnp.zeros_like(o_ref)

        o_ref[...] += _dot_tn(a_ref[...].astype(BF16), b_ref[...].astype(BF16))
        if with_bf16:
            @pl.when(pl.program_id(2) == nsteps - 1)
            def _():
                rest[0][...] = o_ref[...].astype(BF16)

    out_spec = pl.BlockSpec((None, kd, bn), lambda cc, n, tt: (cc, 0, n))
    res = pl.pallas_call(
        body, name=name, grid=(c, nd // bn, nsteps),
        in_specs=[pl.BlockSpec((None, bt, kd), lambda cc, n, tt: (cc if ca > 1 else 0, tt, 0)),
                  pl.BlockSpec((None, bt, bn), lambda cc, n, tt: (cc if cb > 1 else 0, tt, n))],
        out_specs=[out_spec, out_spec] if with_bf16 else out_spec,
        out_shape=[_sds((c, kd, nd), F32), _sds((c, kd, nd), BF16)] if with_bf16 else _sds((c, kd, nd), F32),
        compiler_params=_cp(3),
    )(a, bm)
    return res


def ada_mod_part(c_all, ada_w, name):
    nl, d, n = ada_w.shape
    rows = c_all.shape[0]
    tn = 512

    def body(c_ref, w_ref, o_ref):
        cv = c_ref[...]
        act = (cv * jax.nn.sigmoid(cv)).astype(BF16)
        o_ref[...] = _dot(act, w_ref[...].astype(BF16))

    return pl.pallas_call(
        body, name=name, grid=(nl, n // tn),
        in_specs=[pl.BlockSpec((rows, d), lambda l, j: (0, 0)), pl.BlockSpec((None, d, tn), lambda l, j: (l, 0, j))],
        out_specs=pl.BlockSpec((None, rows, tn), lambda l, j: (l, 0, j)),
        out_shape=_sds((nl, rows, n), F32), compiler_params=_cp(2),
    )(c_all, ada_w)


def ada_grad(c_all_t, dmod, name):
    nl, rows, n = dmod.shape
    d = c_all_t.shape[0]
    tn = 512

    def body(c_ref, dm_ref, o_ref):
        cv = c_ref[...]
        act = (cv * jax.nn.sigmoid(cv)).astype(BF16)
        o_ref[...] = _dot(act, dm_ref[...].astype(BF16))

    return pl.pallas_call(
        body, name=name, grid=(nl, n // tn),
        in_specs=[pl.BlockSpec((d, rows), lambda l, j: (0, 0)), pl.BlockSpec((None, rows, tn), lambda l, j: (l, 0, j))],
        out_specs=pl.BlockSpec((None, d, tn), lambda l, j: (l, 0, j)),
        out_shape=_sds((nl, d, n), F32), compiler_params=_cp(2),
    )(c_all_t, dmod)


def sum_leading(a, name):
    g, r, n = a.shape

    def body(a_ref, o_ref):
        acc = a_ref[0]
        for kk in range(1, g):
            acc = acc + a_ref[kk]
        o_ref[...] = acc

    return pl.pallas_call(
        body, name=name, grid=(1,), in_specs=[pl.BlockSpec((g, r, n), lambda i: (0, 0, 0))],
        out_specs=pl.BlockSpec((r, n), lambda i: (0, 0)), out_shape=_sds((r, n), F32), compiler_params=_cp(1),
    )(a)


def adamw(w, g, m, v, name):
    r, n = w.shape
    br = r
    for cand in (512, 256, 128, 64, 32, 16, 8):
        if r % cand == 0 and r > cand and cand * n * 4 <= ADAMW_BLOCK_BYTES:
            br = cand
            break
    c1 = 1.0 - ADAM_B1 ** ADAM_STEP
    c2 = 1.0 - ADAM_B2 ** ADAM_STEP

    def body(w_ref, g_ref, m_ref, v_ref, d_ref, mo_ref, vo_ref):
        gv = g_ref[...]
        mn = ADAM_B1 * m_ref[...] + (1.0 - ADAM_B1) * gv
        vn = ADAM_B2 * v_ref[...] + (1.0 - ADAM_B2) * (gv * gv)
        mo_ref[...] = mn
        vo_ref[...] = vn
        d_ref[...] = -ADAM_LR * ((mn / c1) / (jnp.sqrt(vn / c2) + ADAM_EPS) + ADAM_WD * w_ref[...])

    spec = pl.BlockSpec((br, n), lambda i: (i, 0))
    return _call(body, name, (r // br,), [spec] * 4, [spec] * 3, [_sds((r, n), F32)] * 3, (w, g, m, v))


def all_gather8(x_blk, name, hosted=None):
    m_per, n = x_blk.shape
    h_in = 0 if hosted is None else len(hosted.inputs)
    h_out = 0 if hosted is None else len(hosted.out_shape)

    def body(x_ref, *refs):
        c_in, (out_ref, *c_out), (send_sems, recv_sems, local_sem, *c_sem) = (
            refs[:h_in], refs[h_in:h_in + 1 + h_out], refs[h_in + 1 + h_out:])
        if hosted is not None:
            hosted.start(c_in, c_out, c_sem)
        gather(x_ref, out_ref, send_sems, recv_sems, local_sem)
        if hosted is not None:
            hosted.finish(c_in, c_out, c_sem)

    def gather(x_ref, out_ref, send_sems, recv_sems, local_sem):
        x, y, c = _place()
        me, sibling = (x, y, c), (x, y, 1 - c)
        chips = [(1 - x, y), (x, 1 - y), (1 - x, 1 - y)]

        def rows(px, py, pc):
            return out_ref.at[pl.ds((4 * px + 2 * py + pc) * m_per, m_per), :]

        def copy(k, block, to, src=None):
            return pltpu.make_async_remote_copy(
                src_ref=rows(*block) if src is None else src, dst_ref=rows(*block),
                send_sem=send_sems.at[k], recv_sem=recv_sems.at[k], device_id=to, device_id_type=MESH)

        mine = pltpu.make_async_copy(x_ref, rows(*me), local_sem)
        mine.start()
        first = [copy(0, me, sibling, src=x_ref)]
        first += [copy(1 + j, me, (*chip, c), src=x_ref) for j, chip in enumerate(chips)]
        for cp in first:
            cp.start()
        passed = [copy(4 + j, (*chip, c), sibling) for j, chip in enumerate(chips)]
        for j, chip in enumerate(chips):
            copy(1 + j, (*chip, c), me).wait_recv()
            passed[j].start()
        copy(0, sibling, me).wait_recv()
        for j, chip in enumerate(chips):
            copy(4 + j, (*chip, 1 - c), me).wait_recv()
        for cp in first + passed:
            cp.wait_send()
        mine.wait()

    hbm = pl.BlockSpec(memory_space=pl.ANY)
    vmem = pl.BlockSpec(memory_space=pltpu.VMEM)
    res = pl.pallas_call(
        body, name=name,
        out_shape=[_sds((8 * m_per, n), x_blk.dtype)] + ([] if hosted is None else list(hosted.out_shape)),
        in_specs=[vmem] + [hbm] * h_in, out_specs=[vmem] + [hbm] * h_out,
        scratch_shapes=[pltpu.SemaphoreType.DMA((7,)), pltpu.SemaphoreType.DMA((7,)), pltpu.SemaphoreType.DMA]
        + ([] if hosted is None else list(hosted.sems)),
        compiler_params=pltpu.CompilerParams(vmem_limit_bytes=VMEM_LIMIT),
    )(x_blk, *([] if hosted is None else hosted.inputs))
    return res[0] if hosted is None else (res[0], res[1:])


def _gather_comm(shards):
    nt = len(shards)

    def parts(w_refs, out_refs, sems, finishing):
        send_sems, recv_sems, own_send, own_recv = sems
        x, y, c = _place()
        sibling = (x, y, 1 - c)
        chips = [(1 - x, y), (x, 1 - y), (1 - x, 1 - y)]

        def copy(t, k, block, to, src=None):
            px, py, hh = block
            dst = out_refs[t].at[2 * px + py, hh]
            return pltpu.make_async_remote_copy(
                src_ref=dst if src is None else src, dst_ref=dst,
                send_sem=send_sems.at[6 * t + k], recv_sem=recv_sems.at[6 * t + k], device_id=to, device_id_type=MESH)

        own = [pltpu.make_async_remote_copy(
            src_ref=w_refs[t], dst_ref=out_refs[t].at[2 * x + y], send_sem=own_send.at[t], recv_sem=own_recv.at[t],
            device_id=sibling, device_id_type=MESH) for t in range(nt)]
        first = [copy(t, j, (x, y, c), (*chip, c), src=w_refs[t].at[c]) for t in range(nt) for j, chip in enumerate(chips)]
        if not finishing:
            return own, first
        landed = [copy(t, j, (*chip, c), (x, y, c)) for t in range(nt) for j, chip in enumerate(chips)]
        passed = [copy(t, 3 + j, (*chip, c), sibling) for t in range(nt) for j, chip in enumerate(chips)]
        from_sibling = [copy(t, 3 + j, (*chip, 1 - c), (x, y, c)) for t in range(nt) for j, chip in enumerate(chips)]
        return own, first, landed, passed, from_sibling

    def start(w_refs, out_refs, sems):
        own, first = parts(w_refs, out_refs, sems, False)
        for cp in own + first:
            cp.start()

    def finish(w_refs, out_refs, sems):
        own, first, landed, passed, from_sibling = parts(w_refs, out_refs, sems, True)
        for arrived, fwd in zip(landed, passed):
            arrived.wait_recv()
            fwd.start()
        for cp in from_sibling:
            cp.wait_recv()
        for cp in first + passed:
            cp.wait_send()
        for cp in own:
            cp.wait()

    sems = [pltpu.SemaphoreType.DMA((6 * nt,)), pltpu.SemaphoreType.DMA((6 * nt,)),
            pltpu.SemaphoreType.DMA((nt,)), pltpu.SemaphoreType.DMA((nt,))]
    return _Hosted(list(shards), [_sds((N_CHIPS, *w.shape), w.dtype) for w in shards], sems, start, finish)


def _row_block(r, n, itemsize):
    best = None
    for br in range(16, r + 1, 16):
        if r % br == 0 and br * n * itemsize <= COMM_BLOCK_BYTES:
            best = br
    return r if best is None else best


def _scatter_comm(parts):
    nt = len(parts)

    def copies(p_refs, b_refs, sems, arriving):
        send_sems, recv_sems = sems
        x, y, c = _place()
        me = 4 * x + 2 * y + c
        cps = []
        for t in range(nt):
            for r in range(1, 8):
                tx = 1 - x if r & 4 else x
                ty = 1 - y if r & 2 else y
                tc = 1 - c if r & 1 else c
                src, dst = (2 * x + y, c), 4 * tx + 2 * ty + tc
                if not arriving:
                    src, dst = (2 * tx + ty, tc), me
                cps.append(pltpu.make_async_remote_copy(
                    src_ref=p_refs[t].at[src], dst_ref=b_refs[t].at[dst], send_sem=send_sems.at[7 * t + r - 1],
                    recv_sem=recv_sems.at[7 * t + r - 1], device_id=(tx, ty, tc), device_id_type=MESH))
        return cps

    def start(p_refs, b_refs, sems):
        for cp in copies(p_refs, b_refs, sems, False):
            cp.start()

    def finish(p_refs, b_refs, sems):
        for cp in copies(p_refs, b_refs, sems, True):
            cp.wait_recv()
        for cp in copies(p_refs, b_refs, sems, False):
            cp.wait_send()

    sems = [pltpu.SemaphoreType.DMA((7 * nt,)), pltpu.SemaphoreType.DMA((7 * nt,))]
    return _Hosted(list(parts), [_sds((2 * N_CHIPS, *p.shape[2:]), p.dtype) for p in parts], sems, start, finish)


def sum_devices(own, recv, place, name, slot=(0, 1, None)):
    _, _, r, n = own.shape
    layer, n_layers, buf = slot
    br = _row_block(r, n, 4 * 8)

    def body(p_ref, o_ref, *rest):
        acc = o_ref[...]
        for kk in range(7):
            acc = acc + rest[kk][...].astype(F32)
        rest[-1][...] = acc

    def arrived(rel):
        return pl.BlockSpec((None, br, n), lambda i, pref: (jnp.bitwise_xor(pref[0], rel), i, 0))

    in_specs = [pl.BlockSpec((None, None, br, n), lambda i, pref: (pref[2], pref[1], i, 0))]
    in_specs += [arrived(rel) for rel in range(1, 8)]
    args = [own] + [recv] * 7
    aliases = {}
    if buf is not None:
        in_specs.append(pl.BlockSpec(memory_space=pl.ANY))
        args.append(buf)
        aliases = {9: 0}
    return pl.pallas_call(
        body, name=name,
        grid_spec=pltpu.PrefetchScalarGridSpec(
            num_scalar_prefetch=1, grid=(r // br,), in_specs=in_specs,
            out_specs=pl.BlockSpec((None, None, br, n), lambda i, pref: (layer, pref[1], i, 0))),
        out_shape=_sds((n_layers, 2, r, n), F32), input_output_aliases=aliases, compiler_params=_cp(1),
    )(place, *args)


def _join_comm(bufs):
    nt = len(bufs)
    layers = [bf.shape[0] for bf in bufs]
    first = [sum(layers[:t]) for t in range(nt)]

    def copies(o_refs, sems, own):
        send_sems, recv_sems = sems
        x, y, c = _place()
        hh = c if own else 1 - c
        return [pltpu.make_async_remote_copy(
            src_ref=o_refs[t].at[l, hh], dst_ref=o_refs[t].at[l, hh], send_sem=send_sems.at[first[t] + l],
            recv_sem=recv_sems.at[first[t] + l], device_id=(x, y, 1 - c), device_id_type=MESH)
            for t in range(nt) for l in range(layers[t])]

    def start(_, o_refs, sems):
        for cp in copies(o_refs, sems, True):
            cp.start()

    def finish(_, o_refs, sems):
        for cp in copies(o_refs, sems, False):
            cp.wait_recv()
        for cp in copies(o_refs, sems, True):
            cp.wait_send()

    sems = [pltpu.SemaphoreType.DMA((sum(layers),)), pltpu.SemaphoreType.DMA((sum(layers),))]
    return _Hosted(list(bufs), [_sds(bf.shape, bf.dtype) for bf in bufs], sems, start, finish, in_place=True)


def sibling_join_halves(bufs, name):
    comm = _join_comm(bufs)
    nt = len(bufs)

    def body(*refs):
        comm.start(refs[:nt], refs[nt:2 * nt], refs[2 * nt:])
        comm.finish(refs[:nt], refs[nt:2 * nt], refs[2 * nt:])

    hbm = pl.BlockSpec(memory_space=pl.ANY)
    return pl.pallas_call(body, name=name, out_shape=comm.out_shape, in_specs=[hbm] * nt, out_specs=[hbm] * nt,
                          input_output_aliases={k: k for k in range(nt)}, scratch_shapes=comm.sems)(*bufs)


_SHARD_KIND = {"mla_w_in": "rows", "mla_w_uq": "cols", "mla_w_uk": "cols", "mla_w_uv": "cols", "mla_w_o": "rows",
               "fox_w_in": "cols", "fox_w_o": "rows", "ffn_w_gate": "chunk", "ffn_w_up": "chunk", "ffn_w_down": "chunk"}
_PACKED = tuple(_SHARD_KIND)
_TRANSPOSED = ("ffn_w_gate", "ffn_w_up", "fox_w_in")


def _halves(shard):
    if shard.ndim == 3 and shard.shape[0] == 2:
        return shard
    r, n = shard.shape[-2:]
    return shard.reshape(2, r // 2, n)


def _cols_to_full(g):
    return jnp.transpose(g, (1, 0, 2)).reshape(g.shape[1], -1)


def _full_to_cols(w):
    k, n4 = w.shape
    return jnp.transpose(w.reshape(k, N_CHIPS, n4 // N_CHIPS), (1, 0, 2))


def _uq_perm():
    per = MLA_NOPE + MLA_ROPE
    half = MLA_ROPE // 2
    nope = [h * per + d for h in range(MLA_HEADS) for d in range(MLA_NOPE)]
    r1 = [h * per + MLA_NOPE + r for h in range(MLA_HEADS) for r in range(half)]
    r2 = [h * per + MLA_NOPE + half + r for h in range(MLA_HEADS) for r in range(half)]
    perm = np.array(nope + r1 + r2, dtype=np.int32)
    return perm, np.argsort(perm).astype(np.int32)


def _rope_matrices():
    half = MLA_ROPE // 2
    nr = MLA_HEADS * MLA_ROPE
    to_heads = np.zeros((nr, nr), np.float32)
    from_heads = np.zeros((MLA_HEADS * 128, nr), np.float32)
    for e in range(2):
        for h in range(MLA_HEADS):
            for r in range(half):
                to_heads[e * MLA_HEADS * half + h * half + r, h * MLA_ROPE + e * half + r] = 1.0
                from_heads[h * 128 + e * half + r, e * MLA_HEADS * half + h * half + r] = 1.0
    head_sum = np.tile(np.eye(MLA_ROPE, dtype=np.float32), (2 * MLA_HEADS, 1))
    dup = np.concatenate([np.eye(MLA_ROPE, dtype=np.float32)] * 2, axis=1)
    return to_heads, from_heads, head_sum, dup


def _ffn_weights(gathered):
    return tuple(g.reshape(N_CHIPS, 2 * g.shape[2], g.shape[3]) for g in gathered)


def _fox_weights(gathered):
    w_in, w_o = gathered
    w_in = jnp.transpose(w_in, (0, 2, 1, 3)).reshape(N_CHIPS * w_in.shape[2], 2 * w_in.shape[3])
    return w_in, w_o.reshape(-1, w_o.shape[-1])


def _local_step(x, positions, target, mods, wts, ln_g, ln_b, mla_g_q, mla_g_kv, fox_b_f, shards=None):
    nb, s, d = x.shape
    t = nb * s
    x0 = x.reshape(t, d)
    tgt = target.reshape(t, d)
    perm, inv_perm = _uq_perm()

    half = MLA_ROPE // 2
    inv_freq = ROPE_THETA ** (-jnp.arange(half, dtype=F32) / half)
    ang = positions.astype(F32).reshape(t, 1) * inv_freq
    cos, sin = jnp.cos(ang), jnp.sin(ang)
    cos8, sin8 = jnp.tile(cos, (1, MLA_HEADS)), jnp.tile(sin, (1, MLA_HEADS))
    cos64 = jnp.concatenate([cos, cos], axis=1)
    sin64s = jnp.concatenate([-sin, sin], axis=1)
    swap64 = jnp.asarray(np.roll(np.eye(MLA_ROPE, dtype=np.float32), half, axis=1))
    to_heads, from_heads, head_sum, dup = _rope_matrices()
    to_heads, from_heads = jnp.asarray(to_heads, dtype=BF16), jnp.asarray(from_heads, dtype=BF16)
    head_sum, dup = jnp.asarray(head_sum, dtype=BF16), jnp.asarray(dup, dtype=BF16)
    sel_mla = jnp.asarray(np.pad(np.kron(np.eye(MLA_HEADS, dtype=np.float32), np.ones((MLA_V, 1), np.float32)),
                                 ((0, 0), (0, 128 - MLA_HEADS))))
    sel_fox = jnp.asarray(np.pad(np.kron(np.eye(FOX_HEADS, dtype=np.float32), np.ones((FOX_HD, 1), np.float32)),
                                 ((0, 0), (0, 128 - FOX_HEADS))))
    tri = jnp.asarray(np.tril(np.ones((128, 128), np.float32)))
    triu = jnp.asarray(np.triu(np.ones((128, 128), np.float32)))
    onehot16 = jnp.asarray(np.eye(16, 128, dtype=np.float32))

    def vec(a):
        return a.reshape(1, -1)

    def carried(key):
        return None if shards is None else _gather_comm(shards[key])

    def split(res):
        return (res, None) if shards is None else res

    w_uq_p = wts["mla_w_uq"][:, perm]
    b_f_pad = jnp.pad(fox_b_f.reshape(1, -1), ((0, 0), (0, 128 - FOX_HEADS)))

    sh_a, sc_a, gt_a, sh_f, sc_f, gt_f = mods[0]
    h_in, u_m = mod_linear(x0, sh_a, sc_a, wts["mla_w_in"], F32, "mla_in", emit_u=True)
    q_m, kn_m, v_m, kr2_m, cq_m, ckv_m = mla_mid_fwd(
        h_in, vec(mla_g_q), vec(mla_g_kv), w_uq_p, wts["mla_w_uk"], wts["mla_w_uv"], cos8, sin8, cos64, sin64s, swap64,
        to_heads, dup, "mla_mid")
    (o_m, lse_m), got = split(mla_attn_fwd(q_m, kn_m, kr2_m, v_m, nb, "mla_attn", hosted=carried("ffn0")))
    ffn0_w = wts["ffn"][0] if got is None else _ffn_weights(got)
    y0, x1 = linear_resid_ln(o_m, wts["mla_w_o"], x0, gt_a, vec(ln_g[0, 0]), vec(ln_b[0, 0]), "mla_out")
    (u_f0, hg0, hu0, y1, x2), got = split(ffn_fwd(x1, sh_f, sc_f, gt_f, *ffn0_w, vec(ln_g[0, 1]), vec(ln_b[0, 1]), "ffn0",
                                                  hosted=carried("fox")))
    fox_w_in_t, fox_w_o = (wts["fox_w_in"].T, wts["fox_w_o"]) if got is None else _fox_weights(got)
    fox_w_f_t = jnp.pad(fox_w_in_t[3 * d:], ((0, 128 - FOX_HEADS), (0, 0)))
    sh_a1, sc_a1, gt_a1, sh_f1, sc_f1, gt_f1 = mods[1]
    qkv, u_x = mod_linear(x2, sh_a1, sc_a1, fox_w_in_t, BF16, "fox_qkv", tn=1024, emit_u=True, w_rows=3 * d)
    hf = mod_linear(x2, sh_a1, sc_a1, fox_w_f_t, F32, "fox_f", w_rows=128)
    cum = fox_gate_fwd(hf, b_f_pad, tri, nb, "fox_gate")
    cum_rows = rows16(cum, "fox_cum_rows")
    (o_x, lse_x), got = split(fox_attn_fwd(qkv, cum, cum_rows, nb, "fox_attn", hosted=carried("ffn1")))
    ffn1_w = wts["ffn"][1] if got is None else _ffn_weights(got)
    y2, x3 = linear_resid_ln(o_x, fox_w_o, x2, gt_a1, vec(ln_g[1, 0]), vec(ln_b[1, 0]), "fox_out")
    u_f1, hg1, hu1, y3, x4 = ffn_fwd(x3, sh_f1, sc_f1, gt_f1, *ffn1_w, vec(ln_g[1, 1]), vec(ln_b[1, 1]), "ffn1")

    parts, recv = {}, {}

    def halves_of(g):
        return g.reshape(N_CHIPS, 2, g.shape[1] // 2, g.shape[2])

    def scatter(keys, sent):
        return None if shards is None else _scatter_comm([sent[k] for k in keys])

    def landed(keys, got):
        if got is not None:
            recv.update(zip(keys, got))

    def ffn_grads(layer, u, dhg, dhu, act, dy):
        sent = {}
        for n, (a_op, b_op) in (("ffn_w_gate", (dhg, u[None])), ("ffn_w_up", (dhu, u[None])), ("ffn_w_down", (act, dy[None]))):
            g32, g16 = wgrad(a_op, b_op, "ffn%d_d%s" % (layer, n[4:]), with_bf16=True)
            parts["%s/%d" % (n, layer)], sent["%s/%d" % (n, layer)] = halves_of(g32), halves_of(g16)
        return sent

    dz3, dy3, dg11, db11, dgt_f1, sq_err = ln_bwd(x4, x3, y3, gt_f1, vec(ln_g[1, 1]), "ffn1_ln_bwd", target=tgt)
    loss_part = 0.5 * jnp.sum(sq_err) / d
    dhg1, dhu1, act1, dx3, dsc_f1, dsh_f1 = ffn_bwd(dy3, hg1, hu1, *ffn1_w, dz3, x3, sc_f1, "ffn1_bwd")
    sent = ffn_grads(1, u_f1, dhg1, dhu1, act1, dy3)
    dz2, dy2, dg10, db10, dgt_a1 = ln_bwd(dx3, x2, y2, gt_a1, vec(ln_g[1, 0]), "fox_ln_bwd")
    do_x, delta_x = linear_nt_delta(dy2, fox_w_o, o_x, sel_fox, "fox_out_bwd")
    (dq_x, dk_x, dv_x, dfq_x, dfk_x), got = split(fox_attn_bwd(
        qkv, do_x, cum, cum_rows, rows16(lse_x, "fox_lse_rows"), rows16(delta_x, "fox_delta_rows"), nb, "fox_attn_bwd",
        hosted=scatter(list(sent), sent)))
    landed(list(sent), got)
    dcum = tokens128(dfq_x + dfk_x, onehot16, "fox_dcum")
    dhf, dbf = fox_gate_bwd(dcum, hf, b_f_pad, triu, nb, "fox_gate_bwd")
    fox_d = [("q", dq_x), ("k", dk_x), ("v", dv_x)]
    dx2, dsc_a1, dsh_a1 = linear_nt_mod_bwd(
        [(dh, fox_w_in_t, i) for i, (_, dh) in enumerate(fox_d)] + [(dhf, fox_w_f_t, 0)], dz2, x2, sc_a1, "fox_in_bwd")
    dw_in_t = [wgrad(dh[None], u_x[None], "fox_dw" + tag)[0] for tag, dh in fox_d]
    dw_in_t.append(wgrad(dhf[None], u_x[None], "fox_dwf")[0][:FOX_HEADS])
    dw_in_t = jnp.concatenate(dw_in_t, axis=0).reshape(N_CHIPS, -1, 2, d // 2)
    parts["fox_w_in"] = jnp.transpose(dw_in_t, (0, 2, 1, 3))
    parts["fox_w_o"] = wgrad(o_x[None], dy2[None], "fox_dwo")[0].reshape(N_CHIPS, 2, -1, d)
    sent = {k: parts[k].astype(BF16) for k in ("fox_w_in", "fox_w_o")}
    dz1, dy1, dg01, db01, dgt_f0 = ln_bwd(dx2, x1, y1, gt_f, vec(ln_g[0, 1]), "ffn0_ln_bwd")
    (dhg0, dhu0, act0, dx1, dsc_f0, dsh_f0), got = split(ffn_bwd(dy1, hg0, hu0, *ffn0_w, dz1, x1, sc_f, "ffn0_bwd",
                                                                 hosted=scatter(list(sent), sent)))
    landed(list(sent), got)
    sent = ffn_grads(0, u_f0, dhg0, dhu0, act0, dy1)
    dz0, dy0, dg00, db00, dgt_a0 = ln_bwd(dx1, x0, y0, gt_a, vec(ln_g[0, 0]), "mla_ln_bwd")
    do_m, delta_m = linear_nt_delta(dy0, wts["mla_w_o"], o_m, sel_mla, "mla_out_bwd")
    parts["mla_w_o"] = wgrad(o_m[None], dy0[None], "mla_dwo")[0].reshape(N_CHIPS, 2, -1, d)
    (dqn_m, dqr_m, dkn_m, dkr_m, dv_m), got = split(mla_attn_bwd(
        q_m, kn_m, kr2_m, v_m, do_m, rows16(lse_m, "mla_lse_rows"), rows16(delta_m, "mla_delta_rows"), nb,
        "mla_attn_bwd", hosted=scatter(list(sent), sent)))
    landed(list(sent), got)
    sent = {"mla_w_o": parts["mla_w_o"].astype(BF16)}
    (dh_in, dq_pre, dgq, dgkv), got = split(mla_mid_bwd(
        dqn_m, dqr_m, dkn_m, dv_m, dkr_m, h_in, vec(mla_g_q), vec(mla_g_kv), w_uq_p, wts["mla_w_uk"],
        wts["mla_w_uv"], cos8, sin8, cos64, sin64s, swap64, from_heads, head_sum, "mla_mid_bwd",
        hosted=scatter(list(sent), sent)))
    landed(list(sent), got)
    parts["mla_w_uq"] = halves_of(_full_to_cols(wgrad(cq_m[None], dq_pre[None], "mla_dwuq")[0][:, inv_perm]))
    parts["mla_w_uk"] = halves_of(_full_to_cols(wgrad(ckv_m[None], dkn_m[None], "mla_dwuk")[0]))
    parts["mla_w_uv"] = halves_of(_full_to_cols(wgrad(ckv_m[None], dv_m[None], "mla_dwuv")[0]))
    parts["mla_w_in"] = wgrad(u_m[None], dh_in[None], "mla_dwin")[0].reshape(N_CHIPS, 2, -1, h_in.shape[1])
    sent = {k: parts[k].astype(BF16) for k in ("mla_w_in", "mla_w_uq", "mla_w_uk", "mla_w_uv")}
    (dx0, dsc_a0, dsh_a0), got = split(linear_nt_mod_bwd([(dh_in, wts["mla_w_in"], None)], dz0, x0, sc_a, "mla_in_bwd",
                                                         hosted=scatter(list(sent), sent)))
    landed(list(sent), got)

    dmods = [(dsh_a0, dsc_a0, dgt_a0, dsh_f0, dsc_f0, dgt_f0), (dsh_a1, dsc_a1, dgt_a1, dsh_f1, dsc_f1, dgt_f1)]
    d_ln_g = jnp.stack([jnp.concatenate([dg00, dg01], axis=0), jnp.concatenate([dg10, dg11], axis=0)])
    d_ln_b = jnp.stack([jnp.concatenate([db00, db01], axis=0), jnp.concatenate([db10, db11], axis=0)])
    return loss_part, dx0.reshape(nb, s, d), (parts, recv), dmods, d_ln_g, d_ln_b, dgq, dgkv, dbf[:, :FOX_HEADS]


def _pad_rows(a, rows):
    return jnp.pad(a, ((0, rows - a.shape[0]), (0, 0)))


def kernel(x, c, positions, mla_w_in, mla_g_q, mla_w_uq, mla_g_kv, mla_w_uk, mla_w_uv, mla_w_o, fox_w_in, fox_b_f, fox_w_o, ada_w, ada_b, ffn_w_gate, ffn_w_up, ffn_w_down, ln_g, ln_b, loss_target, m_mla_w_in, m_mla_g_q, m_mla_w_uq, m_mla_g_kv, m_mla_w_uk, m_mla_w_uv, m_mla_w_o, m_fox_w_in, m_fox_b_f, m_fox_w_o, m_ada_w, m_ada_b, m_ffn_w_gate, m_ffn_w_up, m_ffn_w_down, m_ln_g, m_ln_b, v_mla_w_in, v_mla_g_q, v_mla_w_uq, v_mla_g_kv, v_mla_w_uk, v_mla_w_uv, v_mla_w_o, v_fox_w_in, v_fox_b_f, v_fox_w_o, v_ada_w, v_ada_b, v_ffn_w_gate, v_ffn_w_up, v_ffn_w_down, v_ln_g, v_ln_b):
    args = dict(locals())
    nb, s, d = x.shape
    ax, ay, ac = lax.axis_index("x"), lax.axis_index("y"), lax.axis_index("c")
    chip = 2 * ax + ay
    dev = 2 * chip + ac
    n_dev = 2 * N_CHIPS
    n_all = nb * n_dev

    shard_shapes = {n: (args[n].shape if _SHARD_KIND[n] == "chunk" else args[n].shape[1:]) for n in _PACKED}

    def block(n, layer=None):
        w = args[n].reshape(shard_shapes[n]) if layer is None else args[n][layer]
        return _halves(w.astype(BF16))

    mla_names = [n for n in _PACKED if n.startswith("mla")]
    ffn_names = ("ffn_w_gate", "ffn_w_up", "ffn_w_down")
    fox_in_t = jnp.swapaxes(fox_w_in, 1, 2)[0].astype(BF16)
    fox_in_t = jnp.stack([fox_in_t[:, :d // 2], fox_in_t[:, d // 2:]])
    shards = {"ffn0": [block(n, 0) for n in ffn_names], "fox": [fox_in_t, block("fox_w_o")],
              "ffn1": [block(n, 1) for n in ffn_names]}

    ln_cols = ln_g.shape[-1]
    ln_blk = jnp.concatenate([ln_g.reshape(2 * DEPTH, ln_cols), ln_b.reshape(2 * DEPTH, ln_cols)], axis=0)
    early = jnp.concatenate([_pad_rows(c, 8), jnp.pad(_pad_rows(ln_blk, 8), ((0, 0), (0, d - ln_cols)))], axis=0)
    early, mla_all = all_gather8(early, "gather_c_ln_mla", hosted=_gather_comm([block(n) for n in mla_names]))
    wts = {}
    for n, g in zip(mla_names, mla_all):
        g = g.reshape(N_CHIPS, *shard_shapes[n])
        wts[n] = g.reshape(-1, g.shape[-1]) if _SHARD_KIND[n] == "rows" else _cols_to_full(g)
    early = early.reshape(n_dev, 16, d)
    c_all = early[:, :nb].reshape(n_all, d)
    ln_all = early.reshape(N_CHIPS, 2, 16, d)[:, 0, 8:8 + 4 * DEPTH, :ln_cols]
    ln_all = jnp.transpose(ln_all, (1, 0, 2)).reshape(4 * DEPTH, d)
    ln_g_full = ln_all[:2 * DEPTH].reshape(DEPTH, 2, d)
    ln_b_full = ln_all[2 * DEPTH:].reshape(DEPTH, 2, d)
    mod_part = ada_mod_part(c_all, ada_w, "ada_mod")
    ncol = mod_part.shape[-1]
    mod_g = all_gather8(mod_part.reshape(DEPTH * n_all, ncol), "gather_mod")
    mod_g = mod_g.reshape(N_CHIPS, 2, DEPTH, n_all, ncol)[:, 0]
    mod_full = jnp.transpose(mod_g, (1, 2, 0, 3)).reshape(DEPTH, n_all, N_CHIPS * ncol) + ada_b[:, None, :]
    mod_loc = lax.dynamic_slice_in_dim(mod_full, dev * nb, nb, axis=1)
    mods = [tuple(mod_loc[i, :, k * d:(k + 1) * d].reshape(nb, 1, d) for k in range(6)) for i in range(DEPTH)]

    loss_part, grad_x, (parts, recv), dmods, d_ln_g, d_ln_b, dgq, dgkv, dbf = _local_step(
        x, positions, loss_target, mods, wts, ln_g_full, ln_b_full, mla_g_q[0], mla_g_kv[0], fox_b_f[0], shards)
    loss = lax.psum(loss_part, ("x", "y", "c"))

    dmod_rows = jnp.stack([jnp.concatenate([v_.reshape(nb, d) for v_ in dm], axis=1) for dm in dmods])
    small = jnp.concatenate([
        d_ln_g.reshape(2 * DEPTH, d), d_ln_b.reshape(2 * DEPTH, d),
        jnp.pad(jnp.concatenate([dgq, dgkv, dbf], axis=1), ((0, 0), (0, d - 2 * MLA_QR - FOX_HEADS))),
        dmod_rows.reshape(DEPTH * nb * 6, d)], axis=0)
    n_small = small.shape[0]
    small_rows = -(-n_small // 8) * 8
    small_all = all_gather8(_pad_rows(small, small_rows), "gather_stats").reshape(n_dev, small_rows, d)
    stat_sum = sum_leading(small_all, "sum_stats")
    g_ln_g = lax.dynamic_slice_in_dim(stat_sum[:2 * DEPTH], chip * ln_cols, ln_cols, axis=1).reshape(DEPTH, 2, ln_cols)
    g_ln_b = lax.dynamic_slice_in_dim(stat_sum[2 * DEPTH:4 * DEPTH], chip * ln_cols, ln_cols, axis=1).reshape(DEPTH, 2, ln_cols)
    row = stat_sum[4 * DEPTH]
    g_gq = row[:MLA_QR].reshape(1, MLA_QR)
    g_gkv = row[MLA_QR:2 * MLA_QR].reshape(1, MLA_KVR)
    g_bf = row[2 * MLA_QR:2 * MLA_QR + FOX_HEADS].reshape(1, FOX_HEADS)
    base = 4 * DEPTH + 1
    dmod_all = small_all[:, base:base + DEPTH * nb * 6].reshape(n_dev, DEPTH, nb, 6 * d)
    dmod_all = jnp.transpose(dmod_all, (1, 0, 2, 3)).reshape(DEPTH, n_all, 6 * d)
    g_ada_b = sum_leading(jnp.transpose(dmod_all, (1, 0, 2)), "sum_ada_b")
    dmod_mine = lax.dynamic_slice_in_dim(dmod_all, chip * ncol, ncol, axis=2)
    g_ada_w = ada_grad(c_all.T, dmod_mine, "ada_grad")

    place = jnp.stack([dev, ac, chip]).astype(jnp.int32)
    bufs = []
    for n in _PACKED:
        if _SHARD_KIND[n] == "chunk":
            buf = None
            for layer in range(DEPTH):
                key = "%s/%d" % (n, layer)
                buf = sum_devices(parts[key], recv[key], place, "rs_sum_%s%d" % (n, layer), slot=(layer, DEPTH, buf))
        else:
            buf = sum_devices(parts[n], recv[n], place, "rs_sum_" + n)
        bufs.append(buf)
    joined = sibling_join_halves(bufs, "rs_join")
    g_big = {n: j.reshape(j.shape[0], 2 * j.shape[2], j.shape[3]) for n, j in zip(_PACKED, joined)}
    j = joined[_PACKED.index("fox_w_in")]
    g_big["fox_w_in"] = jnp.transpose(j, (0, 2, 1, 3)).reshape(1, j.shape[2], 2 * j.shape[3])

    g_out = {
        "mla_w_in": g_big["mla_w_in"], "mla_g_q": g_gq, "mla_w_uq": g_big["mla_w_uq"], "mla_g_kv": g_gkv,
        "mla_w_uk": g_big["mla_w_uk"], "mla_w_uv": g_big["mla_w_uv"], "mla_w_o": g_big["mla_w_o"],
        "fox_w_in": g_big["fox_w_in"], "fox_b_f": g_bf, "fox_w_o": g_big["fox_w_o"],
        "ada_w": g_ada_w, "ada_b": g_ada_b, "ffn_w_gate": g_big["ffn_w_gate"], "ffn_w_up": g_big["ffn_w_up"],
        "ffn_w_down": g_big["ffn_w_down"], "ln_g": g_ln_g, "ln_b": g_ln_b}
    names = ["mla_w_in", "mla_g_q", "mla_w_uq", "mla_g_kv", "mla_w_uk", "mla_w_uv", "mla_w_o", "fox_w_in", "fox_b_f",
             "fox_w_o", "ada_w", "ada_b", "ffn_w_gate", "ffn_w_up", "ffn_w_down", "ln_g", "ln_b"]
    small_names = ["mla_g_q", "mla_g_kv", "fox_b_f", "ada_b", "ln_g", "ln_b"]
    deltas, new_m, new_v = {}, {}, {}
    for n in names:
        if n in small_names:
            continue
        shp = args[n].shape
        if n in _TRANSPOSED:
            view = lambda a: jnp.swapaxes(a, 1, 2).reshape(-1, shp[1])
            back = lambda a: jnp.swapaxes(a.reshape(shp[0], shp[2], shp[1]), 1, 2)
        else:
            view = lambda a: a.reshape(-1, shp[-1])
            back = lambda a: a.reshape(shp)
        dl, mn, vn = adamw(view(args[n]), g_out[n].reshape(view(args[n]).shape), view(args["m_" + n]),
                           view(args["v_" + n]), "adamw_" + n)
        g_out[n], deltas[n], new_m[n], new_v[n] = back(g_out[n].reshape(view(args[n]).shape)), back(dl), back(mn), back(vn)

    def small_pack(prefix, src):
        flat = jnp.concatenate([src[prefix + n].reshape(-1) for n in small_names])
        size = -(-flat.shape[0] // (8 * 128)) * 8 * 128
        return jnp.pad(flat, (0, size - flat.shape[0])).reshape(-1, 128)

    sd, sm, sv = adamw(small_pack("", args), small_pack("", g_out), small_pack("m_", args), small_pack("v_", args),
                       "adamw_small")
    off = 0
    for n in small_names:
        shp = args[n].shape
        size = math.prod(shp)
        deltas[n] = sd.reshape(-1)[off:off + size].reshape(shp)
        new_m[n] = sm.reshape(-1)[off:off + size].reshape(shp)
        new_v[n] = sv.reshape(-1)[off:off + size].reshape(shp)
        off += size

    outs = [loss, grad_x]
    outs += [g_out[n].reshape(args[n].shape) for n in names]
    outs += [deltas[n] for n in names] + [new_m[n] for n in names] + [new_v[n] for n in names]
    return tuple(outs)
```

```python
import functools
import math

import numpy as np
import jax
import jax.numpy as jnp
from jax import lax
from jax.experimental import pallas as pl
from jax.experimental.pallas import tpu as pltpu

F32 = jnp.float32
BF16 = jnp.bfloat16
MESH = pl.DeviceIdType.MESH

D_MODEL = 1024
DEPTH = 2
MLA_HEADS = 8
MLA_NOPE = 128
MLA_ROPE = 64
MLA_V = 128
MLA_QR = 256
MLA_KVR = 256
ROPE_THETA = 10000.0
FOX_HEADS = 16
FOX_HD = 64
D_FF = 2816
N_CHIPS = 4
FF_CHUNK = D_FF // N_CHIPS
ALPHA = (2.0 * DEPTH) ** 0.25
EPS = 1e-5
ADAM_LR = 0.001
ADAM_B1 = 0.9
ADAM_B2 = 0.999
ADAM_EPS = 1e-08
ADAM_WD = 0.01
ADAM_STEP = 10

VMEM_LIMIT = 56 * 1024 * 1024
TOKEN_TILE = 512
WGRAD_TOKENS = 2048
ATTN_TILE = 512
FOX_GROUP = 8
MLA_GROUP = 4
COMM_BLOCK_BYTES = 8 * 1024 * 1024
ADAMW_BLOCK_BYTES = 2 * 1024 * 1024


def _cp(n_axes):
    return pltpu.CompilerParams(dimension_semantics=("arbitrary",) * n_axes, vmem_limit_bytes=VMEM_LIMIT)


def _dot(a, b):
    return jnp.dot(a, b, preferred_element_type=F32)


def _dot_nt(a, b):
    return lax.dot_general(a, b, (((1,), (1,)), ((), ())), preferred_element_type=F32)


def _dot_tn(a, b):
    return lax.dot_general(a, b, (((0,), (0,)), ((), ())), preferred_element_type=F32)


def _dot_f32(a, b):
    return jnp.dot(a, b, preferred_element_type=F32, precision=lax.Precision.HIGHEST)


def _sds(shape, dtype):
    return jax.ShapeDtypeStruct(shape, dtype)


def _place():
    return lax.axis_index("x"), lax.axis_index("y"), lax.axis_index("c")


class _Hosted:
    def __init__(self, inputs, out_shape, sems, start, finish, in_place=False):
        self.inputs, self.out_shape, self.sems, self.start, self.finish = inputs, out_shape, sems, start, finish
        self.in_place = in_place


def _call(body, name, grid, in_specs, out_specs, out_shape, args, scratch_shapes=(), hosted=None):
    in_specs, out_specs, out_shape, scratch_shapes = list(in_specs), list(out_specs), list(out_shape), list(scratch_shapes)
    if hosted is None:
        return pl.pallas_call(body, name=name, grid=grid, in_specs=in_specs, out_specs=out_specs, out_shape=out_shape,
                              scratch_shapes=scratch_shapes, compiler_params=_cp(len(grid)))(*args)
    n_in, n_out, n_scr = len(in_specs), len(out_specs), len(scratch_shapes)
    h_in, h_out = len(hosted.inputs), len(hosted.out_shape)

    def carried(*refs):
        o0 = n_in + h_in
        s0 = o0 + n_out + h_out
        c_in, c_out, c_sem = refs[n_in:o0], refs[o0 + n_out:s0], refs[s0 + n_scr:]
        ids = [pl.program_id(a) for a in range(len(grid))]
        first = functools.reduce(jnp.logical_and, [i == 0 for i in ids])
        last = functools.reduce(jnp.logical_and, [i == g - 1 for i, g in zip(ids, grid)])

        @pl.when(first)
        def _():
            hosted.start(c_in, c_out, c_sem)

        body(*refs[:n_in], *refs[o0:o0 + n_out], *refs[s0:s0 + n_scr])

        @pl.when(last)
        def _():
            hosted.finish(c_in, c_out, c_sem)

    hbm = pl.BlockSpec(memory_space=pl.ANY)
    aliases = {n_in + k: n_out + k for k in range(h_in)} if hosted.in_place else {}
    res = pl.pallas_call(
        carried, name=name, grid=grid, in_specs=in_specs + [hbm] * h_in, out_specs=out_specs + [hbm] * h_out,
        out_shape=out_shape + list(hosted.out_shape), scratch_shapes=scratch_shapes + list(hosted.sems),
        input_output_aliases=aliases, compiler_params=_cp(len(grid)))(*args, *hosted.inputs)
    return res[:n_out], res[n_out:]


def mod_linear(x, shift, scale, w, out_dtype, name, tn=None, emit_u=False, w_rows=None):
    t, d = x.shape
    n = w.shape[1] if w_rows is None else w_rows
    tn = n if tn is None else tn
    tm = TOKEN_TILE
    tps = (t // shift.shape[0]) // tm

    def body(x_ref, sh_ref, sc_ref, w_ref, o_ref, *rest):
        u = (x_ref[...] * (1.0 + sc_ref[...]) + sh_ref[...]).astype(BF16)
        o_ref[...] = (_dot(u, w_ref[...]) if w_rows is None else _dot_nt(u, w_ref[...])).astype(out_dtype)
        if emit_u:
            @pl.when(pl.program_id(1) == 0)
            def _():
                rest[0][...] = u

    vec = pl.BlockSpec((None, 1, d), lambda i, j: (i // tps, 0, 0))
    out_shape = [_sds((t, n), out_dtype)]
    out_specs = [pl.BlockSpec((tm, tn), lambda i, j: (i, j))]
    if emit_u:
        out_shape.append(_sds((t, d), BF16))
        out_specs.append(pl.BlockSpec((tm, d), lambda i, j: (i, 0)))
    w_spec = pl.BlockSpec((d, tn), lambda i, j: (0, j)) if w_rows is None else pl.BlockSpec((tn, d), lambda i, j: (j, 0))
    res = pl.pallas_call(
        body, name=name, grid=(t // tm, n // tn),
        in_specs=[pl.BlockSpec((tm, d), lambda i, j: (i, 0)), vec, vec, w_spec],
        out_specs=out_specs, out_shape=out_shape, compiler_params=_cp(2),
    )(x, shift, scale, w)
    return res if emit_u else res[0]


def _rms(h, g):
    rstd = lax.rsqrt(jnp.mean(h * h, axis=-1, keepdims=True) + EPS)
    return h * rstd, rstd


def mla_mid_fwd(h, g_q, g_kv, w_uq, w_uk, w_uv, cos8, sin8, cos64, sin64s, swap64, rope_to_heads, dup64, name):
    t = h.shape[0]
    tm = TOKEN_TILE
    hq = MLA_HEADS * MLA_NOPE
    hr = MLA_HEADS * MLA_ROPE // 2

    def body(h_ref, gq_ref, gkv_ref, wuq_ref, wuk_ref, wuv_ref, c8_ref, s8_ref, c64_ref, s64_ref, sw_ref, p_ref, d_ref,
             q_ref, kn_ref, v_ref, kr_ref, cq_ref, ckv_ref):
        hh = h_ref[...]
        cq = (_rms(hh[:, :MLA_QR], None)[0] * gq_ref[...]).astype(BF16)
        ckv = (_rms(hh[:, MLA_QR:MLA_QR + MLA_KVR], None)[0] * gkv_ref[...]).astype(BF16)
        cq_ref[...] = cq
        ckv_ref[...] = ckv
        q = _dot(cq, wuq_ref[...])
        x1 = q[:, hq:hq + hr]
        x2 = q[:, hq + hr:]
        cs = c8_ref[...]
        sn = s8_ref[...]
        rot = jnp.concatenate([x1 * cs - x2 * sn, x2 * cs + x1 * sn], axis=1).astype(BF16)
        q_ref[...] = jnp.concatenate([q[:, :hq].astype(BF16), _dot(rot, p_ref[...]).astype(BF16)], axis=1)
        kn_ref[...] = _dot(ckv, wuk_ref[...]).astype(BF16)
        v_ref[...] = _dot(ckv, wuv_ref[...]).astype(BF16)
        kr = hh[:, MLA_QR + MLA_KVR:]
        kr = (kr * c64_ref[...] + _dot_f32(kr, sw_ref[...]) * s64_ref[...]).astype(BF16)
        kr_ref[...] = _dot(kr, d_ref[...]).astype(BF16)

    def rows(n):
        return pl.BlockSpec((tm, n), lambda i: (i, 0))

    def whole(a):
        return pl.BlockSpec(a.shape, lambda i: (0,) * a.ndim)

    nq = w_uq.shape[1]
    return pl.pallas_call(
        body, name=name, grid=(t // tm,),
        in_specs=[rows(h.shape[1]), whole(g_q), whole(g_kv), whole(w_uq), whole(w_uk), whole(w_uv),
                  rows(hr), rows(hr), rows(MLA_ROPE), rows(MLA_ROPE), whole(swap64), whole(rope_to_heads), whole(dup64)],
        out_specs=[rows(nq), rows(hq), rows(hq), rows(2 * MLA_ROPE), rows(MLA_QR), rows(MLA_KVR)],
        out_shape=[_sds((t, nq), BF16), _sds((t, hq), BF16), _sds((t, hq), BF16), _sds((t, 2 * MLA_ROPE), BF16),
                   _sds((t, MLA_QR), BF16), _sds((t, MLA_KVR), BF16)],
        compiler_params=_cp(1),
    )(h, g_q, g_kv, w_uq, w_uk, w_uv, cos8, sin8, cos64, sin64s, swap64, rope_to_heads, dup64)


def _pick_lane(tile, idx):
    lane = lax.broadcasted_iota(jnp.int32, tile.shape, 1)
    return jnp.sum(jnp.where(lane == idx, tile, 0.0), axis=1, keepdims=True)


def _pick_row(tile, idx):
    row = lax.broadcasted_iota(jnp.int32, tile.shape, 0)
    return jnp.sum(jnp.where(row == idx, tile, 0.0), axis=0, keepdims=True)


def _put_lane(tile, idx, col):
    lane = lax.broadcasted_iota(jnp.int32, tile.shape, 1)
    return jnp.where(lane == idx, col, tile)


def _put_row(tile, idx, row):
    r = lax.broadcasted_iota(jnp.int32, tile.shape, 0)
    return tile + jnp.where(r == idx, row, 0.0)


def _causal_softmax_blocks(i, tq, heads):
    def block(j, carry, masked):
        new = []
        for (score_fn, pv_fn, _), (m, l, acc) in zip(heads, carry):
            sc = score_fn(j)
            if masked:
                keep = lax.broadcasted_iota(jnp.int32, sc.shape, 0) >= lax.broadcasted_iota(jnp.int32, sc.shape, 1)
                sc = jnp.where(keep, sc, -1e30)
            m_new = jnp.maximum(m, jnp.max(sc, axis=1, keepdims=True))
            a = jnp.exp(m - m_new)
            p = jnp.exp(sc - m_new)
            new.append((m_new, a * l + jnp.sum(p, axis=1, keepdims=True), a * acc + pv_fn(j, p.astype(BF16))))
        return tuple(new)

    init = tuple((jnp.full((tq, 1), -1e30, F32), jnp.zeros((tq, 1), F32), jnp.zeros((tq, dv), F32)) for _, _, dv in heads)
    carry = lax.fori_loop(0, i, lambda j, c: block(j, c, False), init)
    return [(acc / l, m + jnp.log(l)) for m, l, acc in block(i, carry, True)]


def fox_attn_fwd(qkv, cum, cum_rows, nb, name, hosted=None):
    t = qkv.shape[0]
    s = t // nb
    tq = ATTN_TILE
    nq = s // tq
    wide = FOX_GROUP * FOX_HD
    ngroups = FOX_HEADS // FOX_GROUP
    scale = FOX_HD ** -0.5

    def body(q_ref, k_ref, v_ref, cum_ref, cr_ref, o_ref, lse_ref):
        i = pl.program_id(1)
        hg = pl.program_id(2)

        @pl.when(hg == 0)
        def _():
            lse_ref[...] = jnp.zeros_like(lse_ref)

        low = lax.broadcasted_iota(jnp.int32, (tq, 128), 1) < FOX_HD
        cum_t = cum_ref[...]

        def rows_of(j):
            return pl.ds(pl.multiple_of(j * tq, tq), tq)

        def head(a):
            hd = FOX_GROUP * hg + a
            cols = slice(128 * (a // 2), 128 * (a // 2) + 128)
            q = q_ref[:, cols]
            qa = jnp.where(low if a % 2 == 0 else jnp.logical_not(low), q, jnp.zeros_like(q)) * scale
            fq = _pick_lane(cum_t, hd)
            return (lambda j: _dot_nt(qa, k_ref[rows_of(j), cols]) + fq - _pick_row(cr_ref[j], hd),
                    lambda j, p: _dot(p, v_ref[rows_of(j), cols]), 2 * FOX_HD)

        res = _causal_softmax_blocks(i, tq, [head(a) for a in range(FOX_GROUP)])
        o_ref[...] = jnp.concatenate([jnp.where(low, res[a][0], res[a + 1][0]) for a in range(0, FOX_GROUP, 2)],
                                     axis=1).astype(BF16)
        lse_t = lse_ref[...]
        for a in range(FOX_GROUP):
            lse_t = _put_lane(lse_t, FOX_GROUP * hg + a, res[a][1])
        lse_ref[...] = lse_t

    return _call(
        body, name, (nb, nq, ngroups),
        [pl.BlockSpec((tq, wide), lambda b, i, hg: (b * nq + i, hg)),
         pl.BlockSpec((s, wide), lambda b, i, hg: (b, ngroups + hg)),
         pl.BlockSpec((s, wide), lambda b, i, hg: (b, 2 * ngroups + hg)),
         pl.BlockSpec((tq, 128), lambda b, i, hg: (b * nq + i, 0)),
         pl.BlockSpec((nq, 16, tq), lambda b, i, hg: (b, 0, 0))],
        [pl.BlockSpec((tq, wide), lambda b, i, hg: (b * nq + i, hg)),
         pl.BlockSpec((tq, 128), lambda b, i, hg: (b * nq + i, 0))],
        [_sds((t, D_MODEL), BF16), _sds((t, 128), F32)], (qkv, qkv, qkv, cum, cum_rows), hosted=hosted)


def mla_attn_fwd(q, kn, kr2, v, nb, name, hosted=None):
    t = q.shape[0]
    s = t // nb
    tq = ATTN_TILE
    nq = s // tq
    ngroups = MLA_HEADS // MLA_GROUP
    wide = MLA_GROUP * MLA_NOPE
    rwide = MLA_GROUP * MLA_ROPE
    scale = (MLA_NOPE + MLA_ROPE) ** -0.5

    def body(qn_ref, qr_ref, kn_ref, kr_ref, v_ref, o_ref, lse_ref):
        i = pl.program_id(1)
        hg = pl.program_id(2)

        @pl.when(hg == 0)
        def _():
            lse_ref[...] = jnp.zeros_like(lse_ref)

        low = lax.broadcasted_iota(jnp.int32, (tq, 128), 1) < MLA_ROPE

        def rows_of(j):
            return pl.ds(pl.multiple_of(j * tq, tq), tq)

        def head(a):
            cols = slice(a * MLA_NOPE, (a + 1) * MLA_NOPE)
            qr = qr_ref[:, 128 * (a // 2):128 * (a // 2) + 128]
            q_cat = jnp.concatenate([qn_ref[:, cols], jnp.where(low if a % 2 == 0 else jnp.logical_not(low), qr,
                                                                jnp.zeros_like(qr))], axis=1)
            return (lambda j: _dot_nt(q_cat, jnp.concatenate([kn_ref[rows_of(j), cols], kr_ref[rows_of(j), :]], axis=1)) * scale,
                    lambda j, p: _dot(p, v_ref[rows_of(j), cols]), MLA_V)

        res = _causal_softmax_blocks(i, tq, [head(a) for a in range(MLA_GROUP)])
        o_ref[...] = jnp.concatenate([r[0] for r in res], axis=1).astype(BF16)
        lse_t = lse_ref[...]
        for a in range(MLA_GROUP):
            lse_t = _put_lane(lse_t, MLA_GROUP * hg + a, res[a][1])
        lse_ref[...] = lse_t

    rope0 = MLA_HEADS * MLA_NOPE // rwide
    return _call(
        body, name, (nb, nq, ngroups),
        [pl.BlockSpec((tq, wide), lambda b, i, hg: (b * nq + i, hg)),
         pl.BlockSpec((tq, rwide), lambda b, i, hg: (b * nq + i, rope0 + hg)),
         pl.BlockSpec((s, wide), lambda b, i, hg: (b, hg)),
         pl.BlockSpec((s, 128), lambda b, i, hg: (b, 0)),
         pl.BlockSpec((s, wide), lambda b, i, hg: (b, hg))],
        [pl.BlockSpec((tq, wide), lambda b, i, hg: (b * nq + i, hg)),
         pl.BlockSpec((tq, 128), lambda b, i, hg: (b * nq + i, 0))],
        [_sds((t, MLA_HEADS * MLA_V), BF16), _sds((t, 128), F32)], (q, q, kn, kr2, v), hosted=hosted)


def rows16(a, name):
    t = a.shape[0]
    tq = ATTN_TILE

    def body(a_ref, o_ref):
        o_ref[...] = a_ref[...].T[:16, :]

    return pl.pallas_call(
        body, name=name, grid=(t // tq,), in_specs=[pl.BlockSpec((tq, 128), lambda n: (n, 0))],
        out_specs=pl.BlockSpec((None, 16, tq), lambda n: (n, 0, 0)), out_shape=_sds((t // tq, 16, tq), F32),
        compiler_params=_cp(1),
    )(a)


def tokens128(rows, onehot, name):
    nblk, _, tq = rows.shape

    def body(r_ref, e_ref, o_ref):
        o_ref[...] = lax.dot_general(r_ref[...], e_ref[...], (((0,), (0,)), ((), ())), preferred_element_type=F32,
                                     precision=lax.Precision.HIGHEST)

    return pl.pallas_call(
        body, name=name, grid=(nblk,),
        in_specs=[pl.BlockSpec((None, 16, tq), lambda n: (n, 0, 0)), pl.BlockSpec((16, 128), lambda n: (0, 0))],
        out_specs=pl.BlockSpec((tq, 128), lambda n: (n, 0)), out_shape=_sds((nblk * tq, 128), F32),
        compiler_params=_cp(1),
    )(rows, onehot)


def _layer_norm(z, g, b):
    mu = jnp.mean(z, axis=-1, keepdims=True)
    zc = z - mu
    rstd = lax.rsqrt(jnp.mean(zc * zc, axis=-1, keepdims=True) + EPS)
    xhat = zc * rstd
    return xhat * g + b, xhat, rstd


def linear_resid_ln(a, w, x_in, gate, ln_g, ln_b, name):
    t, kdim = a.shape
    d = w.shape[1]
    tm = TOKEN_TILE
    tps = (t // gate.shape[0]) // tm

    def body(a_ref, w_ref, x_ref, gt_ref, g_ref, b_ref, y_ref, xo_ref):
        y = _dot(a_ref[...], w_ref[...])
        y_ref[...] = y
        z = ALPHA * x_ref[...] + (1.0 + gt_ref[...]) * y
        xo_ref[...] = _layer_norm(z, g_ref[...], b_ref[...])[0]

    rows = pl.BlockSpec((tm, d), lambda i: (i, 0))
    vec = pl.BlockSpec((1, d), lambda i: (0, 0))
    return pl.pallas_call(
        body, name=name, grid=(t // tm,),
        in_specs=[pl.BlockSpec((tm, kdim), lambda i: (i, 0)), pl.BlockSpec((kdim, d), lambda i: (0, 0)), rows,
                  pl.BlockSpec((None, 1, d), lambda i: (i // tps, 0, 0)), vec, vec],
        out_specs=[rows, rows], out_shape=[_sds((t, d), F32), _sds((t, d), F32)],
        compiler_params=_cp(1),
    )(a, w, x_in, gate, ln_g, ln_b)


def _resident(a):
    return pl.BlockSpec(a.shape, lambda *_: (0,) * a.ndim, pipeline_mode=pl.Buffered(1))


def ffn_fwd(x_in, shift, scale, gate, wg, wu, wd, ln_g, ln_b, name, hosted=None):
    t, d = x_in.shape
    c, _, fc = wg.shape
    tm = TOKEN_TILE
    tps = (t // gate.shape[0]) // tm

    def body(x_ref, sh_ref, sc_ref, gt_ref, wg_ref, wu_ref, wd_ref, g_ref, b_ref,
             u_ref, hg_ref, hu_ref, y_ref, xo_ref, acc_ref):
        cc = pl.program_id(1)

        @pl.when(cc == 0)
        def _():
            u_ref[...] = (x_ref[...] * (1.0 + sc_ref[...]) + sh_ref[...]).astype(BF16)
            acc_ref[...] = jnp.zeros_like(acc_ref)

        u = u_ref[...]
        hg = _dot(u, wg_ref[cc])
        hu = _dot(u, wu_ref[cc])
        hg_ref[...] = hg.astype(BF16)
        hu_ref[...] = hu.astype(BF16)
        act = (hg * jax.nn.sigmoid(hg) * hu).astype(BF16)
        acc_ref[...] += _dot(act, wd_ref[cc])

        @pl.when(cc == c - 1)
        def _():
            y = acc_ref[...]
            y_ref[...] = y
            z = ALPHA * x_ref[...] + (1.0 + gt_ref[...]) * y
            xo_ref[...] = _layer_norm(z, g_ref[...], b_ref[...])[0]

    rows = pl.BlockSpec((tm, d), lambda i, cc: (i, 0))
    bvec = pl.BlockSpec((None, 1, d), lambda i, cc: (i // tps, 0, 0))
    vec = pl.BlockSpec((1, d), lambda i, cc: (0, 0))
    hspec = pl.BlockSpec((None, tm, fc), lambda i, cc: (cc, i, 0))
    wcol = _resident(wg)
    return _call(
        body, name, (t // tm, c),
        [rows, bvec, bvec, bvec, wcol, wcol, _resident(wd), vec, vec],
        [rows, hspec, hspec, rows, rows],
        [_sds((t, d), BF16), _sds((c, t, fc), BF16), _sds((c, t, fc), BF16), _sds((t, d), F32), _sds((t, d), F32)],
        (x_in, shift, scale, gate, wg, wu, wd, ln_g, ln_b), scratch_shapes=[pltpu.VMEM((tm, d), F32)], hosted=hosted)


def fox_gate_fwd(hf, b_f, tri, n_batch, name):
    t, n = hf.shape
    blk = tri.shape[0]
    nb = (t // n_batch) // blk

    def body(hf_ref, b_ref, tri_ref, o_ref, carry_ref):
        @pl.when(pl.program_id(1) == 0)
        def _():
            carry_ref[...] = jnp.zeros_like(carry_ref)

        xx = hf_ref[...] + b_ref[...]
        lf = jnp.minimum(xx, 0.0) - jnp.log(1.0 + jnp.exp(-jnp.abs(xx)))
        cum = _dot_f32(tri_ref[...], lf) + carry_ref[...]
        o_ref[...] = cum
        carry_ref[...] = cum[blk - 1:blk, :]

    return pl.pallas_call(
        body, name=name, grid=(n_batch, nb),
        in_specs=[pl.BlockSpec((blk, n), lambda bb, i: (bb * nb + i, 0)), pl.BlockSpec((1, n), lambda bb, i: (0, 0)),
                  pl.BlockSpec((blk, blk), lambda bb, i: (0, 0))],
        out_specs=pl.BlockSpec((blk, n), lambda bb, i: (bb * nb + i, 0)),
        out_shape=_sds((t, n), F32), scratch_shapes=[pltpu.VMEM((1, n), F32)],
        compiler_params=_cp(2),
    )(hf, b_f, tri)


def ln_bwd(dxo, x_in, y, gate, ln_g, name, target=None):
    t, d = dxo.shape
    nb = gate.shape[0]
    tm = TOKEN_TILE
    tps = (t // nb) // tm
    with_loss = target is not None

    def body(dxo_ref, *refs):
        if with_loss:
            t_ref, x_ref, y_ref, gt_ref, g_ref, dz_ref, dy_ref, dg_ref, db_ref, dgt_ref, l_ref = refs
        else:
            x_ref, y_ref, gt_ref, g_ref, dz_ref, dy_ref, dg_ref, db_ref, dgt_ref = refs
        i = pl.program_id(0)

        @pl.when(i == 0)
        def _():
            dg_ref[...] = jnp.zeros_like(dg_ref)
            db_ref[...] = jnp.zeros_like(db_ref)
            if with_loss:
                l_ref[...] = jnp.zeros_like(l_ref)

        @pl.when(i % tps == 0)
        def _():
            dgt_ref[...] = jnp.zeros_like(dgt_ref)

        yy = y_ref[...]
        g1 = 1.0 + gt_ref[...]
        z = ALPHA * x_ref[...] + g1 * yy
        _, xhat, rstd = _layer_norm(z, 1.0, 0.0)
        dxo_v = dxo_ref[...]
        if with_loss:
            err = dxo_v - t_ref[...]
            l_ref[...] += jnp.sum(err * err, axis=0, keepdims=True)
            dxo_v = err / d
        dg_ref[...] += jnp.sum(dxo_v * xhat, axis=0, keepdims=True)
        db_ref[...] += jnp.sum(dxo_v, axis=0, keepdims=True)
        dxh = dxo_v * g_ref[...]
        dz = rstd * (dxh - jnp.mean(dxh, axis=-1, keepdims=True) - xhat * jnp.mean(dxh * xhat, axis=-1, keepdims=True))
        dz_ref[...] = dz
        dy_ref[...] = (g1 * dz).astype(BF16)
        dgt_ref[...] += jnp.sum(dz * yy, axis=0, keepdims=True)

    rows = pl.BlockSpec((tm, d), lambda i: (i, 0))
    vec = pl.BlockSpec((1, d), lambda i: (0, 0))
    bvec = pl.BlockSpec((None, 1, d), lambda i: (i // tps, 0, 0))
    return pl.pallas_call(
        body, name=name, grid=(t // tm,), in_specs=[rows] * (4 if with_loss else 3) + [bvec, vec],
        out_specs=[rows, rows, vec, vec, bvec] + ([vec] if with_loss else []),
        out_shape=[_sds((t, d), F32), _sds((t, d), BF16), _sds((1, d), F32), _sds((1, d), F32), _sds((nb, 1, d), F32)]
        + ([_sds((1, d), F32)] if with_loss else []),
        compiler_params=_cp(1),
    )(dxo, *([target] if with_loss else []), x_in, y, gate, ln_g)


def _mod_bwd_tail(du, dz_ref, x_ref, sc_ref, dx_ref, dsc_ref, dsh_ref, first):
    @pl.when(first)
    def _():
        dsc_ref[...] = jnp.zeros_like(dsc_ref)
        dsh_ref[...] = jnp.zeros_like(dsh_ref)

    dx_ref[...] = ALPHA * dz_ref[...] + du * (1.0 + sc_ref[...])
    dsc_ref[...] += jnp.sum(du * x_ref[...], axis=0, keepdims=True)
    dsh_ref[...] += jnp.sum(du, axis=0, keepdims=True)


def ffn_bwd(dy, hg, hu, wg, wu, wd, dz, x_in, scale, name, hosted=None):
    t, d = dy.shape
    c, _, fc = wg.shape
    nb = scale.shape[0]
    tm = TOKEN_TILE
    tps = (t // nb) // tm

    def body(dy_ref, hg_ref, hu_ref, wg_ref, wu_ref, wd_ref, dz_ref, x_ref, sc_ref,
             dhg_ref, dhu_ref, act_ref, dx_ref, dsc_ref, dsh_ref, acc_ref):
        i = pl.program_id(0)
        cc = pl.program_id(1)

        @pl.when(cc == 0)
        def _():
            acc_ref[...] = jnp.zeros_like(acc_ref)

        hgv = hg_ref[...].astype(F32)
        huv = hu_ref[...].astype(F32)
        da = _dot_nt(dy_ref[...], wd_ref[cc])
        sg = jax.nn.sigmoid(hgv)
        sl = hgv * sg
        act_ref[...] = (sl * huv).astype(BF16)
        dhu = (da * sl).astype(BF16)
        dhg = (da * huv * (sg * (1.0 + hgv * (1.0 - sg)))).astype(BF16)
        dhu_ref[...] = dhu
        dhg_ref[...] = dhg
        acc_ref[...] += _dot_nt(dhg, wg_ref[cc]) + _dot_nt(dhu, wu_ref[cc])

        @pl.when(cc == c - 1)
        def _():
            _mod_bwd_tail(acc_ref[...], dz_ref, x_ref, sc_ref, dx_ref, dsc_ref, dsh_ref, i % tps == 0)

    rows = pl.BlockSpec((tm, d), lambda i, cc: (i, 0))
    bvec = pl.BlockSpec((None, 1, d), lambda i, cc: (i // tps, 0, 0))
    hspec = pl.BlockSpec((None, tm, fc), lambda i, cc: (cc, i, 0))
    wcol = _resident(wg)
    return _call(
        body, name, (t // tm, c),
        [rows, hspec, hspec, wcol, wcol, _resident(wd), rows, rows, bvec],
        [hspec, hspec, hspec, rows, bvec, bvec],
        [_sds((c, t, fc), BF16), _sds((c, t, fc), BF16), _sds((c, t, fc), BF16), _sds((t, d), F32),
         _sds((nb, 1, d), F32), _sds((nb, 1, d), F32)],
        (dy, hg, hu, wg, wu, wd, dz, x_in, scale), scratch_shapes=[pltpu.VMEM((tm, d), F32)], hosted=hosted)


def linear_nt_mod_bwd(pairs, dz, x_in, scale, name, hosted=None):
    t, d = dz.shape
    nb = scale.shape[0]
    tm = TOKEN_TILE
    tps = (t // nb) // tm
    npairs = len(pairs)

    def body(*refs):
        dh_refs = refs[:npairs]
        w_refs = refs[npairs:2 * npairs]
        dz_ref, x_ref, sc_ref, dx_ref, dsc_ref, dsh_ref = refs[2 * npairs:]
        du = None
        for (_, _, blk), dh_ref, w_ref in zip(pairs, dh_refs, w_refs):
            dh = dh_ref[...].astype(BF16)
            term = _dot_nt(dh, w_ref[...]) if blk is None else _dot(dh, w_ref[...])
            du = term if du is None else du + term
        _mod_bwd_tail(du, dz_ref, x_ref, sc_ref, dx_ref, dsc_ref, dsh_ref, pl.program_id(0) % tps == 0)

    rows = pl.BlockSpec((tm, d), lambda i: (i, 0))
    bvec = pl.BlockSpec((None, 1, d), lambda i: (i // tps, 0, 0))
    in_specs = [pl.BlockSpec((tm, dh.shape[1]), lambda i: (i, 0)) for dh, _, _ in pairs]
    for dh, w, blk in pairs:
        if blk is None:
            in_specs.append(pl.BlockSpec(w.shape, lambda i: (0, 0)))
        else:
            in_specs.append(pl.BlockSpec((dh.shape[1], d), lambda i, blk=blk: (blk, 0)))
    in_specs += [rows, rows, bvec]
    return _call(
        body, name, (t // tm,), in_specs, [rows, bvec, bvec],
        [_sds((t, d), F32), _sds((nb, 1, d), F32), _sds((nb, 1, d), F32)],
        (*[dh for dh, _, _ in pairs], *[w for _, w, _ in pairs], dz, x_in, scale), hosted=hosted)


def linear_nt_delta(dy, w_o, o, head_sel, name):
    t, d = dy.shape
    hdv = w_o.shape[0]
    tm = TOKEN_TILE

    def body(dy_ref, w_ref, o_ref, sel_ref, do_ref, dl_ref):
        do = _dot_nt(dy_ref[...], w_ref[...])
        do_ref[...] = do.astype(BF16)
        dl_ref[...] = _dot_f32(do * o_ref[...].astype(F32), sel_ref[...])

    return pl.pallas_call(
        body, name=name, grid=(t // tm,),
        in_specs=[pl.BlockSpec((tm, d), lambda i: (i, 0)), pl.BlockSpec((hdv, d), lambda i: (0, 0)),
                  pl.BlockSpec((tm, hdv), lambda i: (i, 0)), pl.BlockSpec(head_sel.shape, lambda i: (0, 0))],
        out_specs=[pl.BlockSpec((tm, hdv), lambda i: (i, 0)), pl.BlockSpec((tm, 128), lambda i: (i, 0))],
        out_shape=[_sds((t, hdv), BF16), _sds((t, 128), F32)], compiler_params=_cp(1),
    )(dy, w_o, o, head_sel)


def _attn_bwd_blocks(j, nk, tk, scale, heads):
    def block(i, carry, masked):
        new = []
        for hd, (dk_acc, dv_acc, dfk_acc) in zip(heads, carry):
            qb = hd["q"](i)
            dob = hd["do"](i)
            lse_row, dl_row = hd["rows"](i)
            st = _dot_nt(hd["k"], qb)
            if scale is not None:
                st = st * scale
            if hd["bias"] is not None:
                fq_row, fk_col = hd["bias"](i)
                st = st + fq_row - fk_col
            if masked:
                keep = lax.broadcasted_iota(jnp.int32, st.shape, 1) >= lax.broadcasted_iota(jnp.int32, st.shape, 0)
                st = jnp.where(keep, st, -1e30)
            pt = jnp.exp(st - lse_row)
            dv_acc = dv_acc + _dot(pt.astype(BF16), dob)
            dst = pt * (_dot_nt(hd["v"], dob) - dl_row)
            if hd["add_dfq"] is not None:
                dfk_acc = dfk_acc - jnp.sum(dst, axis=1, keepdims=True)
                hd["add_dfq"](i, jnp.sum(dst, axis=0, keepdims=True))
            dsb = (dst if scale is None else dst * scale).astype(BF16)
            dk_acc = dk_acc + _dot(dsb, qb)
            hd["add_dq"](i, _dot_tn(dsb, hd["k"] if scale is not None else hd["k_scaled"]))
            new.append((dk_acc, dv_acc, dfk_acc))
        return tuple(new)

    init = tuple((jnp.zeros((tk, hd["k"].shape[1]), F32), jnp.zeros((tk, hd["v"].shape[1]), F32), jnp.zeros((tk, 1), F32))
                 for hd in heads)
    carry = block(j, init, True)
    return lax.fori_loop(j + 1, nk, lambda i, c: block(i, c, False), carry)


def fox_attn_bwd(qkv, do, cum, cum_rows, lse_rows, delta_rows, nb, name, hosted=None):
    t = qkv.shape[0]
    s = t // nb
    tk = ATTN_TILE
    nk = s // tk
    scale = FOX_HD ** -0.5

    def body(q_ref, k_ref, v_ref, do_ref, cum_ref, cr_ref, lr_ref, dr_ref, dq_ref, dk_ref, dv_ref, dfq_ref, dfk_ref):
        hg = pl.program_id(1)
        j = pl.program_id(2)

        @pl.when(j == 0)
        def _():
            dq_ref[...] = jnp.zeros_like(dq_ref)

        @pl.when((j == 0) & (hg == 0))
        def _():
            dfq_ref[...] = jnp.zeros_like(dfq_ref)
            dfk_ref[...] = jnp.zeros_like(dfk_ref)

        low = lax.broadcasted_iota(jnp.int32, (tk, 128), 1) < FOX_HD
        cum_t = cum_ref[...]

        def rows_of(i):
            return pl.ds(pl.multiple_of(i * tk, tk), tk)

        def head(a):
            hd = FOX_GROUP * hg + a
            cols = slice(128 * (a // 2), 128 * (a // 2) + 128)
            half = low if a % 2 == 0 else jnp.logical_not(low)
            kb = k_ref[:, cols]
            vb = v_ref[:, cols]
            fk = _pick_lane(cum_t, hd)

            def add_dq(i, val):
                dq_ref[rows_of(i), cols] += val

            def add_dfq(i, val):
                dfq_ref[i] = _put_row(dfq_ref[i], hd, val)

            ka = jnp.where(half, kb, jnp.zeros_like(kb))
            return dict(q=lambda i: q_ref[rows_of(i), cols] * scale, do=lambda i: do_ref[rows_of(i), cols],
                        k=ka, k_scaled=ka * scale, v=jnp.where(half, vb, jnp.zeros_like(vb)),
                        rows=lambda i: (_pick_row(lr_ref[i], hd), _pick_row(dr_ref[i], hd)),
                        bias=lambda i: (_pick_row(cr_ref[i], hd), fk), add_dq=add_dq, add_dfq=add_dfq)

        res = _attn_bwd_blocks(j, nk, tk, None, [head(a) for a in range(FOX_GROUP)])
        dk_ref[...] = jnp.concatenate([jnp.where(low, res[a][0], res[a + 1][0]) for a in range(0, FOX_GROUP, 2)],
                                      axis=1).astype(BF16)
        dv_ref[...] = jnp.concatenate([jnp.where(low, res[a][1], res[a + 1][1]) for a in range(0, FOX_GROUP, 2)],
                                      axis=1).astype(BF16)
        for a in range(FOX_GROUP):
            dfk_ref[j] = _put_row(dfk_ref[j], FOX_GROUP * hg + a, jnp.broadcast_to(res[a][2], (tk, 128)).T[0:1, :])

    wide = FOX_GROUP * FOX_HD
    ngroups = FOX_HEADS // FOX_GROUP
    rowsp = pl.BlockSpec((nk, 16, tk), lambda b, hg, j: (b, 0, 0))
    return _call(
        body, name, (nb, ngroups, nk),
        [pl.BlockSpec((s, wide), lambda b, hg, j: (b, hg)),
         pl.BlockSpec((tk, wide), lambda b, hg, j: (b * nk + j, ngroups + hg)),
         pl.BlockSpec((tk, wide), lambda b, hg, j: (b * nk + j, 2 * ngroups + hg)),
         pl.BlockSpec((s, wide), lambda b, hg, j: (b, hg)),
         pl.BlockSpec((tk, 128), lambda b, hg, j: (b * nk + j, 0)),
         rowsp, rowsp, rowsp],
        [pl.BlockSpec((s, wide), lambda b, hg, j: (b, hg)),
         pl.BlockSpec((tk, wide), lambda b, hg, j: (b * nk + j, hg)),
         pl.BlockSpec((tk, wide), lambda b, hg, j: (b * nk + j, hg)),
         rowsp, rowsp],
        [_sds((t, D_MODEL), F32), _sds((t, D_MODEL), BF16), _sds((t, D_MODEL), BF16),
         _sds((t // tk, 16, tk), F32), _sds((t // tk, 16, tk), F32)],
        (qkv, qkv, qkv, do, cum, cum_rows, lse_rows, delta_rows), hosted=hosted)


def mla_attn_bwd(q, kn, kr2, v, do, lse_rows, delta_rows, nb, name, hosted=None):
    t = q.shape[0]
    s = t // nb
    tk = ATTN_TILE
    nk = s // tk
    ngroups = MLA_HEADS // MLA_GROUP
    wide = MLA_GROUP * MLA_NOPE
    rwide = MLA_GROUP * MLA_ROPE
    scale = (MLA_NOPE + MLA_ROPE) ** -0.5

    def body(qn_ref, qr_ref, kn_ref, kr_ref, v_ref, do_ref, lr_ref, dr_ref, dqn_ref, dqr_ref, dkn_ref, dkr_ref, dv_ref):
        hg = pl.program_id(1)
        j = pl.program_id(2)

        @pl.when(j == 0)
        def _():
            dqn_ref[...] = jnp.zeros_like(dqn_ref)
            dqr_ref[...] = jnp.zeros_like(dqr_ref)

        low = lax.broadcasted_iota(jnp.int32, (tk, 128), 1) < MLA_ROPE
        kr = kr_ref[...]

        def rows_of(i):
            return pl.ds(pl.multiple_of(i * tk, tk), tk)

        def head(a):
            cols = slice(a * MLA_NOPE, (a + 1) * MLA_NOPE)
            rcols = slice(128 * (a // 2), 128 * (a // 2) + 128)
            mine = low if a % 2 == 0 else jnp.logical_not(low)
            hd = MLA_GROUP * hg + a

            def q_fn(i):
                qr = qr_ref[rows_of(i), rcols]
                return jnp.concatenate([qn_ref[rows_of(i), cols], jnp.where(mine, qr, jnp.zeros_like(qr))], axis=1)

            def add_dq(i, val):
                dqn_ref[rows_of(i), cols] += val[:, :MLA_NOPE]
                dqr_ref[rows_of(i), cols] += val[:, MLA_NOPE:]

            return dict(q=q_fn, do=lambda i: do_ref[rows_of(i), cols], k=jnp.concatenate([kn_ref[:, cols], kr], axis=1),
                        v=v_ref[:, cols], rows=lambda i: (_pick_row(lr_ref[i], hd), _pick_row(dr_ref[i], hd)),
                        bias=None, add_dq=add_dq, add_dfq=None)

        res = _attn_bwd_blocks(j, nk, tk, scale, [head(a) for a in range(MLA_GROUP)])
        dkn_ref[...] = jnp.concatenate([r[0][:, :MLA_NOPE] for r in res], axis=1).astype(BF16)
        dkr_ref[...] = jnp.concatenate([r[0][:, MLA_NOPE:] for r in res], axis=1).astype(BF16)
        dv_ref[...] = jnp.concatenate([r[1] for r in res], axis=1).astype(BF16)

    full = pl.BlockSpec((s, wide), lambda b, hg, j: (b, hg))
    blk = pl.BlockSpec((tk, wide), lambda b, hg, j: (b * nk + j, hg))
    rowsp = pl.BlockSpec((nk, 16, tk), lambda b, hg, j: (b, 0, 0))
    total = MLA_HEADS * MLA_V
    rope0 = MLA_HEADS * MLA_NOPE // rwide
    return _call(
        body, name, (nb, ngroups, nk),
        [full, pl.BlockSpec((s, rwide), lambda b, hg, j: (b, rope0 + hg)), blk,
         pl.BlockSpec((tk, 128), lambda b, hg, j: (b * nk + j, 0)), blk, full, rowsp, rowsp],
        [full, full, blk, blk, blk],
        [_sds((t, total), F32), _sds((t, total), F32), _sds((t, total), BF16), _sds((t, total), BF16),
         _sds((t, total), BF16)],
        (q, q, kn, kr2, v, do, lse_rows, delta_rows), hosted=hosted)


def mla_mid_bwd(dqn, dqr, dkn, dv, dkr_heads, h, g_q, g_kv, w_uq, w_uk, w_uv, cos8, sin8, cos64, sin64s, swap64,
                heads_to_rope, head_sum, name, hosted=None):
    t = h.shape[0]
    tm = TOKEN_TILE
    hq = MLA_HEADS * MLA_NOPE
    hr = MLA_HEADS * MLA_ROPE // 2
    nq = w_uq.shape[1]

    def body(dqn_ref, dqr_ref, dkn_ref, dv_ref, dkr_ref, h_ref, gq_ref, gkv_ref, wuq_ref, wuk_ref, wuv_ref,
             c8_ref, s8_ref, c64_ref, s64_ref, sw_ref, hp_ref, hs_ref, dh_ref, dqp_ref, dgq_ref, dgkv_ref):
        @pl.when(pl.program_id(0) == 0)
        def _():
            dgq_ref[...] = jnp.zeros_like(dgq_ref)
            dgkv_ref[...] = jnp.zeros_like(dgkv_ref)

        drot = _dot(dqr_ref[...].astype(BF16), hp_ref[...])
        o1 = drot[:, :hr]
        o2 = drot[:, hr:]
        cs = c8_ref[...]
        sn = s8_ref[...]
        dqp = jnp.concatenate([dqn_ref[...].astype(BF16), (o1 * cs + o2 * sn).astype(BF16),
                               (o2 * cs - o1 * sn).astype(BF16)], axis=1)
        dqp_ref[...] = dqp
        dcq = _dot_nt(dqp, wuq_ref[...])
        dckv = _dot_nt(dkn_ref[...], wuk_ref[...]) + _dot_nt(dv_ref[...], wuv_ref[...])
        hh = h_ref[...]

        def rms_bwd(hpart, g, dc, dg_ref):
            hhat, rstd = _rms(hpart, None)
            dg_ref[...] += jnp.sum(dc * hhat, axis=0, keepdims=True)
            dcg = dc * g
            return rstd * (dcg - hhat * jnp.mean(dcg * hhat, axis=-1, keepdims=True))

        dhq = rms_bwd(hh[:, :MLA_QR], gq_ref[...], dcq, dgq_ref)
        dhkv = rms_bwd(hh[:, MLA_QR:MLA_QR + MLA_KVR], gkv_ref[...], dckv, dgkv_ref)
        dkr = _dot(dkr_ref[...], hs_ref[...])
        dkr_pre = dkr * c64_ref[...] + _dot_f32(dkr * s64_ref[...], sw_ref[...])
        dh_ref[...] = jnp.concatenate([dhq, dhkv, dkr_pre], axis=1).astype(BF16)

    def rows(n):
        return pl.BlockSpec((tm, n), lambda i: (i, 0))

    def whole(a):
        return pl.BlockSpec(a.shape, lambda i: (0,) * a.ndim)

    return _call(
        body, name, (t // tm,),
        [rows(hq), rows(hq), rows(hq), rows(hq), rows(hq), rows(h.shape[1]), whole(g_q), whole(g_kv),
         whole(w_uq), whole(w_uk), whole(w_uv), rows(hr), rows(hr), rows(MLA_ROPE), rows(MLA_ROPE),
         whole(swap64), whole(heads_to_rope), whole(head_sum)],
        [rows(h.shape[1]), rows(nq), pl.BlockSpec((1, MLA_QR), lambda i: (0, 0)),
         pl.BlockSpec((1, MLA_KVR), lambda i: (0, 0))],
        [_sds((t, h.shape[1]), BF16), _sds((t, nq), BF16), _sds((1, MLA_QR), F32), _sds((1, MLA_KVR), F32)],
        (dqn, dqr, dkn, dv, dkr_heads, h, g_q, g_kv, w_uq, w_uk, w_uv, cos8, sin8, cos64, sin64s, swap64,
         heads_to_rope, head_sum), hosted=hosted)


def fox_gate_bwd(dcum, hf, b_f, triu, n_batch, name):
    t, n = hf.shape
    blk = triu.shape[0]
    nb = (t // n_batch) // blk

    def body(dc_ref, hf_ref, b_ref, tri_ref, o_ref, db_ref, carry_ref):
        @pl.when(pl.program_id(1) == 0)
        def _():
            carry_ref[...] = jnp.zeros_like(carry_ref)

        @pl.when((pl.program_id(0) == 0) & (pl.program_id(1) == 0))
        def _():
            db_ref[...] = jnp.zeros_like(db_ref)

        rc = _dot_f32(tri_ref[...], dc_ref[...]) + carry_ref[...]
        carry_ref[...] = rc[0:1, :]
        dhf = rc * jax.nn.sigmoid(-(hf_ref[...] + b_ref[...]))
        o_ref[...] = dhf.astype(BF16)
        db_ref[...] += jnp.sum(dhf, axis=0, keepdims=True)

    rev = pl.BlockSpec((blk, n), lambda bb, i: (bb * nb + nb - 1 - i, 0))
    return pl.pallas_call(
        body, name=name, grid=(n_batch, nb),
        in_specs=[rev, rev, pl.BlockSpec((1, n), lambda bb, i: (0, 0)), pl.BlockSpec((blk, blk), lambda bb, i: (0, 0))],
        out_specs=[rev, pl.BlockSpec((1, n), lambda bb, i: (0, 0))],
        out_shape=[_sds((t, n), BF16), _sds((1, n), F32)], scratch_shapes=[pltpu.VMEM((1, n), F32)],
        compiler_params=_cp(2),
    )(dcum, hf, b_f, triu)


def wgrad(a, bm, name, with_bf16=False, bt=WGRAD_TOKENS):
    ca, t, kd = a.shape
    cb, _, nd = bm.shape
    c = max(ca, cb)
    bn = nd
    if nd > 1024 and nd % 1024 == 0:
        bn = 1024
    nsteps = t // bt

    def body(a_ref, b_ref, o_ref, *rest):
        @pl.when(pl.program_id(2) == 0)
        def _():
            o_ref[...] = jnp.zeros_like(o_ref)

        o_ref[...] += _dot_tn(a_ref[...].astype(BF16), b_ref[...].astype(BF16))
        if with_bf16:
            @pl.when(pl.program_id(2) == nsteps - 1)
            def _():
                rest[0][...] = o_ref[...].astype(BF16)

    out_spec = pl.BlockSpec((None, kd, bn), lambda cc, n, tt: (cc, 0, n))
    res = pl.pallas_call(
        body, name=name, grid=(c, nd // bn, nsteps),
        in_specs=[pl.BlockSpec((None, bt, kd), lambda cc, n, tt: (cc if ca > 1 else 0, tt, 0)),
                  pl.BlockSpec((None, bt, bn), lambda cc, n, tt: (cc if cb > 1 else 0, tt, n))],
        out_specs=[out_spec, out_spec] if with_bf16 else out_spec,
        out_shape=[_sds((c, kd, nd), F32), _sds((c, kd, nd), BF16)] if with_bf16 else _sds((c, kd, nd), F32),
        compiler_params=_cp(3),
    )(a, bm)
    return res


def ada_mod_part(c_all, ada_w, name):
    nl, d, n = ada_w.shape
    rows = c_all.shape[0]
    tn = 512

    def body(c_ref, w_ref, o_ref):
        cv = c_ref[...]
        act = (cv * jax.nn.sigmoid(cv)).astype(BF16)
        o_ref[...] = _dot(act, w_ref[...].astype(BF16))

    return pl.pallas_call(
        body, name=name, grid=(nl, n // tn),
        in_specs=[pl.BlockSpec((rows, d), lambda l, j: (0, 0)), pl.BlockSpec((None, d, tn), lambda l, j: (l, 0, j))],
        out_specs=pl.BlockSpec((None, rows, tn), lambda l, j: (l, 0, j)),
        out_shape=_sds((nl, rows, n), F32), compiler_params=_cp(2),
    )(c_all, ada_w)


def ada_grad(c_all_t, dmod, name):
    nl, rows, n = dmod.shape
    d = c_all_t.shape[0]
    tn = 512

    def body(c_ref, dm_ref, o_ref):
        cv = c_ref[...]
        act = (cv * jax.nn.sigmoid(cv)).astype(BF16)
        o_ref[...] = _dot(act, dm_ref[...].astype(BF16))

    return pl.pallas_call(
        body, name=name, grid=(nl, n // tn),
        in_specs=[pl.BlockSpec((d, rows), lambda l, j: (0, 0)), pl.BlockSpec((None, rows, tn), lambda l, j: (l, 0, j))],
        out_specs=pl.BlockSpec((None, d, tn), lambda l, j: (l, 0, j)),
        out_shape=_sds((nl, d, n), F32), compiler_params=_cp(2),
    )(c_all_t, dmod)


def sum_leading(a, name):
    g, r, n = a.shape

    def body(a_ref, o_ref):
        acc = a_ref[0]
        for kk in range(1, g):
            acc = acc + a_ref[kk]
        o_ref[...] = acc

    return pl.pallas_call(
        body, name=name, grid=(1,), in_specs=[pl.BlockSpec((g, r, n), lambda i: (0, 0, 0))],
        out_specs=pl.BlockSpec((r, n), lambda i: (0, 0)), out_shape=_sds((r, n), F32), compiler_params=_cp(1),
    )(a)


def adamw(w, g, m, v, name):
    r, n = w.shape
    fits = [cand for cand in range(8, r, 8) if r % cand == 0 and cand * n * 4 <= ADAMW_BLOCK_BYTES]
    br = max(fits) if fits else r
    c1 = 1.0 - ADAM_B1 ** ADAM_STEP
    c2 = 1.0 - ADAM_B2 ** ADAM_STEP

    def body(w_ref, g_ref, m_ref, v_ref, d_ref, mo_ref, vo_ref):
        gv = g_ref[...]
        mn = ADAM_B1 * m_ref[...] + (1.0 - ADAM_B1) * gv
        vn = ADAM_B2 * v_ref[...] + (1.0 - ADAM_B2) * (gv * gv)
        mo_ref[...] = mn
        vo_ref[...] = vn
        d_ref[...] = -ADAM_LR * ((mn / c1) / (jnp.sqrt(vn / c2) + ADAM_EPS) + ADAM_WD * w_ref[...])

    spec = pl.BlockSpec((br, n), lambda i: (i, 0))
    return _call(body, name, (r // br,), [spec] * 4, [spec] * 3, [_sds((r, n), F32)] * 3, (w, g, m, v))


def all_gather8(x_blk, name, hosted=None):
    m_per, n = x_blk.shape
    h_in = 0 if hosted is None else len(hosted.inputs)
    h_out = 0 if hosted is None else len(hosted.out_shape)

    def body(x_ref, *refs):
        c_in, (out_ref, *c_out), (send_sems, recv_sems, local_sem, *c_sem) = (
            refs[:h_in], refs[h_in:h_in + 1 + h_out], refs[h_in + 1 + h_out:])
        if hosted is not None:
            hosted.start(c_in, c_out, c_sem)
        gather(x_ref, out_ref, send_sems, recv_sems, local_sem)
        if hosted is not None:
            hosted.finish(c_in, c_out, c_sem)

    def gather(x_ref, out_ref, send_sems, recv_sems, local_sem):
        x, y, c = _place()
        me, sibling = (x, y, c), (x, y, 1 - c)
        chips = [(1 - x, y), (x, 1 - y), (1 - x, 1 - y)]

        def rows(px, py, pc):
            return out_ref.at[pl.ds((4 * px + 2 * py + pc) * m_per, m_per), :]

        def copy(k, block, to, src=None):
            return pltpu.make_async_remote_copy(
                src_ref=rows(*block) if src is None else src, dst_ref=rows(*block),
                send_sem=send_sems.at[k], recv_sem=recv_sems.at[k], device_id=to, device_id_type=MESH)

        mine = pltpu.make_async_copy(x_ref, rows(*me), local_sem)
        mine.start()
        first = [copy(0, me, sibling, src=x_ref)]
        first += [copy(1 + j, me, (*chip, c), src=x_ref) for j, chip in enumerate(chips)]
        for cp in first:
            cp.start()
        passed = [copy(4 + j, (*chip, c), sibling) for j, chip in enumerate(chips)]
        for j, chip in enumerate(chips):
            copy(1 + j, (*chip, c), me).wait_recv()
            passed[j].start()
        copy(0, sibling, me).wait_recv()
        for j, chip in enumerate(chips):
            copy(4 + j, (*chip, 1 - c), me).wait_recv()
        for cp in first + passed:
            cp.wait_send()
        mine.wait()

    hbm = pl.BlockSpec(memory_space=pl.ANY)
    vmem = pl.BlockSpec(memory_space=pltpu.VMEM)
    res = pl.pallas_call(
        body, name=name,
        out_shape=[_sds((8 * m_per, n), x_blk.dtype)] + ([] if hosted is None else list(hosted.out_shape)),
        in_specs=[vmem] + [hbm] * h_in, out_specs=[vmem] + [hbm] * h_out,
        scratch_shapes=[pltpu.SemaphoreType.DMA((7,)), pltpu.SemaphoreType.DMA((7,)), pltpu.SemaphoreType.DMA]
        + ([] if hosted is None else list(hosted.sems)),
        compiler_params=pltpu.CompilerParams(vmem_limit_bytes=VMEM_LIMIT),
    )(x_blk, *([] if hosted is None else hosted.inputs))
    return res[0] if hosted is None else (res[0], res[1:])


def _gather_comm(shards):
    nt = len(shards)

    def parts(w_refs, out_refs, sems, finishing):
        send_sems, recv_sems, own_send, own_recv = sems
        x, y, c = _place()
        sibling = (x, y, 1 - c)
        chips = [(1 - x, y), (x, 1 - y), (1 - x, 1 - y)]

        def copy(t, k, block, to, src=None):
            px, py, hh = block
            dst = out_refs[t].at[2 * px + py, hh]
            return pltpu.make_async_remote_copy(
                src_ref=dst if src is None else src, dst_ref=dst,
                send_sem=send_sems.at[6 * t + k], recv_sem=recv_sems.at[6 * t + k], device_id=to, device_id_type=MESH)

        own = [pltpu.make_async_remote_copy(
            src_ref=w_refs[t], dst_ref=out_refs[t].at[2 * x + y], send_sem=own_send.at[t], recv_sem=own_recv.at[t],
            device_id=sibling, device_id_type=MESH) for t in range(nt)]
        first = [copy(t, j, (x, y, c), (*chip, c), src=w_refs[t].at[c]) for t in range(nt) for j, chip in enumerate(chips)]
        if not finishing:
            return own, first
        landed = [copy(t, j, (*chip, c), (x, y, c)) for t in range(nt) for j, chip in enumerate(chips)]
        passed = [copy(t, 3 + j, (*chip, c), sibling) for t in range(nt) for j, chip in enumerate(chips)]
        from_sibling = [copy(t, 3 + j, (*chip, 1 - c), (x, y, c)) for t in range(nt) for j, chip in enumerate(chips)]
        return own, first, landed, passed, from_sibling

    def start(w_refs, out_refs, sems):
        own, first = parts(w_refs, out_refs, sems, False)
        for cp in own + first:
            cp.start()

    def finish(w_refs, out_refs, sems):
        own, first, landed, passed, from_sibling = parts(w_refs, out_refs, sems, True)
        for arrived, fwd in zip(landed, passed):
            arrived.wait_recv()
            fwd.start()
        for cp in from_sibling:
            cp.wait_recv()
        for cp in first + passed:
            cp.wait_send()
        for cp in own:
            cp.wait()

    sems = [pltpu.SemaphoreType.DMA((6 * nt,)), pltpu.SemaphoreType.DMA((6 * nt,)),
            pltpu.SemaphoreType.DMA((nt,)), pltpu.SemaphoreType.DMA((nt,))]
    return _Hosted(list(shards), [_sds((N_CHIPS, *w.shape), w.dtype) for w in shards], sems, start, finish)


def _row_block(r, n, itemsize):
    best = None
    for br in range(16, r + 1, 16):
        if r % br == 0 and br * n * itemsize <= COMM_BLOCK_BYTES:
            best = br
    return r if best is None else best


def _scatter_comm(parts):
    nt = len(parts)

    def copies(p_refs, b_refs, sems, arriving):
        send_sems, recv_sems = sems
        x, y, c = _place()
        me = 4 * x + 2 * y + c
        cps = []
        for t in range(nt):
            for r in range(1, 8):
                tx = 1 - x if r & 4 else x
                ty = 1 - y if r & 2 else y
                tc = 1 - c if r & 1 else c
                src, dst = (2 * x + y, c), 4 * tx + 2 * ty + tc
                if not arriving:
                    src, dst = (2 * tx + ty, tc), me
                cps.append(pltpu.make_async_remote_copy(
                    src_ref=p_refs[t].at[src], dst_ref=b_refs[t].at[dst], send_sem=send_sems.at[7 * t + r - 1],
                    recv_sem=recv_sems.at[7 * t + r - 1], device_id=(tx, ty, tc), device_id_type=MESH))
        return cps

    def start(p_refs, b_refs, sems):
        for cp in copies(p_refs, b_refs, sems, False):
            cp.start()

    def finish(p_refs, b_refs, sems):
        for cp in copies(p_refs, b_refs, sems, True):
            cp.wait_recv()
        for cp in copies(p_refs, b_refs, sems, False):
            cp.wait_send()

    sems = [pltpu.SemaphoreType.DMA((7 * nt,)), pltpu.SemaphoreType.DMA((7 * nt,))]
    return _Hosted(list(parts), [_sds((2 * N_CHIPS, *p.shape[2:]), p.dtype) for p in parts], sems, start, finish)


def sum_devices(own, recv, place, name, slot=(0, 1, None)):
    _, _, r, n = own.shape
    layer, n_layers, buf = slot
    br = _row_block(r, n, 4 * 8)

    def body(p_ref, o_ref, *rest):
        acc = o_ref[...]
        for kk in range(7):
            acc = acc + rest[kk][...].astype(F32)
        rest[-1][...] = acc

    def arrived(rel):
        return pl.BlockSpec((None, br, n), lambda i, pref: (jnp.bitwise_xor(pref[0], rel), i, 0))

    in_specs = [pl.BlockSpec((None, None, br, n), lambda i, pref: (pref[2], pref[1], i, 0))]
    in_specs += [arrived(rel) for rel in range(1, 8)]
    args = [own] + [recv] * 7
    aliases = {}
    if buf is not None:
        in_specs.append(pl.BlockSpec(memory_space=pl.ANY))
        args.append(buf)
        aliases = {9: 0}
    return pl.pallas_call(
        body, name=name,
        grid_spec=pltpu.PrefetchScalarGridSpec(
            num_scalar_prefetch=1, grid=(r // br,), in_specs=in_specs,
            out_specs=pl.BlockSpec((None, None, br, n), lambda i, pref: (layer, pref[1], i, 0))),
        out_shape=_sds((n_layers, 2, r, n), F32), input_output_aliases=aliases, compiler_params=_cp(1),
    )(place, *args)


def _join_comm(bufs):
    nt = len(bufs)
    layers = [bf.shape[0] for bf in bufs]
    first = [sum(layers[:t]) for t in range(nt)]

    def copies(o_refs, sems, own):
        send_sems, recv_sems = sems
        x, y, c = _place()
        hh = c if own else 1 - c
        return [pltpu.make_async_remote_copy(
            src_ref=o_refs[t].at[l, hh], dst_ref=o_refs[t].at[l, hh], send_sem=send_sems.at[first[t] + l],
            recv_sem=recv_sems.at[first[t] + l], device_id=(x, y, 1 - c), device_id_type=MESH)
            for t in range(nt) for l in range(layers[t])]

    def start(_, o_refs, sems):
        for cp in copies(o_refs, sems, True):
            cp.start()

    def finish(_, o_refs, sems):
        for cp in copies(o_refs, sems, False):
            cp.wait_recv()
        for cp in copies(o_refs, sems, True):
            cp.wait_send()

    sems = [pltpu.SemaphoreType.DMA((sum(layers),)), pltpu.SemaphoreType.DMA((sum(layers),))]
    return _Hosted(list(bufs), [_sds(bf.shape, bf.dtype) for bf in bufs], sems, start, finish, in_place=True)


def sibling_join_halves(bufs, name):
    comm = _join_comm(bufs)
    nt = len(bufs)

    def body(*refs):
        comm.start(refs[:nt], refs[nt:2 * nt], refs[2 * nt:])
        comm.finish(refs[:nt], refs[nt:2 * nt], refs[2 * nt:])

    hbm = pl.BlockSpec(memory_space=pl.ANY)
    return pl.pallas_call(body, name=name, out_shape=comm.out_shape, in_specs=[hbm] * nt, out_specs=[hbm] * nt,
                          input_output_aliases={k: k for k in range(nt)}, scratch_shapes=comm.sems)(*bufs)


_SHARD_KIND = {"mla_w_in": "rows", "mla_w_uq": "cols", "mla_w_uk": "cols", "mla_w_uv": "cols", "mla_w_o": "rows",
               "fox_w_in": "cols", "fox_w_o": "rows", "ffn_w_gate": "chunk", "ffn_w_up": "chunk", "ffn_w_down": "chunk"}
_PACKED = tuple(_SHARD_KIND)
_TRANSPOSED = ("ffn_w_gate", "ffn_w_up", "fox_w_in")


def _halves(shard):
    if shard.ndim == 3 and shard.shape[0] == 2:
        return shard
    r, n = shard.shape[-2:]
    return shard.reshape(2, r // 2, n)


def _cols_to_full(g):
    return jnp.transpose(g, (1, 0, 2)).reshape(g.shape[1], -1)


def _full_to_cols(w):
    k, n4 = w.shape
    return jnp.transpose(w.reshape(k, N_CHIPS, n4 // N_CHIPS), (1, 0, 2))


def _uq_perm():
    per = MLA_NOPE + MLA_ROPE
    half = MLA_ROPE // 2
    nope = [h * per + d for h in range(MLA_HEADS) for d in range(MLA_NOPE)]
    r1 = [h * per + MLA_NOPE + r for h in range(MLA_HEADS) for r in range(half)]
    r2 = [h * per + MLA_NOPE + half + r for h in range(MLA_HEADS) for r in range(half)]
    perm = np.array(nope + r1 + r2, dtype=np.int32)
    return perm, np.argsort(perm).astype(np.int32)


def _rope_matrices():
    half = MLA_ROPE // 2
    nr = MLA_HEADS * MLA_ROPE
    to_heads = np.zeros((nr, nr), np.float32)
    from_heads = np.zeros((MLA_HEADS * 128, nr), np.float32)
    for e in range(2):
        for h in range(MLA_HEADS):
            for r in range(half):
                to_heads[e * MLA_HEADS * half + h * half + r, h * MLA_ROPE + e * half + r] = 1.0
                from_heads[h * 128 + e * half + r, e * MLA_HEADS * half + h * half + r] = 1.0
    head_sum = np.tile(np.eye(MLA_ROPE, dtype=np.float32), (2 * MLA_HEADS, 1))
    dup = np.concatenate([np.eye(MLA_ROPE, dtype=np.float32)] * 2, axis=1)
    return to_heads, from_heads, head_sum, dup


def _ffn_weights(gathered):
    return tuple(g.reshape(N_CHIPS, 2 * g.shape[2], g.shape[3]) for g in gathered)


def _fox_weights(gathered):
    w_in, w_o = gathered
    w_in = jnp.transpose(w_in, (0, 2, 1, 3)).reshape(N_CHIPS * w_in.shape[2], 2 * w_in.shape[3])
    return w_in, w_o.reshape(-1, w_o.shape[-1])


def _local_step(x, positions, target, mods, wts, ln_g, ln_b, mla_g_q, mla_g_kv, fox_b_f, shards=None):
    nb, s, d = x.shape
    t = nb * s
    x0 = x.reshape(t, d)
    tgt = target.reshape(t, d)
    perm, inv_perm = _uq_perm()

    half = MLA_ROPE // 2
    inv_freq = ROPE_THETA ** (-jnp.arange(half, dtype=F32) / half)
    ang = positions.astype(F32).reshape(t, 1) * inv_freq
    cos, sin = jnp.cos(ang), jnp.sin(ang)
    cos8, sin8 = jnp.tile(cos, (1, MLA_HEADS)), jnp.tile(sin, (1, MLA_HEADS))
    cos64 = jnp.concatenate([cos, cos], axis=1)
    sin64s = jnp.concatenate([-sin, sin], axis=1)
    swap64 = jnp.asarray(np.roll(np.eye(MLA_ROPE, dtype=np.float32), half, axis=1))
    to_heads, from_heads, head_sum, dup = _rope_matrices()
    to_heads, from_heads = jnp.asarray(to_heads, dtype=BF16), jnp.asarray(from_heads, dtype=BF16)
    head_sum, dup = jnp.asarray(head_sum, dtype=BF16), jnp.asarray(dup, dtype=BF16)
    sel_mla = jnp.asarray(np.pad(np.kron(np.eye(MLA_HEADS, dtype=np.float32), np.ones((MLA_V, 1), np.float32)),
                                 ((0, 0), (0, 128 - MLA_HEADS))))
    sel_fox = jnp.asarray(np.pad(np.kron(np.eye(FOX_HEADS, dtype=np.float32), np.ones((FOX_HD, 1), np.float32)),
                                 ((0, 0), (0, 128 - FOX_HEADS))))
    tri = jnp.asarray(np.tril(np.ones((128, 128), np.float32)))
    triu = jnp.asarray(np.triu(np.ones((128, 128), np.float32)))
    onehot16 = jnp.asarray(np.eye(16, 128, dtype=np.float32))

    def vec(a):
        return a.reshape(1, -1)

    def carried(key):
        return None if shards is None else _gather_comm(shards[key])

    def split(res):
        return (res, None) if shards is None else res

    w_uq_p = wts["mla_w_uq"][:, perm]
    b_f_pad = jnp.pad(fox_b_f.reshape(1, -1), ((0, 0), (0, 128 - FOX_HEADS)))

    sh_a, sc_a, gt_a, sh_f, sc_f, gt_f = mods[0]
    h_in, u_m = mod_linear(x0, sh_a, sc_a, wts["mla_w_in"], F32, "mla_in", emit_u=True)
    q_m, kn_m, v_m, kr2_m, cq_m, ckv_m = mla_mid_fwd(
        h_in, vec(mla_g_q), vec(mla_g_kv), w_uq_p, wts["mla_w_uk"], wts["mla_w_uv"], cos8, sin8, cos64, sin64s, swap64,
        to_heads, dup, "mla_mid")
    (o_m, lse_m), got = split(mla_attn_fwd(q_m, kn_m, kr2_m, v_m, nb, "mla_attn", hosted=carried("ffn0")))
    ffn0_w = wts["ffn"][0] if got is None else _ffn_weights(got)
    y0, x1 = linear_resid_ln(o_m, wts["mla_w_o"], x0, gt_a, vec(ln_g[0, 0]), vec(ln_b[0, 0]), "mla_out")
    (u_f0, hg0, hu0, y1, x2), got = split(ffn_fwd(x1, sh_f, sc_f, gt_f, *ffn0_w, vec(ln_g[0, 1]), vec(ln_b[0, 1]), "ffn0",
                                                  hosted=carried("fox")))
    fox_w_in_t, fox_w_o = (wts["fox_w_in"].T, wts["fox_w_o"]) if got is None else _fox_weights(got)
    fox_w_f_t = jnp.pad(fox_w_in_t[3 * d:], ((0, 128 - FOX_HEADS), (0, 0)))
    sh_a1, sc_a1, gt_a1, sh_f1, sc_f1, gt_f1 = mods[1]
    qkv, u_x = mod_linear(x2, sh_a1, sc_a1, fox_w_in_t, BF16, "fox_qkv", tn=1024, emit_u=True, w_rows=3 * d)
    hf = mod_linear(x2, sh_a1, sc_a1, fox_w_f_t, F32, "fox_f", w_rows=128)
    cum = fox_gate_fwd(hf, b_f_pad, tri, nb, "fox_gate")
    cum_rows = rows16(cum, "fox_cum_rows")
    (o_x, lse_x), got = split(fox_attn_fwd(qkv, cum, cum_rows, nb, "fox_attn", hosted=carried("ffn1")))
    ffn1_w = wts["ffn"][1] if got is None else _ffn_weights(got)
    y2, x3 = linear_resid_ln(o_x, fox_w_o, x2, gt_a1, vec(ln_g[1, 0]), vec(ln_b[1, 0]), "fox_out")
    u_f1, hg1, hu1, y3, x4 = ffn_fwd(x3, sh_f1, sc_f1, gt_f1, *ffn1_w, vec(ln_g[1, 1]), vec(ln_b[1, 1]), "ffn1")

    parts, recv = {}, {}

    def halves_of(g):
        return g.reshape(N_CHIPS, 2, g.shape[1] // 2, g.shape[2])

    def scatter(keys, sent):
        return None if shards is None else _scatter_comm([sent[k] for k in keys])

    def landed(keys, got):
        if got is not None:
            recv.update(zip(keys, got))

    def ffn_grads(layer, u, dhg, dhu, act, dy):
        sent = {}
        for n, (a_op, b_op) in (("ffn_w_gate", (dhg, u[None])), ("ffn_w_up", (dhu, u[None])), ("ffn_w_down", (act, dy[None]))):
            g32, g16 = wgrad(a_op, b_op, "ffn%d_d%s" % (layer, n[4:]), with_bf16=True)
            parts["%s/%d" % (n, layer)], sent["%s/%d" % (n, layer)] = halves_of(g32), halves_of(g16)
        return sent

    dz3, dy3, dg11, db11, dgt_f1, sq_err = ln_bwd(x4, x3, y3, gt_f1, vec(ln_g[1, 1]), "ffn1_ln_bwd", target=tgt)
    loss_part = 0.5 * jnp.sum(sq_err) / d
    dhg1, dhu1, act1, dx3, dsc_f1, dsh_f1 = ffn_bwd(dy3, hg1, hu1, *ffn1_w, dz3, x3, sc_f1, "ffn1_bwd")
    sent = ffn_grads(1, u_f1, dhg1, dhu1, act1, dy3)
    dz2, dy2, dg10, db10, dgt_a1 = ln_bwd(dx3, x2, y2, gt_a1, vec(ln_g[1, 0]), "fox_ln_bwd")
    do_x, delta_x = linear_nt_delta(dy2, fox_w_o, o_x, sel_fox, "fox_out_bwd")
    (dq_x, dk_x, dv_x, dfq_x, dfk_x), got = split(fox_attn_bwd(
        qkv, do_x, cum, cum_rows, rows16(lse_x, "fox_lse_rows"), rows16(delta_x, "fox_delta_rows"), nb, "fox_attn_bwd",
        hosted=scatter(list(sent), sent)))
    landed(list(sent), got)
    dcum = tokens128(dfq_x + dfk_x, onehot16, "fox_dcum")
    dhf, dbf = fox_gate_bwd(dcum, hf, b_f_pad, triu, nb, "fox_gate_bwd")
    fox_d = [("q", dq_x), ("k", dk_x), ("v", dv_x)]
    dx2, dsc_a1, dsh_a1 = linear_nt_mod_bwd(
        [(dh, fox_w_in_t, i) for i, (_, dh) in enumerate(fox_d)] + [(dhf, fox_w_f_t, 0)], dz2, x2, sc_a1, "fox_in_bwd")
    dw_in_t = [wgrad(dh[None], u_x[None], "fox_dw" + tag)[0] for tag, dh in fox_d]
    dw_in_t.append(wgrad(dhf[None], u_x[None], "fox_dwf")[0][:FOX_HEADS])
    dw_in_t = jnp.concatenate(dw_in_t, axis=0).reshape(N_CHIPS, -1, 2, d // 2)
    parts["fox_w_in"] = jnp.transpose(dw_in_t, (0, 2, 1, 3))
    parts["fox_w_o"] = wgrad(o_x[None], dy2[None], "fox_dwo")[0].reshape(N_CHIPS, 2, -1, d)
    sent = {k: parts[k].astype(BF16) for k in ("fox_w_in", "fox_w_o")}
    dz1, dy1, dg01, db01, dgt_f0 = ln_bwd(dx2, x1, y1, gt_f, vec(ln_g[0, 1]), "ffn0_ln_bwd")
    (dhg0, dhu0, act0, dx1, dsc_f0, dsh_f0), got = split(ffn_bwd(dy1, hg0, hu0, *ffn0_w, dz1, x1, sc_f, "ffn0_bwd",
                                                                 hosted=scatter(list(sent), sent)))
    landed(list(sent), got)
    sent = ffn_grads(0, u_f0, dhg0, dhu0, act0, dy1)
    dz0, dy0, dg00, db00, dgt_a0 = ln_bwd(dx1, x0, y0, gt_a, vec(ln_g[0, 0]), "mla_ln_bwd")
    do_m, delta_m = linear_nt_delta(dy0, wts["mla_w_o"], o_m, sel_mla, "mla_out_bwd")
    parts["mla_w_o"] = wgrad(o_m[None], dy0[None], "mla_dwo")[0].reshape(N_CHIPS, 2, -1, d)
    (dqn_m, dqr_m, dkn_m, dkr_m, dv_m), got = split(mla_attn_bwd(
        q_m, kn_m, kr2_m, v_m, do_m, rows16(lse_m, "mla_lse_rows"), rows16(delta_m, "mla_delta_rows"), nb,
        "mla_attn_bwd", hosted=scatter(list(sent), sent)))
    landed(list(sent), got)
    sent = {"mla_w_o": parts["mla_w_o"].astype(BF16)}
    (dh_in, dq_pre, dgq, dgkv), got = split(mla_mid_bwd(
        dqn_m, dqr_m, dkn_m, dv_m, dkr_m, h_in, vec(mla_g_q), vec(mla_g_kv), w_uq_p, wts["mla_w_uk"],
        wts["mla_w_uv"], cos8, sin8, cos64, sin64s, swap64, from_heads, head_sum, "mla_mid_bwd",
        hosted=scatter(list(sent), sent)))
    landed(list(sent), got)
    parts["mla_w_uq"] = halves_of(_full_to_cols(wgrad(cq_m[None], dq_pre[None], "mla_dwuq")[0][:, inv_perm]))
    parts["mla_w_uk"] = halves_of(_full_to_cols(wgrad(ckv_m[None], dkn_m[None], "mla_dwuk")[0]))
    parts["mla_w_uv"] = halves_of(_full_to_cols(wgrad(ckv_m[None], dv_m[None], "mla_dwuv")[0]))
    parts["mla_w_in"] = wgrad(u_m[None], dh_in[None], "mla_dwin")[0].reshape(N_CHIPS, 2, -1, h_in.shape[1])
    sent = {k: parts[k].astype(BF16) for k in ("mla_w_in", "mla_w_uq", "mla_w_uk", "mla_w_uv")}
    (dx0, dsc_a0, dsh_a0), got = split(linear_nt_mod_bwd([(dh_in, wts["mla_w_in"], None)], dz0, x0, sc_a, "mla_in_bwd",
                                                         hosted=scatter(list(sent), sent)))
    landed(list(sent), got)

    dmods = [(dsh_a0, dsc_a0, dgt_a0, dsh_f0, dsc_f0, dgt_f0), (dsh_a1, dsc_a1, dgt_a1, dsh_f1, dsc_f1, dgt_f1)]
    d_ln_g = jnp.stack([jnp.concatenate([dg00, dg01], axis=0), jnp.concatenate([dg10, dg11], axis=0)])
    d_ln_b = jnp.stack([jnp.concatenate([db00, db01], axis=0), jnp.concatenate([db10, db11], axis=0)])
    return loss_part, dx0.reshape(nb, s, d), (parts, recv), dmods, d_ln_g, d_ln_b, dgq, dgkv, dbf[:, :FOX_HEADS]


def _pad_rows(a, rows):
    return jnp.pad(a, ((0, rows - a.shape[0]), (0, 0)))


def kernel(x, c, positions, mla_w_in, mla_g_q, mla_w_uq, mla_g_kv, mla_w_uk, mla_w_uv, mla_w_o, fox_w_in, fox_b_f, fox_w_o, ada_w, ada_b, ffn_w_gate, ffn_w_up, ffn_w_down, ln_g, ln_b, loss_target, m_mla_w_in, m_mla_g_q, m_mla_w_uq, m_mla_g_kv, m_mla_w_uk, m_mla_w_uv, m_mla_w_o, m_fox_w_in, m_fox_b_f, m_fox_w_o, m_ada_w, m_ada_b, m_ffn_w_gate, m_ffn_w_up, m_ffn_w_down, m_ln_g, m_ln_b, v_mla_w_in, v_mla_g_q, v_mla_w_uq, v_mla_g_kv, v_mla_w_uk, v_mla_w_uv, v_mla_w_o, v_fox_w_in, v_fox_b_f, v_fox_w_o, v_ada_w, v_ada_b, v_ffn_w_gate, v_ffn_w_up, v_ffn_w_down, v_ln_g, v_ln_b):
    args = dict(locals())
    nb, s, d = x.shape
    ax, ay, ac = lax.axis_index("x"), lax.axis_index("y"), lax.axis_index("c")
    chip = 2 * ax + ay
    dev = 2 * chip + ac
    n_dev = 2 * N_CHIPS
    n_all = nb * n_dev

    shard_shapes = {n: (args[n].shape if _SHARD_KIND[n] == "chunk" else args[n].shape[1:]) for n in _PACKED}

    def block(n, layer=None):
        w = args[n].reshape(shard_shapes[n]) if layer is None else args[n][layer]
        return _halves(w.astype(BF16))

    mla_names = [n for n in _PACKED if n.startswith("mla")]
    ffn_names = ("ffn_w_gate", "ffn_w_up", "ffn_w_down")
    fox_in_t = jnp.swapaxes(fox_w_in, 1, 2)[0].astype(BF16)
    fox_in_t = jnp.stack([fox_in_t[:, :d // 2], fox_in_t[:, d // 2:]])
    shards = {"ffn0": [block(n, 0) for n in ffn_names], "fox": [fox_in_t, block("fox_w_o")],
              "ffn1": [block(n, 1) for n in ffn_names]}

    ln_cols = ln_g.shape[-1]
    ln_blk = jnp.concatenate([ln_g.reshape(2 * DEPTH, ln_cols), ln_b.reshape(2 * DEPTH, ln_cols)], axis=0)
    early = jnp.concatenate([_pad_rows(c, 8), jnp.pad(_pad_rows(ln_blk, 8), ((0, 0), (0, d - ln_cols)))], axis=0)
    early, mla_all = all_gather8(early, "gather_c_ln_mla", hosted=_gather_comm([block(n) for n in mla_names]))
    wts = {}
    for n, g in zip(mla_names, mla_all):
        g = g.reshape(N_CHIPS, *shard_shapes[n])
        wts[n] = g.reshape(-1, g.shape[-1]) if _SHARD_KIND[n] == "rows" else _cols_to_full(g)
    early = early.reshape(n_dev, 16, d)
    c_all = early[:, :nb].reshape(n_all, d)
    ln_all = early.reshape(N_CHIPS, 2, 16, d)[:, 0, 8:8 + 4 * DEPTH, :ln_cols]
    ln_all = jnp.transpose(ln_all, (1, 0, 2)).reshape(4 * DEPTH, d)
    ln_g_full = ln_all[:2 * DEPTH].reshape(DEPTH, 2, d)
    ln_b_full = ln_all[2 * DEPTH:].reshape(DEPTH, 2, d)
    mod_part = ada_mod_part(c_all, ada_w, "ada_mod")
    ncol = mod_part.shape[-1]
    mod_g = all_gather8(mod_part.reshape(DEPTH * n_all, ncol), "gather_mod")
    mod_g = mod_g.reshape(N_CHIPS, 2, DEPTH, n_all, ncol)[:, 0]
    mod_full = jnp.transpose(mod_g, (1, 2, 0, 3)).reshape(DEPTH, n_all, N_CHIPS * ncol) + ada_b[:, None, :]
    mod_loc = lax.dynamic_slice_in_dim(mod_full, dev * nb, nb, axis=1)
    mods = [tuple(mod_loc[i, :, k * d:(k + 1) * d].reshape(nb, 1, d) for k in range(6)) for i in range(DEPTH)]

    loss_part, grad_x, (parts, recv), dmods, d_ln_g, d_ln_b, dgq, dgkv, dbf = _local_step(
        x, positions, loss_target, mods, wts, ln_g_full, ln_b_full, mla_g_q[0], mla_g_kv[0], fox_b_f[0], shards)
    loss = lax.psum(loss_part, ("x", "y", "c"))

    dmod_rows = jnp.stack([jnp.concatenate([v_.reshape(nb, d) for v_ in dm], axis=1) for dm in dmods])
    small = jnp.concatenate([
        d_ln_g.reshape(2 * DEPTH, d), d_ln_b.reshape(2 * DEPTH, d),
        jnp.pad(jnp.concatenate([dgq, dgkv, dbf], axis=1), ((0, 0), (0, d - 2 * MLA_QR - FOX_HEADS))),
        dmod_rows.reshape(DEPTH * nb * 6, d)], axis=0)
    n_small = small.shape[0]
    small_rows = -(-n_small // 8) * 8
    small_all = all_gather8(_pad_rows(small, small_rows), "gather_stats").reshape(n_dev, small_rows, d)
    stat_sum = sum_leading(small_all, "sum_stats")
    g_ln_g = lax.dynamic_slice_in_dim(stat_sum[:2 * DEPTH], chip * ln_cols, ln_cols, axis=1).reshape(DEPTH, 2, ln_cols)
    g_ln_b = lax.dynamic_slice_in_dim(stat_sum[2 * DEPTH:4 * DEPTH], chip * ln_cols, ln_cols, axis=1).reshape(DEPTH, 2, ln_cols)
    row = stat_sum[4 * DEPTH]
    g_gq = row[:MLA_QR].reshape(1, MLA_QR)
    g_gkv = row[MLA_QR:2 * MLA_QR].reshape(1, MLA_KVR)
    g_bf = row[2 * MLA_QR:2 * MLA_QR + FOX_HEADS].reshape(1, FOX_HEADS)
    base = 4 * DEPTH + 1
    dmod_all = small_all[:, base:base + DEPTH * nb * 6].reshape(n_dev, DEPTH, nb, 6 * d)
    dmod_all = jnp.transpose(dmod_all, (1, 0, 2, 3)).reshape(DEPTH, n_all, 6 * d)
    g_ada_b = sum_leading(jnp.transpose(dmod_all, (1, 0, 2)), "sum_ada_b")
    dmod_mine = lax.dynamic_slice_in_dim(dmod_all, chip * ncol, ncol, axis=2)
    g_ada_w = ada_grad(c_all.T, dmod_mine, "ada_grad")

    place = jnp.stack([dev, ac, chip]).astype(jnp.int32)
    bufs = []
    for n in _PACKED:
        if _SHARD_KIND[n] == "chunk":
            buf = None
            for layer in range(DEPTH):
                key = "%s/%d" % (n, layer)
                buf = sum_devices(parts[key], recv[key], place, "rs_sum_%s%d" % (n, layer), slot=(layer, DEPTH, buf))
        else:
            buf = sum_devices(parts[n], recv[n], place, "rs_sum_" + n)
        bufs.append(buf)
    joined = sibling_join_halves(bufs, "rs_join")
    g_big = {n: j.reshape(j.shape[0], 2 * j.shape[2], j.shape[3]) for n, j in zip(_PACKED, joined)}
    j = joined[_PACKED.index("fox_w_in")]
    g_big["fox_w_in"] = jnp.transpose(j, (0, 2, 1, 3)).reshape(1, j.shape[2], 2 * j.shape[3])

    g_out = {
        "mla_w_in": g_big["mla_w_in"], "mla_g_q": g_gq, "mla_w_uq": g_big["mla_w_uq"], "mla_g_kv": g_gkv,
        "mla_w_uk": g_big["mla_w_uk"], "mla_w_uv": g_big["mla_w_uv"], "mla_w_o": g_big["mla_w_o"],
        "fox_w_in": g_big["fox_w_in"], "fox_b_f": g_bf, "fox_w_o": g_big["fox_w_o"],
        "ada_w": g_ada_w, "ada_b": g_ada_b, "ffn_w_gate": g_big["ffn_w_gate"], "ffn_w_up": g_big["ffn_w_up"],
        "ffn_w_down": g_big["ffn_w_down"], "ln_g": g_ln_g, "ln_b": g_ln_b}
    names = ["mla_w_in", "mla_g_q", "mla_w_uq", "mla_g_kv", "mla_w_uk", "mla_w_uv", "mla_w_o", "fox_w_in", "fox_b_f",
             "fox_w_o", "ada_w", "ada_b", "ffn_w_gate", "ffn_w_up", "ffn_w_down", "ln_g", "ln_b"]
    small_names = ["mla_g_q", "mla_g_kv", "fox_b_f", "ada_b", "ln_g", "ln_b"]
    deltas, new_m, new_v = {}, {}, {}
    for n in names:
        if n in small_names:
            continue
        shp = args[n].shape
        if n in _TRANSPOSED:
            view = lambda a: jnp.swapaxes(a, 1, 2).reshape(-1, shp[1])
            back = lambda a: jnp.swapaxes(a.reshape(shp[0], shp[2], shp[1]), 1, 2)
        else:
            view = lambda a: a.reshape(-1, shp[-1])
            back = lambda a: a.reshape(shp)
        dl, mn, vn = adamw(view(args[n]), g_out[n].reshape(view(args[n]).shape), view(args["m_" + n]),
                           view(args["v_" + n]), "adamw_" + n)
        g_out[n], deltas[n], new_m[n], new_v[n] = back(g_out[n].reshape(view(args[n]).shape)), back(dl), back(mn), back(vn)

    def small_pack(prefix, src):
        flat = jnp.concatenate([src[prefix + n].reshape(-1) for n in small_names])
        size = -(-flat.shape[0] // (8 * 128)) * 8 * 128
        return jnp.pad(flat, (0, size - flat.shape[0])).reshape(-1, 128)

    sd, sm, sv = adamw(small_pack("", args), small_pack("", g_out), small_pack("m_", args), small_pack("v_", args),
                       "adamw_small")
    off = 0
    for n in small_names:
        shp = args[n].shape
        size = math.prod(shp)
        deltas[n] = sd.reshape(-1)[off:off + size].reshape(shp)
        new_m[n] = sm.reshape(-1)[off:off + size].reshape(shp)
        new_v[n] = sv.reshape(-1)[off:off + size].reshape(shp)
        off += size

    outs = [loss, grad_x]
    outs += [g_out[n].reshape(args[n].shape) for n in names]
    outs += [deltas[n] for n in names] + [new_m[n] for n in names] + [new_v[n] for n in names]
    return tuple(outs)
```

```python
import functools
import math

import numpy as np
import jax
import jax.numpy as jnp
from jax import lax
from jax.experimental import pallas as pl
from jax.experimental.pallas import tpu as pltpu

F32 = jnp.float32
BF16 = jnp.bfloat16
MESH = pl.DeviceIdType.MESH

D_MODEL = 1024
DEPTH = 2
MLA_HEADS = 8
MLA_NOPE = 128
MLA_ROPE = 64
MLA_V = 128
MLA_QR = 256
MLA_KVR = 256
ROPE_THETA = 10000.0
FOX_HEADS = 16
FOX_HD = 64
D_FF = 2816
N_CHIPS = 4
FF_CHUNK = D_FF // N_CHIPS
ALPHA = (2.0 * DEPTH) ** 0.25
EPS = 1e-5
ADAM_LR = 0.001
ADAM_B1 = 0.9
ADAM_B2 = 0.999
ADAM_EPS = 1e-08
ADAM_WD = 0.01
ADAM_STEP = 10

VMEM_LIMIT = 56 * 1024 * 1024
TOKEN_TILE = 512
WGRAD_TOKENS = 2048
ATTN_TILE = 512
GATE_BLOCK = 512
FOX_GROUP = 8
MLA_GROUP = 4
COMM_BLOCK_BYTES = 8 * 1024 * 1024
ADAMW_BLOCK_BYTES = 2 * 1024 * 1024


def _cp(n_axes):
    return pltpu.CompilerParams(dimension_semantics=("arbitrary",) * n_axes, vmem_limit_bytes=VMEM_LIMIT)


def _dot(a, b):
    return jnp.dot(a, b, preferred_element_type=F32)


def _dot_nt(a, b):
    return lax.dot_general(a, b, (((1,), (1,)), ((), ())), preferred_element_type=F32)


def _dot_tn(a, b):
    return lax.dot_general(a, b, (((0,), (0,)), ((), ())), preferred_element_type=F32)


def _dot_f32(a, b):
    return jnp.dot(a, b, preferred_element_type=F32, precision=lax.Precision.HIGHEST)


def _sds(shape, dtype):
    return jax.ShapeDtypeStruct(shape, dtype)


def _place():
    return lax.axis_index("x"), lax.axis_index("y"), lax.axis_index("c")


class _Hosted:
    def __init__(self, inputs, out_shape, sems, start, finish, in_place=False):
        self.inputs, self.out_shape, self.sems, self.start, self.finish = inputs, out_shape, sems, start, finish
        self.in_place = in_place


def _call(body, name, grid, in_specs, out_specs, out_shape, args, scratch_shapes=(), hosted=None):
    in_specs, out_specs, out_shape, scratch_shapes = list(in_specs), list(out_specs), list(out_shape), list(scratch_shapes)
    if hosted is None:
        return pl.pallas_call(body, name=name, grid=grid, in_specs=in_specs, out_specs=out_specs, out_shape=out_shape,
                              scratch_shapes=scratch_shapes, compiler_params=_cp(len(grid)))(*args)
    n_in, n_out, n_scr = len(in_specs), len(out_specs), len(scratch_shapes)
    h_in, h_out = len(hosted.inputs), len(hosted.out_shape)

    def carried(*refs):
        o0 = n_in + h_in
        s0 = o0 + n_out + h_out
        c_in, c_out, c_sem = refs[n_in:o0], refs[o0 + n_out:s0], refs[s0 + n_scr:]
        ids = [pl.program_id(a) for a in range(len(grid))]
        first = functools.reduce(jnp.logical_and, [i == 0 for i in ids])
        last = functools.reduce(jnp.logical_and, [i == g - 1 for i, g in zip(ids, grid)])

        @pl.when(first)
        def _():
            hosted.start(c_in, c_out, c_sem)

        body(*refs[:n_in], *refs[o0:o0 + n_out], *refs[s0:s0 + n_scr])

        @pl.when(last)
        def _():
            hosted.finish(c_in, c_out, c_sem)

    hbm = pl.BlockSpec(memory_space=pl.ANY)
    aliases = {n_in + k: n_out + k for k in range(h_in)} if hosted.in_place else {}
    res = pl.pallas_call(
        carried, name=name, grid=grid, in_specs=in_specs + [hbm] * h_in, out_specs=out_specs + [hbm] * h_out,
        out_shape=out_shape + list(hosted.out_shape), scratch_shapes=scratch_shapes + list(hosted.sems),
        input_output_aliases=aliases, compiler_params=_cp(len(grid)))(*args, *hosted.inputs)
    return res[:n_out], res[n_out:]


def mod_linear(x, shift, scale, w, out_dtype, name, tn=None, emit_u=False, w_rows=None):
    t, d = x.shape
    n = w.shape[1] if w_rows is None else w_rows
    tn = n if tn is None else tn
    tm = TOKEN_TILE
    tps = (t // shift.shape[0]) // tm

    def body(x_ref, sh_ref, sc_ref, w_ref, o_ref, *rest):
        u = (x_ref[...] * (1.0 + sc_ref[...]) + sh_ref[...]).astype(BF16)
        o_ref[...] = (_dot(u, w_ref[...]) if w_rows is None else _dot_nt(u, w_ref[...])).astype(out_dtype)
        if emit_u:
            @pl.when(pl.program_id(1) == 0)
            def _():
                rest[0][...] = u

    vec = pl.BlockSpec((None, 1, d), lambda i, j: (i // tps, 0, 0))
    out_shape = [_sds((t, n), out_dtype)]
    out_specs = [pl.BlockSpec((tm, tn), lambda i, j: (i, j))]
    if emit_u:
        out_shape.append(_sds((t, d), BF16))
        out_specs.append(pl.BlockSpec((tm, d), lambda i, j: (i, 0)))
    w_spec = pl.BlockSpec((d, tn), lambda i, j: (0, j)) if w_rows is None else pl.BlockSpec((tn, d), lambda i, j: (j, 0))
    res = pl.pallas_call(
        body, name=name, grid=(t // tm, n // tn),
        in_specs=[pl.BlockSpec((tm, d), lambda i, j: (i, 0)), vec, vec, w_spec],
        out_specs=out_specs, out_shape=out_shape, compiler_params=_cp(2),
    )(x, shift, scale, w)
    return res if emit_u else res[0]


def _rms(h, g):
    rstd = lax.rsqrt(jnp.mean(h * h, axis=-1, keepdims=True) + EPS)
    return h * rstd, rstd


def mla_mid_fwd(h, g_q, g_kv, w_uq, w_uk, w_uv, cos8, sin8, cos64, sin64s, swap64, rope_to_heads, dup64, name):
    t = h.shape[0]
    tm = TOKEN_TILE
    hq = MLA_HEADS * MLA_NOPE
    hr = MLA_HEADS * MLA_ROPE // 2

    def body(h_ref, gq_ref, gkv_ref, wuq_ref, wuk_ref, wuv_ref, c8_ref, s8_ref, c64_ref, s64_ref, sw_ref, p_ref, d_ref,
             q_ref, kn_ref, v_ref, kr_ref, cq_ref, ckv_ref):
        hh = h_ref[...]
        cq = (_rms(hh[:, :MLA_QR], None)[0] * gq_ref[...]).astype(BF16)
        ckv = (_rms(hh[:, MLA_QR:MLA_QR + MLA_KVR], None)[0] * gkv_ref[...]).astype(BF16)
        cq_ref[...] = cq
        ckv_ref[...] = ckv
        q = _dot(cq, wuq_ref[...])
        x1 = q[:, hq:hq + hr]
        x2 = q[:, hq + hr:]
        cs = c8_ref[...]
        sn = s8_ref[...]
        rot = jnp.concatenate([x1 * cs - x2 * sn, x2 * cs + x1 * sn], axis=1).astype(BF16)
        q_ref[...] = jnp.concatenate([q[:, :hq].astype(BF16), _dot(rot, p_ref[...]).astype(BF16)], axis=1)
        kn_ref[...] = _dot(ckv, wuk_ref[...]).astype(BF16)
        v_ref[...] = _dot(ckv, wuv_ref[...]).astype(BF16)
        kr = hh[:, MLA_QR + MLA_KVR:]
        kr = (kr * c64_ref[...] + _dot_f32(kr, sw_ref[...]) * s64_ref[...]).astype(BF16)
        kr_ref[...] = _dot(kr, d_ref[...]).astype(BF16)

    def rows(n):
        return pl.BlockSpec((tm, n), lambda i: (i, 0))

    def whole(a):
        return pl.BlockSpec(a.shape, lambda i: (0,) * a.ndim)

    nq = w_uq.shape[1]
    return pl.pallas_call(
        body, name=name, grid=(t // tm,),
        in_specs=[rows(h.shape[1]), whole(g_q), whole(g_kv), whole(w_uq), whole(w_uk), whole(w_uv),
                  rows(hr), rows(hr), rows(MLA_ROPE), rows(MLA_ROPE), whole(swap64), whole(rope_to_heads), whole(dup64)],
        out_specs=[rows(nq), rows(hq), rows(hq), rows(2 * MLA_ROPE), rows(MLA_QR), rows(MLA_KVR)],
        out_shape=[_sds((t, nq), BF16), _sds((t, hq), BF16), _sds((t, hq), BF16), _sds((t, 2 * MLA_ROPE), BF16),
                   _sds((t, MLA_QR), BF16), _sds((t, MLA_KVR), BF16)],
        compiler_params=_cp(1),
    )(h, g_q, g_kv, w_uq, w_uk, w_uv, cos8, sin8, cos64, sin64s, swap64, rope_to_heads, dup64)


def _pick_lane(tile, idx):
    lane = lax.broadcasted_iota(jnp.int32, tile.shape, 1)
    return jnp.sum(jnp.where(lane == idx, tile, 0.0), axis=1, keepdims=True)


def _pick_row(tile, idx):
    row = lax.broadcasted_iota(jnp.int32, tile.shape, 0)
    return jnp.sum(jnp.where(row == idx, tile, 0.0), axis=0, keepdims=True)


def _put_lane(tile, idx, col):
    lane = lax.broadcasted_iota(jnp.int32, tile.shape, 1)
    return jnp.where(lane == idx, col, tile)


def _put_row(tile, idx, row):
    r = lax.broadcasted_iota(jnp.int32, tile.shape, 0)
    return tile + jnp.where(r == idx, row, 0.0)


def _causal_softmax_blocks(i, tq, heads):
    def block(j, carry, masked):
        new = []
        for (score_fn, pv_fn, _), (m, l, acc) in zip(heads, carry):
            sc = score_fn(j)
            if masked:
                keep = lax.broadcasted_iota(jnp.int32, sc.shape, 0) >= lax.broadcasted_iota(jnp.int32, sc.shape, 1)
                sc = jnp.where(keep, sc, -1e30)
            m_new = jnp.maximum(m, jnp.max(sc, axis=1, keepdims=True))
            a = jnp.exp(m - m_new)
            p = jnp.exp(sc - m_new)
            new.append((m_new, a * l + jnp.sum(p, axis=1, keepdims=True), a * acc + pv_fn(j, p.astype(BF16))))
        return tuple(new)

    init = tuple((jnp.full((tq, 1), -1e30, F32), jnp.zeros((tq, 1), F32), jnp.zeros((tq, dv), F32)) for _, _, dv in heads)
    carry = lax.fori_loop(0, i, lambda j, c: block(j, c, False), init)
    return [(acc / l, m + jnp.log(l)) for m, l, acc in block(i, carry, True)]


def fox_attn_fwd(qkv, cum, cum_rows, nb, name, hosted=None):
    t = qkv.shape[0]
    s = t // nb
    tq = ATTN_TILE
    nq = s // tq
    wide = FOX_GROUP * FOX_HD
    ngroups = FOX_HEADS // FOX_GROUP
    scale = FOX_HD ** -0.5

    def body(q_ref, k_ref, v_ref, cum_ref, cr_ref, o_ref, lse_ref):
        i = pl.program_id(1)
        hg = pl.program_id(2)

        @pl.when(hg == 0)
        def _():
            lse_ref[...] = jnp.zeros_like(lse_ref)

        low = lax.broadcasted_iota(jnp.int32, (tq, 128), 1) < FOX_HD
        cum_t = cum_ref[...]

        def rows_of(j):
            return pl.ds(pl.multiple_of(j * tq, tq), tq)

        def head(a):
            hd = FOX_GROUP * hg + a
            cols = slice(128 * (a // 2), 128 * (a // 2) + 128)
            q = q_ref[:, cols]
            qa = jnp.where(low if a % 2 == 0 else jnp.logical_not(low), q, jnp.zeros_like(q)) * scale
            fq = _pick_lane(cum_t, hd)
            return (lambda j: _dot_nt(qa, k_ref[rows_of(j), cols]) + fq - _pick_row(cr_ref[j], hd),
                    lambda j, p: _dot(p, v_ref[rows_of(j), cols]), 2 * FOX_HD)

        res = _causal_softmax_blocks(i, tq, [head(a) for a in range(FOX_GROUP)])
        o_ref[...] = jnp.concatenate([jnp.where(low, res[a][0], res[a + 1][0]) for a in range(0, FOX_GROUP, 2)],
                                     axis=1).astype(BF16)
        lse_t = lse_ref[...]
        for a in range(FOX_GROUP):
            lse_t = _put_lane(lse_t, FOX_GROUP * hg + a, res[a][1])
        lse_ref[...] = lse_t

    return _call(
        body, name, (nb, nq, ngroups),
        [pl.BlockSpec((tq, wide), lambda b, i, hg: (b * nq + i, hg)),
         pl.BlockSpec((s, wide), lambda b, i, hg: (b, ngroups + hg)),
         pl.BlockSpec((s, wide), lambda b, i, hg: (b, 2 * ngroups + hg)),
         pl.BlockSpec((tq, 128), lambda b, i, hg: (b * nq + i, 0)),
         pl.BlockSpec((nq, 16, tq), lambda b, i, hg: (b, 0, 0))],
        [pl.BlockSpec((tq, wide), lambda b, i, hg: (b * nq + i, hg)),
         pl.BlockSpec((tq, 128), lambda b, i, hg: (b * nq + i, 0))],
        [_sds((t, D_MODEL), BF16), _sds((t, 128), F32)], (qkv, qkv, qkv, cum, cum_rows), hosted=hosted)


def mla_attn_fwd(q, kn, kr2, v, nb, name, hosted=None):
    t = q.shape[0]
    s = t // nb
    tq = ATTN_TILE
    nq = s // tq
    ngroups = MLA_HEADS // MLA_GROUP
    wide = MLA_GROUP * MLA_NOPE
    rwide = MLA_GROUP * MLA_ROPE
    scale = (MLA_NOPE + MLA_ROPE) ** -0.5

    def body(qn_ref, qr_ref, kn_ref, kr_ref, v_ref, o_ref, lse_ref):
        i = pl.program_id(1)
        hg = pl.program_id(2)

        @pl.when(hg == 0)
        def _():
            lse_ref[...] = jnp.zeros_like(lse_ref)

        low = lax.broadcasted_iota(jnp.int32, (tq, 128), 1) < MLA_ROPE

        def rows_of(j):
            return pl.ds(pl.multiple_of(j * tq, tq), tq)

        def head(a):
            cols = slice(a * MLA_NOPE, (a + 1) * MLA_NOPE)
            qr = qr_ref[:, 128 * (a // 2):128 * (a // 2) + 128]
            q_cat = jnp.concatenate([qn_ref[:, cols], jnp.where(low if a % 2 == 0 else jnp.logical_not(low), qr,
                                                                jnp.zeros_like(qr))], axis=1)
            return (lambda j: _dot_nt(q_cat, jnp.concatenate([kn_ref[rows_of(j), cols], kr_ref[rows_of(j), :]], axis=1)) * scale,
                    lambda j, p: _dot(p, v_ref[rows_of(j), cols]), MLA_V)

        res = _causal_softmax_blocks(i, tq, [head(a) for a in range(MLA_GROUP)])
        o_ref[...] = jnp.concatenate([r[0] for r in res], axis=1).astype(BF16)
        lse_t = lse_ref[...]
        for a in range(MLA_GROUP):
            lse_t = _put_lane(lse_t, MLA_GROUP * hg + a, res[a][1])
        lse_ref[...] = lse_t

    rope0 = MLA_HEADS * MLA_NOPE // rwide
    return _call(
        body, name, (nb, nq, ngroups),
        [pl.BlockSpec((tq, wide), lambda b, i, hg: (b * nq + i, hg)),
         pl.BlockSpec((tq, rwide), lambda b, i, hg: (b * nq + i, rope0 + hg)),
         pl.BlockSpec((s, wide), lambda b, i, hg: (b, hg)),
         pl.BlockSpec((s, 128), lambda b, i, hg: (b, 0)),
         pl.BlockSpec((s, wide), lambda b, i, hg: (b, hg))],
        [pl.BlockSpec((tq, wide), lambda b, i, hg: (b * nq + i, hg)),
         pl.BlockSpec((tq, 128), lambda b, i, hg: (b * nq + i, 0))],
        [_sds((t, MLA_HEADS * MLA_V), BF16), _sds((t, 128), F32)], (q, q, kn, kr2, v), hosted=hosted)


def rows16(a, name):
    t = a.shape[0]
    tq = ATTN_TILE

    def body(a_ref, o_ref):
        o_ref[...] = a_ref[...].T[:16, :]

    return pl.pallas_call(
        body, name=name, grid=(t // tq,), in_specs=[pl.BlockSpec((tq, 128), lambda n: (n, 0))],
        out_specs=pl.BlockSpec((None, 16, tq), lambda n: (n, 0, 0)), out_shape=_sds((t // tq, 16, tq), F32),
        compiler_params=_cp(1),
    )(a)


def tokens128(rows, onehot, name):
    nblk, _, tq = rows.shape

    def body(r_ref, e_ref, o_ref):
        o_ref[...] = lax.dot_general(r_ref[...], e_ref[...], (((0,), (0,)), ((), ())), preferred_element_type=F32,
                                     precision=lax.Precision.HIGHEST)

    return pl.pallas_call(
        body, name=name, grid=(nblk,),
        in_specs=[pl.BlockSpec((None, 16, tq), lambda n: (n, 0, 0)), pl.BlockSpec((16, 128), lambda n: (0, 0))],
        out_specs=pl.BlockSpec((tq, 128), lambda n: (n, 0)), out_shape=_sds((nblk * tq, 128), F32),
        compiler_params=_cp(1),
    )(rows, onehot)


def _layer_norm(z, g, b):
    mu = jnp.mean(z, axis=-1, keepdims=True)
    zc = z - mu
    rstd = lax.rsqrt(jnp.mean(zc * zc, axis=-1, keepdims=True) + EPS)
    xhat = zc * rstd
    return xhat * g + b, xhat, rstd


def linear_resid_ln(a, w, x_in, gate, ln_g, ln_b, name):
    t, kdim = a.shape
    d = w.shape[1]
    tm = TOKEN_TILE
    tps = (t // gate.shape[0]) // tm

    def body(a_ref, w_ref, x_ref, gt_ref, g_ref, b_ref, y_ref, xo_ref):
        y = _dot(a_ref[...], w_ref[...])
        y_ref[...] = y
        z = ALPHA * x_ref[...] + (1.0 + gt_ref[...]) * y
        xo_ref[...] = _layer_norm(z, g_ref[...], b_ref[...])[0]

    rows = pl.BlockSpec((tm, d), lambda i: (i, 0))
    vec = pl.BlockSpec((1, d), lambda i: (0, 0))
    return pl.pallas_call(
        body, name=name, grid=(t // tm,),
        in_specs=[pl.BlockSpec((tm, kdim), lambda i: (i, 0)), pl.BlockSpec((kdim, d), lambda i: (0, 0)), rows,
                  pl.BlockSpec((None, 1, d), lambda i: (i // tps, 0, 0)), vec, vec],
        out_specs=[rows, rows], out_shape=[_sds((t, d), F32), _sds((t, d), F32)],
        compiler_params=_cp(1),
    )(a, w, x_in, gate, ln_g, ln_b)


def _resident(a):
    return pl.BlockSpec(a.shape, lambda *_: (0,) * a.ndim, pipeline_mode=pl.Buffered(1))


def ffn_fwd(x_in, shift, scale, gate, wg, wu, wd, ln_g, ln_b, name, hosted=None):
    t, d = x_in.shape
    c, _, fc = wg.shape
    tm = TOKEN_TILE
    tps = (t // gate.shape[0]) // tm

    def body(x_ref, sh_ref, sc_ref, gt_ref, wg_ref, wu_ref, wd_ref, g_ref, b_ref,
             u_ref, hg_ref, hu_ref, y_ref, xo_ref, acc_ref):
        cc = pl.program_id(1)

        @pl.when(cc == 0)
        def _():
            u_ref[...] = (x_ref[...] * (1.0 + sc_ref[...]) + sh_ref[...]).astype(BF16)
            acc_ref[...] = jnp.zeros_like(acc_ref)

        u = u_ref[...]
        hg = _dot(u, wg_ref[cc])
        hu = _dot(u, wu_ref[cc])
        hg_ref[...] = hg.astype(BF16)
        hu_ref[...] = hu.astype(BF16)
        act = (hg * jax.nn.sigmoid(hg) * hu).astype(BF16)
        acc_ref[...] += _dot(act, wd_ref[cc])

        @pl.when(cc == c - 1)
        def _():
            y = acc_ref[...]
            y_ref[...] = y
            z = ALPHA * x_ref[...] + (1.0 + gt_ref[...]) * y
            xo_ref[...] = _layer_norm(z, g_ref[...], b_ref[...])[0]

    rows = pl.BlockSpec((tm, d), lambda i, cc: (i, 0))
    bvec = pl.BlockSpec((None, 1, d), lambda i, cc: (i // tps, 0, 0))
    vec = pl.BlockSpec((1, d), lambda i, cc: (0, 0))
    hspec = pl.BlockSpec((None, tm, fc), lambda i, cc: (cc, i, 0))
    wcol = _resident(wg)
    return _call(
        body, name, (t // tm, c),
        [rows, bvec, bvec, bvec, wcol, wcol, _resident(wd), vec, vec],
        [rows, hspec, hspec, rows, rows],
        [_sds((t, d), BF16), _sds((c, t, fc), BF16), _sds((c, t, fc), BF16), _sds((t, d), F32), _sds((t, d), F32)],
        (x_in, shift, scale, gate, wg, wu, wd, ln_g, ln_b), scratch_shapes=[pltpu.VMEM((tm, d), F32)], hosted=hosted)


def fox_gate_fwd(hf, b_f, tri, n_batch, name):
    t, n = hf.shape
    blk = tri.shape[0]
    nb = (t // n_batch) // blk

    def body(hf_ref, b_ref, tri_ref, o_ref, carry_ref):
        @pl.when(pl.program_id(1) == 0)
        def _():
            carry_ref[...] = jnp.zeros_like(carry_ref)

        xx = hf_ref[...] + b_ref[...]
        lf = jnp.minimum(xx, 0.0) - jnp.log(1.0 + jnp.exp(-jnp.abs(xx)))
        cum = _dot_f32(tri_ref[...], lf) + carry_ref[...]
        o_ref[...] = cum
        carry_ref[...] = cum[blk - 1:blk, :]

    return pl.pallas_call(
        body, name=name, grid=(n_batch, nb),
        in_specs=[pl.BlockSpec((blk, n), lambda bb, i: (bb * nb + i, 0)), pl.BlockSpec((1, n), lambda bb, i: (0, 0)),
                  pl.BlockSpec((blk, blk), lambda bb, i: (0, 0))],
        out_specs=pl.BlockSpec((blk, n), lambda bb, i: (bb * nb + i, 0)),
        out_shape=_sds((t, n), F32), scratch_shapes=[pltpu.VMEM((1, n), F32)],
        compiler_params=_cp(2),
    )(hf, b_f, tri)


def ln_bwd(dxo, x_in, y, gate, ln_g, name, target=None):
    t, d = dxo.shape
    nb = gate.shape[0]
    tm = TOKEN_TILE
    tps = (t // nb) // tm
    with_loss = target is not None

    def body(dxo_ref, *refs):
        if with_loss:
            t_ref, x_ref, y_ref, gt_ref, g_ref, dz_ref, dy_ref, dg_ref, db_ref, dgt_ref, l_ref = refs
        else:
            x_ref, y_ref, gt_ref, g_ref, dz_ref, dy_ref, dg_ref, db_ref, dgt_ref = refs
        i = pl.program_id(0)

        @pl.when(i == 0)
        def _():
            dg_ref[...] = jnp.zeros_like(dg_ref)
            db_ref[...] = jnp.zeros_like(db_ref)
            if with_loss:
                l_ref[...] = jnp.zeros_like(l_ref)

        @pl.when(i % tps == 0)
        def _():
            dgt_ref[...] = jnp.zeros_like(dgt_ref)

        yy = y_ref[...]
        g1 = 1.0 + gt_ref[...]
        z = ALPHA * x_ref[...] + g1 * yy
        _, xhat, rstd = _layer_norm(z, 1.0, 0.0)
        dxo_v = dxo_ref[...]
        if with_loss:
            err = dxo_v - t_ref[...]
            l_ref[...] += jnp.sum(err * err, axis=0, keepdims=True)
            dxo_v = err / d
        dg_ref[...] += jnp.sum(dxo_v * xhat, axis=0, keepdims=True)
        db_ref[...] += jnp.sum(dxo_v, axis=0, keepdims=True)
        dxh = dxo_v * g_ref[...]
        dz = rstd * (dxh - jnp.mean(dxh, axis=-1, keepdims=True) - xhat * jnp.mean(dxh * xhat, axis=-1, keepdims=True))
        dz_ref[...] = dz
        dy_ref[...] = (g1 * dz).astype(BF16)
        dgt_ref[...] += jnp.sum(dz * yy, axis=0, keepdims=True)

    rows = pl.BlockSpec((tm, d), lambda i: (i, 0))
    vec = pl.BlockSpec((1, d), lambda i: (0, 0))
    bvec = pl.BlockSpec((None, 1, d), lambda i: (i // tps, 0, 0))
    return pl.pallas_call(
        body, name=name, grid=(t // tm,), in_specs=[rows] * (4 if with_loss else 3) + [bvec, vec],
        out_specs=[rows, rows, vec, vec, bvec] + ([vec] if with_loss else []),
        out_shape=[_sds((t, d), F32), _sds((t, d), BF16), _sds((1, d), F32), _sds((1, d), F32), _sds((nb, 1, d), F32)]
        + ([_sds((1, d), F32)] if with_loss else []),
        compiler_params=_cp(1),
    )(dxo, *([target] if with_loss else []), x_in, y, gate, ln_g)


def _mod_bwd_tail(du, dz_ref, x_ref, sc_ref, dx_ref, dsc_ref, dsh_ref, first):
    @pl.when(first)
    def _():
        dsc_ref[...] = jnp.zeros_like(dsc_ref)
        dsh_ref[...] = jnp.zeros_like(dsh_ref)

    dx_ref[...] = ALPHA * dz_ref[...] + du * (1.0 + sc_ref[...])
    dsc_ref[...] += jnp.sum(du * x_ref[...], axis=0, keepdims=True)
    dsh_ref[...] += jnp.sum(du, axis=0, keepdims=True)


def ffn_bwd(dy, hg, hu, wg, wu, wd, dz, x_in, scale, name, hosted=None):
    t, d = dy.shape
    c, _, fc = wg.shape
    nb = scale.shape[0]
    tm = TOKEN_TILE
    tps = (t // nb) // tm

    def body(dy_ref, hg_ref, hu_ref, wg_ref, wu_ref, wd_ref, dz_ref, x_ref, sc_ref,
             dhg_ref, dhu_ref, act_ref, dx_ref, dsc_ref, dsh_ref, acc_ref):
        i = pl.program_id(0)
        cc = pl.program_id(1)

        @pl.when(cc == 0)
        def _():
            acc_ref[...] = jnp.zeros_like(acc_ref)

        hgv = hg_ref[...].astype(F32)
        huv = hu_ref[...].astype(F32)
        da = _dot_nt(dy_ref[...], wd_ref[cc])
        sg = jax.nn.sigmoid(hgv)
        sl = hgv * sg
        act_ref[...] = (sl * huv).astype(BF16)
        dhu = (da * sl).astype(BF16)
        dhg = (da * huv * (sg * (1.0 + hgv * (1.0 - sg)))).astype(BF16)
        dhu_ref[...] = dhu
        dhg_ref[...] = dhg
        acc_ref[...] += _dot_nt(dhg, wg_ref[cc]) + _dot_nt(dhu, wu_ref[cc])

        @pl.when(cc == c - 1)
        def _():
            _mod_bwd_tail(acc_ref[...], dz_ref, x_ref, sc_ref, dx_ref, dsc_ref, dsh_ref, i % tps == 0)

    rows = pl.BlockSpec((tm, d), lambda i, cc: (i, 0))
    bvec = pl.BlockSpec((None, 1, d), lambda i, cc: (i // tps, 0, 0))
    hspec = pl.BlockSpec((None, tm, fc), lambda i, cc: (cc, i, 0))
    wcol = _resident(wg)
    return _call(
        body, name, (t // tm, c),
        [rows, hspec, hspec, wcol, wcol, _resident(wd), rows, rows, bvec],
        [hspec, hspec, hspec, rows, bvec, bvec],
        [_sds((c, t, fc), BF16), _sds((c, t, fc), BF16), _sds((c, t, fc), BF16), _sds((t, d), F32),
         _sds((nb, 1, d), F32), _sds((nb, 1, d), F32)],
        (dy, hg, hu, wg, wu, wd, dz, x_in, scale), scratch_shapes=[pltpu.VMEM((tm, d), F32)], hosted=hosted)


def linear_nt_mod_bwd(pairs, dz, x_in, scale, name, hosted=None):
    t, d = dz.shape
    nb = scale.shape[0]
    tm = TOKEN_TILE
    tps = (t // nb) // tm
    npairs = len(pairs)

    def body(*refs):
        dh_refs = refs[:npairs]
        w_refs = refs[npairs:2 * npairs]
        dz_ref, x_ref, sc_ref, dx_ref, dsc_ref, dsh_ref = refs[2 * npairs:]
        du = None
        for (_, _, blk), dh_ref, w_ref in zip(pairs, dh_refs, w_refs):
            dh = dh_ref[...].astype(BF16)
            term = _dot_nt(dh, w_ref[...]) if blk is None else _dot(dh, w_ref[...])
            du = term if du is None else du + term
        _mod_bwd_tail(du, dz_ref, x_ref, sc_ref, dx_ref, dsc_ref, dsh_ref, pl.program_id(0) % tps == 0)

    rows = pl.BlockSpec((tm, d), lambda i: (i, 0))
    bvec = pl.BlockSpec((None, 1, d), lambda i: (i // tps, 0, 0))
    in_specs = [pl.BlockSpec((tm, dh.shape[1]), lambda i: (i, 0)) for dh, _, _ in pairs]
    for dh, w, blk in pairs:
        if blk is None:
            in_specs.append(pl.BlockSpec(w.shape, lambda i: (0, 0)))
        else:
            in_specs.append(pl.BlockSpec((dh.shape[1], d), lambda i, blk=blk: (blk, 0)))
    in_specs += [rows, rows, bvec]
    return _call(
        body, name, (t // tm,), in_specs, [rows, bvec, bvec],
        [_sds((t, d), F32), _sds((nb, 1, d), F32), _sds((nb, 1, d), F32)],
        (*[dh for dh, _, _ in pairs], *[w for _, w, _ in pairs], dz, x_in, scale), hosted=hosted)


def linear_nt_delta(dy, w_o, o, head_sel, name):
    t, d = dy.shape
    hdv = w_o.shape[0]
    tm = TOKEN_TILE

    def body(dy_ref, w_ref, o_ref, sel_ref, do_ref, dl_ref):
        do = _dot_nt(dy_ref[...], w_ref[...])
        do_ref[...] = do.astype(BF16)
        dl_ref[...] = _dot_f32(do * o_ref[...].astype(F32), sel_ref[...])

    return pl.pallas_call(
        body, name=name, grid=(t // tm,),
        in_specs=[pl.BlockSpec((tm, d), lambda i: (i, 0)), pl.BlockSpec((hdv, d), lambda i: (0, 0)),
                  pl.BlockSpec((tm, hdv), lambda i: (i, 0)), pl.BlockSpec(head_sel.shape, lambda i: (0, 0))],
        out_specs=[pl.BlockSpec((tm, hdv), lambda i: (i, 0)), pl.BlockSpec((tm, 128), lambda i: (i, 0))],
        out_shape=[_sds((t, hdv), BF16), _sds((t, 128), F32)], compiler_params=_cp(1),
    )(dy, w_o, o, head_sel)


def _attn_bwd_blocks(j, nk, tk, scale, heads):
    def block(i, carry, masked):
        new = []
        for hd, (dk_acc, dv_acc, dfk_acc) in zip(heads, carry):
            qb = hd["q"](i)
            dob = hd["do"](i)
            lse_row, dl_row = hd["rows"](i)
            st = _dot_nt(hd["k"], qb)
            if scale is not None:
                st = st * scale
            if hd["bias"] is not None:
                fq_row, fk_col = hd["bias"](i)
                st = st + fq_row - fk_col
            if masked:
                keep = lax.broadcasted_iota(jnp.int32, st.shape, 1) >= lax.broadcasted_iota(jnp.int32, st.shape, 0)
                st = jnp.where(keep, st, -1e30)
            pt = jnp.exp(st - lse_row)
            dv_acc = dv_acc + _dot(pt.astype(BF16), dob)
            dst = pt * (_dot_nt(hd["v"], dob) - dl_row)
            if hd["add_dfq"] is not None:
                dfk_acc = dfk_acc - jnp.sum(dst, axis=1, keepdims=True)
                hd["add_dfq"](i, jnp.sum(dst, axis=0, keepdims=True))
            dsb = (dst if scale is None else dst * scale).astype(BF16)
            dk_acc = dk_acc + _dot(dsb, qb)
            hd["add_dq"](i, _dot_tn(dsb, hd["k"] if scale is not None else hd["k_scaled"]))
            new.append((dk_acc, dv_acc, dfk_acc))
        return tuple(new)

    init = tuple((jnp.zeros((tk, hd["k"].shape[1]), F32), jnp.zeros((tk, hd["v"].shape[1]), F32), jnp.zeros((tk, 1), F32))
                 for hd in heads)
    carry = block(j, init, True)
    return lax.fori_loop(j + 1, nk, lambda i, c: block(i, c, False), carry)


def fox_attn_bwd(qkv, do, cum, cum_rows, lse_rows, delta_rows, nb, name, hosted=None):
    t = qkv.shape[0]
    s = t // nb
    tk = ATTN_TILE
    nk = s // tk
    scale = FOX_HD ** -0.5

    def body(q_ref, k_ref, v_ref, do_ref, cum_ref, cr_ref, lr_ref, dr_ref, dq_ref, dk_ref, dv_ref, dfq_ref, dfk_ref):
        hg = pl.program_id(1)
        j = pl.program_id(2)

        @pl.when(j == 0)
        def _():
            dq_ref[...] = jnp.zeros_like(dq_ref)

        @pl.when((j == 0) & (hg == 0))
        def _():
            dfq_ref[...] = jnp.zeros_like(dfq_ref)
            dfk_ref[...] = jnp.zeros_like(dfk_ref)

        low = lax.broadcasted_iota(jnp.int32, (tk, 128), 1) < FOX_HD
        cum_t = cum_ref[...]

        def rows_of(i):
            return pl.ds(pl.multiple_of(i * tk, tk), tk)

        def head(a):
            hd = FOX_GROUP * hg + a
            cols = slice(128 * (a // 2), 128 * (a // 2) + 128)
            half = low if a % 2 == 0 else jnp.logical_not(low)
            kb = k_ref[:, cols]
            vb = v_ref[:, cols]
            fk = _pick_lane(cum_t, hd)

            def add_dq(i, val):
                dq_ref[rows_of(i), cols] += val

            def add_dfq(i, val):
                dfq_ref[i] = _put_row(dfq_ref[i], hd, val)

            ka = jnp.where(half, kb, jnp.zeros_like(kb))
            return dict(q=lambda i: q_ref[rows_of(i), cols] * scale, do=lambda i: do_ref[rows_of(i), cols],
                        k=ka, k_scaled=ka * scale, v=jnp.where(half, vb, jnp.zeros_like(vb)),
                        rows=lambda i: (_pick_row(lr_ref[i], hd), _pick_row(dr_ref[i], hd)),
                        bias=lambda i: (_pick_row(cr_ref[i], hd), fk), add_dq=add_dq, add_dfq=add_dfq)

        res = _attn_bwd_blocks(j, nk, tk, None, [head(a) for a in range(FOX_GROUP)])
        dk_ref[...] = jnp.concatenate([jnp.where(low, res[a][0], res[a + 1][0]) for a in range(0, FOX_GROUP, 2)],
                                      axis=1).astype(BF16)
        dv_ref[...] = jnp.concatenate([jnp.where(low, res[a][1], res[a + 1][1]) for a in range(0, FOX_GROUP, 2)],
                                      axis=1).astype(BF16)
        for a in range(FOX_GROUP):
            dfk_ref[j] = _put_row(dfk_ref[j], FOX_GROUP * hg + a, jnp.broadcast_to(res[a][2], (tk, 128)).T[0:1, :])

    wide = FOX_GROUP * FOX_HD
    ngroups = FOX_HEADS // FOX_GROUP
    rowsp = pl.BlockSpec((nk, 16, tk), lambda b, hg, j: (b, 0, 0))
    return _call(
        body, name, (nb, ngroups, nk),
        [pl.BlockSpec((s, wide), lambda b, hg, j: (b, hg)),
         pl.BlockSpec((tk, wide), lambda b, hg, j: (b * nk + j, ngroups + hg)),
         pl.BlockSpec((tk, wide), lambda b, hg, j: (b * nk + j, 2 * ngroups + hg)),
         pl.BlockSpec((s, wide), lambda b, hg, j: (b, hg)),
         pl.BlockSpec((tk, 128), lambda b, hg, j: (b * nk + j, 0)),
         rowsp, rowsp, rowsp],
        [pl.BlockSpec((s, wide), lambda b, hg, j: (b, hg)),
         pl.BlockSpec((tk, wide), lambda b, hg, j: (b * nk + j, hg)),
         pl.BlockSpec((tk, wide), lambda b, hg, j: (b * nk + j, hg)),
         rowsp, rowsp],
        [_sds((t, D_MODEL), F32), _sds((t, D_MODEL), BF16), _sds((t, D_MODEL), BF16),
         _sds((t // tk, 16, tk), F32), _sds((t // tk, 16, tk), F32)],
        (qkv, qkv, qkv, do, cum, cum_rows, lse_rows, delta_rows), hosted=hosted)


def mla_attn_bwd(q, kn, kr2, v, do, lse_rows, delta_rows, nb, name, hosted=None):
    t = q.shape[0]
    s = t // nb
    tk = ATTN_TILE
    nk = s // tk
    ngroups = MLA_HEADS // MLA_GROUP
    wide = MLA_GROUP * MLA_NOPE
    rwide = MLA_GROUP * MLA_ROPE
    scale = (MLA_NOPE + MLA_ROPE) ** -0.5

    def body(qn_ref, qr_ref, kn_ref, kr_ref, v_ref, do_ref, lr_ref, dr_ref, dqn_ref, dqr_ref, dkn_ref, dkr_ref, dv_ref):
        hg = pl.program_id(1)
        j = pl.program_id(2)

        @pl.when(j == 0)
        def _():
            dqn_ref[...] = jnp.zeros_like(dqn_ref)
            dqr_ref[...] = jnp.zeros_like(dqr_ref)

        low = lax.broadcasted_iota(jnp.int32, (tk, 128), 1) < MLA_ROPE
        kr = kr_ref[...]

        def rows_of(i):
            return pl.ds(pl.multiple_of(i * tk, tk), tk)

        def head(a):
            cols = slice(a * MLA_NOPE, (a + 1) * MLA_NOPE)
            rcols = slice(128 * (a // 2), 128 * (a // 2) + 128)
            mine = low if a % 2 == 0 else jnp.logical_not(low)
            hd = MLA_GROUP * hg + a

            def q_fn(i):
                qr = qr_ref[rows_of(i), rcols]
                return jnp.concatenate([qn_ref[rows_of(i), cols], jnp.where(mine, qr, jnp.zeros_like(qr))], axis=1)

            def add_dq(i, val):
                dqn_ref[rows_of(i), cols] += val[:, :MLA_NOPE]
                dqr_ref[rows_of(i), cols] += val[:, MLA_NOPE:]

            return dict(q=q_fn, do=lambda i: do_ref[rows_of(i), cols], k=jnp.concatenate([kn_ref[:, cols], kr], axis=1),
                        v=v_ref[:, cols], rows=lambda i: (_pick_row(lr_ref[i], hd), _pick_row(dr_ref[i], hd)),
                        bias=None, add_dq=add_dq, add_dfq=None)

        res = _attn_bwd_blocks(j, nk, tk, scale, [head(a) for a in range(MLA_GROUP)])
        dkn_ref[...] = jnp.concatenate([r[0][:, :MLA_NOPE] for r in res], axis=1).astype(BF16)
        dkr_ref[...] = jnp.concatenate([r[0][:, MLA_NOPE:] for r in res], axis=1).astype(BF16)
        dv_ref[...] = jnp.concatenate([r[1] for r in res], axis=1).astype(BF16)

    full = pl.BlockSpec((s, wide), lambda b, hg, j: (b, hg))
    blk = pl.BlockSpec((tk, wide), lambda b, hg, j: (b * nk + j, hg))
    rowsp = pl.BlockSpec((nk, 16, tk), lambda b, hg, j: (b, 0, 0))
    total = MLA_HEADS * MLA_V
    rope0 = MLA_HEADS * MLA_NOPE // rwide
    return _call(
        body, name, (nb, ngroups, nk),
        [full, pl.BlockSpec((s, rwide), lambda b, hg, j: (b, rope0 + hg)), blk,
         pl.BlockSpec((tk, 128), lambda b, hg, j: (b * nk + j, 0)), blk, full, rowsp, rowsp],
        [full, full, blk, blk, blk],
        [_sds((t, total), F32), _sds((t, total), F32), _sds((t, total), BF16), _sds((t, total), BF16),
         _sds((t, total), BF16)],
        (q, q, kn, kr2, v, do, lse_rows, delta_rows), hosted=hosted)


def mla_mid_bwd(dqn, dqr, dkn, dv, dkr_heads, h, g_q, g_kv, w_uq, w_uk, w_uv, cos8, sin8, cos64, sin64s, swap64,
                heads_to_rope, head_sum, name, hosted=None):
    t = h.shape[0]
    tm = TOKEN_TILE
    hq = MLA_HEADS * MLA_NOPE
    hr = MLA_HEADS * MLA_ROPE // 2
    nq = w_uq.shape[1]

    def body(dqn_ref, dqr_ref, dkn_ref, dv_ref, dkr_ref, h_ref, gq_ref, gkv_ref, wuq_ref, wuk_ref, wuv_ref,
             c8_ref, s8_ref, c64_ref, s64_ref, sw_ref, hp_ref, hs_ref, dh_ref, dqp_ref, dgq_ref, dgkv_ref):
        @pl.when(pl.program_id(0) == 0)
        def _():
            dgq_ref[...] = jnp.zeros_like(dgq_ref)
            dgkv_ref[...] = jnp.zeros_like(dgkv_ref)

        drot = _dot(dqr_ref[...].astype(BF16), hp_ref[...])
        o1 = drot[:, :hr]
        o2 = drot[:, hr:]
        cs = c8_ref[...]
        sn = s8_ref[...]
        dqp = jnp.concatenate([dqn_ref[...].astype(BF16), (o1 * cs + o2 * sn).astype(BF16),
                               (o2 * cs - o1 * sn).astype(BF16)], axis=1)
        dqp_ref[...] = dqp
        dcq = _dot_nt(dqp, wuq_ref[...])
        dckv = _dot_nt(dkn_ref[...], wuk_ref[...]) + _dot_nt(dv_ref[...], wuv_ref[...])
        hh = h_ref[...]

        def rms_bwd(hpart, g, dc, dg_ref):
            hhat, rstd = _rms(hpart, None)
            dg_ref[...] += jnp.sum(dc * hhat, axis=0, keepdims=True)
            dcg = dc * g
            return rstd * (dcg - hhat * jnp.mean(dcg * hhat, axis=-1, keepdims=True))

        dhq = rms_bwd(hh[:, :MLA_QR], gq_ref[...], dcq, dgq_ref)
        dhkv = rms_bwd(hh[:, MLA_QR:MLA_QR + MLA_KVR], gkv_ref[...], dckv, dgkv_ref)
        dkr = _dot(dkr_ref[...], hs_ref[...])
        dkr_pre = dkr * c64_ref[...] + _dot_f32(dkr * s64_ref[...], sw_ref[...])
        dh_ref[...] = jnp.concatenate([dhq, dhkv, dkr_pre], axis=1).astype(BF16)

    def rows(n):
        return pl.BlockSpec((tm, n), lambda i: (i, 0))

    def whole(a):
        return pl.BlockSpec(a.shape, lambda i: (0,) * a.ndim)

    return _call(
        body, name, (t // tm,),
        [rows(hq), rows(hq), rows(hq), rows(hq), rows(hq), rows(h.shape[1]), whole(g_q), whole(g_kv),
         whole(w_uq), whole(w_uk), whole(w_uv), rows(hr), rows(hr), rows(MLA_ROPE), rows(MLA_ROPE),
         whole(swap64), whole(heads_to_rope), whole(head_sum)],
        [rows(h.shape[1]), rows(nq), pl.BlockSpec((1, MLA_QR), lambda i: (0, 0)),
         pl.BlockSpec((1, MLA_KVR), lambda i: (0, 0))],
        [_sds((t, h.shape[1]), BF16), _sds((t, nq), BF16), _sds((1, MLA_QR), F32), _sds((1, MLA_KVR), F32)],
        (dqn, dqr, dkn, dv, dkr_heads, h, g_q, g_kv, w_uq, w_uk, w_uv, cos8, sin8, cos64, sin64s, swap64,
         heads_to_rope, head_sum), hosted=hosted)


def fox_gate_bwd(dcum, hf, b_f, triu, n_batch, name):
    t, n = hf.shape
    blk = triu.shape[0]
    nb = (t // n_batch) // blk

    def body(dc_ref, hf_ref, b_ref, tri_ref, o_ref, db_ref, carry_ref):
        @pl.when(pl.program_id(1) == 0)
        def _():
            carry_ref[...] = jnp.zeros_like(carry_ref)

        @pl.when((pl.program_id(0) == 0) & (pl.program_id(1) == 0))
        def _():
            db_ref[...] = jnp.zeros_like(db_ref)

        rc = _dot_f32(tri_ref[...], dc_ref[...]) + carry_ref[...]
        carry_ref[...] = rc[0:1, :]
        dhf = rc * jax.nn.sigmoid(-(hf_ref[...] + b_ref[...]))
        o_ref[...] = dhf.astype(BF16)
        db_ref[...] += jnp.sum(dhf, axis=0, keepdims=True)

    rev = pl.BlockSpec((blk, n), lambda bb, i: (bb * nb + nb - 1 - i, 0))
    return pl.pallas_call(
        body, name=name, grid=(n_batch, nb),
        in_specs=[rev, rev, pl.BlockSpec((1, n), lambda bb, i: (0, 0)), pl.BlockSpec((blk, blk), lambda bb, i: (0, 0))],
        out_specs=[rev, pl.BlockSpec((1, n), lambda bb, i: (0, 0))],
        out_shape=[_sds((t, n), BF16), _sds((1, n), F32)], scratch_shapes=[pltpu.VMEM((1, n), F32)],
        compiler_params=_cp(2),
    )(dcum, hf, b_f, triu)


def wgrad(a, bm, name, with_bf16=False, bt=WGRAD_TOKENS):
    ca, t, kd = a.shape
    cb, _, nd = bm.shape
    c = max(ca, cb)
    bn = nd
    if nd > 1024 and nd % 1024 == 0:
        bn = 1024
    nsteps = t // bt

    def body(a_ref, b_ref, o_ref, *rest):
        @pl.when(pl.program_id(2) == 0)
        def _():
            o_ref[...] = jnp.zeros_like(o_ref)

        o_ref[...] += _dot_tn(a_ref[...].astype(BF16), b_ref[...].astype(BF16))
        if with_bf16:
            @pl.when(pl.program_id(2) == nsteps - 1)
            def _():
                rest[0][...] = o_ref[...].astype(BF16)

    out_spec = pl.BlockSpec((None, kd, bn), lambda cc, n, tt: (cc, 0, n))
    res = pl.pallas_call(
        body, name=name, grid=(c, nd // bn, nsteps),
        in_specs=[pl.BlockSpec((None, bt, kd), lambda cc, n, tt: (cc if ca > 1 else 0, tt, 0)),
                  pl.BlockSpec((None, bt, bn), lambda cc, n, tt: (cc if cb > 1 else 0, tt, n))],
        out_specs=[out_spec, out_spec] if with_bf16 else out_spec,
        out_shape=[_sds((c, kd, nd), F32), _sds((c, kd, nd), BF16)] if with_bf16 else _sds((c, kd, nd), F32),
        compiler_params=_cp(3),
    )(a, bm)
    return res


def ada_mod_part(c_all, ada_w, name):
    nl, d, n = ada_w.shape
    rows = c_all.shape[0]
    tn = 512

    def body(c_ref, w_ref, o_ref):
        cv = c_ref[...]
        act = (cv * jax.nn.sigmoid(cv)).astype(BF16)
        o_ref[...] = _dot(act, w_ref[...].astype(BF16))

    return pl.pallas_call(
        body, name=name, grid=(nl, n // tn),
        in_specs=[pl.BlockSpec((rows, d), lambda l, j: (0, 0)), pl.BlockSpec((None, d, tn), lambda l, j: (l, 0, j))],
        out_specs=pl.BlockSpec((None, rows, tn), lambda l, j: (l, 0, j)),
        out_shape=_sds((nl, rows, n), F32), compiler_params=_cp(2),
    )(c_all, ada_w)


def ada_grad(c_all_t, dmod, name):
    nl, rows, n = dmod.shape
    d = c_all_t.shape[0]
    tn = 512

    def body(c_ref, dm_ref, o_ref):
        cv = c_ref[...]
        act = (cv * jax.nn.sigmoid(cv)).astype(BF16)
        o_ref[...] = _dot(act, dm_ref[...].astype(BF16))

    return pl.pallas_call(
        body, name=name, grid=(nl, n // tn),
        in_specs=[pl.BlockSpec((d, rows), lambda l, j: (0, 0)), pl.BlockSpec((None, rows, tn), lambda l, j: (l, 0, j))],
        out_specs=pl.BlockSpec((None, d, tn), lambda l, j: (l, 0, j)),
        out_shape=_sds((nl, d, n), F32), compiler_params=_cp(2),
    )(c_all_t, dmod)


def sum_leading(a, name):
    g, r, n = a.shape

    def body(a_ref, o_ref):
        acc = a_ref[0]
        for kk in range(1, g):
            acc = acc + a_ref[kk]
        o_ref[...] = acc

    return pl.pallas_call(
        body, name=name, grid=(1,), in_specs=[pl.BlockSpec((g, r, n), lambda i: (0, 0, 0))],
        out_specs=pl.BlockSpec((r, n), lambda i: (0, 0)), out_shape=_sds((r, n), F32), compiler_params=_cp(1),
    )(a)


def adamw(w, g, m, v, name):
    r, n = w.shape
    fits = [cand for cand in range(8, r, 8) if r % cand == 0 and cand * n * 4 <= ADAMW_BLOCK_BYTES]
    br = max(fits) if fits else r
    c1 = 1.0 - ADAM_B1 ** ADAM_STEP
    c2 = 1.0 - ADAM_B2 ** ADAM_STEP

    def body(w_ref, g_ref, m_ref, v_ref, d_ref, mo_ref, vo_ref):
        gv = g_ref[...]
        mn = ADAM_B1 * m_ref[...] + (1.0 - ADAM_B1) * gv
        vn = ADAM_B2 * v_ref[...] + (1.0 - ADAM_B2) * (gv * gv)
        mo_ref[...] = mn
        vo_ref[...] = vn
        d_ref[...] = -ADAM_LR * ((mn / c1) / (jnp.sqrt(vn / c2) + ADAM_EPS) + ADAM_WD * w_ref[...])

    spec = pl.BlockSpec((br, n), lambda i: (i, 0))
    return _call(body, name, (r // br,), [spec] * 4, [spec] * 3, [_sds((r, n), F32)] * 3, (w, g, m, v))


def all_gather8(x_blk, name, hosted=None):
    m_per, n = x_blk.shape
    h_in = 0 if hosted is None else len(hosted.inputs)
    h_out = 0 if hosted is None else len(hosted.out_shape)

    def body(x_ref, *refs):
        c_in, (out_ref, *c_out), (send_sems, recv_sems, local_sem, *c_sem) = (
            refs[:h_in], refs[h_in:h_in + 1 + h_out], refs[h_in + 1 + h_out:])
        if hosted is not None:
            hosted.start(c_in, c_out, c_sem)
        gather(x_ref, out_ref, send_sems, recv_sems, local_sem)
        if hosted is not None:
            hosted.finish(c_in, c_out, c_sem)

    def gather(x_ref, out_ref, send_sems, recv_sems, local_sem):
        x, y, c = _place()
        me, sibling = (x, y, c), (x, y, 1 - c)
        chips = [(1 - x, y), (x, 1 - y), (1 - x, 1 - y)]

        def rows(px, py, pc):
            return out_ref.at[pl.ds((4 * px + 2 * py + pc) * m_per, m_per), :]

        def copy(k, block, to, src=None):
            return pltpu.make_async_remote_copy(
                src_ref=rows(*block) if src is None else src, dst_ref=rows(*block),
                send_sem=send_sems.at[k], recv_sem=recv_sems.at[k], device_id=to, device_id_type=MESH)

        mine = pltpu.make_async_copy(x_ref, rows(*me), local_sem)
        mine.start()
        first = [copy(0, me, sibling, src=x_ref)]
        first += [copy(1 + j, me, (*chip, c), src=x_ref) for j, chip in enumerate(chips)]
        for cp in first:
            cp.start()
        passed = [copy(4 + j, (*chip, c), sibling) for j, chip in enumerate(chips)]
        for j, chip in enumerate(chips):
            copy(1 + j, (*chip, c), me).wait_recv()
            passed[j].start()
        copy(0, sibling, me).wait_recv()
        for j, chip in enumerate(chips):
            copy(4 + j, (*chip, 1 - c), me).wait_recv()
        for cp in first + passed:
            cp.wait_send()
        mine.wait()

    hbm = pl.BlockSpec(memory_space=pl.ANY)
    vmem = pl.BlockSpec(memory_space=pltpu.VMEM)
    res = pl.pallas_call(
        body, name=name,
        out_shape=[_sds((8 * m_per, n), x_blk.dtype)] + ([] if hosted is None else list(hosted.out_shape)),
        in_specs=[vmem] + [hbm] * h_in, out_specs=[vmem] + [hbm] * h_out,
        scratch_shapes=[pltpu.SemaphoreType.DMA((7,)), pltpu.SemaphoreType.DMA((7,)), pltpu.SemaphoreType.DMA]
        + ([] if hosted is None else list(hosted.sems)),
        compiler_params=pltpu.CompilerParams(vmem_limit_bytes=VMEM_LIMIT),
    )(x_blk, *([] if hosted is None else hosted.inputs))
    return res[0] if hosted is None else (res[0], res[1:])


def _gather_comm(shards):
    nt = len(shards)

    def parts(w_refs, out_refs, sems, finishing):
        send_sems, recv_sems, own_send, own_recv = sems
        x, y, c = _place()
        sibling = (x, y, 1 - c)
        chips = [(1 - x, y), (x, 1 - y), (1 - x, 1 - y)]

        def copy(t, k, block, to, src=None):
            px, py, hh = block
            dst = out_refs[t].at[2 * px + py, hh]
            return pltpu.make_async_remote_copy(
                src_ref=dst if src is None else src, dst_ref=dst,
                send_sem=send_sems.at[6 * t + k], recv_sem=recv_sems.at[6 * t + k], device_id=to, device_id_type=MESH)

        own = [pltpu.make_async_remote_copy(
            src_ref=w_refs[t], dst_ref=out_refs[t].at[2 * x + y], send_sem=own_send.at[t], recv_sem=own_recv.at[t],
            device_id=sibling, device_id_type=MESH) for t in range(nt)]
        first = [copy(t, j, (x, y, c), (*chip, c), src=w_refs[t].at[c]) for t in range(nt) for j, chip in enumerate(chips)]
        if not finishing:
            return own, first
        landed = [copy(t, j, (*chip, c), (x, y, c)) for t in range(nt) for j, chip in enumerate(chips)]
        passed = [copy(t, 3 + j, (*chip, c), sibling) for t in range(nt) for j, chip in enumerate(chips)]
        from_sibling = [copy(t, 3 + j, (*chip, 1 - c), (x, y, c)) for t in range(nt) for j, chip in enumerate(chips)]
        return own, first, landed, passed, from_sibling

    def start(w_refs, out_refs, sems):
        own, first = parts(w_refs, out_refs, sems, False)
        for cp in own + first:
            cp.start()

    def finish(w_refs, out_refs, sems):
        own, first, landed, passed, from_sibling = parts(w_refs, out_refs, sems, True)
        for arrived, fwd in zip(landed, passed):
            arrived.wait_recv()
            fwd.start()
        for cp in from_sibling:
            cp.wait_recv()
        for cp in first + passed:
            cp.wait_send()
        for cp in own:
            cp.wait()

    sems = [pltpu.SemaphoreType.DMA((6 * nt,)), pltpu.SemaphoreType.DMA((6 * nt,)),
            pltpu.SemaphoreType.DMA((nt,)), pltpu.SemaphoreType.DMA((nt,))]
    return _Hosted(list(shards), [_sds((N_CHIPS, *w.shape), w.dtype) for w in shards], sems, start, finish)


def _row_block(r, n, itemsize):
    best = None
    for br in range(16, r + 1, 16):
        if r % br == 0 and br * n * itemsize <= COMM_BLOCK_BYTES:
            best = br
    return r if best is None else best


def _scatter_comm(parts):
    nt = len(parts)

    def copies(p_refs, b_refs, sems, arriving):
        send_sems, recv_sems = sems
        x, y, c = _place()
        me = 4 * x + 2 * y + c
        cps = []
        for t in range(nt):
            for r in range(1, 8):
                tx = 1 - x if r & 4 else x
                ty = 1 - y if r & 2 else y
                tc = 1 - c if r & 1 else c
                src, dst = (2 * x + y, c), 4 * tx + 2 * ty + tc
                if not arriving:
                    src, dst = (2 * tx + ty, tc), me
                cps.append(pltpu.make_async_remote_copy(
                    src_ref=p_refs[t].at[src], dst_ref=b_refs[t].at[dst], send_sem=send_sems.at[7 * t + r - 1],
                    recv_sem=recv_sems.at[7 * t + r - 1], device_id=(tx, ty, tc), device_id_type=MESH))
        return cps

    def start(p_refs, b_refs, sems):
        for cp in copies(p_refs, b_refs, sems, False):
            cp.start()

    def finish(p_refs, b_refs, sems):
        for cp in copies(p_refs, b_refs, sems, True):
            cp.wait_recv()
        for cp in copies(p_refs, b_refs, sems, False):
            cp.wait_send()

    sems = [pltpu.SemaphoreType.DMA((7 * nt,)), pltpu.SemaphoreType.DMA((7 * nt,))]
    return _Hosted(list(parts), [_sds((2 * N_CHIPS, *p.shape[2:]), p.dtype) for p in parts], sems, start, finish)


def sum_devices(own, recv, place, name, slot=(0, 1, None)):
    _, _, r, n = own.shape
    layer, n_layers, buf = slot
    br = _row_block(r, n, 4 * 8)

    def body(p_ref, o_ref, *rest):
        acc = o_ref[...]
        for kk in range(7):
            acc = acc + rest[kk][...].astype(F32)
        rest[-1][...] = acc

    def arrived(rel):
        return pl.BlockSpec((None, br, n), lambda i, pref: (jnp.bitwise_xor(pref[0], rel), i, 0))

    in_specs = [pl.BlockSpec((None, None, br, n), lambda i, pref: (pref[2], pref[1], i, 0))]
    in_specs += [arrived(rel) for rel in range(1, 8)]
    args = [own] + [recv] * 7
    aliases = {}
    if buf is not None:
        in_specs.append(pl.BlockSpec(memory_space=pl.ANY))
        args.append(buf)
        aliases = {9: 0}
    return pl.pallas_call(
        body, name=name,
        grid_spec=pltpu.PrefetchScalarGridSpec(
            num_scalar_prefetch=1, grid=(r // br,), in_specs=in_specs,
            out_specs=pl.BlockSpec((None, None, br, n), lambda i, pref: (layer, pref[1], i, 0))),
        out_shape=_sds((n_layers, 2, r, n), F32), input_output_aliases=aliases, compiler_params=_cp(1),
    )(place, *args)


def _join_comm(bufs):
    nt = len(bufs)
    layers = [bf.shape[0] for bf in bufs]
    first = [sum(layers[:t]) for t in range(nt)]

    def copies(o_refs, sems, own):
        send_sems, recv_sems = sems
        x, y, c = _place()
        hh = c if own else 1 - c
        return [pltpu.make_async_remote_copy(
            src_ref=o_refs[t].at[l, hh], dst_ref=o_refs[t].at[l, hh], send_sem=send_sems.at[first[t] + l],
            recv_sem=recv_sems.at[first[t] + l], device_id=(x, y, 1 - c), device_id_type=MESH)
            for t in range(nt) for l in range(layers[t])]

    def start(_, o_refs, sems):
        for cp in copies(o_refs, sems, True):
            cp.start()

    def finish(_, o_refs, sems):
        for cp in copies(o_refs, sems, False):
            cp.wait_recv()
        for cp in copies(o_refs, sems, True):
            cp.wait_send()

    sems = [pltpu.SemaphoreType.DMA((sum(layers),)), pltpu.SemaphoreType.DMA((sum(layers),))]
    return _Hosted(list(bufs), [_sds(bf.shape, bf.dtype) for bf in bufs], sems, start, finish, in_place=True)


def sibling_join_halves(bufs, name):
    comm = _join_comm(bufs)
    nt = len(bufs)

    def body(*refs):
        comm.start(refs[:nt], refs[nt:2 * nt], refs[2 * nt:])
        comm.finish(refs[:nt], refs[nt:2 * nt], refs[2 * nt:])

    hbm = pl.BlockSpec(memory_space=pl.ANY)
    return pl.pallas_call(body, name=name, out_shape=comm.out_shape, in_specs=[hbm] * nt, out_specs=[hbm] * nt,
                          input_output_aliases={k: k for k in range(nt)}, scratch_shapes=comm.sems)(*bufs)


_SHARD_KIND = {"mla_w_in": "rows", "mla_w_uq": "cols", "mla_w_uk": "cols", "mla_w_uv": "cols", "mla_w_o": "rows",
               "fox_w_in": "cols", "fox_w_o": "rows", "ffn_w_gate": "chunk", "ffn_w_up": "chunk", "ffn_w_down": "chunk"}
_PACKED = tuple(_SHARD_KIND)
_TRANSPOSED = ("ffn_w_gate", "ffn_w_up", "fox_w_in")


def _halves(shard):
    if shard.ndim == 3 and shard.shape[0] == 2:
        return shard
    r, n = shard.shape[-2:]
    return shard.reshape(2, r // 2, n)


def _cols_to_full(g):
    return jnp.transpose(g, (1, 0, 2)).reshape(g.shape[1], -1)


def _full_to_cols(w):
    k, n4 = w.shape
    return jnp.transpose(w.reshape(k, N_CHIPS, n4 // N_CHIPS), (1, 0, 2))


def _uq_perm():
    per = MLA_NOPE + MLA_ROPE
    half = MLA_ROPE // 2
    nope = [h * per + d for h in range(MLA_HEADS) for d in range(MLA_NOPE)]
    r1 = [h * per + MLA_NOPE + r for h in range(MLA_HEADS) for r in range(half)]
    r2 = [h * per + MLA_NOPE + half + r for h in range(MLA_HEADS) for r in range(half)]
    perm = np.array(nope + r1 + r2, dtype=np.int32)
    return perm, np.argsort(perm).astype(np.int32)


def _rope_matrices():
    half = MLA_ROPE // 2
    nr = MLA_HEADS * MLA_ROPE
    to_heads = np.zeros((nr, nr), np.float32)
    from_heads = np.zeros((MLA_HEADS * 128, nr), np.float32)
    for e in range(2):
        for h in range(MLA_HEADS):
            for r in range(half):
                to_heads[e * MLA_HEADS * half + h * half + r, h * MLA_ROPE + e * half + r] = 1.0
                from_heads[h * 128 + e * half + r, e * MLA_HEADS * half + h * half + r] = 1.0
    head_sum = np.tile(np.eye(MLA_ROPE, dtype=np.float32), (2 * MLA_HEADS, 1))
    dup = np.concatenate([np.eye(MLA_ROPE, dtype=np.float32)] * 2, axis=1)
    return to_heads, from_heads, head_sum, dup


def _ffn_weights(gathered):
    return tuple(g.reshape(N_CHIPS, 2 * g.shape[2], g.shape[3]) for g in gathered)


def _fox_weights(gathered):
    w_in, w_o = gathered
    w_in = jnp.transpose(w_in, (0, 2, 1, 3)).reshape(N_CHIPS * w_in.shape[2], 2 * w_in.shape[3])
    return w_in, w_o.reshape(-1, w_o.shape[-1])


def _local_step(x, positions, target, mods, wts, ln_g, ln_b, mla_g_q, mla_g_kv, fox_b_f, shards=None):
    nb, s, d = x.shape
    t = nb * s
    x0 = x.reshape(t, d)
    tgt = target.reshape(t, d)
    perm, inv_perm = _uq_perm()

    half = MLA_ROPE // 2
    inv_freq = ROPE_THETA ** (-jnp.arange(half, dtype=F32) / half)
    ang = positions.astype(F32).reshape(t, 1) * inv_freq
    cos, sin = jnp.cos(ang), jnp.sin(ang)
    cos8, sin8 = jnp.tile(cos, (1, MLA_HEADS)), jnp.tile(sin, (1, MLA_HEADS))
    cos64 = jnp.concatenate([cos, cos], axis=1)
    sin64s = jnp.concatenate([-sin, sin], axis=1)
    swap64 = jnp.asarray(np.roll(np.eye(MLA_ROPE, dtype=np.float32), half, axis=1))
    to_heads, from_heads, head_sum, dup = _rope_matrices()
    to_heads, from_heads = jnp.asarray(to_heads, dtype=BF16), jnp.asarray(from_heads, dtype=BF16)
    head_sum, dup = jnp.asarray(head_sum, dtype=BF16), jnp.asarray(dup, dtype=BF16)
    sel_mla = jnp.asarray(np.pad(np.kron(np.eye(MLA_HEADS, dtype=np.float32), np.ones((MLA_V, 1), np.float32)),
                                 ((0, 0), (0, 128 - MLA_HEADS))))
    sel_fox = jnp.asarray(np.pad(np.kron(np.eye(FOX_HEADS, dtype=np.float32), np.ones((FOX_HD, 1), np.float32)),
                                 ((0, 0), (0, 128 - FOX_HEADS))))
    tri = jnp.asarray(np.tril(np.ones((GATE_BLOCK, GATE_BLOCK), np.float32)))
    triu = jnp.asarray(np.triu(np.ones((GATE_BLOCK, GATE_BLOCK), np.float32)))
    onehot16 = jnp.asarray(np.eye(16, 128, dtype=np.float32))

    def vec(a):
        return a.reshape(1, -1)

    def carried(key):
        return None if shards is None else _gather_comm(shards[key])

    def split(res):
        return (res, None) if shards is None else res

    w_uq_p = wts["mla_w_uq"][:, perm]
    b_f_pad = jnp.pad(fox_b_f.reshape(1, -1), ((0, 0), (0, 128 - FOX_HEADS)))

    sh_a, sc_a, gt_a, sh_f, sc_f, gt_f = mods[0]
    h_in, u_m = mod_linear(x0, sh_a, sc_a, wts["mla_w_in"], F32, "mla_in", emit_u=True)
    q_m, kn_m, v_m, kr2_m, cq_m, ckv_m = mla_mid_fwd(
        h_in, vec(mla_g_q), vec(mla_g_kv), w_uq_p, wts["mla_w_uk"], wts["mla_w_uv"], cos8, sin8, cos64, sin64s, swap64,
        to_heads, dup, "mla_mid")
    (o_m, lse_m), got = split(mla_attn_fwd(q_m, kn_m, kr2_m, v_m, nb, "mla_attn", hosted=carried("ffn0")))
    ffn0_w = wts["ffn"][0] if got is None else _ffn_weights(got)
    y0, x1 = linear_resid_ln(o_m, wts["mla_w_o"], x0, gt_a, vec(ln_g[0, 0]), vec(ln_b[0, 0]), "mla_out")
    (u_f0, hg0, hu0, y1, x2), got = split(ffn_fwd(x1, sh_f, sc_f, gt_f, *ffn0_w, vec(ln_g[0, 1]), vec(ln_b[0, 1]), "ffn0",
                                                  hosted=carried("fox")))
    fox_w_in_t, fox_w_o = (wts["fox_w_in"].T, wts["fox_w_o"]) if got is None else _fox_weights(got)
    fox_w_f_t = jnp.pad(fox_w_in_t[3 * d:], ((0, 128 - FOX_HEADS), (0, 0)))
    sh_a1, sc_a1, gt_a1, sh_f1, sc_f1, gt_f1 = mods[1]
    qkv, u_x = mod_linear(x2, sh_a1, sc_a1, fox_w_in_t, BF16, "fox_qkv", tn=1024, emit_u=True, w_rows=3 * d)
    hf = mod_linear(x2, sh_a1, sc_a1, fox_w_f_t, F32, "fox_f", w_rows=128)
    cum = fox_gate_fwd(hf, b_f_pad, tri, nb, "fox_gate")
    cum_rows = rows16(cum, "fox_cum_rows")
    (o_x, lse_x), got = split(fox_attn_fwd(qkv, cum, cum_rows, nb, "fox_attn", hosted=carried("ffn1")))
    ffn1_w = wts["ffn"][1] if got is None else _ffn_weights(got)
    y2, x3 = linear_resid_ln(o_x, fox_w_o, x2, gt_a1, vec(ln_g[1, 0]), vec(ln_b[1, 0]), "fox_out")
    u_f1, hg1, hu1, y3, x4 = ffn_fwd(x3, sh_f1, sc_f1, gt_f1, *ffn1_w, vec(ln_g[1, 1]), vec(ln_b[1, 1]), "ffn1")

    parts, recv = {}, {}

    def halves_of(g):
        return g.reshape(N_CHIPS, 2, g.shape[1] // 2, g.shape[2])

    def scatter(keys, sent):
        return None if shards is None else _scatter_comm([sent[k] for k in keys])

    def landed(keys, got):
        if got is not None:
            recv.update(zip(keys, got))

    def ffn_grads(layer, u, dhg, dhu, act, dy):
        sent = {}
        for n, (a_op, b_op) in (("ffn_w_gate", (dhg, u[None])), ("ffn_w_up", (dhu, u[None])), ("ffn_w_down", (act, dy[None]))):
            g32, g16 = wgrad(a_op, b_op, "ffn%d_d%s" % (layer, n[4:]), with_bf16=True)
            parts["%s/%d" % (n, layer)], sent["%s/%d" % (n, layer)] = halves_of(g32), halves_of(g16)
        return sent

    dz3, dy3, dg11, db11, dgt_f1, sq_err = ln_bwd(x4, x3, y3, gt_f1, vec(ln_g[1, 1]), "ffn1_ln_bwd", target=tgt)
    loss_part = 0.5 * jnp.sum(sq_err) / d
    dhg1, dhu1, act1, dx3, dsc_f1, dsh_f1 = ffn_bwd(dy3, hg1, hu1, *ffn1_w, dz3, x3, sc_f1, "ffn1_bwd")
    sent = ffn_grads(1, u_f1, dhg1, dhu1, act1, dy3)
    dz2, dy2, dg10, db10, dgt_a1 = ln_bwd(dx3, x2, y2, gt_a1, vec(ln_g[1, 0]), "fox_ln_bwd")
    do_x, delta_x = linear_nt_delta(dy2, fox_w_o, o_x, sel_fox, "fox_out_bwd")
    (dq_x, dk_x, dv_x, dfq_x, dfk_x), got = split(fox_attn_bwd(
        qkv, do_x, cum, cum_rows, rows16(lse_x, "fox_lse_rows"), rows16(delta_x, "fox_delta_rows"), nb, "fox_attn_bwd",
        hosted=scatter(list(sent), sent)))
    landed(list(sent), got)
    dcum = tokens128(dfq_x + dfk_x, onehot16, "fox_dcum")
    dhf, dbf = fox_gate_bwd(dcum, hf, b_f_pad, triu, nb, "fox_gate_bwd")
    fox_d = [("q", dq_x), ("k", dk_x), ("v", dv_x)]
    dx2, dsc_a1, dsh_a1 = linear_nt_mod_bwd(
        [(dh, fox_w_in_t, i) for i, (_, dh) in enumerate(fox_d)] + [(dhf, fox_w_f_t, 0)], dz2, x2, sc_a1, "fox_in_bwd")
    dw_in_t = [wgrad(dh[None], u_x[None], "fox_dw" + tag)[0] for tag, dh in fox_d]
    dw_in_t.append(wgrad(dhf[None], u_x[None], "fox_dwf")[0][:FOX_HEADS])
    dw_in_t = jnp.concatenate(dw_in_t, axis=0).reshape(N_CHIPS, -1, 2, d // 2)
    parts["fox_w_in"] = jnp.transpose(dw_in_t, (0, 2, 1, 3))
    parts["fox_w_o"] = wgrad(o_x[None], dy2[None], "fox_dwo")[0].reshape(N_CHIPS, 2, -1, d)
    sent = {k: parts[k].astype(BF16) for k in ("fox_w_in", "fox_w_o")}
    dz1, dy1, dg01, db01, dgt_f0 = ln_bwd(dx2, x1, y1, gt_f, vec(ln_g[0, 1]), "ffn0_ln_bwd")
    (dhg0, dhu0, act0, dx1, dsc_f0, dsh_f0), got = split(ffn_bwd(dy1, hg0, hu0, *ffn0_w, dz1, x1, sc_f, "ffn0_bwd",
                                                                 hosted=scatter(list(sent), sent)))
    landed(list(sent), got)
    sent = ffn_grads(0, u_f0, dhg0, dhu0, act0, dy1)
    dz0, dy0, dg00, db00, dgt_a0 = ln_bwd(dx1, x0, y0, gt_a, vec(ln_g[0, 0]), "mla_ln_bwd")
    do_m, delta_m = linear_nt_delta(dy0, wts["mla_w_o"], o_m, sel_mla, "mla_out_bwd")
    parts["mla_w_o"] = wgrad(o_m[None], dy0[None], "mla_dwo")[0].reshape(N_CHIPS, 2, -1, d)
    (dqn_m, dqr_m, dkn_m, dkr_m, dv_m), got = split(mla_attn_bwd(
        q_m, kn_m, kr2_m, v_m, do_m, rows16(lse_m, "mla_lse_rows"), rows16(delta_m, "mla_delta_rows"), nb,
        "mla_attn_bwd", hosted=scatter(list(sent), sent)))
    landed(list(sent), got)
    sent = {"mla_w_o": parts["mla_w_o"].astype(BF16)}
    (dh_in, dq_pre, dgq, dgkv), got = split(mla_mid_bwd(
        dqn_m, dqr_m, dkn_m, dv_m, dkr_m, h_in, vec(mla_g_q), vec(mla_g_kv), w_uq_p, wts["mla_w_uk"],
        wts["mla_w_uv"], cos8, sin8, cos64, sin64s, swap64, from_heads, head_sum, "mla_mid_bwd",
        hosted=scatter(list(sent), sent)))
    landed(list(sent), got)
    parts["mla_w_uq"] = halves_of(_full_to_cols(wgrad(cq_m[None], dq_pre[None], "mla_dwuq")[0][:, inv_perm]))
    parts["mla_w_uk"] = halves_of(_full_to_cols(wgrad(ckv_m[None], dkn_m[None], "mla_dwuk")[0]))
    parts["mla_w_uv"] = halves_of(_full_to_cols(wgrad(ckv_m[None], dv_m[None], "mla_dwuv")[0]))
    parts["mla_w_in"] = wgrad(u_m[None], dh_in[None], "mla_dwin")[0].reshape(N_CHIPS, 2, -1, h_in.shape[1])
    sent = {k: parts[k].astype(BF16) for k in ("mla_w_in", "mla_w_uq", "mla_w_uk", "mla_w_uv")}
    (dx0, dsc_a0, dsh_a0), got = split(linear_nt_mod_bwd([(dh_in, wts["mla_w_in"], None)], dz0, x0, sc_a, "mla_in_bwd",
                                                         hosted=scatter(list(sent), sent)))
    landed(list(sent), got)

    dmods = [(dsh_a0, dsc_a0, dgt_a0, dsh_f0, dsc_f0, dgt_f0), (dsh_a1, dsc_a1, dgt_a1, dsh_f1, dsc_f1, dgt_f1)]
    d_ln_g = jnp.stack([jnp.concatenate([dg00, dg01], axis=0), jnp.concatenate([dg10, dg11], axis=0)])
    d_ln_b = jnp.stack([jnp.concatenate([db00, db01], axis=0), jnp.concatenate([db10, db11], axis=0)])
    return loss_part, dx0.reshape(nb, s, d), (parts, recv), dmods, d_ln_g, d_ln_b, dgq, dgkv, dbf[:, :FOX_HEADS]


def _pad_rows(a, rows):
    return jnp.pad(a, ((0, rows - a.shape[0]), (0, 0)))


def kernel(x, c, positions, mla_w_in, mla_g_q, mla_w_uq, mla_g_kv, mla_w_uk, mla_w_uv, mla_w_o, fox_w_in, fox_b_f, fox_w_o, ada_w, ada_b, ffn_w_gate, ffn_w_up, ffn_w_down, ln_g, ln_b, loss_target, m_mla_w_in, m_mla_g_q, m_mla_w_uq, m_mla_g_kv, m_mla_w_uk, m_mla_w_uv, m_mla_w_o, m_fox_w_in, m_fox_b_f, m_fox_w_o, m_ada_w, m_ada_b, m_ffn_w_gate, m_ffn_w_up, m_ffn_w_down, m_ln_g, m_ln_b, v_mla_w_in, v_mla_g_q, v_mla_w_uq, v_mla_g_kv, v_mla_w_uk, v_mla_w_uv, v_mla_w_o, v_fox_w_in, v_fox_b_f, v_fox_w_o, v_ada_w, v_ada_b, v_ffn_w_gate, v_ffn_w_up, v_ffn_w_down, v_ln_g, v_ln_b):
    args = dict(locals())
    nb, s, d = x.shape
    ax, ay, ac = lax.axis_index("x"), lax.axis_index("y"), lax.axis_index("c")
    chip = 2 * ax + ay
    dev = 2 * chip + ac
    n_dev = 2 * N_CHIPS
    n_all = nb * n_dev

    shard_shapes = {n: (args[n].shape if _SHARD_KIND[n] == "chunk" else args[n].shape[1:]) for n in _PACKED}

    def block(n, layer=None):
        w = args[n].reshape(shard_shapes[n]) if layer is None else args[n][layer]
        return _halves(w.astype(BF16))

    mla_names = [n for n in _PACKED if n.startswith("mla")]
    ffn_names = ("ffn_w_gate", "ffn_w_up", "ffn_w_down")
    fox_in_t = jnp.swapaxes(fox_w_in, 1, 2)[0].astype(BF16)
    fox_in_t = jnp.stack([fox_in_t[:, :d // 2], fox_in_t[:, d // 2:]])
    shards = {"ffn0": [block(n, 0) for n in ffn_names], "fox": [fox_in_t, block("fox_w_o")],
              "ffn1": [block(n, 1) for n in ffn_names]}

    ln_cols = ln_g.shape[-1]
    ln_blk = jnp.concatenate([ln_g.reshape(2 * DEPTH, ln_cols), ln_b.reshape(2 * DEPTH, ln_cols)], axis=0)
    early = jnp.concatenate([_pad_rows(c, 8), jnp.pad(_pad_rows(ln_blk, 8), ((0, 0), (0, d - ln_cols)))], axis=0)
    early, mla_all = all_gather8(early, "gather_c_ln_mla", hosted=_gather_comm([block(n) for n in mla_names]))
    wts = {}
    for n, g in zip(mla_names, mla_all):
        g = g.reshape(N_CHIPS, *shard_shapes[n])
        wts[n] = g.reshape(-1, g.shape[-1]) if _SHARD_KIND[n] == "rows" else _cols_to_full(g)
    early = early.reshape(n_dev, 16, d)
    c_all = early[:, :nb].reshape(n_all, d)
    ln_all = early.reshape(N_CHIPS, 2, 16, d)[:, 0, 8:8 + 4 * DEPTH, :ln_cols]
    ln_all = jnp.transpose(ln_all, (1, 0, 2)).reshape(4 * DEPTH, d)
    ln_g_full = ln_all[:2 * DEPTH].reshape(DEPTH, 2, d)
    ln_b_full = ln_all[2 * DEPTH:].reshape(DEPTH, 2, d)
    mod_part = ada_mod_part(c_all, ada_w, "ada_mod")
    ncol = mod_part.shape[-1]
    mod_g = all_gather8(mod_part.reshape(DEPTH * n_all, ncol), "gather_mod")
    mod_g = mod_g.reshape(N_CHIPS, 2, DEPTH, n_all, ncol)[:, 0]
    mod_full = jnp.transpose(mod_g, (1, 2, 0, 3)).reshape(DEPTH, n_all, N_CHIPS * ncol) + ada_b[:, None, :]
    mod_loc = lax.dynamic_slice_in_dim(mod_full, dev * nb, nb, axis=1)
    mods = [tuple(mod_loc[i, :, k * d:(k + 1) * d].reshape(nb, 1, d) for k in range(6)) for i in range(DEPTH)]

    loss_part, grad_x, (parts, recv), dmods, d_ln_g, d_ln_b, dgq, dgkv, dbf = _local_step(
        x, positions, loss_target, mods, wts, ln_g_full, ln_b_full, mla_g_q[0], mla_g_kv[0], fox_b_f[0], shards)
    loss = lax.psum(loss_part, ("x", "y", "c"))

    dmod_rows = jnp.stack([jnp.concatenate([v_.reshape(nb, d) for v_ in dm], axis=1) for dm in dmods])
    small = jnp.concatenate([
        d_ln_g.reshape(2 * DEPTH, d), d_ln_b.reshape(2 * DEPTH, d),
        jnp.pad(jnp.concatenate([dgq, dgkv, dbf], axis=1), ((0, 0), (0, d - 2 * MLA_QR - FOX_HEADS))),
        dmod_rows.reshape(DEPTH * nb * 6, d)], axis=0)
    n_small = small.shape[0]
    small_rows = -(-n_small // 8) * 8
    small_all = all_gather8(_pad_rows(small, small_rows), "gather_stats").reshape(n_dev, small_rows, d)
    stat_sum = sum_leading(small_all, "sum_stats")
    g_ln_g = lax.dynamic_slice_in_dim(stat_sum[:2 * DEPTH], chip * ln_cols, ln_cols, axis=1).reshape(DEPTH, 2, ln_cols)
    g_ln_b = lax.dynamic_slice_in_dim(stat_sum[2 * DEPTH:4 * DEPTH], chip * ln_cols, ln_cols, axis=1).reshape(DEPTH, 2, ln_cols)
    row = stat_sum[4 * DEPTH]
    g_gq = row[:MLA_QR].reshape(1, MLA_QR)
    g_gkv = row[MLA_QR:2 * MLA_QR].reshape(1, MLA_KVR)
    g_bf = row[2 * MLA_QR:2 * MLA_QR + FOX_HEADS].reshape(1, FOX_HEADS)
    base = 4 * DEPTH + 1
    dmod_all = small_all[:, base:base + DEPTH * nb * 6].reshape(n_dev, DEPTH, nb, 6 * d)
    dmod_all = jnp.transpose(dmod_all, (1, 0, 2, 3)).reshape(DEPTH, n_all, 6 * d)
    g_ada_b = sum_leading(jnp.transpose(dmod_all, (1, 0, 2)), "sum_ada_b")
    dmod_mine = lax.dynamic_slice_in_dim(dmod_all, chip * ncol, ncol, axis=2)
    g_ada_w = ada_grad(c_all.T, dmod_mine, "ada_grad")

    place = jnp.stack([dev, ac, chip]).astype(jnp.int32)
    bufs = []
    for n in _PACKED:
        if _SHARD_KIND[n] == "chunk":
            buf = None
            for layer in range(DEPTH):
                key = "%s/%d" % (n, layer)
                buf = sum_devices(parts[key], recv[key], place, "rs_sum_%s%d" % (n, layer), slot=(layer, DEPTH, buf))
        else:
            buf = sum_devices(parts[n], recv[n], place, "rs_sum_" + n)
        bufs.append(buf)
    joined = sibling_join_halves(bufs, "rs_join")
    g_big = {n: j.reshape(j.shape[0], 2 * j.shape[2], j.shape[3]) for n, j in zip(_PACKED, joined)}
    j = joined[_PACKED.index("fox_w_in")]
    g_big["fox_w_in"] = jnp.transpose(j, (0, 2, 1, 3)).reshape(1, j.shape[2], 2 * j.shape[3])

    g_out = {
        "mla_w_in": g_big["mla_w_in"], "mla_g_q": g_gq, "mla_w_uq": g_big["mla_w_uq"], "mla_g_kv": g_gkv,
        "mla_w_uk": g_big["mla_w_uk"], "mla_w_uv": g_big["mla_w_uv"], "mla_w_o": g_big["mla_w_o"],
        "fox_w_in": g_big["fox_w_in"], "fox_b_f": g_bf, "fox_w_o": g_big["fox_w_o"],
        "ada_w": g_ada_w, "ada_b": g_ada_b, "ffn_w_gate": g_big["ffn_w_gate"], "ffn_w_up": g_big["ffn_w_up"],
        "ffn_w_down": g_big["ffn_w_down"], "ln_g": g_ln_g, "ln_b": g_ln_b}
    names = ["mla_w_in", "mla_g_q", "mla_w_uq", "mla_g_kv", "mla_w_uk", "mla_w_uv", "mla_w_o", "fox_w_in", "fox_b_f",
             "fox_w_o", "ada_w", "ada_b", "ffn_w_gate", "ffn_w_up", "ffn_w_down", "ln_g", "ln_b"]
    small_names = ["mla_g_q", "mla_g_kv", "fox_b_f", "ada_b", "ln_g", "ln_b"]
    deltas, new_m, new_v = {}, {}, {}
    for n in names:
        if n in small_names:
            continue
        shp = args[n].shape
        if n in _TRANSPOSED:
            view = lambda a: jnp.swapaxes(a, 1, 2).reshape(-1, shp[1])
            back = lambda a: jnp.swapaxes(a.reshape(shp[0], shp[2], shp[1]), 1, 2)
        else:
            view = lambda a: a.reshape(-1, shp[-1])
            back = lambda a: a.reshape(shp)
        dl, mn, vn = adamw(view(args[n]), g_out[n].reshape(view(args[n]).shape), view(args["m_" + n]),
                           view(args["v_" + n]), "adamw_" + n)
        g_out[n], deltas[n], new_m[n], new_v[n] = back(g_out[n].reshape(view(args[n]).shape)), back(dl), back(mn), back(vn)

    def small_pack(prefix, src):
        flat = jnp.concatenate([src[prefix + n].reshape(-1) for n in small_names])
        size = -(-flat.shape[0] // (8 * 128)) * 8 * 128
        return jnp.pad(flat, (0, size - flat.shape[0])).reshape(-1, 128)

    sd, sm, sv = adamw(small_pack("", args), small_pack("", g_out), small_pack("m_", args), small_pack("v_", args),
                       "adamw_small")
    off = 0
    for n in small_names:
        shp = args[n].shape
        size = math.prod(shp)
        deltas[n] = sd.reshape(-1)[off:off + size].reshape(shp)
        new_m[n] = sm.reshape(-1)[off:off + size].reshape(shp)
        new_v[n] = sv.reshape(-1)[off:off + size].reshape(shp)
        off += size

    outs = [loss, grad_x]
    outs += [g_out[n].reshape(args[n].shape) for n in names]
    outs += [deltas[n] for n in names] + [new_m[n] for n in names] + [new_v[n] for n in names]
    return tuple(outs)
```

```python
import functools
import math

import numpy as np
import jax
import jax.numpy as jnp
from jax import lax
from jax.experimental import pallas as pl
from jax.experimental.pallas import tpu as pltpu

F32 = jnp.float32
BF16 = jnp.bfloat16
MESH = pl.DeviceIdType.MESH

D_MODEL = 1024
DEPTH = 2
MLA_HEADS = 8
MLA_NOPE = 128
MLA_ROPE = 64
MLA_V = 128
MLA_QR = 256
MLA_KVR = 256
ROPE_THETA = 10000.0
FOX_HEADS = 16
FOX_HD = 64
D_FF = 2816
N_CHIPS = 4
FF_CHUNK = D_FF // N_CHIPS
ALPHA = (2.0 * DEPTH) ** 0.25
EPS = 1e-5
ADAM_LR = 0.001
ADAM_B1 = 0.9
ADAM_B2 = 0.999
ADAM_EPS = 1e-08
ADAM_WD = 0.01
ADAM_STEP = 10

VMEM_LIMIT = 56 * 1024 * 1024
TOKEN_TILE = 512
WGRAD_TOKENS = 2048
ATTN_TILE = 512
GATE_BLOCK = 512
FOX_GROUP = 8
MLA_GROUP = 4
MLA_FWD_GROUP = 8
COMM_BLOCK_BYTES = 8 * 1024 * 1024
ADAMW_BLOCK_BYTES = 2 * 1024 * 1024


def _cp(n_axes):
    return pltpu.CompilerParams(dimension_semantics=("arbitrary",) * n_axes, vmem_limit_bytes=VMEM_LIMIT)


def _dot(a, b):
    return jnp.dot(a, b, preferred_element_type=F32)


def _dot_nt(a, b):
    return lax.dot_general(a, b, (((1,), (1,)), ((), ())), preferred_element_type=F32)


def _dot_tn(a, b):
    return lax.dot_general(a, b, (((0,), (0,)), ((), ())), preferred_element_type=F32)


def _dot_f32(a, b):
    return jnp.dot(a, b, preferred_element_type=F32, precision=lax.Precision.HIGHEST)


def _sds(shape, dtype):
    return jax.ShapeDtypeStruct(shape, dtype)


def _place():
    return lax.axis_index("x"), lax.axis_index("y"), lax.axis_index("c")


class _Hosted:
    def __init__(self, inputs, out_shape, sems, start, finish, in_place=False):
        self.inputs, self.out_shape, self.sems, self.start, self.finish = inputs, out_shape, sems, start, finish
        self.in_place = in_place


def _call(body, name, grid, in_specs, out_specs, out_shape, args, scratch_shapes=(), hosted=None):
    in_specs, out_specs, out_shape, scratch_shapes = list(in_specs), list(out_specs), list(out_shape), list(scratch_shapes)
    if hosted is None:
        return pl.pallas_call(body, name=name, grid=grid, in_specs=in_specs, out_specs=out_specs, out_shape=out_shape,
                              scratch_shapes=scratch_shapes, compiler_params=_cp(len(grid)))(*args)
    n_in, n_out, n_scr = len(in_specs), len(out_specs), len(scratch_shapes)
    h_in, h_out = len(hosted.inputs), len(hosted.out_shape)

    def carried(*refs):
        o0 = n_in + h_in
        s0 = o0 + n_out + h_out
        c_in, c_out, c_sem = refs[n_in:o0], refs[o0 + n_out:s0], refs[s0 + n_scr:]
        ids = [pl.program_id(a) for a in range(len(grid))]
        first = functools.reduce(jnp.logical_and, [i == 0 for i in ids])
        last = functools.reduce(jnp.logical_and, [i == g - 1 for i, g in zip(ids, grid)])

        @pl.when(first)
        def _():
            hosted.start(c_in, c_out, c_sem)

        body(*refs[:n_in], *refs[o0:o0 + n_out], *refs[s0:s0 + n_scr])

        @pl.when(last)
        def _():
            hosted.finish(c_in, c_out, c_sem)

    hbm = pl.BlockSpec(memory_space=pl.ANY)
    aliases = {n_in + k: n_out + k for k in range(h_in)} if hosted.in_place else {}
    res = pl.pallas_call(
        carried, name=name, grid=grid, in_specs=in_specs + [hbm] * h_in, out_specs=out_specs + [hbm] * h_out,
        out_shape=out_shape + list(hosted.out_shape), scratch_shapes=scratch_shapes + list(hosted.sems),
        input_output_aliases=aliases, compiler_params=_cp(len(grid)))(*args, *hosted.inputs)
    return res[:n_out], res[n_out:]


def mod_linear(x, shift, scale, w, out_dtype, name, tn=None, emit_u=False, w_rows=None):
    t, d = x.shape
    n = w.shape[1] if w_rows is None else w_rows
    tn = n if tn is None else tn
    tm = TOKEN_TILE
    tps = (t // shift.shape[0]) // tm

    def body(x_ref, sh_ref, sc_ref, w_ref, o_ref, *rest):
        u = (x_ref[...] * (1.0 + sc_ref[...]) + sh_ref[...]).astype(BF16)
        o_ref[...] = (_dot(u, w_ref[...]) if w_rows is None else _dot_nt(u, w_ref[...])).astype(out_dtype)
        if emit_u:
            @pl.when(pl.program_id(1) == 0)
            def _():
                rest[0][...] = u

    vec = pl.BlockSpec((None, 1, d), lambda i, j: (i // tps, 0, 0))
    out_shape = [_sds((t, n), out_dtype)]
    out_specs = [pl.BlockSpec((tm, tn), lambda i, j: (i, j))]
    if emit_u:
        out_shape.append(_sds((t, d), BF16))
        out_specs.append(pl.BlockSpec((tm, d), lambda i, j: (i, 0)))
    w_spec = pl.BlockSpec((d, tn), lambda i, j: (0, j)) if w_rows is None else pl.BlockSpec((tn, d), lambda i, j: (j, 0))
    res = pl.pallas_call(
        body, name=name, grid=(t // tm, n // tn),
        in_specs=[pl.BlockSpec((tm, d), lambda i, j: (i, 0)), vec, vec, w_spec],
        out_specs=out_specs, out_shape=out_shape, compiler_params=_cp(2),
    )(x, shift, scale, w)
    return res if emit_u else res[0]


def _rms(h, g):
    rstd = lax.rsqrt(jnp.mean(h * h, axis=-1, keepdims=True) + EPS)
    return h * rstd, rstd


def mla_mid_fwd(h, g_q, g_kv, w_uq, w_uk, w_uv, cos8, sin8, cos64, sin64s, swap64, rope_to_heads, dup64, name):
    t = h.shape[0]
    tm = TOKEN_TILE
    hq = MLA_HEADS * MLA_NOPE
    hr = MLA_HEADS * MLA_ROPE // 2

    def body(h_ref, gq_ref, gkv_ref, wuq_ref, wuk_ref, wuv_ref, c8_ref, s8_ref, c64_ref, s64_ref, sw_ref, p_ref, d_ref,
             q_ref, kn_ref, v_ref, kr_ref, cq_ref, ckv_ref):
        hh = h_ref[...]
        cq = (_rms(hh[:, :MLA_QR], None)[0] * gq_ref[...]).astype(BF16)
        ckv = (_rms(hh[:, MLA_QR:MLA_QR + MLA_KVR], None)[0] * gkv_ref[...]).astype(BF16)
        cq_ref[...] = cq
        ckv_ref[...] = ckv
        q = _dot(cq, wuq_ref[...])
        x1 = q[:, hq:hq + hr]
        x2 = q[:, hq + hr:]
        cs = c8_ref[...]
        sn = s8_ref[...]
        rot = jnp.concatenate([x1 * cs - x2 * sn, x2 * cs + x1 * sn], axis=1).astype(BF16)
        q_ref[...] = jnp.concatenate([q[:, :hq].astype(BF16), _dot(rot, p_ref[...]).astype(BF16)], axis=1)
        kn_ref[...] = _dot(ckv, wuk_ref[...]).astype(BF16)
        v_ref[...] = _dot(ckv, wuv_ref[...]).astype(BF16)
        kr = hh[:, MLA_QR + MLA_KVR:]
        kr = (kr * c64_ref[...] + _dot_f32(kr, sw_ref[...]) * s64_ref[...]).astype(BF16)
        kr_ref[...] = _dot(kr, d_ref[...]).astype(BF16)

    def rows(n):
        return pl.BlockSpec((tm, n), lambda i: (i, 0))

    def whole(a):
        return pl.BlockSpec(a.shape, lambda i: (0,) * a.ndim)

    nq = w_uq.shape[1]
    return pl.pallas_call(
        body, name=name, grid=(t // tm,),
        in_specs=[rows(h.shape[1]), whole(g_q), whole(g_kv), whole(w_uq), whole(w_uk), whole(w_uv),
                  rows(hr), rows(hr), rows(MLA_ROPE), rows(MLA_ROPE), whole(swap64), whole(rope_to_heads), whole(dup64)],
        out_specs=[rows(nq), rows(hq), rows(hq), rows(2 * MLA_ROPE), rows(MLA_QR), rows(MLA_KVR)],
        out_shape=[_sds((t, nq), BF16), _sds((t, hq), BF16), _sds((t, hq), BF16), _sds((t, 2 * MLA_ROPE), BF16),
                   _sds((t, MLA_QR), BF16), _sds((t, MLA_KVR), BF16)],
        compiler_params=_cp(1),
    )(h, g_q, g_kv, w_uq, w_uk, w_uv, cos8, sin8, cos64, sin64s, swap64, rope_to_heads, dup64)


def _pick_lane(tile, idx):
    lane = lax.broadcasted_iota(jnp.int32, tile.shape, 1)
    return jnp.sum(jnp.where(lane == idx, tile, 0.0), axis=1, keepdims=True)


def _pick_row(tile, idx):
    row = lax.broadcasted_iota(jnp.int32, tile.shape, 0)
    return jnp.sum(jnp.where(row == idx, tile, 0.0), axis=0, keepdims=True)


def _put_lane(tile, idx, col):
    lane = lax.broadcasted_iota(jnp.int32, tile.shape, 1)
    return jnp.where(lane == idx, col, tile)


def _put_row(tile, idx, row):
    r = lax.broadcasted_iota(jnp.int32, tile.shape, 0)
    return tile + jnp.where(r == idx, row, 0.0)


def _causal_softmax_blocks(i, tq, heads):
    def block(j, carry, masked):
        new = []
        for (score_fn, pv_fn, _), (m, l, acc) in zip(heads, carry):
            sc = score_fn(j)
            if masked:
                keep = lax.broadcasted_iota(jnp.int32, sc.shape, 0) >= lax.broadcasted_iota(jnp.int32, sc.shape, 1)
                sc = jnp.where(keep, sc, -1e30)
            m_new = jnp.maximum(m, jnp.max(sc, axis=1, keepdims=True))
            a = jnp.exp(m - m_new)
            p = jnp.exp(sc - m_new)
            new.append((m_new, a * l + jnp.sum(p, axis=1, keepdims=True), a * acc + pv_fn(j, p.astype(BF16))))
        return tuple(new)

    init = tuple((jnp.full((tq, 1), -1e30, F32), jnp.zeros((tq, 1), F32), jnp.zeros((tq, dv), F32)) for _, _, dv in heads)
    carry = lax.fori_loop(0, i, lambda j, c: block(j, c, False), init)
    return [(acc / l, m + jnp.log(l)) for m, l, acc in block(i, carry, True)]


def fox_attn_fwd(qkv, cum, cum_rows, nb, name, hosted=None):
    t = qkv.shape[0]
    s = t // nb
    tq = ATTN_TILE
    nq = s // tq
    wide = FOX_GROUP * FOX_HD
    ngroups = FOX_HEADS // FOX_GROUP
    scale = FOX_HD ** -0.5

    def body(q_ref, k_ref, v_ref, cum_ref, cr_ref, o_ref, lse_ref):
        i = pl.program_id(1)
        hg = pl.program_id(2)

        @pl.when(hg == 0)
        def _():
            lse_ref[...] = jnp.zeros_like(lse_ref)

        low = lax.broadcasted_iota(jnp.int32, (tq, 128), 1) < FOX_HD
        cum_t = cum_ref[...]

        def rows_of(j):
            return pl.ds(pl.multiple_of(j * tq, tq), tq)

        def head(a):
            hd = FOX_GROUP * hg + a
            cols = slice(128 * (a // 2), 128 * (a // 2) + 128)
            q = q_ref[:, cols]
            qa = jnp.where(low if a % 2 == 0 else jnp.logical_not(low), q, jnp.zeros_like(q)) * scale
            fq = _pick_lane(cum_t, hd)
            return (lambda j: _dot_nt(qa, k_ref[rows_of(j), cols]) + fq - _pick_row(cr_ref[j], hd),
                    lambda j, p: _dot(p, v_ref[rows_of(j), cols]), 2 * FOX_HD)

        res = _causal_softmax_blocks(i, tq, [head(a) for a in range(FOX_GROUP)])
        o_ref[...] = jnp.concatenate([jnp.where(low, res[a][0], res[a + 1][0]) for a in range(0, FOX_GROUP, 2)],
                                     axis=1).astype(BF16)
        lse_t = lse_ref[...]
        for a in range(FOX_GROUP):
            lse_t = _put_lane(lse_t, FOX_GROUP * hg + a, res[a][1])
        lse_ref[...] = lse_t

    return _call(
        body, name, (nb, nq, ngroups),
        [pl.BlockSpec((tq, wide), lambda b, i, hg: (b * nq + i, hg)),
         pl.BlockSpec((s, wide), lambda b, i, hg: (b, ngroups + hg)),
         pl.BlockSpec((s, wide), lambda b, i, hg: (b, 2 * ngroups + hg)),
         pl.BlockSpec((tq, 128), lambda b, i, hg: (b * nq + i, 0)),
         pl.BlockSpec((nq, 16, tq), lambda b, i, hg: (b, 0, 0))],
        [pl.BlockSpec((tq, wide), lambda b, i, hg: (b * nq + i, hg)),
         pl.BlockSpec((tq, 128), lambda b, i, hg: (b * nq + i, 0))],
        [_sds((t, D_MODEL), BF16), _sds((t, 128), F32)], (qkv, qkv, qkv, cum, cum_rows), hosted=hosted)


def mla_attn_fwd(q, kn, kr2, v, nb, name, hosted=None):
    t = q.shape[0]
    s = t // nb
    tq = ATTN_TILE
    nq = s // tq
    ngroups = MLA_HEADS // MLA_FWD_GROUP
    wide = MLA_FWD_GROUP * MLA_NOPE
    rwide = MLA_FWD_GROUP * MLA_ROPE
    scale = (MLA_NOPE + MLA_ROPE) ** -0.5

    def body(qn_ref, qr_ref, kn_ref, kr_ref, v_ref, o_ref, lse_ref):
        i = pl.program_id(1)
        hg = pl.program_id(2)

        @pl.when(hg == 0)
        def _():
            lse_ref[...] = jnp.zeros_like(lse_ref)

        low = lax.broadcasted_iota(jnp.int32, (tq, 128), 1) < MLA_ROPE

        def rows_of(j):
            return pl.ds(pl.multiple_of(j * tq, tq), tq)

        def head(a):
            cols = slice(a * MLA_NOPE, (a + 1) * MLA_NOPE)
            qr = qr_ref[:, 128 * (a // 2):128 * (a // 2) + 128]
            q_cat = jnp.concatenate([qn_ref[:, cols], jnp.where(low if a % 2 == 0 else jnp.logical_not(low), qr,
                                                                jnp.zeros_like(qr))], axis=1)
            return (lambda j: _dot_nt(q_cat, jnp.concatenate([kn_ref[rows_of(j), cols], kr_ref[rows_of(j), :]], axis=1)) * scale,
                    lambda j, p: _dot(p, v_ref[rows_of(j), cols]), MLA_V)

        res = _causal_softmax_blocks(i, tq, [head(a) for a in range(MLA_FWD_GROUP)])
        o_ref[...] = jnp.concatenate([r[0] for r in res], axis=1).astype(BF16)
        lse_t = lse_ref[...]
        for a in range(MLA_FWD_GROUP):
            lse_t = _put_lane(lse_t, MLA_FWD_GROUP * hg + a, res[a][1])
        lse_ref[...] = lse_t

    rope0 = MLA_HEADS * MLA_NOPE // rwide
    return _call(
        body, name, (nb, nq, ngroups),
        [pl.BlockSpec((tq, wide), lambda b, i, hg: (b * nq + i, hg)),
         pl.BlockSpec((tq, rwide), lambda b, i, hg: (b * nq + i, rope0 + hg)),
         pl.BlockSpec((s, wide), lambda b, i, hg: (b, hg)),
         pl.BlockSpec((s, 128), lambda b, i, hg: (b, 0)),
         pl.BlockSpec((s, wide), lambda b, i, hg: (b, hg))],
        [pl.BlockSpec((tq, wide), lambda b, i, hg: (b * nq + i, hg)),
         pl.BlockSpec((tq, 128), lambda b, i, hg: (b * nq + i, 0))],
        [_sds((t, MLA_HEADS * MLA_V), BF16), _sds((t, 128), F32)], (q, q, kn, kr2, v), hosted=hosted)


def rows16(a, name):
    t = a.shape[0]
    tq = ATTN_TILE

    def body(a_ref, o_ref):
        o_ref[...] = a_ref[...].T[:16, :]

    return pl.pallas_call(
        body, name=name, grid=(t // tq,), in_specs=[pl.BlockSpec((tq, 128), lambda n: (n, 0))],
        out_specs=pl.BlockSpec((None, 16, tq), lambda n: (n, 0, 0)), out_shape=_sds((t // tq, 16, tq), F32),
        compiler_params=_cp(1),
    )(a)


def tokens128(rows, onehot, name):
    nblk, _, tq = rows.shape

    def body(r_ref, e_ref, o_ref):
        o_ref[...] = lax.dot_general(r_ref[...], e_ref[...], (((0,), (0,)), ((), ())), preferred_element_type=F32,
                                     precision=lax.Precision.HIGHEST)

    return pl.pallas_call(
        body, name=name, grid=(nblk,),
        in_specs=[pl.BlockSpec((None, 16, tq), lambda n: (n, 0, 0)), pl.BlockSpec((16, 128), lambda n: (0, 0))],
        out_specs=pl.BlockSpec((tq, 128), lambda n: (n, 0)), out_shape=_sds((nblk * tq, 128), F32),
        compiler_params=_cp(1),
    )(rows, onehot)


def _layer_norm(z, g, b):
    mu = jnp.mean(z, axis=-1, keepdims=True)
    zc = z - mu
    rstd = lax.rsqrt(jnp.mean(zc * zc, axis=-1, keepdims=True) + EPS)
    xhat = zc * rstd
    return xhat * g + b, xhat, rstd


def linear_resid_ln(a, w, x_in, gate, ln_g, ln_b, name):
    t, kdim = a.shape
    d = w.shape[1]
    tm = TOKEN_TILE
    tps = (t // gate.shape[0]) // tm

    def body(a_ref, w_ref, x_ref, gt_ref, g_ref, b_ref, y_ref, xo_ref):
        y = _dot(a_ref[...], w_ref[...])
        y_ref[...] = y
        z = ALPHA * x_ref[...] + (1.0 + gt_ref[...]) * y
        xo_ref[...] = _layer_norm(z, g_ref[...], b_ref[...])[0]

    rows = pl.BlockSpec((tm, d), lambda i: (i, 0))
    vec = pl.BlockSpec((1, d), lambda i: (0, 0))
    return pl.pallas_call(
        body, name=name, grid=(t // tm,),
        in_specs=[pl.BlockSpec((tm, kdim), lambda i: (i, 0)), pl.BlockSpec((kdim, d), lambda i: (0, 0)), rows,
                  pl.BlockSpec((None, 1, d), lambda i: (i // tps, 0, 0)), vec, vec],
        out_specs=[rows, rows], out_shape=[_sds((t, d), F32), _sds((t, d), F32)],
        compiler_params=_cp(1),
    )(a, w, x_in, gate, ln_g, ln_b)


def _resident(a):
    return pl.BlockSpec(a.shape, lambda *_: (0,) * a.ndim, pipeline_mode=pl.Buffered(1))


def ffn_fwd(x_in, shift, scale, gate, wg, wu, wd, ln_g, ln_b, name, hosted=None):
    t, d = x_in.shape
    c, _, fc = wg.shape
    tm = TOKEN_TILE
    tps = (t // gate.shape[0]) // tm

    def body(x_ref, sh_ref, sc_ref, gt_ref, wg_ref, wu_ref, wd_ref, g_ref, b_ref,
             u_ref, hg_ref, hu_ref, y_ref, xo_ref, acc_ref):
        cc = pl.program_id(1)

        @pl.when(cc == 0)
        def _():
            u_ref[...] = (x_ref[...] * (1.0 + sc_ref[...]) + sh_ref[...]).astype(BF16)
            acc_ref[...] = jnp.zeros_like(acc_ref)

        u = u_ref[...]
        hg = _dot(u, wg_ref[cc])
        hu = _dot(u, wu_ref[cc])
        hg_ref[...] = hg.astype(BF16)
        hu_ref[...] = hu.astype(BF16)
        act = (hg * jax.nn.sigmoid(hg) * hu).astype(BF16)
        acc_ref[...] += _dot(act, wd_ref[cc])

        @pl.when(cc == c - 1)
        def _():
            y = acc_ref[...]
            y_ref[...] = y
            z = ALPHA * x_ref[...] + (1.0 + gt_ref[...]) * y
            xo_ref[...] = _layer_norm(z, g_ref[...], b_ref[...])[0]

    rows = pl.BlockSpec((tm, d), lambda i, cc: (i, 0))
    bvec = pl.BlockSpec((None, 1, d), lambda i, cc: (i // tps, 0, 0))
    vec = pl.BlockSpec((1, d), lambda i, cc: (0, 0))
    hspec = pl.BlockSpec((None, tm, fc), lambda i, cc: (cc, i, 0))
    wcol = _resident(wg)
    return _call(
        body, name, (t // tm, c),
        [rows, bvec, bvec, bvec, wcol, wcol, _resident(wd), vec, vec],
        [rows, hspec, hspec, rows, rows],
        [_sds((t, d), BF16), _sds((c, t, fc), BF16), _sds((c, t, fc), BF16), _sds((t, d), F32), _sds((t, d), F32)],
        (x_in, shift, scale, gate, wg, wu, wd, ln_g, ln_b), scratch_shapes=[pltpu.VMEM((tm, d), F32)], hosted=hosted)


def fox_gate_fwd(hf, b_f, tri, n_batch, name):
    t, n = hf.shape
    blk = tri.shape[0]
    nb = (t // n_batch) // blk

    def body(hf_ref, b_ref, tri_ref, o_ref, carry_ref):
        @pl.when(pl.program_id(1) == 0)
        def _():
            carry_ref[...] = jnp.zeros_like(carry_ref)

        xx = hf_ref[...] + b_ref[...]
        lf = jnp.minimum(xx, 0.0) - jnp.log(1.0 + jnp.exp(-jnp.abs(xx)))
        cum = _dot_f32(tri_ref[...], lf) + carry_ref[...]
        o_ref[...] = cum
        carry_ref[...] = cum[blk - 1:blk, :]

    return pl.pallas_call(
        body, name=name, grid=(n_batch, nb),
        in_specs=[pl.BlockSpec((blk, n), lambda bb, i: (bb * nb + i, 0)), pl.BlockSpec((1, n), lambda bb, i: (0, 0)),
                  pl.BlockSpec((blk, blk), lambda bb, i: (0, 0))],
        out_specs=pl.BlockSpec((blk, n), lambda bb, i: (bb * nb + i, 0)),
        out_shape=_sds((t, n), F32), scratch_shapes=[pltpu.VMEM((1, n), F32)],
        compiler_params=_cp(2),
    )(hf, b_f, tri)


def ln_bwd(dxo, x_in, y, gate, ln_g, name, target=None):
    t, d = dxo.shape
    nb = gate.shape[0]
    tm = TOKEN_TILE
    tps = (t // nb) // tm
    with_loss = target is not None

    def body(dxo_ref, *refs):
        if with_loss:
            t_ref, x_ref, y_ref, gt_ref, g_ref, dz_ref, dy_ref, dg_ref, db_ref, dgt_ref, l_ref = refs
        else:
            x_ref, y_ref, gt_ref, g_ref, dz_ref, dy_ref, dg_ref, db_ref, dgt_ref = refs
        i = pl.program_id(0)

        @pl.when(i == 0)
        def _():
            dg_ref[...] = jnp.zeros_like(dg_ref)
            db_ref[...] = jnp.zeros_like(db_ref)
            if with_loss:
                l_ref[...] = jnp.zeros_like(l_ref)

        @pl.when(i % tps == 0)
        def _():
            dgt_ref[...] = jnp.zeros_like(dgt_ref)

        yy = y_ref[...]
        g1 = 1.0 + gt_ref[...]
        z = ALPHA * x_ref[...] + g1 * yy
        _, xhat, rstd = _layer_norm(z, 1.0, 0.0)
        dxo_v = dxo_ref[...]
        if with_loss:
            err = dxo_v - t_ref[...]
            l_ref[...] += jnp.sum(err * err, axis=0, keepdims=True)
            dxo_v = err / d
        dg_ref[...] += jnp.sum(dxo_v * xhat, axis=0, keepdims=True)
        db_ref[...] += jnp.sum(dxo_v, axis=0, keepdims=True)
        dxh = dxo_v * g_ref[...]
        dz = rstd * (dxh - jnp.mean(dxh, axis=-1, keepdims=True) - xhat * jnp.mean(dxh * xhat, axis=-1, keepdims=True))
        dz_ref[...] = dz
        dy_ref[...] = (g1 * dz).astype(BF16)
        dgt_ref[...] += jnp.sum(dz * yy, axis=0, keepdims=True)

    rows = pl.BlockSpec((tm, d), lambda i: (i, 0))
    vec = pl.BlockSpec((1, d), lambda i: (0, 0))
    bvec = pl.BlockSpec((None, 1, d), lambda i: (i // tps, 0, 0))
    return pl.pallas_call(
        body, name=name, grid=(t // tm,), in_specs=[rows] * (4 if with_loss else 3) + [bvec, vec],
        out_specs=[rows, rows, vec, vec, bvec] + ([vec] if with_loss else []),
        out_shape=[_sds((t, d), F32), _sds((t, d), BF16), _sds((1, d), F32), _sds((1, d), F32), _sds((nb, 1, d), F32)]
        + ([_sds((1, d), F32)] if with_loss else []),
        compiler_params=_cp(1),
    )(dxo, *([target] if with_loss else []), x_in, y, gate, ln_g)


def _mod_bwd_tail(du, dz_ref, x_ref, sc_ref, dx_ref, dsc_ref, dsh_ref, first):
    @pl.when(first)
    def _():
        dsc_ref[...] = jnp.zeros_like(dsc_ref)
        dsh_ref[...] = jnp.zeros_like(dsh_ref)

    dx_ref[...] = ALPHA * dz_ref[...] + du * (1.0 + sc_ref[...])
    dsc_ref[...] += jnp.sum(du * x_ref[...], axis=0, keepdims=True)
    dsh_ref[...] += jnp.sum(du, axis=0, keepdims=True)


def ffn_bwd(dy, hg, hu, wg, wu, wd, dz, x_in, scale, name, hosted=None):
    t, d = dy.shape
    c, _, fc = wg.shape
    nb = scale.shape[0]
    tm = TOKEN_TILE
    tps = (t // nb) // tm

    def body(dy_ref, hg_ref, hu_ref, wg_ref, wu_ref, wd_ref, dz_ref, x_ref, sc_ref,
             dhg_ref, dhu_ref, act_ref, dx_ref, dsc_ref, dsh_ref, acc_ref):
        i = pl.program_id(0)
        cc = pl.program_id(1)

        @pl.when(cc == 0)
        def _():
            acc_ref[...] = jnp.zeros_like(acc_ref)

        hgv = hg_ref[...].astype(F32)
        huv = hu_ref[...].astype(F32)
        da = _dot_nt(dy_ref[...], wd_ref[cc])
        sg = jax.nn.sigmoid(hgv)
        sl = hgv * sg
        act_ref[...] = (sl * huv).astype(BF16)
        dhu = (da * sl).astype(BF16)
        dhg = (da * huv * (sg * (1.0 + hgv * (1.0 - sg)))).astype(BF16)
        dhu_ref[...] = dhu
        dhg_ref[...] = dhg
        acc_ref[...] += _dot_nt(dhg, wg_ref[cc]) + _dot_nt(dhu, wu_ref[cc])

        @pl.when(cc == c - 1)
        def _():
            _mod_bwd_tail(acc_ref[...], dz_ref, x_ref, sc_ref, dx_ref, dsc_ref, dsh_ref, i % tps == 0)

    rows = pl.BlockSpec((tm, d), lambda i, cc: (i, 0))
    bvec = pl.BlockSpec((None, 1, d), lambda i, cc: (i // tps, 0, 0))
    hspec = pl.BlockSpec((None, tm, fc), lambda i, cc: (cc, i, 0))
    wcol = _resident(wg)
    return _call(
        body, name, (t // tm, c),
        [rows, hspec, hspec, wcol, wcol, _resident(wd), rows, rows, bvec],
        [hspec, hspec, hspec, rows, bvec, bvec],
        [_sds((c, t, fc), BF16), _sds((c, t, fc), BF16), _sds((c, t, fc), BF16), _sds((t, d), F32),
         _sds((nb, 1, d), F32), _sds((nb, 1, d), F32)],
        (dy, hg, hu, wg, wu, wd, dz, x_in, scale), scratch_shapes=[pltpu.VMEM((tm, d), F32)], hosted=hosted)


def linear_nt_mod_bwd(pairs, dz, x_in, scale, name, hosted=None):
    t, d = dz.shape
    nb = scale.shape[0]
    tm = TOKEN_TILE
    tps = (t // nb) // tm
    npairs = len(pairs)

    def body(*refs):
        dh_refs = refs[:npairs]
        w_refs = refs[npairs:2 * npairs]
        dz_ref, x_ref, sc_ref, dx_ref, dsc_ref, dsh_ref = refs[2 * npairs:]
        du = None
        for (_, _, blk), dh_ref, w_ref in zip(pairs, dh_refs, w_refs):
            dh = dh_ref[...].astype(BF16)
            term = _dot_nt(dh, w_ref[...]) if blk is None else _dot(dh, w_ref[...])
            du = term if du is None else du + term
        _mod_bwd_tail(du, dz_ref, x_ref, sc_ref, dx_ref, dsc_ref, dsh_ref, pl.program_id(0) % tps == 0)

    rows = pl.BlockSpec((tm, d), lambda i: (i, 0))
    bvec = pl.BlockSpec((None, 1, d), lambda i: (i // tps, 0, 0))
    in_specs = [pl.BlockSpec((tm, dh.shape[1]), lambda i: (i, 0)) for dh, _, _ in pairs]
    for dh, w, blk in pairs:
        if blk is None:
            in_specs.append(pl.BlockSpec(w.shape, lambda i: (0, 0)))
        else:
            in_specs.append(pl.BlockSpec((dh.shape[1], d), lambda i, blk=blk: (blk, 0)))
    in_specs += [rows, rows, bvec]
    return _call(
        body, name, (t // tm,), in_specs, [rows, bvec, bvec],
        [_sds((t, d), F32), _sds((nb, 1, d), F32), _sds((nb, 1, d), F32)],
        (*[dh for dh, _, _ in pairs], *[w for _, w, _ in pairs], dz, x_in, scale), hosted=hosted)


def linear_nt_delta(dy, w_o, o, head_sel, name):
    t, d = dy.shape
    hdv = w_o.shape[0]
    tm = TOKEN_TILE

    def body(dy_ref, w_ref, o_ref, sel_ref, do_ref, dl_ref):
        do = _dot_nt(dy_ref[...], w_ref[...])
        do_ref[...] = do.astype(BF16)
        dl_ref[...] = _dot_f32(do * o_ref[...].astype(F32), sel_ref[...])

    return pl.pallas_call(
        body, name=name, grid=(t // tm,),
        in_specs=[pl.BlockSpec((tm, d), lambda i: (i, 0)), pl.BlockSpec((hdv, d), lambda i: (0, 0)),
                  pl.BlockSpec((tm, hdv), lambda i: (i, 0)), pl.BlockSpec(head_sel.shape, lambda i: (0, 0))],
        out_specs=[pl.BlockSpec((tm, hdv), lambda i: (i, 0)), pl.BlockSpec((tm, 128), lambda i: (i, 0))],
        out_shape=[_sds((t, hdv), BF16), _sds((t, 128), F32)], compiler_params=_cp(1),
    )(dy, w_o, o, head_sel)


def _attn_bwd_blocks(j, nk, tk, scale, heads):
    def block(i, carry, masked):
        new = []
        for hd, (dk_acc, dv_acc, dfk_acc) in zip(heads, carry):
            qb = hd["q"](i)
            dob = hd["do"](i)
            lse_row, dl_row = hd["rows"](i)
            st = _dot_nt(hd["k"], qb)
            if scale is not None:
                st = st * scale
            if hd["bias"] is not None:
                fq_row, fk_col = hd["bias"](i)
                st = st + fq_row - fk_col
            if masked:
                keep = lax.broadcasted_iota(jnp.int32, st.shape, 1) >= lax.broadcasted_iota(jnp.int32, st.shape, 0)
                st = jnp.where(keep, st, -1e30)
            pt = jnp.exp(st - lse_row)
            dv_acc = dv_acc + _dot(pt.astype(BF16), dob)
            dst = pt * (_dot_nt(hd["v"], dob) - dl_row)
            if hd["add_dfq"] is not None:
                dfk_acc = dfk_acc - jnp.sum(dst, axis=1, keepdims=True)
                hd["add_dfq"](i, jnp.sum(dst, axis=0, keepdims=True))
            dsb = (dst if scale is None else dst * scale).astype(BF16)
            dk_acc = dk_acc + _dot(dsb, qb)
            hd["add_dq"](i, _dot_tn(dsb, hd["k"] if scale is not None else hd["k_scaled"]))
            new.append((dk_acc, dv_acc, dfk_acc))
        return tuple(new)

    init = tuple((jnp.zeros((tk, hd["k"].shape[1]), F32), jnp.zeros((tk, hd["v"].shape[1]), F32), jnp.zeros((tk, 1), F32))
                 for hd in heads)
    carry = block(j, init, True)
    return lax.fori_loop(j + 1, nk, lambda i, c: block(i, c, False), carry)


def fox_attn_bwd(qkv, do, cum, cum_rows, lse_rows, delta_rows, nb, name, hosted=None):
    t = qkv.shape[0]
    s = t // nb
    tk = ATTN_TILE
    nk = s // tk
    scale = FOX_HD ** -0.5

    def body(q_ref, k_ref, v_ref, do_ref, cum_ref, cr_ref, lr_ref, dr_ref, dq_ref, dk_ref, dv_ref, dfq_ref, dfk_ref):
        hg = pl.program_id(1)
        j = pl.program_id(2)

        @pl.when(j == 0)
        def _():
            dq_ref[...] = jnp.zeros_like(dq_ref)

        @pl.when((j == 0) & (hg == 0))
        def _():
            dfq_ref[...] = jnp.zeros_like(dfq_ref)
            dfk_ref[...] = jnp.zeros_like(dfk_ref)

        low = lax.broadcasted_iota(jnp.int32, (tk, 128), 1) < FOX_HD
        cum_t = cum_ref[...]

        def rows_of(i):
            return pl.ds(pl.multiple_of(i * tk, tk), tk)

        def head(a):
            hd = FOX_GROUP * hg + a
            cols = slice(128 * (a // 2), 128 * (a // 2) + 128)
            half = low if a % 2 == 0 else jnp.logical_not(low)
            kb = k_ref[:, cols]
            vb = v_ref[:, cols]
            fk = _pick_lane(cum_t, hd)

            def add_dq(i, val):
                dq_ref[rows_of(i), cols] += val

            def add_dfq(i, val):
                dfq_ref[i] = _put_row(dfq_ref[i], hd, val)

            ka = jnp.where(half, kb, jnp.zeros_like(kb))
            return dict(q=lambda i: q_ref[rows_of(i), cols] * scale, do=lambda i: do_ref[rows_of(i), cols],
                        k=ka, k_scaled=ka * scale, v=jnp.where(half, vb, jnp.zeros_like(vb)),
                        rows=lambda i: (_pick_row(lr_ref[i], hd), _pick_row(dr_ref[i], hd)),
                        bias=lambda i: (_pick_row(cr_ref[i], hd), fk), add_dq=add_dq, add_dfq=add_dfq)

        res = _attn_bwd_blocks(j, nk, tk, None, [head(a) for a in range(FOX_GROUP)])
        dk_ref[...] = jnp.concatenate([jnp.where(low, res[a][0], res[a + 1][0]) for a in range(0, FOX_GROUP, 2)],
                                      axis=1).astype(BF16)
        dv_ref[...] = jnp.concatenate([jnp.where(low, res[a][1], res[a + 1][1]) for a in range(0, FOX_GROUP, 2)],
                                      axis=1).astype(BF16)
        for a in range(FOX_GROUP):
            dfk_ref[j] = _put_row(dfk_ref[j], FOX_GROUP * hg + a, jnp.broadcast_to(res[a][2], (tk, 128)).T[0:1, :])

    wide = FOX_GROUP * FOX_HD
    ngroups = FOX_HEADS // FOX_GROUP
    rowsp = pl.BlockSpec((nk, 16, tk), lambda b, hg, j: (b, 0, 0))
    return _call(
        body, name, (nb, ngroups, nk),
        [pl.BlockSpec((s, wide), lambda b, hg, j: (b, hg)),
         pl.BlockSpec((tk, wide), lambda b, hg, j: (b * nk + j, ngroups + hg)),
         pl.BlockSpec((tk, wide), lambda b, hg, j: (b * nk + j, 2 * ngroups + hg)),
         pl.BlockSpec((s, wide), lambda b, hg, j: (b, hg)),
         pl.BlockSpec((tk, 128), lambda b, hg, j: (b * nk + j, 0)),
         rowsp, rowsp, rowsp],
        [pl.BlockSpec((s, wide), lambda b, hg, j: (b, hg)),
         pl.BlockSpec((tk, wide), lambda b, hg, j: (b * nk + j, hg)),
         pl.BlockSpec((tk, wide), lambda b, hg, j: (b * nk + j, hg)),
         rowsp, rowsp],
        [_sds((t, D_MODEL), F32), _sds((t, D_MODEL), BF16), _sds((t, D_MODEL), BF16),
         _sds((t // tk, 16, tk), F32), _sds((t // tk, 16, tk), F32)],
        (qkv, qkv, qkv, do, cum, cum_rows, lse_rows, delta_rows), hosted=hosted)


def mla_attn_bwd(q, kn, kr2, v, do, lse_rows, delta_rows, nb, name, hosted=None):
    t = q.shape[0]
    s = t // nb
    tk = ATTN_TILE
    nk = s // tk
    ngroups = MLA_HEADS // MLA_GROUP
    wide = MLA_GROUP * MLA_NOPE
    rwide = MLA_GROUP * MLA_ROPE
    scale = (MLA_NOPE + MLA_ROPE) ** -0.5

    def body(qn_ref, qr_ref, kn_ref, kr_ref, v_ref, do_ref, lr_ref, dr_ref, dqn_ref, dqr_ref, dkn_ref, dkr_ref, dv_ref):
        hg = pl.program_id(1)
        j = pl.program_id(2)

        @pl.when(j == 0)
        def _():
            dqn_ref[...] = jnp.zeros_like(dqn_ref)
            dqr_ref[...] = jnp.zeros_like(dqr_ref)

        low = lax.broadcasted_iota(jnp.int32, (tk, 128), 1) < MLA_ROPE
        kr = kr_ref[...]

        def rows_of(i):
            return pl.ds(pl.multiple_of(i * tk, tk), tk)

        def head(a):
            cols = slice(a * MLA_NOPE, (a + 1) * MLA_NOPE)
            rcols = slice(128 * (a // 2), 128 * (a // 2) + 128)
            mine = low if a % 2 == 0 else jnp.logical_not(low)
            hd = MLA_GROUP * hg + a

            def q_fn(i):
                qr = qr_ref[rows_of(i), rcols]
                return jnp.concatenate([qn_ref[rows_of(i), cols], jnp.where(mine, qr, jnp.zeros_like(qr))], axis=1)

            def add_dq(i, val):
                dqn_ref[rows_of(i), cols] += val[:, :MLA_NOPE]
                dqr_ref[rows_of(i), cols] += val[:, MLA_NOPE:]

            return dict(q=q_fn, do=lambda i: do_ref[rows_of(i), cols], k=jnp.concatenate([kn_ref[:, cols], kr], axis=1),
                        v=v_ref[:, cols], rows=lambda i: (_pick_row(lr_ref[i], hd), _pick_row(dr_ref[i], hd)),
                        bias=None, add_dq=add_dq, add_dfq=None)

        res = _attn_bwd_blocks(j, nk, tk, scale, [head(a) for a in range(MLA_GROUP)])
        dkn_ref[...] = jnp.concatenate([r[0][:, :MLA_NOPE] for r in res], axis=1).astype(BF16)
        dkr_ref[...] = jnp.concatenate([r[0][:, MLA_NOPE:] for r in res], axis=1).astype(BF16)
        dv_ref[...] = jnp.concatenate([r[1] for r in res], axis=1).astype(BF16)

    full = pl.BlockSpec((s, wide), lambda b, hg, j: (b, hg))
    blk = pl.BlockSpec((tk, wide), lambda b, hg, j: (b * nk + j, hg))
    rowsp = pl.BlockSpec((nk, 16, tk), lambda b, hg, j: (b, 0, 0))
    total = MLA_HEADS * MLA_V
    rope0 = MLA_HEADS * MLA_NOPE // rwide
    return _call(
        body, name, (nb, ngroups, nk),
        [full, pl.BlockSpec((s, rwide), lambda b, hg, j: (b, rope0 + hg)), blk,
         pl.BlockSpec((tk, 128), lambda b, hg, j: (b * nk + j, 0)), blk, full, rowsp, rowsp],
        [full, full, blk, blk, blk],
        [_sds((t, total), F32), _sds((t, total), F32), _sds((t, total), BF16), _sds((t, total), BF16),
         _sds((t, total), BF16)],
        (q, q, kn, kr2, v, do, lse_rows, delta_rows), hosted=hosted)


def mla_mid_bwd(dqn, dqr, dkn, dv, dkr_heads, h, g_q, g_kv, w_uq, w_uk, w_uv, cos8, sin8, cos64, sin64s, swap64,
                heads_to_rope, head_sum, name, hosted=None):
    t = h.shape[0]
    tm = TOKEN_TILE
    hq = MLA_HEADS * MLA_NOPE
    hr = MLA_HEADS * MLA_ROPE // 2
    nq = w_uq.shape[1]

    def body(dqn_ref, dqr_ref, dkn_ref, dv_ref, dkr_ref, h_ref, gq_ref, gkv_ref, wuq_ref, wuk_ref, wuv_ref,
             c8_ref, s8_ref, c64_ref, s64_ref, sw_ref, hp_ref, hs_ref, dh_ref, dqp_ref, dgq_ref, dgkv_ref):
        @pl.when(pl.program_id(0) == 0)
        def _():
            dgq_ref[...] = jnp.zeros_like(dgq_ref)
            dgkv_ref[...] = jnp.zeros_like(dgkv_ref)

        drot = _dot(dqr_ref[...].astype(BF16), hp_ref[...])
        o1 = drot[:, :hr]
        o2 = drot[:, hr:]
        cs = c8_ref[...]
        sn = s8_ref[...]
        dqp = jnp.concatenate([dqn_ref[...].astype(BF16), (o1 * cs + o2 * sn).astype(BF16),
                               (o2 * cs - o1 * sn).astype(BF16)], axis=1)
        dqp_ref[...] = dqp
        dcq = _dot_nt(dqp, wuq_ref[...])
        dckv = _dot_nt(dkn_ref[...], wuk_ref[...]) + _dot_nt(dv_ref[...], wuv_ref[...])
        hh = h_ref[...]

        def rms_bwd(hpart, g, dc, dg_ref):
            hhat, rstd = _rms(hpart, None)
            dg_ref[...] += jnp.sum(dc * hhat, axis=0, keepdims=True)
            dcg = dc * g
            return rstd * (dcg - hhat * jnp.mean(dcg * hhat, axis=-1, keepdims=True))

        dhq = rms_bwd(hh[:, :MLA_QR], gq_ref[...], dcq, dgq_ref)
        dhkv = rms_bwd(hh[:, MLA_QR:MLA_QR + MLA_KVR], gkv_ref[...], dckv, dgkv_ref)
        dkr = _dot(dkr_ref[...], hs_ref[...])
        dkr_pre = dkr * c64_ref[...] + _dot_f32(dkr * s64_ref[...], sw_ref[...])
        dh_ref[...] = jnp.concatenate([dhq, dhkv, dkr_pre], axis=1).astype(BF16)

    def rows(n):
        return pl.BlockSpec((tm, n), lambda i: (i, 0))

    def whole(a):
        return pl.BlockSpec(a.shape, lambda i: (0,) * a.ndim)

    return _call(
        body, name, (t // tm,),
        [rows(hq), rows(hq), rows(hq), rows(hq), rows(hq), rows(h.shape[1]), whole(g_q), whole(g_kv),
         whole(w_uq), whole(w_uk), whole(w_uv), rows(hr), rows(hr), rows(MLA_ROPE), rows(MLA_ROPE),
         whole(swap64), whole(heads_to_rope), whole(head_sum)],
        [rows(h.shape[1]), rows(nq), pl.BlockSpec((1, MLA_QR), lambda i: (0, 0)),
         pl.BlockSpec((1, MLA_KVR), lambda i: (0, 0))],
        [_sds((t, h.shape[1]), BF16), _sds((t, nq), BF16), _sds((1, MLA_QR), F32), _sds((1, MLA_KVR), F32)],
        (dqn, dqr, dkn, dv, dkr_heads, h, g_q, g_kv, w_uq, w_uk, w_uv, cos8, sin8, cos64, sin64s, swap64,
         heads_to_rope, head_sum), hosted=hosted)


def fox_gate_bwd(dcum, hf, b_f, triu, n_batch, name):
    t, n = hf.shape
    blk = triu.shape[0]
    nb = (t // n_batch) // blk

    def body(dc_ref, hf_ref, b_ref, tri_ref, o_ref, db_ref, carry_ref):
        @pl.when(pl.program_id(1) == 0)
        def _():
            carry_ref[...] = jnp.zeros_like(carry_ref)

        @pl.when((pl.program_id(0) == 0) & (pl.program_id(1) == 0))
        def _():
            db_ref[...] = jnp.zeros_like(db_ref)

        rc = _dot_f32(tri_ref[...], dc_ref[...]) + carry_ref[...]
        carry_ref[...] = rc[0:1, :]
        dhf = rc * jax.nn.sigmoid(-(hf_ref[...] + b_ref[...]))
        o_ref[...] = dhf.astype(BF16)
        db_ref[...] += jnp.sum(dhf, axis=0, keepdims=True)

    rev = pl.BlockSpec((blk, n), lambda bb, i: (bb * nb + nb - 1 - i, 0))
    return pl.pallas_call(
        body, name=name, grid=(n_batch, nb),
        in_specs=[rev, rev, pl.BlockSpec((1, n), lambda bb, i: (0, 0)), pl.BlockSpec((blk, blk), lambda bb, i: (0, 0))],
        out_specs=[rev, pl.BlockSpec((1, n), lambda bb, i: (0, 0))],
        out_shape=[_sds((t, n), BF16), _sds((1, n), F32)], scratch_shapes=[pltpu.VMEM((1, n), F32)],
        compiler_params=_cp(2),
    )(dcum, hf, b_f, triu)


def wgrad(a, bm, name, with_bf16=False, bt=WGRAD_TOKENS):
    ca, t, kd = a.shape
    cb, _, nd = bm.shape
    c = max(ca, cb)
    bn = nd
    if nd > 1024 and nd % 1024 == 0:
        bn = 1024
    nsteps = t // bt

    def body(a_ref, b_ref, o_ref, *rest):
        @pl.when(pl.program_id(2) == 0)
        def _():
            o_ref[...] = jnp.zeros_like(o_ref)

        o_ref[...] += _dot_tn(a_ref[...].astype(BF16), b_ref[...].astype(BF16))
        if with_bf16:
            @pl.when(pl.program_id(2) == nsteps - 1)
            def _():
                rest[0][...] = o_ref[...].astype(BF16)

    out_spec = pl.BlockSpec((None, kd, bn), lambda cc, n, tt: (cc, 0, n))
    res = pl.pallas_call(
        body, name=name, grid=(c, nd // bn, nsteps),
        in_specs=[pl.BlockSpec((None, bt, kd), lambda cc, n, tt: (cc if ca > 1 else 0, tt, 0)),
                  pl.BlockSpec((None, bt, bn), lambda cc, n, tt: (cc if cb > 1 else 0, tt, n))],
        out_specs=[out_spec, out_spec] if with_bf16 else out_spec,
        out_shape=[_sds((c, kd, nd), F32), _sds((c, kd, nd), BF16)] if with_bf16 else _sds((c, kd, nd), F32),
        compiler_params=_cp(3),
    )(a, bm)
    return res


def ada_mod_part(c_all, ada_w, name):
    nl, d, n = ada_w.shape
    rows = c_all.shape[0]
    tn = 512

    def body(c_ref, w_ref, o_ref):
        cv = c_ref[...]
        act = (cv * jax.nn.sigmoid(cv)).astype(BF16)
        o_ref[...] = _dot(act, w_ref[...].astype(BF16))

    return pl.pallas_call(
        body, name=name, grid=(nl, n // tn),
        in_specs=[pl.BlockSpec((rows, d), lambda l, j: (0, 0)), pl.BlockSpec((None, d, tn), lambda l, j: (l, 0, j))],
        out_specs=pl.BlockSpec((None, rows, tn), lambda l, j: (l, 0, j)),
        out_shape=_sds((nl, rows, n), F32), compiler_params=_cp(2),
    )(c_all, ada_w)


def ada_grad(c_all_t, dmod, name):
    nl, rows, n = dmod.shape
    d = c_all_t.shape[0]
    tn = 512

    def body(c_ref, dm_ref, o_ref):
        cv = c_ref[...]
        act = (cv * jax.nn.sigmoid(cv)).astype(BF16)
        o_ref[...] = _dot(act, dm_ref[...].astype(BF16))

    return pl.pallas_call(
        body, name=name, grid=(nl, n // tn),
        in_specs=[pl.BlockSpec((d, rows), lambda l, j: (0, 0)), pl.BlockSpec((None, rows, tn), lambda l, j: (l, 0, j))],
        out_specs=pl.BlockSpec((None, d, tn), lambda l, j: (l, 0, j)),
        out_shape=_sds((nl, d, n), F32), compiler_params=_cp(2),
    )(c_all_t, dmod)


def sum_leading(a, name):
    g, r, n = a.shape

    def body(a_ref, o_ref):
        acc = a_ref[0]
        for kk in range(1, g):
            acc = acc + a_ref[kk]
        o_ref[...] = acc

    return pl.pallas_call(
        body, name=name, grid=(1,), in_specs=[pl.BlockSpec((g, r, n), lambda i: (0, 0, 0))],
        out_specs=pl.BlockSpec((r, n), lambda i: (0, 0)), out_shape=_sds((r, n), F32), compiler_params=_cp(1),
    )(a)


def adamw(w, g, m, v, name):
    r, n = w.shape
    fits = [cand for cand in range(8, r, 8) if r % cand == 0 and cand * n * 4 <= ADAMW_BLOCK_BYTES]
    br = max(fits) if fits else r
    c1 = 1.0 - ADAM_B1 ** ADAM_STEP
    c2 = 1.0 - ADAM_B2 ** ADAM_STEP

    def body(w_ref, g_ref, m_ref, v_ref, d_ref, mo_ref, vo_ref):
        gv = g_ref[...]
        mn = ADAM_B1 * m_ref[...] + (1.0 - ADAM_B1) * gv
        vn = ADAM_B2 * v_ref[...] + (1.0 - ADAM_B2) * (gv * gv)
        mo_ref[...] = mn
        vo_ref[...] = vn
        d_ref[...] = -ADAM_LR * ((mn / c1) / (jnp.sqrt(vn / c2) + ADAM_EPS) + ADAM_WD * w_ref[...])

    spec = pl.BlockSpec((br, n), lambda i: (i, 0))
    return _call(body, name, (r // br,), [spec] * 4, [spec] * 3, [_sds((r, n), F32)] * 3, (w, g, m, v))


def all_gather8(x_blk, name, hosted=None):
    m_per, n = x_blk.shape
    h_in = 0 if hosted is None else len(hosted.inputs)
    h_out = 0 if hosted is None else len(hosted.out_shape)

    def body(x_ref, *refs):
        c_in, (out_ref, *c_out), (send_sems, recv_sems, local_sem, *c_sem) = (
            refs[:h_in], refs[h_in:h_in + 1 + h_out], refs[h_in + 1 + h_out:])
        if hosted is not None:
            hosted.start(c_in, c_out, c_sem)
        gather(x_ref, out_ref, send_sems, recv_sems, local_sem)
        if hosted is not None:
            hosted.finish(c_in, c_out, c_sem)

    def gather(x_ref, out_ref, send_sems, recv_sems, local_sem):
        x, y, c = _place()
        me, sibling = (x, y, c), (x, y, 1 - c)
        chips = [(1 - x, y), (x, 1 - y), (1 - x, 1 - y)]

        def rows(px, py, pc):
            return out_ref.at[pl.ds((4 * px + 2 * py + pc) * m_per, m_per), :]

        def copy(k, block, to, src=None):
            return pltpu.make_async_remote_copy(
                src_ref=rows(*block) if src is None else src, dst_ref=rows(*block),
                send_sem=send_sems.at[k], recv_sem=recv_sems.at[k], device_id=to, device_id_type=MESH)

        mine = pltpu.make_async_copy(x_ref, rows(*me), local_sem)
        mine.start()
        first = [copy(0, me, sibling, src=x_ref)]
        first += [copy(1 + j, me, (*chip, c), src=x_ref) for j, chip in enumerate(chips)]
        for cp in first:
            cp.start()
        passed = [copy(4 + j, (*chip, c), sibling) for j, chip in enumerate(chips)]
        for j, chip in enumerate(chips):
            copy(1 + j, (*chip, c), me).wait_recv()
            passed[j].start()
        copy(0, sibling, me).wait_recv()
        for j, chip in enumerate(chips):
            copy(4 + j, (*chip, 1 - c), me).wait_recv()
        for cp in first + passed:
            cp.wait_send()
        mine.wait()

    hbm = pl.BlockSpec(memory_space=pl.ANY)
    vmem = pl.BlockSpec(memory_space=pltpu.VMEM)
    res = pl.pallas_call(
        body, name=name,
        out_shape=[_sds((8 * m_per, n), x_blk.dtype)] + ([] if hosted is None else list(hosted.out_shape)),
        in_specs=[vmem] + [hbm] * h_in, out_specs=[vmem] + [hbm] * h_out,
        scratch_shapes=[pltpu.SemaphoreType.DMA((7,)), pltpu.SemaphoreType.DMA((7,)), pltpu.SemaphoreType.DMA]
        + ([] if hosted is None else list(hosted.sems)),
        compiler_params=pltpu.CompilerParams(vmem_limit_bytes=VMEM_LIMIT),
    )(x_blk, *([] if hosted is None else hosted.inputs))
    return res[0] if hosted is None else (res[0], res[1:])


def _gather_comm(shards):
    nt = len(shards)

    def parts(w_refs, out_refs, sems, finishing):
        send_sems, recv_sems, own_send, own_recv = sems
        x, y, c = _place()
        sibling = (x, y, 1 - c)
        chips = [(1 - x, y), (x, 1 - y), (1 - x, 1 - y)]

        def copy(t, k, block, to, src=None):
            px, py, hh = block
            dst = out_refs[t].at[2 * px + py, hh]
            return pltpu.make_async_remote_copy(
                src_ref=dst if src is None else src, dst_ref=dst,
                send_sem=send_sems.at[6 * t + k], recv_sem=recv_sems.at[6 * t + k], device_id=to, device_id_type=MESH)

        own = [pltpu.make_async_remote_copy(
            src_ref=w_refs[t], dst_ref=out_refs[t].at[2 * x + y], send_sem=own_send.at[t], recv_sem=own_recv.at[t],
            device_id=sibling, device_id_type=MESH) for t in range(nt)]
        first = [copy(t, j, (x, y, c), (*chip, c), src=w_refs[t].at[c]) for t in range(nt) for j, chip in enumerate(chips)]
        if not finishing:
            return own, first
        landed = [copy(t, j, (*chip, c), (x, y, c)) for t in range(nt) for j, chip in enumerate(chips)]
        passed = [copy(t, 3 + j, (*chip, c), sibling) for t in range(nt) for j, chip in enumerate(chips)]
        from_sibling = [copy(t, 3 + j, (*chip, 1 - c), (x, y, c)) for t in range(nt) for j, chip in enumerate(chips)]
        return own, first, landed, passed, from_sibling

    def start(w_refs, out_refs, sems):
        own, first = parts(w_refs, out_refs, sems, False)
        for cp in own + first:
            cp.start()

    def finish(w_refs, out_refs, sems):
        own, first, landed, passed, from_sibling = parts(w_refs, out_refs, sems, True)
        for arrived, fwd in zip(landed, passed):
            arrived.wait_recv()
            fwd.start()
        for cp in from_sibling:
            cp.wait_recv()
        for cp in first + passed:
            cp.wait_send()
        for cp in own:
            cp.wait()

    sems = [pltpu.SemaphoreType.DMA((6 * nt,)), pltpu.SemaphoreType.DMA((6 * nt,)),
            pltpu.SemaphoreType.DMA((nt,)), pltpu.SemaphoreType.DMA((nt,))]
    return _Hosted(list(shards), [_sds((N_CHIPS, *w.shape), w.dtype) for w in shards], sems, start, finish)


def _row_block(r, n, itemsize):
    best = None
    for br in range(16, r + 1, 16):
        if r % br == 0 and br * n * itemsize <= COMM_BLOCK_BYTES:
            best = br
    return r if best is None else best


def _scatter_comm(parts):
    nt = len(parts)

    def copies(p_refs, b_refs, sems, arriving):
        send_sems, recv_sems = sems
        x, y, c = _place()
        me = 4 * x + 2 * y + c
        cps = []
        for t in range(nt):
            for r in range(1, 8):
                tx = 1 - x if r & 4 else x
                ty = 1 - y if r & 2 else y
                tc = 1 - c if r & 1 else c
                src, dst = (2 * x + y, c), 4 * tx + 2 * ty + tc
                if not arriving:
                    src, dst = (2 * tx + ty, tc), me
                cps.append(pltpu.make_async_remote_copy(
                    src_ref=p_refs[t].at[src], dst_ref=b_refs[t].at[dst], send_sem=send_sems.at[7 * t + r - 1],
                    recv_sem=recv_sems.at[7 * t + r - 1], device_id=(tx, ty, tc), device_id_type=MESH))
        return cps

    def start(p_refs, b_refs, sems):
        for cp in copies(p_refs, b_refs, sems, False):
            cp.start()

    def finish(p_refs, b_refs, sems):
        for cp in copies(p_refs, b_refs, sems, True):
            cp.wait_recv()
        for cp in copies(p_refs, b_refs, sems, False):
            cp.wait_send()

    sems = [pltpu.SemaphoreType.DMA((7 * nt,)), pltpu.SemaphoreType.DMA((7 * nt,))]
    return _Hosted(list(parts), [_sds((2 * N_CHIPS, *p.shape[2:]), p.dtype) for p in parts], sems, start, finish)


def sum_devices(own, recv, place, name, slot=(0, 1, None)):
    _, _, r, n = own.shape
    layer, n_layers, buf = slot
    br = _row_block(r, n, 4 * 8)

    def body(p_ref, o_ref, *rest):
        acc = o_ref[...]
        for kk in range(7):
            acc = acc + rest[kk][...].astype(F32)
        rest[-1][...] = acc

    def arrived(rel):
        return pl.BlockSpec((None, br, n), lambda i, pref: (jnp.bitwise_xor(pref[0], rel), i, 0))

    in_specs = [pl.BlockSpec((None, None, br, n), lambda i, pref: (pref[2], pref[1], i, 0))]
    in_specs += [arrived(rel) for rel in range(1, 8)]
    args = [own] + [recv] * 7
    aliases = {}
    if buf is not None:
        in_specs.append(pl.BlockSpec(memory_space=pl.ANY))
        args.append(buf)
        aliases = {9: 0}
    return pl.pallas_call(
        body, name=name,
        grid_spec=pltpu.PrefetchScalarGridSpec(
            num_scalar_prefetch=1, grid=(r // br,), in_specs=in_specs,
            out_specs=pl.BlockSpec((None, None, br, n), lambda i, pref: (layer, pref[1], i, 0))),
        out_shape=_sds((n_layers, 2, r, n), F32), input_output_aliases=aliases, compiler_params=_cp(1),
    )(place, *args)


def _join_comm(bufs):
    nt = len(bufs)
    layers = [bf.shape[0] for bf in bufs]
    first = [sum(layers[:t]) for t in range(nt)]

    def copies(o_refs, sems, own):
        send_sems, recv_sems = sems
        x, y, c = _place()
        hh = c if own else 1 - c
        return [pltpu.make_async_remote_copy(
            src_ref=o_refs[t].at[l, hh], dst_ref=o_refs[t].at[l, hh], send_sem=send_sems.at[first[t] + l],
            recv_sem=recv_sems.at[first[t] + l], device_id=(x, y, 1 - c), device_id_type=MESH)
            for t in range(nt) for l in range(layers[t])]

    def start(_, o_refs, sems):
        for cp in copies(o_refs, sems, True):
            cp.start()

    def finish(_, o_refs, sems):
        for cp in copies(o_refs, sems, False):
            cp.wait_recv()
        for cp in copies(o_refs, sems, True):
            cp.wait_send()

    sems = [pltpu.SemaphoreType.DMA((sum(layers),)), pltpu.SemaphoreType.DMA((sum(layers),))]
    return _Hosted(list(bufs), [_sds(bf.shape, bf.dtype) for bf in bufs], sems, start, finish, in_place=True)


def sibling_join_halves(bufs, name):
    comm = _join_comm(bufs)
    nt = len(bufs)

    def body(*refs):
        comm.start(refs[:nt], refs[nt:2 * nt], refs[2 * nt:])
        comm.finish(refs[:nt], refs[nt:2 * nt], refs[2 * nt:])

    hbm = pl.BlockSpec(memory_space=pl.ANY)
    return pl.pallas_call(body, name=name, out_shape=comm.out_shape, in_specs=[hbm] * nt, out_specs=[hbm] * nt,
                          input_output_aliases={k: k for k in range(nt)}, scratch_shapes=comm.sems)(*bufs)


_SHARD_KIND = {"mla_w_in": "rows", "mla_w_uq": "cols", "mla_w_uk": "cols", "mla_w_uv": "cols", "mla_w_o": "rows",
               "fox_w_in": "cols", "fox_w_o": "rows", "ffn_w_gate": "chunk", "ffn_w_up": "chunk", "ffn_w_down": "chunk"}
_PACKED = tuple(_SHARD_KIND)
_TRANSPOSED = ("ffn_w_gate", "ffn_w_up", "fox_w_in")


def _halves(shard):
    if shard.ndim == 3 and shard.shape[0] == 2:
        return shard
    r, n = shard.shape[-2:]
    return shard.reshape(2, r // 2, n)


def _cols_to_full(g):
    return jnp.transpose(g, (1, 0, 2)).reshape(g.shape[1], -1)


def _full_to_cols(w):
    k, n4 = w.shape
    return jnp.transpose(w.reshape(k, N_CHIPS, n4 // N_CHIPS), (1, 0, 2))


def _uq_perm():
    per = MLA_NOPE + MLA_ROPE
    half = MLA_ROPE // 2
    nope = [h * per + d for h in range(MLA_HEADS) for d in range(MLA_NOPE)]
    r1 = [h * per + MLA_NOPE + r for h in range(MLA_HEADS) for r in range(half)]
    r2 = [h * per + MLA_NOPE + half + r for h in range(MLA_HEADS) for r in range(half)]
    perm = np.array(nope + r1 + r2, dtype=np.int32)
    return perm, np.argsort(perm).astype(np.int32)


def _rope_matrices():
    half = MLA_ROPE // 2
    nr = MLA_HEADS * MLA_ROPE
    to_heads = np.zeros((nr, nr), np.float32)
    from_heads = np.zeros((MLA_HEADS * 128, nr), np.float32)
    for e in range(2):
        for h in range(MLA_HEADS):
            for r in range(half):
                to_heads[e * MLA_HEADS * half + h * half + r, h * MLA_ROPE + e * half + r] = 1.0
                from_heads[h * 128 + e * half + r, e * MLA_HEADS * half + h * half + r] = 1.0
    head_sum = np.tile(np.eye(MLA_ROPE, dtype=np.float32), (2 * MLA_HEADS, 1))
    dup = np.concatenate([np.eye(MLA_ROPE, dtype=np.float32)] * 2, axis=1)
    return to_heads, from_heads, head_sum, dup


def _ffn_weights(gathered):
    return tuple(g.reshape(N_CHIPS, 2 * g.shape[2], g.shape[3]) for g in gathered)


def _fox_weights(gathered):
    w_in, w_o = gathered
    w_in = jnp.transpose(w_in, (0, 2, 1, 3)).reshape(N_CHIPS * w_in.shape[2], 2 * w_in.shape[3])
    return w_in, w_o.reshape(-1, w_o.shape[-1])


def _local_step(x, positions, target, mods, wts, ln_g, ln_b, mla_g_q, mla_g_kv, fox_b_f, shards=None):
    nb, s, d = x.shape
    t = nb * s
    x0 = x.reshape(t, d)
    tgt = target.reshape(t, d)
    perm, inv_perm = _uq_perm()

    half = MLA_ROPE // 2
    inv_freq = ROPE_THETA ** (-jnp.arange(half, dtype=F32) / half)
    ang = positions.astype(F32).reshape(t, 1) * inv_freq
    cos, sin = jnp.cos(ang), jnp.sin(ang)
    cos8, sin8 = jnp.tile(cos, (1, MLA_HEADS)), jnp.tile(sin, (1, MLA_HEADS))
    cos64 = jnp.concatenate([cos, cos], axis=1)
    sin64s = jnp.concatenate([-sin, sin], axis=1)
    swap64 = jnp.asarray(np.roll(np.eye(MLA_ROPE, dtype=np.float32), half, axis=1))
    to_heads, from_heads, head_sum, dup = _rope_matrices()
    to_heads, from_heads = jnp.asarray(to_heads, dtype=BF16), jnp.asarray(from_heads, dtype=BF16)
    head_sum, dup = jnp.asarray(head_sum, dtype=BF16), jnp.asarray(dup, dtype=BF16)
    sel_mla = jnp.asarray(np.pad(np.kron(np.eye(MLA_HEADS, dtype=np.float32), np.ones((MLA_V, 1), np.float32)),
                                 ((0, 0), (0, 128 - MLA_HEADS))))
    sel_fox = jnp.asarray(np.pad(np.kron(np.eye(FOX_HEADS, dtype=np.float32), np.ones((FOX_HD, 1), np.float32)),
                                 ((0, 0), (0, 128 - FOX_HEADS))))
    tri = jnp.asarray(np.tril(np.ones((GATE_BLOCK, GATE_BLOCK), np.float32)))
    triu = jnp.asarray(np.triu(np.ones((GATE_BLOCK, GATE_BLOCK), np.float32)))
    onehot16 = jnp.asarray(np.eye(16, 128, dtype=np.float32))

    def vec(a):
        return a.reshape(1, -1)

    def carried(key):
        return None if shards is None else _gather_comm(shards[key])

    def split(res):
        return (res, None) if shards is None else res

    w_uq_p = wts["mla_w_uq"][:, perm]
    b_f_pad = jnp.pad(fox_b_f.reshape(1, -1), ((0, 0), (0, 128 - FOX_HEADS)))

    sh_a, sc_a, gt_a, sh_f, sc_f, gt_f = mods[0]
    h_in, u_m = mod_linear(x0, sh_a, sc_a, wts["mla_w_in"], F32, "mla_in", emit_u=True)
    q_m, kn_m, v_m, kr2_m, cq_m, ckv_m = mla_mid_fwd(
        h_in, vec(mla_g_q), vec(mla_g_kv), w_uq_p, wts["mla_w_uk"], wts["mla_w_uv"], cos8, sin8, cos64, sin64s, swap64,
        to_heads, dup, "mla_mid")
    (o_m, lse_m), got = split(mla_attn_fwd(q_m, kn_m, kr2_m, v_m, nb, "mla_attn", hosted=carried("ffn0")))
    ffn0_w = wts["ffn"][0] if got is None else _ffn_weights(got)
    y0, x1 = linear_resid_ln(o_m, wts["mla_w_o"], x0, gt_a, vec(ln_g[0, 0]), vec(ln_b[0, 0]), "mla_out")
    (u_f0, hg0, hu0, y1, x2), got = split(ffn_fwd(x1, sh_f, sc_f, gt_f, *ffn0_w, vec(ln_g[0, 1]), vec(ln_b[0, 1]), "ffn0",
                                                  hosted=carried("fox")))
    fox_w_in_t, fox_w_o = (wts["fox_w_in"].T, wts["fox_w_o"]) if got is None else _fox_weights(got)
    fox_w_f_t = jnp.pad(fox_w_in_t[3 * d:], ((0, 128 - FOX_HEADS), (0, 0)))
    sh_a1, sc_a1, gt_a1, sh_f1, sc_f1, gt_f1 = mods[1]
    qkv, u_x = mod_linear(x2, sh_a1, sc_a1, fox_w_in_t, BF16, "fox_qkv", tn=1024, emit_u=True, w_rows=3 * d)
    hf = mod_linear(x2, sh_a1, sc_a1, fox_w_f_t, F32, "fox_f", w_rows=128)
    cum = fox_gate_fwd(hf, b_f_pad, tri, nb, "fox_gate")
    cum_rows = rows16(cum, "fox_cum_rows")
    (o_x, lse_x), got = split(fox_attn_fwd(qkv, cum, cum_rows, nb, "fox_attn", hosted=carried("ffn1")))
    ffn1_w = wts["ffn"][1] if got is None else _ffn_weights(got)
    y2, x3 = linear_resid_ln(o_x, fox_w_o, x2, gt_a1, vec(ln_g[1, 0]), vec(ln_b[1, 0]), "fox_out")
    u_f1, hg1, hu1, y3, x4 = ffn_fwd(x3, sh_f1, sc_f1, gt_f1, *ffn1_w, vec(ln_g[1, 1]), vec(ln_b[1, 1]), "ffn1")

    parts, recv = {}, {}

    def halves_of(g):
        return g.reshape(N_CHIPS, 2, g.shape[1] // 2, g.shape[2])

    def scatter(keys, sent):
        return None if shards is None else _scatter_comm([sent[k] for k in keys])

    def landed(keys, got):
        if got is not None:
            recv.update(zip(keys, got))

    def ffn_grads(layer, u, dhg, dhu, act, dy):
        sent = {}
        for n, (a_op, b_op) in (("ffn_w_gate", (dhg, u[None])), ("ffn_w_up", (dhu, u[None])), ("ffn_w_down", (act, dy[None]))):
            g32, g16 = wgrad(a_op, b_op, "ffn%d_d%s" % (layer, n[4:]), with_bf16=True)
            parts["%s/%d" % (n, layer)], sent["%s/%d" % (n, layer)] = halves_of(g32), halves_of(g16)
        return sent

    dz3, dy3, dg11, db11, dgt_f1, sq_err = ln_bwd(x4, x3, y3, gt_f1, vec(ln_g[1, 1]), "ffn1_ln_bwd", target=tgt)
    loss_part = 0.5 * jnp.sum(sq_err) / d
    dhg1, dhu1, act1, dx3, dsc_f1, dsh_f1 = ffn_bwd(dy3, hg1, hu1, *ffn1_w, dz3, x3, sc_f1, "ffn1_bwd")
    sent = ffn_grads(1, u_f1, dhg1, dhu1, act1, dy3)
    dz2, dy2, dg10, db10, dgt_a1 = ln_bwd(dx3, x2, y2, gt_a1, vec(ln_g[1, 0]), "fox_ln_bwd")
    do_x, delta_x = linear_nt_delta(dy2, fox_w_o, o_x, sel_fox, "fox_out_bwd")
    (dq_x, dk_x, dv_x, dfq_x, dfk_x), got = split(fox_attn_bwd(
        qkv, do_x, cum, cum_rows, rows16(lse_x, "fox_lse_rows"), rows16(delta_x, "fox_delta_rows"), nb, "fox_attn_bwd",
        hosted=scatter(list(sent), sent)))
    landed(list(sent), got)
    dcum = tokens128(dfq_x + dfk_x, onehot16, "fox_dcum")
    dhf, dbf = fox_gate_bwd(dcum, hf, b_f_pad, triu, nb, "fox_gate_bwd")
    fox_d = [("q", dq_x), ("k", dk_x), ("v", dv_x)]
    dx2, dsc_a1, dsh_a1 = linear_nt_mod_bwd(
        [(dh, fox_w_in_t, i) for i, (_, dh) in enumerate(fox_d)] + [(dhf, fox_w_f_t, 0)], dz2, x2, sc_a1, "fox_in_bwd")
    dw_in_t = [wgrad(dh[None], u_x[None], "fox_dw" + tag)[0] for tag, dh in fox_d]
    dw_in_t.append(wgrad(dhf[None], u_x[None], "fox_dwf")[0][:FOX_HEADS])
    dw_in_t = jnp.concatenate(dw_in_t, axis=0).reshape(N_CHIPS, -1, 2, d // 2)
    parts["fox_w_in"] = jnp.transpose(dw_in_t, (0, 2, 1, 3))
    parts["fox_w_o"] = wgrad(o_x[None], dy2[None], "fox_dwo")[0].reshape(N_CHIPS, 2, -1, d)
    sent = {k: parts[k].astype(BF16) for k in ("fox_w_in", "fox_w_o")}
    dz1, dy1, dg01, db01, dgt_f0 = ln_bwd(dx2, x1, y1, gt_f, vec(ln_g[0, 1]), "ffn0_ln_bwd")
    (dhg0, dhu0, act0, dx1, dsc_f0, dsh_f0), got = split(ffn_bwd(dy1, hg0, hu0, *ffn0_w, dz1, x1, sc_f, "ffn0_bwd",
                                                                 hosted=scatter(list(sent), sent)))
    landed(list(sent), got)
    sent = ffn_grads(0, u_f0, dhg0, dhu0, act0, dy1)
    dz0, dy0, dg00, db00, dgt_a0 = ln_bwd(dx1, x0, y0, gt_a, vec(ln_g[0, 0]), "mla_ln_bwd")
    do_m, delta_m = linear_nt_delta(dy0, wts["mla_w_o"], o_m, sel_mla, "mla_out_bwd")
    parts["mla_w_o"] = wgrad(o_m[None], dy0[None], "mla_dwo")[0].reshape(N_CHIPS, 2, -1, d)
    (dqn_m, dqr_m, dkn_m, dkr_m, dv_m), got = split(mla_attn_bwd(
        q_m, kn_m, kr2_m, v_m, do_m, rows16(lse_m, "mla_lse_rows"), rows16(delta_m, "mla_delta_rows"), nb,
        "mla_attn_bwd", hosted=scatter(list(sent), sent)))
    landed(list(sent), got)
    sent = {"mla_w_o": parts["mla_w_o"].astype(BF16)}
    (dh_in, dq_pre, dgq, dgkv), got = split(mla_mid_bwd(
        dqn_m, dqr_m, dkn_m, dv_m, dkr_m, h_in, vec(mla_g_q), vec(mla_g_kv), w_uq_p, wts["mla_w_uk"],
        wts["mla_w_uv"], cos8, sin8, cos64, sin64s, swap64, from_heads, head_sum, "mla_mid_bwd",
        hosted=scatter(list(sent), sent)))
    landed(list(sent), got)
    parts["mla_w_uq"] = halves_of(_full_to_cols(wgrad(cq_m[None], dq_pre[None], "mla_dwuq")[0][:, inv_perm]))
    parts["mla_w_uk"] = halves_of(_full_to_cols(wgrad(ckv_m[None], dkn_m[None], "mla_dwuk")[0]))
    parts["mla_w_uv"] = halves_of(_full_to_cols(wgrad(ckv_m[None], dv_m[None], "mla_dwuv")[0]))
    parts["mla_w_in"] = wgrad(u_m[None], dh_in[None], "mla_dwin")[0].reshape(N_CHIPS, 2, -1, h_in.shape[1])
    sent = {k: parts[k].astype(BF16) for k in ("mla_w_in", "mla_w_uq", "mla_w_uk", "mla_w_uv")}
    (dx0, dsc_a0, dsh_a0), got = split(linear_nt_mod_bwd([(dh_in, wts["mla_w_in"], None)], dz0, x0, sc_a, "mla_in_bwd",
                                                         hosted=scatter(list(sent), sent)))
    landed(list(sent), got)

    dmods = [(dsh_a0, dsc_a0, dgt_a0, dsh_f0, dsc_f0, dgt_f0), (dsh_a1, dsc_a1, dgt_a1, dsh_f1, dsc_f1, dgt_f1)]
    d_ln_g = jnp.stack([jnp.concatenate([dg00, dg01], axis=0), jnp.concatenate([dg10, dg11], axis=0)])
    d_ln_b = jnp.stack([jnp.concatenate([db00, db01], axis=0), jnp.concatenate([db10, db11], axis=0)])
    return loss_part, dx0.reshape(nb, s, d), (parts, recv), dmods, d_ln_g, d_ln_b, dgq, dgkv, dbf[:, :FOX_HEADS]


def _pad_rows(a, rows):
    return jnp.pad(a, ((0, rows - a.shape[0]), (0, 0)))


def kernel(x, c, positions, mla_w_in, mla_g_q, mla_w_uq, mla_g_kv, mla_w_uk, mla_w_uv, mla_w_o, fox_w_in, fox_b_f, fox_w_o, ada_w, ada_b, ffn_w_gate, ffn_w_up, ffn_w_down, ln_g, ln_b, loss_target, m_mla_w_in, m_mla_g_q, m_mla_w_uq, m_mla_g_kv, m_mla_w_uk, m_mla_w_uv, m_mla_w_o, m_fox_w_in, m_fox_b_f, m_fox_w_o, m_ada_w, m_ada_b, m_ffn_w_gate, m_ffn_w_up, m_ffn_w_down, m_ln_g, m_ln_b, v_mla_w_in, v_mla_g_q, v_mla_w_uq, v_mla_g_kv, v_mla_w_uk, v_mla_w_uv, v_mla_w_o, v_fox_w_in, v_fox_b_f, v_fox_w_o, v_ada_w, v_ada_b, v_ffn_w_gate, v_ffn_w_up, v_ffn_w_down, v_ln_g, v_ln_b):
    args = dict(locals())
    nb, s, d = x.shape
    ax, ay, ac = lax.axis_index("x"), lax.axis_index("y"), lax.axis_index("c")
    chip = 2 * ax + ay
    dev = 2 * chip + ac
    n_dev = 2 * N_CHIPS
    n_all = nb * n_dev

    shard_shapes = {n: (args[n].shape if _SHARD_KIND[n] == "chunk" else args[n].shape[1:]) for n in _PACKED}

    def block(n, layer=None):
        w = args[n].reshape(shard_shapes[n]) if layer is None else args[n][layer]
        return _halves(w.astype(BF16))

    mla_names = [n for n in _PACKED if n.startswith("mla")]
    ffn_names = ("ffn_w_gate", "ffn_w_up", "ffn_w_down")
    fox_in_t = jnp.swapaxes(fox_w_in, 1, 2)[0].astype(BF16)
    fox_in_t = jnp.stack([fox_in_t[:, :d // 2], fox_in_t[:, d // 2:]])
    shards = {"ffn0": [block(n, 0) for n in ffn_names], "fox": [fox_in_t, block("fox_w_o")],
              "ffn1": [block(n, 1) for n in ffn_names]}

    ln_cols = ln_g.shape[-1]
    ln_blk = jnp.concatenate([ln_g.reshape(2 * DEPTH, ln_cols), ln_b.reshape(2 * DEPTH, ln_cols)], axis=0)
    early = jnp.concatenate([_pad_rows(c, 8), jnp.pad(_pad_rows(ln_blk, 8), ((0, 0), (0, d - ln_cols)))], axis=0)
    early, mla_all = all_gather8(early, "gather_c_ln_mla", hosted=_gather_comm([block(n) for n in mla_names]))
    wts = {}
    for n, g in zip(mla_names, mla_all):
        g = g.reshape(N_CHIPS, *shard_shapes[n])
        wts[n] = g.reshape(-1, g.shape[-1]) if _SHARD_KIND[n] == "rows" else _cols_to_full(g)
    early = early.reshape(n_dev, 16, d)
    c_all = early[:, :nb].reshape(n_all, d)
    ln_all = early.reshape(N_CHIPS, 2, 16, d)[:, 0, 8:8 + 4 * DEPTH, :ln_cols]
    ln_all = jnp.transpose(ln_all, (1, 0, 2)).reshape(4 * DEPTH, d)
    ln_g_full = ln_all[:2 * DEPTH].reshape(DEPTH, 2, d)
    ln_b_full = ln_all[2 * DEPTH:].reshape(DEPTH, 2, d)
    mod_part = ada_mod_part(c_all, ada_w, "ada_mod")
    ncol = mod_part.shape[-1]
    mod_g = all_gather8(mod_part.reshape(DEPTH * n_all, ncol), "gather_mod")
    mod_g = mod_g.reshape(N_CHIPS, 2, DEPTH, n_all, ncol)[:, 0]
    mod_full = jnp.transpose(mod_g, (1, 2, 0, 3)).reshape(DEPTH, n_all, N_CHIPS * ncol) + ada_b[:, None, :]
    mod_loc = lax.dynamic_slice_in_dim(mod_full, dev * nb, nb, axis=1)
    mods = [tuple(mod_loc[i, :, k * d:(k + 1) * d].reshape(nb, 1, d) for k in range(6)) for i in range(DEPTH)]

    loss_part, grad_x, (parts, recv), dmods, d_ln_g, d_ln_b, dgq, dgkv, dbf = _local_step(
        x, positions, loss_target, mods, wts, ln_g_full, ln_b_full, mla_g_q[0], mla_g_kv[0], fox_b_f[0], shards)
    loss = lax.psum(loss_part, ("x", "y", "c"))

    dmod_rows = jnp.stack([jnp.concatenate([v_.reshape(nb, d) for v_ in dm], axis=1) for dm in dmods])
    small = jnp.concatenate([
        d_ln_g.reshape(2 * DEPTH, d), d_ln_b.reshape(2 * DEPTH, d),
        jnp.pad(jnp.concatenate([dgq, dgkv, dbf], axis=1), ((0, 0), (0, d - 2 * MLA_QR - FOX_HEADS))),
        dmod_rows.reshape(DEPTH * nb * 6, d)], axis=0)
    n_small = small.shape[0]
    small_rows = -(-n_small // 8) * 8
    small_all = all_gather8(_pad_rows(small, small_rows), "gather_stats").reshape(n_dev, small_rows, d)
    stat_sum = sum_leading(small_all, "sum_stats")
    g_ln_g = lax.dynamic_slice_in_dim(stat_sum[:2 * DEPTH], chip * ln_cols, ln_cols, axis=1).reshape(DEPTH, 2, ln_cols)
    g_ln_b = lax.dynamic_slice_in_dim(stat_sum[2 * DEPTH:4 * DEPTH], chip * ln_cols, ln_cols, axis=1).reshape(DEPTH, 2, ln_cols)
    row = stat_sum[4 * DEPTH]
    g_gq = row[:MLA_QR].reshape(1, MLA_QR)
    g_gkv = row[MLA_QR:2 * MLA_QR].reshape(1, MLA_KVR)
    g_bf = row[2 * MLA_QR:2 * MLA_QR + FOX_HEADS].reshape(1, FOX_HEADS)
    base = 4 * DEPTH + 1
    dmod_all = small_all[:, base:base + DEPTH * nb * 6].reshape(n_dev, DEPTH, nb, 6 * d)
    dmod_all = jnp.transpose(dmod_all, (1, 0, 2, 3)).reshape(DEPTH, n_all, 6 * d)
    g_ada_b = sum_leading(jnp.transpose(dmod_all, (1, 0, 2)), "sum_ada_b")
    dmod_mine = lax.dynamic_slice_in_dim(dmod_all, chip * ncol, ncol, axis=2)
    g_ada_w = ada_grad(c_all.T, dmod_mine, "ada_grad")

    place = jnp.stack([dev, ac, chip]).astype(jnp.int32)
    bufs = []
    for n in _PACKED:
        if _SHARD_KIND[n] == "chunk":
            buf = None
            for layer in range(DEPTH):
                key = "%s/%d" % (n, layer)
                buf = sum_devices(parts[key], recv[key], place, "rs_sum_%s%d" % (n, layer), slot=(layer, DEPTH, buf))
        else:
            buf = sum_devices(parts[n], recv[n], place, "rs_sum_" + n)
        bufs.append(buf)
    joined = sibling_join_halves(bufs, "rs_join")
    g_big = {n: j.reshape(j.shape[0], 2 * j.shape[2], j.shape[3]) for n, j in zip(_PACKED, joined)}
    j = joined[_PACKED.index("fox_w_in")]
    g_big["fox_w_in"] = jnp.transpose(j, (0, 2, 1, 3)).reshape(1, j.shape[2], 2 * j.shape[3])

    g_out = {
        "mla_w_in": g_big["mla_w_in"], "mla_g_q": g_gq, "mla_w_uq": g_big["mla_w_uq"], "mla_g_kv": g_gkv,
        "mla_w_uk": g_big["mla_w_uk"], "mla_w_uv": g_big["mla_w_uv"], "mla_w_o": g_big["mla_w_o"],
        "fox_w_in": g_big["fox_w_in"], "fox_b_f": g_bf, "fox_w_o": g_big["fox_w_o"],
        "ada_w": g_ada_w, "ada_b": g_ada_b, "ffn_w_gate": g_big["ffn_w_gate"], "ffn_w_up": g_big["ffn_w_up"],
        "ffn_w_down": g_big["ffn_w_down"], "ln_g": g_ln_g, "ln_b": g_ln_b}
    names = ["mla_w_in", "mla_g_q", "mla_w_uq", "mla_g_kv", "mla_w_uk", "mla_w_uv", "mla_w_o", "fox_w_in", "fox_b_f",
             "fox_w_o", "ada_w", "ada_b", "ffn_w_gate", "ffn_w_up", "ffn_w_down", "ln_g", "ln_b"]
    small_names = ["mla_g_q", "mla_g_kv", "fox_b_f", "ada_b", "ln_g", "ln_b"]
    deltas, new_m, new_v = {}, {}, {}
    for n in names:
        if n in small_names:
            continue
        shp = args[n].shape
        if n in _TRANSPOSED:
            view = lambda a: jnp.swapaxes(a, 1, 2).reshape(-1, shp[1])
            back = lambda a: jnp.swapaxes(a.reshape(shp[0], shp[2], shp[1]), 1, 2)
        else:
            view = lambda a: a.reshape(-1, shp[-1])
            back = lambda a: a.reshape(shp)
        dl, mn, vn = adamw(view(args[n]), g_out[n].reshape(view(args[n]).shape), view(args["m_" + n]),
                           view(args["v_" + n]), "adamw_" + n)
        g_out[n], deltas[n], new_m[n], new_v[n] = back(g_out[n].reshape(view(args[n]).shape)), back(dl), back(mn), back(vn)

    def small_pack(prefix, src):
        flat = jnp.concatenate([src[prefix + n].reshape(-1) for n in small_names])
        size = -(-flat.shape[0] // (8 * 128)) * 8 * 128
        return jnp.pad(flat, (0, size - flat.shape[0])).reshape(-1, 128)

    sd, sm, sv = adamw(small_pack("", args), small_pack("", g_out), small_pack("m_", args), small_pack("v_", args),
                       "adamw_small")
    off = 0
    for n in small_names:
        shp = args[n].shape
        size = math.prod(shp)
        deltas[n] = sd.reshape(-1)[off:off + size].reshape(shp)
        new_m[n] = sm.reshape(-1)[off:off + size].reshape(shp)
        new_v[n] = sv.reshape(-1)[off:off + size].reshape(shp)
        off += size

    outs = [loss, grad_x]
    outs += [g_out[n].reshape(args[n].shape) for n in names]
    outs += [deltas[n] for n in names] + [new_m[n] for n in names] + [new_v[n] for n in names]
    return tuple(outs)
```

```python
import functools
import math

import numpy as np
import jax
import jax.numpy as jnp
from jax import lax
from jax.experimental import pallas as pl
from jax.experimental.pallas import tpu as pltpu

F32 = jnp.float32
BF16 = jnp.bfloat16
MESH = pl.DeviceIdType.MESH

D_MODEL = 1024
DEPTH = 2
MLA_HEADS = 8
MLA_NOPE = 128
MLA_ROPE = 64
MLA_V = 128
MLA_QR = 256
MLA_KVR = 256
ROPE_THETA = 10000.0
FOX_HEADS = 16
FOX_HD = 64
D_FF = 2816
N_CHIPS = 4
FF_CHUNK = D_FF // N_CHIPS
ALPHA = (2.0 * DEPTH) ** 0.25
EPS = 1e-5
ADAM_LR = 0.001
ADAM_B1 = 0.9
ADAM_B2 = 0.999
ADAM_EPS = 1e-08
ADAM_WD = 0.01
ADAM_STEP = 10

VMEM_LIMIT = 56 * 1024 * 1024
TOKEN_TILE = 512
WGRAD_TOKENS = 2048
ATTN_TILE = 512
GATE_BLOCK = 512
FOX_GROUP = 8
MLA_GROUP = 4
COMM_BLOCK_BYTES = 8 * 1024 * 1024
ADAMW_BLOCK_BYTES = 2 * 1024 * 1024


def _cp(n_axes):
    return pltpu.CompilerParams(dimension_semantics=("arbitrary",) * n_axes, vmem_limit_bytes=VMEM_LIMIT)


def _dot(a, b):
    return jnp.dot(a, b, preferred_element_type=F32)


def _dot_nt(a, b):
    return lax.dot_general(a, b, (((1,), (1,)), ((), ())), preferred_element_type=F32)


def _dot_tn(a, b):
    return lax.dot_general(a, b, (((0,), (0,)), ((), ())), preferred_element_type=F32)


def _dot_f32(a, b):
    return jnp.dot(a, b, preferred_element_type=F32, precision=lax.Precision.HIGHEST)


def _sds(shape, dtype):
    return jax.ShapeDtypeStruct(shape, dtype)


def _place():
    return lax.axis_index("x"), lax.axis_index("y"), lax.axis_index("c")


class _Hosted:
    def __init__(self, inputs, out_shape, sems, start, finish, in_place=False):
        self.inputs, self.out_shape, self.sems, self.start, self.finish = inputs, out_shape, sems, start, finish
        self.in_place = in_place


def _call(body, name, grid, in_specs, out_specs, out_shape, args, scratch_shapes=(), hosted=None):
    in_specs, out_specs, out_shape, scratch_shapes = list(in_specs), list(out_specs), list(out_shape), list(scratch_shapes)
    if hosted is None:
        return pl.pallas_call(body, name=name, grid=grid, in_specs=in_specs, out_specs=out_specs, out_shape=out_shape,
                              scratch_shapes=scratch_shapes, compiler_params=_cp(len(grid)))(*args)
    n_in, n_out, n_scr = len(in_specs), len(out_specs), len(scratch_shapes)
    h_in, h_out = len(hosted.inputs), len(hosted.out_shape)

    def carried(*refs):
        o0 = n_in + h_in
        s0 = o0 + n_out + h_out
        c_in, c_out, c_sem = refs[n_in:o0], refs[o0 + n_out:s0], refs[s0 + n_scr:]
        ids = [pl.program_id(a) for a in range(len(grid))]
        first = functools.reduce(jnp.logical_and, [i == 0 for i in ids])
        last = functools.reduce(jnp.logical_and, [i == g - 1 for i, g in zip(ids, grid)])

        @pl.when(first)
        def _():
            hosted.start(c_in, c_out, c_sem)

        body(*refs[:n_in], *refs[o0:o0 + n_out], *refs[s0:s0 + n_scr])

        @pl.when(last)
        def _():
            hosted.finish(c_in, c_out, c_sem)

    hbm = pl.BlockSpec(memory_space=pl.ANY)
    aliases = {n_in + k: n_out + k for k in range(h_in)} if hosted.in_place else {}
    res = pl.pallas_call(
        carried, name=name, grid=grid, in_specs=in_specs + [hbm] * h_in, out_specs=out_specs + [hbm] * h_out,
        out_shape=out_shape + list(hosted.out_shape), scratch_shapes=scratch_shapes + list(hosted.sems),
        input_output_aliases=aliases, compiler_params=_cp(len(grid)))(*args, *hosted.inputs)
    return res[:n_out], res[n_out:]


def mod_linear(x, shift, scale, w, out_dtype, name, tn=None, emit_u=False, w_rows=None):
    t, d = x.shape
    n = w.shape[1] if w_rows is None else w_rows
    tn = n if tn is None else tn
    tm = TOKEN_TILE
    tps = (t // shift.shape[0]) // tm

    def body(x_ref, sh_ref, sc_ref, w_ref, o_ref, *rest):
        u = (x_ref[...] * (1.0 + sc_ref[...]) + sh_ref[...]).astype(BF16)
        o_ref[...] = (_dot(u, w_ref[...]) if w_rows is None else _dot_nt(u, w_ref[...])).astype(out_dtype)
        if emit_u:
            @pl.when(pl.program_id(1) == 0)
            def _():
                rest[0][...] = u

    vec = pl.BlockSpec((None, 1, d), lambda i, j: (i // tps, 0, 0))
    out_shape = [_sds((t, n), out_dtype)]
    out_specs = [pl.BlockSpec((tm, tn), lambda i, j: (i, j))]
    if emit_u:
        out_shape.append(_sds((t, d), BF16))
        out_specs.append(pl.BlockSpec((tm, d), lambda i, j: (i, 0)))
    w_spec = pl.BlockSpec((d, tn), lambda i, j: (0, j)) if w_rows is None else pl.BlockSpec((tn, d), lambda i, j: (j, 0))
    res = pl.pallas_call(
        body, name=name, grid=(t // tm, n // tn),
        in_specs=[pl.BlockSpec((tm, d), lambda i, j: (i, 0)), vec, vec, w_spec],
        out_specs=out_specs, out_shape=out_shape, compiler_params=_cp(2),
    )(x, shift, scale, w)
    return res if emit_u else res[0]


def _rms(h, g):
    rstd = lax.rsqrt(jnp.mean(h * h, axis=-1, keepdims=True) + EPS)
    return h * rstd, rstd


def mla_mid_fwd(h, g_q, g_kv, w_uq, w_uk, w_uv, cos8, sin8, cos64, sin64s, swap64, rope_to_heads, dup64, name):
    t = h.shape[0]
    tm = TOKEN_TILE
    hq = MLA_HEADS * MLA_NOPE
    hr = MLA_HEADS * MLA_ROPE // 2

    def body(h_ref, gq_ref, gkv_ref, wuq_ref, wuk_ref, wuv_ref, c8_ref, s8_ref, c64_ref, s64_ref, sw_ref, p_ref, d_ref,
             q_ref, kn_ref, v_ref, kr_ref, cq_ref, ckv_ref):
        hh = h_ref[...]
        cq = (_rms(hh[:, :MLA_QR], None)[0] * gq_ref[...]).astype(BF16)
        ckv = (_rms(hh[:, MLA_QR:MLA_QR + MLA_KVR], None)[0] * gkv_ref[...]).astype(BF16)
        cq_ref[...] = cq
        ckv_ref[...] = ckv
        q = _dot(cq, wuq_ref[...])
        x1 = q[:, hq:hq + hr]
        x2 = q[:, hq + hr:]
        cs = c8_ref[...]
        sn = s8_ref[...]
        rot = jnp.concatenate([x1 * cs - x2 * sn, x2 * cs + x1 * sn], axis=1).astype(BF16)
        q_ref[...] = jnp.concatenate([q[:, :hq].astype(BF16), _dot(rot, p_ref[...]).astype(BF16)], axis=1)
        kn_ref[...] = _dot(ckv, wuk_ref[...]).astype(BF16)
        v_ref[...] = _dot(ckv, wuv_ref[...]).astype(BF16)
        kr = hh[:, MLA_QR + MLA_KVR:]
        kr = (kr * c64_ref[...] + _dot_f32(kr, sw_ref[...]) * s64_ref[...]).astype(BF16)
        kr_ref[...] = _dot(kr, d_ref[...]).astype(BF16)

    def rows(n):
        return pl.BlockSpec((tm, n), lambda i: (i, 0))

    def whole(a):
        return pl.BlockSpec(a.shape, lambda i: (0,) * a.ndim)

    nq = w_uq.shape[1]
    return pl.pallas_call(
        body, name=name, grid=(t // tm,),
        in_specs=[rows(h.shape[1]), whole(g_q), whole(g_kv), whole(w_uq), whole(w_uk), whole(w_uv),
                  rows(hr), rows(hr), rows(MLA_ROPE), rows(MLA_ROPE), whole(swap64), whole(rope_to_heads), whole(dup64)],
        out_specs=[rows(nq), rows(hq), rows(hq), rows(2 * MLA_ROPE), rows(MLA_QR), rows(MLA_KVR)],
        out_shape=[_sds((t, nq), BF16), _sds((t, hq), BF16), _sds((t, hq), BF16), _sds((t, 2 * MLA_ROPE), BF16),
                   _sds((t, MLA_QR), BF16), _sds((t, MLA_KVR), BF16)],
        compiler_params=_cp(1),
    )(h, g_q, g_kv, w_uq, w_uk, w_uv, cos8, sin8, cos64, sin64s, swap64, rope_to_heads, dup64)


def _pick_lane(tile, idx):
    lane = lax.broadcasted_iota(jnp.int32, tile.shape, 1)
    return jnp.sum(jnp.where(lane == idx, tile, 0.0), axis=1, keepdims=True)


def _pick_row(tile, idx):
    row = lax.broadcasted_iota(jnp.int32, tile.shape, 0)
    return jnp.sum(jnp.where(row == idx, tile, 0.0), axis=0, keepdims=True)


def _put_lane(tile, idx, col):
    lane = lax.broadcasted_iota(jnp.int32, tile.shape, 1)
    return jnp.where(lane == idx, col, tile)


def _put_row(tile, idx, row):
    r = lax.broadcasted_iota(jnp.int32, tile.shape, 0)
    return tile + jnp.where(r == idx, row, 0.0)


def _causal_softmax_blocks(i, tq, heads):
    def block(j, carry, masked):
        new = []
        for (score_fn, pv_fn, _), (m, l, acc) in zip(heads, carry):
            sc = score_fn(j)
            if masked:
                keep = lax.broadcasted_iota(jnp.int32, sc.shape, 0) >= lax.broadcasted_iota(jnp.int32, sc.shape, 1)
                sc = jnp.where(keep, sc, -1e30)
            m_new = jnp.maximum(m, jnp.max(sc, axis=1, keepdims=True))
            a = jnp.exp(m - m_new)
            p = jnp.exp(sc - m_new)
            new.append((m_new, a * l + jnp.sum(p, axis=1, keepdims=True), a * acc + pv_fn(j, p.astype(BF16))))
        return tuple(new)

    init = tuple((jnp.full((tq, 1), -1e30, F32), jnp.zeros((tq, 1), F32), jnp.zeros((tq, dv), F32)) for _, _, dv in heads)
    carry = lax.fori_loop(0, i, lambda j, c: block(j, c, False), init)
    return [(acc / l, m + jnp.log(l)) for m, l, acc in block(i, carry, True)]


def fox_attn_fwd(qkv, cum, cum_rows, nb, name, hosted=None):
    t = qkv.shape[0]
    s = t // nb
    tq = ATTN_TILE
    nq = s // tq
    wide = FOX_GROUP * FOX_HD
    ngroups = FOX_HEADS // FOX_GROUP
    scale = FOX_HD ** -0.5

    def body(q_ref, k_ref, v_ref, cum_ref, cr_ref, o_ref, lse_ref):
        i = pl.program_id(1)
        hg = pl.program_id(2)

        @pl.when(hg == 0)
        def _():
            lse_ref[...] = jnp.zeros_like(lse_ref)

        low = lax.broadcasted_iota(jnp.int32, (tq, 128), 1) < FOX_HD
        cum_t = cum_ref[...]

        def rows_of(j):
            return pl.ds(pl.multiple_of(j * tq, tq), tq)

        def head(a):
            hd = FOX_GROUP * hg + a
            cols = slice(128 * (a // 2), 128 * (a // 2) + 128)
            q = q_ref[:, cols]
            qa = jnp.where(low if a % 2 == 0 else jnp.logical_not(low), q, jnp.zeros_like(q)) * scale
            fq = _pick_lane(cum_t, hd)
            return (lambda j: _dot_nt(qa, k_ref[rows_of(j), cols]) + fq - _pick_row(cr_ref[j], hd),
                    lambda j, p: _dot(p, v_ref[rows_of(j), cols]), 2 * FOX_HD)

        res = _causal_softmax_blocks(i, tq, [head(a) for a in range(FOX_GROUP)])
        o_ref[...] = jnp.concatenate([jnp.where(low, res[a][0], res[a + 1][0]) for a in range(0, FOX_GROUP, 2)],
                                     axis=1).astype(BF16)
        lse_t = lse_ref[...]
        for a in range(FOX_GROUP):
            lse_t = _put_lane(lse_t, FOX_GROUP * hg + a, res[a][1])
        lse_ref[...] = lse_t

    return _call(
        body, name, (nb, nq, ngroups),
        [pl.BlockSpec((tq, wide), lambda b, i, hg: (b * nq + i, hg)),
         pl.BlockSpec((s, wide), lambda b, i, hg: (b, ngroups + hg)),
         pl.BlockSpec((s, wide), lambda b, i, hg: (b, 2 * ngroups + hg)),
         pl.BlockSpec((tq, 128), lambda b, i, hg: (b * nq + i, 0)),
         pl.BlockSpec((nq, 16, tq), lambda b, i, hg: (b, 0, 0))],
        [pl.BlockSpec((tq, wide), lambda b, i, hg: (b * nq + i, hg)),
         pl.BlockSpec((tq, 128), lambda b, i, hg: (b * nq + i, 0))],
        [_sds((t, D_MODEL), BF16), _sds((t, 128), F32)], (qkv, qkv, qkv, cum, cum_rows), hosted=hosted)


def mla_attn_fwd(q, kn, kr2, v, nb, name, hosted=None):
    t = q.shape[0]
    s = t // nb
    tq = ATTN_TILE
    nq = s // tq
    ngroups = MLA_HEADS // MLA_GROUP
    wide = MLA_GROUP * MLA_NOPE
    rwide = MLA_GROUP * MLA_ROPE
    scale = (MLA_NOPE + MLA_ROPE) ** -0.5

    def body(qn_ref, qr_ref, kn_ref, kr_ref, v_ref, o_ref, lse_ref):
        i = pl.program_id(1)
        hg = pl.program_id(2)

        @pl.when(hg == 0)
        def _():
            lse_ref[...] = jnp.zeros_like(lse_ref)

        low = lax.broadcasted_iota(jnp.int32, (tq, 128), 1) < MLA_ROPE

        def rows_of(j):
            return pl.ds(pl.multiple_of(j * tq, tq), tq)

        def head(a):
            cols = slice(a * MLA_NOPE, (a + 1) * MLA_NOPE)
            qr = qr_ref[:, 128 * (a // 2):128 * (a // 2) + 128]
            q_cat = jnp.concatenate([qn_ref[:, cols], jnp.where(low if a % 2 == 0 else jnp.logical_not(low), qr,
                                                                jnp.zeros_like(qr))], axis=1)
            return (lambda j: _dot_nt(q_cat, jnp.concatenate([kn_ref[rows_of(j), cols], kr_ref[rows_of(j), :]], axis=1)) * scale,
                    lambda j, p: _dot(p, v_ref[rows_of(j), cols]), MLA_V)

        res = _causal_softmax_blocks(i, tq, [head(a) for a in range(MLA_GROUP)])
        o_ref[...] = jnp.concatenate([r[0] for r in res], axis=1).astype(BF16)
        lse_t = lse_ref[...]
        for a in range(MLA_GROUP):
            lse_t = _put_lane(lse_t, MLA_GROUP * hg + a, res[a][1])
        lse_ref[...] = lse_t

    rope0 = MLA_HEADS * MLA_NOPE // rwide
    return _call(
        body, name, (nb, nq, ngroups),
        [pl.BlockSpec((tq, wide), lambda b, i, hg: (b * nq + i, hg)),
         pl.BlockSpec((tq, rwide), lambda b, i, hg: (b * nq + i, rope0 + hg)),
         pl.BlockSpec((s, wide), lambda b, i, hg: (b, hg)),
         pl.BlockSpec((s, 128), lambda b, i, hg: (b, 0)),
         pl.BlockSpec((s, wide), lambda b, i, hg: (b, hg))],
        [pl.BlockSpec((tq, wide), lambda b, i, hg: (b * nq + i, hg)),
         pl.BlockSpec((tq, 128), lambda b, i, hg: (b * nq + i, 0))],
        [_sds((t, MLA_HEADS * MLA_V), BF16), _sds((t, 128), F32)], (q, q, kn, kr2, v), hosted=hosted)


def rows16(a, name):
    t = a.shape[0]
    tq = ATTN_TILE

    def body(a_ref, o_ref):
        o_ref[...] = a_ref[...].T[:16, :]

    return pl.pallas_call(
        body, name=name, grid=(t // tq,), in_specs=[pl.BlockSpec((tq, 128), lambda n: (n, 0))],
        out_specs=pl.BlockSpec((None, 16, tq), lambda n: (n, 0, 0)), out_shape=_sds((t // tq, 16, tq), F32),
        compiler_params=_cp(1),
    )(a)


def tokens128(rows, onehot, name):
    nblk, _, tq = rows.shape

    def body(r_ref, e_ref, o_ref):
        o_ref[...] = lax.dot_general(r_ref[...], e_ref[...], (((0,), (0,)), ((), ())), preferred_element_type=F32,
                                     precision=lax.Precision.HIGHEST)

    return pl.pallas_call(
        body, name=name, grid=(nblk,),
        in_specs=[pl.BlockSpec((None, 16, tq), lambda n: (n, 0, 0)), pl.BlockSpec((16, 128), lambda n: (0, 0))],
        out_specs=pl.BlockSpec((tq, 128), lambda n: (n, 0)), out_shape=_sds((nblk * tq, 128), F32),
        compiler_params=_cp(1),
    )(rows, onehot)


def _layer_norm(z, g, b):
    mu = jnp.mean(z, axis=-1, keepdims=True)
    zc = z - mu
    rstd = lax.rsqrt(jnp.mean(zc * zc, axis=-1, keepdims=True) + EPS)
    xhat = zc * rstd
    return xhat * g + b, xhat, rstd


def linear_resid_ln(a, w, x_in, gate, ln_g, ln_b, name):
    t, kdim = a.shape
    d = w.shape[1]
    tm = TOKEN_TILE
    tps = (t // gate.shape[0]) // tm

    def body(a_ref, w_ref, x_ref, gt_ref, g_ref, b_ref, y_ref, xo_ref):
        y = _dot(a_ref[...], w_ref[...])
        y_ref[...] = y
        z = ALPHA * x_ref[...] + (1.0 + gt_ref[...]) * y
        xo_ref[...] = _layer_norm(z, g_ref[...], b_ref[...])[0]

    rows = pl.BlockSpec((tm, d), lambda i: (i, 0))
    vec = pl.BlockSpec((1, d), lambda i: (0, 0))
    return pl.pallas_call(
        body, name=name, grid=(t // tm,),
        in_specs=[pl.BlockSpec((tm, kdim), lambda i: (i, 0)), pl.BlockSpec((kdim, d), lambda i: (0, 0)), rows,
                  pl.BlockSpec((None, 1, d), lambda i: (i // tps, 0, 0)), vec, vec],
        out_specs=[rows, rows], out_shape=[_sds((t, d), F32), _sds((t, d), F32)],
        compiler_params=_cp(1),
    )(a, w, x_in, gate, ln_g, ln_b)


def _resident(a):
    return pl.BlockSpec(a.shape, lambda *_: (0,) * a.ndim, pipeline_mode=pl.Buffered(1))


def ffn_fwd(x_in, shift, scale, gate, wg, wu, wd, ln_g, ln_b, name, hosted=None):
    t, d = x_in.shape
    c, _, fc = wg.shape
    tm = TOKEN_TILE
    tps = (t // gate.shape[0]) // tm

    def body(x_ref, sh_ref, sc_ref, gt_ref, wg_ref, wu_ref, wd_ref, g_ref, b_ref,
             u_ref, hg_ref, hu_ref, y_ref, xo_ref, acc_ref):
        cc = pl.program_id(1)

        @pl.when(cc == 0)
        def _():
            u_ref[...] = (x_ref[...] * (1.0 + sc_ref[...]) + sh_ref[...]).astype(BF16)
            acc_ref[...] = jnp.zeros_like(acc_ref)

        u = u_ref[...]
        hg = _dot(u, wg_ref[cc])
        hu = _dot(u, wu_ref[cc])
        hg_ref[...] = hg.astype(BF16)
        hu_ref[...] = hu.astype(BF16)
        act = (hg * jax.nn.sigmoid(hg) * hu).astype(BF16)
        acc_ref[...] += _dot(act, wd_ref[cc])

        @pl.when(cc == c - 1)
        def _():
            y = acc_ref[...]
            y_ref[...] = y
            z = ALPHA * x_ref[...] + (1.0 + gt_ref[...]) * y
            xo_ref[...] = _layer_norm(z, g_ref[...], b_ref[...])[0]

    rows = pl.BlockSpec((tm, d), lambda i, cc: (i, 0))
    bvec = pl.BlockSpec((None, 1, d), lambda i, cc: (i // tps, 0, 0))
    vec = pl.BlockSpec((1, d), lambda i, cc: (0, 0))
    hspec = pl.BlockSpec((None, tm, fc), lambda i, cc: (cc, i, 0))
    wcol = _resident(wg)
    return _call(
        body, name, (t // tm, c),
        [rows, bvec, bvec, bvec, wcol, wcol, _resident(wd), vec, vec],
        [rows, hspec, hspec, rows, rows],
        [_sds((t, d), BF16), _sds((c, t, fc), BF16), _sds((c, t, fc), BF16), _sds((t, d), F32), _sds((t, d), F32)],
        (x_in, shift, scale, gate, wg, wu, wd, ln_g, ln_b), scratch_shapes=[pltpu.VMEM((tm, d), F32)], hosted=hosted)


def fox_gate_fwd(hf, b_f, tri, n_batch, name):
    t, n = hf.shape
    blk = tri.shape[0]
    nb = (t // n_batch) // blk

    def body(hf_ref, b_ref, tri_ref, o_ref, carry_ref):
        @pl.when(pl.program_id(1) == 0)
        def _():
            carry_ref[...] = jnp.zeros_like(carry_ref)

        xx = hf_ref[...] + b_ref[...]
        lf = jnp.minimum(xx, 0.0) - jnp.log(1.0 + jnp.exp(-jnp.abs(xx)))
        cum = _dot_f32(tri_ref[...], lf) + carry_ref[...]
        o_ref[...] = cum
        carry_ref[...] = cum[blk - 1:blk, :]

    return pl.pallas_call(
        body, name=name, grid=(n_batch, nb),
        in_specs=[pl.BlockSpec((blk, n), lambda bb, i: (bb * nb + i, 0)), pl.BlockSpec((1, n), lambda bb, i: (0, 0)),
                  pl.BlockSpec((blk, blk), lambda bb, i: (0, 0))],
        out_specs=pl.BlockSpec((blk, n), lambda bb, i: (bb * nb + i, 0)),
        out_shape=_sds((t, n), F32), scratch_shapes=[pltpu.VMEM((1, n), F32)],
        compiler_params=_cp(2),
    )(hf, b_f, tri)


def ln_bwd(dxo, x_in, y, gate, ln_g, name, target=None):
    t, d = dxo.shape
    nb = gate.shape[0]
    tm = TOKEN_TILE
    tps = (t // nb) // tm
    with_loss = target is not None

    def body(dxo_ref, *refs):
        if with_loss:
            t_ref, x_ref, y_ref, gt_ref, g_ref, dz_ref, dy_ref, dg_ref, db_ref, dgt_ref, l_ref = refs
        else:
            x_ref, y_ref, gt_ref, g_ref, dz_ref, dy_ref, dg_ref, db_ref, dgt_ref = refs
        i = pl.program_id(0)

        @pl.when(i == 0)
        def _():
            dg_ref[...] = jnp.zeros_like(dg_ref)
            db_ref[...] = jnp.zeros_like(db_ref)
            if with_loss:
                l_ref[...] = jnp.zeros_like(l_ref)

        @pl.when(i % tps == 0)
        def _():
            dgt_ref[...] = jnp.zeros_like(dgt_ref)

        yy = y_ref[...]
        g1 = 1.0 + gt_ref[...]
        z = ALPHA * x_ref[...] + g1 * yy
        _, xhat, rstd = _layer_norm(z, 1.0, 0.0)
        dxo_v = dxo_ref[...]
        if with_loss:
            err = dxo_v - t_ref[...]
            l_ref[...] += jnp.sum(err * err, axis=0, keepdims=True)
            dxo_v = err / d
        dg_ref[...] += jnp.sum(dxo_v * xhat, axis=0, keepdims=True)
        db_ref[...] += jnp.sum(dxo_v, axis=0, keepdims=True)
        dxh = dxo_v * g_ref[...]
        dz = rstd * (dxh - jnp.mean(dxh, axis=-1, keepdims=True) - xhat * jnp.mean(dxh * xhat, axis=-1, keepdims=True))
        dz_ref[...] = dz
        dy_ref[...] = (g1 * dz).astype(BF16)
        dgt_ref[...] += jnp.sum(dz * yy, axis=0, keepdims=True)

    rows = pl.BlockSpec((tm, d), lambda i: (i, 0))
    vec = pl.BlockSpec((1, d), lambda i: (0, 0))
    bvec = pl.BlockSpec((None, 1, d), lambda i: (i // tps, 0, 0))
    return pl.pallas_call(
        body, name=name, grid=(t // tm,), in_specs=[rows] * (4 if with_loss else 3) + [bvec, vec],
        out_specs=[rows, rows, vec, vec, bvec] + ([vec] if with_loss else []),
        out_shape=[_sds((t, d), F32), _sds((t, d), BF16), _sds((1, d), F32), _sds((1, d), F32), _sds((nb, 1, d), F32)]
        + ([_sds((1, d), F32)] if with_loss else []),
        compiler_params=_cp(1),
    )(dxo, *([target] if with_loss else []), x_in, y, gate, ln_g)


def _mod_bwd_tail(du, dz_ref, x_ref, sc_ref, dx_ref, dsc_ref, dsh_ref, first):
    @pl.when(first)
    def _():
        dsc_ref[...] = jnp.zeros_like(dsc_ref)
        dsh_ref[...] = jnp.zeros_like(dsh_ref)

    dx_ref[...] = ALPHA * dz_ref[...] + du * (1.0 + sc_ref[...])
    dsc_ref[...] += jnp.sum(du * x_ref[...], axis=0, keepdims=True)
    dsh_ref[...] += jnp.sum(du, axis=0, keepdims=True)


def ffn_bwd(dy, hg, hu, wg, wu, wd, dz, x_in, scale, name, hosted=None):
    t, d = dy.shape
    c, _, fc = wg.shape
    nb = scale.shape[0]
    tm = TOKEN_TILE
    tps = (t // nb) // tm

    def body(dy_ref, hg_ref, hu_ref, wg_ref, wu_ref, wd_ref, dz_ref, x_ref, sc_ref,
             dhg_ref, dhu_ref, act_ref, dx_ref, dsc_ref, dsh_ref, acc_ref):
        i = pl.program_id(0)
        cc = pl.program_id(1)

        @pl.when(cc == 0)
        def _():
            acc_ref[...] = jnp.zeros_like(acc_ref)

        hgv = hg_ref[...].astype(F32)
        huv = hu_ref[...].astype(F32)
        da = _dot_nt(dy_ref[...], wd_ref[cc])
        sg = jax.nn.sigmoid(hgv)
        sl = hgv * sg
        act_ref[...] = (sl * huv).astype(BF16)
        dhu = (da * sl).astype(BF16)
        dhg = (da * huv * (sg * (1.0 + hgv * (1.0 - sg)))).astype(BF16)
        dhu_ref[...] = dhu
        dhg_ref[...] = dhg
        acc_ref[...] += _dot_nt(dhg, wg_ref[cc]) + _dot_nt(dhu, wu_ref[cc])

        @pl.when(cc == c - 1)
        def _():
            _mod_bwd_tail(acc_ref[...], dz_ref, x_ref, sc_ref, dx_ref, dsc_ref, dsh_ref, i % tps == 0)

    rows = pl.BlockSpec((tm, d), lambda i, cc: (i, 0))
    bvec = pl.BlockSpec((None, 1, d), lambda i, cc: (i // tps, 0, 0))
    hspec = pl.BlockSpec((None, tm, fc), lambda i, cc: (cc, i, 0))
    wcol = _resident(wg)
    return _call(
        body, name, (t // tm, c),
        [rows, hspec, hspec, wcol, wcol, _resident(wd), rows, rows, bvec],
        [hspec, hspec, hspec, rows, bvec, bvec],
        [_sds((c, t, fc), BF16), _sds((c, t, fc), BF16), _sds((c, t, fc), BF16), _sds((t, d), F32),
         _sds((nb, 1, d), F32), _sds((nb, 1, d), F32)],
        (dy, hg, hu, wg, wu, wd, dz, x_in, scale), scratch_shapes=[pltpu.VMEM((tm, d), F32)], hosted=hosted)


def linear_nt_mod_bwd(pairs, dz, x_in, scale, name, hosted=None):
    t, d = dz.shape
    nb = scale.shape[0]
    tm = TOKEN_TILE
    tps = (t // nb) // tm
    npairs = len(pairs)

    def body(*refs):
        dh_refs = refs[:npairs]
        w_refs = refs[npairs:2 * npairs]
        dz_ref, x_ref, sc_ref, dx_ref, dsc_ref, dsh_ref = refs[2 * npairs:]
        du = None
        for (_, _, blk), dh_ref, w_ref in zip(pairs, dh_refs, w_refs):
            dh = dh_ref[...].astype(BF16)
            term = _dot_nt(dh, w_ref[...]) if blk is None else _dot(dh, w_ref[...])
            du = term if du is None else du + term
        _mod_bwd_tail(du, dz_ref, x_ref, sc_ref, dx_ref, dsc_ref, dsh_ref, pl.program_id(0) % tps == 0)

    rows = pl.BlockSpec((tm, d), lambda i: (i, 0))
    bvec = pl.BlockSpec((None, 1, d), lambda i: (i // tps, 0, 0))
    in_specs = [pl.BlockSpec((tm, dh.shape[1]), lambda i: (i, 0)) for dh, _, _ in pairs]
    for dh, w, blk in pairs:
        if blk is None:
            in_specs.append(pl.BlockSpec(w.shape, lambda i: (0, 0)))
        else:
            in_specs.append(pl.BlockSpec((dh.shape[1], d), lambda i, blk=blk: (blk, 0)))
    in_specs += [rows, rows, bvec]
    return _call(
        body, name, (t // tm,), in_specs, [rows, bvec, bvec],
        [_sds((t, d), F32), _sds((nb, 1, d), F32), _sds((nb, 1, d), F32)],
        (*[dh for dh, _, _ in pairs], *[w for _, w, _ in pairs], dz, x_in, scale), hosted=hosted)


def linear_nt_delta(dy, w_o, o, head_sel, name):
    t, d = dy.shape
    hdv = w_o.shape[0]
    tm = TOKEN_TILE

    def body(dy_ref, w_ref, o_ref, sel_ref, do_ref, dl_ref):
        do = _dot_nt(dy_ref[...], w_ref[...])
        do_ref[...] = do.astype(BF16)
        dl_ref[...] = _dot_f32(do * o_ref[...].astype(F32), sel_ref[...])

    return pl.pallas_call(
        body, name=name, grid=(t // tm,),
        in_specs=[pl.BlockSpec((tm, d), lambda i: (i, 0)), pl.BlockSpec((hdv, d), lambda i: (0, 0)),
                  pl.BlockSpec((tm, hdv), lambda i: (i, 0)), pl.BlockSpec(head_sel.shape, lambda i: (0, 0))],
        out_specs=[pl.BlockSpec((tm, hdv), lambda i: (i, 0)), pl.BlockSpec((tm, 128), lambda i: (i, 0))],
        out_shape=[_sds((t, hdv), BF16), _sds((t, 128), F32)], compiler_params=_cp(1),
    )(dy, w_o, o, head_sel)


def _attn_bwd_blocks(j, nk, tk, scale, heads):
    def block(i, carry, masked):
        new = []
        for hd, (dk_acc, dv_acc, dfk_acc) in zip(heads, carry):
            qb = hd["q"](i)
            dob = hd["do"](i)
            lse_row, dl_row = hd["rows"](i)
            st = _dot_nt(hd["k"], qb)
            if scale is not None:
                st = st * scale
            if hd["bias"] is not None:
                fq_row, fk_col = hd["bias"](i)
                st = st + fq_row - fk_col
            if masked:
                keep = lax.broadcasted_iota(jnp.int32, st.shape, 1) >= lax.broadcasted_iota(jnp.int32, st.shape, 0)
                st = jnp.where(keep, st, -1e30)
            pt = jnp.exp(st - lse_row)
            dv_acc = dv_acc + _dot(pt.astype(BF16), dob)
            dst = pt * (_dot_nt(hd["v"], dob) - dl_row)
            if hd["add_dfq"] is not None:
                dfk_acc = dfk_acc - jnp.sum(dst, axis=1, keepdims=True)
                hd["add_dfq"](i, jnp.sum(dst, axis=0, keepdims=True))
            dsb = (dst if scale is None else dst * scale).astype(BF16)
            dk_acc = dk_acc + _dot(dsb, qb)
            hd["add_dq"](i, _dot_tn(dsb, hd["k"] if scale is not None else hd["k_scaled"]))
            new.append((dk_acc, dv_acc, dfk_acc))
        return tuple(new)

    init = tuple((jnp.zeros((tk, hd["k"].shape[1]), F32), jnp.zeros((tk, hd["v"].shape[1]), F32), jnp.zeros((tk, 1), F32))
                 for hd in heads)
    carry = block(j, init, True)
    return lax.fori_loop(j + 1, nk, lambda i, c: block(i, c, False), carry)


def fox_attn_bwd(qkv, do, cum, cum_rows, lse_rows, delta_rows, nb, name, hosted=None):
    t = qkv.shape[0]
    s = t // nb
    tk = ATTN_TILE
    nk = s // tk
    scale = FOX_HD ** -0.5

    def body(q_ref, k_ref, v_ref, do_ref, cum_ref, cr_ref, lr_ref, dr_ref, dq_ref, dk_ref, dv_ref, dfq_ref, dfk_ref):
        hg = pl.program_id(1)
        j = pl.program_id(2)

        @pl.when(j == 0)
        def _():
            dq_ref[...] = jnp.zeros_like(dq_ref)

        @pl.when((j == 0) & (hg == 0))
        def _():
            dfq_ref[...] = jnp.zeros_like(dfq_ref)
            dfk_ref[...] = jnp.zeros_like(dfk_ref)

        low = lax.broadcasted_iota(jnp.int32, (tk, 128), 1) < FOX_HD
        cum_t = cum_ref[...]

        def rows_of(i):
            return pl.ds(pl.multiple_of(i * tk, tk), tk)

        def head(a):
            hd = FOX_GROUP * hg + a
            cols = slice(128 * (a // 2), 128 * (a // 2) + 128)
            half = low if a % 2 == 0 else jnp.logical_not(low)
            kb = k_ref[:, cols]
            vb = v_ref[:, cols]
            fk = _pick_lane(cum_t, hd)

            def add_dq(i, val):
                dq_ref[rows_of(i), cols] += val

            def add_dfq(i, val):
                dfq_ref[i] = _put_row(dfq_ref[i], hd, val)

            ka = jnp.where(half, kb, jnp.zeros_like(kb))
            return dict(q=lambda i: q_ref[rows_of(i), cols] * scale, do=lambda i: do_ref[rows_of(i), cols],
                        k=ka, k_scaled=ka * scale, v=jnp.where(half, vb, jnp.zeros_like(vb)),
                        rows=lambda i: (_pick_row(lr_ref[i], hd), _pick_row(dr_ref[i], hd)),
                        bias=lambda i: (_pick_row(cr_ref[i], hd), fk), add_dq=add_dq, add_dfq=add_dfq)

        res = _attn_bwd_blocks(j, nk, tk, None, [head(a) for a in range(FOX_GROUP)])
        dk_ref[...] = jnp.concatenate([jnp.where(low, res[a][0], res[a + 1][0]) for a in range(0, FOX_GROUP, 2)],
                                      axis=1).astype(BF16)
        dv_ref[...] = jnp.concatenate([jnp.where(low, res[a][1], res[a + 1][1]) for a in range(0, FOX_GROUP, 2)],
                                      axis=1).astype(BF16)
        for a in range(FOX_GROUP):
            dfk_ref[j] = _put_row(dfk_ref[j], FOX_GROUP * hg + a, jnp.broadcast_to(res[a][2], (tk, 128)).T[0:1, :])

    wide = FOX_GROUP * FOX_HD
    ngroups = FOX_HEADS // FOX_GROUP
    rowsp = pl.BlockSpec((nk, 16, tk), lambda b, hg, j: (b, 0, 0))
    return _call(
        body, name, (nb, ngroups, nk),
        [pl.BlockSpec((s, wide), lambda b, hg, j: (b, hg)),
         pl.BlockSpec((tk, wide), lambda b, hg, j: (b * nk + j, ngroups + hg)),
         pl.BlockSpec((tk, wide), lambda b, hg, j: (b * nk + j, 2 * ngroups + hg)),
         pl.BlockSpec((s, wide), lambda b, hg, j: (b, hg)),
         pl.BlockSpec((tk, 128), lambda b, hg, j: (b * nk + j, 0)),
         rowsp, rowsp, rowsp],
        [pl.BlockSpec((s, wide), lambda b, hg, j: (b, hg)),
         pl.BlockSpec((tk, wide), lambda b, hg, j: (b * nk + j, hg)),
         pl.BlockSpec((tk, wide), lambda b, hg, j: (b * nk + j, hg)),
         rowsp, rowsp],
        [_sds((t, D_MODEL), F32), _sds((t, D_MODEL), BF16), _sds((t, D_MODEL), BF16),
         _sds((t // tk, 16, tk), F32), _sds((t // tk, 16, tk), F32)],
        (qkv, qkv, qkv, do, cum, cum_rows, lse_rows, delta_rows), hosted=hosted)


def mla_attn_bwd(q, kn, kr2, v, do, lse_rows, delta_rows, nb, name, hosted=None):
    t = q.shape[0]
    s = t // nb
    tk = ATTN_TILE
    nk = s // tk
    ngroups = MLA_HEADS // MLA_GROUP
    wide = MLA_GROUP * MLA_NOPE
    rwide = MLA_GROUP * MLA_ROPE
    scale = (MLA_NOPE + MLA_ROPE) ** -0.5

    def body(qn_ref, qr_ref, kn_ref, kr_ref, v_ref, do_ref, lr_ref, dr_ref, dqn_ref, dqr_ref, dkn_ref, dkr_ref, dv_ref):
        hg = pl.program_id(1)
        j = pl.program_id(2)

        @pl.when(j == 0)
        def _():
            dqn_ref[...] = jnp.zeros_like(dqn_ref)
            dqr_ref[...] = jnp.zeros_like(dqr_ref)

        low = lax.broadcasted_iota(jnp.int32, (tk, 128), 1) < MLA_ROPE
        kr = kr_ref[...]

        def rows_of(i):
            return pl.ds(pl.multiple_of(i * tk, tk), tk)

        def head(a):
            cols = slice(a * MLA_NOPE, (a + 1) * MLA_NOPE)
            rcols = slice(128 * (a // 2), 128 * (a // 2) + 128)
            mine = low if a % 2 == 0 else jnp.logical_not(low)
            hd = MLA_GROUP * hg + a

            def q_fn(i):
                qr = qr_ref[rows_of(i), rcols]
                return jnp.concatenate([qn_ref[rows_of(i), cols], jnp.where(mine, qr, jnp.zeros_like(qr))], axis=1)

            def add_dq(i, val):
                dqn_ref[rows_of(i), cols] += val[:, :MLA_NOPE]
                dqr_ref[rows_of(i), cols] += val[:, MLA_NOPE:]

            return dict(q=q_fn, do=lambda i: do_ref[rows_of(i), cols], k=jnp.concatenate([kn_ref[:, cols], kr], axis=1),
                        v=v_ref[:, cols], rows=lambda i: (_pick_row(lr_ref[i], hd), _pick_row(dr_ref[i], hd)),
                        bias=None, add_dq=add_dq, add_dfq=None)

        res = _attn_bwd_blocks(j, nk, tk, scale, [head(a) for a in range(MLA_GROUP)])
        dkn_ref[...] = jnp.concatenate([r[0][:, :MLA_NOPE] for r in res], axis=1).astype(BF16)
        dkr_ref[...] = jnp.concatenate([r[0][:, MLA_NOPE:] for r in res], axis=1).astype(BF16)
        dv_ref[...] = jnp.concatenate([r[1] for r in res], axis=1).astype(BF16)

    full = pl.BlockSpec((s, wide), lambda b, hg, j: (b, hg))
    blk = pl.BlockSpec((tk, wide), lambda b, hg, j: (b * nk + j, hg))
    rowsp = pl.BlockSpec((nk, 16, tk), lambda b, hg, j: (b, 0, 0))
    total = MLA_HEADS * MLA_V
    rope0 = MLA_HEADS * MLA_NOPE // rwide
    return _call(
        body, name, (nb, ngroups, nk),
        [full, pl.BlockSpec((s, rwide), lambda b, hg, j: (b, rope0 + hg)), blk,
         pl.BlockSpec((tk, 128), lambda b, hg, j: (b * nk + j, 0)), blk, full, rowsp, rowsp],
        [full, full, blk, blk, blk],
        [_sds((t, total), F32), _sds((t, total), F32), _sds((t, total), BF16), _sds((t, total), BF16),
         _sds((t, total), BF16)],
        (q, q, kn, kr2, v, do, lse_rows, delta_rows), hosted=hosted)


def mla_mid_bwd(dqn, dqr, dkn, dv, dkr_heads, h, g_q, g_kv, w_uq, w_uk, w_uv, cos8, sin8, cos64, sin64s, swap64,
                heads_to_rope, head_sum, name, hosted=None):
    t = h.shape[0]
    tm = TOKEN_TILE
    hq = MLA_HEADS * MLA_NOPE
    hr = MLA_HEADS * MLA_ROPE // 2
    nq = w_uq.shape[1]

    def body(dqn_ref, dqr_ref, dkn_ref, dv_ref, dkr_ref, h_ref, gq_ref, gkv_ref, wuq_ref, wuk_ref, wuv_ref,
             c8_ref, s8_ref, c64_ref, s64_ref, sw_ref, hp_ref, hs_ref, dh_ref, dqp_ref, dgq_ref, dgkv_ref):
        @pl.when(pl.program_id(0) == 0)
        def _():
            dgq_ref[...] = jnp.zeros_like(dgq_ref)
            dgkv_ref[...] = jnp.zeros_like(dgkv_ref)

        drot = _dot(dqr_ref[...].astype(BF16), hp_ref[...])
        o1 = drot[:, :hr]
        o2 = drot[:, hr:]
        cs = c8_ref[...]
        sn = s8_ref[...]
        dqp = jnp.concatenate([dqn_ref[...].astype(BF16), (o1 * cs + o2 * sn).astype(BF16),
                               (o2 * cs - o1 * sn).astype(BF16)], axis=1)
        dqp_ref[...] = dqp
        dcq = _dot_nt(dqp, wuq_ref[...])
        dckv = _dot_nt(dkn_ref[...], wuk_ref[...]) + _dot_nt(dv_ref[...], wuv_ref[...])
        hh = h_ref[...]

        def rms_bwd(hpart, g, dc, dg_ref):
            hhat, rstd = _rms(hpart, None)
            dg_ref[...] += jnp.sum(dc * hhat, axis=0, keepdims=True)
            dcg = dc * g
            return rstd * (dcg - hhat * jnp.mean(dcg * hhat, axis=-1, keepdims=True))

        dhq = rms_bwd(hh[:, :MLA_QR], gq_ref[...], dcq, dgq_ref)
        dhkv = rms_bwd(hh[:, MLA_QR:MLA_QR + MLA_KVR], gkv_ref[...], dckv, dgkv_ref)
        dkr = _dot(dkr_ref[...], hs_ref[...])
        dkr_pre = dkr * c64_ref[...] + _dot_f32(dkr * s64_ref[...], sw_ref[...])
        dh_ref[...] = jnp.concatenate([dhq, dhkv, dkr_pre], axis=1).astype(BF16)

    def rows(n):
        return pl.BlockSpec((tm, n), lambda i: (i, 0))

    def whole(a):
        return pl.BlockSpec(a.shape, lambda i: (0,) * a.ndim)

    return _call(
        body, name, (t // tm,),
        [rows(hq), rows(hq), rows(hq), rows(hq), rows(hq), rows(h.shape[1]), whole(g_q), whole(g_kv),
         whole(w_uq), whole(w_uk), whole(w_uv), rows(hr), rows(hr), rows(MLA_ROPE), rows(MLA_ROPE),
         whole(swap64), whole(heads_to_rope), whole(head_sum)],
        [rows(h.shape[1]), rows(nq), pl.BlockSpec((1, MLA_QR), lambda i: (0, 0)),
         pl.BlockSpec((1, MLA_KVR), lambda i: (0, 0))],
        [_sds((t, h.shape[1]), BF16), _sds((t, nq), BF16), _sds((1, MLA_QR), F32), _sds((1, MLA_KVR), F32)],
        (dqn, dqr, dkn, dv, dkr_heads, h, g_q, g_kv, w_uq, w_uk, w_uv, cos8, sin8, cos64, sin64s, swap64,
         heads_to_rope, head_sum), hosted=hosted)


def fox_gate_bwd(dcum, hf, b_f, triu, n_batch, name):
    t, n = hf.shape
    blk = triu.shape[0]
    nb = (t // n_batch) // blk

    def body(dc_ref, hf_ref, b_ref, tri_ref, o_ref, db_ref, carry_ref):
        @pl.when(pl.program_id(1) == 0)
        def _():
            carry_ref[...] = jnp.zeros_like(carry_ref)

        @pl.when((pl.program_id(0) == 0) & (pl.program_id(1) == 0))
        def _():
            db_ref[...] = jnp.zeros_like(db_ref)

        rc = _dot_f32(tri_ref[...], dc_ref[...]) + carry_ref[...]
        carry_ref[...] = rc[0:1, :]
        dhf = rc * jax.nn.sigmoid(-(hf_ref[...] + b_ref[...]))
        o_ref[...] = dhf.astype(BF16)
        db_ref[...] += jnp.sum(dhf, axis=0, keepdims=True)

    rev = pl.BlockSpec((blk, n), lambda bb, i: (bb * nb + nb - 1 - i, 0))
    return pl.pallas_call(
        body, name=name, grid=(n_batch, nb),
        in_specs=[rev, rev, pl.BlockSpec((1, n), lambda bb, i: (0, 0)), pl.BlockSpec((blk, blk), lambda bb, i: (0, 0))],
        out_specs=[rev, pl.BlockSpec((1, n), lambda bb, i: (0, 0))],
        out_shape=[_sds((t, n), BF16), _sds((1, n), F32)], scratch_shapes=[pltpu.VMEM((1, n), F32)],
        compiler_params=_cp(2),
    )(dcum, hf, b_f, triu)


def wgrad(a, bm, name, with_bf16=False, bt=WGRAD_TOKENS):
    ca, t, kd = a.shape
    cb, _, nd = bm.shape
    c = max(ca, cb)
    bn = nd
    if nd > 1024 and nd % 1024 == 0:
        bn = 1024
    nsteps = t // bt
    nb = nd // bn
    total = c * nb * nsteps
    ring_bytes = bt * (kd * a.dtype.itemsize + bn * bm.dtype.itemsize)
    depth = 3 if 3 * ring_bytes <= VMEM_LIMIT // 2 else 2

    def body(a_hbm, b_hbm, o_ref, *rest):
        a_buf, b_buf, sems = rest[-3:]
        s = (pl.program_id(0) * nb + pl.program_id(1)) * nsteps + pl.program_id(2)

        def copies(step, slot):
            cc = step // (nb * nsteps) if c > 1 else 0
            n = (step // nsteps) % nb if nb > 1 else 0
            rows = pl.ds(pl.multiple_of((step % nsteps) * bt, bt), bt)
            cols = pl.ds(pl.multiple_of(n * bn, bn), bn) if nb > 1 else slice(None)
            return (pltpu.make_async_copy(a_hbm.at[cc if ca > 1 else 0, rows, :], a_buf.at[slot], sems.at[0, slot]),
                    pltpu.make_async_copy(b_hbm.at[cc if cb > 1 else 0, rows, cols], b_buf.at[slot], sems.at[1, slot]))

        @pl.when(s == 0)
        def _():
            for k in range(min(depth - 1, total)):
                for cp in copies(k, k):
                    cp.start()

        slot = s % depth
        for cp in copies(s, slot):
            cp.wait()

        @pl.when(s + depth - 1 < total)
        def _():
            for cp in copies(s + depth - 1, (s + depth - 1) % depth):
                cp.start()

        @pl.when(pl.program_id(2) == 0)
        def _():
            o_ref[...] = jnp.zeros_like(o_ref)

        o_ref[...] += _dot_tn(a_buf[slot].astype(BF16), b_buf[slot].astype(BF16))
        if with_bf16:
            @pl.when(pl.program_id(2) == nsteps - 1)
            def _():
                rest[0][...] = o_ref[...].astype(BF16)

    out_spec = pl.BlockSpec((None, kd, bn), lambda cc, n, tt: (cc, 0, n))
    res = pl.pallas_call(
        body, name=name, grid=(c, nb, nsteps),
        in_specs=[pl.BlockSpec(memory_space=pl.ANY), pl.BlockSpec(memory_space=pl.ANY)],
        out_specs=[out_spec, out_spec] if with_bf16 else out_spec,
        out_shape=[_sds((c, kd, nd), F32), _sds((c, kd, nd), BF16)] if with_bf16 else _sds((c, kd, nd), F32),
        scratch_shapes=[pltpu.VMEM((depth, bt, kd), a.dtype), pltpu.VMEM((depth, bt, bn), bm.dtype),
                        pltpu.SemaphoreType.DMA((2, depth))],
        compiler_params=_cp(3),
    )(a, bm)
    return res


def ada_mod_part(c_all, ada_w, name):
    nl, d, n = ada_w.shape
    rows = c_all.shape[0]
    tn = 512

    def body(c_ref, w_ref, o_ref):
        cv = c_ref[...]
        act = (cv * jax.nn.sigmoid(cv)).astype(BF16)
        o_ref[...] = _dot(act, w_ref[...].astype(BF16))

    return pl.pallas_call(
        body, name=name, grid=(nl, n // tn),
        in_specs=[pl.BlockSpec((rows, d), lambda l, j: (0, 0)), pl.BlockSpec((None, d, tn), lambda l, j: (l, 0, j))],
        out_specs=pl.BlockSpec((None, rows, tn), lambda l, j: (l, 0, j)),
        out_shape=_sds((nl, rows, n), F32), compiler_params=_cp(2),
    )(c_all, ada_w)


def ada_grad(c_all_t, dmod, name):
    nl, rows, n = dmod.shape
    d = c_all_t.shape[0]
    tn = 512

    def body(c_ref, dm_ref, o_ref):
        cv = c_ref[...]
        act = (cv * jax.nn.sigmoid(cv)).astype(BF16)
        o_ref[...] = _dot(act, dm_ref[...].astype(BF16))

    return pl.pallas_call(
        body, name=name, grid=(nl, n // tn),
        in_specs=[pl.BlockSpec((d, rows), lambda l, j: (0, 0)), pl.BlockSpec((None, rows, tn), lambda l, j: (l, 0, j))],
        out_specs=pl.BlockSpec((None, d, tn), lambda l, j: (l, 0, j)),
        out_shape=_sds((nl, d, n), F32), compiler_params=_cp(2),
    )(c_all_t, dmod)


def sum_leading(a, name):
    g, r, n = a.shape

    def body(a_ref, o_ref):
        acc = a_ref[0]
        for kk in range(1, g):
            acc = acc + a_ref[kk]
        o_ref[...] = acc

    return pl.pallas_call(
        body, name=name, grid=(1,), in_specs=[pl.BlockSpec((g, r, n), lambda i: (0, 0, 0))],
        out_specs=pl.BlockSpec((r, n), lambda i: (0, 0)), out_shape=_sds((r, n), F32), compiler_params=_cp(1),
    )(a)


def adamw(w, g, m, v, name):
    r, n = w.shape
    fits = [cand for cand in range(8, r, 8) if r % cand == 0 and cand * n * 4 <= ADAMW_BLOCK_BYTES]
    br = max(fits) if fits else r
    c1 = 1.0 - ADAM_B1 ** ADAM_STEP
    c2 = 1.0 - ADAM_B2 ** ADAM_STEP

    def body(w_ref, g_ref, m_ref, v_ref, d_ref, mo_ref, vo_ref):
        gv = g_ref[...]
        mn = ADAM_B1 * m_ref[...] + (1.0 - ADAM_B1) * gv
        vn = ADAM_B2 * v_ref[...] + (1.0 - ADAM_B2) * (gv * gv)
        mo_ref[...] = mn
        vo_ref[...] = vn
        d_ref[...] = -ADAM_LR * ((mn / c1) / (jnp.sqrt(vn / c2) + ADAM_EPS) + ADAM_WD * w_ref[...])

    spec = pl.BlockSpec((br, n), lambda i: (i, 0))
    return _call(body, name, (r // br,), [spec] * 4, [spec] * 3, [_sds((r, n), F32)] * 3, (w, g, m, v))


def all_gather8(x_blk, name, hosted=None):
    m_per, n = x_blk.shape
    h_in = 0 if hosted is None else len(hosted.inputs)
    h_out = 0 if hosted is None else len(hosted.out_shape)

    def body(x_ref, *refs):
        c_in, (out_ref, *c_out), (send_sems, recv_sems, local_sem, *c_sem) = (
            refs[:h_in], refs[h_in:h_in + 1 + h_out], refs[h_in + 1 + h_out:])
        if hosted is not None:
            hosted.start(c_in, c_out, c_sem)
        gather(x_ref, out_ref, send_sems, recv_sems, local_sem)
        if hosted is not None:
            hosted.finish(c_in, c_out, c_sem)

    def gather(x_ref, out_ref, send_sems, recv_sems, local_sem):
        x, y, c = _place()
        me, sibling = (x, y, c), (x, y, 1 - c)
        chips = [(1 - x, y), (x, 1 - y), (1 - x, 1 - y)]

        def rows(px, py, pc):
            return out_ref.at[pl.ds((4 * px + 2 * py + pc) * m_per, m_per), :]

        def copy(k, block, to, src=None):
            return pltpu.make_async_remote_copy(
                src_ref=rows(*block) if src is None else src, dst_ref=rows(*block),
                send_sem=send_sems.at[k], recv_sem=recv_sems.at[k], device_id=to, device_id_type=MESH)

        mine = pltpu.make_async_copy(x_ref, rows(*me), local_sem)
        mine.start()
        first = [copy(0, me, sibling, src=x_ref)]
        first += [copy(1 + j, me, (*chip, c), src=x_ref) for j, chip in enumerate(chips)]
        for cp in first:
            cp.start()
        passed = [copy(4 + j, (*chip, c), sibling) for j, chip in enumerate(chips)]
        for j, chip in enumerate(chips):
            copy(1 + j, (*chip, c), me).wait_recv()
            passed[j].start()
        copy(0, sibling, me).wait_recv()
        for j, chip in enumerate(chips):
            copy(4 + j, (*chip, 1 - c), me).wait_recv()
        for cp in first + passed:
            cp.wait_send()
        mine.wait()

    hbm = pl.BlockSpec(memory_space=pl.ANY)
    vmem = pl.BlockSpec(memory_space=pltpu.VMEM)
    res = pl.pallas_call(
        body, name=name,
        out_shape=[_sds((8 * m_per, n), x_blk.dtype)] + ([] if hosted is None else list(hosted.out_shape)),
        in_specs=[vmem] + [hbm] * h_in, out_specs=[vmem] + [hbm] * h_out,
        scratch_shapes=[pltpu.SemaphoreType.DMA((7,)), pltpu.SemaphoreType.DMA((7,)), pltpu.SemaphoreType.DMA]
        + ([] if hosted is None else list(hosted.sems)),
        compiler_params=pltpu.CompilerParams(vmem_limit_bytes=VMEM_LIMIT),
    )(x_blk, *([] if hosted is None else hosted.inputs))
    return res[0] if hosted is None else (res[0], res[1:])


def _gather_comm(shards):
    nt = len(shards)

    def parts(w_refs, out_refs, sems, finishing):
        send_sems, recv_sems, own_send, own_recv = sems
        x, y, c = _place()
        sibling = (x, y, 1 - c)
        chips = [(1 - x, y), (x, 1 - y), (1 - x, 1 - y)]

        def copy(t, k, block, to, src=None):
            px, py, hh = block
            dst = out_refs[t].at[2 * px + py, hh]
            return pltpu.make_async_remote_copy(
                src_ref=dst if src is None else src, dst_ref=dst,
                send_sem=send_sems.at[6 * t + k], recv_sem=recv_sems.at[6 * t + k], device_id=to, device_id_type=MESH)

        own = [pltpu.make_async_remote_copy(
            src_ref=w_refs[t], dst_ref=out_refs[t].at[2 * x + y], send_sem=own_send.at[t], recv_sem=own_recv.at[t],
            device_id=sibling, device_id_type=MESH) for t in range(nt)]
        first = [copy(t, j, (x, y, c), (*chip, c), src=w_refs[t].at[c]) for t in range(nt) for j, chip in enumerate(chips)]
        if not finishing:
            return own, first
        landed = [copy(t, j, (*chip, c), (x, y, c)) for t in range(nt) for j, chip in enumerate(chips)]
        passed = [copy(t, 3 + j, (*chip, c), sibling) for t in range(nt) for j, chip in enumerate(chips)]
        from_sibling = [copy(t, 3 + j, (*chip, 1 - c), (x, y, c)) for t in range(nt) for j, chip in enumerate(chips)]
        return own, first, landed, passed, from_sibling

    def start(w_refs, out_refs, sems):
        own, first = parts(w_refs, out_refs, sems, False)
        for cp in own + first:
            cp.start()

    def finish(w_refs, out_refs, sems):
        own, first, landed, passed, from_sibling = parts(w_refs, out_refs, sems, True)
        for arrived, fwd in zip(landed, passed):
            arrived.wait_recv()
            fwd.start()
        for cp in from_sibling:
            cp.wait_recv()
        for cp in first + passed:
            cp.wait_send()
        for cp in own:
            cp.wait()

    sems = [pltpu.SemaphoreType.DMA((6 * nt,)), pltpu.SemaphoreType.DMA((6 * nt,)),
            pltpu.SemaphoreType.DMA((nt,)), pltpu.SemaphoreType.DMA((nt,))]
    return _Hosted(list(shards), [_sds((N_CHIPS, *w.shape), w.dtype) for w in shards], sems, start, finish)


def _row_block(r, n, itemsize):
    best = None
    for br in range(16, r + 1, 16):
        if r % br == 0 and br * n * itemsize <= COMM_BLOCK_BYTES:
            best = br
    return r if best is None else best


def _scatter_comm(parts):
    nt = len(parts)

    def copies(p_refs, b_refs, sems, arriving):
        send_sems, recv_sems = sems
        x, y, c = _place()
        me = 4 * x + 2 * y + c
        cps = []
        for t in range(nt):
            for r in range(1, 8):
                tx = 1 - x if r & 4 else x
                ty = 1 - y if r & 2 else y
                tc = 1 - c if r & 1 else c
                src, dst = (2 * x + y, c), 4 * tx + 2 * ty + tc
                if not arriving:
                    src, dst = (2 * tx + ty, tc), me
                cps.append(pltpu.make_async_remote_copy(
                    src_ref=p_refs[t].at[src], dst_ref=b_refs[t].at[dst], send_sem=send_sems.at[7 * t + r - 1],
                    recv_sem=recv_sems.at[7 * t + r - 1], device_id=(tx, ty, tc), device_id_type=MESH))
        return cps

    def start(p_refs, b_refs, sems):
        for cp in copies(p_refs, b_refs, sems, False):
            cp.start()

    def finish(p_refs, b_refs, sems):
        for cp in copies(p_refs, b_refs, sems, True):
            cp.wait_recv()
        for cp in copies(p_refs, b_refs, sems, False):
            cp.wait_send()

    sems = [pltpu.SemaphoreType.DMA((7 * nt,)), pltpu.SemaphoreType.DMA((7 * nt,))]
    return _Hosted(list(parts), [_sds((2 * N_CHIPS, *p.shape[2:]), p.dtype) for p in parts], sems, start, finish)


def sum_devices(own, recv, place, name, slot=(0, 1, None)):
    _, _, r, n = own.shape
    layer, n_layers, buf = slot
    br = _row_block(r, n, 4 * 8)

    def body(p_ref, o_ref, *rest):
        acc = o_ref[...]
        for kk in range(7):
            acc = acc + rest[kk][...].astype(F32)
        rest[-1][...] = acc

    def arrived(rel):
        return pl.BlockSpec((None, br, n), lambda i, pref: (jnp.bitwise_xor(pref[0], rel), i, 0))

    in_specs = [pl.BlockSpec((None, None, br, n), lambda i, pref: (pref[2], pref[1], i, 0))]
    in_specs += [arrived(rel) for rel in range(1, 8)]
    args = [own] + [recv] * 7
    aliases = {}
    if buf is not None:
        in_specs.append(pl.BlockSpec(memory_space=pl.ANY))
        args.append(buf)
        aliases = {9: 0}
    return pl.pallas_call(
        body, name=name,
        grid_spec=pltpu.PrefetchScalarGridSpec(
            num_scalar_prefetch=1, grid=(r // br,), in_specs=in_specs,
            out_specs=pl.BlockSpec((None, None, br, n), lambda i, pref: (layer, pref[1], i, 0))),
        out_shape=_sds((n_layers, 2, r, n), F32), input_output_aliases=aliases, compiler_params=_cp(1),
    )(place, *args)


def _join_comm(bufs):
    nt = len(bufs)
    layers = [bf.shape[0] for bf in bufs]
    first = [sum(layers[:t]) for t in range(nt)]

    def copies(o_refs, sems, own):
        send_sems, recv_sems = sems
        x, y, c = _place()
        hh = c if own else 1 - c
        return [pltpu.make_async_remote_copy(
            src_ref=o_refs[t].at[l, hh], dst_ref=o_refs[t].at[l, hh], send_sem=send_sems.at[first[t] + l],
            recv_sem=recv_sems.at[first[t] + l], device_id=(x, y, 1 - c), device_id_type=MESH)
            for t in range(nt) for l in range(layers[t])]

    def start(_, o_refs, sems):
        for cp in copies(o_refs, sems, True):
            cp.start()

    def finish(_, o_refs, sems):
        for cp in copies(o_refs, sems, False):
            cp.wait_recv()
        for cp in copies(o_refs, sems, True):
            cp.wait_send()

    sems = [pltpu.SemaphoreType.DMA((sum(layers),)), pltpu.SemaphoreType.DMA((sum(layers),))]
    return _Hosted(list(bufs), [_sds(bf.shape, bf.dtype) for bf in bufs], sems, start, finish, in_place=True)


def sibling_join_halves(bufs, name):
    comm = _join_comm(bufs)
    nt = len(bufs)

    def body(*refs):
        comm.start(refs[:nt], refs[nt:2 * nt], refs[2 * nt:])
        comm.finish(refs[:nt], refs[nt:2 * nt], refs[2 * nt:])

    hbm = pl.BlockSpec(memory_space=pl.ANY)
    return pl.pallas_call(body, name=name, out_shape=comm.out_shape, in_specs=[hbm] * nt, out_specs=[hbm] * nt,
                          input_output_aliases={k: k for k in range(nt)}, scratch_shapes=comm.sems)(*bufs)


_SHARD_KIND = {"mla_w_in": "rows", "mla_w_uq": "cols", "mla_w_uk": "cols", "mla_w_uv": "cols", "mla_w_o": "rows",
               "fox_w_in": "cols", "fox_w_o": "rows", "ffn_w_gate": "chunk", "ffn_w_up": "chunk", "ffn_w_down": "chunk"}
_PACKED = tuple(_SHARD_KIND)
_TRANSPOSED = ("ffn_w_gate", "ffn_w_up", "fox_w_in")


def _halves(shard):
    if shard.ndim == 3 and shard.shape[0] == 2:
        return shard
    r, n = shard.shape[-2:]
    return shard.reshape(2, r // 2, n)


def _cols_to_full(g):
    return jnp.transpose(g, (1, 0, 2)).reshape(g.shape[1], -1)


def _full_to_cols(w):
    k, n4 = w.shape
    return jnp.transpose(w.reshape(k, N_CHIPS, n4 // N_CHIPS), (1, 0, 2))


def _uq_perm():
    per = MLA_NOPE + MLA_ROPE
    half = MLA_ROPE // 2
    nope = [h * per + d for h in range(MLA_HEADS) for d in range(MLA_NOPE)]
    r1 = [h * per + MLA_NOPE + r for h in range(MLA_HEADS) for r in range(half)]
    r2 = [h * per + MLA_NOPE + half + r for h in range(MLA_HEADS) for r in range(half)]
    perm = np.array(nope + r1 + r2, dtype=np.int32)
    return perm, np.argsort(perm).astype(np.int32)


def _rope_matrices():
    half = MLA_ROPE // 2
    nr = MLA_HEADS * MLA_ROPE
    to_heads = np.zeros((nr, nr), np.float32)
    from_heads = np.zeros((MLA_HEADS * 128, nr), np.float32)
    for e in range(2):
        for h in range(MLA_HEADS):
            for r in range(half):
                to_heads[e * MLA_HEADS * half + h * half + r, h * MLA_ROPE + e * half + r] = 1.0
                from_heads[h * 128 + e * half + r, e * MLA_HEADS * half + h * half + r] = 1.0
    head_sum = np.tile(np.eye(MLA_ROPE, dtype=np.float32), (2 * MLA_HEADS, 1))
    dup = np.concatenate([np.eye(MLA_ROPE, dtype=np.float32)] * 2, axis=1)
    return to_heads, from_heads, head_sum, dup


def _ffn_weights(gathered):
    return tuple(g.reshape(N_CHIPS, 2 * g.shape[2], g.shape[3]) for g in gathered)


def _fox_weights(gathered):
    w_in, w_o = gathered
    w_in = jnp.transpose(w_in, (0, 2, 1, 3)).reshape(N_CHIPS * w_in.shape[2], 2 * w_in.shape[3])
    return w_in, w_o.reshape(-1, w_o.shape[-1])


def _local_step(x, positions, target, mods, wts, ln_g, ln_b, mla_g_q, mla_g_kv, fox_b_f, shards=None):
    nb, s, d = x.shape
    t = nb * s
    x0 = x.reshape(t, d)
    tgt = target.reshape(t, d)
    perm, inv_perm = _uq_perm()

    half = MLA_ROPE // 2
    inv_freq = ROPE_THETA ** (-jnp.arange(half, dtype=F32) / half)
    ang = positions.astype(F32).reshape(t, 1) * inv_freq
    cos, sin = jnp.cos(ang), jnp.sin(ang)
    cos8, sin8 = jnp.tile(cos, (1, MLA_HEADS)), jnp.tile(sin, (1, MLA_HEADS))
    cos64 = jnp.concatenate([cos, cos], axis=1)
    sin64s = jnp.concatenate([-sin, sin], axis=1)
    swap64 = jnp.asarray(np.roll(np.eye(MLA_ROPE, dtype=np.float32), half, axis=1))
    to_heads, from_heads, head_sum, dup = _rope_matrices()
    to_heads, from_heads = jnp.asarray(to_heads, dtype=BF16), jnp.asarray(from_heads, dtype=BF16)
    head_sum, dup = jnp.asarray(head_sum, dtype=BF16), jnp.asarray(dup, dtype=BF16)
    sel_mla = jnp.asarray(np.pad(np.kron(np.eye(MLA_HEADS, dtype=np.float32), np.ones((MLA_V, 1), np.float32)),
                                 ((0, 0), (0, 128 - MLA_HEADS))))
    sel_fox = jnp.asarray(np.pad(np.kron(np.eye(FOX_HEADS, dtype=np.float32), np.ones((FOX_HD, 1), np.float32)),
                                 ((0, 0), (0, 128 - FOX_HEADS))))
    tri = jnp.asarray(np.tril(np.ones((GATE_BLOCK, GATE_BLOCK), np.float32)))
    triu = jnp.asarray(np.triu(np.ones((GATE_BLOCK, GATE_BLOCK), np.float32)))
    onehot16 = jnp.asarray(np.eye(16, 128, dtype=np.float32))

    def vec(a):
        return a.reshape(1, -1)

    def carried(key):
        return None if shards is None else _gather_comm(shards[key])

    def split(res):
        return (res, None) if shards is None else res

    w_uq_p = wts["mla_w_uq"][:, perm]
    b_f_pad = jnp.pad(fox_b_f.reshape(1, -1), ((0, 0), (0, 128 - FOX_HEADS)))

    sh_a, sc_a, gt_a, sh_f, sc_f, gt_f = mods[0]
    h_in, u_m = mod_linear(x0, sh_a, sc_a, wts["mla_w_in"], F32, "mla_in", emit_u=True)
    q_m, kn_m, v_m, kr2_m, cq_m, ckv_m = mla_mid_fwd(
        h_in, vec(mla_g_q), vec(mla_g_kv), w_uq_p, wts["mla_w_uk"], wts["mla_w_uv"], cos8, sin8, cos64, sin64s, swap64,
        to_heads, dup, "mla_mid")
    (o_m, lse_m), got = split(mla_attn_fwd(q_m, kn_m, kr2_m, v_m, nb, "mla_attn", hosted=carried("ffn0")))
    ffn0_w = wts["ffn"][0] if got is None else _ffn_weights(got)
    y0, x1 = linear_resid_ln(o_m, wts["mla_w_o"], x0, gt_a, vec(ln_g[0, 0]), vec(ln_b[0, 0]), "mla_out")
    (u_f0, hg0, hu0, y1, x2), got = split(ffn_fwd(x1, sh_f, sc_f, gt_f, *ffn0_w, vec(ln_g[0, 1]), vec(ln_b[0, 1]), "ffn0",
                                                  hosted=carried("fox")))
    fox_w_in_t, fox_w_o = (wts["fox_w_in"].T, wts["fox_w_o"]) if got is None else _fox_weights(got)
    fox_w_f_t = jnp.pad(fox_w_in_t[3 * d:], ((0, 128 - FOX_HEADS), (0, 0)))
    sh_a1, sc_a1, gt_a1, sh_f1, sc_f1, gt_f1 = mods[1]
    qkv, u_x = mod_linear(x2, sh_a1, sc_a1, fox_w_in_t, BF16, "fox_qkv", tn=1024, emit_u=True, w_rows=3 * d)
    hf = mod_linear(x2, sh_a1, sc_a1, fox_w_f_t, F32, "fox_f", w_rows=128)
    cum = fox_gate_fwd(hf, b_f_pad, tri, nb, "fox_gate")
    cum_rows = rows16(cum, "fox_cum_rows")
    (o_x, lse_x), got = split(fox_attn_fwd(qkv, cum, cum_rows, nb, "fox_attn", hosted=carried("ffn1")))
    ffn1_w = wts["ffn"][1] if got is None else _ffn_weights(got)
    y2, x3 = linear_resid_ln(o_x, fox_w_o, x2, gt_a1, vec(ln_g[1, 0]), vec(ln_b[1, 0]), "fox_out")
    u_f1, hg1, hu1, y3, x4 = ffn_fwd(x3, sh_f1, sc_f1, gt_f1, *ffn1_w, vec(ln_g[1, 1]), vec(ln_b[1, 1]), "ffn1")

    parts, recv = {}, {}

    def halves_of(g):
        return g.reshape(N_CHIPS, 2, g.shape[1] // 2, g.shape[2])

    def scatter(keys, sent):
        return None if shards is None else _scatter_comm([sent[k] for k in keys])

    def landed(keys, got):
        if got is not None:
            recv.update(zip(keys, got))

    def ffn_grads(layer, u, dhg, dhu, act, dy):
        sent = {}
        for n, (a_op, b_op) in (("ffn_w_gate", (dhg, u[None])), ("ffn_w_up", (dhu, u[None])), ("ffn_w_down", (act, dy[None]))):
            g32, g16 = wgrad(a_op, b_op, "ffn%d_d%s" % (layer, n[4:]), with_bf16=True)
            parts["%s/%d" % (n, layer)], sent["%s/%d" % (n, layer)] = halves_of(g32), halves_of(g16)
        return sent

    dz3, dy3, dg11, db11, dgt_f1, sq_err = ln_bwd(x4, x3, y3, gt_f1, vec(ln_g[1, 1]), "ffn1_ln_bwd", target=tgt)
    loss_part = 0.5 * jnp.sum(sq_err) / d
    dhg1, dhu1, act1, dx3, dsc_f1, dsh_f1 = ffn_bwd(dy3, hg1, hu1, *ffn1_w, dz3, x3, sc_f1, "ffn1_bwd")
    sent = ffn_grads(1, u_f1, dhg1, dhu1, act1, dy3)
    dz2, dy2, dg10, db10, dgt_a1 = ln_bwd(dx3, x2, y2, gt_a1, vec(ln_g[1, 0]), "fox_ln_bwd")
    do_x, delta_x = linear_nt_delta(dy2, fox_w_o, o_x, sel_fox, "fox_out_bwd")
    (dq_x, dk_x, dv_x, dfq_x, dfk_x), got = split(fox_attn_bwd(
        qkv, do_x, cum, cum_rows, rows16(lse_x, "fox_lse_rows"), rows16(delta_x, "fox_delta_rows"), nb, "fox_attn_bwd",
        hosted=scatter(list(sent), sent)))
    landed(list(sent), got)
    dcum = tokens128(dfq_x + dfk_x, onehot16, "fox_dcum")
    dhf, dbf = fox_gate_bwd(dcum, hf, b_f_pad, triu, nb, "fox_gate_bwd")
    fox_d = [("q", dq_x), ("k", dk_x), ("v", dv_x)]
    dx2, dsc_a1, dsh_a1 = linear_nt_mod_bwd(
        [(dh, fox_w_in_t, i) for i, (_, dh) in enumerate(fox_d)] + [(dhf, fox_w_f_t, 0)], dz2, x2, sc_a1, "fox_in_bwd")
    dw_in_t = [wgrad(dh[None], u_x[None], "fox_dw" + tag)[0] for tag, dh in fox_d]
    dw_in_t.append(wgrad(dhf[None], u_x[None], "fox_dwf")[0][:FOX_HEADS])
    dw_in_t = jnp.concatenate(dw_in_t, axis=0).reshape(N_CHIPS, -1, 2, d // 2)
    parts["fox_w_in"] = jnp.transpose(dw_in_t, (0, 2, 1, 3))
    parts["fox_w_o"] = wgrad(o_x[None], dy2[None], "fox_dwo")[0].reshape(N_CHIPS, 2, -1, d)
    sent = {k: parts[k].astype(BF16) for k in ("fox_w_in", "fox_w_o")}
    dz1, dy1, dg01, db01, dgt_f0 = ln_bwd(dx2, x1, y1, gt_f, vec(ln_g[0, 1]), "ffn0_ln_bwd")
    (dhg0, dhu0, act0, dx1, dsc_f0, dsh_f0), got = split(ffn_bwd(dy1, hg0, hu0, *ffn0_w, dz1, x1, sc_f, "ffn0_bwd",
                                                                 hosted=scatter(list(sent), sent)))
    landed(list(sent), got)
    sent = ffn_grads(0, u_f0, dhg0, dhu0, act0, dy1)
    dz0, dy0, dg00, db00, dgt_a0 = ln_bwd(dx1, x0, y0, gt_a, vec(ln_g[0, 0]), "mla_ln_bwd")
    do_m, delta_m = linear_nt_delta(dy0, wts["mla_w_o"], o_m, sel_mla, "mla_out_bwd")
    parts["mla_w_o"] = wgrad(o_m[None], dy0[None], "mla_dwo")[0].reshape(N_CHIPS, 2, -1, d)
    (dqn_m, dqr_m, dkn_m, dkr_m, dv_m), got = split(mla_attn_bwd(
        q_m, kn_m, kr2_m, v_m, do_m, rows16(lse_m, "mla_lse_rows"), rows16(delta_m, "mla_delta_rows"), nb,
        "mla_attn_bwd", hosted=scatter(list(sent), sent)))
    landed(list(sent), got)
    sent = {"mla_w_o": parts["mla_w_o"].astype(BF16)}
    (dh_in, dq_pre, dgq, dgkv), got = split(mla_mid_bwd(
        dqn_m, dqr_m, dkn_m, dv_m, dkr_m, h_in, vec(mla_g_q), vec(mla_g_kv), w_uq_p, wts["mla_w_uk"],
        wts["mla_w_uv"], cos8, sin8, cos64, sin64s, swap64, from_heads, head_sum, "mla_mid_bwd",
        hosted=scatter(list(sent), sent)))
    landed(list(sent), got)
    parts["mla_w_uq"] = halves_of(_full_to_cols(wgrad(cq_m[None], dq_pre[None], "mla_dwuq")[0][:, inv_perm]))
    parts["mla_w_uk"] = halves_of(_full_to_cols(wgrad(ckv_m[None], dkn_m[None], "mla_dwuk")[0]))
    parts["mla_w_uv"] = halves_of(_full_to_cols(wgrad(ckv_m[None], dv_m[None], "mla_dwuv")[0]))
    parts["mla_w_in"] = wgrad(u_m[None], dh_in[None], "mla_dwin")[0].reshape(N_CHIPS, 2, -1, h_in.shape[1])
    sent = {k: parts[k].astype(BF16) for k in ("mla_w_in", "mla_w_uq", "mla_w_uk", "mla_w_uv")}
    (dx0, dsc_a0, dsh_a0), got = split(linear_nt_mod_bwd([(dh_in, wts["mla_w_in"], None)], dz0, x0, sc_a, "mla_in_bwd",
                                                         hosted=scatter(list(sent), sent)))
    landed(list(sent), got)

    dmods = [(dsh_a0, dsc_a0, dgt_a0, dsh_f0, dsc_f0, dgt_f0), (dsh_a1, dsc_a1, dgt_a1, dsh_f1, dsc_f1, dgt_f1)]
    d_ln_g = jnp.stack([jnp.concatenate([dg00, dg01], axis=0), jnp.concatenate([dg10, dg11], axis=0)])
    d_ln_b = jnp.stack([jnp.concatenate([db00, db01], axis=0), jnp.concatenate([db10, db11], axis=0)])
    return loss_part, dx0.reshape(nb, s, d), (parts, recv), dmods, d_ln_g, d_ln_b, dgq, dgkv, dbf[:, :FOX_HEADS]


def _pad_rows(a, rows):
    return jnp.pad(a, ((0, rows - a.shape[0]), (0, 0)))


def kernel(x, c, positions, mla_w_in, mla_g_q, mla_w_uq, mla_g_kv, mla_w_uk, mla_w_uv, mla_w_o, fox_w_in, fox_b_f, fox_w_o, ada_w, ada_b, ffn_w_gate, ffn_w_up, ffn_w_down, ln_g, ln_b, loss_target, m_mla_w_in, m_mla_g_q, m_mla_w_uq, m_mla_g_kv, m_mla_w_uk, m_mla_w_uv, m_mla_w_o, m_fox_w_in, m_fox_b_f, m_fox_w_o, m_ada_w, m_ada_b, m_ffn_w_gate, m_ffn_w_up, m_ffn_w_down, m_ln_g, m_ln_b, v_mla_w_in, v_mla_g_q, v_mla_w_uq, v_mla_g_kv, v_mla_w_uk, v_mla_w_uv, v_mla_w_o, v_fox_w_in, v_fox_b_f, v_fox_w_o, v_ada_w, v_ada_b, v_ffn_w_gate, v_ffn_w_up, v_ffn_w_down, v_ln_g, v_ln_b):
    args = dict(locals())
    nb, s, d = x.shape
    ax, ay, ac = lax.axis_index("x"), lax.axis_index("y"), lax.axis_index("c")
    chip = 2 * ax + ay
    dev = 2 * chip + ac
    n_dev = 2 * N_CHIPS
    n_all = nb * n_dev

    shard_shapes = {n: (args[n].shape if _SHARD_KIND[n] == "chunk" else args[n].shape[1:]) for n in _PACKED}

    def block(n, layer=None):
        w = args[n].reshape(shard_shapes[n]) if layer is None else args[n][layer]
        return _halves(w.astype(BF16))

    mla_names = [n for n in _PACKED if n.startswith("mla")]
    ffn_names = ("ffn_w_gate", "ffn_w_up", "ffn_w_down")
    fox_in_t = jnp.swapaxes(fox_w_in, 1, 2)[0].astype(BF16)
    fox_in_t = jnp.stack([fox_in_t[:, :d // 2], fox_in_t[:, d // 2:]])
    shards = {"ffn0": [block(n, 0) for n in ffn_names], "fox": [fox_in_t, block("fox_w_o")],
              "ffn1": [block(n, 1) for n in ffn_names]}

    ln_cols = ln_g.shape[-1]
    ln_blk = jnp.concatenate([ln_g.reshape(2 * DEPTH, ln_cols), ln_b.reshape(2 * DEPTH, ln_cols)], axis=0)
    early = jnp.concatenate([_pad_rows(c, 8), jnp.pad(_pad_rows(ln_blk, 8), ((0, 0), (0, d - ln_cols)))], axis=0)
    early, mla_all = all_gather8(early, "gather_c_ln_mla", hosted=_gather_comm([block(n) for n in mla_names]))
    wts = {}
    for n, g in zip(mla_names, mla_all):
        g = g.reshape(N_CHIPS, *shard_shapes[n])
        wts[n] = g.reshape(-1, g.shape[-1]) if _SHARD_KIND[n] == "rows" else _cols_to_full(g)
    early = early.reshape(n_dev, 16, d)
    c_all = early[:, :nb].reshape(n_all, d)
    ln_all = early.reshape(N_CHIPS, 2, 16, d)[:, 0, 8:8 + 4 * DEPTH, :ln_cols]
    ln_all = jnp.transpose(ln_all, (1, 0, 2)).reshape(4 * DEPTH, d)
    ln_g_full = ln_all[:2 * DEPTH].reshape(DEPTH, 2, d)
    ln_b_full = ln_all[2 * DEPTH:].reshape(DEPTH, 2, d)
    mod_part = ada_mod_part(c_all, ada_w, "ada_mod")
    ncol = mod_part.shape[-1]
    mod_g = all_gather8(mod_part.reshape(DEPTH * n_all, ncol), "gather_mod")
    mod_g = mod_g.reshape(N_CHIPS, 2, DEPTH, n_all, ncol)[:, 0]
    mod_full = jnp.transpose(mod_g, (1, 2, 0, 3)).reshape(DEPTH, n_all, N_CHIPS * ncol) + ada_b[:, None, :]
    mod_loc = lax.dynamic_slice_in_dim(mod_full, dev * nb, nb, axis=1)
    mods = [tuple(mod_loc[i, :, k * d:(k + 1) * d].reshape(nb, 1, d) for k in range(6)) for i in range(DEPTH)]

    loss_part, grad_x, (parts, recv), dmods, d_ln_g, d_ln_b, dgq, dgkv, dbf = _local_step(
        x, positions, loss_target, mods, wts, ln_g_full, ln_b_full, mla_g_q[0], mla_g_kv[0], fox_b_f[0], shards)
    loss = lax.psum(loss_part, ("x", "y", "c"))

    dmod_rows = jnp.stack([jnp.concatenate([v_.reshape(nb, d) for v_ in dm], axis=1) for dm in dmods])
    small = jnp.concatenate([
        d_ln_g.reshape(2 * DEPTH, d), d_ln_b.reshape(2 * DEPTH, d),
        jnp.pad(jnp.concatenate([dgq, dgkv, dbf], axis=1), ((0, 0), (0, d - 2 * MLA_QR - FOX_HEADS))),
        dmod_rows.reshape(DEPTH * nb * 6, d)], axis=0)
    n_small = small.shape[0]
    small_rows = -(-n_small // 8) * 8
    small_all = all_gather8(_pad_rows(small, small_rows), "gather_stats").reshape(n_dev, small_rows, d)
    stat_sum = sum_leading(small_all, "sum_stats")
    g_ln_g = lax.dynamic_slice_in_dim(stat_sum[:2 * DEPTH], chip * ln_cols, ln_cols, axis=1).reshape(DEPTH, 2, ln_cols)
    g_ln_b = lax.dynamic_slice_in_dim(stat_sum[2 * DEPTH:4 * DEPTH], chip * ln_cols, ln_cols, axis=1).reshape(DEPTH, 2, ln_cols)
    row = stat_sum[4 * DEPTH]
    g_gq = row[:MLA_QR].reshape(1, MLA_QR)
    g_gkv = row[MLA_QR:2 * MLA_QR].reshape(1, MLA_KVR)
    g_bf = row[2 * MLA_QR:2 * MLA_QR + FOX_HEADS].reshape(1, FOX_HEADS)
    base = 4 * DEPTH + 1
    dmod_all = small_all[:, base:base + DEPTH * nb * 6].reshape(n_dev, DEPTH, nb, 6 * d)
    dmod_all = jnp.transpose(dmod_all, (1, 0, 2, 3)).reshape(DEPTH, n_all, 6 * d)
    g_ada_b = sum_leading(jnp.transpose(dmod_all, (1, 0, 2)), "sum_ada_b")
    dmod_mine = lax.dynamic_slice_in_dim(dmod_all, chip * ncol, ncol, axis=2)
    g_ada_w = ada_grad(c_all.T, dmod_mine, "ada_grad")

    place = jnp.stack([dev, ac, chip]).astype(jnp.int32)
    bufs = []
    for n in _PACKED:
        if _SHARD_KIND[n] == "chunk":
            buf = None
            for layer in range(DEPTH):
                key = "%s/%d" % (n, layer)
                buf = sum_devices(parts[key], recv[key], place, "rs_sum_%s%d" % (n, layer), slot=(layer, DEPTH, buf))
        else:
            buf = sum_devices(parts[n], recv[n], place, "rs_sum_" + n)
        bufs.append(buf)
    joined = sibling_join_halves(bufs, "rs_join")
    g_big = {n: j.reshape(j.shape[0], 2 * j.shape[2], j.shape[3]) for n, j in zip(_PACKED, joined)}
    j = joined[_PACKED.index("fox_w_in")]
    g_big["fox_w_in"] = jnp.transpose(j, (0, 2, 1, 3)).reshape(1, j.shape[2], 2 * j.shape[3])

    g_out = {
        "mla_w_in": g_big["mla_w_in"], "mla_g_q": g_gq, "mla_w_uq": g_big["mla_w_uq"], "mla_g_kv": g_gkv,
        "mla_w_uk": g_big["mla_w_uk"], "mla_w_uv": g_big["mla_w_uv"], "mla_w_o": g_big["mla_w_o"],
        "fox_w_in": g_big["fox_w_in"], "fox_b_f": g_bf, "fox_w_o": g_big["fox_w_o"],
        "ada_w": g_ada_w, "ada_b": g_ada_b, "ffn_w_gate": g_big["ffn_w_gate"], "ffn_w_up": g_big["ffn_w_up"],
        "ffn_w_down": g_big["ffn_w_down"], "ln_g": g_ln_g, "ln_b": g_ln_b}
    names = ["mla_w_in", "mla_g_q", "mla_w_uq", "mla_g_kv", "mla_w_uk", "mla_w_uv", "mla_w_o", "fox_w_in", "fox_b_f",
             "fox_w_o", "ada_w", "ada_b", "ffn_w_gate", "ffn_w_up", "ffn_w_down", "ln_g", "ln_b"]
    small_names = ["mla_g_q", "mla_g_kv", "fox_b_f", "ada_b", "ln_g", "ln_b"]
    deltas, new_m, new_v = {}, {}, {}
    for n in names:
        if n in small_names:
            continue
        shp = args[n].shape
        if n in _TRANSPOSED:
            view = lambda a: jnp.swapaxes(a, 1, 2).reshape(-1, shp[1])
            back = lambda a: jnp.swapaxes(a.reshape(shp[0], shp[2], shp[1]), 1, 2)
        else:
            view = lambda a: a.reshape(-1, shp[-1])
            back = lambda a: a.reshape(shp)
        dl, mn, vn = adamw(view(args[n]), g_out[n].reshape(view(args[n]).shape), view(args["m_" + n]),
                           view(args["v_" + n]), "adamw_" + n)
        g_out[n], deltas[n], new_m[n], new_v[n] = back(g_out[n].reshape(view(args[n]).shape)), back(dl), back(mn), back(vn)

    def small_pack(prefix, src):
        flat = jnp.concatenate([src[prefix + n].reshape(-1) for n in small_names])
        size = -(-flat.shape[0] // (8 * 128)) * 8 * 128
        return jnp.pad(flat, (0, size - flat.shape[0])).reshape(-1, 128)

    sd, sm, sv = adamw(small_pack("", args), small_pack("", g_out), small_pack("m_", args), small_pack("v_", args),
                       "adamw_small")
    off = 0
    for n in small_names:
        shp = args[n].shape
        size = math.prod(shp)
        deltas[n] = sd.reshape(-1)[off:off + size].reshape(shp)
        new_m[n] = sm.reshape(-1)[off:off + size].reshape(shp)
        new_v[n] = sv.reshape(-1)[off:off + size].reshape(shp)
        off += size

    outs = [loss, grad_x]
    outs += [g_out[n].reshape(args[n].shape) for n in names]
    outs += [deltas[n] for n in names] + [new_m[n] for n in names] + [new_v[n] for n in names]
    return tuple(outs)
```
